```python
import math
import jax, jax.numpy as jnp
from jax import lax
import numpy as np

D_MODEL = 1024
BATCH = 8
SEQ = 8192
DEPTH = 4

CHUNK = 64
N_MIXERS = 3
EPS = 1e-6
GLA_HEADS = 4
GLA_DK = D_MODEL // (2 * GLA_HEADS)
GLA_DV = D_MODEL // GLA_HEADS
GLA_RANK = 16
GLA_TAU = 16.0
SSD_DINNER = 2 * D_MODEL
SSD_HEADDIM = 64
SSD_HEADS = SSD_DINNER // SSD_HEADDIM
SSD_GROUPS = 8
SSD_HPG = SSD_HEADS // SSD_GROUPS
SSD_DSTATE = 128
SSD_CONV = 4
S5_GROUP = 16
S5_GROUPS = D_MODEL // S5_GROUP
S5_STATE = 64
FFN_HIDDEN = ((-(-8 * D_MODEL // 3)) + 255) // 256 * 256

kernel_name = 'chunk_causal_hybrid_gla_ssd_s5'


def _layers_of(mixer):
    return len(range(mixer, DEPTH, N_MIXERS))


def rmsnorm(x, g):
    xf = x.astype(jnp.float32)
    y = xf * lax.rsqrt(jnp.mean(xf * xf, axis=-1, keepdims=True) + EPS)
    return (y * g.astype(jnp.float32)).astype(x.dtype)


def causal_depthwise_conv(x, w, b):
    k = w.shape[0]
    out = lax.conv_general_dilated(
        x, w[:, None, :].astype(x.dtype), window_strides=(1,), padding=[(k - 1, 0)],
        dimension_numbers=('NWC', 'WIO', 'NWC'), feature_group_count=x.shape[-1])
    return out + b.astype(x.dtype)


def gla_mixer(h, w_in, w_a2, b_a, norm_g, w_out):
    bsz, seq, _ = h.shape
    nc = seq // CHUNK
    f32 = jnp.float32
    qk = GLA_HEADS * GLA_DK
    vd = GLA_HEADS * GLA_DV
    proj = h @ w_in
    q, k, v, r, a_low = jnp.split(proj, [qk, 2 * qk, 2 * qk + vd, 2 * qk + 2 * vd], axis=-1)
    log_a = jax.nn.log_sigmoid((a_low @ w_a2 + b_a).astype(f32)) / GLA_TAU

    def chunks(t, d):
        return t.reshape(bsz, nc, CHUNK, GLA_HEADS, d).astype(f32)

    q = chunks(q, GLA_DK) * (GLA_DK ** -0.5)
    k = chunks(k, GLA_DK)
    v = chunks(v, GLA_DV)
    lc = jnp.cumsum(chunks(log_a, GLA_DK), axis=2)
    lend = lc[:, :, -1:]
    q_fwd = q * jnp.exp(lc)
    k_fwd = k * jnp.exp(-lc)
    q_bwd = q * jnp.exp(-lc)
    k_bwd = k * jnp.exp(lc)
    s_past = jnp.einsum('bclhd,bcshd->bchls', q_fwd, k_fwd)
    s_future = jnp.einsum('bclhd,bcshd->bchls', q_bwd, k_bwd)
    past_mask = jnp.tril(jnp.ones((CHUNK, CHUNK), dtype=bool))
    scores = jnp.where(past_mask, s_past, s_future)
    o = jnp.einsum('bchls,bcshv->bclhv', scores, v)
    g_chunk = jnp.exp(lend[:, :, 0])
    d_state = jnp.einsum('bcshd,bcshv->bchdv', k * jnp.exp(lend - lc), v)

    def step(s, inp):
        g, ds = inp
        return g[..., None] * s + ds, s

    s0 = jnp.zeros((bsz, GLA_HEADS, GLA_DK, GLA_DV), f32)
    _, s_prev = lax.scan(step, s0, (jnp.moveaxis(g_chunk, 1, 0), jnp.moveaxis(d_state, 1, 0)))
    s_prev = jnp.moveaxis(s_prev, 0, 1)
    o = o + jnp.einsum('bclhd,bchdv->bclhv', q_fwd, s_prev)
    o = rmsnorm(o.reshape(bsz, seq, GLA_HEADS, GLA_DV), norm_g.reshape(GLA_HEADS, GLA_DV))
    o = o.reshape(bsz, seq, vd) * jax.nn.silu(r.astype(f32))
    return (o @ w_out).astype(h.dtype)


def ssd_mixer(h, w_in, conv_w, conv_b, dt_bias, a_log, d_skip, norm_g, w_out):
    bsz, seq, _ = h.shape
    nc = seq // CHUNK
    f32 = jnp.float32
    gn = SSD_GROUPS * SSD_DSTATE
    proj = h @ w_in
    z, xbc, dt = jnp.split(proj, [SSD_DINNER, 2 * SSD_DINNER + 2 * gn], axis=-1)
    xbc = jax.nn.silu(causal_depthwise_conv(xbc, conv_w, conv_b))
    xs, bm, cm = jnp.split(xbc, [SSD_DINNER, SSD_DINNER + gn], axis=-1)
    dt = jax.nn.softplus(dt.astype(f32) + dt_bias.astype(f32))
    da = dt * (-jnp.exp(a_log.astype(f32)))
    xs = xs.reshape(bsz, nc, CHUNK, SSD_GROUPS, SSD_HPG, SSD_HEADDIM).astype(f32)
    bm = bm.reshape(bsz, nc, CHUNK, SSD_GROUPS, SSD_DSTATE).astype(f32)
    cm = cm.reshape(bsz, nc, CHUNK, SSD_GROUPS, SSD_DSTATE).astype(f32)
    dt = dt.reshape(bsz, nc, CHUNK, SSD_GROUPS, SSD_HPG)
    cum = jnp.cumsum(da.reshape(bsz, nc, CHUNK, SSD_GROUPS, SSD_HPG), axis=2)
    cb = jnp.einsum('bclgn,bcsgn->bcgls', cm, bm)
    cum_h = jnp.moveaxis(cum, 2, -1)
    decay = jnp.exp(-jnp.abs(cum_h[..., :, None] - cum_h[..., None, :]))
    dt_h = jnp.moveaxis(dt, 2, -1)
    mix = cb[:, :, :, None] * decay * dt_h[..., None, :]
    y = jnp.einsum('bcgjls,bcsgjp->bclgjp', mix, xs)
    cum_end = cum[:, :, -1]
    xw = xs * (dt * jnp.exp(cum_end[:, :, None] - cum))[..., None]
    states = jnp.einsum('bcsgn,bcsgjp->bcgjpn', bm, xw)

    def step(hs, inp):
        a, s = inp
        return a[..., None, None] * hs + s, hs

    h0 = jnp.zeros((bsz, SSD_GROUPS, SSD_HPG, SSD_HEADDIM, SSD_DSTATE), f32)
    _, h_prev = lax.scan(step, h0, (jnp.moveaxis(jnp.exp(cum_end), 1, 0), jnp.moveaxis(states, 1, 0)))
    h_prev = jnp.moveaxis(h_prev, 0, 1)
    y = y + jnp.einsum('bclgn,bcgjpn->bclgjp', cm, h_prev) * jnp.exp(cum)[..., None]
    y = y + d_skip.astype(f32).reshape(SSD_GROUPS, SSD_HPG)[:, :, None] * xs
    y = y.reshape(bsz, seq, SSD_DINNER) * jax.nn.silu(z.astype(f32))
    gsz = SSD_DINNER // SSD_GROUPS
    y = rmsnorm(y.reshape(bsz, seq, SSD_GROUPS, gsz), norm_g.reshape(SSD_GROUPS, gsz))
    return (y.reshape(bsz, seq, SSD_DINNER) @ w_out).astype(h.dtype)


def _complex_linear_combine(e1, e2):
    a1r, a1i, b1r, b1i = e1
    a2r, a2i, b2r, b2i = e2
    ar = a2r * a1r - a2i * a1i
    ai = a2r * a1i + a2i * a1r
    br = a2r * b1r - a2i * b1i + b2r
    bi = a2r * b1i + a2i * b1r + b2i
    return ar, ai, br, bi


def s5_mixer(h, log_dt, a_re, a_im, b_re, b_im, c_re, c_im, d_skip, w_glu):
    bsz, seq, _ = h.shape
    f32 = jnp.float32
    u = h.reshape(bsz, seq, S5_GROUPS, S5_GROUP).astype(f32)
    step = jnp.exp(log_dt.astype(f32))[:, None]
    a_re = a_re.astype(f32)
    a_im = a_im.astype(f32)
    mag = jnp.exp(step * a_re)
    abar_re = mag * jnp.cos(step * a_im)
    abar_im = mag * jnp.sin(step * a_im)
    den = a_re * a_re + a_im * a_im
    num_re = abar_re - 1.0
    num_im = abar_im
    f_re = (num_re * a_re + num_im * a_im) / den
    f_im = (num_im * a_re - num_re * a_im) / den
    b_re = b_re.astype(f32)
    b_im = b_im.astype(f32)
    bb_re = f_re[..., None] * b_re - f_im[..., None] * b_im
    bb_im = f_re[..., None] * b_im + f_im[..., None] * b_re
    bu_re = jnp.einsum('gpc,blgc->blgp', bb_re, u)
    bu_im = jnp.einsum('gpc,blgc->blgp', bb_im, u)
    a_seq_re = jnp.broadcast_to(abar_re, bu_re.shape)
    a_seq_im = jnp.broadcast_to(abar_im, bu_im.shape)
    _, _, x_re, x_im = lax.associative_scan(
        _complex_linear_combine, (a_seq_re, a_seq_im, bu_re, bu_im), axis=1)
    y = (jnp.einsum('gcp,blgp->blgc', c_re.astype(f32), x_re)
         - jnp.einsum('gcp,blgp->blgc', c_im.astype(f32), x_im))
    y = y + d_skip.astype(f32).reshape(S5_GROUPS, S5_GROUP) * u
    y = jax.nn.gelu(y.reshape(bsz, seq, D_MODEL))
    val, gate = jnp.split(y @ w_glu, 2, axis=-1)
    return (val * jax.nn.sigmoid(gate)).astype(h.dtype)


def swiglu_ffn(h, w_gu, w_down):
    g, u = jnp.split(h @ w_gu, 2, axis=-1)
    return (jax.nn.silu(g) * u) @ w_down


def _fwd_setup_inputs(seed: int = 0) -> dict:
    key = jax.random.key(seed)
    ks = iter(jax.random.split(key, 48))
    f32 = jnp.float32

    def nrm(shape, scale):
        return jax.random.normal(next(ks), shape, f32) * scale

    n_gla, n_ssd, n_s5 = _layers_of(0), _layers_of(1), _layers_of(2)
    qk = GLA_HEADS * GLA_DK
    vd = GLA_HEADS * GLA_DV
    gla_in = 2 * qk + 2 * vd + GLA_RANK
    gn = SSD_GROUPS * SSD_DSTATE
    ssd_conv_dim = SSD_DINNER + 2 * gn
    ssd_in = SSD_DINNER + ssd_conv_dim + SSD_HEADS

    x = jax.random.normal(next(ks), (BATCH, SEQ, D_MODEL), f32)
    norm_mix_g = 1.0 + nrm((DEPTH, D_MODEL), 0.01)
    norm_ffn_g = 1.0 + nrm((DEPTH, D_MODEL), 0.01)
    gla_w_in = nrm((n_gla, D_MODEL, gla_in), D_MODEL ** -0.5)
    gla_w_a2 = nrm((n_gla, GLA_RANK, qk), GLA_RANK ** -0.5)
    gla_b_a = nrm((n_gla, qk), 0.1)
    gla_norm_g = 1.0 + nrm((n_gla, vd), 0.01)
    gla_w_out = nrm((n_gla, vd, D_MODEL), vd ** -0.5)
    ssd_w_in = nrm((n_ssd, D_MODEL, ssd_in), D_MODEL ** -0.5)
    ssd_conv_w = nrm((n_ssd, SSD_CONV, ssd_conv_dim), SSD_CONV ** -0.5)
    ssd_conv_b = nrm((n_ssd, ssd_conv_dim), 0.02)
    dt0 = jnp.exp(jax.random.uniform(next(ks), (n_ssd, SSD_HEADS), f32, math.log(1e-3), math.log(1e-1)))
    ssd_dt_bias = dt0 + jnp.log(-jnp.expm1(-dt0))
    ssd_a_log = jnp.log(jax.random.uniform(next(ks), (n_ssd, SSD_HEADS), f32, 1.0, 16.0))
    ssd_d = 1.0 + nrm((n_ssd, SSD_HEADS), 0.1)
    ssd_norm_g = 1.0 + nrm((n_ssd, SSD_DINNER), 0.01)
    ssd_w_out = nrm((n_ssd, SSD_DINNER, D_MODEL), SSD_DINNER ** -0.5)
    s5_log_dt = jax.random.uniform(next(ks), (n_s5, S5_GROUPS), f32, math.log(1e-3), math.log(1e-1))
    n_idx = jnp.arange(S5_STATE, dtype=f32)
    s5_a_re = -0.5 + nrm((n_s5, S5_GROUPS, S5_STATE), 0.01)
    s5_a_im = jnp.pi * n_idx + nrm((n_s5, S5_GROUPS, S5_STATE), 0.01)
    bscale = (2.0 * S5_GROUP) ** -0.5
    s5_b_re = nrm((n_s5, S5_GROUPS, S5_STATE, S5_GROUP), bscale)
    s5_b_im = nrm((n_s5, S5_GROUPS, S5_STATE, S5_GROUP), bscale)
    cscale = (2.0 * S5_STATE) ** -0.5
    s5_c_re = nrm((n_s5, S5_GROUPS, S5_GROUP, S5_STATE), cscale)
    s5_c_im = nrm((n_s5, S5_GROUPS, S5_GROUP, S5_STATE), cscale)
    s5_d = 1.0 + nrm((n_s5, D_MODEL), 0.1)
    s5_w_glu = nrm((n_s5, D_MODEL, 2 * D_MODEL), D_MODEL ** -0.5)
    ffn_w_gu = nrm((DEPTH, D_MODEL, 2 * FFN_HIDDEN), D_MODEL ** -0.5)
    ffn_w_down = nrm((DEPTH, FFN_HIDDEN, D_MODEL), FFN_HIDDEN ** -0.5)
    final_norm_g = 1.0 + nrm((D_MODEL,), 0.01)
    return {
        'x': x, 'norm_mix_g': norm_mix_g, 'norm_ffn_g': norm_ffn_g,
        'gla_w_in': gla_w_in, 'gla_w_a2': gla_w_a2, 'gla_b_a': gla_b_a,
        'gla_norm_g': gla_norm_g, 'gla_w_out': gla_w_out,
        'ssd_w_in': ssd_w_in, 'ssd_conv_w': ssd_conv_w, 'ssd_conv_b': ssd_conv_b,
        'ssd_dt_bias': ssd_dt_bias, 'ssd_a_log': ssd_a_log, 'ssd_d': ssd_d,
        'ssd_norm_g': ssd_norm_g, 'ssd_w_out': ssd_w_out,
        's5_log_dt': s5_log_dt, 's5_a_re': s5_a_re, 's5_a_im': s5_a_im,
        's5_b_re': s5_b_re, 's5_b_im': s5_b_im, 's5_c_re': s5_c_re, 's5_c_im': s5_c_im,
        's5_d': s5_d, 's5_w_glu': s5_w_glu,
        'ffn_w_gu': ffn_w_gu, 'ffn_w_down': ffn_w_down, 'final_norm_g': final_norm_g,
    }


def _fwd_reference(x, norm_mix_g, norm_ffn_g,
              gla_w_in, gla_w_a2, gla_b_a, gla_norm_g, gla_w_out,
              ssd_w_in, ssd_conv_w, ssd_conv_b, ssd_dt_bias, ssd_a_log, ssd_d, ssd_norm_g, ssd_w_out,
              s5_log_dt, s5_a_re, s5_a_im, s5_b_re, s5_b_im, s5_c_re, s5_c_im, s5_d, s5_w_glu,
              ffn_w_gu, ffn_w_down, final_norm_g):
    h = x
    for i in range(DEPTH):
        mixer, j = i % N_MIXERS, i // N_MIXERS
        hn = rmsnorm(h, norm_mix_g[i])
        if mixer == 0:
            y = gla_mixer(hn, gla_w_in[j], gla_w_a2[j], gla_b_a[j], gla_norm_g[j], gla_w_out[j])
        elif mixer == 1:
            y = ssd_mixer(hn, ssd_w_in[j], ssd_conv_w[j], ssd_conv_b[j], ssd_dt_bias[j],
                          ssd_a_log[j], ssd_d[j], ssd_norm_g[j], ssd_w_out[j])
        else:
            y = s5_mixer(hn, s5_log_dt[j], s5_a_re[j], s5_a_im[j], s5_b_re[j], s5_b_im[j],
                         s5_c_re[j], s5_c_im[j], s5_d[j], s5_w_glu[j])
        h = h + y.astype(h.dtype)
        h = h + swiglu_ffn(rmsnorm(h, norm_ffn_g[i]), ffn_w_gu[i], ffn_w_down[i]).astype(h.dtype)
    return rmsnorm(h, final_norm_g)


import jax as _jax
import jax.numpy as _jnp

TWIN_FORMAT = 'train_step'
FWD_PARAMS = ['x', 'norm_mix_g', 'norm_ffn_g', 'gla_w_in', 'gla_w_a2', 'gla_b_a', 'gla_norm_g', 'gla_w_out', 'ssd_w_in', 'ssd_conv_w', 'ssd_conv_b', 'ssd_dt_bias', 'ssd_a_log', 'ssd_d', 'ssd_norm_g', 'ssd_w_out', 's5_log_dt', 's5_a_re', 's5_a_im', 's5_b_re', 's5_b_im', 's5_c_re', 's5_c_im', 's5_d', 's5_w_glu', 'ffn_w_gu', 'ffn_w_down', 'final_norm_g']
TWIN_WEIGHTS = ['norm_mix_g', 'norm_ffn_g', 'gla_w_in', 'gla_w_a2', 'gla_b_a', 'gla_norm_g', 'gla_w_out', 'ssd_w_in', 'ssd_conv_w', 'ssd_conv_b', 'ssd_dt_bias', 'ssd_a_log', 'ssd_d', 'ssd_norm_g', 'ssd_w_out', 's5_log_dt', 's5_a_re', 's5_a_im', 's5_b_re', 's5_b_im', 's5_c_re', 's5_c_im', 's5_d', 's5_w_glu', 'ffn_w_gu', 'ffn_w_down', 'final_norm_g']
TWIN_DIFF_INPUT = 'x'
TWIN_INPUTS = ['x', 'norm_mix_g', 'norm_ffn_g', 'gla_w_in', 'gla_w_a2', 'gla_b_a', 'gla_norm_g', 'gla_w_out', 'ssd_w_in', 'ssd_conv_w', 'ssd_conv_b', 'ssd_dt_bias', 'ssd_a_log', 'ssd_d', 'ssd_norm_g', 'ssd_w_out', 's5_log_dt', 's5_a_re', 's5_a_im', 's5_b_re', 's5_b_im', 's5_c_re', 's5_c_im', 's5_d', 's5_w_glu', 'ffn_w_gu', 'ffn_w_down', 'final_norm_g', 'loss_target', 'm_norm_mix_g', 'm_norm_ffn_g', 'm_gla_w_in', 'm_gla_w_a2', 'm_gla_b_a', 'm_gla_norm_g', 'm_gla_w_out', 'm_ssd_w_in', 'm_ssd_conv_w', 'm_ssd_conv_b', 'm_ssd_dt_bias', 'm_ssd_a_log', 'm_ssd_d', 'm_ssd_norm_g', 'm_ssd_w_out', 'm_s5_log_dt', 'm_s5_a_re', 'm_s5_a_im', 'm_s5_b_re', 'm_s5_b_im', 'm_s5_c_re', 'm_s5_c_im', 'm_s5_d', 'm_s5_w_glu', 'm_ffn_w_gu', 'm_ffn_w_down', 'm_final_norm_g', 'v_norm_mix_g', 'v_norm_ffn_g', 'v_gla_w_in', 'v_gla_w_a2', 'v_gla_b_a', 'v_gla_norm_g', 'v_gla_w_out', 'v_ssd_w_in', 'v_ssd_conv_w', 'v_ssd_conv_b', 'v_ssd_dt_bias', 'v_ssd_a_log', 'v_ssd_d', 'v_ssd_norm_g', 'v_ssd_w_out', 'v_s5_log_dt', 'v_s5_a_re', 'v_s5_a_im', 'v_s5_b_re', 'v_s5_b_im', 'v_s5_c_re', 'v_s5_c_im', 'v_s5_d', 'v_s5_w_glu', 'v_ffn_w_gu', 'v_ffn_w_down', 'v_final_norm_g']
TWIN_OUTPUTS = ['loss', 'grad_x', 'grad_norm_mix_g', 'grad_norm_ffn_g', 'grad_gla_w_in', 'grad_gla_w_a2', 'grad_gla_b_a', 'grad_gla_norm_g', 'grad_gla_w_out', 'grad_ssd_w_in', 'grad_ssd_conv_w', 'grad_ssd_conv_b', 'grad_ssd_dt_bias', 'grad_ssd_a_log', 'grad_ssd_d', 'grad_ssd_norm_g', 'grad_ssd_w_out', 'grad_s5_log_dt', 'grad_s5_a_re', 'grad_s5_a_im', 'grad_s5_b_re', 'grad_s5_b_im', 'grad_s5_c_re', 'grad_s5_c_im', 'grad_s5_d', 'grad_s5_w_glu', 'grad_ffn_w_gu', 'grad_ffn_w_down', 'grad_final_norm_g', 'delta_norm_mix_g', 'delta_norm_ffn_g', 'delta_gla_w_in', 'delta_gla_w_a2', 'delta_gla_b_a', 'delta_gla_norm_g', 'delta_gla_w_out', 'delta_ssd_w_in', 'delta_ssd_conv_w', 'delta_ssd_conv_b', 'delta_ssd_dt_bias', 'delta_ssd_a_log', 'delta_ssd_d', 'delta_ssd_norm_g', 'delta_ssd_w_out', 'delta_s5_log_dt', 'delta_s5_a_re', 'delta_s5_a_im', 'delta_s5_b_re', 'delta_s5_b_im', 'delta_s5_c_re', 'delta_s5_c_im', 'delta_s5_d', 'delta_s5_w_glu', 'delta_ffn_w_gu', 'delta_ffn_w_down', 'delta_final_norm_g', 'new_m_norm_mix_g', 'new_m_norm_ffn_g', 'new_m_gla_w_in', 'new_m_gla_w_a2', 'new_m_gla_b_a', 'new_m_gla_norm_g', 'new_m_gla_w_out', 'new_m_ssd_w_in', 'new_m_ssd_conv_w', 'new_m_ssd_conv_b', 'new_m_ssd_dt_bias', 'new_m_ssd_a_log', 'new_m_ssd_d', 'new_m_ssd_norm_g', 'new_m_ssd_w_out', 'new_m_s5_log_dt', 'new_m_s5_a_re', 'new_m_s5_a_im', 'new_m_s5_b_re', 'new_m_s5_b_im', 'new_m_s5_c_re', 'new_m_s5_c_im', 'new_m_s5_d', 'new_m_s5_w_glu', 'new_m_ffn_w_gu', 'new_m_ffn_w_down', 'new_m_final_norm_g', 'new_v_norm_mix_g', 'new_v_norm_ffn_g', 'new_v_gla_w_in', 'new_v_gla_w_a2', 'new_v_gla_b_a', 'new_v_gla_norm_g', 'new_v_gla_w_out', 'new_v_ssd_w_in', 'new_v_ssd_conv_w', 'new_v_ssd_conv_b', 'new_v_ssd_dt_bias', 'new_v_ssd_a_log', 'new_v_ssd_d', 'new_v_ssd_norm_g', 'new_v_ssd_w_out', 'new_v_s5_log_dt', 'new_v_s5_a_re', 'new_v_s5_a_im', 'new_v_s5_b_re', 'new_v_s5_b_im', 'new_v_s5_c_re', 'new_v_s5_c_im', 'new_v_s5_d', 'new_v_s5_w_glu', 'new_v_ffn_w_gu', 'new_v_ffn_w_down', 'new_v_final_norm_g']
TWIN_LEAF_KINDS = {'loss': 'loss', 'grad_x': 'grad_x', 'grad_norm_mix_g': 'grad_w', 'grad_norm_ffn_g': 'grad_w', 'grad_gla_w_in': 'grad_w', 'grad_gla_w_a2': 'grad_w', 'grad_gla_b_a': 'grad_w', 'grad_gla_norm_g': 'grad_w', 'grad_gla_w_out': 'grad_w', 'grad_ssd_w_in': 'grad_w', 'grad_ssd_conv_w': 'grad_w', 'grad_ssd_conv_b': 'grad_w', 'grad_ssd_dt_bias': 'grad_w', 'grad_ssd_a_log': 'grad_w', 'grad_ssd_d': 'grad_w', 'grad_ssd_norm_g': 'grad_w', 'grad_ssd_w_out': 'grad_w', 'grad_s5_log_dt': 'grad_w', 'grad_s5_a_re': 'grad_w', 'grad_s5_a_im': 'grad_w', 'grad_s5_b_re': 'grad_w', 'grad_s5_b_im': 'grad_w', 'grad_s5_c_re': 'grad_w', 'grad_s5_c_im': 'grad_w', 'grad_s5_d': 'grad_w', 'grad_s5_w_glu': 'grad_w', 'grad_ffn_w_gu': 'grad_w', 'grad_ffn_w_down': 'grad_w', 'grad_final_norm_g': 'grad_w', 'delta_norm_mix_g': 'delta_w', 'delta_norm_ffn_g': 'delta_w', 'delta_gla_w_in': 'delta_w', 'delta_gla_w_a2': 'delta_w', 'delta_gla_b_a': 'delta_w', 'delta_gla_norm_g': 'delta_w', 'delta_gla_w_out': 'delta_w', 'delta_ssd_w_in': 'delta_w', 'delta_ssd_conv_w': 'delta_w', 'delta_ssd_conv_b': 'delta_w', 'delta_ssd_dt_bias': 'delta_w', 'delta_ssd_a_log': 'delta_w', 'delta_ssd_d': 'delta_w', 'delta_ssd_norm_g': 'delta_w', 'delta_ssd_w_out': 'delta_w', 'delta_s5_log_dt': 'delta_w', 'delta_s5_a_re': 'delta_w', 'delta_s5_a_im': 'delta_w', 'delta_s5_b_re': 'delta_w', 'delta_s5_b_im': 'delta_w', 'delta_s5_c_re': 'delta_w', 'delta_s5_c_im': 'delta_w', 'delta_s5_d': 'delta_w', 'delta_s5_w_glu': 'delta_w', 'delta_ffn_w_gu': 'delta_w', 'delta_ffn_w_down': 'delta_w', 'delta_final_norm_g': 'delta_w', 'new_m_norm_mix_g': 'new_m', 'new_m_norm_ffn_g': 'new_m', 'new_m_gla_w_in': 'new_m', 'new_m_gla_w_a2': 'new_m', 'new_m_gla_b_a': 'new_m', 'new_m_gla_norm_g': 'new_m', 'new_m_gla_w_out': 'new_m', 'new_m_ssd_w_in': 'new_m', 'new_m_ssd_conv_w': 'new_m', 'new_m_ssd_conv_b': 'new_m', 'new_m_ssd_dt_bias': 'new_m', 'new_m_ssd_a_log': 'new_m', 'new_m_ssd_d': 'new_m', 'new_m_ssd_norm_g': 'new_m', 'new_m_ssd_w_out': 'new_m', 'new_m_s5_log_dt': 'new_m', 'new_m_s5_a_re': 'new_m', 'new_m_s5_a_im': 'new_m', 'new_m_s5_b_re': 'new_m', 'new_m_s5_b_im': 'new_m', 'new_m_s5_c_re': 'new_m', 'new_m_s5_c_im': 'new_m', 'new_m_s5_d': 'new_m', 'new_m_s5_w_glu': 'new_m', 'new_m_ffn_w_gu': 'new_m', 'new_m_ffn_w_down': 'new_m', 'new_m_final_norm_g': 'new_m', 'new_v_norm_mix_g': 'new_v', 'new_v_norm_ffn_g': 'new_v', 'new_v_gla_w_in': 'new_v', 'new_v_gla_w_a2': 'new_v', 'new_v_gla_b_a': 'new_v', 'new_v_gla_norm_g': 'new_v', 'new_v_gla_w_out': 'new_v', 'new_v_ssd_w_in': 'new_v', 'new_v_ssd_conv_w': 'new_v', 'new_v_ssd_conv_b': 'new_v', 'new_v_ssd_dt_bias': 'new_v', 'new_v_ssd_a_log': 'new_v', 'new_v_ssd_d': 'new_v', 'new_v_ssd_norm_g': 'new_v', 'new_v_ssd_w_out': 'new_v', 'new_v_s5_log_dt': 'new_v', 'new_v_s5_a_re': 'new_v', 'new_v_s5_a_im': 'new_v', 'new_v_s5_b_re': 'new_v', 'new_v_s5_b_im': 'new_v', 'new_v_s5_c_re': 'new_v', 'new_v_s5_c_im': 'new_v', 'new_v_s5_d': 'new_v', 'new_v_s5_w_glu': 'new_v', 'new_v_ffn_w_gu': 'new_v', 'new_v_ffn_w_down': 'new_v', 'new_v_final_norm_g': 'new_v'}


def _forward(args):
    return _fwd_reference(*[args[k] for k in FWD_PARAMS])


def _output_shape():
    def fwd():
        inp = _fwd_setup_inputs(0)
        return _fwd_reference(*[inp[k] for k in FWD_PARAMS])
    out = _jax.eval_shape(fwd)
    return out.shape, out.dtype

N_MICROBATCH = 1
ADAM_LR = 0.001
ADAM_B1 = 0.9
ADAM_B2 = 0.999
ADAM_EPS = 1e-08
ADAM_WD = 0.01
ADAM_STEP = 10
PER_EXAMPLE_BATCH_AXIS = {'x': 0, 'loss_target': 0}
SHARED_INPUTS = []
_WEIGHT_DTYPES = {'norm_mix_g': _jnp.float32, 'norm_ffn_g': _jnp.float32, 'gla_w_in': _jnp.float32, 'gla_w_a2': _jnp.float32, 'gla_b_a': _jnp.float32, 'gla_norm_g': _jnp.float32, 'gla_w_out': _jnp.float32, 'ssd_w_in': _jnp.float32, 'ssd_conv_w': _jnp.float32, 'ssd_conv_b': _jnp.float32, 'ssd_dt_bias': _jnp.float32, 'ssd_a_log': _jnp.float32, 'ssd_d': _jnp.float32, 'ssd_norm_g': _jnp.float32, 'ssd_w_out': _jnp.float32, 's5_log_dt': _jnp.float32, 's5_a_re': _jnp.float32, 's5_a_im': _jnp.float32, 's5_b_re': _jnp.float32, 's5_b_im': _jnp.float32, 's5_c_re': _jnp.float32, 's5_c_im': _jnp.float32, 's5_d': _jnp.float32, 's5_w_glu': _jnp.float32, 'ffn_w_gu': _jnp.float32, 'ffn_w_down': _jnp.float32, 'final_norm_g': _jnp.float32}
MOMENT_SCALE = {'norm_mix_g': 2.971948e-01, 'norm_ffn_g': 1.764177e-01, 'gla_w_in': 1.976275e-01, 'gla_w_a2': 2.575029e-02, 'gla_b_a': 9.539218e-02, 'gla_norm_g': 1.693281e-01, 'gla_w_out': 1.681456e-01, 'ssd_w_in': 1.209950e-01, 'ssd_conv_w': 1.048717e-01, 'ssd_conv_b': 1.595577e-01, 'ssd_dt_bias': 2.983272e-01, 'ssd_a_log': 3.576608e-01, 'ssd_d': 6.123351e-01, 'ssd_norm_g': 1.512276e-01, 'ssd_w_out': 2.028818e-01, 's5_log_dt': 2.119008e+00, 's5_a_re': 5.146875e-03, 's5_a_im': 4.300005e-03, 's5_b_re': 2.595136e-03, 's5_b_im': 2.542821e-03, 's5_c_re': 5.116817e-03, 's5_c_im': 4.942638e-03, 's5_d': 7.931843e-02, 's5_w_glu': 4.574361e-02, 'ffn_w_gu': 7.367875e-02, 'ffn_w_down': 1.201962e-01, 'final_norm_g': 6.396264e+01}


def _to_microbatches(a, axis):
    t = _jnp.moveaxis(a, axis, 0)
    t = t.reshape((N_MICROBATCH, t.shape[0] // N_MICROBATCH) + t.shape[1:])
    return _jnp.moveaxis(t, 1, axis + 1)


def setup_inputs(seed: int = 0) -> dict:
    inp = _fwd_setup_inputs(seed)
    key = _jax.random.fold_in(_jax.random.key(seed), 7919)
    shape, _ = _output_shape()
    out = dict(inp)
    out["loss_target"] = _jax.random.normal(_jax.random.fold_in(key, 0), shape, _jnp.float32)
    for i, name in enumerate(TWIN_WEIGHTS):
        w = inp[name].astype(_jnp.float32)
        if MOMENT_SCALE is None:
            s = _jnp.sqrt(_jnp.mean(_jnp.square(w)) + 1e-30)
        else:
            s = MOMENT_SCALE[name]
        km, kv = _jax.random.split(_jax.random.fold_in(key, i + 1))
        out[name] = w
        out["m_" + name] = s * _jax.random.normal(km, w.shape, _jnp.float32)
        out["v_" + name] = (s * s) * _jax.random.uniform(kv, w.shape, _jnp.float32, 0.5, 1.5)
    if N_MICROBATCH > 1:
        for name, axis in PER_EXAMPLE_BATCH_AXIS.items():
            out[name] = _to_microbatches(out[name], axis)
    return {'x': out['x'], 'norm_mix_g': out['norm_mix_g'], 'norm_ffn_g': out['norm_ffn_g'], 'gla_w_in': out['gla_w_in'], 'gla_w_a2': out['gla_w_a2'], 'gla_b_a': out['gla_b_a'], 'gla_norm_g': out['gla_norm_g'], 'gla_w_out': out['gla_w_out'], 'ssd_w_in': out['ssd_w_in'], 'ssd_conv_w': out['ssd_conv_w'], 'ssd_conv_b': out['ssd_conv_b'], 'ssd_dt_bias': out['ssd_dt_bias'], 'ssd_a_log': out['ssd_a_log'], 'ssd_d': out['ssd_d'], 'ssd_norm_g': out['ssd_norm_g'], 'ssd_w_out': out['ssd_w_out'], 's5_log_dt': out['s5_log_dt'], 's5_a_re': out['s5_a_re'], 's5_a_im': out['s5_a_im'], 's5_b_re': out['s5_b_re'], 's5_b_im': out['s5_b_im'], 's5_c_re': out['s5_c_re'], 's5_c_im': out['s5_c_im'], 's5_d': out['s5_d'], 's5_w_glu': out['s5_w_glu'], 'ffn_w_gu': out['ffn_w_gu'], 'ffn_w_down': out['ffn_w_down'], 'final_norm_g': out['final_norm_g'], 'loss_target': out['loss_target'], 'm_norm_mix_g': out['m_norm_mix_g'], 'm_norm_ffn_g': out['m_norm_ffn_g'], 'm_gla_w_in': out['m_gla_w_in'], 'm_gla_w_a2': out['m_gla_w_a2'], 'm_gla_b_a': out['m_gla_b_a'], 'm_gla_norm_g': out['m_gla_norm_g'], 'm_gla_w_out': out['m_gla_w_out'], 'm_ssd_w_in': out['m_ssd_w_in'], 'm_ssd_conv_w': out['m_ssd_conv_w'], 'm_ssd_conv_b': out['m_ssd_conv_b'], 'm_ssd_dt_bias': out['m_ssd_dt_bias'], 'm_ssd_a_log': out['m_ssd_a_log'], 'm_ssd_d': out['m_ssd_d'], 'm_ssd_norm_g': out['m_ssd_norm_g'], 'm_ssd_w_out': out['m_ssd_w_out'], 'm_s5_log_dt': out['m_s5_log_dt'], 'm_s5_a_re': out['m_s5_a_re'], 'm_s5_a_im': out['m_s5_a_im'], 'm_s5_b_re': out['m_s5_b_re'], 'm_s5_b_im': out['m_s5_b_im'], 'm_s5_c_re': out['m_s5_c_re'], 'm_s5_c_im': out['m_s5_c_im'], 'm_s5_d': out['m_s5_d'], 'm_s5_w_glu': out['m_s5_w_glu'], 'm_ffn_w_gu': out['m_ffn_w_gu'], 'm_ffn_w_down': out['m_ffn_w_down'], 'm_final_norm_g': out['m_final_norm_g'], 'v_norm_mix_g': out['v_norm_mix_g'], 'v_norm_ffn_g': out['v_norm_ffn_g'], 'v_gla_w_in': out['v_gla_w_in'], 'v_gla_w_a2': out['v_gla_w_a2'], 'v_gla_b_a': out['v_gla_b_a'], 'v_gla_norm_g': out['v_gla_norm_g'], 'v_gla_w_out': out['v_gla_w_out'], 'v_ssd_w_in': out['v_ssd_w_in'], 'v_ssd_conv_w': out['v_ssd_conv_w'], 'v_ssd_conv_b': out['v_ssd_conv_b'], 'v_ssd_dt_bias': out['v_ssd_dt_bias'], 'v_ssd_a_log': out['v_ssd_a_log'], 'v_ssd_d': out['v_ssd_d'], 'v_ssd_norm_g': out['v_ssd_norm_g'], 'v_ssd_w_out': out['v_ssd_w_out'], 'v_s5_log_dt': out['v_s5_log_dt'], 'v_s5_a_re': out['v_s5_a_re'], 'v_s5_a_im': out['v_s5_a_im'], 'v_s5_b_re': out['v_s5_b_re'], 'v_s5_b_im': out['v_s5_b_im'], 'v_s5_c_re': out['v_s5_c_re'], 'v_s5_c_im': out['v_s5_c_im'], 'v_s5_d': out['v_s5_d'], 'v_s5_w_glu': out['v_s5_w_glu'], 'v_ffn_w_gu': out['v_ffn_w_gu'], 'v_ffn_w_down': out['v_ffn_w_down'], 'v_final_norm_g': out['v_final_norm_g']}


def _loss(weights, diff, rest, loss_target):
    with _jax.named_scope("forward"):
        args = {**rest, TWIN_DIFF_INPUT: diff, **{k: w.astype(_WEIGHT_DTYPES[k]) for k, w in weights.items()}}
        y = _forward(args)
    with _jax.named_scope("loss_head"):
        err = _jnp.square(y.astype(_jnp.float32) - loss_target)
        return 0.5 * _jnp.sum(_jnp.mean(err, axis=-1)) if err.ndim else 0.5 * err


def _adamw(w, g, m, v):
    m = ADAM_B1 * m + (1.0 - ADAM_B1) * g
    v = ADAM_B2 * v + (1.0 - ADAM_B2) * _jnp.square(g)
    m_hat = m / (1.0 - ADAM_B1 ** ADAM_STEP)
    v_hat = v / (1.0 - ADAM_B2 ** ADAM_STEP)
    delta = -ADAM_LR * (m_hat / (_jnp.sqrt(v_hat) + ADAM_EPS) + ADAM_WD * w)
    return delta, m, v


def reference(x, norm_mix_g, norm_ffn_g, gla_w_in, gla_w_a2, gla_b_a, gla_norm_g, gla_w_out, ssd_w_in, ssd_conv_w, ssd_conv_b, ssd_dt_bias, ssd_a_log, ssd_d, ssd_norm_g, ssd_w_out, s5_log_dt, s5_a_re, s5_a_im, s5_b_re, s5_b_im, s5_c_re, s5_c_im, s5_d, s5_w_glu, ffn_w_gu, ffn_w_down, final_norm_g, loss_target, m_norm_mix_g, m_norm_ffn_g, m_gla_w_in, m_gla_w_a2, m_gla_b_a, m_gla_norm_g, m_gla_w_out, m_ssd_w_in, m_ssd_conv_w, m_ssd_conv_b, m_ssd_dt_bias, m_ssd_a_log, m_ssd_d, m_ssd_norm_g, m_ssd_w_out, m_s5_log_dt, m_s5_a_re, m_s5_a_im, m_s5_b_re, m_s5_b_im, m_s5_c_re, m_s5_c_im, m_s5_d, m_s5_w_glu, m_ffn_w_gu, m_ffn_w_down, m_final_norm_g, v_norm_mix_g, v_norm_ffn_g, v_gla_w_in, v_gla_w_a2, v_gla_b_a, v_gla_norm_g, v_gla_w_out, v_ssd_w_in, v_ssd_conv_w, v_ssd_conv_b, v_ssd_dt_bias, v_ssd_a_log, v_ssd_d, v_ssd_norm_g, v_ssd_w_out, v_s5_log_dt, v_s5_a_re, v_s5_a_im, v_s5_b_re, v_s5_b_im, v_s5_c_re, v_s5_c_im, v_s5_d, v_s5_w_glu, v_ffn_w_gu, v_ffn_w_down, v_final_norm_g):
    given = dict(x=x, norm_mix_g=norm_mix_g, norm_ffn_g=norm_ffn_g, gla_w_in=gla_w_in, gla_w_a2=gla_w_a2, gla_b_a=gla_b_a, gla_norm_g=gla_norm_g, gla_w_out=gla_w_out, ssd_w_in=ssd_w_in, ssd_conv_w=ssd_conv_w, ssd_conv_b=ssd_conv_b, ssd_dt_bias=ssd_dt_bias, ssd_a_log=ssd_a_log, ssd_d=ssd_d, ssd_norm_g=ssd_norm_g, ssd_w_out=ssd_w_out, s5_log_dt=s5_log_dt, s5_a_re=s5_a_re, s5_a_im=s5_a_im, s5_b_re=s5_b_re, s5_b_im=s5_b_im, s5_c_re=s5_c_re, s5_c_im=s5_c_im, s5_d=s5_d, s5_w_glu=s5_w_glu, ffn_w_gu=ffn_w_gu, ffn_w_down=ffn_w_down, final_norm_g=final_norm_g, loss_target=loss_target, m_norm_mix_g=m_norm_mix_g, m_norm_ffn_g=m_norm_ffn_g, m_gla_w_in=m_gla_w_in, m_gla_w_a2=m_gla_w_a2, m_gla_b_a=m_gla_b_a, m_gla_norm_g=m_gla_norm_g, m_gla_w_out=m_gla_w_out, m_ssd_w_in=m_ssd_w_in, m_ssd_conv_w=m_ssd_conv_w, m_ssd_conv_b=m_ssd_conv_b, m_ssd_dt_bias=m_ssd_dt_bias, m_ssd_a_log=m_ssd_a_log, m_ssd_d=m_ssd_d, m_ssd_norm_g=m_ssd_norm_g, m_ssd_w_out=m_ssd_w_out, m_s5_log_dt=m_s5_log_dt, m_s5_a_re=m_s5_a_re, m_s5_a_im=m_s5_a_im, m_s5_b_re=m_s5_b_re, m_s5_b_im=m_s5_b_im, m_s5_c_re=m_s5_c_re, m_s5_c_im=m_s5_c_im, m_s5_d=m_s5_d, m_s5_w_glu=m_s5_w_glu, m_ffn_w_gu=m_ffn_w_gu, m_ffn_w_down=m_ffn_w_down, m_final_norm_g=m_final_norm_g, v_norm_mix_g=v_norm_mix_g, v_norm_ffn_g=v_norm_ffn_g, v_gla_w_in=v_gla_w_in, v_gla_w_a2=v_gla_w_a2, v_gla_b_a=v_gla_b_a, v_gla_norm_g=v_gla_norm_g, v_gla_w_out=v_gla_w_out, v_ssd_w_in=v_ssd_w_in, v_ssd_conv_w=v_ssd_conv_w, v_ssd_conv_b=v_ssd_conv_b, v_ssd_dt_bias=v_ssd_dt_bias, v_ssd_a_log=v_ssd_a_log, v_ssd_d=v_ssd_d, v_ssd_norm_g=v_ssd_norm_g, v_ssd_w_out=v_ssd_w_out, v_s5_log_dt=v_s5_log_dt, v_s5_a_re=v_s5_a_re, v_s5_a_im=v_s5_a_im, v_s5_b_re=v_s5_b_re, v_s5_b_im=v_s5_b_im, v_s5_c_re=v_s5_c_re, v_s5_c_im=v_s5_c_im, v_s5_d=v_s5_d, v_s5_w_glu=v_s5_w_glu, v_ffn_w_gu=v_ffn_w_gu, v_ffn_w_down=v_ffn_w_down, v_final_norm_g=v_final_norm_g)
    weights = {n: given[n] for n in TWIN_WEIGHTS}
    shared = {n: given[n] for n in SHARED_INPUTS}
    per_example = {n: given[n] for n in ['x']}
    grad_fn = _jax.value_and_grad(_loss, argnums=(0, 1))

    def one_microbatch(ex, loss_target):
        ex = dict(ex)
        diff = ex.pop(TWIN_DIFF_INPUT)
        return grad_fn(weights, diff, {**shared, **ex}, loss_target)

    if N_MICROBATCH == 1:
        loss, (grad_w, grad_x) = one_microbatch(per_example, given["loss_target"])
    else:
        def body(carry, xs):
            loss_sum, grad_sum = carry
            l_k, (gw_k, gx_k) = one_microbatch(xs[0], xs[1])
            with _jax.named_scope("update"):
                return (loss_sum + l_k, _jax.tree.map(_jnp.add, grad_sum, gw_k)), gx_k

        init = (_jnp.zeros((), _jnp.float32), _jax.tree.map(_jnp.zeros_like, weights))
        (loss, grad_w), grad_x = _jax.lax.scan(body, init, (per_example, given["loss_target"]))
    with _jax.named_scope("update"):
        delta_w, new_m, new_v = {}, {}, {}
        for n in TWIN_WEIGHTS:
            delta_w[n], new_m[n], new_v[n] = _adamw(weights[n], grad_w[n], given["m_" + n], given["v_" + n])
    return (loss, grad_x, *[grad_w[n] for n in TWIN_WEIGHTS], *[delta_w[n] for n in TWIN_WEIGHTS],
            *[new_m[n] for n in TWIN_WEIGHTS], *[new_v[n] for n in TWIN_WEIGHTS])
```

```python
import functools
import math

import jax
import jax.numpy as jnp
from jax import lax
from jax.experimental import pallas as pl
from jax.experimental.pallas import tpu as pltpu

F32 = jnp.float32
BF16 = jnp.bfloat16

D = 1024
DEPTH = 4
CH = 64
EPS = 1e-6
GLA_H, GLA_DK, GLA_DV, GLA_RANK, GLA_TAU = 4, 128, 256, 16, 16.0
GLA_QK = GLA_H * GLA_DK
GLA_VD = GLA_H * GLA_DV
GLA_IN = 2 * GLA_QK + 2 * GLA_VD + GLA_RANK
GLA_INP = 3200
SSD_DI, SSD_HD, SSD_H, SSD_G, SSD_N, SSD_K = 2048, 64, 32, 8, 128, 4
SSD_GN = SSD_G * SSD_N
SSD_CONV = SSD_DI + 2 * SSD_GN
SSD_IN = SSD_DI + SSD_CONV + SSD_H
SSD_INP = 6272
S5_GS, S5_NG, S5_P = 16, 64, 64
S5_BLK = 8
FFN_H = 2816
LANE = 128
VMEM_LIMIT = 52 * 1024 * 1024

ADAM_LR, ADAM_B1, ADAM_B2, ADAM_EPS, ADAM_WD, ADAM_STEP = 0.001, 0.9, 0.999, 1e-08, 0.01, 10

_ARB = "arbitrary"


def _cparams(n):
    return pltpu.CompilerParams(dimension_semantics=(_ARB,) * n, vmem_limit_bytes=VMEM_LIMIT)


def _pick(n, target, mult=LANE):
    best = None
    for c in range(mult, min(n, target) + 1, mult):
        if n % c == 0:
            best = c
    return best if best is not None else n


_DN = {"nn": (((1,), (0,)), ((), ())), "nt": (((1,), (1,)), ((), ())), "tn": (((0,), (0,)), ((), ()))}


def _dot_raw(a, b, form):
    return lax.dot_general(a.astype(BF16), b.astype(BF16), _DN[form], preferred_element_type=F32)


@functools.partial(jax.custom_vjp, nondiff_argnums=(2,))
def bdot(a, b, form):
    return _dot_raw(a, b, form)


def _bdot_fwd(a, b, form):
    return _dot_raw(a, b, form), (a, b)


def _bdot_bwd(form, res, g):
    a, b = res
    if form == "nn":
        return _dot_raw(g, b, "nt"), _dot_raw(a, g, "tn")
    if form == "nt":
        return _dot_raw(g, b, "nn"), _dot_raw(g, a, "tn")
    return _dot_raw(b, g, "nt"), _dot_raw(a, g, "nn")


bdot.defvjp(_bdot_fwd, _bdot_bwd)


def _hdot(a, b):
    return jnp.dot(a, b, precision=lax.Precision.HIGHEST, preferred_element_type=F32)


@jax.custom_vjp
def cdot_left(c, ct, x):
    return _hdot(c, x)


def _cdl_fwd(c, ct, x):
    return _hdot(c, x), (c, ct)


def _cdl_bwd(res, g):
    c, ct = res
    return jnp.zeros_like(c), jnp.zeros_like(ct), _hdot(ct, g)


cdot_left.defvjp(_cdl_fwd, _cdl_bwd)


@jax.custom_vjp
def cdot_right(x, c, ct):
    return _hdot(x, c)


def _cdr_fwd(x, c, ct):
    return _hdot(x, c), (c, ct)


def _cdr_bwd(res, g):
    c, ct = res
    return _hdot(g, ct), jnp.zeros_like(c), jnp.zeros_like(ct)


cdot_right.defvjp(_cdr_fwd, _cdr_bwd)


def _sigmoid(x):
    return 1.0 / (1.0 + jnp.exp(-x))


def _silu(x):
    return x * _sigmoid(x)


def _softplus(x):
    return jnp.maximum(x, 0.0) + jnp.log(1.0 + jnp.exp(-jnp.abs(x)))


def _log_sigmoid(x):
    return jnp.minimum(x, 0.0) - jnp.log(1.0 + jnp.exp(-jnp.abs(x)))


def _gelu(x):
    c = math.sqrt(2.0 / math.pi)
    return 0.5 * x * (1.0 + jnp.tanh(c * (x + 0.044715 * (x * x * x))))


def _rms(x, g):
    return x * lax.rsqrt(jnp.mean(x * x, axis=-1, keepdims=True) + EPS) * g


def _iota2(shape, axis):
    return lax.broadcasted_iota(jnp.int32, shape, axis)


def matmul(a, b, form, *, name, G=1, out_dtype=F32, add=None):
    if form in ("nn", "nt"):
        M = a.shape[0]
        K = a.shape[1] // G
        N = b.shape[2] if form == "nn" else b.shape[1]
        bm, bn, bk = min(M, 512), _pick(N, 1536), _pick(K, 1024)
        nj, nk = N // bn, K // bk
        grid = (G, M // bm, nj, nk)
        a_spec = pl.BlockSpec((bm, bk), lambda g, i, j, k: (i, g * nk + k))
        if form == "nn":
            b_spec = pl.BlockSpec((None, bk, bn), lambda g, i, j, k: (g, k, j))
        else:
            b_spec = pl.BlockSpec((None, bn, bk), lambda g, i, j, k: (g, j, k))
        o_spec = pl.BlockSpec((bm, bn), lambda g, i, j, k: (i, g * nj + j))
        out_shape = jax.ShapeDtypeStruct((M, G * N), out_dtype)
    else:
        T = a.shape[0]
        Ka, Nb = a.shape[1] // G, b.shape[1] // G
        bm, bn, bk = _pick(Ka, 512), _pick(Nb, 1536), min(T, 1024)
        ni, nj, nk = Ka // bm, Nb // bn, T // bk
        grid = (G, ni, nj, nk)
        a_spec = pl.BlockSpec((bk, bm), lambda g, i, j, k: (k, g * ni + i))
        b_spec = pl.BlockSpec((bk, bn), lambda g, i, j, k: (k, g * nj + j))
        o_spec = pl.BlockSpec((None, bm, bn), lambda g, i, j, k: (g, i, j))
        out_shape = jax.ShapeDtypeStruct((G, Ka, Nb), out_dtype)
    has_add = add is not None

    def body(*refs):
        if has_add:
            a_ref, b_ref, add_ref, o_ref, acc_ref = refs
        else:
            a_ref, b_ref, o_ref, acc_ref = refs
        k = pl.program_id(3)

        @pl.when(k == 0)
        def _():
            acc_ref[...] = jnp.zeros_like(acc_ref)

        acc_ref[...] += _dot_raw(a_ref[...], b_ref[...], form)

        @pl.when(k == nk - 1)
        def _():
            r = acc_ref[...]
            if has_add:
                r = r + add_ref[...].astype(F32)
            o_ref[...] = r.astype(o_ref.dtype)

    in_specs = [a_spec, b_spec]
    args = [a, b]
    if has_add:
        in_specs.append(o_spec)
        args.append(add)
    return pl.pallas_call(
        body, name=name, grid=grid, in_specs=in_specs, out_specs=o_spec, out_shape=out_shape,
        scratch_shapes=[pltpu.VMEM((bm, bn), F32)], compiler_params=_cparams(4),
    )(*args)


def _row_entry(e):
    return e if isinstance(e, tuple) else (e, e.shape[1], 0)


def _row_spec(bt, e):
    _, width, idx = e
    return pl.BlockSpec((bt, width), lambda i: (i, idx))


def _full_spec(p):
    return pl.BlockSpec(p.shape, lambda i: (0,) * p.ndim)


def rowwise(f, params, rows, outs, *, bt, name):
    rows = [_row_entry(e) for e in rows]
    T = rows[0][0].shape[0]
    bt = min(bt, T)
    np_, nr = len(params), len(rows)

    def body(*refs):
        p = tuple(r[...].astype(F32) for r in refs[:np_])
        rw = tuple(r[...].astype(F32) for r in refs[np_:np_ + nr])
        res = f(p, rw)
        for o_ref, o in zip(refs[np_ + nr:], res):
            o_ref[...] = o.astype(o_ref.dtype)

    res = pl.pallas_call(
        body, name=name, grid=(T // bt,),
        in_specs=[_full_spec(p) for p in params] + [_row_spec(bt, e) for e in rows],
        out_specs=[pl.BlockSpec((bt, w), lambda i: (i, 0)) for w, _ in outs],
        out_shape=[jax.ShapeDtypeStruct((T, w), dt) for w, dt in outs],
        compiler_params=_cparams(1),
    )(*params, *[e[0] for e in rows])
    return list(res)


def rowwise_vjp(f, params, rows, cts, drow_dtypes, *, bt, name):
    rows = [_row_entry(e) for e in rows]
    cts = [_row_entry(e) for e in cts]
    T = rows[0][0].shape[0]
    bt = min(bt, T)
    np_, nr, nc = len(params), len(rows), len(cts)
    want = [i for i, dt in enumerate(drow_dtypes) if dt is not None]

    def body(*refs):
        p = tuple(r[...].astype(F32) for r in refs[:np_])
        rw = tuple(r[...].astype(F32) for r in refs[np_:np_ + nr])
        ct = tuple(r[...].astype(F32) for r in refs[np_ + nr:np_ + nr + nc])
        outs = refs[np_ + nr + nc:]
        _, vjp = jax.vjp(f, p, rw)
        dp, dr = vjp(ct)

        @pl.when(pl.program_id(0) == 0)
        def _():
            for o in outs[:np_]:
                o[...] = jnp.zeros_like(o)

        for o, d in zip(outs[:np_], dp):
            o[...] += d
        for o, i in zip(outs[np_:], want):
            o[...] = dr[i].astype(o.dtype)

    res = pl.pallas_call(
        body, name=name, grid=(T // bt,),
        in_specs=[_full_spec(p) for p in params] + [_row_spec(bt, e) for e in rows] + [_row_spec(bt, e) for e in cts],
        out_specs=[_full_spec(p) for p in params] + [pl.BlockSpec((bt, rows[i][1]), lambda i_: (i_, 0)) for i in want],
        out_shape=[jax.ShapeDtypeStruct(p.shape, F32) for p in params]
        + [jax.ShapeDtypeStruct((T, rows[i][1]), drow_dtypes[i]) for i in want],
        compiler_params=_cparams(1),
    )(*params, *[e[0] for e in rows], *[e[0] for e in cts])
    res = list(res)
    return res[:np_], res[np_:]


def f_rmsnorm(p, r):
    return (_rms(r[0], p[0]),)


def f_rmsnorm_res(p, r):
    return (_rms(r[0], p[0]), r[0])


def f_swiglu(p, r):
    gu = r[0]
    return (_silu(gu[:, :FFN_H]) * gu[:, FFN_H:],)


def f_gla_gate_in(p, r):
    w_a2, b_a = p
    z = bdot(r[0], w_a2, "nn") + b_a
    return (_log_sigmoid(z) / GLA_TAU,)


def f_gla_gate_in_fwd(p, r):
    w_a2, b_a = p
    z = _dot_raw(r[0], w_a2, "nn") + b_a
    return (_log_sigmoid(z) / GLA_TAU,)


def f_gla_out(p, r):
    (ng,) = p
    o, rr = r
    parts = []
    for h in range(GLA_H):
        sl = slice(h * GLA_DV, (h + 1) * GLA_DV)
        parts.append(_rms(o[:, sl], ng[:, sl]) * _silu(rr[:, sl]))
    return (jnp.concatenate(parts, axis=1),)


def f_ssd_out(p, r):
    (ng,) = p
    y, z = r
    t = y * _silu(z)
    gsz = SSD_DI // SSD_G
    parts = []
    for g in range(SSD_G):
        sl = slice(g * gsz, (g + 1) * gsz)
        parts.append(_rms(t[:, sl], ng[:, sl]))
    return (jnp.concatenate(parts, axis=1),)


def f_s5_act(p, r):
    (dsk,) = p
    ycp, u = r
    return (_gelu(ycp + dsk * u),)


def f_glu_res(p, r):
    vg, h = r
    return (vg[:, :D] * _sigmoid(vg[:, D:]) + h,)


def f_glu(p, r):
    vg = r[0]
    return (vg[:, :D] * _sigmoid(vg[:, D:]),)


def loss_head(h, g, target, *, name):
    T = h.shape[0]
    bt = min(T, 256)

    def lossf(g_, h_, t_):
        e = _rms(h_, g_) - t_
        return (0.5 / D) * jnp.sum(e * e)

    def body(g_ref, h_ref, t_ref, loss_ref, dg_ref, dh_ref):
        @pl.when(pl.program_id(0) == 0)
        def _():
            loss_ref[...] = jnp.zeros_like(loss_ref)
            dg_ref[...] = jnp.zeros_like(dg_ref)

        val, vjp = jax.vjp(lossf, g_ref[...], h_ref[...], t_ref[...])
        dg, dh, _ = vjp(jnp.ones((), F32))
        loss_ref[...] += jnp.full(loss_ref.shape, val, F32)
        dg_ref[...] += dg
        dh_ref[...] = dh

    row = pl.BlockSpec((bt, D), lambda i: (i, 0))
    one = pl.BlockSpec((1, D), lambda i: (0, 0))
    loss, dg, dh = pl.pallas_call(
        body, name=name, grid=(T // bt,), in_specs=[one, row, row],
        out_specs=[pl.BlockSpec((1, LANE), lambda i: (0, 0)), one, row],
        out_shape=[jax.ShapeDtypeStruct((1, LANE), F32), jax.ShapeDtypeStruct((1, D), F32),
                   jax.ShapeDtypeStruct((T, D), F32)],
        compiler_params=_cparams(1),
    )(g, h, target)
    return loss[0, 0], dg, dh


def _gla_consts():
    r, c = _iota2((CH, CH), 0), _iota2((CH, CH), 1)
    return (r >= c).astype(F32), (r <= c).astype(F32), r >= c


def _gla_chunk(q, k, v, la, st, consts, dot, cdl):
    L, Lt, tril = consts
    lc = cdl(L, Lt, la)
    lend = lc[CH - 1:CH, :]
    e, ei = jnp.exp(lc), jnp.exp(-lc)
    qs = q * (GLA_DK ** -0.5)
    qf, kf, qb, kb = qs * e, k * ei, qs * ei, k * e
    sc = jnp.where(tril, dot(qf, kf, "nt"), dot(qb, kb, "nt"))
    o = dot(sc, v, "nn") + dot(qf, st, "nt")
    kd = k * jnp.exp(lend - lc)
    st_new = st * jnp.exp(lend) + dot(v, kd, "tn")
    return o, st_new


def _gla_block(q, k, v, la, st, nc, dot, cdl):
    consts = _gla_consts()
    outs = []
    for c in range(nc):
        sl = slice(c * CH, (c + 1) * CH)
        o, st = _gla_chunk(q[sl], k[sl], v[sl], la[sl], st, consts, dot, cdl)
        outs.append(o)
    return jnp.concatenate(outs, axis=0), st


def _gla_specs(rows, rev, nb):
    t = (lambda j: nb - 1 - j) if rev else (lambda j: j)
    q = pl.BlockSpec((rows, GLA_DK), lambda h, j: (t(j), h))
    k = pl.BlockSpec((rows, GLA_DK), lambda h, j: (t(j), GLA_H + h))
    v = pl.BlockSpec((rows, GLA_DV), lambda h, j: (t(j), GLA_H + h))
    la = pl.BlockSpec((rows, GLA_DK), lambda h, j: (t(j), h))
    ss = pl.BlockSpec((None, None, GLA_DV, GLA_DK), lambda h, j: (t(j), h, 0, 0))
    o = pl.BlockSpec((rows, GLA_DV), lambda h, j: (t(j), h))
    return q, k, v, la, ss, o


def gla_scan_fwd(proj, la, *, nc, name):
    T = proj.shape[0]
    rows = min(T, nc * CH)
    nc = rows // CH
    nb = T // rows
    q_s, k_s, v_s, la_s, ss_s, o_s = _gla_specs(rows, False, nb)

    def body(q_ref, k_ref, v_ref, la_ref, o_ref, ss_ref, st_ref):
        @pl.when(pl.program_id(1) == 0)
        def _():
            st_ref[...] = jnp.zeros_like(st_ref)

        st = st_ref[...]
        ss_ref[...] = st
        o, st = _gla_block(q_ref[...], k_ref[...], v_ref[...], la_ref[...], st, nc, _dot_raw, lambda c, ct, x: _hdot(c, x))
        o_ref[...] = o
        st_ref[...] = st

    return pl.pallas_call(
        body, name=name, grid=(GLA_H, nb), in_specs=[q_s, k_s, v_s, la_s], out_specs=[o_s, ss_s],
        out_shape=[jax.ShapeDtypeStruct((T, GLA_VD), F32), jax.ShapeDtypeStruct((nb, GLA_H, GLA_DV, GLA_DK), F32)],
        scratch_shapes=[pltpu.VMEM((GLA_DV, GLA_DK), F32)], compiler_params=_cparams(2),
    )(proj, proj, proj, la)


def gla_scan_bwd(proj, la, ss, do, *, nc, name):
    T = proj.shape[0]
    rows = min(T, nc * CH)
    nc = rows // CH
    nb = T // rows
    q_s, k_s, v_s, la_s, ss_s, o_s = _gla_specs(rows, True, nb)
    t = lambda j: nb - 1 - j
    dqk_s = pl.BlockSpec((rows, GLA_DK), lambda h, j: (t(j), h))

    def body(q_ref, k_ref, v_ref, la_ref, ss_ref, do_ref, dq_ref, dk_ref, dv_ref, dla_ref, dst_ref):
        @pl.when(pl.program_id(1) == 0)
        def _():
            dst_ref[...] = jnp.zeros_like(dst_ref)

        fn = lambda q, k, v, la_, st: _gla_block(q, k, v, la_, st, nc, bdot, cdot_left)
        _, vjp = jax.vjp(fn, q_ref[...], k_ref[...], v_ref[...], la_ref[...], ss_ref[...])
        dq, dk, dv, dla, dst = vjp((do_ref[...], dst_ref[...]))
        dq_ref[...] = dq.astype(dq_ref.dtype)
        dk_ref[...] = dk.astype(dk_ref.dtype)
        dv_ref[...] = dv.astype(dv_ref.dtype)
        dla_ref[...] = dla
        dst_ref[...] = dst

    return pl.pallas_call(
        body, name=name, grid=(GLA_H, nb), in_specs=[q_s, k_s, v_s, la_s, ss_s, o_s],
        out_specs=[dqk_s, dqk_s, o_s, dqk_s],
        out_shape=[jax.ShapeDtypeStruct((T, GLA_QK), BF16), jax.ShapeDtypeStruct((T, GLA_QK), BF16),
                   jax.ShapeDtypeStruct((T, GLA_VD), BF16), jax.ShapeDtypeStruct((T, GLA_QK), F32)],
        scratch_shapes=[pltpu.VMEM((GLA_DV, GLA_DK), F32)], compiler_params=_cparams(2),
    )(proj, proj, proj, la, ss, do)


_CONV_W = 512
_CONV_OFF = SSD_DI // _CONV_W


def _conv_pre(x, prev8, w_ref, b_ref):
    bt = x.shape[0]
    ext = jnp.concatenate([prev8, x], axis=0)
    shifted = []
    for j in range(SSD_K):
        s = SSD_K - 1 - j
        shifted.append(x if s == 0 else pltpu.roll(ext, s, 0)[8:8 + bt])
    pre = b_ref[...] + sum(w_ref[j:j + 1, :] * shifted[j] for j in range(SSD_K))
    return pre, shifted


def ssd_conv_fwd(proj, w, b, *, name):
    T = proj.shape[0]
    bt = min(T, 512)
    nb = T // bt

    def body(x_ref, w_ref, b_ref, o_ref, carry_ref):
        @pl.when(pl.program_id(1) == 0)
        def _():
            carry_ref[...] = jnp.zeros_like(carry_ref)

        x = x_ref[...]
        pre, _ = _conv_pre(x, carry_ref[...], w_ref, b_ref)
        o_ref[...] = _silu(pre)
        carry_ref[...] = x[bt - 8:, :]

    return pl.pallas_call(
        body, name=name, grid=(SSD_CONV // _CONV_W, nb),
        in_specs=[pl.BlockSpec((bt, _CONV_W), lambda c, t: (t, _CONV_OFF + c)),
                  pl.BlockSpec((SSD_K, _CONV_W), lambda c, t: (0, c)),
                  pl.BlockSpec((1, _CONV_W), lambda c, t: (0, c))],
        out_specs=pl.BlockSpec((bt, _CONV_W), lambda c, t: (t, c)),
        out_shape=jax.ShapeDtypeStruct((T, SSD_CONV), F32),
        scratch_shapes=[pltpu.VMEM((8, _CONV_W), F32)], compiler_params=_cparams(2),
    )(proj, w, b)


def ssd_conv_bwd(proj, w, b, dout, *, name):
    T = proj.shape[0]
    bt = min(T, 512)
    nb = T // bt
    r8 = bt // 8

    def body(x_ref, xp_ref, w_ref, b_ref, do_ref, dx_ref, dw_ref, db_ref, carry_ref):
        t = pl.program_id(1)

        @pl.when(t == 0)
        def _():
            carry_ref[...] = jnp.zeros_like(carry_ref)
            dw_ref[...] = jnp.zeros_like(dw_ref)
            db_ref[...] = jnp.zeros_like(db_ref)

        x = x_ref[...]
        prev8 = jnp.where(t == nb - 1, 0.0, xp_ref[...])
        pre, shifted = _conv_pre(x, prev8, w_ref, b_ref)
        sg = _sigmoid(pre)
        dpre = do_ref[...] * (sg * (1.0 + pre * (1.0 - sg)))
        ext = jnp.concatenate([dpre, carry_ref[...]], axis=0)
        dx = w_ref[SSD_K - 1:SSD_K, :] * dpre
        for j in range(SSD_K - 1):
            s = SSD_K - 1 - j
            dx = dx + w_ref[j:j + 1, :] * pltpu.roll(ext, bt + 8 - s, 0)[:bt]
        dx_ref[...] = dx.astype(dx_ref.dtype)
        dw_ref[...] += jnp.concatenate([jnp.sum(dpre * shifted[j], axis=0, keepdims=True) for j in range(SSD_K)], axis=0)
        db_ref[...] += jnp.sum(dpre, axis=0, keepdims=True)
        carry_ref[...] = dpre[:8, :]

    rt = lambda t: nb - 1 - t
    return pl.pallas_call(
        body, name=name, grid=(SSD_CONV // _CONV_W, nb),
        in_specs=[pl.BlockSpec((bt, _CONV_W), lambda c, t: (rt(t), _CONV_OFF + c)),
                  pl.BlockSpec((8, _CONV_W), lambda c, t: (jnp.maximum(rt(t) * r8 - 1, 0), _CONV_OFF + c)),
                  pl.BlockSpec((SSD_K, _CONV_W), lambda c, t: (0, c)),
                  pl.BlockSpec((1, _CONV_W), lambda c, t: (0, c)),
                  pl.BlockSpec((bt, _CONV_W), lambda c, t: (rt(t), c))],
        out_specs=[pl.BlockSpec((bt, _CONV_W), lambda c, t: (rt(t), c)),
                   pl.BlockSpec((SSD_K, _CONV_W), lambda c, t: (0, c)),
                   pl.BlockSpec((1, _CONV_W), lambda c, t: (0, c))],
        out_shape=[jax.ShapeDtypeStruct((T, SSD_CONV), BF16), jax.ShapeDtypeStruct((SSD_K, SSD_CONV), F32),
                   jax.ShapeDtypeStruct((1, SSD_CONV), F32)],
        scratch_shapes=[pltpu.VMEM((8, _CONV_W), F32)], compiler_params=_cparams(2),
    )(proj, proj, w, b, dout)


_SSD_U = 2 * CH


def _ssd_unit(xs, bm, cm, dtraw, dtb, alog, dsk, hp, g, dot, cdl, cdr):
    U = _SSD_U
    r, c = _iota2((U, U), 0), _iota2((U, U), 1)
    same = (r // CH) == (c // CH)
    Lb = (same & (r >= c)).astype(F32)
    Ub = (same & (r <= c)).astype(F32)
    lane = _iota2((1, U), 1)
    sub = _iota2((U, 1), 0)
    lo_lane = _iota2((1, 2 * SSD_HD), 1) < SSD_HD
    lo_sub = _iota2((2 * SSD_HD, 1), 0) < SSD_HD

    dt = _softplus(dtraw + dtb)
    da = dt * (-jnp.exp(alog))
    cum = cdl(Lb, Ub, da)
    cum_t = cdr(da.T, Ub, Lb)
    dt_t = dt.T
    cb = dot(cm, bm, "nt")
    ys = []
    new_hp = []
    for pr in range(2):
        xs_p = xs[:, pr * 2 * SSD_HD:(pr + 1) * 2 * SSD_HD]
        cols, dts, dks, y_in = [], [], [], []
        for jj in range(2):
            hd = g * (SSD_H // SSD_G) + 2 * pr + jj
            oh_l = (lane == hd).astype(F32)
            oh_s = (sub == hd).astype(F32)
            col = jnp.sum(cum * oh_l, axis=1, keepdims=True)
            row = jnp.sum(cum_t * oh_s, axis=0, keepdims=True)
            dtrow = jnp.sum(dt_t * oh_s, axis=0, keepdims=True)
            decay = jnp.exp(-jnp.abs(col - row))
            mix = jnp.where(same, cb * decay * dtrow, 0.0)
            y_in.append(dot(mix, xs_p, "nn"))
            cols.append(col)
            dts.append(jnp.sum(dt * oh_l, axis=1, keepdims=True))
            dks.append(jnp.sum(dsk * oh_l, axis=1, keepdims=True))
        y_intra = jnp.where(lo_lane, y_in[0], y_in[1])
        dsk_p = jnp.where(lo_lane, dks[0], dks[1])
        h = hp[pr]
        yc = []
        for ci in range(2):
            sl = slice(ci * CH, (ci + 1) * CH)
            ce = [cols[jj][ci * CH + CH - 1:ci * CH + CH, :] for jj in range(2)]
            ecum = jnp.where(lo_lane, jnp.exp(cols[0][sl]), jnp.exp(cols[1][sl]))
            y_inter = dot(cm[sl], h, "nt") * ecum
            wgt = jnp.where(lo_lane, dts[0][sl] * jnp.exp(ce[0] - cols[0][sl]), dts[1][sl] * jnp.exp(ce[1] - cols[1][sl]))
            xw = xs_p[sl] * wgt
            a_p = jnp.where(lo_sub, jnp.exp(ce[0]), jnp.exp(ce[1]))
            h = a_p * h + dot(xw, bm[sl], "tn")
            yc.append(y_intra[sl] + y_inter + dsk_p * xs_p[sl])
        ys.append(jnp.concatenate(yc, axis=0))
        new_hp.append(h)
    return jnp.concatenate(ys, axis=1), tuple(new_hp)


def _ssd_block(xs, bm, cm, dtraw, dtb, alog, dsk, hp, g, nu, dot, cdl, cdr):
    outs = []
    for u in range(nu):
        sl = slice(u * _SSD_U, (u + 1) * _SSD_U)
        y, hp = _ssd_unit(xs[sl], bm[sl], cm[sl], dtraw[sl], dtb, alog, dsk, hp, g, dot, cdl, cdr)
        outs.append(y)
    return jnp.concatenate(outs, axis=0), hp


def _ssd_specs(rows, rev, nb):
    t = (lambda j: nb - 1 - j) if rev else (lambda j: j)
    gw = SSD_DI // SSD_G
    xs = pl.BlockSpec((rows, gw), lambda j, g: (t(j), g))
    bm = pl.BlockSpec((rows, SSD_N), lambda j, g: (t(j), SSD_DI // SSD_N + g))
    cm = pl.BlockSpec((rows, SSD_N), lambda j, g: (t(j), (SSD_DI + SSD_GN) // SSD_N + g))
    dtr = pl.BlockSpec((rows, LANE), lambda j, g: (t(j), (SSD_DI + SSD_CONV) // LANE))
    par = pl.BlockSpec((1, LANE), lambda j, g: (0, 0))
    hs = pl.BlockSpec((None, None, 2, 2 * SSD_HD, SSD_N), lambda j, g: (t(j), g, 0, 0, 0))
    y = pl.BlockSpec((rows, gw), lambda j, g: (t(j), g))
    return xs, bm, cm, dtr, par, hs, y


def ssd_scan_fwd(xbc, proj, dtb, alog, dsk, *, nu, name):
    T = xbc.shape[0]
    rows = min(T, nu * _SSD_U)
    nu = rows // _SSD_U
    nb = T // rows
    xs_s, bm_s, cm_s, dt_s, par_s, hs_s, y_s = _ssd_specs(rows, False, nb)

    def body(xs_ref, bm_ref, cm_ref, dt_ref, dtb_ref, al_ref, dsk_ref, y_ref, hs_ref, h_ref):
        g = pl.program_id(1)

        @pl.when(pl.program_id(0) == 0)
        def _():
            h_ref[g] = jnp.zeros(h_ref.shape[1:], F32)

        hs_ref[...] = h_ref[g]
        hp = (h_ref[g, 0], h_ref[g, 1])
        y, hp = _ssd_block(xs_ref[...], bm_ref[...], cm_ref[...], dt_ref[...], dtb_ref[...], al_ref[...], dsk_ref[...],
                           hp, g, nu, _dot_raw, lambda c, ct, x: _hdot(c, x), lambda x, c, ct: _hdot(x, c))
        y_ref[...] = y
        h_ref[g, 0] = hp[0]
        h_ref[g, 1] = hp[1]

    return pl.pallas_call(
        body, name=name, grid=(nb, SSD_G), in_specs=[xs_s, bm_s, cm_s, dt_s, par_s, par_s, par_s],
        out_specs=[y_s, hs_s],
        out_shape=[jax.ShapeDtypeStruct((T, SSD_DI), F32), jax.ShapeDtypeStruct((nb, SSD_G, 2, 2 * SSD_HD, SSD_N), F32)],
        scratch_shapes=[pltpu.VMEM((SSD_G, 2, 2 * SSD_HD, SSD_N), F32)], compiler_params=_cparams(2),
    )(xbc, xbc, xbc, proj, dtb, alog, dsk)


def ssd_scan_bwd(xbc, proj, dtb, alog, dsk, hs, dy, *, nu, name):
    T = xbc.shape[0]
    rows = min(T, nu * _SSD_U)
    nu = rows // _SSD_U
    nb = T // rows
    xs_s, bm_s, cm_s, dt_s, par_s, hs_s, y_s = _ssd_specs(rows, True, nb)
    t = lambda j: nb - 1 - j
    n_s = pl.BlockSpec((rows, SSD_N), lambda j, g: (t(j), g))
    ddt_s = pl.BlockSpec((rows, LANE), lambda j, g: (t(j), 0))

    def body(xs_ref, bm_ref, cm_ref, dt_ref, dtb_ref, al_ref, dsk_ref, hs_ref, dy_ref,
             dxs_ref, dbm_ref, dcm_ref, ddt_ref, ddtb_ref, dal_ref, ddsk_ref, dh_ref):
        j, g = pl.program_id(0), pl.program_id(1)

        @pl.when(j == 0)
        def _():
            dh_ref[g] = jnp.zeros(dh_ref.shape[1:], F32)

        @pl.when((j == 0) & (g == 0))
        def _():
            ddtb_ref[...] = jnp.zeros_like(ddtb_ref)
            dal_ref[...] = jnp.zeros_like(dal_ref)
            ddsk_ref[...] = jnp.zeros_like(ddsk_ref)

        @pl.when(g == 0)
        def _():
            ddt_ref[...] = jnp.zeros_like(ddt_ref)

        fn = lambda xs, bm, cm, dtr, dtb_, al, dsk_, h0, h1: _ssd_block(
            xs, bm, cm, dtr, dtb_, al, dsk_, (h0, h1), g, nu, bdot, cdot_left, cdot_right)
        _, vjp = jax.vjp(fn, xs_ref[...], bm_ref[...], cm_ref[...], dt_ref[...], dtb_ref[...], al_ref[...], dsk_ref[...],
                         hs_ref[0], hs_ref[1])
        dxs, dbm, dcm, ddt, ddtb, dal, ddsk, dh0, dh1 = vjp((dy_ref[...], (dh_ref[g, 0], dh_ref[g, 1])))
        dxs_ref[...] = dxs
        dbm_ref[...] = dbm
        dcm_ref[...] = dcm
        ddt_ref[...] += ddt
        ddtb_ref[...] += ddtb
        dal_ref[...] += dal
        ddsk_ref[...] += ddsk
        dh_ref[g, 0] = dh0
        dh_ref[g, 1] = dh1

    return pl.pallas_call(
        body, name=name, grid=(nb, SSD_G), in_specs=[xs_s, bm_s, cm_s, dt_s, par_s, par_s, par_s, hs_s, y_s],
        out_specs=[y_s, n_s, n_s, ddt_s, par_s, par_s, par_s],
        out_shape=[jax.ShapeDtypeStruct((T, SSD_DI), F32), jax.ShapeDtypeStruct((T, SSD_GN), F32),
                   jax.ShapeDtypeStruct((T, SSD_GN), F32), jax.ShapeDtypeStruct((T, LANE), F32),
                   jax.ShapeDtypeStruct((1, LANE), F32), jax.ShapeDtypeStruct((1, LANE), F32),
                   jax.ShapeDtypeStruct((1, LANE), F32)],
        scratch_shapes=[pltpu.VMEM((SSD_G, 2, 2 * SSD_HD, SSD_N), F32)], compiler_params=_cparams(2),
    )(xbc, xbc, xbc, proj, dtb, alog, dsk, hs, dy)


def _s5_param_f(log_dt, a_re, a_im, bre_t, bim_t, cim, cdl):
    n = S5_NG * S5_GS
    r, c = _iota2((n, S5_NG), 0), _iota2((n, S5_NG), 1)
    E = ((r // S5_GS) == c).astype(F32)
    rt, ct = _iota2((S5_NG, n), 0), _iota2((S5_NG, n), 1)
    Et = ((ct // S5_GS) == rt).astype(F32)
    step = jnp.exp(log_dt)
    mag = jnp.exp(step * a_re)
    abr = mag * jnp.cos(step * a_im)
    abi = mag * jnp.sin(step * a_im)
    den = a_re * a_re + a_im * a_im
    nr, ni = abr - 1.0, abi
    fr = (nr * a_re + ni * a_im) / den
    fi = (ni * a_re - nr * a_im) / den
    Fr, Fi = cdl(E, Et, fr), cdl(E, Et, fi)
    bbr = Fr * bre_t - Fi * bim_t
    bbi = Fr * bim_t + Fi * bre_t
    return abr, abi, bbr, bbi, -cim


def _whole(a):
    return pl.BlockSpec(a.shape, lambda: (0,) * a.ndim)


def s5_param_fwd(args, *, name):
    def body(*refs):
        res = _s5_param_f(*[r[...] for r in refs[:6]], lambda c, ct, x: _hdot(c, x))
        for o, v in zip(refs[6:], res):
            o[...] = v

    shapes = [(S5_NG, S5_P), (S5_NG, S5_P)] + [(S5_NG * S5_GS, S5_P)] * 3
    return pl.pallas_call(
        body, name=name, in_specs=[_whole(a) for a in args], out_specs=[pl.BlockSpec(s, lambda: (0, 0)) for s in shapes],
        out_shape=[jax.ShapeDtypeStruct(s, F32) for s in shapes],
        compiler_params=pltpu.CompilerParams(vmem_limit_bytes=VMEM_LIMIT),
    )(*args)


def s5_param_bwd(args, cts, *, name):
    def body(*refs):
        fn = lambda *a: _s5_param_f(*a, cdot_left)
        _, vjp = jax.vjp(fn, *[r[...] for r in refs[:6]])
        grads = vjp(tuple(r[...] for r in refs[6:11]))
        for o, v in zip(refs[11:], grads):
            o[...] = v

    return pl.pallas_call(
        body, name=name, in_specs=[_whole(a) for a in list(args) + list(cts)],
        out_specs=[_whole(a) for a in args], out_shape=[jax.ShapeDtypeStruct(a.shape, F32) for a in args],
        compiler_params=pltpu.CompilerParams(vmem_limit_bytes=VMEM_LIMIT),
    )(*args, *cts)


_S5_W = S5_BLK * S5_P


def _cmul_add(xr, xi, pr, pi, sr, si):
    return xr + (pr * sr - pi * si), xi + (pr * si + pi * sr)


def s5_scan_fwd(bu, a_re, a_im, *, name):
    T = bu.shape[0]
    bt = min(T, 256)
    nb = T // bt

    def body(bu_ref, ar_ref, ai_ref, x_ref, carry_ref):
        @pl.when(pl.program_id(1) == 0)
        def _():
            carry_ref[...] = jnp.zeros_like(carry_ref)

        ar, ai = ar_ref[...], ai_ref[...]
        rows = _iota2((bt, _S5_W), 0)
        cr, ci = carry_ref[0:1, :], carry_ref[1:2, :]
        first = rows == 0
        xr = bu_ref[:, :_S5_W] + jnp.where(first, ar * cr - ai * ci, 0.0)
        xi = bu_ref[:, _S5_W:] + jnp.where(first, ar * ci + ai * cr, 0.0)
        pr, pi = ar, ai
        s = 1
        while s < bt:
            m = rows >= s
            sr = jnp.where(m, pltpu.roll(xr, s, 0), 0.0)
            si = jnp.where(m, pltpu.roll(xi, s, 0), 0.0)
            xr, xi = _cmul_add(xr, xi, pr, pi, sr, si)
            pr, pi = pr * pr - pi * pi, 2.0 * pr * pi
            s *= 2
        x_ref[:, :_S5_W] = xr
        x_ref[:, _S5_W:] = xi
        carry_ref[0:1, :] = xr[bt - 1:bt, :]
        carry_ref[1:2, :] = xi[bt - 1:bt, :]

    blk = pl.BlockSpec((bt, 2 * _S5_W), lambda g, t: (t, g))
    a_s = pl.BlockSpec((None, 1, _S5_W), lambda g, t: (g, 0, 0))
    return pl.pallas_call(
        body, name=name, grid=(S5_NG // S5_BLK, nb), in_specs=[blk, a_s, a_s], out_specs=blk,
        out_shape=jax.ShapeDtypeStruct(bu.shape, F32), scratch_shapes=[pltpu.VMEM((8, _S5_W), F32)],
        compiler_params=_cparams(2),
    )(bu, a_re, a_im)


def s5_scan_bwd(dx, x, a_re, a_im, *, name):
    T = dx.shape[0]
    bt = min(T, 256)
    nb = T // bt

    def body(g_ref, x_ref, ar_ref, ai_ref, lam_ref, dar_ref, dai_ref, carry_ref):
        @pl.when(pl.program_id(1) == 0)
        def _():
            carry_ref[...] = jnp.zeros_like(carry_ref)
            dar_ref[...] = jnp.zeros_like(dar_ref)
            dai_ref[...] = jnp.zeros_like(dai_ref)

        ar, ai = ar_ref[...], ai_ref[...]
        rows = _iota2((bt, _S5_W), 0)
        cr, ci = carry_ref[0:1, :], carry_ref[1:2, :]
        last = rows == bt - 1
        lr = g_ref[:, :_S5_W] + jnp.where(last, ar * cr + ai * ci, 0.0)
        li = g_ref[:, _S5_W:] + jnp.where(last, ar * ci - ai * cr, 0.0)
        pr, pi = ar, -ai
        s = 1
        while s < bt:
            m = rows < bt - s
            sr = jnp.where(m, pltpu.roll(lr, bt - s, 0), 0.0)
            si = jnp.where(m, pltpu.roll(li, bt - s, 0), 0.0)
            lr, li = _cmul_add(lr, li, pr, pi, sr, si)
            pr, pi = pr * pr - pi * pi, 2.0 * pr * pi
            s *= 2
        lam_ref[:, :_S5_W] = lr
        lam_ref[:, _S5_W:] = li
        nr = jnp.where(last, cr, pltpu.roll(lr, bt - 1, 0))
        ni = jnp.where(last, ci, pltpu.roll(li, bt - 1, 0))
        xr, xi = x_ref[:, :_S5_W], x_ref[:, _S5_W:]
        dar_ref[...] += jnp.sum(xr * nr + xi * ni, axis=0, keepdims=True)
        dai_ref[...] += jnp.sum(xr * ni - xi * nr, axis=0, keepdims=True)
        carry_ref[0:1, :] = lr[0:1, :]
        carry_ref[1:2, :] = li[0:1, :]

    blk = pl.BlockSpec((bt, 2 * _S5_W), lambda g, t: (nb - 1 - t, g))
    a_s = pl.BlockSpec((None, 1, _S5_W), lambda g, t: (g, 0, 0))
    nblk = S5_NG // S5_BLK
    return pl.pallas_call(
        body, name=name, grid=(nblk, nb), in_specs=[blk, blk, a_s, a_s], out_specs=[blk, a_s, a_s],
        out_shape=[jax.ShapeDtypeStruct(dx.shape, F32), jax.ShapeDtypeStruct((nblk, 1, _S5_W), F32),
                   jax.ShapeDtypeStruct((nblk, 1, _S5_W), F32)],
        scratch_shapes=[pltpu.VMEM((8, _S5_W), F32)], compiler_params=_cparams(2),
    )(dx, x, a_re, a_im)


def _norm_bf16(h, g, name):
    return rowwise(f_rmsnorm, [g], [h], [(D, BF16)], bt=512, name=name)[0]


def _norm_bwd(h, g, cts, name):
    n = len(cts) - 1

    def f(p, r):
        y = _rms(r[0], p[0])
        return (y,) * n + (r[0],)

    (dg,), (dh,) = rowwise_vjp(f, [g], [h], cts, [F32], bt=256, name=name)
    return dh, dg


def ffn_fwd(h, g, w_gu, w_down, tag):
    hn = _norm_bf16(h, g, f"{tag}_norm")
    gu = matmul(hn, w_gu[None], "nn", name=f"{tag}_gu")
    a = rowwise(f_swiglu, [], [gu], [(FFN_H, BF16)], bt=256, name=f"{tag}_act")[0]
    h2 = matmul(a, w_down[None], "nn", add=h, name=f"{tag}_down")
    return h2, (h, hn, gu, a)


def ffn_bwd(d, saved, g, w_gu, w_down, tag):
    h, hn, gu, a = saved
    da = matmul(d, w_down[None], "nt", name=f"{tag}_da")
    dwd = matmul(a, d, "tn", name=f"{tag}_dwd")[0]
    _, (dgu,) = rowwise_vjp(f_swiglu, [], [gu], [da], [BF16], bt=128, name=f"{tag}_dact")
    dwgu = matmul(hn, dgu, "tn", name=f"{tag}_dwgu")[0]
    dhn = matmul(dgu, w_gu[None], "nt", name=f"{tag}_dhn")
    dh, dg = _norm_bwd(h, g, [dhn, d], f"{tag}_dnorm")
    return dh, dg, dwgu, dwd


_GLA_NC = 4
_SSD_NU = 2


def gla_fwd(h, gm, w_in, w_a2, b_a, ng, w_out, tag):
    hn = _norm_bf16(h, gm, f"{tag}_norm")
    proj = matmul(hn, w_in[None], "nn", name=f"{tag}_in")
    alow = (proj, LANE, 2 * (GLA_QK + GLA_VD) // LANE)
    la = rowwise(f_gla_gate_in_fwd, [w_a2, b_a], [alow], [(GLA_QK, F32)], bt=512, name=f"{tag}_gate")[0]
    o, ss = gla_scan_fwd(proj, la, nc=_GLA_NC, name=f"{tag}_scan")
    r = (proj, GLA_VD, 2)
    og = rowwise(f_gla_out, [ng], [o, r], [(GLA_VD, BF16)], bt=256, name=f"{tag}_out")[0]
    h2 = matmul(og, w_out[None], "nn", add=h, name=f"{tag}_proj")
    return h2, (h, hn, proj, la, o, ss, og)


def gla_bwd(d, saved, gm, w_in, w_a2, b_a, ng, w_out, tag):
    h, hn, proj, la, o, ss, og = saved
    dog = matmul(d, w_out[None], "nt", name=f"{tag}_dog")
    dwout = matmul(og, d, "tn", name=f"{tag}_dwout")[0]
    r = (proj, GLA_VD, 2)
    (dng,), (do, dr) = rowwise_vjp(f_gla_out, [ng], [o, r], [dog], [F32, BF16], bt=128, name=f"{tag}_dout")
    dq, dk, dv, dla = gla_scan_bwd(proj, la, ss, do, nc=_GLA_NC, name=f"{tag}_dscan")
    alow = (proj, LANE, 2 * (GLA_QK + GLA_VD) // LANE)
    (dwa2, dba), (dalow,) = rowwise_vjp(f_gla_gate_in, [w_a2, b_a], [alow], [dla], [BF16], bt=512, name=f"{tag}_dgate")
    dproj = jnp.concatenate([dq, dk, dv, dr, dalow], axis=1)
    dwin = matmul(hn, dproj, "tn", name=f"{tag}_dwin")[0]
    dhn = matmul(dproj, w_in[None], "nt", name=f"{tag}_dhn")
    dh, dgm = _norm_bwd(h, gm, [dhn, d], f"{tag}_dnorm")
    return dh, dgm, dwin[:, :GLA_IN], dwa2[:GLA_RANK], dba, dng, dwout


def ssd_fwd(h, gm, w_in, conv_w, conv_b, dtb, alog, dsk, ng, w_out, tag):
    hn = _norm_bf16(h, gm, f"{tag}_norm")
    proj = matmul(hn, w_in[None], "nn", name=f"{tag}_in")
    xbc = ssd_conv_fwd(proj, conv_w, conv_b, name=f"{tag}_conv")
    y, hs = ssd_scan_fwd(xbc, proj, dtb, alog, dsk, nu=_SSD_NU, name=f"{tag}_scan")
    z = (proj, SSD_DI, 0)
    yg = rowwise(f_ssd_out, [ng], [y, z], [(SSD_DI, BF16)], bt=256, name=f"{tag}_out")[0]
    h2 = matmul(yg, w_out[None], "nn", add=h, name=f"{tag}_proj")
    return h2, (h, hn, proj, xbc, y, hs, yg)


def ssd_bwd(d, saved, gm, w_in, conv_w, conv_b, dtb, alog, dsk, ng, w_out, tag):
    h, hn, proj, xbc, y, hs, yg = saved
    dyg = matmul(d, w_out[None], "nt", name=f"{tag}_dyg")
    dwout = matmul(yg, d, "tn", name=f"{tag}_dwout")[0]
    z = (proj, SSD_DI, 0)
    (dng,), (dy, dz) = rowwise_vjp(f_ssd_out, [ng], [y, z], [dyg], [F32, BF16], bt=128, name=f"{tag}_dout")
    dxs, dbm, dcm, ddt, ddtb, dal, ddsk = ssd_scan_bwd(xbc, proj, dtb, alog, dsk, hs, dy, nu=_SSD_NU, name=f"{tag}_dscan")
    dxbc = jnp.concatenate([dxs, dbm, dcm], axis=1)
    dpre, dcw, dcb = ssd_conv_bwd(proj, conv_w, conv_b, dxbc, name=f"{tag}_dconv")
    dproj = jnp.concatenate([dz, dpre, ddt.astype(BF16)], axis=1)
    dwin = matmul(hn, dproj, "tn", name=f"{tag}_dwin")[0]
    dhn = matmul(dproj, w_in[None], "nt", name=f"{tag}_dhn")
    dh, dgm = _norm_bwd(h, gm, [dhn, d], f"{tag}_dnorm")
    return (dh, dgm, dwin[:, :SSD_IN], dcw, dcb, ddtb[:, :SSD_H], dal[:, :SSD_H], ddsk[:, :SSD_H], dng, dwout)


_S5_NB = S5_NG // S5_BLK


def _s5_param_args(log_dt, a_re, a_im, b_re, b_im, c_im):
    n = S5_NG * S5_GS
    tr = lambda b: jnp.transpose(b, (0, 2, 1)).reshape(n, S5_P)
    return [log_dt.reshape(S5_NG, 1), a_re, a_im, tr(b_re), tr(b_im), c_im.reshape(n, S5_P)]


def _s5_blockdiag(t):
    nb, gl, a, b = t.shape
    eye = jnp.eye(gl, dtype=t.dtype)
    return (t[:, :, :, None, :] * eye[None, :, None, :, None]).reshape(nb, gl * a, gl * b)


def _s5_diag(t, a, b):
    nb = t.shape[0]
    gl = t.shape[1] // a
    eye = jnp.eye(gl, dtype=t.dtype)
    return jnp.sum(t.reshape(nb, gl, a, gl, b) * eye[None, :, None, :, None], axis=3)


def _s5_weights(bbr, bbi, c_re, cneg):
    sh = (_S5_NB, S5_BLK, S5_GS, S5_P)
    wb = jnp.concatenate([_s5_blockdiag(bbr.reshape(sh)), _s5_blockdiag(bbi.reshape(sh))], axis=2)
    tr = lambda cc: jnp.transpose(cc.reshape(sh), (0, 1, 3, 2))
    wc = jnp.concatenate([_s5_blockdiag(tr(c_re)), _s5_blockdiag(tr(cneg))], axis=1)
    return wb, wc


def s5_fwd(h, gm, prm, dsk, w_glu, tag):
    log_dt, a_re, a_im, b_re, b_im, c_re, c_im = prm
    hn = rowwise(f_rmsnorm, [gm], [h], [(D, F32)], bt=512, name=f"{tag}_norm")[0]
    pargs = _s5_param_args(log_dt, a_re, a_im, b_re, b_im, c_im)
    abr, abi, bbr, bbi, cneg = s5_param_fwd(pargs, name=f"{tag}_param")
    wb, wc = _s5_weights(bbr, bbi, c_re.reshape(S5_NG * S5_GS, S5_P), cneg)
    ar, ai = abr.reshape(_S5_NB, 1, _S5_W), abi.reshape(_S5_NB, 1, _S5_W)
    bu = matmul(hn, wb, "nn", G=_S5_NB, name=f"{tag}_bu")
    x = s5_scan_fwd(bu, ar, ai, name=f"{tag}_scan")
    ycp = matmul(x, wc, "nn", G=_S5_NB, name=f"{tag}_cx")
    yg = rowwise(f_s5_act, [dsk], [ycp, hn], [(D, BF16)], bt=512, name=f"{tag}_act")[0]
    vg = matmul(yg, w_glu[None], "nn", name=f"{tag}_glu")
    h2 = rowwise(f_glu_res, [], [vg, h], [(D, F32)], bt=512, name=f"{tag}_out")[0]
    return h2, (h, hn, pargs, wb, wc, ar, ai, x, ycp, yg, vg)


def s5_bwd(d, saved, gm, dsk, w_glu, tag):
    h, hn, pargs, wb, wc, ar, ai, x, ycp, yg, vg = saved
    _, (dvg,) = rowwise_vjp(f_glu, [], [vg], [d], [BF16], bt=256, name=f"{tag}_dout")
    dwglu = matmul(yg, dvg, "tn", name=f"{tag}_dwglu")[0]
    dyg = matmul(dvg, w_glu[None], "nt", name=f"{tag}_dyg")
    (ddsk,), (dycp, dhn1) = rowwise_vjp(f_s5_act, [dsk], [ycp, hn], [dyg], [F32, F32], bt=256, name=f"{tag}_dact")
    dx = matmul(dycp, wc, "nt", G=_S5_NB, name=f"{tag}_dx")
    dwc = matmul(x, dycp, "tn", G=_S5_NB, name=f"{tag}_dwc")
    lam, dar, dai = s5_scan_bwd(dx, x, ar, ai, name=f"{tag}_dscan")
    dwb = matmul(hn, lam, "tn", G=_S5_NB, name=f"{tag}_dwb")
    dhn2 = matmul(lam, wb, "nt", G=_S5_NB, name=f"{tag}_dhn")
    dh, dgm = _norm_bwd(h, gm, [dhn1, dhn2, d], f"{tag}_dnorm")
    n = S5_NG * S5_GS
    half = S5_BLK * S5_P
    d_bbr = _s5_diag(dwb[:, :, :half], S5_GS, S5_P).reshape(n, S5_P)
    d_bbi = _s5_diag(dwb[:, :, half:], S5_GS, S5_P).reshape(n, S5_P)
    from_c = lambda t: jnp.transpose(_s5_diag(t, S5_P, S5_GS), (0, 1, 3, 2)).reshape(n, S5_P)
    d_cre = from_c(dwc[:, :half, :])
    d_cneg = from_c(dwc[:, half:, :])
    cts = [dar.reshape(S5_NG, S5_P), dai.reshape(S5_NG, S5_P), d_bbr, d_bbi, d_cneg]
    dlog, dare, daim, dbre_t, dbim_t, dcim = s5_param_bwd(pargs, cts, name=f"{tag}_dparam")
    untr = lambda t: jnp.transpose(t.reshape(S5_NG, S5_GS, S5_P), (0, 2, 1))
    grads = (dlog.reshape(S5_NG), dare, daim, untr(dbre_t), untr(dbim_t),
             d_cre.reshape(S5_NG, S5_GS, S5_P), dcim.reshape(S5_NG, S5_GS, S5_P))
    return dh, dgm, grads, ddsk, dwglu


def _pad_last(w, n):
    return jnp.pad(w, [(0, 0)] * (w.ndim - 1) + [(0, n - w.shape[-1])])


def local_step(x, target, W):
    f32 = lambda a: a.astype(F32)
    row = lambda a: f32(a).reshape(1, -1)
    mixers = []
    for i in range(DEPTH):
        m, j = i % 3, i // 3
        gm = row(W["norm_mix_g"][i])
        if m == 0:
            args = (gm, _pad_last(W["gla_w_in"][j], GLA_INP).astype(BF16),
                    jnp.pad(f32(W["gla_w_a2"][j]), ((0, LANE - GLA_RANK), (0, 0))), row(W["gla_b_a"][j]),
                    row(W["gla_norm_g"][j]), W["gla_w_out"][j].astype(BF16))
        elif m == 1:
            pl_ = lambda a: _pad_last(row(a), LANE)
            args = (gm, _pad_last(W["ssd_w_in"][j], SSD_INP).astype(BF16), f32(W["ssd_conv_w"][j]),
                    row(W["ssd_conv_b"][j]), pl_(W["ssd_dt_bias"][j]), pl_(W["ssd_a_log"][j]), pl_(W["ssd_d"][j]),
                    row(W["ssd_norm_g"][j]), W["ssd_w_out"][j].astype(BF16))
        else:
            prm = tuple(f32(W[k][j]) for k in ("s5_log_dt", "s5_a_re", "s5_a_im", "s5_b_re", "s5_b_im", "s5_c_re", "s5_c_im"))
            args = (gm, prm, row(W["s5_d"][j]), W["s5_w_glu"][j].astype(BF16))
        mixers.append((m, j, args))
    ffns = [(row(W["norm_ffn_g"][i]), W["ffn_w_gu"][i].astype(BF16), W["ffn_w_down"][i].astype(BF16)) for i in range(DEPTH)]

    h = x
    saved = []
    for i in range(DEPTH):
        m, j, args = mixers[i]
        tag = f"l{i}_{('gla', 'ssd', 's5')[m]}"
        h, sm = (gla_fwd, ssd_fwd, s5_fwd)[m](h, *args, tag)
        h, sf = ffn_fwd(h, *ffns[i], f"l{i}_ffn")
        saved.append((sm, sf))
    loss, dfg, d = loss_head(h, row(W["final_norm_g"]), target, name="loss_head")

    G = {k: [None] * v.shape[0] for k, v in W.items() if k != "final_norm_g"}
    G["final_norm_g"] = dfg.reshape(D)
    for i in reversed(range(DEPTH)):
        m, j, args = mixers[i]
        sm, sf = saved[i]
        d, dg, dwgu, dwd = ffn_bwd(d, sf, *ffns[i], f"l{i}_ffn")
        G["norm_ffn_g"][i], G["ffn_w_gu"][i], G["ffn_w_down"][i] = dg.reshape(D), dwgu, dwd
        tag = f"l{i}_{('gla', 'ssd', 's5')[m]}"
        if m == 0:
            d, dgm, dwin, dwa2, dba, dng, dwout = gla_bwd(d, sm, *args, tag)
            G["gla_w_in"][j], G["gla_w_a2"][j], G["gla_b_a"][j] = dwin, dwa2, dba.reshape(-1)
            G["gla_norm_g"][j], G["gla_w_out"][j] = dng.reshape(-1), dwout
        elif m == 1:
            d, dgm, dwin, dcw, dcb, ddtb, dal, ddsk, dng, dwout = ssd_bwd(d, sm, *args, tag)
            G["ssd_w_in"][j], G["ssd_conv_w"][j], G["ssd_conv_b"][j] = dwin, dcw, dcb.reshape(-1)
            G["ssd_dt_bias"][j], G["ssd_a_log"][j], G["ssd_d"][j] = ddtb.reshape(-1), dal.reshape(-1), ddsk.reshape(-1)
            G["ssd_norm_g"][j], G["ssd_w_out"][j] = dng.reshape(-1), dwout
        else:
            d, dgm, pg, ddsk, dwglu = s5_bwd(d, sm, args[0], args[2], args[3], tag)
            for k, v in zip(("s5_log_dt", "s5_a_re", "s5_a_im", "s5_b_re", "s5_b_im", "s5_c_re", "s5_c_im"), pg):
                G[k][j] = v
            G["s5_d"][j], G["s5_w_glu"][j] = ddsk.reshape(-1), dwglu
        G["norm_mix_g"][i] = dgm.reshape(D)
    grads = {k: (v if k == "final_norm_g" else jnp.stack(v)) for k, v in G.items()}
    return loss, d, grads


_MESH = pl.DeviceIdType.MESH
_ANY = pl.BlockSpec(memory_space=pl.ANY)
_DMA = pltpu.SemaphoreType.DMA
_ROWS_ALIGN = 1024


def _place():
    return lax.axis_index("x"), lax.axis_index("y"), lax.axis_index("c")


def _other_chips(x, y):
    return [(1 - x, y), (x, 1 - y), (1 - x, 1 - y)]


def _remote(src, dst, send_sems, recv_sems, k, to):
    return pltpu.make_async_remote_copy(src_ref=src, dst_ref=dst, send_sem=send_sems.at[k], recv_sem=recv_sems.at[k],
                                        device_id=to, device_id_type=_MESH)


def gather_shards(loc, *, name):
    def body(in_ref, out_ref, send_sems, recv_sems, local_sem):
        x, y, c = _place()
        me, sibling = (x, y, c), (x, y, 1 - c)
        chips = _other_chips(x, y)

        def half(px, py, hc):
            return out_ref.at[2 * px + py, hc]

        mine = pltpu.make_async_copy(in_ref, out_ref.at[2 * x + y], local_sem)
        mine.start()
        first = [_remote(in_ref.at[c], half(x, y, c), send_sems, recv_sems, j, (*chip, c)) for j, chip in enumerate(chips)]
        for cp in first:
            cp.start()
        passed = [_remote(half(*chip, c), half(*chip, c), send_sems, recv_sems, 3 + j, sibling) for j, chip in enumerate(chips)]
        for j, chip in enumerate(chips):
            _remote(in_ref.at[c], half(*chip, c), send_sems, recv_sems, j, me).wait_recv()
            passed[j].start()
        for j, chip in enumerate(chips):
            _remote(in_ref.at[c], half(*chip, 1 - c), send_sems, recv_sems, 3 + j, me).wait_recv()
        for cp in first + passed:
            cp.wait_send()
        mine.wait()

    return pl.pallas_call(
        body, name=name, in_specs=[_ANY], out_specs=_ANY,
        out_shape=jax.ShapeDtypeStruct((4,) + loc.shape, loc.dtype),
        scratch_shapes=[_DMA((6,)), _DMA((6,)), _DMA(())],
    )(loc)


def swap_with_sibling(v, *, name):
    def body(v_ref, got_ref, send_sems, recv_sems):
        x, y, c = _place()
        cp = _remote(v_ref, got_ref, send_sems, recv_sems, 0, (x, y, 1 - c))
        cp.start()
        cp.wait()

    return pl.pallas_call(
        body, name=name, in_specs=[_ANY], out_specs=_ANY, out_shape=jax.ShapeDtypeStruct(v.shape, v.dtype),
        scratch_shapes=[_DMA((1,)), _DMA((1,))],
    )(v)


def scatter_to_chips(q, *, name):
    _, R, L = q.shape

    def body(q_ref, r_ref, own_ref, send_sems, recv_sems, local_sem):
        x, y, c = _place()
        mine = pltpu.make_async_copy(q_ref.at[2 * x + y], own_ref, local_sem)
        mine.start()
        cps = [_remote(q_ref.at[2 * px + py], r_ref.at[j], send_sems, recv_sems, j, (px, py, c))
               for j, (px, py) in enumerate(_other_chips(x, y))]
        for cp in cps:
            cp.start()
        for cp in cps:
            cp.wait()
        mine.wait()

    return pl.pallas_call(
        body, name=name, in_specs=[_ANY], out_specs=[_ANY, _ANY],
        out_shape=[jax.ShapeDtypeStruct((3, R, L), q.dtype), jax.ShapeDtypeStruct((R, L), q.dtype)],
        scratch_shapes=[_DMA((3,)), _DMA((3,)), _DMA(())],
    )(q)


def share_halves(f, *, name):
    def body(f_ref, out_ref, send_sems, recv_sems, local_sem):
        x, y, c = _place()
        mine = pltpu.make_async_copy(f_ref, out_ref.at[c], local_sem)
        mine.start()
        cp = _remote(f_ref, out_ref.at[c], send_sems, recv_sems, 0, (x, y, 1 - c))
        cp.start()
        _remote(f_ref, out_ref.at[1 - c], send_sems, recv_sems, 0, (x, y, c)).wait_recv()
        cp.wait_send()
        mine.wait()

    return pl.pallas_call(
        body, name=name, in_specs=[_ANY], out_specs=_ANY, out_shape=jax.ShapeDtypeStruct((2,) + f.shape, f.dtype),
        scratch_shapes=[_DMA((1,)), _DMA((1,)), _DMA(())],
    )(f)


def gather_all(v, *, name):
    def body(v_ref, out_ref, send_sems, recv_sems, local_sem):
        x, y, c = _place()
        flip = lambda p, m: 1 - p if m else p
        peers = [(flip(x, m & 4), flip(y, m & 2), flip(c, m & 1)) for m in range(1, 8)]
        idx = lambda p: 4 * p[0] + 2 * p[1] + p[2]
        mine = pltpu.make_async_copy(v_ref, out_ref.at[idx((x, y, c))], local_sem)
        mine.start()
        cps = [_remote(v_ref, out_ref.at[idx((x, y, c))], send_sems, recv_sems, k, p) for k, p in enumerate(peers)]
        for cp in cps:
            cp.start()
        for k, p in enumerate(peers):
            _remote(v_ref, out_ref.at[idx(p)], send_sems, recv_sems, k, p).wait_recv()
        for cp in cps:
            cp.wait_send()
        mine.wait()

    return pl.pallas_call(
        body, name=name, in_specs=[_ANY], out_specs=_ANY, out_shape=jax.ShapeDtypeStruct((8,) + v.shape, v.dtype),
        scratch_shapes=[_DMA((7,)), _DMA((7,)), _DMA(())],
    )(v)


def sum_stack(a, extra=None, *, name):
    n, R, L = a.shape
    br = _pick(R, _ROWS_ALIGN, 8)

    def body(*refs):
        a_ref, o_ref = refs[0], refs[-1]
        acc = refs[1][...] if extra is not None else a_ref[0]
        for i in range(0 if extra is not None else 1, n):
            acc = acc + a_ref[i]
        o_ref[...] = acc

    row = pl.BlockSpec((br, L), lambda i: (i, 0))
    specs = [pl.BlockSpec((n, br, L), lambda i: (0, i, 0))] + ([row] if extra is not None else [])
    args = [a] + ([extra] if extra is not None else [])
    return pl.pallas_call(body, name=name, grid=(R // br,), in_specs=specs, out_specs=row,
                          out_shape=jax.ShapeDtypeStruct((R, L), a.dtype), compiler_params=_cparams(1))(*args)


def add2(a, b, *, name):
    R, L = a.shape
    br = _pick(R, _ROWS_ALIGN, 8)

    def body(a_ref, b_ref, o_ref):
        o_ref[...] = a_ref[...] + b_ref[...]

    row = pl.BlockSpec((br, L), lambda i: (i, 0))
    return pl.pallas_call(body, name=name, grid=(R // br,), in_specs=[row, row], out_specs=row,
                          out_shape=jax.ShapeDtypeStruct((R, L), a.dtype), compiler_params=_cparams(1))(a, b)


def adamw(w, g, m, v, *, name):
    shape = w.shape
    size = math.prod(shape)
    last = shape[-1]
    if last % LANE != 0 and size % LANE == 0 and size <= (1 << 20):
        last = LANE
    rows = size // last
    budget = (1 << 18) // last
    br = rows
    if rows > budget:
        br = max(c for c in range(8, budget + 1, 8) if rows % c == 0)
    v2 = lambda a: a.reshape(rows, last)

    def body(w_ref, g_ref, m_ref, v_ref, d_ref, nm_ref, nv_ref):
        gg = g_ref[...]
        nm = ADAM_B1 * m_ref[...] + (1.0 - ADAM_B1) * gg
        nv = ADAM_B2 * v_ref[...] + (1.0 - ADAM_B2) * (gg * gg)
        m_hat = nm / (1.0 - ADAM_B1 ** ADAM_STEP)
        v_hat = nv / (1.0 - ADAM_B2 ** ADAM_STEP)
        d_ref[...] = -ADAM_LR * (m_hat / (jnp.sqrt(v_hat) + ADAM_EPS) + ADAM_WD * w_ref[...])
        nm_ref[...] = nm
        nv_ref[...] = nv

    spec = pl.BlockSpec((br, last), lambda i: (i, 0))
    outs = pl.pallas_call(
        body, name=name, grid=(rows // br,), in_specs=[spec] * 4, out_specs=[spec] * 3,
        out_shape=[jax.ShapeDtypeStruct((rows, last), F32)] * 3, compiler_params=_cparams(1),
    )(v2(w), v2(g), v2(m), v2(v))
    return [o.reshape(shape) for o in outs]


_WEIGHTS = ["norm_mix_g", "norm_ffn_g", "gla_w_in", "gla_w_a2", "gla_b_a", "gla_norm_g", "gla_w_out", "ssd_w_in",
            "ssd_conv_w", "ssd_conv_b", "ssd_dt_bias", "ssd_a_log", "ssd_d", "ssd_norm_g", "ssd_w_out", "s5_log_dt",
            "s5_a_re", "s5_a_im", "s5_b_re", "s5_b_im", "s5_c_re", "s5_c_im", "s5_d", "s5_w_glu", "ffn_w_gu",
            "ffn_w_down", "final_norm_g"]
_SHARD_AXIS = {"gla_w_in": 2, "gla_w_a2": 2, "gla_b_a": 1, "gla_norm_g": 1, "gla_w_out": 1, "ssd_w_in": 2,
               "ssd_conv_w": 2, "ssd_w_out": 1, "s5_d": 1, "s5_w_glu": 2, "ffn_w_gu": 2, "ffn_w_down": 1}
_MATMUL_ONLY = ("gla_w_in", "gla_w_a2", "gla_w_out", "ssd_w_in", "ssd_w_out", "s5_w_glu", "ffn_w_gu", "ffn_w_down")
_SHARDED = [n for n in _WEIGHTS if n in _SHARD_AXIS]
_REPLICATED = [n for n in _WEIGHTS if n not in _SHARD_AXIS]


def _to_rows(flat, parts=1):
    per = -(-flat.shape[0] // (parts * LANE * _ROWS_ALIGN)) * _ROWS_ALIGN
    flat = jnp.pad(flat, (0, parts * per * LANE - flat.shape[0]))
    return flat.reshape(parts, per, LANE)


def _gather_weights(local):
    full = {}
    for names, dtype, tag in (([n for n in _SHARDED if n in _MATMUL_ONLY], BF16, "gather_weights"),
                              ([n for n in _SHARDED if n not in _MATMUL_ONLY], F32, "gather_small_weights")):
        flat = jnp.concatenate([local[n].astype(dtype).reshape(-1) for n in names])
        got = gather_shards(_to_rows(flat, 2), name=tag).reshape(4, -1)
        off = 0
        for n in names:
            bs = local[n].shape
            sz = math.prod(bs)
            seg = got[:, off:off + sz].reshape((4,) + bs)
            off += sz
            ax = _SHARD_AXIS[n]
            full[n] = jnp.moveaxis(seg, 0, ax).reshape(bs[:ax] + (4 * bs[ax],) + bs[ax + 1:])
    return full


def _reduce_sharded(grads, local_shapes, c):
    parts = []
    for n in _SHARDED:
        g, ax = grads[n], _SHARD_AXIS[n]
        g = g.reshape(g.shape[:ax] + (4, g.shape[ax] // 4) + g.shape[ax + 1:])
        parts.append(jnp.moveaxis(g, ax, 0).reshape(4, -1))
    flat = jnp.concatenate(parts, axis=1)
    n_el = flat.shape[1]
    per = -(-n_el // (2 * LANE * _ROWS_ALIGN)) * _ROWS_ALIGN
    p = jnp.pad(flat, ((0, 0), (0, 2 * per * LANE - n_el))).reshape(4, 2, per * LANE)
    mine = lax.dynamic_index_in_dim(p, c, 1, keepdims=False).reshape(4 * per, LANE)
    other = lax.dynamic_index_in_dim(p, 1 - c, 1, keepdims=False).reshape(4 * per, LANE)
    got = swap_with_sibling(other, name="reduce_pair_swap")
    q = add2(mine, got, name="reduce_pair_add").reshape(4, per, LANE)
    recv, own = scatter_to_chips(q, name="reduce_chip_scatter")
    f = sum_stack(recv, own, name="reduce_chip_add")
    red = share_halves(f, name="reduce_share").reshape(-1)
    out, off = {}, 0
    for n in _SHARDED:
        sz = math.prod(local_shapes[n])
        out[n] = red[off:off + sz].reshape(local_shapes[n])
        off += sz
    return out


def _reduce_replicated(grads):
    flat = jnp.concatenate([grads[n].astype(F32).reshape(-1) for n in _REPLICATED])
    n_el = flat.shape[0]
    rows = -(-n_el // (LANE * 8)) * 8
    v = jnp.pad(flat, (0, rows * LANE - n_el)).reshape(rows, LANE)
    red = sum_stack(gather_all(v, name="reduce_small_gather"), name="reduce_small_add").reshape(-1)
    out, off = {}, 0
    for n in _REPLICATED:
        sz = math.prod(grads[n].shape)
        out[n] = red[off:off + sz].reshape(grads[n].shape)
        off += sz
    return out


def kernel(x, norm_mix_g, norm_ffn_g, gla_w_in, gla_w_a2, gla_b_a, gla_norm_g, gla_w_out, ssd_w_in, ssd_conv_w, ssd_conv_b, ssd_dt_bias, ssd_a_log, ssd_d, ssd_norm_g, ssd_w_out, s5_log_dt, s5_a_re, s5_a_im, s5_b_re, s5_b_im, s5_c_re, s5_c_im, s5_d, s5_w_glu, ffn_w_gu, ffn_w_down, final_norm_g, loss_target, m_norm_mix_g, m_norm_ffn_g, m_gla_w_in, m_gla_w_a2, m_gla_b_a, m_gla_norm_g, m_gla_w_out, m_ssd_w_in, m_ssd_conv_w, m_ssd_conv_b, m_ssd_dt_bias, m_ssd_a_log, m_ssd_d, m_ssd_norm_g, m_ssd_w_out, m_s5_log_dt, m_s5_a_re, m_s5_a_im, m_s5_b_re, m_s5_b_im, m_s5_c_re, m_s5_c_im, m_s5_d, m_s5_w_glu, m_ffn_w_gu, m_ffn_w_down, m_final_norm_g, v_norm_mix_g, v_norm_ffn_g, v_gla_w_in, v_gla_w_a2, v_gla_b_a, v_gla_norm_g, v_gla_w_out, v_ssd_w_in, v_ssd_conv_w, v_ssd_conv_b, v_ssd_dt_bias, v_ssd_a_log, v_ssd_d, v_ssd_norm_g, v_ssd_w_out, v_s5_log_dt, v_s5_a_re, v_s5_a_im, v_s5_b_re, v_s5_b_im, v_s5_c_re, v_s5_c_im, v_s5_d, v_s5_w_glu, v_ffn_w_gu, v_ffn_w_down, v_final_norm_g):
    given = dict(locals())
    local = {n: given[n] for n in _WEIGHTS}
    c = lax.axis_index("c")

    full = dict(local)
    full.update(_gather_weights(local))
    loss, grad_x, grads = local_step(x[0], loss_target[0], full)
    loss = lax.psum(loss, ("x", "y", "c"))

    red = _reduce_sharded(grads, {n: local[n].shape for n in _SHARDED}, c)
    red.update(_reduce_replicated(grads))

    deltas, new_m, new_v = {}, {}, {}
    for n in _WEIGHTS:
        deltas[n], new_m[n], new_v[n] = adamw(local[n], red[n], given["m_" + n], given["v_" + n], name=f"adamw_{n}")
    return (loss, grad_x[None], *[red[n] for n in _WEIGHTS], *[deltas[n] for n in _WEIGHTS],
            *[new_m[n] for n in _WEIGHTS], *[new_v[n] for n in _WEIGHTS])
```

```python
import functools
import math

import jax
import jax.numpy as jnp
from jax import lax
from jax.experimental import pallas as pl
from jax.experimental.pallas import tpu as pltpu

F32 = jnp.float32
BF16 = jnp.bfloat16

D = 1024
DEPTH = 4
CH = 64
EPS = 1e-6
GLA_H, GLA_DK, GLA_DV, GLA_RANK, GLA_TAU = 4, 128, 256, 16, 16.0
GLA_QK = GLA_H * GLA_DK
GLA_VD = GLA_H * GLA_DV
GLA_IN = 2 * GLA_QK + 2 * GLA_VD + GLA_RANK
GLA_INP = 3200
SSD_DI, SSD_HD, SSD_H, SSD_G, SSD_N, SSD_K = 2048, 64, 32, 8, 128, 4
SSD_GN = SSD_G * SSD_N
SSD_CONV = SSD_DI + 2 * SSD_GN
SSD_IN = SSD_DI + SSD_CONV + SSD_H
SSD_INP = 6272
S5_GS, S5_NG, S5_P = 16, 64, 64
S5_BLK = 8
FFN_H = 2816
LANE = 128
VMEM_LIMIT = 52 * 1024 * 1024
_MATMUL_VMEM = 40 * 1024 * 1024

ADAM_LR, ADAM_B1, ADAM_B2, ADAM_EPS, ADAM_WD, ADAM_STEP = 0.001, 0.9, 0.999, 1e-08, 0.01, 10

_ARB = "arbitrary"


def _cparams(n):
    return pltpu.CompilerParams(dimension_semantics=(_ARB,) * n, vmem_limit_bytes=VMEM_LIMIT)


def _pick(n, target, mult=LANE):
    best = None
    for c in range(mult, min(n, target) + 1, mult):
        if n % c == 0:
            best = c
    return best if best is not None else n


_DN = {"nn": (((1,), (0,)), ((), ())), "nt": (((1,), (1,)), ((), ())), "tn": (((0,), (0,)), ((), ()))}


def _dot_raw(a, b, form):
    return lax.dot_general(a.astype(BF16), b.astype(BF16), _DN[form], preferred_element_type=F32)


@functools.partial(jax.custom_vjp, nondiff_argnums=(2,))
def bdot(a, b, form):
    return _dot_raw(a, b, form)


def _bdot_fwd(a, b, form):
    return _dot_raw(a, b, form), (a, b)


def _bdot_bwd(form, res, g):
    a, b = res
    if form == "nn":
        return _dot_raw(g, b, "nt"), _dot_raw(a, g, "tn")
    if form == "nt":
        return _dot_raw(g, b, "nn"), _dot_raw(g, a, "tn")
    return _dot_raw(b, g, "nt"), _dot_raw(a, g, "nn")


bdot.defvjp(_bdot_fwd, _bdot_bwd)


def _hdot(a, b):
    return jnp.dot(a, b, precision=lax.Precision.HIGHEST, preferred_element_type=F32)


@jax.custom_vjp
def cdot_left(c, ct, x):
    return _hdot(c, x)


def _cdl_fwd(c, ct, x):
    return _hdot(c, x), (c, ct)


def _cdl_bwd(res, g):
    c, ct = res
    return jnp.zeros_like(c), jnp.zeros_like(ct), _hdot(ct, g)


cdot_left.defvjp(_cdl_fwd, _cdl_bwd)


@jax.custom_vjp
def cdot_right(x, c, ct):
    return _hdot(x, c)


def _cdr_fwd(x, c, ct):
    return _hdot(x, c), (c, ct)


def _cdr_bwd(res, g):
    c, ct = res
    return _hdot(g, ct), jnp.zeros_like(c), jnp.zeros_like(ct)


cdot_right.defvjp(_cdr_fwd, _cdr_bwd)


def _sigmoid(x):
    return 1.0 / (1.0 + jnp.exp(-x))


def _silu(x):
    return x * _sigmoid(x)


def _softplus(x):
    return jnp.maximum(x, 0.0) + jnp.log(1.0 + jnp.exp(-jnp.abs(x)))


def _log_sigmoid(x):
    return jnp.minimum(x, 0.0) - jnp.log(1.0 + jnp.exp(-jnp.abs(x)))


def _gelu(x):
    c = math.sqrt(2.0 / math.pi)
    return 0.5 * x * (1.0 + jnp.tanh(c * (x + 0.044715 * (x * x * x))))


def _rms(x, g):
    return x * lax.rsqrt(jnp.mean(x * x, axis=-1, keepdims=True) + EPS) * g


def _iota2(shape, axis):
    return lax.broadcasted_iota(jnp.int32, shape, axis)


def matmul(a, b, form, *, name, G=1, out_dtype=F32, add=None):
    isz = lambda t: jnp.dtype(t.dtype).itemsize
    osz = jnp.dtype(out_dtype).itemsize + (isz(add) if add is not None else 0)

    def fits(bm, bn, bk):
        return 2 * (bm * bk * isz(a) + bk * bn * isz(b) + bm * bn * osz) + 4 * bm * bn <= _MATMUL_VMEM

    if form in ("nn", "nt"):
        M = a.shape[0]
        K = a.shape[1] // G
        N = b.shape[2] if form == "nn" else b.shape[1]
        bm, bn, bk = min(M, 1024), _pick(N, 1536), _pick(K, 2048)
        while not fits(bm, bn, bk) and bk % 256 == 0:
            bk //= 2
        nj, nk = N // bn, K // bk
        grid = (G, M // bm, nj, nk)
        a_spec = pl.BlockSpec((bm, bk), lambda g, i, j, k: (i, g * nk + k))
        if form == "nn":
            b_spec = pl.BlockSpec((None, bk, bn), lambda g, i, j, k: (g, k, j))
        else:
            b_spec = pl.BlockSpec((None, bn, bk), lambda g, i, j, k: (g, j, k))
        o_spec = pl.BlockSpec((bm, bn), lambda g, i, j, k: (i, g * nj + j))
        out_shape = jax.ShapeDtypeStruct((M, G * N), out_dtype)
    else:
        T = a.shape[0]
        Ka, Nb = a.shape[1] // G, b.shape[1] // G
        bm, bn, bk = _pick(Ka, 1408), _pick(Nb, 1536), min(T, 2048)
        while not fits(bm, bn, bk) and bk % 512 == 0:
            bk //= 2
        ni, nj, nk = Ka // bm, Nb // bn, T // bk
        grid = (G, ni, nj, nk)
        a_spec = pl.BlockSpec((bk, bm), lambda g, i, j, k: (k, g * ni + i))
        b_spec = pl.BlockSpec((bk, bn), lambda g, i, j, k: (k, g * nj + j))
        o_spec = pl.BlockSpec((None, bm, bn), lambda g, i, j, k: (g, i, j))
        out_shape = jax.ShapeDtypeStruct((G, Ka, Nb), out_dtype)
    has_add = add is not None

    def finish(refs, r):
        if has_add:
            r = r + refs[2][...].astype(F32)
        o_ref = refs[3] if has_add else refs[2]
        o_ref[...] = r.astype(o_ref.dtype)

    def body_one(*refs):
        finish(refs, _dot_raw(refs[0][...], refs[1][...], form))

    def body_acc(*refs):
        acc_ref = refs[-1]
        k = pl.program_id(3)

        @pl.when(k == 0)
        def _():
            acc_ref[...] = jnp.zeros_like(acc_ref)

        acc_ref[...] += _dot_raw(refs[0][...], refs[1][...], form)

        @pl.when(k == nk - 1)
        def _():
            finish(refs, acc_ref[...])

    in_specs = [a_spec, b_spec]
    args = [a, b]
    if has_add:
        in_specs.append(o_spec)
        args.append(add)
    return pl.pallas_call(
        body_one if nk == 1 else body_acc, name=name, grid=grid, in_specs=in_specs, out_specs=o_spec,
        out_shape=out_shape, scratch_shapes=[] if nk == 1 else [pltpu.VMEM((bm, bn), F32)],
        compiler_params=_cparams(4),
    )(*args)


def ffn_up(hn, w_il, *, name):
    T = hn.shape[0]
    bm, hb = min(T, 512), FFN_H // 2

    def body(a_ref, b_ref, act_ref, gu_ref):
        r = _dot_raw(a_ref[...], b_ref[...], "nn")
        act_ref[...] = (_silu(r[:, :hb]) * r[:, hb:]).astype(act_ref.dtype)
        gu_ref[...] = r.astype(gu_ref.dtype)

    return pl.pallas_call(
        body, name=name, grid=(2, T // bm),
        in_specs=[pl.BlockSpec((bm, D), lambda j, i: (i, 0)), pl.BlockSpec((D, 2 * hb), lambda j, i: (0, j))],
        out_specs=[pl.BlockSpec((bm, hb), lambda j, i: (i, j)), pl.BlockSpec((bm, 2 * hb), lambda j, i: (i, j))],
        out_shape=[jax.ShapeDtypeStruct((T, FFN_H), BF16), jax.ShapeDtypeStruct((T, 2 * FFN_H), BF16)],
        compiler_params=_cparams(2),
    )(hn, w_il)


def ffn_dact(d, w_down, gu, *, name):
    T = d.shape[0]
    bm, hb = min(T, 512), FFN_H // 2

    def body(d_ref, w_ref, gu_ref, o_ref):
        da = _dot_raw(d_ref[...], w_ref[...], "nt")
        g, u = gu_ref[:, :hb].astype(F32), gu_ref[:, hb:].astype(F32)
        sg = _sigmoid(g)
        o_ref[:, :hb] = (da * u * (sg * (1.0 + g * (1.0 - sg)))).astype(o_ref.dtype)
        o_ref[:, hb:] = (da * (g * sg)).astype(o_ref.dtype)

    return pl.pallas_call(
        body, name=name, grid=(2, T // bm),
        in_specs=[pl.BlockSpec((bm, D), lambda j, i: (i, 0)), pl.BlockSpec((hb, D), lambda j, i: (j, 0)),
                  pl.BlockSpec((bm, 2 * hb), lambda j, i: (i, j))],
        out_specs=pl.BlockSpec((bm, 2 * hb), lambda j, i: (i, j)),
        out_shape=jax.ShapeDtypeStruct((T, 2 * FFN_H), BF16), compiler_params=_cparams(2),
    )(d, w_down, gu)


def _row_entry(e):
    return e if isinstance(e, tuple) else (e, e.shape[1], 0)


def _row_spec(bt, e):
    _, width, idx = e
    return pl.BlockSpec((bt, width), lambda i: (i, idx))


def _full_spec(p):
    return pl.BlockSpec(p.shape, lambda i: (0,) * p.ndim)


def rowwise(f, params, rows, outs, *, bt, name):
    rows = [_row_entry(e) for e in rows]
    T = rows[0][0].shape[0]
    bt = min(bt, T)
    np_, nr = len(params), len(rows)

    def body(*refs):
        p = tuple(r[...].astype(F32) for r in refs[:np_])
        rw = tuple(r[...].astype(F32) for r in refs[np_:np_ + nr])
        res = f(p, rw)
        for o_ref, o in zip(refs[np_ + nr:], res):
            o_ref[...] = o.astype(o_ref.dtype)

    res = pl.pallas_call(
        body, name=name, grid=(T // bt,),
        in_specs=[_full_spec(p) for p in params] + [_row_spec(bt, e) for e in rows],
        out_specs=[pl.BlockSpec((bt, w), lambda i: (i, 0)) for w, _ in outs],
        out_shape=[jax.ShapeDtypeStruct((T, w), dt) for w, dt in outs],
        compiler_params=_cparams(1),
    )(*params, *[e[0] for e in rows])
    return list(res)


def rowwise_vjp(f, params, rows, cts, drow_dtypes, *, bt, name):
    rows = [_row_entry(e) for e in rows]
    cts = [_row_entry(e) for e in cts]
    T = rows[0][0].shape[0]
    bt = min(bt, T)
    np_, nr, nc = len(params), len(rows), len(cts)
    want = [i for i, dt in enumerate(drow_dtypes) if dt is not None]

    def body(*refs):
        p = tuple(r[...].astype(F32) for r in refs[:np_])
        rw = tuple(r[...].astype(F32) for r in refs[np_:np_ + nr])
        ct = tuple(r[...].astype(F32) for r in refs[np_ + nr:np_ + nr + nc])
        outs = refs[np_ + nr + nc:]
        _, vjp = jax.vjp(f, p, rw)
        dp, dr = vjp(ct)

        @pl.when(pl.program_id(0) == 0)
        def _():
            for o in outs[:np_]:
                o[...] = jnp.zeros_like(o)

        for o, d in zip(outs[:np_], dp):
            o[...] += d
        for o, i in zip(outs[np_:], want):
            o[...] = dr[i].astype(o.dtype)

    res = pl.pallas_call(
        body, name=name, grid=(T // bt,),
        in_specs=[_full_spec(p) for p in params] + [_row_spec(bt, e) for e in rows] + [_row_spec(bt, e) for e in cts],
        out_specs=[_full_spec(p) for p in params] + [pl.BlockSpec((bt, rows[i][1]), lambda i_: (i_, 0)) for i in want],
        out_shape=[jax.ShapeDtypeStruct(p.shape, F32) for p in params]
        + [jax.ShapeDtypeStruct((T, rows[i][1]), drow_dtypes[i]) for i in want],
        compiler_params=_cparams(1),
    )(*params, *[e[0] for e in rows], *[e[0] for e in cts])
    res = list(res)
    return res[:np_], res[np_:]


def f_rmsnorm(p, r):
    return (_rms(r[0], p[0]),)


def f_rmsnorm_res(p, r):
    return (_rms(r[0], p[0]), r[0])


def f_swiglu(p, r):
    gu = r[0]
    return (_silu(gu[:, :FFN_H]) * gu[:, FFN_H:],)


def f_gla_gate_in(p, r):
    w_a2, b_a = p
    z = bdot(r[0], w_a2, "nn") + b_a
    return (_log_sigmoid(z) / GLA_TAU,)


def f_gla_gate_in_fwd(p, r):
    w_a2, b_a = p
    z = _dot_raw(r[0], w_a2, "nn") + b_a
    return (_log_sigmoid(z) / GLA_TAU,)


def f_gla_out(p, r):
    (ng,) = p
    o, rr = r
    parts = []
    for h in range(GLA_H):
        sl = slice(h * GLA_DV, (h + 1) * GLA_DV)
        parts.append(_rms(o[:, sl], ng[:, sl]) * _silu(rr[:, sl]))
    return (jnp.concatenate(parts, axis=1),)


def f_ssd_out(p, r):
    (ng,) = p
    y, z = r
    t = y * _silu(z)
    gsz = SSD_DI // SSD_G
    parts = []
    for g in range(SSD_G):
        sl = slice(g * gsz, (g + 1) * gsz)
        parts.append(_rms(t[:, sl], ng[:, sl]))
    return (jnp.concatenate(parts, axis=1),)


def f_s5_act(p, r):
    (dsk,) = p
    ycp, u = r
    return (_gelu(ycp + dsk * u),)


def f_glu_res(p, r):
    vg, h = r
    return (vg[:, :D] * _sigmoid(vg[:, D:]) + h,)


def f_glu(p, r):
    vg = r[0]
    return (vg[:, :D] * _sigmoid(vg[:, D:]),)


def loss_head(h, g, target, *, name):
    T = h.shape[0]
    bt = min(T, 256)

    def lossf(g_, h_, t_):
        e = _rms(h_, g_) - t_
        return (0.5 / D) * jnp.sum(e * e)

    def body(g_ref, h_ref, t_ref, loss_ref, dg_ref, dh_ref):
        @pl.when(pl.program_id(0) == 0)
        def _():
            loss_ref[...] = jnp.zeros_like(loss_ref)
            dg_ref[...] = jnp.zeros_like(dg_ref)

        val, vjp = jax.vjp(lossf, g_ref[...], h_ref[...], t_ref[...])
        dg, dh, _ = vjp(jnp.ones((), F32))
        loss_ref[...] += jnp.full(loss_ref.shape, val, F32)
        dg_ref[...] += dg
        dh_ref[...] = dh

    row = pl.BlockSpec((bt, D), lambda i: (i, 0))
    one = pl.BlockSpec((1, D), lambda i: (0, 0))
    loss, dg, dh = pl.pallas_call(
        body, name=name, grid=(T // bt,), in_specs=[one, row, row],
        out_specs=[pl.BlockSpec((1, LANE), lambda i: (0, 0)), one, row],
        out_shape=[jax.ShapeDtypeStruct((1, LANE), F32), jax.ShapeDtypeStruct((1, D), F32),
                   jax.ShapeDtypeStruct((T, D), F32)],
        compiler_params=_cparams(1),
    )(g, h, target)
    return loss[0, 0], dg, dh


def _gla_consts():
    r, c = _iota2((CH, CH), 0), _iota2((CH, CH), 1)
    return (r >= c).astype(F32), (r <= c).astype(F32), r >= c


def _gla_chunk(q, k, v, la, st, consts, dot, cdl):
    L, Lt, tril = consts
    lc = cdl(L, Lt, la)
    lend = lc[CH - 1:CH, :]
    e, ei = jnp.exp(lc), jnp.exp(-lc)
    qs = q * (GLA_DK ** -0.5)
    qf, kf, qb, kb = qs * e, k * ei, qs * ei, k * e
    sc = jnp.where(tril, dot(qf, kf, "nt"), dot(qb, kb, "nt"))
    o = dot(sc, v, "nn") + dot(qf, st, "nt")
    kd = k * jnp.exp(lend - lc)
    st_new = st * jnp.exp(lend) + dot(v, kd, "tn")
    return o, st_new


def _gla_block(q, k, v, la, st, nc, dot, cdl):
    consts = _gla_consts()
    outs = []
    for c in range(nc):
        sl = slice(c * CH, (c + 1) * CH)
        o, st = _gla_chunk(q[sl], k[sl], v[sl], la[sl], st, consts, dot, cdl)
        outs.append(o)
    return jnp.concatenate(outs, axis=0), st


def _gla_specs(rows, rev, nb):
    t = (lambda j: nb - 1 - j) if rev else (lambda j: j)
    q = pl.BlockSpec((rows, GLA_DK), lambda h, j: (t(j), h))
    k = pl.BlockSpec((rows, GLA_DK), lambda h, j: (t(j), GLA_H + h))
    v = pl.BlockSpec((rows, GLA_DV), lambda h, j: (t(j), GLA_H + h))
    la = pl.BlockSpec((rows, GLA_DK), lambda h, j: (t(j), h))
    ss = pl.BlockSpec((None, None, GLA_DV, GLA_DK), lambda h, j: (t(j), h, 0, 0))
    o = pl.BlockSpec((rows, GLA_DV), lambda h, j: (t(j), h))
    return q, k, v, la, ss, o


def gla_scan_fwd(proj, la, *, nc, name):
    T = proj.shape[0]
    rows = min(T, nc * CH)
    nc = rows // CH
    nb = T // rows
    q_s, k_s, v_s, la_s, ss_s, o_s = _gla_specs(rows, False, nb)

    def body(q_ref, k_ref, v_ref, la_ref, o_ref, ss_ref, st_ref):
        @pl.when(pl.program_id(1) == 0)
        def _():
            st_ref[...] = jnp.zeros_like(st_ref)

        st = st_ref[...]
        ss_ref[...] = st
        o, st = _gla_block(q_ref[...], k_ref[...], v_ref[...], la_ref[...], st, nc, _dot_raw, lambda c, ct, x: _hdot(c, x))
        o_ref[...] = o
        st_ref[...] = st

    return pl.pallas_call(
        body, name=name, grid=(GLA_H, nb), in_specs=[q_s, k_s, v_s, la_s], out_specs=[o_s, ss_s],
        out_shape=[jax.ShapeDtypeStruct((T, GLA_VD), F32), jax.ShapeDtypeStruct((nb, GLA_H, GLA_DV, GLA_DK), F32)],
        scratch_shapes=[pltpu.VMEM((GLA_DV, GLA_DK), F32)], compiler_params=_cparams(2),
    )(proj, proj, proj, la)


def gla_scan_bwd(proj, la, ss, do, *, nc, name):
    T = proj.shape[0]
    rows = min(T, nc * CH)
    nc = rows // CH
    nb = T // rows
    q_s, k_s, v_s, la_s, ss_s, o_s = _gla_specs(rows, True, nb)
    t = lambda j: nb - 1 - j
    dqk_s = pl.BlockSpec((rows, GLA_DK), lambda h, j: (t(j), h))

    def body(q_ref, k_ref, v_ref, la_ref, ss_ref, do_ref, dq_ref, dk_ref, dv_ref, dla_ref, dst_ref):
        @pl.when(pl.program_id(1) == 0)
        def _():
            dst_ref[...] = jnp.zeros_like(dst_ref)

        fn = lambda q, k, v, la_, st: _gla_block(q, k, v, la_, st, nc, bdot, cdot_left)
        _, vjp = jax.vjp(fn, q_ref[...], k_ref[...], v_ref[...], la_ref[...], ss_ref[...])
        dq, dk, dv, dla, dst = vjp((do_ref[...], dst_ref[...]))
        dq_ref[...] = dq.astype(dq_ref.dtype)
        dk_ref[...] = dk.astype(dk_ref.dtype)
        dv_ref[...] = dv.astype(dv_ref.dtype)
        dla_ref[...] = dla
        dst_ref[...] = dst

    return pl.pallas_call(
        body, name=name, grid=(GLA_H, nb), in_specs=[q_s, k_s, v_s, la_s, ss_s, o_s],
        out_specs=[dqk_s, dqk_s, o_s, dqk_s],
        out_shape=[jax.ShapeDtypeStruct((T, GLA_QK), BF16), jax.ShapeDtypeStruct((T, GLA_QK), BF16),
                   jax.ShapeDtypeStruct((T, GLA_VD), BF16), jax.ShapeDtypeStruct((T, GLA_QK), F32)],
        scratch_shapes=[pltpu.VMEM((GLA_DV, GLA_DK), F32)], compiler_params=_cparams(2),
    )(proj, proj, proj, la, ss, do)


_CONV_W = 512
_CONV_OFF = SSD_DI // _CONV_W


def _conv_pre(x, prev8, w_ref, b_ref):
    bt = x.shape[0]
    ext = jnp.concatenate([prev8, x], axis=0)
    shifted = []
    for j in range(SSD_K):
        s = SSD_K - 1 - j
        shifted.append(x if s == 0 else pltpu.roll(ext, s, 0)[8:8 + bt])
    pre = b_ref[...] + sum(w_ref[j:j + 1, :] * shifted[j] for j in range(SSD_K))
    return pre, shifted


def ssd_conv_fwd(proj, w, b, *, name):
    T = proj.shape[0]
    bt = min(T, 512)
    nb = T // bt

    def body(x_ref, w_ref, b_ref, o_ref, carry_ref):
        @pl.when(pl.program_id(1) == 0)
        def _():
            carry_ref[...] = jnp.zeros_like(carry_ref)

        x = x_ref[...]
        pre, _ = _conv_pre(x, carry_ref[...], w_ref, b_ref)
        o_ref[...] = _silu(pre)
        carry_ref[...] = x[bt - 8:, :]

    return pl.pallas_call(
        body, name=name, grid=(SSD_CONV // _CONV_W, nb),
        in_specs=[pl.BlockSpec((bt, _CONV_W), lambda c, t: (t, _CONV_OFF + c)),
                  pl.BlockSpec((SSD_K, _CONV_W), lambda c, t: (0, c)),
                  pl.BlockSpec((1, _CONV_W), lambda c, t: (0, c))],
        out_specs=pl.BlockSpec((bt, _CONV_W), lambda c, t: (t, c)),
        out_shape=jax.ShapeDtypeStruct((T, SSD_CONV), F32),
        scratch_shapes=[pltpu.VMEM((8, _CONV_W), F32)], compiler_params=_cparams(2),
    )(proj, w, b)


def ssd_conv_bwd(proj, w, b, dout, *, name):
    T = proj.shape[0]
    bt = min(T, 512)
    nb = T // bt
    r8 = bt // 8

    def body(x_ref, xp_ref, w_ref, b_ref, do_ref, dx_ref, dw_ref, db_ref, carry_ref):
        t = pl.program_id(1)

        @pl.when(t == 0)
        def _():
            carry_ref[...] = jnp.zeros_like(carry_ref)
            dw_ref[...] = jnp.zeros_like(dw_ref)
            db_ref[...] = jnp.zeros_like(db_ref)

        x = x_ref[...]
        prev8 = jnp.where(t == nb - 1, 0.0, xp_ref[...])
        pre, shifted = _conv_pre(x, prev8, w_ref, b_ref)
        sg = _sigmoid(pre)
        dpre = do_ref[...] * (sg * (1.0 + pre * (1.0 - sg)))
        ext = jnp.concatenate([dpre, carry_ref[...]], axis=0)
        dx = w_ref[SSD_K - 1:SSD_K, :] * dpre
        for j in range(SSD_K - 1):
            s = SSD_K - 1 - j
            dx = dx + w_ref[j:j + 1, :] * pltpu.roll(ext, bt + 8 - s, 0)[:bt]
        dx_ref[...] = dx.astype(dx_ref.dtype)
        dw_ref[...] += jnp.concatenate([jnp.sum(dpre * shifted[j], axis=0, keepdims=True) for j in range(SSD_K)], axis=0)
        db_ref[...] += jnp.sum(dpre, axis=0, keepdims=True)
        carry_ref[...] = dpre[:8, :]

    rt = lambda t: nb - 1 - t
    return pl.pallas_call(
        body, name=name, grid=(SSD_CONV // _CONV_W, nb),
        in_specs=[pl.BlockSpec((bt, _CONV_W), lambda c, t: (rt(t), _CONV_OFF + c)),
                  pl.BlockSpec((8, _CONV_W), lambda c, t: (jnp.maximum(rt(t) * r8 - 1, 0), _CONV_OFF + c)),
                  pl.BlockSpec((SSD_K, _CONV_W), lambda c, t: (0, c)),
                  pl.BlockSpec((1, _CONV_W), lambda c, t: (0, c)),
                  pl.BlockSpec((bt, _CONV_W), lambda c, t: (rt(t), c))],
        out_specs=[pl.BlockSpec((bt, _CONV_W), lambda c, t: (rt(t), c)),
                   pl.BlockSpec((SSD_K, _CONV_W), lambda c, t: (0, c)),
                   pl.BlockSpec((1, _CONV_W), lambda c, t: (0, c))],
        out_shape=[jax.ShapeDtypeStruct((T, SSD_CONV), BF16), jax.ShapeDtypeStruct((SSD_K, SSD_CONV), F32),
                   jax.ShapeDtypeStruct((1, SSD_CONV), F32)],
        scratch_shapes=[pltpu.VMEM((8, _CONV_W), F32)], compiler_params=_cparams(2),
    )(proj, proj, w, b, dout)


_SSD_U = 2 * CH


def _ssd_unit(xs, bm, cm, dtraw, dtb, alog, dsk, hp, g, dot, cdl, cdr):
    U = _SSD_U
    r, c = _iota2((U, U), 0), _iota2((U, U), 1)
    same = (r // CH) == (c // CH)
    Lb = (same & (r >= c)).astype(F32)
    Ub = (same & (r <= c)).astype(F32)
    lane = _iota2((1, U), 1)
    sub = _iota2((U, 1), 0)
    lo_lane = _iota2((1, 2 * SSD_HD), 1) < SSD_HD
    lo_sub = _iota2((2 * SSD_HD, 1), 0) < SSD_HD

    dt = _softplus(dtraw + dtb)
    da = dt * (-jnp.exp(alog))
    cum = cdl(Lb, Ub, da)
    cum_t = cdr(da.T, Ub, Lb)
    dt_t = dt.T
    cb = dot(cm, bm, "nt")
    ys = []
    new_hp = []
    for pr in range(2):
        xs_p = xs[:, pr * 2 * SSD_HD:(pr + 1) * 2 * SSD_HD]
        cols, dts, dks, y_in = [], [], [], []
        for jj in range(2):
            hd = g * (SSD_H // SSD_G) + 2 * pr + jj
            oh_l = (lane == hd).astype(F32)
            oh_s = (sub == hd).astype(F32)
            col = jnp.sum(cum * oh_l, axis=1, keepdims=True)
            row = jnp.sum(cum_t * oh_s, axis=0, keepdims=True)
            dtrow = jnp.sum(dt_t * oh_s, axis=0, keepdims=True)
            decay = jnp.exp(-jnp.abs(col - row))
            mix = jnp.where(same, cb * decay * dtrow, 0.0)
            y_in.append(dot(mix, xs_p, "nn"))
            cols.append(col)
            dts.append(jnp.sum(dt * oh_l, axis=1, keepdims=True))
            dks.append(jnp.sum(dsk * oh_l, axis=1, keepdims=True))
        y_intra = jnp.where(lo_lane, y_in[0], y_in[1])
        dsk_p = jnp.where(lo_lane, dks[0], dks[1])
        h = hp[pr]
        yc = []
        for ci in range(2):
            sl = slice(ci * CH, (ci + 1) * CH)
            ce = [cols[jj][ci * CH + CH - 1:ci * CH + CH, :] for jj in range(2)]
            ecum = jnp.where(lo_lane, jnp.exp(cols[0][sl]), jnp.exp(cols[1][sl]))
            y_inter = dot(cm[sl], h, "nt") * ecum
            wgt = jnp.where(lo_lane, dts[0][sl] * jnp.exp(ce[0] - cols[0][sl]), dts[1][sl] * jnp.exp(ce[1] - cols[1][sl]))
            xw = xs_p[sl] * wgt
            a_p = jnp.where(lo_sub, jnp.exp(ce[0]), jnp.exp(ce[1]))
            h = a_p * h + dot(xw, bm[sl], "tn")
            yc.append(y_intra[sl] + y_inter + dsk_p * xs_p[sl])
        ys.append(jnp.concatenate(yc, axis=0))
        new_hp.append(h)
    return jnp.concatenate(ys, axis=1), tuple(new_hp)


def _ssd_block(xs, bm, cm, dtraw, dtb, alog, dsk, hp, g, nu, dot, cdl, cdr):
    outs = []
    for u in range(nu):
        sl = slice(u * _SSD_U, (u + 1) * _SSD_U)
        y, hp = _ssd_unit(xs[sl], bm[sl], cm[sl], dtraw[sl], dtb, alog, dsk, hp, g, dot, cdl, cdr)
        outs.append(y)
    return jnp.concatenate(outs, axis=0), hp


def _ssd_specs(rows, rev, nb):
    t = (lambda j: nb - 1 - j) if rev else (lambda j: j)
    gw = SSD_DI // SSD_G
    xs = pl.BlockSpec((rows, gw), lambda j, g: (t(j), g))
    bm = pl.BlockSpec((rows, SSD_N), lambda j, g: (t(j), SSD_DI // SSD_N + g))
    cm = pl.BlockSpec((rows, SSD_N), lambda j, g: (t(j), (SSD_DI + SSD_GN) // SSD_N + g))
    dtr = pl.BlockSpec((rows, LANE), lambda j, g: (t(j), (SSD_DI + SSD_CONV) // LANE))
    par = pl.BlockSpec((1, LANE), lambda j, g: (0, 0))
    hs = pl.BlockSpec((None, None, 2, 2 * SSD_HD, SSD_N), lambda j, g: (t(j), g, 0, 0, 0))
    y = pl.BlockSpec((rows, gw), lambda j, g: (t(j), g))
    return xs, bm, cm, dtr, par, hs, y


def ssd_scan_fwd(xbc, proj, dtb, alog, dsk, *, nu, name):
    T = xbc.shape[0]
    rows = min(T, nu * _SSD_U)
    nu = rows // _SSD_U
    nb = T // rows
    xs_s, bm_s, cm_s, dt_s, par_s, hs_s, y_s = _ssd_specs(rows, False, nb)

    def body(xs_ref, bm_ref, cm_ref, dt_ref, dtb_ref, al_ref, dsk_ref, y_ref, hs_ref, h_ref):
        g = pl.program_id(1)

        @pl.when(pl.program_id(0) == 0)
        def _():
            h_ref[g] = jnp.zeros(h_ref.shape[1:], F32)

        hs_ref[...] = h_ref[g]
        hp = (h_ref[g, 0], h_ref[g, 1])
        y, hp = _ssd_block(xs_ref[...], bm_ref[...], cm_ref[...], dt_ref[...], dtb_ref[...], al_ref[...], dsk_ref[...],
                           hp, g, nu, _dot_raw, lambda c, ct, x: _hdot(c, x), lambda x, c, ct: _hdot(x, c))
        y_ref[...] = y
        h_ref[g, 0] = hp[0]
        h_ref[g, 1] = hp[1]

    return pl.pallas_call(
        body, name=name, grid=(nb, SSD_G), in_specs=[xs_s, bm_s, cm_s, dt_s, par_s, par_s, par_s],
        out_specs=[y_s, hs_s],
        out_shape=[jax.ShapeDtypeStruct((T, SSD_DI), F32), jax.ShapeDtypeStruct((nb, SSD_G, 2, 2 * SSD_HD, SSD_N), F32)],
        scratch_shapes=[pltpu.VMEM((SSD_G, 2, 2 * SSD_HD, SSD_N), F32)], compiler_params=_cparams(2),
    )(xbc, xbc, xbc, proj, dtb, alog, dsk)


def ssd_scan_bwd(xbc, proj, dtb, alog, dsk, hs, dy, *, nu, name):
    T = xbc.shape[0]
    rows = min(T, nu * _SSD_U)
    nu = rows // _SSD_U
    nb = T // rows
    xs_s, bm_s, cm_s, dt_s, par_s, hs_s, y_s = _ssd_specs(rows, True, nb)
    t = lambda j: nb - 1 - j
    n_s = pl.BlockSpec((rows, SSD_N), lambda j, g: (t(j), g))
    ddt_s = pl.BlockSpec((rows, LANE), lambda j, g: (t(j), 0))

    def body(xs_ref, bm_ref, cm_ref, dt_ref, dtb_ref, al_ref, dsk_ref, hs_ref, dy_ref,
             dxs_ref, dbm_ref, dcm_ref, ddt_ref, ddtb_ref, dal_ref, ddsk_ref, dh_ref):
        j, g = pl.program_id(0), pl.program_id(1)

        @pl.when(j == 0)
        def _():
            dh_ref[g] = jnp.zeros(dh_ref.shape[1:], F32)

        @pl.when((j == 0) & (g == 0))
        def _():
            ddtb_ref[...] = jnp.zeros_like(ddtb_ref)
            dal_ref[...] = jnp.zeros_like(dal_ref)
            ddsk_ref[...] = jnp.zeros_like(ddsk_ref)

        @pl.when(g == 0)
        def _():
            ddt_ref[...] = jnp.zeros_like(ddt_ref)

        fn = lambda xs, bm, cm, dtr, dtb_, al, dsk_, h0, h1: _ssd_block(
            xs, bm, cm, dtr, dtb_, al, dsk_, (h0, h1), g, nu, bdot, cdot_left, cdot_right)
        _, vjp = jax.vjp(fn, xs_ref[...], bm_ref[...], cm_ref[...], dt_ref[...], dtb_ref[...], al_ref[...], dsk_ref[...],
                         hs_ref[0], hs_ref[1])
        dxs, dbm, dcm, ddt, ddtb, dal, ddsk, dh0, dh1 = vjp((dy_ref[...], (dh_ref[g, 0], dh_ref[g, 1])))
        dxs_ref[...] = dxs
        dbm_ref[...] = dbm
        dcm_ref[...] = dcm
        ddt_ref[...] += ddt
        ddtb_ref[...] += ddtb
        dal_ref[...] += dal
        ddsk_ref[...] += ddsk
        dh_ref[g, 0] = dh0
        dh_ref[g, 1] = dh1

    return pl.pallas_call(
        body, name=name, grid=(nb, SSD_G), in_specs=[xs_s, bm_s, cm_s, dt_s, par_s, par_s, par_s, hs_s, y_s],
        out_specs=[y_s, n_s, n_s, ddt_s, par_s, par_s, par_s],
        out_shape=[jax.ShapeDtypeStruct((T, SSD_DI), F32), jax.ShapeDtypeStruct((T, SSD_GN), F32),
                   jax.ShapeDtypeStruct((T, SSD_GN), F32), jax.ShapeDtypeStruct((T, LANE), F32),
                   jax.ShapeDtypeStruct((1, LANE), F32), jax.ShapeDtypeStruct((1, LANE), F32),
                   jax.ShapeDtypeStruct((1, LANE), F32)],
        scratch_shapes=[pltpu.VMEM((SSD_G, 2, 2 * SSD_HD, SSD_N), F32)], compiler_params=_cparams(2),
    )(xbc, xbc, xbc, proj, dtb, alog, dsk, hs, dy)


def _s5_param_f(log_dt, a_re, a_im, bre_t, bim_t, cim, cdl):
    n = S5_NG * S5_GS
    r, c = _iota2((n, S5_NG), 0), _iota2((n, S5_NG), 1)
    E = ((r // S5_GS) == c).astype(F32)
    rt, ct = _iota2((S5_NG, n), 0), _iota2((S5_NG, n), 1)
    Et = ((ct // S5_GS) == rt).astype(F32)
    step = jnp.exp(log_dt)
    mag = jnp.exp(step * a_re)
    abr = mag * jnp.cos(step * a_im)
    abi = mag * jnp.sin(step * a_im)
    den = a_re * a_re + a_im * a_im
    nr, ni = abr - 1.0, abi
    fr = (nr * a_re + ni * a_im) / den
    fi = (ni * a_re - nr * a_im) / den
    Fr, Fi = cdl(E, Et, fr), cdl(E, Et, fi)
    bbr = Fr * bre_t - Fi * bim_t
    bbi = Fr * bim_t + Fi * bre_t
    return abr, abi, bbr, bbi, -cim


def _whole(a):
    return pl.BlockSpec(a.shape, lambda: (0,) * a.ndim)


def s5_param_fwd(args, *, name):
    def body(*refs):
        res = _s5_param_f(*[r[...] for r in refs[:6]], lambda c, ct, x: _hdot(c, x))
        for o, v in zip(refs[6:], res):
            o[...] = v

    shapes = [(S5_NG, S5_P), (S5_NG, S5_P)] + [(S5_NG * S5_GS, S5_P)] * 3
    return pl.pallas_call(
        body, name=name, in_specs=[_whole(a) for a in args], out_specs=[pl.BlockSpec(s, lambda: (0, 0)) for s in shapes],
        out_shape=[jax.ShapeDtypeStruct(s, F32) for s in shapes],
        compiler_params=pltpu.CompilerParams(vmem_limit_bytes=VMEM_LIMIT),
    )(*args)


def s5_param_bwd(args, cts, *, name):
    def body(*refs):
        fn = lambda *a: _s5_param_f(*a, cdot_left)
        _, vjp = jax.vjp(fn, *[r[...] for r in refs[:6]])
        grads = vjp(tuple(r[...] for r in refs[6:11]))
        for o, v in zip(refs[11:], grads):
            o[...] = v

    return pl.pallas_call(
        body, name=name, in_specs=[_whole(a) for a in list(args) + list(cts)],
        out_specs=[_whole(a) for a in args], out_shape=[jax.ShapeDtypeStruct(a.shape, F32) for a in args],
        compiler_params=pltpu.CompilerParams(vmem_limit_bytes=VMEM_LIMIT),
    )(*args, *cts)


_S5_W = S5_BLK * S5_P


def _cmul_add(xr, xi, pr, pi, sr, si):
    return xr + (pr * sr - pi * si), xi + (pr * si + pi * sr)


def s5_scan_fwd(bu, a_re, a_im, *, name):
    T = bu.shape[0]
    bt = min(T, 256)
    nb = T // bt

    def body(bu_ref, ar_ref, ai_ref, x_ref, carry_ref):
        @pl.when(pl.program_id(1) == 0)
        def _():
            carry_ref[...] = jnp.zeros_like(carry_ref)

        ar, ai = ar_ref[...], ai_ref[...]
        rows = _iota2((bt, _S5_W), 0)
        cr, ci = carry_ref[0:1, :], carry_ref[1:2, :]
        first = rows == 0
        xr = bu_ref[:, :_S5_W] + jnp.where(first, ar * cr - ai * ci, 0.0)
        xi = bu_ref[:, _S5_W:] + jnp.where(first, ar * ci + ai * cr, 0.0)
        pr, pi = ar, ai
        s = 1
        while s < bt:
            m = rows >= s
            sr = jnp.where(m, pltpu.roll(xr, s, 0), 0.0)
            si = jnp.where(m, pltpu.roll(xi, s, 0), 0.0)
            xr, xi = _cmul_add(xr, xi, pr, pi, sr, si)
            pr, pi = pr * pr - pi * pi, 2.0 * pr * pi
            s *= 2
        x_ref[:, :_S5_W] = xr
        x_ref[:, _S5_W:] = xi
        carry_ref[0:1, :] = xr[bt - 1:bt, :]
        carry_ref[1:2, :] = xi[bt - 1:bt, :]

    blk = pl.BlockSpec((bt, 2 * _S5_W), lambda g, t: (t, g))
    a_s = pl.BlockSpec((None, 1, _S5_W), lambda g, t: (g, 0, 0))
    return pl.pallas_call(
        body, name=name, grid=(S5_NG // S5_BLK, nb), in_specs=[blk, a_s, a_s], out_specs=blk,
        out_shape=jax.ShapeDtypeStruct(bu.shape, F32), scratch_shapes=[pltpu.VMEM((8, _S5_W), F32)],
        compiler_params=_cparams(2),
    )(bu, a_re, a_im)


def s5_scan_bwd(dx, x, a_re, a_im, *, name):
    T = dx.shape[0]
    bt = min(T, 256)
    nb = T // bt

    def body(g_ref, x_ref, ar_ref, ai_ref, lam_ref, dar_ref, dai_ref, carry_ref):
        @pl.when(pl.program_id(1) == 0)
        def _():
            carry_ref[...] = jnp.zeros_like(carry_ref)
            dar_ref[...] = jnp.zeros_like(dar_ref)
            dai_ref[...] = jnp.zeros_like(dai_ref)

        ar, ai = ar_ref[...], ai_ref[...]
        rows = _iota2((bt, _S5_W), 0)
        cr, ci = carry_ref[0:1, :], carry_ref[1:2, :]
        last = rows == bt - 1
        lr = g_ref[:, :_S5_W] + jnp.where(last, ar * cr + ai * ci, 0.0)
        li = g_ref[:, _S5_W:] + jnp.where(last, ar * ci - ai * cr, 0.0)
        pr, pi = ar, -ai
        s = 1
        while s < bt:
            m = rows < bt - s
            sr = jnp.where(m, pltpu.roll(lr, bt - s, 0), 0.0)
            si = jnp.where(m, pltpu.roll(li, bt - s, 0), 0.0)
            lr, li = _cmul_add(lr, li, pr, pi, sr, si)
            pr, pi = pr * pr - pi * pi, 2.0 * pr * pi
            s *= 2
        lam_ref[:, :_S5_W] = lr
        lam_ref[:, _S5_W:] = li
        nr = jnp.where(last, cr, pltpu.roll(lr, bt - 1, 0))
        ni = jnp.where(last, ci, pltpu.roll(li, bt - 1, 0))
        xr, xi = x_ref[:, :_S5_W], x_ref[:, _S5_W:]
        dar_ref[...] += jnp.sum(xr * nr + xi * ni, axis=0, keepdims=True)
        dai_ref[...] += jnp.sum(xr * ni - xi * nr, axis=0, keepdims=True)
        carry_ref[0:1, :] = lr[0:1, :]
        carry_ref[1:2, :] = li[0:1, :]

    blk = pl.BlockSpec((bt, 2 * _S5_W), lambda g, t: (nb - 1 - t, g))
    a_s = pl.BlockSpec((None, 1, _S5_W), lambda g, t: (g, 0, 0))
    nblk = S5_NG // S5_BLK
    return pl.pallas_call(
        body, name=name, grid=(nblk, nb), in_specs=[blk, blk, a_s, a_s], out_specs=[blk, a_s, a_s],
        out_shape=[jax.ShapeDtypeStruct(dx.shape, F32), jax.ShapeDtypeStruct((nblk, 1, _S5_W), F32),
                   jax.ShapeDtypeStruct((nblk, 1, _S5_W), F32)],
        scratch_shapes=[pltpu.VMEM((8, _S5_W), F32)], compiler_params=_cparams(2),
    )(dx, x, a_re, a_im)


def _norm_bf16(h, g, name):
    return rowwise(f_rmsnorm, [g], [h], [(D, BF16)], bt=512, name=name)[0]


def _norm_bwd(h, g, cts, name):
    n = len(cts) - 1

    def f(p, r):
        y = _rms(r[0], p[0])
        return (y,) * n + (r[0],)

    (dg,), (dh,) = rowwise_vjp(f, [g], [h], cts, [F32], bt=256, name=name)
    return dh, dg


def ffn_fwd(h, g, w_gu, w_down, tag):
    hn = _norm_bf16(h, g, f"{tag}_norm")
    a, gu = ffn_up(hn, w_gu, name=f"{tag}_up")
    h2 = matmul(a, w_down[None], "nn", add=h, name=f"{tag}_down")
    return h2, (h, hn, gu, a)


def ffn_bwd(d, saved, g, w_gu, w_down, tag):
    h, hn, gu, a = saved
    dgu = ffn_dact(d, w_down, gu, name=f"{tag}_dact")
    dwd = matmul(a, d, "tn", name=f"{tag}_dwd")[0]
    dwgu = matmul(hn, dgu, "tn", name=f"{tag}_dwgu")[0]
    dhn = matmul(dgu, w_gu[None], "nt", name=f"{tag}_dhn")
    dh, dg = _norm_bwd(h, g, [dhn, d], f"{tag}_dnorm")
    return dh, dg, dwgu, dwd


_GLA_NC = 4
_SSD_NU = 2


def gla_fwd(h, gm, w_in, w_a2, b_a, ng, w_out, tag):
    hn = _norm_bf16(h, gm, f"{tag}_norm")
    proj = matmul(hn, w_in[None], "nn", name=f"{tag}_in")
    alow = (proj, LANE, 2 * (GLA_QK + GLA_VD) // LANE)
    la = rowwise(f_gla_gate_in_fwd, [w_a2, b_a], [alow], [(GLA_QK, F32)], bt=512, name=f"{tag}_gate")[0]
    o, ss = gla_scan_fwd(proj, la, nc=_GLA_NC, name=f"{tag}_scan")
    r = (proj, GLA_VD, 2)
    og = rowwise(f_gla_out, [ng], [o, r], [(GLA_VD, BF16)], bt=256, name=f"{tag}_out")[0]
    h2 = matmul(og, w_out[None], "nn", add=h, name=f"{tag}_proj")
    return h2, (h, hn, proj, la, o, ss, og)


def gla_bwd(d, saved, gm, w_in, w_a2, b_a, ng, w_out, tag):
    h, hn, proj, la, o, ss, og = saved
    dog = matmul(d, w_out[None], "nt", name=f"{tag}_dog")
    dwout = matmul(og, d, "tn", name=f"{tag}_dwout")[0]
    r = (proj, GLA_VD, 2)
    (dng,), (do, dr) = rowwise_vjp(f_gla_out, [ng], [o, r], [dog], [F32, BF16], bt=128, name=f"{tag}_dout")
    dq, dk, dv, dla = gla_scan_bwd(proj, la, ss, do, nc=_GLA_NC, name=f"{tag}_dscan")
    alow = (proj, LANE, 2 * (GLA_QK + GLA_VD) // LANE)
    (dwa2, dba), (dalow,) = rowwise_vjp(f_gla_gate_in, [w_a2, b_a], [alow], [dla], [BF16], bt=512, name=f"{tag}_dgate")
    dproj = jnp.concatenate([dq, dk, dv, dr, dalow], axis=1)
    dwin = matmul(hn, dproj, "tn", name=f"{tag}_dwin")[0]
    dhn = matmul(dproj, w_in[None], "nt", name=f"{tag}_dhn")
    dh, dgm = _norm_bwd(h, gm, [dhn, d], f"{tag}_dnorm")
    return dh, dgm, dwin, dwa2[:GLA_RANK], dba, dng, dwout


def ssd_fwd(h, gm, w_in, conv_w, conv_b, dtb, alog, dsk, ng, w_out, tag):
    hn = _norm_bf16(h, gm, f"{tag}_norm")
    proj = matmul(hn, w_in[None], "nn", name=f"{tag}_in")
    xbc = ssd_conv_fwd(proj, conv_w, conv_b, name=f"{tag}_conv")
    y, hs = ssd_scan_fwd(xbc, proj, dtb, alog, dsk, nu=_SSD_NU, name=f"{tag}_scan")
    z = (proj, SSD_DI, 0)
    yg = rowwise(f_ssd_out, [ng], [y, z], [(SSD_DI, BF16)], bt=256, name=f"{tag}_out")[0]
    h2 = matmul(yg, w_out[None], "nn", add=h, name=f"{tag}_proj")
    return h2, (h, hn, proj, xbc, y, hs, yg)


def ssd_bwd(d, saved, gm, w_in, conv_w, conv_b, dtb, alog, dsk, ng, w_out, tag):
    h, hn, proj, xbc, y, hs, yg = saved
    dyg = matmul(d, w_out[None], "nt", name=f"{tag}_dyg")
    dwout = matmul(yg, d, "tn", name=f"{tag}_dwout")[0]
    z = (proj, SSD_DI, 0)
    (dng,), (dy, dz) = rowwise_vjp(f_ssd_out, [ng], [y, z], [dyg], [F32, BF16], bt=128, name=f"{tag}_dout")
    dxs, dbm, dcm, ddt, ddtb, dal, ddsk = ssd_scan_bwd(xbc, proj, dtb, alog, dsk, hs, dy, nu=_SSD_NU, name=f"{tag}_dscan")
    dxbc = jnp.concatenate([dxs, dbm, dcm], axis=1)
    dpre, dcw, dcb = ssd_conv_bwd(proj, conv_w, conv_b, dxbc, name=f"{tag}_dconv")
    dproj = jnp.concatenate([dz, dpre, ddt.astype(BF16)], axis=1)
    dwin = matmul(hn, dproj, "tn", name=f"{tag}_dwin")[0]
    dhn = matmul(dproj, w_in[None], "nt", name=f"{tag}_dhn")
    dh, dgm = _norm_bwd(h, gm, [dhn, d], f"{tag}_dnorm")
    return (dh, dgm, dwin, dcw, dcb, ddtb[:, :SSD_H], dal[:, :SSD_H], ddsk[:, :SSD_H], dng, dwout)


_S5_NB = S5_NG // S5_BLK


def _s5_param_args(log_dt, a_re, a_im, b_re, b_im, c_im):
    n = S5_NG * S5_GS
    tr = lambda b: jnp.transpose(b, (0, 2, 1)).reshape(n, S5_P)
    return [log_dt.reshape(S5_NG, 1), a_re, a_im, tr(b_re), tr(b_im), c_im.reshape(n, S5_P)]


def _s5_blockdiag(t):
    nb, gl, a, b = t.shape
    eye = jnp.eye(gl, dtype=t.dtype)
    return (t[:, :, :, None, :] * eye[None, :, None, :, None]).reshape(nb, gl * a, gl * b)


def _s5_diag(t, a, b):
    nb = t.shape[0]
    gl = t.shape[1] // a
    eye = jnp.eye(gl, dtype=t.dtype)
    return jnp.sum(t.reshape(nb, gl, a, gl, b) * eye[None, :, None, :, None], axis=3)


def _s5_weights(bbr, bbi, c_re, cneg):
    sh = (_S5_NB, S5_BLK, S5_GS, S5_P)
    wb = jnp.concatenate([_s5_blockdiag(bbr.reshape(sh)), _s5_blockdiag(bbi.reshape(sh))], axis=2)
    tr = lambda cc: jnp.transpose(cc.reshape(sh), (0, 1, 3, 2))
    wc = jnp.concatenate([_s5_blockdiag(tr(c_re)), _s5_blockdiag(tr(cneg))], axis=1)
    return wb, wc


def s5_fwd(h, gm, prm, dsk, w_glu, tag):
    log_dt, a_re, a_im, b_re, b_im, c_re, c_im = prm
    hn = rowwise(f_rmsnorm, [gm], [h], [(D, F32)], bt=512, name=f"{tag}_norm")[0]
    pargs = _s5_param_args(log_dt, a_re, a_im, b_re, b_im, c_im)
    abr, abi, bbr, bbi, cneg = s5_param_fwd(pargs, name=f"{tag}_param")
    wb, wc = _s5_weights(bbr, bbi, c_re.reshape(S5_NG * S5_GS, S5_P), cneg)
    ar, ai = abr.reshape(_S5_NB, 1, _S5_W), abi.reshape(_S5_NB, 1, _S5_W)
    bu = matmul(hn, wb, "nn", G=_S5_NB, name=f"{tag}_bu")
    x = s5_scan_fwd(bu, ar, ai, name=f"{tag}_scan")
    ycp = matmul(x, wc, "nn", G=_S5_NB, name=f"{tag}_cx")
    yg = rowwise(f_s5_act, [dsk], [ycp, hn], [(D, BF16)], bt=512, name=f"{tag}_act")[0]
    vg = matmul(yg, w_glu[None], "nn", name=f"{tag}_glu")
    h2 = rowwise(f_glu_res, [], [vg, h], [(D, F32)], bt=512, name=f"{tag}_out")[0]
    return h2, (h, hn, pargs, wb, wc, ar, ai, x, ycp, yg, vg)


def s5_bwd(d, saved, gm, dsk, w_glu, tag):
    h, hn, pargs, wb, wc, ar, ai, x, ycp, yg, vg = saved
    _, (dvg,) = rowwise_vjp(f_glu, [], [vg], [d], [BF16], bt=256, name=f"{tag}_dout")
    dwglu = matmul(yg, dvg, "tn", name=f"{tag}_dwglu")[0]
    dyg = matmul(dvg, w_glu[None], "nt", name=f"{tag}_dyg")
    (ddsk,), (dycp, dhn1) = rowwise_vjp(f_s5_act, [dsk], [ycp, hn], [dyg], [F32, F32], bt=256, name=f"{tag}_dact")
    dx = matmul(dycp, wc, "nt", G=_S5_NB, name=f"{tag}_dx")
    dwc = matmul(x, dycp, "tn", G=_S5_NB, name=f"{tag}_dwc")
    lam, dar, dai = s5_scan_bwd(dx, x, ar, ai, name=f"{tag}_dscan")
    dwb = matmul(hn, lam, "tn", G=_S5_NB, name=f"{tag}_dwb")
    dhn2 = matmul(lam, wb, "nt", G=_S5_NB, name=f"{tag}_dhn")
    dh, dgm = _norm_bwd(h, gm, [dhn1, dhn2, d], f"{tag}_dnorm")
    n = S5_NG * S5_GS
    half = S5_BLK * S5_P
    d_bbr = _s5_diag(dwb[:, :, :half], S5_GS, S5_P).reshape(n, S5_P)
    d_bbi = _s5_diag(dwb[:, :, half:], S5_GS, S5_P).reshape(n, S5_P)
    from_c = lambda t: jnp.transpose(_s5_diag(t, S5_P, S5_GS), (0, 1, 3, 2)).reshape(n, S5_P)
    d_cre = from_c(dwc[:, :half, :])
    d_cneg = from_c(dwc[:, half:, :])
    cts = [dar.reshape(S5_NG, S5_P), dai.reshape(S5_NG, S5_P), d_bbr, d_bbi, d_cneg]
    dlog, dare, daim, dbre_t, dbim_t, dcim = s5_param_bwd(pargs, cts, name=f"{tag}_dparam")
    untr = lambda t: jnp.transpose(t.reshape(S5_NG, S5_GS, S5_P), (0, 2, 1))
    grads = (dlog.reshape(S5_NG), dare, daim, untr(dbre_t), untr(dbim_t),
             d_cre.reshape(S5_NG, S5_GS, S5_P), dcim.reshape(S5_NG, S5_GS, S5_P))
    return dh, dgm, grads, ddsk, dwglu


def _pad_last(w, n):
    return jnp.pad(w, [(0, 0)] * (w.ndim - 1) + [(0, n - w.shape[-1])])


_BIG = ("gla_w_in", "gla_w_out", "ssd_w_in", "ssd_w_out", "s5_w_glu", "ffn_w_gu", "ffn_w_down")


def interleave_gu(w):
    q = w.shape[-1] // 4
    return jnp.concatenate([w[..., :q], w[..., 2 * q:3 * q], w[..., q:2 * q], w[..., 3 * q:]], axis=-1)


def local_step(x, target, W):
    f32 = lambda a: a.astype(F32)
    row = lambda a: f32(a).reshape(1, -1)
    mixers = []
    for i in range(DEPTH):
        m, j = i % 3, i // 3
        gm = row(W["norm_mix_g"][i])
        if m == 0:
            args = (gm, W["gla_w_in"][j], jnp.pad(f32(W["gla_w_a2"][j]), ((0, LANE - GLA_RANK), (0, 0))),
                    row(W["gla_b_a"][j]), row(W["gla_norm_g"][j]), W["gla_w_out"][j])
        elif m == 1:
            pl_ = lambda a: _pad_last(row(a), LANE)
            args = (gm, W["ssd_w_in"][j], f32(W["ssd_conv_w"][j]),
                    row(W["ssd_conv_b"][j]), pl_(W["ssd_dt_bias"][j]), pl_(W["ssd_a_log"][j]), pl_(W["ssd_d"][j]),
                    row(W["ssd_norm_g"][j]), W["ssd_w_out"][j])
        else:
            prm = tuple(f32(W[k][j]) for k in ("s5_log_dt", "s5_a_re", "s5_a_im", "s5_b_re", "s5_b_im", "s5_c_re", "s5_c_im"))
            args = (gm, prm, row(W["s5_d"][j]), W["s5_w_glu"][j])
        mixers.append((m, j, args))
    ffns = [(row(W["norm_ffn_g"][i]), W["ffn_w_gu"][i], W["ffn_w_down"][i]) for i in range(DEPTH)]

    h = x
    saved = []
    for i in range(DEPTH):
        m, j, args = mixers[i]
        tag = f"l{i}_{('gla', 'ssd', 's5')[m]}"
        h, sm = (gla_fwd, ssd_fwd, s5_fwd)[m](h, *args, tag)
        h, sf = ffn_fwd(h, *ffns[i], f"l{i}_ffn")
        saved.append((sm, sf))
    loss, dfg, d = loss_head(h, row(W["final_norm_g"]), target, name="loss_head")

    G = {k: [None] * len(v) for k, v in W.items() if k != "final_norm_g"}
    G["final_norm_g"] = dfg.reshape(D)
    for i in reversed(range(DEPTH)):
        m, j, args = mixers[i]
        sm, sf = saved[i]
        d, dg, dwgu, dwd = ffn_bwd(d, sf, *ffns[i], f"l{i}_ffn")
        G["norm_ffn_g"][i], G["ffn_w_gu"][i], G["ffn_w_down"][i] = dg.reshape(D), dwgu, dwd
        tag = f"l{i}_{('gla', 'ssd', 's5')[m]}"
        if m == 0:
            d, dgm, dwin, dwa2, dba, dng, dwout = gla_bwd(d, sm, *args, tag)
            G["gla_w_in"][j], G["gla_w_a2"][j], G["gla_b_a"][j] = dwin, dwa2, dba.reshape(-1)
            G["gla_norm_g"][j], G["gla_w_out"][j] = dng.reshape(-1), dwout
        elif m == 1:
            d, dgm, dwin, dcw, dcb, ddtb, dal, ddsk, dng, dwout = ssd_bwd(d, sm, *args, tag)
            G["ssd_w_in"][j], G["ssd_conv_w"][j], G["ssd_conv_b"][j] = dwin, dcw, dcb.reshape(-1)
            G["ssd_dt_bias"][j], G["ssd_a_log"][j], G["ssd_d"][j] = ddtb.reshape(-1), dal.reshape(-1), ddsk.reshape(-1)
            G["ssd_norm_g"][j], G["ssd_w_out"][j] = dng.reshape(-1), dwout
        else:
            d, dgm, pg, ddsk, dwglu = s5_bwd(d, sm, args[0], args[2], args[3], tag)
            for k, v in zip(("s5_log_dt", "s5_a_re", "s5_a_im", "s5_b_re", "s5_b_im", "s5_c_re", "s5_c_im"), pg):
                G[k][j] = v
            G["s5_d"][j], G["s5_w_glu"][j] = ddsk.reshape(-1), dwglu
        G["norm_mix_g"][i] = dgm.reshape(D)
    grads = {k: (v if k == "final_norm_g" or k in _BIG else jnp.stack(v)) for k, v in G.items()}
    return loss, d, grads


_MESH = pl.DeviceIdType.MESH
_ANY = pl.BlockSpec(memory_space=pl.ANY)
_DMA = pltpu.SemaphoreType.DMA
_ROWS_ALIGN = 1024


def _place():
    return lax.axis_index("x"), lax.axis_index("y"), lax.axis_index("c")


def _other_chips(x, y):
    return [(1 - x, y), (x, 1 - y), (1 - x, 1 - y)]


def _remote(src, dst, send_sems, recv_sems, k, to):
    return pltpu.make_async_remote_copy(src_ref=src, dst_ref=dst, send_sem=send_sems.at[k], recv_sem=recv_sems.at[k],
                                        device_id=to, device_id_type=_MESH)


def gather_shards(loc, *, name):
    def body(in_ref, out_ref, send_sems, recv_sems, local_sem):
        x, y, c = _place()
        me, sibling = (x, y, c), (x, y, 1 - c)
        chips = _other_chips(x, y)

        def half(px, py, hc):
            return out_ref.at[2 * px + py, hc]

        mine = pltpu.make_async_copy(in_ref, out_ref.at[2 * x + y], local_sem)
        mine.start()
        first = [_remote(in_ref.at[c], half(x, y, c), send_sems, recv_sems, j, (*chip, c)) for j, chip in enumerate(chips)]
        for cp in first:
            cp.start()
        passed = [_remote(half(*chip, c), half(*chip, c), send_sems, recv_sems, 3 + j, sibling) for j, chip in enumerate(chips)]
        for j, chip in enumerate(chips):
            _remote(in_ref.at[c], half(*chip, c), send_sems, recv_sems, j, me).wait_recv()
            passed[j].start()
        for j, chip in enumerate(chips):
            _remote(in_ref.at[c], half(*chip, 1 - c), send_sems, recv_sems, 3 + j, me).wait_recv()
        for cp in first + passed:
            cp.wait_send()
        mine.wait()

    return pl.pallas_call(
        body, name=name, in_specs=[_ANY], out_specs=_ANY,
        out_shape=jax.ShapeDtypeStruct((4,) + loc.shape, loc.dtype),
        scratch_shapes=[_DMA((6,)), _DMA((6,)), _DMA(())],
    )(loc)


def _pos(px, py, perm):
    return 2 * py + px if perm else 2 * px + py


def _part(ref, kind, p, loc):
    if kind == "lead":
        return ref.at[p]
    return ref.at[:, pl.ds(pl.multiple_of(p * loc, LANE), loc)]


def _rows(ref, h, hr):
    return ref.at[pl.ds(h * hr, hr)]


def _rows_block(hr, width):
    return max(b for b in range(16, hr + 1, 16) if hr % b == 0 and (b * width <= (1 << 19) or b == 16))


def gather_big(locs, kinds, *, name):
    n = len(locs)

    def body(*refs):
        ins, outs = refs[:n], refs[n:2 * n]
        send_sems, recv_sems, local_sems = refs[2 * n:]
        x, y, c = _place()
        me, sibling = (x, y, c), (x, y, 1 - c)
        chips = _other_chips(x, y)

        def half(i, px, py, h):
            (kind, perm), (rows, loc) = kinds[i], locs[i].shape
            return _rows(_part(outs[i], kind, _pos(px, py, perm), loc), h, rows // 2)

        own, sends = [], []
        for i in range(n):
            (kind, perm), (rows, loc) = kinds[i], locs[i].shape
            own.append(pltpu.make_async_copy(ins[i], _part(outs[i], kind, _pos(x, y, perm), loc), local_sems.at[i]))
            own[-1].start()
            for j, chip in enumerate(chips):
                sends.append(_remote(_rows(ins[i], c, rows // 2), half(i, x, y, c), send_sems, recv_sems, 6 * i + j, (*chip, c)))
                sends[-1].start()
        for i in range(n):
            hr = locs[i].shape[0] // 2
            for j, chip in enumerate(chips):
                _remote(_rows(ins[i], c, hr), half(i, *chip, c), send_sems, recv_sems, 6 * i + j, me).wait_recv()
                sends.append(_remote(half(i, *chip, c), half(i, *chip, c), send_sems, recv_sems, 6 * i + 3 + j, sibling))
                sends[-1].start()
        for i in range(n):
            hr = locs[i].shape[0] // 2
            for j, chip in enumerate(chips):
                _remote(_rows(ins[i], c, hr), half(i, *chip, 1 - c), send_sems, recv_sems, 6 * i + 3 + j, me).wait_recv()
        for cp in sends:
            cp.wait_send()
        for cp in own:
            cp.wait()

    def out_shape(a, kind):
        rows, loc = a.shape
        return jax.ShapeDtypeStruct((4, rows, loc) if kind == "lead" else (rows, 4 * loc), a.dtype)

    return pl.pallas_call(
        body, name=name, in_specs=[_ANY] * n, out_specs=[_ANY] * n,
        out_shape=[out_shape(a, k[0]) for a, k in zip(locs, kinds)],
        scratch_shapes=[_DMA((6 * n,)), _DMA((6 * n,)), _DMA((n,))],
    )(*locs)


def pair_swap(ps, kinds, *, name):
    n = len(ps)

    def body(*refs):
        ins, outs = refs[:n], refs[n:2 * n]
        send_sems, recv_sems = refs[2 * n:]
        x, y, c = _place()
        cps = []
        for i in range(n):
            if kinds[i][0] == "lead":
                hr = ps[i].shape[1] // 2
                src = ins[i].at[:, pl.ds((1 - c) * hr, hr)]
            else:
                hr = ps[i].shape[0] // 2
                src = _rows(ins[i], 1 - c, hr)
            cps.append(_remote(src, outs[i], send_sems, recv_sems, i, (x, y, 1 - c)))
            cps[-1].start()
        for cp in cps:
            cp.wait()

    def out_shape(a, kind):
        s = a.shape
        return jax.ShapeDtypeStruct((4, s[1] // 2, s[2]) if kind == "lead" else (s[0] // 2, s[1]), a.dtype)

    return pl.pallas_call(
        body, name=name, in_specs=[_ANY] * n, out_specs=[_ANY] * n,
        out_shape=[out_shape(a, k[0]) for a, k in zip(ps, kinds)], scratch_shapes=[_DMA((n,)), _DMA((n,))],
    )(*ps)


def pair_add(p, got, c_arr, kind, *, name):
    if kind == "lead":
        _, hr, cols = got.shape
        br = _rows_block(hr, cols)
        nb = hr // br
        grid = (4, nb)
        p_spec = pl.BlockSpec((None, br, cols), lambda s, i, cr: (s, cr[0] * nb + i, 0))
        g_spec = pl.BlockSpec((None, br, cols), lambda s, i, cr: (s, i, 0))
    else:
        hr, w = got.shape
        br = _rows_block(hr, w)
        nb = hr // br
        grid = (nb,)
        p_spec = pl.BlockSpec((br, w), lambda i, cr: (cr[0] * nb + i, 0))
        g_spec = pl.BlockSpec((br, w), lambda i, cr: (i, 0))

    def body(c_ref, p_ref, g_ref, o_ref):
        o_ref[...] = (p_ref[...] + g_ref[...]).astype(o_ref.dtype)

    return pl.pallas_call(
        body, name=name, out_shape=jax.ShapeDtypeStruct(got.shape, BF16),
        grid_spec=pltpu.PrefetchScalarGridSpec(num_scalar_prefetch=1, grid=grid, in_specs=[p_spec, g_spec], out_specs=g_spec),
        compiler_params=_cparams(len(grid)),
    )(c_arr, p, got)


def chip_scatter(qs, kinds, locs, *, name):
    n = len(qs)

    def body(*refs):
        ins, outs = refs[:n], refs[n:2 * n]
        send_sems, recv_sems = refs[2 * n:]
        x, y, c = _place()
        cps = []
        for i in range(n):
            kind, perm = kinds[i]
            for j, (px, py) in enumerate(_other_chips(x, y)):
                cps.append(_remote(_part(ins[i], kind, _pos(px, py, perm), locs[i]), outs[i].at[j], send_sems, recv_sems,
                                   3 * i + j, (px, py, c)))
                cps[-1].start()
        for cp in cps:
            cp.wait()

    def out_shape(a, kind, loc):
        hr = a.shape[1] if kind == "lead" else a.shape[0]
        return jax.ShapeDtypeStruct((3, hr, loc), a.dtype)

    return pl.pallas_call(
        body, name=name, in_specs=[_ANY] * n, out_specs=[_ANY] * n,
        out_shape=[out_shape(a, k[0], l) for a, k, l in zip(qs, kinds, locs)],
        scratch_shapes=[_DMA((3 * n,)), _DMA((3 * n,))],
    )(*qs)


def chip_add(q, r, pos_arr, kind, loc, *, name):
    _, hr, _ = r.shape
    br = _rows_block(hr, loc)
    if kind == "lead":
        q_spec = pl.BlockSpec((None, br, loc), lambda i, pr: (pr[0], i, 0))
    else:
        q_spec = pl.BlockSpec((br, loc), lambda i, pr: (i, pr[0]))
    r_spec = pl.BlockSpec((3, br, loc), lambda i, pr: (0, i, 0))
    o_spec = pl.BlockSpec((br, loc), lambda i, pr: (i, 0))

    def body(p_ref, q_ref, r_ref, o_ref):
        acc = q_ref[...].astype(F32)
        for j in range(3):
            acc = acc + r_ref[j].astype(F32)
        o_ref[...] = acc

    return pl.pallas_call(
        body, name=name, out_shape=jax.ShapeDtypeStruct((hr, loc), F32),
        grid_spec=pltpu.PrefetchScalarGridSpec(num_scalar_prefetch=1, grid=(hr // br,), in_specs=[q_spec, r_spec], out_specs=o_spec),
        compiler_params=_cparams(1),
    )(pos_arr, q, r)


def share_rows(fs, *, name):
    n = len(fs)

    def body(*refs):
        ins, outs = refs[:n], refs[n:2 * n]
        send_sems, recv_sems, local_sems = refs[2 * n:]
        x, y, c = _place()
        cps = []
        for i in range(n):
            hr = fs[i].shape[0]
            mine = pltpu.make_async_copy(ins[i], _rows(outs[i], c, hr), local_sems.at[i])
            mine.start()
            cp = _remote(ins[i], _rows(outs[i], c, hr), send_sems, recv_sems, i, (x, y, 1 - c))
            cp.start()
            cps.append((mine, cp))
        for i, (mine, cp) in enumerate(cps):
            hr = fs[i].shape[0]
            _remote(ins[i], _rows(outs[i], 1 - c, hr), send_sems, recv_sems, i, (x, y, c)).wait_recv()
            cp.wait_send()
            mine.wait()

    return pl.pallas_call(
        body, name=name, in_specs=[_ANY] * n, out_specs=[_ANY] * n,
        out_shape=[jax.ShapeDtypeStruct((2 * f.shape[0], f.shape[1]), f.dtype) for f in fs],
        scratch_shapes=[_DMA((n,)), _DMA((n,)), _DMA((n,))],
    )(*fs)


def gather_all(v, *, name):
    def body(v_ref, out_ref, send_sems, recv_sems, local_sem):
        x, y, c = _place()
        flip = lambda p, m: 1 - p if m else p
        peers = [(flip(x, m & 4), flip(y, m & 2), flip(c, m & 1)) for m in range(1, 8)]
        idx = lambda p: 4 * p[0] + 2 * p[1] + p[2]
        mine = pltpu.make_async_copy(v_ref, out_ref.at[idx((x, y, c))], local_sem)
        mine.start()
        cps = [_remote(v_ref, out_ref.at[idx((x, y, c))], send_sems, recv_sems, k, p) for k, p in enumerate(peers)]
        for cp in cps:
            cp.start()
        for k, p in enumerate(peers):
            _remote(v_ref, out_ref.at[idx(p)], send_sems, recv_sems, k, p).wait_recv()
        for cp in cps:
            cp.wait_send()
        mine.wait()

    return pl.pallas_call(
        body, name=name, in_specs=[_ANY], out_specs=_ANY, out_shape=jax.ShapeDtypeStruct((8,) + v.shape, v.dtype),
        scratch_shapes=[_DMA((7,)), _DMA((7,)), _DMA(())],
    )(v)


def sum_stack(a, extra=None, *, name):
    n, R, L = a.shape
    br = _pick(R, _ROWS_ALIGN, 8)

    def body(*refs):
        a_ref, o_ref = refs[0], refs[-1]
        acc = refs[1][...] if extra is not None else a_ref[0]
        for i in range(0 if extra is not None else 1, n):
            acc = acc + a_ref[i]
        o_ref[...] = acc

    row = pl.BlockSpec((br, L), lambda i: (i, 0))
    specs = [pl.BlockSpec((n, br, L), lambda i: (0, i, 0))] + ([row] if extra is not None else [])
    args = [a] + ([extra] if extra is not None else [])
    return pl.pallas_call(body, name=name, grid=(R // br,), in_specs=specs, out_specs=row,
                          out_shape=jax.ShapeDtypeStruct((R, L), a.dtype), compiler_params=_cparams(1))(*args)


def adamw(w, g, m, v, *, name):
    shape = w.shape
    size = math.prod(shape)
    last = shape[-1]
    if last % LANE != 0 and size % LANE == 0 and size <= (1 << 20):
        last = LANE
    rows = size // last
    budget = (1 << 18) // last
    br = rows
    if rows > budget:
        br = max(c for c in range(8, budget + 1, 8) if rows % c == 0)
    v2 = lambda a: a.reshape(rows, last)

    def body(w_ref, g_ref, m_ref, v_ref, d_ref, nm_ref, nv_ref):
        gg = g_ref[...]
        nm = ADAM_B1 * m_ref[...] + (1.0 - ADAM_B1) * gg
        nv = ADAM_B2 * v_ref[...] + (1.0 - ADAM_B2) * (gg * gg)
        m_hat = nm / (1.0 - ADAM_B1 ** ADAM_STEP)
        v_hat = nv / (1.0 - ADAM_B2 ** ADAM_STEP)
        d_ref[...] = -ADAM_LR * (m_hat / (jnp.sqrt(v_hat) + ADAM_EPS) + ADAM_WD * w_ref[...])
        nm_ref[...] = nm
        nv_ref[...] = nv

    spec = pl.BlockSpec((br, last), lambda i: (i, 0))
    outs = pl.pallas_call(
        body, name=name, grid=(rows // br,), in_specs=[spec] * 4, out_specs=[spec] * 3,
        out_shape=[jax.ShapeDtypeStruct((rows, last), F32)] * 3, compiler_params=_cparams(1),
    )(v2(w), v2(g), v2(m), v2(v))
    return [o.reshape(shape) for o in outs]


_WEIGHTS = ["norm_mix_g", "norm_ffn_g", "gla_w_in", "gla_w_a2", "gla_b_a", "gla_norm_g", "gla_w_out", "ssd_w_in",
            "ssd_conv_w", "ssd_conv_b", "ssd_dt_bias", "ssd_a_log", "ssd_d", "ssd_norm_g", "ssd_w_out", "s5_log_dt",
            "s5_a_re", "s5_a_im", "s5_b_re", "s5_b_im", "s5_c_re", "s5_c_im", "s5_d", "s5_w_glu", "ffn_w_gu",
            "ffn_w_down", "final_norm_g"]
_SHARD_AXIS = {"gla_w_in": 2, "gla_w_a2": 2, "gla_b_a": 1, "gla_norm_g": 1, "gla_w_out": 1, "ssd_w_in": 2,
               "ssd_conv_w": 2, "ssd_w_out": 1, "s5_d": 1, "s5_w_glu": 2, "ffn_w_gu": 2, "ffn_w_down": 1}
_SMALL_SHARDED = [n for n in _WEIGHTS if n in _SHARD_AXIS and n not in _BIG]
_REPLICATED = [n for n in _WEIGHTS if n not in _SHARD_AXIS]
_BIG_KIND = {"gla_w_in": ("lead", False), "gla_w_out": ("lead", False), "ssd_w_in": ("lead", False),
             "ssd_w_out": ("lead", False), "s5_w_glu": ("cols", False), "ffn_w_gu": ("cols", True),
             "ffn_w_down": ("lead", False)}
_PADDED_IN = {"gla_w_in": GLA_INP, "ssd_w_in": SSD_INP}


def _to_rows(flat, parts=1):
    per = -(-flat.shape[0] // (parts * LANE * _ROWS_ALIGN)) * _ROWS_ALIGN
    flat = jnp.pad(flat, (0, parts * per * LANE - flat.shape[0]))
    return flat.reshape(parts, per, LANE)


def _big_layers(local):
    return [(n, j, local[n][j].reshape(-1, local[n].shape[-1])) for n in _BIG for j in range(local[n].shape[0])]


def _gather_weights(local):
    full = {}
    layers = _big_layers(local)
    got = gather_big([w.astype(BF16) for _, _, w in layers], [_BIG_KIND[n] for n, _, _ in layers], name="gather_weights")
    for (n, j, w), g in zip(layers, got):
        if n in _PADDED_IN:
            g = jnp.concatenate([g[s] for s in range(4)] + [jnp.zeros((g.shape[1], _PADDED_IN[n] - 4 * g.shape[2]), BF16)], axis=1)
        elif _BIG_KIND[n][0] == "lead":
            g = g.reshape(4 * g.shape[1], g.shape[2])
        full.setdefault(n, []).append(g)
    flat = jnp.concatenate([local[n].astype(F32).reshape(-1) for n in _SMALL_SHARDED])
    got = gather_shards(_to_rows(flat, 2), name="gather_small_weights").reshape(4, -1)
    off = 0
    for n in _SMALL_SHARDED:
        bs = local[n].shape
        sz = math.prod(bs)
        seg = got[:, off:off + sz].reshape((4,) + bs)
        off += sz
        ax = _SHARD_AXIS[n]
        full[n] = jnp.moveaxis(seg, 0, ax).reshape(bs[:ax] + (4 * bs[ax],) + bs[ax + 1:])
    return full


def _reduce_big(grads, local, x, y, c):
    ops = []
    for n in _BIG:
        kind = _BIG_KIND[n]
        for j, g in enumerate(grads[n]):
            loc = local[n].shape[-1] if kind[0] == "cols" or n in _PADDED_IN else g.shape[1]
            if n in _PADDED_IN:
                g = jnp.stack([g[:, s * loc:(s + 1) * loc] for s in range(4)])
            elif kind[0] == "lead":
                g = g.reshape(4, g.shape[0] // 4, g.shape[1])
            ops.append((n, j, kind, loc, g))
    kinds = [o[2] for o in ops]
    c_arr = jnp.reshape(c, (1,)).astype(jnp.int32)
    gots = pair_swap([o[4] for o in ops], kinds, name="reduce_pair_swap")
    qs = [pair_add(o[4], got, c_arr, o[2][0], name=f"reduce_pair_add_{o[0]}{o[1]}") for o, got in zip(ops, gots)]
    rs = chip_scatter(qs, kinds, [o[3] for o in ops], name="reduce_chip_scatter")
    fs = [chip_add(q, r, jnp.reshape(_pos(x, y, o[2][1]), (1,)).astype(jnp.int32), o[2][0], o[3],
                   name=f"reduce_chip_add_{o[0]}{o[1]}") for o, q, r in zip(ops, qs, rs)]
    outs = share_rows(fs, name="reduce_share")
    red = {}
    for o, r in zip(ops, outs):
        red.setdefault(o[0], []).append(r)
    return {n: jnp.stack(v).reshape(local[n].shape) for n, v in red.items()}


def _reduce_small(grads, local, x, y):
    names = _REPLICATED + _SMALL_SHARDED
    flat = jnp.concatenate([grads[n].astype(F32).reshape(-1) for n in names])
    n_el = flat.shape[0]
    rows = -(-n_el // (LANE * 8)) * 8
    v = jnp.pad(flat, (0, rows * LANE - n_el)).reshape(rows, LANE)
    red = sum_stack(gather_all(v, name="reduce_small_gather"), name="reduce_small_add").reshape(-1)
    out, off = {}, 0
    for n in names:
        sz = math.prod(grads[n].shape)
        g = red[off:off + sz].reshape(grads[n].shape)
        off += sz
        if n in _SHARD_AXIS:
            ax = _SHARD_AXIS[n]
            loc = local[n].shape[ax]
            g = lax.dynamic_slice_in_dim(g, (2 * x + y) * loc, loc, axis=ax)
        out[n] = g
    return out


def kernel(x, norm_mix_g, norm_ffn_g, gla_w_in, gla_w_a2, gla_b_a, gla_norm_g, gla_w_out, ssd_w_in, ssd_conv_w, ssd_conv_b, ssd_dt_bias, ssd_a_log, ssd_d, ssd_norm_g, ssd_w_out, s5_log_dt, s5_a_re, s5_a_im, s5_b_re, s5_b_im, s5_c_re, s5_c_im, s5_d, s5_w_glu, ffn_w_gu, ffn_w_down, final_norm_g, loss_target, m_norm_mix_g, m_norm_ffn_g, m_gla_w_in, m_gla_w_a2, m_gla_b_a, m_gla_norm_g, m_gla_w_out, m_ssd_w_in, m_ssd_conv_w, m_ssd_conv_b, m_ssd_dt_bias, m_ssd_a_log, m_ssd_d, m_ssd_norm_g, m_ssd_w_out, m_s5_log_dt, m_s5_a_re, m_s5_a_im, m_s5_b_re, m_s5_b_im, m_s5_c_re, m_s5_c_im, m_s5_d, m_s5_w_glu, m_ffn_w_gu, m_ffn_w_down, m_final_norm_g, v_norm_mix_g, v_norm_ffn_g, v_gla_w_in, v_gla_w_a2, v_gla_b_a, v_gla_norm_g, v_gla_w_out, v_ssd_w_in, v_ssd_conv_w, v_ssd_conv_b, v_ssd_dt_bias, v_ssd_a_log, v_ssd_d, v_ssd_norm_g, v_ssd_w_out, v_s5_log_dt, v_s5_a_re, v_s5_a_im, v_s5_b_re, v_s5_b_im, v_s5_c_re, v_s5_c_im, v_s5_d, v_s5_w_glu, v_ffn_w_gu, v_ffn_w_down, v_final_norm_g):
    given = dict(locals())
    local = {n: given[n] for n in _WEIGHTS}
    px, py, pc = _place()

    full = dict(local)
    full.update(_gather_weights(local))
    loss, grad_x, grads = local_step(x[0], loss_target[0], full)
    loss = lax.psum(loss, ("x", "y", "c"))

    red = _reduce_big(grads, local, px, py, pc)
    red.update(_reduce_small(grads, local, px, py))

    deltas, new_m, new_v = {}, {}, {}
    for n in _WEIGHTS:
        deltas[n], new_m[n], new_v[n] = adamw(local[n], red[n], given["m_" + n], given["v_" + n], name=f"adamw_{n}")
    return (loss, grad_x[None], *[red[n] for n in _WEIGHTS], *[deltas[n] for n in _WEIGHTS],
            *[new_m[n] for n in _WEIGHTS], *[new_v[n] for n in _WEIGHTS])
```

```python
import functools
import math

import jax
import jax.numpy as jnp
from jax import lax
from jax.experimental import pallas as pl
from jax.experimental.pallas import tpu as pltpu

F32 = jnp.float32
BF16 = jnp.bfloat16

D = 1024
DEPTH = 4
CH = 64
EPS = 1e-6
GLA_H, GLA_DK, GLA_DV, GLA_RANK, GLA_TAU = 4, 128, 256, 16, 16.0
GLA_QK = GLA_H * GLA_DK
GLA_VD = GLA_H * GLA_DV
GLA_IN = 2 * GLA_QK + 2 * GLA_VD + GLA_RANK
GLA_INP = 3200
SSD_DI, SSD_HD, SSD_H, SSD_G, SSD_N, SSD_K = 2048, 64, 32, 8, 128, 4
SSD_GN = SSD_G * SSD_N
SSD_CONV = SSD_DI + 2 * SSD_GN
SSD_IN = SSD_DI + SSD_CONV + SSD_H
SSD_INP = 6272
S5_GS, S5_NG, S5_P = 16, 64, 64
S5_BLK = 8
FFN_H = 2816
LANE = 128
VMEM_LIMIT = 52 * 1024 * 1024
_MATMUL_VMEM = 40 * 1024 * 1024

ADAM_LR, ADAM_B1, ADAM_B2, ADAM_EPS, ADAM_WD, ADAM_STEP = 0.001, 0.9, 0.999, 1e-08, 0.01, 10

_ARB = "arbitrary"


def _cparams(n):
    return pltpu.CompilerParams(dimension_semantics=(_ARB,) * n, vmem_limit_bytes=VMEM_LIMIT)


def _pick(n, target, mult=LANE):
    best = None
    for c in range(mult, min(n, target) + 1, mult):
        if n % c == 0:
            best = c
    return best if best is not None else n


_DN = {"nn": (((1,), (0,)), ((), ())), "nt": (((1,), (1,)), ((), ())), "tn": (((0,), (0,)), ((), ()))}


def _dot_raw(a, b, form):
    return lax.dot_general(a.astype(BF16), b.astype(BF16), _DN[form], preferred_element_type=F32)


@functools.partial(jax.custom_vjp, nondiff_argnums=(2,))
def bdot(a, b, form):
    return _dot_raw(a, b, form)


def _bdot_fwd(a, b, form):
    return _dot_raw(a, b, form), (a, b)


def _bdot_bwd(form, res, g):
    a, b = res
    if form == "nn":
        return _dot_raw(g, b, "nt"), _dot_raw(a, g, "tn")
    if form == "nt":
        return _dot_raw(g, b, "nn"), _dot_raw(g, a, "tn")
    return _dot_raw(b, g, "nt"), _dot_raw(a, g, "nn")


bdot.defvjp(_bdot_fwd, _bdot_bwd)


def _hdot(a, b):
    return jnp.dot(a, b, precision=lax.Precision.HIGHEST, preferred_element_type=F32)


@jax.custom_vjp
def cdot_left(c, ct, x):
    return _hdot(c, x)


def _cdl_fwd(c, ct, x):
    return _hdot(c, x), (c, ct)


def _cdl_bwd(res, g):
    c, ct = res
    return jnp.zeros_like(c), jnp.zeros_like(ct), _hdot(ct, g)


cdot_left.defvjp(_cdl_fwd, _cdl_bwd)


@jax.custom_vjp
def cdot_right(x, c, ct):
    return _hdot(x, c)


def _cdr_fwd(x, c, ct):
    return _hdot(x, c), (c, ct)


def _cdr_bwd(res, g):
    c, ct = res
    return _hdot(g, ct), jnp.zeros_like(c), jnp.zeros_like(ct)


cdot_right.defvjp(_cdr_fwd, _cdr_bwd)


def _sigmoid(x):
    return 1.0 / (1.0 + jnp.exp(-x))


def _silu(x):
    return x * _sigmoid(x)


def _softplus(x):
    return jnp.maximum(x, 0.0) + jnp.log(1.0 + jnp.exp(-jnp.abs(x)))


def _log_sigmoid(x):
    return jnp.minimum(x, 0.0) - jnp.log(1.0 + jnp.exp(-jnp.abs(x)))


def _gelu(x):
    c = math.sqrt(2.0 / math.pi)
    return 0.5 * x * (1.0 + jnp.tanh(c * (x + 0.044715 * (x * x * x))))


def _rms(x, g):
    return x * lax.rsqrt(jnp.mean(x * x, axis=-1, keepdims=True) + EPS) * g


def _iota2(shape, axis):
    return lax.broadcasted_iota(jnp.int32, shape, axis)


def matmul(a, b, form, *, name, G=1, out_dtype=F32, add=None):
    isz = lambda t: jnp.dtype(t.dtype).itemsize
    osz = jnp.dtype(out_dtype).itemsize + (isz(add) if add is not None else 0)

    def fits(bm, bn, bk):
        return 2 * (bm * bk * isz(a) + bk * bn * isz(b) + bm * bn * osz) + 4 * bm * bn <= _MATMUL_VMEM

    if form in ("nn", "nt"):
        M = a.shape[0]
        K = a.shape[1] // G
        N = b.shape[2] if form == "nn" else b.shape[1]
        bm, bn, bk = min(M, 1024), _pick(N, 1536), _pick(K, 2048)
        while not fits(bm, bn, bk) and bk % 256 == 0:
            bk //= 2
        nj, nk = N // bn, K // bk
        grid = (G, M // bm, nj, nk)
        a_spec = pl.BlockSpec((bm, bk), lambda g, i, j, k: (i, g * nk + k))
        if form == "nn":
            b_spec = pl.BlockSpec((None, bk, bn), lambda g, i, j, k: (g, k, j))
        else:
            b_spec = pl.BlockSpec((None, bn, bk), lambda g, i, j, k: (g, j, k))
        o_spec = pl.BlockSpec((bm, bn), lambda g, i, j, k: (i, g * nj + j))
        out_shape = jax.ShapeDtypeStruct((M, G * N), out_dtype)
    else:
        T = a.shape[0]
        Ka, Nb = a.shape[1] // G, b.shape[1] // G
        bm, bn, bk = _pick(Ka, 1408), _pick(Nb, 1536), min(T, 2048)
        while not fits(bm, bn, bk) and bk % 512 == 0:
            bk //= 2
        ni, nj, nk = Ka // bm, Nb // bn, T // bk
        grid = (G, ni, nj, nk)
        a_spec = pl.BlockSpec((bk, bm), lambda g, i, j, k: (k, g * ni + i))
        b_spec = pl.BlockSpec((bk, bn), lambda g, i, j, k: (k, g * nj + j))
        o_spec = pl.BlockSpec((None, bm, bn), lambda g, i, j, k: (g, i, j))
        out_shape = jax.ShapeDtypeStruct((G, Ka, Nb), out_dtype)
    has_add = add is not None

    def finish(refs, r):
        if has_add:
            r = r + refs[2][...].astype(F32)
        o_ref = refs[3] if has_add else refs[2]
        o_ref[...] = r.astype(o_ref.dtype)

    def body_one(*refs):
        finish(refs, _dot_raw(refs[0][...], refs[1][...], form))

    def body_acc(*refs):
        acc_ref = refs[-1]
        k = pl.program_id(3)

        @pl.when(k == 0)
        def _():
            acc_ref[...] = jnp.zeros_like(acc_ref)

        acc_ref[...] += _dot_raw(refs[0][...], refs[1][...], form)

        @pl.when(k == nk - 1)
        def _():
            finish(refs, acc_ref[...])

    in_specs = [a_spec, b_spec]
    args = [a, b]
    if has_add:
        in_specs.append(o_spec)
        args.append(add)
    return pl.pallas_call(
        body_one if nk == 1 else body_acc, name=name, grid=grid, in_specs=in_specs, out_specs=o_spec,
        out_shape=out_shape, scratch_shapes=[] if nk == 1 else [pltpu.VMEM((bm, bn), F32)],
        compiler_params=_cparams(4),
    )(*args)


def ffn_up(hn, w_il, *, name):
    T = hn.shape[0]
    bm, hb = min(T, 512), FFN_H // 2

    def body(a_ref, b_ref, act_ref, gu_ref):
        r = _dot_raw(a_ref[...], b_ref[...], "nn")
        act_ref[...] = (_silu(r[:, :hb]) * r[:, hb:]).astype(act_ref.dtype)
        gu_ref[...] = r.astype(gu_ref.dtype)

    return pl.pallas_call(
        body, name=name, grid=(2, T // bm),
        in_specs=[pl.BlockSpec((bm, D), lambda j, i: (i, 0)), pl.BlockSpec((D, 2 * hb), lambda j, i: (0, j))],
        out_specs=[pl.BlockSpec((bm, hb), lambda j, i: (i, j)), pl.BlockSpec((bm, 2 * hb), lambda j, i: (i, j))],
        out_shape=[jax.ShapeDtypeStruct((T, FFN_H), BF16), jax.ShapeDtypeStruct((T, 2 * FFN_H), BF16)],
        compiler_params=_cparams(2),
    )(hn, w_il)


_DACT_CHUNK = 512


def ffn_dact(d, w_down, gu, *, name):
    T = d.shape[0]
    bm, hb = min(T, 512), FFN_H // 2

    def body(d_ref, w_ref, gu_ref, o_ref):
        d_blk = d_ref[...].astype(BF16)
        for lo in range(0, hb, _DACT_CHUNK):
            hi = min(lo + _DACT_CHUNK, hb)
            da = _dot_raw(d_blk, w_ref[lo:hi, :], "nt")
            g, u = gu_ref[:, lo:hi].astype(F32), gu_ref[:, hb + lo:hb + hi].astype(F32)
            sg = _sigmoid(g)
            o_ref[:, lo:hi] = (da * u * (sg * (1.0 + g * (1.0 - sg)))).astype(o_ref.dtype)
            o_ref[:, hb + lo:hb + hi] = (da * (g * sg)).astype(o_ref.dtype)

    return pl.pallas_call(
        body, name=name, grid=(2, T // bm),
        in_specs=[pl.BlockSpec((bm, D), lambda j, i: (i, 0)), pl.BlockSpec((hb, D), lambda j, i: (j, 0)),
                  pl.BlockSpec((bm, 2 * hb), lambda j, i: (i, j))],
        out_specs=pl.BlockSpec((bm, 2 * hb), lambda j, i: (i, j)),
        out_shape=jax.ShapeDtypeStruct((T, 2 * FFN_H), BF16), compiler_params=_cparams(2),
    )(d, w_down, gu)


def _row_entry(e):
    return e if isinstance(e, tuple) else (e, e.shape[1], 0)


def _row_spec(bt, e):
    _, width, idx = e
    return pl.BlockSpec((bt, width), lambda i: (i, idx))


def _full_spec(p):
    return pl.BlockSpec(p.shape, lambda i: (0,) * p.ndim)


def rowwise(f, params, rows, outs, *, bt, name):
    rows = [_row_entry(e) for e in rows]
    T = rows[0][0].shape[0]
    bt = min(bt, T)
    np_, nr = len(params), len(rows)

    def body(*refs):
        p = tuple(r[...].astype(F32) for r in refs[:np_])
        rw = tuple(r[...].astype(F32) for r in refs[np_:np_ + nr])
        res = f(p, rw)
        for o_ref, o in zip(refs[np_ + nr:], res):
            o_ref[...] = o.astype(o_ref.dtype)

    res = pl.pallas_call(
        body, name=name, grid=(T // bt,),
        in_specs=[_full_spec(p) for p in params] + [_row_spec(bt, e) for e in rows],
        out_specs=[pl.BlockSpec((bt, w), lambda i: (i, 0)) for w, _ in outs],
        out_shape=[jax.ShapeDtypeStruct((T, w), dt) for w, dt in outs],
        compiler_params=_cparams(1),
    )(*params, *[e[0] for e in rows])
    return list(res)


def rowwise_vjp(f, params, rows, cts, drow_dtypes, *, bt, name):
    rows = [_row_entry(e) for e in rows]
    cts = [_row_entry(e) for e in cts]
    T = rows[0][0].shape[0]
    bt = min(bt, T)
    np_, nr, nc = len(params), len(rows), len(cts)
    want = [i for i, dt in enumerate(drow_dtypes) if dt is not None]

    def body(*refs):
        p = tuple(r[...].astype(F32) for r in refs[:np_])
        rw = tuple(r[...].astype(F32) for r in refs[np_:np_ + nr])
        ct = tuple(r[...].astype(F32) for r in refs[np_ + nr:np_ + nr + nc])
        outs = refs[np_ + nr + nc:]
        _, vjp = jax.vjp(f, p, rw)
        dp, dr = vjp(ct)

        @pl.when(pl.program_id(0) == 0)
        def _():
            for o in outs[:np_]:
                o[...] = jnp.zeros_like(o)

        for o, d in zip(outs[:np_], dp):
            o[...] += d
        for o, i in zip(outs[np_:], want):
            o[...] = dr[i].astype(o.dtype)

    res = pl.pallas_call(
        body, name=name, grid=(T // bt,),
        in_specs=[_full_spec(p) for p in params] + [_row_spec(bt, e) for e in rows] + [_row_spec(bt, e) for e in cts],
        out_specs=[_full_spec(p) for p in params] + [pl.BlockSpec((bt, rows[i][1]), lambda i_: (i_, 0)) for i in want],
        out_shape=[jax.ShapeDtypeStruct(p.shape, F32) for p in params]
        + [jax.ShapeDtypeStruct((T, rows[i][1]), drow_dtypes[i]) for i in want],
        compiler_params=_cparams(1),
    )(*params, *[e[0] for e in rows], *[e[0] for e in cts])
    res = list(res)
    return res[:np_], res[np_:]


def f_rmsnorm(p, r):
    return (_rms(r[0], p[0]),)


def f_rmsnorm_res(p, r):
    return (_rms(r[0], p[0]), r[0])


def f_swiglu(p, r):
    gu = r[0]
    return (_silu(gu[:, :FFN_H]) * gu[:, FFN_H:],)


def f_gla_gate_in(p, r):
    w_a2, b_a = p
    z = bdot(r[0], w_a2, "nn") + b_a
    return (_log_sigmoid(z) / GLA_TAU,)


def f_gla_gate_in_fwd(p, r):
    w_a2, b_a = p
    z = _dot_raw(r[0], w_a2, "nn") + b_a
    return (_log_sigmoid(z) / GLA_TAU,)


def f_gla_out(p, r):
    (ng,) = p
    o, rr = r
    parts = []
    for h in range(GLA_H):
        sl = slice(h * GLA_DV, (h + 1) * GLA_DV)
        parts.append(_rms(o[:, sl], ng[:, sl]) * _silu(rr[:, sl]))
    return (jnp.concatenate(parts, axis=1),)


def f_ssd_out(p, r):
    (ng,) = p
    y, z = r
    t = y * _silu(z)
    gsz = SSD_DI // SSD_G
    parts = []
    for g in range(SSD_G):
        sl = slice(g * gsz, (g + 1) * gsz)
        parts.append(_rms(t[:, sl], ng[:, sl]))
    return (jnp.concatenate(parts, axis=1),)


def f_s5_act(p, r):
    (dsk,) = p
    ycp, u = r
    return (_gelu(ycp + dsk * u),)


def f_glu_res(p, r):
    vg, h = r
    return (vg[:, :D] * _sigmoid(vg[:, D:]) + h,)


def f_glu(p, r):
    vg = r[0]
    return (vg[:, :D] * _sigmoid(vg[:, D:]),)


def loss_head(h, g, target, *, name):
    T = h.shape[0]
    bt = min(T, 256)

    def lossf(g_, h_, t_):
        e = _rms(h_, g_) - t_
        return (0.5 / D) * jnp.sum(e * e)

    def body(g_ref, h_ref, t_ref, loss_ref, dg_ref, dh_ref):
        @pl.when(pl.program_id(0) == 0)
        def _():
            loss_ref[...] = jnp.zeros_like(loss_ref)
            dg_ref[...] = jnp.zeros_like(dg_ref)

        val, vjp = jax.vjp(lossf, g_ref[...], h_ref[...], t_ref[...])
        dg, dh, _ = vjp(jnp.ones((), F32))
        loss_ref[...] += jnp.full(loss_ref.shape, val, F32)
        dg_ref[...] += dg
        dh_ref[...] = dh

    row = pl.BlockSpec((bt, D), lambda i: (i, 0))
    one = pl.BlockSpec((1, D), lambda i: (0, 0))
    loss, dg, dh = pl.pallas_call(
        body, name=name, grid=(T // bt,), in_specs=[one, row, row],
        out_specs=[pl.BlockSpec((1, LANE), lambda i: (0, 0)), one, row],
        out_shape=[jax.ShapeDtypeStruct((1, LANE), F32), jax.ShapeDtypeStruct((1, D), F32),
                   jax.ShapeDtypeStruct((T, D), F32)],
        compiler_params=_cparams(1),
    )(g, h, target)
    return loss[0, 0], dg, dh


def _gla_consts():
    r, c = _iota2((CH, CH), 0), _iota2((CH, CH), 1)
    return (r >= c).astype(F32), (r <= c).astype(F32), r >= c


def _gla_chunk(q, k, v, la, st, consts, dot, cdl):
    L, Lt, tril = consts
    lc = cdl(L, Lt, la)
    lend = lc[CH - 1:CH, :]
    e, ei = jnp.exp(lc), jnp.exp(-lc)
    qs = q * (GLA_DK ** -0.5)
    qf, kf, qb, kb = qs * e, k * ei, qs * ei, k * e
    sc = jnp.where(tril, dot(qf, kf, "nt"), dot(qb, kb, "nt"))
    o = dot(sc, v, "nn") + dot(qf, st, "nt")
    kd = k * jnp.exp(lend - lc)
    st_new = st * jnp.exp(lend) + dot(v, kd, "tn")
    return o, st_new


def _gla_block(q, k, v, la, st, nc, dot, cdl):
    consts = _gla_consts()
    outs = []
    for c in range(nc):
        sl = slice(c * CH, (c + 1) * CH)
        o, st = _gla_chunk(q[sl], k[sl], v[sl], la[sl], st, consts, dot, cdl)
        outs.append(o)
    return jnp.concatenate(outs, axis=0), st


def _gla_specs(rows, rev, nb):
    t = (lambda j: nb - 1 - j) if rev else (lambda j: j)
    q = pl.BlockSpec((rows, GLA_DK), lambda h, j: (t(j), h))
    k = pl.BlockSpec((rows, GLA_DK), lambda h, j: (t(j), GLA_H + h))
    v = pl.BlockSpec((rows, GLA_DV), lambda h, j: (t(j), GLA_H + h))
    la = pl.BlockSpec((rows, GLA_DK), lambda h, j: (t(j), h))
    ss = pl.BlockSpec((None, None, GLA_DV, GLA_DK), lambda h, j: (t(j), h, 0, 0))
    o = pl.BlockSpec((rows, GLA_DV), lambda h, j: (t(j), h))
    return q, k, v, la, ss, o


def gla_scan_fwd(proj, la, *, nc, name):
    T = proj.shape[0]
    rows = min(T, nc * CH)
    nc = rows // CH
    nb = T // rows
    q_s, k_s, v_s, la_s, ss_s, o_s = _gla_specs(rows, False, nb)

    def body(q_ref, k_ref, v_ref, la_ref, o_ref, ss_ref, st_ref):
        @pl.when(pl.program_id(1) == 0)
        def _():
            st_ref[...] = jnp.zeros_like(st_ref)

        st = st_ref[...]
        ss_ref[...] = st
        o, st = _gla_block(q_ref[...], k_ref[...], v_ref[...], la_ref[...], st, nc, _dot_raw, lambda c, ct, x: _hdot(c, x))
        o_ref[...] = o
        st_ref[...] = st

    return pl.pallas_call(
        body, name=name, grid=(GLA_H, nb), in_specs=[q_s, k_s, v_s, la_s], out_specs=[o_s, ss_s],
        out_shape=[jax.ShapeDtypeStruct((T, GLA_VD), F32), jax.ShapeDtypeStruct((nb, GLA_H, GLA_DV, GLA_DK), F32)],
        scratch_shapes=[pltpu.VMEM((GLA_DV, GLA_DK), F32)], compiler_params=_cparams(2),
    )(proj, proj, proj, la)


def gla_scan_bwd(proj, la, ss, do, *, nc, name):
    T = proj.shape[0]
    rows = min(T, nc * CH)
    nc = rows // CH
    nb = T // rows
    q_s, k_s, v_s, la_s, ss_s, o_s = _gla_specs(rows, True, nb)
    t = lambda j: nb - 1 - j
    dqk_s = pl.BlockSpec((rows, GLA_DK), lambda h, j: (t(j), h))

    def body(q_ref, k_ref, v_ref, la_ref, ss_ref, do_ref, dq_ref, dk_ref, dv_ref, dla_ref, dst_ref):
        @pl.when(pl.program_id(1) == 0)
        def _():
            dst_ref[...] = jnp.zeros_like(dst_ref)

        fn = lambda q, k, v, la_, st: _gla_block(q, k, v, la_, st, nc, bdot, cdot_left)
        _, vjp = jax.vjp(fn, q_ref[...], k_ref[...], v_ref[...], la_ref[...], ss_ref[...])
        dq, dk, dv, dla, dst = vjp((do_ref[...], dst_ref[...]))
        dq_ref[...] = dq.astype(dq_ref.dtype)
        dk_ref[...] = dk.astype(dk_ref.dtype)
        dv_ref[...] = dv.astype(dv_ref.dtype)
        dla_ref[...] = dla
        dst_ref[...] = dst

    return pl.pallas_call(
        body, name=name, grid=(GLA_H, nb), in_specs=[q_s, k_s, v_s, la_s, ss_s, o_s],
        out_specs=[dqk_s, dqk_s, o_s, dqk_s],
        out_shape=[jax.ShapeDtypeStruct((T, GLA_QK), BF16), jax.ShapeDtypeStruct((T, GLA_QK), BF16),
                   jax.ShapeDtypeStruct((T, GLA_VD), BF16), jax.ShapeDtypeStruct((T, GLA_QK), F32)],
        scratch_shapes=[pltpu.VMEM((GLA_DV, GLA_DK), F32)], compiler_params=_cparams(2),
    )(proj, proj, proj, la, ss, do)


_CONV_W = 512
_CONV_OFF = SSD_DI // _CONV_W


def _conv_pre(x, prev8, w_ref, b_ref):
    bt = x.shape[0]
    ext = jnp.concatenate([prev8, x], axis=0)
    shifted = []
    for j in range(SSD_K):
        s = SSD_K - 1 - j
        shifted.append(x if s == 0 else pltpu.roll(ext, s, 0)[8:8 + bt])
    pre = b_ref[...] + sum(w_ref[j:j + 1, :] * shifted[j] for j in range(SSD_K))
    return pre, shifted


def ssd_conv_fwd(proj, w, b, *, name):
    T = proj.shape[0]
    bt = min(T, 512)
    nb = T // bt

    def body(x_ref, w_ref, b_ref, o_ref, carry_ref):
        @pl.when(pl.program_id(1) == 0)
        def _():
            carry_ref[...] = jnp.zeros_like(carry_ref)

        x = x_ref[...]
        pre, _ = _conv_pre(x, carry_ref[...], w_ref, b_ref)
        o_ref[...] = _silu(pre)
        carry_ref[...] = x[bt - 8:, :]

    return pl.pallas_call(
        body, name=name, grid=(SSD_CONV // _CONV_W, nb),
        in_specs=[pl.BlockSpec((bt, _CONV_W), lambda c, t: (t, _CONV_OFF + c)),
                  pl.BlockSpec((SSD_K, _CONV_W), lambda c, t: (0, c)),
                  pl.BlockSpec((1, _CONV_W), lambda c, t: (0, c))],
        out_specs=pl.BlockSpec((bt, _CONV_W), lambda c, t: (t, c)),
        out_shape=jax.ShapeDtypeStruct((T, SSD_CONV), F32),
        scratch_shapes=[pltpu.VMEM((8, _CONV_W), F32)], compiler_params=_cparams(2),
    )(proj, w, b)


def ssd_conv_bwd(proj, w, b, dout, *, name):
    T = proj.shape[0]
    bt = min(T, 512)
    nb = T // bt
    r8 = bt // 8

    def body(x_ref, xp_ref, w_ref, b_ref, do_ref, dx_ref, dw_ref, db_ref, carry_ref):
        t = pl.program_id(1)

        @pl.when(t == 0)
        def _():
            carry_ref[...] = jnp.zeros_like(carry_ref)
            dw_ref[...] = jnp.zeros_like(dw_ref)
            db_ref[...] = jnp.zeros_like(db_ref)

        x = x_ref[...]
        prev8 = jnp.where(t == nb - 1, 0.0, xp_ref[...])
        pre, shifted = _conv_pre(x, prev8, w_ref, b_ref)
        sg = _sigmoid(pre)
        dpre = do_ref[...] * (sg * (1.0 + pre * (1.0 - sg)))
        ext = jnp.concatenate([dpre, carry_ref[...]], axis=0)
        dx = w_ref[SSD_K - 1:SSD_K, :] * dpre
        for j in range(SSD_K - 1):
            s = SSD_K - 1 - j
            dx = dx + w_ref[j:j + 1, :] * pltpu.roll(ext, bt + 8 - s, 0)[:bt]
        dx_ref[...] = dx.astype(dx_ref.dtype)
        dw_ref[...] += jnp.concatenate([jnp.sum(dpre * shifted[j], axis=0, keepdims=True) for j in range(SSD_K)], axis=0)
        db_ref[...] += jnp.sum(dpre, axis=0, keepdims=True)
        carry_ref[...] = dpre[:8, :]

    rt = lambda t: nb - 1 - t
    return pl.pallas_call(
        body, name=name, grid=(SSD_CONV // _CONV_W, nb),
        in_specs=[pl.BlockSpec((bt, _CONV_W), lambda c, t: (rt(t), _CONV_OFF + c)),
                  pl.BlockSpec((8, _CONV_W), lambda c, t: (jnp.maximum(rt(t) * r8 - 1, 0), _CONV_OFF + c)),
                  pl.BlockSpec((SSD_K, _CONV_W), lambda c, t: (0, c)),
                  pl.BlockSpec((1, _CONV_W), lambda c, t: (0, c)),
                  pl.BlockSpec((bt, _CONV_W), lambda c, t: (rt(t), c))],
        out_specs=[pl.BlockSpec((bt, _CONV_W), lambda c, t: (rt(t), c)),
                   pl.BlockSpec((SSD_K, _CONV_W), lambda c, t: (0, c)),
                   pl.BlockSpec((1, _CONV_W), lambda c, t: (0, c))],
        out_shape=[jax.ShapeDtypeStruct((T, SSD_CONV), BF16), jax.ShapeDtypeStruct((SSD_K, SSD_CONV), F32),
                   jax.ShapeDtypeStruct((1, SSD_CONV), F32)],
        scratch_shapes=[pltpu.VMEM((8, _CONV_W), F32)], compiler_params=_cparams(2),
    )(proj, proj, w, b, dout)


_SSD_U = 2 * CH


def _ssd_unit(xs, bm, cm, dtraw, dtb, alog, dsk, hp, g, dot, cdl, cdr):
    U = _SSD_U
    r, c = _iota2((U, U), 0), _iota2((U, U), 1)
    same = (r // CH) == (c // CH)
    Lb = (same & (r >= c)).astype(F32)
    Ub = (same & (r <= c)).astype(F32)
    lane = _iota2((1, U), 1)
    sub = _iota2((U, 1), 0)
    lo_lane = _iota2((1, 2 * SSD_HD), 1) < SSD_HD
    lo_sub = _iota2((2 * SSD_HD, 1), 0) < SSD_HD

    dt = _softplus(dtraw + dtb)
    da = dt * (-jnp.exp(alog))
    cum = cdl(Lb, Ub, da)
    cum_t = cdr(da.T, Ub, Lb)
    dt_t = dt.T
    cb = dot(cm, bm, "nt")
    ys = []
    new_hp = []
    for pr in range(2):
        xs_p = xs[:, pr * 2 * SSD_HD:(pr + 1) * 2 * SSD_HD]
        cols, dts, dks, y_in = [], [], [], []
        for jj in range(2):
            hd = g * (SSD_H // SSD_G) + 2 * pr + jj
            oh_l = (lane == hd).astype(F32)
            oh_s = (sub == hd).astype(F32)
            col = jnp.sum(cum * oh_l, axis=1, keepdims=True)
            row = jnp.sum(cum_t * oh_s, axis=0, keepdims=True)
            dtrow = jnp.sum(dt_t * oh_s, axis=0, keepdims=True)
            decay = jnp.exp(-jnp.abs(col - row))
            mix = jnp.where(same, cb * decay * dtrow, 0.0)
            y_in.append(dot(mix, xs_p, "nn"))
            cols.append(col)
            dts.append(jnp.sum(dt * oh_l, axis=1, keepdims=True))
            dks.append(jnp.sum(dsk * oh_l, axis=1, keepdims=True))
        y_intra = jnp.where(lo_lane, y_in[0], y_in[1])
        dsk_p = jnp.where(lo_lane, dks[0], dks[1])
        h = hp[pr]
        yc = []
        for ci in range(2):
            sl = slice(ci * CH, (ci + 1) * CH)
            ce = [cols[jj][ci * CH + CH - 1:ci * CH + CH, :] for jj in range(2)]
            ecum = jnp.where(lo_lane, jnp.exp(cols[0][sl]), jnp.exp(cols[1][sl]))
            y_inter = dot(cm[sl], h, "nt") * ecum
            wgt = jnp.where(lo_lane, dts[0][sl] * jnp.exp(ce[0] - cols[0][sl]), dts[1][sl] * jnp.exp(ce[1] - cols[1][sl]))
            xw = xs_p[sl] * wgt
            a_p = jnp.where(lo_sub, jnp.exp(ce[0]), jnp.exp(ce[1]))
            h = a_p * h + dot(xw, bm[sl], "tn")
            yc.append(y_intra[sl] + y_inter + dsk_p * xs_p[sl])
        ys.append(jnp.concatenate(yc, axis=0))
        new_hp.append(h)
    return jnp.concatenate(ys, axis=1), tuple(new_hp)


def _ssd_block(xs, bm, cm, dtraw, dtb, alog, dsk, hp, g, nu, dot, cdl, cdr):
    outs = []
    for u in range(nu):
        sl = slice(u * _SSD_U, (u + 1) * _SSD_U)
        y, hp = _ssd_unit(xs[sl], bm[sl], cm[sl], dtraw[sl], dtb, alog, dsk, hp, g, dot, cdl, cdr)
        outs.append(y)
    return jnp.concatenate(outs, axis=0), hp


def _ssd_specs(rows, rev, nb):
    t = (lambda j: nb - 1 - j) if rev else (lambda j: j)
    gw = SSD_DI // SSD_G
    xs = pl.BlockSpec((rows, gw), lambda j, g: (t(j), g))
    bm = pl.BlockSpec((rows, SSD_N), lambda j, g: (t(j), SSD_DI // SSD_N + g))
    cm = pl.BlockSpec((rows, SSD_N), lambda j, g: (t(j), (SSD_DI + SSD_GN) // SSD_N + g))
    dtr = pl.BlockSpec((rows, LANE), lambda j, g: (t(j), (SSD_DI + SSD_CONV) // LANE))
    par = pl.BlockSpec((1, LANE), lambda j, g: (0, 0))
    hs = pl.BlockSpec((None, None, 2, 2 * SSD_HD, SSD_N), lambda j, g: (t(j), g, 0, 0, 0))
    y = pl.BlockSpec((rows, gw), lambda j, g: (t(j), g))
    return xs, bm, cm, dtr, par, hs, y


def ssd_scan_fwd(xbc, proj, dtb, alog, dsk, *, nu, name):
    T = xbc.shape[0]
    rows = min(T, nu * _SSD_U)
    nu = rows // _SSD_U
    nb = T // rows
    xs_s, bm_s, cm_s, dt_s, par_s, hs_s, y_s = _ssd_specs(rows, False, nb)

    def body(xs_ref, bm_ref, cm_ref, dt_ref, dtb_ref, al_ref, dsk_ref, y_ref, hs_ref, h_ref):
        g = pl.program_id(1)

        @pl.when(pl.program_id(0) == 0)
        def _():
            h_ref[g] = jnp.zeros(h_ref.shape[1:], F32)

        hs_ref[...] = h_ref[g]
        hp = (h_ref[g, 0], h_ref[g, 1])
        y, hp = _ssd_block(xs_ref[...], bm_ref[...], cm_ref[...], dt_ref[...], dtb_ref[...], al_ref[...], dsk_ref[...],
                           hp, g, nu, _dot_raw, lambda c, ct, x: _hdot(c, x), lambda x, c, ct: _hdot(x, c))
        y_ref[...] = y
        h_ref[g, 0] = hp[0]
        h_ref[g, 1] = hp[1]

    return pl.pallas_call(
        body, name=name, grid=(nb, SSD_G), in_specs=[xs_s, bm_s, cm_s, dt_s, par_s, par_s, par_s],
        out_specs=[y_s, hs_s],
        out_shape=[jax.ShapeDtypeStruct((T, SSD_DI), F32), jax.ShapeDtypeStruct((nb, SSD_G, 2, 2 * SSD_HD, SSD_N), F32)],
        scratch_shapes=[pltpu.VMEM((SSD_G, 2, 2 * SSD_HD, SSD_N), F32)], compiler_params=_cparams(2),
    )(xbc, xbc, xbc, proj, dtb, alog, dsk)


def ssd_scan_bwd(xbc, proj, dtb, alog, dsk, hs, dy, *, nu, name):
    T = xbc.shape[0]
    rows = min(T, nu * _SSD_U)
    nu = rows // _SSD_U
    nb = T // rows
    xs_s, bm_s, cm_s, dt_s, par_s, hs_s, y_s = _ssd_specs(rows, True, nb)
    t = lambda j: nb - 1 - j
    n_s = pl.BlockSpec((rows, SSD_N), lambda j, g: (t(j), g))
    ddt_s = pl.BlockSpec((rows, LANE), lambda j, g: (t(j), 0))

    def body(xs_ref, bm_ref, cm_ref, dt_ref, dtb_ref, al_ref, dsk_ref, hs_ref, dy_ref,
             dxs_ref, dbm_ref, dcm_ref, ddt_ref, ddtb_ref, dal_ref, ddsk_ref, dh_ref):
        j, g = pl.program_id(0), pl.program_id(1)

        @pl.when(j == 0)
        def _():
            dh_ref[g] = jnp.zeros(dh_ref.shape[1:], F32)

        @pl.when((j == 0) & (g == 0))
        def _():
            ddtb_ref[...] = jnp.zeros_like(ddtb_ref)
            dal_ref[...] = jnp.zeros_like(dal_ref)
            ddsk_ref[...] = jnp.zeros_like(ddsk_ref)

        @pl.when(g == 0)
        def _():
            ddt_ref[...] = jnp.zeros_like(ddt_ref)

        fn = lambda xs, bm, cm, dtr, dtb_, al, dsk_, h0, h1: _ssd_block(
            xs, bm, cm, dtr, dtb_, al, dsk_, (h0, h1), g, nu, bdot, cdot_left, cdot_right)
        _, vjp = jax.vjp(fn, xs_ref[...], bm_ref[...], cm_ref[...], dt_ref[...], dtb_ref[...], al_ref[...], dsk_ref[...],
                         hs_ref[0], hs_ref[1])
        dxs, dbm, dcm, ddt, ddtb, dal, ddsk, dh0, dh1 = vjp((dy_ref[...], (dh_ref[g, 0], dh_ref[g, 1])))
        dxs_ref[...] = dxs
        dbm_ref[...] = dbm
        dcm_ref[...] = dcm
        ddt_ref[...] += ddt
        ddtb_ref[...] += ddtb
        dal_ref[...] += dal
        ddsk_ref[...] += ddsk
        dh_ref[g, 0] = dh0
        dh_ref[g, 1] = dh1

    return pl.pallas_call(
        body, name=name, grid=(nb, SSD_G), in_specs=[xs_s, bm_s, cm_s, dt_s, par_s, par_s, par_s, hs_s, y_s],
        out_specs=[y_s, n_s, n_s, ddt_s, par_s, par_s, par_s],
        out_shape=[jax.ShapeDtypeStruct((T, SSD_DI), F32), jax.ShapeDtypeStruct((T, SSD_GN), F32),
                   jax.ShapeDtypeStruct((T, SSD_GN), F32), jax.ShapeDtypeStruct((T, LANE), F32),
                   jax.ShapeDtypeStruct((1, LANE), F32), jax.ShapeDtypeStruct((1, LANE), F32),
                   jax.ShapeDtypeStruct((1, LANE), F32)],
        scratch_shapes=[pltpu.VMEM((SSD_G, 2, 2 * SSD_HD, SSD_N), F32)], compiler_params=_cparams(2),
    )(xbc, xbc, xbc, proj, dtb, alog, dsk, hs, dy)


def _s5_param_f(log_dt, a_re, a_im, bre_t, bim_t, cim, cdl):
    n = S5_NG * S5_GS
    r, c = _iota2((n, S5_NG), 0), _iota2((n, S5_NG), 1)
    E = ((r // S5_GS) == c).astype(F32)
    rt, ct = _iota2((S5_NG, n), 0), _iota2((S5_NG, n), 1)
    Et = ((ct // S5_GS) == rt).astype(F32)
    step = jnp.exp(log_dt)
    mag = jnp.exp(step * a_re)
    abr = mag * jnp.cos(step * a_im)
    abi = mag * jnp.sin(step * a_im)
    den = a_re * a_re + a_im * a_im
    nr, ni = abr - 1.0, abi
    fr = (nr * a_re + ni * a_im) / den
    fi = (ni * a_re - nr * a_im) / den
    Fr, Fi = cdl(E, Et, fr), cdl(E, Et, fi)
    bbr = Fr * bre_t - Fi * bim_t
    bbi = Fr * bim_t + Fi * bre_t
    return abr, abi, bbr, bbi, -cim


def _whole(a):
    return pl.BlockSpec(a.shape, lambda: (0,) * a.ndim)


def s5_param_fwd(args, *, name):
    def body(*refs):
        res = _s5_param_f(*[r[...] for r in refs[:6]], lambda c, ct, x: _hdot(c, x))
        for o, v in zip(refs[6:], res):
            o[...] = v

    shapes = [(S5_NG, S5_P), (S5_NG, S5_P)] + [(S5_NG * S5_GS, S5_P)] * 3
    return pl.pallas_call(
        body, name=name, in_specs=[_whole(a) for a in args], out_specs=[pl.BlockSpec(s, lambda: (0, 0)) for s in shapes],
        out_shape=[jax.ShapeDtypeStruct(s, F32) for s in shapes],
        compiler_params=pltpu.CompilerParams(vmem_limit_bytes=VMEM_LIMIT),
    )(*args)


def s5_param_bwd(args, cts, *, name):
    def body(*refs):
        fn = lambda *a: _s5_param_f(*a, cdot_left)
        _, vjp = jax.vjp(fn, *[r[...] for r in refs[:6]])
        grads = vjp(tuple(r[...] for r in refs[6:11]))
        for o, v in zip(refs[11:], grads):
            o[...] = v

    return pl.pallas_call(
        body, name=name, in_specs=[_whole(a) for a in list(args) + list(cts)],
        out_specs=[_whole(a) for a in args], out_shape=[jax.ShapeDtypeStruct(a.shape, F32) for a in args],
        compiler_params=pltpu.CompilerParams(vmem_limit_bytes=VMEM_LIMIT),
    )(*args, *cts)


_S5_W = S5_BLK * S5_P


def _cmul_add(xr, xi, pr, pi, sr, si):
    return xr + (pr * sr - pi * si), xi + (pr * si + pi * sr)


def _s5_powers(ar, ai):
    pw = [(ar, ai)]
    for _ in range(7):
        qr, qi = pw[-1]
        pw.append((qr * ar - qi * ai, qr * ai + qi * ar))
    return pw


def s5_scan_fwd(bu, a_re, a_im, *, name):
    T = bu.shape[0]
    bt = min(T, 256)
    nb = T // bt

    def body(bu_ref, ar_ref, ai_ref, x_ref, carry_ref):
        @pl.when(pl.program_id(1) == 0)
        def _():
            carry_ref[...] = jnp.zeros_like(carry_ref)

        ar, ai = ar_ref[...], ai_ref[...]
        pw = _s5_powers(ar, ai)
        pwr = jnp.concatenate([p[0] for p in pw], axis=0)
        pwi = jnp.concatenate([p[1] for p in pw], axis=0)
        rin = _iota2((8, _S5_W), 0)
        cr, ci = carry_ref[0:1, :], carry_ref[1:2, :]
        for t in range(bt // 8):
            sl = slice(8 * t, 8 * t + 8)
            xr, xi = bu_ref[sl, :_S5_W], bu_ref[sl, _S5_W:]
            for s in (1, 2, 4):
                m = rin >= s
                sr = jnp.where(m, pltpu.roll(xr, s, 0), 0.0)
                si = jnp.where(m, pltpu.roll(xi, s, 0), 0.0)
                xr, xi = _cmul_add(xr, xi, *pw[s - 1], sr, si)
            xr, xi = _cmul_add(xr, xi, pwr, pwi, cr, ci)
            x_ref[sl, :_S5_W] = xr
            x_ref[sl, _S5_W:] = xi
            cr, ci = xr[7:8, :], xi[7:8, :]
        carry_ref[0:1, :] = cr
        carry_ref[1:2, :] = ci

    blk = pl.BlockSpec((bt, 2 * _S5_W), lambda g, t: (t, g))
    a_s = pl.BlockSpec((None, 1, _S5_W), lambda g, t: (g, 0, 0))
    return pl.pallas_call(
        body, name=name, grid=(S5_NG // S5_BLK, nb), in_specs=[blk, a_s, a_s], out_specs=blk,
        out_shape=jax.ShapeDtypeStruct(bu.shape, F32), scratch_shapes=[pltpu.VMEM((8, _S5_W), F32)],
        compiler_params=_cparams(2),
    )(bu, a_re, a_im)


def s5_scan_bwd(dx, x, a_re, a_im, *, name):
    T = dx.shape[0]
    bt = min(T, 256)
    nb = T // bt

    def body(g_ref, x_ref, ar_ref, ai_ref, lam_ref, dar_ref, dai_ref, carry_ref):
        @pl.when(pl.program_id(1) == 0)
        def _():
            carry_ref[...] = jnp.zeros_like(carry_ref)
            dar_ref[...] = jnp.zeros_like(dar_ref)
            dai_ref[...] = jnp.zeros_like(dai_ref)

        pw = _s5_powers(ar_ref[...], -ai_ref[...])
        pwr = jnp.concatenate([p[0] for p in reversed(pw)], axis=0)
        pwi = jnp.concatenate([p[1] for p in reversed(pw)], axis=0)
        rin = _iota2((8, _S5_W), 0)
        cr, ci = carry_ref[0:1, :], carry_ref[1:2, :]
        acc_r = jnp.zeros((8, _S5_W), F32)
        acc_i = jnp.zeros((8, _S5_W), F32)
        for t in reversed(range(bt // 8)):
            sl = slice(8 * t, 8 * t + 8)
            lr, li = g_ref[sl, :_S5_W], g_ref[sl, _S5_W:]
            for s in (1, 2, 4):
                m = rin < 8 - s
                sr = jnp.where(m, pltpu.roll(lr, 8 - s, 0), 0.0)
                si = jnp.where(m, pltpu.roll(li, 8 - s, 0), 0.0)
                lr, li = _cmul_add(lr, li, *pw[s - 1], sr, si)
            lr, li = _cmul_add(lr, li, pwr, pwi, cr, ci)
            lam_ref[sl, :_S5_W] = lr
            lam_ref[sl, _S5_W:] = li
            nr = jnp.where(rin == 7, cr, pltpu.roll(lr, 7, 0))
            ni = jnp.where(rin == 7, ci, pltpu.roll(li, 7, 0))
            xr, xi = x_ref[sl, :_S5_W], x_ref[sl, _S5_W:]
            acc_r = acc_r + (xr * nr + xi * ni)
            acc_i = acc_i + (xr * ni - xi * nr)
            cr, ci = lr[0:1, :], li[0:1, :]
        carry_ref[0:1, :] = cr
        carry_ref[1:2, :] = ci
        dar_ref[...] += jnp.sum(acc_r, axis=0, keepdims=True)
        dai_ref[...] += jnp.sum(acc_i, axis=0, keepdims=True)

    blk = pl.BlockSpec((bt, 2 * _S5_W), lambda g, t: (nb - 1 - t, g))
    a_s = pl.BlockSpec((None, 1, _S5_W), lambda g, t: (g, 0, 0))
    nblk = S5_NG // S5_BLK
    return pl.pallas_call(
        body, name=name, grid=(nblk, nb), in_specs=[blk, blk, a_s, a_s], out_specs=[blk, a_s, a_s],
        out_shape=[jax.ShapeDtypeStruct(dx.shape, F32), jax.ShapeDtypeStruct((nblk, 1, _S5_W), F32),
                   jax.ShapeDtypeStruct((nblk, 1, _S5_W), F32)],
        scratch_shapes=[pltpu.VMEM((8, _S5_W), F32)], compiler_params=_cparams(2),
    )(dx, x, a_re, a_im)


def _norm_bf16(h, g, name):
    return rowwise(f_rmsnorm, [g], [h], [(D, BF16)], bt=512, name=name)[0]


def _norm_bwd(h, g, cts, name):
    n = len(cts) - 1

    def f(p, r):
        y = _rms(r[0], p[0])
        return (y,) * n + (r[0],)

    (dg,), (dh,) = rowwise_vjp(f, [g], [h], cts, [F32], bt=256, name=name)
    return dh, dg


def ffn_fwd(h, g, w_gu, w_down, tag):
    hn = _norm_bf16(h, g, f"{tag}_norm")
    a, gu = ffn_up(hn, w_gu, name=f"{tag}_up")
    h2 = matmul(a, w_down[None], "nn", add=h, name=f"{tag}_down")
    return h2, (h, hn, gu, a)


def ffn_bwd(d, saved, g, w_gu, w_down, tag):
    h, hn, gu, a = saved
    dgu = ffn_dact(d, w_down, gu, name=f"{tag}_dact")
    dwd = matmul(a, d, "tn", name=f"{tag}_dwd")[0]
    dwgu = matmul(hn, dgu, "tn", name=f"{tag}_dwgu")[0]
    dhn = matmul(dgu, w_gu[None], "nt", name=f"{tag}_dhn")
    dh, dg = _norm_bwd(h, g, [dhn, d], f"{tag}_dnorm")
    return dh, dg, dwgu, dwd


_GLA_NC = 4
_SSD_NU = 4


def gla_fwd(h, gm, w_in, w_a2, b_a, ng, w_out, tag):
    hn = _norm_bf16(h, gm, f"{tag}_norm")
    proj = matmul(hn, w_in[None], "nn", name=f"{tag}_in")
    alow = (proj, LANE, 2 * (GLA_QK + GLA_VD) // LANE)
    la = rowwise(f_gla_gate_in_fwd, [w_a2, b_a], [alow], [(GLA_QK, F32)], bt=512, name=f"{tag}_gate")[0]
    o, ss = gla_scan_fwd(proj, la, nc=_GLA_NC, name=f"{tag}_scan")
    r = (proj, GLA_VD, 2)
    og = rowwise(f_gla_out, [ng], [o, r], [(GLA_VD, BF16)], bt=256, name=f"{tag}_out")[0]
    h2 = matmul(og, w_out[None], "nn", add=h, name=f"{tag}_proj")
    return h2, (h, hn, proj, la, o, ss, og)


def gla_bwd(d, saved, gm, w_in, w_a2, b_a, ng, w_out, tag):
    h, hn, proj, la, o, ss, og = saved
    dog = matmul(d, w_out[None], "nt", name=f"{tag}_dog")
    dwout = matmul(og, d, "tn", name=f"{tag}_dwout")[0]
    r = (proj, GLA_VD, 2)
    (dng,), (do, dr) = rowwise_vjp(f_gla_out, [ng], [o, r], [dog], [F32, BF16], bt=128, name=f"{tag}_dout")
    dq, dk, dv, dla = gla_scan_bwd(proj, la, ss, do, nc=_GLA_NC, name=f"{tag}_dscan")
    alow = (proj, LANE, 2 * (GLA_QK + GLA_VD) // LANE)
    (dwa2, dba), (dalow,) = rowwise_vjp(f_gla_gate_in, [w_a2, b_a], [alow], [dla], [BF16], bt=512, name=f"{tag}_dgate")
    dproj = jnp.concatenate([dq, dk, dv, dr, dalow], axis=1)
    dwin = matmul(hn, dproj, "tn", name=f"{tag}_dwin")[0]
    dhn = matmul(dproj, w_in[None], "nt", name=f"{tag}_dhn")
    dh, dgm = _norm_bwd(h, gm, [dhn, d], f"{tag}_dnorm")
    return dh, dgm, dwin, dwa2[:GLA_RANK], dba, dng, dwout


def ssd_fwd(h, gm, w_in, conv_w, conv_b, dtb, alog, dsk, ng, w_out, tag):
    hn = _norm_bf16(h, gm, f"{tag}_norm")
    proj = matmul(hn, w_in[None], "nn", name=f"{tag}_in")
    xbc = ssd_conv_fwd(proj, conv_w, conv_b, name=f"{tag}_conv")
    y, hs = ssd_scan_fwd(xbc, proj, dtb, alog, dsk, nu=_SSD_NU, name=f"{tag}_scan")
    z = (proj, SSD_DI, 0)
    yg = rowwise(f_ssd_out, [ng], [y, z], [(SSD_DI, BF16)], bt=256, name=f"{tag}_out")[0]
    h2 = matmul(yg, w_out[None], "nn", add=h, name=f"{tag}_proj")
    return h2, (h, hn, proj, xbc, y, hs, yg)


def ssd_bwd(d, saved, gm, w_in, conv_w, conv_b, dtb, alog, dsk, ng, w_out, tag):
    h, hn, proj, xbc, y, hs, yg = saved
    dyg = matmul(d, w_out[None], "nt", name=f"{tag}_dyg")
    dwout = matmul(yg, d, "tn", name=f"{tag}_dwout")[0]
    z = (proj, SSD_DI, 0)
    (dng,), (dy, dz) = rowwise_vjp(f_ssd_out, [ng], [y, z], [dyg], [F32, BF16], bt=128, name=f"{tag}_dout")
    dxs, dbm, dcm, ddt, ddtb, dal, ddsk = ssd_scan_bwd(xbc, proj, dtb, alog, dsk, hs, dy, nu=_SSD_NU, name=f"{tag}_dscan")
    dxbc = jnp.concatenate([dxs, dbm, dcm], axis=1)
    dpre, dcw, dcb = ssd_conv_bwd(proj, conv_w, conv_b, dxbc, name=f"{tag}_dconv")
    dproj = jnp.concatenate([dz, dpre, ddt.astype(BF16)], axis=1)
    dwin = matmul(hn, dproj, "tn", name=f"{tag}_dwin")[0]
    dhn = matmul(dproj, w_in[None], "nt", name=f"{tag}_dhn")
    dh, dgm = _norm_bwd(h, gm, [dhn, d], f"{tag}_dnorm")
    return (dh, dgm, dwin, dcw, dcb, ddtb[:, :SSD_H], dal[:, :SSD_H], ddsk[:, :SSD_H], dng, dwout)


_S5_NB = S5_NG // S5_BLK


def _s5_param_args(log_dt, a_re, a_im, b_re, b_im, c_im):
    n = S5_NG * S5_GS
    tr = lambda b: jnp.transpose(b, (0, 2, 1)).reshape(n, S5_P)
    return [log_dt.reshape(S5_NG, 1), a_re, a_im, tr(b_re), tr(b_im), c_im.reshape(n, S5_P)]


def _s5_blockdiag(t):
    nb, gl, a, b = t.shape
    eye = jnp.eye(gl, dtype=t.dtype)
    return (t[:, :, :, None, :] * eye[None, :, None, :, None]).reshape(nb, gl * a, gl * b)


def _s5_diag(t, a, b):
    nb = t.shape[0]
    gl = t.shape[1] // a
    eye = jnp.eye(gl, dtype=t.dtype)
    return jnp.sum(t.reshape(nb, gl, a, gl, b) * eye[None, :, None, :, None], axis=3)


def _s5_weights(bbr, bbi, c_re, cneg):
    sh = (_S5_NB, S5_BLK, S5_GS, S5_P)
    wb = jnp.concatenate([_s5_blockdiag(bbr.reshape(sh)), _s5_blockdiag(bbi.reshape(sh))], axis=2)
    tr = lambda cc: jnp.transpose(cc.reshape(sh), (0, 1, 3, 2))
    wc = jnp.concatenate([_s5_blockdiag(tr(c_re)), _s5_blockdiag(tr(cneg))], axis=1)
    return wb, wc


def s5_fwd(h, gm, prm, dsk, w_glu, tag):
    log_dt, a_re, a_im, b_re, b_im, c_re, c_im = prm
    hn = rowwise(f_rmsnorm, [gm], [h], [(D, F32)], bt=512, name=f"{tag}_norm")[0]
    pargs = _s5_param_args(log_dt, a_re, a_im, b_re, b_im, c_im)
    abr, abi, bbr, bbi, cneg = s5_param_fwd(pargs, name=f"{tag}_param")
    wb, wc = _s5_weights(bbr, bbi, c_re.reshape(S5_NG * S5_GS, S5_P), cneg)
    ar, ai = abr.reshape(_S5_NB, 1, _S5_W), abi.reshape(_S5_NB, 1, _S5_W)
    bu = matmul(hn, wb, "nn", G=_S5_NB, name=f"{tag}_bu")
    x = s5_scan_fwd(bu, ar, ai, name=f"{tag}_scan")
    ycp = matmul(x, wc, "nn", G=_S5_NB, name=f"{tag}_cx")
    yg = rowwise(f_s5_act, [dsk], [ycp, hn], [(D, BF16)], bt=512, name=f"{tag}_act")[0]
    vg = matmul(yg, w_glu[None], "nn", name=f"{tag}_glu")
    h2 = rowwise(f_glu_res, [], [vg, h], [(D, F32)], bt=512, name=f"{tag}_out")[0]
    return h2, (h, hn, pargs, wb, wc, ar, ai, x, ycp, yg, vg)


def s5_bwd(d, saved, gm, dsk, w_glu, tag):
    h, hn, pargs, wb, wc, ar, ai, x, ycp, yg, vg = saved
    _, (dvg,) = rowwise_vjp(f_glu, [], [vg], [d], [BF16], bt=256, name=f"{tag}_dout")
    dwglu = matmul(yg, dvg, "tn", name=f"{tag}_dwglu")[0]
    dyg = matmul(dvg, w_glu[None], "nt", name=f"{tag}_dyg")
    (ddsk,), (dycp, dhn1) = rowwise_vjp(f_s5_act, [dsk], [ycp, hn], [dyg], [F32, F32], bt=256, name=f"{tag}_dact")
    dx = matmul(dycp, wc, "nt", G=_S5_NB, name=f"{tag}_dx")
    dwc = matmul(x, dycp, "tn", G=_S5_NB, name=f"{tag}_dwc")
    lam, dar, dai = s5_scan_bwd(dx, x, ar, ai, name=f"{tag}_dscan")
    dwb = matmul(hn, lam, "tn", G=_S5_NB, name=f"{tag}_dwb")
    dhn2 = matmul(lam, wb, "nt", G=_S5_NB, name=f"{tag}_dhn")
    dh, dgm = _norm_bwd(h, gm, [dhn1, dhn2, d], f"{tag}_dnorm")
    n = S5_NG * S5_GS
    half = S5_BLK * S5_P
    d_bbr = _s5_diag(dwb[:, :, :half], S5_GS, S5_P).reshape(n, S5_P)
    d_bbi = _s5_diag(dwb[:, :, half:], S5_GS, S5_P).reshape(n, S5_P)
    from_c = lambda t: jnp.transpose(_s5_diag(t, S5_P, S5_GS), (0, 1, 3, 2)).reshape(n, S5_P)
    d_cre = from_c(dwc[:, :half, :])
    d_cneg = from_c(dwc[:, half:, :])
    cts = [dar.reshape(S5_NG, S5_P), dai.reshape(S5_NG, S5_P), d_bbr, d_bbi, d_cneg]
    dlog, dare, daim, dbre_t, dbim_t, dcim = s5_param_bwd(pargs, cts, name=f"{tag}_dparam")
    untr = lambda t: jnp.transpose(t.reshape(S5_NG, S5_GS, S5_P), (0, 2, 1))
    grads = (dlog.reshape(S5_NG), dare, daim, untr(dbre_t), untr(dbim_t),
             d_cre.reshape(S5_NG, S5_GS, S5_P), dcim.reshape(S5_NG, S5_GS, S5_P))
    return dh, dgm, grads, ddsk, dwglu


def _pad_last(w, n):
    return jnp.pad(w, [(0, 0)] * (w.ndim - 1) + [(0, n - w.shape[-1])])


_BIG = ("gla_w_in", "gla_w_out", "ssd_w_in", "ssd_w_out", "s5_w_glu", "ffn_w_gu", "ffn_w_down")


def interleave_gu(w):
    q = w.shape[-1] // 4
    return jnp.concatenate([w[..., :q], w[..., 2 * q:3 * q], w[..., q:2 * q], w[..., 3 * q:]], axis=-1)


def local_step(x, target, W):
    f32 = lambda a: a.astype(F32)
    row = lambda a: f32(a).reshape(1, -1)
    mixers = []
    for i in range(DEPTH):
        m, j = i % 3, i // 3
        gm = row(W["norm_mix_g"][i])
        if m == 0:
            args = (gm, W["gla_w_in"][j], jnp.pad(f32(W["gla_w_a2"][j]), ((0, LANE - GLA_RANK), (0, 0))),
                    row(W["gla_b_a"][j]), row(W["gla_norm_g"][j]), W["gla_w_out"][j])
        elif m == 1:
            pl_ = lambda a: _pad_last(row(a), LANE)
            args = (gm, W["ssd_w_in"][j], f32(W["ssd_conv_w"][j]),
                    row(W["ssd_conv_b"][j]), pl_(W["ssd_dt_bias"][j]), pl_(W["ssd_a_log"][j]), pl_(W["ssd_d"][j]),
                    row(W["ssd_norm_g"][j]), W["ssd_w_out"][j])
        else:
            prm = tuple(f32(W[k][j]) for k in ("s5_log_dt", "s5_a_re", "s5_a_im", "s5_b_re", "s5_b_im", "s5_c_re", "s5_c_im"))
            args = (gm, prm, row(W["s5_d"][j]), W["s5_w_glu"][j])
        mixers.append((m, j, args))
    ffns = [(row(W["norm_ffn_g"][i]), W["ffn_w_gu"][i], W["ffn_w_down"][i]) for i in range(DEPTH)]

    h = x
    saved = []
    for i in range(DEPTH):
        m, j, args = mixers[i]
        tag = f"l{i}_{('gla', 'ssd', 's5')[m]}"
        h, sm = (gla_fwd, ssd_fwd, s5_fwd)[m](h, *args, tag)
        h, sf = ffn_fwd(h, *ffns[i], f"l{i}_ffn")
        saved.append((sm, sf))
    loss, dfg, d = loss_head(h, row(W["final_norm_g"]), target, name="loss_head")

    G = {k: [None] * len(v) for k, v in W.items() if k != "final_norm_g"}
    G["final_norm_g"] = dfg.reshape(D)
    for i in reversed(range(DEPTH)):
        m, j, args = mixers[i]
        sm, sf = saved[i]
        d, dg, dwgu, dwd = ffn_bwd(d, sf, *ffns[i], f"l{i}_ffn")
        G["norm_ffn_g"][i], G["ffn_w_gu"][i], G["ffn_w_down"][i] = dg.reshape(D), dwgu, dwd
        tag = f"l{i}_{('gla', 'ssd', 's5')[m]}"
        if m == 0:
            d, dgm, dwin, dwa2, dba, dng, dwout = gla_bwd(d, sm, *args, tag)
            G["gla_w_in"][j], G["gla_w_a2"][j], G["gla_b_a"][j] = dwin, dwa2, dba.reshape(-1)
            G["gla_norm_g"][j], G["gla_w_out"][j] = dng.reshape(-1), dwout
        elif m == 1:
            d, dgm, dwin, dcw, dcb, ddtb, dal, ddsk, dng, dwout = ssd_bwd(d, sm, *args, tag)
            G["ssd_w_in"][j], G["ssd_conv_w"][j], G["ssd_conv_b"][j] = dwin, dcw, dcb.reshape(-1)
            G["ssd_dt_bias"][j], G["ssd_a_log"][j], G["ssd_d"][j] = ddtb.reshape(-1), dal.reshape(-1), ddsk.reshape(-1)
            G["ssd_norm_g"][j], G["ssd_w_out"][j] = dng.reshape(-1), dwout
        else:
            d, dgm, pg, ddsk, dwglu = s5_bwd(d, sm, args[0], args[2], args[3], tag)
            for k, v in zip(("s5_log_dt", "s5_a_re", "s5_a_im", "s5_b_re", "s5_b_im", "s5_c_re", "s5_c_im"), pg):
                G[k][j] = v
            G["s5_d"][j], G["s5_w_glu"][j] = ddsk.reshape(-1), dwglu
        G["norm_mix_g"][i] = dgm.reshape(D)
    grads = {k: (v if k == "final_norm_g" or k in _BIG else jnp.stack(v)) for k, v in G.items()}
    return loss, d, grads


_MESH = pl.DeviceIdType.MESH
_ANY = pl.BlockSpec(memory_space=pl.ANY)
_DMA = pltpu.SemaphoreType.DMA
_ROWS_ALIGN = 1024


def _place():
    return lax.axis_index("x"), lax.axis_index("y"), lax.axis_index("c")


def _other_chips(x, y):
    return [(1 - x, y), (x, 1 - y), (1 - x, 1 - y)]


def _remote(src, dst, send_sems, recv_sems, k, to):
    return pltpu.make_async_remote_copy(src_ref=src, dst_ref=dst, send_sem=send_sems.at[k], recv_sem=recv_sems.at[k],
                                        device_id=to, device_id_type=_MESH)


def gather_shards(loc, *, name):
    def body(in_ref, out_ref, send_sems, recv_sems, local_sem):
        x, y, c = _place()
        me, sibling = (x, y, c), (x, y, 1 - c)
        chips = _other_chips(x, y)

        def half(px, py, hc):
            return out_ref.at[2 * px + py, hc]

        mine = pltpu.make_async_copy(in_ref, out_ref.at[2 * x + y], local_sem)
        mine.start()
        first = [_remote(in_ref.at[c], half(x, y, c), send_sems, recv_sems, j, (*chip, c)) for j, chip in enumerate(chips)]
        for cp in first:
            cp.start()
        passed = [_remote(half(*chip, c), half(*chip, c), send_sems, recv_sems, 3 + j, sibling) for j, chip in enumerate(chips)]
        for j, chip in enumerate(chips):
            _remote(in_ref.at[c], half(*chip, c), send_sems, recv_sems, j, me).wait_recv()
            passed[j].start()
        for j, chip in enumerate(chips):
            _remote(in_ref.at[c], half(*chip, 1 - c), send_sems, recv_sems, 3 + j, me).wait_recv()
        for cp in first + passed:
            cp.wait_send()
        mine.wait()

    return pl.pallas_call(
        body, name=name, in_specs=[_ANY], out_specs=_ANY,
        out_shape=jax.ShapeDtypeStruct((4,) + loc.shape, loc.dtype),
        scratch_shapes=[_DMA((6,)), _DMA((6,)), _DMA(())],
    )(loc)


def _pos(px, py, perm):
    return 2 * py + px if perm else 2 * px + py


def _part(ref, kind, p, loc):
    if kind == "lead":
        return ref.at[p]
    return ref.at[:, pl.ds(pl.multiple_of(p * loc, LANE), loc)]


def _rows(ref, h, hr):
    return ref.at[pl.ds(h * hr, hr)]


def _rows_block(hr, width):
    return max(b for b in range(16, hr + 1, 16) if hr % b == 0 and (b * width <= (1 << 19) or b == 16))


def gather_big(locs, kinds, *, name):
    n = len(locs)

    def body(*refs):
        ins, outs = refs[:n], refs[n:2 * n]
        send_sems, recv_sems = refs[2 * n:]
        x, y, c = _place()
        me, sibling = (x, y, c), (x, y, 1 - c)
        chips = _other_chips(x, y)

        def half(i, px, py, h):
            (kind, perm), (rows, loc) = kinds[i], locs[i].shape
            return _rows(_part(outs[i], kind, _pos(px, py, perm), loc), h, rows // 2)

        sends = []
        for i in range(n):
            (kind, perm), (rows, loc) = kinds[i], locs[i].shape
            own = _part(outs[i], kind, _pos(x, y, perm), loc)
            sends.append(_remote(ins[i], own, send_sems, recv_sems, 6 * n + i, sibling))
            sends[-1].start()
            for j, chip in enumerate(chips):
                sends.append(_remote(_rows(ins[i], c, rows // 2), half(i, x, y, c), send_sems, recv_sems, 6 * i + j, (*chip, c)))
                sends[-1].start()
        for i in range(n):
            hr = locs[i].shape[0] // 2
            for j, chip in enumerate(chips):
                _remote(_rows(ins[i], c, hr), half(i, *chip, c), send_sems, recv_sems, 6 * i + j, me).wait_recv()
                sends.append(_remote(half(i, *chip, c), half(i, *chip, c), send_sems, recv_sems, 6 * i + 3 + j, sibling))
                sends[-1].start()
        for i in range(n):
            (kind, perm), (rows, loc) = kinds[i], locs[i].shape
            for j, chip in enumerate(chips):
                _remote(_rows(ins[i], c, rows // 2), half(i, *chip, 1 - c), send_sems, recv_sems, 6 * i + 3 + j, me).wait_recv()
            _remote(ins[i], _part(outs[i], kind, _pos(x, y, perm), loc), send_sems, recv_sems, 6 * n + i, me).wait_recv()
        for cp in sends:
            cp.wait_send()

    def out_shape(a, kind):
        rows, loc = a.shape
        return jax.ShapeDtypeStruct((4, rows, loc) if kind == "lead" else (rows, 4 * loc), a.dtype)

    return pl.pallas_call(
        body, name=name, in_specs=[_ANY] * n, out_specs=[_ANY] * n,
        out_shape=[out_shape(a, k[0]) for a, k in zip(locs, kinds)],
        scratch_shapes=[_DMA((7 * n,)), _DMA((7 * n,))],
    )(*locs)


def pair_swap(ps, kinds, *, name):
    n = len(ps)

    def body(*refs):
        ins, outs = refs[:n], refs[n:2 * n]
        send_sems, recv_sems = refs[2 * n:]
        x, y, c = _place()
        cps = []
        for i in range(n):
            if kinds[i][0] == "lead":
                hr = ps[i].shape[1] // 2
                src = ins[i].at[:, pl.ds((1 - c) * hr, hr)]
            else:
                hr = ps[i].shape[0] // 2
                src = _rows(ins[i], 1 - c, hr)
            cps.append(_remote(src, outs[i], send_sems, recv_sems, i, (x, y, 1 - c)))
            cps[-1].start()
        for cp in cps:
            cp.wait()

    def out_shape(a, kind):
        s = a.shape
        return jax.ShapeDtypeStruct((4, s[1] // 2, s[2]) if kind == "lead" else (s[0] // 2, s[1]), a.dtype)

    return pl.pallas_call(
        body, name=name, in_specs=[_ANY] * n, out_specs=[_ANY] * n,
        out_shape=[out_shape(a, k[0]) for a, k in zip(ps, kinds)], scratch_shapes=[_DMA((n,)), _DMA((n,))],
    )(*ps)


def pair_add(p, got, c_arr, kind, *, name):
    if kind == "lead":
        _, hr, cols = got.shape
        br = _rows_block(hr, cols)
        nb = hr // br
        grid = (4, nb)
        p_spec = pl.BlockSpec((None, br, cols), lambda s, i, cr: (s, cr[0] * nb + i, 0))
        g_spec = pl.BlockSpec((None, br, cols), lambda s, i, cr: (s, i, 0))
    else:
        hr, w = got.shape
        br = _rows_block(hr, w)
        nb = hr // br
        grid = (nb,)
        p_spec = pl.BlockSpec((br, w), lambda i, cr: (cr[0] * nb + i, 0))
        g_spec = pl.BlockSpec((br, w), lambda i, cr: (i, 0))

    def body(c_ref, p_ref, g_ref, o_ref):
        o_ref[...] = (p_ref[...] + g_ref[...]).astype(o_ref.dtype)

    return pl.pallas_call(
        body, name=name, out_shape=jax.ShapeDtypeStruct(got.shape, BF16),
        grid_spec=pltpu.PrefetchScalarGridSpec(num_scalar_prefetch=1, grid=grid, in_specs=[p_spec, g_spec], out_specs=g_spec),
        compiler_params=_cparams(len(grid)),
    )(c_arr, p, got)


def chip_scatter(qs, kinds, locs, *, name):
    n = len(qs)

    def body(*refs):
        ins, outs = refs[:n], refs[n:2 * n]
        send_sems, recv_sems = refs[2 * n:]
        x, y, c = _place()
        cps = []
        for i in range(n):
            kind, perm = kinds[i]
            for j, (px, py) in enumerate(_other_chips(x, y)):
                cps.append(_remote(_part(ins[i], kind, _pos(px, py, perm), locs[i]), outs[i].at[j], send_sems, recv_sems,
                                   3 * i + j, (px, py, c)))
                cps[-1].start()
        for cp in cps:
            cp.wait()

    def out_shape(a, kind, loc):
        hr = a.shape[1] if kind == "lead" else a.shape[0]
        return jax.ShapeDtypeStruct((3, hr, loc), a.dtype)

    return pl.pallas_call(
        body, name=name, in_specs=[_ANY] * n, out_specs=[_ANY] * n,
        out_shape=[out_shape(a, k[0], l) for a, k, l in zip(qs, kinds, locs)],
        scratch_shapes=[_DMA((3 * n,)), _DMA((3 * n,))],
    )(*qs)


def chip_add(q, r, pos_arr, c_arr, kind, loc, *, name):
    _, hr, _ = r.shape
    br = _rows_block(hr, loc)
    nb = hr // br
    if kind == "lead":
        q_spec = pl.BlockSpec((None, br, loc), lambda i, pr, cr: (pr[0], i, 0))
    else:
        q_spec = pl.BlockSpec((br, loc), lambda i, pr, cr: (i, pr[0]))
    r_spec = pl.BlockSpec((3, br, loc), lambda i, pr, cr: (0, i, 0))
    o_spec = pl.BlockSpec((br, loc), lambda i, pr, cr: (cr[0] * nb + i, 0))

    def body(p_ref, c_ref, q_ref, r_ref, o_ref):
        acc = q_ref[...].astype(F32)
        for j in range(3):
            acc = acc + r_ref[j].astype(F32)
        o_ref[...] = acc

    return pl.pallas_call(
        body, name=name, out_shape=jax.ShapeDtypeStruct((2 * hr, loc), F32),
        grid_spec=pltpu.PrefetchScalarGridSpec(num_scalar_prefetch=2, grid=(nb,), in_specs=[q_spec, r_spec], out_specs=o_spec),
        compiler_params=_cparams(1),
    )(pos_arr, c_arr, q, r)


def share_rows(fs, *, name):
    n = len(fs)

    def body(*refs):
        bufs = refs[n:2 * n]
        send_sems, recv_sems = refs[2 * n:]
        x, y, c = _place()
        cps = []
        for i in range(n):
            hr = fs[i].shape[0] // 2
            cps.append(_remote(_rows(bufs[i], c, hr), _rows(bufs[i], c, hr), send_sems, recv_sems, i, (x, y, 1 - c)))
            cps[-1].start()
        for i, cp in enumerate(cps):
            hr = fs[i].shape[0] // 2
            _remote(_rows(bufs[i], c, hr), _rows(bufs[i], 1 - c, hr), send_sems, recv_sems, i, (x, y, c)).wait_recv()
            cp.wait_send()

    return pl.pallas_call(
        body, name=name, in_specs=[_ANY] * n, out_specs=[_ANY] * n,
        out_shape=[jax.ShapeDtypeStruct(f.shape, f.dtype) for f in fs],
        input_output_aliases={i: i for i in range(n)}, scratch_shapes=[_DMA((n,)), _DMA((n,))],
    )(*fs)


def gather_all(v, *, name):
    def body(v_ref, out_ref, send_sems, recv_sems, local_sem):
        x, y, c = _place()
        flip = lambda p, m: 1 - p if m else p
        peers = [(flip(x, m & 4), flip(y, m & 2), flip(c, m & 1)) for m in range(1, 8)]
        idx = lambda p: 4 * p[0] + 2 * p[1] + p[2]
        mine = pltpu.make_async_copy(v_ref, out_ref.at[idx((x, y, c))], local_sem)
        mine.start()
        cps = [_remote(v_ref, out_ref.at[idx((x, y, c))], send_sems, recv_sems, k, p) for k, p in enumerate(peers)]
        for cp in cps:
            cp.start()
        for k, p in enumerate(peers):
            _remote(v_ref, out_ref.at[idx(p)], send_sems, recv_sems, k, p).wait_recv()
        for cp in cps:
            cp.wait_send()
        mine.wait()

    return pl.pallas_call(
        body, name=name, in_specs=[_ANY], out_specs=_ANY, out_shape=jax.ShapeDtypeStruct((8,) + v.shape, v.dtype),
        scratch_shapes=[_DMA((7,)), _DMA((7,)), _DMA(())],
    )(v)


def sum_stack(a, extra=None, *, name):
    n, R, L = a.shape
    br = _pick(R, _ROWS_ALIGN, 8)

    def body(*refs):
        a_ref, o_ref = refs[0], refs[-1]
        acc = refs[1][...] if extra is not None else a_ref[0]
        for i in range(0 if extra is not None else 1, n):
            acc = acc + a_ref[i]
        o_ref[...] = acc

    row = pl.BlockSpec((br, L), lambda i: (i, 0))
    specs = [pl.BlockSpec((n, br, L), lambda i: (0, i, 0))] + ([row] if extra is not None else [])
    args = [a] + ([extra] if extra is not None else [])
    return pl.pallas_call(body, name=name, grid=(R // br,), in_specs=specs, out_specs=row,
                          out_shape=jax.ShapeDtypeStruct((R, L), a.dtype), compiler_params=_cparams(1))(*args)


def adamw(w, g, m, v, *, name):
    shape = w.shape
    size = math.prod(shape)
    last = shape[-1]
    if last % LANE != 0 and size % LANE == 0 and size <= (1 << 20):
        last = LANE
    rows = size // last
    budget = (1 << 18) // last
    br = rows
    if rows > budget:
        br = max(c for c in range(8, budget + 1, 8) if rows % c == 0)
    v2 = lambda a: a.reshape(rows, last)

    def body(w_ref, g_ref, m_ref, v_ref, d_ref, nm_ref, nv_ref):
        gg = g_ref[...]
        nm = ADAM_B1 * m_ref[...] + (1.0 - ADAM_B1) * gg
        nv = ADAM_B2 * v_ref[...] + (1.0 - ADAM_B2) * (gg * gg)
        m_hat = nm / (1.0 - ADAM_B1 ** ADAM_STEP)
        v_hat = nv / (1.0 - ADAM_B2 ** ADAM_STEP)
        d_ref[...] = -ADAM_LR * (m_hat / (jnp.sqrt(v_hat) + ADAM_EPS) + ADAM_WD * w_ref[...])
        nm_ref[...] = nm
        nv_ref[...] = nv

    spec = pl.BlockSpec((br, last), lambda i: (i, 0))
    outs = pl.pallas_call(
        body, name=name, grid=(rows // br,), in_specs=[spec] * 4, out_specs=[spec] * 3,
        out_shape=[jax.ShapeDtypeStruct((rows, last), F32)] * 3, compiler_params=_cparams(1),
    )(v2(w), v2(g), v2(m), v2(v))
    return [o.reshape(shape) for o in outs]


_WEIGHTS = ["norm_mix_g", "norm_ffn_g", "gla_w_in", "gla_w_a2", "gla_b_a", "gla_norm_g", "gla_w_out", "ssd_w_in",
            "ssd_conv_w", "ssd_conv_b", "ssd_dt_bias", "ssd_a_log", "ssd_d", "ssd_norm_g", "ssd_w_out", "s5_log_dt",
            "s5_a_re", "s5_a_im", "s5_b_re", "s5_b_im", "s5_c_re", "s5_c_im", "s5_d", "s5_w_glu", "ffn_w_gu",
            "ffn_w_down", "final_norm_g"]
_SHARD_AXIS = {"gla_w_in": 2, "gla_w_a2": 2, "gla_b_a": 1, "gla_norm_g": 1, "gla_w_out": 1, "ssd_w_in": 2,
               "ssd_conv_w": 2, "ssd_w_out": 1, "s5_d": 1, "s5_w_glu": 2, "ffn_w_gu": 2, "ffn_w_down": 1}
_SMALL_SHARDED = [n for n in _WEIGHTS if n in _SHARD_AXIS and n not in _BIG]
_REPLICATED = [n for n in _WEIGHTS if n not in _SHARD_AXIS]
_BIG_KIND = {"gla_w_in": ("lead", False), "gla_w_out": ("lead", False), "ssd_w_in": ("lead", False),
             "ssd_w_out": ("lead", False), "s5_w_glu": ("cols", False), "ffn_w_gu": ("cols", True),
             "ffn_w_down": ("lead", False)}
_PADDED_IN = {"gla_w_in": GLA_INP, "ssd_w_in": SSD_INP}


def _to_rows(flat, parts=1):
    per = -(-flat.shape[0] // (parts * LANE * _ROWS_ALIGN)) * _ROWS_ALIGN
    flat = jnp.pad(flat, (0, parts * per * LANE - flat.shape[0]))
    return flat.reshape(parts, per, LANE)


def _big_layers(local):
    return [(n, j, local[n][j].reshape(-1, local[n].shape[-1])) for n in _BIG for j in range(local[n].shape[0])]


def _gather_weights(local):
    full = {}
    layers = _big_layers(local)
    got = gather_big([w.astype(BF16) for _, _, w in layers], [_BIG_KIND[n] for n, _, _ in layers], name="gather_weights")
    for (n, j, w), g in zip(layers, got):
        if n in _PADDED_IN:
            g = jnp.concatenate([g[s] for s in range(4)] + [jnp.zeros((g.shape[1], _PADDED_IN[n] - 4 * g.shape[2]), BF16)], axis=1)
        elif _BIG_KIND[n][0] == "lead":
            g = g.reshape(4 * g.shape[1], g.shape[2])
        full.setdefault(n, []).append(g)
    flat = jnp.concatenate([local[n].astype(F32).reshape(-1) for n in _SMALL_SHARDED])
    got = gather_shards(_to_rows(flat, 2), name="gather_small_weights").reshape(4, -1)
    off = 0
    for n in _SMALL_SHARDED:
        bs = local[n].shape
        sz = math.prod(bs)
        seg = got[:, off:off + sz].reshape((4,) + bs)
        off += sz
        ax = _SHARD_AXIS[n]
        full[n] = jnp.moveaxis(seg, 0, ax).reshape(bs[:ax] + (4 * bs[ax],) + bs[ax + 1:])
    return full


def _reduce_big(grads, local, x, y, c):
    ops = []
    for n in _BIG:
        kind = _BIG_KIND[n]
        for j, g in enumerate(grads[n]):
            loc = local[n].shape[-1] if kind[0] == "cols" or n in _PADDED_IN else g.shape[1]
            if n in _PADDED_IN:
                g = jnp.stack([g[:, s * loc:(s + 1) * loc] for s in range(4)])
            elif kind[0] == "lead":
                g = g.reshape(4, g.shape[0] // 4, g.shape[1])
            ops.append((n, j, kind, loc, g))
    kinds = [o[2] for o in ops]
    c_arr = jnp.reshape(c, (1,)).astype(jnp.int32)
    gots = pair_swap([o[4] for o in ops], kinds, name="reduce_pair_swap")
    qs = [pair_add(o[4], got, c_arr, o[2][0], name=f"reduce_pair_add_{o[0]}{o[1]}") for o, got in zip(ops, gots)]
    rs = chip_scatter(qs, kinds, [o[3] for o in ops], name="reduce_chip_scatter")
    fs = [chip_add(q, r, jnp.reshape(_pos(x, y, o[2][1]), (1,)).astype(jnp.int32), c_arr, o[2][0], o[3],
                   name=f"reduce_chip_add_{o[0]}{o[1]}") for o, q, r in zip(ops, qs, rs)]
    outs = share_rows(fs, name="reduce_share")
    red = {}
    for o, r in zip(ops, outs):
        red.setdefault(o[0], []).append(r)
    return {n: jnp.stack(v).reshape(local[n].shape) for n, v in red.items()}


def _reduce_small(grads, local, x, y):
    names = _REPLICATED + _SMALL_SHARDED
    flat = jnp.concatenate([grads[n].astype(F32).reshape(-1) for n in names])
    n_el = flat.shape[0]
    rows = -(-n_el // (LANE * 8)) * 8
    v = jnp.pad(flat, (0, rows * LANE - n_el)).reshape(rows, LANE)
    red = sum_stack(gather_all(v, name="reduce_small_gather"), name="reduce_small_add").reshape(-1)
    out, off = {}, 0
    for n in names:
        sz = math.prod(grads[n].shape)
        g = red[off:off + sz].reshape(grads[n].shape)
        off += sz
        if n in _SHARD_AXIS:
            ax = _SHARD_AXIS[n]
            loc = local[n].shape[ax]
            g = lax.dynamic_slice_in_dim(g, (2 * x + y) * loc, loc, axis=ax)
        out[n] = g
    return out


def kernel(x, norm_mix_g, norm_ffn_g, gla_w_in, gla_w_a2, gla_b_a, gla_norm_g, gla_w_out, ssd_w_in, ssd_conv_w, ssd_conv_b, ssd_dt_bias, ssd_a_log, ssd_d, ssd_norm_g, ssd_w_out, s5_log_dt, s5_a_re, s5_a_im, s5_b_re, s5_b_im, s5_c_re, s5_c_im, s5_d, s5_w_glu, ffn_w_gu, ffn_w_down, final_norm_g, loss_target, m_norm_mix_g, m_norm_ffn_g, m_gla_w_in, m_gla_w_a2, m_gla_b_a, m_gla_norm_g, m_gla_w_out, m_ssd_w_in, m_ssd_conv_w, m_ssd_conv_b, m_ssd_dt_bias, m_ssd_a_log, m_ssd_d, m_ssd_norm_g, m_ssd_w_out, m_s5_log_dt, m_s5_a_re, m_s5_a_im, m_s5_b_re, m_s5_b_im, m_s5_c_re, m_s5_c_im, m_s5_d, m_s5_w_glu, m_ffn_w_gu, m_ffn_w_down, m_final_norm_g, v_norm_mix_g, v_norm_ffn_g, v_gla_w_in, v_gla_w_a2, v_gla_b_a, v_gla_norm_g, v_gla_w_out, v_ssd_w_in, v_ssd_conv_w, v_ssd_conv_b, v_ssd_dt_bias, v_ssd_a_log, v_ssd_d, v_ssd_norm_g, v_ssd_w_out, v_s5_log_dt, v_s5_a_re, v_s5_a_im, v_s5_b_re, v_s5_b_im, v_s5_c_re, v_s5_c_im, v_s5_d, v_s5_w_glu, v_ffn_w_gu, v_ffn_w_down, v_final_norm_g):
    given = dict(locals())
    local = {n: given[n] for n in _WEIGHTS}
    px, py, pc = _place()

    full = dict(local)
    full.update(_gather_weights(local))
    loss, grad_x, grads = local_step(x[0], loss_target[0], full)
    loss = lax.psum(loss, ("x", "y", "c"))

    red = _reduce_big(grads, local, px, py, pc)
    red.update(_reduce_small(grads, local, px, py))

    deltas, new_m, new_v = {}, {}, {}
    for n in _WEIGHTS:
        deltas[n], new_m[n], new_v[n] = adamw(local[n], red[n], given["m_" + n], given["v_" + n], name=f"adamw_{n}")
    return (loss, grad_x[None], *[red[n] for n in _WEIGHTS], *[deltas[n] for n in _WEIGHTS],
            *[new_m[n] for n in _WEIGHTS], *[new_v[n] for n in _WEIGHTS])
```

```python
import functools
import math

import jax
import jax.numpy as jnp
from jax import lax
from jax.experimental import pallas as pl
from jax.experimental.pallas import tpu as pltpu

F32 = jnp.float32
BF16 = jnp.bfloat16

D = 1024
DEPTH = 4
CH = 64
EPS = 1e-6
GLA_H, GLA_DK, GLA_DV, GLA_RANK, GLA_TAU = 4, 128, 256, 16, 16.0
GLA_QK = GLA_H * GLA_DK
GLA_VD = GLA_H * GLA_DV
GLA_IN = 2 * GLA_QK + 2 * GLA_VD + GLA_RANK
GLA_INP = 3200
SSD_DI, SSD_HD, SSD_H, SSD_G, SSD_N, SSD_K = 2048, 64, 32, 8, 128, 4
SSD_GN = SSD_G * SSD_N
SSD_CONV = SSD_DI + 2 * SSD_GN
SSD_IN = SSD_DI + SSD_CONV + SSD_H
SSD_INP = 6272
S5_GS, S5_NG, S5_P = 16, 64, 64
S5_BLK = 8
FFN_H = 2816
LANE = 128
VMEM_LIMIT = 52 * 1024 * 1024
_MATMUL_VMEM = 40 * 1024 * 1024

ADAM_LR, ADAM_B1, ADAM_B2, ADAM_EPS, ADAM_WD, ADAM_STEP = 0.001, 0.9, 0.999, 1e-08, 0.01, 10

_ARB = "arbitrary"


def _cparams(n):
    return pltpu.CompilerParams(dimension_semantics=(_ARB,) * n, vmem_limit_bytes=VMEM_LIMIT)


def _pick(n, target, mult=LANE):
    best = None
    for c in range(mult, min(n, target) + 1, mult):
        if n % c == 0:
            best = c
    return best if best is not None else n


_DN = {"nn": (((1,), (0,)), ((), ())), "nt": (((1,), (1,)), ((), ())), "tn": (((0,), (0,)), ((), ()))}


def _dot_raw(a, b, form):
    return lax.dot_general(a.astype(BF16), b.astype(BF16), _DN[form], preferred_element_type=F32)


@functools.partial(jax.custom_vjp, nondiff_argnums=(2,))
def bdot(a, b, form):
    return _dot_raw(a, b, form)


def _bdot_fwd(a, b, form):
    return _dot_raw(a, b, form), (a, b)


def _bdot_bwd(form, res, g):
    a, b = res
    if form == "nn":
        return _dot_raw(g, b, "nt"), _dot_raw(a, g, "tn")
    if form == "nt":
        return _dot_raw(g, b, "nn"), _dot_raw(g, a, "tn")
    return _dot_raw(b, g, "nt"), _dot_raw(a, g, "nn")


bdot.defvjp(_bdot_fwd, _bdot_bwd)


def _hdot(a, b):
    return jnp.dot(a, b, precision=lax.Precision.HIGHEST, preferred_element_type=F32)


@jax.custom_vjp
def cdot_left(c, ct, x):
    return _hdot(c, x)


def _cdl_fwd(c, ct, x):
    return _hdot(c, x), (c, ct)


def _cdl_bwd(res, g):
    c, ct = res
    return jnp.zeros_like(c), jnp.zeros_like(ct), _hdot(ct, g)


cdot_left.defvjp(_cdl_fwd, _cdl_bwd)


@jax.custom_vjp
def cdot_right(x, c, ct):
    return _hdot(x, c)


def _cdr_fwd(x, c, ct):
    return _hdot(x, c), (c, ct)


def _cdr_bwd(res, g):
    c, ct = res
    return _hdot(g, ct), jnp.zeros_like(c), jnp.zeros_like(ct)


cdot_right.defvjp(_cdr_fwd, _cdr_bwd)


def _sigmoid(x):
    return 1.0 / (1.0 + jnp.exp(-x))


def _silu(x):
    return x * _sigmoid(x)


def _softplus(x):
    return jnp.maximum(x, 0.0) + jnp.log(1.0 + jnp.exp(-jnp.abs(x)))


def _log_sigmoid(x):
    return jnp.minimum(x, 0.0) - jnp.log(1.0 + jnp.exp(-jnp.abs(x)))


def _gelu(x):
    c = math.sqrt(2.0 / math.pi)
    return 0.5 * x * (1.0 + jnp.tanh(c * (x + 0.044715 * (x * x * x))))


def _rms(x, g):
    return x * lax.rsqrt(jnp.mean(x * x, axis=-1, keepdims=True) + EPS) * g


def _iota2(shape, axis):
    return lax.broadcasted_iota(jnp.int32, shape, axis)


def matmul(a, b, form, *, name, G=1, out_dtype=F32, add=None):
    isz = lambda t: jnp.dtype(t.dtype).itemsize
    osz = jnp.dtype(out_dtype).itemsize + (isz(add) if add is not None else 0)

    def fits(bm, bn, bk):
        return 2 * (bm * bk * isz(a) + bk * bn * isz(b) + bm * bn * osz) + 4 * bm * bn <= _MATMUL_VMEM

    if form in ("nn", "nt"):
        M = a.shape[0]
        K = a.shape[1] // G
        N = b.shape[2] if form == "nn" else b.shape[1]
        bm, bn, bk = min(M, 1024), _pick(N, 1536), _pick(K, 2048)
        while not fits(bm, bn, bk) and bk % 256 == 0:
            bk //= 2
        nj, nk = N // bn, K // bk
        grid = (G, M // bm, nj, nk)
        a_spec = pl.BlockSpec((bm, bk), lambda g, i, j, k: (i, g * nk + k))
        if form == "nn":
            b_spec = pl.BlockSpec((None, bk, bn), lambda g, i, j, k: (g, k, j))
        else:
            b_spec = pl.BlockSpec((None, bn, bk), lambda g, i, j, k: (g, j, k))
        o_spec = pl.BlockSpec((bm, bn), lambda g, i, j, k: (i, g * nj + j))
        out_shape = jax.ShapeDtypeStruct((M, G * N), out_dtype)
    else:
        T = a.shape[0]
        Ka, Nb = a.shape[1] // G, b.shape[1] // G
        bm, bn, bk = _pick(Ka, 1408), _pick(Nb, 1536), min(T, 2048)
        while not fits(bm, bn, bk) and bk % 512 == 0:
            bk //= 2
        ni, nj, nk = Ka // bm, Nb // bn, T // bk
        grid = (G, ni, nj, nk)
        a_spec = pl.BlockSpec((bk, bm), lambda g, i, j, k: (k, g * ni + i))
        b_spec = pl.BlockSpec((bk, bn), lambda g, i, j, k: (k, g * nj + j))
        o_spec = pl.BlockSpec((None, bm, bn), lambda g, i, j, k: (g, i, j))
        out_shape = jax.ShapeDtypeStruct((G, Ka, Nb), out_dtype)
    has_add = add is not None

    def finish(refs, r):
        if has_add:
            r = r + refs[2][...].astype(F32)
        o_ref = refs[3] if has_add else refs[2]
        o_ref[...] = r.astype(o_ref.dtype)

    def body_one(*refs):
        finish(refs, _dot_raw(refs[0][...], refs[1][...], form))

    def body_acc(*refs):
        acc_ref = refs[-1]
        k = pl.program_id(3)

        @pl.when(k == 0)
        def _():
            acc_ref[...] = jnp.zeros_like(acc_ref)

        acc_ref[...] += _dot_raw(refs[0][...], refs[1][...], form)

        @pl.when(k == nk - 1)
        def _():
            finish(refs, acc_ref[...])

    in_specs = [a_spec, b_spec]
    args = [a, b]
    if has_add:
        in_specs.append(o_spec)
        args.append(add)
    return pl.pallas_call(
        body_one if nk == 1 else body_acc, name=name, grid=grid, in_specs=in_specs, out_specs=o_spec,
        out_shape=out_shape, scratch_shapes=[] if nk == 1 else [pltpu.VMEM((bm, bn), F32)],
        compiler_params=_cparams(4),
    )(*args)


def ffn_up(hn, w_il, *, name):
    T = hn.shape[0]
    bm, hb = min(T, 512), FFN_H // 2

    def body(a_ref, b_ref, act_ref, gu_ref):
        r = _dot_raw(a_ref[...], b_ref[...], "nn")
        act_ref[...] = (_silu(r[:, :hb]) * r[:, hb:]).astype(act_ref.dtype)
        gu_ref[...] = r.astype(gu_ref.dtype)

    return pl.pallas_call(
        body, name=name, grid=(2, T // bm),
        in_specs=[pl.BlockSpec((bm, D), lambda j, i: (i, 0)), pl.BlockSpec((D, 2 * hb), lambda j, i: (0, j))],
        out_specs=[pl.BlockSpec((bm, hb), lambda j, i: (i, j)), pl.BlockSpec((bm, 2 * hb), lambda j, i: (i, j))],
        out_shape=[jax.ShapeDtypeStruct((T, FFN_H), BF16), jax.ShapeDtypeStruct((T, 2 * FFN_H), BF16)],
        compiler_params=_cparams(2),
    )(hn, w_il)


_DACT_CHUNK = 512


def ffn_dact(d, w_down, gu, *, name):
    T = d.shape[0]
    bm, hb = min(T, 512), FFN_H // 2

    def body(d_ref, w_ref, gu_ref, o_ref):
        d_blk = d_ref[...].astype(BF16)
        for lo in range(0, hb, _DACT_CHUNK):
            hi = min(lo + _DACT_CHUNK, hb)
            da = _dot_raw(d_blk, w_ref[lo:hi, :], "nt")
            g, u = gu_ref[:, lo:hi].astype(F32), gu_ref[:, hb + lo:hb + hi].astype(F32)
            sg = _sigmoid(g)
            o_ref[:, lo:hi] = (da * u * (sg * (1.0 + g * (1.0 - sg)))).astype(o_ref.dtype)
            o_ref[:, hb + lo:hb + hi] = (da * (g * sg)).astype(o_ref.dtype)

    return pl.pallas_call(
        body, name=name, grid=(2, T // bm),
        in_specs=[pl.BlockSpec((bm, D), lambda j, i: (i, 0)), pl.BlockSpec((hb, D), lambda j, i: (j, 0)),
                  pl.BlockSpec((bm, 2 * hb), lambda j, i: (i, j))],
        out_specs=pl.BlockSpec((bm, 2 * hb), lambda j, i: (i, j)),
        out_shape=jax.ShapeDtypeStruct((T, 2 * FFN_H), BF16), compiler_params=_cparams(2),
    )(d, w_down, gu)


def _row_entry(e):
    return e if isinstance(e, tuple) else (e, e.shape[1], 0)


def _row_spec(bt, e):
    _, width, idx = e
    return pl.BlockSpec((bt, width), lambda i: (i, idx))


def _full_spec(p):
    return pl.BlockSpec(p.shape, lambda i: (0,) * p.ndim)


def rowwise(f, params, rows, outs, *, bt, name):
    rows = [_row_entry(e) for e in rows]
    T = rows[0][0].shape[0]
    bt = min(bt, T)
    np_, nr = len(params), len(rows)

    def body(*refs):
        p = tuple(r[...].astype(F32) for r in refs[:np_])
        rw = tuple(r[...].astype(F32) for r in refs[np_:np_ + nr])
        res = f(p, rw)
        for o_ref, o in zip(refs[np_ + nr:], res):
            o_ref[...] = o.astype(o_ref.dtype)

    res = pl.pallas_call(
        body, name=name, grid=(T // bt,),
        in_specs=[_full_spec(p) for p in params] + [_row_spec(bt, e) for e in rows],
        out_specs=[pl.BlockSpec((bt, w), lambda i: (i, 0)) for w, _ in outs],
        out_shape=[jax.ShapeDtypeStruct((T, w), dt) for w, dt in outs],
        compiler_params=_cparams(1),
    )(*params, *[e[0] for e in rows])
    return list(res)


def rowwise_vjp(f, params, rows, cts, drow_dtypes, *, bt, name):
    rows = [_row_entry(e) for e in rows]
    cts = [_row_entry(e) for e in cts]
    T = rows[0][0].shape[0]
    bt = min(bt, T)
    np_, nr, nc = len(params), len(rows), len(cts)
    want = [i for i, dt in enumerate(drow_dtypes) if dt is not None]

    def body(*refs):
        p = tuple(r[...].astype(F32) for r in refs[:np_])
        rw = tuple(r[...].astype(F32) for r in refs[np_:np_ + nr])
        ct = tuple(r[...].astype(F32) for r in refs[np_ + nr:np_ + nr + nc])
        outs = refs[np_ + nr + nc:]
        _, vjp = jax.vjp(f, p, rw)
        dp, dr = vjp(ct)

        @pl.when(pl.program_id(0) == 0)
        def _():
            for o in outs[:np_]:
                o[...] = jnp.zeros_like(o)

        for o, d in zip(outs[:np_], dp):
            o[...] += d
        for o, i in zip(outs[np_:], want):
            o[...] = dr[i].astype(o.dtype)

    res = pl.pallas_call(
        body, name=name, grid=(T // bt,),
        in_specs=[_full_spec(p) for p in params] + [_row_spec(bt, e) for e in rows] + [_row_spec(bt, e) for e in cts],
        out_specs=[_full_spec(p) for p in params] + [pl.BlockSpec((bt, rows[i][1]), lambda i_: (i_, 0)) for i in want],
        out_shape=[jax.ShapeDtypeStruct(p.shape, F32) for p in params]
        + [jax.ShapeDtypeStruct((T, rows[i][1]), drow_dtypes[i]) for i in want],
        compiler_params=_cparams(1),
    )(*params, *[e[0] for e in rows], *[e[0] for e in cts])
    res = list(res)
    return res[:np_], res[np_:]


def f_rmsnorm(p, r):
    return (_rms(r[0], p[0]),)


def f_rmsnorm_res(p, r):
    return (_rms(r[0], p[0]), r[0])


def f_swiglu(p, r):
    gu = r[0]
    return (_silu(gu[:, :FFN_H]) * gu[:, FFN_H:],)


def f_gla_gate_in(p, r):
    w_a2, b_a = p
    z = bdot(r[0], w_a2, "nn") + b_a
    return (_log_sigmoid(z) / GLA_TAU,)


def f_gla_gate_in_fwd(p, r):
    w_a2, b_a = p
    z = _dot_raw(r[0], w_a2, "nn") + b_a
    return (_log_sigmoid(z) / GLA_TAU,)


def f_gla_out(p, r):
    (ng,) = p
    o, rr = r
    parts = []
    for h in range(GLA_H):
        sl = slice(h * GLA_DV, (h + 1) * GLA_DV)
        parts.append(_rms(o[:, sl], ng[:, sl]) * _silu(rr[:, sl]))
    return (jnp.concatenate(parts, axis=1),)


def f_ssd_out(p, r):
    (ng,) = p
    y, z = r
    t = y * _silu(z)
    gsz = SSD_DI // SSD_G
    parts = []
    for g in range(SSD_G):
        sl = slice(g * gsz, (g + 1) * gsz)
        parts.append(_rms(t[:, sl], ng[:, sl]))
    return (jnp.concatenate(parts, axis=1),)


def f_s5_act(p, r):
    (dsk,) = p
    ycp, u = r
    return (_gelu(ycp + dsk * u),)


def f_glu_res(p, r):
    vg, h = r
    return (vg[:, :D] * _sigmoid(vg[:, D:]) + h,)


def f_glu(p, r):
    vg = r[0]
    return (vg[:, :D] * _sigmoid(vg[:, D:]),)


def loss_head(h, g, target, *, name):
    T = h.shape[0]
    bt = min(T, 256)

    def lossf(g_, h_, t_):
        e = _rms(h_, g_) - t_
        return (0.5 / D) * jnp.sum(e * e)

    def body(g_ref, h_ref, t_ref, loss_ref, dg_ref, dh_ref):
        @pl.when(pl.program_id(0) == 0)
        def _():
            loss_ref[...] = jnp.zeros_like(loss_ref)
            dg_ref[...] = jnp.zeros_like(dg_ref)

        val, vjp = jax.vjp(lossf, g_ref[...], h_ref[...], t_ref[...])
        dg, dh, _ = vjp(jnp.ones((), F32))
        loss_ref[...] += jnp.full(loss_ref.shape, val, F32)
        dg_ref[...] += dg
        dh_ref[...] = dh

    row = pl.BlockSpec((bt, D), lambda i: (i, 0))
    one = pl.BlockSpec((1, D), lambda i: (0, 0))
    loss, dg, dh = pl.pallas_call(
        body, name=name, grid=(T // bt,), in_specs=[one, row, row],
        out_specs=[pl.BlockSpec((1, LANE), lambda i: (0, 0)), one, row],
        out_shape=[jax.ShapeDtypeStruct((1, LANE), F32), jax.ShapeDtypeStruct((1, D), F32),
                   jax.ShapeDtypeStruct((T, D), F32)],
        compiler_params=_cparams(1),
    )(g, h, target)
    return loss[0, 0], dg, dh


def _gla_consts():
    r, c = _iota2((CH, CH), 0), _iota2((CH, CH), 1)
    return (r >= c).astype(F32), (r <= c).astype(F32), r >= c


def _gla_chunk(q, k, v, la, st, consts, dot, cdl):
    L, Lt, tril = consts
    lc = cdl(L, Lt, la)
    lend = lc[CH - 1:CH, :]
    e, ei = jnp.exp(lc), jnp.exp(-lc)
    qs = q * (GLA_DK ** -0.5)
    qf, kf, qb, kb = qs * e, k * ei, qs * ei, k * e
    sc = jnp.where(tril, dot(qf, kf, "nt"), dot(qb, kb, "nt"))
    o = dot(sc, v, "nn") + dot(qf, st, "nt")
    kd = k * jnp.exp(lend - lc)
    st_new = st * jnp.exp(lend) + dot(v, kd, "tn")
    return o, st_new


def _gla_block(q, k, v, la, st, nc, dot, cdl):
    consts = _gla_consts()
    outs = []
    for c in range(nc):
        sl = slice(c * CH, (c + 1) * CH)
        o, st = _gla_chunk(q[sl], k[sl], v[sl], la[sl], st, consts, dot, cdl)
        outs.append(o)
    return jnp.concatenate(outs, axis=0), st


_GLA_HP = 2


def _gla_specs(rows, rev, nb):
    t = (lambda j: nb - 1 - j) if rev else (lambda j: j)
    hp, ng = _GLA_HP, GLA_H // _GLA_HP
    q = pl.BlockSpec((rows, hp * GLA_DK), lambda h, j: (t(j), h))
    k = pl.BlockSpec((rows, hp * GLA_DK), lambda h, j: (t(j), ng + h))
    v = pl.BlockSpec((rows, hp * GLA_DV), lambda h, j: (t(j), ng + h))
    la = pl.BlockSpec((rows, hp * GLA_DK), lambda h, j: (t(j), h))
    ss = pl.BlockSpec((None, hp, GLA_DV, GLA_DK), lambda h, j: (t(j), h, 0, 0))
    o = pl.BlockSpec((rows, hp * GLA_DV), lambda h, j: (t(j), h))
    return q, k, v, la, ss, o


def _gla_heads(q, k, v, la, sts, nc, dot, cdl):
    outs, new = [], []
    for i in range(_GLA_HP):
        kk, vv = slice(i * GLA_DK, (i + 1) * GLA_DK), slice(i * GLA_DV, (i + 1) * GLA_DV)
        o, st = _gla_block(q[:, kk], k[:, kk], v[:, vv], la[:, kk], sts[i], nc, dot, cdl)
        outs.append(o)
        new.append(st)
    return jnp.concatenate(outs, axis=1), tuple(new)


def gla_scan_fwd(proj, la, *, nc, name):
    T = proj.shape[0]
    rows = min(T, nc * CH)
    nc = rows // CH
    nb = T // rows
    q_s, k_s, v_s, la_s, ss_s, o_s = _gla_specs(rows, False, nb)

    def body(q_ref, k_ref, v_ref, la_ref, o_ref, ss_ref, st_ref):
        @pl.when(pl.program_id(1) == 0)
        def _():
            st_ref[...] = jnp.zeros_like(st_ref)

        ss_ref[...] = st_ref[...]
        sts = tuple(st_ref[i] for i in range(_GLA_HP))
        o, sts = _gla_heads(q_ref[...], k_ref[...], v_ref[...], la_ref[...], sts, nc, _dot_raw, lambda c, ct, x: _hdot(c, x))
        o_ref[...] = o
        for i in range(_GLA_HP):
            st_ref[i] = sts[i]

    return pl.pallas_call(
        body, name=name, grid=(GLA_H // _GLA_HP, nb), in_specs=[q_s, k_s, v_s, la_s], out_specs=[o_s, ss_s],
        out_shape=[jax.ShapeDtypeStruct((T, GLA_VD), F32), jax.ShapeDtypeStruct((nb, GLA_H, GLA_DV, GLA_DK), F32)],
        scratch_shapes=[pltpu.VMEM((_GLA_HP, GLA_DV, GLA_DK), F32)], compiler_params=_cparams(2),
    )(proj, proj, proj, la)


def gla_scan_bwd(proj, la, ss, do, *, nc, name):
    T = proj.shape[0]
    rows = min(T, nc * CH)
    nc = rows // CH
    nb = T // rows
    q_s, k_s, v_s, la_s, ss_s, o_s = _gla_specs(rows, True, nb)
    t = lambda j: nb - 1 - j
    dqk_s = pl.BlockSpec((rows, _GLA_HP * GLA_DK), lambda h, j: (t(j), h))

    def body(q_ref, k_ref, v_ref, la_ref, ss_ref, do_ref, dq_ref, dk_ref, dv_ref, dla_ref, dst_ref):
        @pl.when(pl.program_id(1) == 0)
        def _():
            dst_ref[...] = jnp.zeros_like(dst_ref)

        fn = lambda q, k, v, la_, *sts: _gla_heads(q, k, v, la_, sts, nc, bdot, cdot_left)
        _, vjp = jax.vjp(fn, q_ref[...], k_ref[...], v_ref[...], la_ref[...], *[ss_ref[i] for i in range(_GLA_HP)])
        dq, dk, dv, dla, *dsts = vjp((do_ref[...], tuple(dst_ref[i] for i in range(_GLA_HP))))
        dq_ref[...] = dq.astype(dq_ref.dtype)
        dk_ref[...] = dk.astype(dk_ref.dtype)
        dv_ref[...] = dv.astype(dv_ref.dtype)
        dla_ref[...] = dla
        for i in range(_GLA_HP):
            dst_ref[i] = dsts[i]

    return pl.pallas_call(
        body, name=name, grid=(GLA_H // _GLA_HP, nb), in_specs=[q_s, k_s, v_s, la_s, ss_s, o_s],
        out_specs=[dqk_s, dqk_s, o_s, dqk_s],
        out_shape=[jax.ShapeDtypeStruct((T, GLA_QK), BF16), jax.ShapeDtypeStruct((T, GLA_QK), BF16),
                   jax.ShapeDtypeStruct((T, GLA_VD), BF16), jax.ShapeDtypeStruct((T, GLA_QK), F32)],
        scratch_shapes=[pltpu.VMEM((_GLA_HP, GLA_DV, GLA_DK), F32)], compiler_params=_cparams(2),
    )(proj, proj, proj, la, ss, do)


_CONV_W = 512
_CONV_OFF = SSD_DI // _CONV_W


def _conv_pre(x, prev8, w_ref, b_ref):
    bt = x.shape[0]
    ext = jnp.concatenate([prev8, x], axis=0)
    shifted = []
    for j in range(SSD_K):
        s = SSD_K - 1 - j
        shifted.append(x if s == 0 else pltpu.roll(ext, s, 0)[8:8 + bt])
    pre = b_ref[...] + sum(w_ref[j:j + 1, :] * shifted[j] for j in range(SSD_K))
    return pre, shifted


def ssd_conv_fwd(proj, w, b, *, name):
    T = proj.shape[0]
    bt = min(T, 512)
    nb = T // bt

    def body(x_ref, w_ref, b_ref, o_ref, carry_ref):
        @pl.when(pl.program_id(1) == 0)
        def _():
            carry_ref[...] = jnp.zeros_like(carry_ref)

        x = x_ref[...]
        pre, _ = _conv_pre(x, carry_ref[...], w_ref, b_ref)
        o_ref[...] = _silu(pre)
        carry_ref[...] = x[bt - 8:, :]

    return pl.pallas_call(
        body, name=name, grid=(SSD_CONV // _CONV_W, nb),
        in_specs=[pl.BlockSpec((bt, _CONV_W), lambda c, t: (t, _CONV_OFF + c)),
                  pl.BlockSpec((SSD_K, _CONV_W), lambda c, t: (0, c)),
                  pl.BlockSpec((1, _CONV_W), lambda c, t: (0, c))],
        out_specs=pl.BlockSpec((bt, _CONV_W), lambda c, t: (t, c)),
        out_shape=jax.ShapeDtypeStruct((T, SSD_CONV), F32),
        scratch_shapes=[pltpu.VMEM((8, _CONV_W), F32)], compiler_params=_cparams(2),
    )(proj, w, b)


def ssd_conv_bwd(proj, w, b, dout, *, name):
    T = proj.shape[0]
    bt = min(T, 512)
    nb = T // bt
    r8 = bt // 8

    def body(x_ref, xp_ref, w_ref, b_ref, do_ref, dx_ref, dw_ref, db_ref, carry_ref):
        t = pl.program_id(1)

        @pl.when(t == 0)
        def _():
            carry_ref[...] = jnp.zeros_like(carry_ref)
            dw_ref[...] = jnp.zeros_like(dw_ref)
            db_ref[...] = jnp.zeros_like(db_ref)

        x = x_ref[...]
        prev8 = jnp.where(t == nb - 1, 0.0, xp_ref[...])
        pre, shifted = _conv_pre(x, prev8, w_ref, b_ref)
        sg = _sigmoid(pre)
        dpre = do_ref[...] * (sg * (1.0 + pre * (1.0 - sg)))
        ext = jnp.concatenate([dpre, carry_ref[...]], axis=0)
        dx = w_ref[SSD_K - 1:SSD_K, :] * dpre
        for j in range(SSD_K - 1):
            s = SSD_K - 1 - j
            dx = dx + w_ref[j:j + 1, :] * pltpu.roll(ext, bt + 8 - s, 0)[:bt]
        dx_ref[...] = dx.astype(dx_ref.dtype)
        dw_ref[...] += jnp.concatenate([jnp.sum(dpre * shifted[j], axis=0, keepdims=True) for j in range(SSD_K)], axis=0)
        db_ref[...] += jnp.sum(dpre, axis=0, keepdims=True)
        carry_ref[...] = dpre[:8, :]

    rt = lambda t: nb - 1 - t
    return pl.pallas_call(
        body, name=name, grid=(SSD_CONV // _CONV_W, nb),
        in_specs=[pl.BlockSpec((bt, _CONV_W), lambda c, t: (rt(t), _CONV_OFF + c)),
                  pl.BlockSpec((8, _CONV_W), lambda c, t: (jnp.maximum(rt(t) * r8 - 1, 0), _CONV_OFF + c)),
                  pl.BlockSpec((SSD_K, _CONV_W), lambda c, t: (0, c)),
                  pl.BlockSpec((1, _CONV_W), lambda c, t: (0, c)),
                  pl.BlockSpec((bt, _CONV_W), lambda c, t: (rt(t), c))],
        out_specs=[pl.BlockSpec((bt, _CONV_W), lambda c, t: (rt(t), c)),
                   pl.BlockSpec((SSD_K, _CONV_W), lambda c, t: (0, c)),
                   pl.BlockSpec((1, _CONV_W), lambda c, t: (0, c))],
        out_shape=[jax.ShapeDtypeStruct((T, SSD_CONV), BF16), jax.ShapeDtypeStruct((SSD_K, SSD_CONV), F32),
                   jax.ShapeDtypeStruct((1, SSD_CONV), F32)],
        scratch_shapes=[pltpu.VMEM((8, _CONV_W), F32)], compiler_params=_cparams(2),
    )(proj, proj, w, b, dout)


_SSD_U = 2 * CH


def _ssd_unit(xs, bm, cm, dtraw, dtb, alog, dsk, hp, g, dot, cdl, cdr):
    U, P2 = _SSD_U, 2 * SSD_HD
    r, c = _iota2((U, U), 0), _iota2((U, U), 1)
    same = (r // CH) == (c // CH)
    Lb = (same & (r >= c)).astype(F32)
    Ub = (same & (r <= c)).astype(F32)
    lane = _iota2((1, U), 1)
    lo_lane = _iota2((1, P2), 1) < SSD_HD
    lo_sub = _iota2((P2, 1), 0) < SSD_HD
    diag2 = (_iota2((CH, P2), 0) == (_iota2((CH, P2), 1) % CH)).astype(F32)

    dt = _softplus(dtraw + dtb)
    da = dt * (-jnp.exp(alog))
    cum = cdl(Lb, Ub, da)
    ys = []
    new_hp = []
    for pr in range(2):
        xs_p = xs[:, pr * P2:(pr + 1) * P2]
        cols, dts, dks = [], [], []
        for jj in range(2):
            oh_l = (lane == g * (SSD_H // SSD_G) + 2 * pr + jj).astype(F32)
            cols.append(jnp.sum(cum * oh_l, axis=1, keepdims=True))
            dts.append(jnp.sum(dt * oh_l, axis=1, keepdims=True))
            dks.append(jnp.sum(dsk * oh_l, axis=1, keepdims=True))
        dsk_p = jnp.where(lo_lane, dks[0], dks[1])
        h = hp[pr]
        yc = []
        for ci in range(2):
            sl = slice(ci * CH, (ci + 1) * CH)
            xs_c, bm_c, cm_c = xs_p[sl], bm[sl], cm[sl]
            col = jnp.where(lo_lane, cols[0][sl], cols[1][sl])
            dtc = jnp.where(lo_lane, dts[0][sl], dts[1][sl])
            row = jnp.sum(diag2 * col, axis=0, keepdims=True)
            dtrow = jnp.sum(diag2 * dtc, axis=0, keepdims=True)
            cb = dot(cm_c, jnp.concatenate([bm_c, bm_c], axis=0), "nt")
            mix = cb * jnp.exp(-jnp.abs(col - row)) * dtrow
            xbd = jnp.concatenate([jnp.where(lo_lane, xs_c, 0.0), jnp.where(lo_lane, 0.0, xs_c)], axis=0)
            y_intra = dot(mix, xbd, "nn")
            ce = jnp.where(lo_lane, cols[0][ci * CH + CH - 1:ci * CH + CH, :], cols[1][ci * CH + CH - 1:ci * CH + CH, :])
            y_inter = dot(cm_c, h, "nt") * jnp.exp(col)
            xw = xs_c * (dtc * jnp.exp(ce - col))
            ce_s = [cols[jj][ci * CH + CH - 1:ci * CH + CH, :] for jj in range(2)]
            a_p = jnp.where(lo_sub, jnp.exp(ce_s[0]), jnp.exp(ce_s[1]))
            h = a_p * h + dot(xw, bm_c, "tn")
            yc.append(y_intra + y_inter + dsk_p * xs_c)
        ys.append(jnp.concatenate(yc, axis=0))
        new_hp.append(h)
    return jnp.concatenate(ys, axis=1), tuple(new_hp)


def _ssd_block(xs, bm, cm, dtraw, dtb, alog, dsk, hp, g, nu, dot, cdl, cdr):
    outs = []
    for u in range(nu):
        sl = slice(u * _SSD_U, (u + 1) * _SSD_U)
        y, hp = _ssd_unit(xs[sl], bm[sl], cm[sl], dtraw[sl], dtb, alog, dsk, hp, g, dot, cdl, cdr)
        outs.append(y)
    return jnp.concatenate(outs, axis=0), hp


def _ssd_specs(rows, rev, nb):
    t = (lambda j: nb - 1 - j) if rev else (lambda j: j)
    gw = SSD_DI // SSD_G
    xs = pl.BlockSpec((rows, gw), lambda j, g: (t(j), g))
    bm = pl.BlockSpec((rows, SSD_N), lambda j, g: (t(j), SSD_DI // SSD_N + g))
    cm = pl.BlockSpec((rows, SSD_N), lambda j, g: (t(j), (SSD_DI + SSD_GN) // SSD_N + g))
    dtr = pl.BlockSpec((rows, LANE), lambda j, g: (t(j), (SSD_DI + SSD_CONV) // LANE))
    par = pl.BlockSpec((1, LANE), lambda j, g: (0, 0))
    hs = pl.BlockSpec((None, None, 2, 2 * SSD_HD, SSD_N), lambda j, g: (t(j), g, 0, 0, 0))
    y = pl.BlockSpec((rows, gw), lambda j, g: (t(j), g))
    return xs, bm, cm, dtr, par, hs, y


def ssd_scan_fwd(xbc, proj, dtb, alog, dsk, *, nu, name):
    T = xbc.shape[0]
    rows = min(T, nu * _SSD_U)
    nu = rows // _SSD_U
    nb = T // rows
    xs_s, bm_s, cm_s, dt_s, par_s, hs_s, y_s = _ssd_specs(rows, False, nb)

    def body(xs_ref, bm_ref, cm_ref, dt_ref, dtb_ref, al_ref, dsk_ref, y_ref, hs_ref, h_ref):
        g = pl.program_id(1)

        @pl.when(pl.program_id(0) == 0)
        def _():
            h_ref[g] = jnp.zeros(h_ref.shape[1:], F32)

        hs_ref[...] = h_ref[g]
        hp = (h_ref[g, 0], h_ref[g, 1])
        y, hp = _ssd_block(xs_ref[...], bm_ref[...], cm_ref[...], dt_ref[...], dtb_ref[...], al_ref[...], dsk_ref[...],
                           hp, g, nu, _dot_raw, lambda c, ct, x: _hdot(c, x), lambda x, c, ct: _hdot(x, c))
        y_ref[...] = y
        h_ref[g, 0] = hp[0]
        h_ref[g, 1] = hp[1]

    return pl.pallas_call(
        body, name=name, grid=(nb, SSD_G), in_specs=[xs_s, bm_s, cm_s, dt_s, par_s, par_s, par_s],
        out_specs=[y_s, hs_s],
        out_shape=[jax.ShapeDtypeStruct((T, SSD_DI), F32), jax.ShapeDtypeStruct((nb, SSD_G, 2, 2 * SSD_HD, SSD_N), F32)],
        scratch_shapes=[pltpu.VMEM((SSD_G, 2, 2 * SSD_HD, SSD_N), F32)], compiler_params=_cparams(2),
    )(xbc, xbc, xbc, proj, dtb, alog, dsk)


def ssd_scan_bwd(xbc, proj, dtb, alog, dsk, hs, dy, *, nu, name):
    T = xbc.shape[0]
    rows = min(T, nu * _SSD_U)
    nu = rows // _SSD_U
    nb = T // rows
    xs_s, bm_s, cm_s, dt_s, par_s, hs_s, y_s = _ssd_specs(rows, True, nb)
    t = lambda j: nb - 1 - j
    n_s = pl.BlockSpec((rows, SSD_N), lambda j, g: (t(j), g))
    ddt_s = pl.BlockSpec((rows, LANE), lambda j, g: (t(j), 0))

    def body(xs_ref, bm_ref, cm_ref, dt_ref, dtb_ref, al_ref, dsk_ref, hs_ref, dy_ref,
             dxs_ref, dbm_ref, dcm_ref, ddt_ref, ddtb_ref, dal_ref, ddsk_ref, dh_ref):
        j, g = pl.program_id(0), pl.program_id(1)

        @pl.when(j == 0)
        def _():
            dh_ref[g] = jnp.zeros(dh_ref.shape[1:], F32)

        @pl.when((j == 0) & (g == 0))
        def _():
            ddtb_ref[...] = jnp.zeros_like(ddtb_ref)
            dal_ref[...] = jnp.zeros_like(dal_ref)
            ddsk_ref[...] = jnp.zeros_like(ddsk_ref)

        @pl.when(g == 0)
        def _():
            ddt_ref[...] = jnp.zeros_like(ddt_ref)

        fn = lambda xs, bm, cm, dtr, dtb_, al, dsk_, h0, h1: _ssd_block(
            xs, bm, cm, dtr, dtb_, al, dsk_, (h0, h1), g, nu, bdot, cdot_left, cdot_right)
        _, vjp = jax.vjp(fn, xs_ref[...], bm_ref[...], cm_ref[...], dt_ref[...], dtb_ref[...], al_ref[...], dsk_ref[...],
                         hs_ref[0], hs_ref[1])
        dxs, dbm, dcm, ddt, ddtb, dal, ddsk, dh0, dh1 = vjp((dy_ref[...], (dh_ref[g, 0], dh_ref[g, 1])))
        dxs_ref[...] = dxs
        dbm_ref[...] = dbm
        dcm_ref[...] = dcm
        ddt_ref[...] += ddt
        ddtb_ref[...] += ddtb
        dal_ref[...] += dal
        ddsk_ref[...] += ddsk
        dh_ref[g, 0] = dh0
        dh_ref[g, 1] = dh1

    return pl.pallas_call(
        body, name=name, grid=(nb, SSD_G), in_specs=[xs_s, bm_s, cm_s, dt_s, par_s, par_s, par_s, hs_s, y_s],
        out_specs=[y_s, n_s, n_s, ddt_s, par_s, par_s, par_s],
        out_shape=[jax.ShapeDtypeStruct((T, SSD_DI), F32), jax.ShapeDtypeStruct((T, SSD_GN), F32),
                   jax.ShapeDtypeStruct((T, SSD_GN), F32), jax.ShapeDtypeStruct((T, LANE), F32),
                   jax.ShapeDtypeStruct((1, LANE), F32), jax.ShapeDtypeStruct((1, LANE), F32),
                   jax.ShapeDtypeStruct((1, LANE), F32)],
        scratch_shapes=[pltpu.VMEM((SSD_G, 2, 2 * SSD_HD, SSD_N), F32)], compiler_params=_cparams(2),
    )(xbc, xbc, xbc, proj, dtb, alog, dsk, hs, dy)


def _s5_param_f(log_dt, a_re, a_im, bre_t, bim_t, cim, cdl):
    n = S5_NG * S5_GS
    r, c = _iota2((n, S5_NG), 0), _iota2((n, S5_NG), 1)
    E = ((r // S5_GS) == c).astype(F32)
    rt, ct = _iota2((S5_NG, n), 0), _iota2((S5_NG, n), 1)
    Et = ((ct // S5_GS) == rt).astype(F32)
    step = jnp.exp(log_dt)
    mag = jnp.exp(step * a_re)
    abr = mag * jnp.cos(step * a_im)
    abi = mag * jnp.sin(step * a_im)
    den = a_re * a_re + a_im * a_im
    nr, ni = abr - 1.0, abi
    fr = (nr * a_re + ni * a_im) / den
    fi = (ni * a_re - nr * a_im) / den
    Fr, Fi = cdl(E, Et, fr), cdl(E, Et, fi)
    bbr = Fr * bre_t - Fi * bim_t
    bbi = Fr * bim_t + Fi * bre_t
    return abr, abi, bbr, bbi, -cim


def _whole(a):
    return pl.BlockSpec(a.shape, lambda: (0,) * a.ndim)


def s5_param_fwd(args, *, name):
    def body(*refs):
        res = _s5_param_f(*[r[...] for r in refs[:6]], lambda c, ct, x: _hdot(c, x))
        for o, v in zip(refs[6:], res):
            o[...] = v

    shapes = [(S5_NG, S5_P), (S5_NG, S5_P)] + [(S5_NG * S5_GS, S5_P)] * 3
    return pl.pallas_call(
        body, name=name, in_specs=[_whole(a) for a in args], out_specs=[pl.BlockSpec(s, lambda: (0, 0)) for s in shapes],
        out_shape=[jax.ShapeDtypeStruct(s, F32) for s in shapes],
        compiler_params=pltpu.CompilerParams(vmem_limit_bytes=VMEM_LIMIT),
    )(*args)


def s5_param_bwd(args, cts, *, name):
    def body(*refs):
        fn = lambda *a: _s5_param_f(*a, cdot_left)
        _, vjp = jax.vjp(fn, *[r[...] for r in refs[:6]])
        grads = vjp(tuple(r[...] for r in refs[6:11]))
        for o, v in zip(refs[11:], grads):
            o[...] = v

    return pl.pallas_call(
        body, name=name, in_specs=[_whole(a) for a in list(args) + list(cts)],
        out_specs=[_whole(a) for a in args], out_shape=[jax.ShapeDtypeStruct(a.shape, F32) for a in args],
        compiler_params=pltpu.CompilerParams(vmem_limit_bytes=VMEM_LIMIT),
    )(*args, *cts)


_S5_W = S5_BLK * S5_P


def _cmul_add(xr, xi, pr, pi, sr, si):
    return xr + (pr * sr - pi * si), xi + (pr * si + pi * sr)


def _s5_powers(ar, ai):
    pw = [(ar, ai)]
    for _ in range(7):
        qr, qi = pw[-1]
        pw.append((qr * ar - qi * ai, qr * ai + qi * ar))
    return pw


def s5_scan_fwd(bu, a_re, a_im, *, name):
    T = bu.shape[0]
    bt = min(T, 256)
    nb = T // bt

    def body(bu_ref, ar_ref, ai_ref, x_ref, carry_ref):
        @pl.when(pl.program_id(1) == 0)
        def _():
            carry_ref[...] = jnp.zeros_like(carry_ref)

        ar, ai = ar_ref[...], ai_ref[...]
        pw = _s5_powers(ar, ai)
        pwr = jnp.concatenate([p[0] for p in pw], axis=0)
        pwi = jnp.concatenate([p[1] for p in pw], axis=0)
        rin = _iota2((8, _S5_W), 0)
        cr, ci = carry_ref[0:1, :], carry_ref[1:2, :]
        for t in range(bt // 8):
            sl = slice(8 * t, 8 * t + 8)
            xr, xi = bu_ref[sl, :_S5_W], bu_ref[sl, _S5_W:]
            for s in (1, 2, 4):
                m = rin >= s
                sr = jnp.where(m, pltpu.roll(xr, s, 0), 0.0)
                si = jnp.where(m, pltpu.roll(xi, s, 0), 0.0)
                xr, xi = _cmul_add(xr, xi, *pw[s - 1], sr, si)
            xr, xi = _cmul_add(xr, xi, pwr, pwi, cr, ci)
            x_ref[sl, :_S5_W] = xr
            x_ref[sl, _S5_W:] = xi
            cr, ci = xr[7:8, :], xi[7:8, :]
        carry_ref[0:1, :] = cr
        carry_ref[1:2, :] = ci

    blk = pl.BlockSpec((bt, 2 * _S5_W), lambda g, t: (t, g))
    a_s = pl.BlockSpec((None, 1, _S5_W), lambda g, t: (g, 0, 0))
    return pl.pallas_call(
        body, name=name, grid=(S5_NG // S5_BLK, nb), in_specs=[blk, a_s, a_s], out_specs=blk,
        out_shape=jax.ShapeDtypeStruct(bu.shape, F32), scratch_shapes=[pltpu.VMEM((8, _S5_W), F32)],
        compiler_params=_cparams(2),
    )(bu, a_re, a_im)


def s5_scan_bwd(dx, x, a_re, a_im, *, name):
    T = dx.shape[0]
    bt = min(T, 256)
    nb = T // bt

    def body(g_ref, x_ref, ar_ref, ai_ref, lam_ref, dar_ref, dai_ref, carry_ref):
        @pl.when(pl.program_id(1) == 0)
        def _():
            carry_ref[...] = jnp.zeros_like(carry_ref)
            dar_ref[...] = jnp.zeros_like(dar_ref)
            dai_ref[...] = jnp.zeros_like(dai_ref)

        pw = _s5_powers(ar_ref[...], -ai_ref[...])
        pwr = jnp.concatenate([p[0] for p in reversed(pw)], axis=0)
        pwi = jnp.concatenate([p[1] for p in reversed(pw)], axis=0)
        rin = _iota2((8, _S5_W), 0)
        cr, ci = carry_ref[0:1, :], carry_ref[1:2, :]
        acc_r = jnp.zeros((8, _S5_W), F32)
        acc_i = jnp.zeros((8, _S5_W), F32)
        for t in reversed(range(bt // 8)):
            sl = slice(8 * t, 8 * t + 8)
            lr, li = g_ref[sl, :_S5_W], g_ref[sl, _S5_W:]
            for s in (1, 2, 4):
                m = rin < 8 - s
                sr = jnp.where(m, pltpu.roll(lr, 8 - s, 0), 0.0)
                si = jnp.where(m, pltpu.roll(li, 8 - s, 0), 0.0)
                lr, li = _cmul_add(lr, li, *pw[s - 1], sr, si)
            lr, li = _cmul_add(lr, li, pwr, pwi, cr, ci)
            lam_ref[sl, :_S5_W] = lr
            lam_ref[sl, _S5_W:] = li
            nr = jnp.where(rin == 7, cr, pltpu.roll(lr, 7, 0))
            ni = jnp.where(rin == 7, ci, pltpu.roll(li, 7, 0))
            xr, xi = x_ref[sl, :_S5_W], x_ref[sl, _S5_W:]
            acc_r = acc_r + (xr * nr + xi * ni)
            acc_i = acc_i + (xr * ni - xi * nr)
            cr, ci = lr[0:1, :], li[0:1, :]
        carry_ref[0:1, :] = cr
        carry_ref[1:2, :] = ci
        dar_ref[...] += jnp.sum(acc_r, axis=0, keepdims=True)
        dai_ref[...] += jnp.sum(acc_i, axis=0, keepdims=True)

    blk = pl.BlockSpec((bt, 2 * _S5_W), lambda g, t: (nb - 1 - t, g))
    a_s = pl.BlockSpec((None, 1, _S5_W), lambda g, t: (g, 0, 0))
    nblk = S5_NG // S5_BLK
    return pl.pallas_call(
        body, name=name, grid=(nblk, nb), in_specs=[blk, blk, a_s, a_s], out_specs=[blk, a_s, a_s],
        out_shape=[jax.ShapeDtypeStruct(dx.shape, F32), jax.ShapeDtypeStruct((nblk, 1, _S5_W), F32),
                   jax.ShapeDtypeStruct((nblk, 1, _S5_W), F32)],
        scratch_shapes=[pltpu.VMEM((8, _S5_W), F32)], compiler_params=_cparams(2),
    )(dx, x, a_re, a_im)


def _norm_bf16(h, g, name):
    return rowwise(f_rmsnorm, [g], [h], [(D, BF16)], bt=512, name=name)[0]


def _norm_bwd(h, g, cts, name):
    n = len(cts) - 1

    def f(p, r):
        y = _rms(r[0], p[0])
        return (y,) * n + (r[0],)

    (dg,), (dh,) = rowwise_vjp(f, [g], [h], cts, [F32], bt=256, name=name)
    return dh, dg


def ffn_fwd(h, g, w_gu, w_down, tag):
    hn = _norm_bf16(h, g, f"{tag}_norm")
    a, gu = ffn_up(hn, w_gu, name=f"{tag}_up")
    h2 = matmul(a, w_down[None], "nn", add=h, name=f"{tag}_down")
    return h2, (h, hn, gu, a)


def ffn_bwd(d, saved, g, w_gu, w_down, tag):
    h, hn, gu, a = saved
    dgu = ffn_dact(d, w_down, gu, name=f"{tag}_dact")
    dwd = matmul(a, d, "tn", name=f"{tag}_dwd")[0]
    dwgu = matmul(hn, dgu, "tn", name=f"{tag}_dwgu")[0]
    dhn = matmul(dgu, w_gu[None], "nt", name=f"{tag}_dhn")
    dh, dg = _norm_bwd(h, g, [dhn, d], f"{tag}_dnorm")
    return dh, dg, dwgu, dwd


_GLA_NC = 4
_SSD_NU = 4


def gla_fwd(h, gm, w_in, w_a2, b_a, ng, w_out, tag):
    hn = _norm_bf16(h, gm, f"{tag}_norm")
    proj = matmul(hn, w_in[None], "nn", name=f"{tag}_in")
    alow = (proj, LANE, 2 * (GLA_QK + GLA_VD) // LANE)
    la = rowwise(f_gla_gate_in_fwd, [w_a2, b_a], [alow], [(GLA_QK, F32)], bt=512, name=f"{tag}_gate")[0]
    o, ss = gla_scan_fwd(proj, la, nc=_GLA_NC, name=f"{tag}_scan")
    r = (proj, GLA_VD, 2)
    og = rowwise(f_gla_out, [ng], [o, r], [(GLA_VD, BF16)], bt=256, name=f"{tag}_out")[0]
    h2 = matmul(og, w_out[None], "nn", add=h, name=f"{tag}_proj")
    return h2, (h, hn, proj, la, o, ss, og)


def gla_bwd(d, saved, gm, w_in, w_a2, b_a, ng, w_out, tag):
    h, hn, proj, la, o, ss, og = saved
    dog = matmul(d, w_out[None], "nt", name=f"{tag}_dog")
    dwout = matmul(og, d, "tn", name=f"{tag}_dwout")[0]
    r = (proj, GLA_VD, 2)
    (dng,), (do, dr) = rowwise_vjp(f_gla_out, [ng], [o, r], [dog], [F32, BF16], bt=128, name=f"{tag}_dout")
    dq, dk, dv, dla = gla_scan_bwd(proj, la, ss, do, nc=_GLA_NC, name=f"{tag}_dscan")
    alow = (proj, LANE, 2 * (GLA_QK + GLA_VD) // LANE)
    (dwa2, dba), (dalow,) = rowwise_vjp(f_gla_gate_in, [w_a2, b_a], [alow], [dla], [BF16], bt=512, name=f"{tag}_dgate")
    dproj = jnp.concatenate([dq, dk, dv, dr, dalow], axis=1)
    dwin = matmul(hn, dproj, "tn", name=f"{tag}_dwin")[0]
    dhn = matmul(dproj, w_in[None], "nt", name=f"{tag}_dhn")
    dh, dgm = _norm_bwd(h, gm, [dhn, d], f"{tag}_dnorm")
    return dh, dgm, dwin, dwa2[:GLA_RANK], dba, dng, dwout


def ssd_fwd(h, gm, w_in, conv_w, conv_b, dtb, alog, dsk, ng, w_out, tag):
    hn = _norm_bf16(h, gm, f"{tag}_norm")
    proj = matmul(hn, w_in[None], "nn", name=f"{tag}_in")
    xbc = ssd_conv_fwd(proj, conv_w, conv_b, name=f"{tag}_conv")
    y, hs = ssd_scan_fwd(xbc, proj, dtb, alog, dsk, nu=_SSD_NU, name=f"{tag}_scan")
    z = (proj, SSD_DI, 0)
    yg = rowwise(f_ssd_out, [ng], [y, z], [(SSD_DI, BF16)], bt=256, name=f"{tag}_out")[0]
    h2 = matmul(yg, w_out[None], "nn", add=h, name=f"{tag}_proj")
    return h2, (h, hn, proj, xbc, y, hs, yg)


def ssd_bwd(d, saved, gm, w_in, conv_w, conv_b, dtb, alog, dsk, ng, w_out, tag):
    h, hn, proj, xbc, y, hs, yg = saved
    dyg = matmul(d, w_out[None], "nt", name=f"{tag}_dyg")
    dwout = matmul(yg, d, "tn", name=f"{tag}_dwout")[0]
    z = (proj, SSD_DI, 0)
    (dng,), (dy, dz) = rowwise_vjp(f_ssd_out, [ng], [y, z], [dyg], [F32, BF16], bt=128, name=f"{tag}_dout")
    dxs, dbm, dcm, ddt, ddtb, dal, ddsk = ssd_scan_bwd(xbc, proj, dtb, alog, dsk, hs, dy, nu=_SSD_NU, name=f"{tag}_dscan")
    dxbc = jnp.concatenate([dxs, dbm, dcm], axis=1)
    dpre, dcw, dcb = ssd_conv_bwd(proj, conv_w, conv_b, dxbc, name=f"{tag}_dconv")
    dproj = jnp.concatenate([dz, dpre, ddt.astype(BF16)], axis=1)
    dwin = matmul(hn, dproj, "tn", name=f"{tag}_dwin")[0]
    dhn = matmul(dproj, w_in[None], "nt", name=f"{tag}_dhn")
    dh, dgm = _norm_bwd(h, gm, [dhn, d], f"{tag}_dnorm")
    return (dh, dgm, dwin, dcw, dcb, ddtb[:, :SSD_H], dal[:, :SSD_H], ddsk[:, :SSD_H], dng, dwout)


_S5_NB = S5_NG // S5_BLK


def _s5_param_args(log_dt, a_re, a_im, b_re, b_im, c_im):
    n = S5_NG * S5_GS
    tr = lambda b: jnp.transpose(b, (0, 2, 1)).reshape(n, S5_P)
    return [log_dt.reshape(S5_NG, 1), a_re, a_im, tr(b_re), tr(b_im), c_im.reshape(n, S5_P)]


def _s5_blockdiag(t):
    nb, gl, a, b = t.shape
    eye = jnp.eye(gl, dtype=t.dtype)
    return (t[:, :, :, None, :] * eye[None, :, None, :, None]).reshape(nb, gl * a, gl * b)


def _s5_diag(t, a, b):
    nb = t.shape[0]
    gl = t.shape[1] // a
    eye = jnp.eye(gl, dtype=t.dtype)
    return jnp.sum(t.reshape(nb, gl, a, gl, b) * eye[None, :, None, :, None], axis=3)


def _s5_weights(bbr, bbi, c_re, cneg):
    sh = (_S5_NB, S5_BLK, S5_GS, S5_P)
    wb = jnp.concatenate([_s5_blockdiag(bbr.reshape(sh)), _s5_blockdiag(bbi.reshape(sh))], axis=2)
    tr = lambda cc: jnp.transpose(cc.reshape(sh), (0, 1, 3, 2))
    wc = jnp.concatenate([_s5_blockdiag(tr(c_re)), _s5_blockdiag(tr(cneg))], axis=1)
    return wb, wc


def s5_fwd(h, gm, prm, dsk, w_glu, tag):
    log_dt, a_re, a_im, b_re, b_im, c_re, c_im = prm
    hn = rowwise(f_rmsnorm, [gm], [h], [(D, F32)], bt=512, name=f"{tag}_norm")[0]
    pargs = _s5_param_args(log_dt, a_re, a_im, b_re, b_im, c_im)
    abr, abi, bbr, bbi, cneg = s5_param_fwd(pargs, name=f"{tag}_param")
    wb, wc = _s5_weights(bbr, bbi, c_re.reshape(S5_NG * S5_GS, S5_P), cneg)
    ar, ai = abr.reshape(_S5_NB, 1, _S5_W), abi.reshape(_S5_NB, 1, _S5_W)
    bu = matmul(hn, wb, "nn", G=_S5_NB, name=f"{tag}_bu")
    x = s5_scan_fwd(bu, ar, ai, name=f"{tag}_scan")
    ycp = matmul(x, wc, "nn", G=_S5_NB, name=f"{tag}_cx")
    yg = rowwise(f_s5_act, [dsk], [ycp, hn], [(D, BF16)], bt=512, name=f"{tag}_act")[0]
    vg = matmul(yg, w_glu[None], "nn", name=f"{tag}_glu")
    h2 = rowwise(f_glu_res, [], [vg, h], [(D, F32)], bt=512, name=f"{tag}_out")[0]
    return h2, (h, hn, pargs, wb, wc, ar, ai, x, ycp, yg, vg)


def s5_bwd(d, saved, gm, dsk, w_glu, tag):
    h, hn, pargs, wb, wc, ar, ai, x, ycp, yg, vg = saved
    _, (dvg,) = rowwise_vjp(f_glu, [], [vg], [d], [BF16], bt=256, name=f"{tag}_dout")
    dwglu = matmul(yg, dvg, "tn", name=f"{tag}_dwglu")[0]
    dyg = matmul(dvg, w_glu[None], "nt", name=f"{tag}_dyg")
    (ddsk,), (dycp, dhn1) = rowwise_vjp(f_s5_act, [dsk], [ycp, hn], [dyg], [F32, F32], bt=256, name=f"{tag}_dact")
    dx = matmul(dycp, wc, "nt", G=_S5_NB, name=f"{tag}_dx")
    dwc = matmul(x, dycp, "tn", G=_S5_NB, name=f"{tag}_dwc")
    lam, dar, dai = s5_scan_bwd(dx, x, ar, ai, name=f"{tag}_dscan")
    dwb = matmul(hn, lam, "tn", G=_S5_NB, name=f"{tag}_dwb")
    dhn2 = matmul(lam, wb, "nt", G=_S5_NB, name=f"{tag}_dhn")
    dh, dgm = _norm_bwd(h, gm, [dhn1, dhn2, d], f"{tag}_dnorm")
    n = S5_NG * S5_GS
    half = S5_BLK * S5_P
    d_bbr = _s5_diag(dwb[:, :, :half], S5_GS, S5_P).reshape(n, S5_P)
    d_bbi = _s5_diag(dwb[:, :, half:], S5_GS, S5_P).reshape(n, S5_P)
    from_c = lambda t: jnp.transpose(_s5_diag(t, S5_P, S5_GS), (0, 1, 3, 2)).reshape(n, S5_P)
    d_cre = from_c(dwc[:, :half, :])
    d_cneg = from_c(dwc[:, half:, :])
    cts = [dar.reshape(S5_NG, S5_P), dai.reshape(S5_NG, S5_P), d_bbr, d_bbi, d_cneg]
    dlog, dare, daim, dbre_t, dbim_t, dcim = s5_param_bwd(pargs, cts, name=f"{tag}_dparam")
    untr = lambda t: jnp.transpose(t.reshape(S5_NG, S5_GS, S5_P), (0, 2, 1))
    grads = (dlog.reshape(S5_NG), dare, daim, untr(dbre_t), untr(dbim_t),
             d_cre.reshape(S5_NG, S5_GS, S5_P), dcim.reshape(S5_NG, S5_GS, S5_P))
    return dh, dgm, grads, ddsk, dwglu


def _pad_last(w, n):
    return jnp.pad(w, [(0, 0)] * (w.ndim - 1) + [(0, n - w.shape[-1])])


_BIG = ("gla_w_in", "gla_w_out", "ssd_w_in", "ssd_w_out", "s5_w_glu", "ffn_w_gu", "ffn_w_down")


def interleave_gu(w):
    q = w.shape[-1] // 4
    return jnp.concatenate([w[..., :q], w[..., 2 * q:3 * q], w[..., q:2 * q], w[..., 3 * q:]], axis=-1)


def local_step(x, target, W, later_weights=None, later_grads=None):
    f32 = lambda a: a.astype(F32)
    row = lambda a: f32(a).reshape(1, -1)

    def layer_args(i):
        m, j = i % 3, i // 3
        gm = row(W["norm_mix_g"][i])
        if m == 0:
            args = (gm, W["gla_w_in"][j], jnp.pad(f32(W["gla_w_a2"][j]), ((0, LANE - GLA_RANK), (0, 0))),
                    row(W["gla_b_a"][j]), row(W["gla_norm_g"][j]), W["gla_w_out"][j])
        elif m == 1:
            pl_ = lambda a: _pad_last(row(a), LANE)
            args = (gm, W["ssd_w_in"][j], f32(W["ssd_conv_w"][j]),
                    row(W["ssd_conv_b"][j]), pl_(W["ssd_dt_bias"][j]), pl_(W["ssd_a_log"][j]), pl_(W["ssd_d"][j]),
                    row(W["ssd_norm_g"][j]), W["ssd_w_out"][j])
        else:
            prm = tuple(f32(W[k][j]) for k in ("s5_log_dt", "s5_a_re", "s5_a_im", "s5_b_re", "s5_b_im", "s5_c_re", "s5_c_im"))
            args = (gm, prm, row(W["s5_d"][j]), W["s5_w_glu"][j])
        return (m, j, args), (row(W["norm_ffn_g"][i]), W["ffn_w_gu"][i], W["ffn_w_down"][i])

    h = x
    saved, mixers, ffns = [], [], []
    for i in range(DEPTH):
        mixer, ffn = layer_args(i)
        mixers.append(mixer)
        ffns.append(ffn)
        m, j, args = mixer
        tag = f"l{i}_{('gla', 'ssd', 's5')[m]}"
        h, sm = (gla_fwd, ssd_fwd, s5_fwd)[m](h, *args, tag)
        h, sf = ffn_fwd(h, *ffn, f"l{i}_ffn")
        saved.append((sm, sf))
        if i == 0 and later_weights is not None:
            W = {**W, **later_weights(h)}
    loss, dfg, d = loss_head(h, row(W["final_norm_g"]), target, name="loss_head")

    G = {k: [None] * len(v) for k, v in W.items() if k != "final_norm_g"}
    G["final_norm_g"] = dfg.reshape(D)
    for i in reversed(range(DEPTH)):
        m, j, args = mixers[i]
        sm, sf = saved[i]
        if i == 0 and later_grads is not None:
            zero = later_grads(G)
            ffns[0] = (ffns[0][0], ffns[0][1], ffns[0][2] + zero.astype(ffns[0][2].dtype))
        d, dg, dwgu, dwd = ffn_bwd(d, sf, *ffns[i], f"l{i}_ffn")
        G["norm_ffn_g"][i], G["ffn_w_gu"][i], G["ffn_w_down"][i] = dg.reshape(D), dwgu, dwd
        tag = f"l{i}_{('gla', 'ssd', 's5')[m]}"
        if m == 0:
            d, dgm, dwin, dwa2, dba, dng, dwout = gla_bwd(d, sm, *args, tag)
            G["gla_w_in"][j], G["gla_w_a2"][j], G["gla_b_a"][j] = dwin, dwa2, dba.reshape(-1)
            G["gla_norm_g"][j], G["gla_w_out"][j] = dng.reshape(-1), dwout
        elif m == 1:
            d, dgm, dwin, dcw, dcb, ddtb, dal, ddsk, dng, dwout = ssd_bwd(d, sm, *args, tag)
            G["ssd_w_in"][j], G["ssd_conv_w"][j], G["ssd_conv_b"][j] = dwin, dcw, dcb.reshape(-1)
            G["ssd_dt_bias"][j], G["ssd_a_log"][j], G["ssd_d"][j] = ddtb.reshape(-1), dal.reshape(-1), ddsk.reshape(-1)
            G["ssd_norm_g"][j], G["ssd_w_out"][j] = dng.reshape(-1), dwout
        else:
            d, dgm, pg, ddsk, dwglu = s5_bwd(d, sm, args[0], args[2], args[3], tag)
            for k, v in zip(("s5_log_dt", "s5_a_re", "s5_a_im", "s5_b_re", "s5_b_im", "s5_c_re", "s5_c_im"), pg):
                G[k][j] = v
            G["s5_d"][j], G["s5_w_glu"][j] = ddsk.reshape(-1), dwglu
        G["norm_mix_g"][i] = dgm.reshape(D)
    grads = {k: (v if k == "final_norm_g" or k in _BIG else jnp.stack(v)) for k, v in G.items()}
    return loss, d, grads


_MESH = pl.DeviceIdType.MESH
_ANY = pl.BlockSpec(memory_space=pl.ANY)
_DMA = pltpu.SemaphoreType.DMA
_ROWS_ALIGN = 1024


def _place():
    return lax.axis_index("x"), lax.axis_index("y"), lax.axis_index("c")


def _other_chips(x, y):
    return [(1 - x, y), (x, 1 - y), (1 - x, 1 - y)]


def _remote(src, dst, send_sems, recv_sems, k, to):
    return pltpu.make_async_remote_copy(src_ref=src, dst_ref=dst, send_sem=send_sems.at[k], recv_sem=recv_sems.at[k],
                                        device_id=to, device_id_type=_MESH)


def gather_shards(loc, *, name):
    def body(in_ref, out_ref, send_sems, recv_sems, local_sem):
        x, y, c = _place()
        me, sibling = (x, y, c), (x, y, 1 - c)
        chips = _other_chips(x, y)

        def half(px, py, hc):
            return out_ref.at[2 * px + py, hc]

        mine = pltpu.make_async_copy(in_ref, out_ref.at[2 * x + y], local_sem)
        mine.start()
        first = [_remote(in_ref.at[c], half(x, y, c), send_sems, recv_sems, j, (*chip, c)) for j, chip in enumerate(chips)]
        for cp in first:
            cp.start()
        passed = [_remote(half(*chip, c), half(*chip, c), send_sems, recv_sems, 3 + j, sibling) for j, chip in enumerate(chips)]
        for j, chip in enumerate(chips):
            _remote(in_ref.at[c], half(*chip, c), send_sems, recv_sems, j, me).wait_recv()
            passed[j].start()
        for j, chip in enumerate(chips):
            _remote(in_ref.at[c], half(*chip, 1 - c), send_sems, recv_sems, 3 + j, me).wait_recv()
        for cp in first + passed:
            cp.wait_send()
        mine.wait()

    return pl.pallas_call(
        body, name=name, in_specs=[_ANY], out_specs=_ANY,
        out_shape=jax.ShapeDtypeStruct((4,) + loc.shape, loc.dtype),
        scratch_shapes=[_DMA((6,)), _DMA((6,)), _DMA(())],
    )(loc)


def _pos(px, py, perm):
    return 2 * py + px if perm else 2 * px + py


def _part(ref, kind, p, loc):
    if kind == "lead":
        return ref.at[p]
    return ref.at[:, pl.ds(pl.multiple_of(p * loc, LANE), loc)]


def _rows(ref, h, hr):
    return ref.at[pl.ds(h * hr, hr)]


def _rows_block(hr, width):
    return max(b for b in range(16, hr + 1, 16) if hr % b == 0 and (b * width <= (1 << 19) or b == 16))


def gather_big(locs, kinds, *, name):
    n = len(locs)

    def body(*refs):
        ins, outs = refs[:n], refs[n:2 * n]
        send_sems, recv_sems = refs[2 * n:]
        x, y, c = _place()
        me, sibling = (x, y, c), (x, y, 1 - c)
        chips = _other_chips(x, y)

        def half(i, px, py, h):
            (kind, perm), (rows, loc) = kinds[i], locs[i].shape
            return _rows(_part(outs[i], kind, _pos(px, py, perm), loc), h, rows // 2)

        sends = []
        for i in range(n):
            (kind, perm), (rows, loc) = kinds[i], locs[i].shape
            own = _part(outs[i], kind, _pos(x, y, perm), loc)
            sends.append(_remote(ins[i], own, send_sems, recv_sems, 6 * n + i, sibling))
            sends[-1].start()
            for j, chip in enumerate(chips):
                sends.append(_remote(_rows(ins[i], c, rows // 2), half(i, x, y, c), send_sems, recv_sems, 6 * i + j, (*chip, c)))
                sends[-1].start()
        for i in range(n):
            hr = locs[i].shape[0] // 2
            for j, chip in enumerate(chips):
                _remote(_rows(ins[i], c, hr), half(i, *chip, c), send_sems, recv_sems, 6 * i + j, me).wait_recv()
                sends.append(_remote(half(i, *chip, c), half(i, *chip, c), send_sems, recv_sems, 6 * i + 3 + j, sibling))
                sends[-1].start()
        for i in range(n):
            (kind, perm), (rows, loc) = kinds[i], locs[i].shape
            for j, chip in enumerate(chips):
                _remote(_rows(ins[i], c, rows // 2), half(i, *chip, 1 - c), send_sems, recv_sems, 6 * i + 3 + j, me).wait_recv()
            _remote(ins[i], _part(outs[i], kind, _pos(x, y, perm), loc), send_sems, recv_sems, 6 * n + i, me).wait_recv()
        for cp in sends:
            cp.wait_send()

    def out_shape(a, kind):
        rows, loc = a.shape
        return jax.ShapeDtypeStruct((4, rows, loc) if kind == "lead" else (rows, 4 * loc), a.dtype)

    return pl.pallas_call(
        body, name=name, in_specs=[_ANY] * n, out_specs=[_ANY] * n,
        out_shape=[out_shape(a, k[0]) for a, k in zip(locs, kinds)],
        scratch_shapes=[_DMA((7 * n,)), _DMA((7 * n,))],
    )(*locs)


_HBM = pl.BlockSpec(memory_space=pltpu.HBM)
_SEM = pl.BlockSpec(memory_space=pltpu.SEMAPHORE)
_EFFECT = pltpu.SideEffectType.DATAFLOW_SIDE_EFFECTING


def _in_hbm(a):
    return pltpu.with_memory_space_constraint(a, pltpu.HBM)


def _gather_ici_copies(ins, lands, kinds, shapes, send_sems, recv_sems):
    x, y, c = _place()
    sends, arrivals = [], []
    for i, ((kind, perm), (rows, loc)) in enumerate(zip(kinds, shapes)):
        hr = rows // 2
        mine = _part(lands[i], kind, _pos(x, y, perm), loc)
        sends.append(_remote(ins[i], mine, send_sems, recv_sems, 4 * i + 3, (x, y, 1 - c)))
        arrivals.append(_remote(ins[i], mine, send_sems, recv_sems, 4 * i + 3, (x, y, c)))
        for j, (px, py) in enumerate(_other_chips(x, y)):
            sends.append(_remote(_rows(ins[i], c, hr), _rows(mine, c, hr), send_sems, recv_sems, 4 * i + j, (px, py, c)))
            theirs = _rows(_part(lands[i], kind, _pos(px, py, perm), loc), c, hr)
            arrivals.append(_remote(_rows(ins[i], c, hr), theirs, send_sems, recv_sems, 4 * i + j, (x, y, c)))
    return sends, arrivals


def gather_start(locs, kinds, *, name):
    n = len(locs)
    shapes = [a.shape for a in locs]

    def land_shape(a, kind):
        rows, loc = a.shape
        return (4, rows, loc) if kind == "lead" else (rows, 4 * loc)

    def body(*refs):
        sends, _ = _gather_ici_copies(refs[:n], refs[n:2 * n], kinds, shapes, refs[2 * n], refs[2 * n + 1])
        for cp in sends:
            cp.start()
        refs[-1][...] = jnp.zeros_like(refs[-1])

    lands = [lax.empty(land_shape(a, k[0]), a.dtype) for a, k in zip(locs, kinds)]
    outs = pl.pallas_call(
        body, name=name, in_specs=[_HBM] * (2 * n), out_specs=[_SEM, _SEM] + [_HBM] * (2 * n) + [pl.BlockSpec(memory_space=pltpu.VMEM)],
        out_shape=[_DMA((4 * n,)), _DMA((4 * n,))] + [pltpu.HBM(a.shape, a.dtype) for a in locs]
        + [pltpu.HBM(l.shape, l.dtype) for l in lands] + [jax.ShapeDtypeStruct((8, LANE), F32)],
        input_output_aliases={i: 2 + i for i in range(2 * n)},
        compiler_params=pltpu.CompilerParams(has_side_effects=_EFFECT),
    )(*[_in_hbm(a) for a in locs], *[_in_hbm(l) for l in lands])
    return outs[0], outs[1], list(outs[2:2 + n]), list(outs[2 + n:2 + 2 * n]), outs[-1][0, 0]


def gather_wait(send_sems, recv_sems, locs, lands, kinds, after, *, name):
    n = len(locs)
    shapes = [a.shape for a in locs]

    def body(*refs):
        sends, arrivals = _gather_ici_copies(refs[:n], refs[n:2 * n], kinds, shapes, refs[2 * n], refs[2 * n + 1])
        for cp in sends:
            cp.wait_send()
        for cp in arrivals:
            cp.wait_recv()

    outs = pl.pallas_call(
        body, name=name, in_specs=[_HBM] * (2 * n) + [_SEM, _SEM, _ANY], out_specs=[_HBM] * (2 * n),
        out_shape=[pltpu.HBM(a.shape, a.dtype) for a in locs] + [pltpu.HBM(l.shape, l.dtype) for l in lands],
        input_output_aliases={i: i for i in range(2 * n)},
        compiler_params=pltpu.CompilerParams(has_side_effects=_EFFECT),
    )(*locs, *lands, send_sems, recv_sems, after)
    return list(outs[n:])


def gather_finish(lands, kinds, shapes, *, name):
    n = len(lands)

    def body(*refs):
        bufs = refs[n:2 * n]
        send_sems, recv_sems = refs[2 * n:]
        x, y, c = _place()
        sends = []
        for i, ((kind, perm), (rows, loc)) in enumerate(zip(kinds, shapes)):
            for j, (px, py) in enumerate(_other_chips(x, y)):
                part = _part(bufs[i], kind, _pos(px, py, perm), loc)
                sends.append(_remote(_rows(part, c, rows // 2), _rows(part, c, rows // 2), send_sems, recv_sems, 3 * i + j, (x, y, 1 - c)))
                sends[-1].start()
        for i, ((kind, perm), (rows, loc)) in enumerate(zip(kinds, shapes)):
            for j, (px, py) in enumerate(_other_chips(x, y)):
                part = _part(bufs[i], kind, _pos(px, py, perm), loc)
                _remote(_rows(part, c, rows // 2), _rows(part, 1 - c, rows // 2), send_sems, recv_sems, 3 * i + j, (x, y, c)).wait_recv()
        for cp in sends:
            cp.wait_send()

    return list(pl.pallas_call(
        body, name=name, in_specs=[_ANY] * n, out_specs=[_ANY] * n,
        out_shape=[jax.ShapeDtypeStruct(l.shape, l.dtype) for l in lands],
        input_output_aliases={i: i for i in range(n)}, scratch_shapes=[_DMA((3 * n,)), _DMA((3 * n,))],
    )(*lands))


def _scatter_copies(qs, lands, kinds, locs, send_sems, recv_sems):
    x, y, c = _place()
    sends, arrivals = [], []
    for i, (kind, perm) in enumerate(kinds):
        for j, (px, py) in enumerate(_other_chips(x, y)):
            src = _part(qs[i], kind, _pos(px, py, perm), locs[i])
            sends.append(_remote(src, lands[i].at[j], send_sems, recv_sems, 3 * i + j, (px, py, c)))
            arrivals.append(_remote(src, lands[i].at[j], send_sems, recv_sems, 3 * i + j, (x, y, c)))
    return sends, arrivals


def _scatter_land(q, kind, loc):
    return (3, q.shape[1] if kind == "lead" else q.shape[0], loc)


def scatter_start(qs, kinds, locs, *, name):
    n = len(qs)

    def body(*refs):
        sends, _ = _scatter_copies(refs[:n], refs[n:2 * n], kinds, locs, refs[2 * n], refs[2 * n + 1])
        for cp in sends:
            cp.start()
        refs[-1][...] = jnp.zeros_like(refs[-1])

    lands = [lax.empty(_scatter_land(q, k[0], l), q.dtype) for q, k, l in zip(qs, kinds, locs)]
    outs = pl.pallas_call(
        body, name=name, in_specs=[_HBM] * (2 * n), out_specs=[_SEM, _SEM] + [_HBM] * (2 * n) + [pl.BlockSpec(memory_space=pltpu.VMEM)],
        out_shape=[_DMA((3 * n,)), _DMA((3 * n,))] + [pltpu.HBM(q.shape, q.dtype) for q in qs]
        + [pltpu.HBM(l.shape, l.dtype) for l in lands] + [jax.ShapeDtypeStruct((8, LANE), F32)],
        input_output_aliases={i: 2 + i for i in range(2 * n)},
        compiler_params=pltpu.CompilerParams(has_side_effects=_EFFECT),
    )(*[_in_hbm(q) for q in qs], *[_in_hbm(l) for l in lands])
    return outs[0], outs[1], list(outs[2:2 + n]), list(outs[2 + n:2 + 2 * n]), outs[-1][0, 0]


def scatter_wait(send_sems, recv_sems, qs, lands, kinds, locs, after, *, name):
    n = len(qs)

    def body(*refs):
        sends, arrivals = _scatter_copies(refs[:n], refs[n:2 * n], kinds, locs, refs[2 * n], refs[2 * n + 1])
        for cp in sends:
            cp.wait_send()
        for cp in arrivals:
            cp.wait_recv()

    outs = pl.pallas_call(
        body, name=name, in_specs=[_HBM] * (2 * n) + [_SEM, _SEM, _ANY], out_specs=[_HBM] * (2 * n),
        out_shape=[pltpu.HBM(q.shape, q.dtype) for q in qs] + [pltpu.HBM(l.shape, l.dtype) for l in lands],
        input_output_aliases={i: i for i in range(2 * n)},
        compiler_params=pltpu.CompilerParams(has_side_effects=_EFFECT),
    )(*qs, *lands, send_sems, recv_sems, after)
    return list(outs[:n]), list(outs[n:])


def pair_swap(ps, kinds, *, name):
    n = len(ps)

    def body(*refs):
        ins, outs = refs[:n], refs[n:2 * n]
        send_sems, recv_sems = refs[2 * n:]
        x, y, c = _place()
        cps = []
        for i in range(n):
            if kinds[i][0] == "lead":
                hr = ps[i].shape[1] // 2
                src = ins[i].at[:, pl.ds((1 - c) * hr, hr)]
            else:
                hr = ps[i].shape[0] // 2
                src = _rows(ins[i], 1 - c, hr)
            cps.append(_remote(src, outs[i], send_sems, recv_sems, i, (x, y, 1 - c)))
            cps[-1].start()
        for cp in cps:
            cp.wait()

    def out_shape(a, kind):
        s = a.shape
        return jax.ShapeDtypeStruct((4, s[1] // 2, s[2]) if kind == "lead" else (s[0] // 2, s[1]), a.dtype)

    return pl.pallas_call(
        body, name=name, in_specs=[_ANY] * n, out_specs=[_ANY] * n,
        out_shape=[out_shape(a, k[0]) for a, k in zip(ps, kinds)], scratch_shapes=[_DMA((n,)), _DMA((n,))],
    )(*ps)


def pair_add(p, got, c_arr, kind, *, name):
    if kind == "lead":
        _, hr, cols = got.shape
        br = _rows_block(hr, cols)
        nb = hr // br
        grid = (4, nb)
        p_spec = pl.BlockSpec((None, br, cols), lambda s, i, cr: (s, cr[0] * nb + i, 0))
        g_spec = pl.BlockSpec((None, br, cols), lambda s, i, cr: (s, i, 0))
    else:
        hr, w = got.shape
        br = _rows_block(hr, w)
        nb = hr // br
        grid = (nb,)
        p_spec = pl.BlockSpec((br, w), lambda i, cr: (cr[0] * nb + i, 0))
        g_spec = pl.BlockSpec((br, w), lambda i, cr: (i, 0))

    def body(c_ref, p_ref, g_ref, o_ref):
        o_ref[...] = (p_ref[...] + g_ref[...]).astype(o_ref.dtype)

    return pl.pallas_call(
        body, name=name, out_shape=jax.ShapeDtypeStruct(got.shape, BF16),
        grid_spec=pltpu.PrefetchScalarGridSpec(num_scalar_prefetch=1, grid=grid, in_specs=[p_spec, g_spec], out_specs=g_spec),
        compiler_params=_cparams(len(grid)),
    )(c_arr, p, got)


def chip_scatter(qs, kinds, locs, *, name):
    n = len(qs)

    def body(*refs):
        ins, outs = refs[:n], refs[n:2 * n]
        send_sems, recv_sems = refs[2 * n:]
        x, y, c = _place()
        cps = []
        for i in range(n):
            kind, perm = kinds[i]
            for j, (px, py) in enumerate(_other_chips(x, y)):
                cps.append(_remote(_part(ins[i], kind, _pos(px, py, perm), locs[i]), outs[i].at[j], send_sems, recv_sems,
                                   3 * i + j, (px, py, c)))
                cps[-1].start()
        for cp in cps:
            cp.wait()

    def out_shape(a, kind, loc):
        hr = a.shape[1] if kind == "lead" else a.shape[0]
        return jax.ShapeDtypeStruct((3, hr, loc), a.dtype)

    return pl.pallas_call(
        body, name=name, in_specs=[_ANY] * n, out_specs=[_ANY] * n,
        out_shape=[out_shape(a, k[0], l) for a, k, l in zip(qs, kinds, locs)],
        scratch_shapes=[_DMA((3 * n,)), _DMA((3 * n,))],
    )(*qs)


def chip_add(q, r, pos_arr, c_arr, kind, loc, *, name):
    _, hr, _ = r.shape
    br = _rows_block(hr, loc)
    nb = hr // br
    if kind == "lead":
        q_spec = pl.BlockSpec((None, br, loc), lambda i, pr, cr: (pr[0], i, 0))
    else:
        q_spec = pl.BlockSpec((br, loc), lambda i, pr, cr: (i, pr[0]))
    r_spec = pl.BlockSpec((3, br, loc), lambda i, pr, cr: (0, i, 0))
    o_spec = pl.BlockSpec((br, loc), lambda i, pr, cr: (cr[0] * nb + i, 0))

    def body(p_ref, c_ref, q_ref, r_ref, o_ref):
        acc = q_ref[...].astype(F32)
        for j in range(3):
            acc = acc + r_ref[j].astype(F32)
        o_ref[...] = acc

    return pl.pallas_call(
        body, name=name, out_shape=jax.ShapeDtypeStruct((2 * hr, loc), F32),
        grid_spec=pltpu.PrefetchScalarGridSpec(num_scalar_prefetch=2, grid=(nb,), in_specs=[q_spec, r_spec], out_specs=o_spec),
        compiler_params=_cparams(1),
    )(pos_arr, c_arr, q, r)


def share_rows(fs, *, name):
    n = len(fs)

    def body(*refs):
        bufs = refs[n:2 * n]
        send_sems, recv_sems = refs[2 * n:]
        x, y, c = _place()
        cps = []
        for i in range(n):
            hr = fs[i].shape[0] // 2
            cps.append(_remote(_rows(bufs[i], c, hr), _rows(bufs[i], c, hr), send_sems, recv_sems, i, (x, y, 1 - c)))
            cps[-1].start()
        for i, cp in enumerate(cps):
            hr = fs[i].shape[0] // 2
            _remote(_rows(bufs[i], c, hr), _rows(bufs[i], 1 - c, hr), send_sems, recv_sems, i, (x, y, c)).wait_recv()
            cp.wait_send()

    return pl.pallas_call(
        body, name=name, in_specs=[_ANY] * n, out_specs=[_ANY] * n,
        out_shape=[jax.ShapeDtypeStruct(f.shape, f.dtype) for f in fs],
        input_output_aliases={i: i for i in range(n)}, scratch_shapes=[_DMA((n,)), _DMA((n,))],
    )(*fs)


def gather_all(v, *, name):
    def body(v_ref, out_ref, send_sems, recv_sems, local_sem):
        x, y, c = _place()
        flip = lambda p, m: 1 - p if m else p
        peers = [(flip(x, m & 4), flip(y, m & 2), flip(c, m & 1)) for m in range(1, 8)]
        idx = lambda p: 4 * p[0] + 2 * p[1] + p[2]
        mine = pltpu.make_async_copy(v_ref, out_ref.at[idx((x, y, c))], local_sem)
        mine.start()
        cps = [_remote(v_ref, out_ref.at[idx((x, y, c))], send_sems, recv_sems, k, p) for k, p in enumerate(peers)]
        for cp in cps:
            cp.start()
        for k, p in enumerate(peers):
            _remote(v_ref, out_ref.at[idx(p)], send_sems, recv_sems, k, p).wait_recv()
        for cp in cps:
            cp.wait_send()
        mine.wait()

    return pl.pallas_call(
        body, name=name, in_specs=[_ANY], out_specs=_ANY, out_shape=jax.ShapeDtypeStruct((8,) + v.shape, v.dtype),
        scratch_shapes=[_DMA((7,)), _DMA((7,)), _DMA(())],
    )(v)


def sum_stack(a, extra=None, *, name):
    n, R, L = a.shape
    br = _pick(R, _ROWS_ALIGN, 8)

    def body(*refs):
        a_ref, o_ref = refs[0], refs[-1]
        acc = refs[1][...] if extra is not None else a_ref[0]
        for i in range(0 if extra is not None else 1, n):
            acc = acc + a_ref[i]
        o_ref[...] = acc

    row = pl.BlockSpec((br, L), lambda i: (i, 0))
    specs = [pl.BlockSpec((n, br, L), lambda i: (0, i, 0))] + ([row] if extra is not None else [])
    args = [a] + ([extra] if extra is not None else [])
    return pl.pallas_call(body, name=name, grid=(R // br,), in_specs=specs, out_specs=row,
                          out_shape=jax.ShapeDtypeStruct((R, L), a.dtype), compiler_params=_cparams(1))(*args)


def adamw(w, g, m, v, *, name):
    shape = w.shape
    size = math.prod(shape)
    last = shape[-1]
    if last % LANE != 0 and size % LANE == 0 and size <= (1 << 20):
        last = LANE
    rows = size // last
    budget = (1 << 18) // last
    br = rows
    if rows > budget:
        br = max(c for c in range(8, budget + 1, 8) if rows % c == 0)
    v2 = lambda a: a.reshape(rows, last)

    def body(w_ref, g_ref, m_ref, v_ref, d_ref, nm_ref, nv_ref):
        gg = g_ref[...]
        nm = ADAM_B1 * m_ref[...] + (1.0 - ADAM_B1) * gg
        nv = ADAM_B2 * v_ref[...] + (1.0 - ADAM_B2) * (gg * gg)
        m_hat = nm / (1.0 - ADAM_B1 ** ADAM_STEP)
        v_hat = nv / (1.0 - ADAM_B2 ** ADAM_STEP)
        d_ref[...] = -ADAM_LR * (m_hat / (jnp.sqrt(v_hat) + ADAM_EPS) + ADAM_WD * w_ref[...])
        nm_ref[...] = nm
        nv_ref[...] = nv

    spec = pl.BlockSpec((br, last), lambda i: (i, 0))
    outs = pl.pallas_call(
        body, name=name, grid=(rows // br,), in_specs=[spec] * 4, out_specs=[spec] * 3,
        out_shape=[jax.ShapeDtypeStruct((rows, last), F32)] * 3, compiler_params=_cparams(1),
    )(v2(w), v2(g), v2(m), v2(v))
    return [o.reshape(shape) for o in outs]


_WEIGHTS = ["norm_mix_g", "norm_ffn_g", "gla_w_in", "gla_w_a2", "gla_b_a", "gla_norm_g", "gla_w_out", "ssd_w_in",
            "ssd_conv_w", "ssd_conv_b", "ssd_dt_bias", "ssd_a_log", "ssd_d", "ssd_norm_g", "ssd_w_out", "s5_log_dt",
            "s5_a_re", "s5_a_im", "s5_b_re", "s5_b_im", "s5_c_re", "s5_c_im", "s5_d", "s5_w_glu", "ffn_w_gu",
            "ffn_w_down", "final_norm_g"]
_SHARD_AXIS = {"gla_w_in": 2, "gla_w_a2": 2, "gla_b_a": 1, "gla_norm_g": 1, "gla_w_out": 1, "ssd_w_in": 2,
               "ssd_conv_w": 2, "ssd_w_out": 1, "s5_d": 1, "s5_w_glu": 2, "ffn_w_gu": 2, "ffn_w_down": 1}
_SMALL_SHARDED = [n for n in _WEIGHTS if n in _SHARD_AXIS and n not in _BIG]
_REPLICATED = [n for n in _WEIGHTS if n not in _SHARD_AXIS]
_BIG_KIND = {"gla_w_in": ("lead", False), "gla_w_out": ("lead", False), "ssd_w_in": ("lead", False),
             "ssd_w_out": ("lead", False), "s5_w_glu": ("cols", False), "ffn_w_gu": ("cols", True),
             "ffn_w_down": ("lead", False)}
_PADDED_IN = {"gla_w_in": GLA_INP, "ssd_w_in": SSD_INP}


def _to_rows(flat, parts=1):
    per = -(-flat.shape[0] // (parts * LANE * _ROWS_ALIGN)) * _ROWS_ALIGN
    flat = jnp.pad(flat, (0, parts * per * LANE - flat.shape[0]))
    return flat.reshape(parts, per, LANE)


def _big_layers(local):
    return [(n, j, local[n][j].reshape(-1, local[n].shape[-1])) for n in _BIG for j in range(local[n].shape[0])]


def _in_layer0(n, j):
    return j == 0 and n in ("gla_w_in", "gla_w_out", "ffn_w_gu", "ffn_w_down")


def _assemble(n, g):
    if n in _PADDED_IN:
        return jnp.concatenate([g[s] for s in range(4)] + [jnp.zeros((g.shape[1], _PADDED_IN[n] - 4 * g.shape[2]), BF16)], axis=1)
    if _BIG_KIND[n][0] == "lead":
        return g.reshape(4 * g.shape[1], g.shape[2])
    return g


def _gather_first(local):
    layers = _big_layers(local)
    first = [l for l in layers if _in_layer0(l[0], l[1])]
    later = [l for l in layers if not _in_layer0(l[0], l[1])]
    full = {n: [None] * local[n].shape[0] for n in _BIG}
    got = gather_big([w.astype(BF16) for _, _, w in first], [_BIG_KIND[n] for n, _, _ in first], name="gather_weights_first")
    for (n, j, _), g in zip(first, got):
        full[n][j] = _assemble(n, g)
    flat = jnp.concatenate([local[n].astype(F32).reshape(-1) for n in _SMALL_SHARDED])
    got = gather_shards(_to_rows(flat, 2), name="gather_small_weights").reshape(4, -1)
    off = 0
    for n in _SMALL_SHARDED:
        bs = local[n].shape
        sz = math.prod(bs)
        seg = got[:, off:off + sz].reshape((4,) + bs)
        off += sz
        ax = _SHARD_AXIS[n]
        full[n] = jnp.moveaxis(seg, 0, ax).reshape(bs[:ax] + (4 * bs[ax],) + bs[ax + 1:])
    kinds = [_BIG_KIND[n] for n, _, _ in later]
    send_sems, recv_sems, locs, lands, zero = gather_start([w.astype(BF16) for _, _, w in later], kinds, name="gather_weights_start")
    return full, (later, kinds, send_sems, recv_sems, locs, lands), zero


def _gather_rest(full, pending, after):
    later, kinds, send_sems, recv_sems, locs, lands = pending
    lands = gather_wait(send_sems, recv_sems, locs, lands, kinds, after, name="gather_weights_wait")
    lands = gather_finish(lands, kinds, [w.shape for _, _, w in later], name="gather_weights_finish")
    out = {n: list(full[n]) for n in _BIG}
    for (n, j, _), g in zip(later, lands):
        out[n][j] = _assemble(n, g)
    return out


def _reduce_ops(grads, local, want):
    ops = []
    for n in _BIG:
        kind = _BIG_KIND[n]
        for j, g in enumerate(grads[n]):
            if not want(n, j):
                continue
            loc = local[n].shape[-1] if kind[0] == "cols" or n in _PADDED_IN else g.shape[1]
            if n in _PADDED_IN:
                g = jnp.stack([g[:, s * loc:(s + 1) * loc] for s in range(4)])
            elif kind[0] == "lead":
                g = g.reshape(4, g.shape[0] // 4, g.shape[1])
            ops.append((n, j, kind, loc, g))
    return ops


def _pair_sums(ops, c_arr, tag):
    gots = pair_swap([o[4] for o in ops], [o[2] for o in ops], name=f"reduce_pair_swap_{tag}")
    return [pair_add(o[4], got, c_arr, o[2][0], name=f"reduce_pair_add_{o[0]}{o[1]}") for o, got in zip(ops, gots)]


def _reduce_later_start(grads, local, c):
    ops = _reduce_ops(grads, local, lambda n, j: not _in_layer0(n, j))
    c_arr = jnp.reshape(c, (1,)).astype(jnp.int32)
    qs = _pair_sums(ops, c_arr, "later")
    send_sems, recv_sems, qs, lands, zero = scatter_start(qs, [o[2] for o in ops], [o[3] for o in ops], name="reduce_scatter_start")
    return (ops, send_sems, recv_sems, qs, lands), zero


def _reduce_big(grads, local, pending, after, x, y, c):
    c_arr = jnp.reshape(c, (1,)).astype(jnp.int32)
    ops_l, send_sems, recv_sems, qs_l, lands = pending
    qs_l, rs_l = scatter_wait(send_sems, recv_sems, qs_l, lands, [o[2] for o in ops_l], [o[3] for o in ops_l], after,
                              name="reduce_scatter_wait")
    ops_f = _reduce_ops(grads, local, _in_layer0)
    qs_f = _pair_sums(ops_f, c_arr, "first")
    rs_f = list(chip_scatter(qs_f, [o[2] for o in ops_f], [o[3] for o in ops_f], name="reduce_chip_scatter_first"))
    ops = ops_l + ops_f
    fs = [chip_add(q, r, jnp.reshape(_pos(x, y, o[2][1]), (1,)).astype(jnp.int32), c_arr, o[2][0], o[3],
                   name=f"reduce_chip_add_{o[0]}{o[1]}") for o, q, r in zip(ops, qs_l + qs_f, rs_l + rs_f)]
    outs = share_rows(fs, name="reduce_share")
    red = {(o[0], o[1]): r for o, r in zip(ops, outs)}
    return {n: jnp.stack([red[(n, j)] for j in range(local[n].shape[0])]).reshape(local[n].shape) for n in _BIG}


def _reduce_small(grads, local, x, y):
    names = _REPLICATED + _SMALL_SHARDED
    flat = jnp.concatenate([grads[n].astype(F32).reshape(-1) for n in names])
    n_el = flat.shape[0]
    rows = -(-n_el // (LANE * 8)) * 8
    v = jnp.pad(flat, (0, rows * LANE - n_el)).reshape(rows, LANE)
    red = sum_stack(gather_all(v, name="reduce_small_gather"), name="reduce_small_add").reshape(-1)
    out, off = {}, 0
    for n in names:
        sz = math.prod(grads[n].shape)
        g = red[off:off + sz].reshape(grads[n].shape)
        off += sz
        if n in _SHARD_AXIS:
            ax = _SHARD_AXIS[n]
            loc = local[n].shape[ax]
            g = lax.dynamic_slice_in_dim(g, (2 * x + y) * loc, loc, axis=ax)
        out[n] = g
    return out


def kernel(x, norm_mix_g, norm_ffn_g, gla_w_in, gla_w_a2, gla_b_a, gla_norm_g, gla_w_out, ssd_w_in, ssd_conv_w, ssd_conv_b, ssd_dt_bias, ssd_a_log, ssd_d, ssd_norm_g, ssd_w_out, s5_log_dt, s5_a_re, s5_a_im, s5_b_re, s5_b_im, s5_c_re, s5_c_im, s5_d, s5_w_glu, ffn_w_gu, ffn_w_down, final_norm_g, loss_target, m_norm_mix_g, m_norm_ffn_g, m_gla_w_in, m_gla_w_a2, m_gla_b_a, m_gla_norm_g, m_gla_w_out, m_ssd_w_in, m_ssd_conv_w, m_ssd_conv_b, m_ssd_dt_bias, m_ssd_a_log, m_ssd_d, m_ssd_norm_g, m_ssd_w_out, m_s5_log_dt, m_s5_a_re, m_s5_a_im, m_s5_b_re, m_s5_b_im, m_s5_c_re, m_s5_c_im, m_s5_d, m_s5_w_glu, m_ffn_w_gu, m_ffn_w_down, m_final_norm_g, v_norm_mix_g, v_norm_ffn_g, v_gla_w_in, v_gla_w_a2, v_gla_b_a, v_gla_norm_g, v_gla_w_out, v_ssd_w_in, v_ssd_conv_w, v_ssd_conv_b, v_ssd_dt_bias, v_ssd_a_log, v_ssd_d, v_ssd_norm_g, v_ssd_w_out, v_s5_log_dt, v_s5_a_re, v_s5_a_im, v_s5_b_re, v_s5_b_im, v_s5_c_re, v_s5_c_im, v_s5_d, v_s5_w_glu, v_ffn_w_gu, v_ffn_w_down, v_final_norm_g):
    given = dict(locals())
    local = {n: given[n] for n in _WEIGHTS}
    px, py, pc = _place()

    first, gathering, zero = _gather_first(local)
    full = dict(local)
    full.update(first)
    full["norm_mix_g"] = local["norm_mix_g"] + zero
    reducing = []

    def later_grads(g):
        pending, zero = _reduce_later_start(g, local, pc)
        reducing.append(pending)
        return zero

    loss, grad_x, grads = local_step(x[0], loss_target[0], full, lambda h: _gather_rest(first, gathering, h), later_grads)
    loss = lax.psum(loss, ("x", "y", "c"))

    red = _reduce_big(grads, local, reducing[0], grad_x, px, py, pc)
    red.update(_reduce_small(grads, local, px, py))

    deltas, new_m, new_v = {}, {}, {}
    for n in _WEIGHTS:
        deltas[n], new_m[n], new_v[n] = adamw(local[n], red[n], given["m_" + n], given["v_" + n], name=f"adamw_{n}")
    return (loss, grad_x[None], *[red[n] for n in _WEIGHTS], *[deltas[n] for n in _WEIGHTS],
            *[new_m[n] for n in _WEIGHTS], *[new_v[n] for n in _WEIGHTS])
```

```python
import functools
import math

import jax
import jax.numpy as jnp
from jax import lax
from jax.experimental import pallas as pl
from jax.experimental.pallas import tpu as pltpu

F32 = jnp.float32
BF16 = jnp.bfloat16

D = 1024
DEPTH = 4
CH = 64
EPS = 1e-6
GLA_H, GLA_DK, GLA_DV, GLA_RANK, GLA_TAU = 4, 128, 256, 16, 16.0
GLA_QK = GLA_H * GLA_DK
GLA_VD = GLA_H * GLA_DV
GLA_IN = 2 * GLA_QK + 2 * GLA_VD + GLA_RANK
GLA_INP = 3200
SSD_DI, SSD_HD, SSD_H, SSD_G, SSD_N, SSD_K = 2048, 64, 32, 8, 128, 4
SSD_GN = SSD_G * SSD_N
SSD_CONV = SSD_DI + 2 * SSD_GN
SSD_IN = SSD_DI + SSD_CONV + SSD_H
SSD_INP = 6272
S5_GS, S5_NG, S5_P = 16, 64, 64
S5_BLK = 8
FFN_H = 2816
LANE = 128
VMEM_LIMIT = 52 * 1024 * 1024
_MATMUL_VMEM = 40 * 1024 * 1024

ADAM_LR, ADAM_B1, ADAM_B2, ADAM_EPS, ADAM_WD, ADAM_STEP = 0.001, 0.9, 0.999, 1e-08, 0.01, 10

_ARB = "arbitrary"


def _cparams(n):
    return pltpu.CompilerParams(dimension_semantics=(_ARB,) * n, vmem_limit_bytes=VMEM_LIMIT)


def _pick(n, target, mult=LANE):
    best = None
    for c in range(mult, min(n, target) + 1, mult):
        if n % c == 0:
            best = c
    return best if best is not None else n


_DN = {"nn": (((1,), (0,)), ((), ())), "nt": (((1,), (1,)), ((), ())), "tn": (((0,), (0,)), ((), ()))}


def _dot_raw(a, b, form):
    return lax.dot_general(a.astype(BF16), b.astype(BF16), _DN[form], preferred_element_type=F32)


@functools.partial(jax.custom_vjp, nondiff_argnums=(2,))
def bdot(a, b, form):
    return _dot_raw(a, b, form)


def _bdot_fwd(a, b, form):
    return _dot_raw(a, b, form), (a, b)


def _bdot_bwd(form, res, g):
    a, b = res
    if form == "nn":
        return _dot_raw(g, b, "nt"), _dot_raw(a, g, "tn")
    if form == "nt":
        return _dot_raw(g, b, "nn"), _dot_raw(g, a, "tn")
    return _dot_raw(b, g, "nt"), _dot_raw(a, g, "nn")


bdot.defvjp(_bdot_fwd, _bdot_bwd)


def _hdot(a, b):
    return jnp.dot(a, b, precision=lax.Precision.HIGHEST, preferred_element_type=F32)


@jax.custom_vjp
def cdot_left(c, ct, x):
    return _hdot(c, x)


def _cdl_fwd(c, ct, x):
    return _hdot(c, x), (c, ct)


def _cdl_bwd(res, g):
    c, ct = res
    return jnp.zeros_like(c), jnp.zeros_like(ct), _hdot(ct, g)


cdot_left.defvjp(_cdl_fwd, _cdl_bwd)


@jax.custom_vjp
def cdot_right(x, c, ct):
    return _hdot(x, c)


def _cdr_fwd(x, c, ct):
    return _hdot(x, c), (c, ct)


def _cdr_bwd(res, g):
    c, ct = res
    return _hdot(g, ct), jnp.zeros_like(c), jnp.zeros_like(ct)


cdot_right.defvjp(_cdr_fwd, _cdr_bwd)


def _sigmoid(x):
    return 1.0 / (1.0 + jnp.exp(-x))


def _silu(x):
    return x * _sigmoid(x)


def _softplus(x):
    return jnp.maximum(x, 0.0) + jnp.log(1.0 + jnp.exp(-jnp.abs(x)))


def _log_sigmoid(x):
    return jnp.minimum(x, 0.0) - jnp.log(1.0 + jnp.exp(-jnp.abs(x)))


def _gelu(x):
    c = math.sqrt(2.0 / math.pi)
    return 0.5 * x * (1.0 + jnp.tanh(c * (x + 0.044715 * (x * x * x))))


def _rms(x, g):
    return x * lax.rsqrt(jnp.mean(x * x, axis=-1, keepdims=True) + EPS) * g


def _iota2(shape, axis):
    return lax.broadcasted_iota(jnp.int32, shape, axis)


def matmul(a, b, form, *, name, G=1, out_dtype=F32, add=None):
    isz = lambda t: jnp.dtype(t.dtype).itemsize
    osz = jnp.dtype(out_dtype).itemsize + (isz(add) if add is not None else 0)

    def fits(bm, bn, bk):
        return 2 * (bm * bk * isz(a) + bk * bn * isz(b) + bm * bn * osz) + 4 * bm * bn <= _MATMUL_VMEM

    if form in ("nn", "nt"):
        M = a.shape[0]
        K = a.shape[1] // G
        N = b.shape[2] if form == "nn" else b.shape[1]
        bm, bn, bk = min(M, 1024), _pick(N, 1536), _pick(K, 2048)
        while not fits(bm, bn, bk) and bk % 256 == 0:
            bk //= 2
        nj, nk = N // bn, K // bk
        grid = (G, M // bm, nj, nk)
        a_spec = pl.BlockSpec((bm, bk), lambda g, i, j, k: (i, g * nk + k))
        if form == "nn":
            b_spec = pl.BlockSpec((None, bk, bn), lambda g, i, j, k: (g, k, j))
        else:
            b_spec = pl.BlockSpec((None, bn, bk), lambda g, i, j, k: (g, j, k))
        o_spec = pl.BlockSpec((bm, bn), lambda g, i, j, k: (i, g * nj + j))
        out_shape = jax.ShapeDtypeStruct((M, G * N), out_dtype)
    else:
        T = a.shape[0]
        Ka, Nb = a.shape[1] // G, b.shape[1] // G
        bm, bn, bk = _pick(Ka, 1408), _pick(Nb, 1536), min(T, 2048)
        while not fits(bm, bn, bk) and bk % 512 == 0:
            bk //= 2
        ni, nj, nk = Ka // bm, Nb // bn, T // bk
        grid = (G, ni, nj, nk)
        a_spec = pl.BlockSpec((bk, bm), lambda g, i, j, k: (k, g * ni + i))
        b_spec = pl.BlockSpec((bk, bn), lambda g, i, j, k: (k, g * nj + j))
        o_spec = pl.BlockSpec((None, bm, bn), lambda g, i, j, k: (g, i, j))
        out_shape = jax.ShapeDtypeStruct((G, Ka, Nb), out_dtype)
    has_add = add is not None

    def finish(refs, r):
        if has_add:
            r = r + refs[2][...].astype(F32)
        o_ref = refs[3] if has_add else refs[2]
        o_ref[...] = r.astype(o_ref.dtype)

    def body_one(*refs):
        finish(refs, _dot_raw(refs[0][...], refs[1][...], form))

    def body_acc(*refs):
        acc_ref = refs[-1]
        k = pl.program_id(3)

        @pl.when(k == 0)
        def _():
            acc_ref[...] = jnp.zeros_like(acc_ref)

        acc_ref[...] += _dot_raw(refs[0][...], refs[1][...], form)

        @pl.when(k == nk - 1)
        def _():
            finish(refs, acc_ref[...])

    in_specs = [a_spec, b_spec]
    args = [a, b]
    if has_add:
        in_specs.append(o_spec)
        args.append(add)
    return pl.pallas_call(
        body_one if nk == 1 else body_acc, name=name, grid=grid, in_specs=in_specs, out_specs=o_spec,
        out_shape=out_shape, scratch_shapes=[] if nk == 1 else [pltpu.VMEM((bm, bn), F32)],
        compiler_params=_cparams(4),
    )(*args)


def matmul_nt_norm_bwd(a, w, h, g, d, *, name):
    T, K = a.shape
    bm = min(T, 512)
    bk = _pick(K, 2048)
    nk = K // bk

    def body(a_ref, w_ref, h_ref, g_ref, d_ref, dh_ref, dg_ref, acc_ref):
        i, k = pl.program_id(0), pl.program_id(1)

        @pl.when((i == 0) & (k == 0))
        def _():
            dg_ref[...] = jnp.zeros_like(dg_ref)

        @pl.when(k == 0)
        def _():
            acc_ref[...] = jnp.zeros_like(acc_ref)

        acc_ref[...] += _dot_raw(a_ref[...], w_ref[...], "nt")

        @pl.when(k == nk - 1)
        def _():
            _, vjp = jax.vjp(lambda g_, h_: _rms(h_, g_), g_ref[...], h_ref[...])
            dg, dh = vjp(acc_ref[...])
            dh_ref[...] = dh + d_ref[...]
            dg_ref[...] += dg

    row = pl.BlockSpec((bm, D), lambda i, k: (i, 0))
    one = pl.BlockSpec((1, D), lambda i, k: (0, 0))
    return pl.pallas_call(
        body, name=name, grid=(T // bm, nk),
        in_specs=[pl.BlockSpec((bm, bk), lambda i, k: (i, k)), pl.BlockSpec((D, bk), lambda i, k: (0, k)), row, one, row],
        out_specs=[row, one], out_shape=[jax.ShapeDtypeStruct((T, D), F32), jax.ShapeDtypeStruct((1, D), F32)],
        scratch_shapes=[pltpu.VMEM((bm, D), F32)], compiler_params=_cparams(2),
    )(a, w, h, g, d)


def ffn_up(hn, w_il, *, name):
    T = hn.shape[0]
    bm, hb = min(T, 512), FFN_H // 2

    def body(a_ref, b_ref, act_ref, gu_ref):
        r = _dot_raw(a_ref[...], b_ref[...], "nn")
        act_ref[...] = (_silu(r[:, :hb]) * r[:, hb:]).astype(act_ref.dtype)
        gu_ref[...] = r.astype(gu_ref.dtype)

    return pl.pallas_call(
        body, name=name, grid=(2, T // bm),
        in_specs=[pl.BlockSpec((bm, D), lambda j, i: (i, 0)), pl.BlockSpec((D, 2 * hb), lambda j, i: (0, j))],
        out_specs=[pl.BlockSpec((bm, hb), lambda j, i: (i, j)), pl.BlockSpec((bm, 2 * hb), lambda j, i: (i, j))],
        out_shape=[jax.ShapeDtypeStruct((T, FFN_H), BF16), jax.ShapeDtypeStruct((T, 2 * FFN_H), BF16)],
        compiler_params=_cparams(2),
    )(hn, w_il)


_DACT_CHUNK = 512


def ffn_dact(d, w_down, gu, *, name):
    T = d.shape[0]
    bm, hb = min(T, 512), FFN_H // 2

    def body(d_ref, w_ref, gu_ref, o_ref):
        d_blk = d_ref[...].astype(BF16)
        for lo in range(0, hb, _DACT_CHUNK):
            hi = min(lo + _DACT_CHUNK, hb)
            da = _dot_raw(d_blk, w_ref[lo:hi, :], "nt")
            g, u = gu_ref[:, lo:hi].astype(F32), gu_ref[:, hb + lo:hb + hi].astype(F32)
            sg = _sigmoid(g)
            o_ref[:, lo:hi] = (da * u * (sg * (1.0 + g * (1.0 - sg)))).astype(o_ref.dtype)
            o_ref[:, hb + lo:hb + hi] = (da * (g * sg)).astype(o_ref.dtype)

    return pl.pallas_call(
        body, name=name, grid=(2, T // bm),
        in_specs=[pl.BlockSpec((bm, D), lambda j, i: (i, 0)), pl.BlockSpec((hb, D), lambda j, i: (j, 0)),
                  pl.BlockSpec((bm, 2 * hb), lambda j, i: (i, j))],
        out_specs=pl.BlockSpec((bm, 2 * hb), lambda j, i: (i, j)),
        out_shape=jax.ShapeDtypeStruct((T, 2 * FFN_H), BF16), compiler_params=_cparams(2),
    )(d, w_down, gu)


def _row_entry(e):
    return e if isinstance(e, tuple) else (e, e.shape[1], 0)


def _row_spec(bt, e):
    _, width, idx = e
    return pl.BlockSpec((bt, width), lambda i: (i, idx))


def _full_spec(p):
    return pl.BlockSpec(p.shape, lambda i: (0,) * p.ndim)


def rowwise(f, params, rows, outs, *, bt, name):
    rows = [_row_entry(e) for e in rows]
    T = rows[0][0].shape[0]
    bt = min(bt, T)
    np_, nr = len(params), len(rows)

    def body(*refs):
        p = tuple(r[...].astype(F32) for r in refs[:np_])
        rw = tuple(r[...].astype(F32) for r in refs[np_:np_ + nr])
        res = f(p, rw)
        for o_ref, o in zip(refs[np_ + nr:], res):
            o_ref[...] = o.astype(o_ref.dtype)

    res = pl.pallas_call(
        body, name=name, grid=(T // bt,),
        in_specs=[_full_spec(p) for p in params] + [_row_spec(bt, e) for e in rows],
        out_specs=[pl.BlockSpec((bt, w), lambda i: (i, 0)) for w, _ in outs],
        out_shape=[jax.ShapeDtypeStruct((T, w), dt) for w, dt in outs],
        compiler_params=_cparams(1),
    )(*params, *[e[0] for e in rows])
    return list(res)


def rowwise_vjp(f, params, rows, cts, drow_dtypes, *, bt, name):
    rows = [_row_entry(e) for e in rows]
    cts = [_row_entry(e) for e in cts]
    T = rows[0][0].shape[0]
    bt = min(bt, T)
    np_, nr, nc = len(params), len(rows), len(cts)
    want = [i for i, dt in enumerate(drow_dtypes) if dt is not None]

    def body(*refs):
        p = tuple(r[...].astype(F32) for r in refs[:np_])
        rw = tuple(r[...].astype(F32) for r in refs[np_:np_ + nr])
        ct = tuple(r[...].astype(F32) for r in refs[np_ + nr:np_ + nr + nc])
        outs = refs[np_ + nr + nc:]
        _, vjp = jax.vjp(f, p, rw)
        dp, dr = vjp(ct)

        @pl.when(pl.program_id(0) == 0)
        def _():
            for o in outs[:np_]:
                o[...] = jnp.zeros_like(o)

        for o, d in zip(outs[:np_], dp):
            o[...] += d
        for o, i in zip(outs[np_:], want):
            o[...] = dr[i].astype(o.dtype)

    res = pl.pallas_call(
        body, name=name, grid=(T // bt,),
        in_specs=[_full_spec(p) for p in params] + [_row_spec(bt, e) for e in rows] + [_row_spec(bt, e) for e in cts],
        out_specs=[_full_spec(p) for p in params] + [pl.BlockSpec((bt, rows[i][1]), lambda i_: (i_, 0)) for i in want],
        out_shape=[jax.ShapeDtypeStruct(p.shape, F32) for p in params]
        + [jax.ShapeDtypeStruct((T, rows[i][1]), drow_dtypes[i]) for i in want],
        compiler_params=_cparams(1),
    )(*params, *[e[0] for e in rows], *[e[0] for e in cts])
    res = list(res)
    return res[:np_], res[np_:]


def f_rmsnorm(p, r):
    return (_rms(r[0], p[0]),)


def f_rmsnorm_res(p, r):
    return (_rms(r[0], p[0]), r[0])


def f_swiglu(p, r):
    gu = r[0]
    return (_silu(gu[:, :FFN_H]) * gu[:, FFN_H:],)


def f_gla_gate_in(p, r):
    w_a2, b_a = p
    z = bdot(r[0], w_a2, "nn") + b_a
    return (_log_sigmoid(z) / GLA_TAU,)


def f_gla_gate_in_fwd(p, r):
    w_a2, b_a = p
    z = _dot_raw(r[0], w_a2, "nn") + b_a
    return (_log_sigmoid(z) / GLA_TAU,)


def f_gla_out(p, r):
    (ng,) = p
    o, rr = r
    parts = []
    for h in range(GLA_H):
        sl = slice(h * GLA_DV, (h + 1) * GLA_DV)
        parts.append(_rms(o[:, sl], ng[:, sl]) * _silu(rr[:, sl]))
    return (jnp.concatenate(parts, axis=1),)


def f_ssd_out(p, r):
    (ng,) = p
    y, z = r
    t = y * _silu(z)
    gsz = SSD_DI // SSD_G
    parts = []
    for g in range(SSD_G):
        sl = slice(g * gsz, (g + 1) * gsz)
        parts.append(_rms(t[:, sl], ng[:, sl]))
    return (jnp.concatenate(parts, axis=1),)


def f_s5_act(p, r):
    (dsk,) = p
    ycp, u = r
    return (_gelu(ycp + dsk * u),)


def f_glu_res(p, r):
    vg, h = r
    return (vg[:, :D] * _sigmoid(vg[:, D:]) + h,)


def f_glu(p, r):
    vg = r[0]
    return (vg[:, :D] * _sigmoid(vg[:, D:]),)


def loss_head(h, g, target, *, name):
    T = h.shape[0]
    bt = min(T, 256)

    def lossf(g_, h_, t_):
        e = _rms(h_, g_) - t_
        return (0.5 / D) * jnp.sum(e * e)

    def body(g_ref, h_ref, t_ref, loss_ref, dg_ref, dh_ref):
        @pl.when(pl.program_id(0) == 0)
        def _():
            loss_ref[...] = jnp.zeros_like(loss_ref)
            dg_ref[...] = jnp.zeros_like(dg_ref)

        val, vjp = jax.vjp(lossf, g_ref[...], h_ref[...], t_ref[...])
        dg, dh, _ = vjp(jnp.ones((), F32))
        loss_ref[...] += jnp.full(loss_ref.shape, val, F32)
        dg_ref[...] += dg
        dh_ref[...] = dh

    row = pl.BlockSpec((bt, D), lambda i: (i, 0))
    one = pl.BlockSpec((1, D), lambda i: (0, 0))
    loss, dg, dh = pl.pallas_call(
        body, name=name, grid=(T // bt,), in_specs=[one, row, row],
        out_specs=[pl.BlockSpec((1, LANE), lambda i: (0, 0)), one, row],
        out_shape=[jax.ShapeDtypeStruct((1, LANE), F32), jax.ShapeDtypeStruct((1, D), F32),
                   jax.ShapeDtypeStruct((T, D), F32)],
        compiler_params=_cparams(1),
    )(g, h, target)
    return loss[0, 0], dg, dh


def _gla_consts():
    r, c = _iota2((CH, CH), 0), _iota2((CH, CH), 1)
    return (r >= c).astype(F32), (r <= c).astype(F32), r >= c


def _gla_chunk(q, k, v, la, st, consts, dot, cdl):
    L, Lt, tril = consts
    lc = cdl(L, Lt, la)
    lend = lc[CH - 1:CH, :]
    e, ei = jnp.exp(lc), jnp.exp(-lc)
    qs = q * (GLA_DK ** -0.5)
    qf, kf, qb, kb = qs * e, k * ei, qs * ei, k * e
    sc = jnp.where(tril, dot(qf, kf, "nt"), dot(qb, kb, "nt"))
    o = dot(sc, v, "nn") + dot(qf, st, "nt")
    kd = k * jnp.exp(lend - lc)
    st_new = st * jnp.exp(lend) + dot(v, kd, "tn")
    return o, st_new


def _gla_block(q, k, v, la, st, nc, dot, cdl):
    consts = _gla_consts()
    outs = []
    for c in range(nc):
        sl = slice(c * CH, (c + 1) * CH)
        o, st = _gla_chunk(q[sl], k[sl], v[sl], la[sl], st, consts, dot, cdl)
        outs.append(o)
    return jnp.concatenate(outs, axis=0), st


_GLA_HP = 2


def _gla_specs(rows, rev, nb):
    t = (lambda j: nb - 1 - j) if rev else (lambda j: j)
    hp, ng = _GLA_HP, GLA_H // _GLA_HP
    q = pl.BlockSpec((rows, hp * GLA_DK), lambda h, j: (t(j), h))
    k = pl.BlockSpec((rows, hp * GLA_DK), lambda h, j: (t(j), ng + h))
    v = pl.BlockSpec((rows, hp * GLA_DV), lambda h, j: (t(j), ng + h))
    la = pl.BlockSpec((rows, hp * GLA_DK), lambda h, j: (t(j), h))
    ss = pl.BlockSpec((None, hp, GLA_DV, GLA_DK), lambda h, j: (t(j), h, 0, 0))
    o = pl.BlockSpec((rows, hp * GLA_DV), lambda h, j: (t(j), h))
    return q, k, v, la, ss, o


def _gla_heads(q, k, v, la, sts, nc, dot, cdl):
    outs, new = [], []
    for i in range(_GLA_HP):
        kk, vv = slice(i * GLA_DK, (i + 1) * GLA_DK), slice(i * GLA_DV, (i + 1) * GLA_DV)
        o, st = _gla_block(q[:, kk], k[:, kk], v[:, vv], la[:, kk], sts[i], nc, dot, cdl)
        outs.append(o)
        new.append(st)
    return jnp.concatenate(outs, axis=1), tuple(new)


def gla_scan_fwd(proj, la, *, nc, name):
    T = proj.shape[0]
    rows = min(T, nc * CH)
    nc = rows // CH
    nb = T // rows
    q_s, k_s, v_s, la_s, ss_s, o_s = _gla_specs(rows, False, nb)

    def body(q_ref, k_ref, v_ref, la_ref, o_ref, ss_ref, st_ref):
        @pl.when(pl.program_id(1) == 0)
        def _():
            st_ref[...] = jnp.zeros_like(st_ref)

        ss_ref[...] = st_ref[...]
        sts = tuple(st_ref[i] for i in range(_GLA_HP))
        o, sts = _gla_heads(q_ref[...], k_ref[...], v_ref[...], la_ref[...], sts, nc, _dot_raw, lambda c, ct, x: _hdot(c, x))
        o_ref[...] = o
        for i in range(_GLA_HP):
            st_ref[i] = sts[i]

    return pl.pallas_call(
        body, name=name, grid=(GLA_H // _GLA_HP, nb), in_specs=[q_s, k_s, v_s, la_s], out_specs=[o_s, ss_s],
        out_shape=[jax.ShapeDtypeStruct((T, GLA_VD), F32), jax.ShapeDtypeStruct((nb, GLA_H, GLA_DV, GLA_DK), F32)],
        scratch_shapes=[pltpu.VMEM((_GLA_HP, GLA_DV, GLA_DK), F32)], compiler_params=_cparams(2),
    )(proj, proj, proj, la)


def gla_scan_bwd(proj, la, ss, do, *, nc, name):
    T = proj.shape[0]
    rows = min(T, nc * CH)
    nc = rows // CH
    nb = T // rows
    q_s, k_s, v_s, la_s, ss_s, o_s = _gla_specs(rows, True, nb)
    t = lambda j: nb - 1 - j
    dqk_s = pl.BlockSpec((rows, _GLA_HP * GLA_DK), lambda h, j: (t(j), h))

    def body(q_ref, k_ref, v_ref, la_ref, ss_ref, do_ref, dq_ref, dk_ref, dv_ref, dla_ref, dst_ref):
        @pl.when(pl.program_id(1) == 0)
        def _():
            dst_ref[...] = jnp.zeros_like(dst_ref)

        fn = lambda q, k, v, la_, *sts: _gla_heads(q, k, v, la_, sts, nc, bdot, cdot_left)
        _, vjp = jax.vjp(fn, q_ref[...], k_ref[...], v_ref[...], la_ref[...], *[ss_ref[i] for i in range(_GLA_HP)])
        dq, dk, dv, dla, *dsts = vjp((do_ref[...], tuple(dst_ref[i] for i in range(_GLA_HP))))
        dq_ref[...] = dq.astype(dq_ref.dtype)
        dk_ref[...] = dk.astype(dk_ref.dtype)
        dv_ref[...] = dv.astype(dv_ref.dtype)
        dla_ref[...] = dla
        for i in range(_GLA_HP):
            dst_ref[i] = dsts[i]

    return pl.pallas_call(
        body, name=name, grid=(GLA_H // _GLA_HP, nb), in_specs=[q_s, k_s, v_s, la_s, ss_s, o_s],
        out_specs=[dqk_s, dqk_s, o_s, dqk_s],
        out_shape=[jax.ShapeDtypeStruct((T, GLA_QK), BF16), jax.ShapeDtypeStruct((T, GLA_QK), BF16),
                   jax.ShapeDtypeStruct((T, GLA_VD), BF16), jax.ShapeDtypeStruct((T, GLA_QK), F32)],
        scratch_shapes=[pltpu.VMEM((_GLA_HP, GLA_DV, GLA_DK), F32)], compiler_params=_cparams(2),
    )(proj, proj, proj, la, ss, do)


_CONV_W = 512
_CONV_OFF = SSD_DI // _CONV_W


def _conv_pre(x, prev8, w_ref, b_ref):
    bt = x.shape[0]
    ext = jnp.concatenate([prev8, x], axis=0)
    shifted = []
    for j in range(SSD_K):
        s = SSD_K - 1 - j
        shifted.append(x if s == 0 else pltpu.roll(ext, s, 0)[8:8 + bt])
    pre = b_ref[...] + sum(w_ref[j:j + 1, :] * shifted[j] for j in range(SSD_K))
    return pre, shifted


def ssd_conv_fwd(proj, w, b, *, name):
    T = proj.shape[0]
    bt = min(T, 512)
    nb = T // bt

    def body(x_ref, w_ref, b_ref, o_ref, carry_ref):
        @pl.when(pl.program_id(1) == 0)
        def _():
            carry_ref[...] = jnp.zeros_like(carry_ref)

        x = x_ref[...]
        pre, _ = _conv_pre(x, carry_ref[...], w_ref, b_ref)
        o_ref[...] = _silu(pre)
        carry_ref[...] = x[bt - 8:, :]

    return pl.pallas_call(
        body, name=name, grid=(SSD_CONV // _CONV_W, nb),
        in_specs=[pl.BlockSpec((bt, _CONV_W), lambda c, t: (t, _CONV_OFF + c)),
                  pl.BlockSpec((SSD_K, _CONV_W), lambda c, t: (0, c)),
                  pl.BlockSpec((1, _CONV_W), lambda c, t: (0, c))],
        out_specs=pl.BlockSpec((bt, _CONV_W), lambda c, t: (t, c)),
        out_shape=jax.ShapeDtypeStruct((T, SSD_CONV), F32),
        scratch_shapes=[pltpu.VMEM((8, _CONV_W), F32)], compiler_params=_cparams(2),
    )(proj, w, b)


def ssd_conv_bwd(proj, w, b, dout, *, name):
    T = proj.shape[0]
    bt = min(T, 512)
    nb = T // bt
    r8 = bt // 8

    def body(x_ref, xp_ref, w_ref, b_ref, do_ref, dx_ref, dw_ref, db_ref, carry_ref):
        t = pl.program_id(1)

        @pl.when(t == 0)
        def _():
            carry_ref[...] = jnp.zeros_like(carry_ref)
            dw_ref[...] = jnp.zeros_like(dw_ref)
            db_ref[...] = jnp.zeros_like(db_ref)

        x = x_ref[...]
        prev8 = jnp.where(t == nb - 1, 0.0, xp_ref[...])
        pre, shifted = _conv_pre(x, prev8, w_ref, b_ref)
        sg = _sigmoid(pre)
        dpre = do_ref[...] * (sg * (1.0 + pre * (1.0 - sg)))
        ext = jnp.concatenate([dpre, carry_ref[...]], axis=0)
        dx = w_ref[SSD_K - 1:SSD_K, :] * dpre
        for j in range(SSD_K - 1):
            s = SSD_K - 1 - j
            dx = dx + w_ref[j:j + 1, :] * pltpu.roll(ext, bt + 8 - s, 0)[:bt]
        dx_ref[...] = dx.astype(dx_ref.dtype)
        dw_ref[...] += jnp.concatenate([jnp.sum(dpre * shifted[j], axis=0, keepdims=True) for j in range(SSD_K)], axis=0)
        db_ref[...] += jnp.sum(dpre, axis=0, keepdims=True)
        carry_ref[...] = dpre[:8, :]

    rt = lambda t: nb - 1 - t
    return pl.pallas_call(
        body, name=name, grid=(SSD_CONV // _CONV_W, nb),
        in_specs=[pl.BlockSpec((bt, _CONV_W), lambda c, t: (rt(t), _CONV_OFF + c)),
                  pl.BlockSpec((8, _CONV_W), lambda c, t: (jnp.maximum(rt(t) * r8 - 1, 0), _CONV_OFF + c)),
                  pl.BlockSpec((SSD_K, _CONV_W), lambda c, t: (0, c)),
                  pl.BlockSpec((1, _CONV_W), lambda c, t: (0, c)),
                  pl.BlockSpec((bt, _CONV_W), lambda c, t: (rt(t), c))],
        out_specs=[pl.BlockSpec((bt, _CONV_W), lambda c, t: (rt(t), c)),
                   pl.BlockSpec((SSD_K, _CONV_W), lambda c, t: (0, c)),
                   pl.BlockSpec((1, _CONV_W), lambda c, t: (0, c))],
        out_shape=[jax.ShapeDtypeStruct((T, SSD_CONV), BF16), jax.ShapeDtypeStruct((SSD_K, SSD_CONV), F32),
                   jax.ShapeDtypeStruct((1, SSD_CONV), F32)],
        scratch_shapes=[pltpu.VMEM((8, _CONV_W), F32)], compiler_params=_cparams(2),
    )(proj, proj, w, b, dout)


_SSD_U = 2 * CH


def _ssd_unit(xs, bm, cm, dtraw, dtb, alog, dsk, hp, g, dot, cdl, cdr):
    U, P2 = _SSD_U, 2 * SSD_HD
    r, c = _iota2((U, U), 0), _iota2((U, U), 1)
    same = (r // CH) == (c // CH)
    Lb = (same & (r >= c)).astype(F32)
    Ub = (same & (r <= c)).astype(F32)
    lane = _iota2((1, U), 1)
    lo_lane = _iota2((1, P2), 1) < SSD_HD
    lo_sub = _iota2((P2, 1), 0) < SSD_HD
    diag2 = (_iota2((CH, P2), 0) == (_iota2((CH, P2), 1) % CH)).astype(F32)

    dt = _softplus(dtraw + dtb)
    da = dt * (-jnp.exp(alog))
    cum = cdl(Lb, Ub, da)
    ys = []
    new_hp = []
    for pr in range(2):
        xs_p = xs[:, pr * P2:(pr + 1) * P2]
        cols, dts, dks = [], [], []
        for jj in range(2):
            oh_l = (lane == g * (SSD_H // SSD_G) + 2 * pr + jj).astype(F32)
            cols.append(jnp.sum(cum * oh_l, axis=1, keepdims=True))
            dts.append(jnp.sum(dt * oh_l, axis=1, keepdims=True))
            dks.append(jnp.sum(dsk * oh_l, axis=1, keepdims=True))
        dsk_p = jnp.where(lo_lane, dks[0], dks[1])
        h = hp[pr]
        yc = []
        for ci in range(2):
            sl = slice(ci * CH, (ci + 1) * CH)
            xs_c, bm_c, cm_c = xs_p[sl], bm[sl], cm[sl]
            col = jnp.where(lo_lane, cols[0][sl], cols[1][sl])
            dtc = jnp.where(lo_lane, dts[0][sl], dts[1][sl])
            row = jnp.sum(diag2 * col, axis=0, keepdims=True)
            dtrow = jnp.sum(diag2 * dtc, axis=0, keepdims=True)
            cb = dot(cm_c, jnp.concatenate([bm_c, bm_c], axis=0), "nt")
            mix = cb * jnp.exp(-jnp.abs(col - row)) * dtrow
            xbd = jnp.concatenate([jnp.where(lo_lane, xs_c, 0.0), jnp.where(lo_lane, 0.0, xs_c)], axis=0)
            y_intra = dot(mix, xbd, "nn")
            ce = jnp.where(lo_lane, cols[0][ci * CH + CH - 1:ci * CH + CH, :], cols[1][ci * CH + CH - 1:ci * CH + CH, :])
            y_inter = dot(cm_c, h, "nt") * jnp.exp(col)
            xw = xs_c * (dtc * jnp.exp(ce - col))
            ce_s = [cols[jj][ci * CH + CH - 1:ci * CH + CH, :] for jj in range(2)]
            a_p = jnp.where(lo_sub, jnp.exp(ce_s[0]), jnp.exp(ce_s[1]))
            h = a_p * h + dot(xw, bm_c, "tn")
            yc.append(y_intra + y_inter + dsk_p * xs_c)
        ys.append(jnp.concatenate(yc, axis=0))
        new_hp.append(h)
    return jnp.concatenate(ys, axis=1), tuple(new_hp)


def _ssd_block(xs, bm, cm, dtraw, dtb, alog, dsk, hp, g, nu, dot, cdl, cdr):
    outs = []
    for u in range(nu):
        sl = slice(u * _SSD_U, (u + 1) * _SSD_U)
        y, hp = _ssd_unit(xs[sl], bm[sl], cm[sl], dtraw[sl], dtb, alog, dsk, hp, g, dot, cdl, cdr)
        outs.append(y)
    return jnp.concatenate(outs, axis=0), hp


def _ssd_specs(rows, rev, nb):
    t = (lambda j: nb - 1 - j) if rev else (lambda j: j)
    gw = SSD_DI // SSD_G
    xs = pl.BlockSpec((rows, gw), lambda j, g: (t(j), g))
    bm = pl.BlockSpec((rows, SSD_N), lambda j, g: (t(j), SSD_DI // SSD_N + g))
    cm = pl.BlockSpec((rows, SSD_N), lambda j, g: (t(j), (SSD_DI + SSD_GN) // SSD_N + g))
    dtr = pl.BlockSpec((rows, LANE), lambda j, g: (t(j), (SSD_DI + SSD_CONV) // LANE))
    par = pl.BlockSpec((1, LANE), lambda j, g: (0, 0))
    hs = pl.BlockSpec((None, None, 2, 2 * SSD_HD, SSD_N), lambda j, g: (t(j), g, 0, 0, 0))
    y = pl.BlockSpec((rows, gw), lambda j, g: (t(j), g))
    return xs, bm, cm, dtr, par, hs, y


def ssd_scan_fwd(xbc, proj, dtb, alog, dsk, *, nu, name):
    T = xbc.shape[0]
    rows = min(T, nu * _SSD_U)
    nu = rows // _SSD_U
    nb = T // rows
    xs_s, bm_s, cm_s, dt_s, par_s, hs_s, y_s = _ssd_specs(rows, False, nb)

    def body(xs_ref, bm_ref, cm_ref, dt_ref, dtb_ref, al_ref, dsk_ref, y_ref, hs_ref, h_ref):
        g = pl.program_id(1)

        @pl.when(pl.program_id(0) == 0)
        def _():
            h_ref[g] = jnp.zeros(h_ref.shape[1:], F32)

        hs_ref[...] = h_ref[g]
        hp = (h_ref[g, 0], h_ref[g, 1])
        y, hp = _ssd_block(xs_ref[...], bm_ref[...], cm_ref[...], dt_ref[...], dtb_ref[...], al_ref[...], dsk_ref[...],
                           hp, g, nu, _dot_raw, lambda c, ct, x: _hdot(c, x), lambda x, c, ct: _hdot(x, c))
        y_ref[...] = y
        h_ref[g, 0] = hp[0]
        h_ref[g, 1] = hp[1]

    return pl.pallas_call(
        body, name=name, grid=(nb, SSD_G), in_specs=[xs_s, bm_s, cm_s, dt_s, par_s, par_s, par_s],
        out_specs=[y_s, hs_s],
        out_shape=[jax.ShapeDtypeStruct((T, SSD_DI), F32), jax.ShapeDtypeStruct((nb, SSD_G, 2, 2 * SSD_HD, SSD_N), F32)],
        scratch_shapes=[pltpu.VMEM((SSD_G, 2, 2 * SSD_HD, SSD_N), F32)], compiler_params=_cparams(2),
    )(xbc, xbc, xbc, proj, dtb, alog, dsk)


def ssd_scan_bwd(xbc, proj, dtb, alog, dsk, hs, dy, *, nu, name):
    T = xbc.shape[0]
    rows = min(T, nu * _SSD_U)
    nu = rows // _SSD_U
    nb = T // rows
    xs_s, bm_s, cm_s, dt_s, par_s, hs_s, y_s = _ssd_specs(rows, True, nb)
    t = lambda j: nb - 1 - j
    n_s = pl.BlockSpec((rows, SSD_N), lambda j, g: (t(j), g))
    ddt_s = pl.BlockSpec((rows, LANE), lambda j, g: (t(j), 0))

    def body(xs_ref, bm_ref, cm_ref, dt_ref, dtb_ref, al_ref, dsk_ref, hs_ref, dy_ref,
             dxs_ref, dbm_ref, dcm_ref, ddt_ref, ddtb_ref, dal_ref, ddsk_ref, dh_ref):
        j, g = pl.program_id(0), pl.program_id(1)

        @pl.when(j == 0)
        def _():
            dh_ref[g] = jnp.zeros(dh_ref.shape[1:], F32)

        @pl.when((j == 0) & (g == 0))
        def _():
            ddtb_ref[...] = jnp.zeros_like(ddtb_ref)
            dal_ref[...] = jnp.zeros_like(dal_ref)
            ddsk_ref[...] = jnp.zeros_like(ddsk_ref)

        @pl.when(g == 0)
        def _():
            ddt_ref[...] = jnp.zeros_like(ddt_ref)

        fn = lambda xs, bm, cm, dtr, dtb_, al, dsk_, h0, h1: _ssd_block(
            xs, bm, cm, dtr, dtb_, al, dsk_, (h0, h1), g, nu, bdot, cdot_left, cdot_right)
        _, vjp = jax.vjp(fn, xs_ref[...], bm_ref[...], cm_ref[...], dt_ref[...], dtb_ref[...], al_ref[...], dsk_ref[...],
                         hs_ref[0], hs_ref[1])
        dxs, dbm, dcm, ddt, ddtb, dal, ddsk, dh0, dh1 = vjp((dy_ref[...], (dh_ref[g, 0], dh_ref[g, 1])))
        dxs_ref[...] = dxs
        dbm_ref[...] = dbm
        dcm_ref[...] = dcm
        ddt_ref[...] += ddt
        ddtb_ref[...] += ddtb
        dal_ref[...] += dal
        ddsk_ref[...] += ddsk
        dh_ref[g, 0] = dh0
        dh_ref[g, 1] = dh1

    return pl.pallas_call(
        body, name=name, grid=(nb, SSD_G), in_specs=[xs_s, bm_s, cm_s, dt_s, par_s, par_s, par_s, hs_s, y_s],
        out_specs=[y_s, n_s, n_s, ddt_s, par_s, par_s, par_s],
        out_shape=[jax.ShapeDtypeStruct((T, SSD_DI), F32), jax.ShapeDtypeStruct((T, SSD_GN), F32),
                   jax.ShapeDtypeStruct((T, SSD_GN), F32), jax.ShapeDtypeStruct((T, LANE), F32),
                   jax.ShapeDtypeStruct((1, LANE), F32), jax.ShapeDtypeStruct((1, LANE), F32),
                   jax.ShapeDtypeStruct((1, LANE), F32)],
        scratch_shapes=[pltpu.VMEM((SSD_G, 2, 2 * SSD_HD, SSD_N), F32)], compiler_params=_cparams(2),
    )(xbc, xbc, xbc, proj, dtb, alog, dsk, hs, dy)


def _s5_param_f(log_dt, a_re, a_im, bre_t, bim_t, cim, cdl):
    n = S5_NG * S5_GS
    r, c = _iota2((n, S5_NG), 0), _iota2((n, S5_NG), 1)
    E = ((r // S5_GS) == c).astype(F32)
    rt, ct = _iota2((S5_NG, n), 0), _iota2((S5_NG, n), 1)
    Et = ((ct // S5_GS) == rt).astype(F32)
    step = jnp.exp(log_dt)
    mag = jnp.exp(step * a_re)
    abr = mag * jnp.cos(step * a_im)
    abi = mag * jnp.sin(step * a_im)
    den = a_re * a_re + a_im * a_im
    nr, ni = abr - 1.0, abi
    fr = (nr * a_re + ni * a_im) / den
    fi = (ni * a_re - nr * a_im) / den
    Fr, Fi = cdl(E, Et, fr), cdl(E, Et, fi)
    bbr = Fr * bre_t - Fi * bim_t
    bbi = Fr * bim_t + Fi * bre_t
    return abr, abi, bbr, bbi, -cim


def _whole(a):
    return pl.BlockSpec(a.shape, lambda: (0,) * a.ndim)


def s5_param_fwd(args, *, name):
    def body(*refs):
        res = _s5_param_f(*[r[...] for r in refs[:6]], lambda c, ct, x: _hdot(c, x))
        for o, v in zip(refs[6:], res):
            o[...] = v

    shapes = [(S5_NG, S5_P), (S5_NG, S5_P)] + [(S5_NG * S5_GS, S5_P)] * 3
    return pl.pallas_call(
        body, name=name, in_specs=[_whole(a) for a in args], out_specs=[pl.BlockSpec(s, lambda: (0, 0)) for s in shapes],
        out_shape=[jax.ShapeDtypeStruct(s, F32) for s in shapes],
        compiler_params=pltpu.CompilerParams(vmem_limit_bytes=VMEM_LIMIT),
    )(*args)


def s5_param_bwd(args, cts, *, name):
    def body(*refs):
        fn = lambda *a: _s5_param_f(*a, cdot_left)
        _, vjp = jax.vjp(fn, *[r[...] for r in refs[:6]])
        grads = vjp(tuple(r[...] for r in refs[6:11]))
        for o, v in zip(refs[11:], grads):
            o[...] = v

    return pl.pallas_call(
        body, name=name, in_specs=[_whole(a) for a in list(args) + list(cts)],
        out_specs=[_whole(a) for a in args], out_shape=[jax.ShapeDtypeStruct(a.shape, F32) for a in args],
        compiler_params=pltpu.CompilerParams(vmem_limit_bytes=VMEM_LIMIT),
    )(*args, *cts)


_S5_W = S5_BLK * S5_P


def _cmul_add(xr, xi, pr, pi, sr, si):
    return xr + (pr * sr - pi * si), xi + (pr * si + pi * sr)


def _s5_powers(ar, ai):
    pw = [(ar, ai)]
    for _ in range(7):
        qr, qi = pw[-1]
        pw.append((qr * ar - qi * ai, qr * ai + qi * ar))
    return pw


def s5_scan_fwd(bu, a_re, a_im, *, name):
    T = bu.shape[0]
    bt = min(T, 256)
    nb = T // bt

    def body(bu_ref, ar_ref, ai_ref, x_ref, carry_ref):
        @pl.when(pl.program_id(1) == 0)
        def _():
            carry_ref[...] = jnp.zeros_like(carry_ref)

        ar, ai = ar_ref[...], ai_ref[...]
        pw = _s5_powers(ar, ai)
        pwr = jnp.concatenate([p[0] for p in pw], axis=0)
        pwi = jnp.concatenate([p[1] for p in pw], axis=0)
        rin = _iota2((8, _S5_W), 0)
        cr, ci = carry_ref[0:1, :], carry_ref[1:2, :]
        for t in range(bt // 8):
            sl = slice(8 * t, 8 * t + 8)
            xr, xi = bu_ref[sl, :_S5_W], bu_ref[sl, _S5_W:]
            for s in (1, 2, 4):
                m = rin >= s
                sr = jnp.where(m, pltpu.roll(xr, s, 0), 0.0)
                si = jnp.where(m, pltpu.roll(xi, s, 0), 0.0)
                xr, xi = _cmul_add(xr, xi, *pw[s - 1], sr, si)
            xr, xi = _cmul_add(xr, xi, pwr, pwi, cr, ci)
            x_ref[sl, :_S5_W] = xr
            x_ref[sl, _S5_W:] = xi
            cr, ci = xr[7:8, :], xi[7:8, :]
        carry_ref[0:1, :] = cr
        carry_ref[1:2, :] = ci

    blk = pl.BlockSpec((bt, 2 * _S5_W), lambda g, t: (t, g))
    a_s = pl.BlockSpec((None, 1, _S5_W), lambda g, t: (g, 0, 0))
    return pl.pallas_call(
        body, name=name, grid=(S5_NG // S5_BLK, nb), in_specs=[blk, a_s, a_s], out_specs=blk,
        out_shape=jax.ShapeDtypeStruct(bu.shape, F32), scratch_shapes=[pltpu.VMEM((8, _S5_W), F32)],
        compiler_params=_cparams(2),
    )(bu, a_re, a_im)


def s5_scan_bwd(dx, x, a_re, a_im, *, name):
    T = dx.shape[0]
    bt = min(T, 256)
    nb = T // bt

    def body(g_ref, x_ref, ar_ref, ai_ref, lam_ref, dar_ref, dai_ref, carry_ref):
        @pl.when(pl.program_id(1) == 0)
        def _():
            carry_ref[...] = jnp.zeros_like(carry_ref)
            dar_ref[...] = jnp.zeros_like(dar_ref)
            dai_ref[...] = jnp.zeros_like(dai_ref)

        pw = _s5_powers(ar_ref[...], -ai_ref[...])
        pwr = jnp.concatenate([p[0] for p in reversed(pw)], axis=0)
        pwi = jnp.concatenate([p[1] for p in reversed(pw)], axis=0)
        rin = _iota2((8, _S5_W), 0)
        cr, ci = carry_ref[0:1, :], carry_ref[1:2, :]
        acc_r = jnp.zeros((8, _S5_W), F32)
        acc_i = jnp.zeros((8, _S5_W), F32)
        for t in reversed(range(bt // 8)):
            sl = slice(8 * t, 8 * t + 8)
            lr, li = g_ref[sl, :_S5_W], g_ref[sl, _S5_W:]
            for s in (1, 2, 4):
                m = rin < 8 - s
                sr = jnp.where(m, pltpu.roll(lr, 8 - s, 0), 0.0)
                si = jnp.where(m, pltpu.roll(li, 8 - s, 0), 0.0)
                lr, li = _cmul_add(lr, li, *pw[s - 1], sr, si)
            lr, li = _cmul_add(lr, li, pwr, pwi, cr, ci)
            lam_ref[sl, :_S5_W] = lr
            lam_ref[sl, _S5_W:] = li
            nr = jnp.where(rin == 7, cr, pltpu.roll(lr, 7, 0))
            ni = jnp.where(rin == 7, ci, pltpu.roll(li, 7, 0))
            xr, xi = x_ref[sl, :_S5_W], x_ref[sl, _S5_W:]
            acc_r = acc_r + (xr * nr + xi * ni)
            acc_i = acc_i + (xr * ni - xi * nr)
            cr, ci = lr[0:1, :], li[0:1, :]
        carry_ref[0:1, :] = cr
        carry_ref[1:2, :] = ci
        dar_ref[...] += jnp.sum(acc_r, axis=0, keepdims=True)
        dai_ref[...] += jnp.sum(acc_i, axis=0, keepdims=True)

    blk = pl.BlockSpec((bt, 2 * _S5_W), lambda g, t: (nb - 1 - t, g))
    a_s = pl.BlockSpec((None, 1, _S5_W), lambda g, t: (g, 0, 0))
    nblk = S5_NG // S5_BLK
    return pl.pallas_call(
        body, name=name, grid=(nblk, nb), in_specs=[blk, blk, a_s, a_s], out_specs=[blk, a_s, a_s],
        out_shape=[jax.ShapeDtypeStruct(dx.shape, F32), jax.ShapeDtypeStruct((nblk, 1, _S5_W), F32),
                   jax.ShapeDtypeStruct((nblk, 1, _S5_W), F32)],
        scratch_shapes=[pltpu.VMEM((8, _S5_W), F32)], compiler_params=_cparams(2),
    )(dx, x, a_re, a_im)


def _norm_bf16(h, g, name):
    return rowwise(f_rmsnorm, [g], [h], [(D, BF16)], bt=512, name=name)[0]


def _norm_bwd(h, g, cts, name):
    n = len(cts) - 1

    def f(p, r):
        y = _rms(r[0], p[0])
        return (y,) * n + (r[0],)

    (dg,), (dh,) = rowwise_vjp(f, [g], [h], cts, [F32], bt=256, name=name)
    return dh, dg


def ffn_fwd(h, g, w_gu, w_down, tag):
    hn = _norm_bf16(h, g, f"{tag}_norm")
    a, gu = ffn_up(hn, w_gu, name=f"{tag}_up")
    h2 = matmul(a, w_down[None], "nn", add=h, name=f"{tag}_down")
    return h2, (h, hn, gu, a)


def ffn_bwd(d, saved, g, w_gu, w_down, tag):
    h, hn, gu, a = saved
    dgu = ffn_dact(d, w_down, gu, name=f"{tag}_dact")
    dwd = matmul(a, d, "tn", name=f"{tag}_dwd")[0]
    dwgu = matmul(hn, dgu, "tn", name=f"{tag}_dwgu")[0]
    dh, dg = matmul_nt_norm_bwd(dgu, w_gu, h, g, d, name=f"{tag}_dhn")
    return dh, dg, dwgu, dwd


_GLA_NC = 4
_SSD_NU = 4


def gla_fwd(h, gm, w_in, w_a2, b_a, ng, w_out, tag):
    hn = _norm_bf16(h, gm, f"{tag}_norm")
    proj = matmul(hn, w_in[None], "nn", name=f"{tag}_in")
    alow = (proj, LANE, 2 * (GLA_QK + GLA_VD) // LANE)
    la = rowwise(f_gla_gate_in_fwd, [w_a2, b_a], [alow], [(GLA_QK, F32)], bt=512, name=f"{tag}_gate")[0]
    o, ss = gla_scan_fwd(proj, la, nc=_GLA_NC, name=f"{tag}_scan")
    r = (proj, GLA_VD, 2)
    og = rowwise(f_gla_out, [ng], [o, r], [(GLA_VD, BF16)], bt=256, name=f"{tag}_out")[0]
    h2 = matmul(og, w_out[None], "nn", add=h, name=f"{tag}_proj")
    return h2, (h, hn, proj, la, o, ss, og)


def gla_bwd(d, saved, gm, w_in, w_a2, b_a, ng, w_out, tag):
    h, hn, proj, la, o, ss, og = saved
    dog = matmul(d, w_out[None], "nt", name=f"{tag}_dog")
    dwout = matmul(og, d, "tn", name=f"{tag}_dwout")[0]
    r = (proj, GLA_VD, 2)
    (dng,), (do, dr) = rowwise_vjp(f_gla_out, [ng], [o, r], [dog], [F32, BF16], bt=128, name=f"{tag}_dout")
    dq, dk, dv, dla = gla_scan_bwd(proj, la, ss, do, nc=_GLA_NC, name=f"{tag}_dscan")
    alow = (proj, LANE, 2 * (GLA_QK + GLA_VD) // LANE)
    (dwa2, dba), (dalow,) = rowwise_vjp(f_gla_gate_in, [w_a2, b_a], [alow], [dla], [BF16], bt=512, name=f"{tag}_dgate")
    dproj = jnp.concatenate([dq, dk, dv, dr, dalow], axis=1)
    dwin = matmul(hn, dproj, "tn", name=f"{tag}_dwin")[0]
    dh, dgm = matmul_nt_norm_bwd(dproj, w_in, h, gm, d, name=f"{tag}_dhn")
    return dh, dgm, dwin, dwa2[:GLA_RANK], dba, dng, dwout


def ssd_fwd(h, gm, w_in, conv_w, conv_b, dtb, alog, dsk, ng, w_out, tag):
    hn = _norm_bf16(h, gm, f"{tag}_norm")
    proj = matmul(hn, w_in[None], "nn", name=f"{tag}_in")
    xbc = ssd_conv_fwd(proj, conv_w, conv_b, name=f"{tag}_conv")
    y, hs = ssd_scan_fwd(xbc, proj, dtb, alog, dsk, nu=_SSD_NU, name=f"{tag}_scan")
    z = (proj, SSD_DI, 0)
    yg = rowwise(f_ssd_out, [ng], [y, z], [(SSD_DI, BF16)], bt=256, name=f"{tag}_out")[0]
    h2 = matmul(yg, w_out[None], "nn", add=h, name=f"{tag}_proj")
    return h2, (h, hn, proj, xbc, y, hs, yg)


def ssd_bwd(d, saved, gm, w_in, conv_w, conv_b, dtb, alog, dsk, ng, w_out, tag):
    h, hn, proj, xbc, y, hs, yg = saved
    dyg = matmul(d, w_out[None], "nt", name=f"{tag}_dyg")
    dwout = matmul(yg, d, "tn", name=f"{tag}_dwout")[0]
    z = (proj, SSD_DI, 0)
    (dng,), (dy, dz) = rowwise_vjp(f_ssd_out, [ng], [y, z], [dyg], [F32, BF16], bt=128, name=f"{tag}_dout")
    dxs, dbm, dcm, ddt, ddtb, dal, ddsk = ssd_scan_bwd(xbc, proj, dtb, alog, dsk, hs, dy, nu=_SSD_NU, name=f"{tag}_dscan")
    dxbc = jnp.concatenate([dxs, dbm, dcm], axis=1)
    dpre, dcw, dcb = ssd_conv_bwd(proj, conv_w, conv_b, dxbc, name=f"{tag}_dconv")
    dproj = jnp.concatenate([dz, dpre, ddt.astype(BF16)], axis=1)
    dwin = matmul(hn, dproj, "tn", name=f"{tag}_dwin")[0]
    dh, dgm = matmul_nt_norm_bwd(dproj, w_in, h, gm, d, name=f"{tag}_dhn")
    return (dh, dgm, dwin, dcw, dcb, ddtb[:, :SSD_H], dal[:, :SSD_H], ddsk[:, :SSD_H], dng, dwout)


_S5_NB = S5_NG // S5_BLK


def _s5_param_args(log_dt, a_re, a_im, b_re, b_im, c_im):
    n = S5_NG * S5_GS
    tr = lambda b: jnp.transpose(b, (0, 2, 1)).reshape(n, S5_P)
    return [log_dt.reshape(S5_NG, 1), a_re, a_im, tr(b_re), tr(b_im), c_im.reshape(n, S5_P)]


def _s5_blockdiag(t):
    nb, gl, a, b = t.shape
    eye = jnp.eye(gl, dtype=t.dtype)
    return (t[:, :, :, None, :] * eye[None, :, None, :, None]).reshape(nb, gl * a, gl * b)


def _s5_diag(t, a, b):
    nb = t.shape[0]
    gl = t.shape[1] // a
    eye = jnp.eye(gl, dtype=t.dtype)
    return jnp.sum(t.reshape(nb, gl, a, gl, b) * eye[None, :, None, :, None], axis=3)


def _s5_weights(bbr, bbi, c_re, cneg):
    sh = (_S5_NB, S5_BLK, S5_GS, S5_P)
    wb = jnp.concatenate([_s5_blockdiag(bbr.reshape(sh)), _s5_blockdiag(bbi.reshape(sh))], axis=2)
    tr = lambda cc: jnp.transpose(cc.reshape(sh), (0, 1, 3, 2))
    wc = jnp.concatenate([_s5_blockdiag(tr(c_re)), _s5_blockdiag(tr(cneg))], axis=1)
    return wb, wc


def s5_fwd(h, gm, prm, dsk, w_glu, tag):
    log_dt, a_re, a_im, b_re, b_im, c_re, c_im = prm
    hn = rowwise(f_rmsnorm, [gm], [h], [(D, F32)], bt=512, name=f"{tag}_norm")[0]
    pargs = _s5_param_args(log_dt, a_re, a_im, b_re, b_im, c_im)
    abr, abi, bbr, bbi, cneg = s5_param_fwd(pargs, name=f"{tag}_param")
    wb, wc = _s5_weights(bbr, bbi, c_re.reshape(S5_NG * S5_GS, S5_P), cneg)
    ar, ai = abr.reshape(_S5_NB, 1, _S5_W), abi.reshape(_S5_NB, 1, _S5_W)
    bu = matmul(hn, wb, "nn", G=_S5_NB, name=f"{tag}_bu")
    x = s5_scan_fwd(bu, ar, ai, name=f"{tag}_scan")
    ycp = matmul(x, wc, "nn", G=_S5_NB, name=f"{tag}_cx")
    yg = rowwise(f_s5_act, [dsk], [ycp, hn], [(D, BF16)], bt=512, name=f"{tag}_act")[0]
    vg = matmul(yg, w_glu[None], "nn", name=f"{tag}_glu")
    h2 = rowwise(f_glu_res, [], [vg, h], [(D, F32)], bt=512, name=f"{tag}_out")[0]
    return h2, (h, hn, pargs, wb, wc, ar, ai, x, ycp, yg, vg)


def s5_bwd(d, saved, gm, dsk, w_glu, tag):
    h, hn, pargs, wb, wc, ar, ai, x, ycp, yg, vg = saved
    _, (dvg,) = rowwise_vjp(f_glu, [], [vg], [d], [BF16], bt=256, name=f"{tag}_dout")
    dwglu = matmul(yg, dvg, "tn", name=f"{tag}_dwglu")[0]
    dyg = matmul(dvg, w_glu[None], "nt", name=f"{tag}_dyg")
    (ddsk,), (dycp, dhn1) = rowwise_vjp(f_s5_act, [dsk], [ycp, hn], [dyg], [F32, F32], bt=256, name=f"{tag}_dact")
    dx = matmul(dycp, wc, "nt", G=_S5_NB, name=f"{tag}_dx")
    dwc = matmul(x, dycp, "tn", G=_S5_NB, name=f"{tag}_dwc")
    lam, dar, dai = s5_scan_bwd(dx, x, ar, ai, name=f"{tag}_dscan")
    dwb = matmul(hn, lam, "tn", G=_S5_NB, name=f"{tag}_dwb")
    dhn2 = matmul(lam, wb, "nt", G=_S5_NB, name=f"{tag}_dhn")
    dh, dgm = _norm_bwd(h, gm, [dhn1, dhn2, d], f"{tag}_dnorm")
    n = S5_NG * S5_GS
    half = S5_BLK * S5_P
    d_bbr = _s5_diag(dwb[:, :, :half], S5_GS, S5_P).reshape(n, S5_P)
    d_bbi = _s5_diag(dwb[:, :, half:], S5_GS, S5_P).reshape(n, S5_P)
    from_c = lambda t: jnp.transpose(_s5_diag(t, S5_P, S5_GS), (0, 1, 3, 2)).reshape(n, S5_P)
    d_cre = from_c(dwc[:, :half, :])
    d_cneg = from_c(dwc[:, half:, :])
    cts = [dar.reshape(S5_NG, S5_P), dai.reshape(S5_NG, S5_P), d_bbr, d_bbi, d_cneg]
    dlog, dare, daim, dbre_t, dbim_t, dcim = s5_param_bwd(pargs, cts, name=f"{tag}_dparam")
    untr = lambda t: jnp.transpose(t.reshape(S5_NG, S5_GS, S5_P), (0, 2, 1))
    grads = (dlog.reshape(S5_NG), dare, daim, untr(dbre_t), untr(dbim_t),
             d_cre.reshape(S5_NG, S5_GS, S5_P), dcim.reshape(S5_NG, S5_GS, S5_P))
    return dh, dgm, grads, ddsk, dwglu


def _pad_last(w, n):
    return jnp.pad(w, [(0, 0)] * (w.ndim - 1) + [(0, n - w.shape[-1])])


_BIG = ("gla_w_in", "gla_w_out", "ssd_w_in", "ssd_w_out", "s5_w_glu", "ffn_w_gu", "ffn_w_down")


def interleave_gu(w):
    q = w.shape[-1] // 4
    return jnp.concatenate([w[..., :q], w[..., 2 * q:3 * q], w[..., q:2 * q], w[..., 3 * q:]], axis=-1)


def local_step(x, target, W, later_weights=None, later_grads=None):
    f32 = lambda a: a.astype(F32)
    row = lambda a: f32(a).reshape(1, -1)

    def layer_args(i):
        m, j = i % 3, i // 3
        gm = row(W["norm_mix_g"][i])
        if m == 0:
            args = (gm, W["gla_w_in"][j], jnp.pad(f32(W["gla_w_a2"][j]), ((0, LANE - GLA_RANK), (0, 0))),
                    row(W["gla_b_a"][j]), row(W["gla_norm_g"][j]), W["gla_w_out"][j])
        elif m == 1:
            pl_ = lambda a: _pad_last(row(a), LANE)
            args = (gm, W["ssd_w_in"][j], f32(W["ssd_conv_w"][j]),
                    row(W["ssd_conv_b"][j]), pl_(W["ssd_dt_bias"][j]), pl_(W["ssd_a_log"][j]), pl_(W["ssd_d"][j]),
                    row(W["ssd_norm_g"][j]), W["ssd_w_out"][j])
        else:
            prm = tuple(f32(W[k][j]) for k in ("s5_log_dt", "s5_a_re", "s5_a_im", "s5_b_re", "s5_b_im", "s5_c_re", "s5_c_im"))
            args = (gm, prm, row(W["s5_d"][j]), W["s5_w_glu"][j])
        return (m, j, args), (row(W["norm_ffn_g"][i]), W["ffn_w_gu"][i], W["ffn_w_down"][i])

    h = x
    saved, mixers, ffns = [], [], []
    for i in range(DEPTH):
        mixer, ffn = layer_args(i)
        mixers.append(mixer)
        ffns.append(ffn)
        m, j, args = mixer
        tag = f"l{i}_{('gla', 'ssd', 's5')[m]}"
        h, sm = (gla_fwd, ssd_fwd, s5_fwd)[m](h, *args, tag)
        h, sf = ffn_fwd(h, *ffn, f"l{i}_ffn")
        saved.append((sm, sf))
        if i == 0 and later_weights is not None:
            W = {**W, **later_weights(h)}
    loss, dfg, d = loss_head(h, row(W["final_norm_g"]), target, name="loss_head")

    G = {k: [None] * len(v) for k, v in W.items() if k != "final_norm_g"}
    G["final_norm_g"] = dfg.reshape(D)
    for i in reversed(range(DEPTH)):
        m, j, args = mixers[i]
        sm, sf = saved[i]
        if i == 0 and later_grads is not None:
            zero = later_grads(G)
            ffns[0] = (ffns[0][0], ffns[0][1], ffns[0][2] + zero.astype(ffns[0][2].dtype))
        d, dg, dwgu, dwd = ffn_bwd(d, sf, *ffns[i], f"l{i}_ffn")
        G["norm_ffn_g"][i], G["ffn_w_gu"][i], G["ffn_w_down"][i] = dg.reshape(D), dwgu, dwd
        tag = f"l{i}_{('gla', 'ssd', 's5')[m]}"
        if m == 0:
            d, dgm, dwin, dwa2, dba, dng, dwout = gla_bwd(d, sm, *args, tag)
            G["gla_w_in"][j], G["gla_w_a2"][j], G["gla_b_a"][j] = dwin, dwa2, dba.reshape(-1)
            G["gla_norm_g"][j], G["gla_w_out"][j] = dng.reshape(-1), dwout
        elif m == 1:
            d, dgm, dwin, dcw, dcb, ddtb, dal, ddsk, dng, dwout = ssd_bwd(d, sm, *args, tag)
            G["ssd_w_in"][j], G["ssd_conv_w"][j], G["ssd_conv_b"][j] = dwin, dcw, dcb.reshape(-1)
            G["ssd_dt_bias"][j], G["ssd_a_log"][j], G["ssd_d"][j] = ddtb.reshape(-1), dal.reshape(-1), ddsk.reshape(-1)
            G["ssd_norm_g"][j], G["ssd_w_out"][j] = dng.reshape(-1), dwout
        else:
            d, dgm, pg, ddsk, dwglu = s5_bwd(d, sm, args[0], args[2], args[3], tag)
            for k, v in zip(("s5_log_dt", "s5_a_re", "s5_a_im", "s5_b_re", "s5_b_im", "s5_c_re", "s5_c_im"), pg):
                G[k][j] = v
            G["s5_d"][j], G["s5_w_glu"][j] = ddsk.reshape(-1), dwglu
        G["norm_mix_g"][i] = dgm.reshape(D)
    grads = {k: (v if k == "final_norm_g" or k in _BIG else jnp.stack(v)) for k, v in G.items()}
    return loss, d, grads


_MESH = pl.DeviceIdType.MESH
_ANY = pl.BlockSpec(memory_space=pl.ANY)
_DMA = pltpu.SemaphoreType.DMA
_ROWS_ALIGN = 1024


def _place():
    return lax.axis_index("x"), lax.axis_index("y"), lax.axis_index("c")


def _other_chips(x, y):
    return [(1 - x, y), (x, 1 - y), (1 - x, 1 - y)]


def _remote(src, dst, send_sems, recv_sems, k, to):
    return pltpu.make_async_remote_copy(src_ref=src, dst_ref=dst, send_sem=send_sems.at[k], recv_sem=recv_sems.at[k],
                                        device_id=to, device_id_type=_MESH)


def gather_shards(loc, *, name):
    def body(in_ref, out_ref, send_sems, recv_sems, local_sem):
        x, y, c = _place()
        me, sibling = (x, y, c), (x, y, 1 - c)
        chips = _other_chips(x, y)

        def half(px, py, hc):
            return out_ref.at[2 * px + py, hc]

        mine = pltpu.make_async_copy(in_ref, out_ref.at[2 * x + y], local_sem)
        mine.start()
        first = [_remote(in_ref.at[c], half(x, y, c), send_sems, recv_sems, j, (*chip, c)) for j, chip in enumerate(chips)]
        for cp in first:
            cp.start()
        passed = [_remote(half(*chip, c), half(*chip, c), send_sems, recv_sems, 3 + j, sibling) for j, chip in enumerate(chips)]
        for j, chip in enumerate(chips):
            _remote(in_ref.at[c], half(*chip, c), send_sems, recv_sems, j, me).wait_recv()
            passed[j].start()
        for j, chip in enumerate(chips):
            _remote(in_ref.at[c], half(*chip, 1 - c), send_sems, recv_sems, 3 + j, me).wait_recv()
        for cp in first + passed:
            cp.wait_send()
        mine.wait()

    return pl.pallas_call(
        body, name=name, in_specs=[_ANY], out_specs=_ANY,
        out_shape=jax.ShapeDtypeStruct((4,) + loc.shape, loc.dtype),
        scratch_shapes=[_DMA((6,)), _DMA((6,)), _DMA(())],
    )(loc)


def _pos(px, py, perm):
    return 2 * py + px if perm else 2 * px + py


def _part(ref, kind, p, loc):
    if kind == "lead":
        return ref.at[p]
    return ref.at[:, pl.ds(pl.multiple_of(p * loc, LANE), loc)]


def _rows(ref, h, hr):
    return ref.at[pl.ds(h * hr, hr)]


def _rows_block(hr, width):
    return max(b for b in range(16, hr + 1, 16) if hr % b == 0 and (b * width <= (1 << 19) or b == 16))


def gather_big(locs, kinds, *, name):
    n = len(locs)

    def body(*refs):
        ins, outs = refs[:n], refs[n:2 * n]
        send_sems, recv_sems = refs[2 * n + 1:]
        refs[2 * n][...] = jnp.zeros_like(refs[2 * n])
        x, y, c = _place()
        me, sibling = (x, y, c), (x, y, 1 - c)
        chips = _other_chips(x, y)

        def half(i, px, py, h):
            (kind, perm), (rows, loc) = kinds[i], locs[i].shape
            return _rows(_part(outs[i], kind, _pos(px, py, perm), loc), h, rows // 2)

        sends = []
        for i in range(n):
            (kind, perm), (rows, loc) = kinds[i], locs[i].shape
            own = _part(outs[i], kind, _pos(x, y, perm), loc)
            sends.append(_remote(ins[i], own, send_sems, recv_sems, 6 * n + i, sibling))
            sends[-1].start()
            for j, chip in enumerate(chips):
                sends.append(_remote(_rows(ins[i], c, rows // 2), half(i, x, y, c), send_sems, recv_sems, 6 * i + j, (*chip, c)))
                sends[-1].start()
        for i in range(n):
            hr = locs[i].shape[0] // 2
            for j, chip in enumerate(chips):
                _remote(_rows(ins[i], c, hr), half(i, *chip, c), send_sems, recv_sems, 6 * i + j, me).wait_recv()
                sends.append(_remote(half(i, *chip, c), half(i, *chip, c), send_sems, recv_sems, 6 * i + 3 + j, sibling))
                sends[-1].start()
        for i in range(n):
            (kind, perm), (rows, loc) = kinds[i], locs[i].shape
            for j, chip in enumerate(chips):
                _remote(_rows(ins[i], c, rows // 2), half(i, *chip, 1 - c), send_sems, recv_sems, 6 * i + 3 + j, me).wait_recv()
            _remote(ins[i], _part(outs[i], kind, _pos(x, y, perm), loc), send_sems, recv_sems, 6 * n + i, me).wait_recv()
        for cp in sends:
            cp.wait_send()

    def out_shape(a, kind):
        rows, loc = a.shape
        return jax.ShapeDtypeStruct((4, rows, loc) if kind == "lead" else (rows, 4 * loc), a.dtype)

    outs = pl.pallas_call(
        body, name=name, in_specs=[_ANY] * n, out_specs=[_ANY] * n + [pl.BlockSpec(memory_space=pltpu.VMEM)],
        out_shape=[out_shape(a, k[0]) for a, k in zip(locs, kinds)] + [jax.ShapeDtypeStruct((8, LANE), F32)],
        scratch_shapes=[_DMA((7 * n,)), _DMA((7 * n,))],
    )(*locs)
    return list(outs[:n]), outs[n][0, 0]


_HBM = pl.BlockSpec(memory_space=pltpu.HBM)
_SEM = pl.BlockSpec(memory_space=pltpu.SEMAPHORE)
_EFFECT = pltpu.SideEffectType.DATAFLOW_SIDE_EFFECTING


def _in_hbm(a):
    return pltpu.with_memory_space_constraint(a, pltpu.HBM)


def _gather_ici_copies(ins, lands, kinds, shapes, send_sems, recv_sems):
    x, y, c = _place()
    sends, arrivals = [], []
    for i, ((kind, perm), (rows, loc)) in enumerate(zip(kinds, shapes)):
        hr = rows // 2
        mine = _part(lands[i], kind, _pos(x, y, perm), loc)
        sends.append(_remote(ins[i], mine, send_sems, recv_sems, 4 * i + 3, (x, y, 1 - c)))
        arrivals.append(_remote(ins[i], mine, send_sems, recv_sems, 4 * i + 3, (x, y, c)))
        for j, (px, py) in enumerate(_other_chips(x, y)):
            sends.append(_remote(_rows(ins[i], c, hr), _rows(mine, c, hr), send_sems, recv_sems, 4 * i + j, (px, py, c)))
            theirs = _rows(_part(lands[i], kind, _pos(px, py, perm), loc), c, hr)
            arrivals.append(_remote(_rows(ins[i], c, hr), theirs, send_sems, recv_sems, 4 * i + j, (x, y, c)))
    return sends, arrivals


def gather_start(locs, kinds, *, name):
    n = len(locs)
    shapes = [a.shape for a in locs]

    def land_shape(a, kind):
        rows, loc = a.shape
        return (4, rows, loc) if kind == "lead" else (rows, 4 * loc)

    def body(*refs):
        sends, _ = _gather_ici_copies(refs[:n], refs[n:2 * n], kinds, shapes, refs[2 * n], refs[2 * n + 1])
        for cp in sends:
            cp.start()
        refs[-1][...] = jnp.zeros_like(refs[-1])

    lands = [lax.empty(land_shape(a, k[0]), a.dtype) for a, k in zip(locs, kinds)]
    outs = pl.pallas_call(
        body, name=name, in_specs=[_HBM] * (2 * n), out_specs=[_SEM, _SEM] + [_HBM] * (2 * n) + [pl.BlockSpec(memory_space=pltpu.VMEM)],
        out_shape=[_DMA((4 * n,)), _DMA((4 * n,))] + [pltpu.HBM(a.shape, a.dtype) for a in locs]
        + [pltpu.HBM(l.shape, l.dtype) for l in lands] + [jax.ShapeDtypeStruct((8, LANE), F32)],
        input_output_aliases={i: 2 + i for i in range(2 * n)},
        compiler_params=pltpu.CompilerParams(has_side_effects=_EFFECT),
    )(*[_in_hbm(a) for a in locs], *[_in_hbm(l) for l in lands])
    return outs[0], outs[1], list(outs[2:2 + n]), list(outs[2 + n:2 + 2 * n]), outs[-1][0, 0]


def gather_wait(send_sems, recv_sems, locs, lands, kinds, after, *, name):
    n = len(locs)
    shapes = [a.shape for a in locs]

    def body(*refs):
        sends, arrivals = _gather_ici_copies(refs[:n], refs[n:2 * n], kinds, shapes, refs[2 * n], refs[2 * n + 1])
        for cp in sends:
            cp.wait_send()
        for cp in arrivals:
            cp.wait_recv()

    outs = pl.pallas_call(
        body, name=name, in_specs=[_HBM] * (2 * n) + [_SEM, _SEM, _ANY], out_specs=[_HBM] * (2 * n),
        out_shape=[pltpu.HBM(a.shape, a.dtype) for a in locs] + [pltpu.HBM(l.shape, l.dtype) for l in lands],
        input_output_aliases={i: i for i in range(2 * n)},
        compiler_params=pltpu.CompilerParams(has_side_effects=_EFFECT),
    )(*locs, *lands, send_sems, recv_sems, after)
    return list(outs[n:])


def gather_finish(lands, kinds, shapes, *, name):
    n = len(lands)

    def body(*refs):
        bufs = refs[n:2 * n]
        send_sems, recv_sems = refs[2 * n:]
        x, y, c = _place()
        sends = []
        for i, ((kind, perm), (rows, loc)) in enumerate(zip(kinds, shapes)):
            for j, (px, py) in enumerate(_other_chips(x, y)):
                part = _part(bufs[i], kind, _pos(px, py, perm), loc)
                sends.append(_remote(_rows(part, c, rows // 2), _rows(part, c, rows // 2), send_sems, recv_sems, 3 * i + j, (x, y, 1 - c)))
                sends[-1].start()
        for i, ((kind, perm), (rows, loc)) in enumerate(zip(kinds, shapes)):
            for j, (px, py) in enumerate(_other_chips(x, y)):
                part = _part(bufs[i], kind, _pos(px, py, perm), loc)
                _remote(_rows(part, c, rows // 2), _rows(part, 1 - c, rows // 2), send_sems, recv_sems, 3 * i + j, (x, y, c)).wait_recv()
        for cp in sends:
            cp.wait_send()

    return list(pl.pallas_call(
        body, name=name, in_specs=[_ANY] * n, out_specs=[_ANY] * n,
        out_shape=[jax.ShapeDtypeStruct(l.shape, l.dtype) for l in lands],
        input_output_aliases={i: i for i in range(n)}, scratch_shapes=[_DMA((3 * n,)), _DMA((3 * n,))],
    )(*lands))


def _scatter_copies(qs, lands, kinds, locs, send_sems, recv_sems):
    x, y, c = _place()
    sends, arrivals = [], []
    for i, (kind, perm) in enumerate(kinds):
        for j, (px, py) in enumerate(_other_chips(x, y)):
            src = _part(qs[i], kind, _pos(px, py, perm), locs[i])
            sends.append(_remote(src, lands[i].at[j], send_sems, recv_sems, 3 * i + j, (px, py, c)))
            arrivals.append(_remote(src, lands[i].at[j], send_sems, recv_sems, 3 * i + j, (x, y, c)))
    return sends, arrivals


def _scatter_land(q, kind, loc):
    return (3, q.shape[1] if kind == "lead" else q.shape[0], loc)


def scatter_start(qs, kinds, locs, *, name):
    n = len(qs)

    def body(*refs):
        sends, _ = _scatter_copies(refs[:n], refs[n:2 * n], kinds, locs, refs[2 * n], refs[2 * n + 1])
        for cp in sends:
            cp.start()
        refs[-1][...] = jnp.zeros_like(refs[-1])

    lands = [lax.empty(_scatter_land(q, k[0], l), q.dtype) for q, k, l in zip(qs, kinds, locs)]
    outs = pl.pallas_call(
        body, name=name, in_specs=[_HBM] * (2 * n), out_specs=[_SEM, _SEM] + [_HBM] * (2 * n) + [pl.BlockSpec(memory_space=pltpu.VMEM)],
        out_shape=[_DMA((3 * n,)), _DMA((3 * n,))] + [pltpu.HBM(q.shape, q.dtype) for q in qs]
        + [pltpu.HBM(l.shape, l.dtype) for l in lands] + [jax.ShapeDtypeStruct((8, LANE), F32)],
        input_output_aliases={i: 2 + i for i in range(2 * n)},
        compiler_params=pltpu.CompilerParams(has_side_effects=_EFFECT),
    )(*[_in_hbm(q) for q in qs], *[_in_hbm(l) for l in lands])
    return outs[0], outs[1], list(outs[2:2 + n]), list(outs[2 + n:2 + 2 * n]), outs[-1][0, 0]


def scatter_wait(send_sems, recv_sems, qs, lands, kinds, locs, after, *, name):
    n = len(qs)

    def body(*refs):
        sends, arrivals = _scatter_copies(refs[:n], refs[n:2 * n], kinds, locs, refs[2 * n], refs[2 * n + 1])
        for cp in sends:
            cp.wait_send()
        for cp in arrivals:
            cp.wait_recv()

    outs = pl.pallas_call(
        body, name=name, in_specs=[_HBM] * (2 * n) + [_SEM, _SEM, _ANY], out_specs=[_HBM] * (2 * n),
        out_shape=[pltpu.HBM(q.shape, q.dtype) for q in qs] + [pltpu.HBM(l.shape, l.dtype) for l in lands],
        input_output_aliases={i: i for i in range(2 * n)},
        compiler_params=pltpu.CompilerParams(has_side_effects=_EFFECT),
    )(*qs, *lands, send_sems, recv_sems, after)
    return list(outs[:n]), list(outs[n:])


def pair_swap(ps, kinds, *, name):
    n = len(ps)

    def body(*refs):
        ins, outs = refs[:n], refs[n:2 * n]
        send_sems, recv_sems = refs[2 * n:]
        x, y, c = _place()
        cps = []
        for i in range(n):
            if kinds[i][0] == "lead":
                hr = ps[i].shape[1] // 2
                src = ins[i].at[:, pl.ds((1 - c) * hr, hr)]
            else:
                hr = ps[i].shape[0] // 2
                src = _rows(ins[i], 1 - c, hr)
            cps.append(_remote(src, outs[i], send_sems, recv_sems, i, (x, y, 1 - c)))
            cps[-1].start()
        for cp in cps:
            cp.wait()

    def out_shape(a, kind):
        s = a.shape
        return jax.ShapeDtypeStruct((4, s[1] // 2, s[2]) if kind == "lead" else (s[0] // 2, s[1]), a.dtype)

    return pl.pallas_call(
        body, name=name, in_specs=[_ANY] * n, out_specs=[_ANY] * n,
        out_shape=[out_shape(a, k[0]) for a, k in zip(ps, kinds)], scratch_shapes=[_DMA((n,)), _DMA((n,))],
    )(*ps)


def pair_add(p, got, c_arr, kind, *, name):
    if kind == "lead":
        _, hr, cols = got.shape
        br = _rows_block(hr, cols)
        nb = hr // br
        grid = (4, nb)
        p_spec = pl.BlockSpec((None, br, cols), lambda s, i, cr: (s, cr[0] * nb + i, 0))
        g_spec = pl.BlockSpec((None, br, cols), lambda s, i, cr: (s, i, 0))
    else:
        hr, w = got.shape
        br = _rows_block(hr, w)
        nb = hr // br
        grid = (nb,)
        p_spec = pl.BlockSpec((br, w), lambda i, cr: (cr[0] * nb + i, 0))
        g_spec = pl.BlockSpec((br, w), lambda i, cr: (i, 0))

    def body(c_ref, p_ref, g_ref, o_ref):
        o_ref[...] = (p_ref[...] + g_ref[...]).astype(o_ref.dtype)

    return pl.pallas_call(
        body, name=name, out_shape=jax.ShapeDtypeStruct(got.shape, BF16),
        grid_spec=pltpu.PrefetchScalarGridSpec(num_scalar_prefetch=1, grid=grid, in_specs=[p_spec, g_spec], out_specs=g_spec),
        compiler_params=_cparams(len(grid)),
    )(c_arr, p, got)


def chip_scatter(qs, kinds, locs, *, name):
    n = len(qs)

    def body(*refs):
        ins, outs = refs[:n], refs[n:2 * n]
        send_sems, recv_sems = refs[2 * n:]
        x, y, c = _place()
        cps = []
        for i in range(n):
            kind, perm = kinds[i]
            for j, (px, py) in enumerate(_other_chips(x, y)):
                cps.append(_remote(_part(ins[i], kind, _pos(px, py, perm), locs[i]), outs[i].at[j], send_sems, recv_sems,
                                   3 * i + j, (px, py, c)))
                cps[-1].start()
        for cp in cps:
            cp.wait()

    def out_shape(a, kind, loc):
        hr = a.shape[1] if kind == "lead" else a.shape[0]
        return jax.ShapeDtypeStruct((3, hr, loc), a.dtype)

    return pl.pallas_call(
        body, name=name, in_specs=[_ANY] * n, out_specs=[_ANY] * n,
        out_shape=[out_shape(a, k[0], l) for a, k, l in zip(qs, kinds, locs)],
        scratch_shapes=[_DMA((3 * n,)), _DMA((3 * n,))],
    )(*qs)


def chip_add(q, r, pos_arr, c_arr, kind, loc, *, name):
    _, hr, _ = r.shape
    br = _rows_block(hr, loc)
    nb = hr // br
    if kind == "lead":
        q_spec = pl.BlockSpec((None, br, loc), lambda i, pr, cr: (pr[0], i, 0))
    else:
        q_spec = pl.BlockSpec((br, loc), lambda i, pr, cr: (i, pr[0]))
    r_spec = pl.BlockSpec((3, br, loc), lambda i, pr, cr: (0, i, 0))
    o_spec = pl.BlockSpec((br, loc), lambda i, pr, cr: (cr[0] * nb + i, 0))

    def body(p_ref, c_ref, q_ref, r_ref, o_ref):
        acc = q_ref[...].astype(F32)
        for j in range(3):
            acc = acc + r_ref[j].astype(F32)
        o_ref[...] = acc

    return pl.pallas_call(
        body, name=name, out_shape=jax.ShapeDtypeStruct((2 * hr, loc), F32),
        grid_spec=pltpu.PrefetchScalarGridSpec(num_scalar_prefetch=2, grid=(nb,), in_specs=[q_spec, r_spec], out_specs=o_spec),
        compiler_params=_cparams(1),
    )(pos_arr, c_arr, q, r)


def share_rows(fs, *, name):
    n = len(fs)

    def body(*refs):
        bufs = refs[n:2 * n]
        send_sems, recv_sems = refs[2 * n:]
        x, y, c = _place()
        cps = []
        for i in range(n):
            hr = fs[i].shape[0] // 2
            cps.append(_remote(_rows(bufs[i], c, hr), _rows(bufs[i], c, hr), send_sems, recv_sems, i, (x, y, 1 - c)))
            cps[-1].start()
        for i, cp in enumerate(cps):
            hr = fs[i].shape[0] // 2
            _remote(_rows(bufs[i], c, hr), _rows(bufs[i], 1 - c, hr), send_sems, recv_sems, i, (x, y, c)).wait_recv()
            cp.wait_send()

    return pl.pallas_call(
        body, name=name, in_specs=[_ANY] * n, out_specs=[_ANY] * n,
        out_shape=[jax.ShapeDtypeStruct(f.shape, f.dtype) for f in fs],
        input_output_aliases={i: i for i in range(n)}, scratch_shapes=[_DMA((n,)), _DMA((n,))],
    )(*fs)


def gather_all(v, *, name):
    def body(v_ref, out_ref, send_sems, recv_sems, local_sem):
        x, y, c = _place()
        flip = lambda p, m: 1 - p if m else p
        peers = [(flip(x, m & 4), flip(y, m & 2), flip(c, m & 1)) for m in range(1, 8)]
        idx = lambda p: 4 * p[0] + 2 * p[1] + p[2]
        mine = pltpu.make_async_copy(v_ref, out_ref.at[idx((x, y, c))], local_sem)
        mine.start()
        cps = [_remote(v_ref, out_ref.at[idx((x, y, c))], send_sems, recv_sems, k, p) for k, p in enumerate(peers)]
        for cp in cps:
            cp.start()
        for k, p in enumerate(peers):
            _remote(v_ref, out_ref.at[idx(p)], send_sems, recv_sems, k, p).wait_recv()
        for cp in cps:
            cp.wait_send()
        mine.wait()

    return pl.pallas_call(
        body, name=name, in_specs=[_ANY], out_specs=_ANY, out_shape=jax.ShapeDtypeStruct((8,) + v.shape, v.dtype),
        scratch_shapes=[_DMA((7,)), _DMA((7,)), _DMA(())],
    )(v)


def sum_stack(a, extra=None, *, name):
    n, R, L = a.shape
    br = _pick(R, _ROWS_ALIGN, 8)

    def body(*refs):
        a_ref, o_ref = refs[0], refs[-1]
        acc = refs[1][...] if extra is not None else a_ref[0]
        for i in range(0 if extra is not None else 1, n):
            acc = acc + a_ref[i]
        o_ref[...] = acc

    row = pl.BlockSpec((br, L), lambda i: (i, 0))
    specs = [pl.BlockSpec((n, br, L), lambda i: (0, i, 0))] + ([row] if extra is not None else [])
    args = [a] + ([extra] if extra is not None else [])
    return pl.pallas_call(body, name=name, grid=(R // br,), in_specs=specs, out_specs=row,
                          out_shape=jax.ShapeDtypeStruct((R, L), a.dtype), compiler_params=_cparams(1))(*args)


def adamw(w, g, m, v, *, name):
    shape = w.shape
    size = math.prod(shape)
    last = shape[-1]
    if last % LANE != 0 and size % LANE == 0 and size <= (1 << 20):
        last = LANE
    rows = size // last
    budget = (1 << 18) // last
    br = rows
    if rows > budget:
        br = max(c for c in range(8, budget + 1, 8) if rows % c == 0)
    v2 = lambda a: a.reshape(rows, last)

    def body(w_ref, g_ref, m_ref, v_ref, d_ref, nm_ref, nv_ref):
        gg = g_ref[...]
        nm = ADAM_B1 * m_ref[...] + (1.0 - ADAM_B1) * gg
        nv = ADAM_B2 * v_ref[...] + (1.0 - ADAM_B2) * (gg * gg)
        m_hat = nm / (1.0 - ADAM_B1 ** ADAM_STEP)
        v_hat = nv / (1.0 - ADAM_B2 ** ADAM_STEP)
        d_ref[...] = -ADAM_LR * (m_hat / (jnp.sqrt(v_hat) + ADAM_EPS) + ADAM_WD * w_ref[...])
        nm_ref[...] = nm
        nv_ref[...] = nv

    spec = pl.BlockSpec((br, last), lambda i: (i, 0))
    outs = pl.pallas_call(
        body, name=name, grid=(rows // br,), in_specs=[spec] * 4, out_specs=[spec] * 3,
        out_shape=[jax.ShapeDtypeStruct((rows, last), F32)] * 3, compiler_params=_cparams(1),
    )(v2(w), v2(g), v2(m), v2(v))
    return [o.reshape(shape) for o in outs]


_WEIGHTS = ["norm_mix_g", "norm_ffn_g", "gla_w_in", "gla_w_a2", "gla_b_a", "gla_norm_g", "gla_w_out", "ssd_w_in",
            "ssd_conv_w", "ssd_conv_b", "ssd_dt_bias", "ssd_a_log", "ssd_d", "ssd_norm_g", "ssd_w_out", "s5_log_dt",
            "s5_a_re", "s5_a_im", "s5_b_re", "s5_b_im", "s5_c_re", "s5_c_im", "s5_d", "s5_w_glu", "ffn_w_gu",
            "ffn_w_down", "final_norm_g"]
_SHARD_AXIS = {"gla_w_in": 2, "gla_w_a2": 2, "gla_b_a": 1, "gla_norm_g": 1, "gla_w_out": 1, "ssd_w_in": 2,
               "ssd_conv_w": 2, "ssd_w_out": 1, "s5_d": 1, "s5_w_glu": 2, "ffn_w_gu": 2, "ffn_w_down": 1}
_SMALL_SHARDED = [n for n in _WEIGHTS if n in _SHARD_AXIS and n not in _BIG]
_REPLICATED = [n for n in _WEIGHTS if n not in _SHARD_AXIS]
_BIG_KIND = {"gla_w_in": ("lead", False), "gla_w_out": ("lead", False), "ssd_w_in": ("lead", False),
             "ssd_w_out": ("lead", False), "s5_w_glu": ("cols", False), "ffn_w_gu": ("cols", True),
             "ffn_w_down": ("lead", False)}
_PADDED_IN = {"gla_w_in": GLA_INP, "ssd_w_in": SSD_INP}


def _to_rows(flat, parts=1):
    per = -(-flat.shape[0] // (parts * LANE * _ROWS_ALIGN)) * _ROWS_ALIGN
    flat = jnp.pad(flat, (0, parts * per * LANE - flat.shape[0]))
    return flat.reshape(parts, per, LANE)


def _big_layers(local):
    return [(n, j, local[n][j].reshape(-1, local[n].shape[-1])) for n in _BIG for j in range(local[n].shape[0])]


def _in_layer0(n, j):
    return j == 0 and n in ("gla_w_in", "gla_w_out", "ffn_w_gu", "ffn_w_down")


def _assemble(n, g):
    if n in _PADDED_IN:
        return jnp.concatenate([g[s] for s in range(4)] + [jnp.zeros((g.shape[1], _PADDED_IN[n] - 4 * g.shape[2]), BF16)], axis=1)
    if _BIG_KIND[n][0] == "lead":
        return g.reshape(4 * g.shape[1], g.shape[2])
    return g


def _gather_first(local):
    layers = _big_layers(local)
    first = [l for l in layers if _in_layer0(l[0], l[1])]
    later = [l for l in layers if not _in_layer0(l[0], l[1])]
    full = {n: [None] * local[n].shape[0] for n in _BIG}
    got, done = gather_big([w.astype(BF16) for _, _, w in first], [_BIG_KIND[n] for n, _, _ in first], name="gather_weights_first")
    for (n, j, _), g in zip(first, got):
        full[n][j] = _assemble(n, g)
    flat = jnp.concatenate([local[n].astype(F32).reshape(-1) for n in _SMALL_SHARDED])
    got = gather_shards(_to_rows(flat, 2), name="gather_small_weights").reshape(4, -1)
    off = 0
    for n in _SMALL_SHARDED:
        bs = local[n].shape
        sz = math.prod(bs)
        seg = got[:, off:off + sz].reshape((4,) + bs)
        off += sz
        ax = _SHARD_AXIS[n]
        full[n] = jnp.moveaxis(seg, 0, ax).reshape(bs[:ax] + (4 * bs[ax],) + bs[ax + 1:])
    kinds = [_BIG_KIND[n] for n, _, _ in later]
    ops = [(w + done if k == 0 else w).astype(BF16) for k, (_, _, w) in enumerate(later)]
    send_sems, recv_sems, locs, lands, zero = gather_start(ops, kinds, name="gather_weights_start")
    return full, (later, kinds, send_sems, recv_sems, locs, lands), zero


def _gather_rest(full, pending, after):
    later, kinds, send_sems, recv_sems, locs, lands = pending
    lands = gather_wait(send_sems, recv_sems, locs, lands, kinds, after, name="gather_weights_wait")
    lands = gather_finish(lands, kinds, [w.shape for _, _, w in later], name="gather_weights_finish")
    out = {n: list(full[n]) for n in _BIG}
    for (n, j, _), g in zip(later, lands):
        out[n][j] = _assemble(n, g)
    return out


def _reduce_ops(grads, local, want):
    ops = []
    for n in _BIG:
        kind = _BIG_KIND[n]
        for j, g in enumerate(grads[n]):
            if not want(n, j):
                continue
            loc = local[n].shape[-1] if kind[0] == "cols" or n in _PADDED_IN else g.shape[1]
            if n in _PADDED_IN:
                g = jnp.stack([g[:, s * loc:(s + 1) * loc] for s in range(4)])
            elif kind[0] == "lead":
                g = g.reshape(4, g.shape[0] // 4, g.shape[1])
            ops.append((n, j, kind, loc, g))
    return ops


def _pair_sums(ops, c_arr, tag):
    gots = pair_swap([o[4] for o in ops], [o[2] for o in ops], name=f"reduce_pair_swap_{tag}")
    return [pair_add(o[4], got, c_arr, o[2][0], name=f"reduce_pair_add_{o[0]}{o[1]}") for o, got in zip(ops, gots)]


def _reduce_later_start(grads, local, c):
    ops = _reduce_ops(grads, local, lambda n, j: not _in_layer0(n, j))
    c_arr = jnp.reshape(c, (1,)).astype(jnp.int32)
    qs = _pair_sums(ops, c_arr, "later")
    send_sems, recv_sems, qs, lands, zero = scatter_start(qs, [o[2] for o in ops], [o[3] for o in ops], name="reduce_scatter_start")
    return (ops, send_sems, recv_sems, qs, lands), zero


def _reduce_big(grads, local, pending, after, x, y, c):
    c_arr = jnp.reshape(c, (1,)).astype(jnp.int32)
    ops_l, send_sems, recv_sems, qs_l, lands = pending
    qs_l, rs_l = scatter_wait(send_sems, recv_sems, qs_l, lands, [o[2] for o in ops_l], [o[3] for o in ops_l], after,
                              name="reduce_scatter_wait")
    ops_f = _reduce_ops(grads, local, _in_layer0)
    qs_f = _pair_sums(ops_f, c_arr, "first")
    rs_f = list(chip_scatter(qs_f, [o[2] for o in ops_f], [o[3] for o in ops_f], name="reduce_chip_scatter_first"))
    ops = ops_l + ops_f
    fs = [chip_add(q, r, jnp.reshape(_pos(x, y, o[2][1]), (1,)).astype(jnp.int32), c_arr, o[2][0], o[3],
                   name=f"reduce_chip_add_{o[0]}{o[1]}") for o, q, r in zip(ops, qs_l + qs_f, rs_l + rs_f)]
    outs = share_rows(fs, name="reduce_share")
    red = {(o[0], o[1]): r for o, r in zip(ops, outs)}
    return {n: jnp.stack([red[(n, j)] for j in range(local[n].shape[0])]).reshape(local[n].shape) for n in _BIG}


def _reduce_small(grads, local, x, y):
    names = _REPLICATED + _SMALL_SHARDED
    flat = jnp.concatenate([grads[n].astype(F32).reshape(-1) for n in names])
    n_el = flat.shape[0]
    rows = -(-n_el // (LANE * 8)) * 8
    v = jnp.pad(flat, (0, rows * LANE - n_el)).reshape(rows, LANE)
    red = sum_stack(gather_all(v, name="reduce_small_gather"), name="reduce_small_add").reshape(-1)
    out, off = {}, 0
    for n in names:
        sz = math.prod(grads[n].shape)
        g = red[off:off + sz].reshape(grads[n].shape)
        off += sz
        if n in _SHARD_AXIS:
            ax = _SHARD_AXIS[n]
            loc = local[n].shape[ax]
            g = lax.dynamic_slice_in_dim(g, (2 * x + y) * loc, loc, axis=ax)
        out[n] = g
    return out


def kernel(x, norm_mix_g, norm_ffn_g, gla_w_in, gla_w_a2, gla_b_a, gla_norm_g, gla_w_out, ssd_w_in, ssd_conv_w, ssd_conv_b, ssd_dt_bias, ssd_a_log, ssd_d, ssd_norm_g, ssd_w_out, s5_log_dt, s5_a_re, s5_a_im, s5_b_re, s5_b_im, s5_c_re, s5_c_im, s5_d, s5_w_glu, ffn_w_gu, ffn_w_down, final_norm_g, loss_target, m_norm_mix_g, m_norm_ffn_g, m_gla_w_in, m_gla_w_a2, m_gla_b_a, m_gla_norm_g, m_gla_w_out, m_ssd_w_in, m_ssd_conv_w, m_ssd_conv_b, m_ssd_dt_bias, m_ssd_a_log, m_ssd_d, m_ssd_norm_g, m_ssd_w_out, m_s5_log_dt, m_s5_a_re, m_s5_a_im, m_s5_b_re, m_s5_b_im, m_s5_c_re, m_s5_c_im, m_s5_d, m_s5_w_glu, m_ffn_w_gu, m_ffn_w_down, m_final_norm_g, v_norm_mix_g, v_norm_ffn_g, v_gla_w_in, v_gla_w_a2, v_gla_b_a, v_gla_norm_g, v_gla_w_out, v_ssd_w_in, v_ssd_conv_w, v_ssd_conv_b, v_ssd_dt_bias, v_ssd_a_log, v_ssd_d, v_ssd_norm_g, v_ssd_w_out, v_s5_log_dt, v_s5_a_re, v_s5_a_im, v_s5_b_re, v_s5_b_im, v_s5_c_re, v_s5_c_im, v_s5_d, v_s5_w_glu, v_ffn_w_gu, v_ffn_w_down, v_final_norm_g):
    given = dict(locals())
    local = {n: given[n] for n in _WEIGHTS}
    px, py, pc = _place()

    first, gathering, zero = _gather_first(local)
    full = dict(local)
    full.update(first)
    full["norm_mix_g"] = local["norm_mix_g"] + zero
    reducing = []

    def later_grads(g):
        pending, zero = _reduce_later_start(g, local, pc)
        reducing.append(pending)
        return zero

    loss, grad_x, grads = local_step(x[0], loss_target[0], full, lambda h: _gather_rest(first, gathering, h), later_grads)
    loss = lax.psum(loss, ("x", "y", "c"))

    red = _reduce_big(grads, local, reducing[0], grad_x, px, py, pc)
    red.update(_reduce_small(grads, local, px, py))

    deltas, new_m, new_v = {}, {}, {}
    for n in _WEIGHTS:
        deltas[n], new_m[n], new_v[n] = adamw(local[n], red[n], given["m_" + n], given["v_" + n], name=f"adamw_{n}")
    return (loss, grad_x[None], *[red[n] for n in _WEIGHTS], *[deltas[n] for n in _WEIGHTS],
            *[new_m[n] for n in _WEIGHTS], *[new_v[n] for n in _WEIGHTS])
```

```python
import functools
import math

import jax
import jax.numpy as jnp
from jax import lax
from jax.experimental import pallas as pl
from jax.experimental.pallas import tpu as pltpu

F32 = jnp.float32
BF16 = jnp.bfloat16

D = 1024
DEPTH = 4
CH = 64
EPS = 1e-6
GLA_H, GLA_DK, GLA_DV, GLA_RANK, GLA_TAU = 4, 128, 256, 16, 16.0
GLA_QK = GLA_H * GLA_DK
GLA_VD = GLA_H * GLA_DV
GLA_IN = 2 * GLA_QK + 2 * GLA_VD + GLA_RANK
GLA_INP = 3200
SSD_DI, SSD_HD, SSD_H, SSD_G, SSD_N, SSD_K = 2048, 64, 32, 8, 128, 4
SSD_GN = SSD_G * SSD_N
SSD_CONV = SSD_DI + 2 * SSD_GN
SSD_IN = SSD_DI + SSD_CONV + SSD_H
SSD_INP = 6272
S5_GS, S5_NG, S5_P = 16, 64, 64
S5_BLK = 8
FFN_H = 2816
LANE = 128
VMEM_LIMIT = 52 * 1024 * 1024
_MATMUL_VMEM = 40 * 1024 * 1024

ADAM_LR, ADAM_B1, ADAM_B2, ADAM_EPS, ADAM_WD, ADAM_STEP = 0.001, 0.9, 0.999, 1e-08, 0.01, 10

_ARB = "arbitrary"


def _cparams(n):
    return pltpu.CompilerParams(dimension_semantics=(_ARB,) * n, vmem_limit_bytes=VMEM_LIMIT)


def _pick(n, target, mult=LANE):
    best = None
    for c in range(mult, min(n, target) + 1, mult):
        if n % c == 0:
            best = c
    return best if best is not None else n


_DN = {"nn": (((1,), (0,)), ((), ())), "nt": (((1,), (1,)), ((), ())), "tn": (((0,), (0,)), ((), ()))}


def _dot_raw(a, b, form):
    return lax.dot_general(a.astype(BF16), b.astype(BF16), _DN[form], preferred_element_type=F32)


@functools.partial(jax.custom_vjp, nondiff_argnums=(2,))
def bdot(a, b, form):
    return _dot_raw(a, b, form)


def _bdot_fwd(a, b, form):
    return _dot_raw(a, b, form), (a, b)


def _bdot_bwd(form, res, g):
    a, b = res
    if form == "nn":
        return _dot_raw(g, b, "nt"), _dot_raw(a, g, "tn")
    if form == "nt":
        return _dot_raw(g, b, "nn"), _dot_raw(g, a, "tn")
    return _dot_raw(b, g, "nt"), _dot_raw(a, g, "nn")


bdot.defvjp(_bdot_fwd, _bdot_bwd)


def _hdot(a, b):
    return jnp.dot(a, b, precision=lax.Precision.HIGHEST, preferred_element_type=F32)


@jax.custom_vjp
def cdot_left(c, ct, x):
    return _hdot(c, x)


def _cdl_fwd(c, ct, x):
    return _hdot(c, x), (c, ct)


def _cdl_bwd(res, g):
    c, ct = res
    return jnp.zeros_like(c), jnp.zeros_like(ct), _hdot(ct, g)


cdot_left.defvjp(_cdl_fwd, _cdl_bwd)


@jax.custom_vjp
def cdot_right(x, c, ct):
    return _hdot(x, c)


def _cdr_fwd(x, c, ct):
    return _hdot(x, c), (c, ct)


def _cdr_bwd(res, g):
    c, ct = res
    return _hdot(g, ct), jnp.zeros_like(c), jnp.zeros_like(ct)


cdot_right.defvjp(_cdr_fwd, _cdr_bwd)


def _sigmoid(x):
    return 1.0 / (1.0 + jnp.exp(-x))


def _silu(x):
    return x * _sigmoid(x)


def _softplus(x):
    return jnp.maximum(x, 0.0) + jnp.log(1.0 + jnp.exp(-jnp.abs(x)))


def _log_sigmoid(x):
    return jnp.minimum(x, 0.0) - jnp.log(1.0 + jnp.exp(-jnp.abs(x)))


def _gelu(x):
    c = math.sqrt(2.0 / math.pi)
    return 0.5 * x * (1.0 + jnp.tanh(c * (x + 0.044715 * (x * x * x))))


def _rms(x, g):
    return x * lax.rsqrt(jnp.mean(x * x, axis=-1, keepdims=True) + EPS) * g


def _iota2(shape, axis):
    return lax.broadcasted_iota(jnp.int32, shape, axis)


def matmul(a, b, form, *, name, G=1, out_dtype=F32, add=None):
    isz = lambda t: jnp.dtype(t.dtype).itemsize
    osz = jnp.dtype(out_dtype).itemsize + (isz(add) if add is not None else 0)

    def fits(bm, bn, bk):
        return 2 * (bm * bk * isz(a) + bk * bn * isz(b) + bm * bn * osz) + 4 * bm * bn <= _MATMUL_VMEM

    if form in ("nn", "nt"):
        M = a.shape[0]
        K = a.shape[1] // G
        N = b.shape[2] if form == "nn" else b.shape[1]
        bm, bn, bk = min(M, 1024), _pick(N, 1536), _pick(K, 2048)
        while not fits(bm, bn, bk) and bk % 256 == 0:
            bk //= 2
        nj, nk = N // bn, K // bk
        grid = (G, M // bm, nj, nk)
        a_spec = pl.BlockSpec((bm, bk), lambda g, i, j, k: (i, g * nk + k))
        if form == "nn":
            b_spec = pl.BlockSpec((None, bk, bn), lambda g, i, j, k: (g, k, j))
        else:
            b_spec = pl.BlockSpec((None, bn, bk), lambda g, i, j, k: (g, j, k))
        o_spec = pl.BlockSpec((bm, bn), lambda g, i, j, k: (i, g * nj + j))
        out_shape = jax.ShapeDtypeStruct((M, G * N), out_dtype)
    else:
        T = a.shape[0]
        Ka, Nb = a.shape[1] // G, b.shape[1] // G
        bm, bn, bk = _pick(Ka, 1408), _pick(Nb, 1536), min(T, 2048)
        while not fits(bm, bn, bk) and bk % 512 == 0:
            bk //= 2
        ni, nj, nk = Ka // bm, Nb // bn, T // bk
        grid = (G, ni, nj, nk)
        a_spec = pl.BlockSpec((bk, bm), lambda g, i, j, k: (k, g * ni + i))
        b_spec = pl.BlockSpec((bk, bn), lambda g, i, j, k: (k, g * nj + j))
        o_spec = pl.BlockSpec((None, bm, bn), lambda g, i, j, k: (g, i, j))
        out_shape = jax.ShapeDtypeStruct((G, Ka, Nb), out_dtype)
    has_add = add is not None

    def finish(refs, r):
        if has_add:
            r = r + refs[2][...].astype(F32)
        o_ref = refs[3] if has_add else refs[2]
        o_ref[...] = r.astype(o_ref.dtype)

    def body_one(*refs):
        finish(refs, _dot_raw(refs[0][...], refs[1][...], form))

    def body_acc(*refs):
        acc_ref = refs[-1]
        k = pl.program_id(3)

        @pl.when(k == 0)
        def _():
            acc_ref[...] = jnp.zeros_like(acc_ref)

        acc_ref[...] += _dot_raw(refs[0][...], refs[1][...], form)

        @pl.when(k == nk - 1)
        def _():
            finish(refs, acc_ref[...])

    in_specs = [a_spec, b_spec]
    args = [a, b]
    if has_add:
        in_specs.append(o_spec)
        args.append(add)
    return pl.pallas_call(
        body_one if nk == 1 else body_acc, name=name, grid=grid, in_specs=in_specs, out_specs=o_spec,
        out_shape=out_shape, scratch_shapes=[] if nk == 1 else [pltpu.VMEM((bm, bn), F32)],
        compiler_params=_cparams(4),
    )(*args)


def matmul_nt_norm_bwd(a, w, h, g, d, *, name):
    T, K = a.shape
    bm = min(T, 512)
    bk = _pick(K, 2048)
    nk = K // bk

    def body(a_ref, w_ref, h_ref, g_ref, d_ref, dh_ref, dg_ref, acc_ref):
        i, k = pl.program_id(0), pl.program_id(1)

        @pl.when((i == 0) & (k == 0))
        def _():
            dg_ref[...] = jnp.zeros_like(dg_ref)

        @pl.when(k == 0)
        def _():
            acc_ref[...] = jnp.zeros_like(acc_ref)

        acc_ref[...] += _dot_raw(a_ref[...], w_ref[...], "nt")

        @pl.when(k == nk - 1)
        def _():
            _, vjp = jax.vjp(lambda g_, h_: _rms(h_, g_), g_ref[...], h_ref[...])
            dg, dh = vjp(acc_ref[...])
            dh_ref[...] = dh + d_ref[...]
            dg_ref[...] += dg

    row = pl.BlockSpec((bm, D), lambda i, k: (i, 0))
    one = pl.BlockSpec((1, D), lambda i, k: (0, 0))
    return pl.pallas_call(
        body, name=name, grid=(T // bm, nk),
        in_specs=[pl.BlockSpec((bm, bk), lambda i, k: (i, k)), pl.BlockSpec((D, bk), lambda i, k: (0, k)), row, one, row],
        out_specs=[row, one], out_shape=[jax.ShapeDtypeStruct((T, D), F32), jax.ShapeDtypeStruct((1, D), F32)],
        scratch_shapes=[pltpu.VMEM((bm, D), F32)], compiler_params=_cparams(2),
    )(a, w, h, g, d)


def ffn_up(hn, w_il, *, name):
    T = hn.shape[0]
    bm, hb = min(T, 512), FFN_H // 2

    def body(a_ref, b_ref, act_ref, gu_ref):
        r = _dot_raw(a_ref[...], b_ref[...], "nn")
        act_ref[...] = (_silu(r[:, :hb]) * r[:, hb:]).astype(act_ref.dtype)
        gu_ref[...] = r.astype(gu_ref.dtype)

    return pl.pallas_call(
        body, name=name, grid=(2, T // bm),
        in_specs=[pl.BlockSpec((bm, D), lambda j, i: (i, 0)), pl.BlockSpec((D, 2 * hb), lambda j, i: (0, j))],
        out_specs=[pl.BlockSpec((bm, hb), lambda j, i: (i, j)), pl.BlockSpec((bm, 2 * hb), lambda j, i: (i, j))],
        out_shape=[jax.ShapeDtypeStruct((T, FFN_H), BF16), jax.ShapeDtypeStruct((T, 2 * FFN_H), BF16)],
        compiler_params=_cparams(2),
    )(hn, w_il)


_DACT_CHUNK = 512


def ffn_dact(d, w_down, gu, *, name):
    T = d.shape[0]
    bm, hb = min(T, 512), FFN_H // 2

    def body(d_ref, w_ref, gu_ref, o_ref):
        d_blk = d_ref[...].astype(BF16)
        for lo in range(0, hb, _DACT_CHUNK):
            hi = min(lo + _DACT_CHUNK, hb)
            da = _dot_raw(d_blk, w_ref[lo:hi, :], "nt")
            g, u = gu_ref[:, lo:hi].astype(F32), gu_ref[:, hb + lo:hb + hi].astype(F32)
            sg = _sigmoid(g)
            o_ref[:, lo:hi] = (da * u * (sg * (1.0 + g * (1.0 - sg)))).astype(o_ref.dtype)
            o_ref[:, hb + lo:hb + hi] = (da * (g * sg)).astype(o_ref.dtype)

    return pl.pallas_call(
        body, name=name, grid=(2, T // bm),
        in_specs=[pl.BlockSpec((bm, D), lambda j, i: (i, 0)), pl.BlockSpec((hb, D), lambda j, i: (j, 0)),
                  pl.BlockSpec((bm, 2 * hb), lambda j, i: (i, j))],
        out_specs=pl.BlockSpec((bm, 2 * hb), lambda j, i: (i, j)),
        out_shape=jax.ShapeDtypeStruct((T, 2 * FFN_H), BF16), compiler_params=_cparams(2),
    )(d, w_down, gu)


def _row_entry(e):
    return e if isinstance(e, tuple) else (e, e.shape[1], 0)


def _row_spec(bt, e):
    _, width, idx = e
    return pl.BlockSpec((bt, width), lambda i: (i, idx))


def _full_spec(p):
    return pl.BlockSpec(p.shape, lambda i: (0,) * p.ndim)


def rowwise(f, params, rows, outs, *, bt, name):
    rows = [_row_entry(e) for e in rows]
    T = rows[0][0].shape[0]
    bt = min(bt, T)
    np_, nr = len(params), len(rows)

    def body(*refs):
        p = tuple(r[...].astype(F32) for r in refs[:np_])
        rw = tuple(r[...].astype(F32) for r in refs[np_:np_ + nr])
        res = f(p, rw)
        for o_ref, o in zip(refs[np_ + nr:], res):
            o_ref[...] = o.astype(o_ref.dtype)

    res = pl.pallas_call(
        body, name=name, grid=(T // bt,),
        in_specs=[_full_spec(p) for p in params] + [_row_spec(bt, e) for e in rows],
        out_specs=[pl.BlockSpec((bt, w), lambda i: (i, 0)) for w, _ in outs],
        out_shape=[jax.ShapeDtypeStruct((T, w), dt) for w, dt in outs],
        compiler_params=_cparams(1),
    )(*params, *[e[0] for e in rows])
    return list(res)


def rowwise_vjp(f, params, rows, cts, drow_dtypes, *, bt, name):
    rows = [_row_entry(e) for e in rows]
    cts = [_row_entry(e) for e in cts]
    T = rows[0][0].shape[0]
    bt = min(bt, T)
    np_, nr, nc = len(params), len(rows), len(cts)
    want = [i for i, dt in enumerate(drow_dtypes) if dt is not None]

    def body(*refs):
        p = tuple(r[...].astype(F32) for r in refs[:np_])
        rw = tuple(r[...].astype(F32) for r in refs[np_:np_ + nr])
        ct = tuple(r[...].astype(F32) for r in refs[np_ + nr:np_ + nr + nc])
        outs = refs[np_ + nr + nc:]
        _, vjp = jax.vjp(f, p, rw)
        dp, dr = vjp(ct)

        @pl.when(pl.program_id(0) == 0)
        def _():
            for o in outs[:np_]:
                o[...] = jnp.zeros_like(o)

        for o, d in zip(outs[:np_], dp):
            o[...] += d
        for o, i in zip(outs[np_:], want):
            o[...] = dr[i].astype(o.dtype)

    res = pl.pallas_call(
        body, name=name, grid=(T // bt,),
        in_specs=[_full_spec(p) for p in params] + [_row_spec(bt, e) for e in rows] + [_row_spec(bt, e) for e in cts],
        out_specs=[_full_spec(p) for p in params] + [pl.BlockSpec((bt, rows[i][1]), lambda i_: (i_, 0)) for i in want],
        out_shape=[jax.ShapeDtypeStruct(p.shape, F32) for p in params]
        + [jax.ShapeDtypeStruct((T, rows[i][1]), drow_dtypes[i]) for i in want],
        compiler_params=_cparams(1),
    )(*params, *[e[0] for e in rows], *[e[0] for e in cts])
    res = list(res)
    return res[:np_], res[np_:]


def f_rmsnorm(p, r):
    return (_rms(r[0], p[0]),)


def f_rmsnorm_res(p, r):
    return (_rms(r[0], p[0]), r[0])


def f_swiglu(p, r):
    gu = r[0]
    return (_silu(gu[:, :FFN_H]) * gu[:, FFN_H:],)


def f_gla_gate_in(p, r):
    w_a2, b_a = p
    z = bdot(r[0], w_a2, "nn") + b_a
    return (_log_sigmoid(z) / GLA_TAU,)


def f_gla_gate_in_fwd(p, r):
    w_a2, b_a = p
    z = _dot_raw(r[0], w_a2, "nn") + b_a
    return (_log_sigmoid(z) / GLA_TAU,)


def f_gla_out(p, r):
    (ng,) = p
    o, rr = r
    parts = []
    for h in range(GLA_H):
        sl = slice(h * GLA_DV, (h + 1) * GLA_DV)
        parts.append(_rms(o[:, sl], ng[:, sl]) * _silu(rr[:, sl]))
    return (jnp.concatenate(parts, axis=1),)


def f_ssd_out(p, r):
    (ng,) = p
    y, z = r
    t = y * _silu(z)
    gsz = SSD_DI // SSD_G
    parts = []
    for g in range(SSD_G):
        sl = slice(g * gsz, (g + 1) * gsz)
        parts.append(_rms(t[:, sl], ng[:, sl]))
    return (jnp.concatenate(parts, axis=1),)


def f_s5_act(p, r):
    (dsk,) = p
    ycp, u = r
    return (_gelu(ycp + dsk * u),)


def f_glu_res(p, r):
    vg, h = r
    return (vg[:, :D] * _sigmoid(vg[:, D:]) + h,)


def f_glu(p, r):
    vg = r[0]
    return (vg[:, :D] * _sigmoid(vg[:, D:]),)


def loss_head(h, g, target, *, name):
    T = h.shape[0]
    bt = min(T, 256)

    def lossf(g_, h_, t_):
        e = _rms(h_, g_) - t_
        return (0.5 / D) * jnp.sum(e * e)

    def body(g_ref, h_ref, t_ref, loss_ref, dg_ref, dh_ref):
        @pl.when(pl.program_id(0) == 0)
        def _():
            loss_ref[...] = jnp.zeros_like(loss_ref)
            dg_ref[...] = jnp.zeros_like(dg_ref)

        val, vjp = jax.vjp(lossf, g_ref[...], h_ref[...], t_ref[...])
        dg, dh, _ = vjp(jnp.ones((), F32))
        loss_ref[...] += jnp.full(loss_ref.shape, val, F32)
        dg_ref[...] += dg
        dh_ref[...] = dh

    row = pl.BlockSpec((bt, D), lambda i: (i, 0))
    one = pl.BlockSpec((1, D), lambda i: (0, 0))
    loss, dg, dh = pl.pallas_call(
        body, name=name, grid=(T // bt,), in_specs=[one, row, row],
        out_specs=[pl.BlockSpec((1, LANE), lambda i: (0, 0)), one, row],
        out_shape=[jax.ShapeDtypeStruct((1, LANE), F32), jax.ShapeDtypeStruct((1, D), F32),
                   jax.ShapeDtypeStruct((T, D), F32)],
        compiler_params=_cparams(1),
    )(g, h, target)
    return loss[0, 0], dg, dh


def _gla_consts():
    r, c = _iota2((CH, CH), 0), _iota2((CH, CH), 1)
    return (r >= c).astype(F32), (r <= c).astype(F32), r >= c


def _gla_chunk(q, k, v, la, st, consts, dot, cdl):
    L, Lt, tril = consts
    lc = cdl(L, Lt, la)
    lend = lc[CH - 1:CH, :]
    e, ei = jnp.exp(lc), jnp.exp(-lc)
    qs = q * (GLA_DK ** -0.5)
    qf, kf, qb, kb = qs * e, k * ei, qs * ei, k * e
    sc = jnp.where(tril, dot(qf, kf, "nt"), dot(qb, kb, "nt"))
    o = dot(sc, v, "nn") + dot(qf, st, "nt")
    kd = k * jnp.exp(lend - lc)
    st_new = st * jnp.exp(lend) + dot(v, kd, "tn")
    return o, st_new


def _gla_block(q, k, v, la, st, nc, dot, cdl):
    consts = _gla_consts()
    outs = []
    for c in range(nc):
        sl = slice(c * CH, (c + 1) * CH)
        o, st = _gla_chunk(q[sl], k[sl], v[sl], la[sl], st, consts, dot, cdl)
        outs.append(o)
    return jnp.concatenate(outs, axis=0), st


_GLA_HP = 2


def _gla_specs(rows, rev, nb):
    t = (lambda j: nb - 1 - j) if rev else (lambda j: j)
    hp, ng = _GLA_HP, GLA_H // _GLA_HP
    q = pl.BlockSpec((rows, hp * GLA_DK), lambda h, j: (t(j), h))
    k = pl.BlockSpec((rows, hp * GLA_DK), lambda h, j: (t(j), ng + h))
    v = pl.BlockSpec((rows, hp * GLA_DV), lambda h, j: (t(j), ng + h))
    la = pl.BlockSpec((rows, hp * GLA_DK), lambda h, j: (t(j), h))
    ss = pl.BlockSpec((None, hp, GLA_DV, GLA_DK), lambda h, j: (t(j), h, 0, 0))
    o = pl.BlockSpec((rows, hp * GLA_DV), lambda h, j: (t(j), h))
    return q, k, v, la, ss, o


def _gla_heads(q, k, v, la, sts, nc, dot, cdl):
    outs, new = [], []
    for i in range(_GLA_HP):
        kk, vv = slice(i * GLA_DK, (i + 1) * GLA_DK), slice(i * GLA_DV, (i + 1) * GLA_DV)
        o, st = _gla_block(q[:, kk], k[:, kk], v[:, vv], la[:, kk], sts[i], nc, dot, cdl)
        outs.append(o)
        new.append(st)
    return jnp.concatenate(outs, axis=1), tuple(new)


def gla_scan_fwd(proj, la, *, nc, name):
    T = proj.shape[0]
    rows = min(T, nc * CH)
    nc = rows // CH
    nb = T // rows
    q_s, k_s, v_s, la_s, ss_s, o_s = _gla_specs(rows, False, nb)

    def body(q_ref, k_ref, v_ref, la_ref, o_ref, ss_ref, st_ref):
        @pl.when(pl.program_id(1) == 0)
        def _():
            st_ref[...] = jnp.zeros_like(st_ref)

        ss_ref[...] = st_ref[...]
        sts = tuple(st_ref[i] for i in range(_GLA_HP))
        o, sts = _gla_heads(q_ref[...], k_ref[...], v_ref[...], la_ref[...], sts, nc, _dot_raw, lambda c, ct, x: _hdot(c, x))
        o_ref[...] = o
        for i in range(_GLA_HP):
            st_ref[i] = sts[i]

    return pl.pallas_call(
        body, name=name, grid=(GLA_H // _GLA_HP, nb), in_specs=[q_s, k_s, v_s, la_s], out_specs=[o_s, ss_s],
        out_shape=[jax.ShapeDtypeStruct((T, GLA_VD), F32), jax.ShapeDtypeStruct((nb, GLA_H, GLA_DV, GLA_DK), F32)],
        scratch_shapes=[pltpu.VMEM((_GLA_HP, GLA_DV, GLA_DK), F32)], compiler_params=_cparams(2),
    )(proj, proj, proj, la)


def gla_scan_bwd(proj, la, ss, do, *, nc, name):
    T = proj.shape[0]
    rows = min(T, nc * CH)
    nc = rows // CH
    nb = T // rows
    q_s, k_s, v_s, la_s, ss_s, o_s = _gla_specs(rows, True, nb)
    t = lambda j: nb - 1 - j
    dqk_s = pl.BlockSpec((rows, _GLA_HP * GLA_DK), lambda h, j: (t(j), h))

    def body(q_ref, k_ref, v_ref, la_ref, ss_ref, do_ref, dq_ref, dk_ref, dv_ref, dla_ref, dst_ref):
        @pl.when(pl.program_id(1) == 0)
        def _():
            dst_ref[...] = jnp.zeros_like(dst_ref)

        fn = lambda q, k, v, la_, *sts: _gla_heads(q, k, v, la_, sts, nc, bdot, cdot_left)
        _, vjp = jax.vjp(fn, q_ref[...], k_ref[...], v_ref[...], la_ref[...], *[ss_ref[i] for i in range(_GLA_HP)])
        dq, dk, dv, dla, *dsts = vjp((do_ref[...], tuple(dst_ref[i] for i in range(_GLA_HP))))
        dq_ref[...] = dq.astype(dq_ref.dtype)
        dk_ref[...] = dk.astype(dk_ref.dtype)
        dv_ref[...] = dv.astype(dv_ref.dtype)
        dla_ref[...] = dla
        for i in range(_GLA_HP):
            dst_ref[i] = dsts[i]

    return pl.pallas_call(
        body, name=name, grid=(GLA_H // _GLA_HP, nb), in_specs=[q_s, k_s, v_s, la_s, ss_s, o_s],
        out_specs=[dqk_s, dqk_s, o_s, dqk_s],
        out_shape=[jax.ShapeDtypeStruct((T, GLA_QK), BF16), jax.ShapeDtypeStruct((T, GLA_QK), BF16),
                   jax.ShapeDtypeStruct((T, GLA_VD), BF16), jax.ShapeDtypeStruct((T, GLA_QK), F32)],
        scratch_shapes=[pltpu.VMEM((_GLA_HP, GLA_DV, GLA_DK), F32)], compiler_params=_cparams(2),
    )(proj, proj, proj, la, ss, do)


_CONV_W = 512
_CONV_OFF = SSD_DI // _CONV_W


def _conv_pre(x, prev8, w_ref, b_ref):
    bt = x.shape[0]
    ext = jnp.concatenate([prev8, x], axis=0)
    shifted = []
    for j in range(SSD_K):
        s = SSD_K - 1 - j
        shifted.append(x if s == 0 else pltpu.roll(ext, s, 0)[8:8 + bt])
    pre = b_ref[...] + sum(w_ref[j:j + 1, :] * shifted[j] for j in range(SSD_K))
    return pre, shifted


def ssd_conv_fwd(proj, w, b, *, name):
    T = proj.shape[0]
    bt = min(T, 512)
    nb = T // bt

    def body(x_ref, w_ref, b_ref, o_ref, carry_ref):
        @pl.when(pl.program_id(1) == 0)
        def _():
            carry_ref[...] = jnp.zeros_like(carry_ref)

        x = x_ref[...]
        pre, _ = _conv_pre(x, carry_ref[...], w_ref, b_ref)
        o_ref[...] = _silu(pre)
        carry_ref[...] = x[bt - 8:, :]

    return pl.pallas_call(
        body, name=name, grid=(SSD_CONV // _CONV_W, nb),
        in_specs=[pl.BlockSpec((bt, _CONV_W), lambda c, t: (t, _CONV_OFF + c)),
                  pl.BlockSpec((SSD_K, _CONV_W), lambda c, t: (0, c)),
                  pl.BlockSpec((1, _CONV_W), lambda c, t: (0, c))],
        out_specs=pl.BlockSpec((bt, _CONV_W), lambda c, t: (t, c)),
        out_shape=jax.ShapeDtypeStruct((T, SSD_CONV), F32),
        scratch_shapes=[pltpu.VMEM((8, _CONV_W), F32)], compiler_params=_cparams(2),
    )(proj, w, b)


def ssd_conv_bwd(proj, w, b, dout, *, name):
    T = proj.shape[0]
    bt = min(T, 512)
    nb = T // bt
    r8 = bt // 8

    def body(x_ref, xp_ref, w_ref, b_ref, do_ref, dx_ref, dw_ref, db_ref, carry_ref):
        t = pl.program_id(1)

        @pl.when(t == 0)
        def _():
            carry_ref[...] = jnp.zeros_like(carry_ref)
            dw_ref[...] = jnp.zeros_like(dw_ref)
            db_ref[...] = jnp.zeros_like(db_ref)

        x = x_ref[...]
        prev8 = jnp.where(t == nb - 1, 0.0, xp_ref[...])
        pre, shifted = _conv_pre(x, prev8, w_ref, b_ref)
        sg = _sigmoid(pre)
        dpre = do_ref[...] * (sg * (1.0 + pre * (1.0 - sg)))
        ext = jnp.concatenate([dpre, carry_ref[...]], axis=0)
        dx = w_ref[SSD_K - 1:SSD_K, :] * dpre
        for j in range(SSD_K - 1):
            s = SSD_K - 1 - j
            dx = dx + w_ref[j:j + 1, :] * pltpu.roll(ext, bt + 8 - s, 0)[:bt]
        dx_ref[...] = dx.astype(dx_ref.dtype)
        dw_ref[...] += jnp.concatenate([jnp.sum(dpre * shifted[j], axis=0, keepdims=True) for j in range(SSD_K)], axis=0)
        db_ref[...] += jnp.sum(dpre, axis=0, keepdims=True)
        carry_ref[...] = dpre[:8, :]

    rt = lambda t: nb - 1 - t
    return pl.pallas_call(
        body, name=name, grid=(SSD_CONV // _CONV_W, nb),
        in_specs=[pl.BlockSpec((bt, _CONV_W), lambda c, t: (rt(t), _CONV_OFF + c)),
                  pl.BlockSpec((8, _CONV_W), lambda c, t: (jnp.maximum(rt(t) * r8 - 1, 0), _CONV_OFF + c)),
                  pl.BlockSpec((SSD_K, _CONV_W), lambda c, t: (0, c)),
                  pl.BlockSpec((1, _CONV_W), lambda c, t: (0, c)),
                  pl.BlockSpec((bt, _CONV_W), lambda c, t: (rt(t), c))],
        out_specs=[pl.BlockSpec((bt, _CONV_W), lambda c, t: (rt(t), c)),
                   pl.BlockSpec((SSD_K, _CONV_W), lambda c, t: (0, c)),
                   pl.BlockSpec((1, _CONV_W), lambda c, t: (0, c))],
        out_shape=[jax.ShapeDtypeStruct((T, SSD_CONV), BF16), jax.ShapeDtypeStruct((SSD_K, SSD_CONV), F32),
                   jax.ShapeDtypeStruct((1, SSD_CONV), F32)],
        scratch_shapes=[pltpu.VMEM((8, _CONV_W), F32)], compiler_params=_cparams(2),
    )(proj, proj, w, b, dout)


_SSD_U = 2 * CH


def _ssd_unit(xs, bm, cm, dtraw, dtb, alog, dsk, hp, g, dot, cdl, cdr):
    U, P2 = _SSD_U, 2 * SSD_HD
    r, c = _iota2((U, U), 0), _iota2((U, U), 1)
    same = (r // CH) == (c // CH)
    Lb = (same & (r >= c)).astype(F32)
    Ub = (same & (r <= c)).astype(F32)
    lane = _iota2((1, U), 1)
    lo_lane = _iota2((1, P2), 1) < SSD_HD
    lo_sub = _iota2((P2, 1), 0) < SSD_HD
    diag2 = (_iota2((CH, P2), 0) == (_iota2((CH, P2), 1) % CH)).astype(F32)

    dt = _softplus(dtraw + dtb)
    da = dt * (-jnp.exp(alog))
    cum = cdl(Lb, Ub, da)
    ys = []
    new_hp = []
    for pr in range(2):
        xs_p = xs[:, pr * P2:(pr + 1) * P2]
        cols, dts, dks = [], [], []
        for jj in range(2):
            oh_l = (lane == g * (SSD_H // SSD_G) + 2 * pr + jj).astype(F32)
            cols.append(jnp.sum(cum * oh_l, axis=1, keepdims=True))
            dts.append(jnp.sum(dt * oh_l, axis=1, keepdims=True))
            dks.append(jnp.sum(dsk * oh_l, axis=1, keepdims=True))
        dsk_p = jnp.where(lo_lane, dks[0], dks[1])
        h = hp[pr]
        yc = []
        for ci in range(2):
            sl = slice(ci * CH, (ci + 1) * CH)
            xs_c, bm_c, cm_c = xs_p[sl], bm[sl], cm[sl]
            col = jnp.where(lo_lane, cols[0][sl], cols[1][sl])
            dtc = jnp.where(lo_lane, dts[0][sl], dts[1][sl])
            row = jnp.sum(diag2 * col, axis=0, keepdims=True)
            dtrow = jnp.sum(diag2 * dtc, axis=0, keepdims=True)
            cb = dot(cm_c, jnp.concatenate([bm_c, bm_c], axis=0), "nt")
            mix = cb * jnp.exp(-jnp.abs(col - row)) * dtrow
            xbd = jnp.concatenate([jnp.where(lo_lane, xs_c, 0.0), jnp.where(lo_lane, 0.0, xs_c)], axis=0)
            y_intra = dot(mix, xbd, "nn")
            ce = jnp.where(lo_lane, cols[0][ci * CH + CH - 1:ci * CH + CH, :], cols[1][ci * CH + CH - 1:ci * CH + CH, :])
            y_inter = dot(cm_c, h, "nt") * jnp.exp(col)
            xw = xs_c * (dtc * jnp.exp(ce - col))
            ce_s = [cols[jj][ci * CH + CH - 1:ci * CH + CH, :] for jj in range(2)]
            a_p = jnp.where(lo_sub, jnp.exp(ce_s[0]), jnp.exp(ce_s[1]))
            h = a_p * h + dot(xw, bm_c, "tn")
            yc.append(y_intra + y_inter + dsk_p * xs_c)
        ys.append(jnp.concatenate(yc, axis=0))
        new_hp.append(h)
    return jnp.concatenate(ys, axis=1), tuple(new_hp)


def _ssd_block(xs, bm, cm, dtraw, dtb, alog, dsk, hp, g, nu, dot, cdl, cdr):
    outs = []
    for u in range(nu):
        sl = slice(u * _SSD_U, (u + 1) * _SSD_U)
        y, hp = _ssd_unit(xs[sl], bm[sl], cm[sl], dtraw[sl], dtb, alog, dsk, hp, g, dot, cdl, cdr)
        outs.append(y)
    return jnp.concatenate(outs, axis=0), hp


def _ssd_specs(rows, rev, nb):
    t = (lambda j: nb - 1 - j) if rev else (lambda j: j)
    gw = SSD_DI // SSD_G
    xs = pl.BlockSpec((rows, gw), lambda j, g: (t(j), g))
    bm = pl.BlockSpec((rows, SSD_N), lambda j, g: (t(j), SSD_DI // SSD_N + g))
    cm = pl.BlockSpec((rows, SSD_N), lambda j, g: (t(j), (SSD_DI + SSD_GN) // SSD_N + g))
    dtr = pl.BlockSpec((rows, LANE), lambda j, g: (t(j), (SSD_DI + SSD_CONV) // LANE))
    par = pl.BlockSpec((1, LANE), lambda j, g: (0, 0))
    hs = pl.BlockSpec((None, None, 2, 2 * SSD_HD, SSD_N), lambda j, g: (t(j), g, 0, 0, 0))
    y = pl.BlockSpec((rows, gw), lambda j, g: (t(j), g))
    return xs, bm, cm, dtr, par, hs, y


def ssd_scan_fwd(xbc, proj, dtb, alog, dsk, *, nu, name):
    T = xbc.shape[0]
    rows = min(T, nu * _SSD_U)
    nu = rows // _SSD_U
    nb = T // rows
    xs_s, bm_s, cm_s, dt_s, par_s, hs_s, y_s = _ssd_specs(rows, False, nb)

    def body(xs_ref, bm_ref, cm_ref, dt_ref, dtb_ref, al_ref, dsk_ref, y_ref, hs_ref, h_ref):
        g = pl.program_id(1)

        @pl.when(pl.program_id(0) == 0)
        def _():
            h_ref[g] = jnp.zeros(h_ref.shape[1:], F32)

        hs_ref[...] = h_ref[g]
        hp = (h_ref[g, 0], h_ref[g, 1])
        y, hp = _ssd_block(xs_ref[...], bm_ref[...], cm_ref[...], dt_ref[...], dtb_ref[...], al_ref[...], dsk_ref[...],
                           hp, g, nu, _dot_raw, lambda c, ct, x: _hdot(c, x), lambda x, c, ct: _hdot(x, c))
        y_ref[...] = y
        h_ref[g, 0] = hp[0]
        h_ref[g, 1] = hp[1]

    return pl.pallas_call(
        body, name=name, grid=(nb, SSD_G), in_specs=[xs_s, bm_s, cm_s, dt_s, par_s, par_s, par_s],
        out_specs=[y_s, hs_s],
        out_shape=[jax.ShapeDtypeStruct((T, SSD_DI), F32), jax.ShapeDtypeStruct((nb, SSD_G, 2, 2 * SSD_HD, SSD_N), F32)],
        scratch_shapes=[pltpu.VMEM((SSD_G, 2, 2 * SSD_HD, SSD_N), F32)], compiler_params=_cparams(2),
    )(xbc, xbc, xbc, proj, dtb, alog, dsk)


def ssd_scan_bwd(xbc, proj, dtb, alog, dsk, hs, dy, *, nu, name):
    T = xbc.shape[0]
    rows = min(T, nu * _SSD_U)
    nu = rows // _SSD_U
    nb = T // rows
    xs_s, bm_s, cm_s, dt_s, par_s, hs_s, y_s = _ssd_specs(rows, True, nb)
    t = lambda j: nb - 1 - j
    n_s = pl.BlockSpec((rows, SSD_N), lambda j, g: (t(j), g))
    ddt_s = pl.BlockSpec((rows, LANE), lambda j, g: (t(j), 0))

    def body(xs_ref, bm_ref, cm_ref, dt_ref, dtb_ref, al_ref, dsk_ref, hs_ref, dy_ref,
             dxs_ref, dbm_ref, dcm_ref, ddt_ref, ddtb_ref, dal_ref, ddsk_ref, dh_ref):
        j, g = pl.program_id(0), pl.program_id(1)

        @pl.when(j == 0)
        def _():
            dh_ref[g] = jnp.zeros(dh_ref.shape[1:], F32)

        @pl.when((j == 0) & (g == 0))
        def _():
            ddtb_ref[...] = jnp.zeros_like(ddtb_ref)
            dal_ref[...] = jnp.zeros_like(dal_ref)
            ddsk_ref[...] = jnp.zeros_like(ddsk_ref)

        @pl.when(g == 0)
        def _():
            ddt_ref[...] = jnp.zeros_like(ddt_ref)

        fn = lambda xs, bm, cm, dtr, dtb_, al, dsk_, h0, h1: _ssd_block(
            xs, bm, cm, dtr, dtb_, al, dsk_, (h0, h1), g, nu, bdot, cdot_left, cdot_right)
        _, vjp = jax.vjp(fn, xs_ref[...], bm_ref[...], cm_ref[...], dt_ref[...], dtb_ref[...], al_ref[...], dsk_ref[...],
                         hs_ref[0], hs_ref[1])
        dxs, dbm, dcm, ddt, ddtb, dal, ddsk, dh0, dh1 = vjp((dy_ref[...], (dh_ref[g, 0], dh_ref[g, 1])))
        dxs_ref[...] = dxs
        dbm_ref[...] = dbm
        dcm_ref[...] = dcm
        ddt_ref[...] += ddt
        ddtb_ref[...] += ddtb
        dal_ref[...] += dal
        ddsk_ref[...] += ddsk
        dh_ref[g, 0] = dh0
        dh_ref[g, 1] = dh1

    return pl.pallas_call(
        body, name=name, grid=(nb, SSD_G), in_specs=[xs_s, bm_s, cm_s, dt_s, par_s, par_s, par_s, hs_s, y_s],
        out_specs=[y_s, n_s, n_s, ddt_s, par_s, par_s, par_s],
        out_shape=[jax.ShapeDtypeStruct((T, SSD_DI), F32), jax.ShapeDtypeStruct((T, SSD_GN), F32),
                   jax.ShapeDtypeStruct((T, SSD_GN), F32), jax.ShapeDtypeStruct((T, LANE), F32),
                   jax.ShapeDtypeStruct((1, LANE), F32), jax.ShapeDtypeStruct((1, LANE), F32),
                   jax.ShapeDtypeStruct((1, LANE), F32)],
        scratch_shapes=[pltpu.VMEM((SSD_G, 2, 2 * SSD_HD, SSD_N), F32)], compiler_params=_cparams(2),
    )(xbc, xbc, xbc, proj, dtb, alog, dsk, hs, dy)


def _s5_param_f(log_dt, a_re, a_im, bre_t, bim_t, cim, cdl):
    n = S5_NG * S5_GS
    r, c = _iota2((n, S5_NG), 0), _iota2((n, S5_NG), 1)
    E = ((r // S5_GS) == c).astype(F32)
    rt, ct = _iota2((S5_NG, n), 0), _iota2((S5_NG, n), 1)
    Et = ((ct // S5_GS) == rt).astype(F32)
    step = jnp.exp(log_dt)
    mag = jnp.exp(step * a_re)
    abr = mag * jnp.cos(step * a_im)
    abi = mag * jnp.sin(step * a_im)
    den = a_re * a_re + a_im * a_im
    nr, ni = abr - 1.0, abi
    fr = (nr * a_re + ni * a_im) / den
    fi = (ni * a_re - nr * a_im) / den
    Fr, Fi = cdl(E, Et, fr), cdl(E, Et, fi)
    bbr = Fr * bre_t - Fi * bim_t
    bbi = Fr * bim_t + Fi * bre_t
    return abr, abi, bbr, bbi, -cim


def _whole(a):
    return pl.BlockSpec(a.shape, lambda: (0,) * a.ndim)


def s5_param_fwd(args, *, name):
    def body(*refs):
        res = _s5_param_f(*[r[...] for r in refs[:6]], lambda c, ct, x: _hdot(c, x))
        for o, v in zip(refs[6:], res):
            o[...] = v

    shapes = [(S5_NG, S5_P), (S5_NG, S5_P)] + [(S5_NG * S5_GS, S5_P)] * 3
    return pl.pallas_call(
        body, name=name, in_specs=[_whole(a) for a in args], out_specs=[pl.BlockSpec(s, lambda: (0, 0)) for s in shapes],
        out_shape=[jax.ShapeDtypeStruct(s, F32) for s in shapes],
        compiler_params=pltpu.CompilerParams(vmem_limit_bytes=VMEM_LIMIT),
    )(*args)


def s5_param_bwd(args, cts, *, name):
    def body(*refs):
        fn = lambda *a: _s5_param_f(*a, cdot_left)
        _, vjp = jax.vjp(fn, *[r[...] for r in refs[:6]])
        grads = vjp(tuple(r[...] for r in refs[6:11]))
        for o, v in zip(refs[11:], grads):
            o[...] = v

    return pl.pallas_call(
        body, name=name, in_specs=[_whole(a) for a in list(args) + list(cts)],
        out_specs=[_whole(a) for a in args], out_shape=[jax.ShapeDtypeStruct(a.shape, F32) for a in args],
        compiler_params=pltpu.CompilerParams(vmem_limit_bytes=VMEM_LIMIT),
    )(*args, *cts)


_S5_W = S5_BLK * S5_P


def _cmul_add(xr, xi, pr, pi, sr, si):
    return xr + (pr * sr - pi * si), xi + (pr * si + pi * sr)


def _s5_powers(ar, ai):
    pw = [(ar, ai)]
    for _ in range(7):
        qr, qi = pw[-1]
        pw.append((qr * ar - qi * ai, qr * ai + qi * ar))
    return pw


def s5_scan_fwd(u, wb, a_re, a_im, wc, *, name):
    T = u.shape[0]
    bt = min(T, 256)
    nb = T // bt

    def body(u_ref, wb_ref, ar_ref, ai_ref, wc_ref, x_ref, y_ref, bu_ref, carry_ref):
        @pl.when(pl.program_id(1) == 0)
        def _():
            carry_ref[...] = jnp.zeros_like(carry_ref)

        bu_ref[...] = _dot_raw(u_ref[...], wb_ref[...], "nn")
        ar, ai = ar_ref[...], ai_ref[...]
        pw = _s5_powers(ar, ai)
        pwr = jnp.concatenate([p[0] for p in pw], axis=0)
        pwi = jnp.concatenate([p[1] for p in pw], axis=0)
        rin = _iota2((8, _S5_W), 0)
        cr, ci = carry_ref[0:1, :], carry_ref[1:2, :]
        for t in range(bt // 8):
            sl = slice(8 * t, 8 * t + 8)
            xr, xi = bu_ref[sl, :_S5_W], bu_ref[sl, _S5_W:]
            for s in (1, 2, 4):
                m = rin >= s
                sr = jnp.where(m, pltpu.roll(xr, s, 0), 0.0)
                si = jnp.where(m, pltpu.roll(xi, s, 0), 0.0)
                xr, xi = _cmul_add(xr, xi, *pw[s - 1], sr, si)
            xr, xi = _cmul_add(xr, xi, pwr, pwi, cr, ci)
            x_ref[sl, :_S5_W] = xr
            x_ref[sl, _S5_W:] = xi
            cr, ci = xr[7:8, :], xi[7:8, :]
        carry_ref[0:1, :] = cr
        carry_ref[1:2, :] = ci
        y_ref[...] = _dot_raw(x_ref[...], wc_ref[...], "nn")

    nblk = S5_NG // S5_BLK
    blk = pl.BlockSpec((bt, 2 * _S5_W), lambda g, t: (t, g))
    col = pl.BlockSpec((bt, LANE), lambda g, t: (t, g))
    a_s = pl.BlockSpec((None, 1, _S5_W), lambda g, t: (g, 0, 0))
    wb_s = pl.BlockSpec((None, LANE, 2 * _S5_W), lambda g, t: (g, 0, 0))
    wc_s = pl.BlockSpec((None, 2 * _S5_W, LANE), lambda g, t: (g, 0, 0))
    return pl.pallas_call(
        body, name=name, grid=(nblk, nb), in_specs=[col, wb_s, a_s, a_s, wc_s], out_specs=[blk, col],
        out_shape=[jax.ShapeDtypeStruct((T, nblk * 2 * _S5_W), F32), jax.ShapeDtypeStruct((T, nblk * LANE), F32)],
        scratch_shapes=[pltpu.VMEM((bt, 2 * _S5_W), F32), pltpu.VMEM((8, _S5_W), F32)],
        compiler_params=_cparams(2),
    )(u, wb, a_re, a_im, wc)


def s5_scan_bwd(dy, x, u, wb, a_re, a_im, wc, *, name):
    T = dy.shape[0]
    bt = min(T, 256)
    nb = T // bt

    def body(dy_ref, x_ref, u_ref, wb_ref, ar_ref, ai_ref, wc_ref, du_ref, dwb_ref, dwc_ref, dar_ref, dai_ref,
             g_ref, lam_ref, carry_ref):
        @pl.when(pl.program_id(1) == 0)
        def _():
            carry_ref[...] = jnp.zeros_like(carry_ref)
            dar_ref[...] = jnp.zeros_like(dar_ref)
            dai_ref[...] = jnp.zeros_like(dai_ref)
            dwb_ref[...] = jnp.zeros_like(dwb_ref)
            dwc_ref[...] = jnp.zeros_like(dwc_ref)

        g_ref[...] = _dot_raw(dy_ref[...], wc_ref[...], "nt")
        pw = _s5_powers(ar_ref[...], -ai_ref[...])
        pwr = jnp.concatenate([p[0] for p in reversed(pw)], axis=0)
        pwi = jnp.concatenate([p[1] for p in reversed(pw)], axis=0)
        rin = _iota2((8, _S5_W), 0)
        cr, ci = carry_ref[0:1, :], carry_ref[1:2, :]
        acc_r = jnp.zeros((8, _S5_W), F32)
        acc_i = jnp.zeros((8, _S5_W), F32)
        for t in reversed(range(bt // 8)):
            sl = slice(8 * t, 8 * t + 8)
            lr, li = g_ref[sl, :_S5_W], g_ref[sl, _S5_W:]
            for s in (1, 2, 4):
                m = rin < 8 - s
                sr = jnp.where(m, pltpu.roll(lr, 8 - s, 0), 0.0)
                si = jnp.where(m, pltpu.roll(li, 8 - s, 0), 0.0)
                lr, li = _cmul_add(lr, li, *pw[s - 1], sr, si)
            lr, li = _cmul_add(lr, li, pwr, pwi, cr, ci)
            lam_ref[sl, :_S5_W] = lr
            lam_ref[sl, _S5_W:] = li
            nr = jnp.where(rin == 7, cr, pltpu.roll(lr, 7, 0))
            ni = jnp.where(rin == 7, ci, pltpu.roll(li, 7, 0))
            xr, xi = x_ref[sl, :_S5_W], x_ref[sl, _S5_W:]
            acc_r = acc_r + (xr * nr + xi * ni)
            acc_i = acc_i + (xr * ni - xi * nr)
            cr, ci = lr[0:1, :], li[0:1, :]
        carry_ref[0:1, :] = cr
        carry_ref[1:2, :] = ci
        dar_ref[...] += jnp.sum(acc_r, axis=0, keepdims=True)
        dai_ref[...] += jnp.sum(acc_i, axis=0, keepdims=True)
        lam = lam_ref[...]
        du_ref[...] = _dot_raw(lam, wb_ref[...], "nt")
        dwb_ref[...] += _dot_raw(u_ref[...], lam, "tn")
        dwc_ref[...] += _dot_raw(x_ref[...], dy_ref[...], "tn")

    nblk = S5_NG // S5_BLK
    blk = pl.BlockSpec((bt, 2 * _S5_W), lambda g, t: (nb - 1 - t, g))
    col = pl.BlockSpec((bt, LANE), lambda g, t: (nb - 1 - t, g))
    a_s = pl.BlockSpec((None, 1, _S5_W), lambda g, t: (g, 0, 0))
    wb_s = pl.BlockSpec((None, LANE, 2 * _S5_W), lambda g, t: (g, 0, 0))
    wc_s = pl.BlockSpec((None, 2 * _S5_W, LANE), lambda g, t: (g, 0, 0))
    return pl.pallas_call(
        body, name=name, grid=(nblk, nb), in_specs=[col, blk, col, wb_s, a_s, a_s, wc_s],
        out_specs=[col, wb_s, wc_s, a_s, a_s],
        out_shape=[jax.ShapeDtypeStruct((T, nblk * LANE), F32), jax.ShapeDtypeStruct((nblk, LANE, 2 * _S5_W), F32),
                   jax.ShapeDtypeStruct((nblk, 2 * _S5_W, LANE), F32), jax.ShapeDtypeStruct((nblk, 1, _S5_W), F32),
                   jax.ShapeDtypeStruct((nblk, 1, _S5_W), F32)],
        scratch_shapes=[pltpu.VMEM((bt, 2 * _S5_W), F32), pltpu.VMEM((bt, 2 * _S5_W), F32), pltpu.VMEM((8, _S5_W), F32)],
        compiler_params=_cparams(2),
    )(dy, x, u, wb, a_re, a_im, wc)


def _norm_bf16(h, g, name):
    return rowwise(f_rmsnorm, [g], [h], [(D, BF16)], bt=512, name=name)[0]


def _norm_bwd(h, g, cts, name):
    n = len(cts) - 1

    def f(p, r):
        y = _rms(r[0], p[0])
        return (y,) * n + (r[0],)

    (dg,), (dh,) = rowwise_vjp(f, [g], [h], cts, [F32], bt=256, name=name)
    return dh, dg


def ffn_fwd(h, g, w_gu, w_down, tag):
    hn = _norm_bf16(h, g, f"{tag}_norm")
    a, gu = ffn_up(hn, w_gu, name=f"{tag}_up")
    h2 = matmul(a, w_down[None], "nn", add=h, name=f"{tag}_down")
    return h2, (h, hn, gu, a)


def ffn_bwd(d, saved, g, w_gu, w_down, tag):
    h, hn, gu, a = saved
    dgu = ffn_dact(d, w_down, gu, name=f"{tag}_dact")
    dwd = matmul(a, d, "tn", name=f"{tag}_dwd")[0]
    dwgu = matmul(hn, dgu, "tn", name=f"{tag}_dwgu")[0]
    dh, dg = matmul_nt_norm_bwd(dgu, w_gu, h, g, d, name=f"{tag}_dhn")
    return dh, dg, dwgu, dwd


_GLA_NC = 4
_SSD_NU = 4


def gla_fwd(h, gm, w_in, w_a2, b_a, ng, w_out, tag):
    hn = _norm_bf16(h, gm, f"{tag}_norm")
    proj = matmul(hn, w_in[None], "nn", name=f"{tag}_in")
    alow = (proj, LANE, 2 * (GLA_QK + GLA_VD) // LANE)
    la = rowwise(f_gla_gate_in_fwd, [w_a2, b_a], [alow], [(GLA_QK, F32)], bt=512, name=f"{tag}_gate")[0]
    o, ss = gla_scan_fwd(proj, la, nc=_GLA_NC, name=f"{tag}_scan")
    r = (proj, GLA_VD, 2)
    og = rowwise(f_gla_out, [ng], [o, r], [(GLA_VD, BF16)], bt=256, name=f"{tag}_out")[0]
    h2 = matmul(og, w_out[None], "nn", add=h, name=f"{tag}_proj")
    return h2, (h, hn, proj, la, o, ss, og)


def gla_bwd(d, saved, gm, w_in, w_a2, b_a, ng, w_out, tag):
    h, hn, proj, la, o, ss, og = saved
    dog = matmul(d, w_out[None], "nt", name=f"{tag}_dog")
    dwout = matmul(og, d, "tn", name=f"{tag}_dwout")[0]
    r = (proj, GLA_VD, 2)
    (dng,), (do, dr) = rowwise_vjp(f_gla_out, [ng], [o, r], [dog], [F32, BF16], bt=128, name=f"{tag}_dout")
    dq, dk, dv, dla = gla_scan_bwd(proj, la, ss, do, nc=_GLA_NC, name=f"{tag}_dscan")
    alow = (proj, LANE, 2 * (GLA_QK + GLA_VD) // LANE)
    (dwa2, dba), (dalow,) = rowwise_vjp(f_gla_gate_in, [w_a2, b_a], [alow], [dla], [BF16], bt=512, name=f"{tag}_dgate")
    dproj = jnp.concatenate([dq, dk, dv, dr, dalow], axis=1)
    dwin = matmul(hn, dproj, "tn", name=f"{tag}_dwin")[0]
    dh, dgm = matmul_nt_norm_bwd(dproj, w_in, h, gm, d, name=f"{tag}_dhn")
    return dh, dgm, dwin, dwa2[:GLA_RANK], dba, dng, dwout


def ssd_fwd(h, gm, w_in, conv_w, conv_b, dtb, alog, dsk, ng, w_out, tag):
    hn = _norm_bf16(h, gm, f"{tag}_norm")
    proj = matmul(hn, w_in[None], "nn", name=f"{tag}_in")
    xbc = ssd_conv_fwd(proj, conv_w, conv_b, name=f"{tag}_conv")
    y, hs = ssd_scan_fwd(xbc, proj, dtb, alog, dsk, nu=_SSD_NU, name=f"{tag}_scan")
    z = (proj, SSD_DI, 0)
    yg = rowwise(f_ssd_out, [ng], [y, z], [(SSD_DI, BF16)], bt=256, name=f"{tag}_out")[0]
    h2 = matmul(yg, w_out[None], "nn", add=h, name=f"{tag}_proj")
    return h2, (h, hn, proj, xbc, y, hs, yg)


def ssd_bwd(d, saved, gm, w_in, conv_w, conv_b, dtb, alog, dsk, ng, w_out, tag):
    h, hn, proj, xbc, y, hs, yg = saved
    dyg = matmul(d, w_out[None], "nt", name=f"{tag}_dyg")
    dwout = matmul(yg, d, "tn", name=f"{tag}_dwout")[0]
    z = (proj, SSD_DI, 0)
    (dng,), (dy, dz) = rowwise_vjp(f_ssd_out, [ng], [y, z], [dyg], [F32, BF16], bt=128, name=f"{tag}_dout")
    dxs, dbm, dcm, ddt, ddtb, dal, ddsk = ssd_scan_bwd(xbc, proj, dtb, alog, dsk, hs, dy, nu=_SSD_NU, name=f"{tag}_dscan")
    dxbc = jnp.concatenate([dxs, dbm, dcm], axis=1)
    dpre, dcw, dcb = ssd_conv_bwd(proj, conv_w, conv_b, dxbc, name=f"{tag}_dconv")
    dproj = jnp.concatenate([dz, dpre, ddt.astype(BF16)], axis=1)
    dwin = matmul(hn, dproj, "tn", name=f"{tag}_dwin")[0]
    dh, dgm = matmul_nt_norm_bwd(dproj, w_in, h, gm, d, name=f"{tag}_dhn")
    return (dh, dgm, dwin, dcw, dcb, ddtb[:, :SSD_H], dal[:, :SSD_H], ddsk[:, :SSD_H], dng, dwout)


_S5_NB = S5_NG // S5_BLK


def _s5_param_args(log_dt, a_re, a_im, b_re, b_im, c_im):
    n = S5_NG * S5_GS
    tr = lambda b: jnp.transpose(b, (0, 2, 1)).reshape(n, S5_P)
    return [log_dt.reshape(S5_NG, 1), a_re, a_im, tr(b_re), tr(b_im), c_im.reshape(n, S5_P)]


def _s5_blockdiag(t):
    nb, gl, a, b = t.shape
    eye = jnp.eye(gl, dtype=t.dtype)
    return (t[:, :, :, None, :] * eye[None, :, None, :, None]).reshape(nb, gl * a, gl * b)


def _s5_diag(t, a, b):
    nb = t.shape[0]
    gl = t.shape[1] // a
    eye = jnp.eye(gl, dtype=t.dtype)
    return jnp.sum(t.reshape(nb, gl, a, gl, b) * eye[None, :, None, :, None], axis=3)


def _s5_weights(bbr, bbi, c_re, cneg):
    sh = (_S5_NB, S5_BLK, S5_GS, S5_P)
    wb = jnp.concatenate([_s5_blockdiag(bbr.reshape(sh)), _s5_blockdiag(bbi.reshape(sh))], axis=2)
    tr = lambda cc: jnp.transpose(cc.reshape(sh), (0, 1, 3, 2))
    wc = jnp.concatenate([_s5_blockdiag(tr(c_re)), _s5_blockdiag(tr(cneg))], axis=1)
    return wb, wc


def s5_fwd(h, gm, prm, dsk, w_glu, tag):
    log_dt, a_re, a_im, b_re, b_im, c_re, c_im = prm
    hn = rowwise(f_rmsnorm, [gm], [h], [(D, F32)], bt=512, name=f"{tag}_norm")[0]
    pargs = _s5_param_args(log_dt, a_re, a_im, b_re, b_im, c_im)
    abr, abi, bbr, bbi, cneg = s5_param_fwd(pargs, name=f"{tag}_param")
    wb, wc = _s5_weights(bbr, bbi, c_re.reshape(S5_NG * S5_GS, S5_P), cneg)
    ar, ai = abr.reshape(_S5_NB, 1, _S5_W), abi.reshape(_S5_NB, 1, _S5_W)
    wb, wc = wb.astype(BF16), wc.astype(BF16)
    x, ycp = s5_scan_fwd(hn, wb, ar, ai, wc, name=f"{tag}_scan")
    yg = rowwise(f_s5_act, [dsk], [ycp, hn], [(D, BF16)], bt=512, name=f"{tag}_act")[0]
    vg = matmul(yg, w_glu[None], "nn", name=f"{tag}_glu")
    h2 = rowwise(f_glu_res, [], [vg, h], [(D, F32)], bt=512, name=f"{tag}_out")[0]
    return h2, (h, hn, pargs, wb, wc, ar, ai, x, ycp, yg, vg)


def s5_bwd(d, saved, gm, dsk, w_glu, tag):
    h, hn, pargs, wb, wc, ar, ai, x, ycp, yg, vg = saved
    _, (dvg,) = rowwise_vjp(f_glu, [], [vg], [d], [BF16], bt=256, name=f"{tag}_dout")
    dwglu = matmul(yg, dvg, "tn", name=f"{tag}_dwglu")[0]
    dyg = matmul(dvg, w_glu[None], "nt", name=f"{tag}_dyg")
    (ddsk,), (dycp, dhn1) = rowwise_vjp(f_s5_act, [dsk], [ycp, hn], [dyg], [F32, F32], bt=256, name=f"{tag}_dact")
    dhn2, dwb, dwc, dar, dai = s5_scan_bwd(dycp, x, hn, wb, ar, ai, wc, name=f"{tag}_dscan")
    dh, dgm = _norm_bwd(h, gm, [dhn1, dhn2, d], f"{tag}_dnorm")
    n = S5_NG * S5_GS
    half = S5_BLK * S5_P
    d_bbr = _s5_diag(dwb[:, :, :half], S5_GS, S5_P).reshape(n, S5_P)
    d_bbi = _s5_diag(dwb[:, :, half:], S5_GS, S5_P).reshape(n, S5_P)
    from_c = lambda t: jnp.transpose(_s5_diag(t, S5_P, S5_GS), (0, 1, 3, 2)).reshape(n, S5_P)
    d_cre = from_c(dwc[:, :half, :])
    d_cneg = from_c(dwc[:, half:, :])
    cts = [dar.reshape(S5_NG, S5_P), dai.reshape(S5_NG, S5_P), d_bbr, d_bbi, d_cneg]
    dlog, dare, daim, dbre_t, dbim_t, dcim = s5_param_bwd(pargs, cts, name=f"{tag}_dparam")
    untr = lambda t: jnp.transpose(t.reshape(S5_NG, S5_GS, S5_P), (0, 2, 1))
    grads = (dlog.reshape(S5_NG), dare, daim, untr(dbre_t), untr(dbim_t),
             d_cre.reshape(S5_NG, S5_GS, S5_P), dcim.reshape(S5_NG, S5_GS, S5_P))
    return dh, dgm, grads, ddsk, dwglu


def _pad_last(w, n):
    return jnp.pad(w, [(0, 0)] * (w.ndim - 1) + [(0, n - w.shape[-1])])


_BIG = ("gla_w_in", "gla_w_out", "ssd_w_in", "ssd_w_out", "s5_w_glu", "ffn_w_gu", "ffn_w_down")


def interleave_gu(w):
    q = w.shape[-1] // 4
    return jnp.concatenate([w[..., :q], w[..., 2 * q:3 * q], w[..., q:2 * q], w[..., 3 * q:]], axis=-1)


def local_step(x, target, W, later_weights=None, later_grads=None):
    f32 = lambda a: a.astype(F32)
    row = lambda a: f32(a).reshape(1, -1)

    def layer_args(i):
        m, j = i % 3, i // 3
        gm = row(W["norm_mix_g"][i])
        if m == 0:
            args = (gm, W["gla_w_in"][j], jnp.pad(f32(W["gla_w_a2"][j]), ((0, LANE - GLA_RANK), (0, 0))),
                    row(W["gla_b_a"][j]), row(W["gla_norm_g"][j]), W["gla_w_out"][j])
        elif m == 1:
            pl_ = lambda a: _pad_last(row(a), LANE)
            args = (gm, W["ssd_w_in"][j], f32(W["ssd_conv_w"][j]),
                    row(W["ssd_conv_b"][j]), pl_(W["ssd_dt_bias"][j]), pl_(W["ssd_a_log"][j]), pl_(W["ssd_d"][j]),
                    row(W["ssd_norm_g"][j]), W["ssd_w_out"][j])
        else:
            prm = tuple(f32(W[k][j]) for k in ("s5_log_dt", "s5_a_re", "s5_a_im", "s5_b_re", "s5_b_im", "s5_c_re", "s5_c_im"))
            args = (gm, prm, row(W["s5_d"][j]), W["s5_w_glu"][j])
        return (m, j, args), (row(W["norm_ffn_g"][i]), W["ffn_w_gu"][i], W["ffn_w_down"][i])

    h = x
    saved, mixers, ffns = [], [], []
    for i in range(DEPTH):
        mixer, ffn = layer_args(i)
        mixers.append(mixer)
        ffns.append(ffn)
        m, j, args = mixer
        tag = f"l{i}_{('gla', 'ssd', 's5')[m]}"
        h, sm = (gla_fwd, ssd_fwd, s5_fwd)[m](h, *args, tag)
        h, sf = ffn_fwd(h, *ffn, f"l{i}_ffn")
        saved.append((sm, sf))
        if i == 0 and later_weights is not None:
            W = {**W, **later_weights(h)}
    loss, dfg, d = loss_head(h, row(W["final_norm_g"]), target, name="loss_head")

    G = {k: [None] * len(v) for k, v in W.items() if k != "final_norm_g"}
    G["final_norm_g"] = dfg.reshape(D)
    for i in reversed(range(DEPTH)):
        m, j, args = mixers[i]
        sm, sf = saved[i]
        if i == 0 and later_grads is not None:
            zero = later_grads(G)
            ffns[0] = (ffns[0][0], ffns[0][1], ffns[0][2] + zero.astype(ffns[0][2].dtype))
        d, dg, dwgu, dwd = ffn_bwd(d, sf, *ffns[i], f"l{i}_ffn")
        G["norm_ffn_g"][i], G["ffn_w_gu"][i], G["ffn_w_down"][i] = dg.reshape(D), dwgu, dwd
        tag = f"l{i}_{('gla', 'ssd', 's5')[m]}"
        if m == 0:
            d, dgm, dwin, dwa2, dba, dng, dwout = gla_bwd(d, sm, *args, tag)
            G["gla_w_in"][j], G["gla_w_a2"][j], G["gla_b_a"][j] = dwin, dwa2, dba.reshape(-1)
            G["gla_norm_g"][j], G["gla_w_out"][j] = dng.reshape(-1), dwout
        elif m == 1:
            d, dgm, dwin, dcw, dcb, ddtb, dal, ddsk, dng, dwout = ssd_bwd(d, sm, *args, tag)
            G["ssd_w_in"][j], G["ssd_conv_w"][j], G["ssd_conv_b"][j] = dwin, dcw, dcb.reshape(-1)
            G["ssd_dt_bias"][j], G["ssd_a_log"][j], G["ssd_d"][j] = ddtb.reshape(-1), dal.reshape(-1), ddsk.reshape(-1)
            G["ssd_norm_g"][j], G["ssd_w_out"][j] = dng.reshape(-1), dwout
        else:
            d, dgm, pg, ddsk, dwglu = s5_bwd(d, sm, args[0], args[2], args[3], tag)
            for k, v in zip(("s5_log_dt", "s5_a_re", "s5_a_im", "s5_b_re", "s5_b_im", "s5_c_re", "s5_c_im"), pg):
                G[k][j] = v
            G["s5_d"][j], G["s5_w_glu"][j] = ddsk.reshape(-1), dwglu
        G["norm_mix_g"][i] = dgm.reshape(D)
    grads = {k: (v if k == "final_norm_g" or k in _BIG else jnp.stack(v)) for k, v in G.items()}
    return loss, d, grads


_MESH = pl.DeviceIdType.MESH
_ANY = pl.BlockSpec(memory_space=pl.ANY)
_DMA = pltpu.SemaphoreType.DMA
_ROWS_ALIGN = 1024


def _place():
    return lax.axis_index("x"), lax.axis_index("y"), lax.axis_index("c")


def _other_chips(x, y):
    return [(1 - x, y), (x, 1 - y), (1 - x, 1 - y)]


def _remote(src, dst, send_sems, recv_sems, k, to):
    return pltpu.make_async_remote_copy(src_ref=src, dst_ref=dst, send_sem=send_sems.at[k], recv_sem=recv_sems.at[k],
                                        device_id=to, device_id_type=_MESH)


def gather_shards(loc, *, name):
    def body(in_ref, out_ref, send_sems, recv_sems, local_sem):
        x, y, c = _place()
        me, sibling = (x, y, c), (x, y, 1 - c)
        chips = _other_chips(x, y)

        def half(px, py, hc):
            return out_ref.at[2 * px + py, hc]

        mine = pltpu.make_async_copy(in_ref, out_ref.at[2 * x + y], local_sem)
        mine.start()
        first = [_remote(in_ref.at[c], half(x, y, c), send_sems, recv_sems, j, (*chip, c)) for j, chip in enumerate(chips)]
        for cp in first:
            cp.start()
        passed = [_remote(half(*chip, c), half(*chip, c), send_sems, recv_sems, 3 + j, sibling) for j, chip in enumerate(chips)]
        for j, chip in enumerate(chips):
            _remote(in_ref.at[c], half(*chip, c), send_sems, recv_sems, j, me).wait_recv()
            passed[j].start()
        for j, chip in enumerate(chips):
            _remote(in_ref.at[c], half(*chip, 1 - c), send_sems, recv_sems, 3 + j, me).wait_recv()
        for cp in first + passed:
            cp.wait_send()
        mine.wait()

    return pl.pallas_call(
        body, name=name, in_specs=[_ANY], out_specs=_ANY,
        out_shape=jax.ShapeDtypeStruct((4,) + loc.shape, loc.dtype),
        scratch_shapes=[_DMA((6,)), _DMA((6,)), _DMA(())],
    )(loc)


def _pos(px, py, perm):
    return 2 * py + px if perm else 2 * px + py


def _part(ref, kind, p, loc):
    if kind == "lead":
        return ref.at[p]
    return ref.at[:, pl.ds(pl.multiple_of(p * loc, LANE), loc)]


def _rows(ref, h, hr):
    return ref.at[pl.ds(h * hr, hr)]


def _rows_block(hr, width):
    return max(b for b in range(16, hr + 1, 16) if hr % b == 0 and (b * width <= (1 << 19) or b == 16))


def gather_big(locs, kinds, *, name):
    n = len(locs)

    def body(*refs):
        ins, outs = refs[:n], refs[n:2 * n]
        send_sems, recv_sems = refs[2 * n + 1:]
        refs[2 * n][...] = jnp.zeros_like(refs[2 * n])
        x, y, c = _place()
        me, sibling = (x, y, c), (x, y, 1 - c)
        chips = _other_chips(x, y)

        def half(i, px, py, h):
            (kind, perm), (rows, loc) = kinds[i], locs[i].shape
            return _rows(_part(outs[i], kind, _pos(px, py, perm), loc), h, rows // 2)

        sends = []
        for i in range(n):
            (kind, perm), (rows, loc) = kinds[i], locs[i].shape
            own = _part(outs[i], kind, _pos(x, y, perm), loc)
            sends.append(_remote(ins[i], own, send_sems, recv_sems, 6 * n + i, sibling))
            sends[-1].start()
            for j, chip in enumerate(chips):
                sends.append(_remote(_rows(ins[i], c, rows // 2), half(i, x, y, c), send_sems, recv_sems, 6 * i + j, (*chip, c)))
                sends[-1].start()
        for i in range(n):
            hr = locs[i].shape[0] // 2
            for j, chip in enumerate(chips):
                _remote(_rows(ins[i], c, hr), half(i, *chip, c), send_sems, recv_sems, 6 * i + j, me).wait_recv()
                sends.append(_remote(half(i, *chip, c), half(i, *chip, c), send_sems, recv_sems, 6 * i + 3 + j, sibling))
                sends[-1].start()
        for i in range(n):
            (kind, perm), (rows, loc) = kinds[i], locs[i].shape
            for j, chip in enumerate(chips):
                _remote(_rows(ins[i], c, rows // 2), half(i, *chip, 1 - c), send_sems, recv_sems, 6 * i + 3 + j, me).wait_recv()
            _remote(ins[i], _part(outs[i], kind, _pos(x, y, perm), loc), send_sems, recv_sems, 6 * n + i, me).wait_recv()
        for cp in sends:
            cp.wait_send()

    def out_shape(a, kind):
        rows, loc = a.shape
        return jax.ShapeDtypeStruct((4, rows, loc) if kind == "lead" else (rows, 4 * loc), a.dtype)

    outs = pl.pallas_call(
        body, name=name, in_specs=[_ANY] * n, out_specs=[_ANY] * n + [pl.BlockSpec(memory_space=pltpu.VMEM)],
        out_shape=[out_shape(a, k[0]) for a, k in zip(locs, kinds)] + [jax.ShapeDtypeStruct((8, LANE), F32)],
        scratch_shapes=[_DMA((7 * n,)), _DMA((7 * n,))],
    )(*locs)
    return list(outs[:n]), outs[n][0, 0]


_HBM = pl.BlockSpec(memory_space=pltpu.HBM)
_SEM = pl.BlockSpec(memory_space=pltpu.SEMAPHORE)
_EFFECT = pltpu.SideEffectType.DATAFLOW_SIDE_EFFECTING


def _in_hbm(a):
    return pltpu.with_memory_space_constraint(a, pltpu.HBM)


def _gather_ici_copies(ins, lands, kinds, shapes, send_sems, recv_sems):
    x, y, c = _place()
    sends, arrivals = [], []
    for i, ((kind, perm), (rows, loc)) in enumerate(zip(kinds, shapes)):
        hr = rows // 2
        mine = _part(lands[i], kind, _pos(x, y, perm), loc)
        sends.append(_remote(ins[i], mine, send_sems, recv_sems, 4 * i + 3, (x, y, 1 - c)))
        arrivals.append(_remote(ins[i], mine, send_sems, recv_sems, 4 * i + 3, (x, y, c)))
        for j, (px, py) in enumerate(_other_chips(x, y)):
            sends.append(_remote(_rows(ins[i], c, hr), _rows(mine, c, hr), send_sems, recv_sems, 4 * i + j, (px, py, c)))
            theirs = _rows(_part(lands[i], kind, _pos(px, py, perm), loc), c, hr)
            arrivals.append(_remote(_rows(ins[i], c, hr), theirs, send_sems, recv_sems, 4 * i + j, (x, y, c)))
    return sends, arrivals


def gather_start(locs, kinds, *, name):
    n = len(locs)
    shapes = [a.shape for a in locs]

    def land_shape(a, kind):
        rows, loc = a.shape
        return (4, rows, loc) if kind == "lead" else (rows, 4 * loc)

    def body(*refs):
        sends, _ = _gather_ici_copies(refs[:n], refs[n:2 * n], kinds, shapes, refs[2 * n], refs[2 * n + 1])
        for cp in sends:
            cp.start()
        refs[-1][...] = jnp.zeros_like(refs[-1])

    lands = [lax.empty(land_shape(a, k[0]), a.dtype) for a, k in zip(locs, kinds)]
    outs = pl.pallas_call(
        body, name=name, in_specs=[_HBM] * (2 * n), out_specs=[_SEM, _SEM] + [_HBM] * (2 * n) + [pl.BlockSpec(memory_space=pltpu.VMEM)],
        out_shape=[_DMA((4 * n,)), _DMA((4 * n,))] + [pltpu.HBM(a.shape, a.dtype) for a in locs]
        + [pltpu.HBM(l.shape, l.dtype) for l in lands] + [jax.ShapeDtypeStruct((8, LANE), F32)],
        input_output_aliases={i: 2 + i for i in range(2 * n)},
        compiler_params=pltpu.CompilerParams(has_side_effects=_EFFECT),
    )(*[_in_hbm(a) for a in locs], *[_in_hbm(l) for l in lands])
    return outs[0], outs[1], list(outs[2:2 + n]), list(outs[2 + n:2 + 2 * n]), outs[-1][0, 0]


def gather_wait(send_sems, recv_sems, locs, lands, kinds, after, *, name):
    n = len(locs)
    shapes = [a.shape for a in locs]

    def body(*refs):
        sends, arrivals = _gather_ici_copies(refs[:n], refs[n:2 * n], kinds, shapes, refs[2 * n], refs[2 * n + 1])
        for cp in sends:
            cp.wait_send()
        for cp in arrivals:
            cp.wait_recv()

    outs = pl.pallas_call(
        body, name=name, in_specs=[_HBM] * (2 * n) + [_SEM, _SEM, _ANY], out_specs=[_HBM] * (2 * n),
        out_shape=[pltpu.HBM(a.shape, a.dtype) for a in locs] + [pltpu.HBM(l.shape, l.dtype) for l in lands],
        input_output_aliases={i: i for i in range(2 * n)},
        compiler_params=pltpu.CompilerParams(has_side_effects=_EFFECT),
    )(*locs, *lands, send_sems, recv_sems, after)
    return list(outs[n:])


def gather_finish(lands, kinds, shapes, *, name):
    n = len(lands)

    def body(*refs):
        bufs = refs[n:2 * n]
        send_sems, recv_sems = refs[2 * n:]
        x, y, c = _place()
        sends = []
        for i, ((kind, perm), (rows, loc)) in enumerate(zip(kinds, shapes)):
            for j, (px, py) in enumerate(_other_chips(x, y)):
                part = _part(bufs[i], kind, _pos(px, py, perm), loc)
                sends.append(_remote(_rows(part, c, rows // 2), _rows(part, c, rows // 2), send_sems, recv_sems, 3 * i + j, (x, y, 1 - c)))
                sends[-1].start()
        for i, ((kind, perm), (rows, loc)) in enumerate(zip(kinds, shapes)):
            for j, (px, py) in enumerate(_other_chips(x, y)):
                part = _part(bufs[i], kind, _pos(px, py, perm), loc)
                _remote(_rows(part, c, rows // 2), _rows(part, 1 - c, rows // 2), send_sems, recv_sems, 3 * i + j, (x, y, c)).wait_recv()
        for cp in sends:
            cp.wait_send()

    return list(pl.pallas_call(
        body, name=name, in_specs=[_ANY] * n, out_specs=[_ANY] * n,
        out_shape=[jax.ShapeDtypeStruct(l.shape, l.dtype) for l in lands],
        input_output_aliases={i: i for i in range(n)}, scratch_shapes=[_DMA((3 * n,)), _DMA((3 * n,))],
    )(*lands))


def _scatter_copies(qs, lands, kinds, locs, send_sems, recv_sems):
    x, y, c = _place()
    sends, arrivals = [], []
    for i, (kind, perm) in enumerate(kinds):
        for j, (px, py) in enumerate(_other_chips(x, y)):
            src = _part(qs[i], kind, _pos(px, py, perm), locs[i])
            sends.append(_remote(src, lands[i].at[j], send_sems, recv_sems, 3 * i + j, (px, py, c)))
            arrivals.append(_remote(src, lands[i].at[j], send_sems, recv_sems, 3 * i + j, (x, y, c)))
    return sends, arrivals


def _scatter_land(q, kind, loc):
    return (3, q.shape[1] if kind == "lead" else q.shape[0], loc)


def scatter_start(qs, kinds, locs, *, name):
    n = len(qs)

    def body(*refs):
        sends, _ = _scatter_copies(refs[:n], refs[n:2 * n], kinds, locs, refs[2 * n], refs[2 * n + 1])
        for cp in sends:
            cp.start()
        refs[-1][...] = jnp.zeros_like(refs[-1])

    lands = [lax.empty(_scatter_land(q, k[0], l), q.dtype) for q, k, l in zip(qs, kinds, locs)]
    outs = pl.pallas_call(
        body, name=name, in_specs=[_HBM] * (2 * n), out_specs=[_SEM, _SEM] + [_HBM] * (2 * n) + [pl.BlockSpec(memory_space=pltpu.VMEM)],
        out_shape=[_DMA((3 * n,)), _DMA((3 * n,))] + [pltpu.HBM(q.shape, q.dtype) for q in qs]
        + [pltpu.HBM(l.shape, l.dtype) for l in lands] + [jax.ShapeDtypeStruct((8, LANE), F32)],
        input_output_aliases={i: 2 + i for i in range(2 * n)},
        compiler_params=pltpu.CompilerParams(has_side_effects=_EFFECT),
    )(*[_in_hbm(q) for q in qs], *[_in_hbm(l) for l in lands])
    return outs[0], outs[1], list(outs[2:2 + n]), list(outs[2 + n:2 + 2 * n]), outs[-1][0, 0]


def scatter_wait(send_sems, recv_sems, qs, lands, kinds, locs, after, *, name):
    n = len(qs)

    def body(*refs):
        sends, arrivals = _scatter_copies(refs[:n], refs[n:2 * n], kinds, locs, refs[2 * n], refs[2 * n + 1])
        for cp in sends:
            cp.wait_send()
        for cp in arrivals:
            cp.wait_recv()

    outs = pl.pallas_call(
        body, name=name, in_specs=[_HBM] * (2 * n) + [_SEM, _SEM, _ANY], out_specs=[_HBM] * (2 * n),
        out_shape=[pltpu.HBM(q.shape, q.dtype) for q in qs] + [pltpu.HBM(l.shape, l.dtype) for l in lands],
        input_output_aliases={i: i for i in range(2 * n)},
        compiler_params=pltpu.CompilerParams(has_side_effects=_EFFECT),
    )(*qs, *lands, send_sems, recv_sems, after)
    return list(outs[:n]), list(outs[n:])


def pair_swap(ps, kinds, *, name):
    n = len(ps)

    def body(*refs):
        ins, outs = refs[:n], refs[n:2 * n]
        send_sems, recv_sems = refs[2 * n:]
        x, y, c = _place()
        cps = []
        for i in range(n):
            if kinds[i][0] == "lead":
                hr = ps[i].shape[1] // 2
                src = ins[i].at[:, pl.ds((1 - c) * hr, hr)]
            else:
                hr = ps[i].shape[0] // 2
                src = _rows(ins[i], 1 - c, hr)
            cps.append(_remote(src, outs[i], send_sems, recv_sems, i, (x, y, 1 - c)))
            cps[-1].start()
        for cp in cps:
            cp.wait()

    def out_shape(a, kind):
        s = a.shape
        return jax.ShapeDtypeStruct((4, s[1] // 2, s[2]) if kind == "lead" else (s[0] // 2, s[1]), a.dtype)

    return pl.pallas_call(
        body, name=name, in_specs=[_ANY] * n, out_specs=[_ANY] * n,
        out_shape=[out_shape(a, k[0]) for a, k in zip(ps, kinds)], scratch_shapes=[_DMA((n,)), _DMA((n,))],
    )(*ps)


def pair_add(p, got, c_arr, kind, *, name):
    if kind == "lead":
        _, hr, cols = got.shape
        br = _rows_block(hr, cols)
        nb = hr // br
        grid = (4, nb)
        p_spec = pl.BlockSpec((None, br, cols), lambda s, i, cr: (s, cr[0] * nb + i, 0))
        g_spec = pl.BlockSpec((None, br, cols), lambda s, i, cr: (s, i, 0))
    else:
        hr, w = got.shape
        br = _rows_block(hr, w)
        nb = hr // br
        grid = (nb,)
        p_spec = pl.BlockSpec((br, w), lambda i, cr: (cr[0] * nb + i, 0))
        g_spec = pl.BlockSpec((br, w), lambda i, cr: (i, 0))

    def body(c_ref, p_ref, g_ref, o_ref):
        o_ref[...] = (p_ref[...] + g_ref[...]).astype(o_ref.dtype)

    return pl.pallas_call(
        body, name=name, out_shape=jax.ShapeDtypeStruct(got.shape, BF16),
        grid_spec=pltpu.PrefetchScalarGridSpec(num_scalar_prefetch=1, grid=grid, in_specs=[p_spec, g_spec], out_specs=g_spec),
        compiler_params=_cparams(len(grid)),
    )(c_arr, p, got)


def chip_scatter(qs, kinds, locs, *, name):
    n = len(qs)

    def body(*refs):
        ins, outs = refs[:n], refs[n:2 * n]
        send_sems, recv_sems = refs[2 * n:]
        x, y, c = _place()
        cps = []
        for i in range(n):
            kind, perm = kinds[i]
            for j, (px, py) in enumerate(_other_chips(x, y)):
                cps.append(_remote(_part(ins[i], kind, _pos(px, py, perm), locs[i]), outs[i].at[j], send_sems, recv_sems,
                                   3 * i + j, (px, py, c)))
                cps[-1].start()
        for cp in cps:
            cp.wait()

    def out_shape(a, kind, loc):
        hr = a.shape[1] if kind == "lead" else a.shape[0]
        return jax.ShapeDtypeStruct((3, hr, loc), a.dtype)

    return pl.pallas_call(
        body, name=name, in_specs=[_ANY] * n, out_specs=[_ANY] * n,
        out_shape=[out_shape(a, k[0], l) for a, k, l in zip(qs, kinds, locs)],
        scratch_shapes=[_DMA((3 * n,)), _DMA((3 * n,))],
    )(*qs)


def chip_add(q, r, pos_arr, c_arr, kind, loc, *, name):
    _, hr, _ = r.shape
    br = _rows_block(hr, loc)
    nb = hr // br
    if kind == "lead":
        q_spec = pl.BlockSpec((None, br, loc), lambda i, pr, cr: (pr[0], i, 0))
    else:
        q_spec = pl.BlockSpec((br, loc), lambda i, pr, cr: (i, pr[0]))
    r_spec = pl.BlockSpec((3, br, loc), lambda i, pr, cr: (0, i, 0))
    o_spec = pl.BlockSpec((br, loc), lambda i, pr, cr: (cr[0] * nb + i, 0))

    def body(p_ref, c_ref, q_ref, r_ref, o_ref):
        acc = q_ref[...].astype(F32)
        for j in range(3):
            acc = acc + r_ref[j].astype(F32)
        o_ref[...] = acc

    return pl.pallas_call(
        body, name=name, out_shape=jax.ShapeDtypeStruct((2 * hr, loc), F32),
        grid_spec=pltpu.PrefetchScalarGridSpec(num_scalar_prefetch=2, grid=(nb,), in_specs=[q_spec, r_spec], out_specs=o_spec),
        compiler_params=_cparams(1),
    )(pos_arr, c_arr, q, r)


def share_rows(fs, *, name):
    n = len(fs)

    def body(*refs):
        bufs = refs[n:2 * n]
        send_sems, recv_sems = refs[2 * n:]
        x, y, c = _place()
        cps = []
        for i in range(n):
            hr = fs[i].shape[0] // 2
            cps.append(_remote(_rows(bufs[i], c, hr), _rows(bufs[i], c, hr), send_sems, recv_sems, i, (x, y, 1 - c)))
            cps[-1].start()
        for i, cp in enumerate(cps):
            hr = fs[i].shape[0] // 2
            _remote(_rows(bufs[i], c, hr), _rows(bufs[i], 1 - c, hr), send_sems, recv_sems, i, (x, y, c)).wait_recv()
            cp.wait_send()

    return pl.pallas_call(
        body, name=name, in_specs=[_ANY] * n, out_specs=[_ANY] * n,
        out_shape=[jax.ShapeDtypeStruct(f.shape, f.dtype) for f in fs],
        input_output_aliases={i: i for i in range(n)}, scratch_shapes=[_DMA((n,)), _DMA((n,))],
    )(*fs)


def gather_all(v, *, name):
    def body(v_ref, out_ref, send_sems, recv_sems, local_sem):
        x, y, c = _place()
        flip = lambda p, m: 1 - p if m else p
        peers = [(flip(x, m & 4), flip(y, m & 2), flip(c, m & 1)) for m in range(1, 8)]
        idx = lambda p: 4 * p[0] + 2 * p[1] + p[2]
        mine = pltpu.make_async_copy(v_ref, out_ref.at[idx((x, y, c))], local_sem)
        mine.start()
        cps = [_remote(v_ref, out_ref.at[idx((x, y, c))], send_sems, recv_sems, k, p) for k, p in enumerate(peers)]
        for cp in cps:
            cp.start()
        for k, p in enumerate(peers):
            _remote(v_ref, out_ref.at[idx(p)], send_sems, recv_sems, k, p).wait_recv()
        for cp in cps:
            cp.wait_send()
        mine.wait()

    return pl.pallas_call(
        body, name=name, in_specs=[_ANY], out_specs=_ANY, out_shape=jax.ShapeDtypeStruct((8,) + v.shape, v.dtype),
        scratch_shapes=[_DMA((7,)), _DMA((7,)), _DMA(())],
    )(v)


def sum_stack(a, extra=None, *, name):
    n, R, L = a.shape
    br = _pick(R, _ROWS_ALIGN, 8)

    def body(*refs):
        a_ref, o_ref = refs[0], refs[-1]
        acc = refs[1][...] if extra is not None else a_ref[0]
        for i in range(0 if extra is not None else 1, n):
            acc = acc + a_ref[i]
        o_ref[...] = acc

    row = pl.BlockSpec((br, L), lambda i: (i, 0))
    specs = [pl.BlockSpec((n, br, L), lambda i: (0, i, 0))] + ([row] if extra is not None else [])
    args = [a] + ([extra] if extra is not None else [])
    return pl.pallas_call(body, name=name, grid=(R // br,), in_specs=specs, out_specs=row,
                          out_shape=jax.ShapeDtypeStruct((R, L), a.dtype), compiler_params=_cparams(1))(*args)


def adamw(w, g, m, v, *, name):
    shape = w.shape
    size = math.prod(shape)
    last = shape[-1]
    if last % LANE != 0 and size % LANE == 0 and size <= (1 << 20):
        last = LANE
    rows = size // last
    budget = (1 << 18) // last
    br = rows
    if rows > budget:
        br = max(c for c in range(8, budget + 1, 8) if rows % c == 0)
    v2 = lambda a: a.reshape(rows, last)

    def body(w_ref, g_ref, m_ref, v_ref, d_ref, nm_ref, nv_ref):
        gg = g_ref[...]
        nm = ADAM_B1 * m_ref[...] + (1.0 - ADAM_B1) * gg
        nv = ADAM_B2 * v_ref[...] + (1.0 - ADAM_B2) * (gg * gg)
        m_hat = nm / (1.0 - ADAM_B1 ** ADAM_STEP)
        v_hat = nv / (1.0 - ADAM_B2 ** ADAM_STEP)
        d_ref[...] = -ADAM_LR * (m_hat / (jnp.sqrt(v_hat) + ADAM_EPS) + ADAM_WD * w_ref[...])
        nm_ref[...] = nm
        nv_ref[...] = nv

    spec = pl.BlockSpec((br, last), lambda i: (i, 0))
    outs = pl.pallas_call(
        body, name=name, grid=(rows // br,), in_specs=[spec] * 4, out_specs=[spec] * 3,
        out_shape=[jax.ShapeDtypeStruct((rows, last), F32)] * 3, compiler_params=_cparams(1),
    )(v2(w), v2(g), v2(m), v2(v))
    return [o.reshape(shape) for o in outs]


_WEIGHTS = ["norm_mix_g", "norm_ffn_g", "gla_w_in", "gla_w_a2", "gla_b_a", "gla_norm_g", "gla_w_out", "ssd_w_in",
            "ssd_conv_w", "ssd_conv_b", "ssd_dt_bias", "ssd_a_log", "ssd_d", "ssd_norm_g", "ssd_w_out", "s5_log_dt",
            "s5_a_re", "s5_a_im", "s5_b_re", "s5_b_im", "s5_c_re", "s5_c_im", "s5_d", "s5_w_glu", "ffn_w_gu",
            "ffn_w_down", "final_norm_g"]
_SHARD_AXIS = {"gla_w_in": 2, "gla_w_a2": 2, "gla_b_a": 1, "gla_norm_g": 1, "gla_w_out": 1, "ssd_w_in": 2,
               "ssd_conv_w": 2, "ssd_w_out": 1, "s5_d": 1, "s5_w_glu": 2, "ffn_w_gu": 2, "ffn_w_down": 1}
_SMALL_SHARDED = [n for n in _WEIGHTS if n in _SHARD_AXIS and n not in _BIG]
_REPLICATED = [n for n in _WEIGHTS if n not in _SHARD_AXIS]
_BIG_KIND = {"gla_w_in": ("lead", False), "gla_w_out": ("lead", False), "ssd_w_in": ("lead", False),
             "ssd_w_out": ("lead", False), "s5_w_glu": ("cols", False), "ffn_w_gu": ("cols", True),
             "ffn_w_down": ("lead", False)}
_PADDED_IN = {"gla_w_in": GLA_INP, "ssd_w_in": SSD_INP}


def _to_rows(flat, parts=1):
    per = -(-flat.shape[0] // (parts * LANE * _ROWS_ALIGN)) * _ROWS_ALIGN
    flat = jnp.pad(flat, (0, parts * per * LANE - flat.shape[0]))
    return flat.reshape(parts, per, LANE)


def _big_layers(local):
    return [(n, j, local[n][j].reshape(-1, local[n].shape[-1])) for n in _BIG for j in range(local[n].shape[0])]


def _in_layer0(n, j):
    return j == 0 and n in ("gla_w_in", "gla_w_out", "ffn_w_gu", "ffn_w_down")


def _assemble(n, g):
    if n in _PADDED_IN:
        return jnp.concatenate([g[s] for s in range(4)] + [jnp.zeros((g.shape[1], _PADDED_IN[n] - 4 * g.shape[2]), BF16)], axis=1)
    if _BIG_KIND[n][0] == "lead":
        return g.reshape(4 * g.shape[1], g.shape[2])
    return g


def _gather_first(local):
    layers = _big_layers(local)
    first = [l for l in layers if _in_layer0(l[0], l[1])]
    later = [l for l in layers if not _in_layer0(l[0], l[1])]
    full = {n: [None] * local[n].shape[0] for n in _BIG}
    got, done = gather_big([w.astype(BF16) for _, _, w in first], [_BIG_KIND[n] for n, _, _ in first], name="gather_weights_first")
    for (n, j, _), g in zip(first, got):
        full[n][j] = _assemble(n, g)
    flat = jnp.concatenate([local[n].astype(F32).reshape(-1) for n in _SMALL_SHARDED])
    got = gather_shards(_to_rows(flat, 2), name="gather_small_weights").reshape(4, -1)
    off = 0
    for n in _SMALL_SHARDED:
        bs = local[n].shape
        sz = math.prod(bs)
        seg = got[:, off:off + sz].reshape((4,) + bs)
        off += sz
        ax = _SHARD_AXIS[n]
        full[n] = jnp.moveaxis(seg, 0, ax).reshape(bs[:ax] + (4 * bs[ax],) + bs[ax + 1:])
    kinds = [_BIG_KIND[n] for n, _, _ in later]
    ops = [(w + done if k == 0 else w).astype(BF16) for k, (_, _, w) in enumerate(later)]
    send_sems, recv_sems, locs, lands, zero = gather_start(ops, kinds, name="gather_weights_start")
    return full, (later, kinds, send_sems, recv_sems, locs, lands), zero


def _gather_rest(full, pending, after):
    later, kinds, send_sems, recv_sems, locs, lands = pending
    lands = gather_wait(send_sems, recv_sems, locs, lands, kinds, after, name="gather_weights_wait")
    lands = gather_finish(lands, kinds, [w.shape for _, _, w in later], name="gather_weights_finish")
    out = {n: list(full[n]) for n in _BIG}
    for (n, j, _), g in zip(later, lands):
        out[n][j] = _assemble(n, g)
    return out


def _reduce_ops(grads, local, want):
    ops = []
    for n in _BIG:
        kind = _BIG_KIND[n]
        for j, g in enumerate(grads[n]):
            if not want(n, j):
                continue
            loc = local[n].shape[-1] if kind[0] == "cols" or n in _PADDED_IN else g.shape[1]
            if n in _PADDED_IN:
                g = jnp.stack([g[:, s * loc:(s + 1) * loc] for s in range(4)])
            elif kind[0] == "lead":
                g = g.reshape(4, g.shape[0] // 4, g.shape[1])
            ops.append((n, j, kind, loc, g))
    return ops


def _pair_sums(ops, c_arr, tag):
    gots = pair_swap([o[4] for o in ops], [o[2] for o in ops], name=f"reduce_pair_swap_{tag}")
    return [pair_add(o[4], got, c_arr, o[2][0], name=f"reduce_pair_add_{o[0]}{o[1]}") for o, got in zip(ops, gots)]


def _reduce_later_start(grads, local, c):
    ops = _reduce_ops(grads, local, lambda n, j: not _in_layer0(n, j))
    c_arr = jnp.reshape(c, (1,)).astype(jnp.int32)
    qs = _pair_sums(ops, c_arr, "later")
    send_sems, recv_sems, qs, lands, zero = scatter_start(qs, [o[2] for o in ops], [o[3] for o in ops], name="reduce_scatter_start")
    return (ops, send_sems, recv_sems, qs, lands), zero


def _reduce_big(grads, local, pending, after, x, y, c):
    c_arr = jnp.reshape(c, (1,)).astype(jnp.int32)
    ops_l, send_sems, recv_sems, qs_l, lands = pending
    qs_l, rs_l = scatter_wait(send_sems, recv_sems, qs_l, lands, [o[2] for o in ops_l], [o[3] for o in ops_l], after,
                              name="reduce_scatter_wait")
    ops_f = _reduce_ops(grads, local, _in_layer0)
    qs_f = _pair_sums(ops_f, c_arr, "first")
    rs_f = list(chip_scatter(qs_f, [o[2] for o in ops_f], [o[3] for o in ops_f], name="reduce_chip_scatter_first"))
    ops = ops_l + ops_f
    fs = [chip_add(q, r, jnp.reshape(_pos(x, y, o[2][1]), (1,)).astype(jnp.int32), c_arr, o[2][0], o[3],
                   name=f"reduce_chip_add_{o[0]}{o[1]}") for o, q, r in zip(ops, qs_l + qs_f, rs_l + rs_f)]
    outs = share_rows(fs, name="reduce_share")
    red = {(o[0], o[1]): r for o, r in zip(ops, outs)}
    return {n: jnp.stack([red[(n, j)] for j in range(local[n].shape[0])]).reshape(local[n].shape) for n in _BIG}


def _reduce_small(grads, local, x, y):
    names = _REPLICATED + _SMALL_SHARDED
    flat = jnp.concatenate([grads[n].astype(F32).reshape(-1) for n in names])
    n_el = flat.shape[0]
    rows = -(-n_el // (LANE * 8)) * 8
    v = jnp.pad(flat, (0, rows * LANE - n_el)).reshape(rows, LANE)
    red = sum_stack(gather_all(v, name="reduce_small_gather"), name="reduce_small_add").reshape(-1)
    out, off = {}, 0
    for n in names:
        sz = math.prod(grads[n].shape)
        g = red[off:off + sz].reshape(grads[n].shape)
        off += sz
        if n in _SHARD_AXIS:
            ax = _SHARD_AXIS[n]
            loc = local[n].shape[ax]
            g = lax.dynamic_slice_in_dim(g, (2 * x + y) * loc, loc, axis=ax)
        out[n] = g
    return out


def kernel(x, norm_mix_g, norm_ffn_g, gla_w_in, gla_w_a2, gla_b_a, gla_norm_g, gla_w_out, ssd_w_in, ssd_conv_w, ssd_conv_b, ssd_dt_bias, ssd_a_log, ssd_d, ssd_norm_g, ssd_w_out, s5_log_dt, s5_a_re, s5_a_im, s5_b_re, s5_b_im, s5_c_re, s5_c_im, s5_d, s5_w_glu, ffn_w_gu, ffn_w_down, final_norm_g, loss_target, m_norm_mix_g, m_norm_ffn_g, m_gla_w_in, m_gla_w_a2, m_gla_b_a, m_gla_norm_g, m_gla_w_out, m_ssd_w_in, m_ssd_conv_w, m_ssd_conv_b, m_ssd_dt_bias, m_ssd_a_log, m_ssd_d, m_ssd_norm_g, m_ssd_w_out, m_s5_log_dt, m_s5_a_re, m_s5_a_im, m_s5_b_re, m_s5_b_im, m_s5_c_re, m_s5_c_im, m_s5_d, m_s5_w_glu, m_ffn_w_gu, m_ffn_w_down, m_final_norm_g, v_norm_mix_g, v_norm_ffn_g, v_gla_w_in, v_gla_w_a2, v_gla_b_a, v_gla_norm_g, v_gla_w_out, v_ssd_w_in, v_ssd_conv_w, v_ssd_conv_b, v_ssd_dt_bias, v_ssd_a_log, v_ssd_d, v_ssd_norm_g, v_ssd_w_out, v_s5_log_dt, v_s5_a_re, v_s5_a_im, v_s5_b_re, v_s5_b_im, v_s5_c_re, v_s5_c_im, v_s5_d, v_s5_w_glu, v_ffn_w_gu, v_ffn_w_down, v_final_norm_g):
    given = dict(locals())
    local = {n: given[n] for n in _WEIGHTS}
    px, py, pc = _place()

    first, gathering, zero = _gather_first(local)
    full = dict(local)
    full.update(first)
    full["norm_mix_g"] = local["norm_mix_g"] + zero
    reducing = []

    def later_grads(g):
        pending, zero = _reduce_later_start(g, local, pc)
        reducing.append(pending)
        return zero

    loss, grad_x, grads = local_step(x[0], loss_target[0], full, lambda h: _gather_rest(first, gathering, h), later_grads)
    loss = lax.psum(loss, ("x", "y", "c"))

    red = _reduce_big(grads, local, reducing[0], grad_x, px, py, pc)
    red.update(_reduce_small(grads, local, px, py))

    deltas, new_m, new_v = {}, {}, {}
    for n in _WEIGHTS:
        deltas[n], new_m[n], new_v[n] = adamw(local[n], red[n], given["m_" + n], given["v_" + n], name=f"adamw_{n}")
    return (loss, grad_x[None], *[red[n] for n in _WEIGHTS], *[deltas[n] for n in _WEIGHTS],
            *[new_m[n] for n in _WEIGHTS], *[new_v[n] for n in _WEIGHTS])
```

```python
import functools
import math

import jax
import jax.numpy as jnp
from jax import lax
from jax.experimental import pallas as pl
from jax.experimental.pallas import tpu as pltpu

F32 = jnp.float32
BF16 = jnp.bfloat16

D = 1024
DEPTH = 4
CH = 64
EPS = 1e-6
GLA_H, GLA_DK, GLA_DV, GLA_RANK, GLA_TAU = 4, 128, 256, 16, 16.0
GLA_QK = GLA_H * GLA_DK
GLA_VD = GLA_H * GLA_DV
GLA_IN = 2 * GLA_QK + 2 * GLA_VD + GLA_RANK
GLA_INP = 3200
SSD_DI, SSD_HD, SSD_H, SSD_G, SSD_N, SSD_K = 2048, 64, 32, 8, 128, 4
SSD_GN = SSD_G * SSD_N
SSD_CONV = SSD_DI + 2 * SSD_GN
SSD_IN = SSD_DI + SSD_CONV + SSD_H
SSD_INP = 6272
S5_GS, S5_NG, S5_P = 16, 64, 64
S5_BLK = 8
FFN_H = 2816
LANE = 128
VMEM_LIMIT = 52 * 1024 * 1024
_MATMUL_VMEM = 40 * 1024 * 1024

ADAM_LR, ADAM_B1, ADAM_B2, ADAM_EPS, ADAM_WD, ADAM_STEP = 0.001, 0.9, 0.999, 1e-08, 0.01, 10

_ARB = "arbitrary"


def _cparams(n):
    return pltpu.CompilerParams(dimension_semantics=(_ARB,) * n, vmem_limit_bytes=VMEM_LIMIT)


def _pick(n, target, mult=LANE):
    best = None
    for c in range(mult, min(n, target) + 1, mult):
        if n % c == 0:
            best = c
    return best if best is not None else n


_DN = {"nn": (((1,), (0,)), ((), ())), "nt": (((1,), (1,)), ((), ())), "tn": (((0,), (0,)), ((), ()))}


def _dot_raw(a, b, form):
    return lax.dot_general(a.astype(BF16), b.astype(BF16), _DN[form], preferred_element_type=F32)


@functools.partial(jax.custom_vjp, nondiff_argnums=(2,))
def bdot(a, b, form):
    return _dot_raw(a, b, form)


def _bdot_fwd(a, b, form):
    return _dot_raw(a, b, form), (a, b)


def _bdot_bwd(form, res, g):
    a, b = res
    if form == "nn":
        return _dot_raw(g, b, "nt"), _dot_raw(a, g, "tn")
    if form == "nt":
        return _dot_raw(g, b, "nn"), _dot_raw(g, a, "tn")
    return _dot_raw(b, g, "nt"), _dot_raw(a, g, "nn")


bdot.defvjp(_bdot_fwd, _bdot_bwd)


def _hdot(a, b):
    return jnp.dot(a, b, precision=lax.Precision.HIGHEST, preferred_element_type=F32)


@jax.custom_vjp
def cdot_left(c, ct, x):
    return _hdot(c, x)


def _cdl_fwd(c, ct, x):
    return _hdot(c, x), (c, ct)


def _cdl_bwd(res, g):
    c, ct = res
    return jnp.zeros_like(c), jnp.zeros_like(ct), _hdot(ct, g)


cdot_left.defvjp(_cdl_fwd, _cdl_bwd)


@jax.custom_vjp
def cdot_right(x, c, ct):
    return _hdot(x, c)


def _cdr_fwd(x, c, ct):
    return _hdot(x, c), (c, ct)


def _cdr_bwd(res, g):
    c, ct = res
    return _hdot(g, ct), jnp.zeros_like(c), jnp.zeros_like(ct)


cdot_right.defvjp(_cdr_fwd, _cdr_bwd)


def _sigmoid(x):
    return 1.0 / (1.0 + jnp.exp(-x))


def _silu(x):
    return x * _sigmoid(x)


def _softplus(x):
    return jnp.maximum(x, 0.0) + jnp.log(1.0 + jnp.exp(-jnp.abs(x)))


def _log_sigmoid(x):
    return jnp.minimum(x, 0.0) - jnp.log(1.0 + jnp.exp(-jnp.abs(x)))


def _gelu(x):
    c = math.sqrt(2.0 / math.pi)
    return 0.5 * x * (1.0 + jnp.tanh(c * (x + 0.044715 * (x * x * x))))


def _rms(x, g):
    return x * lax.rsqrt(jnp.mean(x * x, axis=-1, keepdims=True) + EPS) * g


def _iota2(shape, axis):
    return lax.broadcasted_iota(jnp.int32, shape, axis)


def matmul(a, b, form, *, name, G=1, out_dtype=F32, add=None):
    isz = lambda t: jnp.dtype(t.dtype).itemsize
    osz = jnp.dtype(out_dtype).itemsize + (isz(add) if add is not None else 0)

    def fits(bm, bn, bk):
        return 2 * (bm * bk * isz(a) + bk * bn * isz(b) + bm * bn * osz) + 4 * bm * bn <= _MATMUL_VMEM

    if form in ("nn", "nt"):
        M = a.shape[0]
        K = a.shape[1] // G
        N = b.shape[2] if form == "nn" else b.shape[1]
        bm, bn, bk = min(M, 1024), _pick(N, 1536), _pick(K, 2048)
        while not fits(bm, bn, bk) and bk % 256 == 0:
            bk //= 2
        nj, nk = N // bn, K // bk
        grid = (G, M // bm, nj, nk)
        a_spec = pl.BlockSpec((bm, bk), lambda g, i, j, k: (i, g * nk + k))
        if form == "nn":
            b_spec = pl.BlockSpec((None, bk, bn), lambda g, i, j, k: (g, k, j))
        else:
            b_spec = pl.BlockSpec((None, bn, bk), lambda g, i, j, k: (g, j, k))
        o_spec = pl.BlockSpec((bm, bn), lambda g, i, j, k: (i, g * nj + j))
        out_shape = jax.ShapeDtypeStruct((M, G * N), out_dtype)
    else:
        T = a.shape[0]
        Ka, Nb = a.shape[1] // G, b.shape[1] // G
        bm, bn, bk = _pick(Ka, 1408), _pick(Nb, 1536), min(T, 2048)
        while not fits(bm, bn, bk) and bk % 512 == 0:
            bk //= 2
        ni, nj, nk = Ka // bm, Nb // bn, T // bk
        grid = (G, ni, nj, nk)
        a_spec = pl.BlockSpec((bk, bm), lambda g, i, j, k: (k, g * ni + i))
        b_spec = pl.BlockSpec((bk, bn), lambda g, i, j, k: (k, g * nj + j))
        o_spec = pl.BlockSpec((None, bm, bn), lambda g, i, j, k: (g, i, j))
        out_shape = jax.ShapeDtypeStruct((G, Ka, Nb), out_dtype)
    has_add = add is not None

    def finish(refs, r):
        if has_add:
            r = r + refs[2][...].astype(F32)
        o_ref = refs[3] if has_add else refs[2]
        o_ref[...] = r.astype(o_ref.dtype)

    def body_one(*refs):
        finish(refs, _dot_raw(refs[0][...], refs[1][...], form))

    def body_acc(*refs):
        acc_ref = refs[-1]
        k = pl.program_id(3)

        @pl.when(k == 0)
        def _():
            acc_ref[...] = jnp.zeros_like(acc_ref)

        acc_ref[...] += _dot_raw(refs[0][...], refs[1][...], form)

        @pl.when(k == nk - 1)
        def _():
            finish(refs, acc_ref[...])

    in_specs = [a_spec, b_spec]
    args = [a, b]
    if has_add:
        in_specs.append(o_spec)
        args.append(add)
    return pl.pallas_call(
        body_one if nk == 1 else body_acc, name=name, grid=grid, in_specs=in_specs, out_specs=o_spec,
        out_shape=out_shape, scratch_shapes=[] if nk == 1 else [pltpu.VMEM((bm, bn), F32)],
        compiler_params=_cparams(4),
    )(*args)


def matmul_nt_norm_bwd(a, w, h, g, d, *, name):
    T, K = a.shape
    bm = min(T, 512)
    bk = _pick(K, 2048)
    nk = K // bk

    def body(a_ref, w_ref, h_ref, g_ref, d_ref, dh_ref, dg_ref, acc_ref):
        i, k = pl.program_id(0), pl.program_id(1)

        @pl.when((i == 0) & (k == 0))
        def _():
            dg_ref[...] = jnp.zeros_like(dg_ref)

        @pl.when(k == 0)
        def _():
            acc_ref[...] = jnp.zeros_like(acc_ref)

        acc_ref[...] += _dot_raw(a_ref[...], w_ref[...], "nt")

        @pl.when(k == nk - 1)
        def _():
            _, vjp = jax.vjp(lambda g_, h_: _rms(h_, g_), g_ref[...], h_ref[...])
            dg, dh = vjp(acc_ref[...])
            dh_ref[...] = dh + d_ref[...]
            dg_ref[...] += dg

    row = pl.BlockSpec((bm, D), lambda i, k: (i, 0))
    one = pl.BlockSpec((1, D), lambda i, k: (0, 0))
    return pl.pallas_call(
        body, name=name, grid=(T // bm, nk),
        in_specs=[pl.BlockSpec((bm, bk), lambda i, k: (i, k)), pl.BlockSpec((D, bk), lambda i, k: (0, k)), row, one, row],
        out_specs=[row, one], out_shape=[jax.ShapeDtypeStruct((T, D), F32), jax.ShapeDtypeStruct((1, D), F32)],
        scratch_shapes=[pltpu.VMEM((bm, D), F32)], compiler_params=_cparams(2),
    )(a, w, h, g, d)


def ffn_up(hn, w_il, *, name):
    T = hn.shape[0]
    bm, hb = min(T, 512), FFN_H // 2

    def body(a_ref, b_ref, act_ref, gu_ref):
        r = _dot_raw(a_ref[...], b_ref[...], "nn")
        act_ref[...] = (_silu(r[:, :hb]) * r[:, hb:]).astype(act_ref.dtype)
        gu_ref[...] = r.astype(gu_ref.dtype)

    return pl.pallas_call(
        body, name=name, grid=(2, T // bm),
        in_specs=[pl.BlockSpec((bm, D), lambda j, i: (i, 0)), pl.BlockSpec((D, 2 * hb), lambda j, i: (0, j))],
        out_specs=[pl.BlockSpec((bm, hb), lambda j, i: (i, j)), pl.BlockSpec((bm, 2 * hb), lambda j, i: (i, j))],
        out_shape=[jax.ShapeDtypeStruct((T, FFN_H), BF16), jax.ShapeDtypeStruct((T, 2 * FFN_H), BF16)],
        compiler_params=_cparams(2),
    )(hn, w_il)


_DACT_CHUNK = 512


def ffn_dact(d, w_down, gu, *, name):
    T = d.shape[0]
    bm, hb = min(T, 512), FFN_H // 2

    def body(d_ref, w_ref, gu_ref, o_ref):
        d_blk = d_ref[...].astype(BF16)
        for lo in range(0, hb, _DACT_CHUNK):
            hi = min(lo + _DACT_CHUNK, hb)
            da = _dot_raw(d_blk, w_ref[lo:hi, :], "nt")
            g, u = gu_ref[:, lo:hi].astype(F32), gu_ref[:, hb + lo:hb + hi].astype(F32)
            sg = _sigmoid(g)
            o_ref[:, lo:hi] = (da * u * (sg * (1.0 + g * (1.0 - sg)))).astype(o_ref.dtype)
            o_ref[:, hb + lo:hb + hi] = (da * (g * sg)).astype(o_ref.dtype)

    return pl.pallas_call(
        body, name=name, grid=(2, T // bm),
        in_specs=[pl.BlockSpec((bm, D), lambda j, i: (i, 0)), pl.BlockSpec((hb, D), lambda j, i: (j, 0)),
                  pl.BlockSpec((bm, 2 * hb), lambda j, i: (i, j))],
        out_specs=pl.BlockSpec((bm, 2 * hb), lambda j, i: (i, j)),
        out_shape=jax.ShapeDtypeStruct((T, 2 * FFN_H), BF16), compiler_params=_cparams(2),
    )(d, w_down, gu)


def _row_entry(e):
    return e if isinstance(e, tuple) else (e, e.shape[1], 0)


def _row_spec(bt, e):
    _, width, idx = e
    return pl.BlockSpec((bt, width), lambda i: (i, idx))


def _full_spec(p):
    return pl.BlockSpec(p.shape, lambda i: (0,) * p.ndim)


def rowwise(f, params, rows, outs, *, bt, name):
    rows = [_row_entry(e) for e in rows]
    T = rows[0][0].shape[0]
    bt = min(bt, T)
    np_, nr = len(params), len(rows)

    def body(*refs):
        p = tuple(r[...].astype(F32) for r in refs[:np_])
        rw = tuple(r[...].astype(F32) for r in refs[np_:np_ + nr])
        res = f(p, rw)
        for o_ref, o in zip(refs[np_ + nr:], res):
            o_ref[...] = o.astype(o_ref.dtype)

    res = pl.pallas_call(
        body, name=name, grid=(T // bt,),
        in_specs=[_full_spec(p) for p in params] + [_row_spec(bt, e) for e in rows],
        out_specs=[pl.BlockSpec((bt, w), lambda i: (i, 0)) for w, _ in outs],
        out_shape=[jax.ShapeDtypeStruct((T, w), dt) for w, dt in outs],
        compiler_params=_cparams(1),
    )(*params, *[e[0] for e in rows])
    return list(res)


def rowwise_vjp(f, params, rows, cts, drow_dtypes, *, bt, name):
    rows = [_row_entry(e) for e in rows]
    cts = [_row_entry(e) for e in cts]
    T = rows[0][0].shape[0]
    bt = min(bt, T)
    np_, nr, nc = len(params), len(rows), len(cts)
    want = [i for i, dt in enumerate(drow_dtypes) if dt is not None]

    def body(*refs):
        p = tuple(r[...].astype(F32) for r in refs[:np_])
        rw = tuple(r[...].astype(F32) for r in refs[np_:np_ + nr])
        ct = tuple(r[...].astype(F32) for r in refs[np_ + nr:np_ + nr + nc])
        outs = refs[np_ + nr + nc:]
        _, vjp = jax.vjp(f, p, rw)
        dp, dr = vjp(ct)

        @pl.when(pl.program_id(0) == 0)
        def _():
            for o in outs[:np_]:
                o[...] = jnp.zeros_like(o)

        for o, d in zip(outs[:np_], dp):
            o[...] += d
        for o, i in zip(outs[np_:], want):
            o[...] = dr[i].astype(o.dtype)

    res = pl.pallas_call(
        body, name=name, grid=(T // bt,),
        in_specs=[_full_spec(p) for p in params] + [_row_spec(bt, e) for e in rows] + [_row_spec(bt, e) for e in cts],
        out_specs=[_full_spec(p) for p in params] + [pl.BlockSpec((bt, rows[i][1]), lambda i_: (i_, 0)) for i in want],
        out_shape=[jax.ShapeDtypeStruct(p.shape, F32) for p in params]
        + [jax.ShapeDtypeStruct((T, rows[i][1]), drow_dtypes[i]) for i in want],
        compiler_params=_cparams(1),
    )(*params, *[e[0] for e in rows], *[e[0] for e in cts])
    res = list(res)
    return res[:np_], res[np_:]


def f_rmsnorm(p, r):
    return (_rms(r[0], p[0]),)


def f_rmsnorm_res(p, r):
    return (_rms(r[0], p[0]), r[0])


def f_swiglu(p, r):
    gu = r[0]
    return (_silu(gu[:, :FFN_H]) * gu[:, FFN_H:],)


def f_gla_gate_in(p, r):
    w_a2, b_a = p
    z = bdot(r[0], w_a2, "nn") + b_a
    return (_log_sigmoid(z) / GLA_TAU,)


def f_gla_gate_in_fwd(p, r):
    w_a2, b_a = p
    z = _dot_raw(r[0], w_a2, "nn") + b_a
    return (_log_sigmoid(z) / GLA_TAU,)


def f_gla_out(p, r):
    (ng,) = p
    o, rr = r
    parts = []
    for h in range(GLA_H):
        sl = slice(h * GLA_DV, (h + 1) * GLA_DV)
        parts.append(_rms(o[:, sl], ng[:, sl]) * _silu(rr[:, sl]))
    return (jnp.concatenate(parts, axis=1),)


def f_ssd_out(p, r):
    (ng,) = p
    y, z = r
    t = y * _silu(z)
    gsz = SSD_DI // SSD_G
    parts = []
    for g in range(SSD_G):
        sl = slice(g * gsz, (g + 1) * gsz)
        parts.append(_rms(t[:, sl], ng[:, sl]))
    return (jnp.concatenate(parts, axis=1),)


def f_s5_act(p, r):
    (dsk,) = p
    ycp, u = r
    return (_gelu(ycp + dsk * u),)


def f_glu_res(p, r):
    vg, h = r
    return (vg[:, :D] * _sigmoid(vg[:, D:]) + h,)


def f_glu(p, r):
    vg = r[0]
    return (vg[:, :D] * _sigmoid(vg[:, D:]),)


def loss_head(h, g, target, *, name):
    T = h.shape[0]
    bt = min(T, 256)

    def lossf(g_, h_, t_):
        e = _rms(h_, g_) - t_
        return (0.5 / D) * jnp.sum(e * e)

    def body(g_ref, h_ref, t_ref, loss_ref, dg_ref, dh_ref):
        @pl.when(pl.program_id(0) == 0)
        def _():
            loss_ref[...] = jnp.zeros_like(loss_ref)
            dg_ref[...] = jnp.zeros_like(dg_ref)

        val, vjp = jax.vjp(lossf, g_ref[...], h_ref[...], t_ref[...])
        dg, dh, _ = vjp(jnp.ones((), F32))
        loss_ref[...] += jnp.full(loss_ref.shape, val, F32)
        dg_ref[...] += dg
        dh_ref[...] = dh

    row = pl.BlockSpec((bt, D), lambda i: (i, 0))
    one = pl.BlockSpec((1, D), lambda i: (0, 0))
    loss, dg, dh = pl.pallas_call(
        body, name=name, grid=(T // bt,), in_specs=[one, row, row],
        out_specs=[pl.BlockSpec((1, LANE), lambda i: (0, 0)), one, row],
        out_shape=[jax.ShapeDtypeStruct((1, LANE), F32), jax.ShapeDtypeStruct((1, D), F32),
                   jax.ShapeDtypeStruct((T, D), F32)],
        compiler_params=_cparams(1),
    )(g, h, target)
    return loss[0, 0], dg, dh


def _gla_consts():
    r, c = _iota2((CH, CH), 0), _iota2((CH, CH), 1)
    return (r >= c).astype(F32), (r <= c).astype(F32), r >= c


def _gla_chunk(q, k, v, la, st, consts, dot, cdl):
    L, Lt, tril = consts
    lc = cdl(L, Lt, la)
    lend = lc[CH - 1:CH, :]
    e, ei = jnp.exp(lc), jnp.exp(-lc)
    qs = q * (GLA_DK ** -0.5)
    qf, kf, qb, kb = qs * e, k * ei, qs * ei, k * e
    sc = jnp.where(tril, dot(qf, kf, "nt"), dot(qb, kb, "nt"))
    o = dot(sc, v, "nn") + dot(qf, st, "nt")
    kd = k * jnp.exp(lend - lc)
    st_new = st * jnp.exp(lend) + dot(v, kd, "tn")
    return o, st_new


def _gla_block(q, k, v, la, st, nc, dot, cdl):
    consts = _gla_consts()
    outs = []
    for c in range(nc):
        sl = slice(c * CH, (c + 1) * CH)
        o, st = _gla_chunk(q[sl], k[sl], v[sl], la[sl], st, consts, dot, cdl)
        outs.append(o)
    return jnp.concatenate(outs, axis=0), st


_GLA_HP = 2


def _gla_specs(rows, rev, nb):
    t = (lambda j: nb - 1 - j) if rev else (lambda j: j)
    hp, ng = _GLA_HP, GLA_H // _GLA_HP
    q = pl.BlockSpec((rows, hp * GLA_DK), lambda h, j: (t(j), h))
    k = pl.BlockSpec((rows, hp * GLA_DK), lambda h, j: (t(j), ng + h))
    v = pl.BlockSpec((rows, hp * GLA_DV), lambda h, j: (t(j), ng + h))
    la = pl.BlockSpec((rows, hp * GLA_DK), lambda h, j: (t(j), h))
    ss = pl.BlockSpec((None, hp, GLA_DV, GLA_DK), lambda h, j: (t(j), h, 0, 0))
    o = pl.BlockSpec((rows, hp * GLA_DV), lambda h, j: (t(j), h))
    r = pl.BlockSpec((rows, hp * GLA_DV), lambda h, j: (t(j), 2 * ng + h))
    g = pl.BlockSpec((1, hp * GLA_DV), lambda h, j: (0, h))
    return q, k, v, la, ss, o, r, g


def _gla_heads(q, k, v, la, r, ng, sts, nc, dot, cdl):
    outs, new = [], []
    for i in range(_GLA_HP):
        kk, vv = slice(i * GLA_DK, (i + 1) * GLA_DK), slice(i * GLA_DV, (i + 1) * GLA_DV)
        o, st = _gla_block(q[:, kk], k[:, kk], v[:, vv], la[:, kk], sts[i], nc, dot, cdl)
        outs.append(_rms(o, ng[:, vv]) * _silu(r[:, vv]))
        new.append(st)
    return jnp.concatenate(outs, axis=1), tuple(new)


def gla_scan_fwd(proj, la, ng, *, nc, name):
    T = proj.shape[0]
    rows = min(T, nc * CH)
    nc = rows // CH
    nb = T // rows
    q_s, k_s, v_s, la_s, ss_s, o_s, r_s, g_s = _gla_specs(rows, False, nb)

    def body(q_ref, k_ref, v_ref, la_ref, r_ref, g_ref, o_ref, ss_ref, st_ref):
        @pl.when(pl.program_id(1) == 0)
        def _():
            st_ref[...] = jnp.zeros_like(st_ref)

        ss_ref[...] = st_ref[...]
        sts = tuple(st_ref[i] for i in range(_GLA_HP))
        o, sts = _gla_heads(q_ref[...], k_ref[...], v_ref[...], la_ref[...], r_ref[...], g_ref[...], sts, nc,
                            _dot_raw, lambda c, ct, x: _hdot(c, x))
        o_ref[...] = o.astype(o_ref.dtype)
        for i in range(_GLA_HP):
            st_ref[i] = sts[i]

    return pl.pallas_call(
        body, name=name, grid=(GLA_H // _GLA_HP, nb), in_specs=[q_s, k_s, v_s, la_s, r_s, g_s], out_specs=[o_s, ss_s],
        out_shape=[jax.ShapeDtypeStruct((T, GLA_VD), BF16), jax.ShapeDtypeStruct((nb, GLA_H, GLA_DV, GLA_DK), F32)],
        scratch_shapes=[pltpu.VMEM((_GLA_HP, GLA_DV, GLA_DK), F32)], compiler_params=_cparams(2),
    )(proj, proj, proj, la, proj, ng)


def gla_scan_bwd(proj, la, ng, ss, do, *, nc, name):
    T = proj.shape[0]
    rows = min(T, nc * CH)
    nc = rows // CH
    nb = T // rows
    q_s, k_s, v_s, la_s, ss_s, o_s, r_s, g_s = _gla_specs(rows, True, nb)
    t = lambda j: nb - 1 - j
    dqk_s = pl.BlockSpec((rows, _GLA_HP * GLA_DK), lambda h, j: (t(j), h))

    def body(q_ref, k_ref, v_ref, la_ref, r_ref, g_ref, ss_ref, do_ref,
             dq_ref, dk_ref, dv_ref, dla_ref, dr_ref, dg_ref, dst_ref):
        @pl.when(pl.program_id(1) == 0)
        def _():
            dst_ref[...] = jnp.zeros_like(dst_ref)
            dg_ref[...] = jnp.zeros_like(dg_ref)

        fn = lambda q, k, v, la_, r, g, *sts: _gla_heads(q, k, v, la_, r, g, sts, nc, bdot, cdot_left)
        _, vjp = jax.vjp(fn, q_ref[...], k_ref[...], v_ref[...], la_ref[...], r_ref[...], g_ref[...],
                         *[ss_ref[i] for i in range(_GLA_HP)])
        dq, dk, dv, dla, dr, dg, *dsts = vjp((do_ref[...], tuple(dst_ref[i] for i in range(_GLA_HP))))
        dq_ref[...] = dq.astype(dq_ref.dtype)
        dk_ref[...] = dk.astype(dk_ref.dtype)
        dv_ref[...] = dv.astype(dv_ref.dtype)
        dla_ref[...] = dla
        dr_ref[...] = dr.astype(dr_ref.dtype)
        dg_ref[...] += dg
        for i in range(_GLA_HP):
            dst_ref[i] = dsts[i]

    return pl.pallas_call(
        body, name=name, grid=(GLA_H // _GLA_HP, nb), in_specs=[q_s, k_s, v_s, la_s, r_s, g_s, ss_s, o_s],
        out_specs=[dqk_s, dqk_s, o_s, dqk_s, o_s, g_s],
        out_shape=[jax.ShapeDtypeStruct((T, GLA_QK), BF16), jax.ShapeDtypeStruct((T, GLA_QK), BF16),
                   jax.ShapeDtypeStruct((T, GLA_VD), BF16), jax.ShapeDtypeStruct((T, GLA_QK), F32),
                   jax.ShapeDtypeStruct((T, GLA_VD), BF16), jax.ShapeDtypeStruct((1, GLA_VD), F32)],
        scratch_shapes=[pltpu.VMEM((_GLA_HP, GLA_DV, GLA_DK), F32)], compiler_params=_cparams(2),
    )(proj, proj, proj, la, proj, ng, ss, do)


_CONV_W = 512
_CONV_OFF = SSD_DI // _CONV_W


def _conv_pre(x, prev8, w_ref, b_ref):
    bt = x.shape[0]
    ext = jnp.concatenate([prev8, x], axis=0)
    shifted = []
    for j in range(SSD_K):
        s = SSD_K - 1 - j
        shifted.append(x if s == 0 else pltpu.roll(ext, s, 0)[8:8 + bt])
    pre = b_ref[...] + sum(w_ref[j:j + 1, :] * shifted[j] for j in range(SSD_K))
    return pre, shifted


def ssd_conv_fwd(proj, w, b, *, name):
    T = proj.shape[0]
    bt = min(T, 512)
    nb = T // bt

    def body(x_ref, w_ref, b_ref, o_ref, carry_ref):
        @pl.when(pl.program_id(1) == 0)
        def _():
            carry_ref[...] = jnp.zeros_like(carry_ref)

        x = x_ref[...]
        pre, _ = _conv_pre(x, carry_ref[...], w_ref, b_ref)
        o_ref[...] = _silu(pre)
        carry_ref[...] = x[bt - 8:, :]

    return pl.pallas_call(
        body, name=name, grid=(SSD_CONV // _CONV_W, nb),
        in_specs=[pl.BlockSpec((bt, _CONV_W), lambda c, t: (t, _CONV_OFF + c)),
                  pl.BlockSpec((SSD_K, _CONV_W), lambda c, t: (0, c)),
                  pl.BlockSpec((1, _CONV_W), lambda c, t: (0, c))],
        out_specs=pl.BlockSpec((bt, _CONV_W), lambda c, t: (t, c)),
        out_shape=jax.ShapeDtypeStruct((T, SSD_CONV), F32),
        scratch_shapes=[pltpu.VMEM((8, _CONV_W), F32)], compiler_params=_cparams(2),
    )(proj, w, b)


def ssd_conv_bwd(proj, w, b, dout, *, name):
    T = proj.shape[0]
    bt = min(T, 512)
    nb = T // bt
    r8 = bt // 8

    def body(x_ref, xp_ref, w_ref, b_ref, do_ref, dx_ref, dw_ref, db_ref, carry_ref):
        t = pl.program_id(1)

        @pl.when(t == 0)
        def _():
            carry_ref[...] = jnp.zeros_like(carry_ref)
            dw_ref[...] = jnp.zeros_like(dw_ref)
            db_ref[...] = jnp.zeros_like(db_ref)

        x = x_ref[...]
        prev8 = jnp.where(t == nb - 1, 0.0, xp_ref[...])
        pre, shifted = _conv_pre(x, prev8, w_ref, b_ref)
        sg = _sigmoid(pre)
        dpre = do_ref[...] * (sg * (1.0 + pre * (1.0 - sg)))
        ext = jnp.concatenate([dpre, carry_ref[...]], axis=0)
        dx = w_ref[SSD_K - 1:SSD_K, :] * dpre
        for j in range(SSD_K - 1):
            s = SSD_K - 1 - j
            dx = dx + w_ref[j:j + 1, :] * pltpu.roll(ext, bt + 8 - s, 0)[:bt]
        dx_ref[...] = dx.astype(dx_ref.dtype)
        dw_ref[...] += jnp.concatenate([jnp.sum(dpre * shifted[j], axis=0, keepdims=True) for j in range(SSD_K)], axis=0)
        db_ref[...] += jnp.sum(dpre, axis=0, keepdims=True)
        carry_ref[...] = dpre[:8, :]

    rt = lambda t: nb - 1 - t
    return pl.pallas_call(
        body, name=name, grid=(SSD_CONV // _CONV_W, nb),
        in_specs=[pl.BlockSpec((bt, _CONV_W), lambda c, t: (rt(t), _CONV_OFF + c)),
                  pl.BlockSpec((8, _CONV_W), lambda c, t: (jnp.maximum(rt(t) * r8 - 1, 0), _CONV_OFF + c)),
                  pl.BlockSpec((SSD_K, _CONV_W), lambda c, t: (0, c)),
                  pl.BlockSpec((1, _CONV_W), lambda c, t: (0, c)),
                  pl.BlockSpec((bt, _CONV_W), lambda c, t: (rt(t), c))],
        out_specs=[pl.BlockSpec((bt, _CONV_W), lambda c, t: (rt(t), c)),
                   pl.BlockSpec((SSD_K, _CONV_W), lambda c, t: (0, c)),
                   pl.BlockSpec((1, _CONV_W), lambda c, t: (0, c))],
        out_shape=[jax.ShapeDtypeStruct((T, SSD_CONV), BF16), jax.ShapeDtypeStruct((SSD_K, SSD_CONV), F32),
                   jax.ShapeDtypeStruct((1, SSD_CONV), F32)],
        scratch_shapes=[pltpu.VMEM((8, _CONV_W), F32)], compiler_params=_cparams(2),
    )(proj, proj, w, b, dout)


_SSD_U = 2 * CH


def _ssd_unit(xs, bm, cm, dtraw, dtb, alog, dsk, hp, g, dot, cdl, cdr):
    U, P2 = _SSD_U, 2 * SSD_HD
    r, c = _iota2((U, U), 0), _iota2((U, U), 1)
    same = (r // CH) == (c // CH)
    Lb = (same & (r >= c)).astype(F32)
    Ub = (same & (r <= c)).astype(F32)
    lane = _iota2((1, U), 1)
    lo_lane = _iota2((1, P2), 1) < SSD_HD
    lo_sub = _iota2((P2, 1), 0) < SSD_HD
    diag2 = (_iota2((CH, P2), 0) == (_iota2((CH, P2), 1) % CH)).astype(F32)

    dt = _softplus(dtraw + dtb)
    da = dt * (-jnp.exp(alog))
    cum = cdl(Lb, Ub, da)
    ys = []
    new_hp = []
    for pr in range(2):
        xs_p = xs[:, pr * P2:(pr + 1) * P2]
        cols, dts, dks = [], [], []
        for jj in range(2):
            oh_l = (lane == g * (SSD_H // SSD_G) + 2 * pr + jj).astype(F32)
            cols.append(jnp.sum(cum * oh_l, axis=1, keepdims=True))
            dts.append(jnp.sum(dt * oh_l, axis=1, keepdims=True))
            dks.append(jnp.sum(dsk * oh_l, axis=1, keepdims=True))
        dsk_p = jnp.where(lo_lane, dks[0], dks[1])
        h = hp[pr]
        yc = []
        for ci in range(2):
            sl = slice(ci * CH, (ci + 1) * CH)
            xs_c, bm_c, cm_c = xs_p[sl], bm[sl], cm[sl]
            col = jnp.where(lo_lane, cols[0][sl], cols[1][sl])
            dtc = jnp.where(lo_lane, dts[0][sl], dts[1][sl])
            row = jnp.sum(diag2 * col, axis=0, keepdims=True)
            dtrow = jnp.sum(diag2 * dtc, axis=0, keepdims=True)
            cb = dot(cm_c, jnp.concatenate([bm_c, bm_c], axis=0), "nt")
            mix = cb * jnp.exp(-jnp.abs(col - row)) * dtrow
            xbd = jnp.concatenate([jnp.where(lo_lane, xs_c, 0.0), jnp.where(lo_lane, 0.0, xs_c)], axis=0)
            y_intra = dot(mix, xbd, "nn")
            ce = jnp.where(lo_lane, cols[0][ci * CH + CH - 1:ci * CH + CH, :], cols[1][ci * CH + CH - 1:ci * CH + CH, :])
            y_inter = dot(cm_c, h, "nt") * jnp.exp(col)
            xw = xs_c * (dtc * jnp.exp(ce - col))
            ce_s = [cols[jj][ci * CH + CH - 1:ci * CH + CH, :] for jj in range(2)]
            a_p = jnp.where(lo_sub, jnp.exp(ce_s[0]), jnp.exp(ce_s[1]))
            h = a_p * h + dot(xw, bm_c, "tn")
            yc.append(y_intra + y_inter + dsk_p * xs_c)
        ys.append(jnp.concatenate(yc, axis=0))
        new_hp.append(h)
    return jnp.concatenate(ys, axis=1), tuple(new_hp)


def _ssd_block(xs, bm, cm, dtraw, z, dtb, alog, dsk, ng, hp, g, nu, dot, cdl, cdr):
    outs = []
    for u in range(nu):
        sl = slice(u * _SSD_U, (u + 1) * _SSD_U)
        y, hp = _ssd_unit(xs[sl], bm[sl], cm[sl], dtraw[sl], dtb, alog, dsk, hp, g, dot, cdl, cdr)
        outs.append(y)
    return _rms(jnp.concatenate(outs, axis=0) * _silu(z), ng), hp


def _ssd_specs(rows, rev, nb):
    t = (lambda j: nb - 1 - j) if rev else (lambda j: j)
    gw = SSD_DI // SSD_G
    xs = pl.BlockSpec((rows, gw), lambda j, g: (t(j), g))
    bm = pl.BlockSpec((rows, SSD_N), lambda j, g: (t(j), SSD_DI // SSD_N + g))
    cm = pl.BlockSpec((rows, SSD_N), lambda j, g: (t(j), (SSD_DI + SSD_GN) // SSD_N + g))
    dtr = pl.BlockSpec((rows, LANE), lambda j, g: (t(j), (SSD_DI + SSD_CONV) // LANE))
    par = pl.BlockSpec((1, LANE), lambda j, g: (0, 0))
    hs = pl.BlockSpec((None, None, 2, 2 * SSD_HD, SSD_N), lambda j, g: (t(j), g, 0, 0, 0))
    y = pl.BlockSpec((rows, gw), lambda j, g: (t(j), g))
    ng = pl.BlockSpec((1, gw), lambda j, g: (0, g))
    return xs, bm, cm, dtr, par, hs, y, ng


def ssd_scan_fwd(xbc, proj, dtb, alog, dsk, ng, *, nu, name):
    T = xbc.shape[0]
    rows = min(T, nu * _SSD_U)
    nu = rows // _SSD_U
    nb = T // rows
    xs_s, bm_s, cm_s, dt_s, par_s, hs_s, y_s, ng_s = _ssd_specs(rows, False, nb)

    def body(xs_ref, bm_ref, cm_ref, dt_ref, z_ref, dtb_ref, al_ref, dsk_ref, ng_ref, y_ref, hs_ref, h_ref):
        g = pl.program_id(1)

        @pl.when(pl.program_id(0) == 0)
        def _():
            h_ref[g] = jnp.zeros(h_ref.shape[1:], F32)

        hs_ref[...] = h_ref[g]
        hp = (h_ref[g, 0], h_ref[g, 1])
        y, hp = _ssd_block(xs_ref[...], bm_ref[...], cm_ref[...], dt_ref[...], z_ref[...], dtb_ref[...], al_ref[...],
                           dsk_ref[...], ng_ref[...], hp, g, nu, _dot_raw, lambda c, ct, x: _hdot(c, x), lambda x, c, ct: _hdot(x, c))
        y_ref[...] = y.astype(y_ref.dtype)
        h_ref[g, 0] = hp[0]
        h_ref[g, 1] = hp[1]

    return pl.pallas_call(
        body, name=name, grid=(nb, SSD_G), in_specs=[xs_s, bm_s, cm_s, dt_s, y_s, par_s, par_s, par_s, ng_s],
        out_specs=[y_s, hs_s],
        out_shape=[jax.ShapeDtypeStruct((T, SSD_DI), BF16), jax.ShapeDtypeStruct((nb, SSD_G, 2, 2 * SSD_HD, SSD_N), F32)],
        scratch_shapes=[pltpu.VMEM((SSD_G, 2, 2 * SSD_HD, SSD_N), F32)], compiler_params=_cparams(2),
    )(xbc, xbc, xbc, proj, proj, dtb, alog, dsk, ng)


def ssd_scan_bwd(xbc, proj, dtb, alog, dsk, ng, hs, dy, *, nu, name):
    T = xbc.shape[0]
    rows = min(T, nu * _SSD_U)
    nu = rows // _SSD_U
    nb = T // rows
    xs_s, bm_s, cm_s, dt_s, par_s, hs_s, y_s, ng_s = _ssd_specs(rows, True, nb)
    gw = SSD_DI // SSD_G
    dng_s = pl.BlockSpec((1, SSD_DI), lambda j, g: (0, 0))
    t = lambda j: nb - 1 - j
    n_s = pl.BlockSpec((rows, SSD_N), lambda j, g: (t(j), g))
    ddt_s = pl.BlockSpec((rows, LANE), lambda j, g: (t(j), 0))

    def body(xs_ref, bm_ref, cm_ref, dt_ref, z_ref, dtb_ref, al_ref, dsk_ref, ng_ref, hs_ref, dy_ref,
             dxs_ref, dbm_ref, dcm_ref, ddt_ref, ddtb_ref, dal_ref, ddsk_ref, dz_ref, dng_ref, dh_ref):
        j, g = pl.program_id(0), pl.program_id(1)

        @pl.when(j == 0)
        def _():
            dh_ref[g] = jnp.zeros(dh_ref.shape[1:], F32)

        @pl.when((j == 0) & (g == 0))
        def _():
            ddtb_ref[...] = jnp.zeros_like(ddtb_ref)
            dal_ref[...] = jnp.zeros_like(dal_ref)
            ddsk_ref[...] = jnp.zeros_like(ddsk_ref)
            dng_ref[...] = jnp.zeros_like(dng_ref)

        @pl.when(g == 0)
        def _():
            ddt_ref[...] = jnp.zeros_like(ddt_ref)

        fn = lambda xs, bm, cm, dtr, z, dtb_, al, dsk_, ng_, h0, h1: _ssd_block(
            xs, bm, cm, dtr, z, dtb_, al, dsk_, ng_, (h0, h1), g, nu, bdot, cdot_left, cdot_right)
        _, vjp = jax.vjp(fn, xs_ref[...], bm_ref[...], cm_ref[...], dt_ref[...], z_ref[...], dtb_ref[...], al_ref[...],
                         dsk_ref[...], ng_ref[...], hs_ref[0], hs_ref[1])
        dxs, dbm, dcm, ddt, dz, ddtb, dal, ddsk, dng, dh0, dh1 = vjp((dy_ref[...], (dh_ref[g, 0], dh_ref[g, 1])))
        dz_ref[...] = dz.astype(dz_ref.dtype)
        lanes = pl.ds(pl.multiple_of(g * gw, gw), gw)
        dng_ref[:, lanes] = dng_ref[:, lanes] + dng
        dxs_ref[...] = dxs
        dbm_ref[...] = dbm
        dcm_ref[...] = dcm
        ddt_ref[...] += ddt
        ddtb_ref[...] += ddtb
        dal_ref[...] += dal
        ddsk_ref[...] += ddsk
        dh_ref[g, 0] = dh0
        dh_ref[g, 1] = dh1

    return pl.pallas_call(
        body, name=name, grid=(nb, SSD_G), in_specs=[xs_s, bm_s, cm_s, dt_s, y_s, par_s, par_s, par_s, ng_s, hs_s, y_s],
        out_specs=[y_s, n_s, n_s, ddt_s, par_s, par_s, par_s, y_s, dng_s],
        out_shape=[jax.ShapeDtypeStruct((T, SSD_DI), F32), jax.ShapeDtypeStruct((T, SSD_GN), F32),
                   jax.ShapeDtypeStruct((T, SSD_GN), F32), jax.ShapeDtypeStruct((T, LANE), F32),
                   jax.ShapeDtypeStruct((1, LANE), F32), jax.ShapeDtypeStruct((1, LANE), F32),
                   jax.ShapeDtypeStruct((1, LANE), F32), jax.ShapeDtypeStruct((T, SSD_DI), BF16),
                   jax.ShapeDtypeStruct((1, SSD_DI), F32)],
        scratch_shapes=[pltpu.VMEM((SSD_G, 2, 2 * SSD_HD, SSD_N), F32)], compiler_params=_cparams(2),
    )(xbc, xbc, xbc, proj, proj, dtb, alog, dsk, ng, hs, dy)


def _s5_param_f(log_dt, a_re, a_im, bre_t, bim_t, cim, cdl):
    n = S5_NG * S5_GS
    r, c = _iota2((n, S5_NG), 0), _iota2((n, S5_NG), 1)
    E = ((r // S5_GS) == c).astype(F32)
    rt, ct = _iota2((S5_NG, n), 0), _iota2((S5_NG, n), 1)
    Et = ((ct // S5_GS) == rt).astype(F32)
    step = jnp.exp(log_dt)
    mag = jnp.exp(step * a_re)
    abr = mag * jnp.cos(step * a_im)
    abi = mag * jnp.sin(step * a_im)
    den = a_re * a_re + a_im * a_im
    nr, ni = abr - 1.0, abi
    fr = (nr * a_re + ni * a_im) / den
    fi = (ni * a_re - nr * a_im) / den
    Fr, Fi = cdl(E, Et, fr), cdl(E, Et, fi)
    bbr = Fr * bre_t - Fi * bim_t
    bbi = Fr * bim_t + Fi * bre_t
    return abr, abi, bbr, bbi, -cim


def _whole(a):
    return pl.BlockSpec(a.shape, lambda: (0,) * a.ndim)


def s5_param_fwd(args, *, name):
    def body(*refs):
        res = _s5_param_f(*[r[...] for r in refs[:6]], lambda c, ct, x: _hdot(c, x))
        for o, v in zip(refs[6:], res):
            o[...] = v

    shapes = [(S5_NG, S5_P), (S5_NG, S5_P)] + [(S5_NG * S5_GS, S5_P)] * 3
    return pl.pallas_call(
        body, name=name, in_specs=[_whole(a) for a in args], out_specs=[pl.BlockSpec(s, lambda: (0, 0)) for s in shapes],
        out_shape=[jax.ShapeDtypeStruct(s, F32) for s in shapes],
        compiler_params=pltpu.CompilerParams(vmem_limit_bytes=VMEM_LIMIT),
    )(*args)


def s5_param_bwd(args, cts, *, name):
    def body(*refs):
        fn = lambda *a: _s5_param_f(*a, cdot_left)
        _, vjp = jax.vjp(fn, *[r[...] for r in refs[:6]])
        grads = vjp(tuple(r[...] for r in refs[6:11]))
        for o, v in zip(refs[11:], grads):
            o[...] = v

    return pl.pallas_call(
        body, name=name, in_specs=[_whole(a) for a in list(args) + list(cts)],
        out_specs=[_whole(a) for a in args], out_shape=[jax.ShapeDtypeStruct(a.shape, F32) for a in args],
        compiler_params=pltpu.CompilerParams(vmem_limit_bytes=VMEM_LIMIT),
    )(*args, *cts)


_S5_W = S5_BLK * S5_P


def _cmul_add(xr, xi, pr, pi, sr, si):
    return xr + (pr * sr - pi * si), xi + (pr * si + pi * sr)


def _s5_powers(ar, ai):
    pw = [(ar, ai)]
    for _ in range(7):
        qr, qi = pw[-1]
        pw.append((qr * ar - qi * ai, qr * ai + qi * ar))
    return pw


def s5_scan_fwd(u, wb, a_re, a_im, wc, *, name):
    T = u.shape[0]
    bt = min(T, 256)
    nb = T // bt

    def body(u_ref, wb_ref, ar_ref, ai_ref, wc_ref, x_ref, y_ref, bu_ref, carry_ref):
        @pl.when(pl.program_id(1) == 0)
        def _():
            carry_ref[...] = jnp.zeros_like(carry_ref)

        bu_ref[...] = _dot_raw(u_ref[...], wb_ref[...], "nn")
        ar, ai = ar_ref[...], ai_ref[...]
        pw = _s5_powers(ar, ai)
        pwr = jnp.concatenate([p[0] for p in pw], axis=0)
        pwi = jnp.concatenate([p[1] for p in pw], axis=0)
        rin = _iota2((8, _S5_W), 0)
        cr, ci = carry_ref[0:1, :], carry_ref[1:2, :]
        for t in range(bt // 8):
            sl = slice(8 * t, 8 * t + 8)
            xr, xi = bu_ref[sl, :_S5_W], bu_ref[sl, _S5_W:]
            for s in (1, 2, 4):
                m = rin >= s
                sr = jnp.where(m, pltpu.roll(xr, s, 0), 0.0)
                si = jnp.where(m, pltpu.roll(xi, s, 0), 0.0)
                xr, xi = _cmul_add(xr, xi, *pw[s - 1], sr, si)
            xr, xi = _cmul_add(xr, xi, pwr, pwi, cr, ci)
            x_ref[sl, :_S5_W] = xr
            x_ref[sl, _S5_W:] = xi
            cr, ci = xr[7:8, :], xi[7:8, :]
        carry_ref[0:1, :] = cr
        carry_ref[1:2, :] = ci
        y_ref[...] = _dot_raw(x_ref[...], wc_ref[...], "nn")

    nblk = S5_NG // S5_BLK
    blk = pl.BlockSpec((bt, 2 * _S5_W), lambda g, t: (t, g))
    col = pl.BlockSpec((bt, LANE), lambda g, t: (t, g))
    a_s = pl.BlockSpec((None, 1, _S5_W), lambda g, t: (g, 0, 0))
    wb_s = pl.BlockSpec((None, LANE, 2 * _S5_W), lambda g, t: (g, 0, 0))
    wc_s = pl.BlockSpec((None, 2 * _S5_W, LANE), lambda g, t: (g, 0, 0))
    return pl.pallas_call(
        body, name=name, grid=(nblk, nb), in_specs=[col, wb_s, a_s, a_s, wc_s], out_specs=[blk, col],
        out_shape=[jax.ShapeDtypeStruct((T, nblk * 2 * _S5_W), F32), jax.ShapeDtypeStruct((T, nblk * LANE), F32)],
        scratch_shapes=[pltpu.VMEM((bt, 2 * _S5_W), F32), pltpu.VMEM((8, _S5_W), F32)],
        compiler_params=_cparams(2),
    )(u, wb, a_re, a_im, wc)


def s5_scan_bwd(dy, x, u, wb, a_re, a_im, wc, *, name):
    T = dy.shape[0]
    bt = min(T, 256)
    nb = T // bt

    def body(dy_ref, x_ref, u_ref, wb_ref, ar_ref, ai_ref, wc_ref, du_ref, dwb_ref, dwc_ref, dar_ref, dai_ref,
             g_ref, lam_ref, carry_ref):
        @pl.when(pl.program_id(1) == 0)
        def _():
            carry_ref[...] = jnp.zeros_like(carry_ref)
            dar_ref[...] = jnp.zeros_like(dar_ref)
            dai_ref[...] = jnp.zeros_like(dai_ref)
            dwb_ref[...] = jnp.zeros_like(dwb_ref)
            dwc_ref[...] = jnp.zeros_like(dwc_ref)

        g_ref[...] = _dot_raw(dy_ref[...], wc_ref[...], "nt")
        pw = _s5_powers(ar_ref[...], -ai_ref[...])
        pwr = jnp.concatenate([p[0] for p in reversed(pw)], axis=0)
        pwi = jnp.concatenate([p[1] for p in reversed(pw)], axis=0)
        rin = _iota2((8, _S5_W), 0)
        cr, ci = carry_ref[0:1, :], carry_ref[1:2, :]
        acc_r = jnp.zeros((8, _S5_W), F32)
        acc_i = jnp.zeros((8, _S5_W), F32)
        for t in reversed(range(bt // 8)):
            sl = slice(8 * t, 8 * t + 8)
            lr, li = g_ref[sl, :_S5_W], g_ref[sl, _S5_W:]
            for s in (1, 2, 4):
                m = rin < 8 - s
                sr = jnp.where(m, pltpu.roll(lr, 8 - s, 0), 0.0)
                si = jnp.where(m, pltpu.roll(li, 8 - s, 0), 0.0)
                lr, li = _cmul_add(lr, li, *pw[s - 1], sr, si)
            lr, li = _cmul_add(lr, li, pwr, pwi, cr, ci)
            lam_ref[sl, :_S5_W] = lr
            lam_ref[sl, _S5_W:] = li
            nr = jnp.where(rin == 7, cr, pltpu.roll(lr, 7, 0))
            ni = jnp.where(rin == 7, ci, pltpu.roll(li, 7, 0))
            xr, xi = x_ref[sl, :_S5_W], x_ref[sl, _S5_W:]
            acc_r = acc_r + (xr * nr + xi * ni)
            acc_i = acc_i + (xr * ni - xi * nr)
            cr, ci = lr[0:1, :], li[0:1, :]
        carry_ref[0:1, :] = cr
        carry_ref[1:2, :] = ci
        dar_ref[...] += jnp.sum(acc_r, axis=0, keepdims=True)
        dai_ref[...] += jnp.sum(acc_i, axis=0, keepdims=True)
        lam = lam_ref[...]
        du_ref[...] = _dot_raw(lam, wb_ref[...], "nt")
        dwb_ref[...] += _dot_raw(u_ref[...], lam, "tn")
        dwc_ref[...] += _dot_raw(x_ref[...], dy_ref[...], "tn")

    nblk = S5_NG // S5_BLK
    blk = pl.BlockSpec((bt, 2 * _S5_W), lambda g, t: (nb - 1 - t, g))
    col = pl.BlockSpec((bt, LANE), lambda g, t: (nb - 1 - t, g))
    a_s = pl.BlockSpec((None, 1, _S5_W), lambda g, t: (g, 0, 0))
    wb_s = pl.BlockSpec((None, LANE, 2 * _S5_W), lambda g, t: (g, 0, 0))
    wc_s = pl.BlockSpec((None, 2 * _S5_W, LANE), lambda g, t: (g, 0, 0))
    return pl.pallas_call(
        body, name=name, grid=(nblk, nb), in_specs=[col, blk, col, wb_s, a_s, a_s, wc_s],
        out_specs=[col, wb_s, wc_s, a_s, a_s],
        out_shape=[jax.ShapeDtypeStruct((T, nblk * LANE), F32), jax.ShapeDtypeStruct((nblk, LANE, 2 * _S5_W), F32),
                   jax.ShapeDtypeStruct((nblk, 2 * _S5_W, LANE), F32), jax.ShapeDtypeStruct((nblk, 1, _S5_W), F32),
                   jax.ShapeDtypeStruct((nblk, 1, _S5_W), F32)],
        scratch_shapes=[pltpu.VMEM((bt, 2 * _S5_W), F32), pltpu.VMEM((bt, 2 * _S5_W), F32), pltpu.VMEM((8, _S5_W), F32)],
        compiler_params=_cparams(2),
    )(dy, x, u, wb, a_re, a_im, wc)


def _norm_bf16(h, g, name):
    return rowwise(f_rmsnorm, [g], [h], [(D, BF16)], bt=512, name=name)[0]


def _norm_bwd(h, g, cts, name):
    n = len(cts) - 1

    def f(p, r):
        y = _rms(r[0], p[0])
        return (y,) * n + (r[0],)

    (dg,), (dh,) = rowwise_vjp(f, [g], [h], cts, [F32], bt=256, name=name)
    return dh, dg


def ffn_fwd(h, g, w_gu, w_down, tag):
    hn = _norm_bf16(h, g, f"{tag}_norm")
    a, gu = ffn_up(hn, w_gu, name=f"{tag}_up")
    h2 = matmul(a, w_down[None], "nn", add=h, name=f"{tag}_down")
    return h2, (h, hn, gu, a)


def ffn_bwd(d, saved, g, w_gu, w_down, tag):
    h, hn, gu, a = saved
    dgu = ffn_dact(d, w_down, gu, name=f"{tag}_dact")
    dwd = matmul(a, d, "tn", name=f"{tag}_dwd")[0]
    dwgu = matmul(hn, dgu, "tn", name=f"{tag}_dwgu")[0]
    dh, dg = matmul_nt_norm_bwd(dgu, w_gu, h, g, d, name=f"{tag}_dhn")
    return dh, dg, dwgu, dwd


_GLA_NC = 4
_SSD_NU = 4


def gla_fwd(h, gm, w_in, w_a2, b_a, ng, w_out, tag):
    hn = _norm_bf16(h, gm, f"{tag}_norm")
    proj = matmul(hn, w_in[None], "nn", name=f"{tag}_in")
    alow = (proj, LANE, 2 * (GLA_QK + GLA_VD) // LANE)
    la = rowwise(f_gla_gate_in_fwd, [w_a2, b_a], [alow], [(GLA_QK, F32)], bt=512, name=f"{tag}_gate")[0]
    og, ss = gla_scan_fwd(proj, la, ng, nc=_GLA_NC, name=f"{tag}_scan")
    h2 = matmul(og, w_out[None], "nn", add=h, name=f"{tag}_proj")
    return h2, (h, hn, proj, la, ss, og)


def gla_bwd(d, saved, gm, w_in, w_a2, b_a, ng, w_out, tag):
    h, hn, proj, la, ss, og = saved
    dog = matmul(d, w_out[None], "nt", name=f"{tag}_dog")
    dwout = matmul(og, d, "tn", name=f"{tag}_dwout")[0]
    dq, dk, dv, dla, dr, dng = gla_scan_bwd(proj, la, ng, ss, dog, nc=_GLA_NC, name=f"{tag}_dscan")
    alow = (proj, LANE, 2 * (GLA_QK + GLA_VD) // LANE)
    (dwa2, dba), (dalow,) = rowwise_vjp(f_gla_gate_in, [w_a2, b_a], [alow], [dla], [BF16], bt=512, name=f"{tag}_dgate")
    dproj = jnp.concatenate([dq, dk, dv, dr, dalow], axis=1)
    dwin = matmul(hn, dproj, "tn", name=f"{tag}_dwin")[0]
    dh, dgm = matmul_nt_norm_bwd(dproj, w_in, h, gm, d, name=f"{tag}_dhn")
    return dh, dgm, dwin, dwa2[:GLA_RANK], dba, dng, dwout


def ssd_fwd(h, gm, w_in, conv_w, conv_b, dtb, alog, dsk, ng, w_out, tag):
    hn = _norm_bf16(h, gm, f"{tag}_norm")
    proj = matmul(hn, w_in[None], "nn", name=f"{tag}_in")
    xbc = ssd_conv_fwd(proj, conv_w, conv_b, name=f"{tag}_conv")
    yg, hs = ssd_scan_fwd(xbc, proj, dtb, alog, dsk, ng, nu=_SSD_NU, name=f"{tag}_scan")
    h2 = matmul(yg, w_out[None], "nn", add=h, name=f"{tag}_proj")
    return h2, (h, hn, proj, xbc, hs, yg)


def ssd_bwd(d, saved, gm, w_in, conv_w, conv_b, dtb, alog, dsk, ng, w_out, tag):
    h, hn, proj, xbc, hs, yg = saved
    dyg = matmul(d, w_out[None], "nt", name=f"{tag}_dyg")
    dwout = matmul(yg, d, "tn", name=f"{tag}_dwout")[0]
    dxs, dbm, dcm, ddt, ddtb, dal, ddsk, dz, dng = ssd_scan_bwd(xbc, proj, dtb, alog, dsk, ng, hs, dyg, nu=_SSD_NU,
                                                               name=f"{tag}_dscan")
    dxbc = jnp.concatenate([dxs, dbm, dcm], axis=1)
    dpre, dcw, dcb = ssd_conv_bwd(proj, conv_w, conv_b, dxbc, name=f"{tag}_dconv")
    dproj = jnp.concatenate([dz, dpre, ddt.astype(BF16)], axis=1)
    dwin = matmul(hn, dproj, "tn", name=f"{tag}_dwin")[0]
    dh, dgm = matmul_nt_norm_bwd(dproj, w_in, h, gm, d, name=f"{tag}_dhn")
    return (dh, dgm, dwin, dcw, dcb, ddtb[:, :SSD_H], dal[:, :SSD_H], ddsk[:, :SSD_H], dng, dwout)


_S5_NB = S5_NG // S5_BLK


def _s5_param_args(log_dt, a_re, a_im, b_re, b_im, c_im):
    n = S5_NG * S5_GS
    tr = lambda b: jnp.transpose(b, (0, 2, 1)).reshape(n, S5_P)
    return [log_dt.reshape(S5_NG, 1), a_re, a_im, tr(b_re), tr(b_im), c_im.reshape(n, S5_P)]


def _s5_blockdiag(t):
    nb, gl, a, b = t.shape
    eye = jnp.eye(gl, dtype=t.dtype)
    return (t[:, :, :, None, :] * eye[None, :, None, :, None]).reshape(nb, gl * a, gl * b)


def _s5_diag(t, a, b):
    nb = t.shape[0]
    gl = t.shape[1] // a
    eye = jnp.eye(gl, dtype=t.dtype)
    return jnp.sum(t.reshape(nb, gl, a, gl, b) * eye[None, :, None, :, None], axis=3)


def _s5_weights(bbr, bbi, c_re, cneg):
    sh = (_S5_NB, S5_BLK, S5_GS, S5_P)
    wb = jnp.concatenate([_s5_blockdiag(bbr.reshape(sh)), _s5_blockdiag(bbi.reshape(sh))], axis=2)
    tr = lambda cc: jnp.transpose(cc.reshape(sh), (0, 1, 3, 2))
    wc = jnp.concatenate([_s5_blockdiag(tr(c_re)), _s5_blockdiag(tr(cneg))], axis=1)
    return wb, wc


def s5_fwd(h, gm, prm, dsk, w_glu, tag):
    log_dt, a_re, a_im, b_re, b_im, c_re, c_im = prm
    hn = rowwise(f_rmsnorm, [gm], [h], [(D, F32)], bt=512, name=f"{tag}_norm")[0]
    pargs = _s5_param_args(log_dt, a_re, a_im, b_re, b_im, c_im)
    abr, abi, bbr, bbi, cneg = s5_param_fwd(pargs, name=f"{tag}_param")
    wb, wc = _s5_weights(bbr, bbi, c_re.reshape(S5_NG * S5_GS, S5_P), cneg)
    ar, ai = abr.reshape(_S5_NB, 1, _S5_W), abi.reshape(_S5_NB, 1, _S5_W)
    wb, wc = wb.astype(BF16), wc.astype(BF16)
    x, ycp = s5_scan_fwd(hn, wb, ar, ai, wc, name=f"{tag}_scan")
    yg = rowwise(f_s5_act, [dsk], [ycp, hn], [(D, BF16)], bt=512, name=f"{tag}_act")[0]
    vg = matmul(yg, w_glu[None], "nn", name=f"{tag}_glu")
    h2 = rowwise(f_glu_res, [], [vg, h], [(D, F32)], bt=512, name=f"{tag}_out")[0]
    return h2, (h, hn, pargs, wb, wc, ar, ai, x, ycp, yg, vg)


def s5_bwd(d, saved, gm, dsk, w_glu, tag):
    h, hn, pargs, wb, wc, ar, ai, x, ycp, yg, vg = saved
    _, (dvg,) = rowwise_vjp(f_glu, [], [vg], [d], [BF16], bt=256, name=f"{tag}_dout")
    dwglu = matmul(yg, dvg, "tn", name=f"{tag}_dwglu")[0]
    dyg = matmul(dvg, w_glu[None], "nt", name=f"{tag}_dyg")
    (ddsk,), (dycp, dhn1) = rowwise_vjp(f_s5_act, [dsk], [ycp, hn], [dyg], [F32, F32], bt=256, name=f"{tag}_dact")
    dhn2, dwb, dwc, dar, dai = s5_scan_bwd(dycp, x, hn, wb, ar, ai, wc, name=f"{tag}_dscan")
    dh, dgm = _norm_bwd(h, gm, [dhn1, dhn2, d], f"{tag}_dnorm")
    n = S5_NG * S5_GS
    half = S5_BLK * S5_P
    d_bbr = _s5_diag(dwb[:, :, :half], S5_GS, S5_P).reshape(n, S5_P)
    d_bbi = _s5_diag(dwb[:, :, half:], S5_GS, S5_P).reshape(n, S5_P)
    from_c = lambda t: jnp.transpose(_s5_diag(t, S5_P, S5_GS), (0, 1, 3, 2)).reshape(n, S5_P)
    d_cre = from_c(dwc[:, :half, :])
    d_cneg = from_c(dwc[:, half:, :])
    cts = [dar.reshape(S5_NG, S5_P), dai.reshape(S5_NG, S5_P), d_bbr, d_bbi, d_cneg]
    dlog, dare, daim, dbre_t, dbim_t, dcim = s5_param_bwd(pargs, cts, name=f"{tag}_dparam")
    untr = lambda t: jnp.transpose(t.reshape(S5_NG, S5_GS, S5_P), (0, 2, 1))
    grads = (dlog.reshape(S5_NG), dare, daim, untr(dbre_t), untr(dbim_t),
             d_cre.reshape(S5_NG, S5_GS, S5_P), dcim.reshape(S5_NG, S5_GS, S5_P))
    return dh, dgm, grads, ddsk, dwglu


def _pad_last(w, n):
    return jnp.pad(w, [(0, 0)] * (w.ndim - 1) + [(0, n - w.shape[-1])])


_BIG = ("gla_w_in", "gla_w_out", "ssd_w_in", "ssd_w_out", "s5_w_glu", "ffn_w_gu", "ffn_w_down")


def interleave_gu(w):
    q = w.shape[-1] // 4
    return jnp.concatenate([w[..., :q], w[..., 2 * q:3 * q], w[..., q:2 * q], w[..., 3 * q:]], axis=-1)


def local_step(x, target, W, later_weights=None, later_grads=None, ffn0_grads=None):
    f32 = lambda a: a.astype(F32)
    row = lambda a: f32(a).reshape(1, -1)

    def layer_args(i):
        m, j = i % 3, i // 3
        gm = row(W["norm_mix_g"][i])
        if m == 0:
            args = (gm, W["gla_w_in"][j], jnp.pad(f32(W["gla_w_a2"][j]), ((0, LANE - GLA_RANK), (0, 0))),
                    row(W["gla_b_a"][j]), row(W["gla_norm_g"][j]), W["gla_w_out"][j])
        elif m == 1:
            pl_ = lambda a: _pad_last(row(a), LANE)
            args = (gm, W["ssd_w_in"][j], f32(W["ssd_conv_w"][j]),
                    row(W["ssd_conv_b"][j]), pl_(W["ssd_dt_bias"][j]), pl_(W["ssd_a_log"][j]), pl_(W["ssd_d"][j]),
                    row(W["ssd_norm_g"][j]), W["ssd_w_out"][j])
        else:
            prm = tuple(f32(W[k][j]) for k in ("s5_log_dt", "s5_a_re", "s5_a_im", "s5_b_re", "s5_b_im", "s5_c_re", "s5_c_im"))
            args = (gm, prm, row(W["s5_d"][j]), W["s5_w_glu"][j])
        return (m, j, args), (row(W["norm_ffn_g"][i]), W["ffn_w_gu"][i], W["ffn_w_down"][i])

    h = x
    saved, mixers, ffns = [], [], []
    for i in range(DEPTH):
        mixer, ffn = layer_args(i)
        mixers.append(mixer)
        ffns.append(ffn)
        m, j, args = mixer
        tag = f"l{i}_{('gla', 'ssd', 's5')[m]}"
        h, sm = (gla_fwd, ssd_fwd, s5_fwd)[m](h, *args, tag)
        h, sf = ffn_fwd(h, *ffn, f"l{i}_ffn")
        saved.append((sm, sf))
        if i == 0 and later_weights is not None:
            W = {**W, **later_weights(h)}
    loss, dfg, d = loss_head(h, row(W["final_norm_g"]), target, name="loss_head")

    G = {k: [None] * len(v) for k, v in W.items() if k != "final_norm_g"}
    G["final_norm_g"] = dfg.reshape(D)
    for i in reversed(range(DEPTH)):
        m, j, args = mixers[i]
        sm, sf = saved[i]
        if i == 0 and later_grads is not None:
            zero = later_grads(G)
            ffns[0] = (ffns[0][0], ffns[0][1], ffns[0][2] + zero.astype(ffns[0][2].dtype))
        d, dg, dwgu, dwd = ffn_bwd(d, sf, *ffns[i], f"l{i}_ffn")
        G["norm_ffn_g"][i], G["ffn_w_gu"][i], G["ffn_w_down"][i] = dg.reshape(D), dwgu, dwd
        if i == 0 and ffn0_grads is not None:
            zero = ffn0_grads(G)
            args = args[:-1] + (args[-1] + zero.astype(args[-1].dtype),)
        tag = f"l{i}_{('gla', 'ssd', 's5')[m]}"
        if m == 0:
            d, dgm, dwin, dwa2, dba, dng, dwout = gla_bwd(d, sm, *args, tag)
            G["gla_w_in"][j], G["gla_w_a2"][j], G["gla_b_a"][j] = dwin, dwa2, dba.reshape(-1)
            G["gla_norm_g"][j], G["gla_w_out"][j] = dng.reshape(-1), dwout
        elif m == 1:
            d, dgm, dwin, dcw, dcb, ddtb, dal, ddsk, dng, dwout = ssd_bwd(d, sm, *args, tag)
            G["ssd_w_in"][j], G["ssd_conv_w"][j], G["ssd_conv_b"][j] = dwin, dcw, dcb.reshape(-1)
            G["ssd_dt_bias"][j], G["ssd_a_log"][j], G["ssd_d"][j] = ddtb.reshape(-1), dal.reshape(-1), ddsk.reshape(-1)
            G["ssd_norm_g"][j], G["ssd_w_out"][j] = dng.reshape(-1), dwout
        else:
            d, dgm, pg, ddsk, dwglu = s5_bwd(d, sm, args[0], args[2], args[3], tag)
            for k, v in zip(("s5_log_dt", "s5_a_re", "s5_a_im", "s5_b_re", "s5_b_im", "s5_c_re", "s5_c_im"), pg):
                G[k][j] = v
            G["s5_d"][j], G["s5_w_glu"][j] = ddsk.reshape(-1), dwglu
        G["norm_mix_g"][i] = dgm.reshape(D)
    grads = {k: (v if k == "final_norm_g" or k in _BIG else jnp.stack(v)) for k, v in G.items()}
    return loss, d, grads


_MESH = pl.DeviceIdType.MESH
_ANY = pl.BlockSpec(memory_space=pl.ANY)
_DMA = pltpu.SemaphoreType.DMA
_ROWS_ALIGN = 1024


def _place():
    return lax.axis_index("x"), lax.axis_index("y"), lax.axis_index("c")


def _other_chips(x, y):
    return [(1 - x, y), (x, 1 - y), (1 - x, 1 - y)]


def _remote(src, dst, send_sems, recv_sems, k, to):
    return pltpu.make_async_remote_copy(src_ref=src, dst_ref=dst, send_sem=send_sems.at[k], recv_sem=recv_sems.at[k],
                                        device_id=to, device_id_type=_MESH)


def gather_shards(loc, *, name):
    def body(in_ref, out_ref, send_sems, recv_sems, local_sem):
        x, y, c = _place()
        me, sibling = (x, y, c), (x, y, 1 - c)
        chips = _other_chips(x, y)

        def half(px, py, hc):
            return out_ref.at[2 * px + py, hc]

        mine = pltpu.make_async_copy(in_ref, out_ref.at[2 * x + y], local_sem)
        mine.start()
        first = [_remote(in_ref.at[c], half(x, y, c), send_sems, recv_sems, j, (*chip, c)) for j, chip in enumerate(chips)]
        for cp in first:
            cp.start()
        passed = [_remote(half(*chip, c), half(*chip, c), send_sems, recv_sems, 3 + j, sibling) for j, chip in enumerate(chips)]
        for j, chip in enumerate(chips):
            _remote(in_ref.at[c], half(*chip, c), send_sems, recv_sems, j, me).wait_recv()
            passed[j].start()
        for j, chip in enumerate(chips):
            _remote(in_ref.at[c], half(*chip, 1 - c), send_sems, recv_sems, 3 + j, me).wait_recv()
        for cp in first + passed:
            cp.wait_send()
        mine.wait()

    return pl.pallas_call(
        body, name=name, in_specs=[_ANY], out_specs=_ANY,
        out_shape=jax.ShapeDtypeStruct((4,) + loc.shape, loc.dtype),
        scratch_shapes=[_DMA((6,)), _DMA((6,)), _DMA(())],
    )(loc)


def _pos(px, py, perm):
    return 2 * py + px if perm else 2 * px + py


def _part(ref, kind, p, loc):
    if kind == "lead":
        return ref.at[p]
    return ref.at[:, pl.ds(pl.multiple_of(p * loc, LANE), loc)]


def _rows(ref, h, hr):
    return ref.at[pl.ds(h * hr, hr)]


def _rows_block(hr, width):
    return max(b for b in range(16, hr + 1, 16) if hr % b == 0 and (b * width <= (1 << 19) or b == 16))


def gather_big(locs, kinds, *, name):
    n = len(locs)

    def body(*refs):
        ins, outs = refs[:n], refs[n:2 * n]
        send_sems, recv_sems = refs[2 * n + 1:]
        refs[2 * n][...] = jnp.zeros_like(refs[2 * n])
        x, y, c = _place()
        me, sibling = (x, y, c), (x, y, 1 - c)
        chips = _other_chips(x, y)

        def half(i, px, py, h):
            (kind, perm), (rows, loc) = kinds[i], locs[i].shape
            return _rows(_part(outs[i], kind, _pos(px, py, perm), loc), h, rows // 2)

        sends = []
        for i in range(n):
            (kind, perm), (rows, loc) = kinds[i], locs[i].shape
            own = _part(outs[i], kind, _pos(x, y, perm), loc)
            sends.append(_remote(ins[i], own, send_sems, recv_sems, 6 * n + i, sibling))
            sends[-1].start()
            for j, chip in enumerate(chips):
                sends.append(_remote(_rows(ins[i], c, rows // 2), half(i, x, y, c), send_sems, recv_sems, 6 * i + j, (*chip, c)))
                sends[-1].start()
        for i in range(n):
            hr = locs[i].shape[0] // 2
            for j, chip in enumerate(chips):
                _remote(_rows(ins[i], c, hr), half(i, *chip, c), send_sems, recv_sems, 6 * i + j, me).wait_recv()
                sends.append(_remote(half(i, *chip, c), half(i, *chip, c), send_sems, recv_sems, 6 * i + 3 + j, sibling))
                sends[-1].start()
        for i in range(n):
            (kind, perm), (rows, loc) = kinds[i], locs[i].shape
            for j, chip in enumerate(chips):
                _remote(_rows(ins[i], c, rows // 2), half(i, *chip, 1 - c), send_sems, recv_sems, 6 * i + 3 + j, me).wait_recv()
            _remote(ins[i], _part(outs[i], kind, _pos(x, y, perm), loc), send_sems, recv_sems, 6 * n + i, me).wait_recv()
        for cp in sends:
            cp.wait_send()

    def out_shape(a, kind):
        rows, loc = a.shape
        return jax.ShapeDtypeStruct((4, rows, loc) if kind == "lead" else (rows, 4 * loc), a.dtype)

    outs = pl.pallas_call(
        body, name=name, in_specs=[_ANY] * n, out_specs=[_ANY] * n + [pl.BlockSpec(memory_space=pltpu.VMEM)],
        out_shape=[out_shape(a, k[0]) for a, k in zip(locs, kinds)] + [jax.ShapeDtypeStruct((8, LANE), F32)],
        scratch_shapes=[_DMA((7 * n,)), _DMA((7 * n,))],
    )(*locs)
    return list(outs[:n]), outs[n][0, 0]


_HBM = pl.BlockSpec(memory_space=pltpu.HBM)
_SEM = pl.BlockSpec(memory_space=pltpu.SEMAPHORE)
_EFFECT = pltpu.SideEffectType.DATAFLOW_SIDE_EFFECTING


def _in_hbm(a):
    return pltpu.with_memory_space_constraint(a, pltpu.HBM)


def _gather_ici_copies(ins, lands, kinds, shapes, send_sems, recv_sems):
    x, y, c = _place()
    sends, arrivals = [], []
    for i, ((kind, perm), (rows, loc)) in enumerate(zip(kinds, shapes)):
        hr = rows // 2
        mine = _part(lands[i], kind, _pos(x, y, perm), loc)
        sends.append(_remote(ins[i], mine, send_sems, recv_sems, 4 * i + 3, (x, y, 1 - c)))
        arrivals.append(_remote(ins[i], mine, send_sems, recv_sems, 4 * i + 3, (x, y, c)))
        for j, (px, py) in enumerate(_other_chips(x, y)):
            sends.append(_remote(_rows(ins[i], c, hr), _rows(mine, c, hr), send_sems, recv_sems, 4 * i + j, (px, py, c)))
            theirs = _rows(_part(lands[i], kind, _pos(px, py, perm), loc), c, hr)
            arrivals.append(_remote(_rows(ins[i], c, hr), theirs, send_sems, recv_sems, 4 * i + j, (x, y, c)))
    return sends, arrivals


def gather_start(locs, kinds, *, name):
    n = len(locs)
    shapes = [a.shape for a in locs]

    def land_shape(a, kind):
        rows, loc = a.shape
        return (4, rows, loc) if kind == "lead" else (rows, 4 * loc)

    def body(*refs):
        sends, _ = _gather_ici_copies(refs[:n], refs[n:2 * n], kinds, shapes, refs[2 * n], refs[2 * n + 1])
        for cp in sends:
            cp.start()
        refs[-1][...] = jnp.zeros_like(refs[-1])

    lands = [lax.empty(land_shape(a, k[0]), a.dtype) for a, k in zip(locs, kinds)]
    outs = pl.pallas_call(
        body, name=name, in_specs=[_HBM] * (2 * n), out_specs=[_SEM, _SEM] + [_HBM] * (2 * n) + [pl.BlockSpec(memory_space=pltpu.VMEM)],
        out_shape=[_DMA((4 * n,)), _DMA((4 * n,))] + [pltpu.HBM(a.shape, a.dtype) for a in locs]
        + [pltpu.HBM(l.shape, l.dtype) for l in lands] + [jax.ShapeDtypeStruct((8, LANE), F32)],
        input_output_aliases={i: 2 + i for i in range(2 * n)},
        compiler_params=pltpu.CompilerParams(has_side_effects=_EFFECT),
    )(*[_in_hbm(a) for a in locs], *[_in_hbm(l) for l in lands])
    return outs[0], outs[1], list(outs[2:2 + n]), list(outs[2 + n:2 + 2 * n]), outs[-1][0, 0]


def gather_wait(send_sems, recv_sems, locs, lands, kinds, after, *, name):
    n = len(locs)
    shapes = [a.shape for a in locs]

    def body(*refs):
        sends, arrivals = _gather_ici_copies(refs[:n], refs[n:2 * n], kinds, shapes, refs[2 * n], refs[2 * n + 1])
        for cp in sends:
            cp.wait_send()
        for cp in arrivals:
            cp.wait_recv()

    outs = pl.pallas_call(
        body, name=name, in_specs=[_HBM] * (2 * n) + [_SEM, _SEM, _ANY], out_specs=[_HBM] * (2 * n),
        out_shape=[pltpu.HBM(a.shape, a.dtype) for a in locs] + [pltpu.HBM(l.shape, l.dtype) for l in lands],
        input_output_aliases={i: i for i in range(2 * n)},
        compiler_params=pltpu.CompilerParams(has_side_effects=_EFFECT),
    )(*locs, *lands, send_sems, recv_sems, after)
    return list(outs[n:])


def gather_finish(lands, kinds, shapes, *, name):
    n = len(lands)

    def body(*refs):
        bufs = refs[n:2 * n]
        send_sems, recv_sems = refs[2 * n:]
        x, y, c = _place()
        sends = []
        for i, ((kind, perm), (rows, loc)) in enumerate(zip(kinds, shapes)):
            for j, (px, py) in enumerate(_other_chips(x, y)):
                part = _part(bufs[i], kind, _pos(px, py, perm), loc)
                sends.append(_remote(_rows(part, c, rows // 2), _rows(part, c, rows // 2), send_sems, recv_sems, 3 * i + j, (x, y, 1 - c)))
                sends[-1].start()
        for i, ((kind, perm), (rows, loc)) in enumerate(zip(kinds, shapes)):
            for j, (px, py) in enumerate(_other_chips(x, y)):
                part = _part(bufs[i], kind, _pos(px, py, perm), loc)
                _remote(_rows(part, c, rows // 2), _rows(part, 1 - c, rows // 2), send_sems, recv_sems, 3 * i + j, (x, y, c)).wait_recv()
        for cp in sends:
            cp.wait_send()

    return list(pl.pallas_call(
        body, name=name, in_specs=[_ANY] * n, out_specs=[_ANY] * n,
        out_shape=[jax.ShapeDtypeStruct(l.shape, l.dtype) for l in lands],
        input_output_aliases={i: i for i in range(n)}, scratch_shapes=[_DMA((3 * n,)), _DMA((3 * n,))],
    )(*lands))


def _scatter_copies(qs, lands, kinds, locs, send_sems, recv_sems):
    x, y, c = _place()
    sends, arrivals = [], []
    for i, (kind, perm) in enumerate(kinds):
        for j, (px, py) in enumerate(_other_chips(x, y)):
            src = _part(qs[i], kind, _pos(px, py, perm), locs[i])
            sends.append(_remote(src, lands[i].at[j], send_sems, recv_sems, 3 * i + j, (px, py, c)))
            arrivals.append(_remote(src, lands[i].at[j], send_sems, recv_sems, 3 * i + j, (x, y, c)))
    return sends, arrivals


def _scatter_land(q, kind, loc):
    return (3, q.shape[1] if kind == "lead" else q.shape[0], loc)


def scatter_start(qs, kinds, locs, *, name):
    n = len(qs)

    def body(*refs):
        sends, _ = _scatter_copies(refs[:n], refs[n:2 * n], kinds, locs, refs[2 * n], refs[2 * n + 1])
        for cp in sends:
            cp.start()
        refs[-1][...] = jnp.zeros_like(refs[-1])

    lands = [lax.empty(_scatter_land(q, k[0], l), q.dtype) for q, k, l in zip(qs, kinds, locs)]
    outs = pl.pallas_call(
        body, name=name, in_specs=[_HBM] * (2 * n), out_specs=[_SEM, _SEM] + [_HBM] * (2 * n) + [pl.BlockSpec(memory_space=pltpu.VMEM)],
        out_shape=[_DMA((3 * n,)), _DMA((3 * n,))] + [pltpu.HBM(q.shape, q.dtype) for q in qs]
        + [pltpu.HBM(l.shape, l.dtype) for l in lands] + [jax.ShapeDtypeStruct((8, LANE), F32)],
        input_output_aliases={i: 2 + i for i in range(2 * n)},
        compiler_params=pltpu.CompilerParams(has_side_effects=_EFFECT),
    )(*[_in_hbm(q) for q in qs], *[_in_hbm(l) for l in lands])
    return outs[0], outs[1], list(outs[2:2 + n]), list(outs[2 + n:2 + 2 * n]), outs[-1][0, 0]


def scatter_wait(send_sems, recv_sems, qs, lands, kinds, locs, after, *, name):
    n = len(qs)

    def body(*refs):
        sends, arrivals = _scatter_copies(refs[:n], refs[n:2 * n], kinds, locs, refs[2 * n], refs[2 * n + 1])
        for cp in sends:
            cp.wait_send()
        for cp in arrivals:
            cp.wait_recv()

    outs = pl.pallas_call(
        body, name=name, in_specs=[_HBM] * (2 * n) + [_SEM, _SEM, _ANY], out_specs=[_HBM] * (2 * n),
        out_shape=[pltpu.HBM(q.shape, q.dtype) for q in qs] + [pltpu.HBM(l.shape, l.dtype) for l in lands],
        input_output_aliases={i: i for i in range(2 * n)},
        compiler_params=pltpu.CompilerParams(has_side_effects=_EFFECT),
    )(*qs, *lands, send_sems, recv_sems, after)
    return list(outs[:n]), list(outs[n:])


def pair_swap(ps, kinds, *, name):
    n = len(ps)

    def body(*refs):
        ins, outs = refs[:n], refs[n:2 * n]
        send_sems, recv_sems = refs[2 * n:]
        x, y, c = _place()
        cps = []
        for i in range(n):
            if kinds[i][0] == "lead":
                hr = ps[i].shape[1] // 2
                src = ins[i].at[:, pl.ds((1 - c) * hr, hr)]
            else:
                hr = ps[i].shape[0] // 2
                src = _rows(ins[i], 1 - c, hr)
            cps.append(_remote(src, outs[i], send_sems, recv_sems, i, (x, y, 1 - c)))
            cps[-1].start()
        for cp in cps:
            cp.wait()

    def out_shape(a, kind):
        s = a.shape
        return jax.ShapeDtypeStruct((4, s[1] // 2, s[2]) if kind == "lead" else (s[0] // 2, s[1]), a.dtype)

    return pl.pallas_call(
        body, name=name, in_specs=[_ANY] * n, out_specs=[_ANY] * n,
        out_shape=[out_shape(a, k[0]) for a, k in zip(ps, kinds)], scratch_shapes=[_DMA((n,)), _DMA((n,))],
    )(*ps)


def pair_add(p, got, c_arr, kind, *, name):
    if kind == "lead":
        _, hr, cols = got.shape
        br = _rows_block(hr, cols)
        nb = hr // br
        grid = (4, nb)
        p_spec = pl.BlockSpec((None, br, cols), lambda s, i, cr: (s, cr[0] * nb + i, 0))
        g_spec = pl.BlockSpec((None, br, cols), lambda s, i, cr: (s, i, 0))
    else:
        hr, w = got.shape
        br = _rows_block(hr, w)
        nb = hr // br
        grid = (nb,)
        p_spec = pl.BlockSpec((br, w), lambda i, cr: (cr[0] * nb + i, 0))
        g_spec = pl.BlockSpec((br, w), lambda i, cr: (i, 0))

    def body(c_ref, p_ref, g_ref, o_ref):
        o_ref[...] = (p_ref[...] + g_ref[...]).astype(o_ref.dtype)

    return pl.pallas_call(
        body, name=name, out_shape=jax.ShapeDtypeStruct(got.shape, BF16),
        grid_spec=pltpu.PrefetchScalarGridSpec(num_scalar_prefetch=1, grid=grid, in_specs=[p_spec, g_spec], out_specs=g_spec),
        compiler_params=_cparams(len(grid)),
    )(c_arr, p, got)


def chip_scatter(qs, kinds, locs, *, name):
    n = len(qs)

    def body(*refs):
        ins, outs = refs[:n], refs[n:2 * n]
        send_sems, recv_sems = refs[2 * n:]
        x, y, c = _place()
        cps = []
        for i in range(n):
            kind, perm = kinds[i]
            for j, (px, py) in enumerate(_other_chips(x, y)):
                cps.append(_remote(_part(ins[i], kind, _pos(px, py, perm), locs[i]), outs[i].at[j], send_sems, recv_sems,
                                   3 * i + j, (px, py, c)))
                cps[-1].start()
        for cp in cps:
            cp.wait()

    def out_shape(a, kind, loc):
        hr = a.shape[1] if kind == "lead" else a.shape[0]
        return jax.ShapeDtypeStruct((3, hr, loc), a.dtype)

    return pl.pallas_call(
        body, name=name, in_specs=[_ANY] * n, out_specs=[_ANY] * n,
        out_shape=[out_shape(a, k[0], l) for a, k, l in zip(qs, kinds, locs)],
        scratch_shapes=[_DMA((3 * n,)), _DMA((3 * n,))],
    )(*qs)


def chip_add(q, r, pos_arr, c_arr, kind, loc, *, name):
    _, hr, _ = r.shape
    br = _rows_block(hr, loc)
    nb = hr // br
    if kind == "lead":
        q_spec = pl.BlockSpec((None, br, loc), lambda i, pr, cr: (pr[0], i, 0))
    else:
        q_spec = pl.BlockSpec((br, loc), lambda i, pr, cr: (i, pr[0]))
    r_spec = pl.BlockSpec((3, br, loc), lambda i, pr, cr: (0, i, 0))
    o_spec = pl.BlockSpec((br, loc), lambda i, pr, cr: (cr[0] * nb + i, 0))

    def body(p_ref, c_ref, q_ref, r_ref, o_ref):
        acc = q_ref[...].astype(F32)
        for j in range(3):
            acc = acc + r_ref[j].astype(F32)
        o_ref[...] = acc

    return pl.pallas_call(
        body, name=name, out_shape=jax.ShapeDtypeStruct((2 * hr, loc), F32),
        grid_spec=pltpu.PrefetchScalarGridSpec(num_scalar_prefetch=2, grid=(nb,), in_specs=[q_spec, r_spec], out_specs=o_spec),
        compiler_params=_cparams(1),
    )(pos_arr, c_arr, q, r)


def share_rows(fs, *, name):
    n = len(fs)

    def body(*refs):
        bufs = refs[n:2 * n]
        send_sems, recv_sems = refs[2 * n:]
        x, y, c = _place()
        cps = []
        for i in range(n):
            hr = fs[i].shape[0] // 2
            cps.append(_remote(_rows(bufs[i], c, hr), _rows(bufs[i], c, hr), send_sems, recv_sems, i, (x, y, 1 - c)))
            cps[-1].start()
        for i, cp in enumerate(cps):
            hr = fs[i].shape[0] // 2
            _remote(_rows(bufs[i], c, hr), _rows(bufs[i], 1 - c, hr), send_sems, recv_sems, i, (x, y, c)).wait_recv()
            cp.wait_send()

    return pl.pallas_call(
        body, name=name, in_specs=[_ANY] * n, out_specs=[_ANY] * n,
        out_shape=[jax.ShapeDtypeStruct(f.shape, f.dtype) for f in fs],
        input_output_aliases={i: i for i in range(n)}, scratch_shapes=[_DMA((n,)), _DMA((n,))],
    )(*fs)


def gather_all(v, *, name):
    def body(v_ref, out_ref, send_sems, recv_sems, local_sem):
        x, y, c = _place()
        flip = lambda p, m: 1 - p if m else p
        peers = [(flip(x, m & 4), flip(y, m & 2), flip(c, m & 1)) for m in range(1, 8)]
        idx = lambda p: 4 * p[0] + 2 * p[1] + p[2]
        mine = pltpu.make_async_copy(v_ref, out_ref.at[idx((x, y, c))], local_sem)
        mine.start()
        cps = [_remote(v_ref, out_ref.at[idx((x, y, c))], send_sems, recv_sems, k, p) for k, p in enumerate(peers)]
        for cp in cps:
            cp.start()
        for k, p in enumerate(peers):
            _remote(v_ref, out_ref.at[idx(p)], send_sems, recv_sems, k, p).wait_recv()
        for cp in cps:
            cp.wait_send()
        mine.wait()

    return pl.pallas_call(
        body, name=name, in_specs=[_ANY], out_specs=_ANY, out_shape=jax.ShapeDtypeStruct((8,) + v.shape, v.dtype),
        scratch_shapes=[_DMA((7,)), _DMA((7,)), _DMA(())],
    )(v)


def sum_stack(a, extra=None, *, name):
    n, R, L = a.shape
    br = _pick(R, _ROWS_ALIGN, 8)

    def body(*refs):
        a_ref, o_ref = refs[0], refs[-1]
        acc = refs[1][...] if extra is not None else a_ref[0]
        for i in range(0 if extra is not None else 1, n):
            acc = acc + a_ref[i]
        o_ref[...] = acc

    row = pl.BlockSpec((br, L), lambda i: (i, 0))
    specs = [pl.BlockSpec((n, br, L), lambda i: (0, i, 0))] + ([row] if extra is not None else [])
    args = [a] + ([extra] if extra is not None else [])
    return pl.pallas_call(body, name=name, grid=(R // br,), in_specs=specs, out_specs=row,
                          out_shape=jax.ShapeDtypeStruct((R, L), a.dtype), compiler_params=_cparams(1))(*args)


def adamw(w, g, m, v, *, name):
    shape = w.shape
    size = math.prod(shape)
    last = shape[-1]
    if last % LANE != 0 and size % LANE == 0 and size <= (1 << 20):
        last = LANE
    rows = size // last
    budget = (1 << 18) // last
    br = rows
    if rows > budget:
        br = max(c for c in range(8, budget + 1, 8) if rows % c == 0)
    v2 = lambda a: a.reshape(rows, last)

    def body(w_ref, g_ref, m_ref, v_ref, d_ref, nm_ref, nv_ref):
        gg = g_ref[...]
        nm = ADAM_B1 * m_ref[...] + (1.0 - ADAM_B1) * gg
        nv = ADAM_B2 * v_ref[...] + (1.0 - ADAM_B2) * (gg * gg)
        m_hat = nm / (1.0 - ADAM_B1 ** ADAM_STEP)
        v_hat = nv / (1.0 - ADAM_B2 ** ADAM_STEP)
        d_ref[...] = -ADAM_LR * (m_hat / (jnp.sqrt(v_hat) + ADAM_EPS) + ADAM_WD * w_ref[...])
        nm_ref[...] = nm
        nv_ref[...] = nv

    spec = pl.BlockSpec((br, last), lambda i: (i, 0))
    outs = pl.pallas_call(
        body, name=name, grid=(rows // br,), in_specs=[spec] * 4, out_specs=[spec] * 3,
        out_shape=[jax.ShapeDtypeStruct((rows, last), F32)] * 3, compiler_params=_cparams(1),
    )(v2(w), v2(g), v2(m), v2(v))
    return [o.reshape(shape) for o in outs]


_WEIGHTS = ["norm_mix_g", "norm_ffn_g", "gla_w_in", "gla_w_a2", "gla_b_a", "gla_norm_g", "gla_w_out", "ssd_w_in",
            "ssd_conv_w", "ssd_conv_b", "ssd_dt_bias", "ssd_a_log", "ssd_d", "ssd_norm_g", "ssd_w_out", "s5_log_dt",
            "s5_a_re", "s5_a_im", "s5_b_re", "s5_b_im", "s5_c_re", "s5_c_im", "s5_d", "s5_w_glu", "ffn_w_gu",
            "ffn_w_down", "final_norm_g"]
_SHARD_AXIS = {"gla_w_in": 2, "gla_w_a2": 2, "gla_b_a": 1, "gla_norm_g": 1, "gla_w_out": 1, "ssd_w_in": 2,
               "ssd_conv_w": 2, "ssd_w_out": 1, "s5_d": 1, "s5_w_glu": 2, "ffn_w_gu": 2, "ffn_w_down": 1}
_SMALL_SHARDED = [n for n in _WEIGHTS if n in _SHARD_AXIS and n not in _BIG]
_REPLICATED = [n for n in _WEIGHTS if n not in _SHARD_AXIS]
_BIG_KIND = {"gla_w_in": ("lead", False), "gla_w_out": ("lead", False), "ssd_w_in": ("lead", False),
             "ssd_w_out": ("lead", False), "s5_w_glu": ("cols", False), "ffn_w_gu": ("cols", True),
             "ffn_w_down": ("lead", False)}
_PADDED_IN = {"gla_w_in": GLA_INP, "ssd_w_in": SSD_INP}


def _to_rows(flat, parts=1):
    per = -(-flat.shape[0] // (parts * LANE * _ROWS_ALIGN)) * _ROWS_ALIGN
    flat = jnp.pad(flat, (0, parts * per * LANE - flat.shape[0]))
    return flat.reshape(parts, per, LANE)


def _big_layers(local):
    return [(n, j, local[n][j].reshape(-1, local[n].shape[-1])) for n in _BIG for j in range(local[n].shape[0])]


def _in_layer0(n, j):
    return j == 0 and n in ("gla_w_in", "gla_w_out", "ffn_w_gu", "ffn_w_down")


def _assemble(n, g):
    if n in _PADDED_IN:
        return jnp.concatenate([g[s] for s in range(4)] + [jnp.zeros((g.shape[1], _PADDED_IN[n] - 4 * g.shape[2]), BF16)], axis=1)
    if _BIG_KIND[n][0] == "lead":
        return g.reshape(4 * g.shape[1], g.shape[2])
    return g


def _gather_first(local):
    layers = _big_layers(local)
    first = [l for l in layers if _in_layer0(l[0], l[1])]
    later = [l for l in layers if not _in_layer0(l[0], l[1])]
    full = {n: [None] * local[n].shape[0] for n in _BIG}
    got, done = gather_big([w.astype(BF16) for _, _, w in first], [_BIG_KIND[n] for n, _, _ in first], name="gather_weights_first")
    for (n, j, _), g in zip(first, got):
        full[n][j] = _assemble(n, g)
    flat = jnp.concatenate([local[n].astype(F32).reshape(-1) for n in _SMALL_SHARDED])
    got = gather_shards(_to_rows(flat, 2), name="gather_small_weights").reshape(4, -1)
    off = 0
    for n in _SMALL_SHARDED:
        bs = local[n].shape
        sz = math.prod(bs)
        seg = got[:, off:off + sz].reshape((4,) + bs)
        off += sz
        ax = _SHARD_AXIS[n]
        full[n] = jnp.moveaxis(seg, 0, ax).reshape(bs[:ax] + (4 * bs[ax],) + bs[ax + 1:])
    kinds = [_BIG_KIND[n] for n, _, _ in later]
    ops = [(w + done if k == 0 else w).astype(BF16) for k, (_, _, w) in enumerate(later)]
    send_sems, recv_sems, locs, lands, zero = gather_start(ops, kinds, name="gather_weights_start")
    return full, (later, kinds, send_sems, recv_sems, locs, lands), zero


def _gather_rest(full, pending, after):
    later, kinds, send_sems, recv_sems, locs, lands = pending
    lands = gather_wait(send_sems, recv_sems, locs, lands, kinds, after, name="gather_weights_wait")
    lands = gather_finish(lands, kinds, [w.shape for _, _, w in later], name="gather_weights_finish")
    out = {n: list(full[n]) for n in _BIG}
    for (n, j, _), g in zip(later, lands):
        out[n][j] = _assemble(n, g)
    return out


def _reduce_ops(grads, local, want):
    ops = []
    for n in _BIG:
        kind = _BIG_KIND[n]
        for j, g in enumerate(grads[n]):
            if not want(n, j):
                continue
            loc = local[n].shape[-1] if kind[0] == "cols" or n in _PADDED_IN else g.shape[1]
            if n in _PADDED_IN:
                g = jnp.stack([g[:, s * loc:(s + 1) * loc] for s in range(4)])
            elif kind[0] == "lead":
                g = g.reshape(4, g.shape[0] // 4, g.shape[1])
            ops.append((n, j, kind, loc, g))
    return ops


def _pair_sums(ops, c_arr, tag):
    gots = pair_swap([o[4] for o in ops], [o[2] for o in ops], name=f"reduce_pair_swap_{tag}")
    return [pair_add(o[4], got, c_arr, o[2][0], name=f"reduce_pair_add_{o[0]}{o[1]}") for o, got in zip(ops, gots)]


def _is_ffn0(n, j):
    return j == 0 and n in ("ffn_w_gu", "ffn_w_down")


def _reduce_start(grads, local, c, want, tag):
    ops = _reduce_ops(grads, local, want)
    c_arr = jnp.reshape(c, (1,)).astype(jnp.int32)
    qs = _pair_sums(ops, c_arr, tag)
    send_sems, recv_sems, qs, lands, zero = scatter_start(qs, [o[2] for o in ops], [o[3] for o in ops],
                                                          name=f"reduce_scatter_start_{tag}")
    return (ops, send_sems, recv_sems, qs, lands, tag), zero


def _reduce_big(grads, local, pendings, after, x, y, c):
    c_arr = jnp.reshape(c, (1,)).astype(jnp.int32)
    ops, qs, rs = [], [], []
    for ops_p, send_sems, recv_sems, qs_p, lands, tag in pendings:
        qs_p, rs_p = scatter_wait(send_sems, recv_sems, qs_p, lands, [o[2] for o in ops_p], [o[3] for o in ops_p], after,
                                  name=f"reduce_scatter_wait_{tag}")
        ops, qs, rs = ops + ops_p, qs + qs_p, rs + rs_p
    ops_f = _reduce_ops(grads, local, lambda n, j: _in_layer0(n, j) and not _is_ffn0(n, j))
    qs_f = _pair_sums(ops_f, c_arr, "first")
    rs_f = list(chip_scatter(qs_f, [o[2] for o in ops_f], [o[3] for o in ops_f], name="reduce_chip_scatter_first"))
    ops, qs, rs = ops + ops_f, qs + qs_f, rs + rs_f
    fs = [chip_add(q, r, jnp.reshape(_pos(x, y, o[2][1]), (1,)).astype(jnp.int32), c_arr, o[2][0], o[3],
                   name=f"reduce_chip_add_{o[0]}{o[1]}") for o, q, r in zip(ops, qs, rs)]
    outs = share_rows(fs, name="reduce_share")
    red = {(o[0], o[1]): r for o, r in zip(ops, outs)}
    return {n: jnp.stack([red[(n, j)] for j in range(local[n].shape[0])]).reshape(local[n].shape) for n in _BIG}


def _reduce_small(grads, local, x, y):
    names = _REPLICATED + _SMALL_SHARDED
    flat = jnp.concatenate([grads[n].astype(F32).reshape(-1) for n in names])
    n_el = flat.shape[0]
    rows = -(-n_el // (LANE * 8)) * 8
    v = jnp.pad(flat, (0, rows * LANE - n_el)).reshape(rows, LANE)
    red = sum_stack(gather_all(v, name="reduce_small_gather"), name="reduce_small_add").reshape(-1)
    out, off = {}, 0
    for n in names:
        sz = math.prod(grads[n].shape)
        g = red[off:off + sz].reshape(grads[n].shape)
        off += sz
        if n in _SHARD_AXIS:
            ax = _SHARD_AXIS[n]
            loc = local[n].shape[ax]
            g = lax.dynamic_slice_in_dim(g, (2 * x + y) * loc, loc, axis=ax)
        out[n] = g
    return out


def kernel(x, norm_mix_g, norm_ffn_g, gla_w_in, gla_w_a2, gla_b_a, gla_norm_g, gla_w_out, ssd_w_in, ssd_conv_w, ssd_conv_b, ssd_dt_bias, ssd_a_log, ssd_d, ssd_norm_g, ssd_w_out, s5_log_dt, s5_a_re, s5_a_im, s5_b_re, s5_b_im, s5_c_re, s5_c_im, s5_d, s5_w_glu, ffn_w_gu, ffn_w_down, final_norm_g, loss_target, m_norm_mix_g, m_norm_ffn_g, m_gla_w_in, m_gla_w_a2, m_gla_b_a, m_gla_norm_g, m_gla_w_out, m_ssd_w_in, m_ssd_conv_w, m_ssd_conv_b, m_ssd_dt_bias, m_ssd_a_log, m_ssd_d, m_ssd_norm_g, m_ssd_w_out, m_s5_log_dt, m_s5_a_re, m_s5_a_im, m_s5_b_re, m_s5_b_im, m_s5_c_re, m_s5_c_im, m_s5_d, m_s5_w_glu, m_ffn_w_gu, m_ffn_w_down, m_final_norm_g, v_norm_mix_g, v_norm_ffn_g, v_gla_w_in, v_gla_w_a2, v_gla_b_a, v_gla_norm_g, v_gla_w_out, v_ssd_w_in, v_ssd_conv_w, v_ssd_conv_b, v_ssd_dt_bias, v_ssd_a_log, v_ssd_d, v_ssd_norm_g, v_ssd_w_out, v_s5_log_dt, v_s5_a_re, v_s5_a_im, v_s5_b_re, v_s5_b_im, v_s5_c_re, v_s5_c_im, v_s5_d, v_s5_w_glu, v_ffn_w_gu, v_ffn_w_down, v_final_norm_g):
    given = dict(locals())
    local = {n: given[n] for n in _WEIGHTS}
    px, py, pc = _place()

    first, gathering, zero = _gather_first(local)
    full = dict(local)
    full.update(first)
    full["norm_mix_g"] = local["norm_mix_g"] + zero
    reducing = []

    def later_grads(g):
        pending, zero = _reduce_start(g, local, pc, lambda n, j: not _in_layer0(n, j), "later")
        reducing.append(pending)
        return zero

    def ffn0_grads(g):
        pending, zero = _reduce_start(g, local, pc, _is_ffn0, "ffn0")
        reducing.append(pending)
        return zero

    loss, grad_x, grads = local_step(x[0], loss_target[0], full, lambda h: _gather_rest(first, gathering, h), later_grads,
                                     ffn0_grads)
    loss = lax.psum(loss, ("x", "y", "c"))

    red = _reduce_big(grads, local, reducing, grad_x, px, py, pc)
    red.update(_reduce_small(grads, local, px, py))

    deltas, new_m, new_v = {}, {}, {}
    for n in _WEIGHTS:
        deltas[n], new_m[n], new_v[n] = adamw(local[n], red[n], given["m_" + n], given["v_" + n], name=f"adamw_{n}")
    return (loss, grad_x[None], *[red[n] for n in _WEIGHTS], *[deltas[n] for n in _WEIGHTS],
            *[new_m[n] for n in _WEIGHTS], *[new_v[n] for n in _WEIGHTS])
```

```python
import functools
import math

import jax
import jax.numpy as jnp
from jax import lax
from jax.experimental import pallas as pl
from jax.experimental.pallas import tpu as pltpu

F32 = jnp.float32
BF16 = jnp.bfloat16

D = 1024
DEPTH = 4
CH = 64
EPS = 1e-6
GLA_H, GLA_DK, GLA_DV, GLA_RANK, GLA_TAU = 4, 128, 256, 16, 16.0
GLA_QK = GLA_H * GLA_DK
GLA_VD = GLA_H * GLA_DV
GLA_IN = 2 * GLA_QK + 2 * GLA_VD + GLA_RANK
GLA_INP = 3200
SSD_DI, SSD_HD, SSD_H, SSD_G, SSD_N, SSD_K = 2048, 64, 32, 8, 128, 4
SSD_GN = SSD_G * SSD_N
SSD_CONV = SSD_DI + 2 * SSD_GN
SSD_IN = SSD_DI + SSD_CONV + SSD_H
SSD_INP = 6272
S5_GS, S5_NG, S5_P = 16, 64, 64
S5_BLK = 8
FFN_H = 2816
LANE = 128
VMEM_LIMIT = 52 * 1024 * 1024
_MATMUL_VMEM = 40 * 1024 * 1024

ADAM_LR, ADAM_B1, ADAM_B2, ADAM_EPS, ADAM_WD, ADAM_STEP = 0.001, 0.9, 0.999, 1e-08, 0.01, 10

_ARB = "arbitrary"


def _cparams(n):
    return pltpu.CompilerParams(dimension_semantics=(_ARB,) * n, vmem_limit_bytes=VMEM_LIMIT)


def _pick(n, target, mult=LANE):
    best = None
    for c in range(mult, min(n, target) + 1, mult):
        if n % c == 0:
            best = c
    return best if best is not None else n


_DN = {"nn": (((1,), (0,)), ((), ())), "nt": (((1,), (1,)), ((), ())), "tn": (((0,), (0,)), ((), ()))}


def _dot_raw(a, b, form):
    return lax.dot_general(a.astype(BF16), b.astype(BF16), _DN[form], preferred_element_type=F32)


@functools.partial(jax.custom_vjp, nondiff_argnums=(2,))
def bdot(a, b, form):
    return _dot_raw(a, b, form)


def _bdot_fwd(a, b, form):
    return _dot_raw(a, b, form), (a, b)


def _bdot_bwd(form, res, g):
    a, b = res
    if form == "nn":
        return _dot_raw(g, b, "nt"), _dot_raw(a, g, "tn")
    if form == "nt":
        return _dot_raw(g, b, "nn"), _dot_raw(g, a, "tn")
    return _dot_raw(b, g, "nt"), _dot_raw(a, g, "nn")


bdot.defvjp(_bdot_fwd, _bdot_bwd)


def _hdot(a, b):
    return jnp.dot(a, b, precision=lax.Precision.HIGHEST, preferred_element_type=F32)


@jax.custom_vjp
def cdot_left(c, ct, x):
    return _hdot(c, x)


def _cdl_fwd(c, ct, x):
    return _hdot(c, x), (c, ct)


def _cdl_bwd(res, g):
    c, ct = res
    return jnp.zeros_like(c), jnp.zeros_like(ct), _hdot(ct, g)


cdot_left.defvjp(_cdl_fwd, _cdl_bwd)


@jax.custom_vjp
def cdot_right(x, c, ct):
    return _hdot(x, c)


def _cdr_fwd(x, c, ct):
    return _hdot(x, c), (c, ct)


def _cdr_bwd(res, g):
    c, ct = res
    return _hdot(g, ct), jnp.zeros_like(c), jnp.zeros_like(ct)


cdot_right.defvjp(_cdr_fwd, _cdr_bwd)


def _sigmoid(x):
    return 1.0 / (1.0 + jnp.exp(-x))


def _silu(x):
    return x * _sigmoid(x)


def _softplus(x):
    return jnp.maximum(x, 0.0) + jnp.log(1.0 + jnp.exp(-jnp.abs(x)))


def _log_sigmoid(x):
    return jnp.minimum(x, 0.0) - jnp.log(1.0 + jnp.exp(-jnp.abs(x)))


def _gelu(x):
    c = math.sqrt(2.0 / math.pi)
    return 0.5 * x * (1.0 + jnp.tanh(c * (x + 0.044715 * (x * x * x))))


def _rms(x, g):
    return x * lax.rsqrt(jnp.mean(x * x, axis=-1, keepdims=True) + EPS) * g


def _iota2(shape, axis):
    return lax.broadcasted_iota(jnp.int32, shape, axis)


def matmul(a, b, form, *, name, G=1, out_dtype=F32, add=None):
    isz = lambda t: jnp.dtype(t.dtype).itemsize
    osz = jnp.dtype(out_dtype).itemsize + (isz(add) if add is not None else 0)

    def fits(bm, bn, bk):
        return 2 * (bm * bk * isz(a) + bk * bn * isz(b) + bm * bn * osz) + 4 * bm * bn <= _MATMUL_VMEM

    if form in ("nn", "nt"):
        M = a.shape[0]
        K = a.shape[1] // G
        N = b.shape[2] if form == "nn" else b.shape[1]
        bm, bn, bk = min(M, 1024), _pick(N, 1536), _pick(K, 2048)
        while not fits(bm, bn, bk) and bk % 256 == 0:
            bk //= 2
        nj, nk = N // bn, K // bk
        grid = (G, M // bm, nj, nk)
        a_spec = pl.BlockSpec((bm, bk), lambda g, i, j, k: (i, g * nk + k))
        if form == "nn":
            b_spec = pl.BlockSpec((None, bk, bn), lambda g, i, j, k: (g, k, j))
        else:
            b_spec = pl.BlockSpec((None, bn, bk), lambda g, i, j, k: (g, j, k))
        o_spec = pl.BlockSpec((bm, bn), lambda g, i, j, k: (i, g * nj + j))
        out_shape = jax.ShapeDtypeStruct((M, G * N), out_dtype)
    else:
        T = a.shape[0]
        Ka, Nb = a.shape[1] // G, b.shape[1] // G
        bm, bn, bk = _pick(Ka, 1408), _pick(Nb, 1536), min(T, 2048)
        while not fits(bm, bn, bk) and bk % 512 == 0:
            bk //= 2
        ni, nj, nk = Ka // bm, Nb // bn, T // bk
        grid = (G, ni, nj, nk)
        a_spec = pl.BlockSpec((bk, bm), lambda g, i, j, k: (k, g * ni + i))
        b_spec = pl.BlockSpec((bk, bn), lambda g, i, j, k: (k, g * nj + j))
        o_spec = pl.BlockSpec((None, bm, bn), lambda g, i, j, k: (g, i, j))
        out_shape = jax.ShapeDtypeStruct((G, Ka, Nb), out_dtype)
    has_add = add is not None

    def finish(refs, r):
        if has_add:
            r = r + refs[2][...].astype(F32)
        o_ref = refs[3] if has_add else refs[2]
        o_ref[...] = r.astype(o_ref.dtype)

    def body_one(*refs):
        finish(refs, _dot_raw(refs[0][...], refs[1][...], form))

    def body_acc(*refs):
        acc_ref = refs[-1]
        k = pl.program_id(3)

        @pl.when(k == 0)
        def _():
            acc_ref[...] = jnp.zeros_like(acc_ref)

        acc_ref[...] += _dot_raw(refs[0][...], refs[1][...], form)

        @pl.when(k == nk - 1)
        def _():
            finish(refs, acc_ref[...])

    in_specs = [a_spec, b_spec]
    args = [a, b]
    if has_add:
        in_specs.append(o_spec)
        args.append(add)
    return pl.pallas_call(
        body_one if nk == 1 else body_acc, name=name, grid=grid, in_specs=in_specs, out_specs=o_spec,
        out_shape=out_shape, scratch_shapes=[] if nk == 1 else [pltpu.VMEM((bm, bn), F32)],
        compiler_params=_cparams(4),
    )(*args)


def matmul_nt_norm_bwd(a, w, h, g, d, *, name):
    T, K = a.shape
    bm = min(T, 512)
    bk = _pick(K, 2048)
    nk = K // bk

    def body(a_ref, w_ref, h_ref, g_ref, d_ref, dh_ref, dg_ref, acc_ref):
        i, k = pl.program_id(0), pl.program_id(1)

        @pl.when((i == 0) & (k == 0))
        def _():
            dg_ref[...] = jnp.zeros_like(dg_ref)

        @pl.when(k == 0)
        def _():
            acc_ref[...] = jnp.zeros_like(acc_ref)

        acc_ref[...] += _dot_raw(a_ref[...], w_ref[...], "nt")

        @pl.when(k == nk - 1)
        def _():
            _, vjp = jax.vjp(lambda g_, h_: _rms(h_, g_), g_ref[...], h_ref[...])
            dg, dh = vjp(acc_ref[...])
            dh_ref[...] = dh + d_ref[...]
            dg_ref[...] += dg

    row = pl.BlockSpec((bm, D), lambda i, k: (i, 0))
    one = pl.BlockSpec((1, D), lambda i, k: (0, 0))
    return pl.pallas_call(
        body, name=name, grid=(T // bm, nk),
        in_specs=[pl.BlockSpec((bm, bk), lambda i, k: (i, k)), pl.BlockSpec((D, bk), lambda i, k: (0, k)), row, one, row],
        out_specs=[row, one], out_shape=[jax.ShapeDtypeStruct((T, D), F32), jax.ShapeDtypeStruct((1, D), F32)],
        scratch_shapes=[pltpu.VMEM((bm, D), F32)], compiler_params=_cparams(2),
    )(a, w, h, g, d)


def ffn_up(hn, w_il, *, name):
    T = hn.shape[0]
    bm, hb = min(T, 512), FFN_H // 2

    def body(a_ref, b_ref, act_ref, gu_ref):
        r = _dot_raw(a_ref[...], b_ref[...], "nn")
        act_ref[...] = (_silu(r[:, :hb]) * r[:, hb:]).astype(act_ref.dtype)
        gu_ref[...] = r.astype(gu_ref.dtype)

    return pl.pallas_call(
        body, name=name, grid=(2, T // bm),
        in_specs=[pl.BlockSpec((bm, D), lambda j, i: (i, 0)), pl.BlockSpec((D, 2 * hb), lambda j, i: (0, j))],
        out_specs=[pl.BlockSpec((bm, hb), lambda j, i: (i, j)), pl.BlockSpec((bm, 2 * hb), lambda j, i: (i, j))],
        out_shape=[jax.ShapeDtypeStruct((T, FFN_H), BF16), jax.ShapeDtypeStruct((T, 2 * FFN_H), BF16)],
        compiler_params=_cparams(2),
    )(hn, w_il)


_DACT_CHUNK = 512


def ffn_dact(d, w_down, gu, *, name):
    T = d.shape[0]
    bm, hb = min(T, 512), FFN_H // 2

    def body(d_ref, w_ref, gu_ref, o_ref):
        d_blk = d_ref[...].astype(BF16)
        for lo in range(0, hb, _DACT_CHUNK):
            hi = min(lo + _DACT_CHUNK, hb)
            da = _dot_raw(d_blk, w_ref[lo:hi, :], "nt")
            g, u = gu_ref[:, lo:hi].astype(F32), gu_ref[:, hb + lo:hb + hi].astype(F32)
            sg = _sigmoid(g)
            o_ref[:, lo:hi] = (da * u * (sg * (1.0 + g * (1.0 - sg)))).astype(o_ref.dtype)
            o_ref[:, hb + lo:hb + hi] = (da * (g * sg)).astype(o_ref.dtype)

    return pl.pallas_call(
        body, name=name, grid=(2, T // bm),
        in_specs=[pl.BlockSpec((bm, D), lambda j, i: (i, 0)), pl.BlockSpec((hb, D), lambda j, i: (j, 0)),
                  pl.BlockSpec((bm, 2 * hb), lambda j, i: (i, j))],
        out_specs=pl.BlockSpec((bm, 2 * hb), lambda j, i: (i, j)),
        out_shape=jax.ShapeDtypeStruct((T, 2 * FFN_H), BF16), compiler_params=_cparams(2),
    )(d, w_down, gu)


def _row_entry(e):
    return e if isinstance(e, tuple) else (e, e.shape[1], 0)


def _row_spec(bt, e):
    _, width, idx = e
    return pl.BlockSpec((bt, width), lambda i: (i, idx))


def _full_spec(p):
    return pl.BlockSpec(p.shape, lambda i: (0,) * p.ndim)


def rowwise(f, params, rows, outs, *, bt, name):
    rows = [_row_entry(e) for e in rows]
    T = rows[0][0].shape[0]
    bt = min(bt, T)
    np_, nr = len(params), len(rows)

    def body(*refs):
        p = tuple(r[...].astype(F32) for r in refs[:np_])
        rw = tuple(r[...].astype(F32) for r in refs[np_:np_ + nr])
        res = f(p, rw)
        for o_ref, o in zip(refs[np_ + nr:], res):
            o_ref[...] = o.astype(o_ref.dtype)

    res = pl.pallas_call(
        body, name=name, grid=(T // bt,),
        in_specs=[_full_spec(p) for p in params] + [_row_spec(bt, e) for e in rows],
        out_specs=[pl.BlockSpec((bt, w), lambda i: (i, 0)) for w, _ in outs],
        out_shape=[jax.ShapeDtypeStruct((T, w), dt) for w, dt in outs],
        compiler_params=_cparams(1),
    )(*params, *[e[0] for e in rows])
    return list(res)


def rowwise_vjp(f, params, rows, cts, drow_dtypes, *, bt, name):
    rows = [_row_entry(e) for e in rows]
    cts = [_row_entry(e) for e in cts]
    T = rows[0][0].shape[0]
    bt = min(bt, T)
    np_, nr, nc = len(params), len(rows), len(cts)
    want = [i for i, dt in enumerate(drow_dtypes) if dt is not None]

    def body(*refs):
        p = tuple(r[...].astype(F32) for r in refs[:np_])
        rw = tuple(r[...].astype(F32) for r in refs[np_:np_ + nr])
        ct = tuple(r[...].astype(F32) for r in refs[np_ + nr:np_ + nr + nc])
        outs = refs[np_ + nr + nc:]
        _, vjp = jax.vjp(f, p, rw)
        dp, dr = vjp(ct)

        @pl.when(pl.program_id(0) == 0)
        def _():
            for o in outs[:np_]:
                o[...] = jnp.zeros_like(o)

        for o, d in zip(outs[:np_], dp):
            o[...] += d
        for o, i in zip(outs[np_:], want):
            o[...] = dr[i].astype(o.dtype)

    res = pl.pallas_call(
        body, name=name, grid=(T // bt,),
        in_specs=[_full_spec(p) for p in params] + [_row_spec(bt, e) for e in rows] + [_row_spec(bt, e) for e in cts],
        out_specs=[_full_spec(p) for p in params] + [pl.BlockSpec((bt, rows[i][1]), lambda i_: (i_, 0)) for i in want],
        out_shape=[jax.ShapeDtypeStruct(p.shape, F32) for p in params]
        + [jax.ShapeDtypeStruct((T, rows[i][1]), drow_dtypes[i]) for i in want],
        compiler_params=_cparams(1),
    )(*params, *[e[0] for e in rows], *[e[0] for e in cts])
    res = list(res)
    return res[:np_], res[np_:]


def f_rmsnorm(p, r):
    return (_rms(r[0], p[0]),)


def f_rmsnorm_res(p, r):
    return (_rms(r[0], p[0]), r[0])


def f_swiglu(p, r):
    gu = r[0]
    return (_silu(gu[:, :FFN_H]) * gu[:, FFN_H:],)


def f_gla_gate_in(p, r):
    w_a2, b_a = p
    z = bdot(r[0], w_a2, "nn") + b_a
    return (_log_sigmoid(z) / GLA_TAU,)


def f_gla_gate_in_fwd(p, r):
    w_a2, b_a = p
    z = _dot_raw(r[0], w_a2, "nn") + b_a
    return (_log_sigmoid(z) / GLA_TAU,)


def f_gla_out(p, r):
    (ng,) = p
    o, rr = r
    parts = []
    for h in range(GLA_H):
        sl = slice(h * GLA_DV, (h + 1) * GLA_DV)
        parts.append(_rms(o[:, sl], ng[:, sl]) * _silu(rr[:, sl]))
    return (jnp.concatenate(parts, axis=1),)


def f_ssd_out(p, r):
    (ng,) = p
    y, z = r
    t = y * _silu(z)
    gsz = SSD_DI // SSD_G
    parts = []
    for g in range(SSD_G):
        sl = slice(g * gsz, (g + 1) * gsz)
        parts.append(_rms(t[:, sl], ng[:, sl]))
    return (jnp.concatenate(parts, axis=1),)


def f_s5_act(p, r):
    (dsk,) = p
    ycp, u = r
    return (_gelu(ycp + dsk * u),)


def f_glu_res(p, r):
    vg, h = r
    return (vg[:, :D] * _sigmoid(vg[:, D:]) + h,)


def f_glu(p, r):
    vg = r[0]
    return (vg[:, :D] * _sigmoid(vg[:, D:]),)


def loss_head(h, g, target, *, name):
    T = h.shape[0]
    bt = min(T, 256)

    def lossf(g_, h_, t_):
        e = _rms(h_, g_) - t_
        return (0.5 / D) * jnp.sum(e * e)

    def body(g_ref, h_ref, t_ref, loss_ref, dg_ref, dh_ref):
        @pl.when(pl.program_id(0) == 0)
        def _():
            loss_ref[...] = jnp.zeros_like(loss_ref)
            dg_ref[...] = jnp.zeros_like(dg_ref)

        val, vjp = jax.vjp(lossf, g_ref[...], h_ref[...], t_ref[...])
        dg, dh, _ = vjp(jnp.ones((), F32))
        loss_ref[...] += jnp.full(loss_ref.shape, val, F32)
        dg_ref[...] += dg
        dh_ref[...] = dh

    row = pl.BlockSpec((bt, D), lambda i: (i, 0))
    one = pl.BlockSpec((1, D), lambda i: (0, 0))
    loss, dg, dh = pl.pallas_call(
        body, name=name, grid=(T // bt,), in_specs=[one, row, row],
        out_specs=[pl.BlockSpec((1, LANE), lambda i: (0, 0)), one, row],
        out_shape=[jax.ShapeDtypeStruct((1, LANE), F32), jax.ShapeDtypeStruct((1, D), F32),
                   jax.ShapeDtypeStruct((T, D), F32)],
        compiler_params=_cparams(1),
    )(g, h, target)
    return loss[0, 0], dg, dh


def _gla_consts():
    r, c = _iota2((CH, CH), 0), _iota2((CH, CH), 1)
    return (r >= c).astype(F32), (r <= c).astype(F32), r >= c


def _gla_chunk(q, k, v, la, st, consts, dot, cdl):
    L, Lt, tril = consts
    lc = cdl(L, Lt, la)
    lend = lc[CH - 1:CH, :]
    e, ei = jnp.exp(lc), jnp.exp(-lc)
    qs = q * (GLA_DK ** -0.5)
    qf, kf, qb, kb = qs * e, k * ei, qs * ei, k * e
    sc = jnp.where(tril, dot(qf, kf, "nt"), dot(qb, kb, "nt"))
    o = dot(sc, v, "nn") + dot(qf, st, "nt")
    kd = k * jnp.exp(lend - lc)
    st_new = st * jnp.exp(lend) + dot(v, kd, "tn")
    return o, st_new


def _gla_block(q, k, v, la, st, nc, dot, cdl):
    consts = _gla_consts()
    outs = []
    for c in range(nc):
        sl = slice(c * CH, (c + 1) * CH)
        o, st = _gla_chunk(q[sl], k[sl], v[sl], la[sl], st, consts, dot, cdl)
        outs.append(o)
    return jnp.concatenate(outs, axis=0), st


_GLA_HP = 2


def _gla_specs(rows, rev, nb):
    t = (lambda j: nb - 1 - j) if rev else (lambda j: j)
    hp, ng = _GLA_HP, GLA_H // _GLA_HP
    q = pl.BlockSpec((rows, hp * GLA_DK), lambda h, j: (t(j), h))
    k = pl.BlockSpec((rows, hp * GLA_DK), lambda h, j: (t(j), ng + h))
    v = pl.BlockSpec((rows, hp * GLA_DV), lambda h, j: (t(j), ng + h))
    la = pl.BlockSpec((rows, hp * GLA_DK), lambda h, j: (t(j), h))
    ss = pl.BlockSpec((None, hp, GLA_DV, GLA_DK), lambda h, j: (t(j), h, 0, 0))
    o = pl.BlockSpec((rows, hp * GLA_DV), lambda h, j: (t(j), h))
    r = pl.BlockSpec((rows, hp * GLA_DV), lambda h, j: (t(j), 2 * ng + h))
    g = pl.BlockSpec((1, hp * GLA_DV), lambda h, j: (0, h))
    return q, k, v, la, ss, o, r, g


def _gla_heads(q, k, v, la, r, ng, sts, nc, dot, cdl):
    outs, new = [], []
    for i in range(_GLA_HP):
        kk, vv = slice(i * GLA_DK, (i + 1) * GLA_DK), slice(i * GLA_DV, (i + 1) * GLA_DV)
        o, st = _gla_block(q[:, kk], k[:, kk], v[:, vv], la[:, kk], sts[i], nc, dot, cdl)
        outs.append(_rms(o, ng[:, vv]) * _silu(r[:, vv]))
        new.append(st)
    return jnp.concatenate(outs, axis=1), tuple(new)


def gla_scan_fwd(proj, la, ng, *, nc, name):
    T = proj.shape[0]
    rows = min(T, nc * CH)
    nc = rows // CH
    nb = T // rows
    q_s, k_s, v_s, la_s, ss_s, o_s, r_s, g_s = _gla_specs(rows, False, nb)

    def body(q_ref, k_ref, v_ref, la_ref, r_ref, g_ref, o_ref, ss_ref, st_ref):
        @pl.when(pl.program_id(1) == 0)
        def _():
            st_ref[...] = jnp.zeros_like(st_ref)

        ss_ref[...] = st_ref[...]
        sts = tuple(st_ref[i] for i in range(_GLA_HP))
        o, sts = _gla_heads(q_ref[...], k_ref[...], v_ref[...], la_ref[...], r_ref[...], g_ref[...], sts, nc,
                            _dot_raw, lambda c, ct, x: _hdot(c, x))
        o_ref[...] = o.astype(o_ref.dtype)
        for i in range(_GLA_HP):
            st_ref[i] = sts[i]

    return pl.pallas_call(
        body, name=name, grid=(GLA_H // _GLA_HP, nb), in_specs=[q_s, k_s, v_s, la_s, r_s, g_s], out_specs=[o_s, ss_s],
        out_shape=[jax.ShapeDtypeStruct((T, GLA_VD), BF16), jax.ShapeDtypeStruct((nb, GLA_H, GLA_DV, GLA_DK), F32)],
        scratch_shapes=[pltpu.VMEM((_GLA_HP, GLA_DV, GLA_DK), F32)], compiler_params=_cparams(2),
    )(proj, proj, proj, la, proj, ng)


def gla_scan_bwd(proj, la, ng, ss, do, *, nc, name):
    T = proj.shape[0]
    rows = min(T, nc * CH)
    nc = rows // CH
    nb = T // rows
    q_s, k_s, v_s, la_s, ss_s, o_s, r_s, g_s = _gla_specs(rows, True, nb)
    t = lambda j: nb - 1 - j
    dqk_s = pl.BlockSpec((rows, _GLA_HP * GLA_DK), lambda h, j: (t(j), h))

    def body(q_ref, k_ref, v_ref, la_ref, r_ref, g_ref, ss_ref, do_ref,
             dq_ref, dk_ref, dv_ref, dla_ref, dr_ref, dg_ref, dst_ref):
        @pl.when(pl.program_id(1) == 0)
        def _():
            dst_ref[...] = jnp.zeros_like(dst_ref)
            dg_ref[...] = jnp.zeros_like(dg_ref)

        fn = lambda q, k, v, la_, r, g, *sts: _gla_heads(q, k, v, la_, r, g, sts, nc, bdot, cdot_left)
        _, vjp = jax.vjp(fn, q_ref[...], k_ref[...], v_ref[...], la_ref[...], r_ref[...], g_ref[...],
                         *[ss_ref[i] for i in range(_GLA_HP)])
        dq, dk, dv, dla, dr, dg, *dsts = vjp((do_ref[...], tuple(dst_ref[i] for i in range(_GLA_HP))))
        dq_ref[...] = dq.astype(dq_ref.dtype)
        dk_ref[...] = dk.astype(dk_ref.dtype)
        dv_ref[...] = dv.astype(dv_ref.dtype)
        dla_ref[...] = dla
        dr_ref[...] = dr.astype(dr_ref.dtype)
        dg_ref[...] += dg
        for i in range(_GLA_HP):
            dst_ref[i] = dsts[i]

    return pl.pallas_call(
        body, name=name, grid=(GLA_H // _GLA_HP, nb), in_specs=[q_s, k_s, v_s, la_s, r_s, g_s, ss_s, o_s],
        out_specs=[dqk_s, dqk_s, o_s, dqk_s, o_s, g_s],
        out_shape=[jax.ShapeDtypeStruct((T, GLA_QK), BF16), jax.ShapeDtypeStruct((T, GLA_QK), BF16),
                   jax.ShapeDtypeStruct((T, GLA_VD), BF16), jax.ShapeDtypeStruct((T, GLA_QK), F32),
                   jax.ShapeDtypeStruct((T, GLA_VD), BF16), jax.ShapeDtypeStruct((1, GLA_VD), F32)],
        scratch_shapes=[pltpu.VMEM((_GLA_HP, GLA_DV, GLA_DK), F32)], compiler_params=_cparams(2),
    )(proj, proj, proj, la, proj, ng, ss, do)


_CONV_W = 512
_CONV_OFF = SSD_DI // _CONV_W


def _conv_pre(x, prev8, w_ref, b_ref):
    bt = x.shape[0]
    ext = jnp.concatenate([prev8, x], axis=0)
    shifted = []
    for j in range(SSD_K):
        s = SSD_K - 1 - j
        shifted.append(x if s == 0 else pltpu.roll(ext, s, 0)[8:8 + bt])
    pre = b_ref[...] + sum(w_ref[j:j + 1, :] * shifted[j] for j in range(SSD_K))
    return pre, shifted


def ssd_conv_fwd(proj, w, b, *, name):
    T = proj.shape[0]
    bt = min(T, 512)
    nb = T // bt

    def body(x_ref, w_ref, b_ref, o_ref, carry_ref):
        @pl.when(pl.program_id(1) == 0)
        def _():
            carry_ref[...] = jnp.zeros_like(carry_ref)

        x = x_ref[...]
        pre, _ = _conv_pre(x, carry_ref[...], w_ref, b_ref)
        o_ref[...] = _silu(pre)
        carry_ref[...] = x[bt - 8:, :]

    return pl.pallas_call(
        body, name=name, grid=(SSD_CONV // _CONV_W, nb),
        in_specs=[pl.BlockSpec((bt, _CONV_W), lambda c, t: (t, _CONV_OFF + c)),
                  pl.BlockSpec((SSD_K, _CONV_W), lambda c, t: (0, c)),
                  pl.BlockSpec((1, _CONV_W), lambda c, t: (0, c))],
        out_specs=pl.BlockSpec((bt, _CONV_W), lambda c, t: (t, c)),
        out_shape=jax.ShapeDtypeStruct((T, SSD_CONV), F32),
        scratch_shapes=[pltpu.VMEM((8, _CONV_W), F32)], compiler_params=_cparams(2),
    )(proj, w, b)


def ssd_conv_bwd(proj, w, b, dout, *, name):
    T = proj.shape[0]
    bt = min(T, 512)
    nb = T // bt
    r8 = bt // 8

    def body(x_ref, xp_ref, w_ref, b_ref, do_ref, dx_ref, dw_ref, db_ref, carry_ref):
        t = pl.program_id(1)

        @pl.when(t == 0)
        def _():
            carry_ref[...] = jnp.zeros_like(carry_ref)
            dw_ref[...] = jnp.zeros_like(dw_ref)
            db_ref[...] = jnp.zeros_like(db_ref)

        x = x_ref[...]
        prev8 = jnp.where(t == nb - 1, 0.0, xp_ref[...])
        pre, shifted = _conv_pre(x, prev8, w_ref, b_ref)
        sg = _sigmoid(pre)
        dpre = do_ref[...] * (sg * (1.0 + pre * (1.0 - sg)))
        ext = jnp.concatenate([dpre, carry_ref[...]], axis=0)
        dx = w_ref[SSD_K - 1:SSD_K, :] * dpre
        for j in range(SSD_K - 1):
            s = SSD_K - 1 - j
            dx = dx + w_ref[j:j + 1, :] * pltpu.roll(ext, bt + 8 - s, 0)[:bt]
        dx_ref[...] = dx.astype(dx_ref.dtype)
        dw_ref[...] += jnp.concatenate([jnp.sum(dpre * shifted[j], axis=0, keepdims=True) for j in range(SSD_K)], axis=0)
        db_ref[...] += jnp.sum(dpre, axis=0, keepdims=True)
        carry_ref[...] = dpre[:8, :]

    rt = lambda t: nb - 1 - t
    return pl.pallas_call(
        body, name=name, grid=(SSD_CONV // _CONV_W, nb),
        in_specs=[pl.BlockSpec((bt, _CONV_W), lambda c, t: (rt(t), _CONV_OFF + c)),
                  pl.BlockSpec((8, _CONV_W), lambda c, t: (jnp.maximum(rt(t) * r8 - 1, 0), _CONV_OFF + c)),
                  pl.BlockSpec((SSD_K, _CONV_W), lambda c, t: (0, c)),
                  pl.BlockSpec((1, _CONV_W), lambda c, t: (0, c)),
                  pl.BlockSpec((bt, _CONV_W), lambda c, t: (rt(t), c))],
        out_specs=[pl.BlockSpec((bt, _CONV_W), lambda c, t: (rt(t), c)),
                   pl.BlockSpec((SSD_K, _CONV_W), lambda c, t: (0, c)),
                   pl.BlockSpec((1, _CONV_W), lambda c, t: (0, c))],
        out_shape=[jax.ShapeDtypeStruct((T, SSD_CONV), BF16), jax.ShapeDtypeStruct((SSD_K, SSD_CONV), F32),
                   jax.ShapeDtypeStruct((1, SSD_CONV), F32)],
        scratch_shapes=[pltpu.VMEM((8, _CONV_W), F32)], compiler_params=_cparams(2),
    )(proj, proj, w, b, dout)


_SSD_U = 2 * CH


def _ssd_unit(xs, bm, cm, dtraw, dtb, alog, dsk, hp, g, dot, cdl, cdr):
    U, P2 = _SSD_U, 2 * SSD_HD
    r, c = _iota2((U, U), 0), _iota2((U, U), 1)
    same = (r // CH) == (c // CH)
    Lb = (same & (r >= c)).astype(F32)
    Ub = (same & (r <= c)).astype(F32)
    lane = _iota2((1, U), 1)
    lo_lane = _iota2((1, P2), 1) < SSD_HD
    lo_sub = _iota2((P2, 1), 0) < SSD_HD
    diag2 = (_iota2((CH, P2), 0) == (_iota2((CH, P2), 1) % CH)).astype(F32)

    dt = _softplus(dtraw + dtb)
    da = dt * (-jnp.exp(alog))
    cum = cdl(Lb, Ub, da)
    ys = []
    new_hp = []
    for pr in range(2):
        xs_p = xs[:, pr * P2:(pr + 1) * P2]
        cols, dts, dks = [], [], []
        for jj in range(2):
            oh_l = (lane == g * (SSD_H // SSD_G) + 2 * pr + jj).astype(F32)
            cols.append(jnp.sum(cum * oh_l, axis=1, keepdims=True))
            dts.append(jnp.sum(dt * oh_l, axis=1, keepdims=True))
            dks.append(jnp.sum(dsk * oh_l, axis=1, keepdims=True))
        dsk_p = jnp.where(lo_lane, dks[0], dks[1])
        h = hp[pr]
        yc = []
        for ci in range(2):
            sl = slice(ci * CH, (ci + 1) * CH)
            xs_c, bm_c, cm_c = xs_p[sl], bm[sl], cm[sl]
            col = jnp.where(lo_lane, cols[0][sl], cols[1][sl])
            dtc = jnp.where(lo_lane, dts[0][sl], dts[1][sl])
            row = jnp.sum(diag2 * col, axis=0, keepdims=True)
            dtrow = jnp.sum(diag2 * dtc, axis=0, keepdims=True)
            cb = dot(cm_c, jnp.concatenate([bm_c, bm_c], axis=0), "nt")
            mix = cb * jnp.exp(-jnp.abs(col - row)) * dtrow
            xbd = jnp.concatenate([jnp.where(lo_lane, xs_c, 0.0), jnp.where(lo_lane, 0.0, xs_c)], axis=0)
            y_intra = dot(mix, xbd, "nn")
            ce = jnp.where(lo_lane, cols[0][ci * CH + CH - 1:ci * CH + CH, :], cols[1][ci * CH + CH - 1:ci * CH + CH, :])
            y_inter = dot(cm_c, h, "nt") * jnp.exp(col)
            xw = xs_c * (dtc * jnp.exp(ce - col))
            ce_s = [cols[jj][ci * CH + CH - 1:ci * CH + CH, :] for jj in range(2)]
            a_p = jnp.where(lo_sub, jnp.exp(ce_s[0]), jnp.exp(ce_s[1]))
            h = a_p * h + dot(xw, bm_c, "tn")
            yc.append(y_intra + y_inter + dsk_p * xs_c)
        ys.append(jnp.concatenate(yc, axis=0))
        new_hp.append(h)
    return jnp.concatenate(ys, axis=1), tuple(new_hp)


def _ssd_block(xs, bm, cm, dtraw, z, dtb, alog, dsk, ng, hp, g, nu, dot, cdl, cdr):
    outs = []
    for u in range(nu):
        sl = slice(u * _SSD_U, (u + 1) * _SSD_U)
        y, hp = _ssd_unit(xs[sl], bm[sl], cm[sl], dtraw[sl], dtb, alog, dsk, hp, g, dot, cdl, cdr)
        outs.append(y)
    return _rms(jnp.concatenate(outs, axis=0) * _silu(z), ng), hp


def _ssd_specs(rows, rev, nb):
    t = (lambda j: nb - 1 - j) if rev else (lambda j: j)
    gw = SSD_DI // SSD_G
    xs = pl.BlockSpec((rows, gw), lambda j, g: (t(j), g))
    bm = pl.BlockSpec((rows, SSD_N), lambda j, g: (t(j), SSD_DI // SSD_N + g))
    cm = pl.BlockSpec((rows, SSD_N), lambda j, g: (t(j), (SSD_DI + SSD_GN) // SSD_N + g))
    dtr = pl.BlockSpec((rows, LANE), lambda j, g: (t(j), (SSD_DI + SSD_CONV) // LANE))
    par = pl.BlockSpec((1, LANE), lambda j, g: (0, 0))
    hs = pl.BlockSpec((None, None, 2, 2 * SSD_HD, SSD_N), lambda j, g: (t(j), g, 0, 0, 0))
    y = pl.BlockSpec((rows, gw), lambda j, g: (t(j), g))
    ng = pl.BlockSpec((1, gw), lambda j, g: (0, g))
    return xs, bm, cm, dtr, par, hs, y, ng


def ssd_scan_fwd(xbc, proj, dtb, alog, dsk, ng, *, nu, name):
    T = xbc.shape[0]
    rows = min(T, nu * _SSD_U)
    nu = rows // _SSD_U
    nb = T // rows
    xs_s, bm_s, cm_s, dt_s, par_s, hs_s, y_s, ng_s = _ssd_specs(rows, False, nb)

    def body(xs_ref, bm_ref, cm_ref, dt_ref, z_ref, dtb_ref, al_ref, dsk_ref, ng_ref, y_ref, hs_ref, h_ref):
        g = pl.program_id(1)

        @pl.when(pl.program_id(0) == 0)
        def _():
            h_ref[g] = jnp.zeros(h_ref.shape[1:], F32)

        hs_ref[...] = h_ref[g]
        hp = (h_ref[g, 0], h_ref[g, 1])
        y, hp = _ssd_block(xs_ref[...], bm_ref[...], cm_ref[...], dt_ref[...], z_ref[...], dtb_ref[...], al_ref[...],
                           dsk_ref[...], ng_ref[...], hp, g, nu, _dot_raw, lambda c, ct, x: _hdot(c, x), lambda x, c, ct: _hdot(x, c))
        y_ref[...] = y.astype(y_ref.dtype)
        h_ref[g, 0] = hp[0]
        h_ref[g, 1] = hp[1]

    return pl.pallas_call(
        body, name=name, grid=(nb, SSD_G), in_specs=[xs_s, bm_s, cm_s, dt_s, y_s, par_s, par_s, par_s, ng_s],
        out_specs=[y_s, hs_s],
        out_shape=[jax.ShapeDtypeStruct((T, SSD_DI), BF16), jax.ShapeDtypeStruct((nb, SSD_G, 2, 2 * SSD_HD, SSD_N), F32)],
        scratch_shapes=[pltpu.VMEM((SSD_G, 2, 2 * SSD_HD, SSD_N), F32)], compiler_params=_cparams(2),
    )(xbc, xbc, xbc, proj, proj, dtb, alog, dsk, ng)


def ssd_scan_bwd(xbc, proj, dtb, alog, dsk, ng, hs, dy, *, nu, name):
    T = xbc.shape[0]
    rows = min(T, nu * _SSD_U)
    nu = rows // _SSD_U
    nb = T // rows
    xs_s, bm_s, cm_s, dt_s, par_s, hs_s, y_s, ng_s = _ssd_specs(rows, True, nb)
    gw = SSD_DI // SSD_G
    dng_s = pl.BlockSpec((1, SSD_DI), lambda j, g: (0, 0))
    t = lambda j: nb - 1 - j
    n_s = pl.BlockSpec((rows, SSD_N), lambda j, g: (t(j), g))
    ddt_s = pl.BlockSpec((rows, LANE), lambda j, g: (t(j), 0))

    def body(xs_ref, bm_ref, cm_ref, dt_ref, z_ref, dtb_ref, al_ref, dsk_ref, ng_ref, hs_ref, dy_ref,
             dxs_ref, dbm_ref, dcm_ref, ddt_ref, ddtb_ref, dal_ref, ddsk_ref, dz_ref, dng_ref, dh_ref):
        j, g = pl.program_id(0), pl.program_id(1)

        @pl.when(j == 0)
        def _():
            dh_ref[g] = jnp.zeros(dh_ref.shape[1:], F32)

        @pl.when((j == 0) & (g == 0))
        def _():
            ddtb_ref[...] = jnp.zeros_like(ddtb_ref)
            dal_ref[...] = jnp.zeros_like(dal_ref)
            ddsk_ref[...] = jnp.zeros_like(ddsk_ref)
            dng_ref[...] = jnp.zeros_like(dng_ref)

        @pl.when(g == 0)
        def _():
            ddt_ref[...] = jnp.zeros_like(ddt_ref)

        fn = lambda xs, bm, cm, dtr, z, dtb_, al, dsk_, ng_, h0, h1: _ssd_block(
            xs, bm, cm, dtr, z, dtb_, al, dsk_, ng_, (h0, h1), g, nu, bdot, cdot_left, cdot_right)
        _, vjp = jax.vjp(fn, xs_ref[...], bm_ref[...], cm_ref[...], dt_ref[...], z_ref[...], dtb_ref[...], al_ref[...],
                         dsk_ref[...], ng_ref[...], hs_ref[0], hs_ref[1])
        dxs, dbm, dcm, ddt, dz, ddtb, dal, ddsk, dng, dh0, dh1 = vjp((dy_ref[...], (dh_ref[g, 0], dh_ref[g, 1])))
        dz_ref[...] = dz.astype(dz_ref.dtype)
        lanes = pl.ds(pl.multiple_of(g * gw, gw), gw)
        dng_ref[:, lanes] = dng_ref[:, lanes] + dng
        dxs_ref[...] = dxs
        dbm_ref[...] = dbm
        dcm_ref[...] = dcm
        ddt_ref[...] += ddt
        ddtb_ref[...] += ddtb
        dal_ref[...] += dal
        ddsk_ref[...] += ddsk
        dh_ref[g, 0] = dh0
        dh_ref[g, 1] = dh1

    return pl.pallas_call(
        body, name=name, grid=(nb, SSD_G), in_specs=[xs_s, bm_s, cm_s, dt_s, y_s, par_s, par_s, par_s, ng_s, hs_s, y_s],
        out_specs=[y_s, n_s, n_s, ddt_s, par_s, par_s, par_s, y_s, dng_s],
        out_shape=[jax.ShapeDtypeStruct((T, SSD_DI), F32), jax.ShapeDtypeStruct((T, SSD_GN), F32),
                   jax.ShapeDtypeStruct((T, SSD_GN), F32), jax.ShapeDtypeStruct((T, LANE), F32),
                   jax.ShapeDtypeStruct((1, LANE), F32), jax.ShapeDtypeStruct((1, LANE), F32),
                   jax.ShapeDtypeStruct((1, LANE), F32), jax.ShapeDtypeStruct((T, SSD_DI), BF16),
                   jax.ShapeDtypeStruct((1, SSD_DI), F32)],
        scratch_shapes=[pltpu.VMEM((SSD_G, 2, 2 * SSD_HD, SSD_N), F32)], compiler_params=_cparams(2),
    )(xbc, xbc, xbc, proj, proj, dtb, alog, dsk, ng, hs, dy)


def _s5_param_f(log_dt, a_re, a_im, bre_t, bim_t, cim, cdl):
    n = S5_NG * S5_GS
    r, c = _iota2((n, S5_NG), 0), _iota2((n, S5_NG), 1)
    E = ((r // S5_GS) == c).astype(F32)
    rt, ct = _iota2((S5_NG, n), 0), _iota2((S5_NG, n), 1)
    Et = ((ct // S5_GS) == rt).astype(F32)
    step = jnp.exp(log_dt)
    mag = jnp.exp(step * a_re)
    abr = mag * jnp.cos(step * a_im)
    abi = mag * jnp.sin(step * a_im)
    den = a_re * a_re + a_im * a_im
    nr, ni = abr - 1.0, abi
    fr = (nr * a_re + ni * a_im) / den
    fi = (ni * a_re - nr * a_im) / den
    Fr, Fi = cdl(E, Et, fr), cdl(E, Et, fi)
    bbr = Fr * bre_t - Fi * bim_t
    bbi = Fr * bim_t + Fi * bre_t
    return abr, abi, bbr, bbi, -cim


def _whole(a):
    return pl.BlockSpec(a.shape, lambda: (0,) * a.ndim)


def s5_param_fwd(args, *, name):
    def body(*refs):
        res = _s5_param_f(*[r[...] for r in refs[:6]], lambda c, ct, x: _hdot(c, x))
        for o, v in zip(refs[6:], res):
            o[...] = v

    shapes = [(S5_NG, S5_P), (S5_NG, S5_P)] + [(S5_NG * S5_GS, S5_P)] * 3
    return pl.pallas_call(
        body, name=name, in_specs=[_whole(a) for a in args], out_specs=[pl.BlockSpec(s, lambda: (0, 0)) for s in shapes],
        out_shape=[jax.ShapeDtypeStruct(s, F32) for s in shapes],
        compiler_params=pltpu.CompilerParams(vmem_limit_bytes=VMEM_LIMIT),
    )(*args)


def s5_param_bwd(args, cts, *, name):
    def body(*refs):
        fn = lambda *a: _s5_param_f(*a, cdot_left)
        _, vjp = jax.vjp(fn, *[r[...] for r in refs[:6]])
        grads = vjp(tuple(r[...] for r in refs[6:11]))
        for o, v in zip(refs[11:], grads):
            o[...] = v

    return pl.pallas_call(
        body, name=name, in_specs=[_whole(a) for a in list(args) + list(cts)],
        out_specs=[_whole(a) for a in args], out_shape=[jax.ShapeDtypeStruct(a.shape, F32) for a in args],
        compiler_params=pltpu.CompilerParams(vmem_limit_bytes=VMEM_LIMIT),
    )(*args, *cts)


_S5_W = S5_BLK * S5_P


def _cmul_add(xr, xi, pr, pi, sr, si):
    return xr + (pr * sr - pi * si), xi + (pr * si + pi * sr)


def _s5_powers(ar, ai):
    pw = [(ar, ai)]
    for _ in range(7):
        qr, qi = pw[-1]
        pw.append((qr * ar - qi * ai, qr * ai + qi * ar))
    return pw


def s5_scan_fwd(u, wb, a_re, a_im, wc, *, name):
    T = u.shape[0]
    bt = min(T, 256)
    nb = T // bt

    def body(u_ref, wb_ref, ar_ref, ai_ref, wc_ref, x_ref, y_ref, bu_ref, carry_ref):
        @pl.when(pl.program_id(1) == 0)
        def _():
            carry_ref[...] = jnp.zeros_like(carry_ref)

        bu_ref[...] = _dot_raw(u_ref[...], wb_ref[...], "nn")
        ar, ai = ar_ref[...], ai_ref[...]
        pw = _s5_powers(ar, ai)
        pwr = jnp.concatenate([p[0] for p in pw], axis=0)
        pwi = jnp.concatenate([p[1] for p in pw], axis=0)
        rin = _iota2((8, _S5_W), 0)
        cr, ci = carry_ref[0:1, :], carry_ref[1:2, :]
        for t in range(bt // 8):
            sl = slice(8 * t, 8 * t + 8)
            xr, xi = bu_ref[sl, :_S5_W], bu_ref[sl, _S5_W:]
            for s in (1, 2, 4):
                m = rin >= s
                sr = jnp.where(m, pltpu.roll(xr, s, 0), 0.0)
                si = jnp.where(m, pltpu.roll(xi, s, 0), 0.0)
                xr, xi = _cmul_add(xr, xi, *pw[s - 1], sr, si)
            xr, xi = _cmul_add(xr, xi, pwr, pwi, cr, ci)
            x_ref[sl, :_S5_W] = xr
            x_ref[sl, _S5_W:] = xi
            cr, ci = xr[7:8, :], xi[7:8, :]
        carry_ref[0:1, :] = cr
        carry_ref[1:2, :] = ci
        y_ref[...] = _dot_raw(x_ref[...], wc_ref[...], "nn")

    nblk = S5_NG // S5_BLK
    blk = pl.BlockSpec((bt, 2 * _S5_W), lambda g, t: (t, g))
    col = pl.BlockSpec((bt, LANE), lambda g, t: (t, g))
    a_s = pl.BlockSpec((None, 1, _S5_W), lambda g, t: (g, 0, 0))
    wb_s = pl.BlockSpec((None, LANE, 2 * _S5_W), lambda g, t: (g, 0, 0))
    wc_s = pl.BlockSpec((None, 2 * _S5_W, LANE), lambda g, t: (g, 0, 0))
    return pl.pallas_call(
        body, name=name, grid=(nblk, nb), in_specs=[col, wb_s, a_s, a_s, wc_s], out_specs=[blk, col],
        out_shape=[jax.ShapeDtypeStruct((T, nblk * 2 * _S5_W), F32), jax.ShapeDtypeStruct((T, nblk * LANE), F32)],
        scratch_shapes=[pltpu.VMEM((bt, 2 * _S5_W), F32), pltpu.VMEM((8, _S5_W), F32)],
        compiler_params=_cparams(2),
    )(u, wb, a_re, a_im, wc)


def s5_scan_bwd(dy, x, u, wb, a_re, a_im, wc, *, name):
    T = dy.shape[0]
    bt = min(T, 256)
    nb = T // bt

    def body(dy_ref, x_ref, u_ref, wb_ref, ar_ref, ai_ref, wc_ref, du_ref, dwb_ref, dwc_ref, dar_ref, dai_ref,
             g_ref, lam_ref, carry_ref):
        @pl.when(pl.program_id(1) == 0)
        def _():
            carry_ref[...] = jnp.zeros_like(carry_ref)
            dar_ref[...] = jnp.zeros_like(dar_ref)
            dai_ref[...] = jnp.zeros_like(dai_ref)
            dwb_ref[...] = jnp.zeros_like(dwb_ref)
            dwc_ref[...] = jnp.zeros_like(dwc_ref)

        g_ref[...] = _dot_raw(dy_ref[...], wc_ref[...], "nt")
        pw = _s5_powers(ar_ref[...], -ai_ref[...])
        pwr = jnp.concatenate([p[0] for p in reversed(pw)], axis=0)
        pwi = jnp.concatenate([p[1] for p in reversed(pw)], axis=0)
        rin = _iota2((8, _S5_W), 0)
        cr, ci = carry_ref[0:1, :], carry_ref[1:2, :]
        acc_r = jnp.zeros((8, _S5_W), F32)
        acc_i = jnp.zeros((8, _S5_W), F32)
        for t in reversed(range(bt // 8)):
            sl = slice(8 * t, 8 * t + 8)
            lr, li = g_ref[sl, :_S5_W], g_ref[sl, _S5_W:]
            for s in (1, 2, 4):
                m = rin < 8 - s
                sr = jnp.where(m, pltpu.roll(lr, 8 - s, 0), 0.0)
                si = jnp.where(m, pltpu.roll(li, 8 - s, 0), 0.0)
                lr, li = _cmul_add(lr, li, *pw[s - 1], sr, si)
            lr, li = _cmul_add(lr, li, pwr, pwi, cr, ci)
            lam_ref[sl, :_S5_W] = lr
            lam_ref[sl, _S5_W:] = li
            nr = jnp.where(rin == 7, cr, pltpu.roll(lr, 7, 0))
            ni = jnp.where(rin == 7, ci, pltpu.roll(li, 7, 0))
            xr, xi = x_ref[sl, :_S5_W], x_ref[sl, _S5_W:]
            acc_r = acc_r + (xr * nr + xi * ni)
            acc_i = acc_i + (xr * ni - xi * nr)
            cr, ci = lr[0:1, :], li[0:1, :]
        carry_ref[0:1, :] = cr
        carry_ref[1:2, :] = ci
        dar_ref[...] += jnp.sum(acc_r, axis=0, keepdims=True)
        dai_ref[...] += jnp.sum(acc_i, axis=0, keepdims=True)
        lam = lam_ref[...]
        du_ref[...] = _dot_raw(lam, wb_ref[...], "nt")
        dwb_ref[...] += _dot_raw(u_ref[...], lam, "tn")
        dwc_ref[...] += _dot_raw(x_ref[...], dy_ref[...], "tn")

    nblk = S5_NG // S5_BLK
    blk = pl.BlockSpec((bt, 2 * _S5_W), lambda g, t: (nb - 1 - t, g))
    col = pl.BlockSpec((bt, LANE), lambda g, t: (nb - 1 - t, g))
    a_s = pl.BlockSpec((None, 1, _S5_W), lambda g, t: (g, 0, 0))
    wb_s = pl.BlockSpec((None, LANE, 2 * _S5_W), lambda g, t: (g, 0, 0))
    wc_s = pl.BlockSpec((None, 2 * _S5_W, LANE), lambda g, t: (g, 0, 0))
    return pl.pallas_call(
        body, name=name, grid=(nblk, nb), in_specs=[col, blk, col, wb_s, a_s, a_s, wc_s],
        out_specs=[col, wb_s, wc_s, a_s, a_s],
        out_shape=[jax.ShapeDtypeStruct((T, nblk * LANE), F32), jax.ShapeDtypeStruct((nblk, LANE, 2 * _S5_W), F32),
                   jax.ShapeDtypeStruct((nblk, 2 * _S5_W, LANE), F32), jax.ShapeDtypeStruct((nblk, 1, _S5_W), F32),
                   jax.ShapeDtypeStruct((nblk, 1, _S5_W), F32)],
        scratch_shapes=[pltpu.VMEM((bt, 2 * _S5_W), F32), pltpu.VMEM((bt, 2 * _S5_W), F32), pltpu.VMEM((8, _S5_W), F32)],
        compiler_params=_cparams(2),
    )(dy, x, u, wb, a_re, a_im, wc)


def _norm_bf16(h, g, name):
    return rowwise(f_rmsnorm, [g], [h], [(D, BF16)], bt=512, name=name)[0]


def _norm_bwd(h, g, cts, name):
    n = len(cts) - 1

    def f(p, r):
        y = _rms(r[0], p[0])
        return (y,) * n + (r[0],)

    (dg,), (dh,) = rowwise_vjp(f, [g], [h], cts, [F32], bt=256, name=name)
    return dh, dg


def ffn_fwd(h, g, w_gu, w_down, tag):
    hn = _norm_bf16(h, g, f"{tag}_norm")
    a, gu = ffn_up(hn, w_gu, name=f"{tag}_up")
    h2 = matmul(a, w_down[None], "nn", add=h, name=f"{tag}_down")
    return h2, (h, hn, gu, a)


def ffn_bwd(d, saved, g, w_gu, w_down, tag):
    h, hn, gu, a = saved
    dgu = ffn_dact(d, w_down, gu, name=f"{tag}_dact")
    dwd = matmul(a, d, "tn", name=f"{tag}_dwd")[0]
    dwgu = matmul(hn, dgu, "tn", name=f"{tag}_dwgu")[0]
    dh, dg = matmul_nt_norm_bwd(dgu, w_gu, h, g, d, name=f"{tag}_dhn")
    return dh, dg, dwgu, dwd


_GLA_NC = 4
_SSD_NU = 4


def gla_fwd(h, gm, w_in, w_a2, b_a, ng, w_out, tag):
    hn = _norm_bf16(h, gm, f"{tag}_norm")
    proj = matmul(hn, w_in[None], "nn", name=f"{tag}_in")
    alow = (proj, LANE, 2 * (GLA_QK + GLA_VD) // LANE)
    la = rowwise(f_gla_gate_in_fwd, [w_a2, b_a], [alow], [(GLA_QK, F32)], bt=512, name=f"{tag}_gate")[0]
    og, ss = gla_scan_fwd(proj, la, ng, nc=_GLA_NC, name=f"{tag}_scan")
    h2 = matmul(og, w_out[None], "nn", add=h, name=f"{tag}_proj")
    return h2, (h, hn, proj, la, ss, og)


def gla_bwd(d, saved, gm, w_in, w_a2, b_a, ng, w_out, tag):
    h, hn, proj, la, ss, og = saved
    dog = matmul(d, w_out[None], "nt", name=f"{tag}_dog")
    dwout = matmul(og, d, "tn", name=f"{tag}_dwout")[0]
    dq, dk, dv, dla, dr, dng = gla_scan_bwd(proj, la, ng, ss, dog, nc=_GLA_NC, name=f"{tag}_dscan")
    alow = (proj, LANE, 2 * (GLA_QK + GLA_VD) // LANE)
    (dwa2, dba), (dalow,) = rowwise_vjp(f_gla_gate_in, [w_a2, b_a], [alow], [dla], [BF16], bt=512, name=f"{tag}_dgate")
    dproj = jnp.concatenate([dq, dk, dv, dr, dalow], axis=1)
    dwin = matmul(hn, dproj, "tn", name=f"{tag}_dwin")[0]
    dh, dgm = matmul_nt_norm_bwd(dproj, w_in, h, gm, d, name=f"{tag}_dhn")
    return dh, dgm, dwin, dwa2[:GLA_RANK], dba, dng, dwout


def ssd_fwd(h, gm, w_in, conv_w, conv_b, dtb, alog, dsk, ng, w_out, tag):
    hn = _norm_bf16(h, gm, f"{tag}_norm")
    proj = matmul(hn, w_in[None], "nn", name=f"{tag}_in")
    xbc = ssd_conv_fwd(proj, conv_w, conv_b, name=f"{tag}_conv")
    yg, hs = ssd_scan_fwd(xbc, proj, dtb, alog, dsk, ng, nu=_SSD_NU, name=f"{tag}_scan")
    h2 = matmul(yg, w_out[None], "nn", add=h, name=f"{tag}_proj")
    return h2, (h, hn, proj, xbc, hs, yg)


def ssd_bwd(d, saved, gm, w_in, conv_w, conv_b, dtb, alog, dsk, ng, w_out, tag):
    h, hn, proj, xbc, hs, yg = saved
    dyg = matmul(d, w_out[None], "nt", name=f"{tag}_dyg")
    dwout = matmul(yg, d, "tn", name=f"{tag}_dwout")[0]
    dxs, dbm, dcm, ddt, ddtb, dal, ddsk, dz, dng = ssd_scan_bwd(xbc, proj, dtb, alog, dsk, ng, hs, dyg, nu=_SSD_NU,
                                                               name=f"{tag}_dscan")
    dxbc = jnp.concatenate([dxs, dbm, dcm], axis=1)
    dpre, dcw, dcb = ssd_conv_bwd(proj, conv_w, conv_b, dxbc, name=f"{tag}_dconv")
    dproj = jnp.concatenate([dz, dpre, ddt.astype(BF16)], axis=1)
    dwin = matmul(hn, dproj, "tn", name=f"{tag}_dwin")[0]
    dh, dgm = matmul_nt_norm_bwd(dproj, w_in, h, gm, d, name=f"{tag}_dhn")
    return (dh, dgm, dwin, dcw, dcb, ddtb[:, :SSD_H], dal[:, :SSD_H], ddsk[:, :SSD_H], dng, dwout)


_S5_NB = S5_NG // S5_BLK


def _s5_param_args(log_dt, a_re, a_im, b_re, b_im, c_im):
    n = S5_NG * S5_GS
    tr = lambda b: jnp.transpose(b, (0, 2, 1)).reshape(n, S5_P)
    return [log_dt.reshape(S5_NG, 1), a_re, a_im, tr(b_re), tr(b_im), c_im.reshape(n, S5_P)]


def _s5_blockdiag(t):
    nb, gl, a, b = t.shape
    eye = jnp.eye(gl, dtype=t.dtype)
    return (t[:, :, :, None, :] * eye[None, :, None, :, None]).reshape(nb, gl * a, gl * b)


def _s5_diag(t, a, b):
    nb = t.shape[0]
    gl = t.shape[1] // a
    eye = jnp.eye(gl, dtype=t.dtype)
    return jnp.sum(t.reshape(nb, gl, a, gl, b) * eye[None, :, None, :, None], axis=3)


def _s5_weights(bbr, bbi, c_re, cneg):
    sh = (_S5_NB, S5_BLK, S5_GS, S5_P)
    wb = jnp.concatenate([_s5_blockdiag(bbr.reshape(sh)), _s5_blockdiag(bbi.reshape(sh))], axis=2)
    tr = lambda cc: jnp.transpose(cc.reshape(sh), (0, 1, 3, 2))
    wc = jnp.concatenate([_s5_blockdiag(tr(c_re)), _s5_blockdiag(tr(cneg))], axis=1)
    return wb, wc


def s5_fwd(h, gm, prm, dsk, w_glu, tag):
    log_dt, a_re, a_im, b_re, b_im, c_re, c_im = prm
    hn = rowwise(f_rmsnorm, [gm], [h], [(D, F32)], bt=512, name=f"{tag}_norm")[0]
    pargs = _s5_param_args(log_dt, a_re, a_im, b_re, b_im, c_im)
    abr, abi, bbr, bbi, cneg = s5_param_fwd(pargs, name=f"{tag}_param")
    wb, wc = _s5_weights(bbr, bbi, c_re.reshape(S5_NG * S5_GS, S5_P), cneg)
    ar, ai = abr.reshape(_S5_NB, 1, _S5_W), abi.reshape(_S5_NB, 1, _S5_W)
    wb, wc = wb.astype(BF16), wc.astype(BF16)
    x, ycp = s5_scan_fwd(hn, wb, ar, ai, wc, name=f"{tag}_scan")
    yg = rowwise(f_s5_act, [dsk], [ycp, hn], [(D, BF16)], bt=512, name=f"{tag}_act")[0]
    vg = matmul(yg, w_glu[None], "nn", name=f"{tag}_glu")
    h2 = rowwise(f_glu_res, [], [vg, h], [(D, F32)], bt=512, name=f"{tag}_out")[0]
    return h2, (h, hn, pargs, wb, wc, ar, ai, x, ycp, yg, vg)


def s5_bwd(d, saved, gm, dsk, w_glu, tag):
    h, hn, pargs, wb, wc, ar, ai, x, ycp, yg, vg = saved
    _, (dvg,) = rowwise_vjp(f_glu, [], [vg], [d], [BF16], bt=256, name=f"{tag}_dout")
    dwglu = matmul(yg, dvg, "tn", name=f"{tag}_dwglu")[0]
    dyg = matmul(dvg, w_glu[None], "nt", name=f"{tag}_dyg")
    (ddsk,), (dycp, dhn1) = rowwise_vjp(f_s5_act, [dsk], [ycp, hn], [dyg], [F32, F32], bt=256, name=f"{tag}_dact")
    dhn2, dwb, dwc, dar, dai = s5_scan_bwd(dycp, x, hn, wb, ar, ai, wc, name=f"{tag}_dscan")
    dh, dgm = _norm_bwd(h, gm, [dhn1, dhn2, d], f"{tag}_dnorm")
    n = S5_NG * S5_GS
    half = S5_BLK * S5_P
    d_bbr = _s5_diag(dwb[:, :, :half], S5_GS, S5_P).reshape(n, S5_P)
    d_bbi = _s5_diag(dwb[:, :, half:], S5_GS, S5_P).reshape(n, S5_P)
    from_c = lambda t: jnp.transpose(_s5_diag(t, S5_P, S5_GS), (0, 1, 3, 2)).reshape(n, S5_P)
    d_cre = from_c(dwc[:, :half, :])
    d_cneg = from_c(dwc[:, half:, :])
    cts = [dar.reshape(S5_NG, S5_P), dai.reshape(S5_NG, S5_P), d_bbr, d_bbi, d_cneg]
    dlog, dare, daim, dbre_t, dbim_t, dcim = s5_param_bwd(pargs, cts, name=f"{tag}_dparam")
    untr = lambda t: jnp.transpose(t.reshape(S5_NG, S5_GS, S5_P), (0, 2, 1))
    grads = (dlog.reshape(S5_NG), dare, daim, untr(dbre_t), untr(dbim_t),
             d_cre.reshape(S5_NG, S5_GS, S5_P), dcim.reshape(S5_NG, S5_GS, S5_P))
    return dh, dgm, grads, ddsk, dwglu


def _pad_last(w, n):
    return jnp.pad(w, [(0, 0)] * (w.ndim - 1) + [(0, n - w.shape[-1])])


_BIG = ("gla_w_in", "gla_w_out", "ssd_w_in", "ssd_w_out", "s5_w_glu", "ffn_w_gu", "ffn_w_down")


def interleave_gu(w):
    q = w.shape[-1] // 4
    return jnp.concatenate([w[..., :q], w[..., 2 * q:3 * q], w[..., q:2 * q], w[..., 3 * q:]], axis=-1)


def local_step(x, target, W, later_weights=None, later_grads=None, ffn0_grads=None, ffn0_weights=None):
    f32 = lambda a: a.astype(F32)
    row = lambda a: f32(a).reshape(1, -1)

    def layer_args(i):
        m, j = i % 3, i // 3
        gm = row(W["norm_mix_g"][i])
        if m == 0:
            args = (gm, W["gla_w_in"][j], jnp.pad(f32(W["gla_w_a2"][j]), ((0, LANE - GLA_RANK), (0, 0))),
                    row(W["gla_b_a"][j]), row(W["gla_norm_g"][j]), W["gla_w_out"][j])
        elif m == 1:
            pl_ = lambda a: _pad_last(row(a), LANE)
            args = (gm, W["ssd_w_in"][j], f32(W["ssd_conv_w"][j]),
                    row(W["ssd_conv_b"][j]), pl_(W["ssd_dt_bias"][j]), pl_(W["ssd_a_log"][j]), pl_(W["ssd_d"][j]),
                    row(W["ssd_norm_g"][j]), W["ssd_w_out"][j])
        else:
            prm = tuple(f32(W[k][j]) for k in ("s5_log_dt", "s5_a_re", "s5_a_im", "s5_b_re", "s5_b_im", "s5_c_re", "s5_c_im"))
            args = (gm, prm, row(W["s5_d"][j]), W["s5_w_glu"][j])
        return m, j, args

    h = x
    saved, mixers, ffns = [], [], []
    for i in range(DEPTH):
        mixer = layer_args(i)
        mixers.append(mixer)
        m, j, args = mixer
        tag = f"l{i}_{('gla', 'ssd', 's5')[m]}"
        h, sm = (gla_fwd, ssd_fwd, s5_fwd)[m](h, *args, tag)
        if i == 0 and ffn0_weights is not None:
            W = {**W, **ffn0_weights(h)}
        ffn = (row(W["norm_ffn_g"][i]), W["ffn_w_gu"][i], W["ffn_w_down"][i])
        ffns.append(ffn)
        h, sf = ffn_fwd(h, *ffn, f"l{i}_ffn")
        saved.append((sm, sf))
        if i == 0 and later_weights is not None:
            W = {**W, **later_weights(h)}
    loss, dfg, d = loss_head(h, row(W["final_norm_g"]), target, name="loss_head")

    G = {k: [None] * len(v) for k, v in W.items() if k != "final_norm_g"}
    G["final_norm_g"] = dfg.reshape(D)
    for i in reversed(range(DEPTH)):
        m, j, args = mixers[i]
        sm, sf = saved[i]
        if i == 0 and later_grads is not None:
            zero = later_grads(G)
            ffns[0] = (ffns[0][0], ffns[0][1], ffns[0][2] + zero.astype(ffns[0][2].dtype))
        d, dg, dwgu, dwd = ffn_bwd(d, sf, *ffns[i], f"l{i}_ffn")
        G["norm_ffn_g"][i], G["ffn_w_gu"][i], G["ffn_w_down"][i] = dg.reshape(D), dwgu, dwd
        if i == 0 and ffn0_grads is not None:
            zero = ffn0_grads(G)
            args = args[:-1] + (args[-1] + zero.astype(args[-1].dtype),)
        tag = f"l{i}_{('gla', 'ssd', 's5')[m]}"
        if m == 0:
            d, dgm, dwin, dwa2, dba, dng, dwout = gla_bwd(d, sm, *args, tag)
            G["gla_w_in"][j], G["gla_w_a2"][j], G["gla_b_a"][j] = dwin, dwa2, dba.reshape(-1)
            G["gla_norm_g"][j], G["gla_w_out"][j] = dng.reshape(-1), dwout
        elif m == 1:
            d, dgm, dwin, dcw, dcb, ddtb, dal, ddsk, dng, dwout = ssd_bwd(d, sm, *args, tag)
            G["ssd_w_in"][j], G["ssd_conv_w"][j], G["ssd_conv_b"][j] = dwin, dcw, dcb.reshape(-1)
            G["ssd_dt_bias"][j], G["ssd_a_log"][j], G["ssd_d"][j] = ddtb.reshape(-1), dal.reshape(-1), ddsk.reshape(-1)
            G["ssd_norm_g"][j], G["ssd_w_out"][j] = dng.reshape(-1), dwout
        else:
            d, dgm, pg, ddsk, dwglu = s5_bwd(d, sm, args[0], args[2], args[3], tag)
            for k, v in zip(("s5_log_dt", "s5_a_re", "s5_a_im", "s5_b_re", "s5_b_im", "s5_c_re", "s5_c_im"), pg):
                G[k][j] = v
            G["s5_d"][j], G["s5_w_glu"][j] = ddsk.reshape(-1), dwglu
        G["norm_mix_g"][i] = dgm.reshape(D)
    grads = {k: (v if k == "final_norm_g" or k in _BIG else jnp.stack(v)) for k, v in G.items()}
    return loss, d, grads


_MESH = pl.DeviceIdType.MESH
_ANY = pl.BlockSpec(memory_space=pl.ANY)
_DMA = pltpu.SemaphoreType.DMA
_ROWS_ALIGN = 1024


def _place():
    return lax.axis_index("x"), lax.axis_index("y"), lax.axis_index("c")


def _other_chips(x, y):
    return [(1 - x, y), (x, 1 - y), (1 - x, 1 - y)]


def _remote(src, dst, send_sems, recv_sems, k, to):
    return pltpu.make_async_remote_copy(src_ref=src, dst_ref=dst, send_sem=send_sems.at[k], recv_sem=recv_sems.at[k],
                                        device_id=to, device_id_type=_MESH)


def gather_shards(loc, *, name):
    def body(in_ref, out_ref, send_sems, recv_sems, local_sem):
        x, y, c = _place()
        me, sibling = (x, y, c), (x, y, 1 - c)
        chips = _other_chips(x, y)

        def half(px, py, hc):
            return out_ref.at[2 * px + py, hc]

        mine = pltpu.make_async_copy(in_ref, out_ref.at[2 * x + y], local_sem)
        mine.start()
        first = [_remote(in_ref.at[c], half(x, y, c), send_sems, recv_sems, j, (*chip, c)) for j, chip in enumerate(chips)]
        for cp in first:
            cp.start()
        passed = [_remote(half(*chip, c), half(*chip, c), send_sems, recv_sems, 3 + j, sibling) for j, chip in enumerate(chips)]
        for j, chip in enumerate(chips):
            _remote(in_ref.at[c], half(*chip, c), send_sems, recv_sems, j, me).wait_recv()
            passed[j].start()
        for j, chip in enumerate(chips):
            _remote(in_ref.at[c], half(*chip, 1 - c), send_sems, recv_sems, 3 + j, me).wait_recv()
        for cp in first + passed:
            cp.wait_send()
        mine.wait()

    return pl.pallas_call(
        body, name=name, in_specs=[_ANY], out_specs=_ANY,
        out_shape=jax.ShapeDtypeStruct((4,) + loc.shape, loc.dtype),
        scratch_shapes=[_DMA((6,)), _DMA((6,)), _DMA(())],
    )(loc)


def _pos(px, py, perm):
    return 2 * py + px if perm else 2 * px + py


def _part(ref, kind, p, loc):
    if kind == "lead":
        return ref.at[p]
    return ref.at[:, pl.ds(pl.multiple_of(p * loc, LANE), loc)]


def _rows(ref, h, hr):
    return ref.at[pl.ds(h * hr, hr)]


def _rows_block(hr, width):
    return max(b for b in range(16, hr + 1, 16) if hr % b == 0 and (b * width <= (1 << 19) or b == 16))


def gather_big(locs, kinds, *, name):
    n = len(locs)

    def body(*refs):
        ins, outs = refs[:n], refs[n:2 * n]
        send_sems, recv_sems = refs[2 * n + 1:]
        refs[2 * n][...] = jnp.zeros_like(refs[2 * n])
        x, y, c = _place()
        me, sibling = (x, y, c), (x, y, 1 - c)
        chips = _other_chips(x, y)

        def half(i, px, py, h):
            (kind, perm), (rows, loc) = kinds[i], locs[i].shape
            return _rows(_part(outs[i], kind, _pos(px, py, perm), loc), h, rows // 2)

        sends = []
        for i in range(n):
            (kind, perm), (rows, loc) = kinds[i], locs[i].shape
            own = _part(outs[i], kind, _pos(x, y, perm), loc)
            sends.append(_remote(ins[i], own, send_sems, recv_sems, 6 * n + i, sibling))
            sends[-1].start()
            for j, chip in enumerate(chips):
                sends.append(_remote(_rows(ins[i], c, rows // 2), half(i, x, y, c), send_sems, recv_sems, 6 * i + j, (*chip, c)))
                sends[-1].start()
        for i in range(n):
            hr = locs[i].shape[0] // 2
            for j, chip in enumerate(chips):
                _remote(_rows(ins[i], c, hr), half(i, *chip, c), send_sems, recv_sems, 6 * i + j, me).wait_recv()
                sends.append(_remote(half(i, *chip, c), half(i, *chip, c), send_sems, recv_sems, 6 * i + 3 + j, sibling))
                sends[-1].start()
        for i in range(n):
            (kind, perm), (rows, loc) = kinds[i], locs[i].shape
            for j, chip in enumerate(chips):
                _remote(_rows(ins[i], c, rows // 2), half(i, *chip, 1 - c), send_sems, recv_sems, 6 * i + 3 + j, me).wait_recv()
            _remote(ins[i], _part(outs[i], kind, _pos(x, y, perm), loc), send_sems, recv_sems, 6 * n + i, me).wait_recv()
        for cp in sends:
            cp.wait_send()

    def out_shape(a, kind):
        rows, loc = a.shape
        return jax.ShapeDtypeStruct((4, rows, loc) if kind == "lead" else (rows, 4 * loc), a.dtype)

    outs = pl.pallas_call(
        body, name=name, in_specs=[_ANY] * n, out_specs=[_ANY] * n + [pl.BlockSpec(memory_space=pltpu.VMEM)],
        out_shape=[out_shape(a, k[0]) for a, k in zip(locs, kinds)] + [jax.ShapeDtypeStruct((8, LANE), F32)],
        scratch_shapes=[_DMA((7 * n,)), _DMA((7 * n,))],
    )(*locs)
    return list(outs[:n]), outs[n][0, 0]


_HBM = pl.BlockSpec(memory_space=pltpu.HBM)
_SEM = pl.BlockSpec(memory_space=pltpu.SEMAPHORE)
_EFFECT = pltpu.SideEffectType.DATAFLOW_SIDE_EFFECTING


def _in_hbm(a):
    return pltpu.with_memory_space_constraint(a, pltpu.HBM)


def _gather_ici_copies(ins, lands, kinds, shapes, send_sems, recv_sems):
    x, y, c = _place()
    sends, arrivals = [], []
    for i, ((kind, perm), (rows, loc)) in enumerate(zip(kinds, shapes)):
        hr = rows // 2
        mine = _part(lands[i], kind, _pos(x, y, perm), loc)
        sends.append(_remote(ins[i], mine, send_sems, recv_sems, 4 * i + 3, (x, y, 1 - c)))
        arrivals.append(_remote(ins[i], mine, send_sems, recv_sems, 4 * i + 3, (x, y, c)))
        for j, (px, py) in enumerate(_other_chips(x, y)):
            sends.append(_remote(_rows(ins[i], c, hr), _rows(mine, c, hr), send_sems, recv_sems, 4 * i + j, (px, py, c)))
            theirs = _rows(_part(lands[i], kind, _pos(px, py, perm), loc), c, hr)
            arrivals.append(_remote(_rows(ins[i], c, hr), theirs, send_sems, recv_sems, 4 * i + j, (x, y, c)))
    return sends, arrivals


def gather_start(locs, kinds, *, name):
    n = len(locs)
    shapes = [a.shape for a in locs]

    def land_shape(a, kind):
        rows, loc = a.shape
        return (4, rows, loc) if kind == "lead" else (rows, 4 * loc)

    def body(*refs):
        sends, _ = _gather_ici_copies(refs[:n], refs[n:2 * n], kinds, shapes, refs[2 * n], refs[2 * n + 1])
        for cp in sends:
            cp.start()
        refs[-1][...] = jnp.zeros_like(refs[-1])

    lands = [lax.empty(land_shape(a, k[0]), a.dtype) for a, k in zip(locs, kinds)]
    outs = pl.pallas_call(
        body, name=name, in_specs=[_HBM] * (2 * n), out_specs=[_SEM, _SEM] + [_HBM] * (2 * n) + [pl.BlockSpec(memory_space=pltpu.VMEM)],
        out_shape=[_DMA((4 * n,)), _DMA((4 * n,))] + [pltpu.HBM(a.shape, a.dtype) for a in locs]
        + [pltpu.HBM(l.shape, l.dtype) for l in lands] + [jax.ShapeDtypeStruct((8, LANE), F32)],
        input_output_aliases={i: 2 + i for i in range(2 * n)},
        compiler_params=pltpu.CompilerParams(has_side_effects=_EFFECT),
    )(*[_in_hbm(a) for a in locs], *[_in_hbm(l) for l in lands])
    return outs[0], outs[1], list(outs[2:2 + n]), list(outs[2 + n:2 + 2 * n]), outs[-1][0, 0]


def gather_wait(send_sems, recv_sems, locs, lands, kinds, after, *, name):
    n = len(locs)
    shapes = [a.shape for a in locs]

    def body(*refs):
        sends, arrivals = _gather_ici_copies(refs[:n], refs[n:2 * n], kinds, shapes, refs[2 * n], refs[2 * n + 1])
        for cp in sends:
            cp.wait_send()
        for cp in arrivals:
            cp.wait_recv()

    outs = pl.pallas_call(
        body, name=name, in_specs=[_HBM] * (2 * n) + [_SEM, _SEM, _ANY], out_specs=[_HBM] * (2 * n),
        out_shape=[pltpu.HBM(a.shape, a.dtype) for a in locs] + [pltpu.HBM(l.shape, l.dtype) for l in lands],
        input_output_aliases={i: i for i in range(2 * n)},
        compiler_params=pltpu.CompilerParams(has_side_effects=_EFFECT),
    )(*locs, *lands, send_sems, recv_sems, after)
    return list(outs[n:])


def gather_finish(lands, kinds, shapes, *, name):
    n = len(lands)

    def body(*refs):
        bufs = refs[n:2 * n]
        send_sems, recv_sems = refs[2 * n:]
        x, y, c = _place()
        sends = []
        for i, ((kind, perm), (rows, loc)) in enumerate(zip(kinds, shapes)):
            for j, (px, py) in enumerate(_other_chips(x, y)):
                part = _part(bufs[i], kind, _pos(px, py, perm), loc)
                sends.append(_remote(_rows(part, c, rows // 2), _rows(part, c, rows // 2), send_sems, recv_sems, 3 * i + j, (x, y, 1 - c)))
                sends[-1].start()
        for i, ((kind, perm), (rows, loc)) in enumerate(zip(kinds, shapes)):
            for j, (px, py) in enumerate(_other_chips(x, y)):
                part = _part(bufs[i], kind, _pos(px, py, perm), loc)
                _remote(_rows(part, c, rows // 2), _rows(part, 1 - c, rows // 2), send_sems, recv_sems, 3 * i + j, (x, y, c)).wait_recv()
        for cp in sends:
            cp.wait_send()

    return list(pl.pallas_call(
        body, name=name, in_specs=[_ANY] * n, out_specs=[_ANY] * n,
        out_shape=[jax.ShapeDtypeStruct(l.shape, l.dtype) for l in lands],
        input_output_aliases={i: i for i in range(n)}, scratch_shapes=[_DMA((3 * n,)), _DMA((3 * n,))],
    )(*lands))


def _scatter_copies(qs, lands, kinds, locs, send_sems, recv_sems):
    x, y, c = _place()
    sends, arrivals = [], []
    for i, (kind, perm) in enumerate(kinds):
        for j, (px, py) in enumerate(_other_chips(x, y)):
            src = _part(qs[i], kind, _pos(px, py, perm), locs[i])
            sends.append(_remote(src, lands[i].at[j], send_sems, recv_sems, 3 * i + j, (px, py, c)))
            arrivals.append(_remote(src, lands[i].at[j], send_sems, recv_sems, 3 * i + j, (x, y, c)))
    return sends, arrivals


def _scatter_land(q, kind, loc):
    return (3, q.shape[1] if kind == "lead" else q.shape[0], loc)


def scatter_start(qs, kinds, locs, *, name):
    n = len(qs)

    def body(*refs):
        sends, _ = _scatter_copies(refs[:n], refs[n:2 * n], kinds, locs, refs[2 * n], refs[2 * n + 1])
        for cp in sends:
            cp.start()
        refs[-1][...] = jnp.zeros_like(refs[-1])

    lands = [lax.empty(_scatter_land(q, k[0], l), q.dtype) for q, k, l in zip(qs, kinds, locs)]
    outs = pl.pallas_call(
        body, name=name, in_specs=[_HBM] * (2 * n), out_specs=[_SEM, _SEM] + [_HBM] * (2 * n) + [pl.BlockSpec(memory_space=pltpu.VMEM)],
        out_shape=[_DMA((3 * n,)), _DMA((3 * n,))] + [pltpu.HBM(q.shape, q.dtype) for q in qs]
        + [pltpu.HBM(l.shape, l.dtype) for l in lands] + [jax.ShapeDtypeStruct((8, LANE), F32)],
        input_output_aliases={i: 2 + i for i in range(2 * n)},
        compiler_params=pltpu.CompilerParams(has_side_effects=_EFFECT),
    )(*[_in_hbm(q) for q in qs], *[_in_hbm(l) for l in lands])
    return outs[0], outs[1], list(outs[2:2 + n]), list(outs[2 + n:2 + 2 * n]), outs[-1][0, 0]


def scatter_wait(send_sems, recv_sems, qs, lands, kinds, locs, after, *, name):
    n = len(qs)

    def body(*refs):
        sends, arrivals = _scatter_copies(refs[:n], refs[n:2 * n], kinds, locs, refs[2 * n], refs[2 * n + 1])
        for cp in sends:
            cp.wait_send()
        for cp in arrivals:
            cp.wait_recv()

    outs = pl.pallas_call(
        body, name=name, in_specs=[_HBM] * (2 * n) + [_SEM, _SEM, _ANY], out_specs=[_HBM] * (2 * n),
        out_shape=[pltpu.HBM(q.shape, q.dtype) for q in qs] + [pltpu.HBM(l.shape, l.dtype) for l in lands],
        input_output_aliases={i: i for i in range(2 * n)},
        compiler_params=pltpu.CompilerParams(has_side_effects=_EFFECT),
    )(*qs, *lands, send_sems, recv_sems, after)
    return list(outs[:n]), list(outs[n:])


def pair_swap(ps, kinds, *, name):
    n = len(ps)

    def body(*refs):
        ins, outs = refs[:n], refs[n:2 * n]
        send_sems, recv_sems = refs[2 * n:]
        x, y, c = _place()
        cps = []
        for i in range(n):
            if kinds[i][0] == "lead":
                hr = ps[i].shape[1] // 2
                src = ins[i].at[:, pl.ds((1 - c) * hr, hr)]
            else:
                hr = ps[i].shape[0] // 2
                src = _rows(ins[i], 1 - c, hr)
            cps.append(_remote(src, outs[i], send_sems, recv_sems, i, (x, y, 1 - c)))
            cps[-1].start()
        for cp in cps:
            cp.wait()

    def out_shape(a, kind):
        s = a.shape
        return jax.ShapeDtypeStruct((4, s[1] // 2, s[2]) if kind == "lead" else (s[0] // 2, s[1]), a.dtype)

    return pl.pallas_call(
        body, name=name, in_specs=[_ANY] * n, out_specs=[_ANY] * n,
        out_shape=[out_shape(a, k[0]) for a, k in zip(ps, kinds)], scratch_shapes=[_DMA((n,)), _DMA((n,))],
    )(*ps)


def pair_add(p, got, c_arr, kind, *, name):
    if kind == "lead":
        _, hr, cols = got.shape
        br = _rows_block(hr, cols)
        nb = hr // br
        grid = (4, nb)
        p_spec = pl.BlockSpec((None, br, cols), lambda s, i, cr: (s, cr[0] * nb + i, 0))
        g_spec = pl.BlockSpec((None, br, cols), lambda s, i, cr: (s, i, 0))
    else:
        hr, w = got.shape
        br = _rows_block(hr, w)
        nb = hr // br
        grid = (nb,)
        p_spec = pl.BlockSpec((br, w), lambda i, cr: (cr[0] * nb + i, 0))
        g_spec = pl.BlockSpec((br, w), lambda i, cr: (i, 0))

    def body(c_ref, p_ref, g_ref, o_ref):
        o_ref[...] = (p_ref[...] + g_ref[...]).astype(o_ref.dtype)

    return pl.pallas_call(
        body, name=name, out_shape=jax.ShapeDtypeStruct(got.shape, BF16),
        grid_spec=pltpu.PrefetchScalarGridSpec(num_scalar_prefetch=1, grid=grid, in_specs=[p_spec, g_spec], out_specs=g_spec),
        compiler_params=_cparams(len(grid)),
    )(c_arr, p, got)


def chip_scatter(qs, kinds, locs, *, name):
    n = len(qs)

    def body(*refs):
        ins, outs = refs[:n], refs[n:2 * n]
        send_sems, recv_sems = refs[2 * n:]
        x, y, c = _place()
        cps = []
        for i in range(n):
            kind, perm = kinds[i]
            for j, (px, py) in enumerate(_other_chips(x, y)):
                cps.append(_remote(_part(ins[i], kind, _pos(px, py, perm), locs[i]), outs[i].at[j], send_sems, recv_sems,
                                   3 * i + j, (px, py, c)))
                cps[-1].start()
        for cp in cps:
            cp.wait()

    def out_shape(a, kind, loc):
        hr = a.shape[1] if kind == "lead" else a.shape[0]
        return jax.ShapeDtypeStruct((3, hr, loc), a.dtype)

    return pl.pallas_call(
        body, name=name, in_specs=[_ANY] * n, out_specs=[_ANY] * n,
        out_shape=[out_shape(a, k[0], l) for a, k, l in zip(qs, kinds, locs)],
        scratch_shapes=[_DMA((3 * n,)), _DMA((3 * n,))],
    )(*qs)


def chip_add(q, r, pos_arr, c_arr, kind, loc, *, name):
    _, hr, _ = r.shape
    br = _rows_block(hr, loc)
    nb = hr // br
    if kind == "lead":
        q_spec = pl.BlockSpec((None, br, loc), lambda i, pr, cr: (pr[0], i, 0))
    else:
        q_spec = pl.BlockSpec((br, loc), lambda i, pr, cr: (i, pr[0]))
    r_spec = pl.BlockSpec((3, br, loc), lambda i, pr, cr: (0, i, 0))
    o_spec = pl.BlockSpec((br, loc), lambda i, pr, cr: (cr[0] * nb + i, 0))

    def body(p_ref, c_ref, q_ref, r_ref, o_ref):
        acc = q_ref[...].astype(F32)
        for j in range(3):
            acc = acc + r_ref[j].astype(F32)
        o_ref[...] = acc

    return pl.pallas_call(
        body, name=name, out_shape=jax.ShapeDtypeStruct((2 * hr, loc), F32),
        grid_spec=pltpu.PrefetchScalarGridSpec(num_scalar_prefetch=2, grid=(nb,), in_specs=[q_spec, r_spec], out_specs=o_spec),
        compiler_params=_cparams(1),
    )(pos_arr, c_arr, q, r)


def share_rows(fs, *, name):
    n = len(fs)

    def body(*refs):
        bufs = refs[n:2 * n]
        send_sems, recv_sems = refs[2 * n:]
        x, y, c = _place()
        cps = []
        for i in range(n):
            hr = fs[i].shape[0] // 2
            cps.append(_remote(_rows(bufs[i], c, hr), _rows(bufs[i], c, hr), send_sems, recv_sems, i, (x, y, 1 - c)))
            cps[-1].start()
        for i, cp in enumerate(cps):
            hr = fs[i].shape[0] // 2
            _remote(_rows(bufs[i], c, hr), _rows(bufs[i], 1 - c, hr), send_sems, recv_sems, i, (x, y, c)).wait_recv()
            cp.wait_send()

    return pl.pallas_call(
        body, name=name, in_specs=[_ANY] * n, out_specs=[_ANY] * n,
        out_shape=[jax.ShapeDtypeStruct(f.shape, f.dtype) for f in fs],
        input_output_aliases={i: i for i in range(n)}, scratch_shapes=[_DMA((n,)), _DMA((n,))],
    )(*fs)


def _gather_all_copies(v_ref, land_ref, send_sems, recv_sems):
    x, y, c = _place()
    flip = lambda p, m: 1 - p if m else p
    idx = lambda p: 4 * p[0] + 2 * p[1] + p[2]
    sends, arrivals = [], []
    for k, m in enumerate(range(1, 8)):
        p = (flip(x, m & 4), flip(y, m & 2), flip(c, m & 1))
        sends.append(_remote(v_ref, land_ref.at[idx((x, y, c))], send_sems, recv_sems, k, p))
        arrivals.append(_remote(v_ref, land_ref.at[idx(p)], send_sems, recv_sems, k, (x, y, c)))
    return sends, arrivals


def gather_all_start(v, *, name):
    def body(v_ref, land_ref, send_sems, recv_sems, v_thru, land_thru, token):
        sends, _ = _gather_all_copies(v_ref, land_ref, send_sems, recv_sems)
        for cp in sends:
            cp.start()
        token[...] = jnp.zeros_like(token)

    land = jnp.zeros((8,) + v.shape, v.dtype)
    outs = pl.pallas_call(
        body, name=name, in_specs=[_HBM, _HBM], out_specs=[_SEM, _SEM, _HBM, _HBM, pl.BlockSpec(memory_space=pltpu.VMEM)],
        out_shape=[_DMA((7,)), _DMA((7,)), pltpu.HBM(v.shape, v.dtype), pltpu.HBM(land.shape, land.dtype),
                   jax.ShapeDtypeStruct((8, LANE), F32)],
        input_output_aliases={0: 2, 1: 3}, compiler_params=pltpu.CompilerParams(has_side_effects=_EFFECT),
    )(_in_hbm(v), _in_hbm(land))
    return outs[0], outs[1], outs[2], outs[3], outs[4][0, 0]


def gather_all_wait(send_sems, recv_sems, v, land, after, *, name):
    def body(v_ref, land_ref, send_sems, recv_sems, after_ref, v_dead, got_ref):
        sends, arrivals = _gather_all_copies(v_ref, land_ref, send_sems, recv_sems)
        for cp in sends:
            cp.wait_send()
        for cp in arrivals:
            cp.wait_recv()

    return pl.pallas_call(
        body, name=name, in_specs=[_HBM, _HBM, _SEM, _SEM, _ANY], out_specs=[_HBM, _HBM],
        out_shape=[pltpu.HBM(v.shape, v.dtype), pltpu.HBM(land.shape, land.dtype)],
        input_output_aliases={0: 0, 1: 1}, compiler_params=pltpu.CompilerParams(has_side_effects=_EFFECT),
    )(v, land, send_sems, recv_sems, after)[1]


def sum_slots(land, v, me_arr, *, name):
    n, R, L = land.shape
    br = _pick(R, _ROWS_ALIGN, 8)

    def body(me_ref, land_ref, v_ref, o_ref):
        acc = None
        for i in range(n):
            term = jnp.where(me_ref[0] == i, v_ref[...], land_ref[i])
            acc = term if acc is None else acc + term
        o_ref[...] = acc

    row = pl.BlockSpec((br, L), lambda i, me: (i, 0))
    return pl.pallas_call(
        body, name=name, out_shape=jax.ShapeDtypeStruct((R, L), land.dtype),
        grid_spec=pltpu.PrefetchScalarGridSpec(num_scalar_prefetch=1, grid=(R // br,),
                                               in_specs=[pl.BlockSpec((n, br, L), lambda i, me: (0, i, 0)), row], out_specs=row),
        compiler_params=_cparams(1),
    )(me_arr, land, v)


def sum_stack(a, extra=None, *, name):
    n, R, L = a.shape
    br = _pick(R, _ROWS_ALIGN, 8)

    def body(*refs):
        a_ref, o_ref = refs[0], refs[-1]
        acc = refs[1][...] if extra is not None else a_ref[0]
        for i in range(0 if extra is not None else 1, n):
            acc = acc + a_ref[i]
        o_ref[...] = acc

    row = pl.BlockSpec((br, L), lambda i: (i, 0))
    specs = [pl.BlockSpec((n, br, L), lambda i: (0, i, 0))] + ([row] if extra is not None else [])
    args = [a] + ([extra] if extra is not None else [])
    return pl.pallas_call(body, name=name, grid=(R // br,), in_specs=specs, out_specs=row,
                          out_shape=jax.ShapeDtypeStruct((R, L), a.dtype), compiler_params=_cparams(1))(*args)


def adamw(w, g, m, v, *, name):
    shape = w.shape
    size = math.prod(shape)
    last = shape[-1]
    if last % LANE != 0 and size % LANE == 0 and size <= (1 << 20):
        last = LANE
    rows = size // last
    budget = (1 << 18) // last
    br = rows
    if rows > budget:
        br = max(c for c in range(8, budget + 1, 8) if rows % c == 0)
    v2 = lambda a: a.reshape(rows, last)

    def body(w_ref, g_ref, m_ref, v_ref, d_ref, nm_ref, nv_ref):
        gg = g_ref[...]
        nm = ADAM_B1 * m_ref[...] + (1.0 - ADAM_B1) * gg
        nv = ADAM_B2 * v_ref[...] + (1.0 - ADAM_B2) * (gg * gg)
        m_hat = nm / (1.0 - ADAM_B1 ** ADAM_STEP)
        v_hat = nv / (1.0 - ADAM_B2 ** ADAM_STEP)
        d_ref[...] = -ADAM_LR * (m_hat / (jnp.sqrt(v_hat) + ADAM_EPS) + ADAM_WD * w_ref[...])
        nm_ref[...] = nm
        nv_ref[...] = nv

    spec = pl.BlockSpec((br, last), lambda i: (i, 0))
    outs = pl.pallas_call(
        body, name=name, grid=(rows // br,), in_specs=[spec] * 4, out_specs=[spec] * 3,
        out_shape=[jax.ShapeDtypeStruct((rows, last), F32)] * 3, compiler_params=_cparams(1),
    )(v2(w), v2(g), v2(m), v2(v))
    return [o.reshape(shape) for o in outs]


_WEIGHTS = ["norm_mix_g", "norm_ffn_g", "gla_w_in", "gla_w_a2", "gla_b_a", "gla_norm_g", "gla_w_out", "ssd_w_in",
            "ssd_conv_w", "ssd_conv_b", "ssd_dt_bias", "ssd_a_log", "ssd_d", "ssd_norm_g", "ssd_w_out", "s5_log_dt",
            "s5_a_re", "s5_a_im", "s5_b_re", "s5_b_im", "s5_c_re", "s5_c_im", "s5_d", "s5_w_glu", "ffn_w_gu",
            "ffn_w_down", "final_norm_g"]
_SHARD_AXIS = {"gla_w_in": 2, "gla_w_a2": 2, "gla_b_a": 1, "gla_norm_g": 1, "gla_w_out": 1, "ssd_w_in": 2,
               "ssd_conv_w": 2, "ssd_w_out": 1, "s5_d": 1, "s5_w_glu": 2, "ffn_w_gu": 2, "ffn_w_down": 1}
_SMALL_SHARDED = [n for n in _WEIGHTS if n in _SHARD_AXIS and n not in _BIG]
_REPLICATED = [n for n in _WEIGHTS if n not in _SHARD_AXIS]
_BIG_KIND = {"gla_w_in": ("lead", False), "gla_w_out": ("lead", False), "ssd_w_in": ("lead", False),
             "ssd_w_out": ("lead", False), "s5_w_glu": ("cols", False), "ffn_w_gu": ("cols", True),
             "ffn_w_down": ("lead", False)}
_PADDED_IN = {"gla_w_in": GLA_INP, "ssd_w_in": SSD_INP}


def _to_rows(flat, parts=1):
    per = -(-flat.shape[0] // (parts * LANE * _ROWS_ALIGN)) * _ROWS_ALIGN
    flat = jnp.pad(flat, (0, parts * per * LANE - flat.shape[0]))
    return flat.reshape(parts, per, LANE)


def _big_layers(local):
    return [(n, j, local[n][j].reshape(-1, local[n].shape[-1])) for n in _BIG for j in range(local[n].shape[0])]


def _in_layer0(n, j):
    return j == 0 and n in ("gla_w_in", "gla_w_out", "ffn_w_gu", "ffn_w_down")


def _assemble(n, g):
    if n in _PADDED_IN:
        return jnp.concatenate([g[s] for s in range(4)] + [jnp.zeros((g.shape[1], _PADDED_IN[n] - 4 * g.shape[2]), BF16)], axis=1)
    if _BIG_KIND[n][0] == "lead":
        return g.reshape(4 * g.shape[1], g.shape[2])
    return g


def _is_gla0(n, j):
    return j == 0 and n in ("gla_w_in", "gla_w_out")


def _gather_first(local):
    layers = _big_layers(local)
    first = [l for l in layers if _is_gla0(l[0], l[1])]
    full = {n: [None] * local[n].shape[0] for n in _BIG}
    got, done = gather_big([w.astype(BF16) for _, _, w in first], [_BIG_KIND[n] for n, _, _ in first], name="gather_weights_first")
    for (n, j, _), g in zip(first, got):
        full[n][j] = _assemble(n, g)
    flat = jnp.concatenate([local[n].astype(F32).reshape(-1) for n in _SMALL_SHARDED])
    got = gather_shards(_to_rows(flat, 2), name="gather_small_weights").reshape(4, -1)
    off = 0
    for n in _SMALL_SHARDED:
        bs = local[n].shape
        sz = math.prod(bs)
        seg = got[:, off:off + sz].reshape((4,) + bs)
        off += sz
        ax = _SHARD_AXIS[n]
        full[n] = jnp.moveaxis(seg, 0, ax).reshape(bs[:ax] + (4 * bs[ax],) + bs[ax + 1:])
    pending = {}
    for tag, want in (("ffn0", _is_ffn0), ("later", lambda n, j: not _in_layer0(n, j))):
        group = [l for l in layers if want(l[0], l[1])]
        kinds = [_BIG_KIND[n] for n, _, _ in group]
        ops = [(w + done if k == 0 else w).astype(BF16) for k, (_, _, w) in enumerate(group)]
        send_sems, recv_sems, locs, lands, done = gather_start(ops, kinds, name=f"gather_weights_start_{tag}")
        pending[tag] = (group, kinds, send_sems, recv_sems, locs, lands)
    return full, pending, done


def _gather_rest(full, pending, after, tag):
    group, kinds, send_sems, recv_sems, locs, lands = pending
    lands = gather_wait(send_sems, recv_sems, locs, lands, kinds, after, name=f"gather_weights_wait_{tag}")
    lands = gather_finish(lands, kinds, [w.shape for _, _, w in group], name=f"gather_weights_finish_{tag}")
    out = {n: list(full[n]) for n in _BIG}
    for (n, j, _), g in zip(group, lands):
        out[n][j] = _assemble(n, g)
    return out


def _reduce_ops(grads, local, want):
    ops = []
    for n in _BIG:
        kind = _BIG_KIND[n]
        for j, g in enumerate(grads[n]):
            if not want(n, j):
                continue
            loc = local[n].shape[-1] if kind[0] == "cols" or n in _PADDED_IN else g.shape[1]
            if n in _PADDED_IN:
                g = jnp.stack([g[:, s * loc:(s + 1) * loc] for s in range(4)])
            elif kind[0] == "lead":
                g = g.reshape(4, g.shape[0] // 4, g.shape[1])
            ops.append((n, j, kind, loc, g))
    return ops


def _pair_sums(ops, c_arr, tag):
    gots = pair_swap([o[4] for o in ops], [o[2] for o in ops], name=f"reduce_pair_swap_{tag}")
    return [pair_add(o[4], got, c_arr, o[2][0], name=f"reduce_pair_add_{o[0]}{o[1]}") for o, got in zip(ops, gots)]


def _is_ffn0(n, j):
    return j == 0 and n in ("ffn_w_gu", "ffn_w_down")


def _reduce_start(grads, local, c, want, tag):
    ops = _reduce_ops(grads, local, want)
    c_arr = jnp.reshape(c, (1,)).astype(jnp.int32)
    qs = _pair_sums(ops, c_arr, tag)
    send_sems, recv_sems, qs, lands, zero = scatter_start(qs, [o[2] for o in ops], [o[3] for o in ops],
                                                          name=f"reduce_scatter_start_{tag}")
    return (ops, send_sems, recv_sems, qs, lands, tag), zero


def _reduce_big(grads, local, pendings, after, x, y, c):
    c_arr = jnp.reshape(c, (1,)).astype(jnp.int32)
    ops, qs, rs = [], [], []
    for ops_p, send_sems, recv_sems, qs_p, lands, tag in pendings:
        qs_p, rs_p = scatter_wait(send_sems, recv_sems, qs_p, lands, [o[2] for o in ops_p], [o[3] for o in ops_p], after,
                                  name=f"reduce_scatter_wait_{tag}")
        ops, qs, rs = ops + ops_p, qs + qs_p, rs + rs_p
    ops_f = _reduce_ops(grads, local, lambda n, j: _in_layer0(n, j) and not _is_ffn0(n, j))
    qs_f = _pair_sums(ops_f, c_arr, "first")
    rs_f = list(chip_scatter(qs_f, [o[2] for o in ops_f], [o[3] for o in ops_f], name="reduce_chip_scatter_first"))
    ops, qs, rs = ops + ops_f, qs + qs_f, rs + rs_f
    fs = [chip_add(q, r, jnp.reshape(_pos(x, y, o[2][1]), (1,)).astype(jnp.int32), c_arr, o[2][0], o[3],
                   name=f"reduce_chip_add_{o[0]}{o[1]}") for o, q, r in zip(ops, qs, rs)]
    outs = share_rows(fs, name="reduce_share")
    red = {(o[0], o[1]): r for o, r in zip(ops, outs)}
    return {n: jnp.stack([red[(n, j)] for j in range(local[n].shape[0])]).reshape(local[n].shape) for n in _BIG}


def _reduce_small_start(grads):
    names = _REPLICATED + _SMALL_SHARDED
    flat = jnp.concatenate([grads[n].astype(F32).reshape(-1) for n in names])
    n_el = flat.shape[0]
    rows = -(-n_el // (LANE * 8)) * 8
    v = jnp.pad(flat, (0, rows * LANE - n_el)).reshape(rows, LANE)
    return gather_all_start(v, name="reduce_small_start")[:4]


def _reduce_small(pending, after, grads, local, x, y, c):
    names = _REPLICATED + _SMALL_SHARDED
    send_sems, recv_sems, v, land = pending
    land = gather_all_wait(send_sems, recv_sems, v, land, after, name="reduce_small_wait")
    me = jnp.reshape(4 * x + 2 * y + c, (1,)).astype(jnp.int32)
    red = sum_slots(land, v, me, name="reduce_small_add").reshape(-1)
    out, off = {}, 0
    for n in names:
        sz = math.prod(grads[n].shape)
        g = red[off:off + sz].reshape(grads[n].shape)
        off += sz
        if n in _SHARD_AXIS:
            ax = _SHARD_AXIS[n]
            loc = local[n].shape[ax]
            g = lax.dynamic_slice_in_dim(g, (2 * x + y) * loc, loc, axis=ax)
        out[n] = g
    return out


def kernel(x, norm_mix_g, norm_ffn_g, gla_w_in, gla_w_a2, gla_b_a, gla_norm_g, gla_w_out, ssd_w_in, ssd_conv_w, ssd_conv_b, ssd_dt_bias, ssd_a_log, ssd_d, ssd_norm_g, ssd_w_out, s5_log_dt, s5_a_re, s5_a_im, s5_b_re, s5_b_im, s5_c_re, s5_c_im, s5_d, s5_w_glu, ffn_w_gu, ffn_w_down, final_norm_g, loss_target, m_norm_mix_g, m_norm_ffn_g, m_gla_w_in, m_gla_w_a2, m_gla_b_a, m_gla_norm_g, m_gla_w_out, m_ssd_w_in, m_ssd_conv_w, m_ssd_conv_b, m_ssd_dt_bias, m_ssd_a_log, m_ssd_d, m_ssd_norm_g, m_ssd_w_out, m_s5_log_dt, m_s5_a_re, m_s5_a_im, m_s5_b_re, m_s5_b_im, m_s5_c_re, m_s5_c_im, m_s5_d, m_s5_w_glu, m_ffn_w_gu, m_ffn_w_down, m_final_norm_g, v_norm_mix_g, v_norm_ffn_g, v_gla_w_in, v_gla_w_a2, v_gla_b_a, v_gla_norm_g, v_gla_w_out, v_ssd_w_in, v_ssd_conv_w, v_ssd_conv_b, v_ssd_dt_bias, v_ssd_a_log, v_ssd_d, v_ssd_norm_g, v_ssd_w_out, v_s5_log_dt, v_s5_a_re, v_s5_a_im, v_s5_b_re, v_s5_b_im, v_s5_c_re, v_s5_c_im, v_s5_d, v_s5_w_glu, v_ffn_w_gu, v_ffn_w_down, v_final_norm_g):
    given = dict(locals())
    local = {n: given[n] for n in _WEIGHTS}
    px, py, pc = _place()

    first, gathering, zero = _gather_first(local)
    full = dict(local)
    full.update(first)
    full["norm_mix_g"] = local["norm_mix_g"] + zero
    big = [first]

    def weights_of(tag):
        def arrived(h):
            big.append(_gather_rest(big[-1], gathering[tag], h, tag))
            return big[-1]
        return arrived

    reducing = []

    def later_grads(g):
        pending, zero = _reduce_start(g, local, pc, lambda n, j: not _in_layer0(n, j), "later")
        reducing.append(pending)
        return zero

    def ffn0_grads(g):
        pending, zero = _reduce_start(g, local, pc, _is_ffn0, "ffn0")
        reducing.append(pending)
        return zero

    loss, grad_x, grads = local_step(x[0], loss_target[0], full, weights_of("later"), later_grads, ffn0_grads, weights_of("ffn0"))
    loss = lax.psum(loss, ("x", "y", "c"))

    small = _reduce_small_start(grads)
    red = _reduce_big(grads, local, reducing, grad_x, px, py, pc)
    red.update(_reduce_small(small, red["ffn_w_down"], grads, local, px, py, pc))

    deltas, new_m, new_v = {}, {}, {}
    for n in _WEIGHTS:
        deltas[n], new_m[n], new_v[n] = adamw(local[n], red[n], given["m_" + n], given["v_" + n], name=f"adamw_{n}")
    return (loss, grad_x[None], *[red[n] for n in _WEIGHTS], *[deltas[n] for n in _WEIGHTS],
            *[new_m[n] for n in _WEIGHTS], *[new_v[n] for n in _WEIGHTS])
```

```python
import functools
import math

import jax
import jax.numpy as jnp
from jax import lax
from jax.experimental import pallas as pl
from jax.experimental.pallas import tpu as pltpu

F32 = jnp.float32
BF16 = jnp.bfloat16

D = 1024
DEPTH = 4
CH = 64
EPS = 1e-6
GLA_H, GLA_DK, GLA_DV, GLA_RANK, GLA_TAU = 4, 128, 256, 16, 16.0
GLA_QK = GLA_H * GLA_DK
GLA_VD = GLA_H * GLA_DV
GLA_IN = 2 * GLA_QK + 2 * GLA_VD + GLA_RANK
GLA_INP = 3200
SSD_DI, SSD_HD, SSD_H, SSD_G, SSD_N, SSD_K = 2048, 64, 32, 8, 128, 4
SSD_GN = SSD_G * SSD_N
SSD_CONV = SSD_DI + 2 * SSD_GN
SSD_IN = SSD_DI + SSD_CONV + SSD_H
SSD_INP = 6272
S5_GS, S5_NG, S5_P = 16, 64, 64
S5_BLK = 8
FFN_H = 2816
LANE = 128
VMEM_LIMIT = 52 * 1024 * 1024
_MATMUL_VMEM = 40 * 1024 * 1024

ADAM_LR, ADAM_B1, ADAM_B2, ADAM_EPS, ADAM_WD, ADAM_STEP = 0.001, 0.9, 0.999, 1e-08, 0.01, 10

_ARB = "arbitrary"


def _cparams(n):
    return pltpu.CompilerParams(dimension_semantics=(_ARB,) * n, vmem_limit_bytes=VMEM_LIMIT)


def _pick(n, target, mult=LANE):
    best = None
    for c in range(mult, min(n, target) + 1, mult):
        if n % c == 0:
            best = c
    return best if best is not None else n


_DN = {"nn": (((1,), (0,)), ((), ())), "nt": (((1,), (1,)), ((), ())), "tn": (((0,), (0,)), ((), ()))}


def _dot_raw(a, b, form):
    return lax.dot_general(a.astype(BF16), b.astype(BF16), _DN[form], preferred_element_type=F32)


@functools.partial(jax.custom_vjp, nondiff_argnums=(2,))
def bdot(a, b, form):
    return _dot_raw(a, b, form)


def _bdot_fwd(a, b, form):
    return _dot_raw(a, b, form), (a, b)


def _bdot_bwd(form, res, g):
    a, b = res
    if form == "nn":
        return _dot_raw(g, b, "nt"), _dot_raw(a, g, "tn")
    if form == "nt":
        return _dot_raw(g, b, "nn"), _dot_raw(g, a, "tn")
    return _dot_raw(b, g, "nt"), _dot_raw(a, g, "nn")


bdot.defvjp(_bdot_fwd, _bdot_bwd)


def _hdot(a, b):
    return jnp.dot(a, b, precision=lax.Precision.HIGHEST, preferred_element_type=F32)


@jax.custom_vjp
def cdot_left(c, ct, x):
    return _hdot(c, x)


def _cdl_fwd(c, ct, x):
    return _hdot(c, x), (c, ct)


def _cdl_bwd(res, g):
    c, ct = res
    return jnp.zeros_like(c), jnp.zeros_like(ct), _hdot(ct, g)


cdot_left.defvjp(_cdl_fwd, _cdl_bwd)


@jax.custom_vjp
def cdot_right(x, c, ct):
    return _hdot(x, c)


def _cdr_fwd(x, c, ct):
    return _hdot(x, c), (c, ct)


def _cdr_bwd(res, g):
    c, ct = res
    return _hdot(g, ct), jnp.zeros_like(c), jnp.zeros_like(ct)


cdot_right.defvjp(_cdr_fwd, _cdr_bwd)


def _sigmoid(x):
    return 1.0 / (1.0 + jnp.exp(-x))


def _silu(x):
    return x * _sigmoid(x)


def _softplus(x):
    return jnp.maximum(x, 0.0) + jnp.log(1.0 + jnp.exp(-jnp.abs(x)))


def _log_sigmoid(x):
    return jnp.minimum(x, 0.0) - jnp.log(1.0 + jnp.exp(-jnp.abs(x)))


def _gelu(x):
    c = math.sqrt(2.0 / math.pi)
    return 0.5 * x * (1.0 + jnp.tanh(c * (x + 0.044715 * (x * x * x))))


def _rms(x, g):
    return x * lax.rsqrt(jnp.mean(x * x, axis=-1, keepdims=True) + EPS) * g


def _iota2(shape, axis):
    return lax.broadcasted_iota(jnp.int32, shape, axis)


def matmul(a, b, form, *, name, G=1, out_dtype=F32, add=None):
    isz = lambda t: jnp.dtype(t.dtype).itemsize
    osz = jnp.dtype(out_dtype).itemsize + (isz(add) if add is not None else 0)

    def fits(bm, bn, bk):
        return 2 * (bm * bk * isz(a) + bk * bn * isz(b) + bm * bn * osz) + 4 * bm * bn <= _MATMUL_VMEM

    if form in ("nn", "nt"):
        M = a.shape[0]
        K = a.shape[1] // G
        N = b.shape[2] if form == "nn" else b.shape[1]
        bm, bn, bk = min(M, 1024), _pick(N, 1536), _pick(K, 2048)
        while not fits(bm, bn, bk) and bk % 256 == 0:
            bk //= 2
        nj, nk = N // bn, K // bk
        grid = (G, M // bm, nj, nk)
        a_spec = pl.BlockSpec((bm, bk), lambda g, i, j, k: (i, g * nk + k))
        if form == "nn":
            b_spec = pl.BlockSpec((None, bk, bn), lambda g, i, j, k: (g, k, j))
        else:
            b_spec = pl.BlockSpec((None, bn, bk), lambda g, i, j, k: (g, j, k))
        o_spec = pl.BlockSpec((bm, bn), lambda g, i, j, k: (i, g * nj + j))
        out_shape = jax.ShapeDtypeStruct((M, G * N), out_dtype)
    else:
        T = a.shape[0]
        Ka, Nb = a.shape[1] // G, b.shape[1] // G
        bm, bn, bk = _pick(Ka, 1408), _pick(Nb, 1536), min(T, 2048)
        while not fits(bm, bn, bk) and bk % 512 == 0:
            bk //= 2
        ni, nj, nk = Ka // bm, Nb // bn, T // bk
        grid = (G, ni, nj, nk)
        a_spec = pl.BlockSpec((bk, bm), lambda g, i, j, k: (k, g * ni + i))
        b_spec = pl.BlockSpec((bk, bn), lambda g, i, j, k: (k, g * nj + j))
        o_spec = pl.BlockSpec((None, bm, bn), lambda g, i, j, k: (g, i, j))
        out_shape = jax.ShapeDtypeStruct((G, Ka, Nb), out_dtype)
    has_add = add is not None

    def finish(refs, r):
        if has_add:
            r = r + refs[2][...].astype(F32)
        o_ref = refs[3] if has_add else refs[2]
        o_ref[...] = r.astype(o_ref.dtype)

    def body_one(*refs):
        finish(refs, _dot_raw(refs[0][...], refs[1][...], form))

    def body_acc(*refs):
        acc_ref = refs[-1]
        k = pl.program_id(3)

        @pl.when(k == 0)
        def _():
            acc_ref[...] = jnp.zeros_like(acc_ref)

        acc_ref[...] += _dot_raw(refs[0][...], refs[1][...], form)

        @pl.when(k == nk - 1)
        def _():
            finish(refs, acc_ref[...])

    in_specs = [a_spec, b_spec]
    args = [a, b]
    if has_add:
        in_specs.append(o_spec)
        args.append(add)
    return pl.pallas_call(
        body_one if nk == 1 else body_acc, name=name, grid=grid, in_specs=in_specs, out_specs=o_spec,
        out_shape=out_shape, scratch_shapes=[] if nk == 1 else [pltpu.VMEM((bm, bn), F32)],
        compiler_params=_cparams(4),
    )(*args)


def matmul_nt_norm_bwd(a, w, h, g, d, *, name):
    T, K = a.shape
    bm = min(T, 512)
    bk = _pick(K, 2048)
    nk = K // bk

    def body(a_ref, w_ref, h_ref, g_ref, d_ref, dh_ref, dg_ref, acc_ref):
        i, k = pl.program_id(0), pl.program_id(1)

        @pl.when((i == 0) & (k == 0))
        def _():
            dg_ref[...] = jnp.zeros_like(dg_ref)

        @pl.when(k == 0)
        def _():
            acc_ref[...] = jnp.zeros_like(acc_ref)

        acc_ref[...] += _dot_raw(a_ref[...], w_ref[...], "nt")

        @pl.when(k == nk - 1)
        def _():
            _, vjp = jax.vjp(lambda g_, h_: _rms(h_, g_), g_ref[...], h_ref[...])
            dg, dh = vjp(acc_ref[...])
            dh_ref[...] = dh + d_ref[...]
            dg_ref[...] += dg

    row = pl.BlockSpec((bm, D), lambda i, k: (i, 0))
    one = pl.BlockSpec((1, D), lambda i, k: (0, 0))
    return pl.pallas_call(
        body, name=name, grid=(T // bm, nk),
        in_specs=[pl.BlockSpec((bm, bk), lambda i, k: (i, k)), pl.BlockSpec((D, bk), lambda i, k: (0, k)), row, one, row],
        out_specs=[row, one], out_shape=[jax.ShapeDtypeStruct((T, D), F32), jax.ShapeDtypeStruct((1, D), F32)],
        scratch_shapes=[pltpu.VMEM((bm, D), F32)], compiler_params=_cparams(2),
    )(a, w, h, g, d)


def ffn_up(hn, w_il, *, name):
    T = hn.shape[0]
    bm, hb = min(T, 512), FFN_H // 2

    def body(a_ref, b_ref, act_ref, gu_ref):
        r = _dot_raw(a_ref[...], b_ref[...], "nn")
        act_ref[...] = (_silu(r[:, :hb]) * r[:, hb:]).astype(act_ref.dtype)
        gu_ref[...] = r.astype(gu_ref.dtype)

    return pl.pallas_call(
        body, name=name, grid=(2, T // bm),
        in_specs=[pl.BlockSpec((bm, D), lambda j, i: (i, 0)), pl.BlockSpec((D, 2 * hb), lambda j, i: (0, j))],
        out_specs=[pl.BlockSpec((bm, hb), lambda j, i: (i, j)), pl.BlockSpec((bm, 2 * hb), lambda j, i: (i, j))],
        out_shape=[jax.ShapeDtypeStruct((T, FFN_H), BF16), jax.ShapeDtypeStruct((T, 2 * FFN_H), BF16)],
        compiler_params=_cparams(2),
    )(hn, w_il)


_DACT_CHUNK = 512


def ffn_dact(d, w_down, gu, *, name):
    T = d.shape[0]
    bm, hb = min(T, 512), FFN_H // 2

    def body(d_ref, w_ref, gu_ref, o_ref):
        d_blk = d_ref[...].astype(BF16)
        for lo in range(0, hb, _DACT_CHUNK):
            hi = min(lo + _DACT_CHUNK, hb)
            da = _dot_raw(d_blk, w_ref[lo:hi, :], "nt")
            g, u = gu_ref[:, lo:hi].astype(F32), gu_ref[:, hb + lo:hb + hi].astype(F32)
            sg = _sigmoid(g)
            o_ref[:, lo:hi] = (da * u * (sg * (1.0 + g * (1.0 - sg)))).astype(o_ref.dtype)
            o_ref[:, hb + lo:hb + hi] = (da * (g * sg)).astype(o_ref.dtype)

    return pl.pallas_call(
        body, name=name, grid=(2, T // bm),
        in_specs=[pl.BlockSpec((bm, D), lambda j, i: (i, 0)), pl.BlockSpec((hb, D), lambda j, i: (j, 0)),
                  pl.BlockSpec((bm, 2 * hb), lambda j, i: (i, j))],
        out_specs=pl.BlockSpec((bm, 2 * hb), lambda j, i: (i, j)),
        out_shape=jax.ShapeDtypeStruct((T, 2 * FFN_H), BF16), compiler_params=_cparams(2),
    )(d, w_down, gu)


def _row_entry(e):
    return e if isinstance(e, tuple) else (e, e.shape[1], 0)


def _row_spec(bt, e):
    _, width, idx = e
    return pl.BlockSpec((bt, width), lambda i: (i, idx))


def _full_spec(p):
    return pl.BlockSpec(p.shape, lambda i: (0,) * p.ndim)


def rowwise(f, params, rows, outs, *, bt, name):
    rows = [_row_entry(e) for e in rows]
    T = rows[0][0].shape[0]
    bt = min(bt, T)
    np_, nr = len(params), len(rows)

    def body(*refs):
        p = tuple(r[...].astype(F32) for r in refs[:np_])
        rw = tuple(r[...].astype(F32) for r in refs[np_:np_ + nr])
        res = f(p, rw)
        for o_ref, o in zip(refs[np_ + nr:], res):
            o_ref[...] = o.astype(o_ref.dtype)

    res = pl.pallas_call(
        body, name=name, grid=(T // bt,),
        in_specs=[_full_spec(p) for p in params] + [_row_spec(bt, e) for e in rows],
        out_specs=[pl.BlockSpec((bt, w), lambda i: (i, 0)) for w, _ in outs],
        out_shape=[jax.ShapeDtypeStruct((T, w), dt) for w, dt in outs],
        compiler_params=_cparams(1),
    )(*params, *[e[0] for e in rows])
    return list(res)


def rowwise_vjp(f, params, rows, cts, drow_dtypes, *, bt, name):
    rows = [_row_entry(e) for e in rows]
    cts = [_row_entry(e) for e in cts]
    T = rows[0][0].shape[0]
    bt = min(bt, T)
    np_, nr, nc = len(params), len(rows), len(cts)
    want = [i for i, dt in enumerate(drow_dtypes) if dt is not None]

    def body(*refs):
        p = tuple(r[...].astype(F32) for r in refs[:np_])
        rw = tuple(r[...].astype(F32) for r in refs[np_:np_ + nr])
        ct = tuple(r[...].astype(F32) for r in refs[np_ + nr:np_ + nr + nc])
        outs = refs[np_ + nr + nc:]
        _, vjp = jax.vjp(f, p, rw)
        dp, dr = vjp(ct)

        @pl.when(pl.program_id(0) == 0)
        def _():
            for o in outs[:np_]:
                o[...] = jnp.zeros_like(o)

        for o, d in zip(outs[:np_], dp):
            o[...] += d
        for o, i in zip(outs[np_:], want):
            o[...] = dr[i].astype(o.dtype)

    res = pl.pallas_call(
        body, name=name, grid=(T // bt,),
        in_specs=[_full_spec(p) for p in params] + [_row_spec(bt, e) for e in rows] + [_row_spec(bt, e) for e in cts],
        out_specs=[_full_spec(p) for p in params] + [pl.BlockSpec((bt, rows[i][1]), lambda i_: (i_, 0)) for i in want],
        out_shape=[jax.ShapeDtypeStruct(p.shape, F32) for p in params]
        + [jax.ShapeDtypeStruct((T, rows[i][1]), drow_dtypes[i]) for i in want],
        compiler_params=_cparams(1),
    )(*params, *[e[0] for e in rows], *[e[0] for e in cts])
    res = list(res)
    return res[:np_], res[np_:]


def f_rmsnorm(p, r):
    return (_rms(r[0], p[0]),)


def f_rmsnorm_res(p, r):
    return (_rms(r[0], p[0]), r[0])


def f_swiglu(p, r):
    gu = r[0]
    return (_silu(gu[:, :FFN_H]) * gu[:, FFN_H:],)


def f_gla_gate_in(p, r):
    w_a2, b_a = p
    z = bdot(r[0], w_a2, "nn") + b_a
    return (_log_sigmoid(z) / GLA_TAU,)


def f_gla_gate_in_fwd(p, r):
    w_a2, b_a = p
    z = _dot_raw(r[0], w_a2, "nn") + b_a
    return (_log_sigmoid(z) / GLA_TAU,)


def f_gla_out(p, r):
    (ng,) = p
    o, rr = r
    parts = []
    for h in range(GLA_H):
        sl = slice(h * GLA_DV, (h + 1) * GLA_DV)
        parts.append(_rms(o[:, sl], ng[:, sl]) * _silu(rr[:, sl]))
    return (jnp.concatenate(parts, axis=1),)


def f_ssd_out(p, r):
    (ng,) = p
    y, z = r
    t = y * _silu(z)
    gsz = SSD_DI // SSD_G
    parts = []
    for g in range(SSD_G):
        sl = slice(g * gsz, (g + 1) * gsz)
        parts.append(_rms(t[:, sl], ng[:, sl]))
    return (jnp.concatenate(parts, axis=1),)


def f_s5_act(p, r):
    (dsk,) = p
    ycp, u = r
    return (_gelu(ycp + dsk * u),)


def f_glu_res(p, r):
    vg, h = r
    return (vg[:, :D] * _sigmoid(vg[:, D:]) + h,)


def f_glu(p, r):
    vg = r[0]
    return (vg[:, :D] * _sigmoid(vg[:, D:]),)


def loss_head(h, g, target, *, name):
    T = h.shape[0]
    bt = min(T, 256)

    def lossf(g_, h_, t_):
        e = _rms(h_, g_) - t_
        return (0.5 / D) * jnp.sum(e * e)

    def body(g_ref, h_ref, t_ref, loss_ref, dg_ref, dh_ref):
        @pl.when(pl.program_id(0) == 0)
        def _():
            loss_ref[...] = jnp.zeros_like(loss_ref)
            dg_ref[...] = jnp.zeros_like(dg_ref)

        val, vjp = jax.vjp(lossf, g_ref[...], h_ref[...], t_ref[...])
        dg, dh, _ = vjp(jnp.ones((), F32))
        loss_ref[...] += jnp.full(loss_ref.shape, val, F32)
        dg_ref[...] += dg
        dh_ref[...] = dh

    row = pl.BlockSpec((bt, D), lambda i: (i, 0))
    one = pl.BlockSpec((1, D), lambda i: (0, 0))
    loss, dg, dh = pl.pallas_call(
        body, name=name, grid=(T // bt,), in_specs=[one, row, row],
        out_specs=[pl.BlockSpec((1, LANE), lambda i: (0, 0)), one, row],
        out_shape=[jax.ShapeDtypeStruct((1, LANE), F32), jax.ShapeDtypeStruct((1, D), F32),
                   jax.ShapeDtypeStruct((T, D), F32)],
        compiler_params=_cparams(1),
    )(g, h, target)
    return loss[0, 0], dg, dh


def _gla_consts():
    r, c = _iota2((CH, CH), 0), _iota2((CH, CH), 1)
    return (r >= c).astype(F32), (r <= c).astype(F32), r >= c


def _gla_chunk(q, k, v, la, st, consts, dot, cdl):
    L, Lt, tril = consts
    lc = cdl(L, Lt, la)
    lend = lc[CH - 1:CH, :]
    e, ei = jnp.exp(lc), jnp.exp(-lc)
    qs = q * (GLA_DK ** -0.5)
    qf, kf, qb, kb = qs * e, k * ei, qs * ei, k * e
    sc = jnp.where(tril, dot(qf, kf, "nt"), dot(qb, kb, "nt"))
    o = dot(sc, v, "nn") + dot(qf, st, "nt")
    kd = k * jnp.exp(lend - lc)
    st_new = st * jnp.exp(lend) + dot(v, kd, "tn")
    return o, st_new


def _gla_block(q, k, v, la, st, nc, dot, cdl):
    consts = _gla_consts()
    outs = []
    for c in range(nc):
        sl = slice(c * CH, (c + 1) * CH)
        o, st = _gla_chunk(q[sl], k[sl], v[sl], la[sl], st, consts, dot, cdl)
        outs.append(o)
    return jnp.concatenate(outs, axis=0), st


_GLA_HP = 2


def _gla_specs(rows, rev, nb):
    t = (lambda j: nb - 1 - j) if rev else (lambda j: j)
    hp, ng = _GLA_HP, GLA_H // _GLA_HP
    q = pl.BlockSpec((rows, hp * GLA_DK), lambda h, j: (t(j), h))
    k = pl.BlockSpec((rows, hp * GLA_DK), lambda h, j: (t(j), ng + h))
    v = pl.BlockSpec((rows, hp * GLA_DV), lambda h, j: (t(j), ng + h))
    la = pl.BlockSpec((rows, hp * GLA_DK), lambda h, j: (t(j), h))
    ss = pl.BlockSpec((None, hp, GLA_DV, GLA_DK), lambda h, j: (t(j), h, 0, 0))
    o = pl.BlockSpec((rows, hp * GLA_DV), lambda h, j: (t(j), h))
    r = pl.BlockSpec((rows, hp * GLA_DV), lambda h, j: (t(j), 2 * ng + h))
    g = pl.BlockSpec((1, hp * GLA_DV), lambda h, j: (0, h))
    return q, k, v, la, ss, o, r, g


def _gla_heads(q, k, v, la, r, ng, sts, nc, dot, cdl):
    outs, new = [], []
    for i in range(_GLA_HP):
        kk, vv = slice(i * GLA_DK, (i + 1) * GLA_DK), slice(i * GLA_DV, (i + 1) * GLA_DV)
        o, st = _gla_block(q[:, kk], k[:, kk], v[:, vv], la[:, kk], sts[i], nc, dot, cdl)
        outs.append(_rms(o, ng[:, vv]) * _silu(r[:, vv]))
        new.append(st)
    return jnp.concatenate(outs, axis=1), tuple(new)


def gla_scan_fwd(proj, la, ng, *, nc, name):
    T = proj.shape[0]
    rows = min(T, nc * CH)
    nc = rows // CH
    nb = T // rows
    q_s, k_s, v_s, la_s, ss_s, o_s, r_s, g_s = _gla_specs(rows, False, nb)

    def body(q_ref, k_ref, v_ref, la_ref, r_ref, g_ref, o_ref, ss_ref, st_ref):
        @pl.when(pl.program_id(1) == 0)
        def _():
            st_ref[...] = jnp.zeros_like(st_ref)

        ss_ref[...] = st_ref[...]
        sts = tuple(st_ref[i] for i in range(_GLA_HP))
        o, sts = _gla_heads(q_ref[...], k_ref[...], v_ref[...], la_ref[...], r_ref[...], g_ref[...], sts, nc,
                            _dot_raw, lambda c, ct, x: _hdot(c, x))
        o_ref[...] = o.astype(o_ref.dtype)
        for i in range(_GLA_HP):
            st_ref[i] = sts[i]

    return pl.pallas_call(
        body, name=name, grid=(GLA_H // _GLA_HP, nb), in_specs=[q_s, k_s, v_s, la_s, r_s, g_s], out_specs=[o_s, ss_s],
        out_shape=[jax.ShapeDtypeStruct((T, GLA_VD), BF16), jax.ShapeDtypeStruct((nb, GLA_H, GLA_DV, GLA_DK), F32)],
        scratch_shapes=[pltpu.VMEM((_GLA_HP, GLA_DV, GLA_DK), F32)], compiler_params=_cparams(2),
    )(proj, proj, proj, la, proj, ng)


def gla_scan_bwd(proj, la, ng, ss, do, *, nc, name):
    T = proj.shape[0]
    rows = min(T, nc * CH)
    nc = rows // CH
    nb = T // rows
    q_s, k_s, v_s, la_s, ss_s, o_s, r_s, g_s = _gla_specs(rows, True, nb)
    t = lambda j: nb - 1 - j
    dqk_s = pl.BlockSpec((rows, _GLA_HP * GLA_DK), lambda h, j: (t(j), h))

    def body(q_ref, k_ref, v_ref, la_ref, r_ref, g_ref, ss_ref, do_ref,
             dq_ref, dk_ref, dv_ref, dla_ref, dr_ref, dg_ref, dst_ref):
        @pl.when(pl.program_id(1) == 0)
        def _():
            dst_ref[...] = jnp.zeros_like(dst_ref)
            dg_ref[...] = jnp.zeros_like(dg_ref)

        fn = lambda q, k, v, la_, r, g, *sts: _gla_heads(q, k, v, la_, r, g, sts, nc, bdot, cdot_left)
        _, vjp = jax.vjp(fn, q_ref[...], k_ref[...], v_ref[...], la_ref[...], r_ref[...], g_ref[...],
                         *[ss_ref[i] for i in range(_GLA_HP)])
        dq, dk, dv, dla, dr, dg, *dsts = vjp((do_ref[...], tuple(dst_ref[i] for i in range(_GLA_HP))))
        dq_ref[...] = dq.astype(dq_ref.dtype)
        dk_ref[...] = dk.astype(dk_ref.dtype)
        dv_ref[...] = dv.astype(dv_ref.dtype)
        dla_ref[...] = dla
        dr_ref[...] = dr.astype(dr_ref.dtype)
        dg_ref[...] += dg
        for i in range(_GLA_HP):
            dst_ref[i] = dsts[i]

    return pl.pallas_call(
        body, name=name, grid=(GLA_H // _GLA_HP, nb), in_specs=[q_s, k_s, v_s, la_s, r_s, g_s, ss_s, o_s],
        out_specs=[dqk_s, dqk_s, o_s, dqk_s, o_s, g_s],
        out_shape=[jax.ShapeDtypeStruct((T, GLA_QK), BF16), jax.ShapeDtypeStruct((T, GLA_QK), BF16),
                   jax.ShapeDtypeStruct((T, GLA_VD), BF16), jax.ShapeDtypeStruct((T, GLA_QK), F32),
                   jax.ShapeDtypeStruct((T, GLA_VD), BF16), jax.ShapeDtypeStruct((1, GLA_VD), F32)],
        scratch_shapes=[pltpu.VMEM((_GLA_HP, GLA_DV, GLA_DK), F32)], compiler_params=_cparams(2),
    )(proj, proj, proj, la, proj, ng, ss, do)


_CONV_W = 512
_CONV_OFF = SSD_DI // _CONV_W


def _conv_pre(x, prev8, w_ref, b_ref):
    bt = x.shape[0]
    ext = jnp.concatenate([prev8, x], axis=0)
    shifted = []
    for j in range(SSD_K):
        s = SSD_K - 1 - j
        shifted.append(x if s == 0 else pltpu.roll(ext, s, 0)[8:8 + bt])
    pre = b_ref[...] + sum(w_ref[j:j + 1, :] * shifted[j] for j in range(SSD_K))
    return pre, shifted


def ssd_conv_fwd(proj, w, b, *, name):
    T = proj.shape[0]
    bt = min(T, 512)
    nb = T // bt

    def body(x_ref, w_ref, b_ref, o_ref, carry_ref):
        @pl.when(pl.program_id(1) == 0)
        def _():
            carry_ref[...] = jnp.zeros_like(carry_ref)

        x = x_ref[...]
        pre, _ = _conv_pre(x, carry_ref[...], w_ref, b_ref)
        o_ref[...] = _silu(pre)
        carry_ref[...] = x[bt - 8:, :]

    return pl.pallas_call(
        body, name=name, grid=(SSD_CONV // _CONV_W, nb),
        in_specs=[pl.BlockSpec((bt, _CONV_W), lambda c, t: (t, _CONV_OFF + c)),
                  pl.BlockSpec((SSD_K, _CONV_W), lambda c, t: (0, c)),
                  pl.BlockSpec((1, _CONV_W), lambda c, t: (0, c))],
        out_specs=pl.BlockSpec((bt, _CONV_W), lambda c, t: (t, c)),
        out_shape=jax.ShapeDtypeStruct((T, SSD_CONV), F32),
        scratch_shapes=[pltpu.VMEM((8, _CONV_W), F32)], compiler_params=_cparams(2),
    )(proj, w, b)


def ssd_conv_bwd(proj, w, b, dout, *, name):
    T = proj.shape[0]
    bt = min(T, 512)
    nb = T // bt
    r8 = bt // 8

    def body(x_ref, xp_ref, w_ref, b_ref, do_ref, dx_ref, dw_ref, db_ref, carry_ref):
        t = pl.program_id(1)

        @pl.when(t == 0)
        def _():
            carry_ref[...] = jnp.zeros_like(carry_ref)
            dw_ref[...] = jnp.zeros_like(dw_ref)
            db_ref[...] = jnp.zeros_like(db_ref)

        x = x_ref[...]
        prev8 = jnp.where(t == nb - 1, 0.0, xp_ref[...])
        pre, shifted = _conv_pre(x, prev8, w_ref, b_ref)
        sg = _sigmoid(pre)
        dpre = do_ref[...] * (sg * (1.0 + pre * (1.0 - sg)))
        ext = jnp.concatenate([dpre, carry_ref[...]], axis=0)
        dx = w_ref[SSD_K - 1:SSD_K, :] * dpre
        for j in range(SSD_K - 1):
            s = SSD_K - 1 - j
            dx = dx + w_ref[j:j + 1, :] * pltpu.roll(ext, bt + 8 - s, 0)[:bt]
        dx_ref[...] = dx.astype(dx_ref.dtype)
        dw_ref[...] += jnp.concatenate([jnp.sum(dpre * shifted[j], axis=0, keepdims=True) for j in range(SSD_K)], axis=0)
        db_ref[...] += jnp.sum(dpre, axis=0, keepdims=True)
        carry_ref[...] = dpre[:8, :]

    rt = lambda t: nb - 1 - t
    return pl.pallas_call(
        body, name=name, grid=(SSD_CONV // _CONV_W, nb),
        in_specs=[pl.BlockSpec((bt, _CONV_W), lambda c, t: (rt(t), _CONV_OFF + c)),
                  pl.BlockSpec((8, _CONV_W), lambda c, t: (jnp.maximum(rt(t) * r8 - 1, 0), _CONV_OFF + c)),
                  pl.BlockSpec((SSD_K, _CONV_W), lambda c, t: (0, c)),
                  pl.BlockSpec((1, _CONV_W), lambda c, t: (0, c)),
                  pl.BlockSpec((bt, _CONV_W), lambda c, t: (rt(t), c))],
        out_specs=[pl.BlockSpec((bt, _CONV_W), lambda c, t: (rt(t), c)),
                   pl.BlockSpec((SSD_K, _CONV_W), lambda c, t: (0, c)),
                   pl.BlockSpec((1, _CONV_W), lambda c, t: (0, c))],
        out_shape=[jax.ShapeDtypeStruct((T, SSD_CONV), BF16), jax.ShapeDtypeStruct((SSD_K, SSD_CONV), F32),
                   jax.ShapeDtypeStruct((1, SSD_CONV), F32)],
        scratch_shapes=[pltpu.VMEM((8, _CONV_W), F32)], compiler_params=_cparams(2),
    )(proj, proj, w, b, dout)


_SSD_U = 2 * CH


def _ssd_unit(xs, bm, cm, dtraw, dtb, alog, dsk, hp, g, dot, cdl, cdr):
    U, P2 = _SSD_U, 2 * SSD_HD
    r, c = _iota2((U, U), 0), _iota2((U, U), 1)
    same = (r // CH) == (c // CH)
    Lb = (same & (r >= c)).astype(F32)
    Ub = (same & (r <= c)).astype(F32)
    lane = _iota2((1, U), 1)
    lo_lane = _iota2((1, P2), 1) < SSD_HD
    lo_sub = _iota2((P2, 1), 0) < SSD_HD
    diag2 = (_iota2((CH, P2), 0) == (_iota2((CH, P2), 1) % CH)).astype(F32)

    dt = _softplus(dtraw + dtb)
    da = dt * (-jnp.exp(alog))
    cum = cdl(Lb, Ub, da)
    ys = []
    new_hp = []
    for pr in range(2):
        xs_p = xs[:, pr * P2:(pr + 1) * P2]
        cols, dts, dks = [], [], []
        for jj in range(2):
            oh_l = (lane == g * (SSD_H // SSD_G) + 2 * pr + jj).astype(F32)
            cols.append(jnp.sum(cum * oh_l, axis=1, keepdims=True))
            dts.append(jnp.sum(dt * oh_l, axis=1, keepdims=True))
            dks.append(jnp.sum(dsk * oh_l, axis=1, keepdims=True))
        dsk_p = jnp.where(lo_lane, dks[0], dks[1])
        h = hp[pr]
        yc = []
        for ci in range(2):
            sl = slice(ci * CH, (ci + 1) * CH)
            xs_c, bm_c, cm_c = xs_p[sl], bm[sl], cm[sl]
            col = jnp.where(lo_lane, cols[0][sl], cols[1][sl])
            dtc = jnp.where(lo_lane, dts[0][sl], dts[1][sl])
            row = jnp.sum(diag2 * col, axis=0, keepdims=True)
            dtrow = jnp.sum(diag2 * dtc, axis=0, keepdims=True)
            cb = dot(cm_c, jnp.concatenate([bm_c, bm_c], axis=0), "nt")
            mix = cb * jnp.exp(-jnp.abs(col - row)) * dtrow
            xbd = jnp.concatenate([jnp.where(lo_lane, xs_c, 0.0), jnp.where(lo_lane, 0.0, xs_c)], axis=0)
            y_intra = dot(mix, xbd, "nn")
            ce = jnp.where(lo_lane, cols[0][ci * CH + CH - 1:ci * CH + CH, :], cols[1][ci * CH + CH - 1:ci * CH + CH, :])
            y_inter = dot(cm_c, h, "nt") * jnp.exp(col)
            xw = xs_c * (dtc * jnp.exp(ce - col))
            ce_s = [cols[jj][ci * CH + CH - 1:ci * CH + CH, :] for jj in range(2)]
            a_p = jnp.where(lo_sub, jnp.exp(ce_s[0]), jnp.exp(ce_s[1]))
            h = a_p * h + dot(xw, bm_c, "tn")
            yc.append(y_intra + y_inter + dsk_p * xs_c)
        ys.append(jnp.concatenate(yc, axis=0))
        new_hp.append(h)
    return jnp.concatenate(ys, axis=1), tuple(new_hp)


def _ssd_block(xs, bm, cm, dtraw, z, dtb, alog, dsk, ng, hp, g, nu, dot, cdl, cdr):
    outs = []
    for u in range(nu):
        sl = slice(u * _SSD_U, (u + 1) * _SSD_U)
        y, hp = _ssd_unit(xs[sl], bm[sl], cm[sl], dtraw[sl], dtb, alog, dsk, hp, g, dot, cdl, cdr)
        outs.append(y)
    return _rms(jnp.concatenate(outs, axis=0) * _silu(z), ng), hp


def _ssd_specs(rows, rev, nb):
    t = (lambda j: nb - 1 - j) if rev else (lambda j: j)
    gw = SSD_DI // SSD_G
    xs = pl.BlockSpec((rows, gw), lambda j, g: (t(j), g))
    bm = pl.BlockSpec((rows, SSD_N), lambda j, g: (t(j), SSD_DI // SSD_N + g))
    cm = pl.BlockSpec((rows, SSD_N), lambda j, g: (t(j), (SSD_DI + SSD_GN) // SSD_N + g))
    dtr = pl.BlockSpec((rows, LANE), lambda j, g: (t(j), (SSD_DI + SSD_CONV) // LANE))
    par = pl.BlockSpec((1, LANE), lambda j, g: (0, 0))
    hs = pl.BlockSpec((None, None, 2, 2 * SSD_HD, SSD_N), lambda j, g: (t(j), g, 0, 0, 0))
    y = pl.BlockSpec((rows, gw), lambda j, g: (t(j), g))
    ng = pl.BlockSpec((1, gw), lambda j, g: (0, g))
    return xs, bm, cm, dtr, par, hs, y, ng


def ssd_scan_fwd(xbc, proj, dtb, alog, dsk, ng, *, nu, name):
    T = xbc.shape[0]
    rows = min(T, nu * _SSD_U)
    nu = rows // _SSD_U
    nb = T // rows
    xs_s, bm_s, cm_s, dt_s, par_s, hs_s, y_s, ng_s = _ssd_specs(rows, False, nb)

    def body(xs_ref, bm_ref, cm_ref, dt_ref, z_ref, dtb_ref, al_ref, dsk_ref, ng_ref, y_ref, hs_ref, h_ref):
        g = pl.program_id(1)

        @pl.when(pl.program_id(0) == 0)
        def _():
            h_ref[g] = jnp.zeros(h_ref.shape[1:], F32)

        hs_ref[...] = h_ref[g]
        hp = (h_ref[g, 0], h_ref[g, 1])
        y, hp = _ssd_block(xs_ref[...], bm_ref[...], cm_ref[...], dt_ref[...], z_ref[...], dtb_ref[...], al_ref[...],
                           dsk_ref[...], ng_ref[...], hp, g, nu, _dot_raw, lambda c, ct, x: _hdot(c, x), lambda x, c, ct: _hdot(x, c))
        y_ref[...] = y.astype(y_ref.dtype)
        h_ref[g, 0] = hp[0]
        h_ref[g, 1] = hp[1]

    return pl.pallas_call(
        body, name=name, grid=(nb, SSD_G), in_specs=[xs_s, bm_s, cm_s, dt_s, y_s, par_s, par_s, par_s, ng_s],
        out_specs=[y_s, hs_s],
        out_shape=[jax.ShapeDtypeStruct((T, SSD_DI), BF16), jax.ShapeDtypeStruct((nb, SSD_G, 2, 2 * SSD_HD, SSD_N), F32)],
        scratch_shapes=[pltpu.VMEM((SSD_G, 2, 2 * SSD_HD, SSD_N), F32)], compiler_params=_cparams(2),
    )(xbc, xbc, xbc, proj, proj, dtb, alog, dsk, ng)


def ssd_scan_bwd(xbc, proj, dtb, alog, dsk, ng, hs, dy, *, nu, name):
    T = xbc.shape[0]
    rows = min(T, nu * _SSD_U)
    nu = rows // _SSD_U
    nb = T // rows
    xs_s, bm_s, cm_s, dt_s, par_s, hs_s, y_s, ng_s = _ssd_specs(rows, True, nb)
    gw = SSD_DI // SSD_G
    dng_s = pl.BlockSpec((1, SSD_DI), lambda j, g: (0, 0))
    t = lambda j: nb - 1 - j
    n_s = pl.BlockSpec((rows, SSD_N), lambda j, g: (t(j), g))
    ddt_s = pl.BlockSpec((rows, LANE), lambda j, g: (t(j), 0))

    def body(xs_ref, bm_ref, cm_ref, dt_ref, z_ref, dtb_ref, al_ref, dsk_ref, ng_ref, hs_ref, dy_ref,
             dxs_ref, dbm_ref, dcm_ref, ddt_ref, ddtb_ref, dal_ref, ddsk_ref, dz_ref, dng_ref, dh_ref):
        j, g = pl.program_id(0), pl.program_id(1)

        @pl.when(j == 0)
        def _():
            dh_ref[g] = jnp.zeros(dh_ref.shape[1:], F32)

        @pl.when((j == 0) & (g == 0))
        def _():
            ddtb_ref[...] = jnp.zeros_like(ddtb_ref)
            dal_ref[...] = jnp.zeros_like(dal_ref)
            ddsk_ref[...] = jnp.zeros_like(ddsk_ref)
            dng_ref[...] = jnp.zeros_like(dng_ref)

        @pl.when(g == 0)
        def _():
            ddt_ref[...] = jnp.zeros_like(ddt_ref)

        fn = lambda xs, bm, cm, dtr, z, dtb_, al, dsk_, ng_, h0, h1: _ssd_block(
            xs, bm, cm, dtr, z, dtb_, al, dsk_, ng_, (h0, h1), g, nu, bdot, cdot_left, cdot_right)
        _, vjp = jax.vjp(fn, xs_ref[...], bm_ref[...], cm_ref[...], dt_ref[...], z_ref[...], dtb_ref[...], al_ref[...],
                         dsk_ref[...], ng_ref[...], hs_ref[0], hs_ref[1])
        dxs, dbm, dcm, ddt, dz, ddtb, dal, ddsk, dng, dh0, dh1 = vjp((dy_ref[...], (dh_ref[g, 0], dh_ref[g, 1])))
        dz_ref[...] = dz.astype(dz_ref.dtype)
        lanes = pl.ds(pl.multiple_of(g * gw, gw), gw)
        dng_ref[:, lanes] = dng_ref[:, lanes] + dng
        dxs_ref[...] = dxs
        dbm_ref[...] = dbm
        dcm_ref[...] = dcm
        ddt_ref[...] += ddt
        ddtb_ref[...] += ddtb
        dal_ref[...] += dal
        ddsk_ref[...] += ddsk
        dh_ref[g, 0] = dh0
        dh_ref[g, 1] = dh1

    return pl.pallas_call(
        body, name=name, grid=(nb, SSD_G), in_specs=[xs_s, bm_s, cm_s, dt_s, y_s, par_s, par_s, par_s, ng_s, hs_s, y_s],
        out_specs=[y_s, n_s, n_s, ddt_s, par_s, par_s, par_s, y_s, dng_s],
        out_shape=[jax.ShapeDtypeStruct((T, SSD_DI), F32), jax.ShapeDtypeStruct((T, SSD_GN), F32),
                   jax.ShapeDtypeStruct((T, SSD_GN), F32), jax.ShapeDtypeStruct((T, LANE), F32),
                   jax.ShapeDtypeStruct((1, LANE), F32), jax.ShapeDtypeStruct((1, LANE), F32),
                   jax.ShapeDtypeStruct((1, LANE), F32), jax.ShapeDtypeStruct((T, SSD_DI), BF16),
                   jax.ShapeDtypeStruct((1, SSD_DI), F32)],
        scratch_shapes=[pltpu.VMEM((SSD_G, 2, 2 * SSD_HD, SSD_N), F32)], compiler_params=_cparams(2),
    )(xbc, xbc, xbc, proj, proj, dtb, alog, dsk, ng, hs, dy)


def _s5_param_f(log_dt, a_re, a_im, bre_t, bim_t, cim, cdl):
    n = S5_NG * S5_GS
    r, c = _iota2((n, S5_NG), 0), _iota2((n, S5_NG), 1)
    E = ((r // S5_GS) == c).astype(F32)
    rt, ct = _iota2((S5_NG, n), 0), _iota2((S5_NG, n), 1)
    Et = ((ct // S5_GS) == rt).astype(F32)
    step = jnp.exp(log_dt)
    mag = jnp.exp(step * a_re)
    abr = mag * jnp.cos(step * a_im)
    abi = mag * jnp.sin(step * a_im)
    den = a_re * a_re + a_im * a_im
    nr, ni = abr - 1.0, abi
    fr = (nr * a_re + ni * a_im) / den
    fi = (ni * a_re - nr * a_im) / den
    Fr, Fi = cdl(E, Et, fr), cdl(E, Et, fi)
    bbr = Fr * bre_t - Fi * bim_t
    bbi = Fr * bim_t + Fi * bre_t
    return abr, abi, bbr, bbi, -cim


def _whole(a):
    return pl.BlockSpec(a.shape, lambda: (0,) * a.ndim)


def s5_param_fwd(args, *, name):
    def body(*refs):
        res = _s5_param_f(*[r[...] for r in refs[:6]], lambda c, ct, x: _hdot(c, x))
        for o, v in zip(refs[6:], res):
            o[...] = v

    shapes = [(S5_NG, S5_P), (S5_NG, S5_P)] + [(S5_NG * S5_GS, S5_P)] * 3
    return pl.pallas_call(
        body, name=name, in_specs=[_whole(a) for a in args], out_specs=[pl.BlockSpec(s, lambda: (0, 0)) for s in shapes],
        out_shape=[jax.ShapeDtypeStruct(s, F32) for s in shapes],
        compiler_params=pltpu.CompilerParams(vmem_limit_bytes=VMEM_LIMIT),
    )(*args)


def s5_param_bwd(args, cts, *, name):
    def body(*refs):
        fn = lambda *a: _s5_param_f(*a, cdot_left)
        _, vjp = jax.vjp(fn, *[r[...] for r in refs[:6]])
        grads = vjp(tuple(r[...] for r in refs[6:11]))
        for o, v in zip(refs[11:], grads):
            o[...] = v

    return pl.pallas_call(
        body, name=name, in_specs=[_whole(a) for a in list(args) + list(cts)],
        out_specs=[_whole(a) for a in args], out_shape=[jax.ShapeDtypeStruct(a.shape, F32) for a in args],
        compiler_params=pltpu.CompilerParams(vmem_limit_bytes=VMEM_LIMIT),
    )(*args, *cts)


_S5_W = S5_BLK * S5_P


def _cmul_add(xr, xi, pr, pi, sr, si):
    return xr + (pr * sr - pi * si), xi + (pr * si + pi * sr)


def _s5_powers(ar, ai):
    pw = [(ar, ai)]
    for _ in range(7):
        qr, qi = pw[-1]
        pw.append((qr * ar - qi * ai, qr * ai + qi * ar))
    return pw


def s5_scan_fwd(u, wb, a_re, a_im, wc, *, name):
    T = u.shape[0]
    bt = min(T, 256)
    nb = T // bt

    def body(u_ref, wb_ref, ar_ref, ai_ref, wc_ref, x_ref, y_ref, bu_ref, carry_ref):
        @pl.when(pl.program_id(1) == 0)
        def _():
            carry_ref[...] = jnp.zeros_like(carry_ref)

        bu_ref[...] = _dot_raw(u_ref[...], wb_ref[...], "nn")
        ar, ai = ar_ref[...], ai_ref[...]
        pw = _s5_powers(ar, ai)
        pwr = jnp.concatenate([p[0] for p in pw], axis=0)
        pwi = jnp.concatenate([p[1] for p in pw], axis=0)
        rin = _iota2((8, _S5_W), 0)
        cr, ci = carry_ref[0:1, :], carry_ref[1:2, :]
        for t in range(bt // 8):
            sl = slice(8 * t, 8 * t + 8)
            xr, xi = bu_ref[sl, :_S5_W], bu_ref[sl, _S5_W:]
            for s in (1, 2, 4):
                m = rin >= s
                sr = jnp.where(m, pltpu.roll(xr, s, 0), 0.0)
                si = jnp.where(m, pltpu.roll(xi, s, 0), 0.0)
                xr, xi = _cmul_add(xr, xi, *pw[s - 1], sr, si)
            xr, xi = _cmul_add(xr, xi, pwr, pwi, cr, ci)
            x_ref[sl, :_S5_W] = xr
            x_ref[sl, _S5_W:] = xi
            cr, ci = xr[7:8, :], xi[7:8, :]
        carry_ref[0:1, :] = cr
        carry_ref[1:2, :] = ci
        y_ref[...] = _dot_raw(x_ref[...], wc_ref[...], "nn")

    nblk = S5_NG // S5_BLK
    blk = pl.BlockSpec((bt, 2 * _S5_W), lambda g, t: (t, g))
    col = pl.BlockSpec((bt, LANE), lambda g, t: (t, g))
    a_s = pl.BlockSpec((None, 1, _S5_W), lambda g, t: (g, 0, 0))
    wb_s = pl.BlockSpec((None, LANE, 2 * _S5_W), lambda g, t: (g, 0, 0))
    wc_s = pl.BlockSpec((None, 2 * _S5_W, LANE), lambda g, t: (g, 0, 0))
    return pl.pallas_call(
        body, name=name, grid=(nblk, nb), in_specs=[col, wb_s, a_s, a_s, wc_s], out_specs=[blk, col],
        out_shape=[jax.ShapeDtypeStruct((T, nblk * 2 * _S5_W), F32), jax.ShapeDtypeStruct((T, nblk * LANE), F32)],
        scratch_shapes=[pltpu.VMEM((bt, 2 * _S5_W), F32), pltpu.VMEM((8, _S5_W), F32)],
        compiler_params=_cparams(2),
    )(u, wb, a_re, a_im, wc)


def s5_scan_bwd(dy, x, u, wb, a_re, a_im, wc, *, name):
    T = dy.shape[0]
    bt = min(T, 256)
    nb = T // bt

    def body(dy_ref, x_ref, u_ref, wb_ref, ar_ref, ai_ref, wc_ref, du_ref, dwb_ref, dwc_ref, dar_ref, dai_ref,
             g_ref, lam_ref, carry_ref):
        @pl.when(pl.program_id(1) == 0)
        def _():
            carry_ref[...] = jnp.zeros_like(carry_ref)
            dar_ref[...] = jnp.zeros_like(dar_ref)
            dai_ref[...] = jnp.zeros_like(dai_ref)
            dwb_ref[...] = jnp.zeros_like(dwb_ref)
            dwc_ref[...] = jnp.zeros_like(dwc_ref)

        g_ref[...] = _dot_raw(dy_ref[...], wc_ref[...], "nt")
        pw = _s5_powers(ar_ref[...], -ai_ref[...])
        pwr = jnp.concatenate([p[0] for p in reversed(pw)], axis=0)
        pwi = jnp.concatenate([p[1] for p in reversed(pw)], axis=0)
        rin = _iota2((8, _S5_W), 0)
        cr, ci = carry_ref[0:1, :], carry_ref[1:2, :]
        acc_r = jnp.zeros((8, _S5_W), F32)
        acc_i = jnp.zeros((8, _S5_W), F32)
        for t in reversed(range(bt // 8)):
            sl = slice(8 * t, 8 * t + 8)
            lr, li = g_ref[sl, :_S5_W], g_ref[sl, _S5_W:]
            for s in (1, 2, 4):
                m = rin < 8 - s
                sr = jnp.where(m, pltpu.roll(lr, 8 - s, 0), 0.0)
                si = jnp.where(m, pltpu.roll(li, 8 - s, 0), 0.0)
                lr, li = _cmul_add(lr, li, *pw[s - 1], sr, si)
            lr, li = _cmul_add(lr, li, pwr, pwi, cr, ci)
            lam_ref[sl, :_S5_W] = lr
            lam_ref[sl, _S5_W:] = li
            nr = jnp.where(rin == 7, cr, pltpu.roll(lr, 7, 0))
            ni = jnp.where(rin == 7, ci, pltpu.roll(li, 7, 0))
            xr, xi = x_ref[sl, :_S5_W], x_ref[sl, _S5_W:]
            acc_r = acc_r + (xr * nr + xi * ni)
            acc_i = acc_i + (xr * ni - xi * nr)
            cr, ci = lr[0:1, :], li[0:1, :]
        carry_ref[0:1, :] = cr
        carry_ref[1:2, :] = ci
        dar_ref[...] += jnp.sum(acc_r, axis=0, keepdims=True)
        dai_ref[...] += jnp.sum(acc_i, axis=0, keepdims=True)
        lam = lam_ref[...]
        du_ref[...] = _dot_raw(lam, wb_ref[...], "nt")
        dwb_ref[...] += _dot_raw(u_ref[...], lam, "tn")
        dwc_ref[...] += _dot_raw(x_ref[...], dy_ref[...], "tn")

    nblk = S5_NG // S5_BLK
    blk = pl.BlockSpec((bt, 2 * _S5_W), lambda g, t: (nb - 1 - t, g))
    col = pl.BlockSpec((bt, LANE), lambda g, t: (nb - 1 - t, g))
    a_s = pl.BlockSpec((None, 1, _S5_W), lambda g, t: (g, 0, 0))
    wb_s = pl.BlockSpec((None, LANE, 2 * _S5_W), lambda g, t: (g, 0, 0))
    wc_s = pl.BlockSpec((None, 2 * _S5_W, LANE), lambda g, t: (g, 0, 0))
    return pl.pallas_call(
        body, name=name, grid=(nblk, nb), in_specs=[col, blk, col, wb_s, a_s, a_s, wc_s],
        out_specs=[col, wb_s, wc_s, a_s, a_s],
        out_shape=[jax.ShapeDtypeStruct((T, nblk * LANE), F32), jax.ShapeDtypeStruct((nblk, LANE, 2 * _S5_W), F32),
                   jax.ShapeDtypeStruct((nblk, 2 * _S5_W, LANE), F32), jax.ShapeDtypeStruct((nblk, 1, _S5_W), F32),
                   jax.ShapeDtypeStruct((nblk, 1, _S5_W), F32)],
        scratch_shapes=[pltpu.VMEM((bt, 2 * _S5_W), F32), pltpu.VMEM((bt, 2 * _S5_W), F32), pltpu.VMEM((8, _S5_W), F32)],
        compiler_params=_cparams(2),
    )(dy, x, u, wb, a_re, a_im, wc)


def _norm_bf16(h, g, name):
    return rowwise(f_rmsnorm, [g], [h], [(D, BF16)], bt=512, name=name)[0]


def _norm_bwd(h, g, cts, name):
    n = len(cts) - 1

    def f(p, r):
        y = _rms(r[0], p[0])
        return (y,) * n + (r[0],)

    (dg,), (dh,) = rowwise_vjp(f, [g], [h], cts, [F32], bt=256, name=name)
    return dh, dg


def ffn_fwd(h, g, w_gu, w_down, tag):
    hn = _norm_bf16(h, g, f"{tag}_norm")
    a, gu = ffn_up(hn, w_gu, name=f"{tag}_up")
    h2 = matmul(a, w_down[None], "nn", add=h, name=f"{tag}_down")
    return h2, (h, hn, gu, a)


def ffn_bwd(d, saved, g, w_gu, w_down, tag):
    h, hn, gu, a = saved
    dgu = ffn_dact(d, w_down, gu, name=f"{tag}_dact")
    dwd = matmul(a, d, "tn", name=f"{tag}_dwd")[0]
    dwgu = matmul(hn, dgu, "tn", name=f"{tag}_dwgu")[0]
    dh, dg = matmul_nt_norm_bwd(dgu, w_gu, h, g, d, name=f"{tag}_dhn")
    return dh, dg, dwgu, dwd


_GLA_NC = 4
_SSD_NU = 4


def gla_fwd(h, gm, w_in, w_a2, b_a, ng, w_out, tag):
    hn = _norm_bf16(h, gm, f"{tag}_norm")
    proj = matmul(hn, w_in[None], "nn", name=f"{tag}_in")
    alow = (proj, LANE, 2 * (GLA_QK + GLA_VD) // LANE)
    la = rowwise(f_gla_gate_in_fwd, [w_a2, b_a], [alow], [(GLA_QK, F32)], bt=512, name=f"{tag}_gate")[0]
    og, ss = gla_scan_fwd(proj, la, ng, nc=_GLA_NC, name=f"{tag}_scan")
    h2 = matmul(og, w_out[None], "nn", add=h, name=f"{tag}_proj")
    return h2, (h, hn, proj, la, ss, og)


def gla_bwd(d, saved, gm, w_in, w_a2, b_a, ng, w_out, tag):
    h, hn, proj, la, ss, og = saved
    dog = matmul(d, w_out[None], "nt", name=f"{tag}_dog")
    dwout = matmul(og, d, "tn", name=f"{tag}_dwout")[0]
    dq, dk, dv, dla, dr, dng = gla_scan_bwd(proj, la, ng, ss, dog, nc=_GLA_NC, name=f"{tag}_dscan")
    alow = (proj, LANE, 2 * (GLA_QK + GLA_VD) // LANE)
    (dwa2, dba), (dalow,) = rowwise_vjp(f_gla_gate_in, [w_a2, b_a], [alow], [dla], [BF16], bt=512, name=f"{tag}_dgate")
    dproj = jnp.concatenate([dq, dk, dv, dr, dalow], axis=1)
    dwin = matmul(hn, dproj, "tn", name=f"{tag}_dwin")[0]
    dh, dgm = matmul_nt_norm_bwd(dproj, w_in, h, gm, d, name=f"{tag}_dhn")
    return dh, dgm, dwin, dwa2[:GLA_RANK], dba, dng, dwout


def ssd_fwd(h, gm, w_in, conv_w, conv_b, dtb, alog, dsk, ng, w_out, tag):
    hn = _norm_bf16(h, gm, f"{tag}_norm")
    proj = matmul(hn, w_in[None], "nn", name=f"{tag}_in")
    xbc = ssd_conv_fwd(proj, conv_w, conv_b, name=f"{tag}_conv")
    yg, hs = ssd_scan_fwd(xbc, proj, dtb, alog, dsk, ng, nu=_SSD_NU, name=f"{tag}_scan")
    h2 = matmul(yg, w_out[None], "nn", add=h, name=f"{tag}_proj")
    return h2, (h, hn, proj, xbc, hs, yg)


def ssd_bwd(d, saved, gm, w_in, conv_w, conv_b, dtb, alog, dsk, ng, w_out, tag):
    h, hn, proj, xbc, hs, yg = saved
    dyg = matmul(d, w_out[None], "nt", name=f"{tag}_dyg")
    dwout = matmul(yg, d, "tn", name=f"{tag}_dwout")[0]
    dxs, dbm, dcm, ddt, ddtb, dal, ddsk, dz, dng = ssd_scan_bwd(xbc, proj, dtb, alog, dsk, ng, hs, dyg, nu=_SSD_NU,
                                                               name=f"{tag}_dscan")
    dxbc = jnp.concatenate([dxs, dbm, dcm], axis=1)
    dpre, dcw, dcb = ssd_conv_bwd(proj, conv_w, conv_b, dxbc, name=f"{tag}_dconv")
    dproj = jnp.concatenate([dz, dpre, ddt.astype(BF16)], axis=1)
    dwin = matmul(hn, dproj, "tn", name=f"{tag}_dwin")[0]
    dh, dgm = matmul_nt_norm_bwd(dproj, w_in, h, gm, d, name=f"{tag}_dhn")
    return (dh, dgm, dwin, dcw, dcb, ddtb[:, :SSD_H], dal[:, :SSD_H], ddsk[:, :SSD_H], dng, dwout)


_S5_NB = S5_NG // S5_BLK


def _s5_param_args(log_dt, a_re, a_im, b_re, b_im, c_im):
    n = S5_NG * S5_GS
    tr = lambda b: jnp.transpose(b, (0, 2, 1)).reshape(n, S5_P)
    return [log_dt.reshape(S5_NG, 1), a_re, a_im, tr(b_re), tr(b_im), c_im.reshape(n, S5_P)]


def _s5_blockdiag(t):
    nb, gl, a, b = t.shape
    eye = jnp.eye(gl, dtype=t.dtype)
    return (t[:, :, :, None, :] * eye[None, :, None, :, None]).reshape(nb, gl * a, gl * b)


def _s5_diag(t, a, b):
    nb = t.shape[0]
    gl = t.shape[1] // a
    eye = jnp.eye(gl, dtype=t.dtype)
    return jnp.sum(t.reshape(nb, gl, a, gl, b) * eye[None, :, None, :, None], axis=3)


def _s5_weights(bbr, bbi, c_re, cneg):
    sh = (_S5_NB, S5_BLK, S5_GS, S5_P)
    wb = jnp.concatenate([_s5_blockdiag(bbr.reshape(sh)), _s5_blockdiag(bbi.reshape(sh))], axis=2)
    tr = lambda cc: jnp.transpose(cc.reshape(sh), (0, 1, 3, 2))
    wc = jnp.concatenate([_s5_blockdiag(tr(c_re)), _s5_blockdiag(tr(cneg))], axis=1)
    return wb, wc


def s5_fwd(h, gm, prm, dsk, w_glu, tag):
    log_dt, a_re, a_im, b_re, b_im, c_re, c_im = prm
    hn = rowwise(f_rmsnorm, [gm], [h], [(D, F32)], bt=512, name=f"{tag}_norm")[0]
    pargs = _s5_param_args(log_dt, a_re, a_im, b_re, b_im, c_im)
    abr, abi, bbr, bbi, cneg = s5_param_fwd(pargs, name=f"{tag}_param")
    wb, wc = _s5_weights(bbr, bbi, c_re.reshape(S5_NG * S5_GS, S5_P), cneg)
    ar, ai = abr.reshape(_S5_NB, 1, _S5_W), abi.reshape(_S5_NB, 1, _S5_W)
    wb, wc = wb.astype(BF16), wc.astype(BF16)
    x, ycp = s5_scan_fwd(hn, wb, ar, ai, wc, name=f"{tag}_scan")
    yg = rowwise(f_s5_act, [dsk], [ycp, hn], [(D, BF16)], bt=512, name=f"{tag}_act")[0]
    vg = matmul(yg, w_glu[None], "nn", name=f"{tag}_glu")
    h2 = rowwise(f_glu_res, [], [vg, h], [(D, F32)], bt=512, name=f"{tag}_out")[0]
    return h2, (h, hn, pargs, wb, wc, ar, ai, x, ycp, yg, vg)


def s5_bwd(d, saved, gm, dsk, w_glu, tag):
    h, hn, pargs, wb, wc, ar, ai, x, ycp, yg, vg = saved
    _, (dvg,) = rowwise_vjp(f_glu, [], [vg], [d], [BF16], bt=256, name=f"{tag}_dout")
    dwglu = matmul(yg, dvg, "tn", name=f"{tag}_dwglu")[0]
    dyg = matmul(dvg, w_glu[None], "nt", name=f"{tag}_dyg")
    (ddsk,), (dycp, dhn1) = rowwise_vjp(f_s5_act, [dsk], [ycp, hn], [dyg], [F32, F32], bt=256, name=f"{tag}_dact")
    dhn2, dwb, dwc, dar, dai = s5_scan_bwd(dycp, x, hn, wb, ar, ai, wc, name=f"{tag}_dscan")
    dh, dgm = _norm_bwd(h, gm, [dhn1, dhn2, d], f"{tag}_dnorm")
    n = S5_NG * S5_GS
    half = S5_BLK * S5_P
    d_bbr = _s5_diag(dwb[:, :, :half], S5_GS, S5_P).reshape(n, S5_P)
    d_bbi = _s5_diag(dwb[:, :, half:], S5_GS, S5_P).reshape(n, S5_P)
    from_c = lambda t: jnp.transpose(_s5_diag(t, S5_P, S5_GS), (0, 1, 3, 2)).reshape(n, S5_P)
    d_cre = from_c(dwc[:, :half, :])
    d_cneg = from_c(dwc[:, half:, :])
    cts = [dar.reshape(S5_NG, S5_P), dai.reshape(S5_NG, S5_P), d_bbr, d_bbi, d_cneg]
    dlog, dare, daim, dbre_t, dbim_t, dcim = s5_param_bwd(pargs, cts, name=f"{tag}_dparam")
    untr = lambda t: jnp.transpose(t.reshape(S5_NG, S5_GS, S5_P), (0, 2, 1))
    grads = (dlog.reshape(S5_NG), dare, daim, untr(dbre_t), untr(dbim_t),
             d_cre.reshape(S5_NG, S5_GS, S5_P), dcim.reshape(S5_NG, S5_GS, S5_P))
    return dh, dgm, grads, ddsk, dwglu


def _pad_last(w, n):
    return jnp.pad(w, [(0, 0)] * (w.ndim - 1) + [(0, n - w.shape[-1])])


_BIG = ("gla_w_in", "gla_w_out", "ssd_w_in", "ssd_w_out", "s5_w_glu", "ffn_w_gu", "ffn_w_down")


def interleave_gu(w):
    q = w.shape[-1] // 4
    return jnp.concatenate([w[..., :q], w[..., 2 * q:3 * q], w[..., q:2 * q], w[..., 3 * q:]], axis=-1)


def local_step(x, target, W, later_weights=None, later_grads=None, ffn0_grads=None, ffn0_weights=None):
    f32 = lambda a: a.astype(F32)
    row = lambda a: f32(a).reshape(1, -1)

    def layer_args(i):
        m, j = i % 3, i // 3
        gm = row(W["norm_mix_g"][i])
        if m == 0:
            args = (gm, W["gla_w_in"][j], jnp.pad(f32(W["gla_w_a2"][j]), ((0, LANE - GLA_RANK), (0, 0))),
                    row(W["gla_b_a"][j]), row(W["gla_norm_g"][j]), W["gla_w_out"][j])
        elif m == 1:
            pl_ = lambda a: _pad_last(row(a), LANE)
            args = (gm, W["ssd_w_in"][j], f32(W["ssd_conv_w"][j]),
                    row(W["ssd_conv_b"][j]), pl_(W["ssd_dt_bias"][j]), pl_(W["ssd_a_log"][j]), pl_(W["ssd_d"][j]),
                    row(W["ssd_norm_g"][j]), W["ssd_w_out"][j])
        else:
            prm = tuple(f32(W[k][j]) for k in ("s5_log_dt", "s5_a_re", "s5_a_im", "s5_b_re", "s5_b_im", "s5_c_re", "s5_c_im"))
            args = (gm, prm, row(W["s5_d"][j]), W["s5_w_glu"][j])
        return m, j, args

    h = x
    saved, mixers, ffns = [], [], []
    for i in range(DEPTH):
        mixer = layer_args(i)
        mixers.append(mixer)
        m, j, args = mixer
        tag = f"l{i}_{('gla', 'ssd', 's5')[m]}"
        h, sm = (gla_fwd, ssd_fwd, s5_fwd)[m](h, *args, tag)
        if i == 0 and ffn0_weights is not None:
            W = {**W, **ffn0_weights(h)}
        ffn = (row(W["norm_ffn_g"][i]), W["ffn_w_gu"][i], W["ffn_w_down"][i])
        ffns.append(ffn)
        h, sf = ffn_fwd(h, *ffn, f"l{i}_ffn")
        saved.append((sm, sf))
        if i == 0 and later_weights is not None:
            W = {**W, **later_weights(h)}
    loss, dfg, d = loss_head(h, row(W["final_norm_g"]), target, name="loss_head")

    G = {k: [None] * len(v) for k, v in W.items() if k != "final_norm_g"}
    G["final_norm_g"] = dfg.reshape(D)
    for i in reversed(range(DEPTH)):
        m, j, args = mixers[i]
        sm, sf = saved[i]
        if i == 0 and later_grads is not None:
            zero = later_grads(G)
            ffns[0] = (ffns[0][0], ffns[0][1], ffns[0][2] + zero.astype(ffns[0][2].dtype))
        d, dg, dwgu, dwd = ffn_bwd(d, sf, *ffns[i], f"l{i}_ffn")
        G["norm_ffn_g"][i], G["ffn_w_gu"][i], G["ffn_w_down"][i] = dg.reshape(D), dwgu, dwd
        if i == 0 and ffn0_grads is not None:
            zero = ffn0_grads(G)
            args = args[:-1] + (args[-1] + zero.astype(args[-1].dtype),)
        tag = f"l{i}_{('gla', 'ssd', 's5')[m]}"
        if m == 0:
            d, dgm, dwin, dwa2, dba, dng, dwout = gla_bwd(d, sm, *args, tag)
            G["gla_w_in"][j], G["gla_w_a2"][j], G["gla_b_a"][j] = dwin, dwa2, dba.reshape(-1)
            G["gla_norm_g"][j], G["gla_w_out"][j] = dng.reshape(-1), dwout
        elif m == 1:
            d, dgm, dwin, dcw, dcb, ddtb, dal, ddsk, dng, dwout = ssd_bwd(d, sm, *args, tag)
            G["ssd_w_in"][j], G["ssd_conv_w"][j], G["ssd_conv_b"][j] = dwin, dcw, dcb.reshape(-1)
            G["ssd_dt_bias"][j], G["ssd_a_log"][j], G["ssd_d"][j] = ddtb.reshape(-1), dal.reshape(-1), ddsk.reshape(-1)
            G["ssd_norm_g"][j], G["ssd_w_out"][j] = dng.reshape(-1), dwout
        else:
            d, dgm, pg, ddsk, dwglu = s5_bwd(d, sm, args[0], args[2], args[3], tag)
            for k, v in zip(("s5_log_dt", "s5_a_re", "s5_a_im", "s5_b_re", "s5_b_im", "s5_c_re", "s5_c_im"), pg):
                G[k][j] = v
            G["s5_d"][j], G["s5_w_glu"][j] = ddsk.reshape(-1), dwglu
        G["norm_mix_g"][i] = dgm.reshape(D)
    grads = {k: (v if k == "final_norm_g" or k in _BIG else jnp.stack(v)) for k, v in G.items()}
    return loss, d, grads


_MESH = pl.DeviceIdType.MESH
_ANY = pl.BlockSpec(memory_space=pl.ANY)
_DMA = pltpu.SemaphoreType.DMA
_ROWS_ALIGN = 1024


def _place():
    return lax.axis_index("x"), lax.axis_index("y"), lax.axis_index("c")


def _other_chips(x, y):
    return [(1 - x, y), (x, 1 - y), (1 - x, 1 - y)]


def _remote(src, dst, send_sems, recv_sems, k, to):
    return pltpu.make_async_remote_copy(src_ref=src, dst_ref=dst, send_sem=send_sems.at[k], recv_sem=recv_sems.at[k],
                                        device_id=to, device_id_type=_MESH)


def gather_shards(loc, *, name):
    def body(in_ref, out_ref, send_sems, recv_sems, local_sem):
        x, y, c = _place()
        me, sibling = (x, y, c), (x, y, 1 - c)
        chips = _other_chips(x, y)

        def half(px, py, hc):
            return out_ref.at[2 * px + py, hc]

        mine = pltpu.make_async_copy(in_ref, out_ref.at[2 * x + y], local_sem)
        mine.start()
        first = [_remote(in_ref.at[c], half(x, y, c), send_sems, recv_sems, j, (*chip, c)) for j, chip in enumerate(chips)]
        for cp in first:
            cp.start()
        passed = [_remote(half(*chip, c), half(*chip, c), send_sems, recv_sems, 3 + j, sibling) for j, chip in enumerate(chips)]
        for j, chip in enumerate(chips):
            _remote(in_ref.at[c], half(*chip, c), send_sems, recv_sems, j, me).wait_recv()
            passed[j].start()
        for j, chip in enumerate(chips):
            _remote(in_ref.at[c], half(*chip, 1 - c), send_sems, recv_sems, 3 + j, me).wait_recv()
        for cp in first + passed:
            cp.wait_send()
        mine.wait()

    return pl.pallas_call(
        body, name=name, in_specs=[_ANY], out_specs=_ANY,
        out_shape=jax.ShapeDtypeStruct((4,) + loc.shape, loc.dtype),
        scratch_shapes=[_DMA((6,)), _DMA((6,)), _DMA(())],
    )(loc)


def _pos(px, py, perm):
    return 2 * py + px if perm else 2 * px + py


def _part(ref, kind, p, loc):
    if kind == "lead":
        return ref.at[p]
    return ref.at[:, pl.ds(pl.multiple_of(p * loc, LANE), loc)]


def _rows(ref, h, hr):
    return ref.at[pl.ds(h * hr, hr)]


def _rows_block(hr, width):
    return max(b for b in range(16, hr + 1, 16) if hr % b == 0 and (b * width <= (1 << 19) or b == 16))


def gather_big(locs, kinds, *, name):
    n = len(locs)

    def body(*refs):
        ins, outs = refs[:n], refs[n:2 * n]
        send_sems, recv_sems = refs[2 * n + 1:]
        refs[2 * n][...] = jnp.zeros_like(refs[2 * n])
        x, y, c = _place()
        me, sibling = (x, y, c), (x, y, 1 - c)
        chips = _other_chips(x, y)

        def half(i, px, py, h):
            (kind, perm), (rows, loc) = kinds[i], locs[i].shape
            return _rows(_part(outs[i], kind, _pos(px, py, perm), loc), h, rows // 2)

        sends = []
        for i in range(n):
            (kind, perm), (rows, loc) = kinds[i], locs[i].shape
            own = _part(outs[i], kind, _pos(x, y, perm), loc)
            sends.append(_remote(ins[i], own, send_sems, recv_sems, 6 * n + i, sibling))
            sends[-1].start()
            for j, chip in enumerate(chips):
                sends.append(_remote(_rows(ins[i], c, rows // 2), half(i, x, y, c), send_sems, recv_sems, 6 * i + j, (*chip, c)))
                sends[-1].start()
        for i in range(n):
            hr = locs[i].shape[0] // 2
            for j, chip in enumerate(chips):
                _remote(_rows(ins[i], c, hr), half(i, *chip, c), send_sems, recv_sems, 6 * i + j, me).wait_recv()
                sends.append(_remote(half(i, *chip, c), half(i, *chip, c), send_sems, recv_sems, 6 * i + 3 + j, sibling))
                sends[-1].start()
        for i in range(n):
            (kind, perm), (rows, loc) = kinds[i], locs[i].shape
            for j, chip in enumerate(chips):
                _remote(_rows(ins[i], c, rows // 2), half(i, *chip, 1 - c), send_sems, recv_sems, 6 * i + 3 + j, me).wait_recv()
            _remote(ins[i], _part(outs[i], kind, _pos(x, y, perm), loc), send_sems, recv_sems, 6 * n + i, me).wait_recv()
        for cp in sends:
            cp.wait_send()

    def out_shape(a, kind):
        rows, loc = a.shape
        return jax.ShapeDtypeStruct((4, rows, loc) if kind == "lead" else (rows, 4 * loc), a.dtype)

    outs = pl.pallas_call(
        body, name=name, in_specs=[_ANY] * n, out_specs=[_ANY] * n + [pl.BlockSpec(memory_space=pltpu.VMEM)],
        out_shape=[out_shape(a, k[0]) for a, k in zip(locs, kinds)] + [jax.ShapeDtypeStruct((8, LANE), F32)],
        scratch_shapes=[_DMA((7 * n,)), _DMA((7 * n,))],
    )(*locs)
    return list(outs[:n]), outs[n][0, 0]


_HBM = pl.BlockSpec(memory_space=pltpu.HBM)
_SEM = pl.BlockSpec(memory_space=pltpu.SEMAPHORE)
_EFFECT = pltpu.SideEffectType.DATAFLOW_SIDE_EFFECTING


def _in_hbm(a):
    return pltpu.with_memory_space_constraint(a, pltpu.HBM)


def _gather_ici_copies(ins, lands, kinds, shapes, send_sems, recv_sems):
    x, y, c = _place()
    sends, arrivals = [], []
    for i, ((kind, perm), (rows, loc)) in enumerate(zip(kinds, shapes)):
        hr = rows // 2
        mine = _part(lands[i], kind, _pos(x, y, perm), loc)
        sends.append(_remote(ins[i], mine, send_sems, recv_sems, 4 * i + 3, (x, y, 1 - c)))
        arrivals.append(_remote(ins[i], mine, send_sems, recv_sems, 4 * i + 3, (x, y, c)))
        for j, (px, py) in enumerate(_other_chips(x, y)):
            sends.append(_remote(_rows(ins[i], c, hr), _rows(mine, c, hr), send_sems, recv_sems, 4 * i + j, (px, py, c)))
            theirs = _rows(_part(lands[i], kind, _pos(px, py, perm), loc), c, hr)
            arrivals.append(_remote(_rows(ins[i], c, hr), theirs, send_sems, recv_sems, 4 * i + j, (x, y, c)))
    return sends, arrivals


def gather_start(locs, kinds, *, name):
    n = len(locs)
    shapes = [a.shape for a in locs]

    def land_shape(a, kind):
        rows, loc = a.shape
        return (4, rows, loc) if kind == "lead" else (rows, 4 * loc)

    def body(*refs):
        sends, _ = _gather_ici_copies(refs[:n], refs[n:2 * n], kinds, shapes, refs[2 * n], refs[2 * n + 1])
        for cp in sends:
            cp.start()
        refs[-1][...] = jnp.zeros_like(refs[-1])

    lands = [lax.empty(land_shape(a, k[0]), a.dtype) for a, k in zip(locs, kinds)]
    outs = pl.pallas_call(
        body, name=name, in_specs=[_HBM] * (2 * n), out_specs=[_SEM, _SEM] + [_HBM] * (2 * n) + [pl.BlockSpec(memory_space=pltpu.VMEM)],
        out_shape=[_DMA((4 * n,)), _DMA((4 * n,))] + [pltpu.HBM(a.shape, a.dtype) for a in locs]
        + [pltpu.HBM(l.shape, l.dtype) for l in lands] + [jax.ShapeDtypeStruct((8, LANE), F32)],
        input_output_aliases={i: 2 + i for i in range(2 * n)},
        compiler_params=pltpu.CompilerParams(has_side_effects=_EFFECT),
    )(*[_in_hbm(a) for a in locs], *[_in_hbm(l) for l in lands])
    return outs[0], outs[1], list(outs[2:2 + n]), list(outs[2 + n:2 + 2 * n]), outs[-1][0, 0]


def gather_wait(send_sems, recv_sems, locs, lands, kinds, after, *, name):
    n = len(locs)
    shapes = [a.shape for a in locs]

    def body(*refs):
        sends, arrivals = _gather_ici_copies(refs[:n], refs[n:2 * n], kinds, shapes, refs[2 * n], refs[2 * n + 1])
        for cp in sends:
            cp.wait_send()
        for cp in arrivals:
            cp.wait_recv()

    outs = pl.pallas_call(
        body, name=name, in_specs=[_HBM] * (2 * n) + [_SEM, _SEM, _ANY], out_specs=[_HBM] * (2 * n),
        out_shape=[pltpu.HBM(a.shape, a.dtype) for a in locs] + [pltpu.HBM(l.shape, l.dtype) for l in lands],
        input_output_aliases={i: i for i in range(2 * n)},
        compiler_params=pltpu.CompilerParams(has_side_effects=_EFFECT),
    )(*locs, *lands, send_sems, recv_sems, after)
    return list(outs[n:])


def gather_finish(lands, kinds, shapes, *, name):
    n = len(lands)

    def body(*refs):
        bufs = refs[n:2 * n]
        send_sems, recv_sems = refs[2 * n:]
        x, y, c = _place()
        sends = []
        for i, ((kind, perm), (rows, loc)) in enumerate(zip(kinds, shapes)):
            for j, (px, py) in enumerate(_other_chips(x, y)):
                part = _part(bufs[i], kind, _pos(px, py, perm), loc)
                sends.append(_remote(_rows(part, c, rows // 2), _rows(part, c, rows // 2), send_sems, recv_sems, 3 * i + j, (x, y, 1 - c)))
                sends[-1].start()
        for i, ((kind, perm), (rows, loc)) in enumerate(zip(kinds, shapes)):
            for j, (px, py) in enumerate(_other_chips(x, y)):
                part = _part(bufs[i], kind, _pos(px, py, perm), loc)
                _remote(_rows(part, c, rows // 2), _rows(part, 1 - c, rows // 2), send_sems, recv_sems, 3 * i + j, (x, y, c)).wait_recv()
        for cp in sends:
            cp.wait_send()

    return list(pl.pallas_call(
        body, name=name, in_specs=[_ANY] * n, out_specs=[_ANY] * n,
        out_shape=[jax.ShapeDtypeStruct(l.shape, l.dtype) for l in lands],
        input_output_aliases={i: i for i in range(n)}, scratch_shapes=[_DMA((3 * n,)), _DMA((3 * n,))],
    )(*lands))


def _scatter_copies(qs, lands, kinds, locs, send_sems, recv_sems):
    x, y, c = _place()
    sends, arrivals = [], []
    for i, (kind, perm) in enumerate(kinds):
        for j, (px, py) in enumerate(_other_chips(x, y)):
            src = _part(qs[i], kind, _pos(px, py, perm), locs[i])
            sends.append(_remote(src, lands[i].at[j], send_sems, recv_sems, 3 * i + j, (px, py, c)))
            arrivals.append(_remote(src, lands[i].at[j], send_sems, recv_sems, 3 * i + j, (x, y, c)))
    return sends, arrivals


def _scatter_land(q, kind, loc):
    return (3, q.shape[1] if kind == "lead" else q.shape[0], loc)


def scatter_start(qs, kinds, locs, *, name):
    n = len(qs)

    def body(*refs):
        sends, _ = _scatter_copies(refs[:n], refs[n:2 * n], kinds, locs, refs[2 * n], refs[2 * n + 1])
        for cp in sends:
            cp.start()
        refs[-1][...] = jnp.zeros_like(refs[-1])

    lands = [lax.empty(_scatter_land(q, k[0], l), q.dtype) for q, k, l in zip(qs, kinds, locs)]
    outs = pl.pallas_call(
        body, name=name, in_specs=[_HBM] * (2 * n), out_specs=[_SEM, _SEM] + [_HBM] * (2 * n) + [pl.BlockSpec(memory_space=pltpu.VMEM)],
        out_shape=[_DMA((3 * n,)), _DMA((3 * n,))] + [pltpu.HBM(q.shape, q.dtype) for q in qs]
        + [pltpu.HBM(l.shape, l.dtype) for l in lands] + [jax.ShapeDtypeStruct((8, LANE), F32)],
        input_output_aliases={i: 2 + i for i in range(2 * n)},
        compiler_params=pltpu.CompilerParams(has_side_effects=_EFFECT),
    )(*[_in_hbm(q) for q in qs], *[_in_hbm(l) for l in lands])
    return outs[0], outs[1], list(outs[2:2 + n]), list(outs[2 + n:2 + 2 * n]), outs[-1][0, 0]


def scatter_wait(send_sems, recv_sems, qs, lands, kinds, locs, after, *, name):
    n = len(qs)

    def body(*refs):
        sends, arrivals = _scatter_copies(refs[:n], refs[n:2 * n], kinds, locs, refs[2 * n], refs[2 * n + 1])
        for cp in sends:
            cp.wait_send()
        for cp in arrivals:
            cp.wait_recv()

    outs = pl.pallas_call(
        body, name=name, in_specs=[_HBM] * (2 * n) + [_SEM, _SEM, _ANY], out_specs=[_HBM] * (2 * n),
        out_shape=[pltpu.HBM(q.shape, q.dtype) for q in qs] + [pltpu.HBM(l.shape, l.dtype) for l in lands],
        input_output_aliases={i: i for i in range(2 * n)},
        compiler_params=pltpu.CompilerParams(has_side_effects=_EFFECT),
    )(*qs, *lands, send_sems, recv_sems, after)
    return list(outs[:n]), list(outs[n:])


def _pair_swap_copies(ins, lands, kinds, send_sems, recv_sems):
    x, y, c = _place()
    sends, arrivals = [], []
    for i, (kind, _) in enumerate(kinds):
        if kind == "lead":
            hr = ins[i].shape[1] // 2
            src = ins[i].at[:, pl.ds((1 - c) * hr, hr)]
        else:
            src = _rows(ins[i], 1 - c, ins[i].shape[0] // 2)
        sends.append(_remote(src, lands[i], send_sems, recv_sems, i, (x, y, 1 - c)))
        arrivals.append(_remote(src, lands[i], send_sems, recv_sems, i, (x, y, c)))
    return sends, arrivals


def _pair_swap_land(a, kind):
    s = a.shape
    return (4, s[1] // 2, s[2]) if kind == "lead" else (s[0] // 2, s[1])


def pair_swap_start(ps, kinds, *, name):
    n = len(ps)

    def body(*refs):
        sends, _ = _pair_swap_copies(refs[:n], refs[n:2 * n], kinds, refs[2 * n], refs[2 * n + 1])
        for cp in sends:
            cp.start()
        refs[-1][...] = jnp.zeros_like(refs[-1])

    lands = [lax.empty(_pair_swap_land(p, k[0]), p.dtype) for p, k in zip(ps, kinds)]
    outs = pl.pallas_call(
        body, name=name, in_specs=[_HBM] * (2 * n), out_specs=[_SEM, _SEM] + [_HBM] * (2 * n) + [pl.BlockSpec(memory_space=pltpu.VMEM)],
        out_shape=[_DMA((n,)), _DMA((n,))] + [pltpu.HBM(p.shape, p.dtype) for p in ps]
        + [pltpu.HBM(l.shape, l.dtype) for l in lands] + [jax.ShapeDtypeStruct((8, LANE), F32)],
        input_output_aliases={i: 2 + i for i in range(2 * n)},
        compiler_params=pltpu.CompilerParams(has_side_effects=_EFFECT),
    )(*[_in_hbm(p) for p in ps], *[_in_hbm(l) for l in lands])
    return outs[0], outs[1], list(outs[2:2 + n]), list(outs[2 + n:2 + 2 * n]), outs[-1][0, 0]


def pair_swap_wait(send_sems, recv_sems, ps, lands, kinds, after, *, name):
    n = len(ps)

    def body(*refs):
        sends, arrivals = _pair_swap_copies(refs[:n], refs[n:2 * n], kinds, refs[2 * n], refs[2 * n + 1])
        for cp in sends:
            cp.wait_send()
        for cp in arrivals:
            cp.wait_recv()

    outs = pl.pallas_call(
        body, name=name, in_specs=[_HBM] * (2 * n) + [_SEM, _SEM, _ANY], out_specs=[_HBM] * (2 * n),
        out_shape=[pltpu.HBM(p.shape, p.dtype) for p in ps] + [pltpu.HBM(l.shape, l.dtype) for l in lands],
        input_output_aliases={i: i for i in range(2 * n)},
        compiler_params=pltpu.CompilerParams(has_side_effects=_EFFECT),
    )(*ps, *lands, send_sems, recv_sems, after)
    return list(outs[:n]), list(outs[n:])


def pair_swap(ps, kinds, *, name):
    n = len(ps)

    def body(*refs):
        ins, outs = refs[:n], refs[n:2 * n]
        send_sems, recv_sems = refs[2 * n:]
        x, y, c = _place()
        cps = []
        for i in range(n):
            if kinds[i][0] == "lead":
                hr = ps[i].shape[1] // 2
                src = ins[i].at[:, pl.ds((1 - c) * hr, hr)]
            else:
                hr = ps[i].shape[0] // 2
                src = _rows(ins[i], 1 - c, hr)
            cps.append(_remote(src, outs[i], send_sems, recv_sems, i, (x, y, 1 - c)))
            cps[-1].start()
        for cp in cps:
            cp.wait()

    def out_shape(a, kind):
        s = a.shape
        return jax.ShapeDtypeStruct((4, s[1] // 2, s[2]) if kind == "lead" else (s[0] // 2, s[1]), a.dtype)

    return pl.pallas_call(
        body, name=name, in_specs=[_ANY] * n, out_specs=[_ANY] * n,
        out_shape=[out_shape(a, k[0]) for a, k in zip(ps, kinds)], scratch_shapes=[_DMA((n,)), _DMA((n,))],
    )(*ps)


def pair_add(p, got, c_arr, kind, *, name):
    if kind == "lead":
        _, hr, cols = got.shape
        br = _rows_block(hr, cols)
        nb = hr // br
        grid = (4, nb)
        p_spec = pl.BlockSpec((None, br, cols), lambda s, i, cr: (s, cr[0] * nb + i, 0))
        g_spec = pl.BlockSpec((None, br, cols), lambda s, i, cr: (s, i, 0))
    else:
        hr, w = got.shape
        br = _rows_block(hr, w)
        nb = hr // br
        grid = (nb,)
        p_spec = pl.BlockSpec((br, w), lambda i, cr: (cr[0] * nb + i, 0))
        g_spec = pl.BlockSpec((br, w), lambda i, cr: (i, 0))

    def body(c_ref, p_ref, g_ref, o_ref):
        o_ref[...] = (p_ref[...] + g_ref[...]).astype(o_ref.dtype)

    return pl.pallas_call(
        body, name=name, out_shape=jax.ShapeDtypeStruct(got.shape, BF16),
        grid_spec=pltpu.PrefetchScalarGridSpec(num_scalar_prefetch=1, grid=grid, in_specs=[p_spec, g_spec], out_specs=g_spec),
        compiler_params=_cparams(len(grid)),
    )(c_arr, p, got)


def chip_scatter(qs, kinds, locs, *, name):
    n = len(qs)

    def body(*refs):
        ins, outs = refs[:n], refs[n:2 * n]
        send_sems, recv_sems = refs[2 * n:]
        x, y, c = _place()
        cps = []
        for i in range(n):
            kind, perm = kinds[i]
            for j, (px, py) in enumerate(_other_chips(x, y)):
                cps.append(_remote(_part(ins[i], kind, _pos(px, py, perm), locs[i]), outs[i].at[j], send_sems, recv_sems,
                                   3 * i + j, (px, py, c)))
                cps[-1].start()
        for cp in cps:
            cp.wait()

    def out_shape(a, kind, loc):
        hr = a.shape[1] if kind == "lead" else a.shape[0]
        return jax.ShapeDtypeStruct((3, hr, loc), a.dtype)

    return pl.pallas_call(
        body, name=name, in_specs=[_ANY] * n, out_specs=[_ANY] * n,
        out_shape=[out_shape(a, k[0], l) for a, k, l in zip(qs, kinds, locs)],
        scratch_shapes=[_DMA((3 * n,)), _DMA((3 * n,))],
    )(*qs)


def chip_add(q, r, pos_arr, c_arr, kind, loc, *, name):
    _, hr, _ = r.shape
    br = _rows_block(hr, loc)
    nb = hr // br
    if kind == "lead":
        q_spec = pl.BlockSpec((None, br, loc), lambda i, pr, cr: (pr[0], i, 0))
    else:
        q_spec = pl.BlockSpec((br, loc), lambda i, pr, cr: (i, pr[0]))
    r_spec = pl.BlockSpec((3, br, loc), lambda i, pr, cr: (0, i, 0))
    o_spec = pl.BlockSpec((br, loc), lambda i, pr, cr: (cr[0] * nb + i, 0))

    def body(p_ref, c_ref, q_ref, r_ref, o_ref):
        acc = q_ref[...].astype(F32)
        for j in range(3):
            acc = acc + r_ref[j].astype(F32)
        o_ref[...] = acc

    return pl.pallas_call(
        body, name=name, out_shape=jax.ShapeDtypeStruct((2 * hr, loc), F32),
        grid_spec=pltpu.PrefetchScalarGridSpec(num_scalar_prefetch=2, grid=(nb,), in_specs=[q_spec, r_spec], out_specs=o_spec),
        compiler_params=_cparams(1),
    )(pos_arr, c_arr, q, r)


def share_rows(fs, *, name):
    n = len(fs)

    def body(*refs):
        bufs = refs[n:2 * n]
        send_sems, recv_sems = refs[2 * n:]
        x, y, c = _place()
        cps = []
        for i in range(n):
            hr = fs[i].shape[0] // 2
            cps.append(_remote(_rows(bufs[i], c, hr), _rows(bufs[i], c, hr), send_sems, recv_sems, i, (x, y, 1 - c)))
            cps[-1].start()
        for i, cp in enumerate(cps):
            hr = fs[i].shape[0] // 2
            _remote(_rows(bufs[i], c, hr), _rows(bufs[i], 1 - c, hr), send_sems, recv_sems, i, (x, y, c)).wait_recv()
            cp.wait_send()

    return pl.pallas_call(
        body, name=name, in_specs=[_ANY] * n, out_specs=[_ANY] * n,
        out_shape=[jax.ShapeDtypeStruct(f.shape, f.dtype) for f in fs],
        input_output_aliases={i: i for i in range(n)}, scratch_shapes=[_DMA((n,)), _DMA((n,))],
    )(*fs)


def _gather_all_copies(v_ref, land_ref, send_sems, recv_sems):
    x, y, c = _place()
    flip = lambda p, m: 1 - p if m else p
    idx = lambda p: 4 * p[0] + 2 * p[1] + p[2]
    sends, arrivals = [], []
    for k, m in enumerate(range(1, 8)):
        p = (flip(x, m & 4), flip(y, m & 2), flip(c, m & 1))
        sends.append(_remote(v_ref, land_ref.at[idx((x, y, c))], send_sems, recv_sems, k, p))
        arrivals.append(_remote(v_ref, land_ref.at[idx(p)], send_sems, recv_sems, k, (x, y, c)))
    return sends, arrivals


def gather_all_start(v, *, name):
    def body(v_ref, land_ref, send_sems, recv_sems, v_thru, land_thru, token):
        sends, _ = _gather_all_copies(v_ref, land_ref, send_sems, recv_sems)
        for cp in sends:
            cp.start()
        token[...] = jnp.zeros_like(token)

    land = jnp.zeros((8,) + v.shape, v.dtype)
    outs = pl.pallas_call(
        body, name=name, in_specs=[_HBM, _HBM], out_specs=[_SEM, _SEM, _HBM, _HBM, pl.BlockSpec(memory_space=pltpu.VMEM)],
        out_shape=[_DMA((7,)), _DMA((7,)), pltpu.HBM(v.shape, v.dtype), pltpu.HBM(land.shape, land.dtype),
                   jax.ShapeDtypeStruct((8, LANE), F32)],
        input_output_aliases={0: 2, 1: 3}, compiler_params=pltpu.CompilerParams(has_side_effects=_EFFECT),
    )(_in_hbm(v), _in_hbm(land))
    return outs[0], outs[1], outs[2], outs[3], outs[4][0, 0]


def gather_all_wait(send_sems, recv_sems, v, land, after, *, name):
    def body(v_ref, land_ref, send_sems, recv_sems, after_ref, v_dead, got_ref):
        sends, arrivals = _gather_all_copies(v_ref, land_ref, send_sems, recv_sems)
        for cp in sends:
            cp.wait_send()
        for cp in arrivals:
            cp.wait_recv()

    return pl.pallas_call(
        body, name=name, in_specs=[_HBM, _HBM, _SEM, _SEM, _ANY], out_specs=[_HBM, _HBM],
        out_shape=[pltpu.HBM(v.shape, v.dtype), pltpu.HBM(land.shape, land.dtype)],
        input_output_aliases={0: 0, 1: 1}, compiler_params=pltpu.CompilerParams(has_side_effects=_EFFECT),
    )(v, land, send_sems, recv_sems, after)[1]


def sum_slots(land, v, me_arr, *, name):
    n, R, L = land.shape
    br = _pick(R, _ROWS_ALIGN, 8)

    def body(me_ref, land_ref, v_ref, o_ref):
        acc = None
        for i in range(n):
            term = jnp.where(me_ref[0] == i, v_ref[...], land_ref[i])
            acc = term if acc is None else acc + term
        o_ref[...] = acc

    row = pl.BlockSpec((br, L), lambda i, me: (i, 0))
    return pl.pallas_call(
        body, name=name, out_shape=jax.ShapeDtypeStruct((R, L), land.dtype),
        grid_spec=pltpu.PrefetchScalarGridSpec(num_scalar_prefetch=1, grid=(R // br,),
                                               in_specs=[pl.BlockSpec((n, br, L), lambda i, me: (0, i, 0)), row], out_specs=row),
        compiler_params=_cparams(1),
    )(me_arr, land, v)


def sum_stack(a, extra=None, *, name):
    n, R, L = a.shape
    br = _pick(R, _ROWS_ALIGN, 8)

    def body(*refs):
        a_ref, o_ref = refs[0], refs[-1]
        acc = refs[1][...] if extra is not None else a_ref[0]
        for i in range(0 if extra is not None else 1, n):
            acc = acc + a_ref[i]
        o_ref[...] = acc

    row = pl.BlockSpec((br, L), lambda i: (i, 0))
    specs = [pl.BlockSpec((n, br, L), lambda i: (0, i, 0))] + ([row] if extra is not None else [])
    args = [a] + ([extra] if extra is not None else [])
    return pl.pallas_call(body, name=name, grid=(R // br,), in_specs=specs, out_specs=row,
                          out_shape=jax.ShapeDtypeStruct((R, L), a.dtype), compiler_params=_cparams(1))(*args)


def adamw(w, g, m, v, *, name):
    shape = w.shape
    size = math.prod(shape)
    last = shape[-1]
    if last % LANE != 0 and size % LANE == 0 and size <= (1 << 20):
        last = LANE
    rows = size // last
    budget = (1 << 18) // last
    br = rows
    if rows > budget:
        br = max(c for c in range(8, budget + 1, 8) if rows % c == 0)
    v2 = lambda a: a.reshape(rows, last)

    def body(w_ref, g_ref, m_ref, v_ref, d_ref, nm_ref, nv_ref):
        gg = g_ref[...]
        nm = ADAM_B1 * m_ref[...] + (1.0 - ADAM_B1) * gg
        nv = ADAM_B2 * v_ref[...] + (1.0 - ADAM_B2) * (gg * gg)
        m_hat = nm / (1.0 - ADAM_B1 ** ADAM_STEP)
        v_hat = nv / (1.0 - ADAM_B2 ** ADAM_STEP)
        d_ref[...] = -ADAM_LR * (m_hat / (jnp.sqrt(v_hat) + ADAM_EPS) + ADAM_WD * w_ref[...])
        nm_ref[...] = nm
        nv_ref[...] = nv

    spec = pl.BlockSpec((br, last), lambda i: (i, 0))
    outs = pl.pallas_call(
        body, name=name, grid=(rows // br,), in_specs=[spec] * 4, out_specs=[spec] * 3,
        out_shape=[jax.ShapeDtypeStruct((rows, last), F32)] * 3, compiler_params=_cparams(1),
    )(v2(w), v2(g), v2(m), v2(v))
    return [o.reshape(shape) for o in outs]


_WEIGHTS = ["norm_mix_g", "norm_ffn_g", "gla_w_in", "gla_w_a2", "gla_b_a", "gla_norm_g", "gla_w_out", "ssd_w_in",
            "ssd_conv_w", "ssd_conv_b", "ssd_dt_bias", "ssd_a_log", "ssd_d", "ssd_norm_g", "ssd_w_out", "s5_log_dt",
            "s5_a_re", "s5_a_im", "s5_b_re", "s5_b_im", "s5_c_re", "s5_c_im", "s5_d", "s5_w_glu", "ffn_w_gu",
            "ffn_w_down", "final_norm_g"]
_SHARD_AXIS = {"gla_w_in": 2, "gla_w_a2": 2, "gla_b_a": 1, "gla_norm_g": 1, "gla_w_out": 1, "ssd_w_in": 2,
               "ssd_conv_w": 2, "ssd_w_out": 1, "s5_d": 1, "s5_w_glu": 2, "ffn_w_gu": 2, "ffn_w_down": 1}
_SMALL_SHARDED = [n for n in _WEIGHTS if n in _SHARD_AXIS and n not in _BIG]
_REPLICATED = [n for n in _WEIGHTS if n not in _SHARD_AXIS]
_BIG_KIND = {"gla_w_in": ("lead", False), "gla_w_out": ("lead", False), "ssd_w_in": ("lead", False),
             "ssd_w_out": ("lead", False), "s5_w_glu": ("cols", False), "ffn_w_gu": ("cols", True),
             "ffn_w_down": ("lead", False)}
_PADDED_IN = {"gla_w_in": GLA_INP, "ssd_w_in": SSD_INP}


def _to_rows(flat, parts=1):
    per = -(-flat.shape[0] // (parts * LANE * _ROWS_ALIGN)) * _ROWS_ALIGN
    flat = jnp.pad(flat, (0, parts * per * LANE - flat.shape[0]))
    return flat.reshape(parts, per, LANE)


def _big_layers(local):
    return [(n, j, local[n][j].reshape(-1, local[n].shape[-1])) for n in _BIG for j in range(local[n].shape[0])]


def _in_layer0(n, j):
    return j == 0 and n in ("gla_w_in", "gla_w_out", "ffn_w_gu", "ffn_w_down")


def _assemble(n, g):
    if n in _PADDED_IN:
        return jnp.concatenate([g[s] for s in range(4)] + [jnp.zeros((g.shape[1], _PADDED_IN[n] - 4 * g.shape[2]), BF16)], axis=1)
    if _BIG_KIND[n][0] == "lead":
        return g.reshape(4 * g.shape[1], g.shape[2])
    return g


def _is_gla0(n, j):
    return j == 0 and n in ("gla_w_in", "gla_w_out")


def _gather_first(local):
    layers = _big_layers(local)
    first = [l for l in layers if _is_gla0(l[0], l[1])]
    full = {n: [None] * local[n].shape[0] for n in _BIG}
    got, done = gather_big([w.astype(BF16) for _, _, w in first], [_BIG_KIND[n] for n, _, _ in first], name="gather_weights_first")
    for (n, j, _), g in zip(first, got):
        full[n][j] = _assemble(n, g)
    flat = jnp.concatenate([local[n].astype(F32).reshape(-1) for n in _SMALL_SHARDED])
    got = gather_shards(_to_rows(flat, 2), name="gather_small_weights").reshape(4, -1)
    off = 0
    for n in _SMALL_SHARDED:
        bs = local[n].shape
        sz = math.prod(bs)
        seg = got[:, off:off + sz].reshape((4,) + bs)
        off += sz
        ax = _SHARD_AXIS[n]
        full[n] = jnp.moveaxis(seg, 0, ax).reshape(bs[:ax] + (4 * bs[ax],) + bs[ax + 1:])
    pending = {}
    for tag, want in (("ffn0", _is_ffn0), ("later", lambda n, j: not _in_layer0(n, j))):
        group = [l for l in layers if want(l[0], l[1])]
        kinds = [_BIG_KIND[n] for n, _, _ in group]
        ops = [(w + done if k == 0 else w).astype(BF16) for k, (_, _, w) in enumerate(group)]
        send_sems, recv_sems, locs, lands, done = gather_start(ops, kinds, name=f"gather_weights_start_{tag}")
        pending[tag] = (group, kinds, send_sems, recv_sems, locs, lands)
    return full, pending, done


def _gather_rest(full, pending, after, tag):
    group, kinds, send_sems, recv_sems, locs, lands = pending
    lands = gather_wait(send_sems, recv_sems, locs, lands, kinds, after, name=f"gather_weights_wait_{tag}")
    lands = gather_finish(lands, kinds, [w.shape for _, _, w in group], name=f"gather_weights_finish_{tag}")
    out = {n: list(full[n]) for n in _BIG}
    for (n, j, _), g in zip(group, lands):
        out[n][j] = _assemble(n, g)
    return out


def _reduce_ops(grads, local, want):
    ops = []
    for n in _BIG:
        kind = _BIG_KIND[n]
        for j, g in enumerate(grads[n]):
            if not want(n, j):
                continue
            loc = local[n].shape[-1] if kind[0] == "cols" or n in _PADDED_IN else g.shape[1]
            if n in _PADDED_IN:
                g = jnp.stack([g[:, s * loc:(s + 1) * loc] for s in range(4)])
            elif kind[0] == "lead":
                g = g.reshape(4, g.shape[0] // 4, g.shape[1])
            ops.append((n, j, kind, loc, g))
    return ops


def _pair_sums(ops, c_arr, tag):
    gots = pair_swap([o[4] for o in ops], [o[2] for o in ops], name=f"reduce_pair_swap_{tag}")
    return [pair_add(o[4], got, c_arr, o[2][0], name=f"reduce_pair_add_{o[0]}{o[1]}") for o, got in zip(ops, gots)]


def _is_ffn0(n, j):
    return j == 0 and n in ("ffn_w_gu", "ffn_w_down")


def _reduce_start(grads, local, c, want, tag):
    ops = _reduce_ops(grads, local, want)
    c_arr = jnp.reshape(c, (1,)).astype(jnp.int32)
    qs = _pair_sums(ops, c_arr, tag)
    send_sems, recv_sems, qs, lands, zero = scatter_start(qs, [o[2] for o in ops], [o[3] for o in ops],
                                                          name=f"reduce_scatter_start_{tag}")
    return (ops, send_sems, recv_sems, qs, lands, tag), zero


def _reduce_swap_start(grads, local, c, want, tag):
    ops = _reduce_ops(grads, local, want)
    send_sems, recv_sems, ps, lands, zero = pair_swap_start([o[4] for o in ops], [o[2] for o in ops],
                                                            name=f"reduce_pair_swap_start_{tag}")
    return (ops, send_sems, recv_sems, ps, lands, tag), zero


def _reduce_scatter_after(pending, after, c):
    ops, send_sems, recv_sems, ps, lands, tag = pending
    ps, gots = pair_swap_wait(send_sems, recv_sems, ps, lands, [o[2] for o in ops], after, name=f"reduce_pair_swap_wait_{tag}")
    c_arr = jnp.reshape(c, (1,)).astype(jnp.int32)
    qs = [pair_add(p, got, c_arr, o[2][0], name=f"reduce_pair_add_{o[0]}{o[1]}") for o, p, got in zip(ops, ps, gots)]
    send_sems, recv_sems, qs, lands, zero = scatter_start(qs, [o[2] for o in ops], [o[3] for o in ops],
                                                          name=f"reduce_scatter_start_{tag}")
    return (ops, send_sems, recv_sems, qs, lands, tag), zero


def _reduce_big(grads, local, pendings, after, x, y, c):
    c_arr = jnp.reshape(c, (1,)).astype(jnp.int32)
    ops, qs, rs = [], [], []
    for ops_p, send_sems, recv_sems, qs_p, lands, tag in pendings:
        qs_p, rs_p = scatter_wait(send_sems, recv_sems, qs_p, lands, [o[2] for o in ops_p], [o[3] for o in ops_p], after,
                                  name=f"reduce_scatter_wait_{tag}")
        ops, qs, rs = ops + ops_p, qs + qs_p, rs + rs_p
    ops_f = _reduce_ops(grads, local, lambda n, j: _in_layer0(n, j) and not _is_ffn0(n, j))
    qs_f = _pair_sums(ops_f, c_arr, "first")
    rs_f = list(chip_scatter(qs_f, [o[2] for o in ops_f], [o[3] for o in ops_f], name="reduce_chip_scatter_first"))
    ops, qs, rs = ops + ops_f, qs + qs_f, rs + rs_f
    fs = [chip_add(q, r, jnp.reshape(_pos(x, y, o[2][1]), (1,)).astype(jnp.int32), c_arr, o[2][0], o[3],
                   name=f"reduce_chip_add_{o[0]}{o[1]}") for o, q, r in zip(ops, qs, rs)]
    outs = share_rows(fs, name="reduce_share")
    red = {(o[0], o[1]): r for o, r in zip(ops, outs)}
    return {n: jnp.stack([red[(n, j)] for j in range(local[n].shape[0])]).reshape(local[n].shape) for n in _BIG}


def _reduce_small_start(grads):
    names = _REPLICATED + _SMALL_SHARDED
    flat = jnp.concatenate([grads[n].astype(F32).reshape(-1) for n in names])
    n_el = flat.shape[0]
    rows = -(-n_el // (LANE * 8)) * 8
    v = jnp.pad(flat, (0, rows * LANE - n_el)).reshape(rows, LANE)
    outs = gather_all_start(v, name="reduce_small_start")
    return outs[:4], outs[4]


def _reduce_small(pending, after, grads, local, x, y, c):
    names = _REPLICATED + _SMALL_SHARDED
    send_sems, recv_sems, v, land = pending
    land = gather_all_wait(send_sems, recv_sems, v, land, after, name="reduce_small_wait")
    me = jnp.reshape(4 * x + 2 * y + c, (1,)).astype(jnp.int32)
    red = sum_slots(land, v, me, name="reduce_small_add").reshape(-1)
    out, off = {}, 0
    for n in names:
        sz = math.prod(grads[n].shape)
        g = red[off:off + sz].reshape(grads[n].shape)
        off += sz
        if n in _SHARD_AXIS:
            ax = _SHARD_AXIS[n]
            loc = local[n].shape[ax]
            g = lax.dynamic_slice_in_dim(g, (2 * x + y) * loc, loc, axis=ax)
        out[n] = g
    return out


def kernel(x, norm_mix_g, norm_ffn_g, gla_w_in, gla_w_a2, gla_b_a, gla_norm_g, gla_w_out, ssd_w_in, ssd_conv_w, ssd_conv_b, ssd_dt_bias, ssd_a_log, ssd_d, ssd_norm_g, ssd_w_out, s5_log_dt, s5_a_re, s5_a_im, s5_b_re, s5_b_im, s5_c_re, s5_c_im, s5_d, s5_w_glu, ffn_w_gu, ffn_w_down, final_norm_g, loss_target, m_norm_mix_g, m_norm_ffn_g, m_gla_w_in, m_gla_w_a2, m_gla_b_a, m_gla_norm_g, m_gla_w_out, m_ssd_w_in, m_ssd_conv_w, m_ssd_conv_b, m_ssd_dt_bias, m_ssd_a_log, m_ssd_d, m_ssd_norm_g, m_ssd_w_out, m_s5_log_dt, m_s5_a_re, m_s5_a_im, m_s5_b_re, m_s5_b_im, m_s5_c_re, m_s5_c_im, m_s5_d, m_s5_w_glu, m_ffn_w_gu, m_ffn_w_down, m_final_norm_g, v_norm_mix_g, v_norm_ffn_g, v_gla_w_in, v_gla_w_a2, v_gla_b_a, v_gla_norm_g, v_gla_w_out, v_ssd_w_in, v_ssd_conv_w, v_ssd_conv_b, v_ssd_dt_bias, v_ssd_a_log, v_ssd_d, v_ssd_norm_g, v_ssd_w_out, v_s5_log_dt, v_s5_a_re, v_s5_a_im, v_s5_b_re, v_s5_b_im, v_s5_c_re, v_s5_c_im, v_s5_d, v_s5_w_glu, v_ffn_w_gu, v_ffn_w_down, v_final_norm_g):
    given = dict(locals())
    local = {n: given[n] for n in _WEIGHTS}
    px, py, pc = _place()

    first, gathering, zero = _gather_first(local)
    full = dict(local)
    full.update(first)
    full["norm_mix_g"] = local["norm_mix_g"] + zero
    big = [first]

    def weights_of(tag):
        def arrived(h):
            big.append(_gather_rest(big[-1], gathering[tag], h, tag))
            return big[-1]
        return arrived

    swapping, reducing = [], []

    def later_grads(g):
        pending, zero = _reduce_swap_start(g, local, pc, lambda n, j: not _in_layer0(n, j), "later")
        swapping.append(pending)
        return zero

    def ffn0_grads(g):
        pending, _ = _reduce_scatter_after(swapping[0], g["ffn_w_down"][0], pc)
        reducing.append(pending)
        pending, zero = _reduce_start(g, local, pc, _is_ffn0, "ffn0")
        reducing.append(pending)
        return zero

    loss, grad_x, grads = local_step(x[0], loss_target[0], full, weights_of("later"), later_grads, ffn0_grads, weights_of("ffn0"))
    loss = lax.psum(loss, ("x", "y", "c"))

    small, zero = _reduce_small_start(grads)
    grads["gla_w_out"][0] = grads["gla_w_out"][0] + zero
    red = _reduce_big(grads, local, reducing, grad_x, px, py, pc)
    red.update(_reduce_small(small, red["ffn_w_down"], grads, local, px, py, pc))

    deltas, new_m, new_v = {}, {}, {}
    for n in _WEIGHTS:
        deltas[n], new_m[n], new_v[n] = adamw(local[n], red[n], given["m_" + n], given["v_" + n], name=f"adamw_{n}")
    return (loss, grad_x[None], *[red[n] for n in _WEIGHTS], *[deltas[n] for n in _WEIGHTS],
            *[new_m[n] for n in _WEIGHTS], *[new_v[n] for n in _WEIGHTS])
```

```python
import functools
import math

import jax
import jax.numpy as jnp
from jax import lax
from jax.experimental import pallas as pl
from jax.experimental.pallas import tpu as pltpu

F32 = jnp.float32
BF16 = jnp.bfloat16

D = 1024
DEPTH = 4
CH = 64
EPS = 1e-6
GLA_H, GLA_DK, GLA_DV, GLA_RANK, GLA_TAU = 4, 128, 256, 16, 16.0
GLA_QK = GLA_H * GLA_DK
GLA_VD = GLA_H * GLA_DV
GLA_IN = 2 * GLA_QK + 2 * GLA_VD + GLA_RANK
GLA_INP = 3200
SSD_DI, SSD_HD, SSD_H, SSD_G, SSD_N, SSD_K = 2048, 64, 32, 8, 128, 4
SSD_GN = SSD_G * SSD_N
SSD_CONV = SSD_DI + 2 * SSD_GN
SSD_IN = SSD_DI + SSD_CONV + SSD_H
SSD_INP = 6272
S5_GS, S5_NG, S5_P = 16, 64, 64
S5_BLK = 8
FFN_H = 2816
LANE = 128
VMEM_LIMIT = 52 * 1024 * 1024
_MATMUL_VMEM = 40 * 1024 * 1024

ADAM_LR, ADAM_B1, ADAM_B2, ADAM_EPS, ADAM_WD, ADAM_STEP = 0.001, 0.9, 0.999, 1e-08, 0.01, 10

_ARB = "arbitrary"


def _cparams(n):
    return pltpu.CompilerParams(dimension_semantics=(_ARB,) * n, vmem_limit_bytes=VMEM_LIMIT)


def _pick(n, target, mult=LANE):
    best = None
    for c in range(mult, min(n, target) + 1, mult):
        if n % c == 0:
            best = c
    return best if best is not None else n


_DN = {"nn": (((1,), (0,)), ((), ())), "nt": (((1,), (1,)), ((), ())), "tn": (((0,), (0,)), ((), ()))}


def _dot_raw(a, b, form):
    return lax.dot_general(a.astype(BF16), b.astype(BF16), _DN[form], preferred_element_type=F32)


@functools.partial(jax.custom_vjp, nondiff_argnums=(2,))
def bdot(a, b, form):
    return _dot_raw(a, b, form)


def _bdot_fwd(a, b, form):
    return _dot_raw(a, b, form), (a, b)


def _bdot_bwd(form, res, g):
    a, b = res
    if form == "nn":
        return _dot_raw(g, b, "nt"), _dot_raw(a, g, "tn")
    if form == "nt":
        return _dot_raw(g, b, "nn"), _dot_raw(g, a, "tn")
    return _dot_raw(b, g, "nt"), _dot_raw(a, g, "nn")


bdot.defvjp(_bdot_fwd, _bdot_bwd)


def _hdot(a, b):
    return jnp.dot(a, b, precision=lax.Precision.HIGHEST, preferred_element_type=F32)


@jax.custom_vjp
def cdot_left(c, ct, x):
    return _hdot(c, x)


def _cdl_fwd(c, ct, x):
    return _hdot(c, x), (c, ct)


def _cdl_bwd(res, g):
    c, ct = res
    return jnp.zeros_like(c), jnp.zeros_like(ct), _hdot(ct, g)


cdot_left.defvjp(_cdl_fwd, _cdl_bwd)


@jax.custom_vjp
def cdot_right(x, c, ct):
    return _hdot(x, c)


def _cdr_fwd(x, c, ct):
    return _hdot(x, c), (c, ct)


def _cdr_bwd(res, g):
    c, ct = res
    return _hdot(g, ct), jnp.zeros_like(c), jnp.zeros_like(ct)


cdot_right.defvjp(_cdr_fwd, _cdr_bwd)


def _sigmoid(x):
    return 1.0 / (1.0 + jnp.exp(-x))


def _silu(x):
    return x * _sigmoid(x)


def _softplus(x):
    return jnp.maximum(x, 0.0) + jnp.log(1.0 + jnp.exp(-jnp.abs(x)))


def _log_sigmoid(x):
    return jnp.minimum(x, 0.0) - jnp.log(1.0 + jnp.exp(-jnp.abs(x)))


def _gelu(x):
    c = math.sqrt(2.0 / math.pi)
    return 0.5 * x * (1.0 + jnp.tanh(c * (x + 0.044715 * (x * x * x))))


def _rms(x, g):
    return x * lax.rsqrt(jnp.mean(x * x, axis=-1, keepdims=True) + EPS) * g


def _iota2(shape, axis):
    return lax.broadcasted_iota(jnp.int32, shape, axis)


def matmul(a, b, form, *, name, G=1, out_dtype=F32, add=None):
    isz = lambda t: jnp.dtype(t.dtype).itemsize
    osz = jnp.dtype(out_dtype).itemsize + (isz(add) if add is not None else 0)

    def fits(bm, bn, bk):
        return 2 * (bm * bk * isz(a) + bk * bn * isz(b) + bm * bn * osz) + 4 * bm * bn <= _MATMUL_VMEM

    if form in ("nn", "nt"):
        M = a.shape[0]
        K = a.shape[1] // G
        N = b.shape[2] if form == "nn" else b.shape[1]
        bm, bn, bk = min(M, 1024), _pick(N, 1536), _pick(K, 2048)
        while not fits(bm, bn, bk) and bk % 256 == 0:
            bk //= 2
        nj, nk = N // bn, K // bk
        grid = (G, M // bm, nj, nk)
        a_spec = pl.BlockSpec((bm, bk), lambda g, i, j, k: (i, g * nk + k))
        if form == "nn":
            b_spec = pl.BlockSpec((None, bk, bn), lambda g, i, j, k: (g, k, j))
        else:
            b_spec = pl.BlockSpec((None, bn, bk), lambda g, i, j, k: (g, j, k))
        o_spec = pl.BlockSpec((bm, bn), lambda g, i, j, k: (i, g * nj + j))
        out_shape = jax.ShapeDtypeStruct((M, G * N), out_dtype)
    else:
        T = a.shape[0]
        Ka, Nb = a.shape[1] // G, b.shape[1] // G
        bm, bn, bk = _pick(Ka, 1408), _pick(Nb, 1536), min(T, 2048)
        while not fits(bm, bn, bk) and bk % 512 == 0:
            bk //= 2
        ni, nj, nk = Ka // bm, Nb // bn, T // bk
        grid = (G, ni, nj, nk)
        a_spec = pl.BlockSpec((bk, bm), lambda g, i, j, k: (k, g * ni + i))
        b_spec = pl.BlockSpec((bk, bn), lambda g, i, j, k: (k, g * nj + j))
        o_spec = pl.BlockSpec((None, bm, bn), lambda g, i, j, k: (g, i, j))
        out_shape = jax.ShapeDtypeStruct((G, Ka, Nb), out_dtype)
    has_add = add is not None

    def finish(refs, r):
        if has_add:
            r = r + refs[2][...].astype(F32)
        o_ref = refs[3] if has_add else refs[2]
        o_ref[...] = r.astype(o_ref.dtype)

    def body_one(*refs):
        finish(refs, _dot_raw(refs[0][...], refs[1][...], form))

    def body_acc(*refs):
        acc_ref = refs[-1]
        k = pl.program_id(3)

        @pl.when(k == 0)
        def _():
            acc_ref[...] = jnp.zeros_like(acc_ref)

        acc_ref[...] += _dot_raw(refs[0][...], refs[1][...], form)

        @pl.when(k == nk - 1)
        def _():
            finish(refs, acc_ref[...])

    in_specs = [a_spec, b_spec]
    args = [a, b]
    if has_add:
        in_specs.append(o_spec)
        args.append(add)
    return pl.pallas_call(
        body_one if nk == 1 else body_acc, name=name, grid=grid, in_specs=in_specs, out_specs=o_spec,
        out_shape=out_shape, scratch_shapes=[] if nk == 1 else [pltpu.VMEM((bm, bn), F32)],
        compiler_params=_cparams(4),
    )(*args)


def matmul_nt_norm_bwd(a, w, h, g, d, *, name):
    T, K = a.shape
    bm = min(T, 512)
    bk = _pick(K, 2048)
    nk = K // bk

    def body(a_ref, w_ref, h_ref, g_ref, d_ref, dh_ref, dg_ref, acc_ref):
        i, k = pl.program_id(0), pl.program_id(1)

        @pl.when((i == 0) & (k == 0))
        def _():
            dg_ref[...] = jnp.zeros_like(dg_ref)

        @pl.when(k == 0)
        def _():
            acc_ref[...] = jnp.zeros_like(acc_ref)

        acc_ref[...] += _dot_raw(a_ref[...], w_ref[...], "nt")

        @pl.when(k == nk - 1)
        def _():
            _, vjp = jax.vjp(lambda g_, h_: _rms(h_, g_), g_ref[...], h_ref[...])
            dg, dh = vjp(acc_ref[...])
            dh_ref[...] = dh + d_ref[...]
            dg_ref[...] += dg

    row = pl.BlockSpec((bm, D), lambda i, k: (i, 0))
    one = pl.BlockSpec((1, D), lambda i, k: (0, 0))
    return pl.pallas_call(
        body, name=name, grid=(T // bm, nk),
        in_specs=[pl.BlockSpec((bm, bk), lambda i, k: (i, k)), pl.BlockSpec((D, bk), lambda i, k: (0, k)), row, one, row],
        out_specs=[row, one], out_shape=[jax.ShapeDtypeStruct((T, D), F32), jax.ShapeDtypeStruct((1, D), F32)],
        scratch_shapes=[pltpu.VMEM((bm, D), F32)], compiler_params=_cparams(2),
    )(a, w, h, g, d)


def ffn_up(hn, w_il, *, name):
    T = hn.shape[0]
    bm, hb = min(T, 512), FFN_H // 2

    def body(a_ref, b_ref, act_ref, gu_ref):
        r = _dot_raw(a_ref[...], b_ref[...], "nn")
        act_ref[...] = (_silu(r[:, :hb]) * r[:, hb:]).astype(act_ref.dtype)
        gu_ref[...] = r.astype(gu_ref.dtype)

    return pl.pallas_call(
        body, name=name, grid=(2, T // bm),
        in_specs=[pl.BlockSpec((bm, D), lambda j, i: (i, 0)), pl.BlockSpec((D, 2 * hb), lambda j, i: (0, j))],
        out_specs=[pl.BlockSpec((bm, hb), lambda j, i: (i, j)), pl.BlockSpec((bm, 2 * hb), lambda j, i: (i, j))],
        out_shape=[jax.ShapeDtypeStruct((T, FFN_H), BF16), jax.ShapeDtypeStruct((T, 2 * FFN_H), BF16)],
        compiler_params=_cparams(2),
    )(hn, w_il)


_DACT_CHUNK = 512


def ffn_dact(d, w_down, gu, *, name):
    T = d.shape[0]
    bm, hb = min(T, 512), FFN_H // 2

    def body(d_ref, w_ref, gu_ref, o_ref):
        d_blk = d_ref[...].astype(BF16)
        for lo in range(0, hb, _DACT_CHUNK):
            hi = min(lo + _DACT_CHUNK, hb)
            da = _dot_raw(d_blk, w_ref[lo:hi, :], "nt")
            g, u = gu_ref[:, lo:hi].astype(F32), gu_ref[:, hb + lo:hb + hi].astype(F32)
            sg = _sigmoid(g)
            o_ref[:, lo:hi] = (da * u * (sg * (1.0 + g * (1.0 - sg)))).astype(o_ref.dtype)
            o_ref[:, hb + lo:hb + hi] = (da * (g * sg)).astype(o_ref.dtype)

    return pl.pallas_call(
        body, name=name, grid=(2, T // bm),
        in_specs=[pl.BlockSpec((bm, D), lambda j, i: (i, 0)), pl.BlockSpec((hb, D), lambda j, i: (j, 0)),
                  pl.BlockSpec((bm, 2 * hb), lambda j, i: (i, j))],
        out_specs=pl.BlockSpec((bm, 2 * hb), lambda j, i: (i, j)),
        out_shape=jax.ShapeDtypeStruct((T, 2 * FFN_H), BF16), compiler_params=_cparams(2),
    )(d, w_down, gu)


def _row_entry(e):
    return e if isinstance(e, tuple) else (e, e.shape[1], 0)


def _row_spec(bt, e):
    _, width, idx = e
    return pl.BlockSpec((bt, width), lambda i: (i, idx))


def _full_spec(p):
    return pl.BlockSpec(p.shape, lambda i: (0,) * p.ndim)


def rowwise(f, params, rows, outs, *, bt, name):
    rows = [_row_entry(e) for e in rows]
    T = rows[0][0].shape[0]
    bt = min(bt, T)
    np_, nr = len(params), len(rows)

    def body(*refs):
        p = tuple(r[...].astype(F32) for r in refs[:np_])
        rw = tuple(r[...].astype(F32) for r in refs[np_:np_ + nr])
        res = f(p, rw)
        for o_ref, o in zip(refs[np_ + nr:], res):
            o_ref[...] = o.astype(o_ref.dtype)

    res = pl.pallas_call(
        body, name=name, grid=(T // bt,),
        in_specs=[_full_spec(p) for p in params] + [_row_spec(bt, e) for e in rows],
        out_specs=[pl.BlockSpec((bt, w), lambda i: (i, 0)) for w, _ in outs],
        out_shape=[jax.ShapeDtypeStruct((T, w), dt) for w, dt in outs],
        compiler_params=_cparams(1),
    )(*params, *[e[0] for e in rows])
    return list(res)


def rowwise_vjp(f, params, rows, cts, drow_dtypes, *, bt, name):
    rows = [_row_entry(e) for e in rows]
    cts = [_row_entry(e) for e in cts]
    T = rows[0][0].shape[0]
    bt = min(bt, T)
    np_, nr, nc = len(params), len(rows), len(cts)
    want = [i for i, dt in enumerate(drow_dtypes) if dt is not None]

    def body(*refs):
        p = tuple(r[...].astype(F32) for r in refs[:np_])
        rw = tuple(r[...].astype(F32) for r in refs[np_:np_ + nr])
        ct = tuple(r[...].astype(F32) for r in refs[np_ + nr:np_ + nr + nc])
        outs = refs[np_ + nr + nc:]
        _, vjp = jax.vjp(f, p, rw)
        dp, dr = vjp(ct)

        @pl.when(pl.program_id(0) == 0)
        def _():
            for o in outs[:np_]:
                o[...] = jnp.zeros_like(o)

        for o, d in zip(outs[:np_], dp):
            o[...] += d
        for o, i in zip(outs[np_:], want):
            o[...] = dr[i].astype(o.dtype)

    res = pl.pallas_call(
        body, name=name, grid=(T // bt,),
        in_specs=[_full_spec(p) for p in params] + [_row_spec(bt, e) for e in rows] + [_row_spec(bt, e) for e in cts],
        out_specs=[_full_spec(p) for p in params] + [pl.BlockSpec((bt, rows[i][1]), lambda i_: (i_, 0)) for i in want],
        out_shape=[jax.ShapeDtypeStruct(p.shape, F32) for p in params]
        + [jax.ShapeDtypeStruct((T, rows[i][1]), drow_dtypes[i]) for i in want],
        compiler_params=_cparams(1),
    )(*params, *[e[0] for e in rows], *[e[0] for e in cts])
    res = list(res)
    return res[:np_], res[np_:]


def f_rmsnorm(p, r):
    return (_rms(r[0], p[0]),)


def f_rmsnorm_res(p, r):
    return (_rms(r[0], p[0]), r[0])


def f_swiglu(p, r):
    gu = r[0]
    return (_silu(gu[:, :FFN_H]) * gu[:, FFN_H:],)


def f_gla_gate_in(p, r):
    w_a2, b_a = p
    z = bdot(r[0], w_a2, "nn") + b_a
    return (_log_sigmoid(z) / GLA_TAU,)


def f_gla_gate_in_fwd(p, r):
    w_a2, b_a = p
    z = _dot_raw(r[0], w_a2, "nn") + b_a
    return (_log_sigmoid(z) / GLA_TAU,)


def f_gla_out(p, r):
    (ng,) = p
    o, rr = r
    parts = []
    for h in range(GLA_H):
        sl = slice(h * GLA_DV, (h + 1) * GLA_DV)
        parts.append(_rms(o[:, sl], ng[:, sl]) * _silu(rr[:, sl]))
    return (jnp.concatenate(parts, axis=1),)


def f_ssd_out(p, r):
    (ng,) = p
    y, z = r
    t = y * _silu(z)
    gsz = SSD_DI // SSD_G
    parts = []
    for g in range(SSD_G):
        sl = slice(g * gsz, (g + 1) * gsz)
        parts.append(_rms(t[:, sl], ng[:, sl]))
    return (jnp.concatenate(parts, axis=1),)


def f_s5_act(p, r):
    (dsk,) = p
    ycp, u = r
    return (_gelu(ycp + dsk * u),)


def f_glu_res(p, r):
    vg, h = r
    return (vg[:, :D] * _sigmoid(vg[:, D:]) + h,)


def f_glu(p, r):
    vg = r[0]
    return (vg[:, :D] * _sigmoid(vg[:, D:]),)


def loss_head(h, g, target, *, name):
    T = h.shape[0]
    bt = min(T, 256)

    def lossf(g_, h_, t_):
        e = _rms(h_, g_) - t_
        return (0.5 / D) * jnp.sum(e * e)

    def body(g_ref, h_ref, t_ref, loss_ref, dg_ref, dh_ref):
        @pl.when(pl.program_id(0) == 0)
        def _():
            loss_ref[...] = jnp.zeros_like(loss_ref)
            dg_ref[...] = jnp.zeros_like(dg_ref)

        val, vjp = jax.vjp(lossf, g_ref[...], h_ref[...], t_ref[...])
        dg, dh, _ = vjp(jnp.ones((), F32))
        loss_ref[...] += jnp.full(loss_ref.shape, val, F32)
        dg_ref[...] += dg
        dh_ref[...] = dh

    row = pl.BlockSpec((bt, D), lambda i: (i, 0))
    one = pl.BlockSpec((1, D), lambda i: (0, 0))
    loss, dg, dh = pl.pallas_call(
        body, name=name, grid=(T // bt,), in_specs=[one, row, row],
        out_specs=[pl.BlockSpec((1, LANE), lambda i: (0, 0)), one, row],
        out_shape=[jax.ShapeDtypeStruct((1, LANE), F32), jax.ShapeDtypeStruct((1, D), F32),
                   jax.ShapeDtypeStruct((T, D), F32)],
        compiler_params=_cparams(1),
    )(g, h, target)
    return loss[0, 0], dg, dh


def _gla_consts():
    r, c = _iota2((CH, CH), 0), _iota2((CH, CH), 1)
    return (r >= c).astype(F32), (r <= c).astype(F32), r >= c


def _gla_chunk(q, k, v, la, st, consts, dot, cdl):
    L, Lt, tril = consts
    lc = cdl(L, Lt, la)
    lend = lc[CH - 1:CH, :]
    e, ei = jnp.exp(lc), jnp.exp(-lc)
    qs = q * (GLA_DK ** -0.5)
    qf, kf, qb, kb = qs * e, k * ei, qs * ei, k * e
    sc = jnp.where(tril, dot(qf, kf, "nt"), dot(qb, kb, "nt"))
    o = dot(sc, v, "nn") + dot(qf, st, "nt")
    kd = k * jnp.exp(lend - lc)
    st_new = st * jnp.exp(lend) + dot(v, kd, "tn")
    return o, st_new


def _gla_block(q, k, v, la, st, nc, dot, cdl):
    consts = _gla_consts()
    outs = []
    for c in range(nc):
        sl = slice(c * CH, (c + 1) * CH)
        o, st = _gla_chunk(q[sl], k[sl], v[sl], la[sl], st, consts, dot, cdl)
        outs.append(o)
    return jnp.concatenate(outs, axis=0), st


_GLA_HP = 2


def _gla_specs(rows, rev, nb):
    t = (lambda j: nb - 1 - j) if rev else (lambda j: j)
    hp, ng = _GLA_HP, GLA_H // _GLA_HP
    q = pl.BlockSpec((rows, hp * GLA_DK), lambda h, j: (t(j), h))
    k = pl.BlockSpec((rows, hp * GLA_DK), lambda h, j: (t(j), ng + h))
    v = pl.BlockSpec((rows, hp * GLA_DV), lambda h, j: (t(j), ng + h))
    la = pl.BlockSpec((rows, hp * GLA_DK), lambda h, j: (t(j), h))
    ss = pl.BlockSpec((None, hp, GLA_DV, GLA_DK), lambda h, j: (t(j), h, 0, 0))
    o = pl.BlockSpec((rows, hp * GLA_DV), lambda h, j: (t(j), h))
    r = pl.BlockSpec((rows, hp * GLA_DV), lambda h, j: (t(j), 2 * ng + h))
    g = pl.BlockSpec((1, hp * GLA_DV), lambda h, j: (0, h))
    return q, k, v, la, ss, o, r, g


def _gla_heads(q, k, v, la, r, ng, sts, nc, dot, cdl):
    outs, new = [], []
    for i in range(_GLA_HP):
        kk, vv = slice(i * GLA_DK, (i + 1) * GLA_DK), slice(i * GLA_DV, (i + 1) * GLA_DV)
        o, st = _gla_block(q[:, kk], k[:, kk], v[:, vv], la[:, kk], sts[i], nc, dot, cdl)
        outs.append(_rms(o, ng[:, vv]) * _silu(r[:, vv]))
        new.append(st)
    return jnp.concatenate(outs, axis=1), tuple(new)


def gla_scan_fwd(proj, la, ng, *, nc, name):
    T = proj.shape[0]
    rows = min(T, nc * CH)
    nc = rows // CH
    nb = T // rows
    q_s, k_s, v_s, la_s, ss_s, o_s, r_s, g_s = _gla_specs(rows, False, nb)

    def body(q_ref, k_ref, v_ref, la_ref, r_ref, g_ref, o_ref, ss_ref, st_ref):
        @pl.when(pl.program_id(1) == 0)
        def _():
            st_ref[...] = jnp.zeros_like(st_ref)

        ss_ref[...] = st_ref[...]
        sts = tuple(st_ref[i] for i in range(_GLA_HP))
        o, sts = _gla_heads(q_ref[...], k_ref[...], v_ref[...], la_ref[...], r_ref[...], g_ref[...], sts, nc,
                            _dot_raw, lambda c, ct, x: _hdot(c, x))
        o_ref[...] = o.astype(o_ref.dtype)
        for i in range(_GLA_HP):
            st_ref[i] = sts[i]

    return pl.pallas_call(
        body, name=name, grid=(GLA_H // _GLA_HP, nb), in_specs=[q_s, k_s, v_s, la_s, r_s, g_s], out_specs=[o_s, ss_s],
        out_shape=[jax.ShapeDtypeStruct((T, GLA_VD), BF16), jax.ShapeDtypeStruct((nb, GLA_H, GLA_DV, GLA_DK), F32)],
        scratch_shapes=[pltpu.VMEM((_GLA_HP, GLA_DV, GLA_DK), F32)], compiler_params=_cparams(2),
    )(proj, proj, proj, la, proj, ng)


def gla_scan_bwd(proj, la, ng, ss, do, *, nc, name):
    T = proj.shape[0]
    rows = min(T, nc * CH)
    nc = rows // CH
    nb = T // rows
    q_s, k_s, v_s, la_s, ss_s, o_s, r_s, g_s = _gla_specs(rows, True, nb)
    t = lambda j: nb - 1 - j
    dqk_s = pl.BlockSpec((rows, _GLA_HP * GLA_DK), lambda h, j: (t(j), h))

    def body(q_ref, k_ref, v_ref, la_ref, r_ref, g_ref, ss_ref, do_ref,
             dq_ref, dk_ref, dv_ref, dla_ref, dr_ref, dg_ref, dst_ref):
        @pl.when(pl.program_id(1) == 0)
        def _():
            dst_ref[...] = jnp.zeros_like(dst_ref)
            dg_ref[...] = jnp.zeros_like(dg_ref)

        fn = lambda q, k, v, la_, r, g, *sts: _gla_heads(q, k, v, la_, r, g, sts, nc, bdot, cdot_left)
        _, vjp = jax.vjp(fn, q_ref[...], k_ref[...], v_ref[...], la_ref[...], r_ref[...], g_ref[...],
                         *[ss_ref[i] for i in range(_GLA_HP)])
        dq, dk, dv, dla, dr, dg, *dsts = vjp((do_ref[...], tuple(dst_ref[i] for i in range(_GLA_HP))))
        dq_ref[...] = dq.astype(dq_ref.dtype)
        dk_ref[...] = dk.astype(dk_ref.dtype)
        dv_ref[...] = dv.astype(dv_ref.dtype)
        dla_ref[...] = dla
        dr_ref[...] = dr.astype(dr_ref.dtype)
        dg_ref[...] += dg
        for i in range(_GLA_HP):
            dst_ref[i] = dsts[i]

    return pl.pallas_call(
        body, name=name, grid=(GLA_H // _GLA_HP, nb), in_specs=[q_s, k_s, v_s, la_s, r_s, g_s, ss_s, o_s],
        out_specs=[dqk_s, dqk_s, o_s, dqk_s, o_s, g_s],
        out_shape=[jax.ShapeDtypeStruct((T, GLA_QK), BF16), jax.ShapeDtypeStruct((T, GLA_QK), BF16),
                   jax.ShapeDtypeStruct((T, GLA_VD), BF16), jax.ShapeDtypeStruct((T, GLA_QK), F32),
                   jax.ShapeDtypeStruct((T, GLA_VD), BF16), jax.ShapeDtypeStruct((1, GLA_VD), F32)],
        scratch_shapes=[pltpu.VMEM((_GLA_HP, GLA_DV, GLA_DK), F32)], compiler_params=_cparams(2),
    )(proj, proj, proj, la, proj, ng, ss, do)


_CONV_W = 512
_CONV_OFF = SSD_DI // _CONV_W


def _conv_pre(x, prev8, w_ref, b_ref):
    bt = x.shape[0]
    ext = jnp.concatenate([prev8, x], axis=0)
    shifted = []
    for j in range(SSD_K):
        s = SSD_K - 1 - j
        shifted.append(x if s == 0 else pltpu.roll(ext, s, 0)[8:8 + bt])
    pre = b_ref[...] + sum(w_ref[j:j + 1, :] * shifted[j] for j in range(SSD_K))
    return pre, shifted


def ssd_conv_fwd(proj, w, b, *, name):
    T = proj.shape[0]
    bt = min(T, 512)
    nb = T // bt

    def body(x_ref, w_ref, b_ref, o_ref, carry_ref):
        @pl.when(pl.program_id(1) == 0)
        def _():
            carry_ref[...] = jnp.zeros_like(carry_ref)

        x = x_ref[...]
        pre, _ = _conv_pre(x, carry_ref[...], w_ref, b_ref)
        o_ref[...] = _silu(pre)
        carry_ref[...] = x[bt - 8:, :]

    return pl.pallas_call(
        body, name=name, grid=(SSD_CONV // _CONV_W, nb),
        in_specs=[pl.BlockSpec((bt, _CONV_W), lambda c, t: (t, _CONV_OFF + c)),
                  pl.BlockSpec((SSD_K, _CONV_W), lambda c, t: (0, c)),
                  pl.BlockSpec((1, _CONV_W), lambda c, t: (0, c))],
        out_specs=pl.BlockSpec((bt, _CONV_W), lambda c, t: (t, c)),
        out_shape=jax.ShapeDtypeStruct((T, SSD_CONV), F32),
        scratch_shapes=[pltpu.VMEM((8, _CONV_W), F32)], compiler_params=_cparams(2),
    )(proj, w, b)


def ssd_conv_bwd(proj, w, b, dout, *, name):
    T = proj.shape[0]
    bt = min(T, 512)
    nb = T // bt
    r8 = bt // 8

    def body(x_ref, xp_ref, w_ref, b_ref, do_ref, dx_ref, dw_ref, db_ref, carry_ref):
        t = pl.program_id(1)

        @pl.when(t == 0)
        def _():
            carry_ref[...] = jnp.zeros_like(carry_ref)
            dw_ref[...] = jnp.zeros_like(dw_ref)
            db_ref[...] = jnp.zeros_like(db_ref)

        x = x_ref[...]
        prev8 = jnp.where(t == nb - 1, 0.0, xp_ref[...])
        pre, shifted = _conv_pre(x, prev8, w_ref, b_ref)
        sg = _sigmoid(pre)
        dpre = do_ref[...] * (sg * (1.0 + pre * (1.0 - sg)))
        ext = jnp.concatenate([dpre, carry_ref[...]], axis=0)
        dx = w_ref[SSD_K - 1:SSD_K, :] * dpre
        for j in range(SSD_K - 1):
            s = SSD_K - 1 - j
            dx = dx + w_ref[j:j + 1, :] * pltpu.roll(ext, bt + 8 - s, 0)[:bt]
        dx_ref[...] = dx.astype(dx_ref.dtype)
        dw_ref[...] += jnp.concatenate([jnp.sum(dpre * shifted[j], axis=0, keepdims=True) for j in range(SSD_K)], axis=0)
        db_ref[...] += jnp.sum(dpre, axis=0, keepdims=True)
        carry_ref[...] = dpre[:8, :]

    rt = lambda t: nb - 1 - t
    return pl.pallas_call(
        body, name=name, grid=(SSD_CONV // _CONV_W, nb),
        in_specs=[pl.BlockSpec((bt, _CONV_W), lambda c, t: (rt(t), _CONV_OFF + c)),
                  pl.BlockSpec((8, _CONV_W), lambda c, t: (jnp.maximum(rt(t) * r8 - 1, 0), _CONV_OFF + c)),
                  pl.BlockSpec((SSD_K, _CONV_W), lambda c, t: (0, c)),
                  pl.BlockSpec((1, _CONV_W), lambda c, t: (0, c)),
                  pl.BlockSpec((bt, _CONV_W), lambda c, t: (rt(t), c))],
        out_specs=[pl.BlockSpec((bt, _CONV_W), lambda c, t: (rt(t), c)),
                   pl.BlockSpec((SSD_K, _CONV_W), lambda c, t: (0, c)),
                   pl.BlockSpec((1, _CONV_W), lambda c, t: (0, c))],
        out_shape=[jax.ShapeDtypeStruct((T, SSD_CONV), BF16), jax.ShapeDtypeStruct((SSD_K, SSD_CONV), F32),
                   jax.ShapeDtypeStruct((1, SSD_CONV), F32)],
        scratch_shapes=[pltpu.VMEM((8, _CONV_W), F32)], compiler_params=_cparams(2),
    )(proj, proj, w, b, dout)


_SSD_U = 2 * CH


def _ssd_unit(xs, bm, cm, dtraw, dtb, alog, dsk, hp, g, dot, cdl, cdr):
    U, P2 = _SSD_U, 2 * SSD_HD
    r, c = _iota2((U, U), 0), _iota2((U, U), 1)
    same = (r // CH) == (c // CH)
    Lb = (same & (r >= c)).astype(F32)
    Ub = (same & (r <= c)).astype(F32)
    lane = _iota2((1, U), 1)
    lo_lane = _iota2((1, P2), 1) < SSD_HD
    lo_sub = _iota2((P2, 1), 0) < SSD_HD
    diag2 = (_iota2((CH, P2), 0) == (_iota2((CH, P2), 1) % CH)).astype(F32)

    dt = _softplus(dtraw + dtb)
    da = dt * (-jnp.exp(alog))
    cum = cdl(Lb, Ub, da)
    ys = []
    new_hp = []
    for pr in range(2):
        xs_p = xs[:, pr * P2:(pr + 1) * P2]
        cols, dts, dks = [], [], []
        for jj in range(2):
            oh_l = (lane == g * (SSD_H // SSD_G) + 2 * pr + jj).astype(F32)
            cols.append(jnp.sum(cum * oh_l, axis=1, keepdims=True))
            dts.append(jnp.sum(dt * oh_l, axis=1, keepdims=True))
            dks.append(jnp.sum(dsk * oh_l, axis=1, keepdims=True))
        dsk_p = jnp.where(lo_lane, dks[0], dks[1])
        h = hp[pr]
        yc = []
        for ci in range(2):
            sl = slice(ci * CH, (ci + 1) * CH)
            xs_c, bm_c, cm_c = xs_p[sl], bm[sl], cm[sl]
            col = jnp.where(lo_lane, cols[0][sl], cols[1][sl])
            dtc = jnp.where(lo_lane, dts[0][sl], dts[1][sl])
            row = jnp.sum(diag2 * col, axis=0, keepdims=True)
            dtrow = jnp.sum(diag2 * dtc, axis=0, keepdims=True)
            cb = dot(cm_c, jnp.concatenate([bm_c, bm_c], axis=0), "nt")
            mix = cb * jnp.exp(-jnp.abs(col - row)) * dtrow
            xbd = jnp.concatenate([jnp.where(lo_lane, xs_c, 0.0), jnp.where(lo_lane, 0.0, xs_c)], axis=0)
            y_intra = dot(mix, xbd, "nn")
            ce = jnp.where(lo_lane, cols[0][ci * CH + CH - 1:ci * CH + CH, :], cols[1][ci * CH + CH - 1:ci * CH + CH, :])
            y_inter = dot(cm_c, h, "nt") * jnp.exp(col)
            xw = xs_c * (dtc * jnp.exp(ce - col))
            ce_s = [cols[jj][ci * CH + CH - 1:ci * CH + CH, :] for jj in range(2)]
            a_p = jnp.where(lo_sub, jnp.exp(ce_s[0]), jnp.exp(ce_s[1]))
            h = a_p * h + dot(xw, bm_c, "tn")
            yc.append(y_intra + y_inter + dsk_p * xs_c)
        ys.append(jnp.concatenate(yc, axis=0))
        new_hp.append(h)
    return jnp.concatenate(ys, axis=1), tuple(new_hp)


def _ssd_block(xs, bm, cm, dtraw, z, dtb, alog, dsk, ng, hp, g, nu, dot, cdl, cdr):
    outs = []
    for u in range(nu):
        sl = slice(u * _SSD_U, (u + 1) * _SSD_U)
        y, hp = _ssd_unit(xs[sl], bm[sl], cm[sl], dtraw[sl], dtb, alog, dsk, hp, g, dot, cdl, cdr)
        outs.append(y)
    return _rms(jnp.concatenate(outs, axis=0) * _silu(z), ng), hp


def _ssd_specs(rows, rev, nb):
    t = (lambda j: nb - 1 - j) if rev else (lambda j: j)
    gw = SSD_DI // SSD_G
    xs = pl.BlockSpec((rows, gw), lambda j, g: (t(j), g))
    bm = pl.BlockSpec((rows, SSD_N), lambda j, g: (t(j), SSD_DI // SSD_N + g))
    cm = pl.BlockSpec((rows, SSD_N), lambda j, g: (t(j), (SSD_DI + SSD_GN) // SSD_N + g))
    dtr = pl.BlockSpec((rows, LANE), lambda j, g: (t(j), (SSD_DI + SSD_CONV) // LANE))
    par = pl.BlockSpec((1, LANE), lambda j, g: (0, 0))
    hs = pl.BlockSpec((None, None, 2, 2 * SSD_HD, SSD_N), lambda j, g: (t(j), g, 0, 0, 0))
    y = pl.BlockSpec((rows, gw), lambda j, g: (t(j), g))
    ng = pl.BlockSpec((1, gw), lambda j, g: (0, g))
    return xs, bm, cm, dtr, par, hs, y, ng


def ssd_scan_fwd(xbc, proj, dtb, alog, dsk, ng, *, nu, name):
    T = xbc.shape[0]
    rows = min(T, nu * _SSD_U)
    nu = rows // _SSD_U
    nb = T // rows
    xs_s, bm_s, cm_s, dt_s, par_s, hs_s, y_s, ng_s = _ssd_specs(rows, False, nb)

    def body(xs_ref, bm_ref, cm_ref, dt_ref, z_ref, dtb_ref, al_ref, dsk_ref, ng_ref, y_ref, hs_ref, h_ref):
        g = pl.program_id(1)

        @pl.when(pl.program_id(0) == 0)
        def _():
            h_ref[g] = jnp.zeros(h_ref.shape[1:], F32)

        hs_ref[...] = h_ref[g]
        hp = (h_ref[g, 0], h_ref[g, 1])
        y, hp = _ssd_block(xs_ref[...], bm_ref[...], cm_ref[...], dt_ref[...], z_ref[...], dtb_ref[...], al_ref[...],
                           dsk_ref[...], ng_ref[...], hp, g, nu, _dot_raw, lambda c, ct, x: _hdot(c, x), lambda x, c, ct: _hdot(x, c))
        y_ref[...] = y.astype(y_ref.dtype)
        h_ref[g, 0] = hp[0]
        h_ref[g, 1] = hp[1]

    return pl.pallas_call(
        body, name=name, grid=(nb, SSD_G), in_specs=[xs_s, bm_s, cm_s, dt_s, y_s, par_s, par_s, par_s, ng_s],
        out_specs=[y_s, hs_s],
        out_shape=[jax.ShapeDtypeStruct((T, SSD_DI), BF16), jax.ShapeDtypeStruct((nb, SSD_G, 2, 2 * SSD_HD, SSD_N), F32)],
        scratch_shapes=[pltpu.VMEM((SSD_G, 2, 2 * SSD_HD, SSD_N), F32)], compiler_params=_cparams(2),
    )(xbc, xbc, xbc, proj, proj, dtb, alog, dsk, ng)


def ssd_scan_bwd(xbc, proj, dtb, alog, dsk, ng, hs, dy, *, nu, name):
    T = xbc.shape[0]
    rows = min(T, nu * _SSD_U)
    nu = rows // _SSD_U
    nb = T // rows
    xs_s, bm_s, cm_s, dt_s, par_s, hs_s, y_s, ng_s = _ssd_specs(rows, True, nb)
    gw = SSD_DI // SSD_G
    dng_s = pl.BlockSpec((1, SSD_DI), lambda j, g: (0, 0))
    t = lambda j: nb - 1 - j
    n_s = pl.BlockSpec((rows, SSD_N), lambda j, g: (t(j), g))
    ddt_s = pl.BlockSpec((rows, LANE), lambda j, g: (t(j), 0))

    def body(xs_ref, bm_ref, cm_ref, dt_ref, z_ref, dtb_ref, al_ref, dsk_ref, ng_ref, hs_ref, dy_ref,
             dxs_ref, dbm_ref, dcm_ref, ddt_ref, ddtb_ref, dal_ref, ddsk_ref, dz_ref, dng_ref, dh_ref):
        j, g = pl.program_id(0), pl.program_id(1)

        @pl.when(j == 0)
        def _():
            dh_ref[g] = jnp.zeros(dh_ref.shape[1:], F32)

        @pl.when((j == 0) & (g == 0))
        def _():
            ddtb_ref[...] = jnp.zeros_like(ddtb_ref)
            dal_ref[...] = jnp.zeros_like(dal_ref)
            ddsk_ref[...] = jnp.zeros_like(ddsk_ref)
            dng_ref[...] = jnp.zeros_like(dng_ref)

        @pl.when(g == 0)
        def _():
            ddt_ref[...] = jnp.zeros_like(ddt_ref)

        fn = lambda xs, bm, cm, dtr, z, dtb_, al, dsk_, ng_, h0, h1: _ssd_block(
            xs, bm, cm, dtr, z, dtb_, al, dsk_, ng_, (h0, h1), g, nu, bdot, cdot_left, cdot_right)
        _, vjp = jax.vjp(fn, xs_ref[...], bm_ref[...], cm_ref[...], dt_ref[...], z_ref[...], dtb_ref[...], al_ref[...],
                         dsk_ref[...], ng_ref[...], hs_ref[0], hs_ref[1])
        dxs, dbm, dcm, ddt, dz, ddtb, dal, ddsk, dng, dh0, dh1 = vjp((dy_ref[...], (dh_ref[g, 0], dh_ref[g, 1])))
        dz_ref[...] = dz.astype(dz_ref.dtype)
        lanes = pl.ds(pl.multiple_of(g * gw, gw), gw)
        dng_ref[:, lanes] = dng_ref[:, lanes] + dng
        dxs_ref[...] = dxs
        dbm_ref[...] = dbm
        dcm_ref[...] = dcm
        ddt_ref[...] += ddt
        ddtb_ref[...] += ddtb
        dal_ref[...] += dal
        ddsk_ref[...] += ddsk
        dh_ref[g, 0] = dh0
        dh_ref[g, 1] = dh1

    return pl.pallas_call(
        body, name=name, grid=(nb, SSD_G), in_specs=[xs_s, bm_s, cm_s, dt_s, y_s, par_s, par_s, par_s, ng_s, hs_s, y_s],
        out_specs=[y_s, n_s, n_s, ddt_s, par_s, par_s, par_s, y_s, dng_s],
        out_shape=[jax.ShapeDtypeStruct((T, SSD_DI), F32), jax.ShapeDtypeStruct((T, SSD_GN), F32),
                   jax.ShapeDtypeStruct((T, SSD_GN), F32), jax.ShapeDtypeStruct((T, LANE), F32),
                   jax.ShapeDtypeStruct((1, LANE), F32), jax.ShapeDtypeStruct((1, LANE), F32),
                   jax.ShapeDtypeStruct((1, LANE), F32), jax.ShapeDtypeStruct((T, SSD_DI), BF16),
                   jax.ShapeDtypeStruct((1, SSD_DI), F32)],
        scratch_shapes=[pltpu.VMEM((SSD_G, 2, 2 * SSD_HD, SSD_N), F32)], compiler_params=_cparams(2),
    )(xbc, xbc, xbc, proj, proj, dtb, alog, dsk, ng, hs, dy)


def _s5_param_f(log_dt, a_re, a_im, bre_t, bim_t, cim, cdl):
    n = S5_NG * S5_GS
    r, c = _iota2((n, S5_NG), 0), _iota2((n, S5_NG), 1)
    E = ((r // S5_GS) == c).astype(F32)
    rt, ct = _iota2((S5_NG, n), 0), _iota2((S5_NG, n), 1)
    Et = ((ct // S5_GS) == rt).astype(F32)
    step = jnp.exp(log_dt)
    mag = jnp.exp(step * a_re)
    abr = mag * jnp.cos(step * a_im)
    abi = mag * jnp.sin(step * a_im)
    den = a_re * a_re + a_im * a_im
    nr, ni = abr - 1.0, abi
    fr = (nr * a_re + ni * a_im) / den
    fi = (ni * a_re - nr * a_im) / den
    Fr, Fi = cdl(E, Et, fr), cdl(E, Et, fi)
    bbr = Fr * bre_t - Fi * bim_t
    bbi = Fr * bim_t + Fi * bre_t
    return abr, abi, bbr, bbi, -cim


def _whole(a):
    return pl.BlockSpec(a.shape, lambda: (0,) * a.ndim)


def s5_param_fwd(args, *, name):
    def body(*refs):
        res = _s5_param_f(*[r[...] for r in refs[:6]], lambda c, ct, x: _hdot(c, x))
        for o, v in zip(refs[6:], res):
            o[...] = v

    shapes = [(S5_NG, S5_P), (S5_NG, S5_P)] + [(S5_NG * S5_GS, S5_P)] * 3
    return pl.pallas_call(
        body, name=name, in_specs=[_whole(a) for a in args], out_specs=[pl.BlockSpec(s, lambda: (0, 0)) for s in shapes],
        out_shape=[jax.ShapeDtypeStruct(s, F32) for s in shapes],
        compiler_params=pltpu.CompilerParams(vmem_limit_bytes=VMEM_LIMIT),
    )(*args)


def s5_param_bwd(args, cts, *, name):
    def body(*refs):
        fn = lambda *a: _s5_param_f(*a, cdot_left)
        _, vjp = jax.vjp(fn, *[r[...] for r in refs[:6]])
        grads = vjp(tuple(r[...] for r in refs[6:11]))
        for o, v in zip(refs[11:], grads):
            o[...] = v

    return pl.pallas_call(
        body, name=name, in_specs=[_whole(a) for a in list(args) + list(cts)],
        out_specs=[_whole(a) for a in args], out_shape=[jax.ShapeDtypeStruct(a.shape, F32) for a in args],
        compiler_params=pltpu.CompilerParams(vmem_limit_bytes=VMEM_LIMIT),
    )(*args, *cts)


_S5_W = S5_BLK * S5_P


def _cmul_add(xr, xi, pr, pi, sr, si):
    return xr + (pr * sr - pi * si), xi + (pr * si + pi * sr)


def _s5_powers(ar, ai):
    pw = [(ar, ai)]
    for _ in range(7):
        qr, qi = pw[-1]
        pw.append((qr * ar - qi * ai, qr * ai + qi * ar))
    return pw


def s5_scan_fwd(u, wb, a_re, a_im, wc, *, name):
    T = u.shape[0]
    bt = min(T, 256)
    nb = T // bt

    def body(u_ref, wb_ref, ar_ref, ai_ref, wc_ref, x_ref, y_ref, bu_ref, carry_ref):
        @pl.when(pl.program_id(1) == 0)
        def _():
            carry_ref[...] = jnp.zeros_like(carry_ref)

        bu_ref[...] = _dot_raw(u_ref[...], wb_ref[...], "nn")
        ar, ai = ar_ref[...], ai_ref[...]
        pw = _s5_powers(ar, ai)
        pwr = jnp.concatenate([p[0] for p in pw], axis=0)
        pwi = jnp.concatenate([p[1] for p in pw], axis=0)
        rin = _iota2((8, _S5_W), 0)
        cr, ci = carry_ref[0:1, :], carry_ref[1:2, :]
        for t in range(bt // 8):
            sl = slice(8 * t, 8 * t + 8)
            xr, xi = bu_ref[sl, :_S5_W], bu_ref[sl, _S5_W:]
            for s in (1, 2, 4):
                m = rin >= s
                sr = jnp.where(m, pltpu.roll(xr, s, 0), 0.0)
                si = jnp.where(m, pltpu.roll(xi, s, 0), 0.0)
                xr, xi = _cmul_add(xr, xi, *pw[s - 1], sr, si)
            xr, xi = _cmul_add(xr, xi, pwr, pwi, cr, ci)
            x_ref[sl, :_S5_W] = xr
            x_ref[sl, _S5_W:] = xi
            cr, ci = xr[7:8, :], xi[7:8, :]
        carry_ref[0:1, :] = cr
        carry_ref[1:2, :] = ci
        y_ref[...] = _dot_raw(x_ref[...], wc_ref[...], "nn")

    nblk = S5_NG // S5_BLK
    blk = pl.BlockSpec((bt, 2 * _S5_W), lambda g, t: (t, g))
    col = pl.BlockSpec((bt, LANE), lambda g, t: (t, g))
    a_s = pl.BlockSpec((None, 1, _S5_W), lambda g, t: (g, 0, 0))
    wb_s = pl.BlockSpec((None, LANE, 2 * _S5_W), lambda g, t: (g, 0, 0))
    wc_s = pl.BlockSpec((None, 2 * _S5_W, LANE), lambda g, t: (g, 0, 0))
    return pl.pallas_call(
        body, name=name, grid=(nblk, nb), in_specs=[col, wb_s, a_s, a_s, wc_s], out_specs=[blk, col],
        out_shape=[jax.ShapeDtypeStruct((T, nblk * 2 * _S5_W), F32), jax.ShapeDtypeStruct((T, nblk * LANE), F32)],
        scratch_shapes=[pltpu.VMEM((bt, 2 * _S5_W), F32), pltpu.VMEM((8, _S5_W), F32)],
        compiler_params=_cparams(2),
    )(u, wb, a_re, a_im, wc)


def s5_scan_bwd(dy, x, u, wb, a_re, a_im, wc, *, name):
    T = dy.shape[0]
    bt = min(T, 256)
    nb = T // bt

    def body(dy_ref, x_ref, u_ref, wb_ref, ar_ref, ai_ref, wc_ref, du_ref, dwb_ref, dwc_ref, dar_ref, dai_ref,
             g_ref, lam_ref, carry_ref):
        @pl.when(pl.program_id(1) == 0)
        def _():
            carry_ref[...] = jnp.zeros_like(carry_ref)
            dar_ref[...] = jnp.zeros_like(dar_ref)
            dai_ref[...] = jnp.zeros_like(dai_ref)
            dwb_ref[...] = jnp.zeros_like(dwb_ref)
            dwc_ref[...] = jnp.zeros_like(dwc_ref)

        g_ref[...] = _dot_raw(dy_ref[...], wc_ref[...], "nt")
        pw = _s5_powers(ar_ref[...], -ai_ref[...])
        pwr = jnp.concatenate([p[0] for p in reversed(pw)], axis=0)
        pwi = jnp.concatenate([p[1] for p in reversed(pw)], axis=0)
        rin = _iota2((8, _S5_W), 0)
        cr, ci = carry_ref[0:1, :], carry_ref[1:2, :]
        acc_r = jnp.zeros((8, _S5_W), F32)
        acc_i = jnp.zeros((8, _S5_W), F32)
        for t in reversed(range(bt // 8)):
            sl = slice(8 * t, 8 * t + 8)
            lr, li = g_ref[sl, :_S5_W], g_ref[sl, _S5_W:]
            for s in (1, 2, 4):
                m = rin < 8 - s
                sr = jnp.where(m, pltpu.roll(lr, 8 - s, 0), 0.0)
                si = jnp.where(m, pltpu.roll(li, 8 - s, 0), 0.0)
                lr, li = _cmul_add(lr, li, *pw[s - 1], sr, si)
            lr, li = _cmul_add(lr, li, pwr, pwi, cr, ci)
            lam_ref[sl, :_S5_W] = lr
            lam_ref[sl, _S5_W:] = li
            nr = jnp.where(rin == 7, cr, pltpu.roll(lr, 7, 0))
            ni = jnp.where(rin == 7, ci, pltpu.roll(li, 7, 0))
            xr, xi = x_ref[sl, :_S5_W], x_ref[sl, _S5_W:]
            acc_r = acc_r + (xr * nr + xi * ni)
            acc_i = acc_i + (xr * ni - xi * nr)
            cr, ci = lr[0:1, :], li[0:1, :]
        carry_ref[0:1, :] = cr
        carry_ref[1:2, :] = ci
        dar_ref[...] += jnp.sum(acc_r, axis=0, keepdims=True)
        dai_ref[...] += jnp.sum(acc_i, axis=0, keepdims=True)
        lam = lam_ref[...]
        du_ref[...] = _dot_raw(lam, wb_ref[...], "nt")
        dwb_ref[...] += _dot_raw(u_ref[...], lam, "tn")
        dwc_ref[...] += _dot_raw(x_ref[...], dy_ref[...], "tn")

    nblk = S5_NG // S5_BLK
    blk = pl.BlockSpec((bt, 2 * _S5_W), lambda g, t: (nb - 1 - t, g))
    col = pl.BlockSpec((bt, LANE), lambda g, t: (nb - 1 - t, g))
    a_s = pl.BlockSpec((None, 1, _S5_W), lambda g, t: (g, 0, 0))
    wb_s = pl.BlockSpec((None, LANE, 2 * _S5_W), lambda g, t: (g, 0, 0))
    wc_s = pl.BlockSpec((None, 2 * _S5_W, LANE), lambda g, t: (g, 0, 0))
    return pl.pallas_call(
        body, name=name, grid=(nblk, nb), in_specs=[col, blk, col, wb_s, a_s, a_s, wc_s],
        out_specs=[col, wb_s, wc_s, a_s, a_s],
        out_shape=[jax.ShapeDtypeStruct((T, nblk * LANE), F32), jax.ShapeDtypeStruct((nblk, LANE, 2 * _S5_W), F32),
                   jax.ShapeDtypeStruct((nblk, 2 * _S5_W, LANE), F32), jax.ShapeDtypeStruct((nblk, 1, _S5_W), F32),
                   jax.ShapeDtypeStruct((nblk, 1, _S5_W), F32)],
        scratch_shapes=[pltpu.VMEM((bt, 2 * _S5_W), F32), pltpu.VMEM((bt, 2 * _S5_W), F32), pltpu.VMEM((8, _S5_W), F32)],
        compiler_params=_cparams(2),
    )(dy, x, u, wb, a_re, a_im, wc)


def _norm_bf16(h, g, name):
    return rowwise(f_rmsnorm, [g], [h], [(D, BF16)], bt=512, name=name)[0]


def _norm_bwd(h, g, cts, name):
    n = len(cts) - 1

    def f(p, r):
        y = _rms(r[0], p[0])
        return (y,) * n + (r[0],)

    (dg,), (dh,) = rowwise_vjp(f, [g], [h], cts, [F32], bt=256, name=name)
    return dh, dg


def ffn_fwd(h, g, w_gu, w_down, tag):
    hn = _norm_bf16(h, g, f"{tag}_norm")
    a, gu = ffn_up(hn, w_gu, name=f"{tag}_up")
    h2 = matmul(a, w_down[None], "nn", add=h, name=f"{tag}_down")
    return h2, (h, hn, gu, a)


def ffn_bwd(d, saved, g, w_gu, w_down, tag):
    h, hn, gu, a = saved
    dgu = ffn_dact(d, w_down, gu, name=f"{tag}_dact")
    dwd = matmul(a, d, "tn", name=f"{tag}_dwd")[0]
    dwgu = matmul(hn, dgu, "tn", name=f"{tag}_dwgu")[0]
    dh, dg = matmul_nt_norm_bwd(dgu, w_gu, h, g, d, name=f"{tag}_dhn")
    return dh, dg, dwgu, dwd


_GLA_NC = 4
_SSD_NU = 4


def gla_fwd(h, gm, w_in, w_a2, b_a, ng, w_out, tag):
    hn = _norm_bf16(h, gm, f"{tag}_norm")
    proj = matmul(hn, w_in[None], "nn", name=f"{tag}_in")
    alow = (proj, LANE, 2 * (GLA_QK + GLA_VD) // LANE)
    la = rowwise(f_gla_gate_in_fwd, [w_a2, b_a], [alow], [(GLA_QK, F32)], bt=512, name=f"{tag}_gate")[0]
    og, ss = gla_scan_fwd(proj, la, ng, nc=_GLA_NC, name=f"{tag}_scan")
    h2 = matmul(og, w_out[None], "nn", add=h, name=f"{tag}_proj")
    return h2, (h, hn, proj, la, ss, og)


def gla_bwd(d, saved, gm, w_in, w_a2, b_a, ng, w_out, tag):
    h, hn, proj, la, ss, og = saved
    dog = matmul(d, w_out[None], "nt", name=f"{tag}_dog")
    dwout = matmul(og, d, "tn", name=f"{tag}_dwout")[0]
    dq, dk, dv, dla, dr, dng = gla_scan_bwd(proj, la, ng, ss, dog, nc=_GLA_NC, name=f"{tag}_dscan")
    alow = (proj, LANE, 2 * (GLA_QK + GLA_VD) // LANE)
    (dwa2, dba), (dalow,) = rowwise_vjp(f_gla_gate_in, [w_a2, b_a], [alow], [dla], [BF16], bt=512, name=f"{tag}_dgate")
    dproj = jnp.concatenate([dq, dk, dv, dr, dalow], axis=1)
    dwin = matmul(hn, dproj, "tn", name=f"{tag}_dwin")[0]
    dh, dgm = matmul_nt_norm_bwd(dproj, w_in, h, gm, d, name=f"{tag}_dhn")
    return dh, dgm, dwin, dwa2[:GLA_RANK], dba, dng, dwout


def ssd_fwd(h, gm, w_in, conv_w, conv_b, dtb, alog, dsk, ng, w_out, tag):
    hn = _norm_bf16(h, gm, f"{tag}_norm")
    proj = matmul(hn, w_in[None], "nn", name=f"{tag}_in")
    xbc = ssd_conv_fwd(proj, conv_w, conv_b, name=f"{tag}_conv")
    yg, hs = ssd_scan_fwd(xbc, proj, dtb, alog, dsk, ng, nu=_SSD_NU, name=f"{tag}_scan")
    h2 = matmul(yg, w_out[None], "nn", add=h, name=f"{tag}_proj")
    return h2, (h, hn, proj, xbc, hs, yg)


def ssd_bwd(d, saved, gm, w_in, conv_w, conv_b, dtb, alog, dsk, ng, w_out, tag):
    h, hn, proj, xbc, hs, yg = saved
    dyg = matmul(d, w_out[None], "nt", name=f"{tag}_dyg")
    dwout = matmul(yg, d, "tn", name=f"{tag}_dwout")[0]
    dxs, dbm, dcm, ddt, ddtb, dal, ddsk, dz, dng = ssd_scan_bwd(xbc, proj, dtb, alog, dsk, ng, hs, dyg, nu=_SSD_NU,
                                                               name=f"{tag}_dscan")
    dxbc = jnp.concatenate([dxs, dbm, dcm], axis=1)
    dpre, dcw, dcb = ssd_conv_bwd(proj, conv_w, conv_b, dxbc, name=f"{tag}_dconv")
    dproj = jnp.concatenate([dz, dpre, ddt.astype(BF16)], axis=1)
    dwin = matmul(hn, dproj, "tn", name=f"{tag}_dwin")[0]
    dh, dgm = matmul_nt_norm_bwd(dproj, w_in, h, gm, d, name=f"{tag}_dhn")
    return (dh, dgm, dwin, dcw, dcb, ddtb[:, :SSD_H], dal[:, :SSD_H], ddsk[:, :SSD_H], dng, dwout)


_S5_NB = S5_NG // S5_BLK


def _s5_param_args(log_dt, a_re, a_im, b_re, b_im, c_im):
    n = S5_NG * S5_GS
    tr = lambda b: jnp.transpose(b, (0, 2, 1)).reshape(n, S5_P)
    return [log_dt.reshape(S5_NG, 1), a_re, a_im, tr(b_re), tr(b_im), c_im.reshape(n, S5_P)]


def _s5_blockdiag(t):
    nb, gl, a, b = t.shape
    eye = jnp.eye(gl, dtype=t.dtype)
    return (t[:, :, :, None, :] * eye[None, :, None, :, None]).reshape(nb, gl * a, gl * b)


def _s5_diag(t, a, b):
    nb = t.shape[0]
    gl = t.shape[1] // a
    eye = jnp.eye(gl, dtype=t.dtype)
    return jnp.sum(t.reshape(nb, gl, a, gl, b) * eye[None, :, None, :, None], axis=3)


def _s5_weights(bbr, bbi, c_re, cneg):
    sh = (_S5_NB, S5_BLK, S5_GS, S5_P)
    wb = jnp.concatenate([_s5_blockdiag(bbr.reshape(sh)), _s5_blockdiag(bbi.reshape(sh))], axis=2)
    tr = lambda cc: jnp.transpose(cc.reshape(sh), (0, 1, 3, 2))
    wc = jnp.concatenate([_s5_blockdiag(tr(c_re)), _s5_blockdiag(tr(cneg))], axis=1)
    return wb, wc


def s5_fwd(h, gm, prm, dsk, w_glu, tag):
    log_dt, a_re, a_im, b_re, b_im, c_re, c_im = prm
    hn = rowwise(f_rmsnorm, [gm], [h], [(D, F32)], bt=512, name=f"{tag}_norm")[0]
    pargs = _s5_param_args(log_dt, a_re, a_im, b_re, b_im, c_im)
    abr, abi, bbr, bbi, cneg = s5_param_fwd(pargs, name=f"{tag}_param")
    wb, wc = _s5_weights(bbr, bbi, c_re.reshape(S5_NG * S5_GS, S5_P), cneg)
    ar, ai = abr.reshape(_S5_NB, 1, _S5_W), abi.reshape(_S5_NB, 1, _S5_W)
    wb, wc = wb.astype(BF16), wc.astype(BF16)
    x, ycp = s5_scan_fwd(hn, wb, ar, ai, wc, name=f"{tag}_scan")
    yg = rowwise(f_s5_act, [dsk], [ycp, hn], [(D, BF16)], bt=512, name=f"{tag}_act")[0]
    vg = matmul(yg, w_glu[None], "nn", name=f"{tag}_glu")
    h2 = rowwise(f_glu_res, [], [vg, h], [(D, F32)], bt=512, name=f"{tag}_out")[0]
    return h2, (h, hn, pargs, wb, wc, ar, ai, x, ycp, yg, vg)


def s5_bwd(d, saved, gm, dsk, w_glu, tag):
    h, hn, pargs, wb, wc, ar, ai, x, ycp, yg, vg = saved
    _, (dvg,) = rowwise_vjp(f_glu, [], [vg], [d], [BF16], bt=256, name=f"{tag}_dout")
    dwglu = matmul(yg, dvg, "tn", name=f"{tag}_dwglu")[0]
    dyg = matmul(dvg, w_glu[None], "nt", name=f"{tag}_dyg")
    (ddsk,), (dycp, dhn1) = rowwise_vjp(f_s5_act, [dsk], [ycp, hn], [dyg], [F32, F32], bt=256, name=f"{tag}_dact")
    dhn2, dwb, dwc, dar, dai = s5_scan_bwd(dycp, x, hn, wb, ar, ai, wc, name=f"{tag}_dscan")
    dh, dgm = _norm_bwd(h, gm, [dhn1, dhn2, d], f"{tag}_dnorm")
    n = S5_NG * S5_GS
    half = S5_BLK * S5_P
    d_bbr = _s5_diag(dwb[:, :, :half], S5_GS, S5_P).reshape(n, S5_P)
    d_bbi = _s5_diag(dwb[:, :, half:], S5_GS, S5_P).reshape(n, S5_P)
    from_c = lambda t: jnp.transpose(_s5_diag(t, S5_P, S5_GS), (0, 1, 3, 2)).reshape(n, S5_P)
    d_cre = from_c(dwc[:, :half, :])
    d_cneg = from_c(dwc[:, half:, :])
    cts = [dar.reshape(S5_NG, S5_P), dai.reshape(S5_NG, S5_P), d_bbr, d_bbi, d_cneg]
    dlog, dare, daim, dbre_t, dbim_t, dcim = s5_param_bwd(pargs, cts, name=f"{tag}_dparam")
    untr = lambda t: jnp.transpose(t.reshape(S5_NG, S5_GS, S5_P), (0, 2, 1))
    grads = (dlog.reshape(S5_NG), dare, daim, untr(dbre_t), untr(dbim_t),
             d_cre.reshape(S5_NG, S5_GS, S5_P), dcim.reshape(S5_NG, S5_GS, S5_P))
    return dh, dgm, grads, ddsk, dwglu


def _pad_last(w, n):
    return jnp.pad(w, [(0, 0)] * (w.ndim - 1) + [(0, n - w.shape[-1])])


_BIG = ("gla_w_in", "gla_w_out", "ssd_w_in", "ssd_w_out", "s5_w_glu", "ffn_w_gu", "ffn_w_down")


def interleave_gu(w):
    q = w.shape[-1] // 4
    return jnp.concatenate([w[..., :q], w[..., 2 * q:3 * q], w[..., q:2 * q], w[..., 3 * q:]], axis=-1)


def local_step(x, target, W, later_weights=None, later_grads=None, ffn0_grads=None, ffn0_weights=None):
    f32 = lambda a: a.astype(F32)
    row = lambda a: f32(a).reshape(1, -1)

    def layer_args(i):
        m, j = i % 3, i // 3
        gm = row(W["norm_mix_g"][i])
        if m == 0:
            args = (gm, W["gla_w_in"][j], jnp.pad(f32(W["gla_w_a2"][j]), ((0, LANE - GLA_RANK), (0, 0))),
                    row(W["gla_b_a"][j]), row(W["gla_norm_g"][j]), W["gla_w_out"][j])
        elif m == 1:
            pl_ = lambda a: _pad_last(row(a), LANE)
            args = (gm, W["ssd_w_in"][j], f32(W["ssd_conv_w"][j]),
                    row(W["ssd_conv_b"][j]), pl_(W["ssd_dt_bias"][j]), pl_(W["ssd_a_log"][j]), pl_(W["ssd_d"][j]),
                    row(W["ssd_norm_g"][j]), W["ssd_w_out"][j])
        else:
            prm = tuple(f32(W[k][j]) for k in ("s5_log_dt", "s5_a_re", "s5_a_im", "s5_b_re", "s5_b_im", "s5_c_re", "s5_c_im"))
            args = (gm, prm, row(W["s5_d"][j]), W["s5_w_glu"][j])
        return m, j, args

    h = x
    saved, mixers, ffns = [], [], []
    for i in range(DEPTH):
        mixer = layer_args(i)
        mixers.append(mixer)
        m, j, args = mixer
        tag = f"l{i}_{('gla', 'ssd', 's5')[m]}"
        h, sm = (gla_fwd, ssd_fwd, s5_fwd)[m](h, *args, tag)
        if i == 0 and ffn0_weights is not None:
            W = {**W, **ffn0_weights(h)}
        ffn = (row(W["norm_ffn_g"][i]), W["ffn_w_gu"][i], W["ffn_w_down"][i])
        ffns.append(ffn)
        h, sf = ffn_fwd(h, *ffn, f"l{i}_ffn")
        saved.append((sm, sf))
        if i == 0 and later_weights is not None:
            W = {**W, **later_weights(h)}
    loss, dfg, d = loss_head(h, row(W["final_norm_g"]), target, name="loss_head")

    G = {k: [None] * len(v) for k, v in W.items() if k != "final_norm_g"}
    G["final_norm_g"] = dfg.reshape(D)
    for i in reversed(range(DEPTH)):
        m, j, args = mixers[i]
        sm, sf = saved[i]
        if i == 0 and later_grads is not None:
            zero = later_grads(G)
            ffns[0] = (ffns[0][0], ffns[0][1], ffns[0][2] + zero.astype(ffns[0][2].dtype))
        d, dg, dwgu, dwd = ffn_bwd(d, sf, *ffns[i], f"l{i}_ffn")
        G["norm_ffn_g"][i], G["ffn_w_gu"][i], G["ffn_w_down"][i] = dg.reshape(D), dwgu, dwd
        if i == 0 and ffn0_grads is not None:
            zero = ffn0_grads(G)
            args = args[:-1] + (args[-1] + zero.astype(args[-1].dtype),)
        tag = f"l{i}_{('gla', 'ssd', 's5')[m]}"
        if m == 0:
            d, dgm, dwin, dwa2, dba, dng, dwout = gla_bwd(d, sm, *args, tag)
            G["gla_w_in"][j], G["gla_w_a2"][j], G["gla_b_a"][j] = dwin, dwa2, dba.reshape(-1)
            G["gla_norm_g"][j], G["gla_w_out"][j] = dng.reshape(-1), dwout
        elif m == 1:
            d, dgm, dwin, dcw, dcb, ddtb, dal, ddsk, dng, dwout = ssd_bwd(d, sm, *args, tag)
            G["ssd_w_in"][j], G["ssd_conv_w"][j], G["ssd_conv_b"][j] = dwin, dcw, dcb.reshape(-1)
            G["ssd_dt_bias"][j], G["ssd_a_log"][j], G["ssd_d"][j] = ddtb.reshape(-1), dal.reshape(-1), ddsk.reshape(-1)
            G["ssd_norm_g"][j], G["ssd_w_out"][j] = dng.reshape(-1), dwout
        else:
            d, dgm, pg, ddsk, dwglu = s5_bwd(d, sm, args[0], args[2], args[3], tag)
            for k, v in zip(("s5_log_dt", "s5_a_re", "s5_a_im", "s5_b_re", "s5_b_im", "s5_c_re", "s5_c_im"), pg):
                G[k][j] = v
            G["s5_d"][j], G["s5_w_glu"][j] = ddsk.reshape(-1), dwglu
        G["norm_mix_g"][i] = dgm.reshape(D)
    grads = {k: (v if k == "final_norm_g" or k in _BIG else jnp.stack(v)) for k, v in G.items()}
    return loss, d, grads


_MESH = pl.DeviceIdType.MESH
_ANY = pl.BlockSpec(memory_space=pl.ANY)
_DMA = pltpu.SemaphoreType.DMA
_ROWS_ALIGN = 1024


def _place():
    return lax.axis_index("x"), lax.axis_index("y"), lax.axis_index("c")


def _other_chips(x, y):
    return [(1 - x, y), (x, 1 - y), (1 - x, 1 - y)]


def _remote(src, dst, send_sems, recv_sems, k, to):
    return pltpu.make_async_remote_copy(src_ref=src, dst_ref=dst, send_sem=send_sems.at[k], recv_sem=recv_sems.at[k],
                                        device_id=to, device_id_type=_MESH)


def gather_shards(loc, *, name):
    def body(in_ref, out_ref, send_sems, recv_sems, local_sem):
        x, y, c = _place()
        me, sibling = (x, y, c), (x, y, 1 - c)
        chips = _other_chips(x, y)

        def half(px, py, hc):
            return out_ref.at[2 * px + py, hc]

        mine = pltpu.make_async_copy(in_ref, out_ref.at[2 * x + y], local_sem)
        mine.start()
        first = [_remote(in_ref.at[c], half(x, y, c), send_sems, recv_sems, j, (*chip, c)) for j, chip in enumerate(chips)]
        for cp in first:
            cp.start()
        passed = [_remote(half(*chip, c), half(*chip, c), send_sems, recv_sems, 3 + j, sibling) for j, chip in enumerate(chips)]
        for j, chip in enumerate(chips):
            _remote(in_ref.at[c], half(*chip, c), send_sems, recv_sems, j, me).wait_recv()
            passed[j].start()
        for j, chip in enumerate(chips):
            _remote(in_ref.at[c], half(*chip, 1 - c), send_sems, recv_sems, 3 + j, me).wait_recv()
        for cp in first + passed:
            cp.wait_send()
        mine.wait()

    return pl.pallas_call(
        body, name=name, in_specs=[_ANY], out_specs=_ANY,
        out_shape=jax.ShapeDtypeStruct((4,) + loc.shape, loc.dtype),
        scratch_shapes=[_DMA((6,)), _DMA((6,)), _DMA(())],
    )(loc)


def _pos(px, py, perm):
    return 2 * py + px if perm else 2 * px + py


def _part(ref, kind, p, loc):
    if kind == "lead":
        return ref.at[p]
    return ref.at[:, pl.ds(pl.multiple_of(p * loc, LANE), loc)]


def _rows(ref, h, hr):
    return ref.at[pl.ds(h * hr, hr)]


def _rows_block(hr, width):
    return max(b for b in range(16, hr + 1, 16) if hr % b == 0 and (b * width <= (1 << 19) or b == 16))


def gather_big(locs, kinds, *, name):
    n = len(locs)

    def body(*refs):
        ins, outs = refs[:n], refs[n:2 * n]
        send_sems, recv_sems = refs[2 * n + 1:]
        refs[2 * n][...] = jnp.zeros_like(refs[2 * n])
        x, y, c = _place()
        me, sibling = (x, y, c), (x, y, 1 - c)
        chips = _other_chips(x, y)

        def half(i, px, py, h):
            (kind, perm), (rows, loc) = kinds[i], locs[i].shape
            return _rows(_part(outs[i], kind, _pos(px, py, perm), loc), h, rows // 2)

        sends = []
        for i in range(n):
            (kind, perm), (rows, loc) = kinds[i], locs[i].shape
            own = _part(outs[i], kind, _pos(x, y, perm), loc)
            sends.append(_remote(ins[i], own, send_sems, recv_sems, 6 * n + i, sibling))
            sends[-1].start()
            for j, chip in enumerate(chips):
                sends.append(_remote(_rows(ins[i], c, rows // 2), half(i, x, y, c), send_sems, recv_sems, 6 * i + j, (*chip, c)))
                sends[-1].start()
        for i in range(n):
            hr = locs[i].shape[0] // 2
            for j, chip in enumerate(chips):
                _remote(_rows(ins[i], c, hr), half(i, *chip, c), send_sems, recv_sems, 6 * i + j, me).wait_recv()
                sends.append(_remote(half(i, *chip, c), half(i, *chip, c), send_sems, recv_sems, 6 * i + 3 + j, sibling))
                sends[-1].start()
        for i in range(n):
            (kind, perm), (rows, loc) = kinds[i], locs[i].shape
            for j, chip in enumerate(chips):
                _remote(_rows(ins[i], c, rows // 2), half(i, *chip, 1 - c), send_sems, recv_sems, 6 * i + 3 + j, me).wait_recv()
            _remote(ins[i], _part(outs[i], kind, _pos(x, y, perm), loc), send_sems, recv_sems, 6 * n + i, me).wait_recv()
        for cp in sends:
            cp.wait_send()

    def out_shape(a, kind):
        rows, loc = a.shape
        return jax.ShapeDtypeStruct((4, rows, loc) if kind == "lead" else (rows, 4 * loc), a.dtype)

    outs = pl.pallas_call(
        body, name=name, in_specs=[_ANY] * n, out_specs=[_ANY] * n + [pl.BlockSpec(memory_space=pltpu.VMEM)],
        out_shape=[out_shape(a, k[0]) for a, k in zip(locs, kinds)] + [jax.ShapeDtypeStruct((8, LANE), F32)],
        scratch_shapes=[_DMA((7 * n,)), _DMA((7 * n,))],
    )(*locs)
    return list(outs[:n]), outs[n][0, 0]


_HBM = pl.BlockSpec(memory_space=pltpu.HBM)
_SEM = pl.BlockSpec(memory_space=pltpu.SEMAPHORE)
_EFFECT = pltpu.SideEffectType.DATAFLOW_SIDE_EFFECTING


def _in_hbm(a):
    return pltpu.with_memory_space_constraint(a, pltpu.HBM)


def _gather_ici_copies(ins, lands, kinds, shapes, send_sems, recv_sems):
    x, y, c = _place()
    sends, arrivals = [], []
    for i, ((kind, perm), (rows, loc)) in enumerate(zip(kinds, shapes)):
        hr = rows // 2
        mine = _part(lands[i], kind, _pos(x, y, perm), loc)
        sends.append(_remote(ins[i], mine, send_sems, recv_sems, 4 * i + 3, (x, y, 1 - c)))
        arrivals.append(_remote(ins[i], mine, send_sems, recv_sems, 4 * i + 3, (x, y, c)))
        for j, (px, py) in enumerate(_other_chips(x, y)):
            sends.append(_remote(_rows(ins[i], c, hr), _rows(mine, c, hr), send_sems, recv_sems, 4 * i + j, (px, py, c)))
            theirs = _rows(_part(lands[i], kind, _pos(px, py, perm), loc), c, hr)
            arrivals.append(_remote(_rows(ins[i], c, hr), theirs, send_sems, recv_sems, 4 * i + j, (x, y, c)))
    return sends, arrivals


def gather_start(locs, kinds, *, name):
    n = len(locs)
    shapes = [a.shape for a in locs]

    def land_shape(a, kind):
        rows, loc = a.shape
        return (4, rows, loc) if kind == "lead" else (rows, 4 * loc)

    def body(*refs):
        sends, _ = _gather_ici_copies(refs[:n], refs[n:2 * n], kinds, shapes, refs[2 * n], refs[2 * n + 1])
        for cp in sends:
            cp.start()
        refs[-1][...] = jnp.zeros_like(refs[-1])

    lands = [lax.empty(land_shape(a, k[0]), a.dtype) for a, k in zip(locs, kinds)]
    outs = pl.pallas_call(
        body, name=name, in_specs=[_HBM] * (2 * n), out_specs=[_SEM, _SEM] + [_HBM] * (2 * n) + [pl.BlockSpec(memory_space=pltpu.VMEM)],
        out_shape=[_DMA((4 * n,)), _DMA((4 * n,))] + [pltpu.HBM(a.shape, a.dtype) for a in locs]
        + [pltpu.HBM(l.shape, l.dtype) for l in lands] + [jax.ShapeDtypeStruct((8, LANE), F32)],
        input_output_aliases={i: 2 + i for i in range(2 * n)},
        compiler_params=pltpu.CompilerParams(has_side_effects=_EFFECT),
    )(*[_in_hbm(a) for a in locs], *[_in_hbm(l) for l in lands])
    return outs[0], outs[1], list(outs[2:2 + n]), list(outs[2 + n:2 + 2 * n]), outs[-1][0, 0]


def gather_wait(send_sems, recv_sems, locs, lands, kinds, after, *, name):
    n = len(locs)
    shapes = [a.shape for a in locs]

    def body(*refs):
        sends, arrivals = _gather_ici_copies(refs[:n], refs[n:2 * n], kinds, shapes, refs[2 * n], refs[2 * n + 1])
        for cp in sends:
            cp.wait_send()
        for cp in arrivals:
            cp.wait_recv()

    outs = pl.pallas_call(
        body, name=name, in_specs=[_HBM] * (2 * n) + [_SEM, _SEM, _ANY], out_specs=[_HBM] * (2 * n),
        out_shape=[pltpu.HBM(a.shape, a.dtype) for a in locs] + [pltpu.HBM(l.shape, l.dtype) for l in lands],
        input_output_aliases={i: i for i in range(2 * n)},
        compiler_params=pltpu.CompilerParams(has_side_effects=_EFFECT),
    )(*locs, *lands, send_sems, recv_sems, after)
    return list(outs[n:])


def gather_finish(lands, kinds, shapes, *, name):
    n = len(lands)

    def body(*refs):
        bufs = refs[n:2 * n]
        send_sems, recv_sems = refs[2 * n:]
        x, y, c = _place()
        sends = []
        for i, ((kind, perm), (rows, loc)) in enumerate(zip(kinds, shapes)):
            for j, (px, py) in enumerate(_other_chips(x, y)):
                part = _part(bufs[i], kind, _pos(px, py, perm), loc)
                sends.append(_remote(_rows(part, c, rows // 2), _rows(part, c, rows // 2), send_sems, recv_sems, 3 * i + j, (x, y, 1 - c)))
                sends[-1].start()
        for i, ((kind, perm), (rows, loc)) in enumerate(zip(kinds, shapes)):
            for j, (px, py) in enumerate(_other_chips(x, y)):
                part = _part(bufs[i], kind, _pos(px, py, perm), loc)
                _remote(_rows(part, c, rows // 2), _rows(part, 1 - c, rows // 2), send_sems, recv_sems, 3 * i + j, (x, y, c)).wait_recv()
        for cp in sends:
            cp.wait_send()

    return list(pl.pallas_call(
        body, name=name, in_specs=[_ANY] * n, out_specs=[_ANY] * n,
        out_shape=[jax.ShapeDtypeStruct(l.shape, l.dtype) for l in lands],
        input_output_aliases={i: i for i in range(n)}, scratch_shapes=[_DMA((3 * n,)), _DMA((3 * n,))],
    )(*lands))


def _scatter_copies(qs, lands, kinds, locs, send_sems, recv_sems):
    x, y, c = _place()
    sends, arrivals = [], []
    for i, (kind, perm) in enumerate(kinds):
        for j, (px, py) in enumerate(_other_chips(x, y)):
            src = _part(qs[i], kind, _pos(px, py, perm), locs[i])
            sends.append(_remote(src, lands[i].at[j], send_sems, recv_sems, 3 * i + j, (px, py, c)))
            arrivals.append(_remote(src, lands[i].at[j], send_sems, recv_sems, 3 * i + j, (x, y, c)))
    return sends, arrivals


def _scatter_land(q, kind, loc):
    return (3, q.shape[1] if kind == "lead" else q.shape[0], loc)


def scatter_start(qs, kinds, locs, *, name):
    n = len(qs)

    def body(*refs):
        sends, _ = _scatter_copies(refs[:n], refs[n:2 * n], kinds, locs, refs[2 * n], refs[2 * n + 1])
        for cp in sends:
            cp.start()
        refs[-1][...] = jnp.zeros_like(refs[-1])

    lands = [lax.empty(_scatter_land(q, k[0], l), q.dtype) for q, k, l in zip(qs, kinds, locs)]
    outs = pl.pallas_call(
        body, name=name, in_specs=[_HBM] * (2 * n), out_specs=[_SEM, _SEM] + [_HBM] * (2 * n) + [pl.BlockSpec(memory_space=pltpu.VMEM)],
        out_shape=[_DMA((3 * n,)), _DMA((3 * n,))] + [pltpu.HBM(q.shape, q.dtype) for q in qs]
        + [pltpu.HBM(l.shape, l.dtype) for l in lands] + [jax.ShapeDtypeStruct((8, LANE), F32)],
        input_output_aliases={i: 2 + i for i in range(2 * n)},
        compiler_params=pltpu.CompilerParams(has_side_effects=_EFFECT),
    )(*[_in_hbm(q) for q in qs], *[_in_hbm(l) for l in lands])
    return outs[0], outs[1], list(outs[2:2 + n]), list(outs[2 + n:2 + 2 * n]), outs[-1][0, 0]


def scatter_wait(send_sems, recv_sems, qs, lands, kinds, locs, after, *, name):
    n = len(qs)

    def body(*refs):
        sends, arrivals = _scatter_copies(refs[:n], refs[n:2 * n], kinds, locs, refs[2 * n], refs[2 * n + 1])
        for cp in sends:
            cp.wait_send()
        for cp in arrivals:
            cp.wait_recv()

    outs = pl.pallas_call(
        body, name=name, in_specs=[_HBM] * (2 * n) + [_SEM, _SEM, _ANY], out_specs=[_HBM] * (2 * n),
        out_shape=[pltpu.HBM(q.shape, q.dtype) for q in qs] + [pltpu.HBM(l.shape, l.dtype) for l in lands],
        input_output_aliases={i: i for i in range(2 * n)},
        compiler_params=pltpu.CompilerParams(has_side_effects=_EFFECT),
    )(*qs, *lands, send_sems, recv_sems, after)
    return list(outs[:n]), list(outs[n:])


def _pair_swap_copies(ins, lands, kinds, send_sems, recv_sems):
    x, y, c = _place()
    sends, arrivals = [], []
    for i, (kind, _) in enumerate(kinds):
        if kind == "lead":
            hr = ins[i].shape[1] // 2
            src = ins[i].at[:, pl.ds((1 - c) * hr, hr)]
        else:
            src = _rows(ins[i], 1 - c, ins[i].shape[0] // 2)
        sends.append(_remote(src, lands[i], send_sems, recv_sems, i, (x, y, 1 - c)))
        arrivals.append(_remote(src, lands[i], send_sems, recv_sems, i, (x, y, c)))
    return sends, arrivals


def _pair_swap_land(a, kind):
    s = a.shape
    return (4, s[1] // 2, s[2]) if kind == "lead" else (s[0] // 2, s[1])


def pair_swap_start(ps, kinds, *, name):
    n = len(ps)

    def body(*refs):
        sends, _ = _pair_swap_copies(refs[:n], refs[n:2 * n], kinds, refs[2 * n], refs[2 * n + 1])
        for cp in sends:
            cp.start()
        refs[-1][...] = jnp.zeros_like(refs[-1])

    lands = [lax.empty(_pair_swap_land(p, k[0]), p.dtype) for p, k in zip(ps, kinds)]
    outs = pl.pallas_call(
        body, name=name, in_specs=[_HBM] * (2 * n), out_specs=[_SEM, _SEM] + [_HBM] * (2 * n) + [pl.BlockSpec(memory_space=pltpu.VMEM)],
        out_shape=[_DMA((n,)), _DMA((n,))] + [pltpu.HBM(p.shape, p.dtype) for p in ps]
        + [pltpu.HBM(l.shape, l.dtype) for l in lands] + [jax.ShapeDtypeStruct((8, LANE), F32)],
        input_output_aliases={i: 2 + i for i in range(2 * n)},
        compiler_params=pltpu.CompilerParams(has_side_effects=_EFFECT),
    )(*[_in_hbm(p) for p in ps], *[_in_hbm(l) for l in lands])
    return outs[0], outs[1], list(outs[2:2 + n]), list(outs[2 + n:2 + 2 * n]), outs[-1][0, 0]


def pair_swap_wait(send_sems, recv_sems, ps, lands, kinds, after, *, name):
    n = len(ps)

    def body(*refs):
        sends, arrivals = _pair_swap_copies(refs[:n], refs[n:2 * n], kinds, refs[2 * n], refs[2 * n + 1])
        for cp in sends:
            cp.wait_send()
        for cp in arrivals:
            cp.wait_recv()

    outs = pl.pallas_call(
        body, name=name, in_specs=[_HBM] * (2 * n) + [_SEM, _SEM, _ANY], out_specs=[_HBM] * (2 * n),
        out_shape=[pltpu.HBM(p.shape, p.dtype) for p in ps] + [pltpu.HBM(l.shape, l.dtype) for l in lands],
        input_output_aliases={i: i for i in range(2 * n)},
        compiler_params=pltpu.CompilerParams(has_side_effects=_EFFECT),
    )(*ps, *lands, send_sems, recv_sems, after)
    return list(outs[:n]), list(outs[n:])


def pair_swap(ps, kinds, *, name):
    n = len(ps)

    def body(*refs):
        ins, outs = refs[:n], refs[n:2 * n]
        send_sems, recv_sems = refs[2 * n:]
        x, y, c = _place()
        cps = []
        for i in range(n):
            if kinds[i][0] == "lead":
                hr = ps[i].shape[1] // 2
                src = ins[i].at[:, pl.ds((1 - c) * hr, hr)]
            else:
                hr = ps[i].shape[0] // 2
                src = _rows(ins[i], 1 - c, hr)
            cps.append(_remote(src, outs[i], send_sems, recv_sems, i, (x, y, 1 - c)))
            cps[-1].start()
        for cp in cps:
            cp.wait()

    def out_shape(a, kind):
        s = a.shape
        return jax.ShapeDtypeStruct((4, s[1] // 2, s[2]) if kind == "lead" else (s[0] // 2, s[1]), a.dtype)

    return pl.pallas_call(
        body, name=name, in_specs=[_ANY] * n, out_specs=[_ANY] * n,
        out_shape=[out_shape(a, k[0]) for a, k in zip(ps, kinds)], scratch_shapes=[_DMA((n,)), _DMA((n,))],
    )(*ps)


def pair_add(p, got, c_arr, kind, *, name):
    if kind == "lead":
        _, hr, cols = got.shape
        br = _rows_block(hr, cols)
        nb = hr // br
        grid = (4, nb)
        p_spec = pl.BlockSpec((None, br, cols), lambda s, i, cr: (s, cr[0] * nb + i, 0))
        g_spec = pl.BlockSpec((None, br, cols), lambda s, i, cr: (s, i, 0))
    else:
        hr, w = got.shape
        br = _rows_block(hr, w)
        nb = hr // br
        grid = (nb,)
        p_spec = pl.BlockSpec((br, w), lambda i, cr: (cr[0] * nb + i, 0))
        g_spec = pl.BlockSpec((br, w), lambda i, cr: (i, 0))

    def body(c_ref, p_ref, g_ref, o_ref):
        o_ref[...] = (p_ref[...] + g_ref[...]).astype(o_ref.dtype)

    return pl.pallas_call(
        body, name=name, out_shape=jax.ShapeDtypeStruct(got.shape, BF16),
        grid_spec=pltpu.PrefetchScalarGridSpec(num_scalar_prefetch=1, grid=grid, in_specs=[p_spec, g_spec], out_specs=g_spec),
        compiler_params=_cparams(len(grid)),
    )(c_arr, p, got)


def chip_scatter(qs, kinds, locs, *, name):
    n = len(qs)

    def body(*refs):
        ins, outs = refs[:n], refs[n:2 * n]
        send_sems, recv_sems = refs[2 * n:]
        x, y, c = _place()
        cps = []
        for i in range(n):
            kind, perm = kinds[i]
            for j, (px, py) in enumerate(_other_chips(x, y)):
                cps.append(_remote(_part(ins[i], kind, _pos(px, py, perm), locs[i]), outs[i].at[j], send_sems, recv_sems,
                                   3 * i + j, (px, py, c)))
                cps[-1].start()
        for cp in cps:
            cp.wait()

    def out_shape(a, kind, loc):
        hr = a.shape[1] if kind == "lead" else a.shape[0]
        return jax.ShapeDtypeStruct((3, hr, loc), a.dtype)

    return pl.pallas_call(
        body, name=name, in_specs=[_ANY] * n, out_specs=[_ANY] * n,
        out_shape=[out_shape(a, k[0], l) for a, k, l in zip(qs, kinds, locs)],
        scratch_shapes=[_DMA((3 * n,)), _DMA((3 * n,))],
    )(*qs)


def chip_add(q, r, pos_arr, c_arr, kind, loc, *, name):
    _, hr, _ = r.shape
    br = _rows_block(hr, loc)
    nb = hr // br
    if kind == "lead":
        q_spec = pl.BlockSpec((None, br, loc), lambda i, pr, cr: (pr[0], i, 0))
    else:
        q_spec = pl.BlockSpec((br, loc), lambda i, pr, cr: (i, pr[0]))
    r_spec = pl.BlockSpec((3, br, loc), lambda i, pr, cr: (0, i, 0))
    o_spec = pl.BlockSpec((br, loc), lambda i, pr, cr: (cr[0] * nb + i, 0))

    def body(p_ref, c_ref, q_ref, r_ref, o_ref):
        acc = q_ref[...].astype(F32)
        for j in range(3):
            acc = acc + r_ref[j].astype(F32)
        o_ref[...] = acc

    return pl.pallas_call(
        body, name=name, out_shape=jax.ShapeDtypeStruct((2 * hr, loc), F32),
        grid_spec=pltpu.PrefetchScalarGridSpec(num_scalar_prefetch=2, grid=(nb,), in_specs=[q_spec, r_spec], out_specs=o_spec),
        compiler_params=_cparams(1),
    )(pos_arr, c_arr, q, r)


def share_rows(fs, *, name):
    n = len(fs)

    def body(*refs):
        bufs = refs[n:2 * n]
        send_sems, recv_sems = refs[2 * n:]
        x, y, c = _place()
        cps = []
        for i in range(n):
            hr = fs[i].shape[0] // 2
            cps.append(_remote(_rows(bufs[i], c, hr), _rows(bufs[i], c, hr), send_sems, recv_sems, i, (x, y, 1 - c)))
            cps[-1].start()
        for i, cp in enumerate(cps):
            hr = fs[i].shape[0] // 2
            _remote(_rows(bufs[i], c, hr), _rows(bufs[i], 1 - c, hr), send_sems, recv_sems, i, (x, y, c)).wait_recv()
            cp.wait_send()

    return pl.pallas_call(
        body, name=name, in_specs=[_ANY] * n, out_specs=[_ANY] * n,
        out_shape=[jax.ShapeDtypeStruct(f.shape, f.dtype) for f in fs],
        input_output_aliases={i: i for i in range(n)}, scratch_shapes=[_DMA((n,)), _DMA((n,))],
    )(*fs)


def _gather_all_copies(v_ref, land_ref, send_sems, recv_sems):
    x, y, c = _place()
    flip = lambda p, m: 1 - p if m else p
    idx = lambda p: 4 * p[0] + 2 * p[1] + p[2]
    sends, arrivals = [], []
    for k, m in enumerate(range(1, 8)):
        p = (flip(x, m & 4), flip(y, m & 2), flip(c, m & 1))
        sends.append(_remote(v_ref, land_ref.at[idx((x, y, c))], send_sems, recv_sems, k, p))
        arrivals.append(_remote(v_ref, land_ref.at[idx(p)], send_sems, recv_sems, k, (x, y, c)))
    return sends, arrivals


def gather_all_start(v, *, name):
    def body(v_ref, land_ref, send_sems, recv_sems, v_thru, land_thru, token):
        sends, _ = _gather_all_copies(v_ref, land_ref, send_sems, recv_sems)
        for cp in sends:
            cp.start()
        token[...] = jnp.zeros_like(token)

    land = jnp.zeros((8,) + v.shape, v.dtype)
    outs = pl.pallas_call(
        body, name=name, in_specs=[_HBM, _HBM], out_specs=[_SEM, _SEM, _HBM, _HBM, pl.BlockSpec(memory_space=pltpu.VMEM)],
        out_shape=[_DMA((7,)), _DMA((7,)), pltpu.HBM(v.shape, v.dtype), pltpu.HBM(land.shape, land.dtype),
                   jax.ShapeDtypeStruct((8, LANE), F32)],
        input_output_aliases={0: 2, 1: 3}, compiler_params=pltpu.CompilerParams(has_side_effects=_EFFECT),
    )(_in_hbm(v), _in_hbm(land))
    return outs[0], outs[1], outs[2], outs[3], outs[4][0, 0]


def gather_all_wait(send_sems, recv_sems, v, land, after, *, name):
    def body(v_ref, land_ref, send_sems, recv_sems, after_ref, v_dead, got_ref):
        sends, arrivals = _gather_all_copies(v_ref, land_ref, send_sems, recv_sems)
        for cp in sends:
            cp.wait_send()
        for cp in arrivals:
            cp.wait_recv()

    return pl.pallas_call(
        body, name=name, in_specs=[_HBM, _HBM, _SEM, _SEM, _ANY], out_specs=[_HBM, _HBM],
        out_shape=[pltpu.HBM(v.shape, v.dtype), pltpu.HBM(land.shape, land.dtype)],
        input_output_aliases={0: 0, 1: 1}, compiler_params=pltpu.CompilerParams(has_side_effects=_EFFECT),
    )(v, land, send_sems, recv_sems, after)[1]


def sum_slots(land, v, me_arr, *, name):
    n, R, L = land.shape
    br = _pick(R, _ROWS_ALIGN, 8)

    def body(me_ref, land_ref, v_ref, o_ref):
        acc = None
        for i in range(n):
            term = jnp.where(me_ref[0] == i, v_ref[...], land_ref[i])
            acc = term if acc is None else acc + term
        o_ref[...] = acc

    row = pl.BlockSpec((br, L), lambda i, me: (i, 0))
    return pl.pallas_call(
        body, name=name, out_shape=jax.ShapeDtypeStruct((R, L), land.dtype),
        grid_spec=pltpu.PrefetchScalarGridSpec(num_scalar_prefetch=1, grid=(R // br,),
                                               in_specs=[pl.BlockSpec((n, br, L), lambda i, me: (0, i, 0)), row], out_specs=row),
        compiler_params=_cparams(1),
    )(me_arr, land, v)


def sum_stack(a, extra=None, *, name):
    n, R, L = a.shape
    br = _pick(R, _ROWS_ALIGN, 8)

    def body(*refs):
        a_ref, o_ref = refs[0], refs[-1]
        acc = refs[1][...] if extra is not None else a_ref[0]
        for i in range(0 if extra is not None else 1, n):
            acc = acc + a_ref[i]
        o_ref[...] = acc

    row = pl.BlockSpec((br, L), lambda i: (i, 0))
    specs = [pl.BlockSpec((n, br, L), lambda i: (0, i, 0))] + ([row] if extra is not None else [])
    args = [a] + ([extra] if extra is not None else [])
    return pl.pallas_call(body, name=name, grid=(R // br,), in_specs=specs, out_specs=row,
                          out_shape=jax.ShapeDtypeStruct((R, L), a.dtype), compiler_params=_cparams(1))(*args)


def adamw(w, g, m, v, *, name):
    shape = w.shape
    size = math.prod(shape)
    last = shape[-1]
    if last % LANE != 0 and size % LANE == 0 and size <= (1 << 20):
        last = LANE
    rows = size // last
    budget = (1 << 18) // last
    br = rows
    if rows > budget:
        br = max(c for c in range(8, budget + 1, 8) if rows % c == 0)
    v2 = lambda a: a.reshape(rows, last)

    def body(w_ref, g_ref, m_ref, v_ref, d_ref, nm_ref, nv_ref):
        gg = g_ref[...]
        nm = ADAM_B1 * m_ref[...] + (1.0 - ADAM_B1) * gg
        nv = ADAM_B2 * v_ref[...] + (1.0 - ADAM_B2) * (gg * gg)
        m_hat = nm / (1.0 - ADAM_B1 ** ADAM_STEP)
        v_hat = nv / (1.0 - ADAM_B2 ** ADAM_STEP)
        d_ref[...] = -ADAM_LR * (m_hat / (jnp.sqrt(v_hat) + ADAM_EPS) + ADAM_WD * w_ref[...])
        nm_ref[...] = nm
        nv_ref[...] = nv

    spec = pl.BlockSpec((br, last), lambda i: (i, 0))
    outs = pl.pallas_call(
        body, name=name, grid=(rows // br,), in_specs=[spec] * 4, out_specs=[spec] * 3,
        out_shape=[jax.ShapeDtypeStruct((rows, last), F32)] * 3, compiler_params=_cparams(1),
    )(v2(w), v2(g), v2(m), v2(v))
    return [o.reshape(shape) for o in outs]


_WEIGHTS = ["norm_mix_g", "norm_ffn_g", "gla_w_in", "gla_w_a2", "gla_b_a", "gla_norm_g", "gla_w_out", "ssd_w_in",
            "ssd_conv_w", "ssd_conv_b", "ssd_dt_bias", "ssd_a_log", "ssd_d", "ssd_norm_g", "ssd_w_out", "s5_log_dt",
            "s5_a_re", "s5_a_im", "s5_b_re", "s5_b_im", "s5_c_re", "s5_c_im", "s5_d", "s5_w_glu", "ffn_w_gu",
            "ffn_w_down", "final_norm_g"]
_SHARD_AXIS = {"gla_w_in": 2, "gla_w_a2": 2, "gla_b_a": 1, "gla_norm_g": 1, "gla_w_out": 1, "ssd_w_in": 2,
               "ssd_conv_w": 2, "ssd_w_out": 1, "s5_d": 1, "s5_w_glu": 2, "ffn_w_gu": 2, "ffn_w_down": 1}
_SMALL_SHARDED = [n for n in _WEIGHTS if n in _SHARD_AXIS and n not in _BIG]
_REPLICATED = [n for n in _WEIGHTS if n not in _SHARD_AXIS]
_BIG_KIND = {"gla_w_in": ("lead", False), "gla_w_out": ("lead", False), "ssd_w_in": ("lead", False),
             "ssd_w_out": ("lead", False), "s5_w_glu": ("cols", False), "ffn_w_gu": ("cols", True),
             "ffn_w_down": ("lead", False)}
_PADDED_IN = {"gla_w_in": GLA_INP, "ssd_w_in": SSD_INP}


def _to_rows(flat, parts=1):
    per = -(-flat.shape[0] // (parts * LANE * _ROWS_ALIGN)) * _ROWS_ALIGN
    flat = jnp.pad(flat, (0, parts * per * LANE - flat.shape[0]))
    return flat.reshape(parts, per, LANE)


def _big_layers(local):
    return [(n, j, local[n][j].reshape(-1, local[n].shape[-1])) for n in _BIG for j in range(local[n].shape[0])]


def _in_layer0(n, j):
    return j == 0 and n in ("gla_w_in", "gla_w_out", "ffn_w_gu", "ffn_w_down")


def _assemble(n, g):
    if n in _PADDED_IN:
        return jnp.concatenate([g[s] for s in range(4)] + [jnp.zeros((g.shape[1], _PADDED_IN[n] - 4 * g.shape[2]), BF16)], axis=1)
    if _BIG_KIND[n][0] == "lead":
        return g.reshape(4 * g.shape[1], g.shape[2])
    return g


def _is_gla0(n, j):
    return j == 0 and n in ("gla_w_in", "gla_w_out")


def _gather_first(local):
    layers = _big_layers(local)
    first = [l for l in layers if _is_gla0(l[0], l[1])]
    full = {n: [None] * local[n].shape[0] for n in _BIG}
    got, done = gather_big([w.astype(BF16) for _, _, w in first], [_BIG_KIND[n] for n, _, _ in first], name="gather_weights_first")
    for (n, j, _), g in zip(first, got):
        full[n][j] = _assemble(n, g)
    flat = jnp.concatenate([local[n].astype(F32).reshape(-1) for n in _SMALL_SHARDED])
    got = gather_shards(_to_rows(flat, 2), name="gather_small_weights").reshape(4, -1)
    off = 0
    for n in _SMALL_SHARDED:
        bs = local[n].shape
        sz = math.prod(bs)
        seg = got[:, off:off + sz].reshape((4,) + bs)
        off += sz
        ax = _SHARD_AXIS[n]
        full[n] = jnp.moveaxis(seg, 0, ax).reshape(bs[:ax] + (4 * bs[ax],) + bs[ax + 1:])
    pending = {}
    for tag, want in (("ffn0", _is_ffn0), ("later", lambda n, j: not _in_layer0(n, j))):
        group = [l for l in layers if want(l[0], l[1])]
        kinds = [_BIG_KIND[n] for n, _, _ in group]
        ops = [(w + done if k == 0 else w).astype(BF16) for k, (_, _, w) in enumerate(group)]
        send_sems, recv_sems, locs, lands, done = gather_start(ops, kinds, name=f"gather_weights_start_{tag}")
        pending[tag] = (group, kinds, send_sems, recv_sems, locs, lands)
    return full, pending, done


def _gather_rest(full, pending, after, tag):
    group, kinds, send_sems, recv_sems, locs, lands = pending
    lands = gather_wait(send_sems, recv_sems, locs, lands, kinds, after, name=f"gather_weights_wait_{tag}")
    lands = gather_finish(lands, kinds, [w.shape for _, _, w in group], name=f"gather_weights_finish_{tag}")
    out = {n: list(full[n]) for n in _BIG}
    for (n, j, _), g in zip(group, lands):
        out[n][j] = _assemble(n, g)
    return out


def _reduce_ops(grads, local, want):
    ops = []
    for n in _BIG:
        kind = _BIG_KIND[n]
        for j, g in enumerate(grads[n]):
            if not want(n, j):
                continue
            loc = local[n].shape[-1] if kind[0] == "cols" or n in _PADDED_IN else g.shape[1]
            if n in _PADDED_IN:
                g = jnp.stack([g[:, s * loc:(s + 1) * loc] for s in range(4)])
            elif kind[0] == "lead":
                g = g.reshape(4, g.shape[0] // 4, g.shape[1])
            ops.append((n, j, kind, loc, g))
    return ops


def _pair_sums(ops, c_arr, tag):
    gots = pair_swap([o[4] for o in ops], [o[2] for o in ops], name=f"reduce_pair_swap_{tag}")
    return [pair_add(o[4], got, c_arr, o[2][0], name=f"reduce_pair_add_{o[0]}{o[1]}") for o, got in zip(ops, gots)]


def _is_ffn0(n, j):
    return j == 0 and n in ("ffn_w_gu", "ffn_w_down")


def _reduce_start(grads, local, c, want, tag):
    ops = _reduce_ops(grads, local, want)
    c_arr = jnp.reshape(c, (1,)).astype(jnp.int32)
    qs = _pair_sums(ops, c_arr, tag)
    send_sems, recv_sems, qs, lands, zero = scatter_start(qs, [o[2] for o in ops], [o[3] for o in ops],
                                                          name=f"reduce_scatter_start_{tag}")
    return (ops, send_sems, recv_sems, qs, lands, tag), zero


def _reduce_swap_start(grads, local, c, want, tag):
    ops = _reduce_ops(grads, local, want)
    send_sems, recv_sems, ps, lands, zero = pair_swap_start([o[4] for o in ops], [o[2] for o in ops],
                                                            name=f"reduce_pair_swap_start_{tag}")
    return (ops, send_sems, recv_sems, ps, lands, tag), zero


def _reduce_scatter_after(pending, after, c):
    ops, send_sems, recv_sems, ps, lands, tag = pending
    ps, gots = pair_swap_wait(send_sems, recv_sems, ps, lands, [o[2] for o in ops], after, name=f"reduce_pair_swap_wait_{tag}")
    c_arr = jnp.reshape(c, (1,)).astype(jnp.int32)
    qs = [pair_add(p, got, c_arr, o[2][0], name=f"reduce_pair_add_{o[0]}{o[1]}") for o, p, got in zip(ops, ps, gots)]
    send_sems, recv_sems, qs, lands, zero = scatter_start(qs, [o[2] for o in ops], [o[3] for o in ops],
                                                          name=f"reduce_scatter_start_{tag}")
    return (ops, send_sems, recv_sems, qs, lands, tag), zero


def _reduce_big(grads, local, pendings, after, x, y, c):
    c_arr = jnp.reshape(c, (1,)).astype(jnp.int32)
    ops, qs, rs = [], [], []
    for ops_p, send_sems, recv_sems, qs_p, lands, tag in pendings:
        qs_p, rs_p = scatter_wait(send_sems, recv_sems, qs_p, lands, [o[2] for o in ops_p], [o[3] for o in ops_p], after,
                                  name=f"reduce_scatter_wait_{tag}")
        ops, qs, rs = ops + ops_p, qs + qs_p, rs + rs_p
    ops_f = _reduce_ops(grads, local, lambda n, j: _in_layer0(n, j) and not _is_ffn0(n, j))
    qs_f = _pair_sums(ops_f, c_arr, "first")
    rs_f = list(chip_scatter(qs_f, [o[2] for o in ops_f], [o[3] for o in ops_f], name="reduce_chip_scatter_first"))
    ops, qs, rs = ops + ops_f, qs + qs_f, rs + rs_f
    fs = [chip_add(q, r, jnp.reshape(_pos(x, y, o[2][1]), (1,)).astype(jnp.int32), c_arr, o[2][0], o[3],
                   name=f"reduce_chip_add_{o[0]}{o[1]}") for o, q, r in zip(ops, qs, rs)]
    outs = share_rows(fs, name="reduce_share")
    red = {(o[0], o[1]): r for o, r in zip(ops, outs)}
    return {n: jnp.stack([red[(n, j)] for j in range(local[n].shape[0])]).reshape(local[n].shape) for n in _BIG}


def _reduce_small_start(grads):
    names = _REPLICATED + _SMALL_SHARDED
    flat = jnp.concatenate([grads[n].astype(F32).reshape(-1) for n in names])
    n_el = flat.shape[0]
    rows = -(-n_el // (LANE * 8)) * 8
    v = jnp.pad(flat, (0, rows * LANE - n_el)).reshape(rows, LANE)
    outs = gather_all_start(v, name="reduce_small_start")
    return outs[:4], outs[4]


def _reduce_small(pending, after, grads, local, x, y, c):
    names = _REPLICATED + _SMALL_SHARDED
    send_sems, recv_sems, v, land = pending
    land = gather_all_wait(send_sems, recv_sems, v, land, after, name="reduce_small_wait")
    me = jnp.reshape(4 * x + 2 * y + c, (1,)).astype(jnp.int32)
    red = sum_slots(land, v, me, name="reduce_small_add").reshape(-1)
    out, off = {}, 0
    for n in names:
        sz = math.prod(grads[n].shape)
        g = red[off:off + sz].reshape(grads[n].shape)
        off += sz
        if n in _SHARD_AXIS:
            ax = _SHARD_AXIS[n]
            loc = local[n].shape[ax]
            g = lax.dynamic_slice_in_dim(g, (2 * x + y) * loc, loc, axis=ax)
        out[n] = g
    return out


def kernel(x, norm_mix_g, norm_ffn_g, gla_w_in, gla_w_a2, gla_b_a, gla_norm_g, gla_w_out, ssd_w_in, ssd_conv_w, ssd_conv_b, ssd_dt_bias, ssd_a_log, ssd_d, ssd_norm_g, ssd_w_out, s5_log_dt, s5_a_re, s5_a_im, s5_b_re, s5_b_im, s5_c_re, s5_c_im, s5_d, s5_w_glu, ffn_w_gu, ffn_w_down, final_norm_g, loss_target, m_norm_mix_g, m_norm_ffn_g, m_gla_w_in, m_gla_w_a2, m_gla_b_a, m_gla_norm_g, m_gla_w_out, m_ssd_w_in, m_ssd_conv_w, m_ssd_conv_b, m_ssd_dt_bias, m_ssd_a_log, m_ssd_d, m_ssd_norm_g, m_ssd_w_out, m_s5_log_dt, m_s5_a_re, m_s5_a_im, m_s5_b_re, m_s5_b_im, m_s5_c_re, m_s5_c_im, m_s5_d, m_s5_w_glu, m_ffn_w_gu, m_ffn_w_down, m_final_norm_g, v_norm_mix_g, v_norm_ffn_g, v_gla_w_in, v_gla_w_a2, v_gla_b_a, v_gla_norm_g, v_gla_w_out, v_ssd_w_in, v_ssd_conv_w, v_ssd_conv_b, v_ssd_dt_bias, v_ssd_a_log, v_ssd_d, v_ssd_norm_g, v_ssd_w_out, v_s5_log_dt, v_s5_a_re, v_s5_a_im, v_s5_b_re, v_s5_b_im, v_s5_c_re, v_s5_c_im, v_s5_d, v_s5_w_glu, v_ffn_w_gu, v_ffn_w_down, v_final_norm_g):
    given = dict(locals())
    local = {n: given[n] for n in _WEIGHTS}
    px, py, pc = _place()

    first, gathering, zero = _gather_first(local)
    full = dict(local)
    full.update(first)
    full["norm_mix_g"] = local["norm_mix_g"] + zero
    big = [first]

    def weights_of(tag):
        def arrived(h):
            big.append(_gather_rest(big[-1], gathering[tag], h, tag))
            return big[-1]
        return arrived

    swapping, reducing = [], []

    def later_grads(g):
        pending, zero = _reduce_swap_start(g, local, pc, lambda n, j: not _in_layer0(n, j), "later")
        swapping.append(pending)
        return zero

    def ffn0_grads(g):
        pending, zero = _reduce_scatter_after(swapping[0], g["ffn_w_down"][0], pc)
        reducing.append(pending)
        g["ffn_w_down"][0] = g["ffn_w_down"][0] + zero
        pending, zero = _reduce_start(g, local, pc, _is_ffn0, "ffn0")
        reducing.append(pending)
        return zero

    loss, grad_x, grads = local_step(x[0], loss_target[0], full, weights_of("later"), later_grads, ffn0_grads, weights_of("ffn0"))
    loss = lax.psum(loss, ("x", "y", "c"))

    small, zero = _reduce_small_start(grads)
    grads["gla_w_out"][0] = grads["gla_w_out"][0] + zero
    red = _reduce_big(grads, local, reducing, grad_x, px, py, pc)
    red.update(_reduce_small(small, red["ffn_w_down"], grads, local, px, py, pc))

    deltas, new_m, new_v = {}, {}, {}
    for n in _WEIGHTS:
        deltas[n], new_m[n], new_v[n] = adamw(local[n], red[n], given["m_" + n], given["v_" + n], name=f"adamw_{n}")
    return (loss, grad_x[None], *[red[n] for n in _WEIGHTS], *[deltas[n] for n in _WEIGHTS],
            *[new_m[n] for n in _WEIGHTS], *[new_v[n] for n in _WEIGHTS])
```

```python
import functools
import math

import jax
import jax.numpy as jnp
from jax import lax
from jax.experimental import pallas as pl
from jax.experimental.pallas import tpu as pltpu

F32 = jnp.float32
BF16 = jnp.bfloat16

D = 1024
DEPTH = 4
CH = 64
EPS = 1e-6
GLA_H, GLA_DK, GLA_DV, GLA_RANK, GLA_TAU = 4, 128, 256, 16, 16.0
GLA_QK = GLA_H * GLA_DK
GLA_VD = GLA_H * GLA_DV
GLA_IN = 2 * GLA_QK + 2 * GLA_VD + GLA_RANK
GLA_INP = 3200
SSD_DI, SSD_HD, SSD_H, SSD_G, SSD_N, SSD_K = 2048, 64, 32, 8, 128, 4
SSD_GN = SSD_G * SSD_N
SSD_CONV = SSD_DI + 2 * SSD_GN
SSD_IN = SSD_DI + SSD_CONV + SSD_H
SSD_INP = 6272
S5_GS, S5_NG, S5_P = 16, 64, 64
S5_BLK = 8
FFN_H = 2816
LANE = 128
VMEM_LIMIT = 52 * 1024 * 1024
_MATMUL_VMEM = 40 * 1024 * 1024

ADAM_LR, ADAM_B1, ADAM_B2, ADAM_EPS, ADAM_WD, ADAM_STEP = 0.001, 0.9, 0.999, 1e-08, 0.01, 10

_ARB = "arbitrary"


def _cparams(n):
    return pltpu.CompilerParams(dimension_semantics=(_ARB,) * n, vmem_limit_bytes=VMEM_LIMIT)


def _pick(n, target, mult=LANE):
    best = None
    for c in range(mult, min(n, target) + 1, mult):
        if n % c == 0:
            best = c
    return best if best is not None else n


_DN = {"nn": (((1,), (0,)), ((), ())), "nt": (((1,), (1,)), ((), ())), "tn": (((0,), (0,)), ((), ()))}


def _dot_raw(a, b, form):
    return lax.dot_general(a.astype(BF16), b.astype(BF16), _DN[form], preferred_element_type=F32)


@functools.partial(jax.custom_vjp, nondiff_argnums=(2,))
def bdot(a, b, form):
    return _dot_raw(a, b, form)


def _bdot_fwd(a, b, form):
    return _dot_raw(a, b, form), (a, b)


def _bdot_bwd(form, res, g):
    a, b = res
    if form == "nn":
        return _dot_raw(g, b, "nt"), _dot_raw(a, g, "tn")
    if form == "nt":
        return _dot_raw(g, b, "nn"), _dot_raw(g, a, "tn")
    return _dot_raw(b, g, "nt"), _dot_raw(a, g, "nn")


bdot.defvjp(_bdot_fwd, _bdot_bwd)


def _hdot(a, b):
    return jnp.dot(a, b, precision=lax.Precision.HIGHEST, preferred_element_type=F32)


@jax.custom_vjp
def cdot_left(c, ct, x):
    return _hdot(c, x)


def _cdl_fwd(c, ct, x):
    return _hdot(c, x), (c, ct)


def _cdl_bwd(res, g):
    c, ct = res
    return jnp.zeros_like(c), jnp.zeros_like(ct), _hdot(ct, g)


cdot_left.defvjp(_cdl_fwd, _cdl_bwd)


@jax.custom_vjp
def cdot_right(x, c, ct):
    return _hdot(x, c)


def _cdr_fwd(x, c, ct):
    return _hdot(x, c), (c, ct)


def _cdr_bwd(res, g):
    c, ct = res
    return _hdot(g, ct), jnp.zeros_like(c), jnp.zeros_like(ct)


cdot_right.defvjp(_cdr_fwd, _cdr_bwd)


def _sigmoid(x):
    return 1.0 / (1.0 + jnp.exp(-x))


def _silu(x):
    return x * _sigmoid(x)


def _softplus(x):
    return jnp.maximum(x, 0.0) + jnp.log(1.0 + jnp.exp(-jnp.abs(x)))


def _log_sigmoid(x):
    return jnp.minimum(x, 0.0) - jnp.log(1.0 + jnp.exp(-jnp.abs(x)))


def _gelu(x):
    c = math.sqrt(2.0 / math.pi)
    return 0.5 * x * (1.0 + jnp.tanh(c * (x + 0.044715 * (x * x * x))))


def _rms(x, g):
    return x * lax.rsqrt(jnp.mean(x * x, axis=-1, keepdims=True) + EPS) * g


def _iota2(shape, axis):
    return lax.broadcasted_iota(jnp.int32, shape, axis)


def matmul(a, b, form, *, name, G=1, out_dtype=F32, add=None):
    isz = lambda t: jnp.dtype(t.dtype).itemsize
    osz = jnp.dtype(out_dtype).itemsize + (isz(add) if add is not None else 0)

    def fits(bm, bn, bk):
        return 2 * (bm * bk * isz(a) + bk * bn * isz(b) + bm * bn * osz) + 4 * bm * bn <= _MATMUL_VMEM

    if form in ("nn", "nt"):
        M = a.shape[0]
        K = a.shape[1] // G
        N = b.shape[2] if form == "nn" else b.shape[1]
        bm, bn, bk = min(M, 1024), _pick(N, 1536), _pick(K, 2048)
        while not fits(bm, bn, bk) and bk % 256 == 0:
            bk //= 2
        nj, nk = N // bn, K // bk
        grid = (G, M // bm, nj, nk)
        a_spec = pl.BlockSpec((bm, bk), lambda g, i, j, k: (i, g * nk + k))
        if form == "nn":
            b_spec = pl.BlockSpec((None, bk, bn), lambda g, i, j, k: (g, k, j))
        else:
            b_spec = pl.BlockSpec((None, bn, bk), lambda g, i, j, k: (g, j, k))
        o_spec = pl.BlockSpec((bm, bn), lambda g, i, j, k: (i, g * nj + j))
        out_shape = jax.ShapeDtypeStruct((M, G * N), out_dtype)
    else:
        T = a.shape[0]
        Ka, Nb = a.shape[1] // G, b.shape[1] // G
        bm, bn, bk = _pick(Ka, 1408), _pick(Nb, 1536), min(T, 2048)
        while not fits(bm, bn, bk) and bk % 512 == 0:
            bk //= 2
        ni, nj, nk = Ka // bm, Nb // bn, T // bk
        grid = (G, ni, nj, nk)
        a_spec = pl.BlockSpec((bk, bm), lambda g, i, j, k: (k, g * ni + i))
        b_spec = pl.BlockSpec((bk, bn), lambda g, i, j, k: (k, g * nj + j))
        o_spec = pl.BlockSpec((None, bm, bn), lambda g, i, j, k: (g, i, j))
        out_shape = jax.ShapeDtypeStruct((G, Ka, Nb), out_dtype)
    has_add = add is not None

    def finish(refs, r):
        if has_add:
            r = r + refs[2][...].astype(F32)
        o_ref = refs[3] if has_add else refs[2]
        o_ref[...] = r.astype(o_ref.dtype)

    def body_one(*refs):
        finish(refs, _dot_raw(refs[0][...], refs[1][...], form))

    def body_acc(*refs):
        acc_ref = refs[-1]
        k = pl.program_id(3)

        @pl.when(k == 0)
        def _():
            acc_ref[...] = jnp.zeros_like(acc_ref)

        acc_ref[...] += _dot_raw(refs[0][...], refs[1][...], form)

        @pl.when(k == nk - 1)
        def _():
            finish(refs, acc_ref[...])

    in_specs = [a_spec, b_spec]
    args = [a, b]
    if has_add:
        in_specs.append(o_spec)
        args.append(add)
    return pl.pallas_call(
        body_one if nk == 1 else body_acc, name=name, grid=grid, in_specs=in_specs, out_specs=o_spec,
        out_shape=out_shape, scratch_shapes=[] if nk == 1 else [pltpu.VMEM((bm, bn), F32)],
        compiler_params=_cparams(4),
    )(*args)


def matmul_nt_norm_bwd(a, w, h, g, d, *, name):
    T, K = a.shape
    bm = min(T, 512)
    bk = _pick(K, 2048)
    nk = K // bk

    def body(a_ref, w_ref, h_ref, g_ref, d_ref, dh_ref, dg_ref, acc_ref):
        i, k = pl.program_id(0), pl.program_id(1)

        @pl.when((i == 0) & (k == 0))
        def _():
            dg_ref[...] = jnp.zeros_like(dg_ref)

        @pl.when(k == 0)
        def _():
            acc_ref[...] = jnp.zeros_like(acc_ref)

        acc_ref[...] += _dot_raw(a_ref[...], w_ref[...], "nt")

        @pl.when(k == nk - 1)
        def _():
            _, vjp = jax.vjp(lambda g_, h_: _rms(h_, g_), g_ref[...], h_ref[...])
            dg, dh = vjp(acc_ref[...])
            dh_ref[...] = dh + d_ref[...]
            dg_ref[...] += dg

    row = pl.BlockSpec((bm, D), lambda i, k: (i, 0))
    one = pl.BlockSpec((1, D), lambda i, k: (0, 0))
    return pl.pallas_call(
        body, name=name, grid=(T // bm, nk),
        in_specs=[pl.BlockSpec((bm, bk), lambda i, k: (i, k)), pl.BlockSpec((D, bk), lambda i, k: (0, k)), row, one, row],
        out_specs=[row, one], out_shape=[jax.ShapeDtypeStruct((T, D), F32), jax.ShapeDtypeStruct((1, D), F32)],
        scratch_shapes=[pltpu.VMEM((bm, D), F32)], compiler_params=_cparams(2),
    )(a, w, h, g, d)


def ffn_up(hn, w_il, *, name):
    T = hn.shape[0]
    bm, hb = min(T, 512), FFN_H // 2

    def body(a_ref, b_ref, act_ref, gu_ref):
        r = _dot_raw(a_ref[...], b_ref[...], "nn")
        act_ref[...] = (_silu(r[:, :hb]) * r[:, hb:]).astype(act_ref.dtype)
        gu_ref[...] = r.astype(gu_ref.dtype)

    return pl.pallas_call(
        body, name=name, grid=(2, T // bm),
        in_specs=[pl.BlockSpec((bm, D), lambda j, i: (i, 0)), pl.BlockSpec((D, 2 * hb), lambda j, i: (0, j))],
        out_specs=[pl.BlockSpec((bm, hb), lambda j, i: (i, j)), pl.BlockSpec((bm, 2 * hb), lambda j, i: (i, j))],
        out_shape=[jax.ShapeDtypeStruct((T, FFN_H), BF16), jax.ShapeDtypeStruct((T, 2 * FFN_H), BF16)],
        compiler_params=_cparams(2),
    )(hn, w_il)


_DACT_CHUNK = 512


def ffn_dact(d, w_down, gu, *, name):
    T = d.shape[0]
    bm, hb = min(T, 512), FFN_H // 2

    def body(d_ref, w_ref, gu_ref, o_ref):
        d_blk = d_ref[...].astype(BF16)
        for lo in range(0, hb, _DACT_CHUNK):
            hi = min(lo + _DACT_CHUNK, hb)
            da = _dot_raw(d_blk, w_ref[lo:hi, :], "nt")
            g, u = gu_ref[:, lo:hi].astype(F32), gu_ref[:, hb + lo:hb + hi].astype(F32)
            sg = _sigmoid(g)
            o_ref[:, lo:hi] = (da * u * (sg * (1.0 + g * (1.0 - sg)))).astype(o_ref.dtype)
            o_ref[:, hb + lo:hb + hi] = (da * (g * sg)).astype(o_ref.dtype)

    return pl.pallas_call(
        body, name=name, grid=(2, T // bm),
        in_specs=[pl.BlockSpec((bm, D), lambda j, i: (i, 0)), pl.BlockSpec((hb, D), lambda j, i: (j, 0)),
                  pl.BlockSpec((bm, 2 * hb), lambda j, i: (i, j))],
        out_specs=pl.BlockSpec((bm, 2 * hb), lambda j, i: (i, j)),
        out_shape=jax.ShapeDtypeStruct((T, 2 * FFN_H), BF16), compiler_params=_cparams(2),
    )(d, w_down, gu)


def _row_entry(e):
    return e if isinstance(e, tuple) else (e, e.shape[1], 0)


def _row_spec(bt, e):
    _, width, idx = e
    return pl.BlockSpec((bt, width), lambda i: (i, idx))


def _full_spec(p):
    return pl.BlockSpec(p.shape, lambda i: (0,) * p.ndim)


def rowwise(f, params, rows, outs, *, bt, name):
    rows = [_row_entry(e) for e in rows]
    T = rows[0][0].shape[0]
    bt = min(bt, T)
    np_, nr = len(params), len(rows)

    def body(*refs):
        p = tuple(r[...].astype(F32) for r in refs[:np_])
        rw = tuple(r[...].astype(F32) for r in refs[np_:np_ + nr])
        res = f(p, rw)
        for o_ref, o in zip(refs[np_ + nr:], res):
            o_ref[...] = o.astype(o_ref.dtype)

    res = pl.pallas_call(
        body, name=name, grid=(T // bt,),
        in_specs=[_full_spec(p) for p in params] + [_row_spec(bt, e) for e in rows],
        out_specs=[pl.BlockSpec((bt, w), lambda i: (i, 0)) for w, _ in outs],
        out_shape=[jax.ShapeDtypeStruct((T, w), dt) for w, dt in outs],
        compiler_params=_cparams(1),
    )(*params, *[e[0] for e in rows])
    return list(res)


def rowwise_vjp(f, params, rows, cts, drow_dtypes, *, bt, name):
    rows = [_row_entry(e) for e in rows]
    cts = [_row_entry(e) for e in cts]
    T = rows[0][0].shape[0]
    bt = min(bt, T)
    np_, nr, nc = len(params), len(rows), len(cts)
    want = [i for i, dt in enumerate(drow_dtypes) if dt is not None]

    def body(*refs):
        p = tuple(r[...].astype(F32) for r in refs[:np_])
        rw = tuple(r[...].astype(F32) for r in refs[np_:np_ + nr])
        ct = tuple(r[...].astype(F32) for r in refs[np_ + nr:np_ + nr + nc])
        outs = refs[np_ + nr + nc:]
        _, vjp = jax.vjp(f, p, rw)
        dp, dr = vjp(ct)

        @pl.when(pl.program_id(0) == 0)
        def _():
            for o in outs[:np_]:
                o[...] = jnp.zeros_like(o)

        for o, d in zip(outs[:np_], dp):
            o[...] += d
        for o, i in zip(outs[np_:], want):
            o[...] = dr[i].astype(o.dtype)

    res = pl.pallas_call(
        body, name=name, grid=(T // bt,),
        in_specs=[_full_spec(p) for p in params] + [_row_spec(bt, e) for e in rows] + [_row_spec(bt, e) for e in cts],
        out_specs=[_full_spec(p) for p in params] + [pl.BlockSpec((bt, rows[i][1]), lambda i_: (i_, 0)) for i in want],
        out_shape=[jax.ShapeDtypeStruct(p.shape, F32) for p in params]
        + [jax.ShapeDtypeStruct((T, rows[i][1]), drow_dtypes[i]) for i in want],
        compiler_params=_cparams(1),
    )(*params, *[e[0] for e in rows], *[e[0] for e in cts])
    res = list(res)
    return res[:np_], res[np_:]


def f_rmsnorm(p, r):
    return (_rms(r[0], p[0]),)


def f_gla_gate_in(p, r):
    w_a2, b_a = p
    z = bdot(r[0], w_a2, "nn") + b_a
    return (_log_sigmoid(z) / GLA_TAU,)


def f_gla_gate_in_fwd(p, r):
    w_a2, b_a = p
    z = _dot_raw(r[0], w_a2, "nn") + b_a
    return (_log_sigmoid(z) / GLA_TAU,)


def f_s5_act(p, r):
    (dsk,) = p
    ycp, u = r
    return (_gelu(ycp + dsk * u),)


def f_glu_res(p, r):
    vg, h = r
    return (vg[:, :D] * _sigmoid(vg[:, D:]) + h,)


def f_glu(p, r):
    vg = r[0]
    return (vg[:, :D] * _sigmoid(vg[:, D:]),)


def loss_head(h, g, target, *, name):
    T = h.shape[0]
    bt = min(T, 256)

    def lossf(g_, h_, t_):
        e = _rms(h_, g_) - t_
        return (0.5 / D) * jnp.sum(e * e)

    def body(g_ref, h_ref, t_ref, loss_ref, dg_ref, dh_ref):
        @pl.when(pl.program_id(0) == 0)
        def _():
            loss_ref[...] = jnp.zeros_like(loss_ref)
            dg_ref[...] = jnp.zeros_like(dg_ref)

        val, vjp = jax.vjp(lossf, g_ref[...], h_ref[...], t_ref[...])
        dg, dh, _ = vjp(jnp.ones((), F32))
        loss_ref[...] += jnp.full(loss_ref.shape, val, F32)
        dg_ref[...] += dg
        dh_ref[...] = dh

    row = pl.BlockSpec((bt, D), lambda i: (i, 0))
    one = pl.BlockSpec((1, D), lambda i: (0, 0))
    loss, dg, dh = pl.pallas_call(
        body, name=name, grid=(T // bt,), in_specs=[one, row, row],
        out_specs=[pl.BlockSpec((1, LANE), lambda i: (0, 0)), one, row],
        out_shape=[jax.ShapeDtypeStruct((1, LANE), F32), jax.ShapeDtypeStruct((1, D), F32),
                   jax.ShapeDtypeStruct((T, D), F32)],
        compiler_params=_cparams(1),
    )(g, h, target)
    return loss[0, 0], dg, dh


def _gla_consts():
    r, c = _iota2((CH, CH), 0), _iota2((CH, CH), 1)
    return (r >= c).astype(F32), (r <= c).astype(F32), r >= c


def _gla_chunk(q, k, v, la, st, consts, dot, cdl):
    L, Lt, tril = consts
    lc = cdl(L, Lt, la)
    lend = lc[CH - 1:CH, :]
    e, ei = jnp.exp(lc), jnp.exp(-lc)
    qs = q * (GLA_DK ** -0.5)
    qf, kf, qb, kb = qs * e, k * ei, qs * ei, k * e
    sc = jnp.where(tril, dot(qf, kf, "nt"), dot(qb, kb, "nt"))
    o = dot(sc, v, "nn") + dot(qf, st, "nt")
    kd = k * jnp.exp(lend - lc)
    st_new = st * jnp.exp(lend) + dot(v, kd, "tn")
    return o, st_new


def _gla_block(q, k, v, la, st, nc, dot, cdl):
    consts = _gla_consts()
    outs = []
    for c in range(nc):
        sl = slice(c * CH, (c + 1) * CH)
        o, st = _gla_chunk(q[sl], k[sl], v[sl], la[sl], st, consts, dot, cdl)
        outs.append(o)
    return jnp.concatenate(outs, axis=0), st


_GLA_HP = 2


def _gla_specs(rows, rev, nb):
    t = (lambda j: nb - 1 - j) if rev else (lambda j: j)
    hp, ng = _GLA_HP, GLA_H // _GLA_HP
    q = pl.BlockSpec((rows, hp * GLA_DK), lambda h, j: (t(j), h))
    k = pl.BlockSpec((rows, hp * GLA_DK), lambda h, j: (t(j), ng + h))
    v = pl.BlockSpec((rows, hp * GLA_DV), lambda h, j: (t(j), ng + h))
    la = pl.BlockSpec((rows, hp * GLA_DK), lambda h, j: (t(j), h))
    ss = pl.BlockSpec((None, hp, GLA_DV, GLA_DK), lambda h, j: (t(j), h, 0, 0))
    o = pl.BlockSpec((rows, hp * GLA_DV), lambda h, j: (t(j), h))
    r = pl.BlockSpec((rows, hp * GLA_DV), lambda h, j: (t(j), 2 * ng + h))
    g = pl.BlockSpec((1, hp * GLA_DV), lambda h, j: (0, h))
    return q, k, v, la, ss, o, r, g


def _gla_heads(q, k, v, la, r, ng, sts, nc, dot, cdl):
    outs, new = [], []
    for i in range(_GLA_HP):
        kk, vv = slice(i * GLA_DK, (i + 1) * GLA_DK), slice(i * GLA_DV, (i + 1) * GLA_DV)
        o, st = _gla_block(q[:, kk], k[:, kk], v[:, vv], la[:, kk], sts[i], nc, dot, cdl)
        outs.append(_rms(o, ng[:, vv]) * _silu(r[:, vv]))
        new.append(st)
    return jnp.concatenate(outs, axis=1), tuple(new)


def gla_scan_fwd(proj, la, ng, *, nc, name):
    T = proj.shape[0]
    rows = min(T, nc * CH)
    nc = rows // CH
    nb = T // rows
    q_s, k_s, v_s, la_s, ss_s, o_s, r_s, g_s = _gla_specs(rows, False, nb)

    def body(q_ref, k_ref, v_ref, la_ref, r_ref, g_ref, o_ref, ss_ref, st_ref):
        @pl.when(pl.program_id(1) == 0)
        def _():
            st_ref[...] = jnp.zeros_like(st_ref)

        ss_ref[...] = st_ref[...]
        sts = tuple(st_ref[i] for i in range(_GLA_HP))
        o, sts = _gla_heads(q_ref[...], k_ref[...], v_ref[...], la_ref[...], r_ref[...], g_ref[...], sts, nc,
                            _dot_raw, lambda c, ct, x: _hdot(c, x))
        o_ref[...] = o.astype(o_ref.dtype)
        for i in range(_GLA_HP):
            st_ref[i] = sts[i]

    return pl.pallas_call(
        body, name=name, grid=(GLA_H // _GLA_HP, nb), in_specs=[q_s, k_s, v_s, la_s, r_s, g_s], out_specs=[o_s, ss_s],
        out_shape=[jax.ShapeDtypeStruct((T, GLA_VD), BF16), jax.ShapeDtypeStruct((nb, GLA_H, GLA_DV, GLA_DK), F32)],
        scratch_shapes=[pltpu.VMEM((_GLA_HP, GLA_DV, GLA_DK), F32)], compiler_params=_cparams(2),
    )(proj, proj, proj, la, proj, ng)


def gla_scan_bwd(proj, la, ng, ss, do, *, nc, name):
    T = proj.shape[0]
    rows = min(T, nc * CH)
    nc = rows // CH
    nb = T // rows
    q_s, k_s, v_s, la_s, ss_s, o_s, r_s, g_s = _gla_specs(rows, True, nb)
    t = lambda j: nb - 1 - j
    dqk_s = pl.BlockSpec((rows, _GLA_HP * GLA_DK), lambda h, j: (t(j), h))

    def body(q_ref, k_ref, v_ref, la_ref, r_ref, g_ref, ss_ref, do_ref,
             dq_ref, dk_ref, dv_ref, dla_ref, dr_ref, dg_ref, dst_ref):
        @pl.when(pl.program_id(1) == 0)
        def _():
            dst_ref[...] = jnp.zeros_like(dst_ref)
            dg_ref[...] = jnp.zeros_like(dg_ref)

        fn = lambda q, k, v, la_, r, g, *sts: _gla_heads(q, k, v, la_, r, g, sts, nc, bdot, cdot_left)
        _, vjp = jax.vjp(fn, q_ref[...], k_ref[...], v_ref[...], la_ref[...], r_ref[...], g_ref[...],
                         *[ss_ref[i] for i in range(_GLA_HP)])
        dq, dk, dv, dla, dr, dg, *dsts = vjp((do_ref[...], tuple(dst_ref[i] for i in range(_GLA_HP))))
        dq_ref[...] = dq.astype(dq_ref.dtype)
        dk_ref[...] = dk.astype(dk_ref.dtype)
        dv_ref[...] = dv.astype(dv_ref.dtype)
        dla_ref[...] = dla
        dr_ref[...] = dr.astype(dr_ref.dtype)
        dg_ref[...] += dg
        for i in range(_GLA_HP):
            dst_ref[i] = dsts[i]

    return pl.pallas_call(
        body, name=name, grid=(GLA_H // _GLA_HP, nb), in_specs=[q_s, k_s, v_s, la_s, r_s, g_s, ss_s, o_s],
        out_specs=[dqk_s, dqk_s, o_s, dqk_s, o_s, g_s],
        out_shape=[jax.ShapeDtypeStruct((T, GLA_QK), BF16), jax.ShapeDtypeStruct((T, GLA_QK), BF16),
                   jax.ShapeDtypeStruct((T, GLA_VD), BF16), jax.ShapeDtypeStruct((T, GLA_QK), F32),
                   jax.ShapeDtypeStruct((T, GLA_VD), BF16), jax.ShapeDtypeStruct((1, GLA_VD), F32)],
        scratch_shapes=[pltpu.VMEM((_GLA_HP, GLA_DV, GLA_DK), F32)], compiler_params=_cparams(2),
    )(proj, proj, proj, la, proj, ng, ss, do)


_CONV_W = 512
_CONV_OFF = SSD_DI // _CONV_W


def _conv_pre(x, prev8, w_ref, b_ref):
    bt = x.shape[0]
    ext = jnp.concatenate([prev8, x], axis=0)
    shifted = []
    for j in range(SSD_K):
        s = SSD_K - 1 - j
        shifted.append(x if s == 0 else pltpu.roll(ext, s, 0)[8:8 + bt])
    pre = b_ref[...] + sum(w_ref[j:j + 1, :] * shifted[j] for j in range(SSD_K))
    return pre, shifted


def ssd_conv_fwd(proj, w, b, *, name):
    T = proj.shape[0]
    bt = min(T, 512)
    nb = T // bt

    def body(x_ref, w_ref, b_ref, o_ref, carry_ref):
        @pl.when(pl.program_id(1) == 0)
        def _():
            carry_ref[...] = jnp.zeros_like(carry_ref)

        x = x_ref[...]
        pre, _ = _conv_pre(x, carry_ref[...], w_ref, b_ref)
        o_ref[...] = _silu(pre)
        carry_ref[...] = x[bt - 8:, :]

    return pl.pallas_call(
        body, name=name, grid=(SSD_CONV // _CONV_W, nb),
        in_specs=[pl.BlockSpec((bt, _CONV_W), lambda c, t: (t, _CONV_OFF + c)),
                  pl.BlockSpec((SSD_K, _CONV_W), lambda c, t: (0, c)),
                  pl.BlockSpec((1, _CONV_W), lambda c, t: (0, c))],
        out_specs=pl.BlockSpec((bt, _CONV_W), lambda c, t: (t, c)),
        out_shape=jax.ShapeDtypeStruct((T, SSD_CONV), F32),
        scratch_shapes=[pltpu.VMEM((8, _CONV_W), F32)], compiler_params=_cparams(2),
    )(proj, w, b)


def ssd_conv_bwd(proj, w, b, dout, col0, *, name):
    T, ncols = dout.shape
    bt = min(T, 512)
    nb = T // bt
    r8 = bt // 8
    c0 = col0 // _CONV_W

    def body(x_ref, xp_ref, w_ref, b_ref, do_ref, dx_ref, dw_ref, db_ref, carry_ref):
        t = pl.program_id(1)

        @pl.when(t == 0)
        def _():
            carry_ref[...] = jnp.zeros_like(carry_ref)
            dw_ref[...] = jnp.zeros_like(dw_ref)
            db_ref[...] = jnp.zeros_like(db_ref)

        x = x_ref[...]
        prev8 = jnp.where(t == nb - 1, 0.0, xp_ref[...])
        pre, shifted = _conv_pre(x, prev8, w_ref, b_ref)
        sg = _sigmoid(pre)
        dpre = do_ref[...] * (sg * (1.0 + pre * (1.0 - sg)))
        ext = jnp.concatenate([dpre, carry_ref[...]], axis=0)
        dx = w_ref[SSD_K - 1:SSD_K, :] * dpre
        for j in range(SSD_K - 1):
            s = SSD_K - 1 - j
            dx = dx + w_ref[j:j + 1, :] * pltpu.roll(ext, bt + 8 - s, 0)[:bt]
        dx_ref[...] = dx.astype(dx_ref.dtype)
        dw_ref[...] += jnp.concatenate([jnp.sum(dpre * shifted[j], axis=0, keepdims=True) for j in range(SSD_K)], axis=0)
        db_ref[...] += jnp.sum(dpre, axis=0, keepdims=True)
        carry_ref[...] = dpre[:8, :]

    rt = lambda t: nb - 1 - t
    return pl.pallas_call(
        body, name=name, grid=(ncols // _CONV_W, nb),
        in_specs=[pl.BlockSpec((bt, _CONV_W), lambda c, t: (rt(t), _CONV_OFF + c0 + c)),
                  pl.BlockSpec((8, _CONV_W), lambda c, t: (jnp.maximum(rt(t) * r8 - 1, 0), _CONV_OFF + c0 + c)),
                  pl.BlockSpec((SSD_K, _CONV_W), lambda c, t: (0, c0 + c)),
                  pl.BlockSpec((1, _CONV_W), lambda c, t: (0, c0 + c)),
                  pl.BlockSpec((bt, _CONV_W), lambda c, t: (rt(t), c))],
        out_specs=[pl.BlockSpec((bt, _CONV_W), lambda c, t: (rt(t), c)),
                   pl.BlockSpec((SSD_K, _CONV_W), lambda c, t: (0, c)),
                   pl.BlockSpec((1, _CONV_W), lambda c, t: (0, c))],
        out_shape=[jax.ShapeDtypeStruct((T, ncols), BF16), jax.ShapeDtypeStruct((SSD_K, ncols), F32),
                   jax.ShapeDtypeStruct((1, ncols), F32)],
        scratch_shapes=[pltpu.VMEM((8, _CONV_W), F32)], compiler_params=_cparams(2),
    )(proj, proj, w, b, dout)


_SSD_U = 2 * CH


def _ssd_unit(xs, bm, cm, dtraw, dtb, alog, dsk, hp, g, dot, cdl, cdr):
    U, P2 = _SSD_U, 2 * SSD_HD
    r, c = _iota2((U, U), 0), _iota2((U, U), 1)
    same = (r // CH) == (c // CH)
    Lb = (same & (r >= c)).astype(F32)
    Ub = (same & (r <= c)).astype(F32)
    lane = _iota2((1, U), 1)
    lo_lane = _iota2((1, P2), 1) < SSD_HD
    lo_sub = _iota2((P2, 1), 0) < SSD_HD
    diag2 = (_iota2((CH, P2), 0) == (_iota2((CH, P2), 1) % CH)).astype(F32)

    dt = _softplus(dtraw + dtb)
    da = dt * (-jnp.exp(alog))
    cum = cdl(Lb, Ub, da)
    ys = []
    new_hp = []
    for pr in range(2):
        xs_p = xs[:, pr * P2:(pr + 1) * P2]
        cols, dts, dks = [], [], []
        for jj in range(2):
            oh_l = (lane == g * (SSD_H // SSD_G) + 2 * pr + jj).astype(F32)
            cols.append(jnp.sum(cum * oh_l, axis=1, keepdims=True))
            dts.append(jnp.sum(dt * oh_l, axis=1, keepdims=True))
            dks.append(jnp.sum(dsk * oh_l, axis=1, keepdims=True))
        dsk_p = jnp.where(lo_lane, dks[0], dks[1])
        h = hp[pr]
        yc = []
        for ci in range(2):
            sl = slice(ci * CH, (ci + 1) * CH)
            xs_c, bm_c, cm_c = xs_p[sl], bm[sl], cm[sl]
            col = jnp.where(lo_lane, cols[0][sl], cols[1][sl])
            dtc = jnp.where(lo_lane, dts[0][sl], dts[1][sl])
            row = jnp.sum(diag2 * col, axis=0, keepdims=True)
            dtrow = jnp.sum(diag2 * dtc, axis=0, keepdims=True)
            cb = dot(cm_c, jnp.concatenate([bm_c, bm_c], axis=0), "nt")
            mix = cb * jnp.exp(-jnp.abs(col - row)) * dtrow
            xbd = jnp.concatenate([jnp.where(lo_lane, xs_c, 0.0), jnp.where(lo_lane, 0.0, xs_c)], axis=0)
            y_intra = dot(mix, xbd, "nn")
            ce = jnp.where(lo_lane, cols[0][ci * CH + CH - 1:ci * CH + CH, :], cols[1][ci * CH + CH - 1:ci * CH + CH, :])
            y_inter = dot(cm_c, h, "nt") * jnp.exp(col)
            xw = xs_c * (dtc * jnp.exp(ce - col))
            ce_s = [cols[jj][ci * CH + CH - 1:ci * CH + CH, :] for jj in range(2)]
            a_p = jnp.where(lo_sub, jnp.exp(ce_s[0]), jnp.exp(ce_s[1]))
            h = a_p * h + dot(xw, bm_c, "tn")
            yc.append(y_intra + y_inter + dsk_p * xs_c)
        ys.append(jnp.concatenate(yc, axis=0))
        new_hp.append(h)
    return jnp.concatenate(ys, axis=1), tuple(new_hp)


def _ssd_block(xs, bm, cm, dtraw, z, dtb, alog, dsk, ng, hp, g, nu, dot, cdl, cdr):
    outs = []
    for u in range(nu):
        sl = slice(u * _SSD_U, (u + 1) * _SSD_U)
        y, hp = _ssd_unit(xs[sl], bm[sl], cm[sl], dtraw[sl], dtb, alog, dsk, hp, g, dot, cdl, cdr)
        outs.append(y)
    return _rms(jnp.concatenate(outs, axis=0) * _silu(z), ng), hp


def _ssd_specs(rows, rev, nb):
    t = (lambda j: nb - 1 - j) if rev else (lambda j: j)
    gw = SSD_DI // SSD_G
    xs = pl.BlockSpec((rows, gw), lambda j, g: (t(j), g))
    bm = pl.BlockSpec((rows, SSD_N), lambda j, g: (t(j), SSD_DI // SSD_N + g))
    cm = pl.BlockSpec((rows, SSD_N), lambda j, g: (t(j), (SSD_DI + SSD_GN) // SSD_N + g))
    dtr = pl.BlockSpec((rows, LANE), lambda j, g: (t(j), (SSD_DI + SSD_CONV) // LANE))
    par = pl.BlockSpec((1, LANE), lambda j, g: (0, 0))
    hs = pl.BlockSpec((None, None, 2, 2 * SSD_HD, SSD_N), lambda j, g: (t(j), g, 0, 0, 0))
    y = pl.BlockSpec((rows, gw), lambda j, g: (t(j), g))
    ng = pl.BlockSpec((1, gw), lambda j, g: (0, g))
    return xs, bm, cm, dtr, par, hs, y, ng


def ssd_scan_fwd(xbc, proj, dtb, alog, dsk, ng, *, nu, name):
    T = xbc.shape[0]
    rows = min(T, nu * _SSD_U)
    nu = rows // _SSD_U
    nb = T // rows
    xs_s, bm_s, cm_s, dt_s, par_s, hs_s, y_s, ng_s = _ssd_specs(rows, False, nb)

    def body(xs_ref, bm_ref, cm_ref, dt_ref, z_ref, dtb_ref, al_ref, dsk_ref, ng_ref, y_ref, hs_ref, h_ref):
        g = pl.program_id(1)

        @pl.when(pl.program_id(0) == 0)
        def _():
            h_ref[g] = jnp.zeros(h_ref.shape[1:], F32)

        hs_ref[...] = h_ref[g]
        hp = (h_ref[g, 0], h_ref[g, 1])
        y, hp = _ssd_block(xs_ref[...], bm_ref[...], cm_ref[...], dt_ref[...], z_ref[...], dtb_ref[...], al_ref[...],
                           dsk_ref[...], ng_ref[...], hp, g, nu, _dot_raw, lambda c, ct, x: _hdot(c, x), lambda x, c, ct: _hdot(x, c))
        y_ref[...] = y.astype(y_ref.dtype)
        h_ref[g, 0] = hp[0]
        h_ref[g, 1] = hp[1]

    return pl.pallas_call(
        body, name=name, grid=(nb, SSD_G), in_specs=[xs_s, bm_s, cm_s, dt_s, y_s, par_s, par_s, par_s, ng_s],
        out_specs=[y_s, hs_s],
        out_shape=[jax.ShapeDtypeStruct((T, SSD_DI), BF16), jax.ShapeDtypeStruct((nb, SSD_G, 2, 2 * SSD_HD, SSD_N), F32)],
        scratch_shapes=[pltpu.VMEM((SSD_G, 2, 2 * SSD_HD, SSD_N), F32)], compiler_params=_cparams(2),
    )(xbc, xbc, xbc, proj, proj, dtb, alog, dsk, ng)


def ssd_scan_bwd(xbc, proj, dtb, alog, dsk, ng, hs, dy, *, nu, name):
    T = xbc.shape[0]
    rows = min(T, nu * _SSD_U)
    nu = rows // _SSD_U
    nb = T // rows
    xs_s, bm_s, cm_s, dt_s, par_s, hs_s, y_s, ng_s = _ssd_specs(rows, True, nb)
    gw = SSD_DI // SSD_G
    dng_s = pl.BlockSpec((1, SSD_DI), lambda j, g: (0, 0))
    t = lambda j: nb - 1 - j
    n_s = pl.BlockSpec((rows, SSD_N), lambda j, g: (t(j), g))
    ddt_s = pl.BlockSpec((rows, LANE), lambda j, g: (t(j), 0))

    def body(xs_ref, bm_ref, cm_ref, dt_ref, z_ref, dtb_ref, al_ref, dsk_ref, ng_ref, hs_ref, dy_ref,
             dxs_ref, dbm_ref, dcm_ref, ddt_ref, ddtb_ref, dal_ref, ddsk_ref, dz_ref, dng_ref, dh_ref):
        j, g = pl.program_id(0), pl.program_id(1)

        @pl.when(j == 0)
        def _():
            dh_ref[g] = jnp.zeros(dh_ref.shape[1:], F32)

        @pl.when((j == 0) & (g == 0))
        def _():
            ddtb_ref[...] = jnp.zeros_like(ddtb_ref)
            dal_ref[...] = jnp.zeros_like(dal_ref)
            ddsk_ref[...] = jnp.zeros_like(ddsk_ref)
            dng_ref[...] = jnp.zeros_like(dng_ref)

        @pl.when(g == 0)
        def _():
            ddt_ref[...] = jnp.zeros_like(ddt_ref)

        fn = lambda xs, bm, cm, dtr, z, dtb_, al, dsk_, ng_, h0, h1: _ssd_block(
            xs, bm, cm, dtr, z, dtb_, al, dsk_, ng_, (h0, h1), g, nu, bdot, cdot_left, cdot_right)
        _, vjp = jax.vjp(fn, xs_ref[...], bm_ref[...], cm_ref[...], dt_ref[...], z_ref[...], dtb_ref[...], al_ref[...],
                         dsk_ref[...], ng_ref[...], hs_ref[0], hs_ref[1])
        dxs, dbm, dcm, ddt, dz, ddtb, dal, ddsk, dng, dh0, dh1 = vjp((dy_ref[...], (dh_ref[g, 0], dh_ref[g, 1])))
        dz_ref[...] = dz.astype(dz_ref.dtype)
        lanes = pl.ds(pl.multiple_of(g * gw, gw), gw)
        dng_ref[:, lanes] = dng_ref[:, lanes] + dng
        dxs_ref[...] = dxs
        dbm_ref[...] = dbm
        dcm_ref[...] = dcm
        ddt_ref[...] += ddt
        ddtb_ref[...] += ddtb
        dal_ref[...] += dal
        ddsk_ref[...] += ddsk
        dh_ref[g, 0] = dh0
        dh_ref[g, 1] = dh1

    return pl.pallas_call(
        body, name=name, grid=(nb, SSD_G), in_specs=[xs_s, bm_s, cm_s, dt_s, y_s, par_s, par_s, par_s, ng_s, hs_s, y_s],
        out_specs=[y_s, n_s, n_s, ddt_s, par_s, par_s, par_s, y_s, dng_s],
        out_shape=[jax.ShapeDtypeStruct((T, SSD_DI), F32), jax.ShapeDtypeStruct((T, SSD_GN), F32),
                   jax.ShapeDtypeStruct((T, SSD_GN), F32), jax.ShapeDtypeStruct((T, LANE), F32),
                   jax.ShapeDtypeStruct((1, LANE), F32), jax.ShapeDtypeStruct((1, LANE), F32),
                   jax.ShapeDtypeStruct((1, LANE), F32), jax.ShapeDtypeStruct((T, SSD_DI), BF16),
                   jax.ShapeDtypeStruct((1, SSD_DI), F32)],
        scratch_shapes=[pltpu.VMEM((SSD_G, 2, 2 * SSD_HD, SSD_N), F32)], compiler_params=_cparams(2),
    )(xbc, xbc, xbc, proj, proj, dtb, alog, dsk, ng, hs, dy)


def _s5_param_f(log_dt, a_re, a_im, bre_t, bim_t, cim, cdl):
    n = S5_NG * S5_GS
    r, c = _iota2((n, S5_NG), 0), _iota2((n, S5_NG), 1)
    E = ((r // S5_GS) == c).astype(F32)
    rt, ct = _iota2((S5_NG, n), 0), _iota2((S5_NG, n), 1)
    Et = ((ct // S5_GS) == rt).astype(F32)
    step = jnp.exp(log_dt)
    mag = jnp.exp(step * a_re)
    abr = mag * jnp.cos(step * a_im)
    abi = mag * jnp.sin(step * a_im)
    den = a_re * a_re + a_im * a_im
    nr, ni = abr - 1.0, abi
    fr = (nr * a_re + ni * a_im) / den
    fi = (ni * a_re - nr * a_im) / den
    Fr, Fi = cdl(E, Et, fr), cdl(E, Et, fi)
    bbr = Fr * bre_t - Fi * bim_t
    bbi = Fr * bim_t + Fi * bre_t
    return abr, abi, bbr, bbi, -cim


def _whole(a):
    return pl.BlockSpec(a.shape, lambda: (0,) * a.ndim)


def s5_param_fwd(args, *, name):
    def body(*refs):
        res = _s5_param_f(*[r[...] for r in refs[:6]], lambda c, ct, x: _hdot(c, x))
        for o, v in zip(refs[6:], res):
            o[...] = v

    shapes = [(S5_NG, S5_P), (S5_NG, S5_P)] + [(S5_NG * S5_GS, S5_P)] * 3
    return pl.pallas_call(
        body, name=name, in_specs=[_whole(a) for a in args], out_specs=[pl.BlockSpec(s, lambda: (0, 0)) for s in shapes],
        out_shape=[jax.ShapeDtypeStruct(s, F32) for s in shapes],
        compiler_params=pltpu.CompilerParams(vmem_limit_bytes=VMEM_LIMIT),
    )(*args)


def s5_param_bwd(args, cts, *, name):
    def body(*refs):
        fn = lambda *a: _s5_param_f(*a, cdot_left)
        _, vjp = jax.vjp(fn, *[r[...] for r in refs[:6]])
        grads = vjp(tuple(r[...] for r in refs[6:11]))
        for o, v in zip(refs[11:], grads):
            o[...] = v

    return pl.pallas_call(
        body, name=name, in_specs=[_whole(a) for a in list(args) + list(cts)],
        out_specs=[_whole(a) for a in args], out_shape=[jax.ShapeDtypeStruct(a.shape, F32) for a in args],
        compiler_params=pltpu.CompilerParams(vmem_limit_bytes=VMEM_LIMIT),
    )(*args, *cts)


_S5_W = S5_BLK * S5_P


def _cmul_add(xr, xi, pr, pi, sr, si):
    return xr + (pr * sr - pi * si), xi + (pr * si + pi * sr)


def _s5_powers(ar, ai):
    pw = [(ar, ai)]
    for _ in range(7):
        qr, qi = pw[-1]
        pw.append((qr * ar - qi * ai, qr * ai + qi * ar))
    return pw


def s5_scan_fwd(u, wb, a_re, a_im, wc, *, name):
    T = u.shape[0]
    bt = min(T, 256)
    nb = T // bt

    def body(u_ref, wb_ref, ar_ref, ai_ref, wc_ref, x_ref, y_ref, bu_ref, carry_ref):
        @pl.when(pl.program_id(1) == 0)
        def _():
            carry_ref[...] = jnp.zeros_like(carry_ref)

        bu_ref[...] = _dot_raw(u_ref[...], wb_ref[...], "nn")
        ar, ai = ar_ref[...], ai_ref[...]
        pw = _s5_powers(ar, ai)
        pwr = jnp.concatenate([p[0] for p in pw], axis=0)
        pwi = jnp.concatenate([p[1] for p in pw], axis=0)
        rin = _iota2((8, _S5_W), 0)
        cr, ci = carry_ref[0:1, :], carry_ref[1:2, :]
        for t in range(bt // 8):
            sl = slice(8 * t, 8 * t + 8)
            xr, xi = bu_ref[sl, :_S5_W], bu_ref[sl, _S5_W:]
            for s in (1, 2, 4):
                m = rin >= s
                sr = jnp.where(m, pltpu.roll(xr, s, 0), 0.0)
                si = jnp.where(m, pltpu.roll(xi, s, 0), 0.0)
                xr, xi = _cmul_add(xr, xi, *pw[s - 1], sr, si)
            xr, xi = _cmul_add(xr, xi, pwr, pwi, cr, ci)
            x_ref[sl, :_S5_W] = xr
            x_ref[sl, _S5_W:] = xi
            cr, ci = xr[7:8, :], xi[7:8, :]
        carry_ref[0:1, :] = cr
        carry_ref[1:2, :] = ci
        y_ref[...] = _dot_raw(x_ref[...], wc_ref[...], "nn")

    nblk = S5_NG // S5_BLK
    blk = pl.BlockSpec((bt, 2 * _S5_W), lambda g, t: (t, g))
    col = pl.BlockSpec((bt, LANE), lambda g, t: (t, g))
    a_s = pl.BlockSpec((None, 1, _S5_W), lambda g, t: (g, 0, 0))
    wb_s = pl.BlockSpec((None, LANE, 2 * _S5_W), lambda g, t: (g, 0, 0))
    wc_s = pl.BlockSpec((None, 2 * _S5_W, LANE), lambda g, t: (g, 0, 0))
    return pl.pallas_call(
        body, name=name, grid=(nblk, nb), in_specs=[col, wb_s, a_s, a_s, wc_s], out_specs=[blk, col],
        out_shape=[jax.ShapeDtypeStruct((T, nblk * 2 * _S5_W), F32), jax.ShapeDtypeStruct((T, nblk * LANE), F32)],
        scratch_shapes=[pltpu.VMEM((bt, 2 * _S5_W), F32), pltpu.VMEM((8, _S5_W), F32)],
        compiler_params=_cparams(2),
    )(u, wb, a_re, a_im, wc)


def s5_scan_bwd(dy, x, u, wb, a_re, a_im, wc, *, name):
    T = dy.shape[0]
    bt = min(T, 256)
    nb = T // bt

    def body(dy_ref, x_ref, u_ref, wb_ref, ar_ref, ai_ref, wc_ref, du_ref, dwb_ref, dwc_ref, dar_ref, dai_ref,
             g_ref, lam_ref, carry_ref):
        @pl.when(pl.program_id(1) == 0)
        def _():
            carry_ref[...] = jnp.zeros_like(carry_ref)
            dar_ref[...] = jnp.zeros_like(dar_ref)
            dai_ref[...] = jnp.zeros_like(dai_ref)
            dwb_ref[...] = jnp.zeros_like(dwb_ref)
            dwc_ref[...] = jnp.zeros_like(dwc_ref)

        g_ref[...] = _dot_raw(dy_ref[...], wc_ref[...], "nt")
        pw = _s5_powers(ar_ref[...], -ai_ref[...])
        pwr = jnp.concatenate([p[0] for p in reversed(pw)], axis=0)
        pwi = jnp.concatenate([p[1] for p in reversed(pw)], axis=0)
        rin = _iota2((8, _S5_W), 0)
        cr, ci = carry_ref[0:1, :], carry_ref[1:2, :]
        acc_r = jnp.zeros((8, _S5_W), F32)
        acc_i = jnp.zeros((8, _S5_W), F32)
        for t in reversed(range(bt // 8)):
            sl = slice(8 * t, 8 * t + 8)
            lr, li = g_ref[sl, :_S5_W], g_ref[sl, _S5_W:]
            for s in (1, 2, 4):
                m = rin < 8 - s
                sr = jnp.where(m, pltpu.roll(lr, 8 - s, 0), 0.0)
                si = jnp.where(m, pltpu.roll(li, 8 - s, 0), 0.0)
                lr, li = _cmul_add(lr, li, *pw[s - 1], sr, si)
            lr, li = _cmul_add(lr, li, pwr, pwi, cr, ci)
            lam_ref[sl, :_S5_W] = lr
            lam_ref[sl, _S5_W:] = li
            nr = jnp.where(rin == 7, cr, pltpu.roll(lr, 7, 0))
            ni = jnp.where(rin == 7, ci, pltpu.roll(li, 7, 0))
            xr, xi = x_ref[sl, :_S5_W], x_ref[sl, _S5_W:]
            acc_r = acc_r + (xr * nr + xi * ni)
            acc_i = acc_i + (xr * ni - xi * nr)
            cr, ci = lr[0:1, :], li[0:1, :]
        carry_ref[0:1, :] = cr
        carry_ref[1:2, :] = ci
        dar_ref[...] += jnp.sum(acc_r, axis=0, keepdims=True)
        dai_ref[...] += jnp.sum(acc_i, axis=0, keepdims=True)
        lam = lam_ref[...]
        du_ref[...] = _dot_raw(lam, wb_ref[...], "nt")
        dwb_ref[...] += _dot_raw(u_ref[...], lam, "tn")
        dwc_ref[...] += _dot_raw(x_ref[...], dy_ref[...], "tn")

    nblk = S5_NG // S5_BLK
    blk = pl.BlockSpec((bt, 2 * _S5_W), lambda g, t: (nb - 1 - t, g))
    col = pl.BlockSpec((bt, LANE), lambda g, t: (nb - 1 - t, g))
    a_s = pl.BlockSpec((None, 1, _S5_W), lambda g, t: (g, 0, 0))
    wb_s = pl.BlockSpec((None, LANE, 2 * _S5_W), lambda g, t: (g, 0, 0))
    wc_s = pl.BlockSpec((None, 2 * _S5_W, LANE), lambda g, t: (g, 0, 0))
    return pl.pallas_call(
        body, name=name, grid=(nblk, nb), in_specs=[col, blk, col, wb_s, a_s, a_s, wc_s],
        out_specs=[col, wb_s, wc_s, a_s, a_s],
        out_shape=[jax.ShapeDtypeStruct((T, nblk * LANE), F32), jax.ShapeDtypeStruct((nblk, LANE, 2 * _S5_W), F32),
                   jax.ShapeDtypeStruct((nblk, 2 * _S5_W, LANE), F32), jax.ShapeDtypeStruct((nblk, 1, _S5_W), F32),
                   jax.ShapeDtypeStruct((nblk, 1, _S5_W), F32)],
        scratch_shapes=[pltpu.VMEM((bt, 2 * _S5_W), F32), pltpu.VMEM((bt, 2 * _S5_W), F32), pltpu.VMEM((8, _S5_W), F32)],
        compiler_params=_cparams(2),
    )(dy, x, u, wb, a_re, a_im, wc)


def _norm_bf16(h, g, name):
    return rowwise(f_rmsnorm, [g], [h], [(D, BF16)], bt=512, name=name)[0]


def _norm_bwd(h, g, cts, name):
    n = len(cts) - 1

    def f(p, r):
        y = _rms(r[0], p[0])
        return (y,) * n + (r[0],)

    (dg,), (dh,) = rowwise_vjp(f, [g], [h], cts, [F32], bt=256, name=name)
    return dh, dg


def ffn_fwd(h, g, w_gu, w_down, tag):
    hn = _norm_bf16(h, g, f"{tag}_norm")
    a, gu = ffn_up(hn, w_gu, name=f"{tag}_up")
    h2 = matmul(a, w_down[None], "nn", add=h, name=f"{tag}_down")
    return h2, (h, hn, gu, a)


def ffn_bwd(d, saved, g, w_gu, w_down, tag):
    h, hn, gu, a = saved
    dgu = ffn_dact(d, w_down, gu, name=f"{tag}_dact")
    dwd = matmul(a, d, "tn", name=f"{tag}_dwd")[0]
    dwgu = matmul(hn, dgu, "tn", name=f"{tag}_dwgu")[0]
    dh, dg = matmul_nt_norm_bwd(dgu, w_gu, h, g, d, name=f"{tag}_dhn")
    return dh, dg, dwgu, dwd


_GLA_NC = 4
_SSD_NU = 4


def gla_fwd(h, gm, w_in, w_a2, b_a, ng, w_out, tag):
    hn = _norm_bf16(h, gm, f"{tag}_norm")
    proj = matmul(hn, w_in[None], "nn", name=f"{tag}_in")
    alow = (proj, LANE, 2 * (GLA_QK + GLA_VD) // LANE)
    la = rowwise(f_gla_gate_in_fwd, [w_a2, b_a], [alow], [(GLA_QK, F32)], bt=512, name=f"{tag}_gate")[0]
    og, ss = gla_scan_fwd(proj, la, ng, nc=_GLA_NC, name=f"{tag}_scan")
    h2 = matmul(og, w_out[None], "nn", add=h, name=f"{tag}_proj")
    return h2, (h, hn, proj, la, ss, og)


def gla_bwd(d, saved, gm, w_in, w_a2, b_a, ng, w_out, tag):
    h, hn, proj, la, ss, og = saved
    dog = matmul(d, w_out[None], "nt", name=f"{tag}_dog")
    dwout = matmul(og, d, "tn", name=f"{tag}_dwout")[0]
    dq, dk, dv, dla, dr, dng = gla_scan_bwd(proj, la, ng, ss, dog, nc=_GLA_NC, name=f"{tag}_dscan")
    alow = (proj, LANE, 2 * (GLA_QK + GLA_VD) // LANE)
    (dwa2, dba), (dalow,) = rowwise_vjp(f_gla_gate_in, [w_a2, b_a], [alow], [dla], [BF16], bt=512, name=f"{tag}_dgate")
    dproj = jnp.concatenate([dq, dk, dv, dr, dalow], axis=1)
    dwin = matmul(hn, dproj, "tn", name=f"{tag}_dwin")[0]
    dh, dgm = matmul_nt_norm_bwd(dproj, w_in, h, gm, d, name=f"{tag}_dhn")
    return dh, dgm, dwin, dwa2[:GLA_RANK], dba, dng, dwout


def ssd_fwd(h, gm, w_in, conv_w, conv_b, dtb, alog, dsk, ng, w_out, tag):
    hn = _norm_bf16(h, gm, f"{tag}_norm")
    proj = matmul(hn, w_in[None], "nn", name=f"{tag}_in")
    xbc = ssd_conv_fwd(proj, conv_w, conv_b, name=f"{tag}_conv")
    yg, hs = ssd_scan_fwd(xbc, proj, dtb, alog, dsk, ng, nu=_SSD_NU, name=f"{tag}_scan")
    h2 = matmul(yg, w_out[None], "nn", add=h, name=f"{tag}_proj")
    return h2, (h, hn, proj, xbc, hs, yg)


def ssd_bwd(d, saved, gm, w_in, conv_w, conv_b, dtb, alog, dsk, ng, w_out, tag):
    h, hn, proj, xbc, hs, yg = saved
    dyg = matmul(d, w_out[None], "nt", name=f"{tag}_dyg")
    dwout = matmul(yg, d, "tn", name=f"{tag}_dwout")[0]
    dxs, dbm, dcm, ddt, ddtb, dal, ddsk, dz, dng = ssd_scan_bwd(xbc, proj, dtb, alog, dsk, ng, hs, dyg, nu=_SSD_NU,
                                                               name=f"{tag}_dscan")
    parts = [ssd_conv_bwd(proj, conv_w, conv_b, dout, col0, name=f"{tag}_dconv{k}")
             for k, (dout, col0) in enumerate(((dxs, 0), (dbm, SSD_DI), (dcm, SSD_DI + SSD_GN)))]
    dcw = jnp.concatenate([p[1] for p in parts], axis=1)
    dcb = jnp.concatenate([p[2] for p in parts], axis=1)
    dproj = jnp.concatenate([dz] + [p[0] for p in parts] + [ddt.astype(BF16)], axis=1)
    dwin = matmul(hn, dproj, "tn", name=f"{tag}_dwin")[0]
    dh, dgm = matmul_nt_norm_bwd(dproj, w_in, h, gm, d, name=f"{tag}_dhn")
    return (dh, dgm, dwin, dcw, dcb, ddtb[:, :SSD_H], dal[:, :SSD_H], ddsk[:, :SSD_H], dng, dwout)


_S5_NB = S5_NG // S5_BLK


def _s5_param_args(log_dt, a_re, a_im, b_re, b_im, c_im):
    n = S5_NG * S5_GS
    tr = lambda b: jnp.transpose(b, (0, 2, 1)).reshape(n, S5_P)
    return [log_dt.reshape(S5_NG, 1), a_re, a_im, tr(b_re), tr(b_im), c_im.reshape(n, S5_P)]


def _s5_blockdiag(t):
    nb, gl, a, b = t.shape
    eye = jnp.eye(gl, dtype=t.dtype)
    return (t[:, :, :, None, :] * eye[None, :, None, :, None]).reshape(nb, gl * a, gl * b)


def _s5_diag(t, a, b):
    nb = t.shape[0]
    gl = t.shape[1] // a
    eye = jnp.eye(gl, dtype=t.dtype)
    return jnp.sum(t.reshape(nb, gl, a, gl, b) * eye[None, :, None, :, None], axis=3)


def _s5_weights(bbr, bbi, c_re, cneg):
    sh = (_S5_NB, S5_BLK, S5_GS, S5_P)
    wb = jnp.concatenate([_s5_blockdiag(bbr.reshape(sh)), _s5_blockdiag(bbi.reshape(sh))], axis=2)
    tr = lambda cc: jnp.transpose(cc.reshape(sh), (0, 1, 3, 2))
    wc = jnp.concatenate([_s5_blockdiag(tr(c_re)), _s5_blockdiag(tr(cneg))], axis=1)
    return wb, wc


def s5_fwd(h, gm, prm, dsk, w_glu, tag):
    log_dt, a_re, a_im, b_re, b_im, c_re, c_im = prm
    hn = rowwise(f_rmsnorm, [gm], [h], [(D, F32)], bt=512, name=f"{tag}_norm")[0]
    pargs = _s5_param_args(log_dt, a_re, a_im, b_re, b_im, c_im)
    abr, abi, bbr, bbi, cneg = s5_param_fwd(pargs, name=f"{tag}_param")
    wb, wc = _s5_weights(bbr, bbi, c_re.reshape(S5_NG * S5_GS, S5_P), cneg)
    ar, ai = abr.reshape(_S5_NB, 1, _S5_W), abi.reshape(_S5_NB, 1, _S5_W)
    wb, wc = wb.astype(BF16), wc.astype(BF16)
    x, ycp = s5_scan_fwd(hn, wb, ar, ai, wc, name=f"{tag}_scan")
    yg = rowwise(f_s5_act, [dsk], [ycp, hn], [(D, BF16)], bt=512, name=f"{tag}_act")[0]
    vg = matmul(yg, w_glu[None], "nn", name=f"{tag}_glu")
    h2 = rowwise(f_glu_res, [], [vg, h], [(D, F32)], bt=512, name=f"{tag}_out")[0]
    return h2, (h, hn, pargs, wb, wc, ar, ai, x, ycp, yg, vg)


def s5_bwd(d, saved, gm, dsk, w_glu, tag):
    h, hn, pargs, wb, wc, ar, ai, x, ycp, yg, vg = saved
    _, (dvg,) = rowwise_vjp(f_glu, [], [vg], [d], [BF16], bt=256, name=f"{tag}_dout")
    dwglu = matmul(yg, dvg, "tn", name=f"{tag}_dwglu")[0]
    dyg = matmul(dvg, w_glu[None], "nt", name=f"{tag}_dyg")
    (ddsk,), (dycp, dhn1) = rowwise_vjp(f_s5_act, [dsk], [ycp, hn], [dyg], [F32, F32], bt=256, name=f"{tag}_dact")
    dhn2, dwb, dwc, dar, dai = s5_scan_bwd(dycp, x, hn, wb, ar, ai, wc, name=f"{tag}_dscan")
    dh, dgm = _norm_bwd(h, gm, [dhn1, dhn2, d], f"{tag}_dnorm")
    n = S5_NG * S5_GS
    half = S5_BLK * S5_P
    d_bbr = _s5_diag(dwb[:, :, :half], S5_GS, S5_P).reshape(n, S5_P)
    d_bbi = _s5_diag(dwb[:, :, half:], S5_GS, S5_P).reshape(n, S5_P)
    from_c = lambda t: jnp.transpose(_s5_diag(t, S5_P, S5_GS), (0, 1, 3, 2)).reshape(n, S5_P)
    d_cre = from_c(dwc[:, :half, :])
    d_cneg = from_c(dwc[:, half:, :])
    cts = [dar.reshape(S5_NG, S5_P), dai.reshape(S5_NG, S5_P), d_bbr, d_bbi, d_cneg]
    dlog, dare, daim, dbre_t, dbim_t, dcim = s5_param_bwd(pargs, cts, name=f"{tag}_dparam")
    untr = lambda t: jnp.transpose(t.reshape(S5_NG, S5_GS, S5_P), (0, 2, 1))
    grads = (dlog.reshape(S5_NG), dare, daim, untr(dbre_t), untr(dbim_t),
             d_cre.reshape(S5_NG, S5_GS, S5_P), dcim.reshape(S5_NG, S5_GS, S5_P))
    return dh, dgm, grads, ddsk, dwglu


def _pad_last(w, n):
    return jnp.pad(w, [(0, 0)] * (w.ndim - 1) + [(0, n - w.shape[-1])])


_BIG = ("gla_w_in", "gla_w_out", "ssd_w_in", "ssd_w_out", "s5_w_glu", "ffn_w_gu", "ffn_w_down")


def interleave_gu(w):
    q = w.shape[-1] // 4
    return jnp.concatenate([w[..., :q], w[..., 2 * q:3 * q], w[..., q:2 * q], w[..., 3 * q:]], axis=-1)


def local_step(x, target, W, later_weights=None, later_grads=None, ffn0_grads=None, ffn0_weights=None):
    f32 = lambda a: a.astype(F32)
    row = lambda a: f32(a).reshape(1, -1)

    def layer_args(i):
        m, j = i % 3, i // 3
        gm = row(W["norm_mix_g"][i])
        if m == 0:
            args = (gm, W["gla_w_in"][j], jnp.pad(f32(W["gla_w_a2"][j]), ((0, LANE - GLA_RANK), (0, 0))),
                    row(W["gla_b_a"][j]), row(W["gla_norm_g"][j]), W["gla_w_out"][j])
        elif m == 1:
            pl_ = lambda a: _pad_last(row(a), LANE)
            args = (gm, W["ssd_w_in"][j], f32(W["ssd_conv_w"][j]),
                    row(W["ssd_conv_b"][j]), pl_(W["ssd_dt_bias"][j]), pl_(W["ssd_a_log"][j]), pl_(W["ssd_d"][j]),
                    row(W["ssd_norm_g"][j]), W["ssd_w_out"][j])
        else:
            prm = tuple(f32(W[k][j]) for k in ("s5_log_dt", "s5_a_re", "s5_a_im", "s5_b_re", "s5_b_im", "s5_c_re", "s5_c_im"))
            args = (gm, prm, row(W["s5_d"][j]), W["s5_w_glu"][j])
        return m, j, args

    h = x
    saved, mixers, ffns = [], [], []
    for i in range(DEPTH):
        mixer = layer_args(i)
        mixers.append(mixer)
        m, j, args = mixer
        tag = f"l{i}_{('gla', 'ssd', 's5')[m]}"
        h, sm = (gla_fwd, ssd_fwd, s5_fwd)[m](h, *args, tag)
        if i == 0 and ffn0_weights is not None:
            W = {**W, **ffn0_weights(h)}
        ffn = (row(W["norm_ffn_g"][i]), W["ffn_w_gu"][i], W["ffn_w_down"][i])
        ffns.append(ffn)
        h, sf = ffn_fwd(h, *ffn, f"l{i}_ffn")
        saved.append((sm, sf))
        if i == 0 and later_weights is not None:
            W = {**W, **later_weights(h)}
    loss, dfg, d = loss_head(h, row(W["final_norm_g"]), target, name="loss_head")

    G = {k: [None] * len(v) for k, v in W.items() if k != "final_norm_g"}
    G["final_norm_g"] = dfg.reshape(D)
    for i in reversed(range(DEPTH)):
        m, j, args = mixers[i]
        sm, sf = saved[i]
        if i == 0 and later_grads is not None:
            zero = later_grads(G)
            ffns[0] = (ffns[0][0], ffns[0][1], ffns[0][2] + zero.astype(ffns[0][2].dtype))
        d, dg, dwgu, dwd = ffn_bwd(d, sf, *ffns[i], f"l{i}_ffn")
        G["norm_ffn_g"][i], G["ffn_w_gu"][i], G["ffn_w_down"][i] = dg.reshape(D), dwgu, dwd
        if i == 0 and ffn0_grads is not None:
            zero = ffn0_grads(G)
            args = args[:-1] + (args[-1] + zero.astype(args[-1].dtype),)
        tag = f"l{i}_{('gla', 'ssd', 's5')[m]}"
        if m == 0:
            d, dgm, dwin, dwa2, dba, dng, dwout = gla_bwd(d, sm, *args, tag)
            G["gla_w_in"][j], G["gla_w_a2"][j], G["gla_b_a"][j] = dwin, dwa2, dba.reshape(-1)
            G["gla_norm_g"][j], G["gla_w_out"][j] = dng.reshape(-1), dwout
        elif m == 1:
            d, dgm, dwin, dcw, dcb, ddtb, dal, ddsk, dng, dwout = ssd_bwd(d, sm, *args, tag)
            G["ssd_w_in"][j], G["ssd_conv_w"][j], G["ssd_conv_b"][j] = dwin, dcw, dcb.reshape(-1)
            G["ssd_dt_bias"][j], G["ssd_a_log"][j], G["ssd_d"][j] = ddtb.reshape(-1), dal.reshape(-1), ddsk.reshape(-1)
            G["ssd_norm_g"][j], G["ssd_w_out"][j] = dng.reshape(-1), dwout
        else:
            d, dgm, pg, ddsk, dwglu = s5_bwd(d, sm, args[0], args[2], args[3], tag)
            for k, v in zip(("s5_log_dt", "s5_a_re", "s5_a_im", "s5_b_re", "s5_b_im", "s5_c_re", "s5_c_im"), pg):
                G[k][j] = v
            G["s5_d"][j], G["s5_w_glu"][j] = ddsk.reshape(-1), dwglu
        G["norm_mix_g"][i] = dgm.reshape(D)
    grads = {k: (v if k == "final_norm_g" or k in _BIG else jnp.stack(v)) for k, v in G.items()}
    return loss, d, grads


_MESH = pl.DeviceIdType.MESH
_ANY = pl.BlockSpec(memory_space=pl.ANY)
_DMA = pltpu.SemaphoreType.DMA
_ROWS_ALIGN = 1024


def _place():
    return lax.axis_index("x"), lax.axis_index("y"), lax.axis_index("c")


def _other_chips(x, y):
    return [(1 - x, y), (x, 1 - y), (1 - x, 1 - y)]


def _remote(src, dst, send_sems, recv_sems, k, to):
    return pltpu.make_async_remote_copy(src_ref=src, dst_ref=dst, send_sem=send_sems.at[k], recv_sem=recv_sems.at[k],
                                        device_id=to, device_id_type=_MESH)


def gather_shards(loc, *, name):
    def body(in_ref, out_ref, send_sems, recv_sems, local_sem):
        x, y, c = _place()
        me, sibling = (x, y, c), (x, y, 1 - c)
        chips = _other_chips(x, y)

        def half(px, py, hc):
            return out_ref.at[2 * px + py, hc]

        mine = pltpu.make_async_copy(in_ref, out_ref.at[2 * x + y], local_sem)
        mine.start()
        first = [_remote(in_ref.at[c], half(x, y, c), send_sems, recv_sems, j, (*chip, c)) for j, chip in enumerate(chips)]
        for cp in first:
            cp.start()
        passed = [_remote(half(*chip, c), half(*chip, c), send_sems, recv_sems, 3 + j, sibling) for j, chip in enumerate(chips)]
        for j, chip in enumerate(chips):
            _remote(in_ref.at[c], half(*chip, c), send_sems, recv_sems, j, me).wait_recv()
            passed[j].start()
        for j, chip in enumerate(chips):
            _remote(in_ref.at[c], half(*chip, 1 - c), send_sems, recv_sems, 3 + j, me).wait_recv()
        for cp in first + passed:
            cp.wait_send()
        mine.wait()

    return pl.pallas_call(
        body, name=name, in_specs=[_ANY], out_specs=_ANY,
        out_shape=jax.ShapeDtypeStruct((4,) + loc.shape, loc.dtype),
        scratch_shapes=[_DMA((6,)), _DMA((6,)), _DMA(())],
    )(loc)


def _pos(px, py, perm):
    return 2 * py + px if perm else 2 * px + py


def _part(ref, kind, p, loc):
    if kind == "lead":
        return ref.at[p]
    return ref.at[:, pl.ds(pl.multiple_of(p * loc, LANE), loc)]


def _rows(ref, h, hr):
    return ref.at[pl.ds(h * hr, hr)]


def _rows_block(hr, width):
    return max(b for b in range(16, hr + 1, 16) if hr % b == 0 and (b * width <= (1 << 19) or b == 16))


def gather_big(locs, kinds, *, name):
    n = len(locs)

    def body(*refs):
        ins, outs = refs[:n], refs[n:2 * n]
        send_sems, recv_sems = refs[2 * n + 1:]
        refs[2 * n][...] = jnp.zeros_like(refs[2 * n])
        x, y, c = _place()
        me, sibling = (x, y, c), (x, y, 1 - c)
        chips = _other_chips(x, y)

        def half(i, px, py, h):
            (kind, perm), (rows, loc) = kinds[i], locs[i].shape
            return _rows(_part(outs[i], kind, _pos(px, py, perm), loc), h, rows // 2)

        sends = []
        for i in range(n):
            (kind, perm), (rows, loc) = kinds[i], locs[i].shape
            own = _part(outs[i], kind, _pos(x, y, perm), loc)
            sends.append(_remote(ins[i], own, send_sems, recv_sems, 6 * n + i, sibling))
            sends[-1].start()
            for j, chip in enumerate(chips):
                sends.append(_remote(_rows(ins[i], c, rows // 2), half(i, x, y, c), send_sems, recv_sems, 6 * i + j, (*chip, c)))
                sends[-1].start()
        for i in range(n):
            hr = locs[i].shape[0] // 2
            for j, chip in enumerate(chips):
                _remote(_rows(ins[i], c, hr), half(i, *chip, c), send_sems, recv_sems, 6 * i + j, me).wait_recv()
                sends.append(_remote(half(i, *chip, c), half(i, *chip, c), send_sems, recv_sems, 6 * i + 3 + j, sibling))
                sends[-1].start()
        for i in range(n):
            (kind, perm), (rows, loc) = kinds[i], locs[i].shape
            for j, chip in enumerate(chips):
                _remote(_rows(ins[i], c, rows // 2), half(i, *chip, 1 - c), send_sems, recv_sems, 6 * i + 3 + j, me).wait_recv()
            _remote(ins[i], _part(outs[i], kind, _pos(x, y, perm), loc), send_sems, recv_sems, 6 * n + i, me).wait_recv()
        for cp in sends:
            cp.wait_send()

    def out_shape(a, kind):
        rows, loc = a.shape
        return jax.ShapeDtypeStruct((4, rows, loc) if kind == "lead" else (rows, 4 * loc), a.dtype)

    outs = pl.pallas_call(
        body, name=name, in_specs=[_ANY] * n, out_specs=[_ANY] * n + [pl.BlockSpec(memory_space=pltpu.VMEM)],
        out_shape=[out_shape(a, k[0]) for a, k in zip(locs, kinds)] + [jax.ShapeDtypeStruct((8, LANE), F32)],
        scratch_shapes=[_DMA((7 * n,)), _DMA((7 * n,))],
    )(*locs)
    return list(outs[:n]), outs[n][0, 0]


_HBM = pl.BlockSpec(memory_space=pltpu.HBM)
_SEM = pl.BlockSpec(memory_space=pltpu.SEMAPHORE)
_EFFECT = pltpu.SideEffectType.DATAFLOW_SIDE_EFFECTING


def _in_hbm(a):
    return pltpu.with_memory_space_constraint(a, pltpu.HBM)


def _gather_ici_copies(ins, lands, kinds, shapes, send_sems, recv_sems):
    x, y, c = _place()
    sends, arrivals = [], []
    for i, ((kind, perm), (rows, loc)) in enumerate(zip(kinds, shapes)):
        hr = rows // 2
        mine = _part(lands[i], kind, _pos(x, y, perm), loc)
        sends.append(_remote(ins[i], mine, send_sems, recv_sems, 4 * i + 3, (x, y, 1 - c)))
        arrivals.append(_remote(ins[i], mine, send_sems, recv_sems, 4 * i + 3, (x, y, c)))
        for j, (px, py) in enumerate(_other_chips(x, y)):
            sends.append(_remote(_rows(ins[i], c, hr), _rows(mine, c, hr), send_sems, recv_sems, 4 * i + j, (px, py, c)))
            theirs = _rows(_part(lands[i], kind, _pos(px, py, perm), loc), c, hr)
            arrivals.append(_remote(_rows(ins[i], c, hr), theirs, send_sems, recv_sems, 4 * i + j, (x, y, c)))
    return sends, arrivals


def gather_start(locs, kinds, *, name):
    n = len(locs)
    shapes = [a.shape for a in locs]

    def land_shape(a, kind):
        rows, loc = a.shape
        return (4, rows, loc) if kind == "lead" else (rows, 4 * loc)

    def body(*refs):
        sends, _ = _gather_ici_copies(refs[:n], refs[n:2 * n], kinds, shapes, refs[2 * n], refs[2 * n + 1])
        for cp in sends:
            cp.start()
        refs[-1][...] = jnp.zeros_like(refs[-1])

    lands = [lax.empty(land_shape(a, k[0]), a.dtype) for a, k in zip(locs, kinds)]
    outs = pl.pallas_call(
        body, name=name, in_specs=[_HBM] * (2 * n), out_specs=[_SEM, _SEM] + [_HBM] * (2 * n) + [pl.BlockSpec(memory_space=pltpu.VMEM)],
        out_shape=[_DMA((4 * n,)), _DMA((4 * n,))] + [pltpu.HBM(a.shape, a.dtype) for a in locs]
        + [pltpu.HBM(l.shape, l.dtype) for l in lands] + [jax.ShapeDtypeStruct((8, LANE), F32)],
        input_output_aliases={i: 2 + i for i in range(2 * n)},
        compiler_params=pltpu.CompilerParams(has_side_effects=_EFFECT),
    )(*[_in_hbm(a) for a in locs], *[_in_hbm(l) for l in lands])
    return outs[0], outs[1], list(outs[2:2 + n]), list(outs[2 + n:2 + 2 * n]), outs[-1][0, 0]


def gather_wait(send_sems, recv_sems, locs, lands, kinds, after, *, name):
    n = len(locs)
    shapes = [a.shape for a in locs]

    def body(*refs):
        sends, arrivals = _gather_ici_copies(refs[:n], refs[n:2 * n], kinds, shapes, refs[2 * n], refs[2 * n + 1])
        for cp in sends:
            cp.wait_send()
        for cp in arrivals:
            cp.wait_recv()

    outs = pl.pallas_call(
        body, name=name, in_specs=[_HBM] * (2 * n) + [_SEM, _SEM, _ANY], out_specs=[_HBM] * (2 * n),
        out_shape=[pltpu.HBM(a.shape, a.dtype) for a in locs] + [pltpu.HBM(l.shape, l.dtype) for l in lands],
        input_output_aliases={i: i for i in range(2 * n)},
        compiler_params=pltpu.CompilerParams(has_side_effects=_EFFECT),
    )(*locs, *lands, send_sems, recv_sems, after)
    return list(outs[n:])


def gather_finish(lands, kinds, shapes, *, name):
    n = len(lands)

    def body(*refs):
        bufs = refs[n:2 * n]
        send_sems, recv_sems = refs[2 * n:]
        x, y, c = _place()
        sends = []
        for i, ((kind, perm), (rows, loc)) in enumerate(zip(kinds, shapes)):
            for j, (px, py) in enumerate(_other_chips(x, y)):
                part = _part(bufs[i], kind, _pos(px, py, perm), loc)
                sends.append(_remote(_rows(part, c, rows // 2), _rows(part, c, rows // 2), send_sems, recv_sems, 3 * i + j, (x, y, 1 - c)))
                sends[-1].start()
        for i, ((kind, perm), (rows, loc)) in enumerate(zip(kinds, shapes)):
            for j, (px, py) in enumerate(_other_chips(x, y)):
                part = _part(bufs[i], kind, _pos(px, py, perm), loc)
                _remote(_rows(part, c, rows // 2), _rows(part, 1 - c, rows // 2), send_sems, recv_sems, 3 * i + j, (x, y, c)).wait_recv()
        for cp in sends:
            cp.wait_send()

    return list(pl.pallas_call(
        body, name=name, in_specs=[_ANY] * n, out_specs=[_ANY] * n,
        out_shape=[jax.ShapeDtypeStruct(l.shape, l.dtype) for l in lands],
        input_output_aliases={i: i for i in range(n)}, scratch_shapes=[_DMA((3 * n,)), _DMA((3 * n,))],
    )(*lands))


def _scatter_copies(qs, lands, kinds, locs, send_sems, recv_sems):
    x, y, c = _place()
    sends, arrivals = [], []
    for i, (kind, perm) in enumerate(kinds):
        for j, (px, py) in enumerate(_other_chips(x, y)):
            src = _part(qs[i], kind, _pos(px, py, perm), locs[i])
            sends.append(_remote(src, lands[i].at[j], send_sems, recv_sems, 3 * i + j, (px, py, c)))
            arrivals.append(_remote(src, lands[i].at[j], send_sems, recv_sems, 3 * i + j, (x, y, c)))
    return sends, arrivals


def _scatter_land(q, kind, loc):
    return (3, q.shape[1] if kind == "lead" else q.shape[0], loc)


def scatter_start(qs, kinds, locs, *, name):
    n = len(qs)

    def body(*refs):
        sends, _ = _scatter_copies(refs[:n], refs[n:2 * n], kinds, locs, refs[2 * n], refs[2 * n + 1])
        for cp in sends:
            cp.start()
        refs[-1][...] = jnp.zeros_like(refs[-1])

    lands = [lax.empty(_scatter_land(q, k[0], l), q.dtype) for q, k, l in zip(qs, kinds, locs)]
    outs = pl.pallas_call(
        body, name=name, in_specs=[_HBM] * (2 * n), out_specs=[_SEM, _SEM] + [_HBM] * (2 * n) + [pl.BlockSpec(memory_space=pltpu.VMEM)],
        out_shape=[_DMA((3 * n,)), _DMA((3 * n,))] + [pltpu.HBM(q.shape, q.dtype) for q in qs]
        + [pltpu.HBM(l.shape, l.dtype) for l in lands] + [jax.ShapeDtypeStruct((8, LANE), F32)],
        input_output_aliases={i: 2 + i for i in range(2 * n)},
        compiler_params=pltpu.CompilerParams(has_side_effects=_EFFECT),
    )(*[_in_hbm(q) for q in qs], *[_in_hbm(l) for l in lands])
    return outs[0], outs[1], list(outs[2:2 + n]), list(outs[2 + n:2 + 2 * n]), outs[-1][0, 0]


def scatter_wait(send_sems, recv_sems, qs, lands, kinds, locs, after, *, name):
    n = len(qs)

    def body(*refs):
        sends, arrivals = _scatter_copies(refs[:n], refs[n:2 * n], kinds, locs, refs[2 * n], refs[2 * n + 1])
        for cp in sends:
            cp.wait_send()
        for cp in arrivals:
            cp.wait_recv()

    outs = pl.pallas_call(
        body, name=name, in_specs=[_HBM] * (2 * n) + [_SEM, _SEM, _ANY], out_specs=[_HBM] * (2 * n),
        out_shape=[pltpu.HBM(q.shape, q.dtype) for q in qs] + [pltpu.HBM(l.shape, l.dtype) for l in lands],
        input_output_aliases={i: i for i in range(2 * n)},
        compiler_params=pltpu.CompilerParams(has_side_effects=_EFFECT),
    )(*qs, *lands, send_sems, recv_sems, after)
    return list(outs[:n]), list(outs[n:])


def _pair_swap_copies(ins, lands, kinds, send_sems, recv_sems):
    x, y, c = _place()
    sends, arrivals = [], []
    for i, (kind, _) in enumerate(kinds):
        if kind == "lead":
            hr = ins[i].shape[1] // 2
            src = ins[i].at[:, pl.ds((1 - c) * hr, hr)]
        else:
            src = _rows(ins[i], 1 - c, ins[i].shape[0] // 2)
        sends.append(_remote(src, lands[i], send_sems, recv_sems, i, (x, y, 1 - c)))
        arrivals.append(_remote(src, lands[i], send_sems, recv_sems, i, (x, y, c)))
    return sends, arrivals


def _pair_swap_land(a, kind):
    s = a.shape
    return (4, s[1] // 2, s[2]) if kind == "lead" else (s[0] // 2, s[1])


def pair_swap_start(ps, kinds, *, name):
    n = len(ps)

    def body(*refs):
        sends, _ = _pair_swap_copies(refs[:n], refs[n:2 * n], kinds, refs[2 * n], refs[2 * n + 1])
        for cp in sends:
            cp.start()
        refs[-1][...] = jnp.zeros_like(refs[-1])

    lands = [lax.empty(_pair_swap_land(p, k[0]), p.dtype) for p, k in zip(ps, kinds)]
    outs = pl.pallas_call(
        body, name=name, in_specs=[_HBM] * (2 * n), out_specs=[_SEM, _SEM] + [_HBM] * (2 * n) + [pl.BlockSpec(memory_space=pltpu.VMEM)],
        out_shape=[_DMA((n,)), _DMA((n,))] + [pltpu.HBM(p.shape, p.dtype) for p in ps]
        + [pltpu.HBM(l.shape, l.dtype) for l in lands] + [jax.ShapeDtypeStruct((8, LANE), F32)],
        input_output_aliases={i: 2 + i for i in range(2 * n)},
        compiler_params=pltpu.CompilerParams(has_side_effects=_EFFECT),
    )(*[_in_hbm(p) for p in ps], *[_in_hbm(l) for l in lands])
    return outs[0], outs[1], list(outs[2:2 + n]), list(outs[2 + n:2 + 2 * n]), outs[-1][0, 0]


def pair_swap_wait(send_sems, recv_sems, ps, lands, kinds, after, *, name):
    n = len(ps)

    def body(*refs):
        sends, arrivals = _pair_swap_copies(refs[:n], refs[n:2 * n], kinds, refs[2 * n], refs[2 * n + 1])
        for cp in sends:
            cp.wait_send()
        for cp in arrivals:
            cp.wait_recv()

    outs = pl.pallas_call(
        body, name=name, in_specs=[_HBM] * (2 * n) + [_SEM, _SEM, _ANY], out_specs=[_HBM] * (2 * n),
        out_shape=[pltpu.HBM(p.shape, p.dtype) for p in ps] + [pltpu.HBM(l.shape, l.dtype) for l in lands],
        input_output_aliases={i: i for i in range(2 * n)},
        compiler_params=pltpu.CompilerParams(has_side_effects=_EFFECT),
    )(*ps, *lands, send_sems, recv_sems, after)
    return list(outs[:n]), list(outs[n:])


def pair_swap(ps, kinds, *, name):
    n = len(ps)

    def body(*refs):
        ins, outs = refs[:n], refs[n:2 * n]
        send_sems, recv_sems = refs[2 * n:]
        x, y, c = _place()
        cps = []
        for i in range(n):
            if kinds[i][0] == "lead":
                hr = ps[i].shape[1] // 2
                src = ins[i].at[:, pl.ds((1 - c) * hr, hr)]
            else:
                hr = ps[i].shape[0] // 2
                src = _rows(ins[i], 1 - c, hr)
            cps.append(_remote(src, outs[i], send_sems, recv_sems, i, (x, y, 1 - c)))
            cps[-1].start()
        for cp in cps:
            cp.wait()

    def out_shape(a, kind):
        s = a.shape
        return jax.ShapeDtypeStruct((4, s[1] // 2, s[2]) if kind == "lead" else (s[0] // 2, s[1]), a.dtype)

    return pl.pallas_call(
        body, name=name, in_specs=[_ANY] * n, out_specs=[_ANY] * n,
        out_shape=[out_shape(a, k[0]) for a, k in zip(ps, kinds)], scratch_shapes=[_DMA((n,)), _DMA((n,))],
    )(*ps)


def pair_add(p, got, c_arr, kind, *, name):
    if kind == "lead":
        _, hr, cols = got.shape
        br = _rows_block(hr, cols)
        nb = hr // br
        grid = (4, nb)
        p_spec = pl.BlockSpec((None, br, cols), lambda s, i, cr: (s, cr[0] * nb + i, 0))
        g_spec = pl.BlockSpec((None, br, cols), lambda s, i, cr: (s, i, 0))
    else:
        hr, w = got.shape
        br = _rows_block(hr, w)
        nb = hr // br
        grid = (nb,)
        p_spec = pl.BlockSpec((br, w), lambda i, cr: (cr[0] * nb + i, 0))
        g_spec = pl.BlockSpec((br, w), lambda i, cr: (i, 0))

    def body(c_ref, p_ref, g_ref, o_ref):
        o_ref[...] = (p_ref[...] + g_ref[...]).astype(o_ref.dtype)

    return pl.pallas_call(
        body, name=name, out_shape=jax.ShapeDtypeStruct(got.shape, BF16),
        grid_spec=pltpu.PrefetchScalarGridSpec(num_scalar_prefetch=1, grid=grid, in_specs=[p_spec, g_spec], out_specs=g_spec),
        compiler_params=_cparams(len(grid)),
    )(c_arr, p, got)


def chip_scatter(qs, kinds, locs, *, name):
    n = len(qs)

    def body(*refs):
        ins, outs = refs[:n], refs[n:2 * n]
        send_sems, recv_sems = refs[2 * n:]
        x, y, c = _place()
        cps = []
        for i in range(n):
            kind, perm = kinds[i]
            for j, (px, py) in enumerate(_other_chips(x, y)):
                cps.append(_remote(_part(ins[i], kind, _pos(px, py, perm), locs[i]), outs[i].at[j], send_sems, recv_sems,
                                   3 * i + j, (px, py, c)))
                cps[-1].start()
        for cp in cps:
            cp.wait()

    def out_shape(a, kind, loc):
        hr = a.shape[1] if kind == "lead" else a.shape[0]
        return jax.ShapeDtypeStruct((3, hr, loc), a.dtype)

    return pl.pallas_call(
        body, name=name, in_specs=[_ANY] * n, out_specs=[_ANY] * n,
        out_shape=[out_shape(a, k[0], l) for a, k, l in zip(qs, kinds, locs)],
        scratch_shapes=[_DMA((3 * n,)), _DMA((3 * n,))],
    )(*qs)


def chip_add(q, r, pos_arr, c_arr, kind, loc, *, name):
    _, hr, _ = r.shape
    br = _rows_block(hr, loc)
    nb = hr // br
    if kind == "lead":
        q_spec = pl.BlockSpec((None, br, loc), lambda i, pr, cr: (pr[0], i, 0))
    else:
        q_spec = pl.BlockSpec((br, loc), lambda i, pr, cr: (i, pr[0]))
    r_spec = pl.BlockSpec((3, br, loc), lambda i, pr, cr: (0, i, 0))
    o_spec = pl.BlockSpec((br, loc), lambda i, pr, cr: (cr[0] * nb + i, 0))

    def body(p_ref, c_ref, q_ref, r_ref, o_ref):
        acc = q_ref[...].astype(F32)
        for j in range(3):
            acc = acc + r_ref[j].astype(F32)
        o_ref[...] = acc

    return pl.pallas_call(
        body, name=name, out_shape=jax.ShapeDtypeStruct((2 * hr, loc), F32),
        grid_spec=pltpu.PrefetchScalarGridSpec(num_scalar_prefetch=2, grid=(nb,), in_specs=[q_spec, r_spec], out_specs=o_spec),
        compiler_params=_cparams(1),
    )(pos_arr, c_arr, q, r)


def share_rows(fs, *, name):
    n = len(fs)

    def body(*refs):
        bufs = refs[n:2 * n]
        send_sems, recv_sems = refs[2 * n:]
        x, y, c = _place()
        cps = []
        for i in range(n):
            hr = fs[i].shape[0] // 2
            cps.append(_remote(_rows(bufs[i], c, hr), _rows(bufs[i], c, hr), send_sems, recv_sems, i, (x, y, 1 - c)))
            cps[-1].start()
        for i, cp in enumerate(cps):
            hr = fs[i].shape[0] // 2
            _remote(_rows(bufs[i], c, hr), _rows(bufs[i], 1 - c, hr), send_sems, recv_sems, i, (x, y, c)).wait_recv()
            cp.wait_send()

    return pl.pallas_call(
        body, name=name, in_specs=[_ANY] * n, out_specs=[_ANY] * n,
        out_shape=[jax.ShapeDtypeStruct(f.shape, f.dtype) for f in fs],
        input_output_aliases={i: i for i in range(n)}, scratch_shapes=[_DMA((n,)), _DMA((n,))],
    )(*fs)


def _gather_all_copies(v_ref, land_ref, send_sems, recv_sems):
    x, y, c = _place()
    flip = lambda p, m: 1 - p if m else p
    idx = lambda p: 4 * p[0] + 2 * p[1] + p[2]
    sends, arrivals = [], []
    for k, m in enumerate(range(1, 8)):
        p = (flip(x, m & 4), flip(y, m & 2), flip(c, m & 1))
        sends.append(_remote(v_ref, land_ref.at[idx((x, y, c))], send_sems, recv_sems, k, p))
        arrivals.append(_remote(v_ref, land_ref.at[idx(p)], send_sems, recv_sems, k, (x, y, c)))
    return sends, arrivals


def gather_all_start(v, *, name):
    def body(v_ref, land_ref, send_sems, recv_sems, v_thru, land_thru, token):
        sends, _ = _gather_all_copies(v_ref, land_ref, send_sems, recv_sems)
        for cp in sends:
            cp.start()
        token[...] = jnp.zeros_like(token)

    land = jnp.zeros((8,) + v.shape, v.dtype)
    outs = pl.pallas_call(
        body, name=name, in_specs=[_HBM, _HBM], out_specs=[_SEM, _SEM, _HBM, _HBM, pl.BlockSpec(memory_space=pltpu.VMEM)],
        out_shape=[_DMA((7,)), _DMA((7,)), pltpu.HBM(v.shape, v.dtype), pltpu.HBM(land.shape, land.dtype),
                   jax.ShapeDtypeStruct((8, LANE), F32)],
        input_output_aliases={0: 2, 1: 3}, compiler_params=pltpu.CompilerParams(has_side_effects=_EFFECT),
    )(_in_hbm(v), _in_hbm(land))
    return outs[0], outs[1], outs[2], outs[3], outs[4][0, 0]


def gather_all_wait(send_sems, recv_sems, v, land, after, *, name):
    def body(v_ref, land_ref, send_sems, recv_sems, after_ref, v_dead, got_ref):
        sends, arrivals = _gather_all_copies(v_ref, land_ref, send_sems, recv_sems)
        for cp in sends:
            cp.wait_send()
        for cp in arrivals:
            cp.wait_recv()

    return pl.pallas_call(
        body, name=name, in_specs=[_HBM, _HBM, _SEM, _SEM, _ANY], out_specs=[_HBM, _HBM],
        out_shape=[pltpu.HBM(v.shape, v.dtype), pltpu.HBM(land.shape, land.dtype)],
        input_output_aliases={0: 0, 1: 1}, compiler_params=pltpu.CompilerParams(has_side_effects=_EFFECT),
    )(v, land, send_sems, recv_sems, after)[1]


def sum_slots(land, v, me_arr, *, name):
    n, R, L = land.shape
    br = _pick(R, _ROWS_ALIGN, 8)

    def body(me_ref, land_ref, v_ref, o_ref):
        acc = None
        for i in range(n):
            term = jnp.where(me_ref[0] == i, v_ref[...], land_ref[i])
            acc = term if acc is None else acc + term
        o_ref[...] = acc

    row = pl.BlockSpec((br, L), lambda i, me: (i, 0))
    return pl.pallas_call(
        body, name=name, out_shape=jax.ShapeDtypeStruct((R, L), land.dtype),
        grid_spec=pltpu.PrefetchScalarGridSpec(num_scalar_prefetch=1, grid=(R // br,),
                                               in_specs=[pl.BlockSpec((n, br, L), lambda i, me: (0, i, 0)), row], out_specs=row),
        compiler_params=_cparams(1),
    )(me_arr, land, v)


def adamw(w, g, m, v, *, name):
    shape = w.shape
    size = math.prod(shape)
    last = shape[-1]
    if last % LANE != 0 and size % LANE == 0 and size <= (1 << 20):
        last = LANE
    rows = size // last
    budget = (1 << 18) // last
    br = rows
    if rows > budget:
        br = max(c for c in range(8, budget + 1, 8) if rows % c == 0)
    v2 = lambda a: a.reshape(rows, last)

    def body(w_ref, g_ref, m_ref, v_ref, d_ref, nm_ref, nv_ref):
        gg = g_ref[...]
        nm = ADAM_B1 * m_ref[...] + (1.0 - ADAM_B1) * gg
        nv = ADAM_B2 * v_ref[...] + (1.0 - ADAM_B2) * (gg * gg)
        m_hat = nm / (1.0 - ADAM_B1 ** ADAM_STEP)
        v_hat = nv / (1.0 - ADAM_B2 ** ADAM_STEP)
        d_ref[...] = -ADAM_LR * (m_hat / (jnp.sqrt(v_hat) + ADAM_EPS) + ADAM_WD * w_ref[...])
        nm_ref[...] = nm
        nv_ref[...] = nv

    spec = pl.BlockSpec((br, last), lambda i: (i, 0))
    outs = pl.pallas_call(
        body, name=name, grid=(rows // br,), in_specs=[spec] * 4, out_specs=[spec] * 3,
        out_shape=[jax.ShapeDtypeStruct((rows, last), F32)] * 3, compiler_params=_cparams(1),
    )(v2(w), v2(g), v2(m), v2(v))
    return [o.reshape(shape) for o in outs]


_WEIGHTS = ["norm_mix_g", "norm_ffn_g", "gla_w_in", "gla_w_a2", "gla_b_a", "gla_norm_g", "gla_w_out", "ssd_w_in",
            "ssd_conv_w", "ssd_conv_b", "ssd_dt_bias", "ssd_a_log", "ssd_d", "ssd_norm_g", "ssd_w_out", "s5_log_dt",
            "s5_a_re", "s5_a_im", "s5_b_re", "s5_b_im", "s5_c_re", "s5_c_im", "s5_d", "s5_w_glu", "ffn_w_gu",
            "ffn_w_down", "final_norm_g"]
_SHARD_AXIS = {"gla_w_in": 2, "gla_w_a2": 2, "gla_b_a": 1, "gla_norm_g": 1, "gla_w_out": 1, "ssd_w_in": 2,
               "ssd_conv_w": 2, "ssd_w_out": 1, "s5_d": 1, "s5_w_glu": 2, "ffn_w_gu": 2, "ffn_w_down": 1}
_SMALL_SHARDED = [n for n in _WEIGHTS if n in _SHARD_AXIS and n not in _BIG]
_REPLICATED = [n for n in _WEIGHTS if n not in _SHARD_AXIS]
_BIG_KIND = {"gla_w_in": ("lead", False), "gla_w_out": ("lead", False), "ssd_w_in": ("lead", False),
             "ssd_w_out": ("lead", False), "s5_w_glu": ("cols", False), "ffn_w_gu": ("cols", True),
             "ffn_w_down": ("lead", False)}
_PADDED_IN = {"gla_w_in": GLA_INP, "ssd_w_in": SSD_INP}


def _to_rows(flat, parts=1):
    per = -(-flat.shape[0] // (parts * LANE * _ROWS_ALIGN)) * _ROWS_ALIGN
    flat = jnp.pad(flat, (0, parts * per * LANE - flat.shape[0]))
    return flat.reshape(parts, per, LANE)


def _big_layers(local):
    return [(n, j, local[n][j].reshape(-1, local[n].shape[-1])) for n in _BIG for j in range(local[n].shape[0])]


def _in_layer0(n, j):
    return j == 0 and n in ("gla_w_in", "gla_w_out", "ffn_w_gu", "ffn_w_down")


def _assemble(n, g):
    if n in _PADDED_IN:
        return jnp.concatenate([g[s] for s in range(4)] + [jnp.zeros((g.shape[1], _PADDED_IN[n] - 4 * g.shape[2]), BF16)], axis=1)
    if _BIG_KIND[n][0] == "lead":
        return g.reshape(4 * g.shape[1], g.shape[2])
    return g


def _is_gla0(n, j):
    return j == 0 and n in ("gla_w_in", "gla_w_out")


def _gather_first(local):
    layers = _big_layers(local)
    first = [l for l in layers if _is_gla0(l[0], l[1])]
    full = {n: [None] * local[n].shape[0] for n in _BIG}
    got, done = gather_big([w.astype(BF16) for _, _, w in first], [_BIG_KIND[n] for n, _, _ in first], name="gather_weights_first")
    for (n, j, _), g in zip(first, got):
        full[n][j] = _assemble(n, g)
    flat = jnp.concatenate([local[n].astype(F32).reshape(-1) for n in _SMALL_SHARDED])
    got = gather_shards(_to_rows(flat, 2), name="gather_small_weights").reshape(4, -1)
    off = 0
    for n in _SMALL_SHARDED:
        bs = local[n].shape
        sz = math.prod(bs)
        seg = got[:, off:off + sz].reshape((4,) + bs)
        off += sz
        ax = _SHARD_AXIS[n]
        full[n] = jnp.moveaxis(seg, 0, ax).reshape(bs[:ax] + (4 * bs[ax],) + bs[ax + 1:])
    pending = {}
    for tag, want in (("ffn0", _is_ffn0), ("later", lambda n, j: not _in_layer0(n, j))):
        group = [l for l in layers if want(l[0], l[1])]
        kinds = [_BIG_KIND[n] for n, _, _ in group]
        ops = [(w + done if k == 0 else w).astype(BF16) for k, (_, _, w) in enumerate(group)]
        send_sems, recv_sems, locs, lands, done = gather_start(ops, kinds, name=f"gather_weights_start_{tag}")
        pending[tag] = (group, kinds, send_sems, recv_sems, locs, lands)
    return full, pending, done


def _gather_rest(full, pending, after, tag):
    group, kinds, send_sems, recv_sems, locs, lands = pending
    lands = gather_wait(send_sems, recv_sems, locs, lands, kinds, after, name=f"gather_weights_wait_{tag}")
    lands = gather_finish(lands, kinds, [w.shape for _, _, w in group], name=f"gather_weights_finish_{tag}")
    out = {n: list(full[n]) for n in _BIG}
    for (n, j, _), g in zip(group, lands):
        out[n][j] = _assemble(n, g)
    return out


def _reduce_ops(grads, local, want):
    ops = []
    for n in _BIG:
        kind = _BIG_KIND[n]
        for j, g in enumerate(grads[n]):
            if not want(n, j):
                continue
            loc = local[n].shape[-1] if kind[0] == "cols" or n in _PADDED_IN else g.shape[1]
            if n in _PADDED_IN:
                g = jnp.stack([g[:, s * loc:(s + 1) * loc] for s in range(4)])
            elif kind[0] == "lead":
                g = g.reshape(4, g.shape[0] // 4, g.shape[1])
            ops.append((n, j, kind, loc, g))
    return ops


def _pair_sums(ops, c_arr, tag):
    gots = pair_swap([o[4] for o in ops], [o[2] for o in ops], name=f"reduce_pair_swap_{tag}")
    return [pair_add(o[4], got, c_arr, o[2][0], name=f"reduce_pair_add_{o[0]}{o[1]}") for o, got in zip(ops, gots)]


def _is_ffn0(n, j):
    return j == 0 and n in ("ffn_w_gu", "ffn_w_down")


def _reduce_start(grads, local, c, want, tag):
    ops = _reduce_ops(grads, local, want)
    c_arr = jnp.reshape(c, (1,)).astype(jnp.int32)
    qs = _pair_sums(ops, c_arr, tag)
    send_sems, recv_sems, qs, lands, zero = scatter_start(qs, [o[2] for o in ops], [o[3] for o in ops],
                                                          name=f"reduce_scatter_start_{tag}")
    return (ops, send_sems, recv_sems, qs, lands, tag), zero


def _reduce_swap_start(grads, local, c, want, tag):
    ops = _reduce_ops(grads, local, want)
    send_sems, recv_sems, ps, lands, zero = pair_swap_start([o[4] for o in ops], [o[2] for o in ops],
                                                            name=f"reduce_pair_swap_start_{tag}")
    return (ops, send_sems, recv_sems, ps, lands, tag), zero


def _reduce_scatter_after(pending, after, c):
    ops, send_sems, recv_sems, ps, lands, tag = pending
    ps, gots = pair_swap_wait(send_sems, recv_sems, ps, lands, [o[2] for o in ops], after, name=f"reduce_pair_swap_wait_{tag}")
    c_arr = jnp.reshape(c, (1,)).astype(jnp.int32)
    qs = [pair_add(p, got, c_arr, o[2][0], name=f"reduce_pair_add_{o[0]}{o[1]}") for o, p, got in zip(ops, ps, gots)]
    send_sems, recv_sems, qs, lands, zero = scatter_start(qs, [o[2] for o in ops], [o[3] for o in ops],
                                                          name=f"reduce_scatter_start_{tag}")
    return (ops, send_sems, recv_sems, qs, lands, tag), zero


def _reduce_big(grads, local, pendings, after, x, y, c):
    c_arr = jnp.reshape(c, (1,)).astype(jnp.int32)
    ops, qs, rs = [], [], []
    for ops_p, send_sems, recv_sems, qs_p, lands, tag in pendings:
        qs_p, rs_p = scatter_wait(send_sems, recv_sems, qs_p, lands, [o[2] for o in ops_p], [o[3] for o in ops_p], after,
                                  name=f"reduce_scatter_wait_{tag}")
        ops, qs, rs = ops + ops_p, qs + qs_p, rs + rs_p
    ops_f = _reduce_ops(grads, local, lambda n, j: _in_layer0(n, j) and not _is_ffn0(n, j))
    qs_f = _pair_sums(ops_f, c_arr, "first")
    rs_f = list(chip_scatter(qs_f, [o[2] for o in ops_f], [o[3] for o in ops_f], name="reduce_chip_scatter_first"))
    ops, qs, rs = ops + ops_f, qs + qs_f, rs + rs_f
    fs = [chip_add(q, r, jnp.reshape(_pos(x, y, o[2][1]), (1,)).astype(jnp.int32), c_arr, o[2][0], o[3],
                   name=f"reduce_chip_add_{o[0]}{o[1]}") for o, q, r in zip(ops, qs, rs)]
    outs = share_rows(fs, name="reduce_share")
    red = {(o[0], o[1]): r for o, r in zip(ops, outs)}
    return {n: jnp.stack([red[(n, j)] for j in range(local[n].shape[0])]).reshape(local[n].shape) for n in _BIG}


def _reduce_small_start(grads):
    names = _REPLICATED + _SMALL_SHARDED
    flat = jnp.concatenate([grads[n].astype(F32).reshape(-1) for n in names])
    n_el = flat.shape[0]
    rows = -(-n_el // (LANE * 8)) * 8
    v = jnp.pad(flat, (0, rows * LANE - n_el)).reshape(rows, LANE)
    outs = gather_all_start(v, name="reduce_small_start")
    return outs[:4], outs[4]


def _reduce_small(pending, after, grads, local, x, y, c):
    names = _REPLICATED + _SMALL_SHARDED
    send_sems, recv_sems, v, land = pending
    land = gather_all_wait(send_sems, recv_sems, v, land, after, name="reduce_small_wait")
    me = jnp.reshape(4 * x + 2 * y + c, (1,)).astype(jnp.int32)
    red = sum_slots(land, v, me, name="reduce_small_add").reshape(-1)
    out, off = {}, 0
    for n in names:
        sz = math.prod(grads[n].shape)
        g = red[off:off + sz].reshape(grads[n].shape)
        off += sz
        if n in _SHARD_AXIS:
            ax = _SHARD_AXIS[n]
            loc = local[n].shape[ax]
            g = lax.dynamic_slice_in_dim(g, (2 * x + y) * loc, loc, axis=ax)
        out[n] = g
    return out


def kernel(x, norm_mix_g, norm_ffn_g, gla_w_in, gla_w_a2, gla_b_a, gla_norm_g, gla_w_out, ssd_w_in, ssd_conv_w, ssd_conv_b, ssd_dt_bias, ssd_a_log, ssd_d, ssd_norm_g, ssd_w_out, s5_log_dt, s5_a_re, s5_a_im, s5_b_re, s5_b_im, s5_c_re, s5_c_im, s5_d, s5_w_glu, ffn_w_gu, ffn_w_down, final_norm_g, loss_target, m_norm_mix_g, m_norm_ffn_g, m_gla_w_in, m_gla_w_a2, m_gla_b_a, m_gla_norm_g, m_gla_w_out, m_ssd_w_in, m_ssd_conv_w, m_ssd_conv_b, m_ssd_dt_bias, m_ssd_a_log, m_ssd_d, m_ssd_norm_g, m_ssd_w_out, m_s5_log_dt, m_s5_a_re, m_s5_a_im, m_s5_b_re, m_s5_b_im, m_s5_c_re, m_s5_c_im, m_s5_d, m_s5_w_glu, m_ffn_w_gu, m_ffn_w_down, m_final_norm_g, v_norm_mix_g, v_norm_ffn_g, v_gla_w_in, v_gla_w_a2, v_gla_b_a, v_gla_norm_g, v_gla_w_out, v_ssd_w_in, v_ssd_conv_w, v_ssd_conv_b, v_ssd_dt_bias, v_ssd_a_log, v_ssd_d, v_ssd_norm_g, v_ssd_w_out, v_s5_log_dt, v_s5_a_re, v_s5_a_im, v_s5_b_re, v_s5_b_im, v_s5_c_re, v_s5_c_im, v_s5_d, v_s5_w_glu, v_ffn_w_gu, v_ffn_w_down, v_final_norm_g):
    given = dict(locals())
    local = {n: given[n] for n in _WEIGHTS}
    px, py, pc = _place()

    first, gathering, zero = _gather_first(local)
    full = dict(local)
    full.update(first)
    full["norm_mix_g"] = local["norm_mix_g"] + zero
    big = [first]

    def weights_of(tag):
        def arrived(h):
            big.append(_gather_rest(big[-1], gathering[tag], h, tag))
            return big[-1]
        return arrived

    swapping, reducing = [], []

    def later_grads(g):
        pending, zero = _reduce_swap_start(g, local, pc, lambda n, j: not _in_layer0(n, j), "later")
        swapping.append(pending)
        return zero

    def ffn0_grads(g):
        pending, zero = _reduce_scatter_after(swapping[0], g["ffn_w_down"][0], pc)
        reducing.append(pending)
        g["ffn_w_down"][0] = g["ffn_w_down"][0] + zero
        pending, zero = _reduce_start(g, local, pc, _is_ffn0, "ffn0")
        reducing.append(pending)
        return zero

    loss, grad_x, grads = local_step(x[0], loss_target[0], full, weights_of("later"), later_grads, ffn0_grads, weights_of("ffn0"))
    loss = lax.psum(loss, ("x", "y", "c"))

    small, zero = _reduce_small_start(grads)
    grads["gla_w_out"][0] = grads["gla_w_out"][0] + zero
    red = _reduce_big(grads, local, reducing, grad_x, px, py, pc)
    red.update(_reduce_small(small, red["ffn_w_down"], grads, local, px, py, pc))

    deltas, new_m, new_v = {}, {}, {}
    for n in _WEIGHTS:
        deltas[n], new_m[n], new_v[n] = adamw(local[n], red[n], given["m_" + n], given["v_" + n], name=f"adamw_{n}")
    return (loss, grad_x[None], *[red[n] for n in _WEIGHTS], *[deltas[n] for n in _WEIGHTS],
            *[new_m[n] for n in _WEIGHTS], *[new_v[n] for n in _WEIGHTS])
```

```python
import functools
import math

import jax
import jax.numpy as jnp
from jax import lax
from jax.experimental import pallas as pl
from jax.experimental.pallas import tpu as pltpu

F32 = jnp.float32
BF16 = jnp.bfloat16

D = 1024
DEPTH = 4
CH = 64
EPS = 1e-6
GLA_H, GLA_DK, GLA_DV, GLA_RANK, GLA_TAU = 4, 128, 256, 16, 16.0
GLA_QK = GLA_H * GLA_DK
GLA_VD = GLA_H * GLA_DV
GLA_IN = 2 * GLA_QK + 2 * GLA_VD + GLA_RANK
GLA_INP = 3200
SSD_DI, SSD_HD, SSD_H, SSD_G, SSD_N, SSD_K = 2048, 64, 32, 8, 128, 4
SSD_GN = SSD_G * SSD_N
SSD_CONV = SSD_DI + 2 * SSD_GN
SSD_IN = SSD_DI + SSD_CONV + SSD_H
SSD_INP = 6272
S5_GS, S5_NG, S5_P = 16, 64, 64
S5_BLK = 8
FFN_H = 2816
LANE = 128
VMEM_LIMIT = 52 * 1024 * 1024
_MATMUL_VMEM = 40 * 1024 * 1024

ADAM_LR, ADAM_B1, ADAM_B2, ADAM_EPS, ADAM_WD, ADAM_STEP = 0.001, 0.9, 0.999, 1e-08, 0.01, 10

_ARB = "arbitrary"


def _cparams(n):
    return pltpu.CompilerParams(dimension_semantics=(_ARB,) * n, vmem_limit_bytes=VMEM_LIMIT)


def _pick(n, target, mult=LANE):
    best = None
    for c in range(mult, min(n, target) + 1, mult):
        if n % c == 0:
            best = c
    return best if best is not None else n


_DN = {"nn": (((1,), (0,)), ((), ())), "nt": (((1,), (1,)), ((), ())), "tn": (((0,), (0,)), ((), ()))}


def _dot_raw(a, b, form):
    return lax.dot_general(a.astype(BF16), b.astype(BF16), _DN[form], preferred_element_type=F32)


@functools.partial(jax.custom_vjp, nondiff_argnums=(2,))
def bdot(a, b, form):
    return _dot_raw(a, b, form)


def _bdot_fwd(a, b, form):
    return _dot_raw(a, b, form), (a, b)


def _bdot_bwd(form, res, g):
    a, b = res
    if form == "nn":
        return _dot_raw(g, b, "nt"), _dot_raw(a, g, "tn")
    if form == "nt":
        return _dot_raw(g, b, "nn"), _dot_raw(g, a, "tn")
    return _dot_raw(b, g, "nt"), _dot_raw(a, g, "nn")


bdot.defvjp(_bdot_fwd, _bdot_bwd)


def _hdot(a, b):
    return jnp.dot(a, b, precision=lax.Precision.HIGHEST, preferred_element_type=F32)


@jax.custom_vjp
def cdot_left(c, ct, x):
    return _hdot(c, x)


def _cdl_fwd(c, ct, x):
    return _hdot(c, x), (c, ct)


def _cdl_bwd(res, g):
    c, ct = res
    return jnp.zeros_like(c), jnp.zeros_like(ct), _hdot(ct, g)


cdot_left.defvjp(_cdl_fwd, _cdl_bwd)


@jax.custom_vjp
def cdot_right(x, c, ct):
    return _hdot(x, c)


def _cdr_fwd(x, c, ct):
    return _hdot(x, c), (c, ct)


def _cdr_bwd(res, g):
    c, ct = res
    return _hdot(g, ct), jnp.zeros_like(c), jnp.zeros_like(ct)


cdot_right.defvjp(_cdr_fwd, _cdr_bwd)


def _sigmoid(x):
    return 1.0 / (1.0 + jnp.exp(-x))


def _silu(x):
    return x * _sigmoid(x)


def _softplus(x):
    return jnp.maximum(x, 0.0) + jnp.log(1.0 + jnp.exp(-jnp.abs(x)))


def _log_sigmoid(x):
    return jnp.minimum(x, 0.0) - jnp.log(1.0 + jnp.exp(-jnp.abs(x)))


def _gelu(x):
    c = math.sqrt(2.0 / math.pi)
    return 0.5 * x * (1.0 + jnp.tanh(c * (x + 0.044715 * (x * x * x))))


def _rms(x, g):
    return x * lax.rsqrt(jnp.mean(x * x, axis=-1, keepdims=True) + EPS) * g


def _iota2(shape, axis):
    return lax.broadcasted_iota(jnp.int32, shape, axis)


def matmul(a, b, form, *, name, G=1, out_dtype=F32, add=None):
    isz = lambda t: jnp.dtype(t.dtype).itemsize
    osz = jnp.dtype(out_dtype).itemsize + (isz(add) if add is not None else 0)

    def fits(bm, bn, bk):
        return 2 * (bm * bk * isz(a) + bk * bn * isz(b) + bm * bn * osz) + 4 * bm * bn <= _MATMUL_VMEM

    if form in ("nn", "nt"):
        M = a.shape[0]
        K = a.shape[1] // G
        N = b.shape[2] if form == "nn" else b.shape[1]
        bm, bn, bk = min(M, 1024), _pick(N, 1536), _pick(K, 2048)
        while not fits(bm, bn, bk) and bk % 256 == 0:
            bk //= 2
        nj, nk = N // bn, K // bk
        grid = (G, M // bm, nj, nk)
        a_spec = pl.BlockSpec((bm, bk), lambda g, i, j, k: (i, g * nk + k))
        if form == "nn":
            b_spec = pl.BlockSpec((None, bk, bn), lambda g, i, j, k: (g, k, j))
        else:
            b_spec = pl.BlockSpec((None, bn, bk), lambda g, i, j, k: (g, j, k))
        o_spec = pl.BlockSpec((bm, bn), lambda g, i, j, k: (i, g * nj + j))
        out_shape = jax.ShapeDtypeStruct((M, G * N), out_dtype)
    else:
        T = a.shape[0]
        Ka, Nb = a.shape[1] // G, b.shape[1] // G
        bm, bn, bk = _pick(Ka, 1408), _pick(Nb, 1536), min(T, 2048)
        while not fits(bm, bn, bk) and bk % 512 == 0:
            bk //= 2
        ni, nj, nk = Ka // bm, Nb // bn, T // bk
        grid = (G, ni, nj, nk)
        a_spec = pl.BlockSpec((bk, bm), lambda g, i, j, k: (k, g * ni + i))
        b_spec = pl.BlockSpec((bk, bn), lambda g, i, j, k: (k, g * nj + j))
        o_spec = pl.BlockSpec((None, bm, bn), lambda g, i, j, k: (g, i, j))
        out_shape = jax.ShapeDtypeStruct((G, Ka, Nb), out_dtype)
    has_add = add is not None

    def finish(refs, r):
        if has_add:
            r = r + refs[2][...].astype(F32)
        o_ref = refs[3] if has_add else refs[2]
        o_ref[...] = r.astype(o_ref.dtype)

    def body_one(*refs):
        finish(refs, _dot_raw(refs[0][...], refs[1][...], form))

    def body_acc(*refs):
        acc_ref = refs[-1]
        k = pl.program_id(3)

        @pl.when(k == 0)
        def _():
            acc_ref[...] = jnp.zeros_like(acc_ref)

        acc_ref[...] += _dot_raw(refs[0][...], refs[1][...], form)

        @pl.when(k == nk - 1)
        def _():
            finish(refs, acc_ref[...])

    in_specs = [a_spec, b_spec]
    args = [a, b]
    if has_add:
        in_specs.append(o_spec)
        args.append(add)
    return pl.pallas_call(
        body_one if nk == 1 else body_acc, name=name, grid=grid, in_specs=in_specs, out_specs=o_spec,
        out_shape=out_shape, scratch_shapes=[] if nk == 1 else [pltpu.VMEM((bm, bn), F32)],
        compiler_params=_cparams(4),
    )(*args)


def matmul_nt_norm_bwd(a, w, h, g, d, *, name):
    T, K = a.shape
    bm = min(T, 512)
    bk = _pick(K, 2048)
    nk = K // bk

    def body(a_ref, w_ref, h_ref, g_ref, d_ref, dh_ref, dg_ref, acc_ref):
        i, k = pl.program_id(0), pl.program_id(1)

        @pl.when((i == 0) & (k == 0))
        def _():
            dg_ref[...] = jnp.zeros_like(dg_ref)

        @pl.when(k == 0)
        def _():
            acc_ref[...] = jnp.zeros_like(acc_ref)

        acc_ref[...] += _dot_raw(a_ref[...], w_ref[...], "nt")

        @pl.when(k == nk - 1)
        def _():
            _, vjp = jax.vjp(lambda g_, h_: _rms(h_, g_), g_ref[...], h_ref[...])
            dg, dh = vjp(acc_ref[...])
            dh_ref[...] = dh + d_ref[...]
            dg_ref[...] += dg

    row = pl.BlockSpec((bm, D), lambda i, k: (i, 0))
    one = pl.BlockSpec((1, D), lambda i, k: (0, 0))
    return pl.pallas_call(
        body, name=name, grid=(T // bm, nk),
        in_specs=[pl.BlockSpec((bm, bk), lambda i, k: (i, k)), pl.BlockSpec((D, bk), lambda i, k: (0, k)), row, one, row],
        out_specs=[row, one], out_shape=[jax.ShapeDtypeStruct((T, D), F32), jax.ShapeDtypeStruct((1, D), F32)],
        scratch_shapes=[pltpu.VMEM((bm, D), F32)], compiler_params=_cparams(2),
    )(a, w, h, g, d)


def ffn_up(hn, w_il, *, name):
    T = hn.shape[0]
    bm, hb = min(T, 512), FFN_H // 2

    def body(a_ref, b_ref, act_ref, gu_ref):
        r = _dot_raw(a_ref[...], b_ref[...], "nn")
        act_ref[...] = (_silu(r[:, :hb]) * r[:, hb:]).astype(act_ref.dtype)
        gu_ref[...] = r.astype(gu_ref.dtype)

    return pl.pallas_call(
        body, name=name, grid=(2, T // bm),
        in_specs=[pl.BlockSpec((bm, D), lambda j, i: (i, 0)), pl.BlockSpec((D, 2 * hb), lambda j, i: (0, j))],
        out_specs=[pl.BlockSpec((bm, hb), lambda j, i: (i, j)), pl.BlockSpec((bm, 2 * hb), lambda j, i: (i, j))],
        out_shape=[jax.ShapeDtypeStruct((T, FFN_H), BF16), jax.ShapeDtypeStruct((T, 2 * FFN_H), BF16)],
        compiler_params=_cparams(2),
    )(hn, w_il)


_DACT_CHUNK = 512


def ffn_dact(d, w_down, gu, *, name):
    T = d.shape[0]
    bm, hb = min(T, 512), FFN_H // 2

    def body(d_ref, w_ref, gu_ref, o_ref):
        d_blk = d_ref[...].astype(BF16)
        for lo in range(0, hb, _DACT_CHUNK):
            hi = min(lo + _DACT_CHUNK, hb)
            da = _dot_raw(d_blk, w_ref[lo:hi, :], "nt")
            g, u = gu_ref[:, lo:hi].astype(F32), gu_ref[:, hb + lo:hb + hi].astype(F32)
            sg = _sigmoid(g)
            o_ref[:, lo:hi] = (da * u * (sg * (1.0 + g * (1.0 - sg)))).astype(o_ref.dtype)
            o_ref[:, hb + lo:hb + hi] = (da * (g * sg)).astype(o_ref.dtype)

    return pl.pallas_call(
        body, name=name, grid=(2, T // bm),
        in_specs=[pl.BlockSpec((bm, D), lambda j, i: (i, 0)), pl.BlockSpec((hb, D), lambda j, i: (j, 0)),
                  pl.BlockSpec((bm, 2 * hb), lambda j, i: (i, j))],
        out_specs=pl.BlockSpec((bm, 2 * hb), lambda j, i: (i, j)),
        out_shape=jax.ShapeDtypeStruct((T, 2 * FFN_H), BF16), compiler_params=_cparams(2),
    )(d, w_down, gu)


def _row_entry(e):
    return e if isinstance(e, tuple) else (e, e.shape[1], 0)


def _row_spec(bt, e):
    _, width, idx = e
    return pl.BlockSpec((bt, width), lambda i: (i, idx))


def _full_spec(p):
    return pl.BlockSpec(p.shape, lambda i: (0,) * p.ndim)


def rowwise(f, params, rows, outs, *, bt, name):
    rows = [_row_entry(e) for e in rows]
    T = rows[0][0].shape[0]
    bt = min(bt, T)
    np_, nr = len(params), len(rows)

    def body(*refs):
        p = tuple(r[...].astype(F32) for r in refs[:np_])
        rw = tuple(r[...].astype(F32) for r in refs[np_:np_ + nr])
        res = f(p, rw)
        for o_ref, o in zip(refs[np_ + nr:], res):
            o_ref[...] = o.astype(o_ref.dtype)

    res = pl.pallas_call(
        body, name=name, grid=(T // bt,),
        in_specs=[_full_spec(p) for p in params] + [_row_spec(bt, e) for e in rows],
        out_specs=[pl.BlockSpec((bt, w), lambda i: (i, 0)) for w, _ in outs],
        out_shape=[jax.ShapeDtypeStruct((T, w), dt) for w, dt in outs],
        compiler_params=_cparams(1),
    )(*params, *[e[0] for e in rows])
    return list(res)


def rowwise_vjp(f, params, rows, cts, drow_dtypes, *, bt, name):
    rows = [_row_entry(e) for e in rows]
    cts = [_row_entry(e) for e in cts]
    T = rows[0][0].shape[0]
    bt = min(bt, T)
    np_, nr, nc = len(params), len(rows), len(cts)
    want = [i for i, dt in enumerate(drow_dtypes) if dt is not None]

    def body(*refs):
        p = tuple(r[...].astype(F32) for r in refs[:np_])
        rw = tuple(r[...].astype(F32) for r in refs[np_:np_ + nr])
        ct = tuple(r[...].astype(F32) for r in refs[np_ + nr:np_ + nr + nc])
        outs = refs[np_ + nr + nc:]
        _, vjp = jax.vjp(f, p, rw)
        dp, dr = vjp(ct)

        @pl.when(pl.program_id(0) == 0)
        def _():
            for o in outs[:np_]:
                o[...] = jnp.zeros_like(o)

        for o, d in zip(outs[:np_], dp):
            o[...] += d
        for o, i in zip(outs[np_:], want):
            o[...] = dr[i].astype(o.dtype)

    res = pl.pallas_call(
        body, name=name, grid=(T // bt,),
        in_specs=[_full_spec(p) for p in params] + [_row_spec(bt, e) for e in rows] + [_row_spec(bt, e) for e in cts],
        out_specs=[_full_spec(p) for p in params] + [pl.BlockSpec((bt, rows[i][1]), lambda i_: (i_, 0)) for i in want],
        out_shape=[jax.ShapeDtypeStruct(p.shape, F32) for p in params]
        + [jax.ShapeDtypeStruct((T, rows[i][1]), drow_dtypes[i]) for i in want],
        compiler_params=_cparams(1),
    )(*params, *[e[0] for e in rows], *[e[0] for e in cts])
    res = list(res)
    return res[:np_], res[np_:]


def f_rmsnorm(p, r):
    return (_rms(r[0], p[0]),)


def f_gla_gate_in(p, r):
    w_a2, b_a = p
    z = bdot(r[0], w_a2, "nn") + b_a
    return (_log_sigmoid(z) / GLA_TAU,)


def f_gla_gate_in_fwd(p, r):
    w_a2, b_a = p
    z = _dot_raw(r[0], w_a2, "nn") + b_a
    return (_log_sigmoid(z) / GLA_TAU,)


def f_s5_act(p, r):
    (dsk,) = p
    ycp, u = r
    return (_gelu(ycp + dsk * u),)


def f_glu_res(p, r):
    vg, h = r
    return (vg[:, :D] * _sigmoid(vg[:, D:]) + h,)


def f_glu(p, r):
    vg = r[0]
    return (vg[:, :D] * _sigmoid(vg[:, D:]),)


def loss_head(h, g, target, *, name):
    T = h.shape[0]
    bt = min(T, 256)

    def lossf(g_, h_, t_):
        e = _rms(h_, g_) - t_
        return (0.5 / D) * jnp.sum(e * e)

    def body(g_ref, h_ref, t_ref, loss_ref, dg_ref, dh_ref):
        @pl.when(pl.program_id(0) == 0)
        def _():
            loss_ref[...] = jnp.zeros_like(loss_ref)
            dg_ref[...] = jnp.zeros_like(dg_ref)

        val, vjp = jax.vjp(lossf, g_ref[...], h_ref[...], t_ref[...])
        dg, dh, _ = vjp(jnp.ones((), F32))
        loss_ref[...] += jnp.full(loss_ref.shape, val, F32)
        dg_ref[...] += dg
        dh_ref[...] = dh

    row = pl.BlockSpec((bt, D), lambda i: (i, 0))
    one = pl.BlockSpec((1, D), lambda i: (0, 0))
    loss, dg, dh = pl.pallas_call(
        body, name=name, grid=(T // bt,), in_specs=[one, row, row],
        out_specs=[pl.BlockSpec((1, LANE), lambda i: (0, 0)), one, row],
        out_shape=[jax.ShapeDtypeStruct((1, LANE), F32), jax.ShapeDtypeStruct((1, D), F32),
                   jax.ShapeDtypeStruct((T, D), F32)],
        compiler_params=_cparams(1),
    )(g, h, target)
    return loss[0, 0], dg, dh


def _gla_consts():
    r, c = _iota2((CH, CH), 0), _iota2((CH, CH), 1)
    return (r >= c).astype(F32), (r <= c).astype(F32), r >= c


def _gla_chunk(q, k, v, la, st, consts, dot, cdl):
    L, Lt, tril = consts
    lc = cdl(L, Lt, la)
    lend = lc[CH - 1:CH, :]
    e, ei = jnp.exp(lc), jnp.exp(-lc)
    qs = q * (GLA_DK ** -0.5)
    qf, kf, qb, kb = qs * e, k * ei, qs * ei, k * e
    sc = jnp.where(tril, dot(qf, kf, "nt"), dot(qb, kb, "nt"))
    o = dot(sc, v, "nn") + dot(qf, st, "nt")
    kd = k * jnp.exp(lend - lc)
    st_new = st * jnp.exp(lend) + dot(v, kd, "tn")
    return o, st_new


def _gla_block(q, k, v, la, st, nc, dot, cdl):
    consts = _gla_consts()
    outs = []
    for c in range(nc):
        sl = slice(c * CH, (c + 1) * CH)
        o, st = _gla_chunk(q[sl], k[sl], v[sl], la[sl], st, consts, dot, cdl)
        outs.append(o)
    return jnp.concatenate(outs, axis=0), st


_GLA_HP = 2


def _gla_specs(rows, rev, nb):
    t = (lambda j: nb - 1 - j) if rev else (lambda j: j)
    hp, ng = _GLA_HP, GLA_H // _GLA_HP
    q = pl.BlockSpec((rows, hp * GLA_DK), lambda h, j: (t(j), h))
    k = pl.BlockSpec((rows, hp * GLA_DK), lambda h, j: (t(j), ng + h))
    v = pl.BlockSpec((rows, hp * GLA_DV), lambda h, j: (t(j), ng + h))
    la = pl.BlockSpec((rows, hp * GLA_DK), lambda h, j: (t(j), h))
    ss = pl.BlockSpec((None, hp, GLA_DV, GLA_DK), lambda h, j: (t(j), h, 0, 0))
    o = pl.BlockSpec((rows, hp * GLA_DV), lambda h, j: (t(j), h))
    r = pl.BlockSpec((rows, hp * GLA_DV), lambda h, j: (t(j), 2 * ng + h))
    g = pl.BlockSpec((1, hp * GLA_DV), lambda h, j: (0, h))
    return q, k, v, la, ss, o, r, g


def _gla_heads(q, k, v, la, r, ng, sts, nc, dot, cdl):
    outs, new = [], []
    for i in range(_GLA_HP):
        kk, vv = slice(i * GLA_DK, (i + 1) * GLA_DK), slice(i * GLA_DV, (i + 1) * GLA_DV)
        o, st = _gla_block(q[:, kk], k[:, kk], v[:, vv], la[:, kk], sts[i], nc, dot, cdl)
        outs.append(_rms(o, ng[:, vv]) * _silu(r[:, vv]))
        new.append(st)
    return jnp.concatenate(outs, axis=1), tuple(new)


def gla_scan_fwd(proj, la, ng, *, nc, name):
    T = proj.shape[0]
    rows = min(T, nc * CH)
    nc = rows // CH
    nb = T // rows
    q_s, k_s, v_s, la_s, ss_s, o_s, r_s, g_s = _gla_specs(rows, False, nb)

    def body(q_ref, k_ref, v_ref, la_ref, r_ref, g_ref, o_ref, ss_ref, st_ref):
        @pl.when(pl.program_id(1) == 0)
        def _():
            st_ref[...] = jnp.zeros_like(st_ref)

        ss_ref[...] = st_ref[...]
        sts = tuple(st_ref[i] for i in range(_GLA_HP))
        o, sts = _gla_heads(q_ref[...], k_ref[...], v_ref[...], la_ref[...], r_ref[...], g_ref[...], sts, nc,
                            _dot_raw, lambda c, ct, x: _hdot(c, x))
        o_ref[...] = o.astype(o_ref.dtype)
        for i in range(_GLA_HP):
            st_ref[i] = sts[i]

    return pl.pallas_call(
        body, name=name, grid=(GLA_H // _GLA_HP, nb), in_specs=[q_s, k_s, v_s, la_s, r_s, g_s], out_specs=[o_s, ss_s],
        out_shape=[jax.ShapeDtypeStruct((T, GLA_VD), BF16), jax.ShapeDtypeStruct((nb, GLA_H, GLA_DV, GLA_DK), F32)],
        scratch_shapes=[pltpu.VMEM((_GLA_HP, GLA_DV, GLA_DK), F32)], compiler_params=_cparams(2),
    )(proj, proj, proj, la, proj, ng)


def gla_scan_bwd(proj, la, ng, ss, do, *, nc, name):
    T = proj.shape[0]
    rows = min(T, nc * CH)
    nc = rows // CH
    nb = T // rows
    q_s, k_s, v_s, la_s, ss_s, o_s, r_s, g_s = _gla_specs(rows, True, nb)
    t = lambda j: nb - 1 - j
    dqk_s = pl.BlockSpec((rows, _GLA_HP * GLA_DK), lambda h, j: (t(j), h))

    def body(q_ref, k_ref, v_ref, la_ref, r_ref, g_ref, ss_ref, do_ref,
             dq_ref, dk_ref, dv_ref, dla_ref, dr_ref, dg_ref, dst_ref):
        @pl.when(pl.program_id(1) == 0)
        def _():
            dst_ref[...] = jnp.zeros_like(dst_ref)
            dg_ref[...] = jnp.zeros_like(dg_ref)

        fn = lambda q, k, v, la_, r, g, *sts: _gla_heads(q, k, v, la_, r, g, sts, nc, bdot, cdot_left)
        _, vjp = jax.vjp(fn, q_ref[...], k_ref[...], v_ref[...], la_ref[...], r_ref[...], g_ref[...],
                         *[ss_ref[i] for i in range(_GLA_HP)])
        dq, dk, dv, dla, dr, dg, *dsts = vjp((do_ref[...], tuple(dst_ref[i] for i in range(_GLA_HP))))
        dq_ref[...] = dq.astype(dq_ref.dtype)
        dk_ref[...] = dk.astype(dk_ref.dtype)
        dv_ref[...] = dv.astype(dv_ref.dtype)
        dla_ref[...] = dla
        dr_ref[...] = dr.astype(dr_ref.dtype)
        dg_ref[...] += dg
        for i in range(_GLA_HP):
            dst_ref[i] = dsts[i]

    return pl.pallas_call(
        body, name=name, grid=(GLA_H // _GLA_HP, nb), in_specs=[q_s, k_s, v_s, la_s, r_s, g_s, ss_s, o_s],
        out_specs=[dqk_s, dqk_s, o_s, dqk_s, o_s, g_s],
        out_shape=[jax.ShapeDtypeStruct((T, GLA_QK), BF16), jax.ShapeDtypeStruct((T, GLA_QK), BF16),
                   jax.ShapeDtypeStruct((T, GLA_VD), BF16), jax.ShapeDtypeStruct((T, GLA_QK), F32),
                   jax.ShapeDtypeStruct((T, GLA_VD), BF16), jax.ShapeDtypeStruct((1, GLA_VD), F32)],
        scratch_shapes=[pltpu.VMEM((_GLA_HP, GLA_DV, GLA_DK), F32)], compiler_params=_cparams(2),
    )(proj, proj, proj, la, proj, ng, ss, do)


_CONV_W = 512
_CONV_OFF = SSD_DI // _CONV_W


def _conv_pre(x, prev8, w_ref, b_ref):
    bt = x.shape[0]
    ext = jnp.concatenate([prev8, x], axis=0)
    shifted = []
    for j in range(SSD_K):
        s = SSD_K - 1 - j
        shifted.append(x if s == 0 else pltpu.roll(ext, s, 0)[8:8 + bt])
    pre = b_ref[...] + sum(w_ref[j:j + 1, :] * shifted[j] for j in range(SSD_K))
    return pre, shifted


def ssd_conv_fwd(proj, w, b, *, name):
    T = proj.shape[0]
    bt = min(T, 512)
    nb = T // bt

    def body(x_ref, w_ref, b_ref, o_ref, carry_ref):
        @pl.when(pl.program_id(1) == 0)
        def _():
            carry_ref[...] = jnp.zeros_like(carry_ref)

        x = x_ref[...]
        pre, _ = _conv_pre(x, carry_ref[...], w_ref, b_ref)
        o_ref[...] = _silu(pre)
        carry_ref[...] = x[bt - 8:, :]

    return pl.pallas_call(
        body, name=name, grid=(SSD_CONV // _CONV_W, nb),
        in_specs=[pl.BlockSpec((bt, _CONV_W), lambda c, t: (t, _CONV_OFF + c)),
                  pl.BlockSpec((SSD_K, _CONV_W), lambda c, t: (0, c)),
                  pl.BlockSpec((1, _CONV_W), lambda c, t: (0, c))],
        out_specs=pl.BlockSpec((bt, _CONV_W), lambda c, t: (t, c)),
        out_shape=jax.ShapeDtypeStruct((T, SSD_CONV), F32),
        scratch_shapes=[pltpu.VMEM((8, _CONV_W), F32)], compiler_params=_cparams(2),
    )(proj, w, b)


def ssd_conv_bwd(proj, w, b, dout, col0, *, name):
    T, ncols = dout.shape
    bt = min(T, 512)
    nb = T // bt
    r8 = bt // 8
    c0 = col0 // _CONV_W

    def body(x_ref, xp_ref, w_ref, b_ref, do_ref, dx_ref, dw_ref, db_ref, carry_ref):
        t = pl.program_id(1)

        @pl.when(t == 0)
        def _():
            carry_ref[...] = jnp.zeros_like(carry_ref)
            dw_ref[...] = jnp.zeros_like(dw_ref)
            db_ref[...] = jnp.zeros_like(db_ref)

        x = x_ref[...]
        prev8 = jnp.where(t == nb - 1, 0.0, xp_ref[...])
        pre, shifted = _conv_pre(x, prev8, w_ref, b_ref)
        sg = _sigmoid(pre)
        dpre = do_ref[...] * (sg * (1.0 + pre * (1.0 - sg)))
        ext = jnp.concatenate([dpre, carry_ref[...]], axis=0)
        dx = w_ref[SSD_K - 1:SSD_K, :] * dpre
        for j in range(SSD_K - 1):
            s = SSD_K - 1 - j
            dx = dx + w_ref[j:j + 1, :] * pltpu.roll(ext, bt + 8 - s, 0)[:bt]
        dx_ref[...] = dx.astype(dx_ref.dtype)
        dw_ref[...] += jnp.concatenate([jnp.sum(dpre * shifted[j], axis=0, keepdims=True) for j in range(SSD_K)], axis=0)
        db_ref[...] += jnp.sum(dpre, axis=0, keepdims=True)
        carry_ref[...] = dpre[:8, :]

    rt = lambda t: nb - 1 - t
    return pl.pallas_call(
        body, name=name, grid=(ncols // _CONV_W, nb),
        in_specs=[pl.BlockSpec((bt, _CONV_W), lambda c, t: (rt(t), _CONV_OFF + c0 + c)),
                  pl.BlockSpec((8, _CONV_W), lambda c, t: (jnp.maximum(rt(t) * r8 - 1, 0), _CONV_OFF + c0 + c)),
                  pl.BlockSpec((SSD_K, _CONV_W), lambda c, t: (0, c0 + c)),
                  pl.BlockSpec((1, _CONV_W), lambda c, t: (0, c0 + c)),
                  pl.BlockSpec((bt, _CONV_W), lambda c, t: (rt(t), c))],
        out_specs=[pl.BlockSpec((bt, _CONV_W), lambda c, t: (rt(t), c)),
                   pl.BlockSpec((SSD_K, _CONV_W), lambda c, t: (0, c)),
                   pl.BlockSpec((1, _CONV_W), lambda c, t: (0, c))],
        out_shape=[jax.ShapeDtypeStruct((T, ncols), BF16), jax.ShapeDtypeStruct((SSD_K, ncols), F32),
                   jax.ShapeDtypeStruct((1, ncols), F32)],
        scratch_shapes=[pltpu.VMEM((8, _CONV_W), F32)], compiler_params=_cparams(2),
    )(proj, proj, w, b, dout)


_SSD_U = 2 * CH


def _ssd_unit(xs, bm, cm, dtraw, dtb, alog, dsk, hp, g, dot, cdl, cdr):
    U, P2 = _SSD_U, 2 * SSD_HD
    r, c = _iota2((U, U), 0), _iota2((U, U), 1)
    same = (r // CH) == (c // CH)
    Lb = (same & (r >= c)).astype(F32)
    Ub = (same & (r <= c)).astype(F32)
    lane = _iota2((1, U), 1)
    lo_lane = _iota2((1, P2), 1) < SSD_HD
    lo_sub = _iota2((P2, 1), 0) < SSD_HD
    diag2 = (_iota2((CH, P2), 0) == (_iota2((CH, P2), 1) % CH)).astype(F32)

    dt = _softplus(dtraw + dtb)
    da = dt * (-jnp.exp(alog))
    cum = cdl(Lb, Ub, da)
    ys = []
    new_hp = []
    for pr in range(2):
        xs_p = xs[:, pr * P2:(pr + 1) * P2]
        cols, dts, dks = [], [], []
        for jj in range(2):
            oh_l = (lane == g * (SSD_H // SSD_G) + 2 * pr + jj).astype(F32)
            cols.append(jnp.sum(cum * oh_l, axis=1, keepdims=True))
            dts.append(jnp.sum(dt * oh_l, axis=1, keepdims=True))
            dks.append(jnp.sum(dsk * oh_l, axis=1, keepdims=True))
        dsk_p = jnp.where(lo_lane, dks[0], dks[1])
        h = hp[pr]
        yc = []
        for ci in range(2):
            sl = slice(ci * CH, (ci + 1) * CH)
            xs_c, bm_c, cm_c = xs_p[sl], bm[sl], cm[sl]
            col = jnp.where(lo_lane, cols[0][sl], cols[1][sl])
            dtc = jnp.where(lo_lane, dts[0][sl], dts[1][sl])
            row = jnp.sum(diag2 * col, axis=0, keepdims=True)
            dtrow = jnp.sum(diag2 * dtc, axis=0, keepdims=True)
            cb = dot(cm_c, jnp.concatenate([bm_c, bm_c], axis=0), "nt")
            mix = cb * jnp.exp(-jnp.abs(col - row)) * dtrow
            xbd = jnp.concatenate([jnp.where(lo_lane, xs_c, 0.0), jnp.where(lo_lane, 0.0, xs_c)], axis=0)
            y_intra = dot(mix, xbd, "nn")
            ce = jnp.where(lo_lane, cols[0][ci * CH + CH - 1:ci * CH + CH, :], cols[1][ci * CH + CH - 1:ci * CH + CH, :])
            y_inter = dot(cm_c, h, "nt") * jnp.exp(col)
            xw = xs_c * (dtc * jnp.exp(ce - col))
            ce_s = [cols[jj][ci * CH + CH - 1:ci * CH + CH, :] for jj in range(2)]
            a_p = jnp.where(lo_sub, jnp.exp(ce_s[0]), jnp.exp(ce_s[1]))
            h = a_p * h + dot(xw, bm_c, "tn")
            yc.append(y_intra + y_inter + dsk_p * xs_c)
        ys.append(jnp.concatenate(yc, axis=0))
        new_hp.append(h)
    return jnp.concatenate(ys, axis=1), tuple(new_hp)


def _ssd_block(xs, bm, cm, dtraw, z, dtb, alog, dsk, ng, hp, g, nu, dot, cdl, cdr):
    outs = []
    for u in range(nu):
        sl = slice(u * _SSD_U, (u + 1) * _SSD_U)
        y, hp = _ssd_unit(xs[sl], bm[sl], cm[sl], dtraw[sl], dtb, alog, dsk, hp, g, dot, cdl, cdr)
        outs.append(y)
    return _rms(jnp.concatenate(outs, axis=0) * _silu(z), ng), hp


def _ssd_specs(rows, rev, nb):
    t = (lambda j: nb - 1 - j) if rev else (lambda j: j)
    gw = SSD_DI // SSD_G
    xs = pl.BlockSpec((rows, gw), lambda j, g: (t(j), g))
    bm = pl.BlockSpec((rows, SSD_N), lambda j, g: (t(j), SSD_DI // SSD_N + g))
    cm = pl.BlockSpec((rows, SSD_N), lambda j, g: (t(j), (SSD_DI + SSD_GN) // SSD_N + g))
    dtr = pl.BlockSpec((rows, LANE), lambda j, g: (t(j), (SSD_DI + SSD_CONV) // LANE))
    par = pl.BlockSpec((1, LANE), lambda j, g: (0, 0))
    hs = pl.BlockSpec((None, None, 2, 2 * SSD_HD, SSD_N), lambda j, g: (t(j), g, 0, 0, 0))
    y = pl.BlockSpec((rows, gw), lambda j, g: (t(j), g))
    ng = pl.BlockSpec((1, gw), lambda j, g: (0, g))
    return xs, bm, cm, dtr, par, hs, y, ng


def ssd_scan_fwd(xbc, proj, dtb, alog, dsk, ng, *, nu, name):
    T = xbc.shape[0]
    rows = min(T, nu * _SSD_U)
    nu = rows // _SSD_U
    nb = T // rows
    xs_s, bm_s, cm_s, dt_s, par_s, hs_s, y_s, ng_s = _ssd_specs(rows, False, nb)

    def body(xs_ref, bm_ref, cm_ref, dt_ref, z_ref, dtb_ref, al_ref, dsk_ref, ng_ref, y_ref, hs_ref, h_ref):
        g = pl.program_id(1)

        @pl.when(pl.program_id(0) == 0)
        def _():
            h_ref[g] = jnp.zeros(h_ref.shape[1:], F32)

        hs_ref[...] = h_ref[g]
        hp = (h_ref[g, 0], h_ref[g, 1])
        y, hp = _ssd_block(xs_ref[...], bm_ref[...], cm_ref[...], dt_ref[...], z_ref[...], dtb_ref[...], al_ref[...],
                           dsk_ref[...], ng_ref[...], hp, g, nu, _dot_raw, lambda c, ct, x: _hdot(c, x), lambda x, c, ct: _hdot(x, c))
        y_ref[...] = y.astype(y_ref.dtype)
        h_ref[g, 0] = hp[0]
        h_ref[g, 1] = hp[1]

    return pl.pallas_call(
        body, name=name, grid=(nb, SSD_G), in_specs=[xs_s, bm_s, cm_s, dt_s, y_s, par_s, par_s, par_s, ng_s],
        out_specs=[y_s, hs_s],
        out_shape=[jax.ShapeDtypeStruct((T, SSD_DI), BF16), jax.ShapeDtypeStruct((nb, SSD_G, 2, 2 * SSD_HD, SSD_N), F32)],
        scratch_shapes=[pltpu.VMEM((SSD_G, 2, 2 * SSD_HD, SSD_N), F32)], compiler_params=_cparams(2),
    )(xbc, xbc, xbc, proj, proj, dtb, alog, dsk, ng)


def ssd_scan_bwd(xbc, proj, dtb, alog, dsk, ng, hs, dy, *, nu, name):
    T = xbc.shape[0]
    rows = min(T, nu * _SSD_U)
    nu = rows // _SSD_U
    nb = T // rows
    xs_s, bm_s, cm_s, dt_s, par_s, hs_s, y_s, ng_s = _ssd_specs(rows, True, nb)
    gw = SSD_DI // SSD_G
    dng_s = pl.BlockSpec((1, SSD_DI), lambda j, g: (0, 0))
    t = lambda j: nb - 1 - j
    n_s = pl.BlockSpec((rows, SSD_N), lambda j, g: (t(j), g))
    ddt_s = pl.BlockSpec((rows, LANE), lambda j, g: (t(j), 0))

    def body(xs_ref, bm_ref, cm_ref, dt_ref, z_ref, dtb_ref, al_ref, dsk_ref, ng_ref, hs_ref, dy_ref,
             dxs_ref, dbm_ref, dcm_ref, ddt_ref, ddtb_ref, dal_ref, ddsk_ref, dz_ref, dng_ref, dh_ref):
        j, g = pl.program_id(0), pl.program_id(1)

        @pl.when(j == 0)
        def _():
            dh_ref[g] = jnp.zeros(dh_ref.shape[1:], F32)

        @pl.when((j == 0) & (g == 0))
        def _():
            ddtb_ref[...] = jnp.zeros_like(ddtb_ref)
            dal_ref[...] = jnp.zeros_like(dal_ref)
            ddsk_ref[...] = jnp.zeros_like(ddsk_ref)
            dng_ref[...] = jnp.zeros_like(dng_ref)

        @pl.when(g == 0)
        def _():
            ddt_ref[...] = jnp.zeros_like(ddt_ref)

        fn = lambda xs, bm, cm, dtr, z, dtb_, al, dsk_, ng_, h0, h1: _ssd_block(
            xs, bm, cm, dtr, z, dtb_, al, dsk_, ng_, (h0, h1), g, nu, bdot, cdot_left, cdot_right)
        _, vjp = jax.vjp(fn, xs_ref[...], bm_ref[...], cm_ref[...], dt_ref[...], z_ref[...], dtb_ref[...], al_ref[...],
                         dsk_ref[...], ng_ref[...], hs_ref[0], hs_ref[1])
        dxs, dbm, dcm, ddt, dz, ddtb, dal, ddsk, dng, dh0, dh1 = vjp((dy_ref[...], (dh_ref[g, 0], dh_ref[g, 1])))
        dz_ref[...] = dz.astype(dz_ref.dtype)
        lanes = pl.ds(pl.multiple_of(g * gw, gw), gw)
        dng_ref[:, lanes] = dng_ref[:, lanes] + dng
        dxs_ref[...] = dxs
        dbm_ref[...] = dbm
        dcm_ref[...] = dcm
        ddt_ref[...] += ddt
        ddtb_ref[...] += ddtb
        dal_ref[...] += dal
        ddsk_ref[...] += ddsk
        dh_ref[g, 0] = dh0
        dh_ref[g, 1] = dh1

    return pl.pallas_call(
        body, name=name, grid=(nb, SSD_G), in_specs=[xs_s, bm_s, cm_s, dt_s, y_s, par_s, par_s, par_s, ng_s, hs_s, y_s],
        out_specs=[y_s, n_s, n_s, ddt_s, par_s, par_s, par_s, y_s, dng_s],
        out_shape=[jax.ShapeDtypeStruct((T, SSD_DI), F32), jax.ShapeDtypeStruct((T, SSD_GN), F32),
                   jax.ShapeDtypeStruct((T, SSD_GN), F32), jax.ShapeDtypeStruct((T, LANE), F32),
                   jax.ShapeDtypeStruct((1, LANE), F32), jax.ShapeDtypeStruct((1, LANE), F32),
                   jax.ShapeDtypeStruct((1, LANE), F32), jax.ShapeDtypeStruct((T, SSD_DI), BF16),
                   jax.ShapeDtypeStruct((1, SSD_DI), F32)],
        scratch_shapes=[pltpu.VMEM((SSD_G, 2, 2 * SSD_HD, SSD_N), F32)], compiler_params=_cparams(2),
    )(xbc, xbc, xbc, proj, proj, dtb, alog, dsk, ng, hs, dy)


def _s5_param_f(log_dt, a_re, a_im, bre_t, bim_t, cim, cdl):
    n = S5_NG * S5_GS
    r, c = _iota2((n, S5_NG), 0), _iota2((n, S5_NG), 1)
    E = ((r // S5_GS) == c).astype(F32)
    rt, ct = _iota2((S5_NG, n), 0), _iota2((S5_NG, n), 1)
    Et = ((ct // S5_GS) == rt).astype(F32)
    step = jnp.exp(log_dt)
    mag = jnp.exp(step * a_re)
    abr = mag * jnp.cos(step * a_im)
    abi = mag * jnp.sin(step * a_im)
    den = a_re * a_re + a_im * a_im
    nr, ni = abr - 1.0, abi
    fr = (nr * a_re + ni * a_im) / den
    fi = (ni * a_re - nr * a_im) / den
    Fr, Fi = cdl(E, Et, fr), cdl(E, Et, fi)
    bbr = Fr * bre_t - Fi * bim_t
    bbi = Fr * bim_t + Fi * bre_t
    return abr, abi, bbr, bbi, -cim


def _whole(a):
    return pl.BlockSpec(a.shape, lambda: (0,) * a.ndim)


def s5_param_fwd(args, *, name):
    def body(*refs):
        res = _s5_param_f(*[r[...] for r in refs[:6]], lambda c, ct, x: _hdot(c, x))
        for o, v in zip(refs[6:], res):
            o[...] = v

    shapes = [(S5_NG, S5_P), (S5_NG, S5_P)] + [(S5_NG * S5_GS, S5_P)] * 3
    return pl.pallas_call(
        body, name=name, in_specs=[_whole(a) for a in args], out_specs=[pl.BlockSpec(s, lambda: (0, 0)) for s in shapes],
        out_shape=[jax.ShapeDtypeStruct(s, F32) for s in shapes],
        compiler_params=pltpu.CompilerParams(vmem_limit_bytes=VMEM_LIMIT),
    )(*args)


def s5_param_bwd(args, cts, *, name):
    def body(*refs):
        fn = lambda *a: _s5_param_f(*a, cdot_left)
        _, vjp = jax.vjp(fn, *[r[...] for r in refs[:6]])
        grads = vjp(tuple(r[...] for r in refs[6:11]))
        for o, v in zip(refs[11:], grads):
            o[...] = v

    return pl.pallas_call(
        body, name=name, in_specs=[_whole(a) for a in list(args) + list(cts)],
        out_specs=[_whole(a) for a in args], out_shape=[jax.ShapeDtypeStruct(a.shape, F32) for a in args],
        compiler_params=pltpu.CompilerParams(vmem_limit_bytes=VMEM_LIMIT),
    )(*args, *cts)


_S5_W = S5_BLK * S5_P


def _cmul_add(xr, xi, pr, pi, sr, si):
    return xr + (pr * sr - pi * si), xi + (pr * si + pi * sr)


def _s5_powers(ar, ai):
    pw = [(ar, ai)]
    for _ in range(7):
        qr, qi = pw[-1]
        pw.append((qr * ar - qi * ai, qr * ai + qi * ar))
    return pw


def s5_scan_fwd(u, wb, a_re, a_im, wc, *, name):
    T = u.shape[0]
    bt = min(T, 256)
    nb = T // bt

    def body(u_ref, wb_ref, ar_ref, ai_ref, wc_ref, x_ref, y_ref, bu_ref, carry_ref):
        @pl.when(pl.program_id(1) == 0)
        def _():
            carry_ref[...] = jnp.zeros_like(carry_ref)

        bu_ref[...] = _dot_raw(u_ref[...], wb_ref[...], "nn")
        ar, ai = ar_ref[...], ai_ref[...]
        pw = _s5_powers(ar, ai)
        pwr = jnp.concatenate([p[0] for p in pw], axis=0)
        pwi = jnp.concatenate([p[1] for p in pw], axis=0)
        rin = _iota2((8, _S5_W), 0)
        cr, ci = carry_ref[0:1, :], carry_ref[1:2, :]
        for t in range(bt // 8):
            sl = slice(8 * t, 8 * t + 8)
            xr, xi = bu_ref[sl, :_S5_W], bu_ref[sl, _S5_W:]
            for s in (1, 2, 4):
                m = rin >= s
                sr = jnp.where(m, pltpu.roll(xr, s, 0), 0.0)
                si = jnp.where(m, pltpu.roll(xi, s, 0), 0.0)
                xr, xi = _cmul_add(xr, xi, *pw[s - 1], sr, si)
            xr, xi = _cmul_add(xr, xi, pwr, pwi, cr, ci)
            x_ref[sl, :_S5_W] = xr
            x_ref[sl, _S5_W:] = xi
            cr, ci = xr[7:8, :], xi[7:8, :]
        carry_ref[0:1, :] = cr
        carry_ref[1:2, :] = ci
        y_ref[...] = _dot_raw(x_ref[...], wc_ref[...], "nn")

    nblk = S5_NG // S5_BLK
    blk = pl.BlockSpec((bt, 2 * _S5_W), lambda g, t: (t, g))
    col = pl.BlockSpec((bt, LANE), lambda g, t: (t, g))
    a_s = pl.BlockSpec((None, 1, _S5_W), lambda g, t: (g, 0, 0))
    wb_s = pl.BlockSpec((None, LANE, 2 * _S5_W), lambda g, t: (g, 0, 0))
    wc_s = pl.BlockSpec((None, 2 * _S5_W, LANE), lambda g, t: (g, 0, 0))
    return pl.pallas_call(
        body, name=name, grid=(nblk, nb), in_specs=[col, wb_s, a_s, a_s, wc_s], out_specs=[blk, col],
        out_shape=[jax.ShapeDtypeStruct((T, nblk * 2 * _S5_W), F32), jax.ShapeDtypeStruct((T, nblk * LANE), F32)],
        scratch_shapes=[pltpu.VMEM((bt, 2 * _S5_W), F32), pltpu.VMEM((8, _S5_W), F32)],
        compiler_params=_cparams(2),
    )(u, wb, a_re, a_im, wc)


def s5_scan_bwd(dy, x, u, wb, a_re, a_im, wc, *, name):
    T = dy.shape[0]
    bt = min(T, 256)
    nb = T // bt

    def body(dy_ref, x_ref, u_ref, wb_ref, ar_ref, ai_ref, wc_ref, du_ref, dwb_ref, dwc_ref, dar_ref, dai_ref,
             g_ref, lam_ref, carry_ref):
        @pl.when(pl.program_id(1) == 0)
        def _():
            carry_ref[...] = jnp.zeros_like(carry_ref)
            dar_ref[...] = jnp.zeros_like(dar_ref)
            dai_ref[...] = jnp.zeros_like(dai_ref)
            dwb_ref[...] = jnp.zeros_like(dwb_ref)
            dwc_ref[...] = jnp.zeros_like(dwc_ref)

        g_ref[...] = _dot_raw(dy_ref[...], wc_ref[...], "nt")
        pw = _s5_powers(ar_ref[...], -ai_ref[...])
        pwr = jnp.concatenate([p[0] for p in reversed(pw)], axis=0)
        pwi = jnp.concatenate([p[1] for p in reversed(pw)], axis=0)
        rin = _iota2((8, _S5_W), 0)
        cr, ci = carry_ref[0:1, :], carry_ref[1:2, :]
        acc_r = jnp.zeros((8, _S5_W), F32)
        acc_i = jnp.zeros((8, _S5_W), F32)
        for t in reversed(range(bt // 8)):
            sl = slice(8 * t, 8 * t + 8)
            lr, li = g_ref[sl, :_S5_W], g_ref[sl, _S5_W:]
            for s in (1, 2, 4):
                m = rin < 8 - s
                sr = jnp.where(m, pltpu.roll(lr, 8 - s, 0), 0.0)
                si = jnp.where(m, pltpu.roll(li, 8 - s, 0), 0.0)
                lr, li = _cmul_add(lr, li, *pw[s - 1], sr, si)
            lr, li = _cmul_add(lr, li, pwr, pwi, cr, ci)
            lam_ref[sl, :_S5_W] = lr
            lam_ref[sl, _S5_W:] = li
            nr = jnp.where(rin == 7, cr, pltpu.roll(lr, 7, 0))
            ni = jnp.where(rin == 7, ci, pltpu.roll(li, 7, 0))
            xr, xi = x_ref[sl, :_S5_W], x_ref[sl, _S5_W:]
            acc_r = acc_r + (xr * nr + xi * ni)
            acc_i = acc_i + (xr * ni - xi * nr)
            cr, ci = lr[0:1, :], li[0:1, :]
        carry_ref[0:1, :] = cr
        carry_ref[1:2, :] = ci
        dar_ref[...] += jnp.sum(acc_r, axis=0, keepdims=True)
        dai_ref[...] += jnp.sum(acc_i, axis=0, keepdims=True)
        lam = lam_ref[...]
        du_ref[...] = _dot_raw(lam, wb_ref[...], "nt")
        dwb_ref[...] += _dot_raw(u_ref[...], lam, "tn")
        dwc_ref[...] += _dot_raw(x_ref[...], dy_ref[...], "tn")

    nblk = S5_NG // S5_BLK
    blk = pl.BlockSpec((bt, 2 * _S5_W), lambda g, t: (nb - 1 - t, g))
    col = pl.BlockSpec((bt, LANE), lambda g, t: (nb - 1 - t, g))
    a_s = pl.BlockSpec((None, 1, _S5_W), lambda g, t: (g, 0, 0))
    wb_s = pl.BlockSpec((None, LANE, 2 * _S5_W), lambda g, t: (g, 0, 0))
    wc_s = pl.BlockSpec((None, 2 * _S5_W, LANE), lambda g, t: (g, 0, 0))
    return pl.pallas_call(
        body, name=name, grid=(nblk, nb), in_specs=[col, blk, col, wb_s, a_s, a_s, wc_s],
        out_specs=[col, wb_s, wc_s, a_s, a_s],
        out_shape=[jax.ShapeDtypeStruct((T, nblk * LANE), F32), jax.ShapeDtypeStruct((nblk, LANE, 2 * _S5_W), F32),
                   jax.ShapeDtypeStruct((nblk, 2 * _S5_W, LANE), F32), jax.ShapeDtypeStruct((nblk, 1, _S5_W), F32),
                   jax.ShapeDtypeStruct((nblk, 1, _S5_W), F32)],
        scratch_shapes=[pltpu.VMEM((bt, 2 * _S5_W), F32), pltpu.VMEM((bt, 2 * _S5_W), F32), pltpu.VMEM((8, _S5_W), F32)],
        compiler_params=_cparams(2),
    )(dy, x, u, wb, a_re, a_im, wc)


def _norm_bf16(h, g, name):
    return rowwise(f_rmsnorm, [g], [h], [(D, BF16)], bt=512, name=name)[0]


def _norm_bwd(h, g, cts, name):
    n = len(cts) - 1

    def f(p, r):
        y = _rms(r[0], p[0])
        return (y,) * n + (r[0],)

    (dg,), (dh,) = rowwise_vjp(f, [g], [h], cts, [F32], bt=256, name=name)
    return dh, dg


def ffn_fwd(h, g, w_gu, w_down, tag):
    hn = _norm_bf16(h, g, f"{tag}_norm")
    a, gu = ffn_up(hn, w_gu, name=f"{tag}_up")
    h2 = matmul(a, w_down[None], "nn", add=h, name=f"{tag}_down")
    return h2, (h, hn, gu, a)


def ffn_bwd(d, saved, g, w_gu, w_down, tag):
    h, hn, gu, a = saved
    dgu = ffn_dact(d, w_down, gu, name=f"{tag}_dact")
    dwd = matmul(a, d, "tn", name=f"{tag}_dwd")[0]
    dwgu = matmul(hn, dgu, "tn", name=f"{tag}_dwgu")[0]
    dh, dg = matmul_nt_norm_bwd(dgu, w_gu, h, g, d, name=f"{tag}_dhn")
    return dh, dg, dwgu, dwd


_GLA_NC = 4
_SSD_NU = 4


def gla_fwd(h, gm, w_in, w_a2, b_a, ng, w_out, tag):
    hn = _norm_bf16(h, gm, f"{tag}_norm")
    proj = matmul(hn, w_in[None], "nn", name=f"{tag}_in")
    alow = (proj, LANE, 2 * (GLA_QK + GLA_VD) // LANE)
    la = rowwise(f_gla_gate_in_fwd, [w_a2, b_a], [alow], [(GLA_QK, F32)], bt=512, name=f"{tag}_gate")[0]
    og, ss = gla_scan_fwd(proj, la, ng, nc=_GLA_NC, name=f"{tag}_scan")
    h2 = matmul(og, w_out[None], "nn", add=h, name=f"{tag}_proj")
    return h2, (h, hn, proj, la, ss, og)


def gla_bwd(d, saved, gm, w_in, w_a2, b_a, ng, w_out, tag):
    h, hn, proj, la, ss, og = saved
    dog = matmul(d, w_out[None], "nt", name=f"{tag}_dog")
    dwout = matmul(og, d, "tn", name=f"{tag}_dwout")[0]
    dq, dk, dv, dla, dr, dng = gla_scan_bwd(proj, la, ng, ss, dog, nc=_GLA_NC, name=f"{tag}_dscan")
    alow = (proj, LANE, 2 * (GLA_QK + GLA_VD) // LANE)
    (dwa2, dba), (dalow,) = rowwise_vjp(f_gla_gate_in, [w_a2, b_a], [alow], [dla], [BF16], bt=512, name=f"{tag}_dgate")
    dproj = jnp.concatenate([dq, dk, dv, dr, dalow], axis=1)
    dwin = matmul(hn, dproj, "tn", name=f"{tag}_dwin")[0]
    dh, dgm = matmul_nt_norm_bwd(dproj, w_in, h, gm, d, name=f"{tag}_dhn")
    return dh, dgm, dwin, dwa2[:GLA_RANK], dba, dng, dwout


def ssd_fwd(h, gm, w_in, conv_w, conv_b, dtb, alog, dsk, ng, w_out, tag):
    hn = _norm_bf16(h, gm, f"{tag}_norm")
    proj = matmul(hn, w_in[None], "nn", name=f"{tag}_in")
    xbc = ssd_conv_fwd(proj, conv_w, conv_b, name=f"{tag}_conv")
    yg, hs = ssd_scan_fwd(xbc, proj, dtb, alog, dsk, ng, nu=_SSD_NU, name=f"{tag}_scan")
    h2 = matmul(yg, w_out[None], "nn", add=h, name=f"{tag}_proj")
    return h2, (h, hn, proj, xbc, hs, yg)


def ssd_bwd(d, saved, gm, w_in, conv_w, conv_b, dtb, alog, dsk, ng, w_out, tag):
    h, hn, proj, xbc, hs, yg = saved
    dyg = matmul(d, w_out[None], "nt", name=f"{tag}_dyg")
    dwout = matmul(yg, d, "tn", name=f"{tag}_dwout")[0]
    dxs, dbm, dcm, ddt, ddtb, dal, ddsk, dz, dng = ssd_scan_bwd(xbc, proj, dtb, alog, dsk, ng, hs, dyg, nu=_SSD_NU,
                                                               name=f"{tag}_dscan")
    parts = [ssd_conv_bwd(proj, conv_w, conv_b, dout, col0, name=f"{tag}_dconv{k}")
             for k, (dout, col0) in enumerate(((dxs, 0), (dbm, SSD_DI), (dcm, SSD_DI + SSD_GN)))]
    dcw = jnp.concatenate([p[1] for p in parts], axis=1)
    dcb = jnp.concatenate([p[2] for p in parts], axis=1)
    dproj = jnp.concatenate([dz] + [p[0] for p in parts] + [ddt.astype(BF16)], axis=1)
    dwin = matmul(hn, dproj, "tn", name=f"{tag}_dwin")[0]
    dh, dgm = matmul_nt_norm_bwd(dproj, w_in, h, gm, d, name=f"{tag}_dhn")
    return (dh, dgm, dwin, dcw, dcb, ddtb[:, :SSD_H], dal[:, :SSD_H], ddsk[:, :SSD_H], dng, dwout)


_S5_NB = S5_NG // S5_BLK


def _s5_param_args(log_dt, a_re, a_im, b_re, b_im, c_im):
    n = S5_NG * S5_GS
    tr = lambda b: jnp.transpose(b, (0, 2, 1)).reshape(n, S5_P)
    return [log_dt.reshape(S5_NG, 1), a_re, a_im, tr(b_re), tr(b_im), c_im.reshape(n, S5_P)]


def _s5_blockdiag(t):
    nb, gl, a, b = t.shape
    eye = jnp.eye(gl, dtype=t.dtype)
    return (t[:, :, :, None, :] * eye[None, :, None, :, None]).reshape(nb, gl * a, gl * b)


def _s5_diag(t, a, b):
    nb = t.shape[0]
    gl = t.shape[1] // a
    eye = jnp.eye(gl, dtype=t.dtype)
    return jnp.sum(t.reshape(nb, gl, a, gl, b) * eye[None, :, None, :, None], axis=3)


def _s5_weights(bbr, bbi, c_re, cneg):
    sh = (_S5_NB, S5_BLK, S5_GS, S5_P)
    wb = jnp.concatenate([_s5_blockdiag(bbr.reshape(sh)), _s5_blockdiag(bbi.reshape(sh))], axis=2)
    tr = lambda cc: jnp.transpose(cc.reshape(sh), (0, 1, 3, 2))
    wc = jnp.concatenate([_s5_blockdiag(tr(c_re)), _s5_blockdiag(tr(cneg))], axis=1)
    return wb, wc


def s5_fwd(h, gm, prm, dsk, w_glu, tag):
    log_dt, a_re, a_im, b_re, b_im, c_re, c_im = prm
    hn = rowwise(f_rmsnorm, [gm], [h], [(D, F32)], bt=512, name=f"{tag}_norm")[0]
    pargs = _s5_param_args(log_dt, a_re, a_im, b_re, b_im, c_im)
    abr, abi, bbr, bbi, cneg = s5_param_fwd(pargs, name=f"{tag}_param")
    wb, wc = _s5_weights(bbr, bbi, c_re.reshape(S5_NG * S5_GS, S5_P), cneg)
    ar, ai = abr.reshape(_S5_NB, 1, _S5_W), abi.reshape(_S5_NB, 1, _S5_W)
    wb, wc = wb.astype(BF16), wc.astype(BF16)
    x, ycp = s5_scan_fwd(hn, wb, ar, ai, wc, name=f"{tag}_scan")
    yg = rowwise(f_s5_act, [dsk], [ycp, hn], [(D, BF16)], bt=512, name=f"{tag}_act")[0]
    vg = matmul(yg, w_glu[None], "nn", name=f"{tag}_glu")
    h2 = rowwise(f_glu_res, [], [vg, h], [(D, F32)], bt=512, name=f"{tag}_out")[0]
    return h2, (h, hn, pargs, wb, wc, ar, ai, x, ycp, yg, vg)


def s5_bwd(d, saved, gm, dsk, w_glu, tag):
    h, hn, pargs, wb, wc, ar, ai, x, ycp, yg, vg = saved
    _, (dvg,) = rowwise_vjp(f_glu, [], [vg], [d], [BF16], bt=256, name=f"{tag}_dout")
    dwglu = matmul(yg, dvg, "tn", name=f"{tag}_dwglu")[0]
    dyg = matmul(dvg, w_glu[None], "nt", name=f"{tag}_dyg")
    (ddsk,), (dycp, dhn1) = rowwise_vjp(f_s5_act, [dsk], [ycp, hn], [dyg], [F32, F32], bt=256, name=f"{tag}_dact")
    dhn2, dwb, dwc, dar, dai = s5_scan_bwd(dycp, x, hn, wb, ar, ai, wc, name=f"{tag}_dscan")
    dh, dgm = _norm_bwd(h, gm, [dhn1, dhn2, d], f"{tag}_dnorm")
    n = S5_NG * S5_GS
    half = S5_BLK * S5_P
    d_bbr = _s5_diag(dwb[:, :, :half], S5_GS, S5_P).reshape(n, S5_P)
    d_bbi = _s5_diag(dwb[:, :, half:], S5_GS, S5_P).reshape(n, S5_P)
    from_c = lambda t: jnp.transpose(_s5_diag(t, S5_P, S5_GS), (0, 1, 3, 2)).reshape(n, S5_P)
    d_cre = from_c(dwc[:, :half, :])
    d_cneg = from_c(dwc[:, half:, :])
    cts = [dar.reshape(S5_NG, S5_P), dai.reshape(S5_NG, S5_P), d_bbr, d_bbi, d_cneg]
    dlog, dare, daim, dbre_t, dbim_t, dcim = s5_param_bwd(pargs, cts, name=f"{tag}_dparam")
    untr = lambda t: jnp.transpose(t.reshape(S5_NG, S5_GS, S5_P), (0, 2, 1))
    grads = (dlog.reshape(S5_NG), dare, daim, untr(dbre_t), untr(dbim_t),
             d_cre.reshape(S5_NG, S5_GS, S5_P), dcim.reshape(S5_NG, S5_GS, S5_P))
    return dh, dgm, grads, ddsk, dwglu


def _pad_last(w, n):
    return jnp.pad(w, [(0, 0)] * (w.ndim - 1) + [(0, n - w.shape[-1])])


_BIG = ("gla_w_in", "gla_w_out", "ssd_w_in", "ssd_w_out", "s5_w_glu", "ffn_w_gu", "ffn_w_down")


def interleave_gu(w):
    q = w.shape[-1] // 4
    return jnp.concatenate([w[..., :q], w[..., 2 * q:3 * q], w[..., q:2 * q], w[..., 3 * q:]], axis=-1)


def local_step(x, target, W, later_weights=None, later_grads=None, ffn0_grads=None, ffn0_weights=None):
    f32 = lambda a: a.astype(F32)
    row = lambda a: f32(a).reshape(1, -1)

    def layer_args(i):
        m, j = i % 3, i // 3
        gm = row(W["norm_mix_g"][i])
        if m == 0:
            args = (gm, W["gla_w_in"][j], jnp.pad(f32(W["gla_w_a2"][j]), ((0, LANE - GLA_RANK), (0, 0))),
                    row(W["gla_b_a"][j]), row(W["gla_norm_g"][j]), W["gla_w_out"][j])
        elif m == 1:
            pl_ = lambda a: _pad_last(row(a), LANE)
            args = (gm, W["ssd_w_in"][j], f32(W["ssd_conv_w"][j]),
                    row(W["ssd_conv_b"][j]), pl_(W["ssd_dt_bias"][j]), pl_(W["ssd_a_log"][j]), pl_(W["ssd_d"][j]),
                    row(W["ssd_norm_g"][j]), W["ssd_w_out"][j])
        else:
            prm = tuple(f32(W[k][j]) for k in ("s5_log_dt", "s5_a_re", "s5_a_im", "s5_b_re", "s5_b_im", "s5_c_re", "s5_c_im"))
            args = (gm, prm, row(W["s5_d"][j]), W["s5_w_glu"][j])
        return m, j, args

    h = x
    saved, mixers, ffns = [], [], []
    for i in range(DEPTH):
        mixer = layer_args(i)
        mixers.append(mixer)
        m, j, args = mixer
        tag = f"l{i}_{('gla', 'ssd', 's5')[m]}"
        h, sm = (gla_fwd, ssd_fwd, s5_fwd)[m](h, *args, tag)
        if i == 0 and ffn0_weights is not None:
            W = {**W, **ffn0_weights(h)}
        ffn = (row(W["norm_ffn_g"][i]), W["ffn_w_gu"][i], W["ffn_w_down"][i])
        ffns.append(ffn)
        h, sf = ffn_fwd(h, *ffn, f"l{i}_ffn")
        saved.append((sm, sf))
        if i == 0 and later_weights is not None:
            W = {**W, **later_weights(h)}
    loss, dfg, d = loss_head(h, row(W["final_norm_g"]), target, name="loss_head")

    G = {k: [None] * len(v) for k, v in W.items() if k != "final_norm_g"}
    G["final_norm_g"] = dfg.reshape(D)
    for i in reversed(range(DEPTH)):
        m, j, args = mixers[i]
        sm, sf = saved[i]
        if i == 0 and later_grads is not None:
            zero = later_grads(G)
            ffns[0] = (ffns[0][0], ffns[0][1], ffns[0][2] + zero.astype(ffns[0][2].dtype))
        d, dg, dwgu, dwd = ffn_bwd(d, sf, *ffns[i], f"l{i}_ffn")
        G["norm_ffn_g"][i], G["ffn_w_gu"][i], G["ffn_w_down"][i] = dg.reshape(D), dwgu, dwd
        if i == 0 and ffn0_grads is not None:
            zero = ffn0_grads(G)
            args = args[:-1] + (args[-1] + zero.astype(args[-1].dtype),)
        tag = f"l{i}_{('gla', 'ssd', 's5')[m]}"
        if m == 0:
            d, dgm, dwin, dwa2, dba, dng, dwout = gla_bwd(d, sm, *args, tag)
            G["gla_w_in"][j], G["gla_w_a2"][j], G["gla_b_a"][j] = dwin, dwa2, dba.reshape(-1)
            G["gla_norm_g"][j], G["gla_w_out"][j] = dng.reshape(-1), dwout
        elif m == 1:
            d, dgm, dwin, dcw, dcb, ddtb, dal, ddsk, dng, dwout = ssd_bwd(d, sm, *args, tag)
            G["ssd_w_in"][j], G["ssd_conv_w"][j], G["ssd_conv_b"][j] = dwin, dcw, dcb.reshape(-1)
            G["ssd_dt_bias"][j], G["ssd_a_log"][j], G["ssd_d"][j] = ddtb.reshape(-1), dal.reshape(-1), ddsk.reshape(-1)
            G["ssd_norm_g"][j], G["ssd_w_out"][j] = dng.reshape(-1), dwout
        else:
            d, dgm, pg, ddsk, dwglu = s5_bwd(d, sm, args[0], args[2], args[3], tag)
            for k, v in zip(("s5_log_dt", "s5_a_re", "s5_a_im", "s5_b_re", "s5_b_im", "s5_c_re", "s5_c_im"), pg):
                G[k][j] = v
            G["s5_d"][j], G["s5_w_glu"][j] = ddsk.reshape(-1), dwglu
        G["norm_mix_g"][i] = dgm.reshape(D)
    grads = {k: (v if k == "final_norm_g" or k in _BIG else jnp.stack(v)) for k, v in G.items()}
    return loss, d, grads


_MESH = pl.DeviceIdType.MESH
_ANY = pl.BlockSpec(memory_space=pl.ANY)
_DMA = pltpu.SemaphoreType.DMA
_ROWS_ALIGN = 1024


def _place():
    return lax.axis_index("x"), lax.axis_index("y"), lax.axis_index("c")


def _other_chips(x, y):
    return [(1 - x, y), (x, 1 - y), (1 - x, 1 - y)]


def _remote(src, dst, send_sems, recv_sems, k, to):
    return pltpu.make_async_remote_copy(src_ref=src, dst_ref=dst, send_sem=send_sems.at[k], recv_sem=recv_sems.at[k],
                                        device_id=to, device_id_type=_MESH)


def gather_shards(loc, *, name):
    def body(in_ref, out_ref, send_sems, recv_sems, local_sem):
        x, y, c = _place()
        me, sibling = (x, y, c), (x, y, 1 - c)
        chips = _other_chips(x, y)

        def half(px, py, hc):
            return out_ref.at[2 * px + py, hc]

        mine = pltpu.make_async_copy(in_ref, out_ref.at[2 * x + y], local_sem)
        mine.start()
        first = [_remote(in_ref.at[c], half(x, y, c), send_sems, recv_sems, j, (*chip, c)) for j, chip in enumerate(chips)]
        for cp in first:
            cp.start()
        passed = [_remote(half(*chip, c), half(*chip, c), send_sems, recv_sems, 3 + j, sibling) for j, chip in enumerate(chips)]
        for j, chip in enumerate(chips):
            _remote(in_ref.at[c], half(*chip, c), send_sems, recv_sems, j, me).wait_recv()
            passed[j].start()
        for j, chip in enumerate(chips):
            _remote(in_ref.at[c], half(*chip, 1 - c), send_sems, recv_sems, 3 + j, me).wait_recv()
        for cp in first + passed:
            cp.wait_send()
        mine.wait()

    return pl.pallas_call(
        body, name=name, in_specs=[_ANY], out_specs=_ANY,
        out_shape=jax.ShapeDtypeStruct((4,) + loc.shape, loc.dtype),
        scratch_shapes=[_DMA((6,)), _DMA((6,)), _DMA(())],
    )(loc)


def _pos(px, py, perm):
    return 2 * py + px if perm else 2 * px + py


def _part(ref, kind, p, loc):
    if kind == "lead":
        return ref.at[p]
    return ref.at[:, pl.ds(pl.multiple_of(p * loc, LANE), loc)]


def _rows(ref, h, hr):
    return ref.at[pl.ds(h * hr, hr)]


def _rows_block(hr, width):
    return max(b for b in range(16, hr + 1, 16) if hr % b == 0 and (b * width <= (1 << 19) or b == 16))


def gather_big(locs, kinds, *, name):
    n = len(locs)

    def body(*refs):
        ins, outs = refs[:n], refs[n:2 * n]
        send_sems, recv_sems = refs[2 * n + 1:]
        refs[2 * n][...] = jnp.zeros_like(refs[2 * n])
        x, y, c = _place()
        me, sibling = (x, y, c), (x, y, 1 - c)
        chips = _other_chips(x, y)

        def half(i, px, py, h):
            (kind, perm), (rows, loc) = kinds[i], locs[i].shape
            return _rows(_part(outs[i], kind, _pos(px, py, perm), loc), h, rows // 2)

        sends = []
        for i in range(n):
            (kind, perm), (rows, loc) = kinds[i], locs[i].shape
            own = _part(outs[i], kind, _pos(x, y, perm), loc)
            sends.append(_remote(ins[i], own, send_sems, recv_sems, 6 * n + i, sibling))
            sends[-1].start()
            for j, chip in enumerate(chips):
                sends.append(_remote(_rows(ins[i], c, rows // 2), half(i, x, y, c), send_sems, recv_sems, 6 * i + j, (*chip, c)))
                sends[-1].start()
        for i in range(n):
            hr = locs[i].shape[0] // 2
            for j, chip in enumerate(chips):
                _remote(_rows(ins[i], c, hr), half(i, *chip, c), send_sems, recv_sems, 6 * i + j, me).wait_recv()
                sends.append(_remote(half(i, *chip, c), half(i, *chip, c), send_sems, recv_sems, 6 * i + 3 + j, sibling))
                sends[-1].start()
        for i in range(n):
            (kind, perm), (rows, loc) = kinds[i], locs[i].shape
            for j, chip in enumerate(chips):
                _remote(_rows(ins[i], c, rows // 2), half(i, *chip, 1 - c), send_sems, recv_sems, 6 * i + 3 + j, me).wait_recv()
            _remote(ins[i], _part(outs[i], kind, _pos(x, y, perm), loc), send_sems, recv_sems, 6 * n + i, me).wait_recv()
        for cp in sends:
            cp.wait_send()

    def out_shape(a, kind):
        rows, loc = a.shape
        return jax.ShapeDtypeStruct((4, rows, loc) if kind == "lead" else (rows, 4 * loc), a.dtype)

    outs = pl.pallas_call(
        body, name=name, in_specs=[_ANY] * n, out_specs=[_ANY] * n + [pl.BlockSpec(memory_space=pltpu.VMEM)],
        out_shape=[out_shape(a, k[0]) for a, k in zip(locs, kinds)] + [jax.ShapeDtypeStruct((8, LANE), F32)],
        scratch_shapes=[_DMA((7 * n,)), _DMA((7 * n,))],
    )(*locs)
    return list(outs[:n]), outs[n][0, 0]


_HBM = pl.BlockSpec(memory_space=pltpu.HBM)
_SEM = pl.BlockSpec(memory_space=pltpu.SEMAPHORE)
_EFFECT = pltpu.SideEffectType.DATAFLOW_SIDE_EFFECTING


def _in_hbm(a):
    return pltpu.with_memory_space_constraint(a, pltpu.HBM)


def _gather_ici_copies(ins, lands, kinds, shapes, send_sems, recv_sems):
    x, y, c = _place()
    sends, arrivals = [], []
    for i, ((kind, perm), (rows, loc)) in enumerate(zip(kinds, shapes)):
        hr = rows // 2
        mine = _part(lands[i], kind, _pos(x, y, perm), loc)
        sends.append(_remote(ins[i], mine, send_sems, recv_sems, 4 * i + 3, (x, y, 1 - c)))
        arrivals.append(_remote(ins[i], mine, send_sems, recv_sems, 4 * i + 3, (x, y, c)))
        for j, (px, py) in enumerate(_other_chips(x, y)):
            sends.append(_remote(_rows(ins[i], c, hr), _rows(mine, c, hr), send_sems, recv_sems, 4 * i + j, (px, py, c)))
            theirs = _rows(_part(lands[i], kind, _pos(px, py, perm), loc), c, hr)
            arrivals.append(_remote(_rows(ins[i], c, hr), theirs, send_sems, recv_sems, 4 * i + j, (x, y, c)))
    return sends, arrivals


def gather_start(locs, kinds, *, name):
    n = len(locs)
    shapes = [a.shape for a in locs]

    def land_shape(a, kind):
        rows, loc = a.shape
        return (4, rows, loc) if kind == "lead" else (rows, 4 * loc)

    def body(*refs):
        sends, _ = _gather_ici_copies(refs[:n], refs[n:2 * n], kinds, shapes, refs[2 * n], refs[2 * n + 1])
        for cp in sends:
            cp.start()
        refs[-1][...] = jnp.zeros_like(refs[-1])

    lands = [lax.empty(land_shape(a, k[0]), a.dtype) for a, k in zip(locs, kinds)]
    outs = pl.pallas_call(
        body, name=name, in_specs=[_HBM] * (2 * n), out_specs=[_SEM, _SEM] + [_HBM] * (2 * n) + [pl.BlockSpec(memory_space=pltpu.VMEM)],
        out_shape=[_DMA((4 * n,)), _DMA((4 * n,))] + [pltpu.HBM(a.shape, a.dtype) for a in locs]
        + [pltpu.HBM(l.shape, l.dtype) for l in lands] + [jax.ShapeDtypeStruct((8, LANE), F32)],
        input_output_aliases={i: 2 + i for i in range(2 * n)},
        compiler_params=pltpu.CompilerParams(has_side_effects=_EFFECT),
    )(*[_in_hbm(a) for a in locs], *[_in_hbm(l) for l in lands])
    return outs[0], outs[1], list(outs[2:2 + n]), list(outs[2 + n:2 + 2 * n]), outs[-1][0, 0]


def gather_wait(send_sems, recv_sems, locs, lands, kinds, after, *, name):
    n = len(locs)
    shapes = [a.shape for a in locs]

    def body(*refs):
        sends, arrivals = _gather_ici_copies(refs[:n], refs[n:2 * n], kinds, shapes, refs[2 * n], refs[2 * n + 1])
        for cp in sends:
            cp.wait_send()
        for cp in arrivals:
            cp.wait_recv()

    outs = pl.pallas_call(
        body, name=name, in_specs=[_HBM] * (2 * n) + [_SEM, _SEM, _ANY], out_specs=[_HBM] * (2 * n),
        out_shape=[pltpu.HBM(a.shape, a.dtype) for a in locs] + [pltpu.HBM(l.shape, l.dtype) for l in lands],
        input_output_aliases={i: i for i in range(2 * n)},
        compiler_params=pltpu.CompilerParams(has_side_effects=_EFFECT),
    )(*locs, *lands, send_sems, recv_sems, after)
    return list(outs[n:])


def gather_finish(lands, kinds, shapes, *, name):
    n = len(lands)

    def body(*refs):
        bufs = refs[n:2 * n]
        send_sems, recv_sems = refs[2 * n:]
        x, y, c = _place()
        sends = []
        for i, ((kind, perm), (rows, loc)) in enumerate(zip(kinds, shapes)):
            for j, (px, py) in enumerate(_other_chips(x, y)):
                part = _part(bufs[i], kind, _pos(px, py, perm), loc)
                sends.append(_remote(_rows(part, c, rows // 2), _rows(part, c, rows // 2), send_sems, recv_sems, 3 * i + j, (x, y, 1 - c)))
                sends[-1].start()
        for i, ((kind, perm), (rows, loc)) in enumerate(zip(kinds, shapes)):
            for j, (px, py) in enumerate(_other_chips(x, y)):
                part = _part(bufs[i], kind, _pos(px, py, perm), loc)
                _remote(_rows(part, c, rows // 2), _rows(part, 1 - c, rows // 2), send_sems, recv_sems, 3 * i + j, (x, y, c)).wait_recv()
        for cp in sends:
            cp.wait_send()

    return list(pl.pallas_call(
        body, name=name, in_specs=[_ANY] * n, out_specs=[_ANY] * n,
        out_shape=[jax.ShapeDtypeStruct(l.shape, l.dtype) for l in lands],
        input_output_aliases={i: i for i in range(n)}, scratch_shapes=[_DMA((3 * n,)), _DMA((3 * n,))],
    )(*lands))


def _scatter_copies(qs, lands, kinds, locs, send_sems, recv_sems):
    x, y, c = _place()
    sends, arrivals = [], []
    for i, (kind, perm) in enumerate(kinds):
        for j, (px, py) in enumerate(_other_chips(x, y)):
            src = _part(qs[i], kind, _pos(px, py, perm), locs[i])
            sends.append(_remote(src, lands[i].at[j], send_sems, recv_sems, 3 * i + j, (px, py, c)))
            arrivals.append(_remote(src, lands[i].at[j], send_sems, recv_sems, 3 * i + j, (x, y, c)))
    return sends, arrivals


def _scatter_land(q, kind, loc):
    return (3, q.shape[1] if kind == "lead" else q.shape[0], loc)


def scatter_start(qs, kinds, locs, *, name):
    n = len(qs)

    def body(*refs):
        sends, _ = _scatter_copies(refs[:n], refs[n:2 * n], kinds, locs, refs[2 * n], refs[2 * n + 1])
        for cp in sends:
            cp.start()
        refs[-1][...] = jnp.zeros_like(refs[-1])

    lands = [lax.empty(_scatter_land(q, k[0], l), q.dtype) for q, k, l in zip(qs, kinds, locs)]
    outs = pl.pallas_call(
        body, name=name, in_specs=[_HBM] * (2 * n), out_specs=[_SEM, _SEM] + [_HBM] * (2 * n) + [pl.BlockSpec(memory_space=pltpu.VMEM)],
        out_shape=[_DMA((3 * n,)), _DMA((3 * n,))] + [pltpu.HBM(q.shape, q.dtype) for q in qs]
        + [pltpu.HBM(l.shape, l.dtype) for l in lands] + [jax.ShapeDtypeStruct((8, LANE), F32)],
        input_output_aliases={i: 2 + i for i in range(2 * n)},
        compiler_params=pltpu.CompilerParams(has_side_effects=_EFFECT),
    )(*[_in_hbm(q) for q in qs], *[_in_hbm(l) for l in lands])
    return outs[0], outs[1], list(outs[2:2 + n]), list(outs[2 + n:2 + 2 * n]), outs[-1][0, 0]


def scatter_wait(send_sems, recv_sems, qs, lands, kinds, locs, after, *, name):
    n = len(qs)

    def body(*refs):
        sends, arrivals = _scatter_copies(refs[:n], refs[n:2 * n], kinds, locs, refs[2 * n], refs[2 * n + 1])
        for cp in sends:
            cp.wait_send()
        for cp in arrivals:
            cp.wait_recv()

    outs = pl.pallas_call(
        body, name=name, in_specs=[_HBM] * (2 * n) + [_SEM, _SEM, _ANY], out_specs=[_HBM] * (2 * n),
        out_shape=[pltpu.HBM(q.shape, q.dtype) for q in qs] + [pltpu.HBM(l.shape, l.dtype) for l in lands],
        input_output_aliases={i: i for i in range(2 * n)},
        compiler_params=pltpu.CompilerParams(has_side_effects=_EFFECT),
    )(*qs, *lands, send_sems, recv_sems, after)
    return list(outs[:n]), list(outs[n:])


def _pair_swap_copies(ins, lands, kinds, send_sems, recv_sems):
    x, y, c = _place()
    sends, arrivals = [], []
    for i, (kind, _) in enumerate(kinds):
        if kind == "lead":
            hr = ins[i].shape[1] // 2
            src = ins[i].at[:, pl.ds((1 - c) * hr, hr)]
        else:
            src = _rows(ins[i], 1 - c, ins[i].shape[0] // 2)
        sends.append(_remote(src, lands[i], send_sems, recv_sems, i, (x, y, 1 - c)))
        arrivals.append(_remote(src, lands[i], send_sems, recv_sems, i, (x, y, c)))
    return sends, arrivals


def _pair_swap_land(a, kind):
    s = a.shape
    return (4, s[1] // 2, s[2]) if kind == "lead" else (s[0] // 2, s[1])


def pair_swap_start(ps, kinds, *, name):
    n = len(ps)

    def body(*refs):
        sends, _ = _pair_swap_copies(refs[:n], refs[n:2 * n], kinds, refs[2 * n], refs[2 * n + 1])
        for cp in sends:
            cp.start()
        refs[-1][...] = jnp.zeros_like(refs[-1])

    lands = [lax.empty(_pair_swap_land(p, k[0]), p.dtype) for p, k in zip(ps, kinds)]
    outs = pl.pallas_call(
        body, name=name, in_specs=[_HBM] * (2 * n), out_specs=[_SEM, _SEM] + [_HBM] * (2 * n) + [pl.BlockSpec(memory_space=pltpu.VMEM)],
        out_shape=[_DMA((n,)), _DMA((n,))] + [pltpu.HBM(p.shape, p.dtype) for p in ps]
        + [pltpu.HBM(l.shape, l.dtype) for l in lands] + [jax.ShapeDtypeStruct((8, LANE), F32)],
        input_output_aliases={i: 2 + i for i in range(2 * n)},
        compiler_params=pltpu.CompilerParams(has_side_effects=_EFFECT),
    )(*[_in_hbm(p) for p in ps], *[_in_hbm(l) for l in lands])
    return outs[0], outs[1], list(outs[2:2 + n]), list(outs[2 + n:2 + 2 * n]), outs[-1][0, 0]


def pair_swap_wait(send_sems, recv_sems, ps, lands, kinds, after, *, name):
    n = len(ps)

    def body(*refs):
        sends, arrivals = _pair_swap_copies(refs[:n], refs[n:2 * n], kinds, refs[2 * n], refs[2 * n + 1])
        for cp in sends:
            cp.wait_send()
        for cp in arrivals:
            cp.wait_recv()

    outs = pl.pallas_call(
        body, name=name, in_specs=[_HBM] * (2 * n) + [_SEM, _SEM, _ANY], out_specs=[_HBM] * (2 * n),
        out_shape=[pltpu.HBM(p.shape, p.dtype) for p in ps] + [pltpu.HBM(l.shape, l.dtype) for l in lands],
        input_output_aliases={i: i for i in range(2 * n)},
        compiler_params=pltpu.CompilerParams(has_side_effects=_EFFECT),
    )(*ps, *lands, send_sems, recv_sems, after)
    return list(outs[:n]), list(outs[n:])


def pair_swap(ps, kinds, *, name):
    n = len(ps)

    def body(*refs):
        ins, outs = refs[:n], refs[n:2 * n]
        send_sems, recv_sems = refs[2 * n:]
        x, y, c = _place()
        cps = []
        for i in range(n):
            if kinds[i][0] == "lead":
                hr = ps[i].shape[1] // 2
                src = ins[i].at[:, pl.ds((1 - c) * hr, hr)]
            else:
                hr = ps[i].shape[0] // 2
                src = _rows(ins[i], 1 - c, hr)
            cps.append(_remote(src, outs[i], send_sems, recv_sems, i, (x, y, 1 - c)))
            cps[-1].start()
        for cp in cps:
            cp.wait()

    def out_shape(a, kind):
        s = a.shape
        return jax.ShapeDtypeStruct((4, s[1] // 2, s[2]) if kind == "lead" else (s[0] // 2, s[1]), a.dtype)

    return pl.pallas_call(
        body, name=name, in_specs=[_ANY] * n, out_specs=[_ANY] * n,
        out_shape=[out_shape(a, k[0]) for a, k in zip(ps, kinds)], scratch_shapes=[_DMA((n,)), _DMA((n,))],
    )(*ps)


def pair_add(p, got, c_arr, kind, *, name):
    if kind == "lead":
        _, hr, cols = got.shape
        br = _rows_block(hr, cols)
        nb = hr // br
        grid = (4, nb)
        p_spec = pl.BlockSpec((None, br, cols), lambda s, i, cr: (s, cr[0] * nb + i, 0))
        g_spec = pl.BlockSpec((None, br, cols), lambda s, i, cr: (s, i, 0))
    else:
        hr, w = got.shape
        br = _rows_block(hr, w)
        nb = hr // br
        grid = (nb,)
        p_spec = pl.BlockSpec((br, w), lambda i, cr: (cr[0] * nb + i, 0))
        g_spec = pl.BlockSpec((br, w), lambda i, cr: (i, 0))

    def body(c_ref, p_ref, g_ref, o_ref):
        o_ref[...] = (p_ref[...] + g_ref[...]).astype(o_ref.dtype)

    return pl.pallas_call(
        body, name=name, out_shape=jax.ShapeDtypeStruct(got.shape, BF16),
        grid_spec=pltpu.PrefetchScalarGridSpec(num_scalar_prefetch=1, grid=grid, in_specs=[p_spec, g_spec], out_specs=g_spec),
        compiler_params=_cparams(len(grid)),
    )(c_arr, p, got)


def chip_scatter(qs, kinds, locs, *, name):
    n = len(qs)

    def body(*refs):
        ins, outs = refs[:n], refs[n:2 * n]
        send_sems, recv_sems = refs[2 * n:]
        x, y, c = _place()
        cps = []
        for i in range(n):
            kind, perm = kinds[i]
            for j, (px, py) in enumerate(_other_chips(x, y)):
                cps.append(_remote(_part(ins[i], kind, _pos(px, py, perm), locs[i]), outs[i].at[j], send_sems, recv_sems,
                                   3 * i + j, (px, py, c)))
                cps[-1].start()
        for cp in cps:
            cp.wait()

    def out_shape(a, kind, loc):
        hr = a.shape[1] if kind == "lead" else a.shape[0]
        return jax.ShapeDtypeStruct((3, hr, loc), a.dtype)

    return pl.pallas_call(
        body, name=name, in_specs=[_ANY] * n, out_specs=[_ANY] * n,
        out_shape=[out_shape(a, k[0], l) for a, k, l in zip(qs, kinds, locs)],
        scratch_shapes=[_DMA((3 * n,)), _DMA((3 * n,))],
    )(*qs)


def chip_add(q, r, pos_arr, c_arr, kind, loc, *, name):
    _, hr, _ = r.shape
    br = _rows_block(hr, loc)
    nb = hr // br
    if kind == "lead":
        q_spec = pl.BlockSpec((None, br, loc), lambda i, pr, cr: (pr[0], i, 0))
    else:
        q_spec = pl.BlockSpec((br, loc), lambda i, pr, cr: (i, pr[0]))
    r_spec = pl.BlockSpec((3, br, loc), lambda i, pr, cr: (0, i, 0))
    o_spec = pl.BlockSpec((br, loc), lambda i, pr, cr: (cr[0] * nb + i, 0))

    def body(p_ref, c_ref, q_ref, r_ref, o_ref):
        acc = q_ref[...].astype(F32)
        for j in range(3):
            acc = acc + r_ref[j].astype(F32)
        o_ref[...] = acc

    return pl.pallas_call(
        body, name=name, out_shape=jax.ShapeDtypeStruct((2 * hr, loc), F32),
        grid_spec=pltpu.PrefetchScalarGridSpec(num_scalar_prefetch=2, grid=(nb,), in_specs=[q_spec, r_spec], out_specs=o_spec),
        compiler_params=_cparams(1),
    )(pos_arr, c_arr, q, r)


def share_rows(fs, *, name):
    n = len(fs)

    def body(*refs):
        bufs = refs[n:2 * n]
        send_sems, recv_sems = refs[2 * n:]
        x, y, c = _place()
        cps = []
        for i in range(n):
            hr = fs[i].shape[0] // 2
            cps.append(_remote(_rows(bufs[i], c, hr), _rows(bufs[i], c, hr), send_sems, recv_sems, i, (x, y, 1 - c)))
            cps[-1].start()
        for i, cp in enumerate(cps):
            hr = fs[i].shape[0] // 2
            _remote(_rows(bufs[i], c, hr), _rows(bufs[i], 1 - c, hr), send_sems, recv_sems, i, (x, y, c)).wait_recv()
            cp.wait_send()

    return pl.pallas_call(
        body, name=name, in_specs=[_ANY] * n, out_specs=[_ANY] * n,
        out_shape=[jax.ShapeDtypeStruct(f.shape, f.dtype) for f in fs],
        input_output_aliases={i: i for i in range(n)}, scratch_shapes=[_DMA((n,)), _DMA((n,))],
    )(*fs)


def _gather_all_copies(v_ref, land_ref, send_sems, recv_sems):
    x, y, c = _place()
    flip = lambda p, m: 1 - p if m else p
    idx = lambda p: 4 * p[0] + 2 * p[1] + p[2]
    sends, arrivals = [], []
    for k, m in enumerate(range(1, 8)):
        p = (flip(x, m & 4), flip(y, m & 2), flip(c, m & 1))
        sends.append(_remote(v_ref, land_ref.at[idx((x, y, c))], send_sems, recv_sems, k, p))
        arrivals.append(_remote(v_ref, land_ref.at[idx(p)], send_sems, recv_sems, k, (x, y, c)))
    return sends, arrivals


def gather_all_start(v, *, name):
    def body(v_ref, land_ref, send_sems, recv_sems, v_thru, land_thru, token):
        sends, _ = _gather_all_copies(v_ref, land_ref, send_sems, recv_sems)
        for cp in sends:
            cp.start()
        token[...] = jnp.zeros_like(token)

    land = jnp.zeros((8,) + v.shape, v.dtype)
    outs = pl.pallas_call(
        body, name=name, in_specs=[_HBM, _HBM], out_specs=[_SEM, _SEM, _HBM, _HBM, pl.BlockSpec(memory_space=pltpu.VMEM)],
        out_shape=[_DMA((7,)), _DMA((7,)), pltpu.HBM(v.shape, v.dtype), pltpu.HBM(land.shape, land.dtype),
                   jax.ShapeDtypeStruct((8, LANE), F32)],
        input_output_aliases={0: 2, 1: 3}, compiler_params=pltpu.CompilerParams(has_side_effects=_EFFECT),
    )(_in_hbm(v), _in_hbm(land))
    return outs[0], outs[1], outs[2], outs[3], outs[4][0, 0]


def gather_all_wait(send_sems, recv_sems, v, land, after, *, name):
    def body(v_ref, land_ref, send_sems, recv_sems, after_ref, v_dead, got_ref):
        sends, arrivals = _gather_all_copies(v_ref, land_ref, send_sems, recv_sems)
        for cp in sends:
            cp.wait_send()
        for cp in arrivals:
            cp.wait_recv()

    return pl.pallas_call(
        body, name=name, in_specs=[_HBM, _HBM, _SEM, _SEM, _ANY], out_specs=[_HBM, _HBM],
        out_shape=[pltpu.HBM(v.shape, v.dtype), pltpu.HBM(land.shape, land.dtype)],
        input_output_aliases={0: 0, 1: 1}, compiler_params=pltpu.CompilerParams(has_side_effects=_EFFECT),
    )(v, land, send_sems, recv_sems, after)[1]


def sum_slots(land, v, me_arr, *, name):
    n, R, L = land.shape
    br = _pick(R, _ROWS_ALIGN, 8)

    def body(me_ref, land_ref, v_ref, o_ref):
        acc = None
        for i in range(n):
            term = jnp.where(me_ref[0] == i, v_ref[...], land_ref[i])
            acc = term if acc is None else acc + term
        o_ref[...] = acc

    row = pl.BlockSpec((br, L), lambda i, me: (i, 0))
    return pl.pallas_call(
        body, name=name, out_shape=jax.ShapeDtypeStruct((R, L), land.dtype),
        grid_spec=pltpu.PrefetchScalarGridSpec(num_scalar_prefetch=1, grid=(R // br,),
                                               in_specs=[pl.BlockSpec((n, br, L), lambda i, me: (0, i, 0)), row], out_specs=row),
        compiler_params=_cparams(1),
    )(me_arr, land, v)


def adamw(w, g, m, v, *, name):
    shape = w.shape
    size = math.prod(shape)
    last = shape[-1]
    if last % LANE != 0 and size % LANE == 0 and size <= (1 << 20):
        last = LANE
    rows = size // last
    budget = (1 << 18) // last
    br = rows
    if rows > budget:
        br = max(c for c in range(8, budget + 1, 8) if rows % c == 0)
    v2 = lambda a: a.reshape(rows, last)

    def body(w_ref, g_ref, m_ref, v_ref, d_ref, nm_ref, nv_ref):
        gg = g_ref[...]
        nm = ADAM_B1 * m_ref[...] + (1.0 - ADAM_B1) * gg
        nv = ADAM_B2 * v_ref[...] + (1.0 - ADAM_B2) * (gg * gg)
        m_hat = nm / (1.0 - ADAM_B1 ** ADAM_STEP)
        v_hat = nv / (1.0 - ADAM_B2 ** ADAM_STEP)
        d_ref[...] = -ADAM_LR * (m_hat / (jnp.sqrt(v_hat) + ADAM_EPS) + ADAM_WD * w_ref[...])
        nm_ref[...] = nm
        nv_ref[...] = nv

    spec = pl.BlockSpec((br, last), lambda i: (i, 0))
    outs = pl.pallas_call(
        body, name=name, grid=(rows // br,), in_specs=[spec] * 4, out_specs=[spec] * 3,
        out_shape=[jax.ShapeDtypeStruct((rows, last), F32)] * 3, compiler_params=_cparams(1),
    )(v2(w), v2(g), v2(m), v2(v))
    return [o.reshape(shape) for o in outs]


_WEIGHTS = ["norm_mix_g", "norm_ffn_g", "gla_w_in", "gla_w_a2", "gla_b_a", "gla_norm_g", "gla_w_out", "ssd_w_in",
            "ssd_conv_w", "ssd_conv_b", "ssd_dt_bias", "ssd_a_log", "ssd_d", "ssd_norm_g", "ssd_w_out", "s5_log_dt",
            "s5_a_re", "s5_a_im", "s5_b_re", "s5_b_im", "s5_c_re", "s5_c_im", "s5_d", "s5_w_glu", "ffn_w_gu",
            "ffn_w_down", "final_norm_g"]
_SHARD_AXIS = {"gla_w_in": 2, "gla_w_a2": 2, "gla_b_a": 1, "gla_norm_g": 1, "gla_w_out": 1, "ssd_w_in": 2,
               "ssd_conv_w": 2, "ssd_w_out": 1, "s5_d": 1, "s5_w_glu": 2, "ffn_w_gu": 2, "ffn_w_down": 1}
_SMALL_SHARDED = [n for n in _WEIGHTS if n in _SHARD_AXIS and n not in _BIG]
_REPLICATED = [n for n in _WEIGHTS if n not in _SHARD_AXIS]
_BIG_KIND = {"gla_w_in": ("lead", False), "gla_w_out": ("lead", False), "ssd_w_in": ("lead", False),
             "ssd_w_out": ("lead", False), "s5_w_glu": ("cols", False), "ffn_w_gu": ("cols", True),
             "ffn_w_down": ("lead", False)}
_PADDED_IN = {"gla_w_in": GLA_INP, "ssd_w_in": SSD_INP}


def _to_rows(flat, parts=1):
    per = -(-flat.shape[0] // (parts * LANE * _ROWS_ALIGN)) * _ROWS_ALIGN
    flat = jnp.pad(flat, (0, parts * per * LANE - flat.shape[0]))
    return flat.reshape(parts, per, LANE)


def _big_layers(local):
    return [(n, j, local[n][j].reshape(-1, local[n].shape[-1])) for n in _BIG for j in range(local[n].shape[0])]


def _in_layer0(n, j):
    return j == 0 and n in ("gla_w_in", "gla_w_out", "ffn_w_gu", "ffn_w_down")


def _assemble(n, g):
    if n in _PADDED_IN:
        return jnp.concatenate([g[s] for s in range(4)] + [jnp.zeros((g.shape[1], _PADDED_IN[n] - 4 * g.shape[2]), BF16)], axis=1)
    if _BIG_KIND[n][0] == "lead":
        return g.reshape(4 * g.shape[1], g.shape[2])
    return g


def _is_gla0(n, j):
    return j == 0 and n in ("gla_w_in", "gla_w_out")


def _gather_first(local):
    layers = _big_layers(local)
    first = [l for l in layers if _is_gla0(l[0], l[1])]
    full = {n: [None] * local[n].shape[0] for n in _BIG}
    got, done = gather_big([w.astype(BF16) for _, _, w in first], [_BIG_KIND[n] for n, _, _ in first], name="gather_weights_first")
    for (n, j, _), g in zip(first, got):
        full[n][j] = _assemble(n, g)
    flat = jnp.concatenate([local[n].astype(F32).reshape(-1) for n in _SMALL_SHARDED])
    got = gather_shards(_to_rows(flat, 2), name="gather_small_weights").reshape(4, -1)
    off = 0
    for n in _SMALL_SHARDED:
        bs = local[n].shape
        sz = math.prod(bs)
        seg = got[:, off:off + sz].reshape((4,) + bs)
        off += sz
        ax = _SHARD_AXIS[n]
        full[n] = jnp.moveaxis(seg, 0, ax).reshape(bs[:ax] + (4 * bs[ax],) + bs[ax + 1:])
    pending = {}
    for tag, want in (("ffn0", _is_ffn0), ("later", lambda n, j: not _in_layer0(n, j))):
        group = [l for l in layers if want(l[0], l[1])]
        kinds = [_BIG_KIND[n] for n, _, _ in group]
        ops = [(w + done if k == 0 else w).astype(BF16) for k, (_, _, w) in enumerate(group)]
        send_sems, recv_sems, locs, lands, done = gather_start(ops, kinds, name=f"gather_weights_start_{tag}")
        pending[tag] = (group, kinds, send_sems, recv_sems, locs, lands)
    return full, pending, done


def _gather_rest(full, pending, after, tag):
    group, kinds, send_sems, recv_sems, locs, lands = pending
    lands = gather_wait(send_sems, recv_sems, locs, lands, kinds, after, name=f"gather_weights_wait_{tag}")
    lands = gather_finish(lands, kinds, [w.shape for _, _, w in group], name=f"gather_weights_finish_{tag}")
    out = {n: list(full[n]) for n in _BIG}
    for (n, j, _), g in zip(group, lands):
        out[n][j] = _assemble(n, g)
    return out


def _reduce_ops(grads, local, want):
    ops = []
    for n in _BIG:
        kind = _BIG_KIND[n]
        for j, g in enumerate(grads[n]):
            if not want(n, j):
                continue
            loc = local[n].shape[-1] if kind[0] == "cols" or n in _PADDED_IN else g.shape[1]
            if n in _PADDED_IN:
                g = jnp.stack([g[:, s * loc:(s + 1) * loc] for s in range(4)])
            elif kind[0] == "lead":
                g = g.reshape(4, g.shape[0] // 4, g.shape[1])
            ops.append((n, j, kind, loc, g))
    return ops


def _pair_sums(ops, c_arr, tag):
    gots = pair_swap([o[4] for o in ops], [o[2] for o in ops], name=f"reduce_pair_swap_{tag}")
    return [pair_add(o[4], got, c_arr, o[2][0], name=f"reduce_pair_add_{o[0]}{o[1]}") for o, got in zip(ops, gots)]


def _is_ffn0(n, j):
    return j == 0 and n in ("ffn_w_gu", "ffn_w_down")


def _reduce_start(grads, local, c, want, tag):
    ops = _reduce_ops(grads, local, want)
    c_arr = jnp.reshape(c, (1,)).astype(jnp.int32)
    qs = _pair_sums(ops, c_arr, tag)
    send_sems, recv_sems, qs, lands, zero = scatter_start(qs, [o[2] for o in ops], [o[3] for o in ops],
                                                          name=f"reduce_scatter_start_{tag}")
    return (ops, send_sems, recv_sems, qs, lands, tag), zero


def _reduce_swap_start(grads, local, c, want, tag):
    ops = _reduce_ops(grads, local, want)
    send_sems, recv_sems, ps, lands, zero = pair_swap_start([o[4] for o in ops], [o[2] for o in ops],
                                                            name=f"reduce_pair_swap_start_{tag}")
    return (ops, send_sems, recv_sems, ps, lands, tag), zero


def _reduce_scatter_after(pending, after, c):
    ops, send_sems, recv_sems, ps, lands, tag = pending
    ps, gots = pair_swap_wait(send_sems, recv_sems, ps, lands, [o[2] for o in ops], after, name=f"reduce_pair_swap_wait_{tag}")
    c_arr = jnp.reshape(c, (1,)).astype(jnp.int32)
    qs = [pair_add(p, got, c_arr, o[2][0], name=f"reduce_pair_add_{o[0]}{o[1]}") for o, p, got in zip(ops, ps, gots)]
    send_sems, recv_sems, qs, lands, zero = scatter_start(qs, [o[2] for o in ops], [o[3] for o in ops],
                                                          name=f"reduce_scatter_start_{tag}")
    return (ops, send_sems, recv_sems, qs, lands, tag), zero


def _reduce_big(grads, local, pendings, after, x, y, c):
    c_arr = jnp.reshape(c, (1,)).astype(jnp.int32)
    ops, qs, rs = [], [], []
    for ops_p, send_sems, recv_sems, qs_p, lands, tag in pendings:
        qs_p, rs_p = scatter_wait(send_sems, recv_sems, qs_p, lands, [o[2] for o in ops_p], [o[3] for o in ops_p], after,
                                  name=f"reduce_scatter_wait_{tag}")
        ops, qs, rs = ops + ops_p, qs + qs_p, rs + rs_p
    ops_f = _reduce_ops(grads, local, lambda n, j: _in_layer0(n, j) and not _is_ffn0(n, j))
    qs_f = _pair_sums(ops_f, c_arr, "first")
    s_sems, r_sems, qs_f, lands_f, zero = scatter_start(qs_f, [o[2] for o in ops_f], [o[3] for o in ops_f],
                                                        name="reduce_scatter_start_first")
    qs[0] = qs[0] + zero.astype(qs[0].dtype)
    red = _reduce_close(ops, qs, rs, x, y, c_arr, "later")
    done = {n: jnp.stack([red[(n, j)] for j in range(local[n].shape[0])]).reshape(local[n].shape)
            for n in _BIG if all((n, j) in red for j in range(local[n].shape[0]))}
    return done, red, (ops_f, s_sems, r_sems, qs_f, lands_f)


def _reduce_close(ops, qs, rs, x, y, c_arr, tag):
    fs = [chip_add(q, r, jnp.reshape(_pos(x, y, o[2][1]), (1,)).astype(jnp.int32), c_arr, o[2][0], o[3],
                   name=f"reduce_chip_add_{o[0]}{o[1]}") for o, q, r in zip(ops, qs, rs)]
    outs = share_rows(fs, name=f"reduce_share_{tag}")
    return {(o[0], o[1]): r for o, r in zip(ops, outs)}


def _reduce_big_first(pending, red, after, local, x, y, c):
    ops_f, s_sems, r_sems, qs_f, lands_f = pending
    qs_f, rs_f = scatter_wait(s_sems, r_sems, qs_f, lands_f, [o[2] for o in ops_f], [o[3] for o in ops_f], after,
                              name="reduce_scatter_wait_first")
    red = {**red, **_reduce_close(ops_f, qs_f, rs_f, x, y, jnp.reshape(c, (1,)).astype(jnp.int32), "first")}
    names = sorted({o[0] for o in ops_f})
    return {n: jnp.stack([red[(n, j)] for j in range(local[n].shape[0])]).reshape(local[n].shape) for n in names}


def _reduce_small_start(grads):
    names = _REPLICATED + _SMALL_SHARDED
    flat = jnp.concatenate([grads[n].astype(F32).reshape(-1) for n in names])
    n_el = flat.shape[0]
    rows = -(-n_el // (LANE * 8)) * 8
    v = jnp.pad(flat, (0, rows * LANE - n_el)).reshape(rows, LANE)
    outs = gather_all_start(v, name="reduce_small_start")
    return outs[:4], outs[4]


def _reduce_small(pending, after, grads, local, x, y, c):
    names = _REPLICATED + _SMALL_SHARDED
    send_sems, recv_sems, v, land = pending
    land = gather_all_wait(send_sems, recv_sems, v, land, after, name="reduce_small_wait")
    me = jnp.reshape(4 * x + 2 * y + c, (1,)).astype(jnp.int32)
    red = sum_slots(land, v, me, name="reduce_small_add").reshape(-1)
    out, off = {}, 0
    for n in names:
        sz = math.prod(grads[n].shape)
        g = red[off:off + sz].reshape(grads[n].shape)
        off += sz
        if n in _SHARD_AXIS:
            ax = _SHARD_AXIS[n]
            loc = local[n].shape[ax]
            g = lax.dynamic_slice_in_dim(g, (2 * x + y) * loc, loc, axis=ax)
        out[n] = g
    return out


def kernel(x, norm_mix_g, norm_ffn_g, gla_w_in, gla_w_a2, gla_b_a, gla_norm_g, gla_w_out, ssd_w_in, ssd_conv_w, ssd_conv_b, ssd_dt_bias, ssd_a_log, ssd_d, ssd_norm_g, ssd_w_out, s5_log_dt, s5_a_re, s5_a_im, s5_b_re, s5_b_im, s5_c_re, s5_c_im, s5_d, s5_w_glu, ffn_w_gu, ffn_w_down, final_norm_g, loss_target, m_norm_mix_g, m_norm_ffn_g, m_gla_w_in, m_gla_w_a2, m_gla_b_a, m_gla_norm_g, m_gla_w_out, m_ssd_w_in, m_ssd_conv_w, m_ssd_conv_b, m_ssd_dt_bias, m_ssd_a_log, m_ssd_d, m_ssd_norm_g, m_ssd_w_out, m_s5_log_dt, m_s5_a_re, m_s5_a_im, m_s5_b_re, m_s5_b_im, m_s5_c_re, m_s5_c_im, m_s5_d, m_s5_w_glu, m_ffn_w_gu, m_ffn_w_down, m_final_norm_g, v_norm_mix_g, v_norm_ffn_g, v_gla_w_in, v_gla_w_a2, v_gla_b_a, v_gla_norm_g, v_gla_w_out, v_ssd_w_in, v_ssd_conv_w, v_ssd_conv_b, v_ssd_dt_bias, v_ssd_a_log, v_ssd_d, v_ssd_norm_g, v_ssd_w_out, v_s5_log_dt, v_s5_a_re, v_s5_a_im, v_s5_b_re, v_s5_b_im, v_s5_c_re, v_s5_c_im, v_s5_d, v_s5_w_glu, v_ffn_w_gu, v_ffn_w_down, v_final_norm_g):
    given = dict(locals())
    local = {n: given[n] for n in _WEIGHTS}
    px, py, pc = _place()

    first, gathering, zero = _gather_first(local)
    full = dict(local)
    full.update(first)
    full["norm_mix_g"] = local["norm_mix_g"] + zero
    big = [first]

    def weights_of(tag):
        def arrived(h):
            big.append(_gather_rest(big[-1], gathering[tag], h, tag))
            return big[-1]
        return arrived

    swapping, reducing = [], []

    def later_grads(g):
        pending, zero = _reduce_swap_start(g, local, pc, lambda n, j: not _in_layer0(n, j), "later")
        swapping.append(pending)
        return zero

    def ffn0_grads(g):
        pending, zero = _reduce_scatter_after(swapping[0], g["ffn_w_down"][0], pc)
        reducing.append(pending)
        g["ffn_w_down"][0] = g["ffn_w_down"][0] + zero
        pending, zero = _reduce_start(g, local, pc, _is_ffn0, "ffn0")
        reducing.append(pending)
        return zero

    loss, grad_x, grads = local_step(x[0], loss_target[0], full, weights_of("later"), later_grads, ffn0_grads, weights_of("ffn0"))
    loss = lax.psum(loss, ("x", "y", "c"))

    small, zero = _reduce_small_start(grads)
    grads["gla_w_out"][0] = grads["gla_w_out"][0] + zero
    red, parts, first_pending = _reduce_big(grads, local, reducing, grad_x, px, py, pc)

    deltas, new_m, new_v = {}, {}, {}

    def update(n):
        deltas[n], new_m[n], new_v[n] = adamw(local[n], red[n], given["m_" + n], given["v_" + n], name=f"adamw_{n}")

    for n in list(red):
        update(n)
    red.update(_reduce_big_first(first_pending, parts, deltas["ffn_w_gu"], local, px, py, pc))
    red.update(_reduce_small(small, red["gla_w_out"], grads, local, px, py, pc))
    for n in _WEIGHTS:
        if n not in deltas:
            update(n)
    return (loss, grad_x[None], *[red[n] for n in _WEIGHTS], *[deltas[n] for n in _WEIGHTS],
            *[new_m[n] for n in _WEIGHTS], *[new_v[n] for n in _WEIGHTS])
```

```python
import functools
import math

import jax
import jax.numpy as jnp
from jax import lax
from jax.experimental import pallas as pl
from jax.experimental.pallas import tpu as pltpu

F32 = jnp.float32
BF16 = jnp.bfloat16

D = 1024
DEPTH = 4
CH = 64
EPS = 1e-6
GLA_H, GLA_DK, GLA_DV, GLA_RANK, GLA_TAU = 4, 128, 256, 16, 16.0
GLA_QK = GLA_H * GLA_DK
GLA_VD = GLA_H * GLA_DV
GLA_IN = 2 * GLA_QK + 2 * GLA_VD + GLA_RANK
GLA_INP = 3200
SSD_DI, SSD_HD, SSD_H, SSD_G, SSD_N, SSD_K = 2048, 64, 32, 8, 128, 4
SSD_GN = SSD_G * SSD_N
SSD_CONV = SSD_DI + 2 * SSD_GN
SSD_IN = SSD_DI + SSD_CONV + SSD_H
SSD_INP = 6272
S5_GS, S5_NG, S5_P = 16, 64, 64
S5_BLK = 8
FFN_H = 2816
LANE = 128
VMEM_LIMIT = 52 * 1024 * 1024
_MATMUL_VMEM = 40 * 1024 * 1024

ADAM_LR, ADAM_B1, ADAM_B2, ADAM_EPS, ADAM_WD, ADAM_STEP = 0.001, 0.9, 0.999, 1e-08, 0.01, 10

_ARB = "arbitrary"


def _cparams(n):
    return pltpu.CompilerParams(dimension_semantics=(_ARB,) * n, vmem_limit_bytes=VMEM_LIMIT)


def _pick(n, target, mult=LANE):
    best = None
    for c in range(mult, min(n, target) + 1, mult):
        if n % c == 0:
            best = c
    return best if best is not None else n


_DN = {"nn": (((1,), (0,)), ((), ())), "nt": (((1,), (1,)), ((), ())), "tn": (((0,), (0,)), ((), ()))}


def _dot_raw(a, b, form):
    return lax.dot_general(a.astype(BF16), b.astype(BF16), _DN[form], preferred_element_type=F32)


@functools.partial(jax.custom_vjp, nondiff_argnums=(2,))
def bdot(a, b, form):
    return _dot_raw(a, b, form)


def _bdot_fwd(a, b, form):
    return _dot_raw(a, b, form), (a, b)


def _bdot_bwd(form, res, g):
    a, b = res
    if form == "nn":
        return _dot_raw(g, b, "nt"), _dot_raw(a, g, "tn")
    if form == "nt":
        return _dot_raw(g, b, "nn"), _dot_raw(g, a, "tn")
    return _dot_raw(b, g, "nt"), _dot_raw(a, g, "nn")


bdot.defvjp(_bdot_fwd, _bdot_bwd)


def _hdot(a, b):
    return jnp.dot(a, b, precision=lax.Precision.HIGHEST, preferred_element_type=F32)


@jax.custom_vjp
def cdot_left(c, ct, x):
    return _hdot(c, x)


def _cdl_fwd(c, ct, x):
    return _hdot(c, x), (c, ct)


def _cdl_bwd(res, g):
    c, ct = res
    return jnp.zeros_like(c), jnp.zeros_like(ct), _hdot(ct, g)


cdot_left.defvjp(_cdl_fwd, _cdl_bwd)


@jax.custom_vjp
def cdot_right(x, c, ct):
    return _hdot(x, c)


def _cdr_fwd(x, c, ct):
    return _hdot(x, c), (c, ct)


def _cdr_bwd(res, g):
    c, ct = res
    return _hdot(g, ct), jnp.zeros_like(c), jnp.zeros_like(ct)


cdot_right.defvjp(_cdr_fwd, _cdr_bwd)


def _sigmoid(x):
    return 1.0 / (1.0 + jnp.exp(-x))


def _silu(x):
    return x * _sigmoid(x)


def _softplus(x):
    return jnp.maximum(x, 0.0) + jnp.log(1.0 + jnp.exp(-jnp.abs(x)))


def _log_sigmoid(x):
    return jnp.minimum(x, 0.0) - jnp.log(1.0 + jnp.exp(-jnp.abs(x)))


def _gelu(x):
    c = math.sqrt(2.0 / math.pi)
    return 0.5 * x * (1.0 + jnp.tanh(c * (x + 0.044715 * (x * x * x))))


def _rms(x, g):
    return x * lax.rsqrt(jnp.mean(x * x, axis=-1, keepdims=True) + EPS) * g


def _iota2(shape, axis):
    return lax.broadcasted_iota(jnp.int32, shape, axis)


def matmul(a, b, form, *, name, G=1, out_dtype=F32, add=None):
    isz = lambda t: jnp.dtype(t.dtype).itemsize
    osz = jnp.dtype(out_dtype).itemsize + (isz(add) if add is not None else 0)

    def fits(bm, bn, bk):
        return 2 * (bm * bk * isz(a) + bk * bn * isz(b) + bm * bn * osz) + 4 * bm * bn <= _MATMUL_VMEM

    if form in ("nn", "nt"):
        M = a.shape[0]
        K = a.shape[1] // G
        N = b.shape[2] if form == "nn" else b.shape[1]
        bm, bn, bk = min(M, 1024), _pick(N, 1536), _pick(K, 2048)
        while not fits(bm, bn, bk) and bk % 256 == 0:
            bk //= 2
        nj, nk = N // bn, K // bk
        grid = (G, M // bm, nj, nk)
        a_spec = pl.BlockSpec((bm, bk), lambda g, i, j, k: (i, g * nk + k))
        if form == "nn":
            b_spec = pl.BlockSpec((None, bk, bn), lambda g, i, j, k: (g, k, j))
        else:
            b_spec = pl.BlockSpec((None, bn, bk), lambda g, i, j, k: (g, j, k))
        o_spec = pl.BlockSpec((bm, bn), lambda g, i, j, k: (i, g * nj + j))
        out_shape = jax.ShapeDtypeStruct((M, G * N), out_dtype)
    else:
        T = a.shape[0]
        Ka, Nb = a.shape[1] // G, b.shape[1] // G
        bm, bn, bk = _pick(Ka, 1408), _pick(Nb, 1536), min(T, 2048)
        while not fits(bm, bn, bk) and bk % 512 == 0:
            bk //= 2
        ni, nj, nk = Ka // bm, Nb // bn, T // bk
        grid = (G, ni, nj, nk)
        a_spec = pl.BlockSpec((bk, bm), lambda g, i, j, k: (k, g * ni + i))
        b_spec = pl.BlockSpec((bk, bn), lambda g, i, j, k: (k, g * nj + j))
        o_spec = pl.BlockSpec((None, bm, bn), lambda g, i, j, k: (g, i, j))
        out_shape = jax.ShapeDtypeStruct((G, Ka, Nb), out_dtype)
    has_add = add is not None

    def finish(refs, r):
        if has_add:
            r = r + refs[2][...].astype(F32)
        o_ref = refs[3] if has_add else refs[2]
        o_ref[...] = r.astype(o_ref.dtype)

    def body_one(*refs):
        finish(refs, _dot_raw(refs[0][...], refs[1][...], form))

    def body_acc(*refs):
        acc_ref = refs[-1]
        k = pl.program_id(3)

        @pl.when(k == 0)
        def _():
            acc_ref[...] = jnp.zeros_like(acc_ref)

        acc_ref[...] += _dot_raw(refs[0][...], refs[1][...], form)

        @pl.when(k == nk - 1)
        def _():
            finish(refs, acc_ref[...])

    in_specs = [a_spec, b_spec]
    args = [a, b]
    if has_add:
        in_specs.append(o_spec)
        args.append(add)
    return pl.pallas_call(
        body_one if nk == 1 else body_acc, name=name, grid=grid, in_specs=in_specs, out_specs=o_spec,
        out_shape=out_shape, scratch_shapes=[] if nk == 1 else [pltpu.VMEM((bm, bn), F32)],
        compiler_params=_cparams(4),
    )(*args)


def matmul_nt_norm_bwd(a, w, h, g, d, *, name):
    T, K = a.shape
    bm = min(T, 512)
    bk = _pick(K, 2048)
    nk = K // bk

    def body(a_ref, w_ref, h_ref, g_ref, d_ref, dh_ref, dg_ref, acc_ref):
        i, k = pl.program_id(0), pl.program_id(1)

        @pl.when((i == 0) & (k == 0))
        def _():
            dg_ref[...] = jnp.zeros_like(dg_ref)

        @pl.when(k == 0)
        def _():
            acc_ref[...] = jnp.zeros_like(acc_ref)

        acc_ref[...] += _dot_raw(a_ref[...], w_ref[...], "nt")

        @pl.when(k == nk - 1)
        def _():
            _, vjp = jax.vjp(lambda g_, h_: _rms(h_, g_), g_ref[...], h_ref[...])
            dg, dh = vjp(acc_ref[...])
            dh_ref[...] = dh + d_ref[...]
            dg_ref[...] += dg

    row = pl.BlockSpec((bm, D), lambda i, k: (i, 0))
    one = pl.BlockSpec((1, D), lambda i, k: (0, 0))
    return pl.pallas_call(
        body, name=name, grid=(T // bm, nk),
        in_specs=[pl.BlockSpec((bm, bk), lambda i, k: (i, k)), pl.BlockSpec((D, bk), lambda i, k: (0, k)), row, one, row],
        out_specs=[row, one], out_shape=[jax.ShapeDtypeStruct((T, D), F32), jax.ShapeDtypeStruct((1, D), F32)],
        scratch_shapes=[pltpu.VMEM((bm, D), F32)], compiler_params=_cparams(2),
    )(a, w, h, g, d)


def ffn_up(hn, w_il, *, name):
    T = hn.shape[0]
    bm, hb = min(T, 512), FFN_H // 2

    def body(a_ref, b_ref, act_ref, gu_ref):
        r = _dot_raw(a_ref[...], b_ref[...], "nn")
        act_ref[...] = (_silu(r[:, :hb]) * r[:, hb:]).astype(act_ref.dtype)
        gu_ref[...] = r.astype(gu_ref.dtype)

    return pl.pallas_call(
        body, name=name, grid=(2, T // bm),
        in_specs=[pl.BlockSpec((bm, D), lambda j, i: (i, 0)), pl.BlockSpec((D, 2 * hb), lambda j, i: (0, j))],
        out_specs=[pl.BlockSpec((bm, hb), lambda j, i: (i, j)), pl.BlockSpec((bm, 2 * hb), lambda j, i: (i, j))],
        out_shape=[jax.ShapeDtypeStruct((T, FFN_H), BF16), jax.ShapeDtypeStruct((T, 2 * FFN_H), BF16)],
        compiler_params=_cparams(2),
    )(hn, w_il)


_DACT_CHUNK = 512


def ffn_dact(d, w_down, gu, *, name):
    T = d.shape[0]
    bm, hb = min(T, 512), FFN_H // 2

    def body(d_ref, w_ref, gu_ref, o_ref):
        d_blk = d_ref[...].astype(BF16)
        for lo in range(0, hb, _DACT_CHUNK):
            hi = min(lo + _DACT_CHUNK, hb)
            da = _dot_raw(d_blk, w_ref[lo:hi, :], "nt")
            g, u = gu_ref[:, lo:hi].astype(F32), gu_ref[:, hb + lo:hb + hi].astype(F32)
            sg = _sigmoid(g)
            o_ref[:, lo:hi] = (da * u * (sg * (1.0 + g * (1.0 - sg)))).astype(o_ref.dtype)
            o_ref[:, hb + lo:hb + hi] = (da * (g * sg)).astype(o_ref.dtype)

    return pl.pallas_call(
        body, name=name, grid=(2, T // bm),
        in_specs=[pl.BlockSpec((bm, D), lambda j, i: (i, 0)), pl.BlockSpec((hb, D), lambda j, i: (j, 0)),
                  pl.BlockSpec((bm, 2 * hb), lambda j, i: (i, j))],
        out_specs=pl.BlockSpec((bm, 2 * hb), lambda j, i: (i, j)),
        out_shape=jax.ShapeDtypeStruct((T, 2 * FFN_H), BF16), compiler_params=_cparams(2),
    )(d, w_down, gu)


def _row_entry(e):
    return e if isinstance(e, tuple) else (e, e.shape[1], 0)


def _row_spec(bt, e):
    _, width, idx = e
    return pl.BlockSpec((bt, width), lambda i: (i, idx))


def _full_spec(p):
    return pl.BlockSpec(p.shape, lambda i: (0,) * p.ndim)


def rowwise(f, params, rows, outs, *, bt, name):
    rows = [_row_entry(e) for e in rows]
    T = rows[0][0].shape[0]
    bt = min(bt, T)
    np_, nr = len(params), len(rows)

    def body(*refs):
        p = tuple(r[...].astype(F32) for r in refs[:np_])
        rw = tuple(r[...].astype(F32) for r in refs[np_:np_ + nr])
        res = f(p, rw)
        for o_ref, o in zip(refs[np_ + nr:], res):
            o_ref[...] = o.astype(o_ref.dtype)

    res = pl.pallas_call(
        body, name=name, grid=(T // bt,),
        in_specs=[_full_spec(p) for p in params] + [_row_spec(bt, e) for e in rows],
        out_specs=[pl.BlockSpec((bt, w), lambda i: (i, 0)) for w, _ in outs],
        out_shape=[jax.ShapeDtypeStruct((T, w), dt) for w, dt in outs],
        compiler_params=_cparams(1),
    )(*params, *[e[0] for e in rows])
    return list(res)


def rowwise_vjp(f, params, rows, cts, drow_dtypes, *, bt, name):
    rows = [_row_entry(e) for e in rows]
    cts = [_row_entry(e) for e in cts]
    T = rows[0][0].shape[0]
    bt = min(bt, T)
    np_, nr, nc = len(params), len(rows), len(cts)
    want = [i for i, dt in enumerate(drow_dtypes) if dt is not None]

    def body(*refs):
        p = tuple(r[...].astype(F32) for r in refs[:np_])
        rw = tuple(r[...].astype(F32) for r in refs[np_:np_ + nr])
        ct = tuple(r[...].astype(F32) for r in refs[np_ + nr:np_ + nr + nc])
        outs = refs[np_ + nr + nc:]
        _, vjp = jax.vjp(f, p, rw)
        dp, dr = vjp(ct)

        @pl.when(pl.program_id(0) == 0)
        def _():
            for o in outs[:np_]:
                o[...] = jnp.zeros_like(o)

        for o, d in zip(outs[:np_], dp):
            o[...] += d
        for o, i in zip(outs[np_:], want):
            o[...] = dr[i].astype(o.dtype)

    res = pl.pallas_call(
        body, name=name, grid=(T // bt,),
        in_specs=[_full_spec(p) for p in params] + [_row_spec(bt, e) for e in rows] + [_row_spec(bt, e) for e in cts],
        out_specs=[_full_spec(p) for p in params] + [pl.BlockSpec((bt, rows[i][1]), lambda i_: (i_, 0)) for i in want],
        out_shape=[jax.ShapeDtypeStruct(p.shape, F32) for p in params]
        + [jax.ShapeDtypeStruct((T, rows[i][1]), drow_dtypes[i]) for i in want],
        compiler_params=_cparams(1),
    )(*params, *[e[0] for e in rows], *[e[0] for e in cts])
    res = list(res)
    return res[:np_], res[np_:]


def f_rmsnorm(p, r):
    return (_rms(r[0], p[0]),)


def f_gla_gate_in(p, r):
    w_a2, b_a = p
    z = bdot(r[0], w_a2, "nn") + b_a
    return (_log_sigmoid(z) / GLA_TAU,)


def f_gla_gate_in_fwd(p, r):
    w_a2, b_a = p
    z = _dot_raw(r[0], w_a2, "nn") + b_a
    return (_log_sigmoid(z) / GLA_TAU,)


def f_s5_act(p, r):
    (dsk,) = p
    ycp, u = r
    return (_gelu(ycp + dsk * u),)


def f_glu_res(p, r):
    vg, h = r
    return (vg[:, :D] * _sigmoid(vg[:, D:]) + h,)


def f_glu(p, r):
    vg = r[0]
    return (vg[:, :D] * _sigmoid(vg[:, D:]),)


def loss_head(h, g, target, *, name):
    T = h.shape[0]
    bt = min(T, 256)

    def lossf(g_, h_, t_):
        e = _rms(h_, g_) - t_
        return (0.5 / D) * jnp.sum(e * e)

    def body(g_ref, h_ref, t_ref, loss_ref, dg_ref, dh_ref):
        @pl.when(pl.program_id(0) == 0)
        def _():
            loss_ref[...] = jnp.zeros_like(loss_ref)
            dg_ref[...] = jnp.zeros_like(dg_ref)

        val, vjp = jax.vjp(lossf, g_ref[...], h_ref[...], t_ref[...])
        dg, dh, _ = vjp(jnp.ones((), F32))
        loss_ref[...] += jnp.full(loss_ref.shape, val, F32)
        dg_ref[...] += dg
        dh_ref[...] = dh

    row = pl.BlockSpec((bt, D), lambda i: (i, 0))
    one = pl.BlockSpec((1, D), lambda i: (0, 0))
    loss, dg, dh = pl.pallas_call(
        body, name=name, grid=(T // bt,), in_specs=[one, row, row],
        out_specs=[pl.BlockSpec((1, LANE), lambda i: (0, 0)), one, row],
        out_shape=[jax.ShapeDtypeStruct((1, LANE), F32), jax.ShapeDtypeStruct((1, D), F32),
                   jax.ShapeDtypeStruct((T, D), F32)],
        compiler_params=_cparams(1),
    )(g, h, target)
    return loss[0, 0], dg, dh


def _gla_consts():
    r, c = _iota2((CH, CH), 0), _iota2((CH, CH), 1)
    return (r >= c).astype(F32), (r <= c).astype(F32), r >= c


def _gla_chunk(q, k, v, la, st, consts, dot, cdl):
    L, Lt, tril = consts
    lc = cdl(L, Lt, la)
    lend = lc[CH - 1:CH, :]
    e, ei = jnp.exp(lc), jnp.exp(-lc)
    qs = q * (GLA_DK ** -0.5)
    qf, kf, qb, kb = qs * e, k * ei, qs * ei, k * e
    sc = jnp.where(tril, dot(qf, kf, "nt"), dot(qb, kb, "nt"))
    o = dot(sc, v, "nn") + dot(qf, st, "nt")
    kd = k * jnp.exp(lend - lc)
    st_new = st * jnp.exp(lend) + dot(v, kd, "tn")
    return o, st_new


def _gla_block(q, k, v, la, st, nc, dot, cdl):
    consts = _gla_consts()
    outs = []
    for c in range(nc):
        sl = slice(c * CH, (c + 1) * CH)
        o, st = _gla_chunk(q[sl], k[sl], v[sl], la[sl], st, consts, dot, cdl)
        outs.append(o)
    return jnp.concatenate(outs, axis=0), st


_GLA_HP = 2


def _gla_specs(rows, rev, nb):
    t = (lambda j: nb - 1 - j) if rev else (lambda j: j)
    hp, ng = _GLA_HP, GLA_H // _GLA_HP
    q = pl.BlockSpec((rows, hp * GLA_DK), lambda h, j: (t(j), h))
    k = pl.BlockSpec((rows, hp * GLA_DK), lambda h, j: (t(j), ng + h))
    v = pl.BlockSpec((rows, hp * GLA_DV), lambda h, j: (t(j), ng + h))
    la = pl.BlockSpec((rows, hp * GLA_DK), lambda h, j: (t(j), h))
    ss = pl.BlockSpec((None, hp, GLA_DV, GLA_DK), lambda h, j: (t(j), h, 0, 0))
    o = pl.BlockSpec((rows, hp * GLA_DV), lambda h, j: (t(j), h))
    r = pl.BlockSpec((rows, hp * GLA_DV), lambda h, j: (t(j), 2 * ng + h))
    g = pl.BlockSpec((1, hp * GLA_DV), lambda h, j: (0, h))
    return q, k, v, la, ss, o, r, g


def _gla_heads(q, k, v, la, r, ng, sts, nc, dot, cdl):
    outs, new = [], []
    for i in range(_GLA_HP):
        kk, vv = slice(i * GLA_DK, (i + 1) * GLA_DK), slice(i * GLA_DV, (i + 1) * GLA_DV)
        o, st = _gla_block(q[:, kk], k[:, kk], v[:, vv], la[:, kk], sts[i], nc, dot, cdl)
        outs.append(_rms(o, ng[:, vv]) * _silu(r[:, vv]))
        new.append(st)
    return jnp.concatenate(outs, axis=1), tuple(new)


def gla_scan_fwd(proj, la, ng, *, nc, name):
    T = proj.shape[0]
    rows = min(T, nc * CH)
    nc = rows // CH
    per = 2 if T % (2 * rows) == 0 else 1
    nb = T // (per * rows)
    q_s, k_s, v_s, la_s, _, o_s, r_s, g_s = _gla_specs(per * rows, False, nb)
    ss_s = pl.BlockSpec((per, _GLA_HP, GLA_DV, GLA_DK), lambda h, j: (j, h, 0, 0))

    def body(q_ref, k_ref, v_ref, la_ref, r_ref, g_ref, o_ref, ss_ref, st_ref):
        @pl.when(pl.program_id(1) == 0)
        def _():
            st_ref[...] = jnp.zeros_like(st_ref)

        sts = tuple(st_ref[i] for i in range(_GLA_HP))
        for p in range(per):
            sl = slice(p * rows, (p + 1) * rows)
            for i in range(_GLA_HP):
                ss_ref[p, i] = sts[i]
            o, sts = _gla_heads(q_ref[sl, :], k_ref[sl, :], v_ref[sl, :], la_ref[sl, :], r_ref[sl, :], g_ref[...], sts, nc,
                                _dot_raw, lambda c, ct, x: _hdot(c, x))
            o_ref[sl, :] = o.astype(o_ref.dtype)
        for i in range(_GLA_HP):
            st_ref[i] = sts[i]

    return pl.pallas_call(
        body, name=name, grid=(GLA_H // _GLA_HP, nb), in_specs=[q_s, k_s, v_s, la_s, r_s, g_s], out_specs=[o_s, ss_s],
        out_shape=[jax.ShapeDtypeStruct((T, GLA_VD), BF16), jax.ShapeDtypeStruct((per * nb, GLA_H, GLA_DV, GLA_DK), F32)],
        scratch_shapes=[pltpu.VMEM((_GLA_HP, GLA_DV, GLA_DK), F32)], compiler_params=_cparams(2),
    )(proj, proj, proj, la, proj, ng)


def gla_scan_bwd(proj, la, ng, ss, do, *, nc, name):
    T = proj.shape[0]
    rows = min(T, nc * CH)
    nc = rows // CH
    nb = T // rows
    q_s, k_s, v_s, la_s, ss_s, o_s, r_s, g_s = _gla_specs(rows, True, nb)
    t = lambda j: nb - 1 - j
    dqk_s = pl.BlockSpec((rows, _GLA_HP * GLA_DK), lambda h, j: (t(j), h))

    def body(q_ref, k_ref, v_ref, la_ref, r_ref, g_ref, ss_ref, do_ref,
             dq_ref, dk_ref, dv_ref, dla_ref, dr_ref, dg_ref, dst_ref):
        @pl.when(pl.program_id(1) == 0)
        def _():
            dst_ref[...] = jnp.zeros_like(dst_ref)
            dg_ref[...] = jnp.zeros_like(dg_ref)

        fn = lambda q, k, v, la_, r, g, *sts: _gla_heads(q, k, v, la_, r, g, sts, nc, bdot, cdot_left)
        _, vjp = jax.vjp(fn, q_ref[...], k_ref[...], v_ref[...], la_ref[...], r_ref[...], g_ref[...],
                         *[ss_ref[i] for i in range(_GLA_HP)])
        dq, dk, dv, dla, dr, dg, *dsts = vjp((do_ref[...], tuple(dst_ref[i] for i in range(_GLA_HP))))
        dq_ref[...] = dq.astype(dq_ref.dtype)
        dk_ref[...] = dk.astype(dk_ref.dtype)
        dv_ref[...] = dv.astype(dv_ref.dtype)
        dla_ref[...] = dla
        dr_ref[...] = dr.astype(dr_ref.dtype)
        dg_ref[...] += dg
        for i in range(_GLA_HP):
            dst_ref[i] = dsts[i]

    return pl.pallas_call(
        body, name=name, grid=(GLA_H // _GLA_HP, nb), in_specs=[q_s, k_s, v_s, la_s, r_s, g_s, ss_s, o_s],
        out_specs=[dqk_s, dqk_s, o_s, dqk_s, o_s, g_s],
        out_shape=[jax.ShapeDtypeStruct((T, GLA_QK), BF16), jax.ShapeDtypeStruct((T, GLA_QK), BF16),
                   jax.ShapeDtypeStruct((T, GLA_VD), BF16), jax.ShapeDtypeStruct((T, GLA_QK), F32),
                   jax.ShapeDtypeStruct((T, GLA_VD), BF16), jax.ShapeDtypeStruct((1, GLA_VD), F32)],
        scratch_shapes=[pltpu.VMEM((_GLA_HP, GLA_DV, GLA_DK), F32)], compiler_params=_cparams(2),
    )(proj, proj, proj, la, proj, ng, ss, do)


_CONV_W = 512
_CONV_OFF = SSD_DI // _CONV_W


def _conv_pre(x, prev8, w_ref, b_ref):
    bt = x.shape[0]
    ext = jnp.concatenate([prev8, x], axis=0)
    shifted = []
    for j in range(SSD_K):
        s = SSD_K - 1 - j
        shifted.append(x if s == 0 else pltpu.roll(ext, s, 0)[8:8 + bt])
    pre = b_ref[...] + sum(w_ref[j:j + 1, :] * shifted[j] for j in range(SSD_K))
    return pre, shifted


def ssd_conv_fwd(proj, w, b, *, name):
    T = proj.shape[0]
    bt = min(T, 512)
    nb = T // bt

    def body(x_ref, w_ref, b_ref, o_ref, carry_ref):
        @pl.when(pl.program_id(1) == 0)
        def _():
            carry_ref[...] = jnp.zeros_like(carry_ref)

        x = x_ref[...]
        pre, _ = _conv_pre(x, carry_ref[...], w_ref, b_ref)
        o_ref[...] = _silu(pre)
        carry_ref[...] = x[bt - 8:, :]

    return pl.pallas_call(
        body, name=name, grid=(SSD_CONV // _CONV_W, nb),
        in_specs=[pl.BlockSpec((bt, _CONV_W), lambda c, t: (t, _CONV_OFF + c)),
                  pl.BlockSpec((SSD_K, _CONV_W), lambda c, t: (0, c)),
                  pl.BlockSpec((1, _CONV_W), lambda c, t: (0, c))],
        out_specs=pl.BlockSpec((bt, _CONV_W), lambda c, t: (t, c)),
        out_shape=jax.ShapeDtypeStruct((T, SSD_CONV), F32),
        scratch_shapes=[pltpu.VMEM((8, _CONV_W), F32)], compiler_params=_cparams(2),
    )(proj, w, b)


def ssd_conv_bwd(proj, w, b, dout, col0, *, name):
    T, ncols = dout.shape
    bt = min(T, 512)
    nb = T // bt
    r8 = bt // 8
    c0 = col0 // _CONV_W

    def body(x_ref, xp_ref, w_ref, b_ref, do_ref, dx_ref, dw_ref, db_ref, carry_ref):
        t = pl.program_id(1)

        @pl.when(t == 0)
        def _():
            carry_ref[...] = jnp.zeros_like(carry_ref)
            dw_ref[...] = jnp.zeros_like(dw_ref)
            db_ref[...] = jnp.zeros_like(db_ref)

        x = x_ref[...]
        prev8 = jnp.where(t == nb - 1, 0.0, xp_ref[...])
        pre, shifted = _conv_pre(x, prev8, w_ref, b_ref)
        sg = _sigmoid(pre)
        dpre = do_ref[...] * (sg * (1.0 + pre * (1.0 - sg)))
        ext = jnp.concatenate([dpre, carry_ref[...]], axis=0)
        dx = w_ref[SSD_K - 1:SSD_K, :] * dpre
        for j in range(SSD_K - 1):
            s = SSD_K - 1 - j
            dx = dx + w_ref[j:j + 1, :] * pltpu.roll(ext, bt + 8 - s, 0)[:bt]
        dx_ref[...] = dx.astype(dx_ref.dtype)
        dw_ref[...] += jnp.concatenate([jnp.sum(dpre * shifted[j], axis=0, keepdims=True) for j in range(SSD_K)], axis=0)
        db_ref[...] += jnp.sum(dpre, axis=0, keepdims=True)
        carry_ref[...] = dpre[:8, :]

    rt = lambda t: nb - 1 - t
    return pl.pallas_call(
        body, name=name, grid=(ncols // _CONV_W, nb),
        in_specs=[pl.BlockSpec((bt, _CONV_W), lambda c, t: (rt(t), _CONV_OFF + c0 + c)),
                  pl.BlockSpec((8, _CONV_W), lambda c, t: (jnp.maximum(rt(t) * r8 - 1, 0), _CONV_OFF + c0 + c)),
                  pl.BlockSpec((SSD_K, _CONV_W), lambda c, t: (0, c0 + c)),
                  pl.BlockSpec((1, _CONV_W), lambda c, t: (0, c0 + c)),
                  pl.BlockSpec((bt, _CONV_W), lambda c, t: (rt(t), c))],
        out_specs=[pl.BlockSpec((bt, _CONV_W), lambda c, t: (rt(t), c)),
                   pl.BlockSpec((SSD_K, _CONV_W), lambda c, t: (0, c)),
                   pl.BlockSpec((1, _CONV_W), lambda c, t: (0, c))],
        out_shape=[jax.ShapeDtypeStruct((T, ncols), BF16), jax.ShapeDtypeStruct((SSD_K, ncols), F32),
                   jax.ShapeDtypeStruct((1, ncols), F32)],
        scratch_shapes=[pltpu.VMEM((8, _CONV_W), F32)], compiler_params=_cparams(2),
    )(proj, proj, w, b, dout)


_SSD_U = 2 * CH


def _ssd_unit(xs, bm, cm, dtraw, dtb, alog, dsk, hp, g, dot, cdl, cdr):
    U, P2 = _SSD_U, 2 * SSD_HD
    r, c = _iota2((U, U), 0), _iota2((U, U), 1)
    same = (r // CH) == (c // CH)
    Lb = (same & (r >= c)).astype(F32)
    Ub = (same & (r <= c)).astype(F32)
    lane = _iota2((1, U), 1)
    lo_lane = _iota2((1, P2), 1) < SSD_HD
    lo_sub = _iota2((P2, 1), 0) < SSD_HD
    diag2 = (_iota2((CH, P2), 0) == (_iota2((CH, P2), 1) % CH)).astype(F32)

    dt = _softplus(dtraw + dtb)
    da = dt * (-jnp.exp(alog))
    cum = cdl(Lb, Ub, da)
    ys = []
    new_hp = []
    for pr in range(2):
        xs_p = xs[:, pr * P2:(pr + 1) * P2]
        cols, dts, dks = [], [], []
        for jj in range(2):
            oh_l = (lane == g * (SSD_H // SSD_G) + 2 * pr + jj).astype(F32)
            cols.append(jnp.sum(cum * oh_l, axis=1, keepdims=True))
            dts.append(jnp.sum(dt * oh_l, axis=1, keepdims=True))
            dks.append(jnp.sum(dsk * oh_l, axis=1, keepdims=True))
        dsk_p = jnp.where(lo_lane, dks[0], dks[1])
        h = hp[pr]
        yc = []
        for ci in range(2):
            sl = slice(ci * CH, (ci + 1) * CH)
            xs_c, bm_c, cm_c = xs_p[sl], bm[sl], cm[sl]
            col = jnp.where(lo_lane, cols[0][sl], cols[1][sl])
            dtc = jnp.where(lo_lane, dts[0][sl], dts[1][sl])
            row = jnp.sum(diag2 * col, axis=0, keepdims=True)
            dtrow = jnp.sum(diag2 * dtc, axis=0, keepdims=True)
            cb = dot(cm_c, jnp.concatenate([bm_c, bm_c], axis=0), "nt")
            mix = cb * jnp.exp(-jnp.abs(col - row)) * dtrow
            xbd = jnp.concatenate([jnp.where(lo_lane, xs_c, 0.0), jnp.where(lo_lane, 0.0, xs_c)], axis=0)
            y_intra = dot(mix, xbd, "nn")
            ce = jnp.where(lo_lane, cols[0][ci * CH + CH - 1:ci * CH + CH, :], cols[1][ci * CH + CH - 1:ci * CH + CH, :])
            y_inter = dot(cm_c, h, "nt") * jnp.exp(col)
            xw = xs_c * (dtc * jnp.exp(ce - col))
            ce_s = [cols[jj][ci * CH + CH - 1:ci * CH + CH, :] for jj in range(2)]
            a_p = jnp.where(lo_sub, jnp.exp(ce_s[0]), jnp.exp(ce_s[1]))
            h = a_p * h + dot(xw, bm_c, "tn")
            yc.append(y_intra + y_inter + dsk_p * xs_c)
        ys.append(jnp.concatenate(yc, axis=0))
        new_hp.append(h)
    return jnp.concatenate(ys, axis=1), tuple(new_hp)


def _ssd_block(xs, bm, cm, dtraw, z, dtb, alog, dsk, ng, hp, g, nu, dot, cdl, cdr):
    outs = []
    for u in range(nu):
        sl = slice(u * _SSD_U, (u + 1) * _SSD_U)
        y, hp = _ssd_unit(xs[sl], bm[sl], cm[sl], dtraw[sl], dtb, alog, dsk, hp, g, dot, cdl, cdr)
        outs.append(y)
    return _rms(jnp.concatenate(outs, axis=0) * _silu(z), ng), hp


def _ssd_specs(rows, rev, nb):
    t = (lambda j: nb - 1 - j) if rev else (lambda j: j)
    gw = SSD_DI // SSD_G
    xs = pl.BlockSpec((rows, gw), lambda j, g: (t(j), g))
    bm = pl.BlockSpec((rows, SSD_N), lambda j, g: (t(j), SSD_DI // SSD_N + g))
    cm = pl.BlockSpec((rows, SSD_N), lambda j, g: (t(j), (SSD_DI + SSD_GN) // SSD_N + g))
    dtr = pl.BlockSpec((rows, LANE), lambda j, g: (t(j), (SSD_DI + SSD_CONV) // LANE))
    par = pl.BlockSpec((1, LANE), lambda j, g: (0, 0))
    hs = pl.BlockSpec((None, None, 2, 2 * SSD_HD, SSD_N), lambda j, g: (t(j), g, 0, 0, 0))
    y = pl.BlockSpec((rows, gw), lambda j, g: (t(j), g))
    ng = pl.BlockSpec((1, gw), lambda j, g: (0, g))
    return xs, bm, cm, dtr, par, hs, y, ng


def ssd_scan_fwd(xbc, proj, dtb, alog, dsk, ng, *, nu, name):
    T = xbc.shape[0]
    rows = min(T, nu * _SSD_U)
    nu = rows // _SSD_U
    nb = T // rows
    xs_s, bm_s, cm_s, dt_s, par_s, hs_s, y_s, ng_s = _ssd_specs(rows, False, nb)

    def body(xs_ref, bm_ref, cm_ref, dt_ref, z_ref, dtb_ref, al_ref, dsk_ref, ng_ref, y_ref, hs_ref, h_ref):
        g = pl.program_id(1)

        @pl.when(pl.program_id(0) == 0)
        def _():
            h_ref[g] = jnp.zeros(h_ref.shape[1:], F32)

        hs_ref[...] = h_ref[g]
        hp = (h_ref[g, 0], h_ref[g, 1])
        y, hp = _ssd_block(xs_ref[...], bm_ref[...], cm_ref[...], dt_ref[...], z_ref[...], dtb_ref[...], al_ref[...],
                           dsk_ref[...], ng_ref[...], hp, g, nu, _dot_raw, lambda c, ct, x: _hdot(c, x), lambda x, c, ct: _hdot(x, c))
        y_ref[...] = y.astype(y_ref.dtype)
        h_ref[g, 0] = hp[0]
        h_ref[g, 1] = hp[1]

    return pl.pallas_call(
        body, name=name, grid=(nb, SSD_G), in_specs=[xs_s, bm_s, cm_s, dt_s, y_s, par_s, par_s, par_s, ng_s],
        out_specs=[y_s, hs_s],
        out_shape=[jax.ShapeDtypeStruct((T, SSD_DI), BF16), jax.ShapeDtypeStruct((nb, SSD_G, 2, 2 * SSD_HD, SSD_N), F32)],
        scratch_shapes=[pltpu.VMEM((SSD_G, 2, 2 * SSD_HD, SSD_N), F32)], compiler_params=_cparams(2),
    )(xbc, xbc, xbc, proj, proj, dtb, alog, dsk, ng)


def ssd_scan_bwd(xbc, proj, dtb, alog, dsk, ng, hs, dy, *, nu, name):
    T = xbc.shape[0]
    rows = min(T, nu * _SSD_U)
    nu = rows // _SSD_U
    nb = T // rows
    xs_s, bm_s, cm_s, dt_s, par_s, hs_s, y_s, ng_s = _ssd_specs(rows, True, nb)
    gw = SSD_DI // SSD_G
    dng_s = pl.BlockSpec((1, SSD_DI), lambda j, g: (0, 0))
    t = lambda j: nb - 1 - j
    n_s = pl.BlockSpec((rows, SSD_N), lambda j, g: (t(j), g))
    ddt_s = pl.BlockSpec((rows, LANE), lambda j, g: (t(j), 0))

    def body(xs_ref, bm_ref, cm_ref, dt_ref, z_ref, dtb_ref, al_ref, dsk_ref, ng_ref, hs_ref, dy_ref,
             dxs_ref, dbm_ref, dcm_ref, ddt_ref, ddtb_ref, dal_ref, ddsk_ref, dz_ref, dng_ref, dh_ref):
        j, g = pl.program_id(0), pl.program_id(1)

        @pl.when(j == 0)
        def _():
            dh_ref[g] = jnp.zeros(dh_ref.shape[1:], F32)

        @pl.when((j == 0) & (g == 0))
        def _():
            ddtb_ref[...] = jnp.zeros_like(ddtb_ref)
            dal_ref[...] = jnp.zeros_like(dal_ref)
            ddsk_ref[...] = jnp.zeros_like(ddsk_ref)
            dng_ref[...] = jnp.zeros_like(dng_ref)

        @pl.when(g == 0)
        def _():
            ddt_ref[...] = jnp.zeros_like(ddt_ref)

        fn = lambda xs, bm, cm, dtr, z, dtb_, al, dsk_, ng_, h0, h1: _ssd_block(
            xs, bm, cm, dtr, z, dtb_, al, dsk_, ng_, (h0, h1), g, nu, bdot, cdot_left, cdot_right)
        _, vjp = jax.vjp(fn, xs_ref[...], bm_ref[...], cm_ref[...], dt_ref[...], z_ref[...], dtb_ref[...], al_ref[...],
                         dsk_ref[...], ng_ref[...], hs_ref[0], hs_ref[1])
        dxs, dbm, dcm, ddt, dz, ddtb, dal, ddsk, dng, dh0, dh1 = vjp((dy_ref[...], (dh_ref[g, 0], dh_ref[g, 1])))
        dz_ref[...] = dz.astype(dz_ref.dtype)
        lanes = pl.ds(pl.multiple_of(g * gw, gw), gw)
        dng_ref[:, lanes] = dng_ref[:, lanes] + dng
        dxs_ref[...] = dxs
        dbm_ref[...] = dbm
        dcm_ref[...] = dcm
        ddt_ref[...] += ddt
        ddtb_ref[...] += ddtb
        dal_ref[...] += dal
        ddsk_ref[...] += ddsk
        dh_ref[g, 0] = dh0
        dh_ref[g, 1] = dh1

    return pl.pallas_call(
        body, name=name, grid=(nb, SSD_G), in_specs=[xs_s, bm_s, cm_s, dt_s, y_s, par_s, par_s, par_s, ng_s, hs_s, y_s],
        out_specs=[y_s, n_s, n_s, ddt_s, par_s, par_s, par_s, y_s, dng_s],
        out_shape=[jax.ShapeDtypeStruct((T, SSD_DI), F32), jax.ShapeDtypeStruct((T, SSD_GN), F32),
                   jax.ShapeDtypeStruct((T, SSD_GN), F32), jax.ShapeDtypeStruct((T, LANE), F32),
                   jax.ShapeDtypeStruct((1, LANE), F32), jax.ShapeDtypeStruct((1, LANE), F32),
                   jax.ShapeDtypeStruct((1, LANE), F32), jax.ShapeDtypeStruct((T, SSD_DI), BF16),
                   jax.ShapeDtypeStruct((1, SSD_DI), F32)],
        scratch_shapes=[pltpu.VMEM((SSD_G, 2, 2 * SSD_HD, SSD_N), F32)], compiler_params=_cparams(2),
    )(xbc, xbc, xbc, proj, proj, dtb, alog, dsk, ng, hs, dy)


def _s5_param_f(log_dt, a_re, a_im, bre_t, bim_t, cim, cdl):
    n = S5_NG * S5_GS
    r, c = _iota2((n, S5_NG), 0), _iota2((n, S5_NG), 1)
    E = ((r // S5_GS) == c).astype(F32)
    rt, ct = _iota2((S5_NG, n), 0), _iota2((S5_NG, n), 1)
    Et = ((ct // S5_GS) == rt).astype(F32)
    step = jnp.exp(log_dt)
    mag = jnp.exp(step * a_re)
    abr = mag * jnp.cos(step * a_im)
    abi = mag * jnp.sin(step * a_im)
    den = a_re * a_re + a_im * a_im
    nr, ni = abr - 1.0, abi
    fr = (nr * a_re + ni * a_im) / den
    fi = (ni * a_re - nr * a_im) / den
    Fr, Fi = cdl(E, Et, fr), cdl(E, Et, fi)
    bbr = Fr * bre_t - Fi * bim_t
    bbi = Fr * bim_t + Fi * bre_t
    return abr, abi, bbr, bbi, -cim


def _whole(a):
    return pl.BlockSpec(a.shape, lambda: (0,) * a.ndim)


def s5_param_fwd(args, *, name):
    def body(*refs):
        res = _s5_param_f(*[r[...] for r in refs[:6]], lambda c, ct, x: _hdot(c, x))
        for o, v in zip(refs[6:], res):
            o[...] = v

    shapes = [(S5_NG, S5_P), (S5_NG, S5_P)] + [(S5_NG * S5_GS, S5_P)] * 3
    return pl.pallas_call(
        body, name=name, in_specs=[_whole(a) for a in args], out_specs=[pl.BlockSpec(s, lambda: (0, 0)) for s in shapes],
        out_shape=[jax.ShapeDtypeStruct(s, F32) for s in shapes],
        compiler_params=pltpu.CompilerParams(vmem_limit_bytes=VMEM_LIMIT),
    )(*args)


def s5_param_bwd(args, cts, *, name):
    def body(*refs):
        fn = lambda *a: _s5_param_f(*a, cdot_left)
        _, vjp = jax.vjp(fn, *[r[...] for r in refs[:6]])
        grads = vjp(tuple(r[...] for r in refs[6:11]))
        for o, v in zip(refs[11:], grads):
            o[...] = v

    return pl.pallas_call(
        body, name=name, in_specs=[_whole(a) for a in list(args) + list(cts)],
        out_specs=[_whole(a) for a in args], out_shape=[jax.ShapeDtypeStruct(a.shape, F32) for a in args],
        compiler_params=pltpu.CompilerParams(vmem_limit_bytes=VMEM_LIMIT),
    )(*args, *cts)


_S5_W = S5_BLK * S5_P


def _cmul_add(xr, xi, pr, pi, sr, si):
    return xr + (pr * sr - pi * si), xi + (pr * si + pi * sr)


def _s5_powers(ar, ai):
    pw = [(ar, ai)]
    for _ in range(7):
        qr, qi = pw[-1]
        pw.append((qr * ar - qi * ai, qr * ai + qi * ar))
    return pw


def s5_scan_fwd(u, wb, a_re, a_im, wc, *, name):
    T = u.shape[0]
    bt = min(T, 256)
    nb = T // bt

    def body(u_ref, wb_ref, ar_ref, ai_ref, wc_ref, x_ref, y_ref, bu_ref, carry_ref):
        @pl.when(pl.program_id(1) == 0)
        def _():
            carry_ref[...] = jnp.zeros_like(carry_ref)

        bu_ref[...] = _dot_raw(u_ref[...], wb_ref[...], "nn")
        ar, ai = ar_ref[...], ai_ref[...]
        pw = _s5_powers(ar, ai)
        pwr = jnp.concatenate([p[0] for p in pw], axis=0)
        pwi = jnp.concatenate([p[1] for p in pw], axis=0)
        rin = _iota2((8, _S5_W), 0)
        cr, ci = carry_ref[0:1, :], carry_ref[1:2, :]
        for t in range(bt // 8):
            sl = slice(8 * t, 8 * t + 8)
            xr, xi = bu_ref[sl, :_S5_W], bu_ref[sl, _S5_W:]
            for s in (1, 2, 4):
                m = rin >= s
                sr = jnp.where(m, pltpu.roll(xr, s, 0), 0.0)
                si = jnp.where(m, pltpu.roll(xi, s, 0), 0.0)
                xr, xi = _cmul_add(xr, xi, *pw[s - 1], sr, si)
            xr, xi = _cmul_add(xr, xi, pwr, pwi, cr, ci)
            x_ref[sl, :_S5_W] = xr
            x_ref[sl, _S5_W:] = xi
            cr, ci = xr[7:8, :], xi[7:8, :]
        carry_ref[0:1, :] = cr
        carry_ref[1:2, :] = ci
        y_ref[...] = _dot_raw(x_ref[...], wc_ref[...], "nn")

    nblk = S5_NG // S5_BLK
    blk = pl.BlockSpec((bt, 2 * _S5_W), lambda g, t: (t, g))
    col = pl.BlockSpec((bt, LANE), lambda g, t: (t, g))
    a_s = pl.BlockSpec((None, 1, _S5_W), lambda g, t: (g, 0, 0))
    wb_s = pl.BlockSpec((None, LANE, 2 * _S5_W), lambda g, t: (g, 0, 0))
    wc_s = pl.BlockSpec((None, 2 * _S5_W, LANE), lambda g, t: (g, 0, 0))
    return pl.pallas_call(
        body, name=name, grid=(nblk, nb), in_specs=[col, wb_s, a_s, a_s, wc_s], out_specs=[blk, col],
        out_shape=[jax.ShapeDtypeStruct((T, nblk * 2 * _S5_W), F32), jax.ShapeDtypeStruct((T, nblk * LANE), F32)],
        scratch_shapes=[pltpu.VMEM((bt, 2 * _S5_W), F32), pltpu.VMEM((8, _S5_W), F32)],
        compiler_params=_cparams(2),
    )(u, wb, a_re, a_im, wc)


def s5_scan_bwd(dy, x, u, wb, a_re, a_im, wc, *, name):
    T = dy.shape[0]
    bt = min(T, 256)
    nb = T // bt

    def body(dy_ref, x_ref, u_ref, wb_ref, ar_ref, ai_ref, wc_ref, du_ref, dwb_ref, dwc_ref, dar_ref, dai_ref,
             g_ref, lam_ref, carry_ref):
        @pl.when(pl.program_id(1) == 0)
        def _():
            carry_ref[...] = jnp.zeros_like(carry_ref)
            dar_ref[...] = jnp.zeros_like(dar_ref)
            dai_ref[...] = jnp.zeros_like(dai_ref)
            dwb_ref[...] = jnp.zeros_like(dwb_ref)
            dwc_ref[...] = jnp.zeros_like(dwc_ref)

        g_ref[...] = _dot_raw(dy_ref[...], wc_ref[...], "nt")
        pw = _s5_powers(ar_ref[...], -ai_ref[...])
        pwr = jnp.concatenate([p[0] for p in reversed(pw)], axis=0)
        pwi = jnp.concatenate([p[1] for p in reversed(pw)], axis=0)
        rin = _iota2((8, _S5_W), 0)
        cr, ci = carry_ref[0:1, :], carry_ref[1:2, :]
        acc_r = jnp.zeros((8, _S5_W), F32)
        acc_i = jnp.zeros((8, _S5_W), F32)
        for t in reversed(range(bt // 8)):
            sl = slice(8 * t, 8 * t + 8)
            lr, li = g_ref[sl, :_S5_W], g_ref[sl, _S5_W:]
            for s in (1, 2, 4):
                m = rin < 8 - s
                sr = jnp.where(m, pltpu.roll(lr, 8 - s, 0), 0.0)
                si = jnp.where(m, pltpu.roll(li, 8 - s, 0), 0.0)
                lr, li = _cmul_add(lr, li, *pw[s - 1], sr, si)
            lr, li = _cmul_add(lr, li, pwr, pwi, cr, ci)
            lam_ref[sl, :_S5_W] = lr
            lam_ref[sl, _S5_W:] = li
            nr = jnp.where(rin == 7, cr, pltpu.roll(lr, 7, 0))
            ni = jnp.where(rin == 7, ci, pltpu.roll(li, 7, 0))
            xr, xi = x_ref[sl, :_S5_W], x_ref[sl, _S5_W:]
            acc_r = acc_r + (xr * nr + xi * ni)
            acc_i = acc_i + (xr * ni - xi * nr)
            cr, ci = lr[0:1, :], li[0:1, :]
        carry_ref[0:1, :] = cr
        carry_ref[1:2, :] = ci
        dar_ref[...] += jnp.sum(acc_r, axis=0, keepdims=True)
        dai_ref[...] += jnp.sum(acc_i, axis=0, keepdims=True)
        lam = lam_ref[...]
        du_ref[...] = _dot_raw(lam, wb_ref[...], "nt")
        dwb_ref[...] += _dot_raw(u_ref[...], lam, "tn")
        dwc_ref[...] += _dot_raw(x_ref[...], dy_ref[...], "tn")

    nblk = S5_NG // S5_BLK
    blk = pl.BlockSpec((bt, 2 * _S5_W), lambda g, t: (nb - 1 - t, g))
    col = pl.BlockSpec((bt, LANE), lambda g, t: (nb - 1 - t, g))
    a_s = pl.BlockSpec((None, 1, _S5_W), lambda g, t: (g, 0, 0))
    wb_s = pl.BlockSpec((None, LANE, 2 * _S5_W), lambda g, t: (g, 0, 0))
    wc_s = pl.BlockSpec((None, 2 * _S5_W, LANE), lambda g, t: (g, 0, 0))
    return pl.pallas_call(
        body, name=name, grid=(nblk, nb), in_specs=[col, blk, col, wb_s, a_s, a_s, wc_s],
        out_specs=[col, wb_s, wc_s, a_s, a_s],
        out_shape=[jax.ShapeDtypeStruct((T, nblk * LANE), F32), jax.ShapeDtypeStruct((nblk, LANE, 2 * _S5_W), F32),
                   jax.ShapeDtypeStruct((nblk, 2 * _S5_W, LANE), F32), jax.ShapeDtypeStruct((nblk, 1, _S5_W), F32),
                   jax.ShapeDtypeStruct((nblk, 1, _S5_W), F32)],
        scratch_shapes=[pltpu.VMEM((bt, 2 * _S5_W), F32), pltpu.VMEM((bt, 2 * _S5_W), F32), pltpu.VMEM((8, _S5_W), F32)],
        compiler_params=_cparams(2),
    )(dy, x, u, wb, a_re, a_im, wc)


def _norm_bf16(h, g, name):
    return rowwise(f_rmsnorm, [g], [h], [(D, BF16)], bt=512, name=name)[0]


def _norm_bwd(h, g, cts, name):
    n = len(cts) - 1

    def f(p, r):
        y = _rms(r[0], p[0])
        return (y,) * n + (r[0],)

    (dg,), (dh,) = rowwise_vjp(f, [g], [h], cts, [F32], bt=256, name=name)
    return dh, dg


def ffn_fwd(h, g, w_gu, w_down, tag):
    hn = _norm_bf16(h, g, f"{tag}_norm")
    a, gu = ffn_up(hn, w_gu, name=f"{tag}_up")
    h2 = matmul(a, w_down[None], "nn", add=h, name=f"{tag}_down")
    return h2, (h, hn, gu, a)


def ffn_bwd(d, saved, g, w_gu, w_down, tag):
    h, hn, gu, a = saved
    dgu = ffn_dact(d, w_down, gu, name=f"{tag}_dact")
    dwd = matmul(a, d, "tn", name=f"{tag}_dwd")[0]
    dwgu = matmul(hn, dgu, "tn", name=f"{tag}_dwgu")[0]
    dh, dg = matmul_nt_norm_bwd(dgu, w_gu, h, g, d, name=f"{tag}_dhn")
    return dh, dg, dwgu, dwd


_GLA_NC = 4
_SSD_NU = 4


def gla_fwd(h, gm, w_in, w_a2, b_a, ng, w_out, tag):
    hn = _norm_bf16(h, gm, f"{tag}_norm")
    proj = matmul(hn, w_in[None], "nn", name=f"{tag}_in")
    alow = (proj, LANE, 2 * (GLA_QK + GLA_VD) // LANE)
    la = rowwise(f_gla_gate_in_fwd, [w_a2, b_a], [alow], [(GLA_QK, F32)], bt=512, name=f"{tag}_gate")[0]
    og, ss = gla_scan_fwd(proj, la, ng, nc=_GLA_NC, name=f"{tag}_scan")
    h2 = matmul(og, w_out[None], "nn", add=h, name=f"{tag}_proj")
    return h2, (h, hn, proj, la, ss, og)


def gla_bwd(d, saved, gm, w_in, w_a2, b_a, ng, w_out, tag):
    h, hn, proj, la, ss, og = saved
    dog = matmul(d, w_out[None], "nt", name=f"{tag}_dog")
    dwout = matmul(og, d, "tn", name=f"{tag}_dwout")[0]
    dq, dk, dv, dla, dr, dng = gla_scan_bwd(proj, la, ng, ss, dog, nc=_GLA_NC, name=f"{tag}_dscan")
    alow = (proj, LANE, 2 * (GLA_QK + GLA_VD) // LANE)
    (dwa2, dba), (dalow,) = rowwise_vjp(f_gla_gate_in, [w_a2, b_a], [alow], [dla], [BF16], bt=512, name=f"{tag}_dgate")
    dproj = jnp.concatenate([dq, dk, dv, dr, dalow], axis=1)
    dwin = matmul(hn, dproj, "tn", name=f"{tag}_dwin")[0]
    dh, dgm = matmul_nt_norm_bwd(dproj, w_in, h, gm, d, name=f"{tag}_dhn")
    return dh, dgm, dwin, dwa2[:GLA_RANK], dba, dng, dwout


def ssd_fwd(h, gm, w_in, conv_w, conv_b, dtb, alog, dsk, ng, w_out, tag):
    hn = _norm_bf16(h, gm, f"{tag}_norm")
    proj = matmul(hn, w_in[None], "nn", name=f"{tag}_in")
    xbc = ssd_conv_fwd(proj, conv_w, conv_b, name=f"{tag}_conv")
    yg, hs = ssd_scan_fwd(xbc, proj, dtb, alog, dsk, ng, nu=_SSD_NU, name=f"{tag}_scan")
    h2 = matmul(yg, w_out[None], "nn", add=h, name=f"{tag}_proj")
    return h2, (h, hn, proj, xbc, hs, yg)


def ssd_bwd(d, saved, gm, w_in, conv_w, conv_b, dtb, alog, dsk, ng, w_out, tag):
    h, hn, proj, xbc, hs, yg = saved
    dyg = matmul(d, w_out[None], "nt", name=f"{tag}_dyg")
    dwout = matmul(yg, d, "tn", name=f"{tag}_dwout")[0]
    dxs, dbm, dcm, ddt, ddtb, dal, ddsk, dz, dng = ssd_scan_bwd(xbc, proj, dtb, alog, dsk, ng, hs, dyg, nu=_SSD_NU,
                                                               name=f"{tag}_dscan")
    parts = [ssd_conv_bwd(proj, conv_w, conv_b, dout, col0, name=f"{tag}_dconv{k}")
             for k, (dout, col0) in enumerate(((dxs, 0), (dbm, SSD_DI), (dcm, SSD_DI + SSD_GN)))]
    dcw = jnp.concatenate([p[1] for p in parts], axis=1)
    dcb = jnp.concatenate([p[2] for p in parts], axis=1)
    dproj = jnp.concatenate([dz] + [p[0] for p in parts] + [ddt.astype(BF16)], axis=1)
    dwin = matmul(hn, dproj, "tn", name=f"{tag}_dwin")[0]
    dh, dgm = matmul_nt_norm_bwd(dproj, w_in, h, gm, d, name=f"{tag}_dhn")
    return (dh, dgm, dwin, dcw, dcb, ddtb[:, :SSD_H], dal[:, :SSD_H], ddsk[:, :SSD_H], dng, dwout)


_S5_NB = S5_NG // S5_BLK


def _s5_param_args(log_dt, a_re, a_im, b_re, b_im, c_im):
    n = S5_NG * S5_GS
    tr = lambda b: jnp.transpose(b, (0, 2, 1)).reshape(n, S5_P)
    return [log_dt.reshape(S5_NG, 1), a_re, a_im, tr(b_re), tr(b_im), c_im.reshape(n, S5_P)]


def _s5_blockdiag(t):
    nb, gl, a, b = t.shape
    eye = jnp.eye(gl, dtype=t.dtype)
    return (t[:, :, :, None, :] * eye[None, :, None, :, None]).reshape(nb, gl * a, gl * b)


def _s5_diag(t, a, b):
    nb = t.shape[0]
    gl = t.shape[1] // a
    eye = jnp.eye(gl, dtype=t.dtype)
    return jnp.sum(t.reshape(nb, gl, a, gl, b) * eye[None, :, None, :, None], axis=3)


def _s5_weights(bbr, bbi, c_re, cneg):
    sh = (_S5_NB, S5_BLK, S5_GS, S5_P)
    wb = jnp.concatenate([_s5_blockdiag(bbr.reshape(sh)), _s5_blockdiag(bbi.reshape(sh))], axis=2)
    tr = lambda cc: jnp.transpose(cc.reshape(sh), (0, 1, 3, 2))
    wc = jnp.concatenate([_s5_blockdiag(tr(c_re)), _s5_blockdiag(tr(cneg))], axis=1)
    return wb, wc


def s5_fwd(h, gm, prm, dsk, w_glu, tag):
    log_dt, a_re, a_im, b_re, b_im, c_re, c_im = prm
    hn = rowwise(f_rmsnorm, [gm], [h], [(D, F32)], bt=512, name=f"{tag}_norm")[0]
    pargs = _s5_param_args(log_dt, a_re, a_im, b_re, b_im, c_im)
    abr, abi, bbr, bbi, cneg = s5_param_fwd(pargs, name=f"{tag}_param")
    wb, wc = _s5_weights(bbr, bbi, c_re.reshape(S5_NG * S5_GS, S5_P), cneg)
    ar, ai = abr.reshape(_S5_NB, 1, _S5_W), abi.reshape(_S5_NB, 1, _S5_W)
    wb, wc = wb.astype(BF16), wc.astype(BF16)
    x, ycp = s5_scan_fwd(hn, wb, ar, ai, wc, name=f"{tag}_scan")
    yg = rowwise(f_s5_act, [dsk], [ycp, hn], [(D, BF16)], bt=512, name=f"{tag}_act")[0]
    vg = matmul(yg, w_glu[None], "nn", name=f"{tag}_glu")
    h2 = rowwise(f_glu_res, [], [vg, h], [(D, F32)], bt=512, name=f"{tag}_out")[0]
    return h2, (h, hn, pargs, wb, wc, ar, ai, x, ycp, yg, vg)


def s5_bwd(d, saved, gm, dsk, w_glu, tag):
    h, hn, pargs, wb, wc, ar, ai, x, ycp, yg, vg = saved
    _, (dvg,) = rowwise_vjp(f_glu, [], [vg], [d], [BF16], bt=256, name=f"{tag}_dout")
    dwglu = matmul(yg, dvg, "tn", name=f"{tag}_dwglu")[0]
    dyg = matmul(dvg, w_glu[None], "nt", name=f"{tag}_dyg")
    (ddsk,), (dycp, dhn1) = rowwise_vjp(f_s5_act, [dsk], [ycp, hn], [dyg], [F32, F32], bt=256, name=f"{tag}_dact")
    dhn2, dwb, dwc, dar, dai = s5_scan_bwd(dycp, x, hn, wb, ar, ai, wc, name=f"{tag}_dscan")
    dh, dgm = _norm_bwd(h, gm, [dhn1, dhn2, d], f"{tag}_dnorm")
    n = S5_NG * S5_GS
    half = S5_BLK * S5_P
    d_bbr = _s5_diag(dwb[:, :, :half], S5_GS, S5_P).reshape(n, S5_P)
    d_bbi = _s5_diag(dwb[:, :, half:], S5_GS, S5_P).reshape(n, S5_P)
    from_c = lambda t: jnp.transpose(_s5_diag(t, S5_P, S5_GS), (0, 1, 3, 2)).reshape(n, S5_P)
    d_cre = from_c(dwc[:, :half, :])
    d_cneg = from_c(dwc[:, half:, :])
    cts = [dar.reshape(S5_NG, S5_P), dai.reshape(S5_NG, S5_P), d_bbr, d_bbi, d_cneg]
    dlog, dare, daim, dbre_t, dbim_t, dcim = s5_param_bwd(pargs, cts, name=f"{tag}_dparam")
    untr = lambda t: jnp.transpose(t.reshape(S5_NG, S5_GS, S5_P), (0, 2, 1))
    grads = (dlog.reshape(S5_NG), dare, daim, untr(dbre_t), untr(dbim_t),
             d_cre.reshape(S5_NG, S5_GS, S5_P), dcim.reshape(S5_NG, S5_GS, S5_P))
    return dh, dgm, grads, ddsk, dwglu


def _pad_last(w, n):
    return jnp.pad(w, [(0, 0)] * (w.ndim - 1) + [(0, n - w.shape[-1])])


_BIG = ("gla_w_in", "gla_w_out", "ssd_w_in", "ssd_w_out", "s5_w_glu", "ffn_w_gu", "ffn_w_down")


def interleave_gu(w):
    q = w.shape[-1] // 4
    return jnp.concatenate([w[..., :q], w[..., 2 * q:3 * q], w[..., q:2 * q], w[..., 3 * q:]], axis=-1)


def local_step(x, target, W, later_weights=None, later_grads=None, ffn0_grads=None, ffn0_weights=None):
    f32 = lambda a: a.astype(F32)
    row = lambda a: f32(a).reshape(1, -1)

    def layer_args(i):
        m, j = i % 3, i // 3
        gm = row(W["norm_mix_g"][i])
        if m == 0:
            args = (gm, W["gla_w_in"][j], jnp.pad(f32(W["gla_w_a2"][j]), ((0, LANE - GLA_RANK), (0, 0))),
                    row(W["gla_b_a"][j]), row(W["gla_norm_g"][j]), W["gla_w_out"][j])
        elif m == 1:
            pl_ = lambda a: _pad_last(row(a), LANE)
            args = (gm, W["ssd_w_in"][j], f32(W["ssd_conv_w"][j]),
                    row(W["ssd_conv_b"][j]), pl_(W["ssd_dt_bias"][j]), pl_(W["ssd_a_log"][j]), pl_(W["ssd_d"][j]),
                    row(W["ssd_norm_g"][j]), W["ssd_w_out"][j])
        else:
            prm = tuple(f32(W[k][j]) for k in ("s5_log_dt", "s5_a_re", "s5_a_im", "s5_b_re", "s5_b_im", "s5_c_re", "s5_c_im"))
            args = (gm, prm, row(W["s5_d"][j]), W["s5_w_glu"][j])
        return m, j, args

    h = x
    saved, mixers, ffns = [], [], []
    for i in range(DEPTH):
        mixer = layer_args(i)
        mixers.append(mixer)
        m, j, args = mixer
        tag = f"l{i}_{('gla', 'ssd', 's5')[m]}"
        h, sm = (gla_fwd, ssd_fwd, s5_fwd)[m](h, *args, tag)
        if i == 0 and ffn0_weights is not None:
            W = {**W, **ffn0_weights(h)}
        ffn = (row(W["norm_ffn_g"][i]), W["ffn_w_gu"][i], W["ffn_w_down"][i])
        ffns.append(ffn)
        h, sf = ffn_fwd(h, *ffn, f"l{i}_ffn")
        saved.append((sm, sf))
        if i == 0 and later_weights is not None:
            W = {**W, **later_weights(h)}
    loss, dfg, d = loss_head(h, row(W["final_norm_g"]), target, name="loss_head")

    G = {k: [None] * len(v) for k, v in W.items() if k != "final_norm_g"}
    G["final_norm_g"] = dfg.reshape(D)
    for i in reversed(range(DEPTH)):
        m, j, args = mixers[i]
        sm, sf = saved[i]
        if i == 0 and later_grads is not None:
            zero = later_grads(G)
            ffns[0] = (ffns[0][0], ffns[0][1], ffns[0][2] + zero.astype(ffns[0][2].dtype))
        d, dg, dwgu, dwd = ffn_bwd(d, sf, *ffns[i], f"l{i}_ffn")
        G["norm_ffn_g"][i], G["ffn_w_gu"][i], G["ffn_w_down"][i] = dg.reshape(D), dwgu, dwd
        if i == 0 and ffn0_grads is not None:
            zero = ffn0_grads(G)
            args = args[:-1] + (args[-1] + zero.astype(args[-1].dtype),)
        tag = f"l{i}_{('gla', 'ssd', 's5')[m]}"
        if m == 0:
            d, dgm, dwin, dwa2, dba, dng, dwout = gla_bwd(d, sm, *args, tag)
            G["gla_w_in"][j], G["gla_w_a2"][j], G["gla_b_a"][j] = dwin, dwa2, dba.reshape(-1)
            G["gla_norm_g"][j], G["gla_w_out"][j] = dng.reshape(-1), dwout
        elif m == 1:
            d, dgm, dwin, dcw, dcb, ddtb, dal, ddsk, dng, dwout = ssd_bwd(d, sm, *args, tag)
            G["ssd_w_in"][j], G["ssd_conv_w"][j], G["ssd_conv_b"][j] = dwin, dcw, dcb.reshape(-1)
            G["ssd_dt_bias"][j], G["ssd_a_log"][j], G["ssd_d"][j] = ddtb.reshape(-1), dal.reshape(-1), ddsk.reshape(-1)
            G["ssd_norm_g"][j], G["ssd_w_out"][j] = dng.reshape(-1), dwout
        else:
            d, dgm, pg, ddsk, dwglu = s5_bwd(d, sm, args[0], args[2], args[3], tag)
            for k, v in zip(("s5_log_dt", "s5_a_re", "s5_a_im", "s5_b_re", "s5_b_im", "s5_c_re", "s5_c_im"), pg):
                G[k][j] = v
            G["s5_d"][j], G["s5_w_glu"][j] = ddsk.reshape(-1), dwglu
        G["norm_mix_g"][i] = dgm.reshape(D)
    grads = {k: (v if k == "final_norm_g" or k in _BIG else jnp.stack(v)) for k, v in G.items()}
    return loss, d, grads


_MESH = pl.DeviceIdType.MESH
_ANY = pl.BlockSpec(memory_space=pl.ANY)
_DMA = pltpu.SemaphoreType.DMA
_ROWS_ALIGN = 1024


def _place():
    return lax.axis_index("x"), lax.axis_index("y"), lax.axis_index("c")


def _other_chips(x, y):
    return [(1 - x, y), (x, 1 - y), (1 - x, 1 - y)]


def _remote(src, dst, send_sems, recv_sems, k, to):
    return pltpu.make_async_remote_copy(src_ref=src, dst_ref=dst, send_sem=send_sems.at[k], recv_sem=recv_sems.at[k],
                                        device_id=to, device_id_type=_MESH)


def gather_shards(loc, *, name):
    def body(in_ref, out_ref, send_sems, recv_sems, local_sem):
        x, y, c = _place()
        me, sibling = (x, y, c), (x, y, 1 - c)
        chips = _other_chips(x, y)

        def half(px, py, hc):
            return out_ref.at[2 * px + py, hc]

        mine = pltpu.make_async_copy(in_ref, out_ref.at[2 * x + y], local_sem)
        mine.start()
        first = [_remote(in_ref.at[c], half(x, y, c), send_sems, recv_sems, j, (*chip, c)) for j, chip in enumerate(chips)]
        for cp in first:
            cp.start()
        passed = [_remote(half(*chip, c), half(*chip, c), send_sems, recv_sems, 3 + j, sibling) for j, chip in enumerate(chips)]
        for j, chip in enumerate(chips):
            _remote(in_ref.at[c], half(*chip, c), send_sems, recv_sems, j, me).wait_recv()
            passed[j].start()
        for j, chip in enumerate(chips):
            _remote(in_ref.at[c], half(*chip, 1 - c), send_sems, recv_sems, 3 + j, me).wait_recv()
        for cp in first + passed:
            cp.wait_send()
        mine.wait()

    return pl.pallas_call(
        body, name=name, in_specs=[_ANY], out_specs=_ANY,
        out_shape=jax.ShapeDtypeStruct((4,) + loc.shape, loc.dtype),
        scratch_shapes=[_DMA((6,)), _DMA((6,)), _DMA(())],
    )(loc)


def _pos(px, py, perm):
    return 2 * py + px if perm else 2 * px + py


def _part(ref, kind, p, loc):
    if kind == "lead":
        return ref.at[p]
    return ref.at[:, pl.ds(pl.multiple_of(p * loc, LANE), loc)]


def _rows(ref, h, hr):
    return ref.at[pl.ds(h * hr, hr)]


def _rows_block(hr, width):
    return max(b for b in range(16, hr + 1, 16) if hr % b == 0 and (b * width <= (1 << 19) or b == 16))


def gather_big(locs, kinds, *, name):
    n = len(locs)

    def body(*refs):
        ins, outs = refs[:n], refs[n:2 * n]
        send_sems, recv_sems = refs[2 * n + 1:]
        refs[2 * n][...] = jnp.zeros_like(refs[2 * n])
        x, y, c = _place()
        me, sibling = (x, y, c), (x, y, 1 - c)
        chips = _other_chips(x, y)

        def half(i, px, py, h):
            (kind, perm), (rows, loc) = kinds[i], locs[i].shape
            return _rows(_part(outs[i], kind, _pos(px, py, perm), loc), h, rows // 2)

        sends = []
        for i in range(n):
            (kind, perm), (rows, loc) = kinds[i], locs[i].shape
            own = _part(outs[i], kind, _pos(x, y, perm), loc)
            sends.append(_remote(ins[i], own, send_sems, recv_sems, 6 * n + i, sibling))
            sends[-1].start()
            for j, chip in enumerate(chips):
                sends.append(_remote(_rows(ins[i], c, rows // 2), half(i, x, y, c), send_sems, recv_sems, 6 * i + j, (*chip, c)))
                sends[-1].start()
        for i in range(n):
            hr = locs[i].shape[0] // 2
            for j, chip in enumerate(chips):
                _remote(_rows(ins[i], c, hr), half(i, *chip, c), send_sems, recv_sems, 6 * i + j, me).wait_recv()
                sends.append(_remote(half(i, *chip, c), half(i, *chip, c), send_sems, recv_sems, 6 * i + 3 + j, sibling))
                sends[-1].start()
        for i in range(n):
            (kind, perm), (rows, loc) = kinds[i], locs[i].shape
            for j, chip in enumerate(chips):
                _remote(_rows(ins[i], c, rows // 2), half(i, *chip, 1 - c), send_sems, recv_sems, 6 * i + 3 + j, me).wait_recv()
            _remote(ins[i], _part(outs[i], kind, _pos(x, y, perm), loc), send_sems, recv_sems, 6 * n + i, me).wait_recv()
        for cp in sends:
            cp.wait_send()

    def out_shape(a, kind):
        rows, loc = a.shape
        return jax.ShapeDtypeStruct((4, rows, loc) if kind == "lead" else (rows, 4 * loc), a.dtype)

    outs = pl.pallas_call(
        body, name=name, in_specs=[_ANY] * n, out_specs=[_ANY] * n + [pl.BlockSpec(memory_space=pltpu.VMEM)],
        out_shape=[out_shape(a, k[0]) for a, k in zip(locs, kinds)] + [jax.ShapeDtypeStruct((8, LANE), F32)],
        scratch_shapes=[_DMA((7 * n,)), _DMA((7 * n,))],
    )(*locs)
    return list(outs[:n]), outs[n][0, 0]


_HBM = pl.BlockSpec(memory_space=pltpu.HBM)
_SEM = pl.BlockSpec(memory_space=pltpu.SEMAPHORE)
_EFFECT = pltpu.SideEffectType.DATAFLOW_SIDE_EFFECTING


def _in_hbm(a):
    return pltpu.with_memory_space_constraint(a, pltpu.HBM)


def _gather_ici_copies(ins, lands, kinds, shapes, send_sems, recv_sems):
    x, y, c = _place()
    sends, arrivals = [], []
    for i, ((kind, perm), (rows, loc)) in enumerate(zip(kinds, shapes)):
        hr = rows // 2
        mine = _part(lands[i], kind, _pos(x, y, perm), loc)
        sends.append(_remote(ins[i], mine, send_sems, recv_sems, 4 * i + 3, (x, y, 1 - c)))
        arrivals.append(_remote(ins[i], mine, send_sems, recv_sems, 4 * i + 3, (x, y, c)))
        for j, (px, py) in enumerate(_other_chips(x, y)):
            sends.append(_remote(_rows(ins[i], c, hr), _rows(mine, c, hr), send_sems, recv_sems, 4 * i + j, (px, py, c)))
            theirs = _rows(_part(lands[i], kind, _pos(px, py, perm), loc), c, hr)
            arrivals.append(_remote(_rows(ins[i], c, hr), theirs, send_sems, recv_sems, 4 * i + j, (x, y, c)))
    return sends, arrivals


def gather_start(locs, kinds, *, name):
    n = len(locs)
    shapes = [a.shape for a in locs]

    def land_shape(a, kind):
        rows, loc = a.shape
        return (4, rows, loc) if kind == "lead" else (rows, 4 * loc)

    def body(*refs):
        sends, _ = _gather_ici_copies(refs[:n], refs[n:2 * n], kinds, shapes, refs[2 * n], refs[2 * n + 1])
        for cp in sends:
            cp.start()
        refs[-1][...] = jnp.zeros_like(refs[-1])

    lands = [lax.empty(land_shape(a, k[0]), a.dtype) for a, k in zip(locs, kinds)]
    outs = pl.pallas_call(
        body, name=name, in_specs=[_HBM] * (2 * n), out_specs=[_SEM, _SEM] + [_HBM] * (2 * n) + [pl.BlockSpec(memory_space=pltpu.VMEM)],
        out_shape=[_DMA((4 * n,)), _DMA((4 * n,))] + [pltpu.HBM(a.shape, a.dtype) for a in locs]
        + [pltpu.HBM(l.shape, l.dtype) for l in lands] + [jax.ShapeDtypeStruct((8, LANE), F32)],
        input_output_aliases={i: 2 + i for i in range(2 * n)},
        compiler_params=pltpu.CompilerParams(has_side_effects=_EFFECT),
    )(*[_in_hbm(a) for a in locs], *[_in_hbm(l) for l in lands])
    return outs[0], outs[1], list(outs[2:2 + n]), list(outs[2 + n:2 + 2 * n]), outs[-1][0, 0]


def gather_wait(send_sems, recv_sems, locs, lands, kinds, after, *, name):
    n = len(locs)
    shapes = [a.shape for a in locs]

    def body(*refs):
        sends, arrivals = _gather_ici_copies(refs[:n], refs[n:2 * n], kinds, shapes, refs[2 * n], refs[2 * n + 1])
        for cp in sends:
            cp.wait_send()
        for cp in arrivals:
            cp.wait_recv()

    outs = pl.pallas_call(
        body, name=name, in_specs=[_HBM] * (2 * n) + [_SEM, _SEM, _ANY], out_specs=[_HBM] * (2 * n),
        out_shape=[pltpu.HBM(a.shape, a.dtype) for a in locs] + [pltpu.HBM(l.shape, l.dtype) for l in lands],
        input_output_aliases={i: i for i in range(2 * n)},
        compiler_params=pltpu.CompilerParams(has_side_effects=_EFFECT),
    )(*locs, *lands, send_sems, recv_sems, after)
    return list(outs[n:])


def gather_finish(lands, kinds, shapes, *, name):
    n = len(lands)

    def body(*refs):
        bufs = refs[n:2 * n]
        send_sems, recv_sems = refs[2 * n:]
        x, y, c = _place()
        sends = []
        for i, ((kind, perm), (rows, loc)) in enumerate(zip(kinds, shapes)):
            for j, (px, py) in enumerate(_other_chips(x, y)):
                part = _part(bufs[i], kind, _pos(px, py, perm), loc)
                sends.append(_remote(_rows(part, c, rows // 2), _rows(part, c, rows // 2), send_sems, recv_sems, 3 * i + j, (x, y, 1 - c)))
                sends[-1].start()
        for i, ((kind, perm), (rows, loc)) in enumerate(zip(kinds, shapes)):
            for j, (px, py) in enumerate(_other_chips(x, y)):
                part = _part(bufs[i], kind, _pos(px, py, perm), loc)
                _remote(_rows(part, c, rows // 2), _rows(part, 1 - c, rows // 2), send_sems, recv_sems, 3 * i + j, (x, y, c)).wait_recv()
        for cp in sends:
            cp.wait_send()

    return list(pl.pallas_call(
        body, name=name, in_specs=[_ANY] * n, out_specs=[_ANY] * n,
        out_shape=[jax.ShapeDtypeStruct(l.shape, l.dtype) for l in lands],
        input_output_aliases={i: i for i in range(n)}, scratch_shapes=[_DMA((3 * n,)), _DMA((3 * n,))],
    )(*lands))


def _scatter_copies(qs, lands, kinds, locs, send_sems, recv_sems):
    x, y, c = _place()
    sends, arrivals = [], []
    for i, (kind, perm) in enumerate(kinds):
        for j, (px, py) in enumerate(_other_chips(x, y)):
            src = _part(qs[i], kind, _pos(px, py, perm), locs[i])
            sends.append(_remote(src, lands[i].at[j], send_sems, recv_sems, 3 * i + j, (px, py, c)))
            arrivals.append(_remote(src, lands[i].at[j], send_sems, recv_sems, 3 * i + j, (x, y, c)))
    return sends, arrivals


def _scatter_land(q, kind, loc):
    return (3, q.shape[1] if kind == "lead" else q.shape[0], loc)


def scatter_start(qs, kinds, locs, *, name):
    n = len(qs)

    def body(*refs):
        sends, _ = _scatter_copies(refs[:n], refs[n:2 * n], kinds, locs, refs[2 * n], refs[2 * n + 1])
        for cp in sends:
            cp.start()
        refs[-1][...] = jnp.zeros_like(refs[-1])

    lands = [lax.empty(_scatter_land(q, k[0], l), q.dtype) for q, k, l in zip(qs, kinds, locs)]
    outs = pl.pallas_call(
        body, name=name, in_specs=[_HBM] * (2 * n), out_specs=[_SEM, _SEM] + [_HBM] * (2 * n) + [pl.BlockSpec(memory_space=pltpu.VMEM)],
        out_shape=[_DMA((3 * n,)), _DMA((3 * n,))] + [pltpu.HBM(q.shape, q.dtype) for q in qs]
        + [pltpu.HBM(l.shape, l.dtype) for l in lands] + [jax.ShapeDtypeStruct((8, LANE), F32)],
        input_output_aliases={i: 2 + i for i in range(2 * n)},
        compiler_params=pltpu.CompilerParams(has_side_effects=_EFFECT),
    )(*[_in_hbm(q) for q in qs], *[_in_hbm(l) for l in lands])
    return outs[0], outs[1], list(outs[2:2 + n]), list(outs[2 + n:2 + 2 * n]), outs[-1][0, 0]


def scatter_wait(send_sems, recv_sems, qs, lands, kinds, locs, after, *, name):
    n = len(qs)

    def body(*refs):
        sends, arrivals = _scatter_copies(refs[:n], refs[n:2 * n], kinds, locs, refs[2 * n], refs[2 * n + 1])
        for cp in sends:
            cp.wait_send()
        for cp in arrivals:
            cp.wait_recv()

    outs = pl.pallas_call(
        body, name=name, in_specs=[_HBM] * (2 * n) + [_SEM, _SEM, _ANY], out_specs=[_HBM] * (2 * n),
        out_shape=[pltpu.HBM(q.shape, q.dtype) for q in qs] + [pltpu.HBM(l.shape, l.dtype) for l in lands],
        input_output_aliases={i: i for i in range(2 * n)},
        compiler_params=pltpu.CompilerParams(has_side_effects=_EFFECT),
    )(*qs, *lands, send_sems, recv_sems, after)
    return list(outs[:n]), list(outs[n:])


def _pair_swap_copies(ins, lands, kinds, send_sems, recv_sems):
    x, y, c = _place()
    sends, arrivals = [], []
    for i, (kind, _) in enumerate(kinds):
        if kind == "lead":
            hr = ins[i].shape[1] // 2
            src = ins[i].at[:, pl.ds((1 - c) * hr, hr)]
        else:
            src = _rows(ins[i], 1 - c, ins[i].shape[0] // 2)
        sends.append(_remote(src, lands[i], send_sems, recv_sems, i, (x, y, 1 - c)))
        arrivals.append(_remote(src, lands[i], send_sems, recv_sems, i, (x, y, c)))
    return sends, arrivals


def _pair_swap_land(a, kind):
    s = a.shape
    return (4, s[1] // 2, s[2]) if kind == "lead" else (s[0] // 2, s[1])


def pair_swap_start(ps, kinds, *, name):
    n = len(ps)

    def body(*refs):
        sends, _ = _pair_swap_copies(refs[:n], refs[n:2 * n], kinds, refs[2 * n], refs[2 * n + 1])
        for cp in sends:
            cp.start()
        refs[-1][...] = jnp.zeros_like(refs[-1])

    lands = [lax.empty(_pair_swap_land(p, k[0]), p.dtype) for p, k in zip(ps, kinds)]
    outs = pl.pallas_call(
        body, name=name, in_specs=[_HBM] * (2 * n), out_specs=[_SEM, _SEM] + [_HBM] * (2 * n) + [pl.BlockSpec(memory_space=pltpu.VMEM)],
        out_shape=[_DMA((n,)), _DMA((n,))] + [pltpu.HBM(p.shape, p.dtype) for p in ps]
        + [pltpu.HBM(l.shape, l.dtype) for l in lands] + [jax.ShapeDtypeStruct((8, LANE), F32)],
        input_output_aliases={i: 2 + i for i in range(2 * n)},
        compiler_params=pltpu.CompilerParams(has_side_effects=_EFFECT),
    )(*[_in_hbm(p) for p in ps], *[_in_hbm(l) for l in lands])
    return outs[0], outs[1], list(outs[2:2 + n]), list(outs[2 + n:2 + 2 * n]), outs[-1][0, 0]


def pair_swap_wait(send_sems, recv_sems, ps, lands, kinds, after, *, name):
    n = len(ps)

    def body(*refs):
        sends, arrivals = _pair_swap_copies(refs[:n], refs[n:2 * n], kinds, refs[2 * n], refs[2 * n + 1])
        for cp in sends:
            cp.wait_send()
        for cp in arrivals:
            cp.wait_recv()

    outs = pl.pallas_call(
        body, name=name, in_specs=[_HBM] * (2 * n) + [_SEM, _SEM, _ANY], out_specs=[_HBM] * (2 * n),
        out_shape=[pltpu.HBM(p.shape, p.dtype) for p in ps] + [pltpu.HBM(l.shape, l.dtype) for l in lands],
        input_output_aliases={i: i for i in range(2 * n)},
        compiler_params=pltpu.CompilerParams(has_side_effects=_EFFECT),
    )(*ps, *lands, send_sems, recv_sems, after)
    return list(outs[:n]), list(outs[n:])


def pair_swap(ps, kinds, *, name):
    n = len(ps)

    def body(*refs):
        ins, outs = refs[:n], refs[n:2 * n]
        send_sems, recv_sems = refs[2 * n:]
        x, y, c = _place()
        cps = []
        for i in range(n):
            if kinds[i][0] == "lead":
                hr = ps[i].shape[1] // 2
                src = ins[i].at[:, pl.ds((1 - c) * hr, hr)]
            else:
                hr = ps[i].shape[0] // 2
                src = _rows(ins[i], 1 - c, hr)
            cps.append(_remote(src, outs[i], send_sems, recv_sems, i, (x, y, 1 - c)))
            cps[-1].start()
        for cp in cps:
            cp.wait()

    def out_shape(a, kind):
        s = a.shape
        return jax.ShapeDtypeStruct((4, s[1] // 2, s[2]) if kind == "lead" else (s[0] // 2, s[1]), a.dtype)

    return pl.pallas_call(
        body, name=name, in_specs=[_ANY] * n, out_specs=[_ANY] * n,
        out_shape=[out_shape(a, k[0]) for a, k in zip(ps, kinds)], scratch_shapes=[_DMA((n,)), _DMA((n,))],
    )(*ps)


def pair_add(p, got, c_arr, kind, *, name):
    if kind == "lead":
        _, hr, cols = got.shape
        br = _rows_block(hr, cols)
        nb = hr // br
        grid = (4, nb)
        p_spec = pl.BlockSpec((None, br, cols), lambda s, i, cr: (s, cr[0] * nb + i, 0))
        g_spec = pl.BlockSpec((None, br, cols), lambda s, i, cr: (s, i, 0))
    else:
        hr, w = got.shape
        br = _rows_block(hr, w)
        nb = hr // br
        grid = (nb,)
        p_spec = pl.BlockSpec((br, w), lambda i, cr: (cr[0] * nb + i, 0))
        g_spec = pl.BlockSpec((br, w), lambda i, cr: (i, 0))

    def body(c_ref, p_ref, g_ref, o_ref):
        o_ref[...] = (p_ref[...] + g_ref[...]).astype(o_ref.dtype)

    return pl.pallas_call(
        body, name=name, out_shape=jax.ShapeDtypeStruct(got.shape, BF16),
        grid_spec=pltpu.PrefetchScalarGridSpec(num_scalar_prefetch=1, grid=grid, in_specs=[p_spec, g_spec], out_specs=g_spec),
        compiler_params=_cparams(len(grid)),
    )(c_arr, p, got)


def chip_scatter(qs, kinds, locs, *, name):
    n = len(qs)

    def body(*refs):
        ins, outs = refs[:n], refs[n:2 * n]
        send_sems, recv_sems = refs[2 * n:]
        x, y, c = _place()
        cps = []
        for i in range(n):
            kind, perm = kinds[i]
            for j, (px, py) in enumerate(_other_chips(x, y)):
                cps.append(_remote(_part(ins[i], kind, _pos(px, py, perm), locs[i]), outs[i].at[j], send_sems, recv_sems,
                                   3 * i + j, (px, py, c)))
                cps[-1].start()
        for cp in cps:
            cp.wait()

    def out_shape(a, kind, loc):
        hr = a.shape[1] if kind == "lead" else a.shape[0]
        return jax.ShapeDtypeStruct((3, hr, loc), a.dtype)

    return pl.pallas_call(
        body, name=name, in_specs=[_ANY] * n, out_specs=[_ANY] * n,
        out_shape=[out_shape(a, k[0], l) for a, k, l in zip(qs, kinds, locs)],
        scratch_shapes=[_DMA((3 * n,)), _DMA((3 * n,))],
    )(*qs)


def chip_add(q, r, pos_arr, c_arr, kind, loc, *, name):
    _, hr, _ = r.shape
    br = _rows_block(hr, loc)
    nb = hr // br
    if kind == "lead":
        q_spec = pl.BlockSpec((None, br, loc), lambda i, pr, cr: (pr[0], i, 0))
    else:
        q_spec = pl.BlockSpec((br, loc), lambda i, pr, cr: (i, pr[0]))
    r_spec = pl.BlockSpec((3, br, loc), lambda i, pr, cr: (0, i, 0))
    o_spec = pl.BlockSpec((br, loc), lambda i, pr, cr: (cr[0] * nb + i, 0))

    def body(p_ref, c_ref, q_ref, r_ref, o_ref):
        acc = q_ref[...].astype(F32)
        for j in range(3):
            acc = acc + r_ref[j].astype(F32)
        o_ref[...] = acc

    return pl.pallas_call(
        body, name=name, out_shape=jax.ShapeDtypeStruct((2 * hr, loc), F32),
        grid_spec=pltpu.PrefetchScalarGridSpec(num_scalar_prefetch=2, grid=(nb,), in_specs=[q_spec, r_spec], out_specs=o_spec),
        compiler_params=_cparams(1),
    )(pos_arr, c_arr, q, r)


def share_rows(fs, *, name):
    n = len(fs)

    def body(*refs):
        bufs = refs[n:2 * n]
        send_sems, recv_sems = refs[2 * n:]
        x, y, c = _place()
        cps = []
        for i in range(n):
            hr = fs[i].shape[0] // 2
            cps.append(_remote(_rows(bufs[i], c, hr), _rows(bufs[i], c, hr), send_sems, recv_sems, i, (x, y, 1 - c)))
            cps[-1].start()
        for i, cp in enumerate(cps):
            hr = fs[i].shape[0] // 2
            _remote(_rows(bufs[i], c, hr), _rows(bufs[i], 1 - c, hr), send_sems, recv_sems, i, (x, y, c)).wait_recv()
            cp.wait_send()

    return pl.pallas_call(
        body, name=name, in_specs=[_ANY] * n, out_specs=[_ANY] * n,
        out_shape=[jax.ShapeDtypeStruct(f.shape, f.dtype) for f in fs],
        input_output_aliases={i: i for i in range(n)}, scratch_shapes=[_DMA((n,)), _DMA((n,))],
    )(*fs)


def _gather_all_copies(v_ref, land_ref, send_sems, recv_sems):
    x, y, c = _place()
    flip = lambda p, m: 1 - p if m else p
    idx = lambda p: 4 * p[0] + 2 * p[1] + p[2]
    sends, arrivals = [], []
    for k, m in enumerate(range(1, 8)):
        p = (flip(x, m & 4), flip(y, m & 2), flip(c, m & 1))
        sends.append(_remote(v_ref, land_ref.at[idx((x, y, c))], send_sems, recv_sems, k, p))
        arrivals.append(_remote(v_ref, land_ref.at[idx(p)], send_sems, recv_sems, k, (x, y, c)))
    return sends, arrivals


def gather_all_start(v, *, name):
    def body(v_ref, land_ref, send_sems, recv_sems, v_thru, land_thru, token):
        sends, _ = _gather_all_copies(v_ref, land_ref, send_sems, recv_sems)
        for cp in sends:
            cp.start()
        token[...] = jnp.zeros_like(token)

    land = jnp.zeros((8,) + v.shape, v.dtype)
    outs = pl.pallas_call(
        body, name=name, in_specs=[_HBM, _HBM], out_specs=[_SEM, _SEM, _HBM, _HBM, pl.BlockSpec(memory_space=pltpu.VMEM)],
        out_shape=[_DMA((7,)), _DMA((7,)), pltpu.HBM(v.shape, v.dtype), pltpu.HBM(land.shape, land.dtype),
                   jax.ShapeDtypeStruct((8, LANE), F32)],
        input_output_aliases={0: 2, 1: 3}, compiler_params=pltpu.CompilerParams(has_side_effects=_EFFECT),
    )(_in_hbm(v), _in_hbm(land))
    return outs[0], outs[1], outs[2], outs[3], outs[4][0, 0]


def gather_all_wait(send_sems, recv_sems, v, land, after, *, name):
    def body(v_ref, land_ref, send_sems, recv_sems, after_ref, v_dead, got_ref):
        sends, arrivals = _gather_all_copies(v_ref, land_ref, send_sems, recv_sems)
        for cp in sends:
            cp.wait_send()
        for cp in arrivals:
            cp.wait_recv()

    return pl.pallas_call(
        body, name=name, in_specs=[_HBM, _HBM, _SEM, _SEM, _ANY], out_specs=[_HBM, _HBM],
        out_shape=[pltpu.HBM(v.shape, v.dtype), pltpu.HBM(land.shape, land.dtype)],
        input_output_aliases={0: 0, 1: 1}, compiler_params=pltpu.CompilerParams(has_side_effects=_EFFECT),
    )(v, land, send_sems, recv_sems, after)[1]


def sum_slots(land, v, me_arr, *, name):
    n, R, L = land.shape
    br = _pick(R, _ROWS_ALIGN, 8)

    def body(me_ref, land_ref, v_ref, o_ref):
        acc = None
        for i in range(n):
            term = jnp.where(me_ref[0] == i, v_ref[...], land_ref[i])
            acc = term if acc is None else acc + term
        o_ref[...] = acc

    row = pl.BlockSpec((br, L), lambda i, me: (i, 0))
    return pl.pallas_call(
        body, name=name, out_shape=jax.ShapeDtypeStruct((R, L), land.dtype),
        grid_spec=pltpu.PrefetchScalarGridSpec(num_scalar_prefetch=1, grid=(R // br,),
                                               in_specs=[pl.BlockSpec((n, br, L), lambda i, me: (0, i, 0)), row], out_specs=row),
        compiler_params=_cparams(1),
    )(me_arr, land, v)


def adamw(w, g, m, v, *, name):
    shape = w.shape
    size = math.prod(shape)
    last = shape[-1]
    if last % LANE != 0 and size % LANE == 0 and size <= (1 << 20):
        last = LANE
    rows = size // last
    budget = (1 << 18) // last
    br = rows
    if rows > budget:
        br = max(c for c in range(8, budget + 1, 8) if rows % c == 0)
    v2 = lambda a: a.reshape(rows, last)

    def body(w_ref, g_ref, m_ref, v_ref, d_ref, nm_ref, nv_ref):
        gg = g_ref[...]
        nm = ADAM_B1 * m_ref[...] + (1.0 - ADAM_B1) * gg
        nv = ADAM_B2 * v_ref[...] + (1.0 - ADAM_B2) * (gg * gg)
        m_hat = nm / (1.0 - ADAM_B1 ** ADAM_STEP)
        v_hat = nv / (1.0 - ADAM_B2 ** ADAM_STEP)
        d_ref[...] = -ADAM_LR * (m_hat / (jnp.sqrt(v_hat) + ADAM_EPS) + ADAM_WD * w_ref[...])
        nm_ref[...] = nm
        nv_ref[...] = nv

    spec = pl.BlockSpec((br, last), lambda i: (i, 0))
    outs = pl.pallas_call(
        body, name=name, grid=(rows // br,), in_specs=[spec] * 4, out_specs=[spec] * 3,
        out_shape=[jax.ShapeDtypeStruct((rows, last), F32)] * 3, compiler_params=_cparams(1),
    )(v2(w), v2(g), v2(m), v2(v))
    return [o.reshape(shape) for o in outs]


_WEIGHTS = ["norm_mix_g", "norm_ffn_g", "gla_w_in", "gla_w_a2", "gla_b_a", "gla_norm_g", "gla_w_out", "ssd_w_in",
            "ssd_conv_w", "ssd_conv_b", "ssd_dt_bias", "ssd_a_log", "ssd_d", "ssd_norm_g", "ssd_w_out", "s5_log_dt",
            "s5_a_re", "s5_a_im", "s5_b_re", "s5_b_im", "s5_c_re", "s5_c_im", "s5_d", "s5_w_glu", "ffn_w_gu",
            "ffn_w_down", "final_norm_g"]
_SHARD_AXIS = {"gla_w_in": 2, "gla_w_a2": 2, "gla_b_a": 1, "gla_norm_g": 1, "gla_w_out": 1, "ssd_w_in": 2,
               "ssd_conv_w": 2, "ssd_w_out": 1, "s5_d": 1, "s5_w_glu": 2, "ffn_w_gu": 2, "ffn_w_down": 1}
_SMALL_SHARDED = [n for n in _WEIGHTS if n in _SHARD_AXIS and n not in _BIG]
_REPLICATED = [n for n in _WEIGHTS if n not in _SHARD_AXIS]
_BIG_KIND = {"gla_w_in": ("lead", False), "gla_w_out": ("lead", False), "ssd_w_in": ("lead", False),
             "ssd_w_out": ("lead", False), "s5_w_glu": ("cols", False), "ffn_w_gu": ("cols", True),
             "ffn_w_down": ("lead", False)}
_PADDED_IN = {"gla_w_in": GLA_INP, "ssd_w_in": SSD_INP}


def _to_rows(flat, parts=1):
    per = -(-flat.shape[0] // (parts * LANE * _ROWS_ALIGN)) * _ROWS_ALIGN
    flat = jnp.pad(flat, (0, parts * per * LANE - flat.shape[0]))
    return flat.reshape(parts, per, LANE)


def _big_layers(local):
    return [(n, j, local[n][j].reshape(-1, local[n].shape[-1])) for n in _BIG for j in range(local[n].shape[0])]


def _in_layer0(n, j):
    return j == 0 and n in ("gla_w_in", "gla_w_out", "ffn_w_gu", "ffn_w_down")


def _assemble(n, g):
    if n in _PADDED_IN:
        return jnp.concatenate([g[s] for s in range(4)] + [jnp.zeros((g.shape[1], _PADDED_IN[n] - 4 * g.shape[2]), BF16)], axis=1)
    if _BIG_KIND[n][0] == "lead":
        return g.reshape(4 * g.shape[1], g.shape[2])
    return g


def _is_gla0(n, j):
    return j == 0 and n in ("gla_w_in", "gla_w_out")


def _gather_first(local):
    layers = _big_layers(local)
    first = [l for l in layers if _is_gla0(l[0], l[1])]
    full = {n: [None] * local[n].shape[0] for n in _BIG}
    got, done = gather_big([w.astype(BF16) for _, _, w in first], [_BIG_KIND[n] for n, _, _ in first], name="gather_weights_first")
    for (n, j, _), g in zip(first, got):
        full[n][j] = _assemble(n, g)
    flat = jnp.concatenate([local[n].astype(F32).reshape(-1) for n in _SMALL_SHARDED])
    got = gather_shards(_to_rows(flat, 2), name="gather_small_weights").reshape(4, -1)
    off = 0
    for n in _SMALL_SHARDED:
        bs = local[n].shape
        sz = math.prod(bs)
        seg = got[:, off:off + sz].reshape((4,) + bs)
        off += sz
        ax = _SHARD_AXIS[n]
        full[n] = jnp.moveaxis(seg, 0, ax).reshape(bs[:ax] + (4 * bs[ax],) + bs[ax + 1:])
    pending = {}
    for tag, want in (("ffn0", _is_ffn0), ("later", lambda n, j: not _in_layer0(n, j))):
        group = [l for l in layers if want(l[0], l[1])]
        kinds = [_BIG_KIND[n] for n, _, _ in group]
        ops = [(w + done if k == 0 else w).astype(BF16) for k, (_, _, w) in enumerate(group)]
        send_sems, recv_sems, locs, lands, done = gather_start(ops, kinds, name=f"gather_weights_start_{tag}")
        pending[tag] = (group, kinds, send_sems, recv_sems, locs, lands)
    return full, pending, done


def _gather_rest(full, pending, after, tag):
    group, kinds, send_sems, recv_sems, locs, lands = pending
    lands = gather_wait(send_sems, recv_sems, locs, lands, kinds, after, name=f"gather_weights_wait_{tag}")
    lands = gather_finish(lands, kinds, [w.shape for _, _, w in group], name=f"gather_weights_finish_{tag}")
    out = {n: list(full[n]) for n in _BIG}
    for (n, j, _), g in zip(group, lands):
        out[n][j] = _assemble(n, g)
    return out


def _reduce_ops(grads, local, want):
    ops = []
    for n in _BIG:
        kind = _BIG_KIND[n]
        for j, g in enumerate(grads[n]):
            if not want(n, j):
                continue
            loc = local[n].shape[-1] if kind[0] == "cols" or n in _PADDED_IN else g.shape[1]
            if n in _PADDED_IN:
                g = jnp.stack([g[:, s * loc:(s + 1) * loc] for s in range(4)])
            elif kind[0] == "lead":
                g = g.reshape(4, g.shape[0] // 4, g.shape[1])
            ops.append((n, j, kind, loc, g))
    return ops


def _pair_sums(ops, c_arr, tag):
    gots = pair_swap([o[4] for o in ops], [o[2] for o in ops], name=f"reduce_pair_swap_{tag}")
    return [pair_add(o[4], got, c_arr, o[2][0], name=f"reduce_pair_add_{o[0]}{o[1]}") for o, got in zip(ops, gots)]


def _is_ffn0(n, j):
    return j == 0 and n in ("ffn_w_gu", "ffn_w_down")


def _reduce_start(grads, local, c, want, tag):
    ops = _reduce_ops(grads, local, want)
    c_arr = jnp.reshape(c, (1,)).astype(jnp.int32)
    qs = _pair_sums(ops, c_arr, tag)
    send_sems, recv_sems, qs, lands, zero = scatter_start(qs, [o[2] for o in ops], [o[3] for o in ops],
                                                          name=f"reduce_scatter_start_{tag}")
    return (ops, send_sems, recv_sems, qs, lands, tag), zero


def _reduce_swap_start(grads, local, c, want, tag):
    ops = _reduce_ops(grads, local, want)
    send_sems, recv_sems, ps, lands, zero = pair_swap_start([o[4] for o in ops], [o[2] for o in ops],
                                                            name=f"reduce_pair_swap_start_{tag}")
    return (ops, send_sems, recv_sems, ps, lands, tag), zero


def _reduce_scatter_after(pending, after, c):
    ops, send_sems, recv_sems, ps, lands, tag = pending
    ps, gots = pair_swap_wait(send_sems, recv_sems, ps, lands, [o[2] for o in ops], after, name=f"reduce_pair_swap_wait_{tag}")
    c_arr = jnp.reshape(c, (1,)).astype(jnp.int32)
    qs = [pair_add(p, got, c_arr, o[2][0], name=f"reduce_pair_add_{o[0]}{o[1]}") for o, p, got in zip(ops, ps, gots)]
    send_sems, recv_sems, qs, lands, zero = scatter_start(qs, [o[2] for o in ops], [o[3] for o in ops],
                                                          name=f"reduce_scatter_start_{tag}")
    return (ops, send_sems, recv_sems, qs, lands, tag), zero


def _reduce_big(grads, local, pendings, after, x, y, c):
    c_arr = jnp.reshape(c, (1,)).astype(jnp.int32)
    ops, qs, rs = [], [], []
    for ops_p, send_sems, recv_sems, qs_p, lands, tag in pendings:
        qs_p, rs_p = scatter_wait(send_sems, recv_sems, qs_p, lands, [o[2] for o in ops_p], [o[3] for o in ops_p], after,
                                  name=f"reduce_scatter_wait_{tag}")
        ops, qs, rs = ops + ops_p, qs + qs_p, rs + rs_p
    ops_f = _reduce_ops(grads, local, lambda n, j: _in_layer0(n, j) and not _is_ffn0(n, j))
    qs_f = _pair_sums(ops_f, c_arr, "first")
    s_sems, r_sems, qs_f, lands_f, zero = scatter_start(qs_f, [o[2] for o in ops_f], [o[3] for o in ops_f],
                                                        name="reduce_scatter_start_first")
    qs[0] = qs[0] + zero.astype(qs[0].dtype)
    red = _reduce_close(ops, qs, rs, x, y, c_arr, "later")
    done = {n: jnp.stack([red[(n, j)] for j in range(local[n].shape[0])]).reshape(local[n].shape)
            for n in _BIG if all((n, j) in red for j in range(local[n].shape[0]))}
    return done, red, (ops_f, s_sems, r_sems, qs_f, lands_f)


def _reduce_close(ops, qs, rs, x, y, c_arr, tag):
    fs = [chip_add(q, r, jnp.reshape(_pos(x, y, o[2][1]), (1,)).astype(jnp.int32), c_arr, o[2][0], o[3],
                   name=f"reduce_chip_add_{o[0]}{o[1]}") for o, q, r in zip(ops, qs, rs)]
    outs = share_rows(fs, name=f"reduce_share_{tag}")
    return {(o[0], o[1]): r for o, r in zip(ops, outs)}


def _reduce_big_first(pending, red, after, local, x, y, c):
    ops_f, s_sems, r_sems, qs_f, lands_f = pending
    qs_f, rs_f = scatter_wait(s_sems, r_sems, qs_f, lands_f, [o[2] for o in ops_f], [o[3] for o in ops_f], after,
                              name="reduce_scatter_wait_first")
    red = {**red, **_reduce_close(ops_f, qs_f, rs_f, x, y, jnp.reshape(c, (1,)).astype(jnp.int32), "first")}
    names = sorted({o[0] for o in ops_f})
    return {n: jnp.stack([red[(n, j)] for j in range(local[n].shape[0])]).reshape(local[n].shape) for n in names}


def _reduce_small_start(grads):
    names = _REPLICATED + _SMALL_SHARDED
    flat = jnp.concatenate([grads[n].astype(F32).reshape(-1) for n in names])
    n_el = flat.shape[0]
    rows = -(-n_el // (LANE * 8)) * 8
    v = jnp.pad(flat, (0, rows * LANE - n_el)).reshape(rows, LANE)
    outs = gather_all_start(v, name="reduce_small_start")
    return outs[:4], outs[4]


def _reduce_small(pending, after, grads, local, x, y, c):
    names = _REPLICATED + _SMALL_SHARDED
    send_sems, recv_sems, v, land = pending
    land = gather_all_wait(send_sems, recv_sems, v, land, after, name="reduce_small_wait")
    me = jnp.reshape(4 * x + 2 * y + c, (1,)).astype(jnp.int32)
    red = sum_slots(land, v, me, name="reduce_small_add").reshape(-1)
    out, off = {}, 0
    for n in names:
        sz = math.prod(grads[n].shape)
        g = red[off:off + sz].reshape(grads[n].shape)
        off += sz
        if n in _SHARD_AXIS:
            ax = _SHARD_AXIS[n]
            loc = local[n].shape[ax]
            g = lax.dynamic_slice_in_dim(g, (2 * x + y) * loc, loc, axis=ax)
        out[n] = g
    return out


def kernel(x, norm_mix_g, norm_ffn_g, gla_w_in, gla_w_a2, gla_b_a, gla_norm_g, gla_w_out, ssd_w_in, ssd_conv_w, ssd_conv_b, ssd_dt_bias, ssd_a_log, ssd_d, ssd_norm_g, ssd_w_out, s5_log_dt, s5_a_re, s5_a_im, s5_b_re, s5_b_im, s5_c_re, s5_c_im, s5_d, s5_w_glu, ffn_w_gu, ffn_w_down, final_norm_g, loss_target, m_norm_mix_g, m_norm_ffn_g, m_gla_w_in, m_gla_w_a2, m_gla_b_a, m_gla_norm_g, m_gla_w_out, m_ssd_w_in, m_ssd_conv_w, m_ssd_conv_b, m_ssd_dt_bias, m_ssd_a_log, m_ssd_d, m_ssd_norm_g, m_ssd_w_out, m_s5_log_dt, m_s5_a_re, m_s5_a_im, m_s5_b_re, m_s5_b_im, m_s5_c_re, m_s5_c_im, m_s5_d, m_s5_w_glu, m_ffn_w_gu, m_ffn_w_down, m_final_norm_g, v_norm_mix_g, v_norm_ffn_g, v_gla_w_in, v_gla_w_a2, v_gla_b_a, v_gla_norm_g, v_gla_w_out, v_ssd_w_in, v_ssd_conv_w, v_ssd_conv_b, v_ssd_dt_bias, v_ssd_a_log, v_ssd_d, v_ssd_norm_g, v_ssd_w_out, v_s5_log_dt, v_s5_a_re, v_s5_a_im, v_s5_b_re, v_s5_b_im, v_s5_c_re, v_s5_c_im, v_s5_d, v_s5_w_glu, v_ffn_w_gu, v_ffn_w_down, v_final_norm_g):
    given = dict(locals())
    local = {n: given[n] for n in _WEIGHTS}
    px, py, pc = _place()

    first, gathering, zero = _gather_first(local)
    full = dict(local)
    full.update(first)
    full["norm_mix_g"] = local["norm_mix_g"] + zero
    big = [first]

    def weights_of(tag):
        def arrived(h):
            big.append(_gather_rest(big[-1], gathering[tag], h, tag))
            return big[-1]
        return arrived

    swapping, reducing = [], []

    def later_grads(g):
        pending, zero = _reduce_swap_start(g, local, pc, lambda n, j: not _in_layer0(n, j), "later")
        swapping.append(pending)
        return zero

    def ffn0_grads(g):
        pending, zero = _reduce_scatter_after(swapping[0], g["ffn_w_down"][0], pc)
        reducing.append(pending)
        g["ffn_w_down"][0] = g["ffn_w_down"][0] + zero
        pending, zero = _reduce_start(g, local, pc, _is_ffn0, "ffn0")
        reducing.append(pending)
        return zero

    loss, grad_x, grads = local_step(x[0], loss_target[0], full, weights_of("later"), later_grads, ffn0_grads, weights_of("ffn0"))
    loss = lax.psum(loss, ("x", "y", "c"))

    small, zero = _reduce_small_start(grads)
    grads["gla_w_out"][0] = grads["gla_w_out"][0] + zero
    red, parts, first_pending = _reduce_big(grads, local, reducing, grad_x, px, py, pc)

    deltas, new_m, new_v = {}, {}, {}

    def update(n):
        deltas[n], new_m[n], new_v[n] = adamw(local[n], red[n], given["m_" + n], given["v_" + n], name=f"adamw_{n}")

    for n in list(red):
        update(n)
    red.update(_reduce_big_first(first_pending, parts, deltas["ffn_w_gu"], local, px, py, pc))
    red.update(_reduce_small(small, red["gla_w_out"], grads, local, px, py, pc))
    for n in _WEIGHTS:
        if n not in deltas:
            update(n)
    return (loss, grad_x[None], *[red[n] for n in _WEIGHTS], *[deltas[n] for n in _WEIGHTS],
            *[new_m[n] for n in _WEIGHTS], *[new_v[n] for n in _WEIGHTS])
```

```python
import functools
import math

import jax
import jax.numpy as jnp
from jax import lax
from jax.experimental import pallas as pl
from jax.experimental.pallas import tpu as pltpu

F32 = jnp.float32
BF16 = jnp.bfloat16

D = 1024
DEPTH = 4
CH = 64
EPS = 1e-6
GLA_H, GLA_DK, GLA_DV, GLA_RANK, GLA_TAU = 4, 128, 256, 16, 16.0
GLA_QK = GLA_H * GLA_DK
GLA_VD = GLA_H * GLA_DV
GLA_IN = 2 * GLA_QK + 2 * GLA_VD + GLA_RANK
GLA_INP = 3200
SSD_DI, SSD_HD, SSD_H, SSD_G, SSD_N, SSD_K = 2048, 64, 32, 8, 128, 4
SSD_GN = SSD_G * SSD_N
SSD_CONV = SSD_DI + 2 * SSD_GN
SSD_IN = SSD_DI + SSD_CONV + SSD_H
SSD_INP = 6272
S5_GS, S5_NG, S5_P = 16, 64, 64
S5_BLK = 8
FFN_H = 2816
LANE = 128
VMEM_LIMIT = 52 * 1024 * 1024
_MATMUL_VMEM = 40 * 1024 * 1024

ADAM_LR, ADAM_B1, ADAM_B2, ADAM_EPS, ADAM_WD, ADAM_STEP = 0.001, 0.9, 0.999, 1e-08, 0.01, 10

_ARB = "arbitrary"


def _cparams(n):
    return pltpu.CompilerParams(dimension_semantics=(_ARB,) * n, vmem_limit_bytes=VMEM_LIMIT)


def _pick(n, target, mult=LANE):
    best = None
    for c in range(mult, min(n, target) + 1, mult):
        if n % c == 0:
            best = c
    return best if best is not None else n


_DN = {"nn": (((1,), (0,)), ((), ())), "nt": (((1,), (1,)), ((), ())), "tn": (((0,), (0,)), ((), ()))}


def _dot_raw(a, b, form):
    return lax.dot_general(a.astype(BF16), b.astype(BF16), _DN[form], preferred_element_type=F32)


@functools.partial(jax.custom_vjp, nondiff_argnums=(2,))
def bdot(a, b, form):
    return _dot_raw(a, b, form)


def _bdot_fwd(a, b, form):
    return _dot_raw(a, b, form), (a, b)


def _bdot_bwd(form, res, g):
    a, b = res
    if form == "nn":
        return _dot_raw(g, b, "nt"), _dot_raw(a, g, "tn")
    if form == "nt":
        return _dot_raw(g, b, "nn"), _dot_raw(g, a, "tn")
    return _dot_raw(b, g, "nt"), _dot_raw(a, g, "nn")


bdot.defvjp(_bdot_fwd, _bdot_bwd)


def _hdot(a, b):
    return jnp.dot(a, b, precision=lax.Precision.HIGHEST, preferred_element_type=F32)


@jax.custom_vjp
def cdot_left(c, ct, x):
    return _hdot(c, x)


def _cdl_fwd(c, ct, x):
    return _hdot(c, x), (c, ct)


def _cdl_bwd(res, g):
    c, ct = res
    return jnp.zeros_like(c), jnp.zeros_like(ct), _hdot(ct, g)


cdot_left.defvjp(_cdl_fwd, _cdl_bwd)


@jax.custom_vjp
def cdot_right(x, c, ct):
    return _hdot(x, c)


def _cdr_fwd(x, c, ct):
    return _hdot(x, c), (c, ct)


def _cdr_bwd(res, g):
    c, ct = res
    return _hdot(g, ct), jnp.zeros_like(c), jnp.zeros_like(ct)


cdot_right.defvjp(_cdr_fwd, _cdr_bwd)


def _sigmoid(x):
    return 1.0 / (1.0 + jnp.exp(-x))


def _silu(x):
    return x * _sigmoid(x)


def _softplus(x):
    return jnp.maximum(x, 0.0) + jnp.log(1.0 + jnp.exp(-jnp.abs(x)))


def _log_sigmoid(x):
    return jnp.minimum(x, 0.0) - jnp.log(1.0 + jnp.exp(-jnp.abs(x)))


def _gelu(x):
    c = math.sqrt(2.0 / math.pi)
    return 0.5 * x * (1.0 + jnp.tanh(c * (x + 0.044715 * (x * x * x))))


def _rms(x, g):
    return x * lax.rsqrt(jnp.mean(x * x, axis=-1, keepdims=True) + EPS) * g


def _iota2(shape, axis):
    return lax.broadcasted_iota(jnp.int32, shape, axis)


def matmul(a, b, form, *, name, G=1, out_dtype=F32, add=None):
    isz = lambda t: jnp.dtype(t.dtype).itemsize
    osz = jnp.dtype(out_dtype).itemsize + (isz(add) if add is not None else 0)

    def fits(bm, bn, bk):
        return 2 * (bm * bk * isz(a) + bk * bn * isz(b) + bm * bn * osz) + 4 * bm * bn <= _MATMUL_VMEM

    if form in ("nn", "nt"):
        M = a.shape[0]
        K = a.shape[1] // G
        N = b.shape[2] if form == "nn" else b.shape[1]
        bm, bn, bk = min(M, 1024), _pick(N, 1536), _pick(K, 2048)
        while not fits(bm, bn, bk) and bk % 256 == 0:
            bk //= 2
        nj, nk = N // bn, K // bk
        grid = (G, M // bm, nj, nk)
        a_spec = pl.BlockSpec((bm, bk), lambda g, i, j, k: (i, g * nk + k))
        if form == "nn":
            b_spec = pl.BlockSpec((None, bk, bn), lambda g, i, j, k: (g, k, j))
        else:
            b_spec = pl.BlockSpec((None, bn, bk), lambda g, i, j, k: (g, j, k))
        o_spec = pl.BlockSpec((bm, bn), lambda g, i, j, k: (i, g * nj + j))
        out_shape = jax.ShapeDtypeStruct((M, G * N), out_dtype)
    else:
        T = a.shape[0]
        Ka, Nb = a.shape[1] // G, b.shape[1] // G
        bm, bn, bk = _pick(Ka, 1408), _pick(Nb, 1536), min(T, 2048)
        while not fits(bm, bn, bk) and bk % 512 == 0:
            bk //= 2
        ni, nj, nk = Ka // bm, Nb // bn, T // bk
        grid = (G, ni, nj, nk)
        a_spec = pl.BlockSpec((bk, bm), lambda g, i, j, k: (k, g * ni + i))
        b_spec = pl.BlockSpec((bk, bn), lambda g, i, j, k: (k, g * nj + j))
        o_spec = pl.BlockSpec((None, bm, bn), lambda g, i, j, k: (g, i, j))
        out_shape = jax.ShapeDtypeStruct((G, Ka, Nb), out_dtype)
    has_add = add is not None

    def finish(refs, r):
        if has_add:
            r = r + refs[2][...].astype(F32)
        o_ref = refs[3] if has_add else refs[2]
        o_ref[...] = r.astype(o_ref.dtype)

    def body_one(*refs):
        finish(refs, _dot_raw(refs[0][...], refs[1][...], form))

    def body_acc(*refs):
        acc_ref = refs[-1]
        k = pl.program_id(3)

        @pl.when(k == 0)
        def _():
            acc_ref[...] = jnp.zeros_like(acc_ref)

        acc_ref[...] += _dot_raw(refs[0][...], refs[1][...], form)

        @pl.when(k == nk - 1)
        def _():
            finish(refs, acc_ref[...])

    in_specs = [a_spec, b_spec]
    args = [a, b]
    if has_add:
        in_specs.append(o_spec)
        args.append(add)
    return pl.pallas_call(
        body_one if nk == 1 else body_acc, name=name, grid=grid, in_specs=in_specs, out_specs=o_spec,
        out_shape=out_shape, scratch_shapes=[] if nk == 1 else [pltpu.VMEM((bm, bn), F32)],
        compiler_params=_cparams(4),
    )(*args)


def matmul_nt_norm_bwd(a, w, h, g, d, *, name):
    T, K = a.shape
    bm = min(T, 512)
    bk = _pick(K, 2048)
    nk = K // bk

    def body(a_ref, w_ref, h_ref, g_ref, d_ref, dh_ref, dg_ref, acc_ref):
        i, k = pl.program_id(0), pl.program_id(1)

        @pl.when((i == 0) & (k == 0))
        def _():
            dg_ref[...] = jnp.zeros_like(dg_ref)

        @pl.when(k == 0)
        def _():
            acc_ref[...] = jnp.zeros_like(acc_ref)

        acc_ref[...] += _dot_raw(a_ref[...], w_ref[...], "nt")

        @pl.when(k == nk - 1)
        def _():
            _, vjp = jax.vjp(lambda g_, h_: _rms(h_, g_), g_ref[...], h_ref[...])
            dg, dh = vjp(acc_ref[...])
            dh_ref[...] = dh + d_ref[...]
            dg_ref[...] += dg

    row = pl.BlockSpec((bm, D), lambda i, k: (i, 0))
    one = pl.BlockSpec((1, D), lambda i, k: (0, 0))
    return pl.pallas_call(
        body, name=name, grid=(T // bm, nk),
        in_specs=[pl.BlockSpec((bm, bk), lambda i, k: (i, k)), pl.BlockSpec((D, bk), lambda i, k: (0, k)), row, one, row],
        out_specs=[row, one], out_shape=[jax.ShapeDtypeStruct((T, D), F32), jax.ShapeDtypeStruct((1, D), F32)],
        scratch_shapes=[pltpu.VMEM((bm, D), F32)], compiler_params=_cparams(2),
    )(a, w, h, g, d)


def ffn_up(hn, w_il, *, name):
    T = hn.shape[0]
    bm, hb = min(T, 512), FFN_H // 2

    def body(a_ref, b_ref, act_ref, gu_ref):
        r = _dot_raw(a_ref[...], b_ref[...], "nn")
        act_ref[...] = (_silu(r[:, :hb]) * r[:, hb:]).astype(act_ref.dtype)
        gu_ref[...] = r.astype(gu_ref.dtype)

    return pl.pallas_call(
        body, name=name, grid=(2, T // bm),
        in_specs=[pl.BlockSpec((bm, D), lambda j, i: (i, 0)), pl.BlockSpec((D, 2 * hb), lambda j, i: (0, j))],
        out_specs=[pl.BlockSpec((bm, hb), lambda j, i: (i, j)), pl.BlockSpec((bm, 2 * hb), lambda j, i: (i, j))],
        out_shape=[jax.ShapeDtypeStruct((T, FFN_H), BF16), jax.ShapeDtypeStruct((T, 2 * FFN_H), BF16)],
        compiler_params=_cparams(2),
    )(hn, w_il)


_DACT_CHUNK = 512


def ffn_dact(d, w_down, gu, *, name):
    T = d.shape[0]
    bm, hb = min(T, 512), FFN_H // 2

    def body(d_ref, w_ref, gu_ref, o_ref):
        d_blk = d_ref[...].astype(BF16)
        for lo in range(0, hb, _DACT_CHUNK):
            hi = min(lo + _DACT_CHUNK, hb)
            da = _dot_raw(d_blk, w_ref[lo:hi, :], "nt")
            g, u = gu_ref[:, lo:hi].astype(F32), gu_ref[:, hb + lo:hb + hi].astype(F32)
            sg = _sigmoid(g)
            o_ref[:, lo:hi] = (da * u * (sg * (1.0 + g * (1.0 - sg)))).astype(o_ref.dtype)
            o_ref[:, hb + lo:hb + hi] = (da * (g * sg)).astype(o_ref.dtype)

    return pl.pallas_call(
        body, name=name, grid=(2, T // bm),
        in_specs=[pl.BlockSpec((bm, D), lambda j, i: (i, 0)), pl.BlockSpec((hb, D), lambda j, i: (j, 0)),
                  pl.BlockSpec((bm, 2 * hb), lambda j, i: (i, j))],
        out_specs=pl.BlockSpec((bm, 2 * hb), lambda j, i: (i, j)),
        out_shape=jax.ShapeDtypeStruct((T, 2 * FFN_H), BF16), compiler_params=_cparams(2),
    )(d, w_down, gu)


def _row_entry(e):
    return e if isinstance(e, tuple) else (e, e.shape[1], 0)


def _row_spec(bt, e):
    _, width, idx = e
    return pl.BlockSpec((bt, width), lambda i: (i, idx))


def _full_spec(p):
    return pl.BlockSpec(p.shape, lambda i: (0,) * p.ndim)


def rowwise(f, params, rows, outs, *, bt, name):
    rows = [_row_entry(e) for e in rows]
    T = rows[0][0].shape[0]
    bt = min(bt, T)
    np_, nr = len(params), len(rows)

    def body(*refs):
        p = tuple(r[...].astype(F32) for r in refs[:np_])
        rw = tuple(r[...].astype(F32) for r in refs[np_:np_ + nr])
        res = f(p, rw)
        for o_ref, o in zip(refs[np_ + nr:], res):
            o_ref[...] = o.astype(o_ref.dtype)

    res = pl.pallas_call(
        body, name=name, grid=(T // bt,),
        in_specs=[_full_spec(p) for p in params] + [_row_spec(bt, e) for e in rows],
        out_specs=[pl.BlockSpec((bt, w), lambda i: (i, 0)) for w, _ in outs],
        out_shape=[jax.ShapeDtypeStruct((T, w), dt) for w, dt in outs],
        compiler_params=_cparams(1),
    )(*params, *[e[0] for e in rows])
    return list(res)


def rowwise_vjp(f, params, rows, cts, drow_dtypes, *, bt, name):
    rows = [_row_entry(e) for e in rows]
    cts = [_row_entry(e) for e in cts]
    T = rows[0][0].shape[0]
    bt = min(bt, T)
    np_, nr, nc = len(params), len(rows), len(cts)
    want = [i for i, dt in enumerate(drow_dtypes) if dt is not None]

    def body(*refs):
        p = tuple(r[...].astype(F32) for r in refs[:np_])
        rw = tuple(r[...].astype(F32) for r in refs[np_:np_ + nr])
        ct = tuple(r[...].astype(F32) for r in refs[np_ + nr:np_ + nr + nc])
        outs = refs[np_ + nr + nc:]
        _, vjp = jax.vjp(f, p, rw)
        dp, dr = vjp(ct)

        @pl.when(pl.program_id(0) == 0)
        def _():
            for o in outs[:np_]:
                o[...] = jnp.zeros_like(o)

        for o, d in zip(outs[:np_], dp):
            o[...] += d
        for o, i in zip(outs[np_:], want):
            o[...] = dr[i].astype(o.dtype)

    res = pl.pallas_call(
        body, name=name, grid=(T // bt,),
        in_specs=[_full_spec(p) for p in params] + [_row_spec(bt, e) for e in rows] + [_row_spec(bt, e) for e in cts],
        out_specs=[_full_spec(p) for p in params] + [pl.BlockSpec((bt, rows[i][1]), lambda i_: (i_, 0)) for i in want],
        out_shape=[jax.ShapeDtypeStruct(p.shape, F32) for p in params]
        + [jax.ShapeDtypeStruct((T, rows[i][1]), drow_dtypes[i]) for i in want],
        compiler_params=_cparams(1),
    )(*params, *[e[0] for e in rows], *[e[0] for e in cts])
    res = list(res)
    return res[:np_], res[np_:]


def f_rmsnorm(p, r):
    return (_rms(r[0], p[0]),)


def f_gla_gate_in(p, r):
    w_a2, b_a = p
    z = bdot(r[0], w_a2, "nn") + b_a
    return (_log_sigmoid(z) / GLA_TAU,)


def f_gla_gate_in_fwd(p, r):
    w_a2, b_a = p
    z = _dot_raw(r[0], w_a2, "nn") + b_a
    return (_log_sigmoid(z) / GLA_TAU,)


def f_s5_act(p, r):
    (dsk,) = p
    ycp, u = r
    return (_gelu(ycp + dsk * u),)


def f_glu_res(p, r):
    vg, h = r
    return (vg[:, :D] * _sigmoid(vg[:, D:]) + h,)


def f_glu(p, r):
    vg = r[0]
    return (vg[:, :D] * _sigmoid(vg[:, D:]),)


def loss_head(h, g, target, *, name):
    T = h.shape[0]
    bt = min(T, 256)

    def lossf(g_, h_, t_):
        e = _rms(h_, g_) - t_
        return (0.5 / D) * jnp.sum(e * e)

    def body(g_ref, h_ref, t_ref, loss_ref, dg_ref, dh_ref):
        @pl.when(pl.program_id(0) == 0)
        def _():
            loss_ref[...] = jnp.zeros_like(loss_ref)
            dg_ref[...] = jnp.zeros_like(dg_ref)

        val, vjp = jax.vjp(lossf, g_ref[...], h_ref[...], t_ref[...])
        dg, dh, _ = vjp(jnp.ones((), F32))
        loss_ref[...] += jnp.full(loss_ref.shape, val, F32)
        dg_ref[...] += dg
        dh_ref[...] = dh

    row = pl.BlockSpec((bt, D), lambda i: (i, 0))
    one = pl.BlockSpec((1, D), lambda i: (0, 0))
    loss, dg, dh = pl.pallas_call(
        body, name=name, grid=(T // bt,), in_specs=[one, row, row],
        out_specs=[pl.BlockSpec((1, LANE), lambda i: (0, 0)), one, row],
        out_shape=[jax.ShapeDtypeStruct((1, LANE), F32), jax.ShapeDtypeStruct((1, D), F32),
                   jax.ShapeDtypeStruct((T, D), F32)],
        compiler_params=_cparams(1),
    )(g, h, target)
    return loss[0, 0], dg, dh


def _gla_consts():
    r, c = _iota2((CH, CH), 0), _iota2((CH, CH), 1)
    return (r >= c).astype(F32), (r <= c).astype(F32), r >= c


def _gla_chunk(q, k, v, la, st, consts, dot, cdl):
    L, Lt, tril = consts
    lc = cdl(L, Lt, la)
    lend = lc[CH - 1:CH, :]
    e, ei = jnp.exp(lc), jnp.exp(-lc)
    qs = q * (GLA_DK ** -0.5)
    qf, kf, qb, kb = qs * e, k * ei, qs * ei, k * e
    sc = jnp.where(tril, dot(qf, kf, "nt"), dot(qb, kb, "nt"))
    o = dot(sc, v, "nn") + dot(qf, st, "nt")
    kd = k * jnp.exp(lend - lc)
    st_new = st * jnp.exp(lend) + dot(v, kd, "tn")
    return o, st_new


def _gla_block(q, k, v, la, st, nc, dot, cdl):
    consts = _gla_consts()
    outs = []
    for c in range(nc):
        sl = slice(c * CH, (c + 1) * CH)
        o, st = _gla_chunk(q[sl], k[sl], v[sl], la[sl], st, consts, dot, cdl)
        outs.append(o)
    return jnp.concatenate(outs, axis=0), st


_GLA_HP = 2


def _gla_specs(rows, rev, nb):
    t = (lambda j: nb - 1 - j) if rev else (lambda j: j)
    hp, ng = _GLA_HP, GLA_H // _GLA_HP
    q = pl.BlockSpec((rows, hp * GLA_DK), lambda h, j: (t(j), h))
    k = pl.BlockSpec((rows, hp * GLA_DK), lambda h, j: (t(j), ng + h))
    v = pl.BlockSpec((rows, hp * GLA_DV), lambda h, j: (t(j), ng + h))
    la = pl.BlockSpec((rows, hp * GLA_DK), lambda h, j: (t(j), h))
    ss = pl.BlockSpec((None, hp, GLA_DV, GLA_DK), lambda h, j: (t(j), h, 0, 0))
    o = pl.BlockSpec((rows, hp * GLA_DV), lambda h, j: (t(j), h))
    r = pl.BlockSpec((rows, hp * GLA_DV), lambda h, j: (t(j), 2 * ng + h))
    g = pl.BlockSpec((1, hp * GLA_DV), lambda h, j: (0, h))
    return q, k, v, la, ss, o, r, g


def _gla_heads(q, k, v, la, r, ng, sts, nc, dot, cdl):
    outs, new = [], []
    for i in range(_GLA_HP):
        kk, vv = slice(i * GLA_DK, (i + 1) * GLA_DK), slice(i * GLA_DV, (i + 1) * GLA_DV)
        o, st = _gla_block(q[:, kk], k[:, kk], v[:, vv], la[:, kk], sts[i], nc, dot, cdl)
        outs.append(_rms(o, ng[:, vv]) * _silu(r[:, vv]))
        new.append(st)
    return jnp.concatenate(outs, axis=1), tuple(new)


def gla_scan_fwd(proj, la, ng, *, nc, name):
    T = proj.shape[0]
    rows = min(T, nc * CH)
    nc = rows // CH
    per = 2 if T % (2 * rows) == 0 else 1
    nb = T // (per * rows)
    q_s, k_s, v_s, la_s, _, o_s, r_s, g_s = _gla_specs(per * rows, False, nb)
    ss_s = pl.BlockSpec((per, _GLA_HP, GLA_DV, GLA_DK), lambda h, j: (j, h, 0, 0))

    def body(q_ref, k_ref, v_ref, la_ref, r_ref, g_ref, o_ref, ss_ref, st_ref):
        @pl.when(pl.program_id(1) == 0)
        def _():
            st_ref[...] = jnp.zeros_like(st_ref)

        sts = tuple(st_ref[i] for i in range(_GLA_HP))
        for p in range(per):
            sl = slice(p * rows, (p + 1) * rows)
            for i in range(_GLA_HP):
                ss_ref[p, i] = sts[i]
            o, sts = _gla_heads(q_ref[sl, :], k_ref[sl, :], v_ref[sl, :], la_ref[sl, :], r_ref[sl, :], g_ref[...], sts, nc,
                                _dot_raw, lambda c, ct, x: _hdot(c, x))
            o_ref[sl, :] = o.astype(o_ref.dtype)
        for i in range(_GLA_HP):
            st_ref[i] = sts[i]

    return pl.pallas_call(
        body, name=name, grid=(GLA_H // _GLA_HP, nb), in_specs=[q_s, k_s, v_s, la_s, r_s, g_s], out_specs=[o_s, ss_s],
        out_shape=[jax.ShapeDtypeStruct((T, GLA_VD), BF16), jax.ShapeDtypeStruct((per * nb, GLA_H, GLA_DV, GLA_DK), F32)],
        scratch_shapes=[pltpu.VMEM((_GLA_HP, GLA_DV, GLA_DK), F32)], compiler_params=_cparams(2),
    )(proj, proj, proj, la, proj, ng)


def gla_scan_bwd(proj, la, ng, ss, do, *, nc, name):
    T = proj.shape[0]
    rows = min(T, nc * CH)
    nc = rows // CH
    nb = T // rows
    q_s, k_s, v_s, la_s, ss_s, o_s, r_s, g_s = _gla_specs(rows, True, nb)
    t = lambda j: nb - 1 - j
    dqk_s = pl.BlockSpec((rows, _GLA_HP * GLA_DK), lambda h, j: (t(j), h))

    def body(q_ref, k_ref, v_ref, la_ref, r_ref, g_ref, ss_ref, do_ref,
             dq_ref, dk_ref, dv_ref, dla_ref, dr_ref, dg_ref, dst_ref):
        @pl.when(pl.program_id(1) == 0)
        def _():
            dst_ref[...] = jnp.zeros_like(dst_ref)
            dg_ref[...] = jnp.zeros_like(dg_ref)

        fn = lambda q, k, v, la_, r, g, *sts: _gla_heads(q, k, v, la_, r, g, sts, nc, bdot, cdot_left)
        _, vjp = jax.vjp(fn, q_ref[...], k_ref[...], v_ref[...], la_ref[...], r_ref[...], g_ref[...],
                         *[ss_ref[i] for i in range(_GLA_HP)])
        dq, dk, dv, dla, dr, dg, *dsts = vjp((do_ref[...], tuple(dst_ref[i] for i in range(_GLA_HP))))
        dq_ref[...] = dq.astype(dq_ref.dtype)
        dk_ref[...] = dk.astype(dk_ref.dtype)
        dv_ref[...] = dv.astype(dv_ref.dtype)
        dla_ref[...] = dla
        dr_ref[...] = dr.astype(dr_ref.dtype)
        dg_ref[...] += dg
        for i in range(_GLA_HP):
            dst_ref[i] = dsts[i]

    return pl.pallas_call(
        body, name=name, grid=(GLA_H // _GLA_HP, nb), in_specs=[q_s, k_s, v_s, la_s, r_s, g_s, ss_s, o_s],
        out_specs=[dqk_s, dqk_s, o_s, dqk_s, o_s, g_s],
        out_shape=[jax.ShapeDtypeStruct((T, GLA_QK), BF16), jax.ShapeDtypeStruct((T, GLA_QK), BF16),
                   jax.ShapeDtypeStruct((T, GLA_VD), BF16), jax.ShapeDtypeStruct((T, GLA_QK), F32),
                   jax.ShapeDtypeStruct((T, GLA_VD), BF16), jax.ShapeDtypeStruct((1, GLA_VD), F32)],
        scratch_shapes=[pltpu.VMEM((_GLA_HP, GLA_DV, GLA_DK), F32)], compiler_params=_cparams(2),
    )(proj, proj, proj, la, proj, ng, ss, do)


_CONV_W = 512
_CONV_OFF = SSD_DI // _CONV_W


def _conv_pre(x, prev8, w_ref, b_ref):
    bt = x.shape[0]
    ext = jnp.concatenate([prev8, x], axis=0)
    shifted = []
    for j in range(SSD_K):
        s = SSD_K - 1 - j
        shifted.append(x if s == 0 else pltpu.roll(ext, s, 0)[8:8 + bt])
    pre = b_ref[...] + sum(w_ref[j:j + 1, :] * shifted[j] for j in range(SSD_K))
    return pre, shifted


def ssd_conv_fwd(proj, w, b, *, name):
    T = proj.shape[0]
    bt = min(T, 512)
    nb = T // bt

    def body(x_ref, w_ref, b_ref, o_ref, carry_ref):
        @pl.when(pl.program_id(1) == 0)
        def _():
            carry_ref[...] = jnp.zeros_like(carry_ref)

        x = x_ref[...]
        pre, _ = _conv_pre(x, carry_ref[...], w_ref, b_ref)
        o_ref[...] = _silu(pre)
        carry_ref[...] = x[bt - 8:, :]

    return pl.pallas_call(
        body, name=name, grid=(SSD_CONV // _CONV_W, nb),
        in_specs=[pl.BlockSpec((bt, _CONV_W), lambda c, t: (t, _CONV_OFF + c)),
                  pl.BlockSpec((SSD_K, _CONV_W), lambda c, t: (0, c)),
                  pl.BlockSpec((1, _CONV_W), lambda c, t: (0, c))],
        out_specs=pl.BlockSpec((bt, _CONV_W), lambda c, t: (t, c)),
        out_shape=jax.ShapeDtypeStruct((T, SSD_CONV), F32),
        scratch_shapes=[pltpu.VMEM((8, _CONV_W), F32)], compiler_params=_cparams(2),
    )(proj, w, b)


def ssd_conv_bwd(proj, w, b, dout, col0, *, name):
    T, ncols = dout.shape
    bt = min(T, 512)
    nb = T // bt
    r8 = bt // 8
    c0 = col0 // _CONV_W

    def body(x_ref, xp_ref, w_ref, b_ref, do_ref, dx_ref, dw_ref, db_ref, carry_ref):
        t = pl.program_id(1)

        @pl.when(t == 0)
        def _():
            carry_ref[...] = jnp.zeros_like(carry_ref)
            dw_ref[...] = jnp.zeros_like(dw_ref)
            db_ref[...] = jnp.zeros_like(db_ref)

        x = x_ref[...]
        prev8 = jnp.where(t == nb - 1, 0.0, xp_ref[...])
        pre, shifted = _conv_pre(x, prev8, w_ref, b_ref)
        sg = _sigmoid(pre)
        dpre = do_ref[...] * (sg * (1.0 + pre * (1.0 - sg)))
        ext = jnp.concatenate([dpre, carry_ref[...]], axis=0)
        dx = w_ref[SSD_K - 1:SSD_K, :] * dpre
        for j in range(SSD_K - 1):
            s = SSD_K - 1 - j
            dx = dx + w_ref[j:j + 1, :] * pltpu.roll(ext, bt + 8 - s, 0)[:bt]
        dx_ref[...] = dx.astype(dx_ref.dtype)
        dw_ref[...] += jnp.concatenate([jnp.sum(dpre * shifted[j], axis=0, keepdims=True) for j in range(SSD_K)], axis=0)
        db_ref[...] += jnp.sum(dpre, axis=0, keepdims=True)
        carry_ref[...] = dpre[:8, :]

    rt = lambda t: nb - 1 - t
    return pl.pallas_call(
        body, name=name, grid=(ncols // _CONV_W, nb),
        in_specs=[pl.BlockSpec((bt, _CONV_W), lambda c, t: (rt(t), _CONV_OFF + c0 + c)),
                  pl.BlockSpec((8, _CONV_W), lambda c, t: (jnp.maximum(rt(t) * r8 - 1, 0), _CONV_OFF + c0 + c)),
                  pl.BlockSpec((SSD_K, _CONV_W), lambda c, t: (0, c0 + c)),
                  pl.BlockSpec((1, _CONV_W), lambda c, t: (0, c0 + c)),
                  pl.BlockSpec((bt, _CONV_W), lambda c, t: (rt(t), c))],
        out_specs=[pl.BlockSpec((bt, _CONV_W), lambda c, t: (rt(t), c)),
                   pl.BlockSpec((SSD_K, _CONV_W), lambda c, t: (0, c)),
                   pl.BlockSpec((1, _CONV_W), lambda c, t: (0, c))],
        out_shape=[jax.ShapeDtypeStruct((T, ncols), BF16), jax.ShapeDtypeStruct((SSD_K, ncols), F32),
                   jax.ShapeDtypeStruct((1, ncols), F32)],
        scratch_shapes=[pltpu.VMEM((8, _CONV_W), F32)], compiler_params=_cparams(2),
    )(proj, proj, w, b, dout)


_SSD_U = 2 * CH


def _ssd_unit(xs, bm, cm, dtraw, dtb, alog, dsk, hp, g, dot, cdl, cdr):
    U, P2 = _SSD_U, 2 * SSD_HD
    r, c = _iota2((U, U), 0), _iota2((U, U), 1)
    same = (r // CH) == (c // CH)
    Lb = (same & (r >= c)).astype(F32)
    Ub = (same & (r <= c)).astype(F32)
    lane = _iota2((1, U), 1)
    lo_lane = _iota2((1, P2), 1) < SSD_HD
    lo_sub = _iota2((P2, 1), 0) < SSD_HD
    diag2 = (_iota2((CH, P2), 0) == (_iota2((CH, P2), 1) % CH)).astype(F32)

    dt = _softplus(dtraw + dtb)
    da = dt * (-jnp.exp(alog))
    cum = cdl(Lb, Ub, da)
    ys = []
    new_hp = []
    for pr in range(2):
        xs_p = xs[:, pr * P2:(pr + 1) * P2]
        cols, dts, dks = [], [], []
        for jj in range(2):
            oh_l = (lane == g * (SSD_H // SSD_G) + 2 * pr + jj).astype(F32)
            cols.append(jnp.sum(cum * oh_l, axis=1, keepdims=True))
            dts.append(jnp.sum(dt * oh_l, axis=1, keepdims=True))
            dks.append(jnp.sum(dsk * oh_l, axis=1, keepdims=True))
        dsk_p = jnp.where(lo_lane, dks[0], dks[1])
        h = hp[pr]
        yc = []
        for ci in range(2):
            sl = slice(ci * CH, (ci + 1) * CH)
            xs_c, bm_c, cm_c = xs_p[sl], bm[sl], cm[sl]
            col = jnp.where(lo_lane, cols[0][sl], cols[1][sl])
            dtc = jnp.where(lo_lane, dts[0][sl], dts[1][sl])
            row = jnp.sum(diag2 * col, axis=0, keepdims=True)
            dtrow = jnp.sum(diag2 * dtc, axis=0, keepdims=True)
            cb = dot(cm_c, jnp.concatenate([bm_c, bm_c], axis=0), "nt")
            mix = cb * jnp.exp(-jnp.abs(col - row)) * dtrow
            xbd = jnp.concatenate([jnp.where(lo_lane, xs_c, 0.0), jnp.where(lo_lane, 0.0, xs_c)], axis=0)
            y_intra = dot(mix, xbd, "nn")
            ce = jnp.where(lo_lane, cols[0][ci * CH + CH - 1:ci * CH + CH, :], cols[1][ci * CH + CH - 1:ci * CH + CH, :])
            y_inter = dot(cm_c, h, "nt") * jnp.exp(col)
            xw = xs_c * (dtc * jnp.exp(ce - col))
            ce_s = [cols[jj][ci * CH + CH - 1:ci * CH + CH, :] for jj in range(2)]
            a_p = jnp.where(lo_sub, jnp.exp(ce_s[0]), jnp.exp(ce_s[1]))
            h = a_p * h + dot(xw, bm_c, "tn")
            yc.append(y_intra + y_inter + dsk_p * xs_c)
        ys.append(jnp.concatenate(yc, axis=0))
        new_hp.append(h)
    return jnp.concatenate(ys, axis=1), tuple(new_hp)


def _ssd_block(xs, bm, cm, dtraw, z, dtb, alog, dsk, ng, hp, g, nu, dot, cdl, cdr):
    outs = []
    for u in range(nu):
        sl = slice(u * _SSD_U, (u + 1) * _SSD_U)
        y, hp = _ssd_unit(xs[sl], bm[sl], cm[sl], dtraw[sl], dtb, alog, dsk, hp, g, dot, cdl, cdr)
        outs.append(y)
    return _rms(jnp.concatenate(outs, axis=0) * _silu(z), ng), hp


def _ssd_specs(rows, rev, nb):
    t = (lambda j: nb - 1 - j) if rev else (lambda j: j)
    gw = SSD_DI // SSD_G
    xs = pl.BlockSpec((rows, gw), lambda j, g: (t(j), g))
    bm = pl.BlockSpec((rows, SSD_N), lambda j, g: (t(j), SSD_DI // SSD_N + g))
    cm = pl.BlockSpec((rows, SSD_N), lambda j, g: (t(j), (SSD_DI + SSD_GN) // SSD_N + g))
    dtr = pl.BlockSpec((rows, LANE), lambda j, g: (t(j), (SSD_DI + SSD_CONV) // LANE))
    par = pl.BlockSpec((1, LANE), lambda j, g: (0, 0))
    hs = pl.BlockSpec((None, None, 2, 2 * SSD_HD, SSD_N), lambda j, g: (t(j), g, 0, 0, 0))
    y = pl.BlockSpec((rows, gw), lambda j, g: (t(j), g))
    ng = pl.BlockSpec((1, gw), lambda j, g: (0, g))
    return xs, bm, cm, dtr, par, hs, y, ng


def ssd_scan_fwd(xbc, proj, dtb, alog, dsk, ng, *, nu, name):
    T = xbc.shape[0]
    rows = min(T, nu * _SSD_U)
    nu = rows // _SSD_U
    nb = T // rows
    xs_s, bm_s, cm_s, dt_s, par_s, hs_s, y_s, ng_s = _ssd_specs(rows, False, nb)

    def body(xs_ref, bm_ref, cm_ref, dt_ref, z_ref, dtb_ref, al_ref, dsk_ref, ng_ref, y_ref, hs_ref, h_ref):
        g = pl.program_id(1)

        @pl.when(pl.program_id(0) == 0)
        def _():
            h_ref[g] = jnp.zeros(h_ref.shape[1:], F32)

        hs_ref[...] = h_ref[g]
        hp = (h_ref[g, 0], h_ref[g, 1])
        y, hp = _ssd_block(xs_ref[...], bm_ref[...], cm_ref[...], dt_ref[...], z_ref[...], dtb_ref[...], al_ref[...],
                           dsk_ref[...], ng_ref[...], hp, g, nu, _dot_raw, lambda c, ct, x: _hdot(c, x), lambda x, c, ct: _hdot(x, c))
        y_ref[...] = y.astype(y_ref.dtype)
        h_ref[g, 0] = hp[0]
        h_ref[g, 1] = hp[1]

    return pl.pallas_call(
        body, name=name, grid=(nb, SSD_G), in_specs=[xs_s, bm_s, cm_s, dt_s, y_s, par_s, par_s, par_s, ng_s],
        out_specs=[y_s, hs_s],
        out_shape=[jax.ShapeDtypeStruct((T, SSD_DI), BF16), jax.ShapeDtypeStruct((nb, SSD_G, 2, 2 * SSD_HD, SSD_N), F32)],
        scratch_shapes=[pltpu.VMEM((SSD_G, 2, 2 * SSD_HD, SSD_N), F32)], compiler_params=_cparams(2),
    )(xbc, xbc, xbc, proj, proj, dtb, alog, dsk, ng)


def ssd_scan_bwd(xbc, proj, dtb, alog, dsk, ng, hs, dy, *, nu, name):
    T = xbc.shape[0]
    rows = min(T, nu * _SSD_U)
    nu = rows // _SSD_U
    nb = T // rows
    xs_s, bm_s, cm_s, dt_s, par_s, hs_s, y_s, ng_s = _ssd_specs(rows, True, nb)
    gw = SSD_DI // SSD_G
    dng_s = pl.BlockSpec((1, SSD_DI), lambda j, g: (0, 0))
    t = lambda j: nb - 1 - j
    n_s = pl.BlockSpec((rows, SSD_N), lambda j, g: (t(j), g))
    ddt_s = pl.BlockSpec((rows, LANE), lambda j, g: (t(j), 0))

    def body(xs_ref, bm_ref, cm_ref, dt_ref, z_ref, dtb_ref, al_ref, dsk_ref, ng_ref, hs_ref, dy_ref,
             dxs_ref, dbm_ref, dcm_ref, ddt_ref, ddtb_ref, dal_ref, ddsk_ref, dz_ref, dng_ref, dh_ref):
        j, g = pl.program_id(0), pl.program_id(1)

        @pl.when(j == 0)
        def _():
            dh_ref[g] = jnp.zeros(dh_ref.shape[1:], F32)

        @pl.when((j == 0) & (g == 0))
        def _():
            ddtb_ref[...] = jnp.zeros_like(ddtb_ref)
            dal_ref[...] = jnp.zeros_like(dal_ref)
            ddsk_ref[...] = jnp.zeros_like(ddsk_ref)
            dng_ref[...] = jnp.zeros_like(dng_ref)

        @pl.when(g == 0)
        def _():
            ddt_ref[...] = jnp.zeros_like(ddt_ref)

        fn = lambda xs, bm, cm, dtr, z, dtb_, al, dsk_, ng_, h0, h1: _ssd_block(
            xs, bm, cm, dtr, z, dtb_, al, dsk_, ng_, (h0, h1), g, nu, bdot, cdot_left, cdot_right)
        _, vjp = jax.vjp(fn, xs_ref[...], bm_ref[...], cm_ref[...], dt_ref[...], z_ref[...], dtb_ref[...], al_ref[...],
                         dsk_ref[...], ng_ref[...], hs_ref[0], hs_ref[1])
        dxs, dbm, dcm, ddt, dz, ddtb, dal, ddsk, dng, dh0, dh1 = vjp((dy_ref[...], (dh_ref[g, 0], dh_ref[g, 1])))
        dz_ref[...] = dz.astype(dz_ref.dtype)
        lanes = pl.ds(pl.multiple_of(g * gw, gw), gw)
        dng_ref[:, lanes] = dng_ref[:, lanes] + dng
        dxs_ref[...] = dxs
        dbm_ref[...] = dbm
        dcm_ref[...] = dcm
        ddt_ref[...] += ddt
        ddtb_ref[...] += ddtb
        dal_ref[...] += dal
        ddsk_ref[...] += ddsk
        dh_ref[g, 0] = dh0
        dh_ref[g, 1] = dh1

    return pl.pallas_call(
        body, name=name, grid=(nb, SSD_G), in_specs=[xs_s, bm_s, cm_s, dt_s, y_s, par_s, par_s, par_s, ng_s, hs_s, y_s],
        out_specs=[y_s, n_s, n_s, ddt_s, par_s, par_s, par_s, y_s, dng_s],
        out_shape=[jax.ShapeDtypeStruct((T, SSD_DI), F32), jax.ShapeDtypeStruct((T, SSD_GN), F32),
                   jax.ShapeDtypeStruct((T, SSD_GN), F32), jax.ShapeDtypeStruct((T, LANE), F32),
                   jax.ShapeDtypeStruct((1, LANE), F32), jax.ShapeDtypeStruct((1, LANE), F32),
                   jax.ShapeDtypeStruct((1, LANE), F32), jax.ShapeDtypeStruct((T, SSD_DI), BF16),
                   jax.ShapeDtypeStruct((1, SSD_DI), F32)],
        scratch_shapes=[pltpu.VMEM((SSD_G, 2, 2 * SSD_HD, SSD_N), F32)], compiler_params=_cparams(2),
    )(xbc, xbc, xbc, proj, proj, dtb, alog, dsk, ng, hs, dy)


def _s5_param_f(log_dt, a_re, a_im, bre_t, bim_t, cim, cdl):
    n = S5_NG * S5_GS
    r, c = _iota2((n, S5_NG), 0), _iota2((n, S5_NG), 1)
    E = ((r // S5_GS) == c).astype(F32)
    rt, ct = _iota2((S5_NG, n), 0), _iota2((S5_NG, n), 1)
    Et = ((ct // S5_GS) == rt).astype(F32)
    step = jnp.exp(log_dt)
    mag = jnp.exp(step * a_re)
    abr = mag * jnp.cos(step * a_im)
    abi = mag * jnp.sin(step * a_im)
    den = a_re * a_re + a_im * a_im
    nr, ni = abr - 1.0, abi
    fr = (nr * a_re + ni * a_im) / den
    fi = (ni * a_re - nr * a_im) / den
    Fr, Fi = cdl(E, Et, fr), cdl(E, Et, fi)
    bbr = Fr * bre_t - Fi * bim_t
    bbi = Fr * bim_t + Fi * bre_t
    return abr, abi, bbr, bbi, -cim


def _whole(a):
    return pl.BlockSpec(a.shape, lambda: (0,) * a.ndim)


def s5_param_fwd(args, *, name):
    def body(*refs):
        res = _s5_param_f(*[r[...] for r in refs[:6]], lambda c, ct, x: _hdot(c, x))
        for o, v in zip(refs[6:], res):
            o[...] = v

    shapes = [(S5_NG, S5_P), (S5_NG, S5_P)] + [(S5_NG * S5_GS, S5_P)] * 3
    return pl.pallas_call(
        body, name=name, in_specs=[_whole(a) for a in args], out_specs=[pl.BlockSpec(s, lambda: (0, 0)) for s in shapes],
        out_shape=[jax.ShapeDtypeStruct(s, F32) for s in shapes],
        compiler_params=pltpu.CompilerParams(vmem_limit_bytes=VMEM_LIMIT),
    )(*args)


def s5_param_bwd(args, cts, *, name):
    def body(*refs):
        fn = lambda *a: _s5_param_f(*a, cdot_left)
        _, vjp = jax.vjp(fn, *[r[...] for r in refs[:6]])
        grads = vjp(tuple(r[...] for r in refs[6:11]))
        for o, v in zip(refs[11:], grads):
            o[...] = v

    return pl.pallas_call(
        body, name=name, in_specs=[_whole(a) for a in list(args) + list(cts)],
        out_specs=[_whole(a) for a in args], out_shape=[jax.ShapeDtypeStruct(a.shape, F32) for a in args],
        compiler_params=pltpu.CompilerParams(vmem_limit_bytes=VMEM_LIMIT),
    )(*args, *cts)


_S5_W = S5_BLK * S5_P
_S5_BT = 512


def _cmul_add(xr, xi, pr, pi, sr, si):
    return xr + (pr * sr - pi * si), xi + (pr * si + pi * sr)


def _s5_powers(ar, ai):
    pw = [(ar, ai)]
    for _ in range(7):
        qr, qi = pw[-1]
        pw.append((qr * ar - qi * ai, qr * ai + qi * ar))
    return pw


def s5_scan_fwd(u, wb, a_re, a_im, wc, *, name):
    T = u.shape[0]
    bt = min(T, _S5_BT)
    nb = T // bt

    def body(u_ref, wb_ref, ar_ref, ai_ref, wc_ref, x_ref, y_ref, bu_ref, carry_ref):
        @pl.when(pl.program_id(1) == 0)
        def _():
            carry_ref[...] = jnp.zeros_like(carry_ref)

        bu_ref[...] = _dot_raw(u_ref[...], wb_ref[...], "nn")
        ar, ai = ar_ref[...], ai_ref[...]
        pw = _s5_powers(ar, ai)
        pwr = jnp.concatenate([p[0] for p in pw], axis=0)
        pwi = jnp.concatenate([p[1] for p in pw], axis=0)
        rin = _iota2((8, _S5_W), 0)
        cr, ci = carry_ref[0:1, :], carry_ref[1:2, :]
        for t in range(bt // 8):
            sl = slice(8 * t, 8 * t + 8)
            xr, xi = bu_ref[sl, :_S5_W], bu_ref[sl, _S5_W:]
            for s in (1, 2, 4):
                m = rin >= s
                sr = jnp.where(m, pltpu.roll(xr, s, 0), 0.0)
                si = jnp.where(m, pltpu.roll(xi, s, 0), 0.0)
                xr, xi = _cmul_add(xr, xi, *pw[s - 1], sr, si)
            xr, xi = _cmul_add(xr, xi, pwr, pwi, cr, ci)
            x_ref[sl, :_S5_W] = xr
            x_ref[sl, _S5_W:] = xi
            cr, ci = xr[7:8, :], xi[7:8, :]
        carry_ref[0:1, :] = cr
        carry_ref[1:2, :] = ci
        y_ref[...] = _dot_raw(x_ref[...], wc_ref[...], "nn")

    nblk = S5_NG // S5_BLK
    blk = pl.BlockSpec((bt, 2 * _S5_W), lambda g, t: (t, g))
    col = pl.BlockSpec((bt, LANE), lambda g, t: (t, g))
    a_s = pl.BlockSpec((None, 1, _S5_W), lambda g, t: (g, 0, 0))
    wb_s = pl.BlockSpec((None, LANE, 2 * _S5_W), lambda g, t: (g, 0, 0))
    wc_s = pl.BlockSpec((None, 2 * _S5_W, LANE), lambda g, t: (g, 0, 0))
    return pl.pallas_call(
        body, name=name, grid=(nblk, nb), in_specs=[col, wb_s, a_s, a_s, wc_s], out_specs=[blk, col],
        out_shape=[jax.ShapeDtypeStruct((T, nblk * 2 * _S5_W), F32), jax.ShapeDtypeStruct((T, nblk * LANE), F32)],
        scratch_shapes=[pltpu.VMEM((bt, 2 * _S5_W), F32), pltpu.VMEM((8, _S5_W), F32)],
        compiler_params=_cparams(2),
    )(u, wb, a_re, a_im, wc)


def s5_scan_bwd(dy, x, u, wb, a_re, a_im, wc, *, name):
    T = dy.shape[0]
    bt = min(T, _S5_BT)
    nb = T // bt

    def body(dy_ref, x_ref, u_ref, wb_ref, ar_ref, ai_ref, wc_ref, du_ref, dwb_ref, dwc_ref, dar_ref, dai_ref,
             g_ref, lam_ref, carry_ref):
        @pl.when(pl.program_id(1) == 0)
        def _():
            carry_ref[...] = jnp.zeros_like(carry_ref)
            dar_ref[...] = jnp.zeros_like(dar_ref)
            dai_ref[...] = jnp.zeros_like(dai_ref)
            dwb_ref[...] = jnp.zeros_like(dwb_ref)
            dwc_ref[...] = jnp.zeros_like(dwc_ref)

        g_ref[...] = _dot_raw(dy_ref[...], wc_ref[...], "nt")
        pw = _s5_powers(ar_ref[...], -ai_ref[...])
        pwr = jnp.concatenate([p[0] for p in reversed(pw)], axis=0)
        pwi = jnp.concatenate([p[1] for p in reversed(pw)], axis=0)
        rin = _iota2((8, _S5_W), 0)
        cr, ci = carry_ref[0:1, :], carry_ref[1:2, :]
        acc_r = jnp.zeros((8, _S5_W), F32)
        acc_i = jnp.zeros((8, _S5_W), F32)
        for t in reversed(range(bt // 8)):
            sl = slice(8 * t, 8 * t + 8)
            lr, li = g_ref[sl, :_S5_W], g_ref[sl, _S5_W:]
            for s in (1, 2, 4):
                m = rin < 8 - s
                sr = jnp.where(m, pltpu.roll(lr, 8 - s, 0), 0.0)
                si = jnp.where(m, pltpu.roll(li, 8 - s, 0), 0.0)
                lr, li = _cmul_add(lr, li, *pw[s - 1], sr, si)
            lr, li = _cmul_add(lr, li, pwr, pwi, cr, ci)
            lam_ref[sl, :_S5_W] = lr
            lam_ref[sl, _S5_W:] = li
            nr = jnp.where(rin == 7, cr, pltpu.roll(lr, 7, 0))
            ni = jnp.where(rin == 7, ci, pltpu.roll(li, 7, 0))
            xr, xi = x_ref[sl, :_S5_W], x_ref[sl, _S5_W:]
            acc_r = acc_r + (xr * nr + xi * ni)
            acc_i = acc_i + (xr * ni - xi * nr)
            cr, ci = lr[0:1, :], li[0:1, :]
        carry_ref[0:1, :] = cr
        carry_ref[1:2, :] = ci
        dar_ref[...] += jnp.sum(acc_r, axis=0, keepdims=True)
        dai_ref[...] += jnp.sum(acc_i, axis=0, keepdims=True)
        lam = lam_ref[...]
        du_ref[...] = _dot_raw(lam, wb_ref[...], "nt")
        dwb_ref[...] += _dot_raw(u_ref[...], lam, "tn")
        dwc_ref[...] += _dot_raw(x_ref[...], dy_ref[...], "tn")

    nblk = S5_NG // S5_BLK
    blk = pl.BlockSpec((bt, 2 * _S5_W), lambda g, t: (nb - 1 - t, g))
    col = pl.BlockSpec((bt, LANE), lambda g, t: (nb - 1 - t, g))
    a_s = pl.BlockSpec((None, 1, _S5_W), lambda g, t: (g, 0, 0))
    wb_s = pl.BlockSpec((None, LANE, 2 * _S5_W), lambda g, t: (g, 0, 0))
    wc_s = pl.BlockSpec((None, 2 * _S5_W, LANE), lambda g, t: (g, 0, 0))
    return pl.pallas_call(
        body, name=name, grid=(nblk, nb), in_specs=[col, blk, col, wb_s, a_s, a_s, wc_s],
        out_specs=[col, wb_s, wc_s, a_s, a_s],
        out_shape=[jax.ShapeDtypeStruct((T, nblk * LANE), F32), jax.ShapeDtypeStruct((nblk, LANE, 2 * _S5_W), F32),
                   jax.ShapeDtypeStruct((nblk, 2 * _S5_W, LANE), F32), jax.ShapeDtypeStruct((nblk, 1, _S5_W), F32),
                   jax.ShapeDtypeStruct((nblk, 1, _S5_W), F32)],
        scratch_shapes=[pltpu.VMEM((bt, 2 * _S5_W), F32), pltpu.VMEM((bt, 2 * _S5_W), F32), pltpu.VMEM((8, _S5_W), F32)],
        compiler_params=_cparams(2),
    )(dy, x, u, wb, a_re, a_im, wc)


def _norm_bf16(h, g, name):
    return rowwise(f_rmsnorm, [g], [h], [(D, BF16)], bt=512, name=name)[0]


def _norm_bwd(h, g, cts, name):
    n = len(cts) - 1

    def f(p, r):
        y = _rms(r[0], p[0])
        return (y,) * n + (r[0],)

    (dg,), (dh,) = rowwise_vjp(f, [g], [h], cts, [F32], bt=256, name=name)
    return dh, dg


def ffn_fwd(h, g, w_gu, w_down, tag):
    hn = _norm_bf16(h, g, f"{tag}_norm")
    a, gu = ffn_up(hn, w_gu, name=f"{tag}_up")
    h2 = matmul(a, w_down[None], "nn", add=h, name=f"{tag}_down")
    return h2, (h, hn, gu, a)


def ffn_bwd(d, saved, g, w_gu, w_down, tag):
    h, hn, gu, a = saved
    dgu = ffn_dact(d, w_down, gu, name=f"{tag}_dact")
    dwd = matmul(a, d, "tn", name=f"{tag}_dwd")[0]
    dwgu = matmul(hn, dgu, "tn", name=f"{tag}_dwgu")[0]
    dh, dg = matmul_nt_norm_bwd(dgu, w_gu, h, g, d, name=f"{tag}_dhn")
    return dh, dg, dwgu, dwd


_GLA_NC = 4
_SSD_NU = 4


def gla_fwd(h, gm, w_in, w_a2, b_a, ng, w_out, tag):
    hn = _norm_bf16(h, gm, f"{tag}_norm")
    proj = matmul(hn, w_in[None], "nn", name=f"{tag}_in")
    alow = (proj, LANE, 2 * (GLA_QK + GLA_VD) // LANE)
    la = rowwise(f_gla_gate_in_fwd, [w_a2, b_a], [alow], [(GLA_QK, F32)], bt=512, name=f"{tag}_gate")[0]
    og, ss = gla_scan_fwd(proj, la, ng, nc=_GLA_NC, name=f"{tag}_scan")
    h2 = matmul(og, w_out[None], "nn", add=h, name=f"{tag}_proj")
    return h2, (h, hn, proj, la, ss, og)


def gla_bwd(d, saved, gm, w_in, w_a2, b_a, ng, w_out, tag):
    h, hn, proj, la, ss, og = saved
    dog = matmul(d, w_out[None], "nt", name=f"{tag}_dog")
    dwout = matmul(og, d, "tn", name=f"{tag}_dwout")[0]
    dq, dk, dv, dla, dr, dng = gla_scan_bwd(proj, la, ng, ss, dog, nc=_GLA_NC, name=f"{tag}_dscan")
    alow = (proj, LANE, 2 * (GLA_QK + GLA_VD) // LANE)
    (dwa2, dba), (dalow,) = rowwise_vjp(f_gla_gate_in, [w_a2, b_a], [alow], [dla], [BF16], bt=512, name=f"{tag}_dgate")
    dproj = jnp.concatenate([dq, dk, dv, dr, dalow], axis=1)
    dwin = matmul(hn, dproj, "tn", name=f"{tag}_dwin")[0]
    dh, dgm = matmul_nt_norm_bwd(dproj, w_in, h, gm, d, name=f"{tag}_dhn")
    return dh, dgm, dwin, dwa2[:GLA_RANK], dba, dng, dwout


def ssd_fwd(h, gm, w_in, conv_w, conv_b, dtb, alog, dsk, ng, w_out, tag):
    hn = _norm_bf16(h, gm, f"{tag}_norm")
    proj = matmul(hn, w_in[None], "nn", name=f"{tag}_in")
    xbc = ssd_conv_fwd(proj, conv_w, conv_b, name=f"{tag}_conv")
    yg, hs = ssd_scan_fwd(xbc, proj, dtb, alog, dsk, ng, nu=_SSD_NU, name=f"{tag}_scan")
    h2 = matmul(yg, w_out[None], "nn", add=h, name=f"{tag}_proj")
    return h2, (h, hn, proj, xbc, hs, yg)


def ssd_bwd(d, saved, gm, w_in, conv_w, conv_b, dtb, alog, dsk, ng, w_out, tag):
    h, hn, proj, xbc, hs, yg = saved
    dyg = matmul(d, w_out[None], "nt", name=f"{tag}_dyg")
    dwout = matmul(yg, d, "tn", name=f"{tag}_dwout")[0]
    dxs, dbm, dcm, ddt, ddtb, dal, ddsk, dz, dng = ssd_scan_bwd(xbc, proj, dtb, alog, dsk, ng, hs, dyg, nu=_SSD_NU,
                                                               name=f"{tag}_dscan")
    parts = [ssd_conv_bwd(proj, conv_w, conv_b, dout, col0, name=f"{tag}_dconv{k}")
             for k, (dout, col0) in enumerate(((dxs, 0), (dbm, SSD_DI), (dcm, SSD_DI + SSD_GN)))]
    dcw = jnp.concatenate([p[1] for p in parts], axis=1)
    dcb = jnp.concatenate([p[2] for p in parts], axis=1)
    dproj = jnp.concatenate([dz] + [p[0] for p in parts] + [ddt.astype(BF16)], axis=1)
    dwin = matmul(hn, dproj, "tn", name=f"{tag}_dwin")[0]
    dh, dgm = matmul_nt_norm_bwd(dproj, w_in, h, gm, d, name=f"{tag}_dhn")
    return (dh, dgm, dwin, dcw, dcb, ddtb[:, :SSD_H], dal[:, :SSD_H], ddsk[:, :SSD_H], dng, dwout)


_S5_NB = S5_NG // S5_BLK


def _s5_param_args(log_dt, a_re, a_im, b_re, b_im, c_im):
    n = S5_NG * S5_GS
    tr = lambda b: jnp.transpose(b, (0, 2, 1)).reshape(n, S5_P)
    return [log_dt.reshape(S5_NG, 1), a_re, a_im, tr(b_re), tr(b_im), c_im.reshape(n, S5_P)]


def _s5_blockdiag(t):
    nb, gl, a, b = t.shape
    eye = jnp.eye(gl, dtype=t.dtype)
    return (t[:, :, :, None, :] * eye[None, :, None, :, None]).reshape(nb, gl * a, gl * b)


def _s5_diag(t, a, b):
    nb = t.shape[0]
    gl = t.shape[1] // a
    eye = jnp.eye(gl, dtype=t.dtype)
    return jnp.sum(t.reshape(nb, gl, a, gl, b) * eye[None, :, None, :, None], axis=3)


def _s5_weights(bbr, bbi, c_re, cneg):
    sh = (_S5_NB, S5_BLK, S5_GS, S5_P)
    wb = jnp.concatenate([_s5_blockdiag(bbr.reshape(sh)), _s5_blockdiag(bbi.reshape(sh))], axis=2)
    tr = lambda cc: jnp.transpose(cc.reshape(sh), (0, 1, 3, 2))
    wc = jnp.concatenate([_s5_blockdiag(tr(c_re)), _s5_blockdiag(tr(cneg))], axis=1)
    return wb, wc


def s5_fwd(h, gm, prm, dsk, w_glu, tag):
    log_dt, a_re, a_im, b_re, b_im, c_re, c_im = prm
    hn = rowwise(f_rmsnorm, [gm], [h], [(D, F32)], bt=512, name=f"{tag}_norm")[0]
    pargs = _s5_param_args(log_dt, a_re, a_im, b_re, b_im, c_im)
    abr, abi, bbr, bbi, cneg = s5_param_fwd(pargs, name=f"{tag}_param")
    wb, wc = _s5_weights(bbr, bbi, c_re.reshape(S5_NG * S5_GS, S5_P), cneg)
    ar, ai = abr.reshape(_S5_NB, 1, _S5_W), abi.reshape(_S5_NB, 1, _S5_W)
    wb, wc = wb.astype(BF16), wc.astype(BF16)
    x, ycp = s5_scan_fwd(hn, wb, ar, ai, wc, name=f"{tag}_scan")
    yg = rowwise(f_s5_act, [dsk], [ycp, hn], [(D, BF16)], bt=512, name=f"{tag}_act")[0]
    vg = matmul(yg, w_glu[None], "nn", name=f"{tag}_glu")
    h2 = rowwise(f_glu_res, [], [vg, h], [(D, F32)], bt=512, name=f"{tag}_out")[0]
    return h2, (h, hn, pargs, wb, wc, ar, ai, x, ycp, yg, vg)


def s5_bwd(d, saved, gm, dsk, w_glu, tag):
    h, hn, pargs, wb, wc, ar, ai, x, ycp, yg, vg = saved
    _, (dvg,) = rowwise_vjp(f_glu, [], [vg], [d], [BF16], bt=256, name=f"{tag}_dout")
    dwglu = matmul(yg, dvg, "tn", name=f"{tag}_dwglu")[0]
    dyg = matmul(dvg, w_glu[None], "nt", name=f"{tag}_dyg")
    (ddsk,), (dycp, dhn1) = rowwise_vjp(f_s5_act, [dsk], [ycp, hn], [dyg], [F32, F32], bt=256, name=f"{tag}_dact")
    dhn2, dwb, dwc, dar, dai = s5_scan_bwd(dycp, x, hn, wb, ar, ai, wc, name=f"{tag}_dscan")
    dh, dgm = _norm_bwd(h, gm, [dhn1, dhn2, d], f"{tag}_dnorm")
    n = S5_NG * S5_GS
    half = S5_BLK * S5_P
    d_bbr = _s5_diag(dwb[:, :, :half], S5_GS, S5_P).reshape(n, S5_P)
    d_bbi = _s5_diag(dwb[:, :, half:], S5_GS, S5_P).reshape(n, S5_P)
    from_c = lambda t: jnp.transpose(_s5_diag(t, S5_P, S5_GS), (0, 1, 3, 2)).reshape(n, S5_P)
    d_cre = from_c(dwc[:, :half, :])
    d_cneg = from_c(dwc[:, half:, :])
    cts = [dar.reshape(S5_NG, S5_P), dai.reshape(S5_NG, S5_P), d_bbr, d_bbi, d_cneg]
    dlog, dare, daim, dbre_t, dbim_t, dcim = s5_param_bwd(pargs, cts, name=f"{tag}_dparam")
    untr = lambda t: jnp.transpose(t.reshape(S5_NG, S5_GS, S5_P), (0, 2, 1))
    grads = (dlog.reshape(S5_NG), dare, daim, untr(dbre_t), untr(dbim_t),
             d_cre.reshape(S5_NG, S5_GS, S5_P), dcim.reshape(S5_NG, S5_GS, S5_P))
    return dh, dgm, grads, ddsk, dwglu


def _pad_last(w, n):
    return jnp.pad(w, [(0, 0)] * (w.ndim - 1) + [(0, n - w.shape[-1])])


_BIG = ("gla_w_in", "gla_w_out", "ssd_w_in", "ssd_w_out", "s5_w_glu", "ffn_w_gu", "ffn_w_down")


def interleave_gu(w):
    q = w.shape[-1] // 4
    return jnp.concatenate([w[..., :q], w[..., 2 * q:3 * q], w[..., q:2 * q], w[..., 3 * q:]], axis=-1)


def local_step(x, target, W, later_weights=None, later_grads=None, ffn0_grads=None, ffn0_weights=None):
    f32 = lambda a: a.astype(F32)
    row = lambda a: f32(a).reshape(1, -1)

    def layer_args(i):
        m, j = i % 3, i // 3
        gm = row(W["norm_mix_g"][i])
        if m == 0:
            args = (gm, W["gla_w_in"][j], jnp.pad(f32(W["gla_w_a2"][j]), ((0, LANE - GLA_RANK), (0, 0))),
                    row(W["gla_b_a"][j]), row(W["gla_norm_g"][j]), W["gla_w_out"][j])
        elif m == 1:
            pl_ = lambda a: _pad_last(row(a), LANE)
            args = (gm, W["ssd_w_in"][j], f32(W["ssd_conv_w"][j]),
                    row(W["ssd_conv_b"][j]), pl_(W["ssd_dt_bias"][j]), pl_(W["ssd_a_log"][j]), pl_(W["ssd_d"][j]),
                    row(W["ssd_norm_g"][j]), W["ssd_w_out"][j])
        else:
            prm = tuple(f32(W[k][j]) for k in ("s5_log_dt", "s5_a_re", "s5_a_im", "s5_b_re", "s5_b_im", "s5_c_re", "s5_c_im"))
            args = (gm, prm, row(W["s5_d"][j]), W["s5_w_glu"][j])
        return m, j, args

    h = x
    saved, mixers, ffns = [], [], []
    for i in range(DEPTH):
        mixer = layer_args(i)
        mixers.append(mixer)
        m, j, args = mixer
        tag = f"l{i}_{('gla', 'ssd', 's5')[m]}"
        h, sm = (gla_fwd, ssd_fwd, s5_fwd)[m](h, *args, tag)
        if i == 0 and ffn0_weights is not None:
            W = {**W, **ffn0_weights(h)}
        ffn = (row(W["norm_ffn_g"][i]), W["ffn_w_gu"][i], W["ffn_w_down"][i])
        ffns.append(ffn)
        h, sf = ffn_fwd(h, *ffn, f"l{i}_ffn")
        saved.append((sm, sf))
        if i == 0 and later_weights is not None:
            W = {**W, **later_weights(h)}
    loss, dfg, d = loss_head(h, row(W["final_norm_g"]), target, name="loss_head")

    G = {k: [None] * len(v) for k, v in W.items() if k != "final_norm_g"}
    G["final_norm_g"] = dfg.reshape(D)
    for i in reversed(range(DEPTH)):
        m, j, args = mixers[i]
        sm, sf = saved[i]
        if i == 0 and later_grads is not None:
            zero = later_grads(G)
            ffns[0] = (ffns[0][0], ffns[0][1], ffns[0][2] + zero.astype(ffns[0][2].dtype))
        d, dg, dwgu, dwd = ffn_bwd(d, sf, *ffns[i], f"l{i}_ffn")
        G["norm_ffn_g"][i], G["ffn_w_gu"][i], G["ffn_w_down"][i] = dg.reshape(D), dwgu, dwd
        if i == 0 and ffn0_grads is not None:
            zero = ffn0_grads(G)
            args = args[:-1] + (args[-1] + zero.astype(args[-1].dtype),)
        tag = f"l{i}_{('gla', 'ssd', 's5')[m]}"
        if m == 0:
            d, dgm, dwin, dwa2, dba, dng, dwout = gla_bwd(d, sm, *args, tag)
            G["gla_w_in"][j], G["gla_w_a2"][j], G["gla_b_a"][j] = dwin, dwa2, dba.reshape(-1)
            G["gla_norm_g"][j], G["gla_w_out"][j] = dng.reshape(-1), dwout
        elif m == 1:
            d, dgm, dwin, dcw, dcb, ddtb, dal, ddsk, dng, dwout = ssd_bwd(d, sm, *args, tag)
            G["ssd_w_in"][j], G["ssd_conv_w"][j], G["ssd_conv_b"][j] = dwin, dcw, dcb.reshape(-1)
            G["ssd_dt_bias"][j], G["ssd_a_log"][j], G["ssd_d"][j] = ddtb.reshape(-1), dal.reshape(-1), ddsk.reshape(-1)
            G["ssd_norm_g"][j], G["ssd_w_out"][j] = dng.reshape(-1), dwout
        else:
            d, dgm, pg, ddsk, dwglu = s5_bwd(d, sm, args[0], args[2], args[3], tag)
            for k, v in zip(("s5_log_dt", "s5_a_re", "s5_a_im", "s5_b_re", "s5_b_im", "s5_c_re", "s5_c_im"), pg):
                G[k][j] = v
            G["s5_d"][j], G["s5_w_glu"][j] = ddsk.reshape(-1), dwglu
        G["norm_mix_g"][i] = dgm.reshape(D)
    grads = {k: (v if k == "final_norm_g" or k in _BIG else jnp.stack(v)) for k, v in G.items()}
    return loss, d, grads


_MESH = pl.DeviceIdType.MESH
_ANY = pl.BlockSpec(memory_space=pl.ANY)
_DMA = pltpu.SemaphoreType.DMA
_ROWS_ALIGN = 1024


def _place():
    return lax.axis_index("x"), lax.axis_index("y"), lax.axis_index("c")


def _other_chips(x, y):
    return [(1 - x, y), (x, 1 - y), (1 - x, 1 - y)]


def _remote(src, dst, send_sems, recv_sems, k, to):
    return pltpu.make_async_remote_copy(src_ref=src, dst_ref=dst, send_sem=send_sems.at[k], recv_sem=recv_sems.at[k],
                                        device_id=to, device_id_type=_MESH)


def gather_shards(loc, *, name):
    def body(in_ref, out_ref, send_sems, recv_sems, local_sem):
        x, y, c = _place()
        me, sibling = (x, y, c), (x, y, 1 - c)
        chips = _other_chips(x, y)

        def half(px, py, hc):
            return out_ref.at[2 * px + py, hc]

        mine = pltpu.make_async_copy(in_ref, out_ref.at[2 * x + y], local_sem)
        mine.start()
        first = [_remote(in_ref.at[c], half(x, y, c), send_sems, recv_sems, j, (*chip, c)) for j, chip in enumerate(chips)]
        for cp in first:
            cp.start()
        passed = [_remote(half(*chip, c), half(*chip, c), send_sems, recv_sems, 3 + j, sibling) for j, chip in enumerate(chips)]
        for j, chip in enumerate(chips):
            _remote(in_ref.at[c], half(*chip, c), send_sems, recv_sems, j, me).wait_recv()
            passed[j].start()
        for j, chip in enumerate(chips):
            _remote(in_ref.at[c], half(*chip, 1 - c), send_sems, recv_sems, 3 + j, me).wait_recv()
        for cp in first + passed:
            cp.wait_send()
        mine.wait()

    return pl.pallas_call(
        body, name=name, in_specs=[_ANY], out_specs=_ANY,
        out_shape=jax.ShapeDtypeStruct((4,) + loc.shape, loc.dtype),
        scratch_shapes=[_DMA((6,)), _DMA((6,)), _DMA(())],
    )(loc)


def _pos(px, py, perm):
    return 2 * py + px if perm else 2 * px + py


def _part(ref, kind, p, loc):
    if kind == "lead":
        return ref.at[p]
    return ref.at[:, pl.ds(pl.multiple_of(p * loc, LANE), loc)]


def _rows(ref, h, hr):
    return ref.at[pl.ds(h * hr, hr)]


def _rows_block(hr, width):
    return max(b for b in range(16, hr + 1, 16) if hr % b == 0 and (b * width <= (1 << 19) or b == 16))


def gather_big(locs, kinds, *, name):
    n = len(locs)

    def body(*refs):
        ins, outs = refs[:n], refs[n:2 * n]
        send_sems, recv_sems = refs[2 * n + 1:]
        refs[2 * n][...] = jnp.zeros_like(refs[2 * n])
        x, y, c = _place()
        me, sibling = (x, y, c), (x, y, 1 - c)
        chips = _other_chips(x, y)

        def half(i, px, py, h):
            (kind, perm), (rows, loc) = kinds[i], locs[i].shape
            return _rows(_part(outs[i], kind, _pos(px, py, perm), loc), h, rows // 2)

        sends = []
        for i in range(n):
            (kind, perm), (rows, loc) = kinds[i], locs[i].shape
            own = _part(outs[i], kind, _pos(x, y, perm), loc)
            sends.append(_remote(ins[i], own, send_sems, recv_sems, 6 * n + i, sibling))
            sends[-1].start()
            for j, chip in enumerate(chips):
                sends.append(_remote(_rows(ins[i], c, rows // 2), half(i, x, y, c), send_sems, recv_sems, 6 * i + j, (*chip, c)))
                sends[-1].start()
        for i in range(n):
            hr = locs[i].shape[0] // 2
            for j, chip in enumerate(chips):
                _remote(_rows(ins[i], c, hr), half(i, *chip, c), send_sems, recv_sems, 6 * i + j, me).wait_recv()
                sends.append(_remote(half(i, *chip, c), half(i, *chip, c), send_sems, recv_sems, 6 * i + 3 + j, sibling))
                sends[-1].start()
        for i in range(n):
            (kind, perm), (rows, loc) = kinds[i], locs[i].shape
            for j, chip in enumerate(chips):
                _remote(_rows(ins[i], c, rows // 2), half(i, *chip, 1 - c), send_sems, recv_sems, 6 * i + 3 + j, me).wait_recv()
            _remote(ins[i], _part(outs[i], kind, _pos(x, y, perm), loc), send_sems, recv_sems, 6 * n + i, me).wait_recv()
        for cp in sends:
            cp.wait_send()

    def out_shape(a, kind):
        rows, loc = a.shape
        return jax.ShapeDtypeStruct((4, rows, loc) if kind == "lead" else (rows, 4 * loc), a.dtype)

    outs = pl.pallas_call(
        body, name=name, in_specs=[_ANY] * n, out_specs=[_ANY] * n + [pl.BlockSpec(memory_space=pltpu.VMEM)],
        out_shape=[out_shape(a, k[0]) for a, k in zip(locs, kinds)] + [jax.ShapeDtypeStruct((8, LANE), F32)],
        scratch_shapes=[_DMA((7 * n,)), _DMA((7 * n,))],
    )(*locs)
    return list(outs[:n]), outs[n][0, 0]


_HBM = pl.BlockSpec(memory_space=pltpu.HBM)
_SEM = pl.BlockSpec(memory_space=pltpu.SEMAPHORE)
_EFFECT = pltpu.SideEffectType.DATAFLOW_SIDE_EFFECTING


def _in_hbm(a):
    return pltpu.with_memory_space_constraint(a, pltpu.HBM)


def _gather_ici_copies(ins, lands, kinds, shapes, send_sems, recv_sems):
    x, y, c = _place()
    sends, arrivals = [], []
    for i, ((kind, perm), (rows, loc)) in enumerate(zip(kinds, shapes)):
        hr = rows // 2
        mine = _part(lands[i], kind, _pos(x, y, perm), loc)
        sends.append(_remote(ins[i], mine, send_sems, recv_sems, 4 * i + 3, (x, y, 1 - c)))
        arrivals.append(_remote(ins[i], mine, send_sems, recv_sems, 4 * i + 3, (x, y, c)))
        for j, (px, py) in enumerate(_other_chips(x, y)):
            sends.append(_remote(_rows(ins[i], c, hr), _rows(mine, c, hr), send_sems, recv_sems, 4 * i + j, (px, py, c)))
            theirs = _rows(_part(lands[i], kind, _pos(px, py, perm), loc), c, hr)
            arrivals.append(_remote(_rows(ins[i], c, hr), theirs, send_sems, recv_sems, 4 * i + j, (x, y, c)))
    return sends, arrivals


def gather_start(locs, kinds, *, name):
    n = len(locs)
    shapes = [a.shape for a in locs]

    def land_shape(a, kind):
        rows, loc = a.shape
        return (4, rows, loc) if kind == "lead" else (rows, 4 * loc)

    def body(*refs):
        sends, _ = _gather_ici_copies(refs[:n], refs[n:2 * n], kinds, shapes, refs[2 * n], refs[2 * n + 1])
        for cp in sends:
            cp.start()
        refs[-1][...] = jnp.zeros_like(refs[-1])

    lands = [lax.empty(land_shape(a, k[0]), a.dtype) for a, k in zip(locs, kinds)]
    outs = pl.pallas_call(
        body, name=name, in_specs=[_HBM] * (2 * n), out_specs=[_SEM, _SEM] + [_HBM] * (2 * n) + [pl.BlockSpec(memory_space=pltpu.VMEM)],
        out_shape=[_DMA((4 * n,)), _DMA((4 * n,))] + [pltpu.HBM(a.shape, a.dtype) for a in locs]
        + [pltpu.HBM(l.shape, l.dtype) for l in lands] + [jax.ShapeDtypeStruct((8, LANE), F32)],
        input_output_aliases={i: 2 + i for i in range(2 * n)},
        compiler_params=pltpu.CompilerParams(has_side_effects=_EFFECT),
    )(*[_in_hbm(a) for a in locs], *[_in_hbm(l) for l in lands])
    return outs[0], outs[1], list(outs[2:2 + n]), list(outs[2 + n:2 + 2 * n]), outs[-1][0, 0]


def gather_wait(send_sems, recv_sems, locs, lands, kinds, after, *, name):
    n = len(locs)
    shapes = [a.shape for a in locs]

    def body(*refs):
        sends, arrivals = _gather_ici_copies(refs[:n], refs[n:2 * n], kinds, shapes, refs[2 * n], refs[2 * n + 1])
        for cp in sends:
            cp.wait_send()
        for cp in arrivals:
            cp.wait_recv()

    outs = pl.pallas_call(
        body, name=name, in_specs=[_HBM] * (2 * n) + [_SEM, _SEM, _ANY], out_specs=[_HBM] * (2 * n),
        out_shape=[pltpu.HBM(a.shape, a.dtype) for a in locs] + [pltpu.HBM(l.shape, l.dtype) for l in lands],
        input_output_aliases={i: i for i in range(2 * n)},
        compiler_params=pltpu.CompilerParams(has_side_effects=_EFFECT),
    )(*locs, *lands, send_sems, recv_sems, after)
    return list(outs[n:])


def gather_finish(lands, kinds, shapes, *, name):
    n = len(lands)

    def body(*refs):
        bufs = refs[n:2 * n]
        send_sems, recv_sems = refs[2 * n:]
        x, y, c = _place()
        sends = []
        for i, ((kind, perm), (rows, loc)) in enumerate(zip(kinds, shapes)):
            for j, (px, py) in enumerate(_other_chips(x, y)):
                part = _part(bufs[i], kind, _pos(px, py, perm), loc)
                sends.append(_remote(_rows(part, c, rows // 2), _rows(part, c, rows // 2), send_sems, recv_sems, 3 * i + j, (x, y, 1 - c)))
                sends[-1].start()
        for i, ((kind, perm), (rows, loc)) in enumerate(zip(kinds, shapes)):
            for j, (px, py) in enumerate(_other_chips(x, y)):
                part = _part(bufs[i], kind, _pos(px, py, perm), loc)
                _remote(_rows(part, c, rows // 2), _rows(part, 1 - c, rows // 2), send_sems, recv_sems, 3 * i + j, (x, y, c)).wait_recv()
        for cp in sends:
            cp.wait_send()

    return list(pl.pallas_call(
        body, name=name, in_specs=[_ANY] * n, out_specs=[_ANY] * n,
        out_shape=[jax.ShapeDtypeStruct(l.shape, l.dtype) for l in lands],
        input_output_aliases={i: i for i in range(n)}, scratch_shapes=[_DMA((3 * n,)), _DMA((3 * n,))],
    )(*lands))


def _scatter_copies(qs, lands, kinds, locs, send_sems, recv_sems):
    x, y, c = _place()
    sends, arrivals = [], []
    for i, (kind, perm) in enumerate(kinds):
        for j, (px, py) in enumerate(_other_chips(x, y)):
            src = _part(qs[i], kind, _pos(px, py, perm), locs[i])
            sends.append(_remote(src, lands[i].at[j], send_sems, recv_sems, 3 * i + j, (px, py, c)))
            arrivals.append(_remote(src, lands[i].at[j], send_sems, recv_sems, 3 * i + j, (x, y, c)))
    return sends, arrivals


def _scatter_land(q, kind, loc):
    return (3, q.shape[1] if kind == "lead" else q.shape[0], loc)


def scatter_start(qs, kinds, locs, *, name):
    n = len(qs)

    def body(*refs):
        sends, _ = _scatter_copies(refs[:n], refs[n:2 * n], kinds, locs, refs[2 * n], refs[2 * n + 1])
        for cp in sends:
            cp.start()
        refs[-1][...] = jnp.zeros_like(refs[-1])

    lands = [lax.empty(_scatter_land(q, k[0], l), q.dtype) for q, k, l in zip(qs, kinds, locs)]
    outs = pl.pallas_call(
        body, name=name, in_specs=[_HBM] * (2 * n), out_specs=[_SEM, _SEM] + [_HBM] * (2 * n) + [pl.BlockSpec(memory_space=pltpu.VMEM)],
        out_shape=[_DMA((3 * n,)), _DMA((3 * n,))] + [pltpu.HBM(q.shape, q.dtype) for q in qs]
        + [pltpu.HBM(l.shape, l.dtype) for l in lands] + [jax.ShapeDtypeStruct((8, LANE), F32)],
        input_output_aliases={i: 2 + i for i in range(2 * n)},
        compiler_params=pltpu.CompilerParams(has_side_effects=_EFFECT),
    )(*[_in_hbm(q) for q in qs], *[_in_hbm(l) for l in lands])
    return outs[0], outs[1], list(outs[2:2 + n]), list(outs[2 + n:2 + 2 * n]), outs[-1][0, 0]


def scatter_wait(send_sems, recv_sems, qs, lands, kinds, locs, after, *, name):
    n = len(qs)

    def body(*refs):
        sends, arrivals = _scatter_copies(refs[:n], refs[n:2 * n], kinds, locs, refs[2 * n], refs[2 * n + 1])
        for cp in sends:
            cp.wait_send()
        for cp in arrivals:
            cp.wait_recv()

    outs = pl.pallas_call(
        body, name=name, in_specs=[_HBM] * (2 * n) + [_SEM, _SEM, _ANY], out_specs=[_HBM] * (2 * n),
        out_shape=[pltpu.HBM(q.shape, q.dtype) for q in qs] + [pltpu.HBM(l.shape, l.dtype) for l in lands],
        input_output_aliases={i: i for i in range(2 * n)},
        compiler_params=pltpu.CompilerParams(has_side_effects=_EFFECT),
    )(*qs, *lands, send_sems, recv_sems, after)
    return list(outs[:n]), list(outs[n:])


def _pair_swap_copies(ins, lands, kinds, send_sems, recv_sems):
    x, y, c = _place()
    sends, arrivals = [], []
    for i, (kind, _) in enumerate(kinds):
        if kind == "lead":
            hr = ins[i].shape[1] // 2
            src = ins[i].at[:, pl.ds((1 - c) * hr, hr)]
        else:
            src = _rows(ins[i], 1 - c, ins[i].shape[0] // 2)
        sends.append(_remote(src, lands[i], send_sems, recv_sems, i, (x, y, 1 - c)))
        arrivals.append(_remote(src, lands[i], send_sems, recv_sems, i, (x, y, c)))
    return sends, arrivals


def _pair_swap_land(a, kind):
    s = a.shape
    return (4, s[1] // 2, s[2]) if kind == "lead" else (s[0] // 2, s[1])


def pair_swap_start(ps, kinds, *, name):
    n = len(ps)

    def body(*refs):
        sends, _ = _pair_swap_copies(refs[:n], refs[n:2 * n], kinds, refs[2 * n], refs[2 * n + 1])
        for cp in sends:
            cp.start()
        refs[-1][...] = jnp.zeros_like(refs[-1])

    lands = [lax.empty(_pair_swap_land(p, k[0]), p.dtype) for p, k in zip(ps, kinds)]
    outs = pl.pallas_call(
        body, name=name, in_specs=[_HBM] * (2 * n), out_specs=[_SEM, _SEM] + [_HBM] * (2 * n) + [pl.BlockSpec(memory_space=pltpu.VMEM)],
        out_shape=[_DMA((n,)), _DMA((n,))] + [pltpu.HBM(p.shape, p.dtype) for p in ps]
        + [pltpu.HBM(l.shape, l.dtype) for l in lands] + [jax.ShapeDtypeStruct((8, LANE), F32)],
        input_output_aliases={i: 2 + i for i in range(2 * n)},
        compiler_params=pltpu.CompilerParams(has_side_effects=_EFFECT),
    )(*[_in_hbm(p) for p in ps], *[_in_hbm(l) for l in lands])
    return outs[0], outs[1], list(outs[2:2 + n]), list(outs[2 + n:2 + 2 * n]), outs[-1][0, 0]


def pair_swap_wait(send_sems, recv_sems, ps, lands, kinds, after, *, name):
    n = len(ps)

    def body(*refs):
        sends, arrivals = _pair_swap_copies(refs[:n], refs[n:2 * n], kinds, refs[2 * n], refs[2 * n + 1])
        for cp in sends:
            cp.wait_send()
        for cp in arrivals:
            cp.wait_recv()

    outs = pl.pallas_call(
        body, name=name, in_specs=[_HBM] * (2 * n) + [_SEM, _SEM, _ANY], out_specs=[_HBM] * (2 * n),
        out_shape=[pltpu.HBM(p.shape, p.dtype) for p in ps] + [pltpu.HBM(l.shape, l.dtype) for l in lands],
        input_output_aliases={i: i for i in range(2 * n)},
        compiler_params=pltpu.CompilerParams(has_side_effects=_EFFECT),
    )(*ps, *lands, send_sems, recv_sems, after)
    return list(outs[:n]), list(outs[n:])


def pair_swap(ps, kinds, *, name):
    n = len(ps)

    def body(*refs):
        ins, outs = refs[:n], refs[n:2 * n]
        send_sems, recv_sems = refs[2 * n:]
        x, y, c = _place()
        cps = []
        for i in range(n):
            if kinds[i][0] == "lead":
                hr = ps[i].shape[1] // 2
                src = ins[i].at[:, pl.ds((1 - c) * hr, hr)]
            else:
                hr = ps[i].shape[0] // 2
                src = _rows(ins[i], 1 - c, hr)
            cps.append(_remote(src, outs[i], send_sems, recv_sems, i, (x, y, 1 - c)))
            cps[-1].start()
        for cp in cps:
            cp.wait()

    def out_shape(a, kind):
        s = a.shape
        return jax.ShapeDtypeStruct((4, s[1] // 2, s[2]) if kind == "lead" else (s[0] // 2, s[1]), a.dtype)

    return pl.pallas_call(
        body, name=name, in_specs=[_ANY] * n, out_specs=[_ANY] * n,
        out_shape=[out_shape(a, k[0]) for a, k in zip(ps, kinds)], scratch_shapes=[_DMA((n,)), _DMA((n,))],
    )(*ps)


def pair_add(p, got, c_arr, kind, *, name):
    if kind == "lead":
        _, hr, cols = got.shape
        br = _rows_block(hr, cols)
        nb = hr // br
        grid = (4, nb)
        p_spec = pl.BlockSpec((None, br, cols), lambda s, i, cr: (s, cr[0] * nb + i, 0))
        g_spec = pl.BlockSpec((None, br, cols), lambda s, i, cr: (s, i, 0))
    else:
        hr, w = got.shape
        br = _rows_block(hr, w)
        nb = hr // br
        grid = (nb,)
        p_spec = pl.BlockSpec((br, w), lambda i, cr: (cr[0] * nb + i, 0))
        g_spec = pl.BlockSpec((br, w), lambda i, cr: (i, 0))

    def body(c_ref, p_ref, g_ref, o_ref):
        o_ref[...] = (p_ref[...] + g_ref[...]).astype(o_ref.dtype)

    return pl.pallas_call(
        body, name=name, out_shape=jax.ShapeDtypeStruct(got.shape, BF16),
        grid_spec=pltpu.PrefetchScalarGridSpec(num_scalar_prefetch=1, grid=grid, in_specs=[p_spec, g_spec], out_specs=g_spec),
        compiler_params=_cparams(len(grid)),
    )(c_arr, p, got)


def chip_scatter(qs, kinds, locs, *, name):
    n = len(qs)

    def body(*refs):
        ins, outs = refs[:n], refs[n:2 * n]
        send_sems, recv_sems = refs[2 * n:]
        x, y, c = _place()
        cps = []
        for i in range(n):
            kind, perm = kinds[i]
            for j, (px, py) in enumerate(_other_chips(x, y)):
                cps.append(_remote(_part(ins[i], kind, _pos(px, py, perm), locs[i]), outs[i].at[j], send_sems, recv_sems,
                                   3 * i + j, (px, py, c)))
                cps[-1].start()
        for cp in cps:
            cp.wait()

    def out_shape(a, kind, loc):
        hr = a.shape[1] if kind == "lead" else a.shape[0]
        return jax.ShapeDtypeStruct((3, hr, loc), a.dtype)

    return pl.pallas_call(
        body, name=name, in_specs=[_ANY] * n, out_specs=[_ANY] * n,
        out_shape=[out_shape(a, k[0], l) for a, k, l in zip(qs, kinds, locs)],
        scratch_shapes=[_DMA((3 * n,)), _DMA((3 * n,))],
    )(*qs)


def chip_add(q, r, pos_arr, c_arr, kind, loc, *, name):
    _, hr, _ = r.shape
    br = _rows_block(hr, loc)
    nb = hr // br
    if kind == "lead":
        q_spec = pl.BlockSpec((None, br, loc), lambda i, pr, cr: (pr[0], i, 0))
    else:
        q_spec = pl.BlockSpec((br, loc), lambda i, pr, cr: (i, pr[0]))
    r_spec = pl.BlockSpec((3, br, loc), lambda i, pr, cr: (0, i, 0))
    o_spec = pl.BlockSpec((br, loc), lambda i, pr, cr: (cr[0] * nb + i, 0))

    def body(p_ref, c_ref, q_ref, r_ref, o_ref):
        acc = q_ref[...].astype(F32)
        for j in range(3):
            acc = acc + r_ref[j].astype(F32)
        o_ref[...] = acc

    return pl.pallas_call(
        body, name=name, out_shape=jax.ShapeDtypeStruct((2 * hr, loc), F32),
        grid_spec=pltpu.PrefetchScalarGridSpec(num_scalar_prefetch=2, grid=(nb,), in_specs=[q_spec, r_spec], out_specs=o_spec),
        compiler_params=_cparams(1),
    )(pos_arr, c_arr, q, r)


def share_rows(fs, *, name):
    n = len(fs)

    def body(*refs):
        bufs = refs[n:2 * n]
        send_sems, recv_sems = refs[2 * n:]
        x, y, c = _place()
        cps = []
        for i in range(n):
            hr = fs[i].shape[0] // 2
            cps.append(_remote(_rows(bufs[i], c, hr), _rows(bufs[i], c, hr), send_sems, recv_sems, i, (x, y, 1 - c)))
            cps[-1].start()
        for i, cp in enumerate(cps):
            hr = fs[i].shape[0] // 2
            _remote(_rows(bufs[i], c, hr), _rows(bufs[i], 1 - c, hr), send_sems, recv_sems, i, (x, y, c)).wait_recv()
            cp.wait_send()

    return pl.pallas_call(
        body, name=name, in_specs=[_ANY] * n, out_specs=[_ANY] * n,
        out_shape=[jax.ShapeDtypeStruct(f.shape, f.dtype) for f in fs],
        input_output_aliases={i: i for i in range(n)}, scratch_shapes=[_DMA((n,)), _DMA((n,))],
    )(*fs)


def _gather_all_copies(v_ref, land_ref, send_sems, recv_sems):
    x, y, c = _place()
    flip = lambda p, m: 1 - p if m else p
    idx = lambda p: 4 * p[0] + 2 * p[1] + p[2]
    sends, arrivals = [], []
    for k, m in enumerate(range(1, 8)):
        p = (flip(x, m & 4), flip(y, m & 2), flip(c, m & 1))
        sends.append(_remote(v_ref, land_ref.at[idx((x, y, c))], send_sems, recv_sems, k, p))
        arrivals.append(_remote(v_ref, land_ref.at[idx(p)], send_sems, recv_sems, k, (x, y, c)))
    return sends, arrivals


def gather_all_start(v, *, name):
    def body(v_ref, land_ref, send_sems, recv_sems, v_thru, land_thru, token):
        sends, _ = _gather_all_copies(v_ref, land_ref, send_sems, recv_sems)
        for cp in sends:
            cp.start()
        token[...] = jnp.zeros_like(token)

    land = jnp.zeros((8,) + v.shape, v.dtype)
    outs = pl.pallas_call(
        body, name=name, in_specs=[_HBM, _HBM], out_specs=[_SEM, _SEM, _HBM, _HBM, pl.BlockSpec(memory_space=pltpu.VMEM)],
        out_shape=[_DMA((7,)), _DMA((7,)), pltpu.HBM(v.shape, v.dtype), pltpu.HBM(land.shape, land.dtype),
                   jax.ShapeDtypeStruct((8, LANE), F32)],
        input_output_aliases={0: 2, 1: 3}, compiler_params=pltpu.CompilerParams(has_side_effects=_EFFECT),
    )(_in_hbm(v), _in_hbm(land))
    return outs[0], outs[1], outs[2], outs[3], outs[4][0, 0]


def gather_all_wait(send_sems, recv_sems, v, land, after, *, name):
    def body(v_ref, land_ref, send_sems, recv_sems, after_ref, v_dead, got_ref):
        sends, arrivals = _gather_all_copies(v_ref, land_ref, send_sems, recv_sems)
        for cp in sends:
            cp.wait_send()
        for cp in arrivals:
            cp.wait_recv()

    return pl.pallas_call(
        body, name=name, in_specs=[_HBM, _HBM, _SEM, _SEM, _ANY], out_specs=[_HBM, _HBM],
        out_shape=[pltpu.HBM(v.shape, v.dtype), pltpu.HBM(land.shape, land.dtype)],
        input_output_aliases={0: 0, 1: 1}, compiler_params=pltpu.CompilerParams(has_side_effects=_EFFECT),
    )(v, land, send_sems, recv_sems, after)[1]


def sum_slots(land, v, me_arr, *, name):
    n, R, L = land.shape
    br = _pick(R, _ROWS_ALIGN, 8)

    def body(me_ref, land_ref, v_ref, o_ref):
        acc = None
        for i in range(n):
            term = jnp.where(me_ref[0] == i, v_ref[...], land_ref[i])
            acc = term if acc is None else acc + term
        o_ref[...] = acc

    row = pl.BlockSpec((br, L), lambda i, me: (i, 0))
    return pl.pallas_call(
        body, name=name, out_shape=jax.ShapeDtypeStruct((R, L), land.dtype),
        grid_spec=pltpu.PrefetchScalarGridSpec(num_scalar_prefetch=1, grid=(R // br,),
                                               in_specs=[pl.BlockSpec((n, br, L), lambda i, me: (0, i, 0)), row], out_specs=row),
        compiler_params=_cparams(1),
    )(me_arr, land, v)


def adamw(w, g, m, v, *, name):
    shape = w.shape
    size = math.prod(shape)
    last = shape[-1]
    if last % LANE != 0 and size % LANE == 0 and size <= (1 << 20):
        last = LANE
    rows = size // last
    budget = (1 << 18) // last
    br = rows
    if rows > budget:
        br = max(c for c in range(8, budget + 1, 8) if rows % c == 0)
    v2 = lambda a: a.reshape(rows, last)

    def body(w_ref, g_ref, m_ref, v_ref, d_ref, nm_ref, nv_ref):
        gg = g_ref[...]
        nm = ADAM_B1 * m_ref[...] + (1.0 - ADAM_B1) * gg
        nv = ADAM_B2 * v_ref[...] + (1.0 - ADAM_B2) * (gg * gg)
        m_hat = nm / (1.0 - ADAM_B1 ** ADAM_STEP)
        v_hat = nv / (1.0 - ADAM_B2 ** ADAM_STEP)
        d_ref[...] = -ADAM_LR * (m_hat / (jnp.sqrt(v_hat) + ADAM_EPS) + ADAM_WD * w_ref[...])
        nm_ref[...] = nm
        nv_ref[...] = nv

    spec = pl.BlockSpec((br, last), lambda i: (i, 0))
    outs = pl.pallas_call(
        body, name=name, grid=(rows // br,), in_specs=[spec] * 4, out_specs=[spec] * 3,
        out_shape=[jax.ShapeDtypeStruct((rows, last), F32)] * 3, compiler_params=_cparams(1),
    )(v2(w), v2(g), v2(m), v2(v))
    return [o.reshape(shape) for o in outs]


_WEIGHTS = ["norm_mix_g", "norm_ffn_g", "gla_w_in", "gla_w_a2", "gla_b_a", "gla_norm_g", "gla_w_out", "ssd_w_in",
            "ssd_conv_w", "ssd_conv_b", "ssd_dt_bias", "ssd_a_log", "ssd_d", "ssd_norm_g", "ssd_w_out", "s5_log_dt",
            "s5_a_re", "s5_a_im", "s5_b_re", "s5_b_im", "s5_c_re", "s5_c_im", "s5_d", "s5_w_glu", "ffn_w_gu",
            "ffn_w_down", "final_norm_g"]
_SHARD_AXIS = {"gla_w_in": 2, "gla_w_a2": 2, "gla_b_a": 1, "gla_norm_g": 1, "gla_w_out": 1, "ssd_w_in": 2,
               "ssd_conv_w": 2, "ssd_w_out": 1, "s5_d": 1, "s5_w_glu": 2, "ffn_w_gu": 2, "ffn_w_down": 1}
_SMALL_SHARDED = [n for n in _WEIGHTS if n in _SHARD_AXIS and n not in _BIG]
_REPLICATED = [n for n in _WEIGHTS if n not in _SHARD_AXIS]
_BIG_KIND = {"gla_w_in": ("lead", False), "gla_w_out": ("lead", False), "ssd_w_in": ("lead", False),
             "ssd_w_out": ("lead", False), "s5_w_glu": ("cols", False), "ffn_w_gu": ("cols", True),
             "ffn_w_down": ("lead", False)}
_PADDED_IN = {"gla_w_in": GLA_INP, "ssd_w_in": SSD_INP}


def _to_rows(flat, parts=1):
    per = -(-flat.shape[0] // (parts * LANE * _ROWS_ALIGN)) * _ROWS_ALIGN
    flat = jnp.pad(flat, (0, parts * per * LANE - flat.shape[0]))
    return flat.reshape(parts, per, LANE)


def _big_layers(local):
    return [(n, j, local[n][j].reshape(-1, local[n].shape[-1])) for n in _BIG for j in range(local[n].shape[0])]


def _in_layer0(n, j):
    return j == 0 and n in ("gla_w_in", "gla_w_out", "ffn_w_gu", "ffn_w_down")


def _assemble(n, g):
    if n in _PADDED_IN:
        return jnp.concatenate([g[s] for s in range(4)] + [jnp.zeros((g.shape[1], _PADDED_IN[n] - 4 * g.shape[2]), BF16)], axis=1)
    if _BIG_KIND[n][0] == "lead":
        return g.reshape(4 * g.shape[1], g.shape[2])
    return g


def _is_gla0(n, j):
    return j == 0 and n in ("gla_w_in", "gla_w_out")


def _gather_first(local):
    layers = _big_layers(local)
    first = [l for l in layers if _is_gla0(l[0], l[1])]
    full = {n: [None] * local[n].shape[0] for n in _BIG}
    got, done = gather_big([w.astype(BF16) for _, _, w in first], [_BIG_KIND[n] for n, _, _ in first], name="gather_weights_first")
    for (n, j, _), g in zip(first, got):
        full[n][j] = _assemble(n, g)
    flat = jnp.concatenate([local[n].astype(F32).reshape(-1) for n in _SMALL_SHARDED])
    got = gather_shards(_to_rows(flat, 2), name="gather_small_weights").reshape(4, -1)
    off = 0
    for n in _SMALL_SHARDED:
        bs = local[n].shape
        sz = math.prod(bs)
        seg = got[:, off:off + sz].reshape((4,) + bs)
        off += sz
        ax = _SHARD_AXIS[n]
        full[n] = jnp.moveaxis(seg, 0, ax).reshape(bs[:ax] + (4 * bs[ax],) + bs[ax + 1:])
    pending = {}
    for tag, want in (("ffn0", _is_ffn0), ("later", lambda n, j: not _in_layer0(n, j))):
        group = [l for l in layers if want(l[0], l[1])]
        kinds = [_BIG_KIND[n] for n, _, _ in group]
        ops = [(w + done if k == 0 else w).astype(BF16) for k, (_, _, w) in enumerate(group)]
        send_sems, recv_sems, locs, lands, done = gather_start(ops, kinds, name=f"gather_weights_start_{tag}")
        pending[tag] = (group, kinds, send_sems, recv_sems, locs, lands)
    return full, pending, done


def _gather_rest(full, pending, after, tag):
    group, kinds, send_sems, recv_sems, locs, lands = pending
    lands = gather_wait(send_sems, recv_sems, locs, lands, kinds, after, name=f"gather_weights_wait_{tag}")
    lands = gather_finish(lands, kinds, [w.shape for _, _, w in group], name=f"gather_weights_finish_{tag}")
    out = {n: list(full[n]) for n in _BIG}
    for (n, j, _), g in zip(group, lands):
        out[n][j] = _assemble(n, g)
    return out


def _reduce_ops(grads, local, want):
    ops = []
    for n in _BIG:
        kind = _BIG_KIND[n]
        for j, g in enumerate(grads[n]):
            if not want(n, j):
                continue
            loc = local[n].shape[-1] if kind[0] == "cols" or n in _PADDED_IN else g.shape[1]
            if n in _PADDED_IN:
                g = jnp.stack([g[:, s * loc:(s + 1) * loc] for s in range(4)])
            elif kind[0] == "lead":
                g = g.reshape(4, g.shape[0] // 4, g.shape[1])
            ops.append((n, j, kind, loc, g))
    return ops


def _pair_sums(ops, c_arr, tag):
    gots = pair_swap([o[4] for o in ops], [o[2] for o in ops], name=f"reduce_pair_swap_{tag}")
    return [pair_add(o[4], got, c_arr, o[2][0], name=f"reduce_pair_add_{o[0]}{o[1]}") for o, got in zip(ops, gots)]


def _is_ffn0(n, j):
    return j == 0 and n in ("ffn_w_gu", "ffn_w_down")


def _reduce_start(grads, local, c, want, tag):
    ops = _reduce_ops(grads, local, want)
    c_arr = jnp.reshape(c, (1,)).astype(jnp.int32)
    qs = _pair_sums(ops, c_arr, tag)
    send_sems, recv_sems, qs, lands, zero = scatter_start(qs, [o[2] for o in ops], [o[3] for o in ops],
                                                          name=f"reduce_scatter_start_{tag}")
    return (ops, send_sems, recv_sems, qs, lands, tag), zero


def _reduce_swap_start(grads, local, c, want, tag):
    ops = _reduce_ops(grads, local, want)
    send_sems, recv_sems, ps, lands, zero = pair_swap_start([o[4] for o in ops], [o[2] for o in ops],
                                                            name=f"reduce_pair_swap_start_{tag}")
    return (ops, send_sems, recv_sems, ps, lands, tag), zero


def _reduce_scatter_after(pending, after, c):
    ops, send_sems, recv_sems, ps, lands, tag = pending
    ps, gots = pair_swap_wait(send_sems, recv_sems, ps, lands, [o[2] for o in ops], after, name=f"reduce_pair_swap_wait_{tag}")
    c_arr = jnp.reshape(c, (1,)).astype(jnp.int32)
    qs = [pair_add(p, got, c_arr, o[2][0], name=f"reduce_pair_add_{o[0]}{o[1]}") for o, p, got in zip(ops, ps, gots)]
    send_sems, recv_sems, qs, lands, zero = scatter_start(qs, [o[2] for o in ops], [o[3] for o in ops],
                                                          name=f"reduce_scatter_start_{tag}")
    return (ops, send_sems, recv_sems, qs, lands, tag), zero


def _reduce_big(grads, local, pendings, after, x, y, c):
    c_arr = jnp.reshape(c, (1,)).astype(jnp.int32)
    ops, qs, rs = [], [], []
    for ops_p, send_sems, recv_sems, qs_p, lands, tag in pendings:
        qs_p, rs_p = scatter_wait(send_sems, recv_sems, qs_p, lands, [o[2] for o in ops_p], [o[3] for o in ops_p], after,
                                  name=f"reduce_scatter_wait_{tag}")
        ops, qs, rs = ops + ops_p, qs + qs_p, rs + rs_p
    ops_f = _reduce_ops(grads, local, lambda n, j: _in_layer0(n, j) and not _is_ffn0(n, j))
    qs_f = _pair_sums(ops_f, c_arr, "first")
    s_sems, r_sems, qs_f, lands_f, zero = scatter_start(qs_f, [o[2] for o in ops_f], [o[3] for o in ops_f],
                                                        name="reduce_scatter_start_first")
    qs[0] = qs[0] + zero.astype(qs[0].dtype)
    red = _reduce_close(ops, qs, rs, x, y, c_arr, "later")
    done = {n: jnp.stack([red[(n, j)] for j in range(local[n].shape[0])]).reshape(local[n].shape)
            for n in _BIG if all((n, j) in red for j in range(local[n].shape[0]))}
    return done, red, (ops_f, s_sems, r_sems, qs_f, lands_f)


def _reduce_close(ops, qs, rs, x, y, c_arr, tag):
    fs = [chip_add(q, r, jnp.reshape(_pos(x, y, o[2][1]), (1,)).astype(jnp.int32), c_arr, o[2][0], o[3],
                   name=f"reduce_chip_add_{o[0]}{o[1]}") for o, q, r in zip(ops, qs, rs)]
    outs = share_rows(fs, name=f"reduce_share_{tag}")
    return {(o[0], o[1]): r for o, r in zip(ops, outs)}


def _reduce_big_first(pending, red, after, local, x, y, c):
    ops_f, s_sems, r_sems, qs_f, lands_f = pending
    qs_f, rs_f = scatter_wait(s_sems, r_sems, qs_f, lands_f, [o[2] for o in ops_f], [o[3] for o in ops_f], after,
                              name="reduce_scatter_wait_first")
    red = {**red, **_reduce_close(ops_f, qs_f, rs_f, x, y, jnp.reshape(c, (1,)).astype(jnp.int32), "first")}
    names = sorted({o[0] for o in ops_f})
    return {n: jnp.stack([red[(n, j)] for j in range(local[n].shape[0])]).reshape(local[n].shape) for n in names}


def _reduce_small_start(grads):
    names = _REPLICATED + _SMALL_SHARDED
    flat = jnp.concatenate([grads[n].astype(F32).reshape(-1) for n in names])
    n_el = flat.shape[0]
    rows = -(-n_el // (LANE * 8)) * 8
    v = jnp.pad(flat, (0, rows * LANE - n_el)).reshape(rows, LANE)
    outs = gather_all_start(v, name="reduce_small_start")
    return outs[:4], outs[4]


def _reduce_small(pending, after, grads, local, x, y, c):
    names = _REPLICATED + _SMALL_SHARDED
    send_sems, recv_sems, v, land = pending
    land = gather_all_wait(send_sems, recv_sems, v, land, after, name="reduce_small_wait")
    me = jnp.reshape(4 * x + 2 * y + c, (1,)).astype(jnp.int32)
    red = sum_slots(land, v, me, name="reduce_small_add").reshape(-1)
    out, off = {}, 0
    for n in names:
        sz = math.prod(grads[n].shape)
        g = red[off:off + sz].reshape(grads[n].shape)
        off += sz
        if n in _SHARD_AXIS:
            ax = _SHARD_AXIS[n]
            loc = local[n].shape[ax]
            g = lax.dynamic_slice_in_dim(g, (2 * x + y) * loc, loc, axis=ax)
        out[n] = g
    return out


def kernel(x, norm_mix_g, norm_ffn_g, gla_w_in, gla_w_a2, gla_b_a, gla_norm_g, gla_w_out, ssd_w_in, ssd_conv_w, ssd_conv_b, ssd_dt_bias, ssd_a_log, ssd_d, ssd_norm_g, ssd_w_out, s5_log_dt, s5_a_re, s5_a_im, s5_b_re, s5_b_im, s5_c_re, s5_c_im, s5_d, s5_w_glu, ffn_w_gu, ffn_w_down, final_norm_g, loss_target, m_norm_mix_g, m_norm_ffn_g, m_gla_w_in, m_gla_w_a2, m_gla_b_a, m_gla_norm_g, m_gla_w_out, m_ssd_w_in, m_ssd_conv_w, m_ssd_conv_b, m_ssd_dt_bias, m_ssd_a_log, m_ssd_d, m_ssd_norm_g, m_ssd_w_out, m_s5_log_dt, m_s5_a_re, m_s5_a_im, m_s5_b_re, m_s5_b_im, m_s5_c_re, m_s5_c_im, m_s5_d, m_s5_w_glu, m_ffn_w_gu, m_ffn_w_down, m_final_norm_g, v_norm_mix_g, v_norm_ffn_g, v_gla_w_in, v_gla_w_a2, v_gla_b_a, v_gla_norm_g, v_gla_w_out, v_ssd_w_in, v_ssd_conv_w, v_ssd_conv_b, v_ssd_dt_bias, v_ssd_a_log, v_ssd_d, v_ssd_norm_g, v_ssd_w_out, v_s5_log_dt, v_s5_a_re, v_s5_a_im, v_s5_b_re, v_s5_b_im, v_s5_c_re, v_s5_c_im, v_s5_d, v_s5_w_glu, v_ffn_w_gu, v_ffn_w_down, v_final_norm_g):
    given = dict(locals())
    local = {n: given[n] for n in _WEIGHTS}
    px, py, pc = _place()

    first, gathering, zero = _gather_first(local)
    full = dict(local)
    full.update(first)
    full["norm_mix_g"] = local["norm_mix_g"] + zero
    big = [first]

    def weights_of(tag):
        def arrived(h):
            big.append(_gather_rest(big[-1], gathering[tag], h, tag))
            return big[-1]
        return arrived

    swapping, reducing = [], []

    def later_grads(g):
        pending, zero = _reduce_swap_start(g, local, pc, lambda n, j: not _in_layer0(n, j), "later")
        swapping.append(pending)
        return zero

    def ffn0_grads(g):
        pending, zero = _reduce_scatter_after(swapping[0], g["ffn_w_down"][0], pc)
        reducing.append(pending)
        g["ffn_w_down"][0] = g["ffn_w_down"][0] + zero
        pending, zero = _reduce_start(g, local, pc, _is_ffn0, "ffn0")
        reducing.append(pending)
        return zero

    loss, grad_x, grads = local_step(x[0], loss_target[0], full, weights_of("later"), later_grads, ffn0_grads, weights_of("ffn0"))
    loss = lax.psum(loss, ("x", "y", "c"))

    small, zero = _reduce_small_start(grads)
    grads["gla_w_out"][0] = grads["gla_w_out"][0] + zero
    red, parts, first_pending = _reduce_big(grads, local, reducing, grad_x, px, py, pc)

    deltas, new_m, new_v = {}, {}, {}

    def update(n):
        deltas[n], new_m[n], new_v[n] = adamw(local[n], red[n], given["m_" + n], given["v_" + n], name=f"adamw_{n}")

    for n in list(red):
        update(n)
    red.update(_reduce_big_first(first_pending, parts, deltas["ffn_w_gu"], local, px, py, pc))
    red.update(_reduce_small(small, red["gla_w_out"], grads, local, px, py, pc))
    for n in _WEIGHTS:
        if n not in deltas:
            update(n)
    return (loss, grad_x[None], *[red[n] for n in _WEIGHTS], *[deltas[n] for n in _WEIGHTS],
            *[new_m[n] for n in _WEIGHTS], *[new_v[n] for n in _WEIGHTS])
```

```python
import functools
import math

import jax
import jax.numpy as jnp
from jax import lax
from jax.experimental import pallas as pl
from jax.experimental.pallas import tpu as pltpu

F32 = jnp.float32
BF16 = jnp.bfloat16

D = 1024
DEPTH = 4
CH = 64
EPS = 1e-6
GLA_H, GLA_DK, GLA_DV, GLA_RANK, GLA_TAU = 4, 128, 256, 16, 16.0
GLA_QK = GLA_H * GLA_DK
GLA_VD = GLA_H * GLA_DV
GLA_IN = 2 * GLA_QK + 2 * GLA_VD + GLA_RANK
GLA_INP = 3200
SSD_DI, SSD_HD, SSD_H, SSD_G, SSD_N, SSD_K = 2048, 64, 32, 8, 128, 4
SSD_GN = SSD_G * SSD_N
SSD_CONV = SSD_DI + 2 * SSD_GN
SSD_IN = SSD_DI + SSD_CONV + SSD_H
SSD_INP = 6272
S5_GS, S5_NG, S5_P = 16, 64, 64
S5_BLK = 8
FFN_H = 2816
LANE = 128
VMEM_LIMIT = 52 * 1024 * 1024
_MATMUL_VMEM = 40 * 1024 * 1024

ADAM_LR, ADAM_B1, ADAM_B2, ADAM_EPS, ADAM_WD, ADAM_STEP = 0.001, 0.9, 0.999, 1e-08, 0.01, 10

_ARB = "arbitrary"


def _cparams(n):
    return pltpu.CompilerParams(dimension_semantics=(_ARB,) * n, vmem_limit_bytes=VMEM_LIMIT)


def _pick(n, target, mult=LANE):
    best = None
    for c in range(mult, min(n, target) + 1, mult):
        if n % c == 0:
            best = c
    return best if best is not None else n


_DN = {"nn": (((1,), (0,)), ((), ())), "nt": (((1,), (1,)), ((), ())), "tn": (((0,), (0,)), ((), ()))}


def _dot_raw(a, b, form):
    return lax.dot_general(a.astype(BF16), b.astype(BF16), _DN[form], preferred_element_type=F32)


@functools.partial(jax.custom_vjp, nondiff_argnums=(2,))
def bdot(a, b, form):
    return _dot_raw(a, b, form)


def _bdot_fwd(a, b, form):
    return _dot_raw(a, b, form), (a, b)


def _bdot_bwd(form, res, g):
    a, b = res
    if form == "nn":
        return _dot_raw(g, b, "nt"), _dot_raw(a, g, "tn")
    if form == "nt":
        return _dot_raw(g, b, "nn"), _dot_raw(g, a, "tn")
    return _dot_raw(b, g, "nt"), _dot_raw(a, g, "nn")


bdot.defvjp(_bdot_fwd, _bdot_bwd)


def _hdot(a, b):
    return jnp.dot(a, b, precision=lax.Precision.HIGHEST, preferred_element_type=F32)


@jax.custom_vjp
def cdot_left(c, ct, x):
    return _hdot(c, x)


def _cdl_fwd(c, ct, x):
    return _hdot(c, x), (c, ct)


def _cdl_bwd(res, g):
    c, ct = res
    return jnp.zeros_like(c), jnp.zeros_like(ct), _hdot(ct, g)


cdot_left.defvjp(_cdl_fwd, _cdl_bwd)


@jax.custom_vjp
def cdot_right(x, c, ct):
    return _hdot(x, c)


def _cdr_fwd(x, c, ct):
    return _hdot(x, c), (c, ct)


def _cdr_bwd(res, g):
    c, ct = res
    return _hdot(g, ct), jnp.zeros_like(c), jnp.zeros_like(ct)


cdot_right.defvjp(_cdr_fwd, _cdr_bwd)


def _sigmoid(x):
    return 1.0 / (1.0 + jnp.exp(-x))


def _silu(x):
    return x * _sigmoid(x)


def _softplus(x):
    return jnp.maximum(x, 0.0) + jnp.log(1.0 + jnp.exp(-jnp.abs(x)))


def _log_sigmoid(x):
    return jnp.minimum(x, 0.0) - jnp.log(1.0 + jnp.exp(-jnp.abs(x)))


def _gelu(x):
    c = math.sqrt(2.0 / math.pi)
    return 0.5 * x * (1.0 + jnp.tanh(c * (x + 0.044715 * (x * x * x))))


def _rms(x, g):
    return x * lax.rsqrt(jnp.mean(x * x, axis=-1, keepdims=True) + EPS) * g


def _iota2(shape, axis):
    return lax.broadcasted_iota(jnp.int32, shape, axis)


def matmul(a, b, form, *, name, G=1, out_dtype=F32, add=None):
    isz = lambda t: jnp.dtype(t.dtype).itemsize
    osz = jnp.dtype(out_dtype).itemsize + (isz(add) if add is not None else 0)

    def fits(bm, bn, bk):
        return 2 * (bm * bk * isz(a) + bk * bn * isz(b) + bm * bn * osz) + 4 * bm * bn <= _MATMUL_VMEM

    if form in ("nn", "nt"):
        M = a.shape[0]
        K = a.shape[1] // G
        N = b.shape[2] if form == "nn" else b.shape[1]
        bm, bn, bk = min(M, 1024), _pick(N, 1536), _pick(K, 2048)
        while not fits(bm, bn, bk) and bk % 256 == 0:
            bk //= 2
        nj, nk = N // bn, K // bk
        grid = (G, M // bm, nj, nk)
        a_spec = pl.BlockSpec((bm, bk), lambda g, i, j, k: (i, g * nk + k))
        if form == "nn":
            b_spec = pl.BlockSpec((None, bk, bn), lambda g, i, j, k: (g, k, j))
        else:
            b_spec = pl.BlockSpec((None, bn, bk), lambda g, i, j, k: (g, j, k))
        o_spec = pl.BlockSpec((bm, bn), lambda g, i, j, k: (i, g * nj + j))
        out_shape = jax.ShapeDtypeStruct((M, G * N), out_dtype)
    else:
        T = a.shape[0]
        Ka, Nb = a.shape[1] // G, b.shape[1] // G
        bm, bn, bk = _pick(Ka, 1408), _pick(Nb, 1536), min(T, 2048)
        while not fits(bm, bn, bk) and bk % 512 == 0:
            bk //= 2
        ni, nj, nk = Ka // bm, Nb // bn, T // bk
        grid = (G, ni, nj, nk)
        a_spec = pl.BlockSpec((bk, bm), lambda g, i, j, k: (k, g * ni + i))
        b_spec = pl.BlockSpec((bk, bn), lambda g, i, j, k: (k, g * nj + j))
        o_spec = pl.BlockSpec((None, bm, bn), lambda g, i, j, k: (g, i, j))
        out_shape = jax.ShapeDtypeStruct((G, Ka, Nb), out_dtype)
    has_add = add is not None

    def finish(refs, r):
        if has_add:
            r = r + refs[2][...].astype(F32)
        o_ref = refs[3] if has_add else refs[2]
        o_ref[...] = r.astype(o_ref.dtype)

    def body_one(*refs):
        finish(refs, _dot_raw(refs[0][...], refs[1][...], form))

    def body_acc(*refs):
        acc_ref = refs[-1]
        k = pl.program_id(3)

        @pl.when(k == 0)
        def _():
            acc_ref[...] = jnp.zeros_like(acc_ref)

        acc_ref[...] += _dot_raw(refs[0][...], refs[1][...], form)

        @pl.when(k == nk - 1)
        def _():
            finish(refs, acc_ref[...])

    in_specs = [a_spec, b_spec]
    args = [a, b]
    if has_add:
        in_specs.append(o_spec)
        args.append(add)
    return pl.pallas_call(
        body_one if nk == 1 else body_acc, name=name, grid=grid, in_specs=in_specs, out_specs=o_spec,
        out_shape=out_shape, scratch_shapes=[] if nk == 1 else [pltpu.VMEM((bm, bn), F32)],
        compiler_params=_cparams(4),
    )(*args)


def matmul_nt_norm_bwd(a, w, h, g, d, *, name):
    T, K = a.shape
    bm = min(T, 512)
    bk = _pick(K, 2048)
    nk = K // bk

    def body(a_ref, w_ref, h_ref, g_ref, d_ref, dh_ref, dg_ref, acc_ref):
        i, k = pl.program_id(0), pl.program_id(1)

        @pl.when((i == 0) & (k == 0))
        def _():
            dg_ref[...] = jnp.zeros_like(dg_ref)

        @pl.when(k == 0)
        def _():
            acc_ref[...] = jnp.zeros_like(acc_ref)

        acc_ref[...] += _dot_raw(a_ref[...], w_ref[...], "nt")

        @pl.when(k == nk - 1)
        def _():
            _, vjp = jax.vjp(lambda g_, h_: _rms(h_, g_), g_ref[...], h_ref[...])
            dg, dh = vjp(acc_ref[...])
            dh_ref[...] = dh + d_ref[...]
            dg_ref[...] += dg

    row = pl.BlockSpec((bm, D), lambda i, k: (i, 0))
    one = pl.BlockSpec((1, D), lambda i, k: (0, 0))
    return pl.pallas_call(
        body, name=name, grid=(T // bm, nk),
        in_specs=[pl.BlockSpec((bm, bk), lambda i, k: (i, k)), pl.BlockSpec((D, bk), lambda i, k: (0, k)), row, one, row],
        out_specs=[row, one], out_shape=[jax.ShapeDtypeStruct((T, D), F32), jax.ShapeDtypeStruct((1, D), F32)],
        scratch_shapes=[pltpu.VMEM((bm, D), F32)], compiler_params=_cparams(2),
    )(a, w, h, g, d)


def ffn_up(hn, w_il, *, name):
    T = hn.shape[0]
    bm, hb = min(T, 512), FFN_H // 2

    def body(a_ref, b_ref, act_ref, gu_ref):
        r = _dot_raw(a_ref[...], b_ref[...], "nn")
        act_ref[...] = (_silu(r[:, :hb]) * r[:, hb:]).astype(act_ref.dtype)
        gu_ref[...] = r.astype(gu_ref.dtype)

    return pl.pallas_call(
        body, name=name, grid=(2, T // bm),
        in_specs=[pl.BlockSpec((bm, D), lambda j, i: (i, 0)), pl.BlockSpec((D, 2 * hb), lambda j, i: (0, j))],
        out_specs=[pl.BlockSpec((bm, hb), lambda j, i: (i, j)), pl.BlockSpec((bm, 2 * hb), lambda j, i: (i, j))],
        out_shape=[jax.ShapeDtypeStruct((T, FFN_H), BF16), jax.ShapeDtypeStruct((T, 2 * FFN_H), BF16)],
        compiler_params=_cparams(2),
    )(hn, w_il)


_DACT_CHUNK = 512


def ffn_dact(d, w_down, gu, *, name):
    T = d.shape[0]
    bm, hb = min(T, 512), FFN_H // 2

    def body(d_ref, w_ref, gu_ref, o_ref):
        d_blk = d_ref[...].astype(BF16)
        for lo in range(0, hb, _DACT_CHUNK):
            hi = min(lo + _DACT_CHUNK, hb)
            da = _dot_raw(d_blk, w_ref[lo:hi, :], "nt")
            g, u = gu_ref[:, lo:hi].astype(F32), gu_ref[:, hb + lo:hb + hi].astype(F32)
            sg = _sigmoid(g)
            o_ref[:, lo:hi] = (da * u * (sg * (1.0 + g * (1.0 - sg)))).astype(o_ref.dtype)
            o_ref[:, hb + lo:hb + hi] = (da * (g * sg)).astype(o_ref.dtype)

    return pl.pallas_call(
        body, name=name, grid=(2, T // bm),
        in_specs=[pl.BlockSpec((bm, D), lambda j, i: (i, 0)), pl.BlockSpec((hb, D), lambda j, i: (j, 0)),
                  pl.BlockSpec((bm, 2 * hb), lambda j, i: (i, j))],
        out_specs=pl.BlockSpec((bm, 2 * hb), lambda j, i: (i, j)),
        out_shape=jax.ShapeDtypeStruct((T, 2 * FFN_H), BF16), compiler_params=_cparams(2),
    )(d, w_down, gu)


def _row_entry(e):
    return e if isinstance(e, tuple) else (e, e.shape[1], 0)


def _row_spec(bt, e):
    _, width, idx = e
    return pl.BlockSpec((bt, width), lambda i: (i, idx))


def _full_spec(p):
    return pl.BlockSpec(p.shape, lambda i: (0,) * p.ndim)


def rowwise(f, params, rows, outs, *, bt, name):
    rows = [_row_entry(e) for e in rows]
    T = rows[0][0].shape[0]
    bt = min(bt, T)
    np_, nr = len(params), len(rows)

    def body(*refs):
        p = tuple(r[...].astype(F32) for r in refs[:np_])
        rw = tuple(r[...].astype(F32) for r in refs[np_:np_ + nr])
        res = f(p, rw)
        for o_ref, o in zip(refs[np_ + nr:], res):
            o_ref[...] = o.astype(o_ref.dtype)

    res = pl.pallas_call(
        body, name=name, grid=(T // bt,),
        in_specs=[_full_spec(p) for p in params] + [_row_spec(bt, e) for e in rows],
        out_specs=[pl.BlockSpec((bt, w), lambda i: (i, 0)) for w, _ in outs],
        out_shape=[jax.ShapeDtypeStruct((T, w), dt) for w, dt in outs],
        compiler_params=_cparams(1),
    )(*params, *[e[0] for e in rows])
    return list(res)


def rowwise_vjp(f, params, rows, cts, drow_dtypes, *, bt, name):
    rows = [_row_entry(e) for e in rows]
    cts = [_row_entry(e) for e in cts]
    T = rows[0][0].shape[0]
    bt = min(bt, T)
    np_, nr, nc = len(params), len(rows), len(cts)
    want = [i for i, dt in enumerate(drow_dtypes) if dt is not None]

    def body(*refs):
        p = tuple(r[...].astype(F32) for r in refs[:np_])
        rw = tuple(r[...].astype(F32) for r in refs[np_:np_ + nr])
        ct = tuple(r[...].astype(F32) for r in refs[np_ + nr:np_ + nr + nc])
        outs = refs[np_ + nr + nc:]
        _, vjp = jax.vjp(f, p, rw)
        dp, dr = vjp(ct)

        @pl.when(pl.program_id(0) == 0)
        def _():
            for o in outs[:np_]:
                o[...] = jnp.zeros_like(o)

        for o, d in zip(outs[:np_], dp):
            o[...] += d
        for o, i in zip(outs[np_:], want):
            o[...] = dr[i].astype(o.dtype)

    res = pl.pallas_call(
        body, name=name, grid=(T // bt,),
        in_specs=[_full_spec(p) for p in params] + [_row_spec(bt, e) for e in rows] + [_row_spec(bt, e) for e in cts],
        out_specs=[_full_spec(p) for p in params] + [pl.BlockSpec((bt, rows[i][1]), lambda i_: (i_, 0)) for i in want],
        out_shape=[jax.ShapeDtypeStruct(p.shape, F32) for p in params]
        + [jax.ShapeDtypeStruct((T, rows[i][1]), drow_dtypes[i]) for i in want],
        compiler_params=_cparams(1),
    )(*params, *[e[0] for e in rows], *[e[0] for e in cts])
    res = list(res)
    return res[:np_], res[np_:]


def f_rmsnorm(p, r):
    return (_rms(r[0], p[0]),)


def f_gla_gate_in(p, r):
    w_a2, b_a = p
    z = bdot(r[0], w_a2, "nn") + b_a
    return (_log_sigmoid(z) / GLA_TAU,)


def f_gla_gate_in_fwd(p, r):
    w_a2, b_a = p
    z = _dot_raw(r[0], w_a2, "nn") + b_a
    return (_log_sigmoid(z) / GLA_TAU,)


def f_s5_act(p, r):
    (dsk,) = p
    ycp, u = r
    return (_gelu(ycp + dsk * u),)


def f_glu_res(p, r):
    vg, h = r
    return (vg[:, :D] * _sigmoid(vg[:, D:]) + h,)


def f_glu(p, r):
    vg = r[0]
    return (vg[:, :D] * _sigmoid(vg[:, D:]),)


def loss_head(h, g, target, *, name):
    T = h.shape[0]
    bt = min(T, 256)

    def lossf(g_, h_, t_):
        e = _rms(h_, g_) - t_
        return (0.5 / D) * jnp.sum(e * e)

    def body(g_ref, h_ref, t_ref, loss_ref, dg_ref, dh_ref):
        @pl.when(pl.program_id(0) == 0)
        def _():
            loss_ref[...] = jnp.zeros_like(loss_ref)
            dg_ref[...] = jnp.zeros_like(dg_ref)

        val, vjp = jax.vjp(lossf, g_ref[...], h_ref[...], t_ref[...])
        dg, dh, _ = vjp(jnp.ones((), F32))
        loss_ref[...] += jnp.full(loss_ref.shape, val, F32)
        dg_ref[...] += dg
        dh_ref[...] = dh

    row = pl.BlockSpec((bt, D), lambda i: (i, 0))
    one = pl.BlockSpec((1, D), lambda i: (0, 0))
    loss, dg, dh = pl.pallas_call(
        body, name=name, grid=(T // bt,), in_specs=[one, row, row],
        out_specs=[pl.BlockSpec((1, LANE), lambda i: (0, 0)), one, row],
        out_shape=[jax.ShapeDtypeStruct((1, LANE), F32), jax.ShapeDtypeStruct((1, D), F32),
                   jax.ShapeDtypeStruct((T, D), F32)],
        compiler_params=_cparams(1),
    )(g, h, target)
    return loss[0, 0], dg, dh


def _gla_consts():
    r, c = _iota2((CH, CH), 0), _iota2((CH, CH), 1)
    return (r >= c).astype(F32), (r <= c).astype(F32), r >= c


def _gla_chunk(q, k, v, la, st, consts, dot, cdl):
    L, Lt, tril = consts
    lc = cdl(L, Lt, la)
    lend = lc[CH - 1:CH, :]
    e, ei = jnp.exp(lc), jnp.exp(-lc)
    qs = q * (GLA_DK ** -0.5)
    qf, kf, qb, kb = qs * e, k * ei, qs * ei, k * e
    sc = jnp.where(tril, dot(qf, kf, "nt"), dot(qb, kb, "nt"))
    o = dot(sc, v, "nn") + dot(qf, st, "nt")
    kd = k * jnp.exp(lend - lc)
    st_new = st * jnp.exp(lend) + dot(v, kd, "tn")
    return o, st_new


def _gla_block(q, k, v, la, st, nc, dot, cdl):
    consts = _gla_consts()
    outs = []
    for c in range(nc):
        sl = slice(c * CH, (c + 1) * CH)
        o, st = _gla_chunk(q[sl], k[sl], v[sl], la[sl], st, consts, dot, cdl)
        outs.append(o)
    return jnp.concatenate(outs, axis=0), st


_GLA_HP = 2


def _gla_specs(rows, rev, nb):
    t = (lambda j: nb - 1 - j) if rev else (lambda j: j)
    hp, ng = _GLA_HP, GLA_H // _GLA_HP
    q = pl.BlockSpec((rows, hp * GLA_DK), lambda h, j: (t(j), h))
    k = pl.BlockSpec((rows, hp * GLA_DK), lambda h, j: (t(j), ng + h))
    v = pl.BlockSpec((rows, hp * GLA_DV), lambda h, j: (t(j), ng + h))
    la = pl.BlockSpec((rows, hp * GLA_DK), lambda h, j: (t(j), h))
    ss = pl.BlockSpec((None, hp, GLA_DV, GLA_DK), lambda h, j: (t(j), h, 0, 0))
    o = pl.BlockSpec((rows, hp * GLA_DV), lambda h, j: (t(j), h))
    r = pl.BlockSpec((rows, hp * GLA_DV), lambda h, j: (t(j), 2 * ng + h))
    g = pl.BlockSpec((1, hp * GLA_DV), lambda h, j: (0, h))
    return q, k, v, la, ss, o, r, g


def _gla_heads(q, k, v, la, r, ng, sts, nc, dot, cdl):
    outs, new = [], []
    for i in range(_GLA_HP):
        kk, vv = slice(i * GLA_DK, (i + 1) * GLA_DK), slice(i * GLA_DV, (i + 1) * GLA_DV)
        o, st = _gla_block(q[:, kk], k[:, kk], v[:, vv], la[:, kk], sts[i], nc, dot, cdl)
        outs.append(_rms(o, ng[:, vv]) * _silu(r[:, vv]))
        new.append(st)
    return jnp.concatenate(outs, axis=1), tuple(new)


def gla_scan_fwd(proj, la, ng, *, nc, name):
    T = proj.shape[0]
    rows = min(T, nc * CH)
    nc = rows // CH
    per = 2 if T % (2 * rows) == 0 else 1
    nb = T // (per * rows)
    q_s, k_s, v_s, la_s, _, o_s, r_s, g_s = _gla_specs(per * rows, False, nb)
    ss_s = pl.BlockSpec((per, _GLA_HP, GLA_DV, GLA_DK), lambda h, j: (j, h, 0, 0))

    def body(q_ref, k_ref, v_ref, la_ref, r_ref, g_ref, o_ref, ss_ref, st_ref):
        @pl.when(pl.program_id(1) == 0)
        def _():
            st_ref[...] = jnp.zeros_like(st_ref)

        sts = tuple(st_ref[i] for i in range(_GLA_HP))
        for p in range(per):
            sl = slice(p * rows, (p + 1) * rows)
            for i in range(_GLA_HP):
                ss_ref[p, i] = sts[i]
            o, sts = _gla_heads(q_ref[sl, :], k_ref[sl, :], v_ref[sl, :], la_ref[sl, :], r_ref[sl, :], g_ref[...], sts, nc,
                                _dot_raw, lambda c, ct, x: _hdot(c, x))
            o_ref[sl, :] = o.astype(o_ref.dtype)
        for i in range(_GLA_HP):
            st_ref[i] = sts[i]

    return pl.pallas_call(
        body, name=name, grid=(GLA_H // _GLA_HP, nb), in_specs=[q_s, k_s, v_s, la_s, r_s, g_s], out_specs=[o_s, ss_s],
        out_shape=[jax.ShapeDtypeStruct((T, GLA_VD), BF16), jax.ShapeDtypeStruct((per * nb, GLA_H, GLA_DV, GLA_DK), F32)],
        scratch_shapes=[pltpu.VMEM((_GLA_HP, GLA_DV, GLA_DK), F32)], compiler_params=_cparams(2),
    )(proj, proj, proj, la, proj, ng)


def gla_scan_bwd(proj, la, ng, ss, do, *, nc, name):
    T = proj.shape[0]
    rows = min(T, nc * CH)
    nc = rows // CH
    nb = T // rows
    q_s, k_s, v_s, la_s, ss_s, o_s, r_s, g_s = _gla_specs(rows, True, nb)
    t = lambda j: nb - 1 - j
    dqk_s = pl.BlockSpec((rows, _GLA_HP * GLA_DK), lambda h, j: (t(j), h))

    def body(q_ref, k_ref, v_ref, la_ref, r_ref, g_ref, ss_ref, do_ref,
             dq_ref, dk_ref, dv_ref, dla_ref, dr_ref, dg_ref, dst_ref):
        @pl.when(pl.program_id(1) == 0)
        def _():
            dst_ref[...] = jnp.zeros_like(dst_ref)
            dg_ref[...] = jnp.zeros_like(dg_ref)

        fn = lambda q, k, v, la_, r, g, *sts: _gla_heads(q, k, v, la_, r, g, sts, nc, bdot, cdot_left)
        _, vjp = jax.vjp(fn, q_ref[...], k_ref[...], v_ref[...], la_ref[...], r_ref[...], g_ref[...],
                         *[ss_ref[i] for i in range(_GLA_HP)])
        dq, dk, dv, dla, dr, dg, *dsts = vjp((do_ref[...], tuple(dst_ref[i] for i in range(_GLA_HP))))
        dq_ref[...] = dq.astype(dq_ref.dtype)
        dk_ref[...] = dk.astype(dk_ref.dtype)
        dv_ref[...] = dv.astype(dv_ref.dtype)
        dla_ref[...] = dla
        dr_ref[...] = dr.astype(dr_ref.dtype)
        dg_ref[...] += dg
        for i in range(_GLA_HP):
            dst_ref[i] = dsts[i]

    return pl.pallas_call(
        body, name=name, grid=(GLA_H // _GLA_HP, nb), in_specs=[q_s, k_s, v_s, la_s, r_s, g_s, ss_s, o_s],
        out_specs=[dqk_s, dqk_s, o_s, dqk_s, o_s, g_s],
        out_shape=[jax.ShapeDtypeStruct((T, GLA_QK), BF16), jax.ShapeDtypeStruct((T, GLA_QK), BF16),
                   jax.ShapeDtypeStruct((T, GLA_VD), BF16), jax.ShapeDtypeStruct((T, GLA_QK), F32),
                   jax.ShapeDtypeStruct((T, GLA_VD), BF16), jax.ShapeDtypeStruct((1, GLA_VD), F32)],
        scratch_shapes=[pltpu.VMEM((_GLA_HP, GLA_DV, GLA_DK), F32)], compiler_params=_cparams(2),
    )(proj, proj, proj, la, proj, ng, ss, do)


_CONV_W = 512
_CONV_OFF = SSD_DI // _CONV_W


def _conv_pre(x, prev8, w_ref, b_ref):
    bt = x.shape[0]
    ext = jnp.concatenate([prev8, x], axis=0)
    shifted = []
    for j in range(SSD_K):
        s = SSD_K - 1 - j
        shifted.append(x if s == 0 else pltpu.roll(ext, s, 0)[8:8 + bt])
    pre = b_ref[...] + sum(w_ref[j:j + 1, :] * shifted[j] for j in range(SSD_K))
    return pre, shifted


def ssd_conv_fwd(proj, w, b, *, name):
    T = proj.shape[0]
    bt = min(T, 512)
    nb = T // bt

    def body(x_ref, w_ref, b_ref, o_ref, carry_ref):
        @pl.when(pl.program_id(1) == 0)
        def _():
            carry_ref[...] = jnp.zeros_like(carry_ref)

        x = x_ref[...]
        pre, _ = _conv_pre(x, carry_ref[...], w_ref, b_ref)
        o_ref[...] = _silu(pre)
        carry_ref[...] = x[bt - 8:, :]

    return pl.pallas_call(
        body, name=name, grid=(SSD_CONV // _CONV_W, nb),
        in_specs=[pl.BlockSpec((bt, _CONV_W), lambda c, t: (t, _CONV_OFF + c)),
                  pl.BlockSpec((SSD_K, _CONV_W), lambda c, t: (0, c)),
                  pl.BlockSpec((1, _CONV_W), lambda c, t: (0, c))],
        out_specs=pl.BlockSpec((bt, _CONV_W), lambda c, t: (t, c)),
        out_shape=jax.ShapeDtypeStruct((T, SSD_CONV), F32),
        scratch_shapes=[pltpu.VMEM((8, _CONV_W), F32)], compiler_params=_cparams(2),
    )(proj, w, b)


def ssd_conv_bwd(proj, w, b, dout, col0, *, name):
    T, ncols = dout.shape
    bt = min(T, 512)
    nb = T // bt
    r8 = bt // 8
    c0 = col0 // _CONV_W

    def body(x_ref, xp_ref, w_ref, b_ref, do_ref, dx_ref, dw_ref, db_ref, carry_ref):
        t = pl.program_id(1)

        @pl.when(t == 0)
        def _():
            carry_ref[...] = jnp.zeros_like(carry_ref)
            dw_ref[...] = jnp.zeros_like(dw_ref)
            db_ref[...] = jnp.zeros_like(db_ref)

        x = x_ref[...]
        prev8 = jnp.where(t == nb - 1, 0.0, xp_ref[...])
        pre, shifted = _conv_pre(x, prev8, w_ref, b_ref)
        sg = _sigmoid(pre)
        dpre = do_ref[...] * (sg * (1.0 + pre * (1.0 - sg)))
        ext = jnp.concatenate([dpre, carry_ref[...]], axis=0)
        dx = w_ref[SSD_K - 1:SSD_K, :] * dpre
        for j in range(SSD_K - 1):
            s = SSD_K - 1 - j
            dx = dx + w_ref[j:j + 1, :] * pltpu.roll(ext, bt + 8 - s, 0)[:bt]
        dx_ref[...] = dx.astype(dx_ref.dtype)
        dw_ref[...] += jnp.concatenate([jnp.sum(dpre * shifted[j], axis=0, keepdims=True) for j in range(SSD_K)], axis=0)
        db_ref[...] += jnp.sum(dpre, axis=0, keepdims=True)
        carry_ref[...] = dpre[:8, :]

    rt = lambda t: nb - 1 - t
    return pl.pallas_call(
        body, name=name, grid=(ncols // _CONV_W, nb),
        in_specs=[pl.BlockSpec((bt, _CONV_W), lambda c, t: (rt(t), _CONV_OFF + c0 + c)),
                  pl.BlockSpec((8, _CONV_W), lambda c, t: (jnp.maximum(rt(t) * r8 - 1, 0), _CONV_OFF + c0 + c)),
                  pl.BlockSpec((SSD_K, _CONV_W), lambda c, t: (0, c0 + c)),
                  pl.BlockSpec((1, _CONV_W), lambda c, t: (0, c0 + c)),
                  pl.BlockSpec((bt, _CONV_W), lambda c, t: (rt(t), c))],
        out_specs=[pl.BlockSpec((bt, _CONV_W), lambda c, t: (rt(t), c)),
                   pl.BlockSpec((SSD_K, _CONV_W), lambda c, t: (0, c)),
                   pl.BlockSpec((1, _CONV_W), lambda c, t: (0, c))],
        out_shape=[jax.ShapeDtypeStruct((T, ncols), BF16), jax.ShapeDtypeStruct((SSD_K, ncols), F32),
                   jax.ShapeDtypeStruct((1, ncols), F32)],
        scratch_shapes=[pltpu.VMEM((8, _CONV_W), F32)], compiler_params=_cparams(2),
    )(proj, proj, w, b, dout)


_SSD_U = 2 * CH


def _ssd_unit(xs, bm, cm, dtraw, dtb, alog, dsk, hp, g, dot, cdl, cdr):
    U, P2 = _SSD_U, 2 * SSD_HD
    r, c = _iota2((U, U), 0), _iota2((U, U), 1)
    same = (r // CH) == (c // CH)
    Lb = (same & (r >= c)).astype(F32)
    Ub = (same & (r <= c)).astype(F32)
    lane = _iota2((1, U), 1)
    lo_lane = _iota2((1, P2), 1) < SSD_HD
    lo_sub = _iota2((P2, 1), 0) < SSD_HD
    diag2 = (_iota2((CH, P2), 0) == (_iota2((CH, P2), 1) % CH)).astype(F32)

    dt = _softplus(dtraw + dtb)
    da = dt * (-jnp.exp(alog))
    cum = cdl(Lb, Ub, da)
    ys = []
    new_hp = []
    for pr in range(2):
        xs_p = xs[:, pr * P2:(pr + 1) * P2]
        cols, dts, dks = [], [], []
        for jj in range(2):
            oh_l = (lane == g * (SSD_H // SSD_G) + 2 * pr + jj).astype(F32)
            cols.append(jnp.sum(cum * oh_l, axis=1, keepdims=True))
            dts.append(jnp.sum(dt * oh_l, axis=1, keepdims=True))
            dks.append(jnp.sum(dsk * oh_l, axis=1, keepdims=True))
        dsk_p = jnp.where(lo_lane, dks[0], dks[1])
        h = hp[pr]
        yc = []
        for ci in range(2):
            sl = slice(ci * CH, (ci + 1) * CH)
            xs_c, bm_c, cm_c = xs_p[sl], bm[sl], cm[sl]
            col = jnp.where(lo_lane, cols[0][sl], cols[1][sl])
            dtc = jnp.where(lo_lane, dts[0][sl], dts[1][sl])
            row = jnp.sum(diag2 * col, axis=0, keepdims=True)
            dtrow = jnp.sum(diag2 * dtc, axis=0, keepdims=True)
            cb = dot(cm_c, jnp.concatenate([bm_c, bm_c], axis=0), "nt")
            mix = cb * jnp.exp(-jnp.abs(col - row)) * dtrow
            xbd = jnp.concatenate([jnp.where(lo_lane, xs_c, 0.0), jnp.where(lo_lane, 0.0, xs_c)], axis=0)
            y_intra = dot(mix, xbd, "nn")
            ce = jnp.where(lo_lane, cols[0][ci * CH + CH - 1:ci * CH + CH, :], cols[1][ci * CH + CH - 1:ci * CH + CH, :])
            y_inter = dot(cm_c, h, "nt") * jnp.exp(col)
            xw = xs_c * (dtc * jnp.exp(ce - col))
            ce_s = [cols[jj][ci * CH + CH - 1:ci * CH + CH, :] for jj in range(2)]
            a_p = jnp.where(lo_sub, jnp.exp(ce_s[0]), jnp.exp(ce_s[1]))
            h = a_p * h + dot(xw, bm_c, "tn")
            yc.append(y_intra + y_inter + dsk_p * xs_c)
        ys.append(jnp.concatenate(yc, axis=0))
        new_hp.append(h)
    return jnp.concatenate(ys, axis=1), tuple(new_hp)


def _ssd_block(xs, bm, cm, dtraw, z, dtb, alog, dsk, ng, hp, g, nu, dot, cdl, cdr):
    outs = []
    for u in range(nu):
        sl = slice(u * _SSD_U, (u + 1) * _SSD_U)
        y, hp = _ssd_unit(xs[sl], bm[sl], cm[sl], dtraw[sl], dtb, alog, dsk, hp, g, dot, cdl, cdr)
        outs.append(y)
    return _rms(jnp.concatenate(outs, axis=0) * _silu(z), ng), hp


def _ssd_specs(rows, rev, nb):
    t = (lambda j: nb - 1 - j) if rev else (lambda j: j)
    gw = SSD_DI // SSD_G
    xs = pl.BlockSpec((rows, gw), lambda j, g: (t(j), g))
    bm = pl.BlockSpec((rows, SSD_N), lambda j, g: (t(j), SSD_DI // SSD_N + g))
    cm = pl.BlockSpec((rows, SSD_N), lambda j, g: (t(j), (SSD_DI + SSD_GN) // SSD_N + g))
    dtr = pl.BlockSpec((rows, LANE), lambda j, g: (t(j), (SSD_DI + SSD_CONV) // LANE))
    par = pl.BlockSpec((1, LANE), lambda j, g: (0, 0))
    hs = pl.BlockSpec((None, None, 2, 2 * SSD_HD, SSD_N), lambda j, g: (t(j), g, 0, 0, 0))
    y = pl.BlockSpec((rows, gw), lambda j, g: (t(j), g))
    ng = pl.BlockSpec((1, gw), lambda j, g: (0, g))
    return xs, bm, cm, dtr, par, hs, y, ng


def ssd_scan_fwd(xbc, proj, dtb, alog, dsk, ng, *, nu, name):
    T = xbc.shape[0]
    rows = min(T, nu * _SSD_U)
    nu = rows // _SSD_U
    nb = T // rows
    xs_s, bm_s, cm_s, dt_s, par_s, hs_s, y_s, ng_s = _ssd_specs(rows, False, nb)

    def body(xs_ref, bm_ref, cm_ref, dt_ref, z_ref, dtb_ref, al_ref, dsk_ref, ng_ref, y_ref, hs_ref, h_ref):
        g = pl.program_id(1)

        @pl.when(pl.program_id(0) == 0)
        def _():
            h_ref[g] = jnp.zeros(h_ref.shape[1:], F32)

        hs_ref[...] = h_ref[g]
        hp = (h_ref[g, 0], h_ref[g, 1])
        y, hp = _ssd_block(xs_ref[...], bm_ref[...], cm_ref[...], dt_ref[...], z_ref[...], dtb_ref[...], al_ref[...],
                           dsk_ref[...], ng_ref[...], hp, g, nu, _dot_raw, lambda c, ct, x: _hdot(c, x), lambda x, c, ct: _hdot(x, c))
        y_ref[...] = y.astype(y_ref.dtype)
        h_ref[g, 0] = hp[0]
        h_ref[g, 1] = hp[1]

    return pl.pallas_call(
        body, name=name, grid=(nb, SSD_G), in_specs=[xs_s, bm_s, cm_s, dt_s, y_s, par_s, par_s, par_s, ng_s],
        out_specs=[y_s, hs_s],
        out_shape=[jax.ShapeDtypeStruct((T, SSD_DI), BF16), jax.ShapeDtypeStruct((nb, SSD_G, 2, 2 * SSD_HD, SSD_N), F32)],
        scratch_shapes=[pltpu.VMEM((SSD_G, 2, 2 * SSD_HD, SSD_N), F32)], compiler_params=_cparams(2),
    )(xbc, xbc, xbc, proj, proj, dtb, alog, dsk, ng)


def ssd_scan_bwd(xbc, proj, dtb, alog, dsk, ng, hs, dy, *, nu, name):
    T = xbc.shape[0]
    rows = min(T, nu * _SSD_U)
    nu = rows // _SSD_U
    nb = T // rows
    xs_s, bm_s, cm_s, dt_s, par_s, hs_s, y_s, ng_s = _ssd_specs(rows, True, nb)
    gw = SSD_DI // SSD_G
    dng_s = pl.BlockSpec((1, SSD_DI), lambda j, g: (0, 0))
    t = lambda j: nb - 1 - j
    n_s = pl.BlockSpec((rows, SSD_N), lambda j, g: (t(j), g))
    ddt_s = pl.BlockSpec((rows, LANE), lambda j, g: (t(j), 0))

    def body(xs_ref, bm_ref, cm_ref, dt_ref, z_ref, dtb_ref, al_ref, dsk_ref, ng_ref, hs_ref, dy_ref,
             dxs_ref, dbm_ref, dcm_ref, ddt_ref, ddtb_ref, dal_ref, ddsk_ref, dz_ref, dng_ref, dh_ref):
        j, g = pl.program_id(0), pl.program_id(1)

        @pl.when(j == 0)
        def _():
            dh_ref[g] = jnp.zeros(dh_ref.shape[1:], F32)

        @pl.when((j == 0) & (g == 0))
        def _():
            ddtb_ref[...] = jnp.zeros_like(ddtb_ref)
            dal_ref[...] = jnp.zeros_like(dal_ref)
            ddsk_ref[...] = jnp.zeros_like(ddsk_ref)
            dng_ref[...] = jnp.zeros_like(dng_ref)

        @pl.when(g == 0)
        def _():
            ddt_ref[...] = jnp.zeros_like(ddt_ref)

        fn = lambda xs, bm, cm, dtr, z, dtb_, al, dsk_, ng_, h0, h1: _ssd_block(
            xs, bm, cm, dtr, z, dtb_, al, dsk_, ng_, (h0, h1), g, nu, bdot, cdot_left, cdot_right)
        _, vjp = jax.vjp(fn, xs_ref[...], bm_ref[...], cm_ref[...], dt_ref[...], z_ref[...], dtb_ref[...], al_ref[...],
                         dsk_ref[...], ng_ref[...], hs_ref[0], hs_ref[1])
        dxs, dbm, dcm, ddt, dz, ddtb, dal, ddsk, dng, dh0, dh1 = vjp((dy_ref[...], (dh_ref[g, 0], dh_ref[g, 1])))
        dz_ref[...] = dz.astype(dz_ref.dtype)
        lanes = pl.ds(pl.multiple_of(g * gw, gw), gw)
        dng_ref[:, lanes] = dng_ref[:, lanes] + dng
        dxs_ref[...] = dxs
        dbm_ref[...] = dbm
        dcm_ref[...] = dcm
        ddt_ref[...] += ddt
        ddtb_ref[...] += ddtb
        dal_ref[...] += dal
        ddsk_ref[...] += ddsk
        dh_ref[g, 0] = dh0
        dh_ref[g, 1] = dh1

    return pl.pallas_call(
        body, name=name, grid=(nb, SSD_G), in_specs=[xs_s, bm_s, cm_s, dt_s, y_s, par_s, par_s, par_s, ng_s, hs_s, y_s],
        out_specs=[y_s, n_s, n_s, ddt_s, par_s, par_s, par_s, y_s, dng_s],
        out_shape=[jax.ShapeDtypeStruct((T, SSD_DI), F32), jax.ShapeDtypeStruct((T, SSD_GN), F32),
                   jax.ShapeDtypeStruct((T, SSD_GN), F32), jax.ShapeDtypeStruct((T, LANE), F32),
                   jax.ShapeDtypeStruct((1, LANE), F32), jax.ShapeDtypeStruct((1, LANE), F32),
                   jax.ShapeDtypeStruct((1, LANE), F32), jax.ShapeDtypeStruct((T, SSD_DI), BF16),
                   jax.ShapeDtypeStruct((1, SSD_DI), F32)],
        scratch_shapes=[pltpu.VMEM((SSD_G, 2, 2 * SSD_HD, SSD_N), F32)], compiler_params=_cparams(2),
    )(xbc, xbc, xbc, proj, proj, dtb, alog, dsk, ng, hs, dy)


def _s5_param_f(log_dt, a_re, a_im, bre_t, bim_t, cim, cdl):
    n = S5_NG * S5_GS
    r, c = _iota2((n, S5_NG), 0), _iota2((n, S5_NG), 1)
    E = ((r // S5_GS) == c).astype(F32)
    rt, ct = _iota2((S5_NG, n), 0), _iota2((S5_NG, n), 1)
    Et = ((ct // S5_GS) == rt).astype(F32)
    step = jnp.exp(log_dt)
    mag = jnp.exp(step * a_re)
    abr = mag * jnp.cos(step * a_im)
    abi = mag * jnp.sin(step * a_im)
    den = a_re * a_re + a_im * a_im
    nr, ni = abr - 1.0, abi
    fr = (nr * a_re + ni * a_im) / den
    fi = (ni * a_re - nr * a_im) / den
    Fr, Fi = cdl(E, Et, fr), cdl(E, Et, fi)
    bbr = Fr * bre_t - Fi * bim_t
    bbi = Fr * bim_t + Fi * bre_t
    return abr, abi, bbr, bbi, -cim


def _whole(a):
    return pl.BlockSpec(a.shape, lambda: (0,) * a.ndim)


def s5_param_fwd(args, *, name):
    def body(*refs):
        res = _s5_param_f(*[r[...] for r in refs[:6]], lambda c, ct, x: _hdot(c, x))
        for o, v in zip(refs[6:], res):
            o[...] = v

    shapes = [(S5_NG, S5_P), (S5_NG, S5_P)] + [(S5_NG * S5_GS, S5_P)] * 3
    return pl.pallas_call(
        body, name=name, in_specs=[_whole(a) for a in args], out_specs=[pl.BlockSpec(s, lambda: (0, 0)) for s in shapes],
        out_shape=[jax.ShapeDtypeStruct(s, F32) for s in shapes],
        compiler_params=pltpu.CompilerParams(vmem_limit_bytes=VMEM_LIMIT),
    )(*args)


def s5_param_bwd(args, cts, *, name):
    def body(*refs):
        fn = lambda *a: _s5_param_f(*a, cdot_left)
        _, vjp = jax.vjp(fn, *[r[...] for r in refs[:6]])
        grads = vjp(tuple(r[...] for r in refs[6:11]))
        for o, v in zip(refs[11:], grads):
            o[...] = v

    return pl.pallas_call(
        body, name=name, in_specs=[_whole(a) for a in list(args) + list(cts)],
        out_specs=[_whole(a) for a in args], out_shape=[jax.ShapeDtypeStruct(a.shape, F32) for a in args],
        compiler_params=pltpu.CompilerParams(vmem_limit_bytes=VMEM_LIMIT),
    )(*args, *cts)


_S5_W = S5_BLK * S5_P
_S5_BT = 1024


def _cmul_add(xr, xi, pr, pi, sr, si):
    return xr + (pr * sr - pi * si), xi + (pr * si + pi * sr)


def _s5_powers(ar, ai):
    pw = [(ar, ai)]
    for _ in range(7):
        qr, qi = pw[-1]
        pw.append((qr * ar - qi * ai, qr * ai + qi * ar))
    return pw


def s5_scan_fwd(u, wb, a_re, a_im, wc, *, name):
    T = u.shape[0]
    bt = min(T, _S5_BT)
    nb = T // bt

    def body(u_ref, wb_ref, ar_ref, ai_ref, wc_ref, x_ref, y_ref, bu_ref, carry_ref):
        @pl.when(pl.program_id(1) == 0)
        def _():
            carry_ref[...] = jnp.zeros_like(carry_ref)

        bu_ref[...] = _dot_raw(u_ref[...], wb_ref[...], "nn")
        ar, ai = ar_ref[...], ai_ref[...]
        pw = _s5_powers(ar, ai)
        pwr = jnp.concatenate([p[0] for p in pw], axis=0)
        pwi = jnp.concatenate([p[1] for p in pw], axis=0)
        rin = _iota2((8, _S5_W), 0)
        cr, ci = carry_ref[0:1, :], carry_ref[1:2, :]
        for t in range(bt // 8):
            sl = slice(8 * t, 8 * t + 8)
            xr, xi = bu_ref[sl, :_S5_W], bu_ref[sl, _S5_W:]
            for s in (1, 2, 4):
                m = rin >= s
                sr = jnp.where(m, pltpu.roll(xr, s, 0), 0.0)
                si = jnp.where(m, pltpu.roll(xi, s, 0), 0.0)
                xr, xi = _cmul_add(xr, xi, *pw[s - 1], sr, si)
            xr, xi = _cmul_add(xr, xi, pwr, pwi, cr, ci)
            x_ref[sl, :_S5_W] = xr
            x_ref[sl, _S5_W:] = xi
            cr, ci = xr[7:8, :], xi[7:8, :]
        carry_ref[0:1, :] = cr
        carry_ref[1:2, :] = ci
        y_ref[...] = _dot_raw(x_ref[...], wc_ref[...], "nn")

    nblk = S5_NG // S5_BLK
    blk = pl.BlockSpec((bt, 2 * _S5_W), lambda g, t: (t, g))
    col = pl.BlockSpec((bt, LANE), lambda g, t: (t, g))
    a_s = pl.BlockSpec((None, 1, _S5_W), lambda g, t: (g, 0, 0))
    wb_s = pl.BlockSpec((None, LANE, 2 * _S5_W), lambda g, t: (g, 0, 0))
    wc_s = pl.BlockSpec((None, 2 * _S5_W, LANE), lambda g, t: (g, 0, 0))
    return pl.pallas_call(
        body, name=name, grid=(nblk, nb), in_specs=[col, wb_s, a_s, a_s, wc_s], out_specs=[blk, col],
        out_shape=[jax.ShapeDtypeStruct((T, nblk * 2 * _S5_W), F32), jax.ShapeDtypeStruct((T, nblk * LANE), F32)],
        scratch_shapes=[pltpu.VMEM((bt, 2 * _S5_W), F32), pltpu.VMEM((8, _S5_W), F32)],
        compiler_params=_cparams(2),
    )(u, wb, a_re, a_im, wc)


def s5_scan_bwd(dy, x, u, wb, a_re, a_im, wc, *, name):
    T = dy.shape[0]
    bt = min(T, _S5_BT)
    nb = T // bt

    def body(dy_ref, x_ref, u_ref, wb_ref, ar_ref, ai_ref, wc_ref, du_ref, dwb_ref, dwc_ref, dar_ref, dai_ref,
             g_ref, lam_ref, carry_ref):
        @pl.when(pl.program_id(1) == 0)
        def _():
            carry_ref[...] = jnp.zeros_like(carry_ref)
            dar_ref[...] = jnp.zeros_like(dar_ref)
            dai_ref[...] = jnp.zeros_like(dai_ref)
            dwb_ref[...] = jnp.zeros_like(dwb_ref)
            dwc_ref[...] = jnp.zeros_like(dwc_ref)

        g_ref[...] = _dot_raw(dy_ref[...], wc_ref[...], "nt")
        pw = _s5_powers(ar_ref[...], -ai_ref[...])
        pwr = jnp.concatenate([p[0] for p in reversed(pw)], axis=0)
        pwi = jnp.concatenate([p[1] for p in reversed(pw)], axis=0)
        rin = _iota2((8, _S5_W), 0)
        cr, ci = carry_ref[0:1, :], carry_ref[1:2, :]
        acc_r = jnp.zeros((8, _S5_W), F32)
        acc_i = jnp.zeros((8, _S5_W), F32)
        for t in reversed(range(bt // 8)):
            sl = slice(8 * t, 8 * t + 8)
            lr, li = g_ref[sl, :_S5_W], g_ref[sl, _S5_W:]
            for s in (1, 2, 4):
                m = rin < 8 - s
                sr = jnp.where(m, pltpu.roll(lr, 8 - s, 0), 0.0)
                si = jnp.where(m, pltpu.roll(li, 8 - s, 0), 0.0)
                lr, li = _cmul_add(lr, li, *pw[s - 1], sr, si)
            lr, li = _cmul_add(lr, li, pwr, pwi, cr, ci)
            lam_ref[sl, :_S5_W] = lr
            lam_ref[sl, _S5_W:] = li
            nr = jnp.where(rin == 7, cr, pltpu.roll(lr, 7, 0))
            ni = jnp.where(rin == 7, ci, pltpu.roll(li, 7, 0))
            xr, xi = x_ref[sl, :_S5_W], x_ref[sl, _S5_W:]
            acc_r = acc_r + (xr * nr + xi * ni)
            acc_i = acc_i + (xr * ni - xi * nr)
            cr, ci = lr[0:1, :], li[0:1, :]
        carry_ref[0:1, :] = cr
        carry_ref[1:2, :] = ci
        dar_ref[...] += jnp.sum(acc_r, axis=0, keepdims=True)
        dai_ref[...] += jnp.sum(acc_i, axis=0, keepdims=True)
        lam = lam_ref[...]
        du_ref[...] = _dot_raw(lam, wb_ref[...], "nt")
        dwb_ref[...] += _dot_raw(u_ref[...], lam, "tn")
        dwc_ref[...] += _dot_raw(x_ref[...], dy_ref[...], "tn")

    nblk = S5_NG // S5_BLK
    blk = pl.BlockSpec((bt, 2 * _S5_W), lambda g, t: (nb - 1 - t, g))
    col = pl.BlockSpec((bt, LANE), lambda g, t: (nb - 1 - t, g))
    a_s = pl.BlockSpec((None, 1, _S5_W), lambda g, t: (g, 0, 0))
    wb_s = pl.BlockSpec((None, LANE, 2 * _S5_W), lambda g, t: (g, 0, 0))
    wc_s = pl.BlockSpec((None, 2 * _S5_W, LANE), lambda g, t: (g, 0, 0))
    return pl.pallas_call(
        body, name=name, grid=(nblk, nb), in_specs=[col, blk, col, wb_s, a_s, a_s, wc_s],
        out_specs=[col, wb_s, wc_s, a_s, a_s],
        out_shape=[jax.ShapeDtypeStruct((T, nblk * LANE), F32), jax.ShapeDtypeStruct((nblk, LANE, 2 * _S5_W), F32),
                   jax.ShapeDtypeStruct((nblk, 2 * _S5_W, LANE), F32), jax.ShapeDtypeStruct((nblk, 1, _S5_W), F32),
                   jax.ShapeDtypeStruct((nblk, 1, _S5_W), F32)],
        scratch_shapes=[pltpu.VMEM((bt, 2 * _S5_W), F32), pltpu.VMEM((bt, 2 * _S5_W), F32), pltpu.VMEM((8, _S5_W), F32)],
        compiler_params=_cparams(2),
    )(dy, x, u, wb, a_re, a_im, wc)


def _norm_bf16(h, g, name):
    return rowwise(f_rmsnorm, [g], [h], [(D, BF16)], bt=512, name=name)[0]


def _norm_bwd(h, g, cts, name):
    n = len(cts) - 1

    def f(p, r):
        y = _rms(r[0], p[0])
        return (y,) * n + (r[0],)

    (dg,), (dh,) = rowwise_vjp(f, [g], [h], cts, [F32], bt=256, name=name)
    return dh, dg


def ffn_fwd(h, g, w_gu, w_down, tag):
    hn = _norm_bf16(h, g, f"{tag}_norm")
    a, gu = ffn_up(hn, w_gu, name=f"{tag}_up")
    h2 = matmul(a, w_down[None], "nn", add=h, name=f"{tag}_down")
    return h2, (h, hn, gu, a)


def ffn_bwd(d, saved, g, w_gu, w_down, tag):
    h, hn, gu, a = saved
    dgu = ffn_dact(d, w_down, gu, name=f"{tag}_dact")
    dwd = matmul(a, d, "tn", name=f"{tag}_dwd")[0]
    dwgu = matmul(hn, dgu, "tn", name=f"{tag}_dwgu")[0]
    dh, dg = matmul_nt_norm_bwd(dgu, w_gu, h, g, d, name=f"{tag}_dhn")
    return dh, dg, dwgu, dwd


_GLA_NC = 4
_SSD_NU = 4


def gla_fwd(h, gm, w_in, w_a2, b_a, ng, w_out, tag):
    hn = _norm_bf16(h, gm, f"{tag}_norm")
    proj = matmul(hn, w_in[None], "nn", name=f"{tag}_in")
    alow = (proj, LANE, 2 * (GLA_QK + GLA_VD) // LANE)
    la = rowwise(f_gla_gate_in_fwd, [w_a2, b_a], [alow], [(GLA_QK, F32)], bt=512, name=f"{tag}_gate")[0]
    og, ss = gla_scan_fwd(proj, la, ng, nc=_GLA_NC, name=f"{tag}_scan")
    h2 = matmul(og, w_out[None], "nn", add=h, name=f"{tag}_proj")
    return h2, (h, hn, proj, la, ss, og)


def gla_bwd(d, saved, gm, w_in, w_a2, b_a, ng, w_out, tag):
    h, hn, proj, la, ss, og = saved
    dog = matmul(d, w_out[None], "nt", name=f"{tag}_dog")
    dwout = matmul(og, d, "tn", name=f"{tag}_dwout")[0]
    dq, dk, dv, dla, dr, dng = gla_scan_bwd(proj, la, ng, ss, dog, nc=_GLA_NC, name=f"{tag}_dscan")
    alow = (proj, LANE, 2 * (GLA_QK + GLA_VD) // LANE)
    (dwa2, dba), (dalow,) = rowwise_vjp(f_gla_gate_in, [w_a2, b_a], [alow], [dla], [BF16], bt=512, name=f"{tag}_dgate")
    dproj = jnp.concatenate([dq, dk, dv, dr, dalow], axis=1)
    dwin = matmul(hn, dproj, "tn", name=f"{tag}_dwin")[0]
    dh, dgm = matmul_nt_norm_bwd(dproj, w_in, h, gm, d, name=f"{tag}_dhn")
    return dh, dgm, dwin, dwa2[:GLA_RANK], dba, dng, dwout


def ssd_fwd(h, gm, w_in, conv_w, conv_b, dtb, alog, dsk, ng, w_out, tag):
    hn = _norm_bf16(h, gm, f"{tag}_norm")
    proj = matmul(hn, w_in[None], "nn", name=f"{tag}_in")
    xbc = ssd_conv_fwd(proj, conv_w, conv_b, name=f"{tag}_conv")
    yg, hs = ssd_scan_fwd(xbc, proj, dtb, alog, dsk, ng, nu=_SSD_NU, name=f"{tag}_scan")
    h2 = matmul(yg, w_out[None], "nn", add=h, name=f"{tag}_proj")
    return h2, (h, hn, proj, xbc, hs, yg)


def ssd_bwd(d, saved, gm, w_in, conv_w, conv_b, dtb, alog, dsk, ng, w_out, tag):
    h, hn, proj, xbc, hs, yg = saved
    dyg = matmul(d, w_out[None], "nt", name=f"{tag}_dyg")
    dwout = matmul(yg, d, "tn", name=f"{tag}_dwout")[0]
    dxs, dbm, dcm, ddt, ddtb, dal, ddsk, dz, dng = ssd_scan_bwd(xbc, proj, dtb, alog, dsk, ng, hs, dyg, nu=_SSD_NU,
                                                               name=f"{tag}_dscan")
    parts = [ssd_conv_bwd(proj, conv_w, conv_b, dout, col0, name=f"{tag}_dconv{k}")
             for k, (dout, col0) in enumerate(((dxs, 0), (dbm, SSD_DI), (dcm, SSD_DI + SSD_GN)))]
    dcw = jnp.concatenate([p[1] for p in parts], axis=1)
    dcb = jnp.concatenate([p[2] for p in parts], axis=1)
    dproj = jnp.concatenate([dz] + [p[0] for p in parts] + [ddt.astype(BF16)], axis=1)
    dwin = matmul(hn, dproj, "tn", name=f"{tag}_dwin")[0]
    dh, dgm = matmul_nt_norm_bwd(dproj, w_in, h, gm, d, name=f"{tag}_dhn")
    return (dh, dgm, dwin, dcw, dcb, ddtb[:, :SSD_H], dal[:, :SSD_H], ddsk[:, :SSD_H], dng, dwout)


_S5_NB = S5_NG // S5_BLK


def _s5_param_args(log_dt, a_re, a_im, b_re, b_im, c_im):
    n = S5_NG * S5_GS
    tr = lambda b: jnp.transpose(b, (0, 2, 1)).reshape(n, S5_P)
    return [log_dt.reshape(S5_NG, 1), a_re, a_im, tr(b_re), tr(b_im), c_im.reshape(n, S5_P)]


def _s5_blockdiag(t):
    nb, gl, a, b = t.shape
    eye = jnp.eye(gl, dtype=t.dtype)
    return (t[:, :, :, None, :] * eye[None, :, None, :, None]).reshape(nb, gl * a, gl * b)


def _s5_diag(t, a, b):
    nb = t.shape[0]
    gl = t.shape[1] // a
    eye = jnp.eye(gl, dtype=t.dtype)
    return jnp.sum(t.reshape(nb, gl, a, gl, b) * eye[None, :, None, :, None], axis=3)


def _s5_weights(bbr, bbi, c_re, cneg):
    sh = (_S5_NB, S5_BLK, S5_GS, S5_P)
    wb = jnp.concatenate([_s5_blockdiag(bbr.reshape(sh)), _s5_blockdiag(bbi.reshape(sh))], axis=2)
    tr = lambda cc: jnp.transpose(cc.reshape(sh), (0, 1, 3, 2))
    wc = jnp.concatenate([_s5_blockdiag(tr(c_re)), _s5_blockdiag(tr(cneg))], axis=1)
    return wb, wc


def s5_fwd(h, gm, prm, dsk, w_glu, tag):
    log_dt, a_re, a_im, b_re, b_im, c_re, c_im = prm
    hn = rowwise(f_rmsnorm, [gm], [h], [(D, F32)], bt=512, name=f"{tag}_norm")[0]
    pargs = _s5_param_args(log_dt, a_re, a_im, b_re, b_im, c_im)
    abr, abi, bbr, bbi, cneg = s5_param_fwd(pargs, name=f"{tag}_param")
    wb, wc = _s5_weights(bbr, bbi, c_re.reshape(S5_NG * S5_GS, S5_P), cneg)
    ar, ai = abr.reshape(_S5_NB, 1, _S5_W), abi.reshape(_S5_NB, 1, _S5_W)
    wb, wc = wb.astype(BF16), wc.astype(BF16)
    x, ycp = s5_scan_fwd(hn, wb, ar, ai, wc, name=f"{tag}_scan")
    yg = rowwise(f_s5_act, [dsk], [ycp, hn], [(D, BF16)], bt=512, name=f"{tag}_act")[0]
    vg = matmul(yg, w_glu[None], "nn", name=f"{tag}_glu")
    h2 = rowwise(f_glu_res, [], [vg, h], [(D, F32)], bt=512, name=f"{tag}_out")[0]
    return h2, (h, hn, pargs, wb, wc, ar, ai, x, ycp, yg, vg)


def s5_bwd(d, saved, gm, dsk, w_glu, tag):
    h, hn, pargs, wb, wc, ar, ai, x, ycp, yg, vg = saved
    _, (dvg,) = rowwise_vjp(f_glu, [], [vg], [d], [BF16], bt=256, name=f"{tag}_dout")
    dwglu = matmul(yg, dvg, "tn", name=f"{tag}_dwglu")[0]
    dyg = matmul(dvg, w_glu[None], "nt", name=f"{tag}_dyg")
    (ddsk,), (dycp, dhn1) = rowwise_vjp(f_s5_act, [dsk], [ycp, hn], [dyg], [F32, F32], bt=256, name=f"{tag}_dact")
    dhn2, dwb, dwc, dar, dai = s5_scan_bwd(dycp, x, hn, wb, ar, ai, wc, name=f"{tag}_dscan")
    dh, dgm = _norm_bwd(h, gm, [dhn1, dhn2, d], f"{tag}_dnorm")
    n = S5_NG * S5_GS
    half = S5_BLK * S5_P
    d_bbr = _s5_diag(dwb[:, :, :half], S5_GS, S5_P).reshape(n, S5_P)
    d_bbi = _s5_diag(dwb[:, :, half:], S5_GS, S5_P).reshape(n, S5_P)
    from_c = lambda t: jnp.transpose(_s5_diag(t, S5_P, S5_GS), (0, 1, 3, 2)).reshape(n, S5_P)
    d_cre = from_c(dwc[:, :half, :])
    d_cneg = from_c(dwc[:, half:, :])
    cts = [dar.reshape(S5_NG, S5_P), dai.reshape(S5_NG, S5_P), d_bbr, d_bbi, d_cneg]
    dlog, dare, daim, dbre_t, dbim_t, dcim = s5_param_bwd(pargs, cts, name=f"{tag}_dparam")
    untr = lambda t: jnp.transpose(t.reshape(S5_NG, S5_GS, S5_P), (0, 2, 1))
    grads = (dlog.reshape(S5_NG), dare, daim, untr(dbre_t), untr(dbim_t),
             d_cre.reshape(S5_NG, S5_GS, S5_P), dcim.reshape(S5_NG, S5_GS, S5_P))
    return dh, dgm, grads, ddsk, dwglu


def _pad_last(w, n):
    return jnp.pad(w, [(0, 0)] * (w.ndim - 1) + [(0, n - w.shape[-1])])


_BIG = ("gla_w_in", "gla_w_out", "ssd_w_in", "ssd_w_out", "s5_w_glu", "ffn_w_gu", "ffn_w_down")


def interleave_gu(w):
    q = w.shape[-1] // 4
    return jnp.concatenate([w[..., :q], w[..., 2 * q:3 * q], w[..., q:2 * q], w[..., 3 * q:]], axis=-1)


def local_step(x, target, W, later_weights=None, later_grads=None, ffn0_grads=None, ffn0_weights=None):
    f32 = lambda a: a.astype(F32)
    row = lambda a: f32(a).reshape(1, -1)

    def layer_args(i):
        m, j = i % 3, i // 3
        gm = row(W["norm_mix_g"][i])
        if m == 0:
            args = (gm, W["gla_w_in"][j], jnp.pad(f32(W["gla_w_a2"][j]), ((0, LANE - GLA_RANK), (0, 0))),
                    row(W["gla_b_a"][j]), row(W["gla_norm_g"][j]), W["gla_w_out"][j])
        elif m == 1:
            pl_ = lambda a: _pad_last(row(a), LANE)
            args = (gm, W["ssd_w_in"][j], f32(W["ssd_conv_w"][j]),
                    row(W["ssd_conv_b"][j]), pl_(W["ssd_dt_bias"][j]), pl_(W["ssd_a_log"][j]), pl_(W["ssd_d"][j]),
                    row(W["ssd_norm_g"][j]), W["ssd_w_out"][j])
        else:
            prm = tuple(f32(W[k][j]) for k in ("s5_log_dt", "s5_a_re", "s5_a_im", "s5_b_re", "s5_b_im", "s5_c_re", "s5_c_im"))
            args = (gm, prm, row(W["s5_d"][j]), W["s5_w_glu"][j])
        return m, j, args

    h = x
    saved, mixers, ffns = [], [], []
    for i in range(DEPTH):
        mixer = layer_args(i)
        mixers.append(mixer)
        m, j, args = mixer
        tag = f"l{i}_{('gla', 'ssd', 's5')[m]}"
        h, sm = (gla_fwd, ssd_fwd, s5_fwd)[m](h, *args, tag)
        if i == 0 and ffn0_weights is not None:
            W = {**W, **ffn0_weights(h)}
        ffn = (row(W["norm_ffn_g"][i]), W["ffn_w_gu"][i], W["ffn_w_down"][i])
        ffns.append(ffn)
        h, sf = ffn_fwd(h, *ffn, f"l{i}_ffn")
        saved.append((sm, sf))
        if i == 0 and later_weights is not None:
            W = {**W, **later_weights(h)}
    loss, dfg, d = loss_head(h, row(W["final_norm_g"]), target, name="loss_head")

    G = {k: [None] * len(v) for k, v in W.items() if k != "final_norm_g"}
    G["final_norm_g"] = dfg.reshape(D)
    for i in reversed(range(DEPTH)):
        m, j, args = mixers[i]
        sm, sf = saved[i]
        if i == 0 and later_grads is not None:
            zero = later_grads(G)
            ffns[0] = (ffns[0][0], ffns[0][1], ffns[0][2] + zero.astype(ffns[0][2].dtype))
        d, dg, dwgu, dwd = ffn_bwd(d, sf, *ffns[i], f"l{i}_ffn")
        G["norm_ffn_g"][i], G["ffn_w_gu"][i], G["ffn_w_down"][i] = dg.reshape(D), dwgu, dwd
        if i == 0 and ffn0_grads is not None:
            zero = ffn0_grads(G)
            args = args[:-1] + (args[-1] + zero.astype(args[-1].dtype),)
        tag = f"l{i}_{('gla', 'ssd', 's5')[m]}"
        if m == 0:
            d, dgm, dwin, dwa2, dba, dng, dwout = gla_bwd(d, sm, *args, tag)
            G["gla_w_in"][j], G["gla_w_a2"][j], G["gla_b_a"][j] = dwin, dwa2, dba.reshape(-1)
            G["gla_norm_g"][j], G["gla_w_out"][j] = dng.reshape(-1), dwout
        elif m == 1:
            d, dgm, dwin, dcw, dcb, ddtb, dal, ddsk, dng, dwout = ssd_bwd(d, sm, *args, tag)
            G["ssd_w_in"][j], G["ssd_conv_w"][j], G["ssd_conv_b"][j] = dwin, dcw, dcb.reshape(-1)
            G["ssd_dt_bias"][j], G["ssd_a_log"][j], G["ssd_d"][j] = ddtb.reshape(-1), dal.reshape(-1), ddsk.reshape(-1)
            G["ssd_norm_g"][j], G["ssd_w_out"][j] = dng.reshape(-1), dwout
        else:
            d, dgm, pg, ddsk, dwglu = s5_bwd(d, sm, args[0], args[2], args[3], tag)
            for k, v in zip(("s5_log_dt", "s5_a_re", "s5_a_im", "s5_b_re", "s5_b_im", "s5_c_re", "s5_c_im"), pg):
                G[k][j] = v
            G["s5_d"][j], G["s5_w_glu"][j] = ddsk.reshape(-1), dwglu
        G["norm_mix_g"][i] = dgm.reshape(D)
    grads = {k: (v if k == "final_norm_g" or k in _BIG else jnp.stack(v)) for k, v in G.items()}
    return loss, d, grads


_MESH = pl.DeviceIdType.MESH
_ANY = pl.BlockSpec(memory_space=pl.ANY)
_DMA = pltpu.SemaphoreType.DMA
_ROWS_ALIGN = 1024


def _place():
    return lax.axis_index("x"), lax.axis_index("y"), lax.axis_index("c")


def _other_chips(x, y):
    return [(1 - x, y), (x, 1 - y), (1 - x, 1 - y)]


def _remote(src, dst, send_sems, recv_sems, k, to):
    return pltpu.make_async_remote_copy(src_ref=src, dst_ref=dst, send_sem=send_sems.at[k], recv_sem=recv_sems.at[k],
                                        device_id=to, device_id_type=_MESH)


def gather_shards(loc, *, name):
    def body(in_ref, out_ref, send_sems, recv_sems, local_sem):
        x, y, c = _place()
        me, sibling = (x, y, c), (x, y, 1 - c)
        chips = _other_chips(x, y)

        def half(px, py, hc):
            return out_ref.at[2 * px + py, hc]

        mine = pltpu.make_async_copy(in_ref, out_ref.at[2 * x + y], local_sem)
        mine.start()
        first = [_remote(in_ref.at[c], half(x, y, c), send_sems, recv_sems, j, (*chip, c)) for j, chip in enumerate(chips)]
        for cp in first:
            cp.start()
        passed = [_remote(half(*chip, c), half(*chip, c), send_sems, recv_sems, 3 + j, sibling) for j, chip in enumerate(chips)]
        for j, chip in enumerate(chips):
            _remote(in_ref.at[c], half(*chip, c), send_sems, recv_sems, j, me).wait_recv()
            passed[j].start()
        for j, chip in enumerate(chips):
            _remote(in_ref.at[c], half(*chip, 1 - c), send_sems, recv_sems, 3 + j, me).wait_recv()
        for cp in first + passed:
            cp.wait_send()
        mine.wait()

    return pl.pallas_call(
        body, name=name, in_specs=[_ANY], out_specs=_ANY,
        out_shape=jax.ShapeDtypeStruct((4,) + loc.shape, loc.dtype),
        scratch_shapes=[_DMA((6,)), _DMA((6,)), _DMA(())],
    )(loc)


def _pos(px, py, perm):
    return 2 * py + px if perm else 2 * px + py


def _part(ref, kind, p, loc):
    if kind == "lead":
        return ref.at[p]
    return ref.at[:, pl.ds(pl.multiple_of(p * loc, LANE), loc)]


def _rows(ref, h, hr):
    return ref.at[pl.ds(h * hr, hr)]


def _rows_block(hr, width):
    return max(b for b in range(16, hr + 1, 16) if hr % b == 0 and (b * width <= (1 << 19) or b == 16))


def gather_big(locs, kinds, *, name):
    n = len(locs)

    def body(*refs):
        ins, outs = refs[:n], refs[n:2 * n]
        send_sems, recv_sems = refs[2 * n + 1:]
        refs[2 * n][...] = jnp.zeros_like(refs[2 * n])
        x, y, c = _place()
        me, sibling = (x, y, c), (x, y, 1 - c)
        chips = _other_chips(x, y)

        def half(i, px, py, h):
            (kind, perm), (rows, loc) = kinds[i], locs[i].shape
            return _rows(_part(outs[i], kind, _pos(px, py, perm), loc), h, rows // 2)

        sends = []
        for i in range(n):
            (kind, perm), (rows, loc) = kinds[i], locs[i].shape
            own = _part(outs[i], kind, _pos(x, y, perm), loc)
            sends.append(_remote(ins[i], own, send_sems, recv_sems, 6 * n + i, sibling))
            sends[-1].start()
            for j, chip in enumerate(chips):
                sends.append(_remote(_rows(ins[i], c, rows // 2), half(i, x, y, c), send_sems, recv_sems, 6 * i + j, (*chip, c)))
                sends[-1].start()
        for i in range(n):
            hr = locs[i].shape[0] // 2
            for j, chip in enumerate(chips):
                _remote(_rows(ins[i], c, hr), half(i, *chip, c), send_sems, recv_sems, 6 * i + j, me).wait_recv()
                sends.append(_remote(half(i, *chip, c), half(i, *chip, c), send_sems, recv_sems, 6 * i + 3 + j, sibling))
                sends[-1].start()
        for i in range(n):
            (kind, perm), (rows, loc) = kinds[i], locs[i].shape
            for j, chip in enumerate(chips):
                _remote(_rows(ins[i], c, rows // 2), half(i, *chip, 1 - c), send_sems, recv_sems, 6 * i + 3 + j, me).wait_recv()
            _remote(ins[i], _part(outs[i], kind, _pos(x, y, perm), loc), send_sems, recv_sems, 6 * n + i, me).wait_recv()
        for cp in sends:
            cp.wait_send()

    def out_shape(a, kind):
        rows, loc = a.shape
        return jax.ShapeDtypeStruct((4, rows, loc) if kind == "lead" else (rows, 4 * loc), a.dtype)

    outs = pl.pallas_call(
        body, name=name, in_specs=[_ANY] * n, out_specs=[_ANY] * n + [pl.BlockSpec(memory_space=pltpu.VMEM)],
        out_shape=[out_shape(a, k[0]) for a, k in zip(locs, kinds)] + [jax.ShapeDtypeStruct((8, LANE), F32)],
        scratch_shapes=[_DMA((7 * n,)), _DMA((7 * n,))],
    )(*locs)
    return list(outs[:n]), outs[n][0, 0]


_HBM = pl.BlockSpec(memory_space=pltpu.HBM)
_SEM = pl.BlockSpec(memory_space=pltpu.SEMAPHORE)
_EFFECT = pltpu.SideEffectType.DATAFLOW_SIDE_EFFECTING


def _in_hbm(a):
    return pltpu.with_memory_space_constraint(a, pltpu.HBM)


def _gather_ici_copies(ins, lands, kinds, shapes, send_sems, recv_sems):
    x, y, c = _place()
    sends, arrivals = [], []
    for i, ((kind, perm), (rows, loc)) in enumerate(zip(kinds, shapes)):
        hr = rows // 2
        mine = _part(lands[i], kind, _pos(x, y, perm), loc)
        sends.append(_remote(ins[i], mine, send_sems, recv_sems, 4 * i + 3, (x, y, 1 - c)))
        arrivals.append(_remote(ins[i], mine, send_sems, recv_sems, 4 * i + 3, (x, y, c)))
        for j, (px, py) in enumerate(_other_chips(x, y)):
            sends.append(_remote(_rows(ins[i], c, hr), _rows(mine, c, hr), send_sems, recv_sems, 4 * i + j, (px, py, c)))
            theirs = _rows(_part(lands[i], kind, _pos(px, py, perm), loc), c, hr)
            arrivals.append(_remote(_rows(ins[i], c, hr), theirs, send_sems, recv_sems, 4 * i + j, (x, y, c)))
    return sends, arrivals


def gather_start(locs, kinds, *, name):
    n = len(locs)
    shapes = [a.shape for a in locs]

    def land_shape(a, kind):
        rows, loc = a.shape
        return (4, rows, loc) if kind == "lead" else (rows, 4 * loc)

    def body(*refs):
        sends, _ = _gather_ici_copies(refs[:n], refs[n:2 * n], kinds, shapes, refs[2 * n], refs[2 * n + 1])
        for cp in sends:
            cp.start()
        refs[-1][...] = jnp.zeros_like(refs[-1])

    lands = [lax.empty(land_shape(a, k[0]), a.dtype) for a, k in zip(locs, kinds)]
    outs = pl.pallas_call(
        body, name=name, in_specs=[_HBM] * (2 * n), out_specs=[_SEM, _SEM] + [_HBM] * (2 * n) + [pl.BlockSpec(memory_space=pltpu.VMEM)],
        out_shape=[_DMA((4 * n,)), _DMA((4 * n,))] + [pltpu.HBM(a.shape, a.dtype) for a in locs]
        + [pltpu.HBM(l.shape, l.dtype) for l in lands] + [jax.ShapeDtypeStruct((8, LANE), F32)],
        input_output_aliases={i: 2 + i for i in range(2 * n)},
        compiler_params=pltpu.CompilerParams(has_side_effects=_EFFECT),
    )(*[_in_hbm(a) for a in locs], *[_in_hbm(l) for l in lands])
    return outs[0], outs[1], list(outs[2:2 + n]), list(outs[2 + n:2 + 2 * n]), outs[-1][0, 0]


def gather_wait(send_sems, recv_sems, locs, lands, kinds, after, *, name):
    n = len(locs)
    shapes = [a.shape for a in locs]

    def body(*refs):
        sends, arrivals = _gather_ici_copies(refs[:n], refs[n:2 * n], kinds, shapes, refs[2 * n], refs[2 * n + 1])
        for cp in sends:
            cp.wait_send()
        for cp in arrivals:
            cp.wait_recv()

    outs = pl.pallas_call(
        body, name=name, in_specs=[_HBM] * (2 * n) + [_SEM, _SEM, _ANY], out_specs=[_HBM] * (2 * n),
        out_shape=[pltpu.HBM(a.shape, a.dtype) for a in locs] + [pltpu.HBM(l.shape, l.dtype) for l in lands],
        input_output_aliases={i: i for i in range(2 * n)},
        compiler_params=pltpu.CompilerParams(has_side_effects=_EFFECT),
    )(*locs, *lands, send_sems, recv_sems, after)
    return list(outs[n:])


def gather_finish(lands, kinds, shapes, *, name):
    n = len(lands)

    def body(*refs):
        bufs = refs[n:2 * n]
        send_sems, recv_sems = refs[2 * n:]
        x, y, c = _place()
        sends = []
        for i, ((kind, perm), (rows, loc)) in enumerate(zip(kinds, shapes)):
            for j, (px, py) in enumerate(_other_chips(x, y)):
                part = _part(bufs[i], kind, _pos(px, py, perm), loc)
                sends.append(_remote(_rows(part, c, rows // 2), _rows(part, c, rows // 2), send_sems, recv_sems, 3 * i + j, (x, y, 1 - c)))
                sends[-1].start()
        for i, ((kind, perm), (rows, loc)) in enumerate(zip(kinds, shapes)):
            for j, (px, py) in enumerate(_other_chips(x, y)):
                part = _part(bufs[i], kind, _pos(px, py, perm), loc)
                _remote(_rows(part, c, rows // 2), _rows(part, 1 - c, rows // 2), send_sems, recv_sems, 3 * i + j, (x, y, c)).wait_recv()
        for cp in sends:
            cp.wait_send()

    return list(pl.pallas_call(
        body, name=name, in_specs=[_ANY] * n, out_specs=[_ANY] * n,
        out_shape=[jax.ShapeDtypeStruct(l.shape, l.dtype) for l in lands],
        input_output_aliases={i: i for i in range(n)}, scratch_shapes=[_DMA((3 * n,)), _DMA((3 * n,))],
    )(*lands))


def _scatter_copies(qs, lands, kinds, locs, send_sems, recv_sems):
    x, y, c = _place()
    sends, arrivals = [], []
    for i, (kind, perm) in enumerate(kinds):
        for j, (px, py) in enumerate(_other_chips(x, y)):
            src = _part(qs[i], kind, _pos(px, py, perm), locs[i])
            sends.append(_remote(src, lands[i].at[j], send_sems, recv_sems, 3 * i + j, (px, py, c)))
            arrivals.append(_remote(src, lands[i].at[j], send_sems, recv_sems, 3 * i + j, (x, y, c)))
    return sends, arrivals


def _scatter_land(q, kind, loc):
    return (3, q.shape[1] if kind == "lead" else q.shape[0], loc)


def scatter_start(qs, kinds, locs, *, name):
    n = len(qs)

    def body(*refs):
        sends, _ = _scatter_copies(refs[:n], refs[n:2 * n], kinds, locs, refs[2 * n], refs[2 * n + 1])
        for cp in sends:
            cp.start()
        refs[-1][...] = jnp.zeros_like(refs[-1])

    lands = [lax.empty(_scatter_land(q, k[0], l), q.dtype) for q, k, l in zip(qs, kinds, locs)]
    outs = pl.pallas_call(
        body, name=name, in_specs=[_HBM] * (2 * n), out_specs=[_SEM, _SEM] + [_HBM] * (2 * n) + [pl.BlockSpec(memory_space=pltpu.VMEM)],
        out_shape=[_DMA((3 * n,)), _DMA((3 * n,))] + [pltpu.HBM(q.shape, q.dtype) for q in qs]
        + [pltpu.HBM(l.shape, l.dtype) for l in lands] + [jax.ShapeDtypeStruct((8, LANE), F32)],
        input_output_aliases={i: 2 + i for i in range(2 * n)},
        compiler_params=pltpu.CompilerParams(has_side_effects=_EFFECT),
    )(*[_in_hbm(q) for q in qs], *[_in_hbm(l) for l in lands])
    return outs[0], outs[1], list(outs[2:2 + n]), list(outs[2 + n:2 + 2 * n]), outs[-1][0, 0]


def scatter_wait(send_sems, recv_sems, qs, lands, kinds, locs, after, *, name):
    n = len(qs)

    def body(*refs):
        sends, arrivals = _scatter_copies(refs[:n], refs[n:2 * n], kinds, locs, refs[2 * n], refs[2 * n + 1])
        for cp in sends:
            cp.wait_send()
        for cp in arrivals:
            cp.wait_recv()

    outs = pl.pallas_call(
        body, name=name, in_specs=[_HBM] * (2 * n) + [_SEM, _SEM, _ANY], out_specs=[_HBM] * (2 * n),
        out_shape=[pltpu.HBM(q.shape, q.dtype) for q in qs] + [pltpu.HBM(l.shape, l.dtype) for l in lands],
        input_output_aliases={i: i for i in range(2 * n)},
        compiler_params=pltpu.CompilerParams(has_side_effects=_EFFECT),
    )(*qs, *lands, send_sems, recv_sems, after)
    return list(outs[:n]), list(outs[n:])


def _pair_swap_copies(ins, lands, kinds, send_sems, recv_sems):
    x, y, c = _place()
    sends, arrivals = [], []
    for i, (kind, _) in enumerate(kinds):
        if kind == "lead":
            hr = ins[i].shape[1] // 2
            src = ins[i].at[:, pl.ds((1 - c) * hr, hr)]
        else:
            src = _rows(ins[i], 1 - c, ins[i].shape[0] // 2)
        sends.append(_remote(src, lands[i], send_sems, recv_sems, i, (x, y, 1 - c)))
        arrivals.append(_remote(src, lands[i], send_sems, recv_sems, i, (x, y, c)))
    return sends, arrivals


def _pair_swap_land(a, kind):
    s = a.shape
    return (4, s[1] // 2, s[2]) if kind == "lead" else (s[0] // 2, s[1])


def pair_swap_start(ps, kinds, *, name):
    n = len(ps)

    def body(*refs):
        sends, _ = _pair_swap_copies(refs[:n], refs[n:2 * n], kinds, refs[2 * n], refs[2 * n + 1])
        for cp in sends:
            cp.start()
        refs[-1][...] = jnp.zeros_like(refs[-1])

    lands = [lax.empty(_pair_swap_land(p, k[0]), p.dtype) for p, k in zip(ps, kinds)]
    outs = pl.pallas_call(
        body, name=name, in_specs=[_HBM] * (2 * n), out_specs=[_SEM, _SEM] + [_HBM] * (2 * n) + [pl.BlockSpec(memory_space=pltpu.VMEM)],
        out_shape=[_DMA((n,)), _DMA((n,))] + [pltpu.HBM(p.shape, p.dtype) for p in ps]
        + [pltpu.HBM(l.shape, l.dtype) for l in lands] + [jax.ShapeDtypeStruct((8, LANE), F32)],
        input_output_aliases={i: 2 + i for i in range(2 * n)},
        compiler_params=pltpu.CompilerParams(has_side_effects=_EFFECT),
    )(*[_in_hbm(p) for p in ps], *[_in_hbm(l) for l in lands])
    return outs[0], outs[1], list(outs[2:2 + n]), list(outs[2 + n:2 + 2 * n]), outs[-1][0, 0]


def pair_swap_wait(send_sems, recv_sems, ps, lands, kinds, after, *, name):
    n = len(ps)

    def body(*refs):
        sends, arrivals = _pair_swap_copies(refs[:n], refs[n:2 * n], kinds, refs[2 * n], refs[2 * n + 1])
        for cp in sends:
            cp.wait_send()
        for cp in arrivals:
            cp.wait_recv()

    outs = pl.pallas_call(
        body, name=name, in_specs=[_HBM] * (2 * n) + [_SEM, _SEM, _ANY], out_specs=[_HBM] * (2 * n),
        out_shape=[pltpu.HBM(p.shape, p.dtype) for p in ps] + [pltpu.HBM(l.shape, l.dtype) for l in lands],
        input_output_aliases={i: i for i in range(2 * n)},
        compiler_params=pltpu.CompilerParams(has_side_effects=_EFFECT),
    )(*ps, *lands, send_sems, recv_sems, after)
    return list(outs[:n]), list(outs[n:])


def pair_swap(ps, kinds, *, name):
    n = len(ps)

    def body(*refs):
        ins, outs = refs[:n], refs[n:2 * n]
        send_sems, recv_sems = refs[2 * n:]
        x, y, c = _place()
        cps = []
        for i in range(n):
            if kinds[i][0] == "lead":
                hr = ps[i].shape[1] // 2
                src = ins[i].at[:, pl.ds((1 - c) * hr, hr)]
            else:
                hr = ps[i].shape[0] // 2
                src = _rows(ins[i], 1 - c, hr)
            cps.append(_remote(src, outs[i], send_sems, recv_sems, i, (x, y, 1 - c)))
            cps[-1].start()
        for cp in cps:
            cp.wait()

    def out_shape(a, kind):
        s = a.shape
        return jax.ShapeDtypeStruct((4, s[1] // 2, s[2]) if kind == "lead" else (s[0] // 2, s[1]), a.dtype)

    return pl.pallas_call(
        body, name=name, in_specs=[_ANY] * n, out_specs=[_ANY] * n,
        out_shape=[out_shape(a, k[0]) for a, k in zip(ps, kinds)], scratch_shapes=[_DMA((n,)), _DMA((n,))],
    )(*ps)


def pair_add(p, got, c_arr, kind, *, name):
    if kind == "lead":
        _, hr, cols = got.shape
        br = _rows_block(hr, cols)
        nb = hr // br
        grid = (4, nb)
        p_spec = pl.BlockSpec((None, br, cols), lambda s, i, cr: (s, cr[0] * nb + i, 0))
        g_spec = pl.BlockSpec((None, br, cols), lambda s, i, cr: (s, i, 0))
    else:
        hr, w = got.shape
        br = _rows_block(hr, w)
        nb = hr // br
        grid = (nb,)
        p_spec = pl.BlockSpec((br, w), lambda i, cr: (cr[0] * nb + i, 0))
        g_spec = pl.BlockSpec((br, w), lambda i, cr: (i, 0))

    def body(c_ref, p_ref, g_ref, o_ref):
        o_ref[...] = (p_ref[...] + g_ref[...]).astype(o_ref.dtype)

    return pl.pallas_call(
        body, name=name, out_shape=jax.ShapeDtypeStruct(got.shape, BF16),
        grid_spec=pltpu.PrefetchScalarGridSpec(num_scalar_prefetch=1, grid=grid, in_specs=[p_spec, g_spec], out_specs=g_spec),
        compiler_params=_cparams(len(grid)),
    )(c_arr, p, got)


def chip_scatter(qs, kinds, locs, *, name):
    n = len(qs)

    def body(*refs):
        ins, outs = refs[:n], refs[n:2 * n]
        send_sems, recv_sems = refs[2 * n:]
        x, y, c = _place()
        cps = []
        for i in range(n):
            kind, perm = kinds[i]
            for j, (px, py) in enumerate(_other_chips(x, y)):
                cps.append(_remote(_part(ins[i], kind, _pos(px, py, perm), locs[i]), outs[i].at[j], send_sems, recv_sems,
                                   3 * i + j, (px, py, c)))
                cps[-1].start()
        for cp in cps:
            cp.wait()

    def out_shape(a, kind, loc):
        hr = a.shape[1] if kind == "lead" else a.shape[0]
        return jax.ShapeDtypeStruct((3, hr, loc), a.dtype)

    return pl.pallas_call(
        body, name=name, in_specs=[_ANY] * n, out_specs=[_ANY] * n,
        out_shape=[out_shape(a, k[0], l) for a, k, l in zip(qs, kinds, locs)],
        scratch_shapes=[_DMA((3 * n,)), _DMA((3 * n,))],
    )(*qs)


def chip_add(q, r, pos_arr, c_arr, kind, loc, *, name):
    _, hr, _ = r.shape
    br = _rows_block(hr, loc)
    nb = hr // br
    if kind == "lead":
        q_spec = pl.BlockSpec((None, br, loc), lambda i, pr, cr: (pr[0], i, 0))
    else:
        q_spec = pl.BlockSpec((br, loc), lambda i, pr, cr: (i, pr[0]))
    r_spec = pl.BlockSpec((3, br, loc), lambda i, pr, cr: (0, i, 0))
    o_spec = pl.BlockSpec((br, loc), lambda i, pr, cr: (cr[0] * nb + i, 0))

    def body(p_ref, c_ref, q_ref, r_ref, o_ref):
        acc = q_ref[...].astype(F32)
        for j in range(3):
            acc = acc + r_ref[j].astype(F32)
        o_ref[...] = acc

    return pl.pallas_call(
        body, name=name, out_shape=jax.ShapeDtypeStruct((2 * hr, loc), F32),
        grid_spec=pltpu.PrefetchScalarGridSpec(num_scalar_prefetch=2, grid=(nb,), in_specs=[q_spec, r_spec], out_specs=o_spec),
        compiler_params=_cparams(1),
    )(pos_arr, c_arr, q, r)


def share_rows(fs, *, name):
    n = len(fs)

    def body(*refs):
        bufs = refs[n:2 * n]
        send_sems, recv_sems = refs[2 * n:]
        x, y, c = _place()
        cps = []
        for i in range(n):
            hr = fs[i].shape[0] // 2
            cps.append(_remote(_rows(bufs[i], c, hr), _rows(bufs[i], c, hr), send_sems, recv_sems, i, (x, y, 1 - c)))
            cps[-1].start()
        for i, cp in enumerate(cps):
            hr = fs[i].shape[0] // 2
            _remote(_rows(bufs[i], c, hr), _rows(bufs[i], 1 - c, hr), send_sems, recv_sems, i, (x, y, c)).wait_recv()
            cp.wait_send()

    return pl.pallas_call(
        body, name=name, in_specs=[_ANY] * n, out_specs=[_ANY] * n,
        out_shape=[jax.ShapeDtypeStruct(f.shape, f.dtype) for f in fs],
        input_output_aliases={i: i for i in range(n)}, scratch_shapes=[_DMA((n,)), _DMA((n,))],
    )(*fs)


def _gather_all_copies(v_ref, land_ref, send_sems, recv_sems):
    x, y, c = _place()
    flip = lambda p, m: 1 - p if m else p
    idx = lambda p: 4 * p[0] + 2 * p[1] + p[2]
    sends, arrivals = [], []
    for k, m in enumerate(range(1, 8)):
        p = (flip(x, m & 4), flip(y, m & 2), flip(c, m & 1))
        sends.append(_remote(v_ref, land_ref.at[idx((x, y, c))], send_sems, recv_sems, k, p))
        arrivals.append(_remote(v_ref, land_ref.at[idx(p)], send_sems, recv_sems, k, (x, y, c)))
    return sends, arrivals


def gather_all_start(v, *, name):
    def body(v_ref, land_ref, send_sems, recv_sems, v_thru, land_thru, token):
        sends, _ = _gather_all_copies(v_ref, land_ref, send_sems, recv_sems)
        for cp in sends:
            cp.start()
        token[...] = jnp.zeros_like(token)

    land = jnp.zeros((8,) + v.shape, v.dtype)
    outs = pl.pallas_call(
        body, name=name, in_specs=[_HBM, _HBM], out_specs=[_SEM, _SEM, _HBM, _HBM, pl.BlockSpec(memory_space=pltpu.VMEM)],
        out_shape=[_DMA((7,)), _DMA((7,)), pltpu.HBM(v.shape, v.dtype), pltpu.HBM(land.shape, land.dtype),
                   jax.ShapeDtypeStruct((8, LANE), F32)],
        input_output_aliases={0: 2, 1: 3}, compiler_params=pltpu.CompilerParams(has_side_effects=_EFFECT),
    )(_in_hbm(v), _in_hbm(land))
    return outs[0], outs[1], outs[2], outs[3], outs[4][0, 0]


def gather_all_wait(send_sems, recv_sems, v, land, after, *, name):
    def body(v_ref, land_ref, send_sems, recv_sems, after_ref, v_dead, got_ref):
        sends, arrivals = _gather_all_copies(v_ref, land_ref, send_sems, recv_sems)
        for cp in sends:
            cp.wait_send()
        for cp in arrivals:
            cp.wait_recv()

    return pl.pallas_call(
        body, name=name, in_specs=[_HBM, _HBM, _SEM, _SEM, _ANY], out_specs=[_HBM, _HBM],
        out_shape=[pltpu.HBM(v.shape, v.dtype), pltpu.HBM(land.shape, land.dtype)],
        input_output_aliases={0: 0, 1: 1}, compiler_params=pltpu.CompilerParams(has_side_effects=_EFFECT),
    )(v, land, send_sems, recv_sems, after)[1]


def sum_slots(land, v, me_arr, *, name):
    n, R, L = land.shape
    br = _pick(R, _ROWS_ALIGN, 8)

    def body(me_ref, land_ref, v_ref, o_ref):
        acc = None
        for i in range(n):
            term = jnp.where(me_ref[0] == i, v_ref[...], land_ref[i])
            acc = term if acc is None else acc + term
        o_ref[...] = acc

    row = pl.BlockSpec((br, L), lambda i, me: (i, 0))
    return pl.pallas_call(
        body, name=name, out_shape=jax.ShapeDtypeStruct((R, L), land.dtype),
        grid_spec=pltpu.PrefetchScalarGridSpec(num_scalar_prefetch=1, grid=(R // br,),
                                               in_specs=[pl.BlockSpec((n, br, L), lambda i, me: (0, i, 0)), row], out_specs=row),
        compiler_params=_cparams(1),
    )(me_arr, land, v)


def adamw(w, g, m, v, *, name):
    shape = w.shape
    size = math.prod(shape)
    last = shape[-1]
    if last % LANE != 0 and size % LANE == 0 and size <= (1 << 20):
        last = LANE
    rows = size // last
    budget = (1 << 18) // last
    br = rows
    if rows > budget:
        br = max(c for c in range(8, budget + 1, 8) if rows % c == 0)
    v2 = lambda a: a.reshape(rows, last)

    def body(w_ref, g_ref, m_ref, v_ref, d_ref, nm_ref, nv_ref):
        gg = g_ref[...]
        nm = ADAM_B1 * m_ref[...] + (1.0 - ADAM_B1) * gg
        nv = ADAM_B2 * v_ref[...] + (1.0 - ADAM_B2) * (gg * gg)
        m_hat = nm / (1.0 - ADAM_B1 ** ADAM_STEP)
        v_hat = nv / (1.0 - ADAM_B2 ** ADAM_STEP)
        d_ref[...] = -ADAM_LR * (m_hat / (jnp.sqrt(v_hat) + ADAM_EPS) + ADAM_WD * w_ref[...])
        nm_ref[...] = nm
        nv_ref[...] = nv

    spec = pl.BlockSpec((br, last), lambda i: (i, 0))
    outs = pl.pallas_call(
        body, name=name, grid=(rows // br,), in_specs=[spec] * 4, out_specs=[spec] * 3,
        out_shape=[jax.ShapeDtypeStruct((rows, last), F32)] * 3, compiler_params=_cparams(1),
    )(v2(w), v2(g), v2(m), v2(v))
    return [o.reshape(shape) for o in outs]


_WEIGHTS = ["norm_mix_g", "norm_ffn_g", "gla_w_in", "gla_w_a2", "gla_b_a", "gla_norm_g", "gla_w_out", "ssd_w_in",
            "ssd_conv_w", "ssd_conv_b", "ssd_dt_bias", "ssd_a_log", "ssd_d", "ssd_norm_g", "ssd_w_out", "s5_log_dt",
            "s5_a_re", "s5_a_im", "s5_b_re", "s5_b_im", "s5_c_re", "s5_c_im", "s5_d", "s5_w_glu", "ffn_w_gu",
            "ffn_w_down", "final_norm_g"]
_SHARD_AXIS = {"gla_w_in": 2, "gla_w_a2": 2, "gla_b_a": 1, "gla_norm_g": 1, "gla_w_out": 1, "ssd_w_in": 2,
               "ssd_conv_w": 2, "ssd_w_out": 1, "s5_d": 1, "s5_w_glu": 2, "ffn_w_gu": 2, "ffn_w_down": 1}
_SMALL_SHARDED = [n for n in _WEIGHTS if n in _SHARD_AXIS and n not in _BIG]
_REPLICATED = [n for n in _WEIGHTS if n not in _SHARD_AXIS]
_BIG_KIND = {"gla_w_in": ("lead", False), "gla_w_out": ("lead", False), "ssd_w_in": ("lead", False),
             "ssd_w_out": ("lead", False), "s5_w_glu": ("cols", False), "ffn_w_gu": ("cols", True),
             "ffn_w_down": ("lead", False)}
_PADDED_IN = {"gla_w_in": GLA_INP, "ssd_w_in": SSD_INP}


def _to_rows(flat, parts=1):
    per = -(-flat.shape[0] // (parts * LANE * _ROWS_ALIGN)) * _ROWS_ALIGN
    flat = jnp.pad(flat, (0, parts * per * LANE - flat.shape[0]))
    return flat.reshape(parts, per, LANE)


def _big_layers(local):
    return [(n, j, local[n][j].reshape(-1, local[n].shape[-1])) for n in _BIG for j in range(local[n].shape[0])]


def _in_layer0(n, j):
    return j == 0 and n in ("gla_w_in", "gla_w_out", "ffn_w_gu", "ffn_w_down")


def _assemble(n, g):
    if n in _PADDED_IN:
        return jnp.concatenate([g[s] for s in range(4)] + [jnp.zeros((g.shape[1], _PADDED_IN[n] - 4 * g.shape[2]), BF16)], axis=1)
    if _BIG_KIND[n][0] == "lead":
        return g.reshape(4 * g.shape[1], g.shape[2])
    return g


def _is_gla0(n, j):
    return j == 0 and n in ("gla_w_in", "gla_w_out")


def _gather_first(local):
    layers = _big_layers(local)
    first = [l for l in layers if _is_gla0(l[0], l[1])]
    full = {n: [None] * local[n].shape[0] for n in _BIG}
    got, done = gather_big([w.astype(BF16) for _, _, w in first], [_BIG_KIND[n] for n, _, _ in first], name="gather_weights_first")
    for (n, j, _), g in zip(first, got):
        full[n][j] = _assemble(n, g)
    flat = jnp.concatenate([local[n].astype(F32).reshape(-1) for n in _SMALL_SHARDED])
    got = gather_shards(_to_rows(flat, 2), name="gather_small_weights").reshape(4, -1)
    off = 0
    for n in _SMALL_SHARDED:
        bs = local[n].shape
        sz = math.prod(bs)
        seg = got[:, off:off + sz].reshape((4,) + bs)
        off += sz
        ax = _SHARD_AXIS[n]
        full[n] = jnp.moveaxis(seg, 0, ax).reshape(bs[:ax] + (4 * bs[ax],) + bs[ax + 1:])
    pending = {}
    for tag, want in (("ffn0", _is_ffn0), ("later", lambda n, j: not _in_layer0(n, j))):
        group = [l for l in layers if want(l[0], l[1])]
        kinds = [_BIG_KIND[n] for n, _, _ in group]
        ops = [(w + done if k == 0 else w).astype(BF16) for k, (_, _, w) in enumerate(group)]
        send_sems, recv_sems, locs, lands, done = gather_start(ops, kinds, name=f"gather_weights_start_{tag}")
        pending[tag] = (group, kinds, send_sems, recv_sems, locs, lands)
    return full, pending, done


def _gather_rest(full, pending, after, tag):
    group, kinds, send_sems, recv_sems, locs, lands = pending
    lands = gather_wait(send_sems, recv_sems, locs, lands, kinds, after, name=f"gather_weights_wait_{tag}")
    lands = gather_finish(lands, kinds, [w.shape for _, _, w in group], name=f"gather_weights_finish_{tag}")
    out = {n: list(full[n]) for n in _BIG}
    for (n, j, _), g in zip(group, lands):
        out[n][j] = _assemble(n, g)
    return out


def _reduce_ops(grads, local, want):
    ops = []
    for n in _BIG:
        kind = _BIG_KIND[n]
        for j, g in enumerate(grads[n]):
            if not want(n, j):
                continue
            loc = local[n].shape[-1] if kind[0] == "cols" or n in _PADDED_IN else g.shape[1]
            if n in _PADDED_IN:
                g = jnp.stack([g[:, s * loc:(s + 1) * loc] for s in range(4)])
            elif kind[0] == "lead":
                g = g.reshape(4, g.shape[0] // 4, g.shape[1])
            ops.append((n, j, kind, loc, g))
    return ops


def _pair_sums(ops, c_arr, tag):
    gots = pair_swap([o[4] for o in ops], [o[2] for o in ops], name=f"reduce_pair_swap_{tag}")
    return [pair_add(o[4], got, c_arr, o[2][0], name=f"reduce_pair_add_{o[0]}{o[1]}") for o, got in zip(ops, gots)]


def _is_ffn0(n, j):
    return j == 0 and n in ("ffn_w_gu", "ffn_w_down")


def _reduce_start(grads, local, c, want, tag):
    ops = _reduce_ops(grads, local, want)
    c_arr = jnp.reshape(c, (1,)).astype(jnp.int32)
    qs = _pair_sums(ops, c_arr, tag)
    send_sems, recv_sems, qs, lands, zero = scatter_start(qs, [o[2] for o in ops], [o[3] for o in ops],
                                                          name=f"reduce_scatter_start_{tag}")
    return (ops, send_sems, recv_sems, qs, lands, tag), zero


def _reduce_swap_start(grads, local, c, want, tag):
    ops = _reduce_ops(grads, local, want)
    send_sems, recv_sems, ps, lands, zero = pair_swap_start([o[4] for o in ops], [o[2] for o in ops],
                                                            name=f"reduce_pair_swap_start_{tag}")
    return (ops, send_sems, recv_sems, ps, lands, tag), zero


def _reduce_scatter_after(pending, after, c):
    ops, send_sems, recv_sems, ps, lands, tag = pending
    ps, gots = pair_swap_wait(send_sems, recv_sems, ps, lands, [o[2] for o in ops], after, name=f"reduce_pair_swap_wait_{tag}")
    c_arr = jnp.reshape(c, (1,)).astype(jnp.int32)
    qs = [pair_add(p, got, c_arr, o[2][0], name=f"reduce_pair_add_{o[0]}{o[1]}") for o, p, got in zip(ops, ps, gots)]
    send_sems, recv_sems, qs, lands, zero = scatter_start(qs, [o[2] for o in ops], [o[3] for o in ops],
                                                          name=f"reduce_scatter_start_{tag}")
    return (ops, send_sems, recv_sems, qs, lands, tag), zero


def _reduce_big(grads, local, pendings, after, x, y, c):
    c_arr = jnp.reshape(c, (1,)).astype(jnp.int32)
    ops, qs, rs = [], [], []
    for ops_p, send_sems, recv_sems, qs_p, lands, tag in pendings:
        qs_p, rs_p = scatter_wait(send_sems, recv_sems, qs_p, lands, [o[2] for o in ops_p], [o[3] for o in ops_p], after,
                                  name=f"reduce_scatter_wait_{tag}")
        ops, qs, rs = ops + ops_p, qs + qs_p, rs + rs_p
    ops_f = _reduce_ops(grads, local, lambda n, j: _in_layer0(n, j) and not _is_ffn0(n, j))
    qs_f = _pair_sums(ops_f, c_arr, "first")
    s_sems, r_sems, qs_f, lands_f, zero = scatter_start(qs_f, [o[2] for o in ops_f], [o[3] for o in ops_f],
                                                        name="reduce_scatter_start_first")
    qs[0] = qs[0] + zero.astype(qs[0].dtype)
    red = _reduce_close(ops, qs, rs, x, y, c_arr, "later")
    done = {n: jnp.stack([red[(n, j)] for j in range(local[n].shape[0])]).reshape(local[n].shape)
            for n in _BIG if all((n, j) in red for j in range(local[n].shape[0]))}
    return done, red, (ops_f, s_sems, r_sems, qs_f, lands_f)


def _reduce_close(ops, qs, rs, x, y, c_arr, tag):
    fs = [chip_add(q, r, jnp.reshape(_pos(x, y, o[2][1]), (1,)).astype(jnp.int32), c_arr, o[2][0], o[3],
                   name=f"reduce_chip_add_{o[0]}{o[1]}") for o, q, r in zip(ops, qs, rs)]
    outs = share_rows(fs, name=f"reduce_share_{tag}")
    return {(o[0], o[1]): r for o, r in zip(ops, outs)}


def _reduce_big_first(pending, red, after, local, x, y, c):
    ops_f, s_sems, r_sems, qs_f, lands_f = pending
    qs_f, rs_f = scatter_wait(s_sems, r_sems, qs_f, lands_f, [o[2] for o in ops_f], [o[3] for o in ops_f], after,
                              name="reduce_scatter_wait_first")
    red = {**red, **_reduce_close(ops_f, qs_f, rs_f, x, y, jnp.reshape(c, (1,)).astype(jnp.int32), "first")}
    names = sorted({o[0] for o in ops_f})
    return {n: jnp.stack([red[(n, j)] for j in range(local[n].shape[0])]).reshape(local[n].shape) for n in names}


def _reduce_small_start(grads):
    names = _REPLICATED + _SMALL_SHARDED
    flat = jnp.concatenate([grads[n].astype(F32).reshape(-1) for n in names])
    n_el = flat.shape[0]
    rows = -(-n_el // (LANE * 8)) * 8
    v = jnp.pad(flat, (0, rows * LANE - n_el)).reshape(rows, LANE)
    outs = gather_all_start(v, name="reduce_small_start")
    return outs[:4], outs[4]


def _reduce_small(pending, after, grads, local, x, y, c):
    names = _REPLICATED + _SMALL_SHARDED
    send_sems, recv_sems, v, land = pending
    land = gather_all_wait(send_sems, recv_sems, v, land, after, name="reduce_small_wait")
    me = jnp.reshape(4 * x + 2 * y + c, (1,)).astype(jnp.int32)
    red = sum_slots(land, v, me, name="reduce_small_add").reshape(-1)
    out, off = {}, 0
    for n in names:
        sz = math.prod(grads[n].shape)
        g = red[off:off + sz].reshape(grads[n].shape)
        off += sz
        if n in _SHARD_AXIS:
            ax = _SHARD_AXIS[n]
            loc = local[n].shape[ax]
            g = lax.dynamic_slice_in_dim(g, (2 * x + y) * loc, loc, axis=ax)
        out[n] = g
    return out


def kernel(x, norm_mix_g, norm_ffn_g, gla_w_in, gla_w_a2, gla_b_a, gla_norm_g, gla_w_out, ssd_w_in, ssd_conv_w, ssd_conv_b, ssd_dt_bias, ssd_a_log, ssd_d, ssd_norm_g, ssd_w_out, s5_log_dt, s5_a_re, s5_a_im, s5_b_re, s5_b_im, s5_c_re, s5_c_im, s5_d, s5_w_glu, ffn_w_gu, ffn_w_down, final_norm_g, loss_target, m_norm_mix_g, m_norm_ffn_g, m_gla_w_in, m_gla_w_a2, m_gla_b_a, m_gla_norm_g, m_gla_w_out, m_ssd_w_in, m_ssd_conv_w, m_ssd_conv_b, m_ssd_dt_bias, m_ssd_a_log, m_ssd_d, m_ssd_norm_g, m_ssd_w_out, m_s5_log_dt, m_s5_a_re, m_s5_a_im, m_s5_b_re, m_s5_b_im, m_s5_c_re, m_s5_c_im, m_s5_d, m_s5_w_glu, m_ffn_w_gu, m_ffn_w_down, m_final_norm_g, v_norm_mix_g, v_norm_ffn_g, v_gla_w_in, v_gla_w_a2, v_gla_b_a, v_gla_norm_g, v_gla_w_out, v_ssd_w_in, v_ssd_conv_w, v_ssd_conv_b, v_ssd_dt_bias, v_ssd_a_log, v_ssd_d, v_ssd_norm_g, v_ssd_w_out, v_s5_log_dt, v_s5_a_re, v_s5_a_im, v_s5_b_re, v_s5_b_im, v_s5_c_re, v_s5_c_im, v_s5_d, v_s5_w_glu, v_ffn_w_gu, v_ffn_w_down, v_final_norm_g):
    given = dict(locals())
    local = {n: given[n] for n in _WEIGHTS}
    px, py, pc = _place()

    first, gathering, zero = _gather_first(local)
    full = dict(local)
    full.update(first)
    full["norm_mix_g"] = local["norm_mix_g"] + zero
    big = [first]

    def weights_of(tag):
        def arrived(h):
            big.append(_gather_rest(big[-1], gathering[tag], h, tag))
            return big[-1]
        return arrived

    swapping, reducing = [], []

    def later_grads(g):
        pending, zero = _reduce_swap_start(g, local, pc, lambda n, j: not _in_layer0(n, j), "later")
        swapping.append(pending)
        return zero

    def ffn0_grads(g):
        pending, zero = _reduce_scatter_after(swapping[0], g["ffn_w_down"][0], pc)
        reducing.append(pending)
        g["ffn_w_down"][0] = g["ffn_w_down"][0] + zero
        pending, zero = _reduce_start(g, local, pc, _is_ffn0, "ffn0")
        reducing.append(pending)
        return zero

    loss, grad_x, grads = local_step(x[0], loss_target[0], full, weights_of("later"), later_grads, ffn0_grads, weights_of("ffn0"))
    loss = lax.psum(loss, ("x", "y", "c"))

    small, zero = _reduce_small_start(grads)
    grads["gla_w_out"][0] = grads["gla_w_out"][0] + zero
    red, parts, first_pending = _reduce_big(grads, local, reducing, grad_x, px, py, pc)

    deltas, new_m, new_v = {}, {}, {}

    def update(n):
        deltas[n], new_m[n], new_v[n] = adamw(local[n], red[n], given["m_" + n], given["v_" + n], name=f"adamw_{n}")

    for n in list(red):
        update(n)
    red.update(_reduce_big_first(first_pending, parts, deltas["ffn_w_gu"], local, px, py, pc))
    red.update(_reduce_small(small, red["gla_w_out"], grads, local, px, py, pc))
    for n in _WEIGHTS:
        if n not in deltas:
            update(n)
    return (loss, grad_x[None], *[red[n] for n in _WEIGHTS], *[deltas[n] for n in _WEIGHTS],
            *[new_m[n] for n in _WEIGHTS], *[new_v[n] for n in _WEIGHTS])
```

```python
import functools
import math

import jax
import jax.numpy as jnp
from jax import lax
from jax.experimental import pallas as pl
from jax.experimental.pallas import tpu as pltpu

F32 = jnp.float32
BF16 = jnp.bfloat16

D = 1024
DEPTH = 4
CH = 64
EPS = 1e-6
GLA_H, GLA_DK, GLA_DV, GLA_RANK, GLA_TAU = 4, 128, 256, 16, 16.0
GLA_QK = GLA_H * GLA_DK
GLA_VD = GLA_H * GLA_DV
GLA_IN = 2 * GLA_QK + 2 * GLA_VD + GLA_RANK
GLA_INP = 3200
SSD_DI, SSD_HD, SSD_H, SSD_G, SSD_N, SSD_K = 2048, 64, 32, 8, 128, 4
SSD_GN = SSD_G * SSD_N
SSD_CONV = SSD_DI + 2 * SSD_GN
SSD_IN = SSD_DI + SSD_CONV + SSD_H
SSD_INP = 6272
S5_GS, S5_NG, S5_P = 16, 64, 64
S5_BLK = 8
FFN_H = 2816
LANE = 128
VMEM_LIMIT = 52 * 1024 * 1024
_MATMUL_VMEM = 40 * 1024 * 1024

ADAM_LR, ADAM_B1, ADAM_B2, ADAM_EPS, ADAM_WD, ADAM_STEP = 0.001, 0.9, 0.999, 1e-08, 0.01, 10

_ARB = "arbitrary"


def _cparams(n):
    return pltpu.CompilerParams(dimension_semantics=(_ARB,) * n, vmem_limit_bytes=VMEM_LIMIT)


def _pick(n, target, mult=LANE):
    best = None
    for c in range(mult, min(n, target) + 1, mult):
        if n % c == 0:
            best = c
    return best if best is not None else n


_DN = {"nn": (((1,), (0,)), ((), ())), "nt": (((1,), (1,)), ((), ())), "tn": (((0,), (0,)), ((), ()))}


def _dot_raw(a, b, form):
    return lax.dot_general(a.astype(BF16), b.astype(BF16), _DN[form], preferred_element_type=F32)


@functools.partial(jax.custom_vjp, nondiff_argnums=(2,))
def bdot(a, b, form):
    return _dot_raw(a, b, form)


def _bdot_fwd(a, b, form):
    return _dot_raw(a, b, form), (a, b)


def _bdot_bwd(form, res, g):
    a, b = res
    if form == "nn":
        return _dot_raw(g, b, "nt"), _dot_raw(a, g, "tn")
    if form == "nt":
        return _dot_raw(g, b, "nn"), _dot_raw(g, a, "tn")
    return _dot_raw(b, g, "nt"), _dot_raw(a, g, "nn")


bdot.defvjp(_bdot_fwd, _bdot_bwd)


def _hdot(a, b):
    return jnp.dot(a, b, precision=lax.Precision.HIGHEST, preferred_element_type=F32)


@jax.custom_vjp
def cdot_left(c, ct, x):
    return _hdot(c, x)


def _cdl_fwd(c, ct, x):
    return _hdot(c, x), (c, ct)


def _cdl_bwd(res, g):
    c, ct = res
    return jnp.zeros_like(c), jnp.zeros_like(ct), _hdot(ct, g)


cdot_left.defvjp(_cdl_fwd, _cdl_bwd)


@jax.custom_vjp
def cdot_right(x, c, ct):
    return _hdot(x, c)


def _cdr_fwd(x, c, ct):
    return _hdot(x, c), (c, ct)


def _cdr_bwd(res, g):
    c, ct = res
    return _hdot(g, ct), jnp.zeros_like(c), jnp.zeros_like(ct)


cdot_right.defvjp(_cdr_fwd, _cdr_bwd)


def _sigmoid(x):
    return 1.0 / (1.0 + jnp.exp(-x))


def _silu(x):
    return x * _sigmoid(x)


def _softplus(x):
    return jnp.maximum(x, 0.0) + jnp.log(1.0 + jnp.exp(-jnp.abs(x)))


def _log_sigmoid(x):
    return jnp.minimum(x, 0.0) - jnp.log(1.0 + jnp.exp(-jnp.abs(x)))


def _gelu(x):
    c = math.sqrt(2.0 / math.pi)
    return 0.5 * x * (1.0 + jnp.tanh(c * (x + 0.044715 * (x * x * x))))


def _rms(x, g):
    return x * lax.rsqrt(jnp.mean(x * x, axis=-1, keepdims=True) + EPS) * g


def _iota2(shape, axis):
    return lax.broadcasted_iota(jnp.int32, shape, axis)


def matmul(a, b, form, *, name, G=1, out_dtype=F32, add=None):
    isz = lambda t: jnp.dtype(t.dtype).itemsize
    osz = jnp.dtype(out_dtype).itemsize + (isz(add) if add is not None else 0)

    def fits(bm, bn, bk):
        return 2 * (bm * bk * isz(a) + bk * bn * isz(b) + bm * bn * osz) + 4 * bm * bn <= _MATMUL_VMEM

    if form in ("nn", "nt"):
        M = a.shape[0]
        K = a.shape[1] // G
        N = b.shape[2] if form == "nn" else b.shape[1]
        bm, bn, bk = min(M, 1024), _pick(N, 1536), _pick(K, 2048)
        while not fits(bm, bn, bk) and bk % 256 == 0:
            bk //= 2
        nj, nk = N // bn, K // bk
        grid = (G, M // bm, nj, nk)
        a_spec = pl.BlockSpec((bm, bk), lambda g, i, j, k: (i, g * nk + k))
        if form == "nn":
            b_spec = pl.BlockSpec((None, bk, bn), lambda g, i, j, k: (g, k, j))
        else:
            b_spec = pl.BlockSpec((None, bn, bk), lambda g, i, j, k: (g, j, k))
        o_spec = pl.BlockSpec((bm, bn), lambda g, i, j, k: (i, g * nj + j))
        out_shape = jax.ShapeDtypeStruct((M, G * N), out_dtype)
    else:
        T = a.shape[0]
        Ka, Nb = a.shape[1] // G, b.shape[1] // G
        bm, bn, bk = _pick(Ka, 1408), _pick(Nb, 1536), min(T, 2048)
        while not fits(bm, bn, bk) and bk % 512 == 0:
            bk //= 2
        ni, nj, nk = Ka // bm, Nb // bn, T // bk
        grid = (G, ni, nj, nk)
        a_spec = pl.BlockSpec((bk, bm), lambda g, i, j, k: (k, g * ni + i))
        b_spec = pl.BlockSpec((bk, bn), lambda g, i, j, k: (k, g * nj + j))
        o_spec = pl.BlockSpec((None, bm, bn), lambda g, i, j, k: (g, i, j))
        out_shape = jax.ShapeDtypeStruct((G, Ka, Nb), out_dtype)
    has_add = add is not None

    def finish(refs, r):
        if has_add:
            r = r + refs[2][...].astype(F32)
        o_ref = refs[3] if has_add else refs[2]
        o_ref[...] = r.astype(o_ref.dtype)

    def body_one(*refs):
        finish(refs, _dot_raw(refs[0][...], refs[1][...], form))

    def body_acc(*refs):
        acc_ref = refs[-1]
        k = pl.program_id(3)

        @pl.when(k == 0)
        def _():
            acc_ref[...] = jnp.zeros_like(acc_ref)

        acc_ref[...] += _dot_raw(refs[0][...], refs[1][...], form)

        @pl.when(k == nk - 1)
        def _():
            finish(refs, acc_ref[...])

    in_specs = [a_spec, b_spec]
    args = [a, b]
    if has_add:
        in_specs.append(o_spec)
        args.append(add)
    return pl.pallas_call(
        body_one if nk == 1 else body_acc, name=name, grid=grid, in_specs=in_specs, out_specs=o_spec,
        out_shape=out_shape, scratch_shapes=[] if nk == 1 else [pltpu.VMEM((bm, bn), F32)],
        compiler_params=_cparams(4),
    )(*args)


_NORM_BWD_ROWS = 128


def matmul_nt_norm_bwd(a, w, h, g, d, *, name):
    T, K = a.shape
    bm = min(T, 512)
    bk = _pick(K, 2048)
    nk = K // bk

    def body(a_ref, w_ref, h_ref, g_ref, d_ref, dh_ref, dg_ref, acc_ref):
        i, k = pl.program_id(0), pl.program_id(1)

        @pl.when((i == 0) & (k == 0))
        def _():
            dg_ref[...] = jnp.zeros_like(dg_ref)

        @pl.when(k == 0)
        def _():
            acc_ref[...] = jnp.zeros_like(acc_ref)

        acc_ref[...] += _dot_raw(a_ref[...], w_ref[...], "nt")

        @pl.when(k == nk - 1)
        def _():
            g = g_ref[...]
            dg_sum = jnp.zeros_like(g)
            for lo in range(0, bm, _NORM_BWD_ROWS):
                sl = slice(lo, min(lo + _NORM_BWD_ROWS, bm))
                _, vjp = jax.vjp(lambda g_, h_: _rms(h_, g_), g, h_ref[sl, :])
                dg, dh = vjp(acc_ref[sl, :])
                dh_ref[sl, :] = dh + d_ref[sl, :]
                dg_sum = dg_sum + dg
            dg_ref[...] += dg_sum

    row = pl.BlockSpec((bm, D), lambda i, k: (i, 0))
    one = pl.BlockSpec((1, D), lambda i, k: (0, 0))
    return pl.pallas_call(
        body, name=name, grid=(T // bm, nk),
        in_specs=[pl.BlockSpec((bm, bk), lambda i, k: (i, k)), pl.BlockSpec((D, bk), lambda i, k: (0, k)), row, one, row],
        out_specs=[row, one], out_shape=[jax.ShapeDtypeStruct((T, D), F32), jax.ShapeDtypeStruct((1, D), F32)],
        scratch_shapes=[pltpu.VMEM((bm, D), F32)], compiler_params=_cparams(2),
    )(a, w, h, g, d)


def ffn_up(hn, w_il, *, name):
    T = hn.shape[0]
    bm, hb = min(T, 512), FFN_H // 2

    def body(a_ref, b_ref, act_ref, gu_ref):
        r = _dot_raw(a_ref[...], b_ref[...], "nn")
        act_ref[...] = (_silu(r[:, :hb]) * r[:, hb:]).astype(act_ref.dtype)
        gu_ref[...] = r.astype(gu_ref.dtype)

    return pl.pallas_call(
        body, name=name, grid=(2, T // bm),
        in_specs=[pl.BlockSpec((bm, D), lambda j, i: (i, 0)), pl.BlockSpec((D, 2 * hb), lambda j, i: (0, j))],
        out_specs=[pl.BlockSpec((bm, hb), lambda j, i: (i, j)), pl.BlockSpec((bm, 2 * hb), lambda j, i: (i, j))],
        out_shape=[jax.ShapeDtypeStruct((T, FFN_H), BF16), jax.ShapeDtypeStruct((T, 2 * FFN_H), BF16)],
        compiler_params=_cparams(2),
    )(hn, w_il)


_DACT_CHUNK = 512


def ffn_dact(d, w_down, gu, *, name):
    T = d.shape[0]
    bm, hb = min(T, 512), FFN_H // 2

    def body(d_ref, w_ref, gu_ref, o_ref):
        d_blk = d_ref[...].astype(BF16)
        for lo in range(0, hb, _DACT_CHUNK):
            hi = min(lo + _DACT_CHUNK, hb)
            da = _dot_raw(d_blk, w_ref[lo:hi, :], "nt")
            g, u = gu_ref[:, lo:hi].astype(F32), gu_ref[:, hb + lo:hb + hi].astype(F32)
            sg = _sigmoid(g)
            o_ref[:, lo:hi] = (da * u * (sg * (1.0 + g * (1.0 - sg)))).astype(o_ref.dtype)
            o_ref[:, hb + lo:hb + hi] = (da * (g * sg)).astype(o_ref.dtype)

    return pl.pallas_call(
        body, name=name, grid=(2, T // bm),
        in_specs=[pl.BlockSpec((bm, D), lambda j, i: (i, 0)), pl.BlockSpec((hb, D), lambda j, i: (j, 0)),
                  pl.BlockSpec((bm, 2 * hb), lambda j, i: (i, j))],
        out_specs=pl.BlockSpec((bm, 2 * hb), lambda j, i: (i, j)),
        out_shape=jax.ShapeDtypeStruct((T, 2 * FFN_H), BF16), compiler_params=_cparams(2),
    )(d, w_down, gu)


def _row_entry(e):
    return e if isinstance(e, tuple) else (e, e.shape[1], 0)


def _row_spec(bt, e):
    _, width, idx = e
    return pl.BlockSpec((bt, width), lambda i: (i, idx))


def _full_spec(p):
    return pl.BlockSpec(p.shape, lambda i: (0,) * p.ndim)


def rowwise(f, params, rows, outs, *, bt, name):
    rows = [_row_entry(e) for e in rows]
    T = rows[0][0].shape[0]
    bt = min(bt, T)
    np_, nr = len(params), len(rows)

    def body(*refs):
        p = tuple(r[...].astype(F32) for r in refs[:np_])
        rw = tuple(r[...].astype(F32) for r in refs[np_:np_ + nr])
        res = f(p, rw)
        for o_ref, o in zip(refs[np_ + nr:], res):
            o_ref[...] = o.astype(o_ref.dtype)

    res = pl.pallas_call(
        body, name=name, grid=(T // bt,),
        in_specs=[_full_spec(p) for p in params] + [_row_spec(bt, e) for e in rows],
        out_specs=[pl.BlockSpec((bt, w), lambda i: (i, 0)) for w, _ in outs],
        out_shape=[jax.ShapeDtypeStruct((T, w), dt) for w, dt in outs],
        compiler_params=_cparams(1),
    )(*params, *[e[0] for e in rows])
    return list(res)


def rowwise_vjp(f, params, rows, cts, drow_dtypes, *, bt, name):
    rows = [_row_entry(e) for e in rows]
    cts = [_row_entry(e) for e in cts]
    T = rows[0][0].shape[0]
    bt = min(bt, T)
    np_, nr, nc = len(params), len(rows), len(cts)
    want = [i for i, dt in enumerate(drow_dtypes) if dt is not None]

    def body(*refs):
        p = tuple(r[...].astype(F32) for r in refs[:np_])
        rw = tuple(r[...].astype(F32) for r in refs[np_:np_ + nr])
        ct = tuple(r[...].astype(F32) for r in refs[np_ + nr:np_ + nr + nc])
        outs = refs[np_ + nr + nc:]
        _, vjp = jax.vjp(f, p, rw)
        dp, dr = vjp(ct)

        @pl.when(pl.program_id(0) == 0)
        def _():
            for o in outs[:np_]:
                o[...] = jnp.zeros_like(o)

        for o, d in zip(outs[:np_], dp):
            o[...] += d
        for o, i in zip(outs[np_:], want):
            o[...] = dr[i].astype(o.dtype)

    res = pl.pallas_call(
        body, name=name, grid=(T // bt,),
        in_specs=[_full_spec(p) for p in params] + [_row_spec(bt, e) for e in rows] + [_row_spec(bt, e) for e in cts],
        out_specs=[_full_spec(p) for p in params] + [pl.BlockSpec((bt, rows[i][1]), lambda i_: (i_, 0)) for i in want],
        out_shape=[jax.ShapeDtypeStruct(p.shape, F32) for p in params]
        + [jax.ShapeDtypeStruct((T, rows[i][1]), drow_dtypes[i]) for i in want],
        compiler_params=_cparams(1),
    )(*params, *[e[0] for e in rows], *[e[0] for e in cts])
    res = list(res)
    return res[:np_], res[np_:]


def f_rmsnorm(p, r):
    return (_rms(r[0], p[0]),)


def f_gla_gate_in(p, r):
    w_a2, b_a = p
    z = bdot(r[0], w_a2, "nn") + b_a
    return (_log_sigmoid(z) / GLA_TAU,)


def f_gla_gate_in_fwd(p, r):
    w_a2, b_a = p
    z = _dot_raw(r[0], w_a2, "nn") + b_a
    return (_log_sigmoid(z) / GLA_TAU,)


def f_s5_act(p, r):
    (dsk,) = p
    ycp, u = r
    return (_gelu(ycp + dsk * u),)


def f_glu_res(p, r):
    vg, h = r
    return (vg[:, :D] * _sigmoid(vg[:, D:]) + h,)


def f_glu(p, r):
    vg = r[0]
    return (vg[:, :D] * _sigmoid(vg[:, D:]),)


def loss_head(h, g, target, *, name):
    T = h.shape[0]
    bt = min(T, 256)

    def lossf(g_, h_, t_):
        e = _rms(h_, g_) - t_
        return (0.5 / D) * jnp.sum(e * e)

    def body(g_ref, h_ref, t_ref, loss_ref, dg_ref, dh_ref):
        @pl.when(pl.program_id(0) == 0)
        def _():
            loss_ref[...] = jnp.zeros_like(loss_ref)
            dg_ref[...] = jnp.zeros_like(dg_ref)

        val, vjp = jax.vjp(lossf, g_ref[...], h_ref[...], t_ref[...])
        dg, dh, _ = vjp(jnp.ones((), F32))
        loss_ref[...] += jnp.full(loss_ref.shape, val, F32)
        dg_ref[...] += dg
        dh_ref[...] = dh

    row = pl.BlockSpec((bt, D), lambda i: (i, 0))
    one = pl.BlockSpec((1, D), lambda i: (0, 0))
    loss, dg, dh = pl.pallas_call(
        body, name=name, grid=(T // bt,), in_specs=[one, row, row],
        out_specs=[pl.BlockSpec((1, LANE), lambda i: (0, 0)), one, row],
        out_shape=[jax.ShapeDtypeStruct((1, LANE), F32), jax.ShapeDtypeStruct((1, D), F32),
                   jax.ShapeDtypeStruct((T, D), F32)],
        compiler_params=_cparams(1),
    )(g, h, target)
    return loss[0, 0], dg, dh


def _gla_consts():
    r, c = _iota2((CH, CH), 0), _iota2((CH, CH), 1)
    return (r >= c).astype(F32), (r <= c).astype(F32), r >= c


def _gla_chunk(q, k, v, la, st, consts, dot, cdl):
    L, Lt, tril = consts
    lc = cdl(L, Lt, la)
    lend = lc[CH - 1:CH, :]
    e, ei = jnp.exp(lc), jnp.exp(-lc)
    qs = q * (GLA_DK ** -0.5)
    qf, kf, qb, kb = qs * e, k * ei, qs * ei, k * e
    sc = jnp.where(tril, dot(qf, kf, "nt"), dot(qb, kb, "nt"))
    o = dot(sc, v, "nn") + dot(qf, st, "nt")
    kd = k * jnp.exp(lend - lc)
    st_new = st * jnp.exp(lend) + dot(v, kd, "tn")
    return o, st_new


def _gla_block(q, k, v, la, st, nc, dot, cdl):
    consts = _gla_consts()
    outs = []
    for c in range(nc):
        sl = slice(c * CH, (c + 1) * CH)
        o, st = _gla_chunk(q[sl], k[sl], v[sl], la[sl], st, consts, dot, cdl)
        outs.append(o)
    return jnp.concatenate(outs, axis=0), st


_GLA_HP = 2


def _gla_specs(rows, rev, nb):
    t = (lambda j: nb - 1 - j) if rev else (lambda j: j)
    hp, ng = _GLA_HP, GLA_H // _GLA_HP
    q = pl.BlockSpec((rows, hp * GLA_DK), lambda h, j: (t(j), h))
    k = pl.BlockSpec((rows, hp * GLA_DK), lambda h, j: (t(j), ng + h))
    v = pl.BlockSpec((rows, hp * GLA_DV), lambda h, j: (t(j), ng + h))
    la = pl.BlockSpec((rows, hp * GLA_DK), lambda h, j: (t(j), h))
    ss = pl.BlockSpec((None, hp, GLA_DV, GLA_DK), lambda h, j: (t(j), h, 0, 0))
    o = pl.BlockSpec((rows, hp * GLA_DV), lambda h, j: (t(j), h))
    r = pl.BlockSpec((rows, hp * GLA_DV), lambda h, j: (t(j), 2 * ng + h))
    g = pl.BlockSpec((1, hp * GLA_DV), lambda h, j: (0, h))
    return q, k, v, la, ss, o, r, g


def _gla_heads(q, k, v, la, r, ng, sts, nc, dot, cdl):
    outs, new = [], []
    for i in range(_GLA_HP):
        kk, vv = slice(i * GLA_DK, (i + 1) * GLA_DK), slice(i * GLA_DV, (i + 1) * GLA_DV)
        o, st = _gla_block(q[:, kk], k[:, kk], v[:, vv], la[:, kk], sts[i], nc, dot, cdl)
        outs.append(_rms(o, ng[:, vv]) * _silu(r[:, vv]))
        new.append(st)
    return jnp.concatenate(outs, axis=1), tuple(new)


def gla_scan_fwd(proj, la, ng, *, nc, name):
    T = proj.shape[0]
    rows = min(T, nc * CH)
    nc = rows // CH
    per = 2 if T % (2 * rows) == 0 else 1
    nb = T // (per * rows)
    q_s, k_s, v_s, la_s, _, o_s, r_s, g_s = _gla_specs(per * rows, False, nb)
    ss_s = pl.BlockSpec((per, _GLA_HP, GLA_DV, GLA_DK), lambda h, j: (j, h, 0, 0))

    def body(q_ref, k_ref, v_ref, la_ref, r_ref, g_ref, o_ref, ss_ref, st_ref):
        @pl.when(pl.program_id(1) == 0)
        def _():
            st_ref[...] = jnp.zeros_like(st_ref)

        sts = tuple(st_ref[i] for i in range(_GLA_HP))
        for p in range(per):
            sl = slice(p * rows, (p + 1) * rows)
            for i in range(_GLA_HP):
                ss_ref[p, i] = sts[i]
            o, sts = _gla_heads(q_ref[sl, :], k_ref[sl, :], v_ref[sl, :], la_ref[sl, :], r_ref[sl, :], g_ref[...], sts, nc,
                                _dot_raw, lambda c, ct, x: _hdot(c, x))
            o_ref[sl, :] = o.astype(o_ref.dtype)
        for i in range(_GLA_HP):
            st_ref[i] = sts[i]

    return pl.pallas_call(
        body, name=name, grid=(GLA_H // _GLA_HP, nb), in_specs=[q_s, k_s, v_s, la_s, r_s, g_s], out_specs=[o_s, ss_s],
        out_shape=[jax.ShapeDtypeStruct((T, GLA_VD), BF16), jax.ShapeDtypeStruct((per * nb, GLA_H, GLA_DV, GLA_DK), F32)],
        scratch_shapes=[pltpu.VMEM((_GLA_HP, GLA_DV, GLA_DK), F32)], compiler_params=_cparams(2),
    )(proj, proj, proj, la, proj, ng)


def gla_scan_bwd(proj, la, ng, ss, do, *, nc, name):
    T = proj.shape[0]
    rows = min(T, nc * CH)
    nc = rows // CH
    nb = T // rows
    q_s, k_s, v_s, la_s, ss_s, o_s, r_s, g_s = _gla_specs(rows, True, nb)
    t = lambda j: nb - 1 - j
    dqk_s = pl.BlockSpec((rows, _GLA_HP * GLA_DK), lambda h, j: (t(j), h))

    def body(q_ref, k_ref, v_ref, la_ref, r_ref, g_ref, ss_ref, do_ref,
             dq_ref, dk_ref, dv_ref, dla_ref, dr_ref, dg_ref, dst_ref):
        @pl.when(pl.program_id(1) == 0)
        def _():
            dst_ref[...] = jnp.zeros_like(dst_ref)
            dg_ref[...] = jnp.zeros_like(dg_ref)

        fn = lambda q, k, v, la_, r, g, *sts: _gla_heads(q, k, v, la_, r, g, sts, nc, bdot, cdot_left)
        _, vjp = jax.vjp(fn, q_ref[...], k_ref[...], v_ref[...], la_ref[...], r_ref[...], g_ref[...],
                         *[ss_ref[i] for i in range(_GLA_HP)])
        dq, dk, dv, dla, dr, dg, *dsts = vjp((do_ref[...], tuple(dst_ref[i] for i in range(_GLA_HP))))
        dq_ref[...] = dq.astype(dq_ref.dtype)
        dk_ref[...] = dk.astype(dk_ref.dtype)
        dv_ref[...] = dv.astype(dv_ref.dtype)
        dla_ref[...] = dla
        dr_ref[...] = dr.astype(dr_ref.dtype)
        dg_ref[...] += dg
        for i in range(_GLA_HP):
            dst_ref[i] = dsts[i]

    return pl.pallas_call(
        body, name=name, grid=(GLA_H // _GLA_HP, nb), in_specs=[q_s, k_s, v_s, la_s, r_s, g_s, ss_s, o_s],
        out_specs=[dqk_s, dqk_s, o_s, dqk_s, o_s, g_s],
        out_shape=[jax.ShapeDtypeStruct((T, GLA_QK), BF16), jax.ShapeDtypeStruct((T, GLA_QK), BF16),
                   jax.ShapeDtypeStruct((T, GLA_VD), BF16), jax.ShapeDtypeStruct((T, GLA_QK), F32),
                   jax.ShapeDtypeStruct((T, GLA_VD), BF16), jax.ShapeDtypeStruct((1, GLA_VD), F32)],
        scratch_shapes=[pltpu.VMEM((_GLA_HP, GLA_DV, GLA_DK), F32)], compiler_params=_cparams(2),
    )(proj, proj, proj, la, proj, ng, ss, do)


_CONV_W = 512
_CONV_OFF = SSD_DI // _CONV_W


def _conv_pre(x, prev8, w_ref, b_ref):
    bt = x.shape[0]
    ext = jnp.concatenate([prev8, x], axis=0)
    shifted = []
    for j in range(SSD_K):
        s = SSD_K - 1 - j
        shifted.append(x if s == 0 else pltpu.roll(ext, s, 0)[8:8 + bt])
    pre = b_ref[...] + sum(w_ref[j:j + 1, :] * shifted[j] for j in range(SSD_K))
    return pre, shifted


def ssd_conv_fwd(proj, w, b, *, name):
    T = proj.shape[0]
    bt = min(T, 512)
    nb = T // bt

    def body(x_ref, w_ref, b_ref, o_ref, carry_ref):
        @pl.when(pl.program_id(1) == 0)
        def _():
            carry_ref[...] = jnp.zeros_like(carry_ref)

        x = x_ref[...]
        pre, _ = _conv_pre(x, carry_ref[...], w_ref, b_ref)
        o_ref[...] = _silu(pre)
        carry_ref[...] = x[bt - 8:, :]

    return pl.pallas_call(
        body, name=name, grid=(SSD_CONV // _CONV_W, nb),
        in_specs=[pl.BlockSpec((bt, _CONV_W), lambda c, t: (t, _CONV_OFF + c)),
                  pl.BlockSpec((SSD_K, _CONV_W), lambda c, t: (0, c)),
                  pl.BlockSpec((1, _CONV_W), lambda c, t: (0, c))],
        out_specs=pl.BlockSpec((bt, _CONV_W), lambda c, t: (t, c)),
        out_shape=jax.ShapeDtypeStruct((T, SSD_CONV), F32),
        scratch_shapes=[pltpu.VMEM((8, _CONV_W), F32)], compiler_params=_cparams(2),
    )(proj, w, b)


def ssd_conv_bwd(proj, w, b, dout, col0, *, name):
    T, ncols = dout.shape
    bt = min(T, 512)
    nb = T // bt
    r8 = bt // 8
    c0 = col0 // _CONV_W

    def body(x_ref, xp_ref, w_ref, b_ref, do_ref, dx_ref, dw_ref, db_ref, carry_ref):
        t = pl.program_id(1)

        @pl.when(t == 0)
        def _():
            carry_ref[...] = jnp.zeros_like(carry_ref)
            dw_ref[...] = jnp.zeros_like(dw_ref)
            db_ref[...] = jnp.zeros_like(db_ref)

        x = x_ref[...]
        prev8 = jnp.where(t == nb - 1, 0.0, xp_ref[...])
        pre, shifted = _conv_pre(x, prev8, w_ref, b_ref)
        sg = _sigmoid(pre)
        dpre = do_ref[...] * (sg * (1.0 + pre * (1.0 - sg)))
        ext = jnp.concatenate([dpre, carry_ref[...]], axis=0)
        dx = w_ref[SSD_K - 1:SSD_K, :] * dpre
        for j in range(SSD_K - 1):
            s = SSD_K - 1 - j
            dx = dx + w_ref[j:j + 1, :] * pltpu.roll(ext, bt + 8 - s, 0)[:bt]
        dx_ref[...] = dx.astype(dx_ref.dtype)
        dw_ref[...] += jnp.concatenate([jnp.sum(dpre * shifted[j], axis=0, keepdims=True) for j in range(SSD_K)], axis=0)
        db_ref[...] += jnp.sum(dpre, axis=0, keepdims=True)
        carry_ref[...] = dpre[:8, :]

    rt = lambda t: nb - 1 - t
    return pl.pallas_call(
        body, name=name, grid=(ncols // _CONV_W, nb),
        in_specs=[pl.BlockSpec((bt, _CONV_W), lambda c, t: (rt(t), _CONV_OFF + c0 + c)),
                  pl.BlockSpec((8, _CONV_W), lambda c, t: (jnp.maximum(rt(t) * r8 - 1, 0), _CONV_OFF + c0 + c)),
                  pl.BlockSpec((SSD_K, _CONV_W), lambda c, t: (0, c0 + c)),
                  pl.BlockSpec((1, _CONV_W), lambda c, t: (0, c0 + c)),
                  pl.BlockSpec((bt, _CONV_W), lambda c, t: (rt(t), c))],
        out_specs=[pl.BlockSpec((bt, _CONV_W), lambda c, t: (rt(t), c)),
                   pl.BlockSpec((SSD_K, _CONV_W), lambda c, t: (0, c)),
                   pl.BlockSpec((1, _CONV_W), lambda c, t: (0, c))],
        out_shape=[jax.ShapeDtypeStruct((T, ncols), BF16), jax.ShapeDtypeStruct((SSD_K, ncols), F32),
                   jax.ShapeDtypeStruct((1, ncols), F32)],
        scratch_shapes=[pltpu.VMEM((8, _CONV_W), F32)], compiler_params=_cparams(2),
    )(proj, proj, w, b, dout)


_SSD_U = 2 * CH


def _ssd_unit(xs, bm, cm, dtraw, dtb, alog, dsk, hp, g, dot, cdl, cdr):
    U, P2 = _SSD_U, 2 * SSD_HD
    r, c = _iota2((U, U), 0), _iota2((U, U), 1)
    same = (r // CH) == (c // CH)
    Lb = (same & (r >= c)).astype(F32)
    Ub = (same & (r <= c)).astype(F32)
    lane = _iota2((1, U), 1)
    lo_lane = _iota2((1, P2), 1) < SSD_HD
    lo_sub = _iota2((P2, 1), 0) < SSD_HD
    diag2 = (_iota2((CH, P2), 0) == (_iota2((CH, P2), 1) % CH)).astype(F32)

    dt = _softplus(dtraw + dtb)
    da = dt * (-jnp.exp(alog))
    cum = cdl(Lb, Ub, da)
    ys = []
    new_hp = []
    for pr in range(2):
        xs_p = xs[:, pr * P2:(pr + 1) * P2]
        cols, dts, dks = [], [], []
        for jj in range(2):
            oh_l = (lane == g * (SSD_H // SSD_G) + 2 * pr + jj).astype(F32)
            cols.append(jnp.sum(cum * oh_l, axis=1, keepdims=True))
            dts.append(jnp.sum(dt * oh_l, axis=1, keepdims=True))
            dks.append(jnp.sum(dsk * oh_l, axis=1, keepdims=True))
        dsk_p = jnp.where(lo_lane, dks[0], dks[1])
        h = hp[pr]
        yc = []
        for ci in range(2):
            sl = slice(ci * CH, (ci + 1) * CH)
            xs_c, bm_c, cm_c = xs_p[sl], bm[sl], cm[sl]
            col = jnp.where(lo_lane, cols[0][sl], cols[1][sl])
            dtc = jnp.where(lo_lane, dts[0][sl], dts[1][sl])
            row = jnp.sum(diag2 * col, axis=0, keepdims=True)
            dtrow = jnp.sum(diag2 * dtc, axis=0, keepdims=True)
            cb = dot(cm_c, jnp.concatenate([bm_c, bm_c], axis=0), "nt")
            mix = cb * jnp.exp(-jnp.abs(col - row)) * dtrow
            xbd = jnp.concatenate([jnp.where(lo_lane, xs_c, 0.0), jnp.where(lo_lane, 0.0, xs_c)], axis=0)
            y_intra = dot(mix, xbd, "nn")
            ce = jnp.where(lo_lane, cols[0][ci * CH + CH - 1:ci * CH + CH, :], cols[1][ci * CH + CH - 1:ci * CH + CH, :])
            y_inter = dot(cm_c, h, "nt") * jnp.exp(col)
            xw = xs_c * (dtc * jnp.exp(ce - col))
            ce_s = [cols[jj][ci * CH + CH - 1:ci * CH + CH, :] for jj in range(2)]
            a_p = jnp.where(lo_sub, jnp.exp(ce_s[0]), jnp.exp(ce_s[1]))
            h = a_p * h + dot(xw, bm_c, "tn")
            yc.append(y_intra + y_inter + dsk_p * xs_c)
        ys.append(jnp.concatenate(yc, axis=0))
        new_hp.append(h)
    return jnp.concatenate(ys, axis=1), tuple(new_hp)


def _ssd_block(xs, bm, cm, dtraw, z, dtb, alog, dsk, ng, hp, g, nu, dot, cdl, cdr):
    outs = []
    for u in range(nu):
        sl = slice(u * _SSD_U, (u + 1) * _SSD_U)
        y, hp = _ssd_unit(xs[sl], bm[sl], cm[sl], dtraw[sl], dtb, alog, dsk, hp, g, dot, cdl, cdr)
        outs.append(y)
    return _rms(jnp.concatenate(outs, axis=0) * _silu(z), ng), hp


def _ssd_specs(rows, rev, nb):
    t = (lambda j: nb - 1 - j) if rev else (lambda j: j)
    gw = SSD_DI // SSD_G
    xs = pl.BlockSpec((rows, gw), lambda j, g: (t(j), g))
    bm = pl.BlockSpec((rows, SSD_N), lambda j, g: (t(j), SSD_DI // SSD_N + g))
    cm = pl.BlockSpec((rows, SSD_N), lambda j, g: (t(j), (SSD_DI + SSD_GN) // SSD_N + g))
    dtr = pl.BlockSpec((rows, LANE), lambda j, g: (t(j), (SSD_DI + SSD_CONV) // LANE))
    par = pl.BlockSpec((1, LANE), lambda j, g: (0, 0))
    hs = pl.BlockSpec((None, None, 2, 2 * SSD_HD, SSD_N), lambda j, g: (t(j), g, 0, 0, 0))
    y = pl.BlockSpec((rows, gw), lambda j, g: (t(j), g))
    ng = pl.BlockSpec((1, gw), lambda j, g: (0, g))
    return xs, bm, cm, dtr, par, hs, y, ng


def ssd_scan_fwd(xbc, proj, dtb, alog, dsk, ng, *, nu, name):
    T = xbc.shape[0]
    rows = min(T, nu * _SSD_U)
    nu = rows // _SSD_U
    nb = T // rows
    xs_s, bm_s, cm_s, dt_s, par_s, hs_s, y_s, ng_s = _ssd_specs(rows, False, nb)

    def body(xs_ref, bm_ref, cm_ref, dt_ref, z_ref, dtb_ref, al_ref, dsk_ref, ng_ref, y_ref, hs_ref, h_ref):
        g = pl.program_id(1)

        @pl.when(pl.program_id(0) == 0)
        def _():
            h_ref[g] = jnp.zeros(h_ref.shape[1:], F32)

        hs_ref[...] = h_ref[g]
        hp = (h_ref[g, 0], h_ref[g, 1])
        y, hp = _ssd_block(xs_ref[...], bm_ref[...], cm_ref[...], dt_ref[...], z_ref[...], dtb_ref[...], al_ref[...],
                           dsk_ref[...], ng_ref[...], hp, g, nu, _dot_raw, lambda c, ct, x: _hdot(c, x), lambda x, c, ct: _hdot(x, c))
        y_ref[...] = y.astype(y_ref.dtype)
        h_ref[g, 0] = hp[0]
        h_ref[g, 1] = hp[1]

    return pl.pallas_call(
        body, name=name, grid=(nb, SSD_G), in_specs=[xs_s, bm_s, cm_s, dt_s, y_s, par_s, par_s, par_s, ng_s],
        out_specs=[y_s, hs_s],
        out_shape=[jax.ShapeDtypeStruct((T, SSD_DI), BF16), jax.ShapeDtypeStruct((nb, SSD_G, 2, 2 * SSD_HD, SSD_N), F32)],
        scratch_shapes=[pltpu.VMEM((SSD_G, 2, 2 * SSD_HD, SSD_N), F32)], compiler_params=_cparams(2),
    )(xbc, xbc, xbc, proj, proj, dtb, alog, dsk, ng)


def ssd_scan_bwd(xbc, proj, dtb, alog, dsk, ng, hs, dy, *, nu, name):
    T = xbc.shape[0]
    rows = min(T, nu * _SSD_U)
    nu = rows // _SSD_U
    nb = T // rows
    xs_s, bm_s, cm_s, dt_s, par_s, hs_s, y_s, ng_s = _ssd_specs(rows, True, nb)
    gw = SSD_DI // SSD_G
    dng_s = pl.BlockSpec((1, SSD_DI), lambda j, g: (0, 0))
    t = lambda j: nb - 1 - j
    n_s = pl.BlockSpec((rows, SSD_N), lambda j, g: (t(j), g))
    ddt_s = pl.BlockSpec((rows, LANE), lambda j, g: (t(j), 0))

    def body(xs_ref, bm_ref, cm_ref, dt_ref, z_ref, dtb_ref, al_ref, dsk_ref, ng_ref, hs_ref, dy_ref,
             dxs_ref, dbm_ref, dcm_ref, ddt_ref, ddtb_ref, dal_ref, ddsk_ref, dz_ref, dng_ref, dh_ref):
        j, g = pl.program_id(0), pl.program_id(1)

        @pl.when(j == 0)
        def _():
            dh_ref[g] = jnp.zeros(dh_ref.shape[1:], F32)

        @pl.when((j == 0) & (g == 0))
        def _():
            ddtb_ref[...] = jnp.zeros_like(ddtb_ref)
            dal_ref[...] = jnp.zeros_like(dal_ref)
            ddsk_ref[...] = jnp.zeros_like(ddsk_ref)
            dng_ref[...] = jnp.zeros_like(dng_ref)

        @pl.when(g == 0)
        def _():
            ddt_ref[...] = jnp.zeros_like(ddt_ref)

        fn = lambda xs, bm, cm, dtr, z, dtb_, al, dsk_, ng_, h0, h1: _ssd_block(
            xs, bm, cm, dtr, z, dtb_, al, dsk_, ng_, (h0, h1), g, nu, bdot, cdot_left, cdot_right)
        _, vjp = jax.vjp(fn, xs_ref[...], bm_ref[...], cm_ref[...], dt_ref[...], z_ref[...], dtb_ref[...], al_ref[...],
                         dsk_ref[...], ng_ref[...], hs_ref[0], hs_ref[1])
        dxs, dbm, dcm, ddt, dz, ddtb, dal, ddsk, dng, dh0, dh1 = vjp((dy_ref[...], (dh_ref[g, 0], dh_ref[g, 1])))
        dz_ref[...] = dz.astype(dz_ref.dtype)
        lanes = pl.ds(pl.multiple_of(g * gw, gw), gw)
        dng_ref[:, lanes] = dng_ref[:, lanes] + dng
        dxs_ref[...] = dxs
        dbm_ref[...] = dbm
        dcm_ref[...] = dcm
        ddt_ref[...] += ddt
        ddtb_ref[...] += ddtb
        dal_ref[...] += dal
        ddsk_ref[...] += ddsk
        dh_ref[g, 0] = dh0
        dh_ref[g, 1] = dh1

    return pl.pallas_call(
        body, name=name, grid=(nb, SSD_G), in_specs=[xs_s, bm_s, cm_s, dt_s, y_s, par_s, par_s, par_s, ng_s, hs_s, y_s],
        out_specs=[y_s, n_s, n_s, ddt_s, par_s, par_s, par_s, y_s, dng_s],
        out_shape=[jax.ShapeDtypeStruct((T, SSD_DI), F32), jax.ShapeDtypeStruct((T, SSD_GN), F32),
                   jax.ShapeDtypeStruct((T, SSD_GN), F32), jax.ShapeDtypeStruct((T, LANE), F32),
                   jax.ShapeDtypeStruct((1, LANE), F32), jax.ShapeDtypeStruct((1, LANE), F32),
                   jax.ShapeDtypeStruct((1, LANE), F32), jax.ShapeDtypeStruct((T, SSD_DI), BF16),
                   jax.ShapeDtypeStruct((1, SSD_DI), F32)],
        scratch_shapes=[pltpu.VMEM((SSD_G, 2, 2 * SSD_HD, SSD_N), F32)], compiler_params=_cparams(2),
    )(xbc, xbc, xbc, proj, proj, dtb, alog, dsk, ng, hs, dy)


def _s5_param_f(log_dt, a_re, a_im, bre_t, bim_t, cim, cdl):
    n = S5_NG * S5_GS
    r, c = _iota2((n, S5_NG), 0), _iota2((n, S5_NG), 1)
    E = ((r // S5_GS) == c).astype(F32)
    rt, ct = _iota2((S5_NG, n), 0), _iota2((S5_NG, n), 1)
    Et = ((ct // S5_GS) == rt).astype(F32)
    step = jnp.exp(log_dt)
    mag = jnp.exp(step * a_re)
    abr = mag * jnp.cos(step * a_im)
    abi = mag * jnp.sin(step * a_im)
    den = a_re * a_re + a_im * a_im
    nr, ni = abr - 1.0, abi
    fr = (nr * a_re + ni * a_im) / den
    fi = (ni * a_re - nr * a_im) / den
    Fr, Fi = cdl(E, Et, fr), cdl(E, Et, fi)
    bbr = Fr * bre_t - Fi * bim_t
    bbi = Fr * bim_t + Fi * bre_t
    return abr, abi, bbr, bbi, -cim


def _whole(a):
    return pl.BlockSpec(a.shape, lambda: (0,) * a.ndim)


def s5_param_fwd(args, *, name):
    def body(*refs):
        res = _s5_param_f(*[r[...] for r in refs[:6]], lambda c, ct, x: _hdot(c, x))
        for o, v in zip(refs[6:], res):
            o[...] = v

    shapes = [(S5_NG, S5_P), (S5_NG, S5_P)] + [(S5_NG * S5_GS, S5_P)] * 3
    return pl.pallas_call(
        body, name=name, in_specs=[_whole(a) for a in args], out_specs=[pl.BlockSpec(s, lambda: (0, 0)) for s in shapes],
        out_shape=[jax.ShapeDtypeStruct(s, F32) for s in shapes],
        compiler_params=pltpu.CompilerParams(vmem_limit_bytes=VMEM_LIMIT),
    )(*args)


def s5_param_bwd(args, cts, *, name):
    def body(*refs):
        fn = lambda *a: _s5_param_f(*a, cdot_left)
        _, vjp = jax.vjp(fn, *[r[...] for r in refs[:6]])
        grads = vjp(tuple(r[...] for r in refs[6:11]))
        for o, v in zip(refs[11:], grads):
            o[...] = v

    return pl.pallas_call(
        body, name=name, in_specs=[_whole(a) for a in list(args) + list(cts)],
        out_specs=[_whole(a) for a in args], out_shape=[jax.ShapeDtypeStruct(a.shape, F32) for a in args],
        compiler_params=pltpu.CompilerParams(vmem_limit_bytes=VMEM_LIMIT),
    )(*args, *cts)


_S5_W = S5_BLK * S5_P
_S5_BT = 1024


def _cmul_add(xr, xi, pr, pi, sr, si):
    return xr + (pr * sr - pi * si), xi + (pr * si + pi * sr)


def _s5_powers(ar, ai):
    pw = [(ar, ai)]
    for _ in range(7):
        qr, qi = pw[-1]
        pw.append((qr * ar - qi * ai, qr * ai + qi * ar))
    return pw


def s5_scan_fwd(u, wb, a_re, a_im, wc, *, name):
    T = u.shape[0]
    bt = min(T, _S5_BT)
    nb = T // bt

    def body(u_ref, wb_ref, ar_ref, ai_ref, wc_ref, x_ref, y_ref, bu_ref, carry_ref):
        @pl.when(pl.program_id(1) == 0)
        def _():
            carry_ref[...] = jnp.zeros_like(carry_ref)

        bu_ref[...] = _dot_raw(u_ref[...], wb_ref[...], "nn")
        ar, ai = ar_ref[...], ai_ref[...]
        pw = _s5_powers(ar, ai)
        pwr = jnp.concatenate([p[0] for p in pw], axis=0)
        pwi = jnp.concatenate([p[1] for p in pw], axis=0)
        rin = _iota2((8, _S5_W), 0)
        cr, ci = carry_ref[0:1, :], carry_ref[1:2, :]
        for t in range(bt // 8):
            sl = slice(8 * t, 8 * t + 8)
            xr, xi = bu_ref[sl, :_S5_W], bu_ref[sl, _S5_W:]
            for s in (1, 2, 4):
                m = rin >= s
                sr = jnp.where(m, pltpu.roll(xr, s, 0), 0.0)
                si = jnp.where(m, pltpu.roll(xi, s, 0), 0.0)
                xr, xi = _cmul_add(xr, xi, *pw[s - 1], sr, si)
            xr, xi = _cmul_add(xr, xi, pwr, pwi, cr, ci)
            x_ref[sl, :_S5_W] = xr
            x_ref[sl, _S5_W:] = xi
            cr, ci = xr[7:8, :], xi[7:8, :]
        carry_ref[0:1, :] = cr
        carry_ref[1:2, :] = ci
        y_ref[...] = _dot_raw(x_ref[...], wc_ref[...], "nn")

    nblk = S5_NG // S5_BLK
    blk = pl.BlockSpec((bt, 2 * _S5_W), lambda g, t: (t, g))
    col = pl.BlockSpec((bt, LANE), lambda g, t: (t, g))
    a_s = pl.BlockSpec((None, 1, _S5_W), lambda g, t: (g, 0, 0))
    wb_s = pl.BlockSpec((None, LANE, 2 * _S5_W), lambda g, t: (g, 0, 0))
    wc_s = pl.BlockSpec((None, 2 * _S5_W, LANE), lambda g, t: (g, 0, 0))
    return pl.pallas_call(
        body, name=name, grid=(nblk, nb), in_specs=[col, wb_s, a_s, a_s, wc_s], out_specs=[blk, col],
        out_shape=[jax.ShapeDtypeStruct((T, nblk * 2 * _S5_W), F32), jax.ShapeDtypeStruct((T, nblk * LANE), F32)],
        scratch_shapes=[pltpu.VMEM((bt, 2 * _S5_W), F32), pltpu.VMEM((8, _S5_W), F32)],
        compiler_params=_cparams(2),
    )(u, wb, a_re, a_im, wc)


def s5_scan_bwd(dy, x, u, wb, a_re, a_im, wc, *, name):
    T = dy.shape[0]
    bt = min(T, _S5_BT)
    nb = T // bt

    def body(dy_ref, x_ref, u_ref, wb_ref, ar_ref, ai_ref, wc_ref, du_ref, dwb_ref, dwc_ref, dar_ref, dai_ref,
             g_ref, lam_ref, carry_ref):
        @pl.when(pl.program_id(1) == 0)
        def _():
            carry_ref[...] = jnp.zeros_like(carry_ref)
            dar_ref[...] = jnp.zeros_like(dar_ref)
            dai_ref[...] = jnp.zeros_like(dai_ref)
            dwb_ref[...] = jnp.zeros_like(dwb_ref)
            dwc_ref[...] = jnp.zeros_like(dwc_ref)

        g_ref[...] = _dot_raw(dy_ref[...], wc_ref[...], "nt")
        pw = _s5_powers(ar_ref[...], -ai_ref[...])
        pwr = jnp.concatenate([p[0] for p in reversed(pw)], axis=0)
        pwi = jnp.concatenate([p[1] for p in reversed(pw)], axis=0)
        rin = _iota2((8, _S5_W), 0)
        cr, ci = carry_ref[0:1, :], carry_ref[1:2, :]
        acc_r = jnp.zeros((8, _S5_W), F32)
        acc_i = jnp.zeros((8, _S5_W), F32)
        for t in reversed(range(bt // 8)):
            sl = slice(8 * t, 8 * t + 8)
            lr, li = g_ref[sl, :_S5_W], g_ref[sl, _S5_W:]
            for s in (1, 2, 4):
                m = rin < 8 - s
                sr = jnp.where(m, pltpu.roll(lr, 8 - s, 0), 0.0)
                si = jnp.where(m, pltpu.roll(li, 8 - s, 0), 0.0)
                lr, li = _cmul_add(lr, li, *pw[s - 1], sr, si)
            lr, li = _cmul_add(lr, li, pwr, pwi, cr, ci)
            lam_ref[sl, :_S5_W] = lr
            lam_ref[sl, _S5_W:] = li
            nr = jnp.where(rin == 7, cr, pltpu.roll(lr, 7, 0))
            ni = jnp.where(rin == 7, ci, pltpu.roll(li, 7, 0))
            xr, xi = x_ref[sl, :_S5_W], x_ref[sl, _S5_W:]
            acc_r = acc_r + (xr * nr + xi * ni)
            acc_i = acc_i + (xr * ni - xi * nr)
            cr, ci = lr[0:1, :], li[0:1, :]
        carry_ref[0:1, :] = cr
        carry_ref[1:2, :] = ci
        dar_ref[...] += jnp.sum(acc_r, axis=0, keepdims=True)
        dai_ref[...] += jnp.sum(acc_i, axis=0, keepdims=True)
        lam = lam_ref[...]
        du_ref[...] = _dot_raw(lam, wb_ref[...], "nt")
        dwb_ref[...] += _dot_raw(u_ref[...], lam, "tn")
        dwc_ref[...] += _dot_raw(x_ref[...], dy_ref[...], "tn")

    nblk = S5_NG // S5_BLK
    blk = pl.BlockSpec((bt, 2 * _S5_W), lambda g, t: (nb - 1 - t, g))
    col = pl.BlockSpec((bt, LANE), lambda g, t: (nb - 1 - t, g))
    a_s = pl.BlockSpec((None, 1, _S5_W), lambda g, t: (g, 0, 0))
    wb_s = pl.BlockSpec((None, LANE, 2 * _S5_W), lambda g, t: (g, 0, 0))
    wc_s = pl.BlockSpec((None, 2 * _S5_W, LANE), lambda g, t: (g, 0, 0))
    return pl.pallas_call(
        body, name=name, grid=(nblk, nb), in_specs=[col, blk, col, wb_s, a_s, a_s, wc_s],
        out_specs=[col, wb_s, wc_s, a_s, a_s],
        out_shape=[jax.ShapeDtypeStruct((T, nblk * LANE), F32), jax.ShapeDtypeStruct((nblk, LANE, 2 * _S5_W), F32),
                   jax.ShapeDtypeStruct((nblk, 2 * _S5_W, LANE), F32), jax.ShapeDtypeStruct((nblk, 1, _S5_W), F32),
                   jax.ShapeDtypeStruct((nblk, 1, _S5_W), F32)],
        scratch_shapes=[pltpu.VMEM((bt, 2 * _S5_W), F32), pltpu.VMEM((bt, 2 * _S5_W), F32), pltpu.VMEM((8, _S5_W), F32)],
        compiler_params=_cparams(2),
    )(dy, x, u, wb, a_re, a_im, wc)


def _norm_bf16(h, g, name):
    return rowwise(f_rmsnorm, [g], [h], [(D, BF16)], bt=512, name=name)[0]


def _norm_bwd(h, g, cts, name):
    n = len(cts) - 1

    def f(p, r):
        y = _rms(r[0], p[0])
        return (y,) * n + (r[0],)

    (dg,), (dh,) = rowwise_vjp(f, [g], [h], cts, [F32], bt=256, name=name)
    return dh, dg


def ffn_fwd(h, g, w_gu, w_down, tag):
    hn = _norm_bf16(h, g, f"{tag}_norm")
    a, gu = ffn_up(hn, w_gu, name=f"{tag}_up")
    h2 = matmul(a, w_down[None], "nn", add=h, name=f"{tag}_down")
    return h2, (h, hn, gu, a)


def ffn_bwd(d, saved, g, w_gu, w_down, tag):
    h, hn, gu, a = saved
    dgu = ffn_dact(d, w_down, gu, name=f"{tag}_dact")
    dwd = matmul(a, d, "tn", name=f"{tag}_dwd")[0]
    dwgu = matmul(hn, dgu, "tn", name=f"{tag}_dwgu")[0]
    dh, dg = matmul_nt_norm_bwd(dgu, w_gu, h, g, d, name=f"{tag}_dhn")
    return dh, dg, dwgu, dwd


_GLA_NC = 4
_SSD_NU = 4


def gla_fwd(h, gm, w_in, w_a2, b_a, ng, w_out, tag):
    hn = _norm_bf16(h, gm, f"{tag}_norm")
    proj = matmul(hn, w_in[None], "nn", name=f"{tag}_in")
    alow = (proj, LANE, 2 * (GLA_QK + GLA_VD) // LANE)
    la = rowwise(f_gla_gate_in_fwd, [w_a2, b_a], [alow], [(GLA_QK, F32)], bt=512, name=f"{tag}_gate")[0]
    og, ss = gla_scan_fwd(proj, la, ng, nc=_GLA_NC, name=f"{tag}_scan")
    h2 = matmul(og, w_out[None], "nn", add=h, name=f"{tag}_proj")
    return h2, (h, hn, proj, la, ss, og)


def gla_bwd(d, saved, gm, w_in, w_a2, b_a, ng, w_out, tag):
    h, hn, proj, la, ss, og = saved
    dog = matmul(d, w_out[None], "nt", name=f"{tag}_dog")
    dwout = matmul(og, d, "tn", name=f"{tag}_dwout")[0]
    dq, dk, dv, dla, dr, dng = gla_scan_bwd(proj, la, ng, ss, dog, nc=_GLA_NC, name=f"{tag}_dscan")
    alow = (proj, LANE, 2 * (GLA_QK + GLA_VD) // LANE)
    (dwa2, dba), (dalow,) = rowwise_vjp(f_gla_gate_in, [w_a2, b_a], [alow], [dla], [BF16], bt=512, name=f"{tag}_dgate")
    dproj = jnp.concatenate([dq, dk, dv, dr, dalow], axis=1)
    dwin = matmul(hn, dproj, "tn", name=f"{tag}_dwin")[0]
    dh, dgm = matmul_nt_norm_bwd(dproj, w_in, h, gm, d, name=f"{tag}_dhn")
    return dh, dgm, dwin, dwa2[:GLA_RANK], dba, dng, dwout


def ssd_fwd(h, gm, w_in, conv_w, conv_b, dtb, alog, dsk, ng, w_out, tag):
    hn = _norm_bf16(h, gm, f"{tag}_norm")
    proj = matmul(hn, w_in[None], "nn", name=f"{tag}_in")
    xbc = ssd_conv_fwd(proj, conv_w, conv_b, name=f"{tag}_conv")
    yg, hs = ssd_scan_fwd(xbc, proj, dtb, alog, dsk, ng, nu=_SSD_NU, name=f"{tag}_scan")
    h2 = matmul(yg, w_out[None], "nn", add=h, name=f"{tag}_proj")
    return h2, (h, hn, proj, xbc, hs, yg)


def ssd_bwd(d, saved, gm, w_in, conv_w, conv_b, dtb, alog, dsk, ng, w_out, tag):
    h, hn, proj, xbc, hs, yg = saved
    dyg = matmul(d, w_out[None], "nt", name=f"{tag}_dyg")
    dwout = matmul(yg, d, "tn", name=f"{tag}_dwout")[0]
    dxs, dbm, dcm, ddt, ddtb, dal, ddsk, dz, dng = ssd_scan_bwd(xbc, proj, dtb, alog, dsk, ng, hs, dyg, nu=_SSD_NU,
                                                               name=f"{tag}_dscan")
    parts = [ssd_conv_bwd(proj, conv_w, conv_b, dout, col0, name=f"{tag}_dconv{k}")
             for k, (dout, col0) in enumerate(((dxs, 0), (dbm, SSD_DI), (dcm, SSD_DI + SSD_GN)))]
    dcw = jnp.concatenate([p[1] for p in parts], axis=1)
    dcb = jnp.concatenate([p[2] for p in parts], axis=1)
    dproj = jnp.concatenate([dz] + [p[0] for p in parts] + [ddt.astype(BF16)], axis=1)
    dwin = matmul(hn, dproj, "tn", name=f"{tag}_dwin")[0]
    dh, dgm = matmul_nt_norm_bwd(dproj, w_in, h, gm, d, name=f"{tag}_dhn")
    return (dh, dgm, dwin, dcw, dcb, ddtb[:, :SSD_H], dal[:, :SSD_H], ddsk[:, :SSD_H], dng, dwout)


_S5_NB = S5_NG // S5_BLK


def _s5_param_args(log_dt, a_re, a_im, b_re, b_im, c_im):
    n = S5_NG * S5_GS
    tr = lambda b: jnp.transpose(b, (0, 2, 1)).reshape(n, S5_P)
    return [log_dt.reshape(S5_NG, 1), a_re, a_im, tr(b_re), tr(b_im), c_im.reshape(n, S5_P)]


def _s5_blockdiag(t):
    nb, gl, a, b = t.shape
    eye = jnp.eye(gl, dtype=t.dtype)
    return (t[:, :, :, None, :] * eye[None, :, None, :, None]).reshape(nb, gl * a, gl * b)


def _s5_diag(t, a, b):
    nb = t.shape[0]
    gl = t.shape[1] // a
    eye = jnp.eye(gl, dtype=t.dtype)
    return jnp.sum(t.reshape(nb, gl, a, gl, b) * eye[None, :, None, :, None], axis=3)


def _s5_weights(bbr, bbi, c_re, cneg):
    sh = (_S5_NB, S5_BLK, S5_GS, S5_P)
    wb = jnp.concatenate([_s5_blockdiag(bbr.reshape(sh)), _s5_blockdiag(bbi.reshape(sh))], axis=2)
    tr = lambda cc: jnp.transpose(cc.reshape(sh), (0, 1, 3, 2))
    wc = jnp.concatenate([_s5_blockdiag(tr(c_re)), _s5_blockdiag(tr(cneg))], axis=1)
    return wb, wc


def s5_fwd(h, gm, prm, dsk, w_glu, tag):
    log_dt, a_re, a_im, b_re, b_im, c_re, c_im = prm
    hn = rowwise(f_rmsnorm, [gm], [h], [(D, F32)], bt=512, name=f"{tag}_norm")[0]
    pargs = _s5_param_args(log_dt, a_re, a_im, b_re, b_im, c_im)
    abr, abi, bbr, bbi, cneg = s5_param_fwd(pargs, name=f"{tag}_param")
    wb, wc = _s5_weights(bbr, bbi, c_re.reshape(S5_NG * S5_GS, S5_P), cneg)
    ar, ai = abr.reshape(_S5_NB, 1, _S5_W), abi.reshape(_S5_NB, 1, _S5_W)
    wb, wc = wb.astype(BF16), wc.astype(BF16)
    x, ycp = s5_scan_fwd(hn, wb, ar, ai, wc, name=f"{tag}_scan")
    yg = rowwise(f_s5_act, [dsk], [ycp, hn], [(D, BF16)], bt=512, name=f"{tag}_act")[0]
    vg = matmul(yg, w_glu[None], "nn", name=f"{tag}_glu")
    h2 = rowwise(f_glu_res, [], [vg, h], [(D, F32)], bt=512, name=f"{tag}_out")[0]
    return h2, (h, hn, pargs, wb, wc, ar, ai, x, ycp, yg, vg)


def s5_bwd(d, saved, gm, dsk, w_glu, tag):
    h, hn, pargs, wb, wc, ar, ai, x, ycp, yg, vg = saved
    _, (dvg,) = rowwise_vjp(f_glu, [], [vg], [d], [BF16], bt=256, name=f"{tag}_dout")
    dwglu = matmul(yg, dvg, "tn", name=f"{tag}_dwglu")[0]
    dyg = matmul(dvg, w_glu[None], "nt", name=f"{tag}_dyg")
    (ddsk,), (dycp, dhn1) = rowwise_vjp(f_s5_act, [dsk], [ycp, hn], [dyg], [F32, F32], bt=256, name=f"{tag}_dact")
    dhn2, dwb, dwc, dar, dai = s5_scan_bwd(dycp, x, hn, wb, ar, ai, wc, name=f"{tag}_dscan")
    dh, dgm = _norm_bwd(h, gm, [dhn1, dhn2, d], f"{tag}_dnorm")
    n = S5_NG * S5_GS
    half = S5_BLK * S5_P
    d_bbr = _s5_diag(dwb[:, :, :half], S5_GS, S5_P).reshape(n, S5_P)
    d_bbi = _s5_diag(dwb[:, :, half:], S5_GS, S5_P).reshape(n, S5_P)
    from_c = lambda t: jnp.transpose(_s5_diag(t, S5_P, S5_GS), (0, 1, 3, 2)).reshape(n, S5_P)
    d_cre = from_c(dwc[:, :half, :])
    d_cneg = from_c(dwc[:, half:, :])
    cts = [dar.reshape(S5_NG, S5_P), dai.reshape(S5_NG, S5_P), d_bbr, d_bbi, d_cneg]
    dlog, dare, daim, dbre_t, dbim_t, dcim = s5_param_bwd(pargs, cts, name=f"{tag}_dparam")
    untr = lambda t: jnp.transpose(t.reshape(S5_NG, S5_GS, S5_P), (0, 2, 1))
    grads = (dlog.reshape(S5_NG), dare, daim, untr(dbre_t), untr(dbim_t),
             d_cre.reshape(S5_NG, S5_GS, S5_P), dcim.reshape(S5_NG, S5_GS, S5_P))
    return dh, dgm, grads, ddsk, dwglu


def _pad_last(w, n):
    return jnp.pad(w, [(0, 0)] * (w.ndim - 1) + [(0, n - w.shape[-1])])


_BIG = ("gla_w_in", "gla_w_out", "ssd_w_in", "ssd_w_out", "s5_w_glu", "ffn_w_gu", "ffn_w_down")


def interleave_gu(w):
    q = w.shape[-1] // 4
    return jnp.concatenate([w[..., :q], w[..., 2 * q:3 * q], w[..., q:2 * q], w[..., 3 * q:]], axis=-1)


def local_step(x, target, W, later_weights=None, later_grads=None, ffn0_grads=None, ffn0_weights=None):
    f32 = lambda a: a.astype(F32)
    row = lambda a: f32(a).reshape(1, -1)

    def layer_args(i):
        m, j = i % 3, i // 3
        gm = row(W["norm_mix_g"][i])
        if m == 0:
            args = (gm, W["gla_w_in"][j], jnp.pad(f32(W["gla_w_a2"][j]), ((0, LANE - GLA_RANK), (0, 0))),
                    row(W["gla_b_a"][j]), row(W["gla_norm_g"][j]), W["gla_w_out"][j])
        elif m == 1:
            pl_ = lambda a: _pad_last(row(a), LANE)
            args = (gm, W["ssd_w_in"][j], f32(W["ssd_conv_w"][j]),
                    row(W["ssd_conv_b"][j]), pl_(W["ssd_dt_bias"][j]), pl_(W["ssd_a_log"][j]), pl_(W["ssd_d"][j]),
                    row(W["ssd_norm_g"][j]), W["ssd_w_out"][j])
        else:
            prm = tuple(f32(W[k][j]) for k in ("s5_log_dt", "s5_a_re", "s5_a_im", "s5_b_re", "s5_b_im", "s5_c_re", "s5_c_im"))
            args = (gm, prm, row(W["s5_d"][j]), W["s5_w_glu"][j])
        return m, j, args

    h = x
    saved, mixers, ffns = [], [], []
    for i in range(DEPTH):
        mixer = layer_args(i)
        mixers.append(mixer)
        m, j, args = mixer
        tag = f"l{i}_{('gla', 'ssd', 's5')[m]}"
        h, sm = (gla_fwd, ssd_fwd, s5_fwd)[m](h, *args, tag)
        if i == 0 and ffn0_weights is not None:
            W = {**W, **ffn0_weights(h)}
        ffn = (row(W["norm_ffn_g"][i]), W["ffn_w_gu"][i], W["ffn_w_down"][i])
        ffns.append(ffn)
        h, sf = ffn_fwd(h, *ffn, f"l{i}_ffn")
        saved.append((sm, sf))
        if i == 0 and later_weights is not None:
            W = {**W, **later_weights(h)}
    loss, dfg, d = loss_head(h, row(W["final_norm_g"]), target, name="loss_head")

    G = {k: [None] * len(v) for k, v in W.items() if k != "final_norm_g"}
    G["final_norm_g"] = dfg.reshape(D)
    for i in reversed(range(DEPTH)):
        m, j, args = mixers[i]
        sm, sf = saved[i]
        if i == 0 and later_grads is not None:
            zero = later_grads(G)
            ffns[0] = (ffns[0][0], ffns[0][1], ffns[0][2] + zero.astype(ffns[0][2].dtype))
        d, dg, dwgu, dwd = ffn_bwd(d, sf, *ffns[i], f"l{i}_ffn")
        G["norm_ffn_g"][i], G["ffn_w_gu"][i], G["ffn_w_down"][i] = dg.reshape(D), dwgu, dwd
        if i == 0 and ffn0_grads is not None:
            zero = ffn0_grads(G)
            args = args[:-1] + (args[-1] + zero.astype(args[-1].dtype),)
        tag = f"l{i}_{('gla', 'ssd', 's5')[m]}"
        if m == 0:
            d, dgm, dwin, dwa2, dba, dng, dwout = gla_bwd(d, sm, *args, tag)
            G["gla_w_in"][j], G["gla_w_a2"][j], G["gla_b_a"][j] = dwin, dwa2, dba.reshape(-1)
            G["gla_norm_g"][j], G["gla_w_out"][j] = dng.reshape(-1), dwout
        elif m == 1:
            d, dgm, dwin, dcw, dcb, ddtb, dal, ddsk, dng, dwout = ssd_bwd(d, sm, *args, tag)
            G["ssd_w_in"][j], G["ssd_conv_w"][j], G["ssd_conv_b"][j] = dwin, dcw, dcb.reshape(-1)
            G["ssd_dt_bias"][j], G["ssd_a_log"][j], G["ssd_d"][j] = ddtb.reshape(-1), dal.reshape(-1), ddsk.reshape(-1)
            G["ssd_norm_g"][j], G["ssd_w_out"][j] = dng.reshape(-1), dwout
        else:
            d, dgm, pg, ddsk, dwglu = s5_bwd(d, sm, args[0], args[2], args[3], tag)
            for k, v in zip(("s5_log_dt", "s5_a_re", "s5_a_im", "s5_b_re", "s5_b_im", "s5_c_re", "s5_c_im"), pg):
                G[k][j] = v
            G["s5_d"][j], G["s5_w_glu"][j] = ddsk.reshape(-1), dwglu
        G["norm_mix_g"][i] = dgm.reshape(D)
    grads = {k: (v if k == "final_norm_g" or k in _BIG else jnp.stack(v)) for k, v in G.items()}
    return loss, d, grads


_MESH = pl.DeviceIdType.MESH
_ANY = pl.BlockSpec(memory_space=pl.ANY)
_DMA = pltpu.SemaphoreType.DMA
_ROWS_ALIGN = 1024


def _place():
    return lax.axis_index("x"), lax.axis_index("y"), lax.axis_index("c")


def _other_chips(x, y):
    return [(1 - x, y), (x, 1 - y), (1 - x, 1 - y)]


def _remote(src, dst, send_sems, recv_sems, k, to):
    return pltpu.make_async_remote_copy(src_ref=src, dst_ref=dst, send_sem=send_sems.at[k], recv_sem=recv_sems.at[k],
                                        device_id=to, device_id_type=_MESH)


def gather_shards(loc, *, name):
    def body(in_ref, out_ref, send_sems, recv_sems, local_sem):
        x, y, c = _place()
        me, sibling = (x, y, c), (x, y, 1 - c)
        chips = _other_chips(x, y)

        def half(px, py, hc):
            return out_ref.at[2 * px + py, hc]

        mine = pltpu.make_async_copy(in_ref, out_ref.at[2 * x + y], local_sem)
        mine.start()
        first = [_remote(in_ref.at[c], half(x, y, c), send_sems, recv_sems, j, (*chip, c)) for j, chip in enumerate(chips)]
        for cp in first:
            cp.start()
        passed = [_remote(half(*chip, c), half(*chip, c), send_sems, recv_sems, 3 + j, sibling) for j, chip in enumerate(chips)]
        for j, chip in enumerate(chips):
            _remote(in_ref.at[c], half(*chip, c), send_sems, recv_sems, j, me).wait_recv()
            passed[j].start()
        for j, chip in enumerate(chips):
            _remote(in_ref.at[c], half(*chip, 1 - c), send_sems, recv_sems, 3 + j, me).wait_recv()
        for cp in first + passed:
            cp.wait_send()
        mine.wait()

    return pl.pallas_call(
        body, name=name, in_specs=[_ANY], out_specs=_ANY,
        out_shape=jax.ShapeDtypeStruct((4,) + loc.shape, loc.dtype),
        scratch_shapes=[_DMA((6,)), _DMA((6,)), _DMA(())],
    )(loc)


def _pos(px, py, perm):
    return 2 * py + px if perm else 2 * px + py


def _part(ref, kind, p, loc):
    if kind == "lead":
        return ref.at[p]
    return ref.at[:, pl.ds(pl.multiple_of(p * loc, LANE), loc)]


def _rows(ref, h, hr):
    return ref.at[pl.ds(h * hr, hr)]


def _rows_block(hr, width):
    return max(b for b in range(16, hr + 1, 16) if hr % b == 0 and (b * width <= (1 << 19) or b == 16))


def gather_big(locs, kinds, *, name):
    n = len(locs)

    def body(*refs):
        ins, outs = refs[:n], refs[n:2 * n]
        send_sems, recv_sems = refs[2 * n + 1:]
        refs[2 * n][...] = jnp.zeros_like(refs[2 * n])
        x, y, c = _place()
        me, sibling = (x, y, c), (x, y, 1 - c)
        chips = _other_chips(x, y)

        def half(i, px, py, h):
            (kind, perm), (rows, loc) = kinds[i], locs[i].shape
            return _rows(_part(outs[i], kind, _pos(px, py, perm), loc), h, rows // 2)

        sends = []
        for i in range(n):
            (kind, perm), (rows, loc) = kinds[i], locs[i].shape
            own = _part(outs[i], kind, _pos(x, y, perm), loc)
            sends.append(_remote(ins[i], own, send_sems, recv_sems, 6 * n + i, sibling))
            sends[-1].start()
            for j, chip in enumerate(chips):
                sends.append(_remote(_rows(ins[i], c, rows // 2), half(i, x, y, c), send_sems, recv_sems, 6 * i + j, (*chip, c)))
                sends[-1].start()
        for i in range(n):
            hr = locs[i].shape[0] // 2
            for j, chip in enumerate(chips):
                _remote(_rows(ins[i], c, hr), half(i, *chip, c), send_sems, recv_sems, 6 * i + j, me).wait_recv()
                sends.append(_remote(half(i, *chip, c), half(i, *chip, c), send_sems, recv_sems, 6 * i + 3 + j, sibling))
                sends[-1].start()
        for i in range(n):
            (kind, perm), (rows, loc) = kinds[i], locs[i].shape
            for j, chip in enumerate(chips):
                _remote(_rows(ins[i], c, rows // 2), half(i, *chip, 1 - c), send_sems, recv_sems, 6 * i + 3 + j, me).wait_recv()
            _remote(ins[i], _part(outs[i], kind, _pos(x, y, perm), loc), send_sems, recv_sems, 6 * n + i, me).wait_recv()
        for cp in sends:
            cp.wait_send()

    def out_shape(a, kind):
        rows, loc = a.shape
        return jax.ShapeDtypeStruct((4, rows, loc) if kind == "lead" else (rows, 4 * loc), a.dtype)

    outs = pl.pallas_call(
        body, name=name, in_specs=[_ANY] * n, out_specs=[_ANY] * n + [pl.BlockSpec(memory_space=pltpu.VMEM)],
        out_shape=[out_shape(a, k[0]) for a, k in zip(locs, kinds)] + [jax.ShapeDtypeStruct((8, LANE), F32)],
        scratch_shapes=[_DMA((7 * n,)), _DMA((7 * n,))],
    )(*locs)
    return list(outs[:n]), outs[n][0, 0]


_HBM = pl.BlockSpec(memory_space=pltpu.HBM)
_SEM = pl.BlockSpec(memory_space=pltpu.SEMAPHORE)
_EFFECT = pltpu.SideEffectType.DATAFLOW_SIDE_EFFECTING


def _in_hbm(a):
    return pltpu.with_memory_space_constraint(a, pltpu.HBM)


def _gather_ici_copies(ins, lands, kinds, shapes, send_sems, recv_sems):
    x, y, c = _place()
    sends, arrivals = [], []
    for i, ((kind, perm), (rows, loc)) in enumerate(zip(kinds, shapes)):
        hr = rows // 2
        mine = _part(lands[i], kind, _pos(x, y, perm), loc)
        sends.append(_remote(ins[i], mine, send_sems, recv_sems, 4 * i + 3, (x, y, 1 - c)))
        arrivals.append(_remote(ins[i], mine, send_sems, recv_sems, 4 * i + 3, (x, y, c)))
        for j, (px, py) in enumerate(_other_chips(x, y)):
            sends.append(_remote(_rows(ins[i], c, hr), _rows(mine, c, hr), send_sems, recv_sems, 4 * i + j, (px, py, c)))
            theirs = _rows(_part(lands[i], kind, _pos(px, py, perm), loc), c, hr)
            arrivals.append(_remote(_rows(ins[i], c, hr), theirs, send_sems, recv_sems, 4 * i + j, (x, y, c)))
    return sends, arrivals


def gather_start(locs, kinds, *, name):
    n = len(locs)
    shapes = [a.shape for a in locs]

    def land_shape(a, kind):
        rows, loc = a.shape
        return (4, rows, loc) if kind == "lead" else (rows, 4 * loc)

    def body(*refs):
        sends, _ = _gather_ici_copies(refs[:n], refs[n:2 * n], kinds, shapes, refs[2 * n], refs[2 * n + 1])
        for cp in sends:
            cp.start()
        refs[-1][...] = jnp.zeros_like(refs[-1])

    lands = [lax.empty(land_shape(a, k[0]), a.dtype) for a, k in zip(locs, kinds)]
    outs = pl.pallas_call(
        body, name=name, in_specs=[_HBM] * (2 * n), out_specs=[_SEM, _SEM] + [_HBM] * (2 * n) + [pl.BlockSpec(memory_space=pltpu.VMEM)],
        out_shape=[_DMA((4 * n,)), _DMA((4 * n,))] + [pltpu.HBM(a.shape, a.dtype) for a in locs]
        + [pltpu.HBM(l.shape, l.dtype) for l in lands] + [jax.ShapeDtypeStruct((8, LANE), F32)],
        input_output_aliases={i: 2 + i for i in range(2 * n)},
        compiler_params=pltpu.CompilerParams(has_side_effects=_EFFECT),
    )(*[_in_hbm(a) for a in locs], *[_in_hbm(l) for l in lands])
    return outs[0], outs[1], list(outs[2:2 + n]), list(outs[2 + n:2 + 2 * n]), outs[-1][0, 0]


def gather_wait(send_sems, recv_sems, locs, lands, kinds, after, *, name):
    n = len(locs)
    shapes = [a.shape for a in locs]

    def body(*refs):
        sends, arrivals = _gather_ici_copies(refs[:n], refs[n:2 * n], kinds, shapes, refs[2 * n], refs[2 * n + 1])
        for cp in sends:
            cp.wait_send()
        for cp in arrivals:
            cp.wait_recv()

    outs = pl.pallas_call(
        body, name=name, in_specs=[_HBM] * (2 * n) + [_SEM, _SEM, _ANY], out_specs=[_HBM] * (2 * n),
        out_shape=[pltpu.HBM(a.shape, a.dtype) for a in locs] + [pltpu.HBM(l.shape, l.dtype) for l in lands],
        input_output_aliases={i: i for i in range(2 * n)},
        compiler_params=pltpu.CompilerParams(has_side_effects=_EFFECT),
    )(*locs, *lands, send_sems, recv_sems, after)
    return list(outs[n:])


def gather_finish(lands, kinds, shapes, *, name):
    n = len(lands)

    def body(*refs):
        bufs = refs[n:2 * n]
        send_sems, recv_sems = refs[2 * n:]
        x, y, c = _place()
        sends = []
        for i, ((kind, perm), (rows, loc)) in enumerate(zip(kinds, shapes)):
            for j, (px, py) in enumerate(_other_chips(x, y)):
                part = _part(bufs[i], kind, _pos(px, py, perm), loc)
                sends.append(_remote(_rows(part, c, rows // 2), _rows(part, c, rows // 2), send_sems, recv_sems, 3 * i + j, (x, y, 1 - c)))
                sends[-1].start()
        for i, ((kind, perm), (rows, loc)) in enumerate(zip(kinds, shapes)):
            for j, (px, py) in enumerate(_other_chips(x, y)):
                part = _part(bufs[i], kind, _pos(px, py, perm), loc)
                _remote(_rows(part, c, rows // 2), _rows(part, 1 - c, rows // 2), send_sems, recv_sems, 3 * i + j, (x, y, c)).wait_recv()
        for cp in sends:
            cp.wait_send()

    return list(pl.pallas_call(
        body, name=name, in_specs=[_ANY] * n, out_specs=[_ANY] * n,
        out_shape=[jax.ShapeDtypeStruct(l.shape, l.dtype) for l in lands],
        input_output_aliases={i: i for i in range(n)}, scratch_shapes=[_DMA((3 * n,)), _DMA((3 * n,))],
    )(*lands))


def _scatter_copies(qs, lands, kinds, locs, send_sems, recv_sems):
    x, y, c = _place()
    sends, arrivals = [], []
    for i, (kind, perm) in enumerate(kinds):
        for j, (px, py) in enumerate(_other_chips(x, y)):
            src = _part(qs[i], kind, _pos(px, py, perm), locs[i])
            sends.append(_remote(src, lands[i].at[j], send_sems, recv_sems, 3 * i + j, (px, py, c)))
            arrivals.append(_remote(src, lands[i].at[j], send_sems, recv_sems, 3 * i + j, (x, y, c)))
    return sends, arrivals


def _scatter_land(q, kind, loc):
    return (3, q.shape[1] if kind == "lead" else q.shape[0], loc)


def scatter_start(qs, kinds, locs, *, name):
    n = len(qs)

    def body(*refs):
        sends, _ = _scatter_copies(refs[:n], refs[n:2 * n], kinds, locs, refs[2 * n], refs[2 * n + 1])
        for cp in sends:
            cp.start()
        refs[-1][...] = jnp.zeros_like(refs[-1])

    lands = [lax.empty(_scatter_land(q, k[0], l), q.dtype) for q, k, l in zip(qs, kinds, locs)]
    outs = pl.pallas_call(
        body, name=name, in_specs=[_HBM] * (2 * n), out_specs=[_SEM, _SEM] + [_HBM] * (2 * n) + [pl.BlockSpec(memory_space=pltpu.VMEM)],
        out_shape=[_DMA((3 * n,)), _DMA((3 * n,))] + [pltpu.HBM(q.shape, q.dtype) for q in qs]
        + [pltpu.HBM(l.shape, l.dtype) for l in lands] + [jax.ShapeDtypeStruct((8, LANE), F32)],
        input_output_aliases={i: 2 + i for i in range(2 * n)},
        compiler_params=pltpu.CompilerParams(has_side_effects=_EFFECT),
    )(*[_in_hbm(q) for q in qs], *[_in_hbm(l) for l in lands])
    return outs[0], outs[1], list(outs[2:2 + n]), list(outs[2 + n:2 + 2 * n]), outs[-1][0, 0]


def scatter_wait(send_sems, recv_sems, qs, lands, kinds, locs, after, *, name):
    n = len(qs)

    def body(*refs):
        sends, arrivals = _scatter_copies(refs[:n], refs[n:2 * n], kinds, locs, refs[2 * n], refs[2 * n + 1])
        for cp in sends:
            cp.wait_send()
        for cp in arrivals:
            cp.wait_recv()

    outs = pl.pallas_call(
        body, name=name, in_specs=[_HBM] * (2 * n) + [_SEM, _SEM, _ANY], out_specs=[_HBM] * (2 * n),
        out_shape=[pltpu.HBM(q.shape, q.dtype) for q in qs] + [pltpu.HBM(l.shape, l.dtype) for l in lands],
        input_output_aliases={i: i for i in range(2 * n)},
        compiler_params=pltpu.CompilerParams(has_side_effects=_EFFECT),
    )(*qs, *lands, send_sems, recv_sems, after)
    return list(outs[:n]), list(outs[n:])


def _pair_swap_copies(ins, lands, kinds, send_sems, recv_sems):
    x, y, c = _place()
    sends, arrivals = [], []
    for i, (kind, _) in enumerate(kinds):
        if kind == "lead":
            hr = ins[i].shape[1] // 2
            src = ins[i].at[:, pl.ds((1 - c) * hr, hr)]
        else:
            src = _rows(ins[i], 1 - c, ins[i].shape[0] // 2)
        sends.append(_remote(src, lands[i], send_sems, recv_sems, i, (x, y, 1 - c)))
        arrivals.append(_remote(src, lands[i], send_sems, recv_sems, i, (x, y, c)))
    return sends, arrivals


def _pair_swap_land(a, kind):
    s = a.shape
    return (4, s[1] // 2, s[2]) if kind == "lead" else (s[0] // 2, s[1])


def pair_swap_start(ps, kinds, *, name):
    n = len(ps)

    def body(*refs):
        sends, _ = _pair_swap_copies(refs[:n], refs[n:2 * n], kinds, refs[2 * n], refs[2 * n + 1])
        for cp in sends:
            cp.start()
        refs[-1][...] = jnp.zeros_like(refs[-1])

    lands = [lax.empty(_pair_swap_land(p, k[0]), p.dtype) for p, k in zip(ps, kinds)]
    outs = pl.pallas_call(
        body, name=name, in_specs=[_HBM] * (2 * n), out_specs=[_SEM, _SEM] + [_HBM] * (2 * n) + [pl.BlockSpec(memory_space=pltpu.VMEM)],
        out_shape=[_DMA((n,)), _DMA((n,))] + [pltpu.HBM(p.shape, p.dtype) for p in ps]
        + [pltpu.HBM(l.shape, l.dtype) for l in lands] + [jax.ShapeDtypeStruct((8, LANE), F32)],
        input_output_aliases={i: 2 + i for i in range(2 * n)},
        compiler_params=pltpu.CompilerParams(has_side_effects=_EFFECT),
    )(*[_in_hbm(p) for p in ps], *[_in_hbm(l) for l in lands])
    return outs[0], outs[1], list(outs[2:2 + n]), list(outs[2 + n:2 + 2 * n]), outs[-1][0, 0]


def pair_swap_wait(send_sems, recv_sems, ps, lands, kinds, after, *, name):
    n = len(ps)

    def body(*refs):
        sends, arrivals = _pair_swap_copies(refs[:n], refs[n:2 * n], kinds, refs[2 * n], refs[2 * n + 1])
        for cp in sends:
            cp.wait_send()
        for cp in arrivals:
            cp.wait_recv()

    outs = pl.pallas_call(
        body, name=name, in_specs=[_HBM] * (2 * n) + [_SEM, _SEM, _ANY], out_specs=[_HBM] * (2 * n),
        out_shape=[pltpu.HBM(p.shape, p.dtype) for p in ps] + [pltpu.HBM(l.shape, l.dtype) for l in lands],
        input_output_aliases={i: i for i in range(2 * n)},
        compiler_params=pltpu.CompilerParams(has_side_effects=_EFFECT),
    )(*ps, *lands, send_sems, recv_sems, after)
    return list(outs[:n]), list(outs[n:])


def pair_swap(ps, kinds, *, name):
    n = len(ps)

    def body(*refs):
        ins, outs = refs[:n], refs[n:2 * n]
        send_sems, recv_sems = refs[2 * n:]
        x, y, c = _place()
        cps = []
        for i in range(n):
            if kinds[i][0] == "lead":
                hr = ps[i].shape[1] // 2
                src = ins[i].at[:, pl.ds((1 - c) * hr, hr)]
            else:
                hr = ps[i].shape[0] // 2
                src = _rows(ins[i], 1 - c, hr)
            cps.append(_remote(src, outs[i], send_sems, recv_sems, i, (x, y, 1 - c)))
            cps[-1].start()
        for cp in cps:
            cp.wait()

    def out_shape(a, kind):
        s = a.shape
        return jax.ShapeDtypeStruct((4, s[1] // 2, s[2]) if kind == "lead" else (s[0] // 2, s[1]), a.dtype)

    return pl.pallas_call(
        body, name=name, in_specs=[_ANY] * n, out_specs=[_ANY] * n,
        out_shape=[out_shape(a, k[0]) for a, k in zip(ps, kinds)], scratch_shapes=[_DMA((n,)), _DMA((n,))],
    )(*ps)


def pair_add(p, got, c_arr, kind, *, name):
    if kind == "lead":
        _, hr, cols = got.shape
        br = _rows_block(hr, cols)
        nb = hr // br
        grid = (4, nb)
        p_spec = pl.BlockSpec((None, br, cols), lambda s, i, cr: (s, cr[0] * nb + i, 0))
        g_spec = pl.BlockSpec((None, br, cols), lambda s, i, cr: (s, i, 0))
    else:
        hr, w = got.shape
        br = _rows_block(hr, w)
        nb = hr // br
        grid = (nb,)
        p_spec = pl.BlockSpec((br, w), lambda i, cr: (cr[0] * nb + i, 0))
        g_spec = pl.BlockSpec((br, w), lambda i, cr: (i, 0))

    def body(c_ref, p_ref, g_ref, o_ref):
        o_ref[...] = (p_ref[...] + g_ref[...]).astype(o_ref.dtype)

    return pl.pallas_call(
        body, name=name, out_shape=jax.ShapeDtypeStruct(got.shape, BF16),
        grid_spec=pltpu.PrefetchScalarGridSpec(num_scalar_prefetch=1, grid=grid, in_specs=[p_spec, g_spec], out_specs=g_spec),
        compiler_params=_cparams(len(grid)),
    )(c_arr, p, got)


def chip_scatter(qs, kinds, locs, *, name):
    n = len(qs)

    def body(*refs):
        ins, outs = refs[:n], refs[n:2 * n]
        send_sems, recv_sems = refs[2 * n:]
        x, y, c = _place()
        cps = []
        for i in range(n):
            kind, perm = kinds[i]
            for j, (px, py) in enumerate(_other_chips(x, y)):
                cps.append(_remote(_part(ins[i], kind, _pos(px, py, perm), locs[i]), outs[i].at[j], send_sems, recv_sems,
                                   3 * i + j, (px, py, c)))
                cps[-1].start()
        for cp in cps:
            cp.wait()

    def out_shape(a, kind, loc):
        hr = a.shape[1] if kind == "lead" else a.shape[0]
        return jax.ShapeDtypeStruct((3, hr, loc), a.dtype)

    return pl.pallas_call(
        body, name=name, in_specs=[_ANY] * n, out_specs=[_ANY] * n,
        out_shape=[out_shape(a, k[0], l) for a, k, l in zip(qs, kinds, locs)],
        scratch_shapes=[_DMA((3 * n,)), _DMA((3 * n,))],
    )(*qs)


def chip_add(q, r, pos_arr, c_arr, kind, loc, *, name):
    _, hr, _ = r.shape
    br = _rows_block(hr, loc)
    nb = hr // br
    if kind == "lead":
        q_spec = pl.BlockSpec((None, br, loc), lambda i, pr, cr: (pr[0], i, 0))
    else:
        q_spec = pl.BlockSpec((br, loc), lambda i, pr, cr: (i, pr[0]))
    r_spec = pl.BlockSpec((3, br, loc), lambda i, pr, cr: (0, i, 0))
    o_spec = pl.BlockSpec((br, loc), lambda i, pr, cr: (cr[0] * nb + i, 0))

    def body(p_ref, c_ref, q_ref, r_ref, o_ref):
        acc = q_ref[...].astype(F32)
        for j in range(3):
            acc = acc + r_ref[j].astype(F32)
        o_ref[...] = acc

    return pl.pallas_call(
        body, name=name, out_shape=jax.ShapeDtypeStruct((2 * hr, loc), F32),
        grid_spec=pltpu.PrefetchScalarGridSpec(num_scalar_prefetch=2, grid=(nb,), in_specs=[q_spec, r_spec], out_specs=o_spec),
        compiler_params=_cparams(1),
    )(pos_arr, c_arr, q, r)


def share_rows(fs, *, name):
    n = len(fs)

    def body(*refs):
        bufs = refs[n:2 * n]
        send_sems, recv_sems = refs[2 * n:]
        x, y, c = _place()
        cps = []
        for i in range(n):
            hr = fs[i].shape[0] // 2
            cps.append(_remote(_rows(bufs[i], c, hr), _rows(bufs[i], c, hr), send_sems, recv_sems, i, (x, y, 1 - c)))
            cps[-1].start()
        for i, cp in enumerate(cps):
            hr = fs[i].shape[0] // 2
            _remote(_rows(bufs[i], c, hr), _rows(bufs[i], 1 - c, hr), send_sems, recv_sems, i, (x, y, c)).wait_recv()
            cp.wait_send()

    return pl.pallas_call(
        body, name=name, in_specs=[_ANY] * n, out_specs=[_ANY] * n,
        out_shape=[jax.ShapeDtypeStruct(f.shape, f.dtype) for f in fs],
        input_output_aliases={i: i for i in range(n)}, scratch_shapes=[_DMA((n,)), _DMA((n,))],
    )(*fs)


def _gather_all_copies(v_ref, land_ref, send_sems, recv_sems):
    x, y, c = _place()
    flip = lambda p, m: 1 - p if m else p
    idx = lambda p: 4 * p[0] + 2 * p[1] + p[2]
    sends, arrivals = [], []
    for k, m in enumerate(range(1, 8)):
        p = (flip(x, m & 4), flip(y, m & 2), flip(c, m & 1))
        sends.append(_remote(v_ref, land_ref.at[idx((x, y, c))], send_sems, recv_sems, k, p))
        arrivals.append(_remote(v_ref, land_ref.at[idx(p)], send_sems, recv_sems, k, (x, y, c)))
    return sends, arrivals


def gather_all_start(v, *, name):
    def body(v_ref, land_ref, send_sems, recv_sems, v_thru, land_thru, token):
        sends, _ = _gather_all_copies(v_ref, land_ref, send_sems, recv_sems)
        for cp in sends:
            cp.start()
        token[...] = jnp.zeros_like(token)

    land = jnp.zeros((8,) + v.shape, v.dtype)
    outs = pl.pallas_call(
        body, name=name, in_specs=[_HBM, _HBM], out_specs=[_SEM, _SEM, _HBM, _HBM, pl.BlockSpec(memory_space=pltpu.VMEM)],
        out_shape=[_DMA((7,)), _DMA((7,)), pltpu.HBM(v.shape, v.dtype), pltpu.HBM(land.shape, land.dtype),
                   jax.ShapeDtypeStruct((8, LANE), F32)],
        input_output_aliases={0: 2, 1: 3}, compiler_params=pltpu.CompilerParams(has_side_effects=_EFFECT),
    )(_in_hbm(v), _in_hbm(land))
    return outs[0], outs[1], outs[2], outs[3], outs[4][0, 0]


def gather_all_wait(send_sems, recv_sems, v, land, after, *, name):
    def body(v_ref, land_ref, send_sems, recv_sems, after_ref, v_dead, got_ref):
        sends, arrivals = _gather_all_copies(v_ref, land_ref, send_sems, recv_sems)
        for cp in sends:
            cp.wait_send()
        for cp in arrivals:
            cp.wait_recv()

    return pl.pallas_call(
        body, name=name, in_specs=[_HBM, _HBM, _SEM, _SEM, _ANY], out_specs=[_HBM, _HBM],
        out_shape=[pltpu.HBM(v.shape, v.dtype), pltpu.HBM(land.shape, land.dtype)],
        input_output_aliases={0: 0, 1: 1}, compiler_params=pltpu.CompilerParams(has_side_effects=_EFFECT),
    )(v, land, send_sems, recv_sems, after)[1]


def sum_slots(land, v, me_arr, *, name):
    n, R, L = land.shape
    br = _pick(R, _ROWS_ALIGN, 8)

    def body(me_ref, land_ref, v_ref, o_ref):
        acc = None
        for i in range(n):
            term = jnp.where(me_ref[0] == i, v_ref[...], land_ref[i])
            acc = term if acc is None else acc + term
        o_ref[...] = acc

    row = pl.BlockSpec((br, L), lambda i, me: (i, 0))
    return pl.pallas_call(
        body, name=name, out_shape=jax.ShapeDtypeStruct((R, L), land.dtype),
        grid_spec=pltpu.PrefetchScalarGridSpec(num_scalar_prefetch=1, grid=(R // br,),
                                               in_specs=[pl.BlockSpec((n, br, L), lambda i, me: (0, i, 0)), row], out_specs=row),
        compiler_params=_cparams(1),
    )(me_arr, land, v)


def adamw(w, g, m, v, *, name):
    shape = w.shape
    size = math.prod(shape)
    last = shape[-1]
    if last % LANE != 0 and size % LANE == 0 and size <= (1 << 20):
        last = LANE
    rows = size // last
    budget = (1 << 18) // last
    br = rows
    if rows > budget:
        br = max(c for c in range(8, budget + 1, 8) if rows % c == 0)
    v2 = lambda a: a.reshape(rows, last)

    def body(w_ref, g_ref, m_ref, v_ref, d_ref, nm_ref, nv_ref):
        gg = g_ref[...]
        nm = ADAM_B1 * m_ref[...] + (1.0 - ADAM_B1) * gg
        nv = ADAM_B2 * v_ref[...] + (1.0 - ADAM_B2) * (gg * gg)
        m_hat = nm / (1.0 - ADAM_B1 ** ADAM_STEP)
        v_hat = nv / (1.0 - ADAM_B2 ** ADAM_STEP)
        d_ref[...] = -ADAM_LR * (m_hat / (jnp.sqrt(v_hat) + ADAM_EPS) + ADAM_WD * w_ref[...])
        nm_ref[...] = nm
        nv_ref[...] = nv

    spec = pl.BlockSpec((br, last), lambda i: (i, 0))
    outs = pl.pallas_call(
        body, name=name, grid=(rows // br,), in_specs=[spec] * 4, out_specs=[spec] * 3,
        out_shape=[jax.ShapeDtypeStruct((rows, last), F32)] * 3, compiler_params=_cparams(1),
    )(v2(w), v2(g), v2(m), v2(v))
    return [o.reshape(shape) for o in outs]


_WEIGHTS = ["norm_mix_g", "norm_ffn_g", "gla_w_in", "gla_w_a2", "gla_b_a", "gla_norm_g", "gla_w_out", "ssd_w_in",
            "ssd_conv_w", "ssd_conv_b", "ssd_dt_bias", "ssd_a_log", "ssd_d", "ssd_norm_g", "ssd_w_out", "s5_log_dt",
            "s5_a_re", "s5_a_im", "s5_b_re", "s5_b_im", "s5_c_re", "s5_c_im", "s5_d", "s5_w_glu", "ffn_w_gu",
            "ffn_w_down", "final_norm_g"]
_SHARD_AXIS = {"gla_w_in": 2, "gla_w_a2": 2, "gla_b_a": 1, "gla_norm_g": 1, "gla_w_out": 1, "ssd_w_in": 2,
               "ssd_conv_w": 2, "ssd_w_out": 1, "s5_d": 1, "s5_w_glu": 2, "ffn_w_gu": 2, "ffn_w_down": 1}
_SMALL_SHARDED = [n for n in _WEIGHTS if n in _SHARD_AXIS and n not in _BIG]
_REPLICATED = [n for n in _WEIGHTS if n not in _SHARD_AXIS]
_BIG_KIND = {"gla_w_in": ("lead", False), "gla_w_out": ("lead", False), "ssd_w_in": ("lead", False),
             "ssd_w_out": ("lead", False), "s5_w_glu": ("cols", False), "ffn_w_gu": ("cols", True),
             "ffn_w_down": ("lead", False)}
_PADDED_IN = {"gla_w_in": GLA_INP, "ssd_w_in": SSD_INP}


def _to_rows(flat, parts=1):
    per = -(-flat.shape[0] // (parts * LANE * _ROWS_ALIGN)) * _ROWS_ALIGN
    flat = jnp.pad(flat, (0, parts * per * LANE - flat.shape[0]))
    return flat.reshape(parts, per, LANE)


def _big_layers(local):
    return [(n, j, local[n][j].reshape(-1, local[n].shape[-1])) for n in _BIG for j in range(local[n].shape[0])]


def _in_layer0(n, j):
    return j == 0 and n in ("gla_w_in", "gla_w_out", "ffn_w_gu", "ffn_w_down")


def _assemble(n, g):
    if n in _PADDED_IN:
        return jnp.concatenate([g[s] for s in range(4)] + [jnp.zeros((g.shape[1], _PADDED_IN[n] - 4 * g.shape[2]), BF16)], axis=1)
    if _BIG_KIND[n][0] == "lead":
        return g.reshape(4 * g.shape[1], g.shape[2])
    return g


def _is_gla0(n, j):
    return j == 0 and n in ("gla_w_in", "gla_w_out")


def _gather_first(local):
    layers = _big_layers(local)
    first = [l for l in layers if _is_gla0(l[0], l[1])]
    full = {n: [None] * local[n].shape[0] for n in _BIG}
    got, done = gather_big([w.astype(BF16) for _, _, w in first], [_BIG_KIND[n] for n, _, _ in first], name="gather_weights_first")
    for (n, j, _), g in zip(first, got):
        full[n][j] = _assemble(n, g)
    flat = jnp.concatenate([local[n].astype(F32).reshape(-1) for n in _SMALL_SHARDED])
    got = gather_shards(_to_rows(flat, 2), name="gather_small_weights").reshape(4, -1)
    off = 0
    for n in _SMALL_SHARDED:
        bs = local[n].shape
        sz = math.prod(bs)
        seg = got[:, off:off + sz].reshape((4,) + bs)
        off += sz
        ax = _SHARD_AXIS[n]
        full[n] = jnp.moveaxis(seg, 0, ax).reshape(bs[:ax] + (4 * bs[ax],) + bs[ax + 1:])
    pending = {}
    for tag, want in (("ffn0", _is_ffn0), ("later", lambda n, j: not _in_layer0(n, j))):
        group = [l for l in layers if want(l[0], l[1])]
        kinds = [_BIG_KIND[n] for n, _, _ in group]
        ops = [(w + done if k == 0 else w).astype(BF16) for k, (_, _, w) in enumerate(group)]
        send_sems, recv_sems, locs, lands, done = gather_start(ops, kinds, name=f"gather_weights_start_{tag}")
        pending[tag] = (group, kinds, send_sems, recv_sems, locs, lands)
    return full, pending, done


def _gather_rest(full, pending, after, tag):
    group, kinds, send_sems, recv_sems, locs, lands = pending
    lands = gather_wait(send_sems, recv_sems, locs, lands, kinds, after, name=f"gather_weights_wait_{tag}")
    lands = gather_finish(lands, kinds, [w.shape for _, _, w in group], name=f"gather_weights_finish_{tag}")
    out = {n: list(full[n]) for n in _BIG}
    for (n, j, _), g in zip(group, lands):
        out[n][j] = _assemble(n, g)
    return out


def _reduce_ops(grads, local, want):
    ops = []
    for n in _BIG:
        kind = _BIG_KIND[n]
        for j, g in enumerate(grads[n]):
            if not want(n, j):
                continue
            loc = local[n].shape[-1] if kind[0] == "cols" or n in _PADDED_IN else g.shape[1]
            if n in _PADDED_IN:
                g = jnp.stack([g[:, s * loc:(s + 1) * loc] for s in range(4)])
            elif kind[0] == "lead":
                g = g.reshape(4, g.shape[0] // 4, g.shape[1])
            ops.append((n, j, kind, loc, g))
    return ops


def _pair_sums(ops, c_arr, tag):
    gots = pair_swap([o[4] for o in ops], [o[2] for o in ops], name=f"reduce_pair_swap_{tag}")
    return [pair_add(o[4], got, c_arr, o[2][0], name=f"reduce_pair_add_{o[0]}{o[1]}") for o, got in zip(ops, gots)]


def _is_ffn0(n, j):
    return j == 0 and n in ("ffn_w_gu", "ffn_w_down")


def _reduce_start(grads, local, c, want, tag):
    ops = _reduce_ops(grads, local, want)
    c_arr = jnp.reshape(c, (1,)).astype(jnp.int32)
    qs = _pair_sums(ops, c_arr, tag)
    send_sems, recv_sems, qs, lands, zero = scatter_start(qs, [o[2] for o in ops], [o[3] for o in ops],
                                                          name=f"reduce_scatter_start_{tag}")
    return (ops, send_sems, recv_sems, qs, lands, tag), zero


def _reduce_swap_start(grads, local, c, want, tag):
    ops = _reduce_ops(grads, local, want)
    send_sems, recv_sems, ps, lands, zero = pair_swap_start([o[4] for o in ops], [o[2] for o in ops],
                                                            name=f"reduce_pair_swap_start_{tag}")
    return (ops, send_sems, recv_sems, ps, lands, tag), zero


def _reduce_scatter_after(pending, after, c):
    ops, send_sems, recv_sems, ps, lands, tag = pending
    ps, gots = pair_swap_wait(send_sems, recv_sems, ps, lands, [o[2] for o in ops], after, name=f"reduce_pair_swap_wait_{tag}")
    c_arr = jnp.reshape(c, (1,)).astype(jnp.int32)
    qs = [pair_add(p, got, c_arr, o[2][0], name=f"reduce_pair_add_{o[0]}{o[1]}") for o, p, got in zip(ops, ps, gots)]
    send_sems, recv_sems, qs, lands, zero = scatter_start(qs, [o[2] for o in ops], [o[3] for o in ops],
                                                          name=f"reduce_scatter_start_{tag}")
    return (ops, send_sems, recv_sems, qs, lands, tag), zero


def _reduce_big(grads, local, pendings, after, x, y, c):
    c_arr = jnp.reshape(c, (1,)).astype(jnp.int32)
    ops, qs, rs = [], [], []
    for ops_p, send_sems, recv_sems, qs_p, lands, tag in pendings:
        qs_p, rs_p = scatter_wait(send_sems, recv_sems, qs_p, lands, [o[2] for o in ops_p], [o[3] for o in ops_p], after,
                                  name=f"reduce_scatter_wait_{tag}")
        ops, qs, rs = ops + ops_p, qs + qs_p, rs + rs_p
    ops_f = _reduce_ops(grads, local, lambda n, j: _in_layer0(n, j) and not _is_ffn0(n, j))
    qs_f = _pair_sums(ops_f, c_arr, "first")
    s_sems, r_sems, qs_f, lands_f, zero = scatter_start(qs_f, [o[2] for o in ops_f], [o[3] for o in ops_f],
                                                        name="reduce_scatter_start_first")
    qs[0] = qs[0] + zero.astype(qs[0].dtype)
    red = _reduce_close(ops, qs, rs, x, y, c_arr, "later")
    done = {n: jnp.stack([red[(n, j)] for j in range(local[n].shape[0])]).reshape(local[n].shape)
            for n in _BIG if all((n, j) in red for j in range(local[n].shape[0]))}
    return done, red, (ops_f, s_sems, r_sems, qs_f, lands_f)


def _reduce_close(ops, qs, rs, x, y, c_arr, tag):
    fs = [chip_add(q, r, jnp.reshape(_pos(x, y, o[2][1]), (1,)).astype(jnp.int32), c_arr, o[2][0], o[3],
                   name=f"reduce_chip_add_{o[0]}{o[1]}") for o, q, r in zip(ops, qs, rs)]
    outs = share_rows(fs, name=f"reduce_share_{tag}")
    return {(o[0], o[1]): r for o, r in zip(ops, outs)}


def _reduce_big_first(pending, red, after, local, x, y, c):
    ops_f, s_sems, r_sems, qs_f, lands_f = pending
    qs_f, rs_f = scatter_wait(s_sems, r_sems, qs_f, lands_f, [o[2] for o in ops_f], [o[3] for o in ops_f], after,
                              name="reduce_scatter_wait_first")
    red = {**red, **_reduce_close(ops_f, qs_f, rs_f, x, y, jnp.reshape(c, (1,)).astype(jnp.int32), "first")}
    names = sorted({o[0] for o in ops_f})
    return {n: jnp.stack([red[(n, j)] for j in range(local[n].shape[0])]).reshape(local[n].shape) for n in names}


def _reduce_small_start(grads):
    names = _REPLICATED + _SMALL_SHARDED
    flat = jnp.concatenate([grads[n].astype(F32).reshape(-1) for n in names])
    n_el = flat.shape[0]
    rows = -(-n_el // (LANE * 8)) * 8
    v = jnp.pad(flat, (0, rows * LANE - n_el)).reshape(rows, LANE)
    outs = gather_all_start(v, name="reduce_small_start")
    return outs[:4], outs[4]


def _reduce_small(pending, after, grads, local, x, y, c):
    names = _REPLICATED + _SMALL_SHARDED
    send_sems, recv_sems, v, land = pending
    land = gather_all_wait(send_sems, recv_sems, v, land, after, name="reduce_small_wait")
    me = jnp.reshape(4 * x + 2 * y + c, (1,)).astype(jnp.int32)
    red = sum_slots(land, v, me, name="reduce_small_add").reshape(-1)
    out, off = {}, 0
    for n in names:
        sz = math.prod(grads[n].shape)
        g = red[off:off + sz].reshape(grads[n].shape)
        off += sz
        if n in _SHARD_AXIS:
            ax = _SHARD_AXIS[n]
            loc = local[n].shape[ax]
            g = lax.dynamic_slice_in_dim(g, (2 * x + y) * loc, loc, axis=ax)
        out[n] = g
    return out


def kernel(x, norm_mix_g, norm_ffn_g, gla_w_in, gla_w_a2, gla_b_a, gla_norm_g, gla_w_out, ssd_w_in, ssd_conv_w, ssd_conv_b, ssd_dt_bias, ssd_a_log, ssd_d, ssd_norm_g, ssd_w_out, s5_log_dt, s5_a_re, s5_a_im, s5_b_re, s5_b_im, s5_c_re, s5_c_im, s5_d, s5_w_glu, ffn_w_gu, ffn_w_down, final_norm_g, loss_target, m_norm_mix_g, m_norm_ffn_g, m_gla_w_in, m_gla_w_a2, m_gla_b_a, m_gla_norm_g, m_gla_w_out, m_ssd_w_in, m_ssd_conv_w, m_ssd_conv_b, m_ssd_dt_bias, m_ssd_a_log, m_ssd_d, m_ssd_norm_g, m_ssd_w_out, m_s5_log_dt, m_s5_a_re, m_s5_a_im, m_s5_b_re, m_s5_b_im, m_s5_c_re, m_s5_c_im, m_s5_d, m_s5_w_glu, m_ffn_w_gu, m_ffn_w_down, m_final_norm_g, v_norm_mix_g, v_norm_ffn_g, v_gla_w_in, v_gla_w_a2, v_gla_b_a, v_gla_norm_g, v_gla_w_out, v_ssd_w_in, v_ssd_conv_w, v_ssd_conv_b, v_ssd_dt_bias, v_ssd_a_log, v_ssd_d, v_ssd_norm_g, v_ssd_w_out, v_s5_log_dt, v_s5_a_re, v_s5_a_im, v_s5_b_re, v_s5_b_im, v_s5_c_re, v_s5_c_im, v_s5_d, v_s5_w_glu, v_ffn_w_gu, v_ffn_w_down, v_final_norm_g):
    given = dict(locals())
    local = {n: given[n] for n in _WEIGHTS}
    px, py, pc = _place()

    first, gathering, zero = _gather_first(local)
    full = dict(local)
    full.update(first)
    full["norm_mix_g"] = local["norm_mix_g"] + zero
    big = [first]

    def weights_of(tag):
        def arrived(h):
            big.append(_gather_rest(big[-1], gathering[tag], h, tag))
            return big[-1]
        return arrived

    swapping, reducing = [], []

    def later_grads(g):
        pending, zero = _reduce_swap_start(g, local, pc, lambda n, j: not _in_layer0(n, j), "later")
        swapping.append(pending)
        return zero

    def ffn0_grads(g):
        pending, zero = _reduce_scatter_after(swapping[0], g["ffn_w_down"][0], pc)
        reducing.append(pending)
        g["ffn_w_down"][0] = g["ffn_w_down"][0] + zero
        pending, zero = _reduce_start(g, local, pc, _is_ffn0, "ffn0")
        reducing.append(pending)
        return zero

    loss, grad_x, grads = local_step(x[0], loss_target[0], full, weights_of("later"), later_grads, ffn0_grads, weights_of("ffn0"))
    loss = lax.psum(loss, ("x", "y", "c"))

    small, zero = _reduce_small_start(grads)
    grads["gla_w_out"][0] = grads["gla_w_out"][0] + zero
    red, parts, first_pending = _reduce_big(grads, local, reducing, grad_x, px, py, pc)

    deltas, new_m, new_v = {}, {}, {}

    def update(n):
        deltas[n], new_m[n], new_v[n] = adamw(local[n], red[n], given["m_" + n], given["v_" + n], name=f"adamw_{n}")

    for n in list(red):
        update(n)
    red.update(_reduce_big_first(first_pending, parts, deltas["ffn_w_gu"], local, px, py, pc))
    red.update(_reduce_small(small, red["gla_w_out"], grads, local, px, py, pc))
    for n in _WEIGHTS:
        if n not in deltas:
            update(n)
    return (loss, grad_x[None], *[red[n] for n in _WEIGHTS], *[deltas[n] for n in _WEIGHTS],
            *[new_m[n] for n in _WEIGHTS], *[new_v[n] for n in _WEIGHTS])
```
